```python
import jax, jax.numpy as jnp
from jax import lax
import numpy as np

D_MODEL = 1024
BATCH = 8
SEQ = 2048
DEPTH = 1

D_MIX = D_MODEL
HEAD_DIM = 64
N_HEADS = 8
N_KV_HEADS = 2
GQA_GROUP = N_HEADS // N_KV_HEADS
D_ATTN = N_HEADS * HEAD_DIM
D_KV = N_KV_HEADS * HEAD_DIM
D_POOL = D_MIX - D_ATTN
POOL_WINDOWS = (2, 4, 8, 16)
N_POOL_GROUPS = len(POOL_WINDOWS)
POOL_GROUP_DIM = D_POOL // N_POOL_GROUPS
D_IN = D_ATTN + 2 * D_KV + D_POOL
WINDOW = 128
BLOCK = 128
N_BUCKETS = 32
MAX_DISTANCE = 128
D_FF = 2816
EPS = 1e-6
NEG = -1e30

kernel_name = "hybrid_swa_sink_pool_macaron"


def _rmsnorm(x, g):
    x32 = x.astype(jnp.float32)
    y = x32 * lax.rsqrt(jnp.mean(x32 * x32, axis=-1, keepdims=True) + EPS)
    return (y * g.astype(jnp.float32)).astype(x.dtype)


def _swiglu(h, w_gate, w_up, w_down):
    return (jax.nn.silu(h @ w_gate) * (h @ w_up)) @ w_down


def _t5_bucket(dist):
    n = np.maximum(dist, 0)
    max_exact = N_BUCKETS // 2
    large = max_exact + (np.log(np.maximum(n, 1) / max_exact)
                         / np.log(MAX_DISTANCE / max_exact)
                         * (N_BUCKETS - max_exact)).astype(np.int32)
    large = np.minimum(large, N_BUCKETS - 1)
    return np.where(n < max_exact, n, large).astype(np.int32)


def _band_structure(n_blocks):
    ql = np.arange(BLOCK)[:, None]
    kl = np.arange(2 * BLOCK)[None, :]
    dist = ql + BLOCK - kl
    bucket = _t5_bucket(dist)
    blk = np.arange(n_blocks)[:, None, None]
    k_abs = blk * BLOCK - BLOCK + kl[None]
    mask = (dist[None] >= 0) & (dist[None] < WINDOW) & (k_abs >= 0)
    return bucket, mask


def _sliding_window_attention(q, k, v, q_gain, k_gain, sinks, rel_bias):
    B, S = q.shape[0], q.shape[1]
    nb = S // BLOCK
    q = _rmsnorm(q, q_gain)
    k = _rmsnorm(k, k_gain)
    bucket, mask = _band_structure(nb)
    bias = jnp.transpose(rel_bias[jnp.asarray(bucket)], (2, 0, 1)).astype(jnp.float32)
    bias = bias.reshape(N_KV_HEADS, GQA_GROUP, BLOCK, 2 * BLOCK)
    mask = jnp.asarray(mask)[None, :, None, None]

    qb = q.reshape(B, nb, BLOCK, N_KV_HEADS, GQA_GROUP, HEAD_DIM)
    pad = ((0, 0), (BLOCK, 0), (0, 0), (0, 0))
    kp = jnp.pad(k, pad).reshape(B, nb + 1, BLOCK, N_KV_HEADS, HEAD_DIM)
    vp = jnp.pad(v, pad).reshape(B, nb + 1, BLOCK, N_KV_HEADS, HEAD_DIM)
    kb = jnp.concatenate([kp[:, :-1], kp[:, 1:]], axis=2)
    vb = jnp.concatenate([vp[:, :-1], vp[:, 1:]], axis=2)

    logits = jnp.einsum('bnqkgd,bnskd->bnkgqs', qb, kb).astype(jnp.float32)
    logits = logits * (HEAD_DIM ** -0.5) + bias[None, None]
    logits = jnp.where(mask, logits, NEG)
    sink = sinks.astype(jnp.float32).reshape(N_KV_HEADS, GQA_GROUP)[None, None, :, :, None, None]
    m = jnp.maximum(jnp.max(logits, axis=-1, keepdims=True), sink)
    e = jnp.exp(logits - m)
    probs = e / (jnp.sum(e, axis=-1, keepdims=True) + jnp.exp(sink - m))
    out = jnp.einsum('bnkgqs,bnskd->bnqkgd', probs.astype(vb.dtype), vb)
    return out.reshape(B, S, D_ATTN)


def _pool_mixer(u, w_pool, scale):
    B, S = u.shape[0], u.shape[1]
    ug = u.reshape(B, S, N_POOL_GROUPS, POOL_GROUP_DIM)
    csum = jnp.cumsum(ug.astype(jnp.float32), axis=1)
    pos = jnp.arange(1, S + 1, dtype=jnp.float32)
    means = []
    for g, w in enumerate(POOL_WINDOWS):
        cg = csum[:, :, g]
        prev = jnp.pad(cg, ((0, 0), (w, 0), (0, 0)))[:, :S]
        cnt = jnp.minimum(pos, float(w))[None, :, None]
        means.append((cg - prev) / cnt)
    pooled = jnp.stack(means, axis=2).astype(u.dtype) - ug
    y = jnp.einsum('bsgc,gcd->bsgd', pooled, w_pool).reshape(B, S, D_POOL)
    return y * scale


def _fwd_setup_inputs(seed: int = 0) -> dict:
    key = jax.random.key(seed)
    ks = jax.random.split(key, 20)
    nrm = lambda k, shape, fan_in: jax.random.normal(k, shape, jnp.float32) * fan_in ** -0.5
    gain = lambda k, shape: 1.0 + 0.02 * jax.random.normal(k, shape, jnp.float32)
    L = DEPTH
    return {
        "x": jax.random.normal(ks[0], (BATCH, SEQ, D_MODEL), jnp.float32),
        "ffn1_norm": gain(ks[1], (L, D_MODEL)),
        "ffn1_w_gate": nrm(ks[2], (L, D_MODEL, D_FF), D_MODEL),
        "ffn1_w_up": nrm(ks[3], (L, D_MODEL, D_FF), D_MODEL),
        "ffn1_w_down": nrm(ks[4], (L, D_FF, D_MODEL), D_FF),
        "mix_norm": gain(ks[5], (L, D_MODEL)),
        "w_in": nrm(ks[6], (L, D_MODEL, D_IN), D_MODEL),
        "q_norm": gain(ks[7], (L, HEAD_DIM)),
        "k_norm": gain(ks[8], (L, HEAD_DIM)),
        "attn_sinks": 0.5 * jax.random.normal(ks[9], (L, N_HEADS), jnp.float32),
        "rel_bias": 0.1 * jax.random.normal(ks[10], (N_BUCKETS, N_HEADS), jnp.float32),
        "pool_w": nrm(ks[11], (L, N_POOL_GROUPS, POOL_GROUP_DIM, POOL_GROUP_DIM), POOL_GROUP_DIM),
        "pool_scale": gain(ks[12], (L, D_POOL)),
        "w_out": nrm(ks[13], (L, D_MIX, D_MODEL), D_MIX),
        "ffn2_norm": gain(ks[14], (L, D_MODEL)),
        "ffn2_w_gate": nrm(ks[15], (L, D_MODEL, D_FF), D_MODEL),
        "ffn2_w_up": nrm(ks[16], (L, D_MODEL, D_FF), D_MODEL),
        "ffn2_w_down": nrm(ks[17], (L, D_FF, D_MODEL), D_FF),
    }


def _fwd_reference(x, ffn1_norm, ffn1_w_gate, ffn1_w_up, ffn1_w_down, mix_norm, w_in,
              q_norm, k_norm, attn_sinks, rel_bias, pool_w, pool_scale, w_out,
              ffn2_norm, ffn2_w_gate, ffn2_w_up, ffn2_w_down):
    B, S = x.shape[0], x.shape[1]
    for l in range(DEPTH):
        h = _rmsnorm(x, ffn1_norm[l])
        x = x + 0.5 * _swiglu(h, ffn1_w_gate[l], ffn1_w_up[l], ffn1_w_down[l])
        h = _rmsnorm(x, mix_norm[l])
        z = h @ w_in[l]
        q = z[..., :D_ATTN].reshape(B, S, N_HEADS, HEAD_DIM)
        k = z[..., D_ATTN:D_ATTN + D_KV].reshape(B, S, N_KV_HEADS, HEAD_DIM)
        v = z[..., D_ATTN + D_KV:D_ATTN + 2 * D_KV].reshape(B, S, N_KV_HEADS, HEAD_DIM)
        u = z[..., D_ATTN + 2 * D_KV:]
        y_attn = _sliding_window_attention(q, k, v, q_norm[l], k_norm[l],
                                           attn_sinks[l], rel_bias)
        y_pool = _pool_mixer(u, pool_w[l], pool_scale[l])
        x = x + jnp.concatenate([y_attn, y_pool], axis=-1) @ w_out[l]
        h = _rmsnorm(x, ffn2_norm[l])
        x = x + 0.5 * _swiglu(h, ffn2_w_gate[l], ffn2_w_up[l], ffn2_w_down[l])
    return x


import jax as _jax
import jax.numpy as _jnp

TWIN_FORMAT = 'train_step'
FWD_PARAMS = ['x', 'ffn1_norm', 'ffn1_w_gate', 'ffn1_w_up', 'ffn1_w_down', 'mix_norm', 'w_in', 'q_norm', 'k_norm', 'attn_sinks', 'rel_bias', 'pool_w', 'pool_scale', 'w_out', 'ffn2_norm', 'ffn2_w_gate', 'ffn2_w_up', 'ffn2_w_down']
TWIN_WEIGHTS = ['ffn1_norm', 'ffn1_w_gate', 'ffn1_w_up', 'ffn1_w_down', 'mix_norm', 'w_in', 'q_norm', 'k_norm', 'attn_sinks', 'rel_bias', 'pool_w', 'pool_scale', 'w_out', 'ffn2_norm', 'ffn2_w_gate', 'ffn2_w_up', 'ffn2_w_down']
TWIN_DIFF_INPUT = 'x'
TWIN_INPUTS = ['x', 'ffn1_norm', 'ffn1_w_gate', 'ffn1_w_up', 'ffn1_w_down', 'mix_norm', 'w_in', 'q_norm', 'k_norm', 'attn_sinks', 'rel_bias', 'pool_w', 'pool_scale', 'w_out', 'ffn2_norm', 'ffn2_w_gate', 'ffn2_w_up', 'ffn2_w_down', 'loss_target', 'm_ffn1_norm', 'm_ffn1_w_gate', 'm_ffn1_w_up', 'm_ffn1_w_down', 'm_mix_norm', 'm_w_in', 'm_q_norm', 'm_k_norm', 'm_attn_sinks', 'm_rel_bias', 'm_pool_w', 'm_pool_scale', 'm_w_out', 'm_ffn2_norm', 'm_ffn2_w_gate', 'm_ffn2_w_up', 'm_ffn2_w_down', 'v_ffn1_norm', 'v_ffn1_w_gate', 'v_ffn1_w_up', 'v_ffn1_w_down', 'v_mix_norm', 'v_w_in', 'v_q_norm', 'v_k_norm', 'v_attn_sinks', 'v_rel_bias', 'v_pool_w', 'v_pool_scale', 'v_w_out', 'v_ffn2_norm', 'v_ffn2_w_gate', 'v_ffn2_w_up', 'v_ffn2_w_down']
TWIN_OUTPUTS = ['loss', 'grad_x', 'grad_ffn1_norm', 'grad_ffn1_w_gate', 'grad_ffn1_w_up', 'grad_ffn1_w_down', 'grad_mix_norm', 'grad_w_in', 'grad_q_norm', 'grad_k_norm', 'grad_attn_sinks', 'grad_rel_bias', 'grad_pool_w', 'grad_pool_scale', 'grad_w_out', 'grad_ffn2_norm', 'grad_ffn2_w_gate', 'grad_ffn2_w_up', 'grad_ffn2_w_down', 'delta_ffn1_norm', 'delta_ffn1_w_gate', 'delta_ffn1_w_up', 'delta_ffn1_w_down', 'delta_mix_norm', 'delta_w_in', 'delta_q_norm', 'delta_k_norm', 'delta_attn_sinks', 'delta_rel_bias', 'delta_pool_w', 'delta_pool_scale', 'delta_w_out', 'delta_ffn2_norm', 'delta_ffn2_w_gate', 'delta_ffn2_w_up', 'delta_ffn2_w_down', 'new_m_ffn1_norm', 'new_m_ffn1_w_gate', 'new_m_ffn1_w_up', 'new_m_ffn1_w_down', 'new_m_mix_norm', 'new_m_w_in', 'new_m_q_norm', 'new_m_k_norm', 'new_m_attn_sinks', 'new_m_rel_bias', 'new_m_pool_w', 'new_m_pool_scale', 'new_m_w_out', 'new_m_ffn2_norm', 'new_m_ffn2_w_gate', 'new_m_ffn2_w_up', 'new_m_ffn2_w_down', 'new_v_ffn1_norm', 'new_v_ffn1_w_gate', 'new_v_ffn1_w_up', 'new_v_ffn1_w_down', 'new_v_mix_norm', 'new_v_w_in', 'new_v_q_norm', 'new_v_k_norm', 'new_v_attn_sinks', 'new_v_rel_bias', 'new_v_pool_w', 'new_v_pool_scale', 'new_v_w_out', 'new_v_ffn2_norm', 'new_v_ffn2_w_gate', 'new_v_ffn2_w_up', 'new_v_ffn2_w_down']
TWIN_LEAF_KINDS = {'loss': 'loss', 'grad_x': 'grad_x', 'grad_ffn1_norm': 'grad_w', 'grad_ffn1_w_gate': 'grad_w', 'grad_ffn1_w_up': 'grad_w', 'grad_ffn1_w_down': 'grad_w', 'grad_mix_norm': 'grad_w', 'grad_w_in': 'grad_w', 'grad_q_norm': 'grad_w', 'grad_k_norm': 'grad_w', 'grad_attn_sinks': 'grad_w', 'grad_rel_bias': 'grad_w', 'grad_pool_w': 'grad_w', 'grad_pool_scale': 'grad_w', 'grad_w_out': 'grad_w', 'grad_ffn2_norm': 'grad_w', 'grad_ffn2_w_gate': 'grad_w', 'grad_ffn2_w_up': 'grad_w', 'grad_ffn2_w_down': 'grad_w', 'delta_ffn1_norm': 'delta_w', 'delta_ffn1_w_gate': 'delta_w', 'delta_ffn1_w_up': 'delta_w', 'delta_ffn1_w_down': 'delta_w', 'delta_mix_norm': 'delta_w', 'delta_w_in': 'delta_w', 'delta_q_norm': 'delta_w', 'delta_k_norm': 'delta_w', 'delta_attn_sinks': 'delta_w', 'delta_rel_bias': 'delta_w', 'delta_pool_w': 'delta_w', 'delta_pool_scale': 'delta_w', 'delta_w_out': 'delta_w', 'delta_ffn2_norm': 'delta_w', 'delta_ffn2_w_gate': 'delta_w', 'delta_ffn2_w_up': 'delta_w', 'delta_ffn2_w_down': 'delta_w', 'new_m_ffn1_norm': 'new_m', 'new_m_ffn1_w_gate': 'new_m', 'new_m_ffn1_w_up': 'new_m', 'new_m_ffn1_w_down': 'new_m', 'new_m_mix_norm': 'new_m', 'new_m_w_in': 'new_m', 'new_m_q_norm': 'new_m', 'new_m_k_norm': 'new_m', 'new_m_attn_sinks': 'new_m', 'new_m_rel_bias': 'new_m', 'new_m_pool_w': 'new_m', 'new_m_pool_scale': 'new_m', 'new_m_w_out': 'new_m', 'new_m_ffn2_norm': 'new_m', 'new_m_ffn2_w_gate': 'new_m', 'new_m_ffn2_w_up': 'new_m', 'new_m_ffn2_w_down': 'new_m', 'new_v_ffn1_norm': 'new_v', 'new_v_ffn1_w_gate': 'new_v', 'new_v_ffn1_w_up': 'new_v', 'new_v_ffn1_w_down': 'new_v', 'new_v_mix_norm': 'new_v', 'new_v_w_in': 'new_v', 'new_v_q_norm': 'new_v', 'new_v_k_norm': 'new_v', 'new_v_attn_sinks': 'new_v', 'new_v_rel_bias': 'new_v', 'new_v_pool_w': 'new_v', 'new_v_pool_scale': 'new_v', 'new_v_w_out': 'new_v', 'new_v_ffn2_norm': 'new_v', 'new_v_ffn2_w_gate': 'new_v', 'new_v_ffn2_w_up': 'new_v', 'new_v_ffn2_w_down': 'new_v'}


def _forward(args):
    return _fwd_reference(*[args[k] for k in FWD_PARAMS])


def _output_shape():
    out = _jax.eval_shape(lambda: _forward(_fwd_setup_inputs(0)))
    return out.shape, out.dtype

N_MICROBATCH = 1
ADAM_LR = 0.001
ADAM_B1 = 0.9
ADAM_B2 = 0.999
ADAM_EPS = 1e-08
ADAM_WD = 0.01
ADAM_STEP = 10
PER_EXAMPLE_BATCH_AXIS = {'x': 0, 'loss_target': 0}
SHARED_INPUTS = []
_WEIGHT_DTYPES = {'ffn1_norm': _jnp.float32, 'ffn1_w_gate': _jnp.float32, 'ffn1_w_up': _jnp.float32, 'ffn1_w_down': _jnp.float32, 'mix_norm': _jnp.float32, 'w_in': _jnp.float32, 'q_norm': _jnp.float32, 'k_norm': _jnp.float32, 'attn_sinks': _jnp.float32, 'rel_bias': _jnp.float32, 'pool_w': _jnp.float32, 'pool_scale': _jnp.float32, 'w_out': _jnp.float32, 'ffn2_norm': _jnp.float32, 'ffn2_w_gate': _jnp.float32, 'ffn2_w_up': _jnp.float32, 'ffn2_w_down': _jnp.float32}
MOMENT_SCALE = {'ffn1_norm': 3.056180e+00, 'ffn1_w_gate': 5.272336e-02, 'ffn1_w_up': 5.954234e-02, 'ffn1_w_down': 9.915781e-02, 'mix_norm': 6.157837e+00, 'w_in': 3.970289e-01, 'q_norm': 1.735572e+00, 'k_norm': 1.738012e+00, 'attn_sinks': 5.285513e-01, 'rel_bias': 8.284544e-02, 'pool_w': 1.226365e+00, 'pool_scale': 1.247776e+01, 'w_out': 4.554429e-01, 'ffn2_norm': 3.100774e+00, 'ffn2_w_gate': 4.283251e-02, 'ffn2_w_up': 5.351121e-02, 'ffn2_w_down': 8.541306e-02}


def _to_microbatches(a, axis):
    t = _jnp.moveaxis(a, axis, 0)
    t = t.reshape((N_MICROBATCH, t.shape[0] // N_MICROBATCH) + t.shape[1:])
    return _jnp.moveaxis(t, 1, axis + 1)


def setup_inputs(seed: int = 0) -> dict:
    inp = _fwd_setup_inputs(seed)
    key = _jax.random.fold_in(_jax.random.key(seed), 7919)
    shape, _ = _output_shape()
    out = dict(inp)
    out["loss_target"] = _jax.random.normal(_jax.random.fold_in(key, 0), shape, _jnp.float32)
    for i, name in enumerate(TWIN_WEIGHTS):
        w = inp[name].astype(_jnp.float32)
        if MOMENT_SCALE is None:
            s = _jnp.sqrt(_jnp.mean(_jnp.square(w)) + 1e-30)
        else:
            s = MOMENT_SCALE[name]
        km, kv = _jax.random.split(_jax.random.fold_in(key, i + 1))
        out[name] = w
        out["m_" + name] = s * _jax.random.normal(km, w.shape, _jnp.float32)
        out["v_" + name] = (s * s) * _jax.random.uniform(kv, w.shape, _jnp.float32, 0.5, 1.5)
    if N_MICROBATCH > 1:
        for name, axis in PER_EXAMPLE_BATCH_AXIS.items():
            out[name] = _to_microbatches(out[name], axis)
    return {'x': out['x'], 'ffn1_norm': out['ffn1_norm'], 'ffn1_w_gate': out['ffn1_w_gate'], 'ffn1_w_up': out['ffn1_w_up'], 'ffn1_w_down': out['ffn1_w_down'], 'mix_norm': out['mix_norm'], 'w_in': out['w_in'], 'q_norm': out['q_norm'], 'k_norm': out['k_norm'], 'attn_sinks': out['attn_sinks'], 'rel_bias': out['rel_bias'], 'pool_w': out['pool_w'], 'pool_scale': out['pool_scale'], 'w_out': out['w_out'], 'ffn2_norm': out['ffn2_norm'], 'ffn2_w_gate': out['ffn2_w_gate'], 'ffn2_w_up': out['ffn2_w_up'], 'ffn2_w_down': out['ffn2_w_down'], 'loss_target': out['loss_target'], 'm_ffn1_norm': out['m_ffn1_norm'], 'm_ffn1_w_gate': out['m_ffn1_w_gate'], 'm_ffn1_w_up': out['m_ffn1_w_up'], 'm_ffn1_w_down': out['m_ffn1_w_down'], 'm_mix_norm': out['m_mix_norm'], 'm_w_in': out['m_w_in'], 'm_q_norm': out['m_q_norm'], 'm_k_norm': out['m_k_norm'], 'm_attn_sinks': out['m_attn_sinks'], 'm_rel_bias': out['m_rel_bias'], 'm_pool_w': out['m_pool_w'], 'm_pool_scale': out['m_pool_scale'], 'm_w_out': out['m_w_out'], 'm_ffn2_norm': out['m_ffn2_norm'], 'm_ffn2_w_gate': out['m_ffn2_w_gate'], 'm_ffn2_w_up': out['m_ffn2_w_up'], 'm_ffn2_w_down': out['m_ffn2_w_down'], 'v_ffn1_norm': out['v_ffn1_norm'], 'v_ffn1_w_gate': out['v_ffn1_w_gate'], 'v_ffn1_w_up': out['v_ffn1_w_up'], 'v_ffn1_w_down': out['v_ffn1_w_down'], 'v_mix_norm': out['v_mix_norm'], 'v_w_in': out['v_w_in'], 'v_q_norm': out['v_q_norm'], 'v_k_norm': out['v_k_norm'], 'v_attn_sinks': out['v_attn_sinks'], 'v_rel_bias': out['v_rel_bias'], 'v_pool_w': out['v_pool_w'], 'v_pool_scale': out['v_pool_scale'], 'v_w_out': out['v_w_out'], 'v_ffn2_norm': out['v_ffn2_norm'], 'v_ffn2_w_gate': out['v_ffn2_w_gate'], 'v_ffn2_w_up': out['v_ffn2_w_up'], 'v_ffn2_w_down': out['v_ffn2_w_down']}


def _loss(weights, diff, rest, loss_target):
    with _jax.named_scope("forward"):
        args = {**rest, TWIN_DIFF_INPUT: diff, **{k: w.astype(_WEIGHT_DTYPES[k]) for k, w in weights.items()}}
        y = _forward(args)
    with _jax.named_scope("loss_head"):
        err = _jnp.square(y.astype(_jnp.float32) - loss_target)
        return 0.5 * _jnp.sum(_jnp.mean(err, axis=-1)) if err.ndim else 0.5 * err


def _adamw(w, g, m, v):
    m = ADAM_B1 * m + (1.0 - ADAM_B1) * g
    v = ADAM_B2 * v + (1.0 - ADAM_B2) * _jnp.square(g)
    m_hat = m / (1.0 - ADAM_B1 ** ADAM_STEP)
    v_hat = v / (1.0 - ADAM_B2 ** ADAM_STEP)
    delta = -ADAM_LR * (m_hat / (_jnp.sqrt(v_hat) + ADAM_EPS) + ADAM_WD * w)
    return delta, m, v


def reference(x, ffn1_norm, ffn1_w_gate, ffn1_w_up, ffn1_w_down, mix_norm, w_in, q_norm, k_norm, attn_sinks, rel_bias, pool_w, pool_scale, w_out, ffn2_norm, ffn2_w_gate, ffn2_w_up, ffn2_w_down, loss_target, m_ffn1_norm, m_ffn1_w_gate, m_ffn1_w_up, m_ffn1_w_down, m_mix_norm, m_w_in, m_q_norm, m_k_norm, m_attn_sinks, m_rel_bias, m_pool_w, m_pool_scale, m_w_out, m_ffn2_norm, m_ffn2_w_gate, m_ffn2_w_up, m_ffn2_w_down, v_ffn1_norm, v_ffn1_w_gate, v_ffn1_w_up, v_ffn1_w_down, v_mix_norm, v_w_in, v_q_norm, v_k_norm, v_attn_sinks, v_rel_bias, v_pool_w, v_pool_scale, v_w_out, v_ffn2_norm, v_ffn2_w_gate, v_ffn2_w_up, v_ffn2_w_down):
    given = dict(x=x, ffn1_norm=ffn1_norm, ffn1_w_gate=ffn1_w_gate, ffn1_w_up=ffn1_w_up, ffn1_w_down=ffn1_w_down, mix_norm=mix_norm, w_in=w_in, q_norm=q_norm, k_norm=k_norm, attn_sinks=attn_sinks, rel_bias=rel_bias, pool_w=pool_w, pool_scale=pool_scale, w_out=w_out, ffn2_norm=ffn2_norm, ffn2_w_gate=ffn2_w_gate, ffn2_w_up=ffn2_w_up, ffn2_w_down=ffn2_w_down, loss_target=loss_target, m_ffn1_norm=m_ffn1_norm, m_ffn1_w_gate=m_ffn1_w_gate, m_ffn1_w_up=m_ffn1_w_up, m_ffn1_w_down=m_ffn1_w_down, m_mix_norm=m_mix_norm, m_w_in=m_w_in, m_q_norm=m_q_norm, m_k_norm=m_k_norm, m_attn_sinks=m_attn_sinks, m_rel_bias=m_rel_bias, m_pool_w=m_pool_w, m_pool_scale=m_pool_scale, m_w_out=m_w_out, m_ffn2_norm=m_ffn2_norm, m_ffn2_w_gate=m_ffn2_w_gate, m_ffn2_w_up=m_ffn2_w_up, m_ffn2_w_down=m_ffn2_w_down, v_ffn1_norm=v_ffn1_norm, v_ffn1_w_gate=v_ffn1_w_gate, v_ffn1_w_up=v_ffn1_w_up, v_ffn1_w_down=v_ffn1_w_down, v_mix_norm=v_mix_norm, v_w_in=v_w_in, v_q_norm=v_q_norm, v_k_norm=v_k_norm, v_attn_sinks=v_attn_sinks, v_rel_bias=v_rel_bias, v_pool_w=v_pool_w, v_pool_scale=v_pool_scale, v_w_out=v_w_out, v_ffn2_norm=v_ffn2_norm, v_ffn2_w_gate=v_ffn2_w_gate, v_ffn2_w_up=v_ffn2_w_up, v_ffn2_w_down=v_ffn2_w_down)
    weights = {n: given[n] for n in TWIN_WEIGHTS}
    shared = {n: given[n] for n in SHARED_INPUTS}
    per_example = {n: given[n] for n in ['x']}
    grad_fn = _jax.value_and_grad(_loss, argnums=(0, 1))

    def one_microbatch(ex, loss_target):
        ex = dict(ex)
        diff = ex.pop(TWIN_DIFF_INPUT)
        return grad_fn(weights, diff, {**shared, **ex}, loss_target)

    if N_MICROBATCH == 1:
        loss, (grad_w, grad_x) = one_microbatch(per_example, given["loss_target"])
    else:
        def body(carry, xs):
            loss_sum, grad_sum = carry
            l_k, (gw_k, gx_k) = one_microbatch(xs[0], xs[1])
            with _jax.named_scope("update"):
                return (loss_sum + l_k, _jax.tree.map(_jnp.add, grad_sum, gw_k)), gx_k

        init = (_jnp.zeros((), _jnp.float32), _jax.tree.map(_jnp.zeros_like, weights))
        (loss, grad_w), grad_x = _jax.lax.scan(body, init, (per_example, given["loss_target"]))
    with _jax.named_scope("update"):
        delta_w, new_m, new_v = {}, {}, {}
        for n in TWIN_WEIGHTS:
            delta_w[n], new_m[n], new_v[n] = _adamw(weights[n], grad_w[n], given["m_" + n], given["v_" + n])
    return (loss, grad_x, *[grad_w[n] for n in TWIN_WEIGHTS], *[delta_w[n] for n in TWIN_WEIGHTS],
            *[new_m[n] for n in TWIN_WEIGHTS], *[new_v[n] for n in TWIN_WEIGHTS])
```

```python
import functools

import jax
import jax.numpy as jnp
import numpy as np
from jax import lax
from jax.experimental import pallas as pl
from jax.experimental.pallas import tpu as pltpu

f32, bf16, i32 = jnp.float32, jnp.bfloat16, jnp.int32
SDS = jax.ShapeDtypeStruct

D = 1024
F = 2816
HD = 64
NH = 8
NKV = 2
GQA = NH // NKV
DATTN = NH * HD
DKV = NKV * HD
DPOOL = 512
POOL_WINDOWS = (2, 4, 8, 16)
PGD = DPOOL // len(POOL_WINDOWS)
DIN = DATTN + 2 * DKV + DPOOL
DMIX = DATTN + DPOOL
BLK = 128
NBUCK = 32
MAX_DISTANCE = 128
EPS = 1e-6
NEG = -1e30
SCALE = HD ** -0.5

ADAM_LR, ADAM_B1, ADAM_B2, ADAM_EPS, ADAM_WD, ADAM_STEP = 0.001, 0.9, 0.999, 1e-08, 0.01, 10

NDEV = 8
FS = F // NDEV
INS = DIN // NDEV
OUTS = DMIX // NDEV
PIECE_ROWS = (FS, FS, FS, INS, OUTS, FS, FS, FS)
PIECE_OFF = tuple(int(v) for v in np.cumsum((0,) + PIECE_ROWS[:-1]))
PACK_ROWS = sum(PIECE_ROWS)

VMEM_LIMIT_V7X = 56 * 1024 * 1024

MESH = pl.DeviceIdType.MESH


def _cparams(sem=None, vmem=None):
    return pltpu.CompilerParams(dimension_semantics=sem, vmem_limit_bytes=vmem)


def _nt(a, b):
    return lax.dot_general(a, b, (((1,), (1,)), ((), ())), preferred_element_type=f32)


def _tn(a, b):
    return lax.dot_general(a, b, (((0,), (0,)), ((), ())), preferred_element_type=f32)


def _nn(a, b):
    return jnp.dot(a, b, preferred_element_type=f32)


def _sigmoid(x):
    return 1.0 / (1.0 + jnp.exp(-x))


def _norm_fwd(x, g, name):
    T = x.shape[0]
    tm = min(512, T)

    def body(x_ref, g_ref, h_ref):
        xv = x_ref[...]
        r = lax.rsqrt(jnp.mean(xv * xv, axis=-1, keepdims=True) + EPS)
        h_ref[...] = (xv * r * g_ref[...]).astype(bf16)

    return pl.pallas_call(
        body, grid=(T // tm,),
        in_specs=[pl.BlockSpec((tm, D), lambda i: (i, 0)), pl.BlockSpec((1, D), lambda i: (0, 0))],
        out_specs=pl.BlockSpec((tm, D), lambda i: (i, 0)),
        out_shape=SDS((T, D), bf16), name=name)(x, g)


def _norm_bwd(dh, x, g, dres, out_scale, name):
    T = x.shape[0]
    tm = min(512, T)

    def body(dh_ref, x_ref, g_ref, dr_ref, dx_ref, dxb_ref, dg_ref):
        i = pl.program_id(0)
        xv = x_ref[...]
        r = lax.rsqrt(jnp.mean(xv * xv, axis=-1, keepdims=True) + EPS)
        xh = xv * r
        dhv = dh_ref[...]
        dxh = dhv * g_ref[...]
        dx = dr_ref[...] + r * (dxh - xh * jnp.mean(dxh * xh, axis=-1, keepdims=True))
        dx_ref[...] = dx
        dxb_ref[...] = (out_scale * dx).astype(bf16)
        dg = jnp.sum(dhv * xh, axis=0, keepdims=True)

        @pl.when(i == 0)
        def _():
            dg_ref[...] = dg

        @pl.when(i > 0)
        def _():
            dg_ref[...] += dg

    tok = pl.BlockSpec((tm, D), lambda i: (i, 0))
    vec = pl.BlockSpec((1, D), lambda i: (0, 0))
    return pl.pallas_call(
        body, grid=(T // tm,),
        in_specs=[tok, tok, vec, tok], out_specs=[tok, tok, vec],
        out_shape=(SDS((T, D), f32), SDS((T, D), bf16), SDS((1, D), f32)),
        compiler_params=_cparams(("arbitrary",)), name=name)(dh, x, g, dres)


def _ffn_tiles(T):
    return min(1024, T), 256


def _ffn_fwd(h, w, x, name):
    T = h.shape[0]
    tm, tf = _ffn_tiles(T)
    nf = F // tf

    def body(h_ref, w_ref, x_ref, xo_ref, g_ref, u_ref, acc):
        fi = pl.program_id(1)
        hv = h_ref[...]
        gate = _nt(hv, w_ref[0])
        up = _nt(hv, w_ref[1])
        act = gate * _sigmoid(gate) * up
        g_ref[...] = gate.astype(bf16)
        u_ref[...] = up.astype(bf16)
        part = _nn(act.astype(bf16), w_ref[2])

        @pl.when(fi == 0)
        def _():
            acc[...] = part

        @pl.when(fi > 0)
        def _():
            acc[...] += part

        @pl.when(fi == nf - 1)
        def _():
            xo_ref[...] = x_ref[...] + 0.5 * acc[...]

    tok = pl.BlockSpec((tm, D), lambda i, f: (i, 0))
    act_spec = pl.BlockSpec((tm, tf), lambda i, f: (i, f))
    return pl.pallas_call(
        body, grid=(T // tm, nf),
        in_specs=[tok, pl.BlockSpec((3, tf, D), lambda i, f: (0, f, 0)), tok],
        out_specs=[tok, act_spec, act_spec],
        out_shape=(SDS((T, D), f32), SDS((T, F), bf16), SDS((T, F), bf16)),
        scratch_shapes=[pltpu.VMEM((tm, D), f32)],
        compiler_params=_cparams(("arbitrary", "arbitrary"), VMEM_LIMIT_V7X), name=name)(h, w, x)


def _ffn_bwd(dob, h, gate, up, w, name):
    T = h.shape[0]
    tm, tf = _ffn_tiles(T)
    nf, nt = F // tf, T // tm

    def body(do_ref, h_ref, g_ref, u_ref, w_ref, dh_ref, dw_ref, dwacc):
        fi, ti = pl.program_id(0), pl.program_id(1)
        dov = do_ref[...]
        gv = g_ref[...].astype(f32)
        uv = u_ref[...].astype(f32)
        sg = _sigmoid(gv)
        sil = gv * sg
        dact = _nt(dov, w_ref[2])
        dup = dact * sil
        dgate = dact * uv * (sg * (1.0 + gv * (1.0 - sg)))
        dgu = jnp.concatenate([dgate.astype(bf16), dup.astype(bf16)], axis=1)
        actb = (sil * uv).astype(bf16)
        wgu = w_ref[0:2].reshape(2 * tf, D)
        dh_part = _nn(dgu, wgu)
        rows = pl.ds(pl.multiple_of(ti * tm, tm), tm)

        @pl.when(fi == 0)
        def _():
            dh_ref[rows, :] = dh_part

        @pl.when(fi > 0)
        def _():
            dh_ref[rows, :] += dh_part

        dwgu = _tn(dgu, h_ref[...])
        dwd = _tn(actb, dov)

        @pl.when(ti == 0)
        def _():
            dwacc[0:2 * tf, :] = dwgu
            dwacc[2 * tf:3 * tf, :] = dwd

        @pl.when(ti > 0)
        def _():
            dwacc[0:2 * tf, :] += dwgu
            dwacc[2 * tf:3 * tf, :] += dwd

        @pl.when(ti == nt - 1)
        def _():
            dw_ref[...] = dwacc[...].reshape(3, tf, D).astype(bf16)

    tok = pl.BlockSpec((tm, D), lambda f, i: (i, 0))
    act_spec = pl.BlockSpec((tm, tf), lambda f, i: (i, f))
    wspec = pl.BlockSpec((3, tf, D), lambda f, i: (0, f, 0))
    return pl.pallas_call(
        body, grid=(nf, nt),
        in_specs=[tok, tok, act_spec, act_spec, wspec],
        out_specs=[pl.BlockSpec((T, D), lambda f, i: (0, 0)), wspec],
        out_shape=(SDS((T, D), f32), SDS((3, F, D), bf16)),
        scratch_shapes=[pltpu.VMEM((3 * tf, D), f32)],
        compiler_params=_cparams(("arbitrary", "arbitrary"), VMEM_LIMIT_V7X), name=name)(dob, h, gate, up, w)


def _loss_grad(y, target, name):
    T = y.shape[0]
    tm = min(512, T)

    def body(y_ref, t_ref, dy_ref, dyb_ref, l_ref):
        i = pl.program_id(0)
        e = y_ref[...] - t_ref[...]
        dy = e * (1.0 / D)
        dy_ref[...] = dy
        dyb_ref[...] = (0.5 * dy).astype(bf16)
        col = jnp.sum(e * e, axis=0, keepdims=True) * (0.5 / D)
        lanes = col[:, 0:128]
        for k in range(1, D // 128):
            lanes = lanes + col[:, 128 * k:128 * (k + 1)]

        @pl.when(i == 0)
        def _():
            l_ref[...] = lanes

        @pl.when(i > 0)
        def _():
            l_ref[...] += lanes

    tok = pl.BlockSpec((tm, D), lambda i: (i, 0))
    return pl.pallas_call(
        body, grid=(T // tm,), in_specs=[tok, tok],
        out_specs=[tok, tok, pl.BlockSpec((1, 128), lambda i: (0, 0))],
        out_shape=(SDS((T, D), f32), SDS((T, D), bf16), SDS((1, 128), f32)),
        compiler_params=_cparams(("arbitrary",)), name=name)(y, target)


def _in_proj_fwd(h, wint, name):
    T = h.shape[0]
    tm = min(512, T)

    def body(h_ref, w_ref, z_ref):
        z_ref[...] = _nt(h_ref[...], w_ref[...])

    return pl.pallas_call(
        body, grid=(T // tm,),
        in_specs=[pl.BlockSpec((tm, D), lambda i: (i, 0)), pl.BlockSpec((DIN, D), lambda i: (0, 0))],
        out_specs=pl.BlockSpec((tm, DIN), lambda i: (i, 0)),
        out_shape=SDS((T, DIN), f32), name=name)(h, wint)


def _in_proj_bwd(dz, wint, h, name):
    T = h.shape[0]
    tm = min(512, T)
    nt = T // tm

    def body(dz_ref, w_ref, h_ref, dh_ref, dw_ref, acc):
        i = pl.program_id(0)
        dzb = dz_ref[...].astype(bf16)
        dh_ref[...] = _nn(dzb, w_ref[...])
        part = _tn(dzb, h_ref[...])

        @pl.when(i == 0)
        def _():
            acc[...] = part

        @pl.when(i > 0)
        def _():
            acc[...] += part

        @pl.when(i == nt - 1)
        def _():
            dw_ref[...] = acc[...].astype(bf16)

    wspec = pl.BlockSpec((DIN, D), lambda i: (0, 0))
    return pl.pallas_call(
        body, grid=(nt,),
        in_specs=[pl.BlockSpec((tm, DIN), lambda i: (i, 0)), wspec, pl.BlockSpec((tm, D), lambda i: (i, 0))],
        out_specs=[pl.BlockSpec((tm, D), lambda i: (i, 0)), wspec],
        out_shape=(SDS((T, D), f32), SDS((DIN, D), bf16)),
        scratch_shapes=[pltpu.VMEM((DIN, D), f32)],
        compiler_params=_cparams(("arbitrary",)), name=name)(dz, wint, h)


def _out_proj_fwd(ymix, wout, x, name):
    T = x.shape[0]
    tm = min(512, T)

    def body(y_ref, w_ref, x_ref, o_ref):
        o_ref[...] = x_ref[...] + _nn(y_ref[...], w_ref[...])

    tok = pl.BlockSpec((tm, D), lambda i: (i, 0))
    return pl.pallas_call(
        body, grid=(T // tm,),
        in_specs=[pl.BlockSpec((tm, DMIX), lambda i: (i, 0)), pl.BlockSpec((DMIX, D), lambda i: (0, 0)), tok],
        out_specs=tok, out_shape=SDS((T, D), f32), name=name)(ymix, wout, x)


def _out_proj_bwd(dxb, wout, ymix, name):
    T = dxb.shape[0]
    tm = min(512, T)
    nt = T // tm

    def body(dx_ref, w_ref, y_ref, dy_ref, dw_ref, acc):
        i = pl.program_id(0)
        dxv = dx_ref[...]
        dy_ref[...] = _nt(dxv, w_ref[...])
        part = _tn(y_ref[...], dxv)

        @pl.when(i == 0)
        def _():
            acc[...] = part

        @pl.when(i > 0)
        def _():
            acc[...] += part

        @pl.when(i == nt - 1)
        def _():
            dw_ref[...] = acc[...].astype(bf16)

    wspec = pl.BlockSpec((DMIX, D), lambda i: (0, 0))
    return pl.pallas_call(
        body, grid=(nt,),
        in_specs=[pl.BlockSpec((tm, D), lambda i: (i, 0)), wspec, pl.BlockSpec((tm, DMIX), lambda i: (i, 0))],
        out_specs=[pl.BlockSpec((tm, DMIX), lambda i: (i, 0)), wspec],
        out_shape=(SDS((T, DMIX), f32), SDS((DMIX, D), bf16)),
        scratch_shapes=[pltpu.VMEM((DMIX, D), f32)],
        compiler_params=_cparams(("arbitrary",)), name=name)(dxb, wout, ymix)


def _t5_bucket_table():
    ql = np.arange(BLK)[:, None]
    kl = np.arange(2 * BLK)[None, :]
    n = np.maximum(ql + BLK - kl, 0)
    max_exact = NBUCK // 2
    large = max_exact + (np.log(np.maximum(n, 1) / max_exact) / np.log(MAX_DISTANCE / max_exact)
                         * (NBUCK - max_exact)).astype(np.int32)
    large = np.minimum(large, NBUCK - 1)
    return np.where(n < max_exact, n, large).astype(np.int32)


def _fill_bias(bk_ref, rb_ref, bias_scr):
    bk = bk_ref[...]
    for h in range(NH):
        def step(b, acc, h=h):
            return acc + jnp.where(bk == b, rb_ref[b, h], 0.0)
        bias_scr[h] = lax.fori_loop(0, NBUCK, step, jnp.zeros((BLK, 2 * BLK), f32))


def _attn_probs(zc_ref, zp_ref, kh, qg, kg, sk_ref, bias_scr, n):
    kc = DATTN + HD * kh
    vc = DATTN + DKV + HD * kh
    kx = jnp.concatenate([zp_ref[:, kc:kc + HD], zc_ref[:, kc:kc + HD]], axis=0)
    vx = jnp.concatenate([zp_ref[:, vc:vc + HD], zc_ref[:, vc:vc + HD]], axis=0)
    qx = jnp.concatenate([zc_ref[:, HD * (GQA * kh + g):HD * (GQA * kh + g + 1)] for g in range(GQA)], axis=0)
    rq = lax.rsqrt(jnp.mean(qx * qx, axis=-1, keepdims=True) + EPS)
    rk = lax.rsqrt(jnp.mean(kx * kx, axis=-1, keepdims=True) + EPS)
    qhat, khat = qx * rq, kx * rk
    qnb, knb = (qhat * qg).astype(bf16), (khat * kg).astype(bf16)
    s = _nt(qnb, knb) * SCALE + bias_scr[GQA * kh:GQA * (kh + 1)].reshape(GQA * BLK, 2 * BLK)
    row = lax.broadcasted_iota(i32, (GQA * BLK, 2 * BLK), 0) & (BLK - 1)
    col = lax.broadcasted_iota(i32, (GQA * BLK, 2 * BLK), 1)
    mask = (col > row) & (col <= row + BLK) & ((col >= BLK) | (n > 0))
    s = jnp.where(mask, s, NEG)
    ridx = lax.broadcasted_iota(i32, (GQA * BLK, 1), 0)
    sink = jnp.full((GQA * BLK, 1), sk_ref[GQA * kh + GQA - 1], f32)
    for g in range(GQA - 2, -1, -1):
        sink = jnp.where(ridx < (g + 1) * BLK, sk_ref[GQA * kh + g], sink)
    m = jnp.maximum(jnp.max(s, axis=-1, keepdims=True), sink)
    e = jnp.exp(s - m)
    es = jnp.exp(sink - m)
    den = jnp.sum(e, axis=-1, keepdims=True) + es
    return dict(p=e / den, psink=es / den, qhat=qhat, khat=khat, rq=rq, rk=rk, qnb=qnb, knb=knb, vb=vx.astype(bf16))


def _pool_group(zc_ref, zp_ref, g, w, n):
    c0 = DATTN + 2 * DKV + PGD * g
    uc = zc_ref[:, c0:c0 + PGD]
    up = jnp.where(n > 0, zp_ref[:, c0:c0 + PGD], 0.0)
    ue = jnp.concatenate([up, uc], axis=0)
    hi = ue.astype(bf16)
    lo = (ue - hi.astype(f32)).astype(bf16)
    t = lax.broadcasted_iota(i32, (BLK, 2 * BLK), 0)
    s = lax.broadcasted_iota(i32, (BLK, 2 * BLK), 1)
    band = jnp.where((s <= t + BLK) & (s > t + BLK - w), 1.0, 0.0).astype(bf16)
    sm = _nn(band, hi) + _nn(band, lo)
    pos = n * BLK + lax.broadcasted_iota(i32, (BLK, 1), 0) + 1
    cnt = jnp.minimum(pos, w).astype(f32)
    return sm / cnt - uc, band, cnt


def _mix_fwd(z, qg, kg, sinks, relb, bucket, pool_w, pscale, name):
    T = z.shape[0]
    nb = T // BLK

    def body(zc_ref, zp_ref, qg_ref, kg_ref, sk_ref, rb_ref, bk_ref, pw_ref, ps_ref, y_ref, bias_scr, yacc):
        n = pl.program_id(0)

        @pl.when(n == 0)
        def _():
            _fill_bias(bk_ref, rb_ref, bias_scr)

        for kh in range(NKV):
            a = _attn_probs(zc_ref, zp_ref, kh, qg_ref[...], kg_ref[...], sk_ref, bias_scr, n)
            o = _nn(a["p"].astype(bf16), a["vb"])
            for g in range(GQA):
                hc = HD * (GQA * kh + g)
                yacc[:, hc:hc + HD] = o[g * BLK:(g + 1) * BLK]
        for g, w in enumerate(POOL_WINDOWS):
            pooled, _, _ = _pool_group(zc_ref, zp_ref, g, w, n)
            yp = _nn(pooled.astype(bf16), pw_ref[g].astype(bf16)) * ps_ref[:, PGD * g:PGD * (g + 1)]
            yacc[:, DATTN + PGD * g:DATTN + PGD * (g + 1)] = yp
        y_ref[...] = yacc[...].astype(bf16)

    full = lambda *shape: pl.BlockSpec(shape, lambda n: (0,) * len(shape))
    smem = pl.BlockSpec(memory_space=pltpu.SMEM)
    return pl.pallas_call(
        body, grid=(nb,),
        in_specs=[pl.BlockSpec((BLK, DIN), lambda n: (n, 0)),
                  pl.BlockSpec((BLK, DIN), lambda n: (jnp.maximum(n - 1, 0), 0)),
                  full(1, HD), full(1, HD), smem, smem, full(BLK, 2 * BLK),
                  full(len(POOL_WINDOWS), PGD, PGD), full(1, DPOOL)],
        out_specs=pl.BlockSpec((BLK, DMIX), lambda n: (n, 0)),
        out_shape=SDS((T, DMIX), bf16),
        scratch_shapes=[pltpu.VMEM((NH, BLK, 2 * BLK), f32), pltpu.VMEM((BLK, DMIX), f32)],
        compiler_params=_cparams(("arbitrary",)), name=name)(z, z, qg, kg, sinks, relb, bucket, pool_w, pscale)


def _mix_bwd(z, dy, qg, kg, sinks, relb, bucket, pool_w, pscale, name):
    T = z.shape[0]
    nb = T // BLK

    def body(zc_ref, zp_ref, dy_ref, qg_ref, kg_ref, sk_ref, rb_ref, bk_ref, pw_ref, ps_ref,
             dz_ref, dqg_ref, dkg_ref, dsk_ref, drb_ref, dpw_ref, dps_ref, bias_scr, dbias_scr):
        n = pl.program_id(0)
        rows = pl.ds(pl.multiple_of(n * BLK, BLK), BLK)
        prow = pl.ds(pl.multiple_of(jnp.maximum(n - 1, 0) * BLK, BLK), BLK)

        @pl.when(n == 0)
        def _():
            _fill_bias(bk_ref, rb_ref, bias_scr)
            dbias_scr[...] = jnp.zeros_like(dbias_scr)
            dqg_ref[...] = jnp.zeros_like(dqg_ref)
            dkg_ref[...] = jnp.zeros_like(dkg_ref)
            dsk_ref[...] = jnp.zeros_like(dsk_ref)
            dpw_ref[...] = jnp.zeros_like(dpw_ref)
            dps_ref[...] = jnp.zeros_like(dps_ref)

        qg, kg = qg_ref[...], kg_ref[...]
        lane = lax.broadcasted_iota(i32, (1, 128), 1)
        dsk = jnp.zeros((1, 128), f32)
        for kh in range(NKV):
            a = _attn_probs(zc_ref, zp_ref, kh, qg, kg, sk_ref, bias_scr, n)
            p = a["p"]
            do = jnp.concatenate([dy_ref[:, HD * (GQA * kh + g):HD * (GQA * kh + g + 1)] for g in range(GQA)],
                                 axis=0).astype(bf16)
            dv = _tn(p.astype(bf16), do)
            dp = _nt(do, a["vb"])
            delta = jnp.sum(p * dp, axis=-1, keepdims=True)
            ds = p * (dp - delta)
            sinkterm = a["psink"] * delta
            for g in range(GQA):
                h = GQA * kh + g
                dbias_scr[h] += ds[g * BLK:(g + 1) * BLK]
                tot = jnp.sum(sinkterm[g * BLK:(g + 1) * BLK], axis=0, keepdims=True)
                dsk = dsk - jnp.where(lane == h, tot, 0.0)
            dsb = ds.astype(bf16)
            dqn = _nn(dsb, a["knb"]) * SCALE
            dkn = _tn(dsb, a["qnb"]) * SCALE
            qhat, khat = a["qhat"], a["khat"]
            dqg_ref[...] += jnp.sum(dqn * qhat, axis=0, keepdims=True)
            dkg_ref[...] += jnp.sum(dkn * khat, axis=0, keepdims=True)
            dqh = dqn * qg
            dq = a["rq"] * (dqh - qhat * jnp.mean(dqh * qhat, axis=-1, keepdims=True))
            dkh = dkn * kg
            dk = a["rk"] * (dkh - khat * jnp.mean(dkh * khat, axis=-1, keepdims=True))
            kc = DATTN + HD * kh
            vc = DATTN + DKV + HD * kh
            for g in range(GQA):
                hc = HD * (GQA * kh + g)
                dz_ref[rows, hc:hc + HD] = dq[g * BLK:(g + 1) * BLK]
            dz_ref[rows, kc:kc + HD] = dk[BLK:2 * BLK]
            dz_ref[rows, vc:vc + HD] = dv[BLK:2 * BLK]

            @pl.when(n > 0)
            def _(dk=dk, dv=dv, kc=kc, vc=vc):
                dz_ref[prow, kc:kc + HD] += dk[0:BLK]
                dz_ref[prow, vc:vc + HD] += dv[0:BLK]

        dsk_ref[...] += dsk

        for g, w in enumerate(POOL_WINDOWS):
            c0 = DATTN + 2 * DKV + PGD * g
            pooled, band, cnt = _pool_group(zc_ref, zp_ref, g, w, n)
            pb = pooled.astype(bf16)
            wb = pw_ref[g].astype(bf16)
            dyp = dy_ref[:, DATTN + PGD * g:DATTN + PGD * (g + 1)]
            ypre = _nn(pb, wb)
            dps_ref[:, PGD * g:PGD * (g + 1)] += jnp.sum(dyp * ypre, axis=0, keepdims=True)
            dyg = (dyp * ps_ref[:, PGD * g:PGD * (g + 1)]).astype(bf16)
            dpw_ref[g] += _tn(pb, dyg)
            dpooled = _nt(dyg, wb)
            dsm = dpooled / cnt
            hi = dsm.astype(bf16)
            lo = (dsm - hi.astype(f32)).astype(bf16)
            due = _tn(band, hi) + _tn(band, lo)
            dz_ref[rows, c0:c0 + PGD] = due[BLK:2 * BLK] - dpooled

            @pl.when(n > 0)
            def _(due=due, c0=c0):
                dz_ref[prow, c0:c0 + PGD] += due[0:BLK]

        @pl.when(n == nb - 1)
        def _():
            bk = bk_ref[...]
            ri = lax.broadcasted_iota(i32, (NBUCK, NH), 0)
            ci = lax.broadcasted_iota(i32, (NBUCK, NH), 1)

            def step(b, acc):
                for h in range(NH):
                    sel = jnp.where(bk == b, dbias_scr[h], 0.0)
                    tot = jnp.sum(jnp.sum(sel, axis=1, keepdims=True), axis=0, keepdims=True)
                    acc = acc + jnp.where((ri == b) & (ci == h), tot, 0.0)
                return acc

            drb_ref[...] = lax.fori_loop(0, NBUCK, step, jnp.zeros((NBUCK, NH), f32))

    full = lambda *shape: pl.BlockSpec(shape, lambda n: (0,) * len(shape))
    smem = pl.BlockSpec(memory_space=pltpu.SMEM)
    npg = len(POOL_WINDOWS)
    return pl.pallas_call(
        body, grid=(nb,),
        in_specs=[pl.BlockSpec((BLK, DIN), lambda n: (n, 0)),
                  pl.BlockSpec((BLK, DIN), lambda n: (jnp.maximum(n - 1, 0), 0)),
                  pl.BlockSpec((BLK, DMIX), lambda n: (n, 0)),
                  full(1, HD), full(1, HD), smem, smem, full(BLK, 2 * BLK), full(npg, PGD, PGD), full(1, DPOOL)],
        out_specs=[full(T, DIN), full(1, HD), full(1, HD), full(1, 128), full(NBUCK, NH),
                   full(npg, PGD, PGD), full(1, DPOOL)],
        out_shape=(SDS((T, DIN), f32), SDS((1, HD), f32), SDS((1, HD), f32), SDS((1, 128), f32),
                   SDS((NBUCK, NH), f32), SDS((npg, PGD, PGD), f32), SDS((1, DPOOL), f32)),
        scratch_shapes=[pltpu.VMEM((NH, BLK, 2 * BLK), f32), pltpu.VMEM((NH, BLK, 2 * BLK), f32)],
        compiler_params=_cparams(("arbitrary",), VMEM_LIMIT_V7X),
        name=name)(z, z, dy, qg, kg, sinks, relb, bucket, pool_w, pscale)


def _local_step(x, target, w1, wint, wout, w2, g1, gm, g3, qg, kg, sinks, relb, pool_w, pscale):
    bucket = jnp.asarray(_t5_bucket_table())
    sk = sinks.reshape(NH)
    h1 = _norm_fwd(x, g1, "norm1_fwd")
    x1, gate1, up1 = _ffn_fwd(h1, w1, x, "ffn1_fwd")
    h2 = _norm_fwd(x1, gm, "norm2_fwd")
    z = _in_proj_fwd(h2, wint, "in_proj_fwd")
    ymix = _mix_fwd(z, qg, kg, sk, relb, bucket, pool_w, pscale, "mix_fwd")
    x2 = _out_proj_fwd(ymix, wout, x1, "out_proj_fwd")
    h3 = _norm_fwd(x2, g3, "norm3_fwd")
    y, gate2, up2 = _ffn_fwd(h3, w2, x2, "ffn2_fwd")
    dy, dyb, loss_lanes = _loss_grad(y, target, "loss_grad")

    dh3, dw2 = _ffn_bwd(dyb, h3, gate2, up2, w2, "ffn2_bwd")
    dx2, dx2b, dg3 = _norm_bwd(dh3, x2, g3, dy, 1.0, "norm3_bwd")
    dymix, dwout = _out_proj_bwd(dx2b, wout, ymix, "out_proj_bwd")
    dz, dqg, dkg, dsk, drb, dpw, dps = _mix_bwd(z, dymix, qg, kg, sk, relb, bucket, pool_w, pscale, "mix_bwd")
    dh2, dwint = _in_proj_bwd(dz, wint, h2, "in_proj_bwd")
    dx1, dx1b, dgm = _norm_bwd(dh2, x1, gm, dx2, 0.5, "norm2_bwd")
    dh1, dw1 = _ffn_bwd(dx1b, h1, gate1, up1, w1, "ffn1_bwd")
    gx, _, dg1 = _norm_bwd(dh1, x, g1, dx1, 1.0, "norm1_bwd")
    small = dict(ffn1_norm=dg1, mix_norm=dgm, ffn2_norm=dg3, pool_scale=dps, q_norm=dqg, k_norm=dkg,
                 attn_sinks=dsk[:, :NH], rel_bias=drb, pool_w=dpw, loss=loss_lanes)
    return gx, (dw1, dwint, dwout, dw2), small


SMALL_NAMES = ("ffn1_norm", "mix_norm", "ffn2_norm", "pool_scale", "q_norm", "k_norm", "attn_sinks", "rel_bias",
               "pool_w", "loss")
SMALL_SHAPES = dict(ffn1_norm=(1, D), mix_norm=(1, D), ffn2_norm=(1, D), pool_scale=(1, DPOOL), q_norm=(1, HD),
                    k_norm=(1, HD), attn_sinks=(1, NH), rel_bias=(NBUCK, NH),
                    pool_w=(1, len(POOL_WINDOWS), PGD, PGD), loss=(1, 128))


def _small_rows(name):
    return -(-int(np.prod(SMALL_SHAPES[name])) // 128)


SMALL_OFF = {}
_r = 0
for _n in SMALL_NAMES:
    SMALL_OFF[_n] = _r
    _r += _small_rows(_n)
SMALL_ROWS = -(-_r // 8) * 8
LOSS_ROW = SMALL_OFF["loss"]


def _pack_small(vals):
    parts = []
    for n in SMALL_NAMES:
        size = _small_rows(n) * 128
        if n in vals:
            flat = vals[n].astype(f32).reshape(-1)
            parts.append(jnp.pad(flat, (0, size - flat.shape[0])))
        else:
            parts.append(jnp.zeros((size,), f32))
    flat = jnp.concatenate(parts)
    flat = jnp.pad(flat, (0, SMALL_ROWS * 128 - flat.shape[0]))
    return flat.reshape(SMALL_ROWS, 128)


def _unpack_small(packed, name):
    size = int(np.prod(SMALL_SHAPES[name]))
    r0 = SMALL_OFF[name]
    return packed[r0:r0 + _small_rows(name)].reshape(-1)[:size].reshape(SMALL_SHAPES[name])


def _position():
    return lax.axis_index("x"), lax.axis_index("y"), lax.axis_index("c")


def _dev_index(x, y, c):
    return 4 * x + 2 * y + c


def _all_gather_weights(shard):
    def body(s_ref, w1_ref, wi_ref, wo_ref, w2_ref, send_sems, recv_sems, local_sem):
        x, y, c = _position()
        me, sib = (x, y, c), (x, y, 1 - c)
        chips = [(1 - x, y), (x, 1 - y), (1 - x, 1 - y)]
        outs = (w1_ref.at[0], w1_ref.at[1], w1_ref.at[2], wi_ref, wo_ref, w2_ref.at[0], w2_ref.at[1], w2_ref.at[2])

        def piece_dst(k, dev):
            r = PIECE_ROWS[k]
            return outs[k].at[pl.ds(pl.multiple_of(_dev_index(*dev) * r, 16), r), :]

        def piece_src(k):
            return s_ref.at[pl.ds(PIECE_OFF[k], PIECE_ROWS[k]), :]

        def copies(rel, block, to, from_shard):
            return [pltpu.make_async_remote_copy(
                src_ref=piece_src(k) if from_shard else piece_dst(k, block), dst_ref=piece_dst(k, block),
                send_sem=send_sems.at[rel], recv_sem=recv_sems.at[rel], device_id=to, device_id_type=MESH)
                for k in range(len(PIECE_ROWS))]

        def whole(rel):
            return pltpu.make_async_remote_copy(src_ref=s_ref, dst_ref=s_ref, send_sem=send_sems.at[rel],
                                                recv_sem=recv_sems.at[rel], device_id=me, device_id_type=MESH)

        mine = [pltpu.make_async_copy(piece_src(k), piece_dst(k, me), local_sem) for k in range(len(PIECE_ROWS))]
        for cp in mine:
            cp.start()
        for cp in copies(0, me, sib, True):
            cp.start()
        for j, chip in enumerate(chips):
            for cp in copies(1 + j, me, (*chip, c), True):
                cp.start()
        for j, chip in enumerate(chips):
            whole(1 + j).wait_recv()
            for cp in copies(4 + j, (*chip, c), sib, False):
                cp.start()
        whole(0).wait_recv()
        for j in range(3):
            whole(4 + j).wait_recv()
        for rel in range(7):
            whole(rel).wait_send()
        pltpu.make_async_copy(s_ref, s_ref, local_sem).wait()

    hbm = pl.BlockSpec(memory_space=pl.ANY)
    return pl.pallas_call(
        body, in_specs=[hbm], out_specs=[hbm] * 4,
        out_shape=(SDS((3, F, D), bf16), SDS((DIN, D), bf16), SDS((DMIX, D), bf16), SDS((3, F, D), bf16)),
        scratch_shapes=[pltpu.SemaphoreType.DMA((7,)), pltpu.SemaphoreType.DMA((7,)), pltpu.SemaphoreType.DMA],
        compiler_params=pltpu.CompilerParams(has_side_effects=True),
        name="all_gather_weights")(shard)


RS_CHUNK = 240


def _reduce_scatter_grads(dw1, dwint, dwout, dw2):
    nchunk = PACK_ROWS // RS_CHUNK

    def body(d1_ref, di_ref, do_ref, d2_ref, red_ref, rx1_ref, rx2_ref,
             own_buf, rx_buf, tx_buf, acc, sa, ra, sb, rb, lsem):
        x, y, c = _position()
        me, sib = (x, y, c), (x, y, 1 - c)
        rel_chips = [(x, y), (1 - x, y), (x, 1 - y), (1 - x, 1 - y)]
        srcs = (d1_ref.at[0], d1_ref.at[1], d1_ref.at[2], di_ref, do_ref, d2_ref.at[0], d2_ref.at[1], d2_ref.at[2])

        def piece(k, dev):
            r = PIECE_ROWS[k]
            return srcs[k].at[pl.ds(pl.multiple_of(_dev_index(*dev) * r, 16), r), :]

        def packed(ref, k):
            return ref.at[pl.ds(PIECE_OFF[k], PIECE_ROWS[k]), :]

        for j, chip in enumerate(rel_chips):
            for k in range(len(PIECE_ROWS)):
                pltpu.make_async_remote_copy(
                    src_ref=piece(k, (*chip, 1 - c)), dst_ref=packed(rx1_ref.at[j], k),
                    send_sem=sa.at[j], recv_sem=ra.at[j], device_id=sib, device_id_type=MESH).start()

        def wait_a(j):
            return pltpu.make_async_remote_copy(src_ref=rx1_ref.at[j], dst_ref=rx1_ref.at[j], send_sem=sa.at[j],
                                                recv_sem=ra.at[j], device_id=me, device_id_type=MESH)

        def ici(j):
            return pltpu.make_async_remote_copy(
                src_ref=tx_buf.at[j - 1], dst_ref=rx2_ref.at[j - 1], send_sem=sb.at[j - 1], recv_sem=rb.at[j - 1],
                device_id=(*rel_chips[j], c), device_id_type=MESH)

        for j in (1, 2, 3, 0):
            loads = [pltpu.make_async_copy(piece(k, (*rel_chips[j], c)), packed(own_buf, k), lsem)
                     for k in range(len(PIECE_ROWS))]
            for cp in loads:
                cp.start()
            wait_a(j).wait_recv()
            got = pltpu.make_async_copy(rx1_ref.at[j], rx_buf, lsem)
            got.start()
            pltpu.make_async_copy(rx_buf, rx_buf, lsem).wait()
            got.wait()

            def add(i, carry, j=j):
                rows = pl.ds(pl.multiple_of(i * RS_CHUNK, 16), RS_CHUNK)
                tot = own_buf[rows, :].astype(f32) + rx_buf[rows, :].astype(f32)
                if j == 0:
                    acc[rows, :] = tot
                else:
                    tx_buf[j - 1, rows, :] = tot.astype(bf16)
                return carry

            lax.fori_loop(0, nchunk, add, 0)
            if j != 0:
                ici(j).start()

        for j in (1, 2, 3):
            ici(j).wait_recv()
            got = pltpu.make_async_copy(rx2_ref.at[j - 1], rx_buf, lsem)
            got.start()
            got.wait()

            def add2(i, carry):
                rows = pl.ds(pl.multiple_of(i * RS_CHUNK, 16), RS_CHUNK)
                acc[rows, :] += rx_buf[rows, :].astype(f32)
                return carry

            lax.fori_loop(0, nchunk, add2, 0)
        out = pltpu.make_async_copy(acc, red_ref, lsem)
        out.start()
        out.wait()
        for j in range(4):
            wait_a(j).wait_send()
        for j in (1, 2, 3):
            ici(j).wait_send()

    hbm = pl.BlockSpec(memory_space=pl.ANY)
    red, _, _ = pl.pallas_call(
        body, in_specs=[hbm] * 4, out_specs=[hbm] * 3,
        out_shape=(SDS((PACK_ROWS, D), f32), SDS((4, PACK_ROWS, D), bf16), SDS((3, PACK_ROWS, D), bf16)),
        scratch_shapes=[pltpu.VMEM((PACK_ROWS, D), bf16), pltpu.VMEM((PACK_ROWS, D), bf16),
                        pltpu.VMEM((3, PACK_ROWS, D), bf16), pltpu.VMEM((PACK_ROWS, D), f32),
                        pltpu.SemaphoreType.DMA((4,)), pltpu.SemaphoreType.DMA((4,)),
                        pltpu.SemaphoreType.DMA((3,)), pltpu.SemaphoreType.DMA((3,)), pltpu.SemaphoreType.DMA],
        compiler_params=pltpu.CompilerParams(has_side_effects=True, vmem_limit_bytes=VMEM_LIMIT_V7X),
        name="reduce_scatter_grads")(dw1, dwint, dwout, dw2)
    return red


def _all_reduce_small(packed):
    def body(p_ref, o_ref, gat, send_sems, recv_sems):
        x, y, c = _position()
        my = _dev_index(x, y, c)
        gat[my] = p_ref[...]
        cps = []
        for k in range(1, NDEV):
            peer = (x ^ (k >> 2), y ^ ((k >> 1) & 1), c ^ (k & 1))
            cps.append(pltpu.make_async_remote_copy(
                src_ref=p_ref, dst_ref=gat.at[my], send_sem=send_sems.at[k - 1], recv_sem=recv_sems.at[k - 1],
                device_id=peer, device_id_type=MESH))
        for cp in cps:
            cp.start()
        for cp in cps:
            cp.wait_recv()
        tot = gat[0]
        for d in range(1, NDEV):
            tot = tot + gat[d]
        o_ref[...] = tot
        loss = jnp.sum(tot[LOSS_ROW:LOSS_ROW + 1, :], axis=-1, keepdims=True)
        o_ref[LOSS_ROW:LOSS_ROW + 1, :] = jnp.broadcast_to(loss, (1, 128))
        for cp in cps:
            cp.wait_send()

    vm = pl.BlockSpec(memory_space=pltpu.VMEM)
    return pl.pallas_call(
        body, in_specs=[vm], out_specs=vm, out_shape=SDS((SMALL_ROWS, 128), f32),
        scratch_shapes=[pltpu.VMEM((NDEV, SMALL_ROWS, 128), f32),
                        pltpu.SemaphoreType.DMA((NDEV - 1,)), pltpu.SemaphoreType.DMA((NDEV - 1,))],
        compiler_params=pltpu.CompilerParams(has_side_effects=True),
        name="all_reduce_small")(packed)


def _adamw_math(w, g, m, v):
    m = ADAM_B1 * m + (1.0 - ADAM_B1) * g
    v = ADAM_B2 * v + (1.0 - ADAM_B2) * (g * g)
    m_hat = m / (1.0 - ADAM_B1 ** ADAM_STEP)
    v_hat = v / (1.0 - ADAM_B2 ** ADAM_STEP)
    delta = -ADAM_LR * (m_hat / (jnp.sqrt(v_hat) + ADAM_EPS) + ADAM_WD * w)
    return delta, m, v


def _adamw_rows(w, m, v, red, off, transposed, name):
    shape = w.shape
    r = shape[1] if transposed else shape[0]

    def body(w_ref, m_ref, v_ref, red_ref, g_ref, d_ref, nm_ref, nv_ref, gbuf, sem):
        cp = pltpu.make_async_copy(red_ref.at[pl.ds(off, r), :], gbuf, sem)
        cp.start()
        cp.wait()
        g = gbuf[...].T if transposed else gbuf[...]
        d, nm, nv = _adamw_math(w_ref[...], g, m_ref[...], v_ref[...])
        g_ref[...] = g
        d_ref[...] = d
        nm_ref[...] = nm
        nv_ref[...] = nv

    vm = pl.BlockSpec(memory_space=pltpu.VMEM)
    return pl.pallas_call(
        body, in_specs=[vm, vm, vm, pl.BlockSpec(memory_space=pl.ANY)], out_specs=[vm] * 4,
        out_shape=tuple(SDS(shape, f32) for _ in range(4)),
        scratch_shapes=[pltpu.VMEM((r, D), f32), pltpu.SemaphoreType.DMA],
        compiler_params=_cparams(None, VMEM_LIMIT_V7X), name=name)(w, m, v, red)


def _adamw_small(w, m, v, g, name):
    def body(w_ref, m_ref, v_ref, g_ref, d_ref, nm_ref, nv_ref):
        d, nm, nv = _adamw_math(w_ref[...], g_ref[...], m_ref[...], v_ref[...])
        d_ref[...] = d
        nm_ref[...] = nm
        nv_ref[...] = nv

    vm = pl.BlockSpec(memory_space=pltpu.VMEM)
    return pl.pallas_call(
        body, in_specs=[vm] * 4, out_specs=[vm] * 3,
        out_shape=tuple(SDS(w.shape, f32) for _ in range(3)), name=name)(w, m, v, g)


WEIGHTS = ("ffn1_norm", "ffn1_w_gate", "ffn1_w_up", "ffn1_w_down", "mix_norm", "w_in", "q_norm", "k_norm",
           "attn_sinks", "rel_bias", "pool_w", "pool_scale", "w_out", "ffn2_norm", "ffn2_w_gate", "ffn2_w_up",
           "ffn2_w_down")
BIG = (("ffn1_w_gate", True), ("ffn1_w_up", True), ("ffn1_w_down", False), ("w_in", True), ("w_out", False),
       ("ffn2_w_gate", True), ("ffn2_w_up", True), ("ffn2_w_down", False))


def kernel(x, ffn1_norm, ffn1_w_gate, ffn1_w_up, ffn1_w_down, mix_norm, w_in, q_norm, k_norm, attn_sinks, rel_bias, pool_w, pool_scale, w_out, ffn2_norm, ffn2_w_gate, ffn2_w_up, ffn2_w_down, loss_target, m_ffn1_norm, m_ffn1_w_gate, m_ffn1_w_up, m_ffn1_w_down, m_mix_norm, m_w_in, m_q_norm, m_k_norm, m_attn_sinks, m_rel_bias, m_pool_w, m_pool_scale, m_w_out, m_ffn2_norm, m_ffn2_w_gate, m_ffn2_w_up, m_ffn2_w_down, v_ffn1_norm, v_ffn1_w_gate, v_ffn1_w_up, v_ffn1_w_down, v_mix_norm, v_w_in, v_q_norm, v_k_norm, v_attn_sinks, v_rel_bias, v_pool_w, v_pool_scale, v_w_out, v_ffn2_norm, v_ffn2_w_gate, v_ffn2_w_up, v_ffn2_w_down):
    args = dict(locals())
    w = {n: args[n] for n in WEIGHTS}
    m = {n: args["m_" + n] for n in WEIGHTS}
    v = {n: args["v_" + n] for n in WEIGHTS}

    rows = [(w[n][0].T if tr else w[n][0]).astype(bf16) for n, tr in BIG]
    shard = jnp.concatenate(rows, axis=0)
    w1, wint, wout, w2 = _all_gather_weights(shard)

    gx, (dw1, dwint, dwout, dw2), small = _local_step(
        x[0], loss_target[0], w1, wint, wout, w2, ffn1_norm, mix_norm, ffn2_norm, q_norm, k_norm, attn_sinks,
        rel_bias, pool_w[0], pool_scale)

    red = _reduce_scatter_grads(dw1, dwint, dwout, dw2)
    small_tot = _all_reduce_small(_pack_small(small))

    grads, deltas, new_m, new_v = {}, {}, {}, {}
    for k, (n, tr) in enumerate(BIG):
        g, d, nm, nv = _adamw_rows(w[n][0], m[n][0], v[n][0], red, PIECE_OFF[k], tr, "adamw_" + n)
        grads[n], deltas[n], new_m[n], new_v[n] = g[None], d[None], nm[None], nv[None]
    small_names = [n for n in SMALL_NAMES if n != "loss"]
    ds, nms, nvs = _adamw_small(_pack_small({n: w[n] for n in small_names}), _pack_small({n: m[n] for n in small_names}),
                                _pack_small({n: v[n] for n in small_names}), small_tot, "adamw_small")
    for n in small_names:
        grads[n] = _unpack_small(small_tot, n)
        deltas[n], new_m[n], new_v[n] = _unpack_small(ds, n), _unpack_small(nms, n), _unpack_small(nvs, n)
    loss = small_tot[LOSS_ROW, 0]
    return (loss, gx[None], *[grads[n] for n in WEIGHTS], *[deltas[n] for n in WEIGHTS],
            *[new_m[n] for n in WEIGHTS], *[new_v[n] for n in WEIGHTS])
```

```python
import functools

import jax
import jax.numpy as jnp
import numpy as np
from jax import lax
from jax.experimental import pallas as pl
from jax.experimental.pallas import tpu as pltpu

f32, bf16, i32 = jnp.float32, jnp.bfloat16, jnp.int32
SDS = jax.ShapeDtypeStruct

D = 1024
F = 2816
HD = 64
NH = 8
NKV = 2
GQA = NH // NKV
DATTN = NH * HD
DKV = NKV * HD
DPOOL = 512
POOL_WINDOWS = (2, 4, 8, 16)
PGD = DPOOL // len(POOL_WINDOWS)
DIN = DATTN + 2 * DKV + DPOOL
DMIX = DATTN + DPOOL
BLK = 128
NBUCK = 32
MAX_DISTANCE = 128
EPS = 1e-6
NEG = -1e30
SCALE = HD ** -0.5

ADAM_LR, ADAM_B1, ADAM_B2, ADAM_EPS, ADAM_WD, ADAM_STEP = 0.001, 0.9, 0.999, 1e-08, 0.01, 10

NDEV = 8
FS = F // NDEV
INS = DIN // NDEV
OUTS = DMIX // NDEV
PIECE_ROWS = (FS, FS, FS, INS, OUTS, FS, FS, FS)
PIECE_OFF = tuple(int(v) for v in np.cumsum((0,) + PIECE_ROWS[:-1]))
PACK_ROWS = sum(PIECE_ROWS)

VMEM_LIMIT_V7X = 56 * 1024 * 1024

MESH = pl.DeviceIdType.MESH


def _cparams(sem=None, vmem=None):
    return pltpu.CompilerParams(dimension_semantics=sem, vmem_limit_bytes=vmem)


def _nt(a, b):
    return lax.dot_general(a, b, (((1,), (1,)), ((), ())), preferred_element_type=f32)


def _tn(a, b):
    return lax.dot_general(a, b, (((0,), (0,)), ((), ())), preferred_element_type=f32)


def _nn(a, b):
    return jnp.dot(a, b, preferred_element_type=f32)


def _sigmoid(x):
    return 1.0 / (1.0 + jnp.exp(-x))


def _norm_fwd(x, g, name):
    T = x.shape[0]
    tm = min(512, T)

    def body(x_ref, g_ref, h_ref):
        xv = x_ref[...]
        r = lax.rsqrt(jnp.mean(xv * xv, axis=-1, keepdims=True) + EPS)
        h_ref[...] = (xv * r * g_ref[...]).astype(bf16)

    return pl.pallas_call(
        body, grid=(T // tm,),
        in_specs=[pl.BlockSpec((tm, D), lambda i: (i, 0)), pl.BlockSpec((1, D), lambda i: (0, 0))],
        out_specs=pl.BlockSpec((tm, D), lambda i: (i, 0)),
        out_shape=SDS((T, D), bf16), name=name)(x, g)


def _norm_bwd(dh, x, g, dres, out_scale, name):
    T = x.shape[0]
    tm = min(512, T)

    def body(dh_ref, x_ref, g_ref, dr_ref, dx_ref, dxb_ref, dg_ref):
        i = pl.program_id(0)
        xv = x_ref[...]
        r = lax.rsqrt(jnp.mean(xv * xv, axis=-1, keepdims=True) + EPS)
        xh = xv * r
        dhv = dh_ref[...]
        dxh = dhv * g_ref[...]
        dx = dr_ref[...] + r * (dxh - xh * jnp.mean(dxh * xh, axis=-1, keepdims=True))
        dx_ref[...] = dx
        dxb_ref[...] = (out_scale * dx).astype(bf16)
        dg = jnp.sum(dhv * xh, axis=0, keepdims=True)

        @pl.when(i == 0)
        def _():
            dg_ref[...] = dg

        @pl.when(i > 0)
        def _():
            dg_ref[...] += dg

    tok = pl.BlockSpec((tm, D), lambda i: (i, 0))
    vec = pl.BlockSpec((1, D), lambda i: (0, 0))
    return pl.pallas_call(
        body, grid=(T // tm,),
        in_specs=[tok, tok, vec, tok], out_specs=[tok, tok, vec],
        out_shape=(SDS((T, D), f32), SDS((T, D), bf16), SDS((1, D), f32)),
        compiler_params=_cparams(("arbitrary",)), name=name)(dh, x, g, dres)


def _ffn_tiles(T):
    return min(1024, T), 256


def _ffn_fwd(h, w, x, name):
    T = h.shape[0]
    tm, tf = _ffn_tiles(T)
    nf = F // tf

    def body(h_ref, w_ref, x_ref, xo_ref, g_ref, u_ref, acc):
        fi = pl.program_id(1)
        hv = h_ref[...]
        gate = _nt(hv, w_ref[0])
        up = _nt(hv, w_ref[1])
        act = gate * _sigmoid(gate) * up
        g_ref[...] = gate.astype(bf16)
        u_ref[...] = up.astype(bf16)
        part = _nn(act.astype(bf16), w_ref[2])

        @pl.when(fi == 0)
        def _():
            acc[...] = part

        @pl.when(fi > 0)
        def _():
            acc[...] += part

        @pl.when(fi == nf - 1)
        def _():
            xo_ref[...] = x_ref[...] + 0.5 * acc[...]

    tok = pl.BlockSpec((tm, D), lambda i, f: (i, 0))
    act_spec = pl.BlockSpec((tm, tf), lambda i, f: (i, f))
    return pl.pallas_call(
        body, grid=(T // tm, nf),
        in_specs=[tok, pl.BlockSpec((3, tf, D), lambda i, f: (0, f, 0)), tok],
        out_specs=[tok, act_spec, act_spec],
        out_shape=(SDS((T, D), f32), SDS((T, F), bf16), SDS((T, F), bf16)),
        scratch_shapes=[pltpu.VMEM((tm, D), f32)],
        compiler_params=_cparams(("arbitrary", "arbitrary"), VMEM_LIMIT_V7X), name=name)(h, w, x)


def _ffn_bwd(dob, h, gate, up, w, name):
    T = h.shape[0]
    tm, tf = _ffn_tiles(T)
    nf, nt = F // tf, T // tm

    def body(do_ref, h_ref, g_ref, u_ref, w_ref, dh_ref, dw_ref, dwacc):
        fi, ti = pl.program_id(0), pl.program_id(1)
        dov = do_ref[...]
        gv = g_ref[...].astype(f32)
        uv = u_ref[...].astype(f32)
        sg = _sigmoid(gv)
        sil = gv * sg
        dact = _nt(dov, w_ref[2])
        dup = dact * sil
        dgate = dact * uv * (sg * (1.0 + gv * (1.0 - sg)))
        dgu = jnp.concatenate([dgate.astype(bf16), dup.astype(bf16)], axis=1)
        actb = (sil * uv).astype(bf16)
        wgu = w_ref[0:2].reshape(2 * tf, D)
        dh_part = _nn(dgu, wgu)
        rows = pl.ds(pl.multiple_of(ti * tm, tm), tm)

        @pl.when(fi == 0)
        def _():
            dh_ref[rows, :] = dh_part

        @pl.when(fi > 0)
        def _():
            dh_ref[rows, :] += dh_part

        dwgu = _tn(dgu, h_ref[...])
        dwd = _tn(actb, dov)

        @pl.when(ti == 0)
        def _():
            dwacc[0:2 * tf, :] = dwgu
            dwacc[2 * tf:3 * tf, :] = dwd

        @pl.when(ti > 0)
        def _():
            dwacc[0:2 * tf, :] += dwgu
            dwacc[2 * tf:3 * tf, :] += dwd

        @pl.when(ti == nt - 1)
        def _():
            dw_ref[...] = dwacc[...].reshape(3, tf, D).astype(bf16)

    tok = pl.BlockSpec((tm, D), lambda f, i: (i, 0))
    act_spec = pl.BlockSpec((tm, tf), lambda f, i: (i, f))
    wspec = pl.BlockSpec((3, tf, D), lambda f, i: (0, f, 0))
    return pl.pallas_call(
        body, grid=(nf, nt),
        in_specs=[tok, tok, act_spec, act_spec, wspec],
        out_specs=[pl.BlockSpec((T, D), lambda f, i: (0, 0)), wspec],
        out_shape=(SDS((T, D), f32), SDS((3, F, D), bf16)),
        scratch_shapes=[pltpu.VMEM((3 * tf, D), f32)],
        compiler_params=_cparams(("arbitrary", "arbitrary"), VMEM_LIMIT_V7X), name=name)(dob, h, gate, up, w)


def _loss_grad(y, target, name):
    T = y.shape[0]
    tm = min(512, T)

    def body(y_ref, t_ref, dy_ref, dyb_ref, l_ref):
        i = pl.program_id(0)
        e = y_ref[...] - t_ref[...]
        dy = e * (1.0 / D)
        dy_ref[...] = dy
        dyb_ref[...] = (0.5 * dy).astype(bf16)
        col = jnp.sum(e * e, axis=0, keepdims=True) * (0.5 / D)
        lanes = col[:, 0:128]
        for k in range(1, D // 128):
            lanes = lanes + col[:, 128 * k:128 * (k + 1)]

        @pl.when(i == 0)
        def _():
            l_ref[...] = lanes

        @pl.when(i > 0)
        def _():
            l_ref[...] += lanes

    tok = pl.BlockSpec((tm, D), lambda i: (i, 0))
    return pl.pallas_call(
        body, grid=(T // tm,), in_specs=[tok, tok],
        out_specs=[tok, tok, pl.BlockSpec((1, 128), lambda i: (0, 0))],
        out_shape=(SDS((T, D), f32), SDS((T, D), bf16), SDS((1, 128), f32)),
        compiler_params=_cparams(("arbitrary",)), name=name)(y, target)


def _in_proj_fwd(h, wint, name):
    T = h.shape[0]
    tm = min(512, T)

    def body(h_ref, w_ref, z_ref):
        z_ref[...] = _nt(h_ref[...], w_ref[...])

    return pl.pallas_call(
        body, grid=(T // tm,),
        in_specs=[pl.BlockSpec((tm, D), lambda i: (i, 0)), pl.BlockSpec((DIN, D), lambda i: (0, 0))],
        out_specs=pl.BlockSpec((tm, DIN), lambda i: (i, 0)),
        out_shape=SDS((T, DIN), f32), name=name)(h, wint)


def _in_proj_bwd(dz, wint, h, name):
    T = h.shape[0]
    tm = min(512, T)
    nt = T // tm

    def body(dz_ref, w_ref, h_ref, dh_ref, dw_ref, acc):
        i = pl.program_id(0)
        dzb = dz_ref[...].astype(bf16)
        dh_ref[...] = _nn(dzb, w_ref[...])
        part = _tn(dzb, h_ref[...])

        @pl.when(i == 0)
        def _():
            acc[...] = part

        @pl.when(i > 0)
        def _():
            acc[...] += part

        @pl.when(i == nt - 1)
        def _():
            dw_ref[...] = acc[...].astype(bf16)

    wspec = pl.BlockSpec((DIN, D), lambda i: (0, 0))
    return pl.pallas_call(
        body, grid=(nt,),
        in_specs=[pl.BlockSpec((tm, DIN), lambda i: (i, 0)), wspec, pl.BlockSpec((tm, D), lambda i: (i, 0))],
        out_specs=[pl.BlockSpec((tm, D), lambda i: (i, 0)), wspec],
        out_shape=(SDS((T, D), f32), SDS((DIN, D), bf16)),
        scratch_shapes=[pltpu.VMEM((DIN, D), f32)],
        compiler_params=_cparams(("arbitrary",)), name=name)(dz, wint, h)


def _out_proj_fwd(ymix, wout, x, name):
    T = x.shape[0]
    tm = min(512, T)

    def body(y_ref, w_ref, x_ref, o_ref):
        o_ref[...] = x_ref[...] + _nn(y_ref[...], w_ref[...])

    tok = pl.BlockSpec((tm, D), lambda i: (i, 0))
    return pl.pallas_call(
        body, grid=(T // tm,),
        in_specs=[pl.BlockSpec((tm, DMIX), lambda i: (i, 0)), pl.BlockSpec((DMIX, D), lambda i: (0, 0)), tok],
        out_specs=tok, out_shape=SDS((T, D), f32), name=name)(ymix, wout, x)


def _out_proj_bwd(dxb, wout, ymix, name):
    T = dxb.shape[0]
    tm = min(512, T)
    nt = T // tm

    def body(dx_ref, w_ref, y_ref, dy_ref, dw_ref, acc):
        i = pl.program_id(0)
        dxv = dx_ref[...]
        dy_ref[...] = _nt(dxv, w_ref[...])
        part = _tn(y_ref[...], dxv)

        @pl.when(i == 0)
        def _():
            acc[...] = part

        @pl.when(i > 0)
        def _():
            acc[...] += part

        @pl.when(i == nt - 1)
        def _():
            dw_ref[...] = acc[...].astype(bf16)

    wspec = pl.BlockSpec((DMIX, D), lambda i: (0, 0))
    return pl.pallas_call(
        body, grid=(nt,),
        in_specs=[pl.BlockSpec((tm, D), lambda i: (i, 0)), wspec, pl.BlockSpec((tm, DMIX), lambda i: (i, 0))],
        out_specs=[pl.BlockSpec((tm, DMIX), lambda i: (i, 0)), wspec],
        out_shape=(SDS((T, DMIX), f32), SDS((DMIX, D), bf16)),
        scratch_shapes=[pltpu.VMEM((DMIX, D), f32)],
        compiler_params=_cparams(("arbitrary",)), name=name)(dxb, wout, ymix)


def _t5_bucket_table():
    ql = np.arange(BLK)[:, None]
    kl = np.arange(2 * BLK)[None, :]
    n = np.maximum(ql + BLK - kl, 0)
    max_exact = NBUCK // 2
    large = max_exact + (np.log(np.maximum(n, 1) / max_exact) / np.log(MAX_DISTANCE / max_exact)
                         * (NBUCK - max_exact)).astype(np.int32)
    large = np.minimum(large, NBUCK - 1)
    return np.where(n < max_exact, n, large).astype(np.int32)


def _fill_bias(bk_ref, rb_ref, bias_scr):
    bk = bk_ref[...]
    for h in range(NH):
        def step(b, acc, h=h):
            return acc + jnp.where(bk == b, rb_ref[b, h], 0.0)
        bias_scr[h] = lax.fori_loop(0, NBUCK, step, jnp.zeros((BLK, 2 * BLK), f32))


def _attn_probs(zc_ref, zp_ref, kh, qg, kg, sk_ref, bias_scr, n):
    kc = DATTN + HD * kh
    vc = DATTN + DKV + HD * kh
    kx = jnp.concatenate([zp_ref[:, kc:kc + HD], zc_ref[:, kc:kc + HD]], axis=0)
    vx = jnp.concatenate([zp_ref[:, vc:vc + HD], zc_ref[:, vc:vc + HD]], axis=0)
    qx = jnp.concatenate([zc_ref[:, HD * (GQA * kh + g):HD * (GQA * kh + g + 1)] for g in range(GQA)], axis=0)
    rq = lax.rsqrt(jnp.mean(qx * qx, axis=-1, keepdims=True) + EPS)
    rk = lax.rsqrt(jnp.mean(kx * kx, axis=-1, keepdims=True) + EPS)
    qhat, khat = qx * rq, kx * rk
    qnb, knb = (qhat * qg).astype(bf16), (khat * kg).astype(bf16)
    s = _nt(qnb, knb) * SCALE + bias_scr[GQA * kh:GQA * (kh + 1)].reshape(GQA * BLK, 2 * BLK)
    row = lax.broadcasted_iota(i32, (GQA * BLK, 2 * BLK), 0) & (BLK - 1)
    col = lax.broadcasted_iota(i32, (GQA * BLK, 2 * BLK), 1)
    mask = (col > row) & (col <= row + BLK) & ((col >= BLK) | (n > 0))
    s = jnp.where(mask, s, NEG)
    ridx = lax.broadcasted_iota(i32, (GQA * BLK, 1), 0)
    sink = jnp.full((GQA * BLK, 1), sk_ref[GQA * kh + GQA - 1], f32)
    for g in range(GQA - 2, -1, -1):
        sink = jnp.where(ridx < (g + 1) * BLK, sk_ref[GQA * kh + g], sink)
    m = jnp.maximum(jnp.max(s, axis=-1, keepdims=True), sink)
    e = jnp.exp(s - m)
    es = jnp.exp(sink - m)
    den = jnp.sum(e, axis=-1, keepdims=True) + es
    return dict(p=e / den, psink=es / den, qhat=qhat, khat=khat, rq=rq, rk=rk, qnb=qnb, knb=knb, vb=vx.astype(bf16))


def _pool_group(zc_ref, zp_ref, g, w, n):
    c0 = DATTN + 2 * DKV + PGD * g
    uc = zc_ref[:, c0:c0 + PGD]
    up = jnp.where(n > 0, zp_ref[:, c0:c0 + PGD], 0.0)
    ue = jnp.concatenate([up, uc], axis=0)
    hi = ue.astype(bf16)
    lo = (ue - hi.astype(f32)).astype(bf16)
    t = lax.broadcasted_iota(i32, (BLK, 2 * BLK), 0)
    s = lax.broadcasted_iota(i32, (BLK, 2 * BLK), 1)
    band = jnp.where((s <= t + BLK) & (s > t + BLK - w), 1.0, 0.0).astype(bf16)
    sm = _nn(band, hi) + _nn(band, lo)
    pos = n * BLK + lax.broadcasted_iota(i32, (BLK, 1), 0) + 1
    cnt = jnp.minimum(pos, w).astype(f32)
    return sm / cnt - uc, band, cnt


def _mix_fwd(z, qg, kg, sinks, relb, bucket, pool_w, pscale, name):
    T = z.shape[0]
    nb = T // BLK

    def body(zc_ref, zp_ref, qg_ref, kg_ref, sk_ref, rb_ref, bk_ref, pw_ref, ps_ref, y_ref, bias_scr, yacc):
        n = pl.program_id(0)

        @pl.when(n == 0)
        def _():
            _fill_bias(bk_ref, rb_ref, bias_scr)

        for kh in range(NKV):
            a = _attn_probs(zc_ref, zp_ref, kh, qg_ref[...], kg_ref[...], sk_ref, bias_scr, n)
            o = _nn(a["p"].astype(bf16), a["vb"])
            for g in range(GQA):
                hc = HD * (GQA * kh + g)
                yacc[:, hc:hc + HD] = o[g * BLK:(g + 1) * BLK]
        for g, w in enumerate(POOL_WINDOWS):
            pooled, _, _ = _pool_group(zc_ref, zp_ref, g, w, n)
            yp = _nn(pooled.astype(bf16), pw_ref[g].astype(bf16)) * ps_ref[:, PGD * g:PGD * (g + 1)]
            yacc[:, DATTN + PGD * g:DATTN + PGD * (g + 1)] = yp
        y_ref[...] = yacc[...].astype(bf16)

    full = lambda *shape: pl.BlockSpec(shape, lambda n: (0,) * len(shape))
    smem = pl.BlockSpec(memory_space=pltpu.SMEM)
    return pl.pallas_call(
        body, grid=(nb,),
        in_specs=[pl.BlockSpec((BLK, DIN), lambda n: (n, 0)),
                  pl.BlockSpec((BLK, DIN), lambda n: (jnp.maximum(n - 1, 0), 0)),
                  full(1, HD), full(1, HD), smem, smem, full(BLK, 2 * BLK),
                  full(len(POOL_WINDOWS), PGD, PGD), full(1, DPOOL)],
        out_specs=pl.BlockSpec((BLK, DMIX), lambda n: (n, 0)),
        out_shape=SDS((T, DMIX), bf16),
        scratch_shapes=[pltpu.VMEM((NH, BLK, 2 * BLK), f32), pltpu.VMEM((BLK, DMIX), f32)],
        compiler_params=_cparams(("arbitrary",)), name=name)(z, z, qg, kg, sinks, relb, bucket, pool_w, pscale)


def _mix_bwd(z, dy, qg, kg, sinks, relb, bucket, pool_w, pscale, name):
    T = z.shape[0]
    nb = T // BLK

    def body(zc_ref, zp_ref, dy_ref, qg_ref, kg_ref, sk_ref, rb_ref, bk_ref, pw_ref, ps_ref,
             dz_ref, dqg_ref, dkg_ref, dsk_ref, drb_ref, dpw_ref, dps_ref, bias_scr, dbias_scr):
        n = pl.program_id(0)
        rows = pl.ds(pl.multiple_of(n * BLK, BLK), BLK)
        prow = pl.ds(pl.multiple_of(jnp.maximum(n - 1, 0) * BLK, BLK), BLK)

        @pl.when(n == 0)
        def _():
            _fill_bias(bk_ref, rb_ref, bias_scr)
            dbias_scr[...] = jnp.zeros_like(dbias_scr)
            dqg_ref[...] = jnp.zeros_like(dqg_ref)
            dkg_ref[...] = jnp.zeros_like(dkg_ref)
            dsk_ref[...] = jnp.zeros_like(dsk_ref)
            dpw_ref[...] = jnp.zeros_like(dpw_ref)
            dps_ref[...] = jnp.zeros_like(dps_ref)

        qg, kg = qg_ref[...], kg_ref[...]
        lane = lax.broadcasted_iota(i32, (1, 128), 1)
        dsk = jnp.zeros((1, 128), f32)
        for kh in range(NKV):
            a = _attn_probs(zc_ref, zp_ref, kh, qg, kg, sk_ref, bias_scr, n)
            p = a["p"]
            do = jnp.concatenate([dy_ref[:, HD * (GQA * kh + g):HD * (GQA * kh + g + 1)] for g in range(GQA)],
                                 axis=0).astype(bf16)
            dv = _tn(p.astype(bf16), do)
            dp = _nt(do, a["vb"])
            delta = jnp.sum(p * dp, axis=-1, keepdims=True)
            ds = p * (dp - delta)
            sinkterm = a["psink"] * delta
            for g in range(GQA):
                h = GQA * kh + g
                dbias_scr[h] += ds[g * BLK:(g + 1) * BLK]
                tot = jnp.sum(sinkterm[g * BLK:(g + 1) * BLK], axis=0, keepdims=True)
                dsk = dsk - jnp.where(lane == h, tot, 0.0)
            dsb = ds.astype(bf16)
            dqn = _nn(dsb, a["knb"]) * SCALE
            dkn = _tn(dsb, a["qnb"]) * SCALE
            qhat, khat = a["qhat"], a["khat"]
            dqg_ref[...] += jnp.sum(dqn * qhat, axis=0, keepdims=True)
            dkg_ref[...] += jnp.sum(dkn * khat, axis=0, keepdims=True)
            dqh = dqn * qg
            dq = a["rq"] * (dqh - qhat * jnp.mean(dqh * qhat, axis=-1, keepdims=True))
            dkh = dkn * kg
            dk = a["rk"] * (dkh - khat * jnp.mean(dkh * khat, axis=-1, keepdims=True))
            kc = DATTN + HD * kh
            vc = DATTN + DKV + HD * kh
            for g in range(GQA):
                hc = HD * (GQA * kh + g)
                dz_ref[rows, hc:hc + HD] = dq[g * BLK:(g + 1) * BLK]
            dz_ref[rows, kc:kc + HD] = dk[BLK:2 * BLK]
            dz_ref[rows, vc:vc + HD] = dv[BLK:2 * BLK]

            @pl.when(n > 0)
            def _(dk=dk, dv=dv, kc=kc, vc=vc):
                dz_ref[prow, kc:kc + HD] += dk[0:BLK]
                dz_ref[prow, vc:vc + HD] += dv[0:BLK]

        dsk_ref[...] += dsk

        for g, w in enumerate(POOL_WINDOWS):
            c0 = DATTN + 2 * DKV + PGD * g
            pooled, band, cnt = _pool_group(zc_ref, zp_ref, g, w, n)
            pb = pooled.astype(bf16)
            wb = pw_ref[g].astype(bf16)
            dyp = dy_ref[:, DATTN + PGD * g:DATTN + PGD * (g + 1)]
            ypre = _nn(pb, wb)
            dps_ref[:, PGD * g:PGD * (g + 1)] += jnp.sum(dyp * ypre, axis=0, keepdims=True)
            dyg = (dyp * ps_ref[:, PGD * g:PGD * (g + 1)]).astype(bf16)
            dpw_ref[g] += _tn(pb, dyg)
            dpooled = _nt(dyg, wb)
            dsm = dpooled / cnt
            hi = dsm.astype(bf16)
            lo = (dsm - hi.astype(f32)).astype(bf16)
            due = _tn(band, hi) + _tn(band, lo)
            dz_ref[rows, c0:c0 + PGD] = due[BLK:2 * BLK] - dpooled

            @pl.when(n > 0)
            def _(due=due, c0=c0):
                dz_ref[prow, c0:c0 + PGD] += due[0:BLK]

        @pl.when(n == nb - 1)
        def _():
            bk = bk_ref[...]
            ri = lax.broadcasted_iota(i32, (NBUCK, NH), 0)
            ci = lax.broadcasted_iota(i32, (NBUCK, NH), 1)

            def step(b, acc):
                for h in range(NH):
                    sel = jnp.where(bk == b, dbias_scr[h], 0.0)
                    tot = jnp.sum(jnp.sum(sel, axis=1, keepdims=True), axis=0, keepdims=True)
                    acc = acc + jnp.where((ri == b) & (ci == h), tot, 0.0)
                return acc

            drb_ref[...] = lax.fori_loop(0, NBUCK, step, jnp.zeros((NBUCK, NH), f32))

    full = lambda *shape: pl.BlockSpec(shape, lambda n: (0,) * len(shape))
    smem = pl.BlockSpec(memory_space=pltpu.SMEM)
    npg = len(POOL_WINDOWS)
    return pl.pallas_call(
        body, grid=(nb,),
        in_specs=[pl.BlockSpec((BLK, DIN), lambda n: (n, 0)),
                  pl.BlockSpec((BLK, DIN), lambda n: (jnp.maximum(n - 1, 0), 0)),
                  pl.BlockSpec((BLK, DMIX), lambda n: (n, 0)),
                  full(1, HD), full(1, HD), smem, smem, full(BLK, 2 * BLK), full(npg, PGD, PGD), full(1, DPOOL)],
        out_specs=[full(T, DIN), full(1, HD), full(1, HD), full(1, 128), full(NBUCK, NH),
                   full(npg, PGD, PGD), full(1, DPOOL)],
        out_shape=(SDS((T, DIN), f32), SDS((1, HD), f32), SDS((1, HD), f32), SDS((1, 128), f32),
                   SDS((NBUCK, NH), f32), SDS((npg, PGD, PGD), f32), SDS((1, DPOOL), f32)),
        scratch_shapes=[pltpu.VMEM((NH, BLK, 2 * BLK), f32), pltpu.VMEM((NH, BLK, 2 * BLK), f32)],
        compiler_params=_cparams(("arbitrary",), VMEM_LIMIT_V7X),
        name=name)(z, z, dy, qg, kg, sinks, relb, bucket, pool_w, pscale)


def _local_step(x, target, w1, wint, wout, w2, g1, gm, g3, qg, kg, sinks, relb, pool_w, pscale):
    bucket = jnp.asarray(_t5_bucket_table())
    sk = sinks.reshape(NH)
    h1 = _norm_fwd(x, g1, "norm1_fwd")
    x1, gate1, up1 = _ffn_fwd(h1, w1, x, "ffn1_fwd")
    h2 = _norm_fwd(x1, gm, "norm2_fwd")
    z = _in_proj_fwd(h2, wint, "in_proj_fwd")
    ymix = _mix_fwd(z, qg, kg, sk, relb, bucket, pool_w, pscale, "mix_fwd")
    x2 = _out_proj_fwd(ymix, wout, x1, "out_proj_fwd")
    h3 = _norm_fwd(x2, g3, "norm3_fwd")
    y, gate2, up2 = _ffn_fwd(h3, w2, x2, "ffn2_fwd")
    dy, dyb, loss_lanes = _loss_grad(y, target, "loss_grad")

    dh3, dw2 = _ffn_bwd(dyb, h3, gate2, up2, w2, "ffn2_bwd")
    dx2, dx2b, dg3 = _norm_bwd(dh3, x2, g3, dy, 1.0, "norm3_bwd")
    dymix, dwout = _out_proj_bwd(dx2b, wout, ymix, "out_proj_bwd")
    dz, dqg, dkg, dsk, drb, dpw, dps = _mix_bwd(z, dymix, qg, kg, sk, relb, bucket, pool_w, pscale, "mix_bwd")
    dh2, dwint = _in_proj_bwd(dz, wint, h2, "in_proj_bwd")
    dx1, dx1b, dgm = _norm_bwd(dh2, x1, gm, dx2, 0.5, "norm2_bwd")
    dh1, dw1 = _ffn_bwd(dx1b, h1, gate1, up1, w1, "ffn1_bwd")
    gx, _, dg1 = _norm_bwd(dh1, x, g1, dx1, 1.0, "norm1_bwd")
    small = dict(ffn1_norm=dg1, mix_norm=dgm, ffn2_norm=dg3, pool_scale=dps, q_norm=dqg, k_norm=dkg,
                 attn_sinks=dsk[:, :NH], rel_bias=drb, pool_w=dpw, loss=loss_lanes)
    return gx, (dw1, dwint, dwout, dw2), small


SMALL_NAMES = ("ffn1_norm", "mix_norm", "ffn2_norm", "pool_scale", "q_norm", "k_norm", "attn_sinks", "rel_bias",
               "pool_w", "loss")
SMALL_SHAPES = dict(ffn1_norm=(1, D), mix_norm=(1, D), ffn2_norm=(1, D), pool_scale=(1, DPOOL), q_norm=(1, HD),
                    k_norm=(1, HD), attn_sinks=(1, NH), rel_bias=(NBUCK, NH),
                    pool_w=(1, len(POOL_WINDOWS), PGD, PGD), loss=(1, 128))


def _small_rows(name):
    return -(-int(np.prod(SMALL_SHAPES[name])) // 128)


SMALL_OFF = {}
_r = 0
for _n in SMALL_NAMES:
    SMALL_OFF[_n] = _r
    _r += _small_rows(_n)
SMALL_ROWS = -(-_r // 8) * 8
LOSS_ROW = SMALL_OFF["loss"]


def _pack_small(vals):
    parts = []
    for n in SMALL_NAMES:
        size = _small_rows(n) * 128
        if n in vals:
            flat = vals[n].astype(f32).reshape(-1)
            parts.append(jnp.pad(flat, (0, size - flat.shape[0])))
        else:
            parts.append(jnp.zeros((size,), f32))
    flat = jnp.concatenate(parts)
    flat = jnp.pad(flat, (0, SMALL_ROWS * 128 - flat.shape[0]))
    return flat.reshape(SMALL_ROWS, 128)


def _unpack_small(packed, name):
    size = int(np.prod(SMALL_SHAPES[name]))
    r0 = SMALL_OFF[name]
    return packed[r0:r0 + _small_rows(name)].reshape(-1)[:size].reshape(SMALL_SHAPES[name])


def _position():
    return lax.axis_index("x"), lax.axis_index("y"), lax.axis_index("c")


def _dev_index(x, y, c):
    return 4 * x + 2 * y + c


def _all_gather_weights(shard):
    def body(s_ref, w1_ref, wi_ref, wo_ref, w2_ref, send_sems, recv_sems, local_sem):
        x, y, c = _position()
        me, sib = (x, y, c), (x, y, 1 - c)
        chips = [(1 - x, y), (x, 1 - y), (1 - x, 1 - y)]
        outs = (w1_ref.at[0], w1_ref.at[1], w1_ref.at[2], wi_ref, wo_ref, w2_ref.at[0], w2_ref.at[1], w2_ref.at[2])

        def piece_dst(k, dev):
            r = PIECE_ROWS[k]
            return outs[k].at[pl.ds(pl.multiple_of(_dev_index(*dev) * r, 16), r), :]

        def piece_src(k):
            return s_ref.at[pl.ds(PIECE_OFF[k], PIECE_ROWS[k]), :]

        def copies(rel, block, to, from_shard):
            return [pltpu.make_async_remote_copy(
                src_ref=piece_src(k) if from_shard else piece_dst(k, block), dst_ref=piece_dst(k, block),
                send_sem=send_sems.at[rel], recv_sem=recv_sems.at[rel], device_id=to, device_id_type=MESH)
                for k in range(len(PIECE_ROWS))]

        def whole(rel):
            return pltpu.make_async_remote_copy(src_ref=s_ref, dst_ref=s_ref, send_sem=send_sems.at[rel],
                                                recv_sem=recv_sems.at[rel], device_id=me, device_id_type=MESH)

        mine = [pltpu.make_async_copy(piece_src(k), piece_dst(k, me), local_sem) for k in range(len(PIECE_ROWS))]
        for cp in mine:
            cp.start()
        for cp in copies(0, me, sib, True):
            cp.start()
        for j, chip in enumerate(chips):
            for cp in copies(1 + j, me, (*chip, c), True):
                cp.start()
        for j, chip in enumerate(chips):
            whole(1 + j).wait_recv()
            for cp in copies(4 + j, (*chip, c), sib, False):
                cp.start()
        whole(0).wait_recv()
        for j in range(3):
            whole(4 + j).wait_recv()
        for rel in range(7):
            whole(rel).wait_send()
        pltpu.make_async_copy(s_ref, s_ref, local_sem).wait()

    hbm = pl.BlockSpec(memory_space=pl.ANY)
    return pl.pallas_call(
        body, in_specs=[hbm], out_specs=[hbm] * 4,
        out_shape=(SDS((3, F, D), bf16), SDS((DIN, D), bf16), SDS((DMIX, D), bf16), SDS((3, F, D), bf16)),
        scratch_shapes=[pltpu.SemaphoreType.DMA((7,)), pltpu.SemaphoreType.DMA((7,)), pltpu.SemaphoreType.DMA],
        compiler_params=pltpu.CompilerParams(has_side_effects=True),
        name="all_gather_weights")(shard)


RS_CHUNK = 240


def _reduce_scatter_grads(dw1, dwint, dwout, dw2):
    nchunk = PACK_ROWS // RS_CHUNK

    def body(d1_ref, di_ref, do_ref, d2_ref, red_ref, rx1_ref, rx2_ref,
             own_buf, rx_buf, tx_buf, acc, sa, ra, sb, rb, lsem):
        x, y, c = _position()
        me, sib = (x, y, c), (x, y, 1 - c)
        rel_chips = [(x, y), (1 - x, y), (x, 1 - y), (1 - x, 1 - y)]
        srcs = (d1_ref.at[0], d1_ref.at[1], d1_ref.at[2], di_ref, do_ref, d2_ref.at[0], d2_ref.at[1], d2_ref.at[2])

        def piece(k, dev):
            r = PIECE_ROWS[k]
            return srcs[k].at[pl.ds(pl.multiple_of(_dev_index(*dev) * r, 16), r), :]

        def packed(ref, k):
            return ref.at[pl.ds(PIECE_OFF[k], PIECE_ROWS[k]), :]

        for j, chip in enumerate(rel_chips):
            for k in range(len(PIECE_ROWS)):
                pltpu.make_async_remote_copy(
                    src_ref=piece(k, (*chip, 1 - c)), dst_ref=packed(rx1_ref.at[j], k),
                    send_sem=sa.at[j], recv_sem=ra.at[j], device_id=sib, device_id_type=MESH).start()

        def wait_a(j):
            return pltpu.make_async_remote_copy(src_ref=rx1_ref.at[j], dst_ref=rx1_ref.at[j], send_sem=sa.at[j],
                                                recv_sem=ra.at[j], device_id=me, device_id_type=MESH)

        def ici(j):
            return pltpu.make_async_remote_copy(
                src_ref=tx_buf.at[j - 1], dst_ref=rx2_ref.at[j - 1], send_sem=sb.at[j - 1], recv_sem=rb.at[j - 1],
                device_id=(*rel_chips[j], c), device_id_type=MESH)

        for j in (1, 2, 3, 0):
            loads = [pltpu.make_async_copy(piece(k, (*rel_chips[j], c)), packed(own_buf, k), lsem)
                     for k in range(len(PIECE_ROWS))]
            for cp in loads:
                cp.start()
            wait_a(j).wait_recv()
            got = pltpu.make_async_copy(rx1_ref.at[j], rx_buf, lsem)
            got.start()
            pltpu.make_async_copy(rx_buf, rx_buf, lsem).wait()
            got.wait()

            def add(i, carry, j=j):
                rows = pl.ds(pl.multiple_of(i * RS_CHUNK, 16), RS_CHUNK)
                tot = own_buf[rows, :].astype(f32) + rx_buf[rows, :].astype(f32)
                if j == 0:
                    acc[rows, :] = tot
                else:
                    tx_buf[j - 1, rows, :] = tot.astype(bf16)
                return carry

            lax.fori_loop(0, nchunk, add, 0)
            if j != 0:
                ici(j).start()

        for j in (1, 2, 3):
            ici(j).wait_recv()
            got = pltpu.make_async_copy(rx2_ref.at[j - 1], rx_buf, lsem)
            got.start()
            got.wait()

            def add2(i, carry):
                rows = pl.ds(pl.multiple_of(i * RS_CHUNK, 16), RS_CHUNK)
                acc[rows, :] += rx_buf[rows, :].astype(f32)
                return carry

            lax.fori_loop(0, nchunk, add2, 0)
        out = pltpu.make_async_copy(acc, red_ref, lsem)
        out.start()
        out.wait()
        for j in range(4):
            wait_a(j).wait_send()
        for j in (1, 2, 3):
            ici(j).wait_send()

    hbm = pl.BlockSpec(memory_space=pl.ANY)
    red, _, _ = pl.pallas_call(
        body, in_specs=[hbm] * 4, out_specs=[hbm] * 3,
        out_shape=(SDS((PACK_ROWS, D), f32), SDS((4, PACK_ROWS, D), bf16), SDS((3, PACK_ROWS, D), bf16)),
        scratch_shapes=[pltpu.VMEM((PACK_ROWS, D), bf16), pltpu.VMEM((PACK_ROWS, D), bf16),
                        pltpu.VMEM((3, PACK_ROWS, D), bf16), pltpu.VMEM((PACK_ROWS, D), f32),
                        pltpu.SemaphoreType.DMA((4,)), pltpu.SemaphoreType.DMA((4,)),
                        pltpu.SemaphoreType.DMA((3,)), pltpu.SemaphoreType.DMA((3,)), pltpu.SemaphoreType.DMA],
        compiler_params=pltpu.CompilerParams(has_side_effects=True, vmem_limit_bytes=VMEM_LIMIT_V7X),
        name="reduce_scatter_grads")(dw1, dwint, dwout, dw2)
    return red


def _all_reduce_small(packed):
    def body(p_ref, o_ref, pair, chips, send_sems, recv_sems):
        x, y, c = _position()
        chip = 2 * x + y
        pair[c] = p_ref[...]
        swap = pltpu.make_async_remote_copy(
            src_ref=p_ref, dst_ref=pair.at[c], send_sem=send_sems.at[0], recv_sem=recv_sems.at[0],
            device_id=(x, y, 1 - c), device_id_type=MESH)
        swap.start()
        swap.wait_recv()
        chips[chip] = pair[0] + pair[1]
        cps = [pltpu.make_async_remote_copy(
            src_ref=chips.at[chip], dst_ref=chips.at[chip], send_sem=send_sems.at[1 + j], recv_sem=recv_sems.at[1 + j],
            device_id=(*other, c), device_id_type=MESH)
            for j, other in enumerate([(1 - x, y), (x, 1 - y), (1 - x, 1 - y)])]
        for cp in cps:
            cp.start()
        for cp in cps:
            cp.wait_recv()
        tot = (chips[0] + chips[1]) + (chips[2] + chips[3])
        o_ref[...] = tot
        loss = jnp.sum(tot[LOSS_ROW:LOSS_ROW + 1, :], axis=-1, keepdims=True)
        o_ref[LOSS_ROW:LOSS_ROW + 1, :] = jnp.broadcast_to(loss, (1, 128))
        swap.wait_send()
        for cp in cps:
            cp.wait_send()

    vm = pl.BlockSpec(memory_space=pltpu.VMEM)
    return pl.pallas_call(
        body, in_specs=[vm], out_specs=vm, out_shape=SDS((SMALL_ROWS, 128), f32),
        scratch_shapes=[pltpu.VMEM((2, SMALL_ROWS, 128), f32), pltpu.VMEM((4, SMALL_ROWS, 128), f32),
                        pltpu.SemaphoreType.DMA((4,)), pltpu.SemaphoreType.DMA((4,))],
        compiler_params=pltpu.CompilerParams(has_side_effects=True),
        name="all_reduce_small")(packed)


def _adamw_math(w, g, m, v):
    m = ADAM_B1 * m + (1.0 - ADAM_B1) * g
    v = ADAM_B2 * v + (1.0 - ADAM_B2) * (g * g)
    m_hat = m / (1.0 - ADAM_B1 ** ADAM_STEP)
    v_hat = v / (1.0 - ADAM_B2 ** ADAM_STEP)
    delta = -ADAM_LR * (m_hat / (jnp.sqrt(v_hat) + ADAM_EPS) + ADAM_WD * w)
    return delta, m, v


def _adamw_big(ws, ms, vs, red):
    npiece = len(BIG)
    shapes = [tuple(w.shape[1:]) for w in ws]
    classes = sorted(set(shapes))

    def body(*refs):
        w_refs, m_refs, v_refs = refs[0:npiece], refs[npiece:2 * npiece], refs[2 * npiece:3 * npiece]
        red_ref = refs[3 * npiece]
        out_refs = refs[3 * npiece + 1:7 * npiece + 1]
        scr = refs[7 * npiece + 1:]
        bufs = {cls: scr[8 * i:8 * i + 8] for i, cls in enumerate(classes)}
        sems = scr[8 * len(classes)]
        for k, (_, transposed) in enumerate(BIG):
            wb, mb, vb, gb, o0, o1, o2, o3 = bufs[shapes[k]]
            r = PIECE_ROWS[k]
            loads = [pltpu.make_async_copy(w_refs[k].at[0], wb, sems.at[0]),
                     pltpu.make_async_copy(m_refs[k].at[0], mb, sems.at[1]),
                     pltpu.make_async_copy(v_refs[k].at[0], vb, sems.at[2]),
                     pltpu.make_async_copy(red_ref.at[pl.ds(PIECE_OFF[k], r), :], gb, sems.at[3])]
            for cp in loads:
                cp.start()
            for cp in loads:
                cp.wait()
            g = gb[...].T if transposed else gb[...]
            d, nm, nv = _adamw_math(wb[...], g, mb[...], vb[...])
            o0[...] = g
            o1[...] = d
            o2[...] = nm
            o3[...] = nv
            stores = [pltpu.make_async_copy(o, out_refs[q * npiece + k].at[0], sems.at[4 + q])
                      for q, o in enumerate((o0, o1, o2, o3))]
            for cp in stores:
                cp.start()
            for cp in stores:
                cp.wait()

    scratch = []
    for cls in classes:
        rows = cls[1] if cls[0] == D and cls[1] != D else cls[0]
        scratch += [pltpu.VMEM(cls, f32)] * 3 + [pltpu.VMEM((rows, D), f32)] + [pltpu.VMEM(cls, f32)] * 4
    scratch.append(pltpu.SemaphoreType.DMA((8,)))
    hbm = pl.BlockSpec(memory_space=pl.ANY)
    outs = pl.pallas_call(
        body, in_specs=[hbm] * (3 * npiece + 1), out_specs=[hbm] * (4 * npiece),
        out_shape=tuple(SDS(w.shape, f32) for _ in range(4) for w in ws),
        scratch_shapes=scratch, compiler_params=_cparams(None, VMEM_LIMIT_V7X), name="adamw_big")(*ws, *ms, *vs, red)
    return [list(outs[q * npiece:(q + 1) * npiece]) for q in range(4)]


def _adamw_small(w, m, v, g, name):
    def body(w_ref, m_ref, v_ref, g_ref, d_ref, nm_ref, nv_ref):
        d, nm, nv = _adamw_math(w_ref[...], g_ref[...], m_ref[...], v_ref[...])
        d_ref[...] = d
        nm_ref[...] = nm
        nv_ref[...] = nv

    return pl.pallas_call(
        body, out_shape=tuple(SDS(w.shape, f32) for _ in range(3)), name=name)(w, m, v, g)


WEIGHTS = ("ffn1_norm", "ffn1_w_gate", "ffn1_w_up", "ffn1_w_down", "mix_norm", "w_in", "q_norm", "k_norm",
           "attn_sinks", "rel_bias", "pool_w", "pool_scale", "w_out", "ffn2_norm", "ffn2_w_gate", "ffn2_w_up",
           "ffn2_w_down")
BIG = (("ffn1_w_gate", True), ("ffn1_w_up", True), ("ffn1_w_down", False), ("w_in", True), ("w_out", False),
       ("ffn2_w_gate", True), ("ffn2_w_up", True), ("ffn2_w_down", False))


def kernel(x, ffn1_norm, ffn1_w_gate, ffn1_w_up, ffn1_w_down, mix_norm, w_in, q_norm, k_norm, attn_sinks, rel_bias, pool_w, pool_scale, w_out, ffn2_norm, ffn2_w_gate, ffn2_w_up, ffn2_w_down, loss_target, m_ffn1_norm, m_ffn1_w_gate, m_ffn1_w_up, m_ffn1_w_down, m_mix_norm, m_w_in, m_q_norm, m_k_norm, m_attn_sinks, m_rel_bias, m_pool_w, m_pool_scale, m_w_out, m_ffn2_norm, m_ffn2_w_gate, m_ffn2_w_up, m_ffn2_w_down, v_ffn1_norm, v_ffn1_w_gate, v_ffn1_w_up, v_ffn1_w_down, v_mix_norm, v_w_in, v_q_norm, v_k_norm, v_attn_sinks, v_rel_bias, v_pool_w, v_pool_scale, v_w_out, v_ffn2_norm, v_ffn2_w_gate, v_ffn2_w_up, v_ffn2_w_down):
    args = dict(locals())
    w = {n: args[n] for n in WEIGHTS}
    m = {n: args["m_" + n] for n in WEIGHTS}
    v = {n: args["v_" + n] for n in WEIGHTS}

    rows = [(w[n][0].T if tr else w[n][0]).astype(bf16) for n, tr in BIG]
    shard = jnp.concatenate(rows, axis=0)
    w1, wint, wout, w2 = _all_gather_weights(shard)

    gx, (dw1, dwint, dwout, dw2), small = _local_step(
        x[0], loss_target[0], w1, wint, wout, w2, ffn1_norm, mix_norm, ffn2_norm, q_norm, k_norm, attn_sinks,
        rel_bias, pool_w[0], pool_scale)

    red = _reduce_scatter_grads(dw1, dwint, dwout, dw2)
    small_tot = _all_reduce_small(_pack_small(small))

    grads, deltas, new_m, new_v = {}, {}, {}, {}
    big = [n for n, _ in BIG]
    gs, dls, nms_, nvs_ = _adamw_big([w[n] for n in big], [m[n] for n in big], [v[n] for n in big], red)
    for k, n in enumerate(big):
        grads[n], deltas[n], new_m[n], new_v[n] = gs[k], dls[k], nms_[k], nvs_[k]
    small_names = [n for n in SMALL_NAMES if n != "loss"]
    ds, nms, nvs = _adamw_small(_pack_small({n: w[n] for n in small_names}), _pack_small({n: m[n] for n in small_names}),
                                _pack_small({n: v[n] for n in small_names}), small_tot, "adamw_small")
    for n in small_names:
        grads[n] = _unpack_small(small_tot, n)
        deltas[n], new_m[n], new_v[n] = _unpack_small(ds, n), _unpack_small(nms, n), _unpack_small(nvs, n)
    loss = small_tot[LOSS_ROW, 0]
    return (loss, gx[None], *[grads[n] for n in WEIGHTS], *[deltas[n] for n in WEIGHTS],
            *[new_m[n] for n in WEIGHTS], *[new_v[n] for n in WEIGHTS])
```

```python
import functools

import jax
import jax.numpy as jnp
import numpy as np
from jax import lax
from jax.experimental import pallas as pl
from jax.experimental.pallas import tpu as pltpu

f32, bf16, i32 = jnp.float32, jnp.bfloat16, jnp.int32
SDS = jax.ShapeDtypeStruct

D = 1024
F = 2816
HD = 64
NH = 8
NKV = 2
GQA = NH // NKV
DATTN = NH * HD
DKV = NKV * HD
DPOOL = 512
POOL_WINDOWS = (2, 4, 8, 16)
PGD = DPOOL // len(POOL_WINDOWS)
DIN = DATTN + 2 * DKV + DPOOL
DMIX = DATTN + DPOOL
BLK = 128
NBUCK = 32
MAX_DISTANCE = 128
EPS = 1e-6
NEG = -1e30
SCALE = HD ** -0.5

ADAM_LR, ADAM_B1, ADAM_B2, ADAM_EPS, ADAM_WD, ADAM_STEP = 0.001, 0.9, 0.999, 1e-08, 0.01, 10

NDEV = 8
FS = F // NDEV
INS = DIN // NDEV
OUTS = DMIX // NDEV
PIECE_ROWS = (FS, FS, FS, INS, OUTS, FS, FS, FS)
PIECE_OFF = tuple(int(v) for v in np.cumsum((0,) + PIECE_ROWS[:-1]))
PACK_ROWS = sum(PIECE_ROWS)

VMEM_LIMIT_V7X = 56 * 1024 * 1024

MESH = pl.DeviceIdType.MESH


def _cparams(sem=None, vmem=None):
    return pltpu.CompilerParams(dimension_semantics=sem, vmem_limit_bytes=vmem)


def _nt(a, b):
    return lax.dot_general(a, b, (((1,), (1,)), ((), ())), preferred_element_type=f32)


def _tn(a, b):
    return lax.dot_general(a, b, (((0,), (0,)), ((), ())), preferred_element_type=f32)


def _nn(a, b):
    return jnp.dot(a, b, preferred_element_type=f32)


def _sigmoid(x):
    return 1.0 / (1.0 + jnp.exp(-x))


def _norm_fwd(x, g, name):
    T = x.shape[0]
    tm = min(512, T)

    def body(x_ref, g_ref, h_ref):
        xv = x_ref[...]
        r = lax.rsqrt(jnp.mean(xv * xv, axis=-1, keepdims=True) + EPS)
        h_ref[...] = (xv * r * g_ref[...]).astype(bf16)

    return pl.pallas_call(
        body, grid=(T // tm,),
        in_specs=[pl.BlockSpec((tm, D), lambda i: (i, 0)), pl.BlockSpec((1, D), lambda i: (0, 0))],
        out_specs=pl.BlockSpec((tm, D), lambda i: (i, 0)),
        out_shape=SDS((T, D), bf16), name=name)(x, g)


def _norm_bwd(dh, x, g, dres, out_scale, name):
    T = x.shape[0]
    tm = min(512, T)

    def body(dh_ref, x_ref, g_ref, dr_ref, dx_ref, dxb_ref, dg_ref):
        i = pl.program_id(0)
        xv = x_ref[...]
        r = lax.rsqrt(jnp.mean(xv * xv, axis=-1, keepdims=True) + EPS)
        xh = xv * r
        dhv = dh_ref[...]
        dxh = dhv * g_ref[...]
        dx = dr_ref[...] + r * (dxh - xh * jnp.mean(dxh * xh, axis=-1, keepdims=True))
        dx_ref[...] = dx
        dxb_ref[...] = (out_scale * dx).astype(bf16)
        dg = jnp.sum(dhv * xh, axis=0, keepdims=True)

        @pl.when(i == 0)
        def _():
            dg_ref[...] = dg

        @pl.when(i > 0)
        def _():
            dg_ref[...] += dg

    tok = pl.BlockSpec((tm, D), lambda i: (i, 0))
    vec = pl.BlockSpec((1, D), lambda i: (0, 0))
    return pl.pallas_call(
        body, grid=(T // tm,),
        in_specs=[tok, tok, vec, tok], out_specs=[tok, tok, vec],
        out_shape=(SDS((T, D), f32), SDS((T, D), bf16), SDS((1, D), f32)),
        compiler_params=_cparams(("arbitrary",)), name=name)(dh, x, g, dres)


def _ffn_tiles(T):
    return min(1024, T), 256


def _ffn_fwd(h, w, x, name):
    T = h.shape[0]
    tm, tf = _ffn_tiles(T)
    nf = F // tf

    def body(h_ref, w_ref, x_ref, xo_ref, g_ref, u_ref, acc):
        fi = pl.program_id(1)
        hv = h_ref[...]
        gate = _nt(hv, w_ref[0])
        up = _nt(hv, w_ref[1])
        act = gate * _sigmoid(gate) * up
        g_ref[...] = gate.astype(bf16)
        u_ref[...] = up.astype(bf16)
        part = _nn(act.astype(bf16), w_ref[2])

        @pl.when(fi == 0)
        def _():
            acc[...] = part

        @pl.when(fi > 0)
        def _():
            acc[...] += part

        @pl.when(fi == nf - 1)
        def _():
            xo_ref[...] = x_ref[...] + 0.5 * acc[...]

    tok = pl.BlockSpec((tm, D), lambda i, f: (i, 0))
    act_spec = pl.BlockSpec((tm, tf), lambda i, f: (i, f))
    return pl.pallas_call(
        body, grid=(T // tm, nf),
        in_specs=[tok, pl.BlockSpec((3, tf, D), lambda i, f: (0, f, 0)), tok],
        out_specs=[tok, act_spec, act_spec],
        out_shape=(SDS((T, D), f32), SDS((T, F), bf16), SDS((T, F), bf16)),
        scratch_shapes=[pltpu.VMEM((tm, D), f32)],
        compiler_params=_cparams(("arbitrary", "arbitrary"), VMEM_LIMIT_V7X), name=name)(h, w, x)


def _ffn_bwd(dob, h, gate, up, w, name):
    T = h.shape[0]
    tm, tf = _ffn_tiles(T)
    nf, nt = F // tf, T // tm

    def body(do_ref, h_ref, g_ref, u_ref, w_ref, dh_ref, dw_ref, dwacc):
        fi, ti = pl.program_id(0), pl.program_id(1)
        dov = do_ref[...]
        gv = g_ref[...].astype(f32)
        uv = u_ref[...].astype(f32)
        sg = _sigmoid(gv)
        sil = gv * sg
        dact = _nt(dov, w_ref[2])
        dup = dact * sil
        dgate = dact * uv * (sg * (1.0 + gv * (1.0 - sg)))
        dgu = jnp.concatenate([dgate.astype(bf16), dup.astype(bf16)], axis=1)
        actb = (sil * uv).astype(bf16)
        wgu = w_ref[0:2].reshape(2 * tf, D)
        dh_part = _nn(dgu, wgu)
        rows = pl.ds(pl.multiple_of(ti * tm, tm), tm)

        @pl.when(fi == 0)
        def _():
            dh_ref[rows, :] = dh_part

        @pl.when(fi > 0)
        def _():
            dh_ref[rows, :] += dh_part

        dwgu = _tn(dgu, h_ref[...])
        dwd = _tn(actb, dov)

        @pl.when(ti == 0)
        def _():
            dwacc[0:2 * tf, :] = dwgu
            dwacc[2 * tf:3 * tf, :] = dwd

        @pl.when(ti > 0)
        def _():
            dwacc[0:2 * tf, :] += dwgu
            dwacc[2 * tf:3 * tf, :] += dwd

        @pl.when(ti == nt - 1)
        def _():
            dw_ref[...] = dwacc[...].reshape(3, tf, D).astype(bf16)

    tok = pl.BlockSpec((tm, D), lambda f, i: (i, 0))
    act_spec = pl.BlockSpec((tm, tf), lambda f, i: (i, f))
    wspec = pl.BlockSpec((3, tf, D), lambda f, i: (0, f, 0))
    return pl.pallas_call(
        body, grid=(nf, nt),
        in_specs=[tok, tok, act_spec, act_spec, wspec],
        out_specs=[pl.BlockSpec((T, D), lambda f, i: (0, 0)), wspec],
        out_shape=(SDS((T, D), f32), SDS((3, F, D), bf16)),
        scratch_shapes=[pltpu.VMEM((3 * tf, D), f32)],
        compiler_params=_cparams(("arbitrary", "arbitrary"), VMEM_LIMIT_V7X), name=name)(dob, h, gate, up, w)


def _loss_grad(y, target, name):
    T = y.shape[0]
    tm = min(512, T)

    def body(y_ref, t_ref, dy_ref, dyb_ref, l_ref):
        i = pl.program_id(0)
        e = y_ref[...] - t_ref[...]
        dy = e * (1.0 / D)
        dy_ref[...] = dy
        dyb_ref[...] = (0.5 * dy).astype(bf16)
        col = jnp.sum(e * e, axis=0, keepdims=True) * (0.5 / D)
        lanes = col[:, 0:128]
        for k in range(1, D // 128):
            lanes = lanes + col[:, 128 * k:128 * (k + 1)]

        @pl.when(i == 0)
        def _():
            l_ref[...] = lanes

        @pl.when(i > 0)
        def _():
            l_ref[...] += lanes

    tok = pl.BlockSpec((tm, D), lambda i: (i, 0))
    return pl.pallas_call(
        body, grid=(T // tm,), in_specs=[tok, tok],
        out_specs=[tok, tok, pl.BlockSpec((1, 128), lambda i: (0, 0))],
        out_shape=(SDS((T, D), f32), SDS((T, D), bf16), SDS((1, 128), f32)),
        compiler_params=_cparams(("arbitrary",)), name=name)(y, target)


def _in_proj_fwd(h, wint, name):
    T = h.shape[0]
    tm = min(512, T)

    def body(h_ref, w_ref, z_ref):
        z_ref[...] = _nt(h_ref[...], w_ref[...])

    return pl.pallas_call(
        body, grid=(T // tm,),
        in_specs=[pl.BlockSpec((tm, D), lambda i: (i, 0)), pl.BlockSpec((DIN, D), lambda i: (0, 0))],
        out_specs=pl.BlockSpec((tm, DIN), lambda i: (i, 0)),
        out_shape=SDS((T, DIN), f32), name=name)(h, wint)


def _in_proj_bwd(dz, wint, h, name):
    T = h.shape[0]
    tm = min(512, T)
    nt = T // tm

    def body(dz_ref, w_ref, h_ref, dh_ref, dw_ref, acc):
        i = pl.program_id(0)
        dzb = dz_ref[...].astype(bf16)
        dh_ref[...] = _nn(dzb, w_ref[...])
        part = _tn(dzb, h_ref[...])

        @pl.when(i == 0)
        def _():
            acc[...] = part

        @pl.when(i > 0)
        def _():
            acc[...] += part

        @pl.when(i == nt - 1)
        def _():
            dw_ref[...] = acc[...].astype(bf16)

    wspec = pl.BlockSpec((DIN, D), lambda i: (0, 0))
    return pl.pallas_call(
        body, grid=(nt,),
        in_specs=[pl.BlockSpec((tm, DIN), lambda i: (i, 0)), wspec, pl.BlockSpec((tm, D), lambda i: (i, 0))],
        out_specs=[pl.BlockSpec((tm, D), lambda i: (i, 0)), wspec],
        out_shape=(SDS((T, D), f32), SDS((DIN, D), bf16)),
        scratch_shapes=[pltpu.VMEM((DIN, D), f32)],
        compiler_params=_cparams(("arbitrary",)), name=name)(dz, wint, h)


def _out_proj_fwd(ymix, wout, x, name):
    T = x.shape[0]
    tm = min(512, T)

    def body(y_ref, w_ref, x_ref, o_ref):
        o_ref[...] = x_ref[...] + _nn(y_ref[...], w_ref[...])

    tok = pl.BlockSpec((tm, D), lambda i: (i, 0))
    return pl.pallas_call(
        body, grid=(T // tm,),
        in_specs=[pl.BlockSpec((tm, DMIX), lambda i: (i, 0)), pl.BlockSpec((DMIX, D), lambda i: (0, 0)), tok],
        out_specs=tok, out_shape=SDS((T, D), f32), name=name)(ymix, wout, x)


def _out_proj_bwd(dxb, wout, ymix, name):
    T = dxb.shape[0]
    tm = min(512, T)
    nt = T // tm

    def body(dx_ref, w_ref, y_ref, dy_ref, dw_ref, acc):
        i = pl.program_id(0)
        dxv = dx_ref[...]
        dy_ref[...] = _nt(dxv, w_ref[...])
        part = _tn(y_ref[...], dxv)

        @pl.when(i == 0)
        def _():
            acc[...] = part

        @pl.when(i > 0)
        def _():
            acc[...] += part

        @pl.when(i == nt - 1)
        def _():
            dw_ref[...] = acc[...].astype(bf16)

    wspec = pl.BlockSpec((DMIX, D), lambda i: (0, 0))
    return pl.pallas_call(
        body, grid=(nt,),
        in_specs=[pl.BlockSpec((tm, D), lambda i: (i, 0)), wspec, pl.BlockSpec((tm, DMIX), lambda i: (i, 0))],
        out_specs=[pl.BlockSpec((tm, DMIX), lambda i: (i, 0)), wspec],
        out_shape=(SDS((T, DMIX), f32), SDS((DMIX, D), bf16)),
        scratch_shapes=[pltpu.VMEM((DMIX, D), f32)],
        compiler_params=_cparams(("arbitrary",)), name=name)(dxb, wout, ymix)


def _t5_bucket_table():
    ql = np.arange(BLK)[:, None]
    kl = np.arange(2 * BLK)[None, :]
    n = np.maximum(ql + BLK - kl, 0)
    max_exact = NBUCK // 2
    large = max_exact + (np.log(np.maximum(n, 1) / max_exact) / np.log(MAX_DISTANCE / max_exact)
                         * (NBUCK - max_exact)).astype(np.int32)
    large = np.minimum(large, NBUCK - 1)
    return np.where(n < max_exact, n, large).astype(np.int32)


def _fill_bias(bk_ref, rb_ref, bias_scr):
    bk = bk_ref[...]
    for h in range(NH):
        def step(b, acc, h=h):
            return acc + jnp.where(bk == b, rb_ref[b, h], 0.0)
        bias_scr[h] = lax.fori_loop(0, NBUCK, step, jnp.zeros((BLK, 2 * BLK), f32))


def _attn_probs(zc_ref, zp_ref, kh, qg, kg, sk_ref, bias_scr, n):
    kc = DATTN + HD * kh
    vc = DATTN + DKV + HD * kh
    kx = jnp.concatenate([zp_ref[:, kc:kc + HD], zc_ref[:, kc:kc + HD]], axis=0)
    vx = jnp.concatenate([zp_ref[:, vc:vc + HD], zc_ref[:, vc:vc + HD]], axis=0)
    qx = jnp.concatenate([zc_ref[:, HD * (GQA * kh + g):HD * (GQA * kh + g + 1)] for g in range(GQA)], axis=0)
    rq = lax.rsqrt(jnp.mean(qx * qx, axis=-1, keepdims=True) + EPS)
    rk = lax.rsqrt(jnp.mean(kx * kx, axis=-1, keepdims=True) + EPS)
    qhat, khat = qx * rq, kx * rk
    qnb, knb = (qhat * qg).astype(bf16), (khat * kg).astype(bf16)
    s = _nt(qnb, knb) * SCALE + bias_scr[GQA * kh:GQA * (kh + 1)].reshape(GQA * BLK, 2 * BLK)
    row = lax.broadcasted_iota(i32, (GQA * BLK, 2 * BLK), 0) & (BLK - 1)
    col = lax.broadcasted_iota(i32, (GQA * BLK, 2 * BLK), 1)
    mask = (col > row) & (col <= row + BLK) & ((col >= BLK) | (n > 0))
    s = jnp.where(mask, s, NEG)
    ridx = lax.broadcasted_iota(i32, (GQA * BLK, 1), 0)
    sink = jnp.full((GQA * BLK, 1), sk_ref[GQA * kh + GQA - 1], f32)
    for g in range(GQA - 2, -1, -1):
        sink = jnp.where(ridx < (g + 1) * BLK, sk_ref[GQA * kh + g], sink)
    m = jnp.maximum(jnp.max(s, axis=-1, keepdims=True), sink)
    e = jnp.exp(s - m)
    es = jnp.exp(sink - m)
    den = jnp.sum(e, axis=-1, keepdims=True) + es
    return dict(p=e / den, psink=es / den, qhat=qhat, khat=khat, rq=rq, rk=rk, qnb=qnb, knb=knb, vb=vx.astype(bf16))


def _pool_group(zc_ref, zp_ref, g, w, n):
    c0 = DATTN + 2 * DKV + PGD * g
    uc = zc_ref[:, c0:c0 + PGD]
    up = jnp.where(n > 0, zp_ref[:, c0:c0 + PGD], 0.0)
    ue = jnp.concatenate([up, uc], axis=0)
    hi = ue.astype(bf16)
    lo = (ue - hi.astype(f32)).astype(bf16)
    t = lax.broadcasted_iota(i32, (BLK, 2 * BLK), 0)
    s = lax.broadcasted_iota(i32, (BLK, 2 * BLK), 1)
    band = jnp.where((s <= t + BLK) & (s > t + BLK - w), 1.0, 0.0).astype(bf16)
    sm = _nn(band, hi) + _nn(band, lo)
    pos = n * BLK + lax.broadcasted_iota(i32, (BLK, 1), 0) + 1
    cnt = jnp.minimum(pos, w).astype(f32)
    return sm / cnt - uc, band, cnt


def _mix_fwd(z, qg, kg, sinks, relb, bucket, pool_w, pscale, name):
    T = z.shape[0]
    nb = T // BLK

    def body(zc_ref, zp_ref, qg_ref, kg_ref, sk_ref, rb_ref, bk_ref, pw_ref, ps_ref, y_ref, bias_scr, yacc):
        n = pl.program_id(0)

        @pl.when(n == 0)
        def _():
            _fill_bias(bk_ref, rb_ref, bias_scr)

        for kh in range(NKV):
            a = _attn_probs(zc_ref, zp_ref, kh, qg_ref[...], kg_ref[...], sk_ref, bias_scr, n)
            o = _nn(a["p"].astype(bf16), a["vb"])
            for g in range(GQA):
                hc = HD * (GQA * kh + g)
                yacc[:, hc:hc + HD] = o[g * BLK:(g + 1) * BLK]
        for g, w in enumerate(POOL_WINDOWS):
            pooled, _, _ = _pool_group(zc_ref, zp_ref, g, w, n)
            yp = _nn(pooled.astype(bf16), pw_ref[g].astype(bf16)) * ps_ref[:, PGD * g:PGD * (g + 1)]
            yacc[:, DATTN + PGD * g:DATTN + PGD * (g + 1)] = yp
        y_ref[...] = yacc[...].astype(bf16)

    full = lambda *shape: pl.BlockSpec(shape, lambda n: (0,) * len(shape))
    smem = pl.BlockSpec(memory_space=pltpu.SMEM)
    return pl.pallas_call(
        body, grid=(nb,),
        in_specs=[pl.BlockSpec((BLK, DIN), lambda n: (n, 0)),
                  pl.BlockSpec((BLK, DIN), lambda n: (jnp.maximum(n - 1, 0), 0)),
                  full(1, HD), full(1, HD), smem, smem, full(BLK, 2 * BLK),
                  full(len(POOL_WINDOWS), PGD, PGD), full(1, DPOOL)],
        out_specs=pl.BlockSpec((BLK, DMIX), lambda n: (n, 0)),
        out_shape=SDS((T, DMIX), bf16),
        scratch_shapes=[pltpu.VMEM((NH, BLK, 2 * BLK), f32), pltpu.VMEM((BLK, DMIX), f32)],
        compiler_params=_cparams(("arbitrary",)), name=name)(z, z, qg, kg, sinks, relb, bucket, pool_w, pscale)


def _mix_bwd(z, dy, qg, kg, sinks, relb, bucket, pool_w, pscale, name):
    T = z.shape[0]
    nb = T // BLK

    def body(zc_ref, zp_ref, dy_ref, qg_ref, kg_ref, sk_ref, rb_ref, bk_ref, pw_ref, ps_ref,
             dz_ref, dqg_ref, dkg_ref, dsk_ref, drb_ref, dpw_ref, dps_ref, bias_scr, dbias_scr):
        n = pl.program_id(0)
        rows = pl.ds(pl.multiple_of(n * BLK, BLK), BLK)
        prow = pl.ds(pl.multiple_of(jnp.maximum(n - 1, 0) * BLK, BLK), BLK)

        @pl.when(n == 0)
        def _():
            _fill_bias(bk_ref, rb_ref, bias_scr)
            dbias_scr[...] = jnp.zeros_like(dbias_scr)
            dqg_ref[...] = jnp.zeros_like(dqg_ref)
            dkg_ref[...] = jnp.zeros_like(dkg_ref)
            dsk_ref[...] = jnp.zeros_like(dsk_ref)
            dpw_ref[...] = jnp.zeros_like(dpw_ref)
            dps_ref[...] = jnp.zeros_like(dps_ref)

        qg, kg = qg_ref[...], kg_ref[...]
        lane = lax.broadcasted_iota(i32, (1, 128), 1)
        dsk = jnp.zeros((1, 128), f32)
        for kh in range(NKV):
            a = _attn_probs(zc_ref, zp_ref, kh, qg, kg, sk_ref, bias_scr, n)
            p = a["p"]
            do = jnp.concatenate([dy_ref[:, HD * (GQA * kh + g):HD * (GQA * kh + g + 1)] for g in range(GQA)],
                                 axis=0).astype(bf16)
            dv = _tn(p.astype(bf16), do)
            dp = _nt(do, a["vb"])
            delta = jnp.sum(p * dp, axis=-1, keepdims=True)
            ds = p * (dp - delta)
            sinkterm = a["psink"] * delta
            for g in range(GQA):
                h = GQA * kh + g
                dbias_scr[h] += ds[g * BLK:(g + 1) * BLK]
                tot = jnp.sum(sinkterm[g * BLK:(g + 1) * BLK], axis=0, keepdims=True)
                dsk = dsk - jnp.where(lane == h, tot, 0.0)
            dsb = ds.astype(bf16)
            dqn = _nn(dsb, a["knb"]) * SCALE
            dkn = _tn(dsb, a["qnb"]) * SCALE
            qhat, khat = a["qhat"], a["khat"]
            dqg_ref[...] += jnp.sum(dqn * qhat, axis=0, keepdims=True)
            dkg_ref[...] += jnp.sum(dkn * khat, axis=0, keepdims=True)
            dqh = dqn * qg
            dq = a["rq"] * (dqh - qhat * jnp.mean(dqh * qhat, axis=-1, keepdims=True))
            dkh = dkn * kg
            dk = a["rk"] * (dkh - khat * jnp.mean(dkh * khat, axis=-1, keepdims=True))
            kc = DATTN + HD * kh
            vc = DATTN + DKV + HD * kh
            for g in range(GQA):
                hc = HD * (GQA * kh + g)
                dz_ref[rows, hc:hc + HD] = dq[g * BLK:(g + 1) * BLK]
            dz_ref[rows, kc:kc + HD] = dk[BLK:2 * BLK]
            dz_ref[rows, vc:vc + HD] = dv[BLK:2 * BLK]

            @pl.when(n > 0)
            def _(dk=dk, dv=dv, kc=kc, vc=vc):
                dz_ref[prow, kc:kc + HD] += dk[0:BLK]
                dz_ref[prow, vc:vc + HD] += dv[0:BLK]

        dsk_ref[...] += dsk

        for g, w in enumerate(POOL_WINDOWS):
            c0 = DATTN + 2 * DKV + PGD * g
            pooled, band, cnt = _pool_group(zc_ref, zp_ref, g, w, n)
            pb = pooled.astype(bf16)
            wb = pw_ref[g].astype(bf16)
            dyp = dy_ref[:, DATTN + PGD * g:DATTN + PGD * (g + 1)]
            ypre = _nn(pb, wb)
            dps_ref[:, PGD * g:PGD * (g + 1)] += jnp.sum(dyp * ypre, axis=0, keepdims=True)
            dyg = (dyp * ps_ref[:, PGD * g:PGD * (g + 1)]).astype(bf16)
            dpw_ref[g] += _tn(pb, dyg)
            dpooled = _nt(dyg, wb)
            dsm = dpooled / cnt
            hi = dsm.astype(bf16)
            lo = (dsm - hi.astype(f32)).astype(bf16)
            due = _tn(band, hi) + _tn(band, lo)
            dz_ref[rows, c0:c0 + PGD] = due[BLK:2 * BLK] - dpooled

            @pl.when(n > 0)
            def _(due=due, c0=c0):
                dz_ref[prow, c0:c0 + PGD] += due[0:BLK]

        @pl.when(n == nb - 1)
        def _():
            bk = bk_ref[...]
            ri = lax.broadcasted_iota(i32, (NBUCK, NH), 0)
            ci = lax.broadcasted_iota(i32, (NBUCK, NH), 1)

            def step(b, acc):
                for h in range(NH):
                    sel = jnp.where(bk == b, dbias_scr[h], 0.0)
                    tot = jnp.sum(jnp.sum(sel, axis=1, keepdims=True), axis=0, keepdims=True)
                    acc = acc + jnp.where((ri == b) & (ci == h), tot, 0.0)
                return acc

            drb_ref[...] = lax.fori_loop(0, NBUCK, step, jnp.zeros((NBUCK, NH), f32))

    full = lambda *shape: pl.BlockSpec(shape, lambda n: (0,) * len(shape))
    smem = pl.BlockSpec(memory_space=pltpu.SMEM)
    npg = len(POOL_WINDOWS)
    return pl.pallas_call(
        body, grid=(nb,),
        in_specs=[pl.BlockSpec((BLK, DIN), lambda n: (n, 0)),
                  pl.BlockSpec((BLK, DIN), lambda n: (jnp.maximum(n - 1, 0), 0)),
                  pl.BlockSpec((BLK, DMIX), lambda n: (n, 0)),
                  full(1, HD), full(1, HD), smem, smem, full(BLK, 2 * BLK), full(npg, PGD, PGD), full(1, DPOOL)],
        out_specs=[full(T, DIN), full(1, HD), full(1, HD), full(1, 128), full(NBUCK, NH),
                   full(npg, PGD, PGD), full(1, DPOOL)],
        out_shape=(SDS((T, DIN), f32), SDS((1, HD), f32), SDS((1, HD), f32), SDS((1, 128), f32),
                   SDS((NBUCK, NH), f32), SDS((npg, PGD, PGD), f32), SDS((1, DPOOL), f32)),
        scratch_shapes=[pltpu.VMEM((NH, BLK, 2 * BLK), f32), pltpu.VMEM((NH, BLK, 2 * BLK), f32)],
        compiler_params=_cparams(("arbitrary",), VMEM_LIMIT_V7X),
        name=name)(z, z, dy, qg, kg, sinks, relb, bucket, pool_w, pscale)


def _local_step(x, target, w1, wint, wout, w2, g1, gm, g3, qg, kg, sinks, relb, pool_w, pscale):
    bucket = jnp.asarray(_t5_bucket_table())
    sk = sinks.reshape(NH)
    h1 = _norm_fwd(x, g1, "norm1_fwd")
    x1, gate1, up1 = _ffn_fwd(h1, w1, x, "ffn1_fwd")
    h2 = _norm_fwd(x1, gm, "norm2_fwd")
    z = _in_proj_fwd(h2, wint, "in_proj_fwd")
    ymix = _mix_fwd(z, qg, kg, sk, relb, bucket, pool_w, pscale, "mix_fwd")
    x2 = _out_proj_fwd(ymix, wout, x1, "out_proj_fwd")
    h3 = _norm_fwd(x2, g3, "norm3_fwd")
    y, gate2, up2 = _ffn_fwd(h3, w2, x2, "ffn2_fwd")
    dy, dyb, loss_lanes = _loss_grad(y, target, "loss_grad")

    dh3, dw2 = _ffn_bwd(dyb, h3, gate2, up2, w2, "ffn2_bwd")
    dx2, dx2b, dg3 = _norm_bwd(dh3, x2, g3, dy, 1.0, "norm3_bwd")
    dymix, dwout = _out_proj_bwd(dx2b, wout, ymix, "out_proj_bwd")
    dz, dqg, dkg, dsk, drb, dpw, dps = _mix_bwd(z, dymix, qg, kg, sk, relb, bucket, pool_w, pscale, "mix_bwd")
    dh2, dwint = _in_proj_bwd(dz, wint, h2, "in_proj_bwd")
    dx1, dx1b, dgm = _norm_bwd(dh2, x1, gm, dx2, 0.5, "norm2_bwd")
    dh1, dw1 = _ffn_bwd(dx1b, h1, gate1, up1, w1, "ffn1_bwd")
    gx, _, dg1 = _norm_bwd(dh1, x, g1, dx1, 1.0, "norm1_bwd")
    small = dict(ffn1_norm=dg1, mix_norm=dgm, ffn2_norm=dg3, pool_scale=dps, q_norm=dqg, k_norm=dkg,
                 attn_sinks=dsk[:, :NH], rel_bias=drb, pool_w=dpw, loss=loss_lanes)
    return gx, (dw1, dwint, dwout, dw2), small


SMALL_NAMES = ("ffn1_norm", "mix_norm", "ffn2_norm", "pool_scale", "q_norm", "k_norm", "attn_sinks", "rel_bias",
               "pool_w", "loss")
SMALL_SHAPES = dict(ffn1_norm=(1, D), mix_norm=(1, D), ffn2_norm=(1, D), pool_scale=(1, DPOOL), q_norm=(1, HD),
                    k_norm=(1, HD), attn_sinks=(1, NH), rel_bias=(NBUCK, NH),
                    pool_w=(1, len(POOL_WINDOWS), PGD, PGD), loss=(1, 128))


def _small_rows(name):
    return -(-int(np.prod(SMALL_SHAPES[name])) // 128)


SMALL_OFF = {}
_r = 0
for _n in SMALL_NAMES:
    SMALL_OFF[_n] = _r
    _r += _small_rows(_n)
SMALL_ROWS = -(-_r // 8) * 8
LOSS_ROW = SMALL_OFF["loss"]


def _pack_small(vals):
    parts = []
    for n in SMALL_NAMES:
        size = _small_rows(n) * 128
        if n in vals:
            flat = vals[n].astype(f32).reshape(-1)
            parts.append(jnp.pad(flat, (0, size - flat.shape[0])))
        else:
            parts.append(jnp.zeros((size,), f32))
    flat = jnp.concatenate(parts)
    flat = jnp.pad(flat, (0, SMALL_ROWS * 128 - flat.shape[0]))
    return flat.reshape(SMALL_ROWS, 128)


def _unpack_small(packed, name):
    size = int(np.prod(SMALL_SHAPES[name]))
    r0 = SMALL_OFF[name]
    return packed[r0:r0 + _small_rows(name)].reshape(-1)[:size].reshape(SMALL_SHAPES[name])


def _position():
    return lax.axis_index("x"), lax.axis_index("y"), lax.axis_index("c")


def _dev_index(x, y, c):
    return 4 * x + 2 * y + c


def _all_gather_weights(shard):
    def body(s_ref, w1_ref, wi_ref, wo_ref, w2_ref, send_sems, recv_sems, local_sem):
        x, y, c = _position()
        me, sib = (x, y, c), (x, y, 1 - c)
        chips = [(1 - x, y), (x, 1 - y), (1 - x, 1 - y)]
        outs = (w1_ref.at[0], w1_ref.at[1], w1_ref.at[2], wi_ref, wo_ref, w2_ref.at[0], w2_ref.at[1], w2_ref.at[2])

        def piece_dst(k, dev):
            r = PIECE_ROWS[k]
            return outs[k].at[pl.ds(pl.multiple_of(_dev_index(*dev) * r, 16), r), :]

        def piece_src(k):
            return s_ref.at[pl.ds(PIECE_OFF[k], PIECE_ROWS[k]), :]

        def copies(rel, block, to, from_shard):
            return [pltpu.make_async_remote_copy(
                src_ref=piece_src(k) if from_shard else piece_dst(k, block), dst_ref=piece_dst(k, block),
                send_sem=send_sems.at[rel], recv_sem=recv_sems.at[rel], device_id=to, device_id_type=MESH)
                for k in range(len(PIECE_ROWS))]

        def whole(rel):
            return pltpu.make_async_remote_copy(src_ref=s_ref, dst_ref=s_ref, send_sem=send_sems.at[rel],
                                                recv_sem=recv_sems.at[rel], device_id=me, device_id_type=MESH)

        mine = [pltpu.make_async_copy(piece_src(k), piece_dst(k, me), local_sem) for k in range(len(PIECE_ROWS))]
        for cp in mine:
            cp.start()
        for cp in copies(0, me, sib, True):
            cp.start()
        for j, chip in enumerate(chips):
            for cp in copies(1 + j, me, (*chip, c), True):
                cp.start()
        for j, chip in enumerate(chips):
            whole(1 + j).wait_recv()
            for cp in copies(4 + j, (*chip, c), sib, False):
                cp.start()
        whole(0).wait_recv()
        for j in range(3):
            whole(4 + j).wait_recv()
        for rel in range(7):
            whole(rel).wait_send()
        pltpu.make_async_copy(s_ref, s_ref, local_sem).wait()

    hbm = pl.BlockSpec(memory_space=pl.ANY)
    return pl.pallas_call(
        body, in_specs=[hbm], out_specs=[hbm] * 4,
        out_shape=(SDS((3, F, D), bf16), SDS((DIN, D), bf16), SDS((DMIX, D), bf16), SDS((3, F, D), bf16)),
        scratch_shapes=[pltpu.SemaphoreType.DMA((7,)), pltpu.SemaphoreType.DMA((7,)), pltpu.SemaphoreType.DMA],
        compiler_params=pltpu.CompilerParams(has_side_effects=True),
        name="all_gather_weights")(shard)


RS_CHUNK = 240


def _reduce_scatter_grads(dw1, dwint, dwout, dw2):
    nchunk = PACK_ROWS // RS_CHUNK

    def body(d1_ref, di_ref, do_ref, d2_ref, red_ref, rx1_ref, rx2_ref,
             own_buf, rx_buf, tx_buf, acc, sa, ra, sb, rb, lsem):
        x, y, c = _position()
        me, sib = (x, y, c), (x, y, 1 - c)
        rel_chips = [(x, y), (1 - x, y), (x, 1 - y), (1 - x, 1 - y)]
        srcs = (d1_ref.at[0], d1_ref.at[1], d1_ref.at[2], di_ref, do_ref, d2_ref.at[0], d2_ref.at[1], d2_ref.at[2])

        def piece(k, dev):
            r = PIECE_ROWS[k]
            return srcs[k].at[pl.ds(pl.multiple_of(_dev_index(*dev) * r, 16), r), :]

        def packed(ref, k):
            return ref.at[pl.ds(PIECE_OFF[k], PIECE_ROWS[k]), :]

        for j, chip in enumerate(rel_chips):
            for k in range(len(PIECE_ROWS)):
                pltpu.make_async_remote_copy(
                    src_ref=piece(k, (*chip, 1 - c)), dst_ref=packed(rx1_ref.at[j], k),
                    send_sem=sa.at[j], recv_sem=ra.at[j], device_id=sib, device_id_type=MESH).start()

        def wait_a(j):
            return pltpu.make_async_remote_copy(src_ref=rx1_ref.at[j], dst_ref=rx1_ref.at[j], send_sem=sa.at[j],
                                                recv_sem=ra.at[j], device_id=me, device_id_type=MESH)

        def ici(j):
            return pltpu.make_async_remote_copy(
                src_ref=tx_buf.at[j - 1], dst_ref=rx2_ref.at[j - 1], send_sem=sb.at[j - 1], recv_sem=rb.at[j - 1],
                device_id=(*rel_chips[j], c), device_id_type=MESH)

        for j in (1, 2, 3, 0):
            loads = [pltpu.make_async_copy(piece(k, (*rel_chips[j], c)), packed(own_buf, k), lsem)
                     for k in range(len(PIECE_ROWS))]
            for cp in loads:
                cp.start()
            wait_a(j).wait_recv()
            got = pltpu.make_async_copy(rx1_ref.at[j], rx_buf, lsem)
            got.start()
            pltpu.make_async_copy(rx_buf, rx_buf, lsem).wait()
            got.wait()

            def add(i, carry, j=j):
                rows = pl.ds(pl.multiple_of(i * RS_CHUNK, 16), RS_CHUNK)
                tot = own_buf[rows, :].astype(f32) + rx_buf[rows, :].astype(f32)
                if j == 0:
                    acc[rows, :] = tot
                else:
                    tx_buf[j - 1, rows, :] = tot.astype(bf16)
                return carry

            lax.fori_loop(0, nchunk, add, 0)
            if j != 0:
                ici(j).start()

        for j in (1, 2, 3):
            ici(j).wait_recv()
            got = pltpu.make_async_copy(rx2_ref.at[j - 1], rx_buf, lsem)
            got.start()
            got.wait()

            def add2(i, carry):
                rows = pl.ds(pl.multiple_of(i * RS_CHUNK, 16), RS_CHUNK)
                acc[rows, :] += rx_buf[rows, :].astype(f32)
                return carry

            lax.fori_loop(0, nchunk, add2, 0)
        out = pltpu.make_async_copy(acc, red_ref, lsem)
        out.start()
        out.wait()
        for j in range(4):
            wait_a(j).wait_send()
        for j in (1, 2, 3):
            ici(j).wait_send()

    hbm = pl.BlockSpec(memory_space=pl.ANY)
    red, _, _ = pl.pallas_call(
        body, in_specs=[hbm] * 4, out_specs=[hbm] * 3,
        out_shape=(SDS((PACK_ROWS, D), f32), SDS((4, PACK_ROWS, D), bf16), SDS((3, PACK_ROWS, D), bf16)),
        scratch_shapes=[pltpu.VMEM((PACK_ROWS, D), bf16), pltpu.VMEM((PACK_ROWS, D), bf16),
                        pltpu.VMEM((3, PACK_ROWS, D), bf16), pltpu.VMEM((PACK_ROWS, D), f32),
                        pltpu.SemaphoreType.DMA((4,)), pltpu.SemaphoreType.DMA((4,)),
                        pltpu.SemaphoreType.DMA((3,)), pltpu.SemaphoreType.DMA((3,)), pltpu.SemaphoreType.DMA],
        compiler_params=pltpu.CompilerParams(has_side_effects=True, vmem_limit_bytes=VMEM_LIMIT_V7X),
        name="reduce_scatter_grads")(dw1, dwint, dwout, dw2)
    return red


def _all_reduce_small(packed):
    def body(p_ref, o_ref, pair, chips, send_sems, recv_sems):
        x, y, c = _position()
        chip = 2 * x + y
        pair[c] = p_ref[...]
        swap = pltpu.make_async_remote_copy(
            src_ref=p_ref, dst_ref=pair.at[c], send_sem=send_sems.at[0], recv_sem=recv_sems.at[0],
            device_id=(x, y, 1 - c), device_id_type=MESH)
        swap.start()
        swap.wait_recv()
        chips[chip] = pair[0] + pair[1]
        cps = [pltpu.make_async_remote_copy(
            src_ref=chips.at[chip], dst_ref=chips.at[chip], send_sem=send_sems.at[1 + j], recv_sem=recv_sems.at[1 + j],
            device_id=(*other, c), device_id_type=MESH)
            for j, other in enumerate([(1 - x, y), (x, 1 - y), (1 - x, 1 - y)])]
        for cp in cps:
            cp.start()
        for cp in cps:
            cp.wait_recv()
        tot = (chips[0] + chips[1]) + (chips[2] + chips[3])
        o_ref[...] = tot
        loss = jnp.sum(tot[LOSS_ROW:LOSS_ROW + 1, :], axis=-1, keepdims=True)
        o_ref[LOSS_ROW:LOSS_ROW + 1, :] = jnp.broadcast_to(loss, (1, 128))
        swap.wait_send()
        for cp in cps:
            cp.wait_send()

    vm = pl.BlockSpec(memory_space=pltpu.VMEM)
    return pl.pallas_call(
        body, in_specs=[vm], out_specs=vm, out_shape=SDS((SMALL_ROWS, 128), f32),
        scratch_shapes=[pltpu.VMEM((2, SMALL_ROWS, 128), f32), pltpu.VMEM((4, SMALL_ROWS, 128), f32),
                        pltpu.SemaphoreType.DMA((4,)), pltpu.SemaphoreType.DMA((4,))],
        compiler_params=pltpu.CompilerParams(has_side_effects=True),
        name="all_reduce_small")(packed)


def _adamw_math(w, g, m, v):
    m = ADAM_B1 * m + (1.0 - ADAM_B1) * g
    v = ADAM_B2 * v + (1.0 - ADAM_B2) * (g * g)
    m_hat = m / (1.0 - ADAM_B1 ** ADAM_STEP)
    v_hat = v / (1.0 - ADAM_B2 ** ADAM_STEP)
    delta = -ADAM_LR * (m_hat / (jnp.sqrt(v_hat) + ADAM_EPS) + ADAM_WD * w)
    return delta, m, v


def _adamw_big(ws, ms, vs, red):
    npiece = len(BIG)
    rmax = max(PIECE_ROWS)

    def body(*refs):
        ins = (refs[0:npiece], refs[npiece:2 * npiece], refs[2 * npiece:3 * npiece])
        red_ref = refs[3 * npiece]
        out_refs = refs[3 * npiece + 1:7 * npiece + 1]
        inb, outb, in_sems, out_sems = refs[7 * npiece + 1:]

        def loads(k):
            s, r = k % 2, PIECE_ROWS[k]
            cps = [pltpu.make_async_copy(ins[q][k].at[0], inb.at[s, q, pl.ds(0, r), :], in_sems.at[4 * s + q])
                   for q in range(3)]
            cps.append(pltpu.make_async_copy(red_ref.at[pl.ds(PIECE_OFF[k], r), :], inb.at[s, 3, pl.ds(0, r), :],
                                             in_sems.at[4 * s + 3]))
            return cps

        def stores(k):
            s, r = k % 2, PIECE_ROWS[k]
            return [pltpu.make_async_copy(outb.at[s, q, pl.ds(0, r), :], out_refs[q * npiece + k].at[0],
                                          out_sems.at[4 * s + q]) for q in range(4)]

        for cp in loads(0):
            cp.start()
        for k in range(npiece):
            s, r = k % 2, PIECE_ROWS[k]
            if k + 1 < npiece:
                for cp in loads(k + 1):
                    cp.start()
            for cp in loads(k):
                cp.wait()
            if k >= 2:
                for cp in stores(k - 2):
                    cp.wait()
            g = inb[s, 3, 0:r, :]
            d, nm, nv = _adamw_math(inb[s, 0, 0:r, :], g, inb[s, 1, 0:r, :], inb[s, 2, 0:r, :])
            outb[s, 0, 0:r, :] = g
            outb[s, 1, 0:r, :] = d
            outb[s, 2, 0:r, :] = nm
            outb[s, 3, 0:r, :] = nv
            for cp in stores(k):
                cp.start()
        for k in (npiece - 2, npiece - 1):
            for cp in stores(k):
                cp.wait()

    hbm = pl.BlockSpec(memory_space=pl.ANY)
    outs = pl.pallas_call(
        body, in_specs=[hbm] * (3 * npiece + 1), out_specs=[hbm] * (4 * npiece),
        out_shape=tuple(SDS(w.shape, f32) for _ in range(4) for w in ws),
        scratch_shapes=[pltpu.VMEM((2, 4, rmax, D), f32), pltpu.VMEM((2, 4, rmax, D), f32),
                        pltpu.SemaphoreType.DMA((8,)), pltpu.SemaphoreType.DMA((8,))],
        compiler_params=_cparams(None, VMEM_LIMIT_V7X), name="adamw_big")(*ws, *ms, *vs, red)
    return [list(outs[q * npiece:(q + 1) * npiece]) for q in range(4)]


def _adamw_small(w, m, v, g, name):
    def body(w_ref, m_ref, v_ref, g_ref, d_ref, nm_ref, nv_ref):
        d, nm, nv = _adamw_math(w_ref[...], g_ref[...], m_ref[...], v_ref[...])
        d_ref[...] = d
        nm_ref[...] = nm
        nv_ref[...] = nv

    return pl.pallas_call(
        body, out_shape=tuple(SDS(w.shape, f32) for _ in range(3)), name=name)(w, m, v, g)


WEIGHTS = ("ffn1_norm", "ffn1_w_gate", "ffn1_w_up", "ffn1_w_down", "mix_norm", "w_in", "q_norm", "k_norm",
           "attn_sinks", "rel_bias", "pool_w", "pool_scale", "w_out", "ffn2_norm", "ffn2_w_gate", "ffn2_w_up",
           "ffn2_w_down")
BIG = (("ffn1_w_gate", True), ("ffn1_w_up", True), ("ffn1_w_down", False), ("w_in", True), ("w_out", False),
       ("ffn2_w_gate", True), ("ffn2_w_up", True), ("ffn2_w_down", False))


def kernel(x, ffn1_norm, ffn1_w_gate, ffn1_w_up, ffn1_w_down, mix_norm, w_in, q_norm, k_norm, attn_sinks, rel_bias, pool_w, pool_scale, w_out, ffn2_norm, ffn2_w_gate, ffn2_w_up, ffn2_w_down, loss_target, m_ffn1_norm, m_ffn1_w_gate, m_ffn1_w_up, m_ffn1_w_down, m_mix_norm, m_w_in, m_q_norm, m_k_norm, m_attn_sinks, m_rel_bias, m_pool_w, m_pool_scale, m_w_out, m_ffn2_norm, m_ffn2_w_gate, m_ffn2_w_up, m_ffn2_w_down, v_ffn1_norm, v_ffn1_w_gate, v_ffn1_w_up, v_ffn1_w_down, v_mix_norm, v_w_in, v_q_norm, v_k_norm, v_attn_sinks, v_rel_bias, v_pool_w, v_pool_scale, v_w_out, v_ffn2_norm, v_ffn2_w_gate, v_ffn2_w_up, v_ffn2_w_down):
    args = dict(locals())
    w = {n: args[n] for n in WEIGHTS}
    m = {n: args["m_" + n] for n in WEIGHTS}
    v = {n: args["v_" + n] for n in WEIGHTS}

    as_rows = lambda a, tr: jnp.swapaxes(a, 1, 2) if tr else a
    shard = jnp.concatenate([as_rows(w[n], tr)[0].astype(bf16) for n, tr in BIG], axis=0)
    w1, wint, wout, w2 = _all_gather_weights(shard)

    gx, (dw1, dwint, dwout, dw2), small = _local_step(
        x[0], loss_target[0], w1, wint, wout, w2, ffn1_norm, mix_norm, ffn2_norm, q_norm, k_norm, attn_sinks,
        rel_bias, pool_w[0], pool_scale)

    red = _reduce_scatter_grads(dw1, dwint, dwout, dw2)
    small_tot = _all_reduce_small(_pack_small(small))

    grads, deltas, new_m, new_v = {}, {}, {}, {}
    big_out = _adamw_big(*[[as_rows(t[n], tr) for n, tr in BIG] for t in (w, m, v)], red)
    for k, (n, tr) in enumerate(BIG):
        grads[n], deltas[n], new_m[n], new_v[n] = [as_rows(o[k], tr) for o in big_out]
    small_names = [n for n in SMALL_NAMES if n != "loss"]
    ds, nms, nvs = _adamw_small(_pack_small({n: w[n] for n in small_names}), _pack_small({n: m[n] for n in small_names}),
                                _pack_small({n: v[n] for n in small_names}), small_tot, "adamw_small")
    for n in small_names:
        grads[n] = _unpack_small(small_tot, n)
        deltas[n], new_m[n], new_v[n] = _unpack_small(ds, n), _unpack_small(nms, n), _unpack_small(nvs, n)
    loss = small_tot[LOSS_ROW, 0]
    return (loss, gx[None], *[grads[n] for n in WEIGHTS], *[deltas[n] for n in WEIGHTS],
            *[new_m[n] for n in WEIGHTS], *[new_v[n] for n in WEIGHTS])
```

```python
import functools

import jax
import jax.numpy as jnp
import numpy as np
from jax import lax
from jax.experimental import pallas as pl
from jax.experimental.pallas import tpu as pltpu

f32, bf16, i32 = jnp.float32, jnp.bfloat16, jnp.int32
SDS = jax.ShapeDtypeStruct

D = 1024
F = 2816
HD = 64
NH = 8
NKV = 2
GQA = NH // NKV
DATTN = NH * HD
DKV = NKV * HD
DPOOL = 512
POOL_WINDOWS = (2, 4, 8, 16)
PGD = DPOOL // len(POOL_WINDOWS)
DIN = DATTN + 2 * DKV + DPOOL
DMIX = DATTN + DPOOL
BLK = 128
NBUCK = 32
MAX_DISTANCE = 128
EPS = 1e-6
NEG = -1e30
SCALE = HD ** -0.5

ADAM_LR, ADAM_B1, ADAM_B2, ADAM_EPS, ADAM_WD, ADAM_STEP = 0.001, 0.9, 0.999, 1e-08, 0.01, 10

NDEV = 8
FS = F // NDEV
INS = DIN // NDEV
OUTS = DMIX // NDEV
PIECE_ROWS = (FS, FS, FS, INS, OUTS, FS, FS, FS)
PIECE_OFF = tuple(int(v) for v in np.cumsum((0,) + PIECE_ROWS[:-1]))
PACK_ROWS = sum(PIECE_ROWS)

VMEM_LIMIT_V7X = 56 * 1024 * 1024

MESH = pl.DeviceIdType.MESH


def _cparams(sem=None, vmem=None):
    return pltpu.CompilerParams(dimension_semantics=sem, vmem_limit_bytes=vmem)


def _nt(a, b):
    return lax.dot_general(a, b, (((1,), (1,)), ((), ())), preferred_element_type=f32)


def _tn(a, b):
    return lax.dot_general(a, b, (((0,), (0,)), ((), ())), preferred_element_type=f32)


def _nn(a, b):
    return jnp.dot(a, b, preferred_element_type=f32)


def _sigmoid(x):
    return 1.0 / (1.0 + jnp.exp(-x))


def _norm_fwd(x, g, name):
    T = x.shape[0]
    tm = min(512, T)

    def body(x_ref, g_ref, h_ref):
        xv = x_ref[...]
        r = lax.rsqrt(jnp.mean(xv * xv, axis=-1, keepdims=True) + EPS)
        h_ref[...] = (xv * r * g_ref[...]).astype(bf16)

    return pl.pallas_call(
        body, grid=(T // tm,),
        in_specs=[pl.BlockSpec((tm, D), lambda i: (i, 0)), pl.BlockSpec((1, D), lambda i: (0, 0))],
        out_specs=pl.BlockSpec((tm, D), lambda i: (i, 0)),
        out_shape=SDS((T, D), bf16), name=name)(x, g)


def _norm_bwd(dh, x, g, dres, out_scale, name):
    T = x.shape[0]
    tm = min(512, T)

    def body(dh_ref, x_ref, g_ref, dr_ref, dx_ref, dxb_ref, dg_ref):
        i = pl.program_id(0)
        xv = x_ref[...]
        r = lax.rsqrt(jnp.mean(xv * xv, axis=-1, keepdims=True) + EPS)
        xh = xv * r
        dhv = dh_ref[...]
        dxh = dhv * g_ref[...]
        dx = dr_ref[...] + r * (dxh - xh * jnp.mean(dxh * xh, axis=-1, keepdims=True))
        dx_ref[...] = dx
        dxb_ref[...] = (out_scale * dx).astype(bf16)
        dg = jnp.sum(dhv * xh, axis=0, keepdims=True)

        @pl.when(i == 0)
        def _():
            dg_ref[...] = dg

        @pl.when(i > 0)
        def _():
            dg_ref[...] += dg

    tok = pl.BlockSpec((tm, D), lambda i: (i, 0))
    vec = pl.BlockSpec((1, D), lambda i: (0, 0))
    return pl.pallas_call(
        body, grid=(T // tm,),
        in_specs=[tok, tok, vec, tok], out_specs=[tok, tok, vec],
        out_shape=(SDS((T, D), f32), SDS((T, D), bf16), SDS((1, D), f32)),
        compiler_params=_cparams(("arbitrary",)), name=name)(dh, x, g, dres)


def _ffn_tiles(T):
    return min(1024, T), 256


def _ffn_fwd(h, w, x, name):
    T = h.shape[0]
    tm, tf = _ffn_tiles(T)
    nf = F // tf

    def body(h_ref, w_ref, x_ref, xo_ref, g_ref, u_ref, acc):
        fi = pl.program_id(1)
        hv = h_ref[...]
        gate = _nt(hv, w_ref[0])
        up = _nt(hv, w_ref[1])
        act = gate * _sigmoid(gate) * up
        g_ref[...] = gate.astype(bf16)
        u_ref[...] = up.astype(bf16)
        part = _nn(act.astype(bf16), w_ref[2])

        @pl.when(fi == 0)
        def _():
            acc[...] = part

        @pl.when(fi > 0)
        def _():
            acc[...] += part

        @pl.when(fi == nf - 1)
        def _():
            xo_ref[...] = x_ref[...] + 0.5 * acc[...]

    tok = pl.BlockSpec((tm, D), lambda i, f: (i, 0))
    act_spec = pl.BlockSpec((tm, tf), lambda i, f: (i, f))
    return pl.pallas_call(
        body, grid=(T // tm, nf),
        in_specs=[tok, pl.BlockSpec((3, tf, D), lambda i, f: (0, f, 0)), tok],
        out_specs=[tok, act_spec, act_spec],
        out_shape=(SDS((T, D), f32), SDS((T, F), bf16), SDS((T, F), bf16)),
        scratch_shapes=[pltpu.VMEM((tm, D), f32)],
        compiler_params=_cparams(("arbitrary", "arbitrary"), VMEM_LIMIT_V7X), name=name)(h, w, x)


def _ffn_bwd(dob, h, gate, up, w, name):
    T = h.shape[0]
    tm, tf = _ffn_tiles(T)
    nf, nt = F // tf, T // tm

    def body(do_ref, h_ref, g_ref, u_ref, w_ref, dh_ref, dw_ref, dwacc):
        fi, ti = pl.program_id(0), pl.program_id(1)
        dov = do_ref[...]
        gv = g_ref[...].astype(f32)
        uv = u_ref[...].astype(f32)
        sg = _sigmoid(gv)
        sil = gv * sg
        dact = _nt(dov, w_ref[2])
        dup = dact * sil
        dgate = dact * uv * (sg * (1.0 + gv * (1.0 - sg)))
        dgu = jnp.concatenate([dgate.astype(bf16), dup.astype(bf16)], axis=1)
        actb = (sil * uv).astype(bf16)
        wgu = w_ref[0:2].reshape(2 * tf, D)
        dh_part = _nn(dgu, wgu)
        rows = pl.ds(pl.multiple_of(ti * tm, tm), tm)

        @pl.when(fi == 0)
        def _():
            dh_ref[rows, :] = dh_part

        @pl.when(fi > 0)
        def _():
            dh_ref[rows, :] += dh_part

        dwgu = _tn(dgu, h_ref[...])
        dwd = _tn(actb, dov)

        @pl.when(ti == 0)
        def _():
            dwacc[0:2 * tf, :] = dwgu
            dwacc[2 * tf:3 * tf, :] = dwd

        @pl.when(ti > 0)
        def _():
            dwacc[0:2 * tf, :] += dwgu
            dwacc[2 * tf:3 * tf, :] += dwd

        @pl.when(ti == nt - 1)
        def _():
            dw_ref[...] = dwacc[...].reshape(3, tf, D).astype(bf16)

    tok = pl.BlockSpec((tm, D), lambda f, i: (i, 0))
    act_spec = pl.BlockSpec((tm, tf), lambda f, i: (i, f))
    wspec = pl.BlockSpec((3, tf, D), lambda f, i: (0, f, 0))
    return pl.pallas_call(
        body, grid=(nf, nt),
        in_specs=[tok, tok, act_spec, act_spec, wspec],
        out_specs=[pl.BlockSpec((T, D), lambda f, i: (0, 0)), wspec],
        out_shape=(SDS((T, D), f32), SDS((3, F, D), bf16)),
        scratch_shapes=[pltpu.VMEM((3 * tf, D), f32)],
        compiler_params=_cparams(("arbitrary", "arbitrary"), VMEM_LIMIT_V7X), name=name)(dob, h, gate, up, w)


def _loss_grad(y, target, name):
    T = y.shape[0]
    tm = min(512, T)

    def body(y_ref, t_ref, dy_ref, dyb_ref, l_ref):
        i = pl.program_id(0)
        e = y_ref[...] - t_ref[...]
        dy = e * (1.0 / D)
        dy_ref[...] = dy
        dyb_ref[...] = (0.5 * dy).astype(bf16)
        col = jnp.sum(e * e, axis=0, keepdims=True) * (0.5 / D)
        lanes = col[:, 0:128]
        for k in range(1, D // 128):
            lanes = lanes + col[:, 128 * k:128 * (k + 1)]

        @pl.when(i == 0)
        def _():
            l_ref[...] = lanes

        @pl.when(i > 0)
        def _():
            l_ref[...] += lanes

    tok = pl.BlockSpec((tm, D), lambda i: (i, 0))
    return pl.pallas_call(
        body, grid=(T // tm,), in_specs=[tok, tok],
        out_specs=[tok, tok, pl.BlockSpec((1, 128), lambda i: (0, 0))],
        out_shape=(SDS((T, D), f32), SDS((T, D), bf16), SDS((1, 128), f32)),
        compiler_params=_cparams(("arbitrary",)), name=name)(y, target)


def _in_proj_fwd(h, wint, name):
    T = h.shape[0]
    tm = min(512, T)

    def body(h_ref, w_ref, z_ref):
        z_ref[...] = _nt(h_ref[...], w_ref[...])

    return pl.pallas_call(
        body, grid=(T // tm,),
        in_specs=[pl.BlockSpec((tm, D), lambda i: (i, 0)), pl.BlockSpec((DIN, D), lambda i: (0, 0))],
        out_specs=pl.BlockSpec((tm, DIN), lambda i: (i, 0)),
        out_shape=SDS((T, DIN), f32), name=name)(h, wint)


def _in_proj_bwd(dz, wint, h, name):
    T = h.shape[0]
    tm = min(512, T)
    nt = T // tm

    def body(dz_ref, w_ref, h_ref, dh_ref, dw_ref, acc):
        i = pl.program_id(0)
        dzb = dz_ref[...].astype(bf16)
        dh_ref[...] = _nn(dzb, w_ref[...])
        part = _tn(dzb, h_ref[...])

        @pl.when(i == 0)
        def _():
            acc[...] = part

        @pl.when(i > 0)
        def _():
            acc[...] += part

        @pl.when(i == nt - 1)
        def _():
            dw_ref[...] = acc[...].astype(bf16)

    wspec = pl.BlockSpec((DIN, D), lambda i: (0, 0))
    return pl.pallas_call(
        body, grid=(nt,),
        in_specs=[pl.BlockSpec((tm, DIN), lambda i: (i, 0)), wspec, pl.BlockSpec((tm, D), lambda i: (i, 0))],
        out_specs=[pl.BlockSpec((tm, D), lambda i: (i, 0)), wspec],
        out_shape=(SDS((T, D), f32), SDS((DIN, D), bf16)),
        scratch_shapes=[pltpu.VMEM((DIN, D), f32)],
        compiler_params=_cparams(("arbitrary",)), name=name)(dz, wint, h)


def _out_proj_fwd(ymix, wout, x, name):
    T = x.shape[0]
    tm = min(512, T)

    def body(y_ref, w_ref, x_ref, o_ref):
        o_ref[...] = x_ref[...] + _nn(y_ref[...], w_ref[...])

    tok = pl.BlockSpec((tm, D), lambda i: (i, 0))
    return pl.pallas_call(
        body, grid=(T // tm,),
        in_specs=[pl.BlockSpec((tm, DMIX), lambda i: (i, 0)), pl.BlockSpec((DMIX, D), lambda i: (0, 0)), tok],
        out_specs=tok, out_shape=SDS((T, D), f32), name=name)(ymix, wout, x)


def _out_proj_bwd(dxb, wout, ymix, name):
    T = dxb.shape[0]
    tm = min(512, T)
    nt = T // tm

    def body(dx_ref, w_ref, y_ref, dy_ref, dw_ref, acc):
        i = pl.program_id(0)
        dxv = dx_ref[...]
        dy_ref[...] = _nt(dxv, w_ref[...])
        part = _tn(y_ref[...], dxv)

        @pl.when(i == 0)
        def _():
            acc[...] = part

        @pl.when(i > 0)
        def _():
            acc[...] += part

        @pl.when(i == nt - 1)
        def _():
            dw_ref[...] = acc[...].astype(bf16)

    wspec = pl.BlockSpec((DMIX, D), lambda i: (0, 0))
    return pl.pallas_call(
        body, grid=(nt,),
        in_specs=[pl.BlockSpec((tm, D), lambda i: (i, 0)), wspec, pl.BlockSpec((tm, DMIX), lambda i: (i, 0))],
        out_specs=[pl.BlockSpec((tm, DMIX), lambda i: (i, 0)), wspec],
        out_shape=(SDS((T, DMIX), f32), SDS((DMIX, D), bf16)),
        scratch_shapes=[pltpu.VMEM((DMIX, D), f32)],
        compiler_params=_cparams(("arbitrary",)), name=name)(dxb, wout, ymix)


def _t5_bucket_table():
    ql = np.arange(BLK)[:, None]
    kl = np.arange(2 * BLK)[None, :]
    n = np.maximum(ql + BLK - kl, 0)
    max_exact = NBUCK // 2
    large = max_exact + (np.log(np.maximum(n, 1) / max_exact) / np.log(MAX_DISTANCE / max_exact)
                         * (NBUCK - max_exact)).astype(np.int32)
    large = np.minimum(large, NBUCK - 1)
    return np.where(n < max_exact, n, large).astype(np.int32)


def _fill_bias(bk_ref, rb_ref, bias_scr):
    bk = bk_ref[...]
    for h in range(NH):
        def step(b, acc, h=h):
            return acc + jnp.where(bk == b, rb_ref[b, h], 0.0)
        bias_scr[h] = lax.fori_loop(0, NBUCK, step, jnp.zeros((BLK, 2 * BLK), f32))


def _attn_probs(zc_ref, zp_ref, kh, qg, kg, sk_ref, bias_scr, n):
    kc = DATTN + HD * kh
    vc = DATTN + DKV + HD * kh
    kx = jnp.concatenate([zp_ref[:, kc:kc + HD], zc_ref[:, kc:kc + HD]], axis=0)
    vx = jnp.concatenate([zp_ref[:, vc:vc + HD], zc_ref[:, vc:vc + HD]], axis=0)
    qx = jnp.concatenate([zc_ref[:, HD * (GQA * kh + g):HD * (GQA * kh + g + 1)] for g in range(GQA)], axis=0)
    rq = lax.rsqrt(jnp.mean(qx * qx, axis=-1, keepdims=True) + EPS)
    rk = lax.rsqrt(jnp.mean(kx * kx, axis=-1, keepdims=True) + EPS)
    qhat, khat = qx * rq, kx * rk
    qnb, knb = (qhat * qg).astype(bf16), (khat * kg).astype(bf16)
    s = _nt(qnb, knb) * SCALE + bias_scr[GQA * kh:GQA * (kh + 1)].reshape(GQA * BLK, 2 * BLK)
    row = lax.broadcasted_iota(i32, (GQA * BLK, 2 * BLK), 0) & (BLK - 1)
    col = lax.broadcasted_iota(i32, (GQA * BLK, 2 * BLK), 1)
    mask = (col > row) & (col <= row + BLK) & ((col >= BLK) | (n > 0))
    s = jnp.where(mask, s, NEG)
    ridx = lax.broadcasted_iota(i32, (GQA * BLK, 1), 0)
    sink = jnp.full((GQA * BLK, 1), sk_ref[GQA * kh + GQA - 1], f32)
    for g in range(GQA - 2, -1, -1):
        sink = jnp.where(ridx < (g + 1) * BLK, sk_ref[GQA * kh + g], sink)
    m = jnp.maximum(jnp.max(s, axis=-1, keepdims=True), sink)
    e = jnp.exp(s - m)
    es = jnp.exp(sink - m)
    den = jnp.sum(e, axis=-1, keepdims=True) + es
    return dict(p=e / den, psink=es / den, qhat=qhat, khat=khat, rq=rq, rk=rk, qnb=qnb, knb=knb, vb=vx.astype(bf16))


def _pool_group(zc_ref, zp_ref, g, w, n):
    c0 = DATTN + 2 * DKV + PGD * g
    uc = zc_ref[:, c0:c0 + PGD]
    up = jnp.where(n > 0, zp_ref[:, c0:c0 + PGD], 0.0)
    ue = jnp.concatenate([up, uc], axis=0)
    hi = ue.astype(bf16)
    lo = (ue - hi.astype(f32)).astype(bf16)
    t = lax.broadcasted_iota(i32, (BLK, 2 * BLK), 0)
    s = lax.broadcasted_iota(i32, (BLK, 2 * BLK), 1)
    band = jnp.where((s <= t + BLK) & (s > t + BLK - w), 1.0, 0.0).astype(bf16)
    sm = _nn(band, hi) + _nn(band, lo)
    pos = n * BLK + lax.broadcasted_iota(i32, (BLK, 1), 0) + 1
    cnt = jnp.minimum(pos, w).astype(f32)
    return sm / cnt - uc, band, cnt


def _mix_fwd(z, qg, kg, sinks, relb, bucket, pool_w, pscale, name):
    T = z.shape[0]
    nb = T // BLK

    def body(zc_ref, zp_ref, qg_ref, kg_ref, sk_ref, rb_ref, bk_ref, pw_ref, ps_ref, y_ref, bias_scr, yacc):
        n = pl.program_id(0)

        @pl.when(n == 0)
        def _():
            _fill_bias(bk_ref, rb_ref, bias_scr)

        for kh in range(NKV):
            a = _attn_probs(zc_ref, zp_ref, kh, qg_ref[...], kg_ref[...], sk_ref, bias_scr, n)
            o = _nn(a["p"].astype(bf16), a["vb"])
            for g in range(GQA):
                hc = HD * (GQA * kh + g)
                yacc[:, hc:hc + HD] = o[g * BLK:(g + 1) * BLK]
        for g, w in enumerate(POOL_WINDOWS):
            pooled, _, _ = _pool_group(zc_ref, zp_ref, g, w, n)
            yp = _nn(pooled.astype(bf16), pw_ref[g].astype(bf16)) * ps_ref[:, PGD * g:PGD * (g + 1)]
            yacc[:, DATTN + PGD * g:DATTN + PGD * (g + 1)] = yp
        y_ref[...] = yacc[...].astype(bf16)

    full = lambda *shape: pl.BlockSpec(shape, lambda n: (0,) * len(shape))
    smem = pl.BlockSpec(memory_space=pltpu.SMEM)
    return pl.pallas_call(
        body, grid=(nb,),
        in_specs=[pl.BlockSpec((BLK, DIN), lambda n: (n, 0)),
                  pl.BlockSpec((BLK, DIN), lambda n: (jnp.maximum(n - 1, 0), 0)),
                  full(1, HD), full(1, HD), smem, smem, full(BLK, 2 * BLK),
                  full(len(POOL_WINDOWS), PGD, PGD), full(1, DPOOL)],
        out_specs=pl.BlockSpec((BLK, DMIX), lambda n: (n, 0)),
        out_shape=SDS((T, DMIX), bf16),
        scratch_shapes=[pltpu.VMEM((NH, BLK, 2 * BLK), f32), pltpu.VMEM((BLK, DMIX), f32)],
        compiler_params=_cparams(("arbitrary",)), name=name)(z, z, qg, kg, sinks, relb, bucket, pool_w, pscale)


def _mix_bwd(z, dy, qg, kg, sinks, relb, bucket, pool_w, pscale, name):
    T = z.shape[0]
    nb = T // BLK

    def body(zc_ref, zp_ref, dy_ref, qg_ref, kg_ref, sk_ref, rb_ref, bk_ref, pw_ref, ps_ref,
             dz_ref, dqg_ref, dkg_ref, dsk_ref, drb_ref, dpw_ref, dps_ref, bias_scr, dbias_scr):
        n = pl.program_id(0)
        rows = pl.ds(pl.multiple_of(n * BLK, BLK), BLK)
        prow = pl.ds(pl.multiple_of(jnp.maximum(n - 1, 0) * BLK, BLK), BLK)

        @pl.when(n == 0)
        def _():
            _fill_bias(bk_ref, rb_ref, bias_scr)
            dbias_scr[...] = jnp.zeros_like(dbias_scr)
            dqg_ref[...] = jnp.zeros_like(dqg_ref)
            dkg_ref[...] = jnp.zeros_like(dkg_ref)
            dsk_ref[...] = jnp.zeros_like(dsk_ref)
            dpw_ref[...] = jnp.zeros_like(dpw_ref)
            dps_ref[...] = jnp.zeros_like(dps_ref)

        qg, kg = qg_ref[...], kg_ref[...]
        lane = lax.broadcasted_iota(i32, (1, 128), 1)
        dsk = jnp.zeros((1, 128), f32)
        for kh in range(NKV):
            a = _attn_probs(zc_ref, zp_ref, kh, qg, kg, sk_ref, bias_scr, n)
            p = a["p"]
            do = jnp.concatenate([dy_ref[:, HD * (GQA * kh + g):HD * (GQA * kh + g + 1)] for g in range(GQA)],
                                 axis=0).astype(bf16)
            dv = _tn(p.astype(bf16), do)
            dp = _nt(do, a["vb"])
            delta = jnp.sum(p * dp, axis=-1, keepdims=True)
            ds = p * (dp - delta)
            sinkterm = a["psink"] * delta
            for g in range(GQA):
                h = GQA * kh + g
                dbias_scr[h] += ds[g * BLK:(g + 1) * BLK]
                tot = jnp.sum(sinkterm[g * BLK:(g + 1) * BLK], axis=0, keepdims=True)
                dsk = dsk - jnp.where(lane == h, tot, 0.0)
            dsb = ds.astype(bf16)
            dqn = _nn(dsb, a["knb"]) * SCALE
            dkn = _tn(dsb, a["qnb"]) * SCALE
            qhat, khat = a["qhat"], a["khat"]
            dqg_ref[...] += jnp.sum(dqn * qhat, axis=0, keepdims=True)
            dkg_ref[...] += jnp.sum(dkn * khat, axis=0, keepdims=True)
            dqh = dqn * qg
            dq = a["rq"] * (dqh - qhat * jnp.mean(dqh * qhat, axis=-1, keepdims=True))
            dkh = dkn * kg
            dk = a["rk"] * (dkh - khat * jnp.mean(dkh * khat, axis=-1, keepdims=True))
            kc = DATTN + HD * kh
            vc = DATTN + DKV + HD * kh
            for g in range(GQA):
                hc = HD * (GQA * kh + g)
                dz_ref[rows, hc:hc + HD] = dq[g * BLK:(g + 1) * BLK]
            dz_ref[rows, kc:kc + HD] = dk[BLK:2 * BLK]
            dz_ref[rows, vc:vc + HD] = dv[BLK:2 * BLK]

            @pl.when(n > 0)
            def _(dk=dk, dv=dv, kc=kc, vc=vc):
                dz_ref[prow, kc:kc + HD] += dk[0:BLK]
                dz_ref[prow, vc:vc + HD] += dv[0:BLK]

        dsk_ref[...] += dsk

        for g, w in enumerate(POOL_WINDOWS):
            c0 = DATTN + 2 * DKV + PGD * g
            pooled, band, cnt = _pool_group(zc_ref, zp_ref, g, w, n)
            pb = pooled.astype(bf16)
            wb = pw_ref[g].astype(bf16)
            dyp = dy_ref[:, DATTN + PGD * g:DATTN + PGD * (g + 1)]
            ypre = _nn(pb, wb)
            dps_ref[:, PGD * g:PGD * (g + 1)] += jnp.sum(dyp * ypre, axis=0, keepdims=True)
            dyg = (dyp * ps_ref[:, PGD * g:PGD * (g + 1)]).astype(bf16)
            dpw_ref[g] += _tn(pb, dyg)
            dpooled = _nt(dyg, wb)
            dsm = dpooled / cnt
            hi = dsm.astype(bf16)
            lo = (dsm - hi.astype(f32)).astype(bf16)
            due = _tn(band, hi) + _tn(band, lo)
            dz_ref[rows, c0:c0 + PGD] = due[BLK:2 * BLK] - dpooled

            @pl.when(n > 0)
            def _(due=due, c0=c0):
                dz_ref[prow, c0:c0 + PGD] += due[0:BLK]

        @pl.when(n == nb - 1)
        def _():
            bk = bk_ref[...]
            ri = lax.broadcasted_iota(i32, (NBUCK, NH), 0)
            ci = lax.broadcasted_iota(i32, (NBUCK, NH), 1)

            def step(b, acc):
                for h in range(NH):
                    sel = jnp.where(bk == b, dbias_scr[h], 0.0)
                    tot = jnp.sum(jnp.sum(sel, axis=1, keepdims=True), axis=0, keepdims=True)
                    acc = acc + jnp.where((ri == b) & (ci == h), tot, 0.0)
                return acc

            drb_ref[...] = lax.fori_loop(0, NBUCK, step, jnp.zeros((NBUCK, NH), f32))

    full = lambda *shape: pl.BlockSpec(shape, lambda n: (0,) * len(shape))
    smem = pl.BlockSpec(memory_space=pltpu.SMEM)
    npg = len(POOL_WINDOWS)
    return pl.pallas_call(
        body, grid=(nb,),
        in_specs=[pl.BlockSpec((BLK, DIN), lambda n: (n, 0)),
                  pl.BlockSpec((BLK, DIN), lambda n: (jnp.maximum(n - 1, 0), 0)),
                  pl.BlockSpec((BLK, DMIX), lambda n: (n, 0)),
                  full(1, HD), full(1, HD), smem, smem, full(BLK, 2 * BLK), full(npg, PGD, PGD), full(1, DPOOL)],
        out_specs=[full(T, DIN), full(1, HD), full(1, HD), full(1, 128), full(NBUCK, NH),
                   full(npg, PGD, PGD), full(1, DPOOL)],
        out_shape=(SDS((T, DIN), f32), SDS((1, HD), f32), SDS((1, HD), f32), SDS((1, 128), f32),
                   SDS((NBUCK, NH), f32), SDS((npg, PGD, PGD), f32), SDS((1, DPOOL), f32)),
        scratch_shapes=[pltpu.VMEM((NH, BLK, 2 * BLK), f32), pltpu.VMEM((NH, BLK, 2 * BLK), f32)],
        compiler_params=_cparams(("arbitrary",), VMEM_LIMIT_V7X),
        name=name)(z, z, dy, qg, kg, sinks, relb, bucket, pool_w, pscale)


class _LocalWeights:
    def __init__(self, w1, wint, wout, w2):
        self.w1, self.wint, self.wout, self.w2 = w1, wint, wout, w2

    def ffn1(self):
        return self.w1

    def mix(self, after):
        return self.wint, self.wout

    def before_out_proj(self, wout, after):
        return wout

    def ffn2(self, after):
        return self.w2


def _local_step(x, target, weights, g1, gm, g3, qg, kg, sinks, relb, pool_w, pscale):
    bucket = jnp.asarray(_t5_bucket_table())
    sk = sinks.reshape(NH)
    w1 = weights.ffn1()
    h1 = _norm_fwd(x, g1, "norm1_fwd")
    x1, gate1, up1 = _ffn_fwd(h1, w1, x, "ffn1_fwd")
    h2 = _norm_fwd(x1, gm, "norm2_fwd")
    wint, wout = weights.mix(h2)
    z = _in_proj_fwd(h2, wint, "in_proj_fwd")
    ymix = _mix_fwd(z, qg, kg, sk, relb, bucket, pool_w, pscale, "mix_fwd")
    wout = weights.before_out_proj(wout, ymix)
    x2 = _out_proj_fwd(ymix, wout, x1, "out_proj_fwd")
    h3 = _norm_fwd(x2, g3, "norm3_fwd")
    w2 = weights.ffn2(h3)
    y, gate2, up2 = _ffn_fwd(h3, w2, x2, "ffn2_fwd")
    dy, dyb, loss_lanes = _loss_grad(y, target, "loss_grad")

    dh3, dw2 = _ffn_bwd(dyb, h3, gate2, up2, w2, "ffn2_bwd")
    dx2, dx2b, dg3 = _norm_bwd(dh3, x2, g3, dy, 1.0, "norm3_bwd")
    dymix, dwout = _out_proj_bwd(dx2b, wout, ymix, "out_proj_bwd")
    dz, dqg, dkg, dsk, drb, dpw, dps = _mix_bwd(z, dymix, qg, kg, sk, relb, bucket, pool_w, pscale, "mix_bwd")
    dh2, dwint = _in_proj_bwd(dz, wint, h2, "in_proj_bwd")
    dx1, dx1b, dgm = _norm_bwd(dh2, x1, gm, dx2, 0.5, "norm2_bwd")
    dh1, dw1 = _ffn_bwd(dx1b, h1, gate1, up1, w1, "ffn1_bwd")
    gx, _, dg1 = _norm_bwd(dh1, x, g1, dx1, 1.0, "norm1_bwd")
    small = dict(ffn1_norm=dg1, mix_norm=dgm, ffn2_norm=dg3, pool_scale=dps, q_norm=dqg, k_norm=dkg,
                 attn_sinks=dsk[:, :NH], rel_bias=drb, pool_w=dpw, loss=loss_lanes)
    return gx, (dw1, dwint, dwout, dw2), small


SMALL_NAMES = ("ffn1_norm", "mix_norm", "ffn2_norm", "pool_scale", "q_norm", "k_norm", "attn_sinks", "rel_bias",
               "pool_w", "loss")
SMALL_SHAPES = dict(ffn1_norm=(1, D), mix_norm=(1, D), ffn2_norm=(1, D), pool_scale=(1, DPOOL), q_norm=(1, HD),
                    k_norm=(1, HD), attn_sinks=(1, NH), rel_bias=(NBUCK, NH),
                    pool_w=(1, len(POOL_WINDOWS), PGD, PGD), loss=(1, 128))


def _small_rows(name):
    return -(-int(np.prod(SMALL_SHAPES[name])) // 128)


SMALL_OFF = {}
_r = 0
for _n in SMALL_NAMES:
    SMALL_OFF[_n] = _r
    _r += _small_rows(_n)
SMALL_ROWS = -(-_r // 8) * 8
LOSS_ROW = SMALL_OFF["loss"]


def _pack_small(vals):
    parts = []
    for n in SMALL_NAMES:
        size = _small_rows(n) * 128
        if n in vals:
            flat = vals[n].astype(f32).reshape(-1)
            parts.append(jnp.pad(flat, (0, size - flat.shape[0])))
        else:
            parts.append(jnp.zeros((size,), f32))
    flat = jnp.concatenate(parts)
    flat = jnp.pad(flat, (0, SMALL_ROWS * 128 - flat.shape[0]))
    return flat.reshape(SMALL_ROWS, 128)


def _unpack_small(packed, name):
    size = int(np.prod(SMALL_SHAPES[name]))
    r0 = SMALL_OFF[name]
    return packed[r0:r0 + _small_rows(name)].reshape(-1)[:size].reshape(SMALL_SHAPES[name])


def _position():
    return lax.axis_index("x"), lax.axis_index("y"), lax.axis_index("c")


def _dev_index(x, y, c):
    return 4 * x + 2 * y + c


G1_PIECES, MIX_PIECES, F2_PIECES = (0, 1, 2), (3, 4), (5, 6, 7)


def _group_rows(pieces):
    return sum(PIECE_ROWS[k] for k in pieces)


def _shard_piece(s_ref, k):
    return s_ref.at[pl.ds(PIECE_OFF[k], PIECE_ROWS[k]), :]


def _shard_group(s_ref, pieces):
    return s_ref.at[pl.ds(PIECE_OFF[pieces[0]], _group_rows(pieces)), :]


def _weight_pieces(w1_ref=None, wi_ref=None, wo_ref=None, w2_ref=None):
    arrs = {}
    if w1_ref is not None:
        arrs.update({0: w1_ref.at[0], 1: w1_ref.at[1], 2: w1_ref.at[2]})
    if wi_ref is not None:
        arrs[3] = wi_ref
    if wo_ref is not None:
        arrs[4] = wo_ref
    if w2_ref is not None:
        arrs.update({5: w2_ref.at[0], 6: w2_ref.at[1], 7: w2_ref.at[2]})
    return arrs


def _block_rows(arrs, k, dev):
    r = PIECE_ROWS[k]
    return arrs[k].at[pl.ds(pl.multiple_of(_dev_index(*dev) * r, 16), r), :]


def _all_gather_ffn1(shard):
    pieces = G1_PIECES

    def body(s_ref, w1_ref, send_sems, recv_sems, local_sem):
        x, y, c = _position()
        me, sib = (x, y, c), (x, y, 1 - c)
        chips = [(1 - x, y), (x, 1 - y), (1 - x, 1 - y)]
        arrs = _weight_pieces(w1_ref=w1_ref)

        def copies(rel, block, to, from_shard):
            return [pltpu.make_async_remote_copy(
                src_ref=_shard_piece(s_ref, k) if from_shard else _block_rows(arrs, k, block),
                dst_ref=_block_rows(arrs, k, block),
                send_sem=send_sems.at[rel], recv_sem=recv_sems.at[rel], device_id=to, device_id_type=MESH)
                for k in pieces]

        def whole(rel):
            grp = _shard_group(s_ref, pieces)
            return pltpu.make_async_remote_copy(src_ref=grp, dst_ref=grp, send_sem=send_sems.at[rel],
                                                recv_sem=recv_sems.at[rel], device_id=me, device_id_type=MESH)

        mine = [pltpu.make_async_copy(_shard_piece(s_ref, k), _block_rows(arrs, k, me), local_sem) for k in pieces]
        for cp in mine:
            cp.start()
        for cp in copies(0, me, sib, True):
            cp.start()
        for j, chip in enumerate(chips):
            for cp in copies(1 + j, me, (*chip, c), True):
                cp.start()
        for j, chip in enumerate(chips):
            whole(1 + j).wait_recv()
            for cp in copies(4 + j, (*chip, c), sib, False):
                cp.start()
        whole(0).wait_recv()
        for j in range(3):
            whole(4 + j).wait_recv()
        for rel in range(7):
            whole(rel).wait_send()
        grp = _shard_group(s_ref, pieces)
        pltpu.make_async_copy(grp, grp, local_sem).wait()

    hbm = pl.BlockSpec(memory_space=pl.ANY)
    return pl.pallas_call(
        body, in_specs=[hbm], out_specs=hbm, out_shape=SDS((3, F, D), bf16),
        scratch_shapes=[pltpu.SemaphoreType.DMA((7,)), pltpu.SemaphoreType.DMA((7,)), pltpu.SemaphoreType.DMA],
        compiler_params=pltpu.CompilerParams(has_side_effects=True),
        name="all_gather_ffn1")(shard)


HBM_SPEC = pl.BlockSpec(memory_space=pltpu.HBM)
SEM_SPEC = pl.BlockSpec(memory_space=pltpu.SEMAPHORE)
ANY_SPEC = pl.BlockSpec(memory_space=pl.ANY)
SPLIT_EFFECT = pltpu.SideEffectType.DATAFLOW_SIDE_EFFECTING


def _in_hbm(a):
    return pltpu.with_memory_space_constraint(a, pltpu.HBM)


def _hbm_like(a):
    return pltpu.HBM(a.shape, a.dtype)


def _place_own_rows(shard):
    pieces = MIX_PIECES + F2_PIECES

    def body(s_ref, wi_ref, wo_ref, w2_ref, sem):
        x, y, c = _position()
        arrs = _weight_pieces(wi_ref=wi_ref, wo_ref=wo_ref, w2_ref=w2_ref)
        for k in pieces:
            pltpu.make_async_copy(_shard_piece(s_ref, k), _block_rows(arrs, k, (x, y, c)), sem).start()
        grp = _shard_group(s_ref, pieces)
        pltpu.make_async_copy(grp, grp, sem).wait()

    return pl.pallas_call(
        body, in_specs=[ANY_SPEC], out_specs=[ANY_SPEC] * 3,
        out_shape=(SDS((DIN, D), bf16), SDS((DMIX, D), bf16), SDS((3, F, D), bf16)),
        scratch_shapes=[pltpu.SemaphoreType.DMA], name="place_own_rows")(shard)


def _xor_peer(x, y, c, k):
    return (x ^ (k >> 2), y ^ ((k >> 1) & 1), c ^ (k & 1))


def _gather_rest_start(shard, wi, wo, w2, w1):
    def body(s_ref, wi_ref, wo_ref, w2_ref, w1_ref,
             ssem_m, rsem_m, ssem_f, rsem_f0, rsem_f, s_o, wi_o, wo_o, w2_o, w1_o):
        x, y, c = _position()
        me, sib = (x, y, c), (x, y, 1 - c)
        chips = [(1 - x, y), (x, 1 - y), (1 - x, 1 - y)]
        arrs = _weight_pieces(wi_ref=wi_ref, wo_ref=wo_ref, w2_ref=w2_ref)
        for k in range(1, NDEV):
            for p in MIX_PIECES:
                pltpu.make_async_remote_copy(
                    src_ref=_shard_piece(s_ref, p), dst_ref=_block_rows(arrs, p, me), send_sem=ssem_m.at[k - 1],
                    recv_sem=rsem_m.at[k - 1], device_id=_xor_peer(x, y, c, k), device_id_type=MESH).start()
        for p in F2_PIECES:
            pltpu.make_async_remote_copy(
                src_ref=_shard_piece(s_ref, p), dst_ref=_block_rows(arrs, p, me), send_sem=ssem_f.at[0],
                recv_sem=rsem_f0, device_id=sib, device_id_type=MESH).start()
        for j, chip in enumerate(chips):
            for p in F2_PIECES:
                pltpu.make_async_remote_copy(
                    src_ref=_shard_piece(s_ref, p), dst_ref=_block_rows(arrs, p, me), send_sem=ssem_f.at[1 + j],
                    recv_sem=rsem_f.at[j], device_id=(*chip, c), device_id_type=MESH).start()

    dma = pltpu.SemaphoreType.DMA
    return pl.pallas_call(
        body, name="gather_rest_start",
        out_shape=(dma((7,)), dma((7,)), dma((4,)), dma(()), dma((3,)),
                   _hbm_like(shard), _hbm_like(wi), _hbm_like(wo), _hbm_like(w2), _hbm_like(w1)),
        in_specs=(HBM_SPEC,) * 5, out_specs=(SEM_SPEC,) * 5 + (HBM_SPEC,) * 5,
        input_output_aliases={0: 5, 1: 6, 2: 7, 3: 8, 4: 9},
        compiler_params=pltpu.CompilerParams(has_side_effects=SPLIT_EFFECT),
    )(_in_hbm(shard), _in_hbm(wi), _in_hbm(wo), _in_hbm(w2), _in_hbm(w1))


def _gather_mix_wait(ssem_m, rsem_m, shard, wi, wo, after):
    def body(s_ref, wi_ref, wo_ref, ssem, rsem, after_ref, s_o, wi_o, wo_o):
        x, y, c = _position()
        grp = _shard_group(s_ref, MIX_PIECES)
        for k in range(NDEV - 1):
            d = pltpu.make_async_remote_copy(src_ref=grp, dst_ref=grp, send_sem=ssem.at[k], recv_sem=rsem.at[k],
                                             device_id=(x, y, c), device_id_type=MESH)
            d.wait_recv()
            d.wait_send()

    return pl.pallas_call(
        body, name="gather_mix_wait", out_shape=(_hbm_like(shard), _hbm_like(wi), _hbm_like(wo)),
        in_specs=(HBM_SPEC, HBM_SPEC, HBM_SPEC, SEM_SPEC, SEM_SPEC, ANY_SPEC), out_specs=(HBM_SPEC,) * 3,
        input_output_aliases={0: 0, 1: 1, 2: 2},
        compiler_params=pltpu.CompilerParams(has_side_effects=SPLIT_EFFECT),
    )(shard, wi, wo, ssem_m, rsem_m, after)


def _gather_ffn2_pass_on(rsem_f, w2, wo, after):
    def body(w2_ref, wo_ref, rsem, after_ref, fsend, frecv, w2_o, wo_o):
        x, y, c = _position()
        sib = (x, y, 1 - c)
        chips = [(1 - x, y), (x, 1 - y), (1 - x, 1 - y)]
        arrs = _weight_pieces(w2_ref=w2_ref)
        three = w2_ref.at[0, pl.ds(0, _group_rows(F2_PIECES)), :]
        for j, chip in enumerate(chips):
            pltpu.make_async_remote_copy(src_ref=three, dst_ref=three, send_sem=fsend.at[j], recv_sem=rsem.at[j],
                                         device_id=(x, y, c), device_id_type=MESH).wait_recv()
            for p in F2_PIECES:
                rows = _block_rows(arrs, p, (*chip, c))
                pltpu.make_async_remote_copy(src_ref=rows, dst_ref=rows, send_sem=fsend.at[j], recv_sem=frecv.at[j],
                                             device_id=sib, device_id_type=MESH).start()

    dma = pltpu.SemaphoreType.DMA
    return pl.pallas_call(
        body, name="gather_ffn2_pass_on", out_shape=(dma((3,)), dma((3,)), _hbm_like(w2), _hbm_like(wo)),
        in_specs=(HBM_SPEC, HBM_SPEC, SEM_SPEC, ANY_SPEC), out_specs=(SEM_SPEC, SEM_SPEC, HBM_SPEC, HBM_SPEC),
        input_output_aliases={0: 2, 1: 3},
        compiler_params=pltpu.CompilerParams(has_side_effects=SPLIT_EFFECT),
    )(w2, wo, rsem_f, after)


def _gather_ffn2_wait(ssem_f, rsem_f0, fsend, frecv, shard, w2, after):
    def body(s_ref, w2_ref, ssem, rsem0, fs, fr, after_ref, w2_o):
        x, y, c = _position()
        grp = _shard_group(s_ref, F2_PIECES)

        def waiter(send_sem, recv_sem):
            return pltpu.make_async_remote_copy(src_ref=grp, dst_ref=grp, send_sem=send_sem, recv_sem=recv_sem,
                                                device_id=(x, y, c), device_id_type=MESH)

        waiter(ssem.at[0], rsem0).wait_recv()
        for j in range(3):
            waiter(fs.at[j], fr.at[j]).wait_recv()
        for rel in range(4):
            waiter(ssem.at[rel], rsem0).wait_send()
        for j in range(3):
            waiter(fs.at[j], fr.at[j]).wait_send()

    return pl.pallas_call(
        body, name="gather_ffn2_wait", out_shape=_hbm_like(w2),
        in_specs=(HBM_SPEC, HBM_SPEC, SEM_SPEC, SEM_SPEC, SEM_SPEC, SEM_SPEC, ANY_SPEC), out_specs=HBM_SPEC,
        input_output_aliases={1: 0},
        compiler_params=pltpu.CompilerParams(has_side_effects=SPLIT_EFFECT),
    )(shard, w2, ssem_f, rsem_f0, fsend, frecv, after)


class _GatheredWeights(_LocalWeights):
    def __init__(self, shard):
        w1 = _all_gather_ffn1(shard)
        wi, wo, w2 = _place_own_rows(shard)
        (self.ssem_m, self.rsem_m, self.ssem_f, self.rsem_f0, self.rsem_f,
         self.shard, self.wi, self.wo, self.w2_part, self.w1) = _gather_rest_start(shard, wi, wo, w2, w1)

    def mix(self, after):
        self.shard, wint, wout = _gather_mix_wait(self.ssem_m, self.rsem_m, self.shard, self.wi, self.wo, after)
        return wint, wout

    def before_out_proj(self, wout, after):
        self.fsend, self.frecv, self.w2_part, wout = _gather_ffn2_pass_on(self.rsem_f, self.w2_part, wout, after)
        return wout

    def ffn2(self, after):
        return _gather_ffn2_wait(self.ssem_f, self.rsem_f0, self.fsend, self.frecv, self.shard, self.w2_part, after)


RS_CHUNK = 240


def _reduce_scatter_grads(dw1, dwint, dwout, dw2):
    nchunk = PACK_ROWS // RS_CHUNK

    def body(d1_ref, di_ref, do_ref, d2_ref, red_ref, rx1_ref, rx2_ref,
             own_buf, rx_buf, tx_buf, acc, sa, ra, sb, rb, lsem):
        x, y, c = _position()
        me, sib = (x, y, c), (x, y, 1 - c)
        rel_chips = [(x, y), (1 - x, y), (x, 1 - y), (1 - x, 1 - y)]
        srcs = (d1_ref.at[0], d1_ref.at[1], d1_ref.at[2], di_ref, do_ref, d2_ref.at[0], d2_ref.at[1], d2_ref.at[2])

        def piece(k, dev):
            r = PIECE_ROWS[k]
            return srcs[k].at[pl.ds(pl.multiple_of(_dev_index(*dev) * r, 16), r), :]

        def packed(ref, k):
            return ref.at[pl.ds(PIECE_OFF[k], PIECE_ROWS[k]), :]

        for j, chip in enumerate(rel_chips):
            for k in range(len(PIECE_ROWS)):
                pltpu.make_async_remote_copy(
                    src_ref=piece(k, (*chip, 1 - c)), dst_ref=packed(rx1_ref.at[j], k),
                    send_sem=sa.at[j], recv_sem=ra.at[j], device_id=sib, device_id_type=MESH).start()

        def wait_a(j):
            return pltpu.make_async_remote_copy(src_ref=rx1_ref.at[j], dst_ref=rx1_ref.at[j], send_sem=sa.at[j],
                                                recv_sem=ra.at[j], device_id=me, device_id_type=MESH)

        def ici(j):
            return pltpu.make_async_remote_copy(
                src_ref=tx_buf.at[j - 1], dst_ref=rx2_ref.at[j - 1], send_sem=sb.at[j - 1], recv_sem=rb.at[j - 1],
                device_id=(*rel_chips[j], c), device_id_type=MESH)

        for j in (1, 2, 3, 0):
            loads = [pltpu.make_async_copy(piece(k, (*rel_chips[j], c)), packed(own_buf, k), lsem)
                     for k in range(len(PIECE_ROWS))]
            for cp in loads:
                cp.start()
            wait_a(j).wait_recv()
            got = pltpu.make_async_copy(rx1_ref.at[j], rx_buf, lsem)
            got.start()
            pltpu.make_async_copy(rx_buf, rx_buf, lsem).wait()
            got.wait()

            def add(i, carry, j=j):
                rows = pl.ds(pl.multiple_of(i * RS_CHUNK, 16), RS_CHUNK)
                tot = own_buf[rows, :].astype(f32) + rx_buf[rows, :].astype(f32)
                if j == 0:
                    acc[rows, :] = tot
                else:
                    tx_buf[j - 1, rows, :] = tot.astype(bf16)
                return carry

            lax.fori_loop(0, nchunk, add, 0)
            if j != 0:
                ici(j).start()

        for j in (1, 2, 3):
            ici(j).wait_recv()
            got = pltpu.make_async_copy(rx2_ref.at[j - 1], rx_buf, lsem)
            got.start()
            got.wait()

            def add2(i, carry):
                rows = pl.ds(pl.multiple_of(i * RS_CHUNK, 16), RS_CHUNK)
                acc[rows, :] += rx_buf[rows, :].astype(f32)
                return carry

            lax.fori_loop(0, nchunk, add2, 0)
        out = pltpu.make_async_copy(acc, red_ref, lsem)
        out.start()
        out.wait()
        for j in range(4):
            wait_a(j).wait_send()
        for j in (1, 2, 3):
            ici(j).wait_send()

    hbm = pl.BlockSpec(memory_space=pl.ANY)
    red, _, _ = pl.pallas_call(
        body, in_specs=[hbm] * 4, out_specs=[hbm] * 3,
        out_shape=(SDS((PACK_ROWS, D), f32), SDS((4, PACK_ROWS, D), bf16), SDS((3, PACK_ROWS, D), bf16)),
        scratch_shapes=[pltpu.VMEM((PACK_ROWS, D), bf16), pltpu.VMEM((PACK_ROWS, D), bf16),
                        pltpu.VMEM((3, PACK_ROWS, D), bf16), pltpu.VMEM((PACK_ROWS, D), f32),
                        pltpu.SemaphoreType.DMA((4,)), pltpu.SemaphoreType.DMA((4,)),
                        pltpu.SemaphoreType.DMA((3,)), pltpu.SemaphoreType.DMA((3,)), pltpu.SemaphoreType.DMA],
        compiler_params=pltpu.CompilerParams(has_side_effects=True, vmem_limit_bytes=VMEM_LIMIT_V7X),
        name="reduce_scatter_grads")(dw1, dwint, dwout, dw2)
    return red


def _all_reduce_small(packed):
    def body(p_ref, o_ref, pair, chips, send_sems, recv_sems):
        x, y, c = _position()
        chip = 2 * x + y
        pair[c] = p_ref[...]
        swap = pltpu.make_async_remote_copy(
            src_ref=p_ref, dst_ref=pair.at[c], send_sem=send_sems.at[0], recv_sem=recv_sems.at[0],
            device_id=(x, y, 1 - c), device_id_type=MESH)
        swap.start()
        swap.wait_recv()
        chips[chip] = pair[0] + pair[1]
        cps = [pltpu.make_async_remote_copy(
            src_ref=chips.at[chip], dst_ref=chips.at[chip], send_sem=send_sems.at[1 + j], recv_sem=recv_sems.at[1 + j],
            device_id=(*other, c), device_id_type=MESH)
            for j, other in enumerate([(1 - x, y), (x, 1 - y), (1 - x, 1 - y)])]
        for cp in cps:
            cp.start()
        for cp in cps:
            cp.wait_recv()
        tot = (chips[0] + chips[1]) + (chips[2] + chips[3])
        o_ref[...] = tot
        loss = jnp.sum(tot[LOSS_ROW:LOSS_ROW + 1, :], axis=-1, keepdims=True)
        o_ref[LOSS_ROW:LOSS_ROW + 1, :] = jnp.broadcast_to(loss, (1, 128))
        swap.wait_send()
        for cp in cps:
            cp.wait_send()

    vm = pl.BlockSpec(memory_space=pltpu.VMEM)
    return pl.pallas_call(
        body, in_specs=[vm], out_specs=vm, out_shape=SDS((SMALL_ROWS, 128), f32),
        scratch_shapes=[pltpu.VMEM((2, SMALL_ROWS, 128), f32), pltpu.VMEM((4, SMALL_ROWS, 128), f32),
                        pltpu.SemaphoreType.DMA((4,)), pltpu.SemaphoreType.DMA((4,))],
        compiler_params=pltpu.CompilerParams(has_side_effects=True),
        name="all_reduce_small")(packed)


def _adamw_math(w, g, m, v):
    m = ADAM_B1 * m + (1.0 - ADAM_B1) * g
    v = ADAM_B2 * v + (1.0 - ADAM_B2) * (g * g)
    m_hat = m / (1.0 - ADAM_B1 ** ADAM_STEP)
    v_hat = v / (1.0 - ADAM_B2 ** ADAM_STEP)
    delta = -ADAM_LR * (m_hat / (jnp.sqrt(v_hat) + ADAM_EPS) + ADAM_WD * w)
    return delta, m, v


def _adamw_big(ws, ms, vs, red):
    npiece = len(BIG)
    rmax = max(PIECE_ROWS)

    def body(*refs):
        ins = (refs[0:npiece], refs[npiece:2 * npiece], refs[2 * npiece:3 * npiece])
        red_ref = refs[3 * npiece]
        out_refs = refs[3 * npiece + 1:7 * npiece + 1]
        inb, outb, in_sems, out_sems = refs[7 * npiece + 1:]

        def loads(k):
            s, r = k % 2, PIECE_ROWS[k]
            cps = [pltpu.make_async_copy(ins[q][k].at[0], inb.at[s, q, pl.ds(0, r), :], in_sems.at[4 * s + q])
                   for q in range(3)]
            cps.append(pltpu.make_async_copy(red_ref.at[pl.ds(PIECE_OFF[k], r), :], inb.at[s, 3, pl.ds(0, r), :],
                                             in_sems.at[4 * s + 3]))
            return cps

        def stores(k):
            s, r = k % 2, PIECE_ROWS[k]
            return [pltpu.make_async_copy(outb.at[s, q, pl.ds(0, r), :], out_refs[q * npiece + k].at[0],
                                          out_sems.at[4 * s + q]) for q in range(4)]

        for cp in loads(0):
            cp.start()
        for k in range(npiece):
            s, r = k % 2, PIECE_ROWS[k]
            if k + 1 < npiece:
                for cp in loads(k + 1):
                    cp.start()
            for cp in loads(k):
                cp.wait()
            if k >= 2:
                for cp in stores(k - 2):
                    cp.wait()
            g = inb[s, 3, 0:r, :]
            d, nm, nv = _adamw_math(inb[s, 0, 0:r, :], g, inb[s, 1, 0:r, :], inb[s, 2, 0:r, :])
            outb[s, 0, 0:r, :] = g
            outb[s, 1, 0:r, :] = d
            outb[s, 2, 0:r, :] = nm
            outb[s, 3, 0:r, :] = nv
            for cp in stores(k):
                cp.start()
        for k in (npiece - 2, npiece - 1):
            for cp in stores(k):
                cp.wait()

    hbm = pl.BlockSpec(memory_space=pl.ANY)
    outs = pl.pallas_call(
        body, in_specs=[hbm] * (3 * npiece + 1), out_specs=[hbm] * (4 * npiece),
        out_shape=tuple(SDS(w.shape, f32) for _ in range(4) for w in ws),
        scratch_shapes=[pltpu.VMEM((2, 4, rmax, D), f32), pltpu.VMEM((2, 4, rmax, D), f32),
                        pltpu.SemaphoreType.DMA((8,)), pltpu.SemaphoreType.DMA((8,))],
        compiler_params=_cparams(None, VMEM_LIMIT_V7X), name="adamw_big")(*ws, *ms, *vs, red)
    return [list(outs[q * npiece:(q + 1) * npiece]) for q in range(4)]


def _adamw_small(w, m, v, g, name):
    def body(w_ref, m_ref, v_ref, g_ref, d_ref, nm_ref, nv_ref):
        d, nm, nv = _adamw_math(w_ref[...], g_ref[...], m_ref[...], v_ref[...])
        d_ref[...] = d
        nm_ref[...] = nm
        nv_ref[...] = nv

    return pl.pallas_call(
        body, out_shape=tuple(SDS(w.shape, f32) for _ in range(3)), name=name)(w, m, v, g)


WEIGHTS = ("ffn1_norm", "ffn1_w_gate", "ffn1_w_up", "ffn1_w_down", "mix_norm", "w_in", "q_norm", "k_norm",
           "attn_sinks", "rel_bias", "pool_w", "pool_scale", "w_out", "ffn2_norm", "ffn2_w_gate", "ffn2_w_up",
           "ffn2_w_down")
BIG = (("ffn1_w_gate", True), ("ffn1_w_up", True), ("ffn1_w_down", False), ("w_in", True), ("w_out", False),
       ("ffn2_w_gate", True), ("ffn2_w_up", True), ("ffn2_w_down", False))


def kernel(x, ffn1_norm, ffn1_w_gate, ffn1_w_up, ffn1_w_down, mix_norm, w_in, q_norm, k_norm, attn_sinks, rel_bias, pool_w, pool_scale, w_out, ffn2_norm, ffn2_w_gate, ffn2_w_up, ffn2_w_down, loss_target, m_ffn1_norm, m_ffn1_w_gate, m_ffn1_w_up, m_ffn1_w_down, m_mix_norm, m_w_in, m_q_norm, m_k_norm, m_attn_sinks, m_rel_bias, m_pool_w, m_pool_scale, m_w_out, m_ffn2_norm, m_ffn2_w_gate, m_ffn2_w_up, m_ffn2_w_down, v_ffn1_norm, v_ffn1_w_gate, v_ffn1_w_up, v_ffn1_w_down, v_mix_norm, v_w_in, v_q_norm, v_k_norm, v_attn_sinks, v_rel_bias, v_pool_w, v_pool_scale, v_w_out, v_ffn2_norm, v_ffn2_w_gate, v_ffn2_w_up, v_ffn2_w_down):
    args = dict(locals())
    w = {n: args[n] for n in WEIGHTS}
    m = {n: args["m_" + n] for n in WEIGHTS}
    v = {n: args["v_" + n] for n in WEIGHTS}

    as_rows = lambda a, tr: jnp.swapaxes(a, 1, 2) if tr else a
    shard = jnp.concatenate([as_rows(w[n], tr)[0].astype(bf16) for n, tr in BIG], axis=0)
    gx, (dw1, dwint, dwout, dw2), small = _local_step(
        x[0], loss_target[0], _GatheredWeights(shard), ffn1_norm, mix_norm, ffn2_norm, q_norm, k_norm, attn_sinks,
        rel_bias, pool_w[0], pool_scale)

    red = _reduce_scatter_grads(dw1, dwint, dwout, dw2)
    small_tot = _all_reduce_small(_pack_small(small))

    grads, deltas, new_m, new_v = {}, {}, {}, {}
    big_out = _adamw_big(*[[as_rows(t[n], tr) for n, tr in BIG] for t in (w, m, v)], red)
    for k, (n, tr) in enumerate(BIG):
        grads[n], deltas[n], new_m[n], new_v[n] = [as_rows(o[k], tr) for o in big_out]
    small_names = [n for n in SMALL_NAMES if n != "loss"]
    ds, nms, nvs = _adamw_small(_pack_small({n: w[n] for n in small_names}), _pack_small({n: m[n] for n in small_names}),
                                _pack_small({n: v[n] for n in small_names}), small_tot, "adamw_small")
    for n in small_names:
        grads[n] = _unpack_small(small_tot, n)
        deltas[n], new_m[n], new_v[n] = _unpack_small(ds, n), _unpack_small(nms, n), _unpack_small(nvs, n)
    loss = small_tot[LOSS_ROW, 0]
    return (loss, gx[None], *[grads[n] for n in WEIGHTS], *[deltas[n] for n in WEIGHTS],
            *[new_m[n] for n in WEIGHTS], *[new_v[n] for n in WEIGHTS])
```

```python
import functools

import jax
import jax.numpy as jnp
import numpy as np
from jax import lax
from jax.experimental import pallas as pl
from jax.experimental.pallas import tpu as pltpu

f32, bf16, i32 = jnp.float32, jnp.bfloat16, jnp.int32
SDS = jax.ShapeDtypeStruct

D = 1024
F = 2816
HD = 64
NH = 8
NKV = 2
GQA = NH // NKV
DATTN = NH * HD
DKV = NKV * HD
DPOOL = 512
POOL_WINDOWS = (2, 4, 8, 16)
PGD = DPOOL // len(POOL_WINDOWS)
DIN = DATTN + 2 * DKV + DPOOL
DMIX = DATTN + DPOOL
BLK = 128
NBUCK = 32
MAX_DISTANCE = 128
EPS = 1e-6
NEG = -1e30
SCALE = HD ** -0.5

ADAM_LR, ADAM_B1, ADAM_B2, ADAM_EPS, ADAM_WD, ADAM_STEP = 0.001, 0.9, 0.999, 1e-08, 0.01, 10

NDEV = 8
FS = F // NDEV
INS = DIN // NDEV
OUTS = DMIX // NDEV
PIECE_ROWS = (FS, FS, FS, INS, OUTS, FS, FS, FS)
PIECE_OFF = tuple(int(v) for v in np.cumsum((0,) + PIECE_ROWS[:-1]))
PACK_ROWS = sum(PIECE_ROWS)

VMEM_LIMIT_V7X = 56 * 1024 * 1024

MESH = pl.DeviceIdType.MESH


def _cparams(sem=None, vmem=None):
    return pltpu.CompilerParams(dimension_semantics=sem, vmem_limit_bytes=vmem)


def _nt(a, b):
    return lax.dot_general(a, b, (((1,), (1,)), ((), ())), preferred_element_type=f32)


def _tn(a, b):
    return lax.dot_general(a, b, (((0,), (0,)), ((), ())), preferred_element_type=f32)


def _nn(a, b):
    return jnp.dot(a, b, preferred_element_type=f32)


def _sigmoid(x):
    return 1.0 / (1.0 + jnp.exp(-x))


def _norm_fwd(x, g, name):
    T = x.shape[0]
    tm = min(512, T)

    def body(x_ref, g_ref, h_ref):
        xv = x_ref[...]
        r = lax.rsqrt(jnp.mean(xv * xv, axis=-1, keepdims=True) + EPS)
        h_ref[...] = (xv * r * g_ref[...]).astype(bf16)

    return pl.pallas_call(
        body, grid=(T // tm,),
        in_specs=[pl.BlockSpec((tm, D), lambda i: (i, 0)), pl.BlockSpec((1, D), lambda i: (0, 0))],
        out_specs=pl.BlockSpec((tm, D), lambda i: (i, 0)),
        out_shape=SDS((T, D), bf16), name=name)(x, g)


def _norm_bwd(dh, x, g, dres, out_scale, name):
    T = x.shape[0]
    tm = min(512, T)

    def body(dh_ref, x_ref, g_ref, dr_ref, dx_ref, dxb_ref, dg_ref):
        i = pl.program_id(0)
        xv = x_ref[...]
        r = lax.rsqrt(jnp.mean(xv * xv, axis=-1, keepdims=True) + EPS)
        xh = xv * r
        dhv = dh_ref[...]
        dxh = dhv * g_ref[...]
        dx = dr_ref[...] + r * (dxh - xh * jnp.mean(dxh * xh, axis=-1, keepdims=True))
        dx_ref[...] = dx
        dxb_ref[...] = (out_scale * dx).astype(bf16)
        dg = jnp.sum(dhv * xh, axis=0, keepdims=True)

        @pl.when(i == 0)
        def _():
            dg_ref[...] = dg

        @pl.when(i > 0)
        def _():
            dg_ref[...] += dg

    tok = pl.BlockSpec((tm, D), lambda i: (i, 0))
    vec = pl.BlockSpec((1, D), lambda i: (0, 0))
    return pl.pallas_call(
        body, grid=(T // tm,),
        in_specs=[tok, tok, vec, tok], out_specs=[tok, tok, vec],
        out_shape=(SDS((T, D), f32), SDS((T, D), bf16), SDS((1, D), f32)),
        compiler_params=_cparams(("arbitrary",)), name=name)(dh, x, g, dres)


def _ffn_tiles(T):
    return min(1024, T), 256


def _ffn_fwd(h, w, x, name):
    T = h.shape[0]
    tm, tf = _ffn_tiles(T)
    nf = F // tf

    def body(h_ref, w_ref, x_ref, xo_ref, g_ref, u_ref, acc):
        fi = pl.program_id(1)
        hv = h_ref[...]
        gate = _nt(hv, w_ref[0])
        up = _nt(hv, w_ref[1])
        act = gate * _sigmoid(gate) * up
        g_ref[...] = gate.astype(bf16)
        u_ref[...] = up.astype(bf16)
        part = _nn(act.astype(bf16), w_ref[2])

        @pl.when(fi == 0)
        def _():
            acc[...] = part

        @pl.when(fi > 0)
        def _():
            acc[...] += part

        @pl.when(fi == nf - 1)
        def _():
            xo_ref[...] = x_ref[...] + 0.5 * acc[...]

    tok = pl.BlockSpec((tm, D), lambda i, f: (i, 0))
    act_spec = pl.BlockSpec((tm, tf), lambda i, f: (i, f))
    return pl.pallas_call(
        body, grid=(T // tm, nf),
        in_specs=[tok, pl.BlockSpec((3, tf, D), lambda i, f: (0, f, 0)), tok],
        out_specs=[tok, act_spec, act_spec],
        out_shape=(SDS((T, D), f32), SDS((T, F), bf16), SDS((T, F), bf16)),
        scratch_shapes=[pltpu.VMEM((tm, D), f32)],
        compiler_params=_cparams(("arbitrary", "arbitrary"), VMEM_LIMIT_V7X), name=name)(h, w, x)


def _ffn_bwd(dob, h, gate, up, w, name):
    T = h.shape[0]
    tm, tf = _ffn_tiles(T)
    nf, nt = F // tf, T // tm

    def body(do_ref, h_ref, g_ref, u_ref, w_ref, dh_ref, dw_ref, dwacc):
        fi, ti = pl.program_id(0), pl.program_id(1)
        dov = do_ref[...]
        gv = g_ref[...].astype(f32)
        uv = u_ref[...].astype(f32)
        sg = _sigmoid(gv)
        sil = gv * sg
        dact = _nt(dov, w_ref[2])
        dup = dact * sil
        dgate = dact * uv * (sg * (1.0 + gv * (1.0 - sg)))
        dgu = jnp.concatenate([dgate.astype(bf16), dup.astype(bf16)], axis=1)
        actb = (sil * uv).astype(bf16)
        wgu = w_ref[0:2].reshape(2 * tf, D)
        dh_part = _nn(dgu, wgu)
        rows = pl.ds(pl.multiple_of(ti * tm, tm), tm)

        @pl.when(fi == 0)
        def _():
            dh_ref[rows, :] = dh_part

        @pl.when(fi > 0)
        def _():
            dh_ref[rows, :] += dh_part

        dwgu = _tn(dgu, h_ref[...])
        dwd = _tn(actb, dov)

        @pl.when(ti == 0)
        def _():
            dwacc[0:2 * tf, :] = dwgu
            dwacc[2 * tf:3 * tf, :] = dwd

        @pl.when(ti > 0)
        def _():
            dwacc[0:2 * tf, :] += dwgu
            dwacc[2 * tf:3 * tf, :] += dwd

        @pl.when(ti == nt - 1)
        def _():
            dw_ref[...] = dwacc[...].reshape(3, tf, D).astype(bf16)

    tok = pl.BlockSpec((tm, D), lambda f, i: (i, 0))
    act_spec = pl.BlockSpec((tm, tf), lambda f, i: (i, f))
    wspec = pl.BlockSpec((3, tf, D), lambda f, i: (0, f, 0))
    return pl.pallas_call(
        body, grid=(nf, nt),
        in_specs=[tok, tok, act_spec, act_spec, wspec],
        out_specs=[pl.BlockSpec((T, D), lambda f, i: (0, 0)), wspec],
        out_shape=(SDS((T, D), f32), SDS((3, F, D), bf16)),
        scratch_shapes=[pltpu.VMEM((3 * tf, D), f32)],
        compiler_params=_cparams(("arbitrary", "arbitrary"), VMEM_LIMIT_V7X), name=name)(dob, h, gate, up, w)


def _loss_grad(y, target, name):
    T = y.shape[0]
    tm = min(512, T)

    def body(y_ref, t_ref, dy_ref, dyb_ref, l_ref):
        i = pl.program_id(0)
        e = y_ref[...] - t_ref[...]
        dy = e * (1.0 / D)
        dy_ref[...] = dy
        dyb_ref[...] = (0.5 * dy).astype(bf16)
        col = jnp.sum(e * e, axis=0, keepdims=True) * (0.5 / D)
        lanes = col[:, 0:128]
        for k in range(1, D // 128):
            lanes = lanes + col[:, 128 * k:128 * (k + 1)]

        @pl.when(i == 0)
        def _():
            l_ref[...] = lanes

        @pl.when(i > 0)
        def _():
            l_ref[...] += lanes

    tok = pl.BlockSpec((tm, D), lambda i: (i, 0))
    return pl.pallas_call(
        body, grid=(T // tm,), in_specs=[tok, tok],
        out_specs=[tok, tok, pl.BlockSpec((1, 128), lambda i: (0, 0))],
        out_shape=(SDS((T, D), f32), SDS((T, D), bf16), SDS((1, 128), f32)),
        compiler_params=_cparams(("arbitrary",)), name=name)(y, target)


def _in_proj_fwd(h, wint, name):
    T = h.shape[0]
    tm = min(512, T)

    def body(h_ref, w_ref, z_ref):
        z_ref[...] = _nt(h_ref[...], w_ref[...])

    return pl.pallas_call(
        body, grid=(T // tm,),
        in_specs=[pl.BlockSpec((tm, D), lambda i: (i, 0)), pl.BlockSpec((DIN, D), lambda i: (0, 0))],
        out_specs=pl.BlockSpec((tm, DIN), lambda i: (i, 0)),
        out_shape=SDS((T, DIN), f32), name=name)(h, wint)


def _in_proj_bwd(dz, wint, h, name):
    T = h.shape[0]
    tm = min(512, T)
    nt = T // tm

    def body(dz_ref, w_ref, h_ref, dh_ref, dw_ref, acc):
        i = pl.program_id(0)
        dzb = dz_ref[...].astype(bf16)
        dh_ref[...] = _nn(dzb, w_ref[...])
        part = _tn(dzb, h_ref[...])

        @pl.when(i == 0)
        def _():
            acc[...] = part

        @pl.when(i > 0)
        def _():
            acc[...] += part

        @pl.when(i == nt - 1)
        def _():
            dw_ref[...] = acc[...].astype(bf16)

    wspec = pl.BlockSpec((DIN, D), lambda i: (0, 0))
    return pl.pallas_call(
        body, grid=(nt,),
        in_specs=[pl.BlockSpec((tm, DIN), lambda i: (i, 0)), wspec, pl.BlockSpec((tm, D), lambda i: (i, 0))],
        out_specs=[pl.BlockSpec((tm, D), lambda i: (i, 0)), wspec],
        out_shape=(SDS((T, D), f32), SDS((DIN, D), bf16)),
        scratch_shapes=[pltpu.VMEM((DIN, D), f32)],
        compiler_params=_cparams(("arbitrary",)), name=name)(dz, wint, h)


def _out_proj_fwd(ymix, wout, x, name):
    T = x.shape[0]
    tm = min(512, T)

    def body(y_ref, w_ref, x_ref, o_ref):
        o_ref[...] = x_ref[...] + _nn(y_ref[...], w_ref[...])

    tok = pl.BlockSpec((tm, D), lambda i: (i, 0))
    return pl.pallas_call(
        body, grid=(T // tm,),
        in_specs=[pl.BlockSpec((tm, DMIX), lambda i: (i, 0)), pl.BlockSpec((DMIX, D), lambda i: (0, 0)), tok],
        out_specs=tok, out_shape=SDS((T, D), f32), name=name)(ymix, wout, x)


def _out_proj_bwd(dxb, wout, ymix, name):
    T = dxb.shape[0]
    tm = min(512, T)
    nt = T // tm

    def body(dx_ref, w_ref, y_ref, dy_ref, dw_ref, acc):
        i = pl.program_id(0)
        dxv = dx_ref[...]
        dy_ref[...] = _nt(dxv, w_ref[...])
        part = _tn(y_ref[...], dxv)

        @pl.when(i == 0)
        def _():
            acc[...] = part

        @pl.when(i > 0)
        def _():
            acc[...] += part

        @pl.when(i == nt - 1)
        def _():
            dw_ref[...] = acc[...].astype(bf16)

    wspec = pl.BlockSpec((DMIX, D), lambda i: (0, 0))
    return pl.pallas_call(
        body, grid=(nt,),
        in_specs=[pl.BlockSpec((tm, D), lambda i: (i, 0)), wspec, pl.BlockSpec((tm, DMIX), lambda i: (i, 0))],
        out_specs=[pl.BlockSpec((tm, DMIX), lambda i: (i, 0)), wspec],
        out_shape=(SDS((T, DMIX), f32), SDS((DMIX, D), bf16)),
        scratch_shapes=[pltpu.VMEM((DMIX, D), f32)],
        compiler_params=_cparams(("arbitrary",)), name=name)(dxb, wout, ymix)


def _t5_bucket_table():
    ql = np.arange(BLK)[:, None]
    kl = np.arange(2 * BLK)[None, :]
    n = np.maximum(ql + BLK - kl, 0)
    max_exact = NBUCK // 2
    large = max_exact + (np.log(np.maximum(n, 1) / max_exact) / np.log(MAX_DISTANCE / max_exact)
                         * (NBUCK - max_exact)).astype(np.int32)
    large = np.minimum(large, NBUCK - 1)
    return np.where(n < max_exact, n, large).astype(np.int32)


def _fill_bias(bk_ref, rb_ref, bias_scr):
    bk = bk_ref[...]
    for h in range(NH):
        def step(b, acc, h=h):
            return acc + jnp.where(bk == b, rb_ref[b, h], 0.0)
        bias_scr[h] = lax.fori_loop(0, NBUCK, step, jnp.zeros((BLK, 2 * BLK), f32))


def _attn_probs(zc_ref, zp_ref, kh, qg, kg, sk_ref, bias_scr, n):
    kc = DATTN + HD * kh
    vc = DATTN + DKV + HD * kh
    kx = jnp.concatenate([zp_ref[:, kc:kc + HD], zc_ref[:, kc:kc + HD]], axis=0)
    vx = jnp.concatenate([zp_ref[:, vc:vc + HD], zc_ref[:, vc:vc + HD]], axis=0)
    qx = jnp.concatenate([zc_ref[:, HD * (GQA * kh + g):HD * (GQA * kh + g + 1)] for g in range(GQA)], axis=0)
    rq = lax.rsqrt(jnp.mean(qx * qx, axis=-1, keepdims=True) + EPS)
    rk = lax.rsqrt(jnp.mean(kx * kx, axis=-1, keepdims=True) + EPS)
    qhat, khat = qx * rq, kx * rk
    qnb, knb = (qhat * qg).astype(bf16), (khat * kg).astype(bf16)
    s = _nt(qnb, knb) * SCALE + bias_scr[GQA * kh:GQA * (kh + 1)].reshape(GQA * BLK, 2 * BLK)
    row = lax.broadcasted_iota(i32, (GQA * BLK, 2 * BLK), 0) & (BLK - 1)
    col = lax.broadcasted_iota(i32, (GQA * BLK, 2 * BLK), 1)
    mask = (col > row) & (col <= row + BLK) & ((col >= BLK) | (n > 0))
    s = jnp.where(mask, s, NEG)
    ridx = lax.broadcasted_iota(i32, (GQA * BLK, 1), 0)
    sink = jnp.full((GQA * BLK, 1), sk_ref[GQA * kh + GQA - 1], f32)
    for g in range(GQA - 2, -1, -1):
        sink = jnp.where(ridx < (g + 1) * BLK, sk_ref[GQA * kh + g], sink)
    m = jnp.maximum(jnp.max(s, axis=-1, keepdims=True), sink)
    e = jnp.exp(s - m)
    es = jnp.exp(sink - m)
    den = jnp.sum(e, axis=-1, keepdims=True) + es
    return dict(p=e / den, psink=es / den, qhat=qhat, khat=khat, rq=rq, rk=rk, qnb=qnb, knb=knb, vb=vx.astype(bf16))


def _pool_group(zc_ref, zp_ref, g, w, n):
    c0 = DATTN + 2 * DKV + PGD * g
    uc = zc_ref[:, c0:c0 + PGD]
    up = jnp.where(n > 0, zp_ref[:, c0:c0 + PGD], 0.0)
    ue = jnp.concatenate([up, uc], axis=0)
    hi = ue.astype(bf16)
    lo = (ue - hi.astype(f32)).astype(bf16)
    t = lax.broadcasted_iota(i32, (BLK, 2 * BLK), 0)
    s = lax.broadcasted_iota(i32, (BLK, 2 * BLK), 1)
    band = jnp.where((s <= t + BLK) & (s > t + BLK - w), 1.0, 0.0).astype(bf16)
    sm = _nn(band, hi) + _nn(band, lo)
    pos = n * BLK + lax.broadcasted_iota(i32, (BLK, 1), 0) + 1
    cnt = jnp.minimum(pos, w).astype(f32)
    return sm / cnt - uc, band, cnt


def _mix_fwd(z, qg, kg, sinks, relb, bucket, pool_w, pscale, name):
    T = z.shape[0]
    nb = T // BLK

    def body(zc_ref, zp_ref, qg_ref, kg_ref, sk_ref, rb_ref, bk_ref, pw_ref, ps_ref, y_ref, bias_scr, yacc):
        n = pl.program_id(0)

        @pl.when(n == 0)
        def _():
            _fill_bias(bk_ref, rb_ref, bias_scr)

        for kh in range(NKV):
            a = _attn_probs(zc_ref, zp_ref, kh, qg_ref[...], kg_ref[...], sk_ref, bias_scr, n)
            o = _nn(a["p"].astype(bf16), a["vb"])
            for g in range(GQA):
                hc = HD * (GQA * kh + g)
                yacc[:, hc:hc + HD] = o[g * BLK:(g + 1) * BLK]
        for g, w in enumerate(POOL_WINDOWS):
            pooled, _, _ = _pool_group(zc_ref, zp_ref, g, w, n)
            yp = _nn(pooled.astype(bf16), pw_ref[g].astype(bf16)) * ps_ref[:, PGD * g:PGD * (g + 1)]
            yacc[:, DATTN + PGD * g:DATTN + PGD * (g + 1)] = yp
        y_ref[...] = yacc[...].astype(bf16)

    full = lambda *shape: pl.BlockSpec(shape, lambda n: (0,) * len(shape))
    smem = pl.BlockSpec(memory_space=pltpu.SMEM)
    return pl.pallas_call(
        body, grid=(nb,),
        in_specs=[pl.BlockSpec((BLK, DIN), lambda n: (n, 0)),
                  pl.BlockSpec((BLK, DIN), lambda n: (jnp.maximum(n - 1, 0), 0)),
                  full(1, HD), full(1, HD), smem, smem, full(BLK, 2 * BLK),
                  full(len(POOL_WINDOWS), PGD, PGD), full(1, DPOOL)],
        out_specs=pl.BlockSpec((BLK, DMIX), lambda n: (n, 0)),
        out_shape=SDS((T, DMIX), bf16),
        scratch_shapes=[pltpu.VMEM((NH, BLK, 2 * BLK), f32), pltpu.VMEM((BLK, DMIX), f32)],
        compiler_params=_cparams(("arbitrary",)), name=name)(z, z, qg, kg, sinks, relb, bucket, pool_w, pscale)


def _mix_bwd(z, dy, qg, kg, sinks, relb, bucket, pool_w, pscale, name):
    T = z.shape[0]
    nb = T // BLK

    def body(zc_ref, zp_ref, dy_ref, qg_ref, kg_ref, sk_ref, rb_ref, bk_ref, pw_ref, ps_ref,
             dz_ref, dqg_ref, dkg_ref, dsk_ref, drb_ref, dpw_ref, dps_ref, bias_scr, dbias_scr):
        n = pl.program_id(0)
        rows = pl.ds(pl.multiple_of(n * BLK, BLK), BLK)
        prow = pl.ds(pl.multiple_of(jnp.maximum(n - 1, 0) * BLK, BLK), BLK)

        @pl.when(n == 0)
        def _():
            _fill_bias(bk_ref, rb_ref, bias_scr)
            dbias_scr[...] = jnp.zeros_like(dbias_scr)
            dqg_ref[...] = jnp.zeros_like(dqg_ref)
            dkg_ref[...] = jnp.zeros_like(dkg_ref)
            dsk_ref[...] = jnp.zeros_like(dsk_ref)
            dpw_ref[...] = jnp.zeros_like(dpw_ref)
            dps_ref[...] = jnp.zeros_like(dps_ref)

        qg, kg = qg_ref[...], kg_ref[...]
        lane = lax.broadcasted_iota(i32, (1, 128), 1)
        dsk = jnp.zeros((1, 128), f32)
        for kh in range(NKV):
            a = _attn_probs(zc_ref, zp_ref, kh, qg, kg, sk_ref, bias_scr, n)
            p = a["p"]
            do = jnp.concatenate([dy_ref[:, HD * (GQA * kh + g):HD * (GQA * kh + g + 1)] for g in range(GQA)],
                                 axis=0).astype(bf16)
            dv = _tn(p.astype(bf16), do)
            dp = _nt(do, a["vb"])
            delta = jnp.sum(p * dp, axis=-1, keepdims=True)
            ds = p * (dp - delta)
            sinkterm = a["psink"] * delta
            for g in range(GQA):
                h = GQA * kh + g
                dbias_scr[h] += ds[g * BLK:(g + 1) * BLK]
                tot = jnp.sum(sinkterm[g * BLK:(g + 1) * BLK], axis=0, keepdims=True)
                dsk = dsk - jnp.where(lane == h, tot, 0.0)
            dsb = ds.astype(bf16)
            dqn = _nn(dsb, a["knb"]) * SCALE
            dkn = _tn(dsb, a["qnb"]) * SCALE
            qhat, khat = a["qhat"], a["khat"]
            dqg_ref[...] += jnp.sum(dqn * qhat, axis=0, keepdims=True)
            dkg_ref[...] += jnp.sum(dkn * khat, axis=0, keepdims=True)
            dqh = dqn * qg
            dq = a["rq"] * (dqh - qhat * jnp.mean(dqh * qhat, axis=-1, keepdims=True))
            dkh = dkn * kg
            dk = a["rk"] * (dkh - khat * jnp.mean(dkh * khat, axis=-1, keepdims=True))
            kc = DATTN + HD * kh
            vc = DATTN + DKV + HD * kh
            for g in range(GQA):
                hc = HD * (GQA * kh + g)
                dz_ref[rows, hc:hc + HD] = dq[g * BLK:(g + 1) * BLK]
            dz_ref[rows, kc:kc + HD] = dk[BLK:2 * BLK]
            dz_ref[rows, vc:vc + HD] = dv[BLK:2 * BLK]

            @pl.when(n > 0)
            def _(dk=dk, dv=dv, kc=kc, vc=vc):
                dz_ref[prow, kc:kc + HD] += dk[0:BLK]
                dz_ref[prow, vc:vc + HD] += dv[0:BLK]

        dsk_ref[...] += dsk

        for g, w in enumerate(POOL_WINDOWS):
            c0 = DATTN + 2 * DKV + PGD * g
            pooled, band, cnt = _pool_group(zc_ref, zp_ref, g, w, n)
            pb = pooled.astype(bf16)
            wb = pw_ref[g].astype(bf16)
            dyp = dy_ref[:, DATTN + PGD * g:DATTN + PGD * (g + 1)]
            ypre = _nn(pb, wb)
            dps_ref[:, PGD * g:PGD * (g + 1)] += jnp.sum(dyp * ypre, axis=0, keepdims=True)
            dyg = (dyp * ps_ref[:, PGD * g:PGD * (g + 1)]).astype(bf16)
            dpw_ref[g] += _tn(pb, dyg)
            dpooled = _nt(dyg, wb)
            dsm = dpooled / cnt
            hi = dsm.astype(bf16)
            lo = (dsm - hi.astype(f32)).astype(bf16)
            due = _tn(band, hi) + _tn(band, lo)
            dz_ref[rows, c0:c0 + PGD] = due[BLK:2 * BLK] - dpooled

            @pl.when(n > 0)
            def _(due=due, c0=c0):
                dz_ref[prow, c0:c0 + PGD] += due[0:BLK]

        @pl.when(n == nb - 1)
        def _():
            bk = bk_ref[...]
            ri = lax.broadcasted_iota(i32, (NBUCK, NH), 0)
            ci = lax.broadcasted_iota(i32, (NBUCK, NH), 1)

            def step(b, acc):
                for h in range(NH):
                    sel = jnp.where(bk == b, dbias_scr[h], 0.0)
                    tot = jnp.sum(jnp.sum(sel, axis=1, keepdims=True), axis=0, keepdims=True)
                    acc = acc + jnp.where((ri == b) & (ci == h), tot, 0.0)
                return acc

            drb_ref[...] = lax.fori_loop(0, NBUCK, step, jnp.zeros((NBUCK, NH), f32))

    full = lambda *shape: pl.BlockSpec(shape, lambda n: (0,) * len(shape))
    smem = pl.BlockSpec(memory_space=pltpu.SMEM)
    npg = len(POOL_WINDOWS)
    return pl.pallas_call(
        body, grid=(nb,),
        in_specs=[pl.BlockSpec((BLK, DIN), lambda n: (n, 0)),
                  pl.BlockSpec((BLK, DIN), lambda n: (jnp.maximum(n - 1, 0), 0)),
                  pl.BlockSpec((BLK, DMIX), lambda n: (n, 0)),
                  full(1, HD), full(1, HD), smem, smem, full(BLK, 2 * BLK), full(npg, PGD, PGD), full(1, DPOOL)],
        out_specs=[full(T, DIN), full(1, HD), full(1, HD), full(1, 128), full(NBUCK, NH),
                   full(npg, PGD, PGD), full(1, DPOOL)],
        out_shape=(SDS((T, DIN), f32), SDS((1, HD), f32), SDS((1, HD), f32), SDS((1, 128), f32),
                   SDS((NBUCK, NH), f32), SDS((npg, PGD, PGD), f32), SDS((1, DPOOL), f32)),
        scratch_shapes=[pltpu.VMEM((NH, BLK, 2 * BLK), f32), pltpu.VMEM((NH, BLK, 2 * BLK), f32)],
        compiler_params=_cparams(("arbitrary",), VMEM_LIMIT_V7X),
        name=name)(z, z, dy, qg, kg, sinks, relb, bucket, pool_w, pscale)


class _LocalWeights:
    def __init__(self, w1, wint, wout, w2):
        self.w1, self.wint, self.wout, self.w2 = w1, wint, wout, w2

    def ffn1(self):
        return self.w1

    def mix(self, after):
        return self.wint, self.wout

    def before_out_proj(self, wout, after):
        return wout

    def ffn2(self, after):
        return self.w2

    def mix_ffn2_grads_ready(self, dwint, dwout, dw2, dh2):
        self.grads_rest = (dwint, dwout, dw2)
        return dh2

    def before_ffn1_bwd(self, dx1b):
        return dx1b


def _local_step(x, target, weights, g1, gm, g3, qg, kg, sinks, relb, pool_w, pscale):
    bucket = jnp.asarray(_t5_bucket_table())
    sk = sinks.reshape(NH)
    w1 = weights.ffn1()
    h1 = _norm_fwd(x, g1, "norm1_fwd")
    x1, gate1, up1 = _ffn_fwd(h1, w1, x, "ffn1_fwd")
    h2 = _norm_fwd(x1, gm, "norm2_fwd")
    wint, wout = weights.mix(h2)
    z = _in_proj_fwd(h2, wint, "in_proj_fwd")
    ymix = _mix_fwd(z, qg, kg, sk, relb, bucket, pool_w, pscale, "mix_fwd")
    wout = weights.before_out_proj(wout, ymix)
    x2 = _out_proj_fwd(ymix, wout, x1, "out_proj_fwd")
    h3 = _norm_fwd(x2, g3, "norm3_fwd")
    w2 = weights.ffn2(h3)
    y, gate2, up2 = _ffn_fwd(h3, w2, x2, "ffn2_fwd")
    dy, dyb, loss_lanes = _loss_grad(y, target, "loss_grad")

    dh3, dw2 = _ffn_bwd(dyb, h3, gate2, up2, w2, "ffn2_bwd")
    dx2, dx2b, dg3 = _norm_bwd(dh3, x2, g3, dy, 1.0, "norm3_bwd")
    dymix, dwout = _out_proj_bwd(dx2b, wout, ymix, "out_proj_bwd")
    dz, dqg, dkg, dsk, drb, dpw, dps = _mix_bwd(z, dymix, qg, kg, sk, relb, bucket, pool_w, pscale, "mix_bwd")
    dh2, dwint = _in_proj_bwd(dz, wint, h2, "in_proj_bwd")
    dh2 = weights.mix_ffn2_grads_ready(dwint, dwout, dw2, dh2)
    dx1, dx1b, dgm = _norm_bwd(dh2, x1, gm, dx2, 0.5, "norm2_bwd")
    dx1b = weights.before_ffn1_bwd(dx1b)
    dh1, dw1 = _ffn_bwd(dx1b, h1, gate1, up1, w1, "ffn1_bwd")
    gx, _, dg1 = _norm_bwd(dh1, x, g1, dx1, 1.0, "norm1_bwd")
    small = dict(ffn1_norm=dg1, mix_norm=dgm, ffn2_norm=dg3, pool_scale=dps, q_norm=dqg, k_norm=dkg,
                 attn_sinks=dsk[:, :NH], rel_bias=drb, pool_w=dpw, loss=loss_lanes)
    return gx, (dw1, dwint, dwout, dw2), small


SMALL_NAMES = ("ffn1_norm", "mix_norm", "ffn2_norm", "pool_scale", "q_norm", "k_norm", "attn_sinks", "rel_bias",
               "pool_w", "loss")
SMALL_SHAPES = dict(ffn1_norm=(1, D), mix_norm=(1, D), ffn2_norm=(1, D), pool_scale=(1, DPOOL), q_norm=(1, HD),
                    k_norm=(1, HD), attn_sinks=(1, NH), rel_bias=(NBUCK, NH),
                    pool_w=(1, len(POOL_WINDOWS), PGD, PGD), loss=(1, 128))


def _small_rows(name):
    return -(-int(np.prod(SMALL_SHAPES[name])) // 128)


SMALL_OFF = {}
_r = 0
for _n in SMALL_NAMES:
    SMALL_OFF[_n] = _r
    _r += _small_rows(_n)
SMALL_ROWS = -(-_r // 8) * 8
LOSS_ROW = SMALL_OFF["loss"]


def _pack_small(vals):
    parts = []
    for n in SMALL_NAMES:
        size = _small_rows(n) * 128
        if n in vals:
            flat = vals[n].astype(f32).reshape(-1)
            parts.append(jnp.pad(flat, (0, size - flat.shape[0])))
        else:
            parts.append(jnp.zeros((size,), f32))
    flat = jnp.concatenate(parts)
    flat = jnp.pad(flat, (0, SMALL_ROWS * 128 - flat.shape[0]))
    return flat.reshape(SMALL_ROWS, 128)


def _unpack_small(packed, name):
    size = int(np.prod(SMALL_SHAPES[name]))
    r0 = SMALL_OFF[name]
    return packed[r0:r0 + _small_rows(name)].reshape(-1)[:size].reshape(SMALL_SHAPES[name])


def _position():
    return lax.axis_index("x"), lax.axis_index("y"), lax.axis_index("c")


def _dev_index(x, y, c):
    return 4 * x + 2 * y + c


G1_PIECES, MIX_PIECES, F2_PIECES = (0, 1, 2), (3, 4), (5, 6, 7)


def _group_rows(pieces):
    return sum(PIECE_ROWS[k] for k in pieces)


def _shard_piece(s_ref, k):
    return s_ref.at[pl.ds(PIECE_OFF[k], PIECE_ROWS[k]), :]


def _shard_group(s_ref, pieces):
    return s_ref.at[pl.ds(PIECE_OFF[pieces[0]], _group_rows(pieces)), :]


def _weight_pieces(w1_ref=None, wi_ref=None, wo_ref=None, w2_ref=None):
    arrs = {}
    if w1_ref is not None:
        arrs.update({0: w1_ref.at[0], 1: w1_ref.at[1], 2: w1_ref.at[2]})
    if wi_ref is not None:
        arrs[3] = wi_ref
    if wo_ref is not None:
        arrs[4] = wo_ref
    if w2_ref is not None:
        arrs.update({5: w2_ref.at[0], 6: w2_ref.at[1], 7: w2_ref.at[2]})
    return arrs


def _block_rows(arrs, k, dev):
    r = PIECE_ROWS[k]
    return arrs[k].at[pl.ds(pl.multiple_of(_dev_index(*dev) * r, 16), r), :]


def _all_gather_ffn1(shard):
    pieces = G1_PIECES

    def body(s_ref, w1_ref, send_sems, recv_sems, local_sem):
        x, y, c = _position()
        me, sib = (x, y, c), (x, y, 1 - c)
        chips = [(1 - x, y), (x, 1 - y), (1 - x, 1 - y)]
        arrs = _weight_pieces(w1_ref=w1_ref)

        def copies(rel, block, to, from_shard):
            return [pltpu.make_async_remote_copy(
                src_ref=_shard_piece(s_ref, k) if from_shard else _block_rows(arrs, k, block),
                dst_ref=_block_rows(arrs, k, block),
                send_sem=send_sems.at[rel], recv_sem=recv_sems.at[rel], device_id=to, device_id_type=MESH)
                for k in pieces]

        def whole(rel):
            grp = _shard_group(s_ref, pieces)
            return pltpu.make_async_remote_copy(src_ref=grp, dst_ref=grp, send_sem=send_sems.at[rel],
                                                recv_sem=recv_sems.at[rel], device_id=me, device_id_type=MESH)

        mine = [pltpu.make_async_copy(_shard_piece(s_ref, k), _block_rows(arrs, k, me), local_sem) for k in pieces]
        for cp in mine:
            cp.start()
        for cp in copies(0, me, sib, True):
            cp.start()
        for j, chip in enumerate(chips):
            for cp in copies(1 + j, me, (*chip, c), True):
                cp.start()
        for j, chip in enumerate(chips):
            whole(1 + j).wait_recv()
            for cp in copies(4 + j, (*chip, c), sib, False):
                cp.start()
        whole(0).wait_recv()
        for j in range(3):
            whole(4 + j).wait_recv()
        for rel in range(7):
            whole(rel).wait_send()
        grp = _shard_group(s_ref, pieces)
        pltpu.make_async_copy(grp, grp, local_sem).wait()

    hbm = pl.BlockSpec(memory_space=pl.ANY)
    return pl.pallas_call(
        body, in_specs=[hbm], out_specs=hbm, out_shape=SDS((3, F, D), bf16),
        scratch_shapes=[pltpu.SemaphoreType.DMA((7,)), pltpu.SemaphoreType.DMA((7,)), pltpu.SemaphoreType.DMA],
        compiler_params=pltpu.CompilerParams(has_side_effects=True),
        name="all_gather_ffn1")(shard)


HBM_SPEC = pl.BlockSpec(memory_space=pltpu.HBM)
SEM_SPEC = pl.BlockSpec(memory_space=pltpu.SEMAPHORE)
ANY_SPEC = pl.BlockSpec(memory_space=pl.ANY)
SPLIT_EFFECT = pltpu.SideEffectType.DATAFLOW_SIDE_EFFECTING


def _in_hbm(a):
    return pltpu.with_memory_space_constraint(a, pltpu.HBM)


def _hbm_like(a):
    return pltpu.HBM(a.shape, a.dtype)


def _place_own_rows(shard):
    pieces = MIX_PIECES + F2_PIECES

    def body(s_ref, wi_ref, wo_ref, w2_ref, buf, sems):
        x, y, c = _position()
        arrs = _weight_pieces(wi_ref=wi_ref, wo_ref=wo_ref, w2_ref=w2_ref)
        grp = _shard_group(s_ref, pieces)
        load = pltpu.make_async_copy(grp, buf, sems.at[0])
        load.start()
        load.wait()
        base = PIECE_OFF[pieces[0]]
        for k in pieces:
            pltpu.make_async_copy(buf.at[pl.ds(PIECE_OFF[k] - base, PIECE_ROWS[k]), :],
                                  _block_rows(arrs, k, (x, y, c)), sems.at[1]).start()
        pltpu.make_async_copy(grp, buf, sems.at[1]).wait()

    return pl.pallas_call(
        body, in_specs=[ANY_SPEC], out_specs=[ANY_SPEC] * 3,
        out_shape=(SDS((DIN, D), bf16), SDS((DMIX, D), bf16), SDS((3, F, D), bf16)),
        scratch_shapes=[pltpu.VMEM((_group_rows(pieces), D), bf16), pltpu.SemaphoreType.DMA((2,))],
        name="place_own_rows")(shard)


def _xor_peer(x, y, c, k):
    return (x ^ (k >> 2), y ^ ((k >> 1) & 1), c ^ (k & 1))


def _gather_rest_start(shard, wi, wo, w2, w1):
    def body(s_ref, wi_ref, wo_ref, w2_ref, w1_ref,
             ssem_m, rsem_m, ssem_f, rsem_f0, rsem_f, s_o, wi_o, wo_o, w2_o, w1_o):
        x, y, c = _position()
        me, sib = (x, y, c), (x, y, 1 - c)
        chips = [(1 - x, y), (x, 1 - y), (1 - x, 1 - y)]
        arrs = _weight_pieces(wi_ref=wi_ref, wo_ref=wo_ref, w2_ref=w2_ref)
        for k in range(1, NDEV):
            for p in MIX_PIECES:
                pltpu.make_async_remote_copy(
                    src_ref=_shard_piece(s_ref, p), dst_ref=_block_rows(arrs, p, me), send_sem=ssem_m.at[k - 1],
                    recv_sem=rsem_m.at[k - 1], device_id=_xor_peer(x, y, c, k), device_id_type=MESH).start()
        for p in F2_PIECES:
            pltpu.make_async_remote_copy(
                src_ref=_shard_piece(s_ref, p), dst_ref=_block_rows(arrs, p, me), send_sem=ssem_f.at[0],
                recv_sem=rsem_f0, device_id=sib, device_id_type=MESH).start()
        for j, chip in enumerate(chips):
            for p in F2_PIECES:
                pltpu.make_async_remote_copy(
                    src_ref=_shard_piece(s_ref, p), dst_ref=_block_rows(arrs, p, me), send_sem=ssem_f.at[1 + j],
                    recv_sem=rsem_f.at[j], device_id=(*chip, c), device_id_type=MESH).start()

    dma = pltpu.SemaphoreType.DMA
    return pl.pallas_call(
        body, name="gather_rest_start",
        out_shape=(dma((7,)), dma((7,)), dma((4,)), dma(()), dma((3,)),
                   _hbm_like(shard), _hbm_like(wi), _hbm_like(wo), _hbm_like(w2), _hbm_like(w1)),
        in_specs=(HBM_SPEC,) * 5, out_specs=(SEM_SPEC,) * 5 + (HBM_SPEC,) * 5,
        input_output_aliases={0: 5, 1: 6, 2: 7, 3: 8, 4: 9},
        compiler_params=pltpu.CompilerParams(has_side_effects=SPLIT_EFFECT),
    )(_in_hbm(shard), _in_hbm(wi), _in_hbm(wo), _in_hbm(w2), _in_hbm(w1))


def _gather_mix_wait(ssem_m, rsem_m, shard, wi, wo, after):
    def body(s_ref, wi_ref, wo_ref, ssem, rsem, after_ref, s_o, wi_o, wo_o):
        x, y, c = _position()
        grp = _shard_group(s_ref, MIX_PIECES)
        for k in range(NDEV - 1):
            d = pltpu.make_async_remote_copy(src_ref=grp, dst_ref=grp, send_sem=ssem.at[k], recv_sem=rsem.at[k],
                                             device_id=(x, y, c), device_id_type=MESH)
            d.wait_recv()
            d.wait_send()

    return pl.pallas_call(
        body, name="gather_mix_wait", out_shape=(_hbm_like(shard), _hbm_like(wi), _hbm_like(wo)),
        in_specs=(HBM_SPEC, HBM_SPEC, HBM_SPEC, SEM_SPEC, SEM_SPEC, ANY_SPEC), out_specs=(HBM_SPEC,) * 3,
        input_output_aliases={0: 0, 1: 1, 2: 2},
        compiler_params=pltpu.CompilerParams(has_side_effects=SPLIT_EFFECT),
    )(shard, wi, wo, ssem_m, rsem_m, after)


def _gather_ffn2_pass_on(rsem_f, w2, wo, after):
    def body(w2_ref, wo_ref, rsem, after_ref, fsend, frecv, w2_o, wo_o):
        x, y, c = _position()
        sib = (x, y, 1 - c)
        chips = [(1 - x, y), (x, 1 - y), (1 - x, 1 - y)]
        arrs = _weight_pieces(w2_ref=w2_ref)
        three = w2_ref.at[0, pl.ds(0, _group_rows(F2_PIECES)), :]
        for j, chip in enumerate(chips):
            pltpu.make_async_remote_copy(src_ref=three, dst_ref=three, send_sem=fsend.at[j], recv_sem=rsem.at[j],
                                         device_id=(x, y, c), device_id_type=MESH).wait_recv()
            for p in F2_PIECES:
                rows = _block_rows(arrs, p, (*chip, c))
                pltpu.make_async_remote_copy(src_ref=rows, dst_ref=rows, send_sem=fsend.at[j], recv_sem=frecv.at[j],
                                             device_id=sib, device_id_type=MESH).start()

    dma = pltpu.SemaphoreType.DMA
    return pl.pallas_call(
        body, name="gather_ffn2_pass_on", out_shape=(dma((3,)), dma((3,)), _hbm_like(w2), _hbm_like(wo)),
        in_specs=(HBM_SPEC, HBM_SPEC, SEM_SPEC, ANY_SPEC), out_specs=(SEM_SPEC, SEM_SPEC, HBM_SPEC, HBM_SPEC),
        input_output_aliases={0: 2, 1: 3},
        compiler_params=pltpu.CompilerParams(has_side_effects=SPLIT_EFFECT),
    )(w2, wo, rsem_f, after)


def _gather_ffn2_wait(ssem_f, rsem_f0, fsend, frecv, shard, w2, after):
    def body(s_ref, w2_ref, ssem, rsem0, fs, fr, after_ref, w2_o):
        x, y, c = _position()
        grp = _shard_group(s_ref, F2_PIECES)

        def waiter(send_sem, recv_sem):
            return pltpu.make_async_remote_copy(src_ref=grp, dst_ref=grp, send_sem=send_sem, recv_sem=recv_sem,
                                                device_id=(x, y, c), device_id_type=MESH)

        waiter(ssem.at[0], rsem0).wait_recv()
        for j in range(3):
            waiter(fs.at[j], fr.at[j]).wait_recv()
        for rel in range(4):
            waiter(ssem.at[rel], rsem0).wait_send()
        for j in range(3):
            waiter(fs.at[j], fr.at[j]).wait_send()

    return pl.pallas_call(
        body, name="gather_ffn2_wait", out_shape=_hbm_like(w2),
        in_specs=(HBM_SPEC, HBM_SPEC, SEM_SPEC, SEM_SPEC, SEM_SPEC, SEM_SPEC, ANY_SPEC), out_specs=HBM_SPEC,
        input_output_aliases={1: 0},
        compiler_params=pltpu.CompilerParams(has_side_effects=SPLIT_EFFECT),
    )(shard, w2, ssem_f, rsem_f0, fsend, frecv, after)


class _GatheredWeights(_LocalWeights):
    def __init__(self, shard):
        w1 = _all_gather_ffn1(shard)
        wi, wo, w2 = _place_own_rows(shard)
        (self.ssem_m, self.rsem_m, self.ssem_f, self.rsem_f0, self.rsem_f,
         self.shard, self.wi, self.wo, self.w2_part, self.w1) = _gather_rest_start(shard, wi, wo, w2, w1)

    def mix(self, after):
        self.shard, wint, wout = _gather_mix_wait(self.ssem_m, self.rsem_m, self.shard, self.wi, self.wo, after)
        return wint, wout

    def before_out_proj(self, wout, after):
        self.fsend, self.frecv, self.w2_part, wout = _gather_ffn2_pass_on(self.rsem_f, self.w2_part, wout, after)
        return wout

    def ffn2(self, after):
        return _gather_ffn2_wait(self.ssem_f, self.rsem_f0, self.fsend, self.frecv, self.shard, self.w2_part, after)

    def mix_ffn2_grads_ready(self, dwint, dwout, dw2, dh2):
        rx1 = lax.empty((4, RSA_ROWS, D), bf16)
        self.sa, self.ra, dwint, dwout, dw2, rx1, dh2 = _rsa_level1_start(dwint, dwout, dw2, rx1, dh2)
        self.level1 = (dwint, dwout, dw2, rx1)
        return dh2

    def before_ffn1_bwd(self, dx1b):
        dwint, dwout, dw2, rx1 = _rsa_level1_wait(self.sa, self.ra, *self.level1, dx1b)
        tx, self.acc = _rsa_chip_sums(dwint, dwout, dw2, rx1)
        rx2 = lax.empty((3, RSA_ROWS, D), bf16)
        self.sb, self.rb, self.tx, self.rx2, dx1b = _rsa_level2_start(tx, rx2, dx1b)
        return dx1b

    def mix_ffn2_grads_total(self, after):
        rx2 = _rsa_level2_wait(self.sb, self.rb, self.tx, self.rx2, after)
        return _rsa_total(self.acc, rx2)


RS_CHUNK = 176


def _reduce_scatter_ffn1(dw1):
    pieces = G1_PIECES
    nrows = _group_rows(pieces)
    nchunk = nrows // RS_CHUNK

    def body(d1_ref, red_ref, rx1_ref, rx2_ref,
             own_buf, rx_buf, tx_buf, acc, sa, ra, sb, rb, lsem):
        x, y, c = _position()
        me, sib = (x, y, c), (x, y, 1 - c)
        rel_chips = [(x, y), (1 - x, y), (x, 1 - y), (1 - x, 1 - y)]
        srcs = _weight_pieces(w1_ref=d1_ref)

        def piece(k, dev):
            r = PIECE_ROWS[k]
            return srcs[k].at[pl.ds(pl.multiple_of(_dev_index(*dev) * r, 16), r), :]

        def packed(ref, k):
            return ref.at[pl.ds(PIECE_OFF[k], PIECE_ROWS[k]), :]

        for j, chip in enumerate(rel_chips):
            for k in pieces:
                pltpu.make_async_remote_copy(
                    src_ref=piece(k, (*chip, 1 - c)), dst_ref=packed(rx1_ref.at[j], k),
                    send_sem=sa.at[j], recv_sem=ra.at[j], device_id=sib, device_id_type=MESH).start()

        def wait_a(j):
            return pltpu.make_async_remote_copy(src_ref=rx1_ref.at[j], dst_ref=rx1_ref.at[j], send_sem=sa.at[j],
                                                recv_sem=ra.at[j], device_id=me, device_id_type=MESH)

        def ici(j):
            return pltpu.make_async_remote_copy(
                src_ref=tx_buf.at[j - 1], dst_ref=rx2_ref.at[j - 1], send_sem=sb.at[j - 1], recv_sem=rb.at[j - 1],
                device_id=(*rel_chips[j], c), device_id_type=MESH)

        for j in (1, 2, 3, 0):
            loads = [pltpu.make_async_copy(piece(k, (*rel_chips[j], c)), packed(own_buf, k), lsem)
                     for k in pieces]
            for cp in loads:
                cp.start()
            wait_a(j).wait_recv()
            got = pltpu.make_async_copy(rx1_ref.at[j], rx_buf, lsem)
            got.start()
            pltpu.make_async_copy(rx_buf, rx_buf, lsem).wait()
            got.wait()

            def add(i, carry, j=j):
                rows = pl.ds(pl.multiple_of(i * RS_CHUNK, 16), RS_CHUNK)
                tot = own_buf[rows, :].astype(f32) + rx_buf[rows, :].astype(f32)
                if j == 0:
                    acc[rows, :] = tot
                else:
                    tx_buf[j - 1, rows, :] = tot.astype(bf16)
                return carry

            lax.fori_loop(0, nchunk, add, 0)
            if j != 0:
                ici(j).start()

        for j in (1, 2, 3):
            ici(j).wait_recv()
            got = pltpu.make_async_copy(rx2_ref.at[j - 1], rx_buf, lsem)
            got.start()
            got.wait()

            def add2(i, carry):
                rows = pl.ds(pl.multiple_of(i * RS_CHUNK, 16), RS_CHUNK)
                acc[rows, :] += rx_buf[rows, :].astype(f32)
                return carry

            lax.fori_loop(0, nchunk, add2, 0)
        out = pltpu.make_async_copy(acc, red_ref, lsem)
        out.start()
        out.wait()
        for j in range(4):
            wait_a(j).wait_send()
        for j in (1, 2, 3):
            ici(j).wait_send()

    hbm = pl.BlockSpec(memory_space=pl.ANY)
    red, _, _ = pl.pallas_call(
        body, in_specs=[hbm], out_specs=[hbm] * 3,
        out_shape=(SDS((nrows, D), f32), SDS((4, nrows, D), bf16), SDS((3, nrows, D), bf16)),
        scratch_shapes=[pltpu.VMEM((nrows, D), bf16), pltpu.VMEM((nrows, D), bf16),
                        pltpu.VMEM((3, nrows, D), bf16), pltpu.VMEM((nrows, D), f32),
                        pltpu.SemaphoreType.DMA((4,)), pltpu.SemaphoreType.DMA((4,)),
                        pltpu.SemaphoreType.DMA((3,)), pltpu.SemaphoreType.DMA((3,)), pltpu.SemaphoreType.DMA],
        compiler_params=pltpu.CompilerParams(has_side_effects=True, vmem_limit_bytes=VMEM_LIMIT_V7X),
        name="reduce_scatter_ffn1")(dw1)
    return red


RSA_PIECES = MIX_PIECES + F2_PIECES
RSA_ROWS = _group_rows(RSA_PIECES)
RSA_OFF = {k: PIECE_OFF[k] - PIECE_OFF[RSA_PIECES[0]] for k in RSA_PIECES}
RSA_BLOCK = 192


def _rsa_rows(ref, k):
    return ref.at[pl.ds(RSA_OFF[k], PIECE_ROWS[k]), :]


def _rsa_level1_start(dwint, dwout, dw2, rx1, thru):
    def body(di_ref, do_ref, d2_ref, rx1_ref, thru_ref, sa, ra, di_o, do_o, d2_o, rx1_o, thru_o):
        x, y, c = _position()
        srcs = _weight_pieces(wi_ref=di_ref, wo_ref=do_ref, w2_ref=d2_ref)
        for j, chip in enumerate([(x, y), (1 - x, y), (x, 1 - y), (1 - x, 1 - y)]):
            for k in RSA_PIECES:
                pltpu.make_async_remote_copy(
                    src_ref=_block_rows(srcs, k, (*chip, 1 - c)), dst_ref=_rsa_rows(rx1_ref.at[j], k),
                    send_sem=sa.at[j], recv_sem=ra.at[j], device_id=(x, y, 1 - c), device_id_type=MESH).start()

    dma = pltpu.SemaphoreType.DMA
    arrs = (dwint, dwout, dw2, rx1, thru)
    return pl.pallas_call(
        body, name="rsa_level1_start", out_shape=(dma((4,)), dma((4,))) + tuple(_hbm_like(a) for a in arrs),
        in_specs=(HBM_SPEC,) * 5, out_specs=(SEM_SPEC,) * 2 + (HBM_SPEC,) * 5,
        input_output_aliases={0: 2, 1: 3, 2: 4, 3: 5, 4: 6},
        compiler_params=pltpu.CompilerParams(has_side_effects=SPLIT_EFFECT),
    )(*[_in_hbm(a) for a in arrs])


def _rsa_level1_wait(sa, ra, dwint, dwout, dw2, rx1, after):
    def body(di_ref, do_ref, d2_ref, rx1_ref, sa_ref, ra_ref, after_ref, di_o, do_o, d2_o, rx1_o):
        x, y, c = _position()
        for j in range(4):
            d = pltpu.make_async_remote_copy(src_ref=rx1_ref.at[j], dst_ref=rx1_ref.at[j], send_sem=sa_ref.at[j],
                                             recv_sem=ra_ref.at[j], device_id=(x, y, c), device_id_type=MESH)
            d.wait_recv()
            d.wait_send()

    arrs = (dwint, dwout, dw2, rx1)
    return pl.pallas_call(
        body, name="rsa_level1_wait", out_shape=tuple(_hbm_like(a) for a in arrs),
        in_specs=(HBM_SPEC,) * 4 + (SEM_SPEC, SEM_SPEC, ANY_SPEC), out_specs=(HBM_SPEC,) * 4,
        input_output_aliases={0: 0, 1: 1, 2: 2, 3: 3},
        compiler_params=pltpu.CompilerParams(has_side_effects=SPLIT_EFFECT),
    )(*arrs, sa, ra, after)


def _rsa_chip_sums(dwint, dwout, dw2, rx1):
    nblk = RSA_ROWS // RSA_BLOCK

    def body(di_ref, do_ref, d2_ref, rx1_ref, tx_ref, acc_ref, own_buf, rx_buf, tx_buf, acc_buf, lsems):
        x, y, c = _position()
        srcs = _weight_pieces(wi_ref=di_ref, wo_ref=do_ref, w2_ref=d2_ref)
        for j, chip in enumerate([(x, y), (1 - x, y), (x, 1 - y), (1 - x, 1 - y)]):
            loads = [pltpu.make_async_copy(_block_rows(srcs, k, (*chip, c)), _rsa_rows(own_buf, k), lsems.at[0])
                     for k in RSA_PIECES]
            got = pltpu.make_async_copy(rx1_ref.at[j], rx_buf, lsems.at[1])
            for cp in loads + [got]:
                cp.start()
            pltpu.make_async_copy(rx_buf, rx_buf, lsems.at[0]).wait()
            got.wait()

            def add(i, carry, j=j):
                rows = pl.ds(pl.multiple_of(i * RSA_BLOCK, 16), RSA_BLOCK)
                tot = own_buf[rows, :].astype(f32) + rx_buf[rows, :].astype(f32)
                if j == 0:
                    acc_buf[rows, :] = tot
                else:
                    tx_buf[rows, :] = tot.astype(bf16)
                return carry

            lax.fori_loop(0, nblk, add, 0)
            out = (pltpu.make_async_copy(acc_buf, acc_ref, lsems.at[2]) if j == 0
                   else pltpu.make_async_copy(tx_buf, tx_ref.at[j - 1], lsems.at[2]))
            out.start()
            out.wait()

    return pl.pallas_call(
        body, in_specs=[ANY_SPEC] * 4, out_specs=[ANY_SPEC] * 2,
        out_shape=(SDS((3, RSA_ROWS, D), bf16), SDS((RSA_ROWS, D), f32)),
        scratch_shapes=[pltpu.VMEM((RSA_ROWS, D), bf16), pltpu.VMEM((RSA_ROWS, D), bf16),
                        pltpu.VMEM((RSA_ROWS, D), bf16), pltpu.VMEM((RSA_ROWS, D), f32),
                        pltpu.SemaphoreType.DMA((3,))],
        compiler_params=_cparams(None, VMEM_LIMIT_V7X), name="rsa_chip_sums")(dwint, dwout, dw2, rx1)


def _rsa_level2_start(tx, rx2, thru):
    def body(tx_ref, rx2_ref, thru_ref, sb, rb, tx_o, rx2_o, thru_o):
        x, y, c = _position()
        for j, chip in enumerate([(1 - x, y), (x, 1 - y), (1 - x, 1 - y)]):
            pltpu.make_async_remote_copy(src_ref=tx_ref.at[j], dst_ref=rx2_ref.at[j], send_sem=sb.at[j],
                                         recv_sem=rb.at[j], device_id=(*chip, c), device_id_type=MESH).start()

    dma = pltpu.SemaphoreType.DMA
    arrs = (tx, rx2, thru)
    return pl.pallas_call(
        body, name="rsa_level2_start", out_shape=(dma((3,)), dma((3,))) + tuple(_hbm_like(a) for a in arrs),
        in_specs=(HBM_SPEC,) * 3, out_specs=(SEM_SPEC,) * 2 + (HBM_SPEC,) * 3,
        input_output_aliases={0: 2, 1: 3, 2: 4},
        compiler_params=pltpu.CompilerParams(has_side_effects=SPLIT_EFFECT),
    )(*[_in_hbm(a) for a in arrs])


def _rsa_level2_wait(sb, rb, tx, rx2, after):
    def body(tx_ref, rx2_ref, sb_ref, rb_ref, after_ref, rx2_o):
        x, y, c = _position()
        for j in range(3):
            d = pltpu.make_async_remote_copy(src_ref=tx_ref.at[j], dst_ref=rx2_ref.at[j], send_sem=sb_ref.at[j],
                                             recv_sem=rb_ref.at[j], device_id=(x, y, c), device_id_type=MESH)
            d.wait_recv()
            d.wait_send()

    return pl.pallas_call(
        body, name="rsa_level2_wait", out_shape=_hbm_like(rx2),
        in_specs=(HBM_SPEC, HBM_SPEC, SEM_SPEC, SEM_SPEC, ANY_SPEC), out_specs=HBM_SPEC,
        input_output_aliases={1: 0},
        compiler_params=pltpu.CompilerParams(has_side_effects=SPLIT_EFFECT),
    )(tx, rx2, sb, rb, after)


def _rsa_total(acc, rx2):
    def body(a_ref, r_ref, o_ref):
        o_ref[...] = ((a_ref[...] + r_ref[0].astype(f32)) + r_ref[1].astype(f32)) + r_ref[2].astype(f32)

    return pl.pallas_call(
        body, grid=(RSA_ROWS // RSA_BLOCK,),
        in_specs=[pl.BlockSpec((RSA_BLOCK, D), lambda i: (i, 0)), pl.BlockSpec((3, RSA_BLOCK, D), lambda i: (0, i, 0))],
        out_specs=pl.BlockSpec((RSA_BLOCK, D), lambda i: (i, 0)),
        out_shape=SDS((RSA_ROWS, D), f32), name="rsa_total")(acc, rx2)


def _all_reduce_small(packed):
    def body(p_ref, o_ref, pair, chips, send_sems, recv_sems):
        x, y, c = _position()
        chip = 2 * x + y
        pair[c] = p_ref[...]
        swap = pltpu.make_async_remote_copy(
            src_ref=p_ref, dst_ref=pair.at[c], send_sem=send_sems.at[0], recv_sem=recv_sems.at[0],
            device_id=(x, y, 1 - c), device_id_type=MESH)
        swap.start()
        swap.wait_recv()
        chips[chip] = pair[0] + pair[1]
        cps = [pltpu.make_async_remote_copy(
            src_ref=chips.at[chip], dst_ref=chips.at[chip], send_sem=send_sems.at[1 + j], recv_sem=recv_sems.at[1 + j],
            device_id=(*other, c), device_id_type=MESH)
            for j, other in enumerate([(1 - x, y), (x, 1 - y), (1 - x, 1 - y)])]
        for cp in cps:
            cp.start()
        for cp in cps:
            cp.wait_recv()
        tot = (chips[0] + chips[1]) + (chips[2] + chips[3])
        o_ref[...] = tot
        loss = jnp.sum(tot[LOSS_ROW:LOSS_ROW + 1, :], axis=-1, keepdims=True)
        o_ref[LOSS_ROW:LOSS_ROW + 1, :] = jnp.broadcast_to(loss, (1, 128))
        swap.wait_send()
        for cp in cps:
            cp.wait_send()

    vm = pl.BlockSpec(memory_space=pltpu.VMEM)
    return pl.pallas_call(
        body, in_specs=[vm], out_specs=vm, out_shape=SDS((SMALL_ROWS, 128), f32),
        scratch_shapes=[pltpu.VMEM((2, SMALL_ROWS, 128), f32), pltpu.VMEM((4, SMALL_ROWS, 128), f32),
                        pltpu.SemaphoreType.DMA((4,)), pltpu.SemaphoreType.DMA((4,))],
        compiler_params=pltpu.CompilerParams(has_side_effects=True),
        name="all_reduce_small")(packed)


def _adamw_math(w, g, m, v):
    m = ADAM_B1 * m + (1.0 - ADAM_B1) * g
    v = ADAM_B2 * v + (1.0 - ADAM_B2) * (g * g)
    m_hat = m / (1.0 - ADAM_B1 ** ADAM_STEP)
    v_hat = v / (1.0 - ADAM_B2 ** ADAM_STEP)
    delta = -ADAM_LR * (m_hat / (jnp.sqrt(v_hat) + ADAM_EPS) + ADAM_WD * w)
    return delta, m, v


def _adamw_big(ws, ms, vs, red1, red_rest):
    npiece = len(BIG)
    rmax = max(PIECE_ROWS)

    def body(*refs):
        ins = (refs[0:npiece], refs[npiece:2 * npiece], refs[2 * npiece:3 * npiece])
        red1_ref, rest_ref = refs[3 * npiece:3 * npiece + 2]
        out_refs = refs[3 * npiece + 2:7 * npiece + 2]
        inb, outb, in_sems, out_sems = refs[7 * npiece + 2:]

        def grad_rows(k):
            if k in G1_PIECES:
                return red1_ref.at[pl.ds(PIECE_OFF[k], PIECE_ROWS[k]), :]
            return _rsa_rows(rest_ref, k)

        def loads(k):
            s, r = k % 2, PIECE_ROWS[k]
            cps = [pltpu.make_async_copy(ins[q][k].at[0], inb.at[s, q, pl.ds(0, r), :], in_sems.at[4 * s + q])
                   for q in range(3)]
            cps.append(pltpu.make_async_copy(grad_rows(k), inb.at[s, 3, pl.ds(0, r), :], in_sems.at[4 * s + 3]))
            return cps

        def stores(k):
            s, r = k % 2, PIECE_ROWS[k]
            return [pltpu.make_async_copy(outb.at[s, q, pl.ds(0, r), :], out_refs[q * npiece + k].at[0],
                                          out_sems.at[4 * s + q]) for q in range(4)]

        for cp in loads(0):
            cp.start()
        for k in range(npiece):
            s, r = k % 2, PIECE_ROWS[k]
            if k + 1 < npiece:
                for cp in loads(k + 1):
                    cp.start()
            for cp in loads(k):
                cp.wait()
            if k >= 2:
                for cp in stores(k - 2):
                    cp.wait()
            g = inb[s, 3, 0:r, :]
            d, nm, nv = _adamw_math(inb[s, 0, 0:r, :], g, inb[s, 1, 0:r, :], inb[s, 2, 0:r, :])
            outb[s, 0, 0:r, :] = g
            outb[s, 1, 0:r, :] = d
            outb[s, 2, 0:r, :] = nm
            outb[s, 3, 0:r, :] = nv
            for cp in stores(k):
                cp.start()
        for k in (npiece - 2, npiece - 1):
            for cp in stores(k):
                cp.wait()

    hbm = pl.BlockSpec(memory_space=pl.ANY)
    outs = pl.pallas_call(
        body, in_specs=[hbm] * (3 * npiece + 2), out_specs=[hbm] * (4 * npiece),
        out_shape=tuple(SDS(w.shape, f32) for _ in range(4) for w in ws),
        scratch_shapes=[pltpu.VMEM((2, 4, rmax, D), f32), pltpu.VMEM((2, 4, rmax, D), f32),
                        pltpu.SemaphoreType.DMA((8,)), pltpu.SemaphoreType.DMA((8,))],
        compiler_params=_cparams(None, VMEM_LIMIT_V7X), name="adamw_big")(*ws, *ms, *vs, red1, red_rest)
    return [list(outs[q * npiece:(q + 1) * npiece]) for q in range(4)]


def _adamw_small(w, m, v, g, name):
    def body(w_ref, m_ref, v_ref, g_ref, d_ref, nm_ref, nv_ref):
        d, nm, nv = _adamw_math(w_ref[...], g_ref[...], m_ref[...], v_ref[...])
        d_ref[...] = d
        nm_ref[...] = nm
        nv_ref[...] = nv

    return pl.pallas_call(
        body, out_shape=tuple(SDS(w.shape, f32) for _ in range(3)), name=name)(w, m, v, g)


WEIGHTS = ("ffn1_norm", "ffn1_w_gate", "ffn1_w_up", "ffn1_w_down", "mix_norm", "w_in", "q_norm", "k_norm",
           "attn_sinks", "rel_bias", "pool_w", "pool_scale", "w_out", "ffn2_norm", "ffn2_w_gate", "ffn2_w_up",
           "ffn2_w_down")
BIG = (("ffn1_w_gate", True), ("ffn1_w_up", True), ("ffn1_w_down", False), ("w_in", True), ("w_out", False),
       ("ffn2_w_gate", True), ("ffn2_w_up", True), ("ffn2_w_down", False))


def kernel(x, ffn1_norm, ffn1_w_gate, ffn1_w_up, ffn1_w_down, mix_norm, w_in, q_norm, k_norm, attn_sinks, rel_bias, pool_w, pool_scale, w_out, ffn2_norm, ffn2_w_gate, ffn2_w_up, ffn2_w_down, loss_target, m_ffn1_norm, m_ffn1_w_gate, m_ffn1_w_up, m_ffn1_w_down, m_mix_norm, m_w_in, m_q_norm, m_k_norm, m_attn_sinks, m_rel_bias, m_pool_w, m_pool_scale, m_w_out, m_ffn2_norm, m_ffn2_w_gate, m_ffn2_w_up, m_ffn2_w_down, v_ffn1_norm, v_ffn1_w_gate, v_ffn1_w_up, v_ffn1_w_down, v_mix_norm, v_w_in, v_q_norm, v_k_norm, v_attn_sinks, v_rel_bias, v_pool_w, v_pool_scale, v_w_out, v_ffn2_norm, v_ffn2_w_gate, v_ffn2_w_up, v_ffn2_w_down):
    args = dict(locals())
    w = {n: args[n] for n in WEIGHTS}
    m = {n: args["m_" + n] for n in WEIGHTS}
    v = {n: args["v_" + n] for n in WEIGHTS}

    as_rows = lambda a, tr: jnp.swapaxes(a, 1, 2) if tr else a
    shard = jnp.concatenate([as_rows(w[n], tr)[0].astype(bf16) for n, tr in BIG], axis=0)
    exchanges = _GatheredWeights(shard)
    gx, (dw1, _, _, _), small = _local_step(
        x[0], loss_target[0], exchanges, ffn1_norm, mix_norm, ffn2_norm, q_norm, k_norm, attn_sinks,
        rel_bias, pool_w[0], pool_scale)

    red1 = _reduce_scatter_ffn1(dw1)
    red_rest = exchanges.mix_ffn2_grads_total(red1)
    small_tot = _all_reduce_small(_pack_small(small))

    grads, deltas, new_m, new_v = {}, {}, {}, {}
    big_out = _adamw_big(*[[as_rows(t[n], tr) for n, tr in BIG] for t in (w, m, v)], red1, red_rest)
    for k, (n, tr) in enumerate(BIG):
        grads[n], deltas[n], new_m[n], new_v[n] = [as_rows(o[k], tr) for o in big_out]
    small_names = [n for n in SMALL_NAMES if n != "loss"]
    ds, nms, nvs = _adamw_small(_pack_small({n: w[n] for n in small_names}), _pack_small({n: m[n] for n in small_names}),
                                _pack_small({n: v[n] for n in small_names}), small_tot, "adamw_small")
    for n in small_names:
        grads[n] = _unpack_small(small_tot, n)
        deltas[n], new_m[n], new_v[n] = _unpack_small(ds, n), _unpack_small(nms, n), _unpack_small(nvs, n)
    loss = small_tot[LOSS_ROW, 0]
    return (loss, gx[None], *[grads[n] for n in WEIGHTS], *[deltas[n] for n in WEIGHTS],
            *[new_m[n] for n in WEIGHTS], *[new_v[n] for n in WEIGHTS])
```

```python
import functools

import jax
import jax.numpy as jnp
import numpy as np
from jax import lax
from jax.experimental import pallas as pl
from jax.experimental.pallas import tpu as pltpu

f32, bf16, i32 = jnp.float32, jnp.bfloat16, jnp.int32
SDS = jax.ShapeDtypeStruct

D = 1024
F = 2816
HD = 64
NH = 8
NKV = 2
GQA = NH // NKV
DATTN = NH * HD
DKV = NKV * HD
DPOOL = 512
POOL_WINDOWS = (2, 4, 8, 16)
PGD = DPOOL // len(POOL_WINDOWS)
DIN = DATTN + 2 * DKV + DPOOL
DMIX = DATTN + DPOOL
BLK = 128
NBUCK = 32
MAX_DISTANCE = 128
EPS = 1e-6
NEG = -1e30
SCALE = HD ** -0.5

ADAM_LR, ADAM_B1, ADAM_B2, ADAM_EPS, ADAM_WD, ADAM_STEP = 0.001, 0.9, 0.999, 1e-08, 0.01, 10

NDEV = 8
FS = F // NDEV
INS = DIN // NDEV
OUTS = DMIX // NDEV
PIECE_ROWS = (FS, FS, FS, INS, OUTS, FS, FS, FS)
PIECE_OFF = tuple(int(v) for v in np.cumsum((0,) + PIECE_ROWS[:-1]))
PACK_ROWS = sum(PIECE_ROWS)

VMEM_LIMIT_V7X = 56 * 1024 * 1024

MESH = pl.DeviceIdType.MESH


def _cparams(sem=None, vmem=None):
    return pltpu.CompilerParams(dimension_semantics=sem, vmem_limit_bytes=vmem)


def _nt(a, b):
    return lax.dot_general(a, b, (((1,), (1,)), ((), ())), preferred_element_type=f32)


def _tn(a, b):
    return lax.dot_general(a, b, (((0,), (0,)), ((), ())), preferred_element_type=f32)


def _nn(a, b):
    return jnp.dot(a, b, preferred_element_type=f32)


def _sigmoid(x):
    return 1.0 / (1.0 + jnp.exp(-x))


def _norm_fwd(x, g, name):
    T = x.shape[0]
    tm = min(512, T)

    def body(x_ref, g_ref, h_ref):
        xv = x_ref[...]
        r = lax.rsqrt(jnp.mean(xv * xv, axis=-1, keepdims=True) + EPS)
        h_ref[...] = (xv * r * g_ref[...]).astype(bf16)

    return pl.pallas_call(
        body, grid=(T // tm,),
        in_specs=[pl.BlockSpec((tm, D), lambda i: (i, 0)), pl.BlockSpec((1, D), lambda i: (0, 0))],
        out_specs=pl.BlockSpec((tm, D), lambda i: (i, 0)),
        out_shape=SDS((T, D), bf16), name=name)(x, g)


def _norm_bwd(dh, x, g, dres, out_scale, name):
    T = x.shape[0]
    tm = min(512, T)

    def body(dh_ref, x_ref, g_ref, dr_ref, dx_ref, dxb_ref, dg_ref):
        i = pl.program_id(0)
        xv = x_ref[...]
        r = lax.rsqrt(jnp.mean(xv * xv, axis=-1, keepdims=True) + EPS)
        xh = xv * r
        dhv = dh_ref[...]
        dxh = dhv * g_ref[...]
        dx = dr_ref[...] + r * (dxh - xh * jnp.mean(dxh * xh, axis=-1, keepdims=True))
        dx_ref[...] = dx
        dxb_ref[...] = (out_scale * dx).astype(bf16)
        dg = jnp.sum(dhv * xh, axis=0, keepdims=True)

        @pl.when(i == 0)
        def _():
            dg_ref[...] = dg

        @pl.when(i > 0)
        def _():
            dg_ref[...] += dg

    tok = pl.BlockSpec((tm, D), lambda i: (i, 0))
    vec = pl.BlockSpec((1, D), lambda i: (0, 0))
    return pl.pallas_call(
        body, grid=(T // tm,),
        in_specs=[tok, tok, vec, tok], out_specs=[tok, tok, vec],
        out_shape=(SDS((T, D), f32), SDS((T, D), bf16), SDS((1, D), f32)),
        compiler_params=_cparams(("arbitrary",)), name=name)(dh, x, g, dres)


FFN_ROW_CHUNK = 256


def _ffn_tiles(T):
    return min(1024, T), 256


def _ffn_fwd(h, w, x, name):
    T = h.shape[0]
    tm, tf = _ffn_tiles(T)
    nf = F // tf

    def body(h_ref, w_ref, x_ref, xo_ref, g_ref, u_ref, acc):
        fi = pl.program_id(1)

        @pl.when(fi == 0)
        def _():
            acc[...] = jnp.zeros_like(acc)

        wgu = w_ref[0:2].reshape(2 * tf, D)
        for r in range(0, tm, FFN_ROW_CHUNK):
            rows = slice(r, r + FFN_ROW_CHUNK)
            gu = _nt(h_ref[rows, :], wgu)
            gate, up = gu[:, :tf], gu[:, tf:]
            act = gate * _sigmoid(gate) * up
            g_ref[rows, :] = gate.astype(bf16)
            u_ref[rows, :] = up.astype(bf16)
            acc[rows, :] += _nn(act.astype(bf16), w_ref[2])

        @pl.when(fi == nf - 1)
        def _():
            xo_ref[...] = x_ref[...] + 0.5 * acc[...]

    tok = pl.BlockSpec((tm, D), lambda i, f: (i, 0))
    act_spec = pl.BlockSpec((tm, tf), lambda i, f: (i, f))
    return pl.pallas_call(
        body, grid=(T // tm, nf),
        in_specs=[tok, pl.BlockSpec((3, tf, D), lambda i, f: (0, f, 0)), tok],
        out_specs=[tok, act_spec, act_spec],
        out_shape=(SDS((T, D), f32), SDS((T, F), bf16), SDS((T, F), bf16)),
        scratch_shapes=[pltpu.VMEM((tm, D), f32)],
        compiler_params=_cparams(("arbitrary", "arbitrary"), VMEM_LIMIT_V7X), name=name)(h, w, x)


def _ffn_bwd(dob, h, gate, up, w, name):
    T = h.shape[0]
    tm, tf = _ffn_tiles(T)
    nf, nt = F // tf, T // tm

    def body(do_ref, h_ref, g_ref, u_ref, w_ref, dh_ref, dw_ref, dwacc):
        fi, ti = pl.program_id(0), pl.program_id(1)
        wgu = w_ref[0:2].reshape(2 * tf, D)

        @pl.when(fi == 0)
        def _():
            dh_ref[pl.ds(pl.multiple_of(ti * tm, tm), tm), :] = jnp.zeros((tm, D), f32)

        @pl.when(ti == 0)
        def _():
            dwacc[...] = jnp.zeros_like(dwacc)

        for r in range(0, tm, FFN_ROW_CHUNK):
            rows = slice(r, r + FFN_ROW_CHUNK)
            dov = do_ref[rows, :]
            gv = g_ref[rows, :].astype(f32)
            uv = u_ref[rows, :].astype(f32)
            sg = _sigmoid(gv)
            sil = gv * sg
            dact = _nt(dov, w_ref[2])
            dup = dact * sil
            dgate = dact * uv * (sg * (1.0 + gv * (1.0 - sg)))
            dgu = jnp.concatenate([dgate.astype(bf16), dup.astype(bf16)], axis=1)
            actb = (sil * uv).astype(bf16)
            dh_ref[pl.ds(pl.multiple_of(ti * tm + r, FFN_ROW_CHUNK), FFN_ROW_CHUNK), :] += _nn(dgu, wgu)
            dwacc[0:2 * tf, :] += _tn(dgu, h_ref[rows, :])
            dwacc[2 * tf:3 * tf, :] += _tn(actb, dov)

        @pl.when(ti == nt - 1)
        def _():
            dw_ref[...] = dwacc[...].reshape(3, tf, D).astype(bf16)

    tok = pl.BlockSpec((tm, D), lambda f, i: (i, 0))
    act_spec = pl.BlockSpec((tm, tf), lambda f, i: (i, f))
    wspec = pl.BlockSpec((3, tf, D), lambda f, i: (0, f, 0))
    return pl.pallas_call(
        body, grid=(nf, nt),
        in_specs=[tok, tok, act_spec, act_spec, wspec],
        out_specs=[pl.BlockSpec((T, D), lambda f, i: (0, 0)), wspec],
        out_shape=(SDS((T, D), f32), SDS((3, F, D), bf16)),
        scratch_shapes=[pltpu.VMEM((3 * tf, D), f32)],
        compiler_params=_cparams(("arbitrary", "arbitrary"), VMEM_LIMIT_V7X), name=name)(dob, h, gate, up, w)


def _loss_grad(y, target, name):
    T = y.shape[0]
    tm = min(512, T)

    def body(y_ref, t_ref, dy_ref, dyb_ref, l_ref):
        i = pl.program_id(0)
        e = y_ref[...] - t_ref[...]
        dy = e * (1.0 / D)
        dy_ref[...] = dy
        dyb_ref[...] = (0.5 * dy).astype(bf16)
        col = jnp.sum(e * e, axis=0, keepdims=True) * (0.5 / D)
        lanes = col[:, 0:128]
        for k in range(1, D // 128):
            lanes = lanes + col[:, 128 * k:128 * (k + 1)]

        @pl.when(i == 0)
        def _():
            l_ref[...] = lanes

        @pl.when(i > 0)
        def _():
            l_ref[...] += lanes

    tok = pl.BlockSpec((tm, D), lambda i: (i, 0))
    return pl.pallas_call(
        body, grid=(T // tm,), in_specs=[tok, tok],
        out_specs=[tok, tok, pl.BlockSpec((1, 128), lambda i: (0, 0))],
        out_shape=(SDS((T, D), f32), SDS((T, D), bf16), SDS((1, 128), f32)),
        compiler_params=_cparams(("arbitrary",)), name=name)(y, target)


def _in_proj_fwd(h, wint, name):
    T = h.shape[0]
    tm = min(512, T)

    def body(h_ref, w_ref, z_ref):
        z_ref[...] = _nt(h_ref[...], w_ref[...])

    return pl.pallas_call(
        body, grid=(T // tm,),
        in_specs=[pl.BlockSpec((tm, D), lambda i: (i, 0)), pl.BlockSpec((DIN, D), lambda i: (0, 0))],
        out_specs=pl.BlockSpec((tm, DIN), lambda i: (i, 0)),
        out_shape=SDS((T, DIN), f32), name=name)(h, wint)


def _in_proj_bwd(dz, wint, h, name):
    T = h.shape[0]
    tm = min(512, T)
    nt = T // tm

    def body(dz_ref, w_ref, h_ref, dh_ref, dw_ref, acc):
        i = pl.program_id(0)
        dzb = dz_ref[...].astype(bf16)
        dh_ref[...] = _nn(dzb, w_ref[...])
        part = _tn(dzb, h_ref[...])

        @pl.when(i == 0)
        def _():
            acc[...] = part

        @pl.when(i > 0)
        def _():
            acc[...] += part

        @pl.when(i == nt - 1)
        def _():
            dw_ref[...] = acc[...].astype(bf16)

    wspec = pl.BlockSpec((DIN, D), lambda i: (0, 0))
    return pl.pallas_call(
        body, grid=(nt,),
        in_specs=[pl.BlockSpec((tm, DIN), lambda i: (i, 0)), wspec, pl.BlockSpec((tm, D), lambda i: (i, 0))],
        out_specs=[pl.BlockSpec((tm, D), lambda i: (i, 0)), wspec],
        out_shape=(SDS((T, D), f32), SDS((DIN, D), bf16)),
        scratch_shapes=[pltpu.VMEM((DIN, D), f32)],
        compiler_params=_cparams(("arbitrary",)), name=name)(dz, wint, h)


def _out_proj_fwd(ymix, wout, x, name):
    T = x.shape[0]
    tm = min(512, T)

    def body(y_ref, w_ref, x_ref, o_ref):
        o_ref[...] = x_ref[...] + _nn(y_ref[...], w_ref[...])

    tok = pl.BlockSpec((tm, D), lambda i: (i, 0))
    return pl.pallas_call(
        body, grid=(T // tm,),
        in_specs=[pl.BlockSpec((tm, DMIX), lambda i: (i, 0)), pl.BlockSpec((DMIX, D), lambda i: (0, 0)), tok],
        out_specs=tok, out_shape=SDS((T, D), f32), name=name)(ymix, wout, x)


def _out_proj_bwd(dxb, wout, ymix, name):
    T = dxb.shape[0]
    tm = min(512, T)
    nt = T // tm

    def body(dx_ref, w_ref, y_ref, dy_ref, dw_ref, acc):
        i = pl.program_id(0)
        dxv = dx_ref[...]
        dy_ref[...] = _nt(dxv, w_ref[...])
        part = _tn(y_ref[...], dxv)

        @pl.when(i == 0)
        def _():
            acc[...] = part

        @pl.when(i > 0)
        def _():
            acc[...] += part

        @pl.when(i == nt - 1)
        def _():
            dw_ref[...] = acc[...].astype(bf16)

    wspec = pl.BlockSpec((DMIX, D), lambda i: (0, 0))
    return pl.pallas_call(
        body, grid=(nt,),
        in_specs=[pl.BlockSpec((tm, D), lambda i: (i, 0)), wspec, pl.BlockSpec((tm, DMIX), lambda i: (i, 0))],
        out_specs=[pl.BlockSpec((tm, DMIX), lambda i: (i, 0)), wspec],
        out_shape=(SDS((T, DMIX), f32), SDS((DMIX, D), bf16)),
        scratch_shapes=[pltpu.VMEM((DMIX, D), f32)],
        compiler_params=_cparams(("arbitrary",)), name=name)(dxb, wout, ymix)


def _t5_bucket_table():
    ql = np.arange(BLK)[:, None]
    kl = np.arange(2 * BLK)[None, :]
    n = np.maximum(ql + BLK - kl, 0)
    max_exact = NBUCK // 2
    large = max_exact + (np.log(np.maximum(n, 1) / max_exact) / np.log(MAX_DISTANCE / max_exact)
                         * (NBUCK - max_exact)).astype(np.int32)
    large = np.minimum(large, NBUCK - 1)
    return np.where(n < max_exact, n, large).astype(np.int32)


def _fill_bias(bk_ref, rb_ref, bias_scr):
    bk = bk_ref[...]
    for h in range(NH):
        def step(b, acc, h=h):
            return acc + jnp.where(bk == b, rb_ref[b, h], 0.0)
        bias_scr[h] = lax.fori_loop(0, NBUCK, step, jnp.zeros((BLK, 2 * BLK), f32))


def _attn_probs(zc_ref, zp_ref, kh, qg, kg, sk_ref, bias_scr, n):
    kc = DATTN + HD * kh
    vc = DATTN + DKV + HD * kh
    kx = jnp.concatenate([zp_ref[:, kc:kc + HD], zc_ref[:, kc:kc + HD]], axis=0)
    vx = jnp.concatenate([zp_ref[:, vc:vc + HD], zc_ref[:, vc:vc + HD]], axis=0)
    qx = jnp.concatenate([zc_ref[:, HD * (GQA * kh + g):HD * (GQA * kh + g + 1)] for g in range(GQA)], axis=0)
    rq = lax.rsqrt(jnp.mean(qx * qx, axis=-1, keepdims=True) + EPS)
    rk = lax.rsqrt(jnp.mean(kx * kx, axis=-1, keepdims=True) + EPS)
    qhat, khat = qx * rq, kx * rk
    qnb, knb = (qhat * qg).astype(bf16), (khat * kg).astype(bf16)
    s = _nt(qnb, knb) * SCALE + bias_scr[GQA * kh:GQA * (kh + 1)].reshape(GQA * BLK, 2 * BLK)
    row = lax.broadcasted_iota(i32, (GQA * BLK, 2 * BLK), 0) & (BLK - 1)
    col = lax.broadcasted_iota(i32, (GQA * BLK, 2 * BLK), 1)
    mask = (col > row) & (col <= row + BLK) & ((col >= BLK) | (n > 0))
    s = jnp.where(mask, s, NEG)
    ridx = lax.broadcasted_iota(i32, (GQA * BLK, 1), 0)
    sink = jnp.full((GQA * BLK, 1), sk_ref[GQA * kh + GQA - 1], f32)
    for g in range(GQA - 2, -1, -1):
        sink = jnp.where(ridx < (g + 1) * BLK, sk_ref[GQA * kh + g], sink)
    m = jnp.maximum(jnp.max(s, axis=-1, keepdims=True), sink)
    e = jnp.exp(s - m)
    es = jnp.exp(sink - m)
    den = jnp.sum(e, axis=-1, keepdims=True) + es
    return dict(p=e / den, psink=es / den, qhat=qhat, khat=khat, rq=rq, rk=rk, qnb=qnb, knb=knb, vb=vx.astype(bf16))


def _pool_group(zc_ref, zp_ref, g, w, n):
    c0 = DATTN + 2 * DKV + PGD * g
    uc = zc_ref[:, c0:c0 + PGD]
    up = jnp.where(n > 0, zp_ref[:, c0:c0 + PGD], 0.0)
    ue = jnp.concatenate([up, uc], axis=0)
    hi = ue.astype(bf16)
    lo = (ue - hi.astype(f32)).astype(bf16)
    t = lax.broadcasted_iota(i32, (BLK, 2 * BLK), 0)
    s = lax.broadcasted_iota(i32, (BLK, 2 * BLK), 1)
    band = jnp.where((s <= t + BLK) & (s > t + BLK - w), 1.0, 0.0).astype(bf16)
    sm = _nn(band, hi) + _nn(band, lo)
    pos = n * BLK + lax.broadcasted_iota(i32, (BLK, 1), 0) + 1
    cnt = jnp.minimum(pos, w).astype(f32)
    return sm / cnt - uc, band, cnt


def _mix_fwd(z, qg, kg, sinks, relb, bucket, pool_w, pscale, name):
    T = z.shape[0]
    nb = T // BLK

    def body(zc_ref, zp_ref, qg_ref, kg_ref, sk_ref, rb_ref, bk_ref, pw_ref, ps_ref, y_ref, bias_scr, yacc):
        n = pl.program_id(0)

        @pl.when(n == 0)
        def _():
            _fill_bias(bk_ref, rb_ref, bias_scr)

        for kh in range(NKV):
            a = _attn_probs(zc_ref, zp_ref, kh, qg_ref[...], kg_ref[...], sk_ref, bias_scr, n)
            o = _nn(a["p"].astype(bf16), a["vb"])
            for g in range(GQA):
                hc = HD * (GQA * kh + g)
                yacc[:, hc:hc + HD] = o[g * BLK:(g + 1) * BLK]
        for g, w in enumerate(POOL_WINDOWS):
            pooled, _, _ = _pool_group(zc_ref, zp_ref, g, w, n)
            yp = _nn(pooled.astype(bf16), pw_ref[g].astype(bf16)) * ps_ref[:, PGD * g:PGD * (g + 1)]
            yacc[:, DATTN + PGD * g:DATTN + PGD * (g + 1)] = yp
        y_ref[...] = yacc[...].astype(bf16)

    full = lambda *shape: pl.BlockSpec(shape, lambda n: (0,) * len(shape))
    smem = pl.BlockSpec(memory_space=pltpu.SMEM)
    return pl.pallas_call(
        body, grid=(nb,),
        in_specs=[pl.BlockSpec((BLK, DIN), lambda n: (n, 0)),
                  pl.BlockSpec((BLK, DIN), lambda n: (jnp.maximum(n - 1, 0), 0)),
                  full(1, HD), full(1, HD), smem, smem, full(BLK, 2 * BLK),
                  full(len(POOL_WINDOWS), PGD, PGD), full(1, DPOOL)],
        out_specs=pl.BlockSpec((BLK, DMIX), lambda n: (n, 0)),
        out_shape=SDS((T, DMIX), bf16),
        scratch_shapes=[pltpu.VMEM((NH, BLK, 2 * BLK), f32), pltpu.VMEM((BLK, DMIX), f32)],
        compiler_params=_cparams(("arbitrary",)), name=name)(z, z, qg, kg, sinks, relb, bucket, pool_w, pscale)


def _mix_bwd(z, dy, qg, kg, sinks, relb, bucket, pool_w, pscale, name):
    T = z.shape[0]
    nb = T // BLK

    def body(zc_ref, zp_ref, dy_ref, qg_ref, kg_ref, sk_ref, rb_ref, bk_ref, pw_ref, ps_ref,
             dz_ref, dqg_ref, dkg_ref, dsk_ref, drb_ref, dpw_ref, dps_ref, bias_scr, dbias_scr):
        n = pl.program_id(0)
        rows = pl.ds(pl.multiple_of(n * BLK, BLK), BLK)
        prow = pl.ds(pl.multiple_of(jnp.maximum(n - 1, 0) * BLK, BLK), BLK)

        @pl.when(n == 0)
        def _():
            _fill_bias(bk_ref, rb_ref, bias_scr)
            dbias_scr[...] = jnp.zeros_like(dbias_scr)
            dqg_ref[...] = jnp.zeros_like(dqg_ref)
            dkg_ref[...] = jnp.zeros_like(dkg_ref)
            dsk_ref[...] = jnp.zeros_like(dsk_ref)
            dpw_ref[...] = jnp.zeros_like(dpw_ref)
            dps_ref[...] = jnp.zeros_like(dps_ref)

        qg, kg = qg_ref[...], kg_ref[...]
        lane = lax.broadcasted_iota(i32, (1, 128), 1)
        dsk = jnp.zeros((1, 128), f32)
        for kh in range(NKV):
            a = _attn_probs(zc_ref, zp_ref, kh, qg, kg, sk_ref, bias_scr, n)
            p = a["p"]
            do = jnp.concatenate([dy_ref[:, HD * (GQA * kh + g):HD * (GQA * kh + g + 1)] for g in range(GQA)],
                                 axis=0).astype(bf16)
            dv = _tn(p.astype(bf16), do)
            dp = _nt(do, a["vb"])
            delta = jnp.sum(p * dp, axis=-1, keepdims=True)
            ds = p * (dp - delta)
            sinkterm = a["psink"] * delta
            for g in range(GQA):
                h = GQA * kh + g
                dbias_scr[h] += ds[g * BLK:(g + 1) * BLK]
                tot = jnp.sum(sinkterm[g * BLK:(g + 1) * BLK], axis=0, keepdims=True)
                dsk = dsk - jnp.where(lane == h, tot, 0.0)
            dsb = ds.astype(bf16)
            dqn = _nn(dsb, a["knb"]) * SCALE
            dkn = _tn(dsb, a["qnb"]) * SCALE
            qhat, khat = a["qhat"], a["khat"]
            dqg_ref[...] += jnp.sum(dqn * qhat, axis=0, keepdims=True)
            dkg_ref[...] += jnp.sum(dkn * khat, axis=0, keepdims=True)
            dqh = dqn * qg
            dq = a["rq"] * (dqh - qhat * jnp.mean(dqh * qhat, axis=-1, keepdims=True))
            dkh = dkn * kg
            dk = a["rk"] * (dkh - khat * jnp.mean(dkh * khat, axis=-1, keepdims=True))
            kc = DATTN + HD * kh
            vc = DATTN + DKV + HD * kh
            for g in range(GQA):
                hc = HD * (GQA * kh + g)
                dz_ref[rows, hc:hc + HD] = dq[g * BLK:(g + 1) * BLK]
            dz_ref[rows, kc:kc + HD] = dk[BLK:2 * BLK]
            dz_ref[rows, vc:vc + HD] = dv[BLK:2 * BLK]

            @pl.when(n > 0)
            def _(dk=dk, dv=dv, kc=kc, vc=vc):
                dz_ref[prow, kc:kc + HD] += dk[0:BLK]
                dz_ref[prow, vc:vc + HD] += dv[0:BLK]

        dsk_ref[...] += dsk

        for g, w in enumerate(POOL_WINDOWS):
            c0 = DATTN + 2 * DKV + PGD * g
            pooled, band, cnt = _pool_group(zc_ref, zp_ref, g, w, n)
            pb = pooled.astype(bf16)
            wb = pw_ref[g].astype(bf16)
            dyp = dy_ref[:, DATTN + PGD * g:DATTN + PGD * (g + 1)]
            ypre = _nn(pb, wb)
            dps_ref[:, PGD * g:PGD * (g + 1)] += jnp.sum(dyp * ypre, axis=0, keepdims=True)
            dyg = (dyp * ps_ref[:, PGD * g:PGD * (g + 1)]).astype(bf16)
            dpw_ref[g] += _tn(pb, dyg)
            dpooled = _nt(dyg, wb)
            dsm = dpooled / cnt
            hi = dsm.astype(bf16)
            lo = (dsm - hi.astype(f32)).astype(bf16)
            due = _tn(band, hi) + _tn(band, lo)
            dz_ref[rows, c0:c0 + PGD] = due[BLK:2 * BLK] - dpooled

            @pl.when(n > 0)
            def _(due=due, c0=c0):
                dz_ref[prow, c0:c0 + PGD] += due[0:BLK]

        @pl.when(n == nb - 1)
        def _():
            bk = bk_ref[...]
            ri = lax.broadcasted_iota(i32, (NBUCK, NH), 0)
            ci = lax.broadcasted_iota(i32, (NBUCK, NH), 1)

            def step(b, acc):
                for h in range(NH):
                    sel = jnp.where(bk == b, dbias_scr[h], 0.0)
                    tot = jnp.sum(jnp.sum(sel, axis=1, keepdims=True), axis=0, keepdims=True)
                    acc = acc + jnp.where((ri == b) & (ci == h), tot, 0.0)
                return acc

            drb_ref[...] = lax.fori_loop(0, NBUCK, step, jnp.zeros((NBUCK, NH), f32))

    full = lambda *shape: pl.BlockSpec(shape, lambda n: (0,) * len(shape))
    smem = pl.BlockSpec(memory_space=pltpu.SMEM)
    npg = len(POOL_WINDOWS)
    return pl.pallas_call(
        body, grid=(nb,),
        in_specs=[pl.BlockSpec((BLK, DIN), lambda n: (n, 0)),
                  pl.BlockSpec((BLK, DIN), lambda n: (jnp.maximum(n - 1, 0), 0)),
                  pl.BlockSpec((BLK, DMIX), lambda n: (n, 0)),
                  full(1, HD), full(1, HD), smem, smem, full(BLK, 2 * BLK), full(npg, PGD, PGD), full(1, DPOOL)],
        out_specs=[full(T, DIN), full(1, HD), full(1, HD), full(1, 128), full(NBUCK, NH),
                   full(npg, PGD, PGD), full(1, DPOOL)],
        out_shape=(SDS((T, DIN), f32), SDS((1, HD), f32), SDS((1, HD), f32), SDS((1, 128), f32),
                   SDS((NBUCK, NH), f32), SDS((npg, PGD, PGD), f32), SDS((1, DPOOL), f32)),
        scratch_shapes=[pltpu.VMEM((NH, BLK, 2 * BLK), f32), pltpu.VMEM((NH, BLK, 2 * BLK), f32)],
        compiler_params=_cparams(("arbitrary",), VMEM_LIMIT_V7X),
        name=name)(z, z, dy, qg, kg, sinks, relb, bucket, pool_w, pscale)


class _LocalWeights:
    def __init__(self, w1, wint, wout, w2):
        self.w1, self.wint, self.wout, self.w2 = w1, wint, wout, w2

    def ffn1(self):
        return self.w1

    def mix(self, after):
        return self.wint, self.wout

    def before_out_proj(self, wout, after):
        return wout

    def ffn2(self, after):
        return self.w2

    def mix_ffn2_grads_ready(self, dwint, dwout, dw2, dh2):
        self.grads_rest = (dwint, dwout, dw2)
        return dh2

    def before_ffn1_bwd(self, dx1b):
        return dx1b


def _local_step(x, target, weights, g1, gm, g3, qg, kg, sinks, relb, pool_w, pscale):
    bucket = jnp.asarray(_t5_bucket_table())
    sk = sinks.reshape(NH)
    w1 = weights.ffn1()
    h1 = _norm_fwd(x, g1, "norm1_fwd")
    x1, gate1, up1 = _ffn_fwd(h1, w1, x, "ffn1_fwd")
    h2 = _norm_fwd(x1, gm, "norm2_fwd")
    wint, wout = weights.mix(h2)
    z = _in_proj_fwd(h2, wint, "in_proj_fwd")
    ymix = _mix_fwd(z, qg, kg, sk, relb, bucket, pool_w, pscale, "mix_fwd")
    wout = weights.before_out_proj(wout, ymix)
    x2 = _out_proj_fwd(ymix, wout, x1, "out_proj_fwd")
    h3 = _norm_fwd(x2, g3, "norm3_fwd")
    w2 = weights.ffn2(h3)
    y, gate2, up2 = _ffn_fwd(h3, w2, x2, "ffn2_fwd")
    dy, dyb, loss_lanes = _loss_grad(y, target, "loss_grad")

    dh3, dw2 = _ffn_bwd(dyb, h3, gate2, up2, w2, "ffn2_bwd")
    dx2, dx2b, dg3 = _norm_bwd(dh3, x2, g3, dy, 1.0, "norm3_bwd")
    dymix, dwout = _out_proj_bwd(dx2b, wout, ymix, "out_proj_bwd")
    dz, dqg, dkg, dsk, drb, dpw, dps = _mix_bwd(z, dymix, qg, kg, sk, relb, bucket, pool_w, pscale, "mix_bwd")
    dh2, dwint = _in_proj_bwd(dz, wint, h2, "in_proj_bwd")
    dh2 = weights.mix_ffn2_grads_ready(dwint, dwout, dw2, dh2)
    dx1, dx1b, dgm = _norm_bwd(dh2, x1, gm, dx2, 0.5, "norm2_bwd")
    dx1b = weights.before_ffn1_bwd(dx1b)
    dh1, dw1 = _ffn_bwd(dx1b, h1, gate1, up1, w1, "ffn1_bwd")
    gx, _, dg1 = _norm_bwd(dh1, x, g1, dx1, 1.0, "norm1_bwd")
    small = dict(ffn1_norm=dg1, mix_norm=dgm, ffn2_norm=dg3, pool_scale=dps, q_norm=dqg, k_norm=dkg,
                 attn_sinks=dsk[:, :NH], rel_bias=drb, pool_w=dpw, loss=loss_lanes)
    return gx, (dw1, dwint, dwout, dw2), small


SMALL_NAMES = ("ffn1_norm", "mix_norm", "ffn2_norm", "pool_scale", "q_norm", "k_norm", "attn_sinks", "rel_bias",
               "pool_w", "loss")
SMALL_SHAPES = dict(ffn1_norm=(1, D), mix_norm=(1, D), ffn2_norm=(1, D), pool_scale=(1, DPOOL), q_norm=(1, HD),
                    k_norm=(1, HD), attn_sinks=(1, NH), rel_bias=(NBUCK, NH),
                    pool_w=(1, len(POOL_WINDOWS), PGD, PGD), loss=(1, 128))


def _small_rows(name):
    return -(-int(np.prod(SMALL_SHAPES[name])) // 128)


SMALL_OFF = {}
_r = 0
for _n in SMALL_NAMES:
    SMALL_OFF[_n] = _r
    _r += _small_rows(_n)
SMALL_ROWS = -(-_r // 8) * 8
LOSS_ROW = SMALL_OFF["loss"]


def _pack_small(vals):
    parts = []
    for n in SMALL_NAMES:
        size = _small_rows(n) * 128
        if n in vals:
            flat = vals[n].astype(f32).reshape(-1)
            parts.append(jnp.pad(flat, (0, size - flat.shape[0])))
        else:
            parts.append(jnp.zeros((size,), f32))
    flat = jnp.concatenate(parts)
    flat = jnp.pad(flat, (0, SMALL_ROWS * 128 - flat.shape[0]))
    return flat.reshape(SMALL_ROWS, 128)


def _unpack_small(packed, name):
    size = int(np.prod(SMALL_SHAPES[name]))
    r0 = SMALL_OFF[name]
    return packed[r0:r0 + _small_rows(name)].reshape(-1)[:size].reshape(SMALL_SHAPES[name])


def _position():
    return lax.axis_index("x"), lax.axis_index("y"), lax.axis_index("c")


def _dev_index(x, y, c):
    return 4 * x + 2 * y + c


G1_PIECES, MIX_PIECES, F2_PIECES = (0, 1, 2), (3, 4), (5, 6, 7)


def _group_rows(pieces):
    return sum(PIECE_ROWS[k] for k in pieces)


def _shard_piece(s_ref, k):
    return s_ref.at[pl.ds(PIECE_OFF[k], PIECE_ROWS[k]), :]


def _shard_group(s_ref, pieces):
    return s_ref.at[pl.ds(PIECE_OFF[pieces[0]], _group_rows(pieces)), :]


def _weight_pieces(w1_ref=None, wi_ref=None, wo_ref=None, w2_ref=None):
    arrs = {}
    if w1_ref is not None:
        arrs.update({0: w1_ref.at[0], 1: w1_ref.at[1], 2: w1_ref.at[2]})
    if wi_ref is not None:
        arrs[3] = wi_ref
    if wo_ref is not None:
        arrs[4] = wo_ref
    if w2_ref is not None:
        arrs.update({5: w2_ref.at[0], 6: w2_ref.at[1], 7: w2_ref.at[2]})
    return arrs


def _block_rows(arrs, k, dev):
    r = PIECE_ROWS[k]
    return arrs[k].at[pl.ds(pl.multiple_of(_dev_index(*dev) * r, 16), r), :]


def _all_gather_ffn1(shard):
    pieces = G1_PIECES

    def body(s_ref, w1_ref, send_sems, recv_sems, local_sem):
        x, y, c = _position()
        me, sib = (x, y, c), (x, y, 1 - c)
        chips = [(1 - x, y), (x, 1 - y), (1 - x, 1 - y)]
        arrs = _weight_pieces(w1_ref=w1_ref)

        def copies(rel, block, to, from_shard):
            return [pltpu.make_async_remote_copy(
                src_ref=_shard_piece(s_ref, k) if from_shard else _block_rows(arrs, k, block),
                dst_ref=_block_rows(arrs, k, block),
                send_sem=send_sems.at[rel], recv_sem=recv_sems.at[rel], device_id=to, device_id_type=MESH)
                for k in pieces]

        def whole(rel):
            grp = _shard_group(s_ref, pieces)
            return pltpu.make_async_remote_copy(src_ref=grp, dst_ref=grp, send_sem=send_sems.at[rel],
                                                recv_sem=recv_sems.at[rel], device_id=me, device_id_type=MESH)

        mine = [pltpu.make_async_copy(_shard_piece(s_ref, k), _block_rows(arrs, k, me), local_sem) for k in pieces]
        for cp in mine:
            cp.start()
        for cp in copies(0, me, sib, True):
            cp.start()
        for j, chip in enumerate(chips):
            for cp in copies(1 + j, me, (*chip, c), True):
                cp.start()
        for j, chip in enumerate(chips):
            whole(1 + j).wait_recv()
            for cp in copies(4 + j, (*chip, c), sib, False):
                cp.start()
        whole(0).wait_recv()
        for j in range(3):
            whole(4 + j).wait_recv()
        for rel in range(7):
            whole(rel).wait_send()
        grp = _shard_group(s_ref, pieces)
        pltpu.make_async_copy(grp, grp, local_sem).wait()

    hbm = pl.BlockSpec(memory_space=pl.ANY)
    return pl.pallas_call(
        body, in_specs=[hbm], out_specs=hbm, out_shape=SDS((3, F, D), bf16),
        scratch_shapes=[pltpu.SemaphoreType.DMA((7,)), pltpu.SemaphoreType.DMA((7,)), pltpu.SemaphoreType.DMA],
        compiler_params=pltpu.CompilerParams(has_side_effects=True),
        name="all_gather_ffn1")(shard)


HBM_SPEC = pl.BlockSpec(memory_space=pltpu.HBM)
SEM_SPEC = pl.BlockSpec(memory_space=pltpu.SEMAPHORE)
ANY_SPEC = pl.BlockSpec(memory_space=pl.ANY)
SPLIT_EFFECT = pltpu.SideEffectType.DATAFLOW_SIDE_EFFECTING


def _in_hbm(a):
    return pltpu.with_memory_space_constraint(a, pltpu.HBM)


def _hbm_like(a):
    return pltpu.HBM(a.shape, a.dtype)


def _place_own_rows(shard):
    pieces = MIX_PIECES + F2_PIECES

    def body(s_ref, wi_ref, wo_ref, w2_ref, buf, sems):
        x, y, c = _position()
        arrs = _weight_pieces(wi_ref=wi_ref, wo_ref=wo_ref, w2_ref=w2_ref)
        grp = _shard_group(s_ref, pieces)
        load = pltpu.make_async_copy(grp, buf, sems.at[0])
        load.start()
        load.wait()
        base = PIECE_OFF[pieces[0]]
        for k in pieces:
            pltpu.make_async_copy(buf.at[pl.ds(PIECE_OFF[k] - base, PIECE_ROWS[k]), :],
                                  _block_rows(arrs, k, (x, y, c)), sems.at[1]).start()
        pltpu.make_async_copy(grp, buf, sems.at[1]).wait()

    return pl.pallas_call(
        body, in_specs=[ANY_SPEC], out_specs=[ANY_SPEC] * 3,
        out_shape=(SDS((DIN, D), bf16), SDS((DMIX, D), bf16), SDS((3, F, D), bf16)),
        scratch_shapes=[pltpu.VMEM((_group_rows(pieces), D), bf16), pltpu.SemaphoreType.DMA((2,))],
        name="place_own_rows")(shard)


def _xor_peer(x, y, c, k):
    return (x ^ (k >> 2), y ^ ((k >> 1) & 1), c ^ (k & 1))


def _gather_rest_start(shard, wi, wo, w2, w1):
    def body(s_ref, wi_ref, wo_ref, w2_ref, w1_ref,
             ssem_m, rsem_m, ssem_f, rsem_f0, rsem_f, s_o, wi_o, wo_o, w2_o, w1_o):
        x, y, c = _position()
        me, sib = (x, y, c), (x, y, 1 - c)
        chips = [(1 - x, y), (x, 1 - y), (1 - x, 1 - y)]
        arrs = _weight_pieces(wi_ref=wi_ref, wo_ref=wo_ref, w2_ref=w2_ref)
        for k in range(1, NDEV):
            for p in MIX_PIECES:
                pltpu.make_async_remote_copy(
                    src_ref=_shard_piece(s_ref, p), dst_ref=_block_rows(arrs, p, me), send_sem=ssem_m.at[k - 1],
                    recv_sem=rsem_m.at[k - 1], device_id=_xor_peer(x, y, c, k), device_id_type=MESH).start()
        for p in F2_PIECES:
            pltpu.make_async_remote_copy(
                src_ref=_shard_piece(s_ref, p), dst_ref=_block_rows(arrs, p, me), send_sem=ssem_f.at[0],
                recv_sem=rsem_f0, device_id=sib, device_id_type=MESH).start()
        for j, chip in enumerate(chips):
            for p in F2_PIECES:
                pltpu.make_async_remote_copy(
                    src_ref=_shard_piece(s_ref, p), dst_ref=_block_rows(arrs, p, me), send_sem=ssem_f.at[1 + j],
                    recv_sem=rsem_f.at[j], device_id=(*chip, c), device_id_type=MESH).start()

    dma = pltpu.SemaphoreType.DMA
    return pl.pallas_call(
        body, name="gather_rest_start",
        out_shape=(dma((7,)), dma((7,)), dma((4,)), dma(()), dma((3,)),
                   _hbm_like(shard), _hbm_like(wi), _hbm_like(wo), _hbm_like(w2), _hbm_like(w1)),
        in_specs=(HBM_SPEC,) * 5, out_specs=(SEM_SPEC,) * 5 + (HBM_SPEC,) * 5,
        input_output_aliases={0: 5, 1: 6, 2: 7, 3: 8, 4: 9},
        compiler_params=pltpu.CompilerParams(has_side_effects=SPLIT_EFFECT),
    )(_in_hbm(shard), _in_hbm(wi), _in_hbm(wo), _in_hbm(w2), _in_hbm(w1))


def _gather_mix_wait(ssem_m, rsem_m, shard, wi, wo, after):
    def body(s_ref, wi_ref, wo_ref, ssem, rsem, after_ref, s_o, wi_o, wo_o):
        x, y, c = _position()
        grp = _shard_group(s_ref, MIX_PIECES)
        for k in range(NDEV - 1):
            d = pltpu.make_async_remote_copy(src_ref=grp, dst_ref=grp, send_sem=ssem.at[k], recv_sem=rsem.at[k],
                                             device_id=(x, y, c), device_id_type=MESH)
            d.wait_recv()
            d.wait_send()

    return pl.pallas_call(
        body, name="gather_mix_wait", out_shape=(_hbm_like(shard), _hbm_like(wi), _hbm_like(wo)),
        in_specs=(HBM_SPEC, HBM_SPEC, HBM_SPEC, SEM_SPEC, SEM_SPEC, ANY_SPEC), out_specs=(HBM_SPEC,) * 3,
        input_output_aliases={0: 0, 1: 1, 2: 2},
        compiler_params=pltpu.CompilerParams(has_side_effects=SPLIT_EFFECT),
    )(shard, wi, wo, ssem_m, rsem_m, after)


def _gather_ffn2_pass_on(rsem_f, w2, wo, after):
    def body(w2_ref, wo_ref, rsem, after_ref, fsend, frecv, w2_o, wo_o):
        x, y, c = _position()
        sib = (x, y, 1 - c)
        chips = [(1 - x, y), (x, 1 - y), (1 - x, 1 - y)]
        arrs = _weight_pieces(w2_ref=w2_ref)
        three = w2_ref.at[0, pl.ds(0, _group_rows(F2_PIECES)), :]
        for j, chip in enumerate(chips):
            pltpu.make_async_remote_copy(src_ref=three, dst_ref=three, send_sem=fsend.at[j], recv_sem=rsem.at[j],
                                         device_id=(x, y, c), device_id_type=MESH).wait_recv()
            for p in F2_PIECES:
                rows = _block_rows(arrs, p, (*chip, c))
                pltpu.make_async_remote_copy(src_ref=rows, dst_ref=rows, send_sem=fsend.at[j], recv_sem=frecv.at[j],
                                             device_id=sib, device_id_type=MESH).start()

    dma = pltpu.SemaphoreType.DMA
    return pl.pallas_call(
        body, name="gather_ffn2_pass_on", out_shape=(dma((3,)), dma((3,)), _hbm_like(w2), _hbm_like(wo)),
        in_specs=(HBM_SPEC, HBM_SPEC, SEM_SPEC, ANY_SPEC), out_specs=(SEM_SPEC, SEM_SPEC, HBM_SPEC, HBM_SPEC),
        input_output_aliases={0: 2, 1: 3},
        compiler_params=pltpu.CompilerParams(has_side_effects=SPLIT_EFFECT),
    )(w2, wo, rsem_f, after)


def _gather_ffn2_wait(ssem_f, rsem_f0, fsend, frecv, shard, w2, after):
    def body(s_ref, w2_ref, ssem, rsem0, fs, fr, after_ref, w2_o):
        x, y, c = _position()
        grp = _shard_group(s_ref, F2_PIECES)

        def waiter(send_sem, recv_sem):
            return pltpu.make_async_remote_copy(src_ref=grp, dst_ref=grp, send_sem=send_sem, recv_sem=recv_sem,
                                                device_id=(x, y, c), device_id_type=MESH)

        waiter(ssem.at[0], rsem0).wait_recv()
        for j in range(3):
            waiter(fs.at[j], fr.at[j]).wait_recv()
        for rel in range(4):
            waiter(ssem.at[rel], rsem0).wait_send()
        for j in range(3):
            waiter(fs.at[j], fr.at[j]).wait_send()

    return pl.pallas_call(
        body, name="gather_ffn2_wait", out_shape=_hbm_like(w2),
        in_specs=(HBM_SPEC, HBM_SPEC, SEM_SPEC, SEM_SPEC, SEM_SPEC, SEM_SPEC, ANY_SPEC), out_specs=HBM_SPEC,
        input_output_aliases={1: 0},
        compiler_params=pltpu.CompilerParams(has_side_effects=SPLIT_EFFECT),
    )(shard, w2, ssem_f, rsem_f0, fsend, frecv, after)


class _GatheredWeights(_LocalWeights):
    def __init__(self, shard):
        w1 = _all_gather_ffn1(shard)
        wi, wo, w2 = _place_own_rows(shard)
        (self.ssem_m, self.rsem_m, self.ssem_f, self.rsem_f0, self.rsem_f,
         self.shard, self.wi, self.wo, self.w2_part, self.w1) = _gather_rest_start(shard, wi, wo, w2, w1)

    def mix(self, after):
        self.shard, wint, wout = _gather_mix_wait(self.ssem_m, self.rsem_m, self.shard, self.wi, self.wo, after)
        return wint, wout

    def before_out_proj(self, wout, after):
        self.fsend, self.frecv, self.w2_part, wout = _gather_ffn2_pass_on(self.rsem_f, self.w2_part, wout, after)
        return wout

    def ffn2(self, after):
        return _gather_ffn2_wait(self.ssem_f, self.rsem_f0, self.fsend, self.frecv, self.shard, self.w2_part, after)

    def mix_ffn2_grads_ready(self, dwint, dwout, dw2, dh2):
        rx1 = lax.empty((4, RSA_ROWS, D), bf16)
        self.sa, self.ra, dwint, dwout, dw2, rx1, dh2 = _rsa_level1_start(dwint, dwout, dw2, rx1, dh2)
        self.level1 = (dwint, dwout, dw2, rx1)
        return dh2

    def before_ffn1_bwd(self, dx1b):
        dwint, dwout, dw2, rx1 = _rsa_level1_wait(self.sa, self.ra, *self.level1, dx1b)
        tx, self.acc = _rsa_chip_sums(dwint, dwout, dw2, rx1)
        rx2 = lax.empty((3, RSA_ROWS, D), bf16)
        self.sb, self.rb, self.tx, self.rx2, dx1b = _rsa_level2_start(tx, rx2, dx1b)
        return dx1b

    def mix_ffn2_grads_total(self, after):
        rx2 = _rsa_level2_wait(self.sb, self.rb, self.tx, self.rx2, after)
        return _rsa_total(self.acc, rx2)


RS_CHUNK = 176


def _reduce_scatter_ffn1(dw1):
    pieces = G1_PIECES
    nrows = _group_rows(pieces)
    nchunk = nrows // RS_CHUNK

    def body(d1_ref, red_ref, rx1_ref, rx2_ref,
             own_buf, rx_buf, tx_buf, acc, sa, ra, sb, rb, lsem):
        x, y, c = _position()
        me, sib = (x, y, c), (x, y, 1 - c)
        rel_chips = [(x, y), (1 - x, y), (x, 1 - y), (1 - x, 1 - y)]
        srcs = _weight_pieces(w1_ref=d1_ref)

        def piece(k, dev):
            r = PIECE_ROWS[k]
            return srcs[k].at[pl.ds(pl.multiple_of(_dev_index(*dev) * r, 16), r), :]

        def packed(ref, k):
            return ref.at[pl.ds(PIECE_OFF[k], PIECE_ROWS[k]), :]

        for j, chip in enumerate(rel_chips):
            for k in pieces:
                pltpu.make_async_remote_copy(
                    src_ref=piece(k, (*chip, 1 - c)), dst_ref=packed(rx1_ref.at[j], k),
                    send_sem=sa.at[j], recv_sem=ra.at[j], device_id=sib, device_id_type=MESH).start()

        def wait_a(j):
            return pltpu.make_async_remote_copy(src_ref=rx1_ref.at[j], dst_ref=rx1_ref.at[j], send_sem=sa.at[j],
                                                recv_sem=ra.at[j], device_id=me, device_id_type=MESH)

        def ici(j):
            return pltpu.make_async_remote_copy(
                src_ref=tx_buf.at[j - 1], dst_ref=rx2_ref.at[j - 1], send_sem=sb.at[j - 1], recv_sem=rb.at[j - 1],
                device_id=(*rel_chips[j], c), device_id_type=MESH)

        for j in (1, 2, 3, 0):
            loads = [pltpu.make_async_copy(piece(k, (*rel_chips[j], c)), packed(own_buf, k), lsem)
                     for k in pieces]
            for cp in loads:
                cp.start()
            wait_a(j).wait_recv()
            got = pltpu.make_async_copy(rx1_ref.at[j], rx_buf, lsem)
            got.start()
            pltpu.make_async_copy(rx_buf, rx_buf, lsem).wait()
            got.wait()

            def add(i, carry, j=j):
                rows = pl.ds(pl.multiple_of(i * RS_CHUNK, 16), RS_CHUNK)
                tot = own_buf[rows, :].astype(f32) + rx_buf[rows, :].astype(f32)
                if j == 0:
                    acc[rows, :] = tot
                else:
                    tx_buf[j - 1, rows, :] = tot.astype(bf16)
                return carry

            lax.fori_loop(0, nchunk, add, 0)
            if j != 0:
                ici(j).start()

        for j in (1, 2, 3):
            ici(j).wait_recv()
            got = pltpu.make_async_copy(rx2_ref.at[j - 1], rx_buf, lsem)
            got.start()
            got.wait()

            def add2(i, carry):
                rows = pl.ds(pl.multiple_of(i * RS_CHUNK, 16), RS_CHUNK)
                acc[rows, :] += rx_buf[rows, :].astype(f32)
                return carry

            lax.fori_loop(0, nchunk, add2, 0)
        out = pltpu.make_async_copy(acc, red_ref, lsem)
        out.start()
        out.wait()
        for j in range(4):
            wait_a(j).wait_send()
        for j in (1, 2, 3):
            ici(j).wait_send()

    hbm = pl.BlockSpec(memory_space=pl.ANY)
    red, _, _ = pl.pallas_call(
        body, in_specs=[hbm], out_specs=[hbm] * 3,
        out_shape=(SDS((nrows, D), f32), SDS((4, nrows, D), bf16), SDS((3, nrows, D), bf16)),
        scratch_shapes=[pltpu.VMEM((nrows, D), bf16), pltpu.VMEM((nrows, D), bf16),
                        pltpu.VMEM((3, nrows, D), bf16), pltpu.VMEM((nrows, D), f32),
                        pltpu.SemaphoreType.DMA((4,)), pltpu.SemaphoreType.DMA((4,)),
                        pltpu.SemaphoreType.DMA((3,)), pltpu.SemaphoreType.DMA((3,)), pltpu.SemaphoreType.DMA],
        compiler_params=pltpu.CompilerParams(has_side_effects=True, vmem_limit_bytes=VMEM_LIMIT_V7X),
        name="reduce_scatter_ffn1")(dw1)
    return red


RSA_PIECES = MIX_PIECES + F2_PIECES
RSA_ROWS = _group_rows(RSA_PIECES)
RSA_OFF = {k: PIECE_OFF[k] - PIECE_OFF[RSA_PIECES[0]] for k in RSA_PIECES}
RSA_BLOCK = 192


def _rsa_rows(ref, k):
    return ref.at[pl.ds(RSA_OFF[k], PIECE_ROWS[k]), :]


def _rsa_level1_start(dwint, dwout, dw2, rx1, thru):
    def body(di_ref, do_ref, d2_ref, rx1_ref, thru_ref, sa, ra, di_o, do_o, d2_o, rx1_o, thru_o):
        x, y, c = _position()
        srcs = _weight_pieces(wi_ref=di_ref, wo_ref=do_ref, w2_ref=d2_ref)
        for j, chip in enumerate([(x, y), (1 - x, y), (x, 1 - y), (1 - x, 1 - y)]):
            for k in RSA_PIECES:
                pltpu.make_async_remote_copy(
                    src_ref=_block_rows(srcs, k, (*chip, 1 - c)), dst_ref=_rsa_rows(rx1_ref.at[j], k),
                    send_sem=sa.at[j], recv_sem=ra.at[j], device_id=(x, y, 1 - c), device_id_type=MESH).start()

    dma = pltpu.SemaphoreType.DMA
    arrs = (dwint, dwout, dw2, rx1, thru)
    return pl.pallas_call(
        body, name="rsa_level1_start", out_shape=(dma((4,)), dma((4,))) + tuple(_hbm_like(a) for a in arrs),
        in_specs=(HBM_SPEC,) * 5, out_specs=(SEM_SPEC,) * 2 + (HBM_SPEC,) * 5,
        input_output_aliases={0: 2, 1: 3, 2: 4, 3: 5, 4: 6},
        compiler_params=pltpu.CompilerParams(has_side_effects=SPLIT_EFFECT),
    )(*[_in_hbm(a) for a in arrs])


def _rsa_level1_wait(sa, ra, dwint, dwout, dw2, rx1, after):
    def body(di_ref, do_ref, d2_ref, rx1_ref, sa_ref, ra_ref, after_ref, di_o, do_o, d2_o, rx1_o):
        x, y, c = _position()
        for j in range(4):
            d = pltpu.make_async_remote_copy(src_ref=rx1_ref.at[j], dst_ref=rx1_ref.at[j], send_sem=sa_ref.at[j],
                                             recv_sem=ra_ref.at[j], device_id=(x, y, c), device_id_type=MESH)
            d.wait_recv()
            d.wait_send()

    arrs = (dwint, dwout, dw2, rx1)
    return pl.pallas_call(
        body, name="rsa_level1_wait", out_shape=tuple(_hbm_like(a) for a in arrs),
        in_specs=(HBM_SPEC,) * 4 + (SEM_SPEC, SEM_SPEC, ANY_SPEC), out_specs=(HBM_SPEC,) * 4,
        input_output_aliases={0: 0, 1: 1, 2: 2, 3: 3},
        compiler_params=pltpu.CompilerParams(has_side_effects=SPLIT_EFFECT),
    )(*arrs, sa, ra, after)


def _rsa_chip_sums(dwint, dwout, dw2, rx1):
    nblk = RSA_ROWS // RSA_BLOCK

    def body(di_ref, do_ref, d2_ref, rx1_ref, tx_ref, acc_ref, own_buf, rx_buf, tx_buf, acc_buf, lsems):
        x, y, c = _position()
        srcs = _weight_pieces(wi_ref=di_ref, wo_ref=do_ref, w2_ref=d2_ref)
        for j, chip in enumerate([(x, y), (1 - x, y), (x, 1 - y), (1 - x, 1 - y)]):
            loads = [pltpu.make_async_copy(_block_rows(srcs, k, (*chip, c)), _rsa_rows(own_buf, k), lsems.at[0])
                     for k in RSA_PIECES]
            got = pltpu.make_async_copy(rx1_ref.at[j], rx_buf, lsems.at[1])
            for cp in loads + [got]:
                cp.start()
            pltpu.make_async_copy(rx_buf, rx_buf, lsems.at[0]).wait()
            got.wait()

            def add(i, carry, j=j):
                rows = pl.ds(pl.multiple_of(i * RSA_BLOCK, 16), RSA_BLOCK)
                tot = own_buf[rows, :].astype(f32) + rx_buf[rows, :].astype(f32)
                if j == 0:
                    acc_buf[rows, :] = tot
                else:
                    tx_buf[rows, :] = tot.astype(bf16)
                return carry

            lax.fori_loop(0, nblk, add, 0)
            out = (pltpu.make_async_copy(acc_buf, acc_ref, lsems.at[2]) if j == 0
                   else pltpu.make_async_copy(tx_buf, tx_ref.at[j - 1], lsems.at[2]))
            out.start()
            out.wait()

    return pl.pallas_call(
        body, in_specs=[ANY_SPEC] * 4, out_specs=[ANY_SPEC] * 2,
        out_shape=(SDS((3, RSA_ROWS, D), bf16), SDS((RSA_ROWS, D), f32)),
        scratch_shapes=[pltpu.VMEM((RSA_ROWS, D), bf16), pltpu.VMEM((RSA_ROWS, D), bf16),
                        pltpu.VMEM((RSA_ROWS, D), bf16), pltpu.VMEM((RSA_ROWS, D), f32),
                        pltpu.SemaphoreType.DMA((3,))],
        compiler_params=_cparams(None, VMEM_LIMIT_V7X), name="rsa_chip_sums")(dwint, dwout, dw2, rx1)


def _rsa_level2_start(tx, rx2, thru):
    def body(tx_ref, rx2_ref, thru_ref, sb, rb, tx_o, rx2_o, thru_o):
        x, y, c = _position()
        for j, chip in enumerate([(1 - x, y), (x, 1 - y), (1 - x, 1 - y)]):
            pltpu.make_async_remote_copy(src_ref=tx_ref.at[j], dst_ref=rx2_ref.at[j], send_sem=sb.at[j],
                                         recv_sem=rb.at[j], device_id=(*chip, c), device_id_type=MESH).start()

    dma = pltpu.SemaphoreType.DMA
    arrs = (tx, rx2, thru)
    return pl.pallas_call(
        body, name="rsa_level2_start", out_shape=(dma((3,)), dma((3,))) + tuple(_hbm_like(a) for a in arrs),
        in_specs=(HBM_SPEC,) * 3, out_specs=(SEM_SPEC,) * 2 + (HBM_SPEC,) * 3,
        input_output_aliases={0: 2, 1: 3, 2: 4},
        compiler_params=pltpu.CompilerParams(has_side_effects=SPLIT_EFFECT),
    )(*[_in_hbm(a) for a in arrs])


def _rsa_level2_wait(sb, rb, tx, rx2, after):
    def body(tx_ref, rx2_ref, sb_ref, rb_ref, after_ref, rx2_o):
        x, y, c = _position()
        for j in range(3):
            d = pltpu.make_async_remote_copy(src_ref=tx_ref.at[j], dst_ref=rx2_ref.at[j], send_sem=sb_ref.at[j],
                                             recv_sem=rb_ref.at[j], device_id=(x, y, c), device_id_type=MESH)
            d.wait_recv()
            d.wait_send()

    return pl.pallas_call(
        body, name="rsa_level2_wait", out_shape=_hbm_like(rx2),
        in_specs=(HBM_SPEC, HBM_SPEC, SEM_SPEC, SEM_SPEC, ANY_SPEC), out_specs=HBM_SPEC,
        input_output_aliases={1: 0},
        compiler_params=pltpu.CompilerParams(has_side_effects=SPLIT_EFFECT),
    )(tx, rx2, sb, rb, after)


def _rsa_total(acc, rx2):
    def body(a_ref, r_ref, o_ref):
        o_ref[...] = ((a_ref[...] + r_ref[0].astype(f32)) + r_ref[1].astype(f32)) + r_ref[2].astype(f32)

    return pl.pallas_call(
        body, grid=(RSA_ROWS // RSA_BLOCK,),
        in_specs=[pl.BlockSpec((RSA_BLOCK, D), lambda i: (i, 0)), pl.BlockSpec((3, RSA_BLOCK, D), lambda i: (0, i, 0))],
        out_specs=pl.BlockSpec((RSA_BLOCK, D), lambda i: (i, 0)),
        out_shape=SDS((RSA_ROWS, D), f32), name="rsa_total")(acc, rx2)


def _all_reduce_small(packed):
    def body(p_ref, o_ref, pair, chips, send_sems, recv_sems):
        x, y, c = _position()
        chip = 2 * x + y
        pair[c] = p_ref[...]
        swap = pltpu.make_async_remote_copy(
            src_ref=p_ref, dst_ref=pair.at[c], send_sem=send_sems.at[0], recv_sem=recv_sems.at[0],
            device_id=(x, y, 1 - c), device_id_type=MESH)
        swap.start()
        swap.wait_recv()
        chips[chip] = pair[0] + pair[1]
        cps = [pltpu.make_async_remote_copy(
            src_ref=chips.at[chip], dst_ref=chips.at[chip], send_sem=send_sems.at[1 + j], recv_sem=recv_sems.at[1 + j],
            device_id=(*other, c), device_id_type=MESH)
            for j, other in enumerate([(1 - x, y), (x, 1 - y), (1 - x, 1 - y)])]
        for cp in cps:
            cp.start()
        for cp in cps:
            cp.wait_recv()
        tot = (chips[0] + chips[1]) + (chips[2] + chips[3])
        o_ref[...] = tot
        loss = jnp.sum(tot[LOSS_ROW:LOSS_ROW + 1, :], axis=-1, keepdims=True)
        o_ref[LOSS_ROW:LOSS_ROW + 1, :] = jnp.broadcast_to(loss, (1, 128))
        swap.wait_send()
        for cp in cps:
            cp.wait_send()

    vm = pl.BlockSpec(memory_space=pltpu.VMEM)
    return pl.pallas_call(
        body, in_specs=[vm], out_specs=vm, out_shape=SDS((SMALL_ROWS, 128), f32),
        scratch_shapes=[pltpu.VMEM((2, SMALL_ROWS, 128), f32), pltpu.VMEM((4, SMALL_ROWS, 128), f32),
                        pltpu.SemaphoreType.DMA((4,)), pltpu.SemaphoreType.DMA((4,))],
        compiler_params=pltpu.CompilerParams(has_side_effects=True),
        name="all_reduce_small")(packed)


def _adamw_math(w, g, m, v):
    m = ADAM_B1 * m + (1.0 - ADAM_B1) * g
    v = ADAM_B2 * v + (1.0 - ADAM_B2) * (g * g)
    m_hat = m / (1.0 - ADAM_B1 ** ADAM_STEP)
    v_hat = v / (1.0 - ADAM_B2 ** ADAM_STEP)
    delta = -ADAM_LR * (m_hat / (jnp.sqrt(v_hat) + ADAM_EPS) + ADAM_WD * w)
    return delta, m, v


def _adamw_big(ws, ms, vs, red1, red_rest):
    npiece = len(BIG)
    rmax = max(PIECE_ROWS)

    def body(*refs):
        ins = (refs[0:npiece], refs[npiece:2 * npiece], refs[2 * npiece:3 * npiece])
        red1_ref, rest_ref = refs[3 * npiece:3 * npiece + 2]
        out_refs = refs[3 * npiece + 2:7 * npiece + 2]
        inb, outb, in_sems, out_sems = refs[7 * npiece + 2:]

        def grad_rows(k):
            if k in G1_PIECES:
                return red1_ref.at[pl.ds(PIECE_OFF[k], PIECE_ROWS[k]), :]
            return _rsa_rows(rest_ref, k)

        def loads(k):
            s, r = k % 2, PIECE_ROWS[k]
            cps = [pltpu.make_async_copy(ins[q][k].at[0], inb.at[s, q, pl.ds(0, r), :], in_sems.at[4 * s + q])
                   for q in range(3)]
            cps.append(pltpu.make_async_copy(grad_rows(k), inb.at[s, 3, pl.ds(0, r), :], in_sems.at[4 * s + 3]))
            return cps

        def stores(k):
            s, r = k % 2, PIECE_ROWS[k]
            return [pltpu.make_async_copy(outb.at[s, q, pl.ds(0, r), :], out_refs[q * npiece + k].at[0],
                                          out_sems.at[4 * s + q]) for q in range(4)]

        for cp in loads(0):
            cp.start()
        for k in range(npiece):
            s, r = k % 2, PIECE_ROWS[k]
            if k + 1 < npiece:
                for cp in loads(k + 1):
                    cp.start()
            for cp in loads(k):
                cp.wait()
            if k >= 2:
                for cp in stores(k - 2):
                    cp.wait()
            g = inb[s, 3, 0:r, :]
            d, nm, nv = _adamw_math(inb[s, 0, 0:r, :], g, inb[s, 1, 0:r, :], inb[s, 2, 0:r, :])
            outb[s, 0, 0:r, :] = g
            outb[s, 1, 0:r, :] = d
            outb[s, 2, 0:r, :] = nm
            outb[s, 3, 0:r, :] = nv
            for cp in stores(k):
                cp.start()
        for k in (npiece - 2, npiece - 1):
            for cp in stores(k):
                cp.wait()

    hbm = pl.BlockSpec(memory_space=pl.ANY)
    outs = pl.pallas_call(
        body, in_specs=[hbm] * (3 * npiece + 2), out_specs=[hbm] * (4 * npiece),
        out_shape=tuple(SDS(w.shape, f32) for _ in range(4) for w in ws),
        scratch_shapes=[pltpu.VMEM((2, 4, rmax, D), f32), pltpu.VMEM((2, 4, rmax, D), f32),
                        pltpu.SemaphoreType.DMA((8,)), pltpu.SemaphoreType.DMA((8,))],
        compiler_params=_cparams(None, VMEM_LIMIT_V7X), name="adamw_big")(*ws, *ms, *vs, red1, red_rest)
    return [list(outs[q * npiece:(q + 1) * npiece]) for q in range(4)]


def _adamw_small(w, m, v, g, name):
    def body(w_ref, m_ref, v_ref, g_ref, d_ref, nm_ref, nv_ref):
        d, nm, nv = _adamw_math(w_ref[...], g_ref[...], m_ref[...], v_ref[...])
        d_ref[...] = d
        nm_ref[...] = nm
        nv_ref[...] = nv

    return pl.pallas_call(
        body, out_shape=tuple(SDS(w.shape, f32) for _ in range(3)), name=name)(w, m, v, g)


WEIGHTS = ("ffn1_norm", "ffn1_w_gate", "ffn1_w_up", "ffn1_w_down", "mix_norm", "w_in", "q_norm", "k_norm",
           "attn_sinks", "rel_bias", "pool_w", "pool_scale", "w_out", "ffn2_norm", "ffn2_w_gate", "ffn2_w_up",
           "ffn2_w_down")
BIG = (("ffn1_w_gate", True), ("ffn1_w_up", True), ("ffn1_w_down", False), ("w_in", True), ("w_out", False),
       ("ffn2_w_gate", True), ("ffn2_w_up", True), ("ffn2_w_down", False))


def kernel(x, ffn1_norm, ffn1_w_gate, ffn1_w_up, ffn1_w_down, mix_norm, w_in, q_norm, k_norm, attn_sinks, rel_bias, pool_w, pool_scale, w_out, ffn2_norm, ffn2_w_gate, ffn2_w_up, ffn2_w_down, loss_target, m_ffn1_norm, m_ffn1_w_gate, m_ffn1_w_up, m_ffn1_w_down, m_mix_norm, m_w_in, m_q_norm, m_k_norm, m_attn_sinks, m_rel_bias, m_pool_w, m_pool_scale, m_w_out, m_ffn2_norm, m_ffn2_w_gate, m_ffn2_w_up, m_ffn2_w_down, v_ffn1_norm, v_ffn1_w_gate, v_ffn1_w_up, v_ffn1_w_down, v_mix_norm, v_w_in, v_q_norm, v_k_norm, v_attn_sinks, v_rel_bias, v_pool_w, v_pool_scale, v_w_out, v_ffn2_norm, v_ffn2_w_gate, v_ffn2_w_up, v_ffn2_w_down):
    args = dict(locals())
    w = {n: args[n] for n in WEIGHTS}
    m = {n: args["m_" + n] for n in WEIGHTS}
    v = {n: args["v_" + n] for n in WEIGHTS}

    as_rows = lambda a, tr: jnp.swapaxes(a, 1, 2) if tr else a
    shard = jnp.concatenate([as_rows(w[n], tr)[0].astype(bf16) for n, tr in BIG], axis=0)
    exchanges = _GatheredWeights(shard)
    gx, (dw1, _, _, _), small = _local_step(
        x[0], loss_target[0], exchanges, ffn1_norm, mix_norm, ffn2_norm, q_norm, k_norm, attn_sinks,
        rel_bias, pool_w[0], pool_scale)

    red1 = _reduce_scatter_ffn1(dw1)
    red_rest = exchanges.mix_ffn2_grads_total(red1)
    small_tot = _all_reduce_small(_pack_small(small))

    grads, deltas, new_m, new_v = {}, {}, {}, {}
    big_out = _adamw_big(*[[as_rows(t[n], tr) for n, tr in BIG] for t in (w, m, v)], red1, red_rest)
    for k, (n, tr) in enumerate(BIG):
        grads[n], deltas[n], new_m[n], new_v[n] = [as_rows(o[k], tr) for o in big_out]
    small_names = [n for n in SMALL_NAMES if n != "loss"]
    ds, nms, nvs = _adamw_small(_pack_small({n: w[n] for n in small_names}), _pack_small({n: m[n] for n in small_names}),
                                _pack_small({n: v[n] for n in small_names}), small_tot, "adamw_small")
    for n in small_names:
        grads[n] = _unpack_small(small_tot, n)
        deltas[n], new_m[n], new_v[n] = _unpack_small(ds, n), _unpack_small(nms, n), _unpack_small(nvs, n)
    loss = small_tot[LOSS_ROW, 0]
    return (loss, gx[None], *[grads[n] for n in WEIGHTS], *[deltas[n] for n in WEIGHTS],
            *[new_m[n] for n in WEIGHTS], *[new_v[n] for n in WEIGHTS])
```

```python
import functools

import jax
import jax.numpy as jnp
import numpy as np
from jax import lax
from jax.experimental import pallas as pl
from jax.experimental.pallas import tpu as pltpu

f32, bf16, i32 = jnp.float32, jnp.bfloat16, jnp.int32
SDS = jax.ShapeDtypeStruct

D = 1024
F = 2816
HD = 64
NH = 8
NKV = 2
GQA = NH // NKV
DATTN = NH * HD
DKV = NKV * HD
DPOOL = 512
POOL_WINDOWS = (2, 4, 8, 16)
PGD = DPOOL // len(POOL_WINDOWS)
DIN = DATTN + 2 * DKV + DPOOL
DMIX = DATTN + DPOOL
BLK = 128
NBUCK = 32
MAX_DISTANCE = 128
EPS = 1e-6
NEG = -1e30
SCALE = HD ** -0.5

ADAM_LR, ADAM_B1, ADAM_B2, ADAM_EPS, ADAM_WD, ADAM_STEP = 0.001, 0.9, 0.999, 1e-08, 0.01, 10

NDEV = 8
FS = F // NDEV
INS = DIN // NDEV
OUTS = DMIX // NDEV
PIECE_ROWS = (FS, FS, FS, INS, OUTS, FS, FS, FS)
PIECE_OFF = tuple(int(v) for v in np.cumsum((0,) + PIECE_ROWS[:-1]))
PACK_ROWS = sum(PIECE_ROWS)

VMEM_LIMIT_V7X = 56 * 1024 * 1024

MESH = pl.DeviceIdType.MESH


def _cparams(sem=None, vmem=None):
    return pltpu.CompilerParams(dimension_semantics=sem, vmem_limit_bytes=vmem)


def _nt(a, b):
    return lax.dot_general(a, b, (((1,), (1,)), ((), ())), preferred_element_type=f32)


def _tn(a, b):
    return lax.dot_general(a, b, (((0,), (0,)), ((), ())), preferred_element_type=f32)


def _nn(a, b):
    return jnp.dot(a, b, preferred_element_type=f32)


def _sigmoid(x):
    return 1.0 / (1.0 + jnp.exp(-x))


def _norm_fwd(x, g, name):
    T = x.shape[0]
    tm = min(512, T)

    def body(x_ref, g_ref, h_ref):
        xv = x_ref[...]
        r = lax.rsqrt(jnp.mean(xv * xv, axis=-1, keepdims=True) + EPS)
        h_ref[...] = (xv * r * g_ref[...]).astype(bf16)

    return pl.pallas_call(
        body, grid=(T // tm,),
        in_specs=[pl.BlockSpec((tm, D), lambda i: (i, 0)), pl.BlockSpec((1, D), lambda i: (0, 0))],
        out_specs=pl.BlockSpec((tm, D), lambda i: (i, 0)),
        out_shape=SDS((T, D), bf16), name=name)(x, g)


def _norm_bwd(dh, x, g, dres, out_scale, name):
    T = x.shape[0]
    tm = min(512, T)

    def body(dh_ref, x_ref, g_ref, dr_ref, dx_ref, dxb_ref, dg_ref):
        i = pl.program_id(0)
        xv = x_ref[...]
        r = lax.rsqrt(jnp.mean(xv * xv, axis=-1, keepdims=True) + EPS)
        xh = xv * r
        dhv = dh_ref[...]
        dxh = dhv * g_ref[...]
        dx = dr_ref[...] + r * (dxh - xh * jnp.mean(dxh * xh, axis=-1, keepdims=True))
        dx_ref[...] = dx
        dxb_ref[...] = (out_scale * dx).astype(bf16)
        dg = jnp.sum(dhv * xh, axis=0, keepdims=True)

        @pl.when(i == 0)
        def _():
            dg_ref[...] = dg

        @pl.when(i > 0)
        def _():
            dg_ref[...] += dg

    tok = pl.BlockSpec((tm, D), lambda i: (i, 0))
    vec = pl.BlockSpec((1, D), lambda i: (0, 0))
    return pl.pallas_call(
        body, grid=(T // tm,),
        in_specs=[tok, tok, vec, tok], out_specs=[tok, tok, vec],
        out_shape=(SDS((T, D), f32), SDS((T, D), bf16), SDS((1, D), f32)),
        compiler_params=_cparams(("arbitrary",)), name=name)(dh, x, g, dres)


FFN_ROW_CHUNK = 256


def _ffn_tiles(T):
    return min(1024, T), 256


def _ffn_fwd(h, w, x, name):
    T = h.shape[0]
    tm, tf = _ffn_tiles(T)
    nf = F // tf

    def body(h_ref, w_ref, x_ref, xo_ref, g_ref, u_ref, acc):
        fi = pl.program_id(1)

        @pl.when(fi == 0)
        def _():
            acc[...] = jnp.zeros_like(acc)

        gu = _nt(h_ref[...], w_ref[0:2].reshape(2 * tf, D))
        gate, up = gu[:, :tf], gu[:, tf:]
        act = gate * _sigmoid(gate) * up
        g_ref[...] = gate.astype(bf16)
        u_ref[...] = up.astype(bf16)
        acc[...] += _nn(act.astype(bf16), w_ref[2])

        @pl.when(fi == nf - 1)
        def _():
            xo_ref[...] = x_ref[...] + 0.5 * acc[...]

    tok = pl.BlockSpec((tm, D), lambda i, f: (i, 0))
    act_spec = pl.BlockSpec((tm, tf), lambda i, f: (i, f))
    return pl.pallas_call(
        body, grid=(T // tm, nf),
        in_specs=[tok, pl.BlockSpec((3, tf, D), lambda i, f: (0, f, 0)), tok],
        out_specs=[tok, act_spec, act_spec],
        out_shape=(SDS((T, D), f32), SDS((T, F), bf16), SDS((T, F), bf16)),
        scratch_shapes=[pltpu.VMEM((tm, D), f32)],
        compiler_params=_cparams(("arbitrary", "arbitrary"), VMEM_LIMIT_V7X), name=name)(h, w, x)


def _ffn_bwd(dob, h, gate, up, w, name):
    T = h.shape[0]
    tm, tf = _ffn_tiles(T)
    nf, nt = F // tf, T // tm

    def body(do_ref, h_ref, g_ref, u_ref, w_ref, dh_ref, dw_ref, dwacc):
        fi, ti = pl.program_id(0), pl.program_id(1)
        wgu = w_ref[0:2].reshape(2 * tf, D)

        @pl.when(fi == 0)
        def _():
            dh_ref[pl.ds(pl.multiple_of(ti * tm, tm), tm), :] = jnp.zeros((tm, D), f32)

        @pl.when(ti == 0)
        def _():
            dwacc[...] = jnp.zeros_like(dwacc)

        for r in range(0, tm, FFN_ROW_CHUNK):
            rows = slice(r, r + FFN_ROW_CHUNK)
            dov = do_ref[rows, :]
            gv = g_ref[rows, :].astype(f32)
            uv = u_ref[rows, :].astype(f32)
            sg = _sigmoid(gv)
            sil = gv * sg
            dact = _nt(dov, w_ref[2])
            dup = dact * sil
            dgate = dact * uv * (sg * (1.0 + gv * (1.0 - sg)))
            dgu = jnp.concatenate([dgate.astype(bf16), dup.astype(bf16)], axis=1)
            actb = (sil * uv).astype(bf16)
            dh_ref[pl.ds(pl.multiple_of(ti * tm + r, FFN_ROW_CHUNK), FFN_ROW_CHUNK), :] += _nn(dgu, wgu)
            dwacc[0:2 * tf, :] += _tn(dgu, h_ref[rows, :])
            dwacc[2 * tf:3 * tf, :] += _tn(actb, dov)

        @pl.when(ti == nt - 1)
        def _():
            dw_ref[...] = dwacc[...].reshape(3, tf, D).astype(bf16)

    tok = pl.BlockSpec((tm, D), lambda f, i: (i, 0))
    act_spec = pl.BlockSpec((tm, tf), lambda f, i: (i, f))
    wspec = pl.BlockSpec((3, tf, D), lambda f, i: (0, f, 0))
    return pl.pallas_call(
        body, grid=(nf, nt),
        in_specs=[tok, tok, act_spec, act_spec, wspec],
        out_specs=[pl.BlockSpec((T, D), lambda f, i: (0, 0)), wspec],
        out_shape=(SDS((T, D), f32), SDS((3, F, D), bf16)),
        scratch_shapes=[pltpu.VMEM((3 * tf, D), f32)],
        compiler_params=_cparams(("arbitrary", "arbitrary"), VMEM_LIMIT_V7X), name=name)(dob, h, gate, up, w)


def _loss_grad(y, target, name):
    T = y.shape[0]
    tm = min(512, T)

    def body(y_ref, t_ref, dy_ref, dyb_ref, l_ref):
        i = pl.program_id(0)
        e = y_ref[...] - t_ref[...]
        dy = e * (1.0 / D)
        dy_ref[...] = dy
        dyb_ref[...] = (0.5 * dy).astype(bf16)
        col = jnp.sum(e * e, axis=0, keepdims=True) * (0.5 / D)
        lanes = col[:, 0:128]
        for k in range(1, D // 128):
            lanes = lanes + col[:, 128 * k:128 * (k + 1)]

        @pl.when(i == 0)
        def _():
            l_ref[...] = lanes

        @pl.when(i > 0)
        def _():
            l_ref[...] += lanes

    tok = pl.BlockSpec((tm, D), lambda i: (i, 0))
    return pl.pallas_call(
        body, grid=(T // tm,), in_specs=[tok, tok],
        out_specs=[tok, tok, pl.BlockSpec((1, 128), lambda i: (0, 0))],
        out_shape=(SDS((T, D), f32), SDS((T, D), bf16), SDS((1, 128), f32)),
        compiler_params=_cparams(("arbitrary",)), name=name)(y, target)


def _in_proj_fwd(h, wint, name):
    T = h.shape[0]
    tm = min(512, T)

    def body(h_ref, w_ref, z_ref):
        z_ref[...] = _nt(h_ref[...], w_ref[...])

    return pl.pallas_call(
        body, grid=(T // tm,),
        in_specs=[pl.BlockSpec((tm, D), lambda i: (i, 0)), pl.BlockSpec((DIN, D), lambda i: (0, 0))],
        out_specs=pl.BlockSpec((tm, DIN), lambda i: (i, 0)),
        out_shape=SDS((T, DIN), f32), name=name)(h, wint)


def _in_proj_bwd(dz, wint, h, name):
    T = h.shape[0]
    tm = min(512, T)
    nt = T // tm

    def body(dz_ref, w_ref, h_ref, dh_ref, dw_ref, acc):
        i = pl.program_id(0)
        dzb = dz_ref[...].astype(bf16)
        dh_ref[...] = _nn(dzb, w_ref[...])
        part = _tn(dzb, h_ref[...])

        @pl.when(i == 0)
        def _():
            acc[...] = part

        @pl.when(i > 0)
        def _():
            acc[...] += part

        @pl.when(i == nt - 1)
        def _():
            dw_ref[...] = acc[...].astype(bf16)

    wspec = pl.BlockSpec((DIN, D), lambda i: (0, 0))
    return pl.pallas_call(
        body, grid=(nt,),
        in_specs=[pl.BlockSpec((tm, DIN), lambda i: (i, 0)), wspec, pl.BlockSpec((tm, D), lambda i: (i, 0))],
        out_specs=[pl.BlockSpec((tm, D), lambda i: (i, 0)), wspec],
        out_shape=(SDS((T, D), f32), SDS((DIN, D), bf16)),
        scratch_shapes=[pltpu.VMEM((DIN, D), f32)],
        compiler_params=_cparams(("arbitrary",)), name=name)(dz, wint, h)


def _out_proj_fwd(ymix, wout, x, name):
    T = x.shape[0]
    tm = min(512, T)

    def body(y_ref, w_ref, x_ref, o_ref):
        o_ref[...] = x_ref[...] + _nn(y_ref[...], w_ref[...])

    tok = pl.BlockSpec((tm, D), lambda i: (i, 0))
    return pl.pallas_call(
        body, grid=(T // tm,),
        in_specs=[pl.BlockSpec((tm, DMIX), lambda i: (i, 0)), pl.BlockSpec((DMIX, D), lambda i: (0, 0)), tok],
        out_specs=tok, out_shape=SDS((T, D), f32), name=name)(ymix, wout, x)


def _out_proj_bwd(dxb, wout, ymix, name):
    T = dxb.shape[0]
    tm = min(512, T)
    nt = T // tm

    def body(dx_ref, w_ref, y_ref, dy_ref, dw_ref, acc):
        i = pl.program_id(0)
        dxv = dx_ref[...]
        dy_ref[...] = _nt(dxv, w_ref[...])
        part = _tn(y_ref[...], dxv)

        @pl.when(i == 0)
        def _():
            acc[...] = part

        @pl.when(i > 0)
        def _():
            acc[...] += part

        @pl.when(i == nt - 1)
        def _():
            dw_ref[...] = acc[...].astype(bf16)

    wspec = pl.BlockSpec((DMIX, D), lambda i: (0, 0))
    return pl.pallas_call(
        body, grid=(nt,),
        in_specs=[pl.BlockSpec((tm, D), lambda i: (i, 0)), wspec, pl.BlockSpec((tm, DMIX), lambda i: (i, 0))],
        out_specs=[pl.BlockSpec((tm, DMIX), lambda i: (i, 0)), wspec],
        out_shape=(SDS((T, DMIX), f32), SDS((DMIX, D), bf16)),
        scratch_shapes=[pltpu.VMEM((DMIX, D), f32)],
        compiler_params=_cparams(("arbitrary",)), name=name)(dxb, wout, ymix)


def _t5_bucket_table():
    ql = np.arange(BLK)[:, None]
    kl = np.arange(2 * BLK)[None, :]
    n = np.maximum(ql + BLK - kl, 0)
    max_exact = NBUCK // 2
    large = max_exact + (np.log(np.maximum(n, 1) / max_exact) / np.log(MAX_DISTANCE / max_exact)
                         * (NBUCK - max_exact)).astype(np.int32)
    large = np.minimum(large, NBUCK - 1)
    return np.where(n < max_exact, n, large).astype(np.int32)


def _fill_bias(bk_ref, rb_ref, bias_scr):
    bk = bk_ref[...]
    for h in range(NH):
        def step(b, acc, h=h):
            return acc + jnp.where(bk == b, rb_ref[b, h], 0.0)
        bias_scr[h] = lax.fori_loop(0, NBUCK, step, jnp.zeros((BLK, 2 * BLK), f32))


MIX_SUB = 2


class _Window:
    def __init__(self, zc_ref, zp_ref, n, s):
        self.blk = n * MIX_SUB + s
        self.first_in_step = s == 0
        self.cur = lambda a, b: zc_ref[s * BLK:(s + 1) * BLK, a:b]
        self.prev = (lambda a, b: zp_ref[:, a:b]) if s == 0 else (lambda a, b: zc_ref[(s - 1) * BLK:s * BLK, a:b])


def _attn_probs(win, kh, qg, kg, sk_ref, bias_scr):
    n = win.blk
    kc = DATTN + HD * kh
    vc = DATTN + DKV + HD * kh
    kx = jnp.concatenate([win.prev(kc, kc + HD), win.cur(kc, kc + HD)], axis=0)
    vx = jnp.concatenate([win.prev(vc, vc + HD), win.cur(vc, vc + HD)], axis=0)
    qx = jnp.concatenate([win.cur(HD * (GQA * kh + g), HD * (GQA * kh + g + 1)) for g in range(GQA)], axis=0)
    rq = lax.rsqrt(jnp.mean(qx * qx, axis=-1, keepdims=True) + EPS)
    rk = lax.rsqrt(jnp.mean(kx * kx, axis=-1, keepdims=True) + EPS)
    qhat, khat = qx * rq, kx * rk
    qnb, knb = (qhat * qg).astype(bf16), (khat * kg).astype(bf16)
    s = _nt(qnb, knb) * SCALE + bias_scr[GQA * kh:GQA * (kh + 1)].reshape(GQA * BLK, 2 * BLK)
    row = lax.broadcasted_iota(i32, (GQA * BLK, 2 * BLK), 0) & (BLK - 1)
    col = lax.broadcasted_iota(i32, (GQA * BLK, 2 * BLK), 1)
    mask = (col > row) & (col <= row + BLK) & ((col >= BLK) | (n > 0))
    s = jnp.where(mask, s, NEG)
    ridx = lax.broadcasted_iota(i32, (GQA * BLK, 1), 0)
    sink = jnp.full((GQA * BLK, 1), sk_ref[GQA * kh + GQA - 1], f32)
    for g in range(GQA - 2, -1, -1):
        sink = jnp.where(ridx < (g + 1) * BLK, sk_ref[GQA * kh + g], sink)
    m = jnp.maximum(jnp.max(s, axis=-1, keepdims=True), sink)
    e = jnp.exp(s - m)
    es = jnp.exp(sink - m)
    den = jnp.sum(e, axis=-1, keepdims=True) + es
    return dict(p=e / den, psink=es / den, qhat=qhat, khat=khat, rq=rq, rk=rk, qnb=qnb, knb=knb, vb=vx.astype(bf16))


def _pool_group(win, g, w):
    n = win.blk
    c0 = DATTN + 2 * DKV + PGD * g
    uc = win.cur(c0, c0 + PGD)
    up = jnp.where(n > 0, win.prev(c0, c0 + PGD), 0.0)
    ue = jnp.concatenate([up, uc], axis=0)
    hi = ue.astype(bf16)
    lo = (ue - hi.astype(f32)).astype(bf16)
    t = lax.broadcasted_iota(i32, (BLK, 2 * BLK), 0)
    s = lax.broadcasted_iota(i32, (BLK, 2 * BLK), 1)
    band = jnp.where((s <= t + BLK) & (s > t + BLK - w), 1.0, 0.0).astype(bf16)
    sm = _nn(band, hi) + _nn(band, lo)
    pos = n * BLK + lax.broadcasted_iota(i32, (BLK, 1), 0) + 1
    cnt = jnp.minimum(pos, w).astype(f32)
    return sm / cnt - uc, band, cnt


def _mix_fwd(z, qg, kg, sinks, relb, bucket, pool_w, pscale, name):
    T = z.shape[0]
    step_rows = MIX_SUB * BLK
    nsteps = T // step_rows

    def body(zc_ref, zp_ref, qg_ref, kg_ref, sk_ref, rb_ref, bk_ref, pw_ref, ps_ref, y_ref, bias_scr, yacc):
        n = pl.program_id(0)

        @pl.when(n == 0)
        def _():
            _fill_bias(bk_ref, rb_ref, bias_scr)

        for s in range(MIX_SUB):
            win = _Window(zc_ref, zp_ref, n, s)
            rows = slice(s * BLK, (s + 1) * BLK)
            for kh in range(NKV):
                a = _attn_probs(win, kh, qg_ref[...], kg_ref[...], sk_ref, bias_scr)
                o = _nn(a["p"].astype(bf16), a["vb"])
                for g in range(GQA):
                    hc = HD * (GQA * kh + g)
                    yacc[rows, hc:hc + HD] = o[g * BLK:(g + 1) * BLK]
            for g, w in enumerate(POOL_WINDOWS):
                pooled, _, _ = _pool_group(win, g, w)
                yp = _nn(pooled.astype(bf16), pw_ref[g].astype(bf16)) * ps_ref[:, PGD * g:PGD * (g + 1)]
                yacc[rows, DATTN + PGD * g:DATTN + PGD * (g + 1)] = yp
        y_ref[...] = yacc[...].astype(bf16)

    full = lambda *shape: pl.BlockSpec(shape, lambda n: (0,) * len(shape))
    smem = pl.BlockSpec(memory_space=pltpu.SMEM)
    return pl.pallas_call(
        body, grid=(nsteps,),
        in_specs=[pl.BlockSpec((step_rows, DIN), lambda n: (n, 0)),
                  pl.BlockSpec((BLK, DIN), lambda n: (jnp.maximum(n * MIX_SUB - 1, 0), 0)),
                  full(1, HD), full(1, HD), smem, smem, full(BLK, 2 * BLK),
                  full(len(POOL_WINDOWS), PGD, PGD), full(1, DPOOL)],
        out_specs=pl.BlockSpec((step_rows, DMIX), lambda n: (n, 0)),
        out_shape=SDS((T, DMIX), bf16),
        scratch_shapes=[pltpu.VMEM((NH, BLK, 2 * BLK), f32), pltpu.VMEM((step_rows, DMIX), f32)],
        compiler_params=_cparams(("arbitrary",)), name=name)(z, z, qg, kg, sinks, relb, bucket, pool_w, pscale)


def _mix_bwd(z, dy, qg, kg, sinks, relb, bucket, pool_w, pscale, name):
    T = z.shape[0]
    step_rows = MIX_SUB * BLK
    nsteps = T // step_rows

    def body(zc_ref, zp_ref, dy_ref, qg_ref, kg_ref, sk_ref, rb_ref, bk_ref, pw_ref, ps_ref,
             dz_ref, dqg_ref, dkg_ref, dsk_ref, drb_ref, dpw_ref, dps_ref, bias_scr, dbias_scr):
        n = pl.program_id(0)

        @pl.when(n == 0)
        def _():
            _fill_bias(bk_ref, rb_ref, bias_scr)
            dbias_scr[...] = jnp.zeros_like(dbias_scr)
            dqg_ref[...] = jnp.zeros_like(dqg_ref)
            dkg_ref[...] = jnp.zeros_like(dkg_ref)
            dsk_ref[...] = jnp.zeros_like(dsk_ref)
            dpw_ref[...] = jnp.zeros_like(dpw_ref)
            dps_ref[...] = jnp.zeros_like(dps_ref)

        qg, kg = qg_ref[...], kg_ref[...]
        lane = lax.broadcasted_iota(i32, (1, 128), 1)
        dsk = jnp.zeros((1, 128), f32)
        for s in range(MIX_SUB):
            win = _Window(zc_ref, zp_ref, n, s)
            blk = win.blk
            rows = pl.ds(pl.multiple_of(blk * BLK, BLK), BLK)
            prow = pl.ds(pl.multiple_of(jnp.maximum(blk - 1, 0) * BLK, BLK), BLK)
            dyr = slice(s * BLK, (s + 1) * BLK)

            def into_prev(fn, s=s):
                if s == 0:
                    pl.when(n > 0)(fn)
                else:
                    fn()

            for kh in range(NKV):
                a = _attn_probs(win, kh, qg, kg, sk_ref, bias_scr)
                p = a["p"]
                do = jnp.concatenate([dy_ref[dyr, HD * (GQA * kh + g):HD * (GQA * kh + g + 1)] for g in range(GQA)],
                                     axis=0).astype(bf16)
                dv = _tn(p.astype(bf16), do)
                dp = _nt(do, a["vb"])
                delta = jnp.sum(p * dp, axis=-1, keepdims=True)
                ds = p * (dp - delta)
                sinkterm = a["psink"] * delta
                for g in range(GQA):
                    h = GQA * kh + g
                    dbias_scr[h] += ds[g * BLK:(g + 1) * BLK]
                    tot = jnp.sum(sinkterm[g * BLK:(g + 1) * BLK], axis=0, keepdims=True)
                    dsk = dsk - jnp.where(lane == h, tot, 0.0)
                dsb = ds.astype(bf16)
                dqn = _nn(dsb, a["knb"]) * SCALE
                dkn = _tn(dsb, a["qnb"]) * SCALE
                qhat, khat = a["qhat"], a["khat"]
                dqg_ref[...] += jnp.sum(dqn * qhat, axis=0, keepdims=True)
                dkg_ref[...] += jnp.sum(dkn * khat, axis=0, keepdims=True)
                dqh = dqn * qg
                dq = a["rq"] * (dqh - qhat * jnp.mean(dqh * qhat, axis=-1, keepdims=True))
                dkh = dkn * kg
                dk = a["rk"] * (dkh - khat * jnp.mean(dkh * khat, axis=-1, keepdims=True))
                kc = DATTN + HD * kh
                vc = DATTN + DKV + HD * kh
                for g in range(GQA):
                    hc = HD * (GQA * kh + g)
                    dz_ref[rows, hc:hc + HD] = dq[g * BLK:(g + 1) * BLK]
                dz_ref[rows, kc:kc + HD] = dk[BLK:2 * BLK]
                dz_ref[rows, vc:vc + HD] = dv[BLK:2 * BLK]

                def kv_prev(dk=dk, dv=dv, kc=kc, vc=vc, prow=prow):
                    dz_ref[prow, kc:kc + HD] += dk[0:BLK]
                    dz_ref[prow, vc:vc + HD] += dv[0:BLK]

                into_prev(kv_prev)

            for g, w in enumerate(POOL_WINDOWS):
                c0 = DATTN + 2 * DKV + PGD * g
                pooled, band, cnt = _pool_group(win, g, w)
                pb = pooled.astype(bf16)
                wb = pw_ref[g].astype(bf16)
                dyp = dy_ref[dyr, DATTN + PGD * g:DATTN + PGD * (g + 1)]
                ypre = _nn(pb, wb)
                dps_ref[:, PGD * g:PGD * (g + 1)] += jnp.sum(dyp * ypre, axis=0, keepdims=True)
                dyg = (dyp * ps_ref[:, PGD * g:PGD * (g + 1)]).astype(bf16)
                dpw_ref[g] += _tn(pb, dyg)
                dpooled = _nt(dyg, wb)
                dsm = dpooled / cnt
                hi = dsm.astype(bf16)
                lo = (dsm - hi.astype(f32)).astype(bf16)
                due = _tn(band, hi) + _tn(band, lo)
                dz_ref[rows, c0:c0 + PGD] = due[BLK:2 * BLK] - dpooled

                def pool_prev(due=due, c0=c0, prow=prow):
                    dz_ref[prow, c0:c0 + PGD] += due[0:BLK]

                into_prev(pool_prev)

        dsk_ref[...] += dsk

        @pl.when(n == nsteps - 1)
        def _():
            bk = bk_ref[...]
            ri = lax.broadcasted_iota(i32, (NBUCK, NH), 0)
            ci = lax.broadcasted_iota(i32, (NBUCK, NH), 1)

            def step(b, acc):
                for h in range(NH):
                    sel = jnp.where(bk == b, dbias_scr[h], 0.0)
                    tot = jnp.sum(jnp.sum(sel, axis=1, keepdims=True), axis=0, keepdims=True)
                    acc = acc + jnp.where((ri == b) & (ci == h), tot, 0.0)
                return acc

            drb_ref[...] = lax.fori_loop(0, NBUCK, step, jnp.zeros((NBUCK, NH), f32))

    full = lambda *shape: pl.BlockSpec(shape, lambda n: (0,) * len(shape))
    smem = pl.BlockSpec(memory_space=pltpu.SMEM)
    npg = len(POOL_WINDOWS)
    return pl.pallas_call(
        body, grid=(nsteps,),
        in_specs=[pl.BlockSpec((step_rows, DIN), lambda n: (n, 0)),
                  pl.BlockSpec((BLK, DIN), lambda n: (jnp.maximum(n * MIX_SUB - 1, 0), 0)),
                  pl.BlockSpec((step_rows, DMIX), lambda n: (n, 0)),
                  full(1, HD), full(1, HD), smem, smem, full(BLK, 2 * BLK), full(npg, PGD, PGD), full(1, DPOOL)],
        out_specs=[full(T, DIN), full(1, HD), full(1, HD), full(1, 128), full(NBUCK, NH),
                   full(npg, PGD, PGD), full(1, DPOOL)],
        out_shape=(SDS((T, DIN), f32), SDS((1, HD), f32), SDS((1, HD), f32), SDS((1, 128), f32),
                   SDS((NBUCK, NH), f32), SDS((npg, PGD, PGD), f32), SDS((1, DPOOL), f32)),
        scratch_shapes=[pltpu.VMEM((NH, BLK, 2 * BLK), f32), pltpu.VMEM((NH, BLK, 2 * BLK), f32)],
        compiler_params=_cparams(("arbitrary",), VMEM_LIMIT_V7X),
        name=name)(z, z, dy, qg, kg, sinks, relb, bucket, pool_w, pscale)


class _LocalWeights:
    def __init__(self, w1, wint, wout, w2):
        self.w1, self.wint, self.wout, self.w2 = w1, wint, wout, w2

    def ffn1(self):
        return self.w1

    def mix(self, after):
        return self.wint, self.wout

    def before_out_proj(self, wout, after):
        return wout

    def ffn2(self, after):
        return self.w2

    def mix_ffn2_grads_ready(self, dwint, dwout, dw2, dh2):
        self.grads_rest = (dwint, dwout, dw2)
        return dh2

    def before_ffn1_bwd(self, dx1b):
        return dx1b


def _local_step(x, target, weights, g1, gm, g3, qg, kg, sinks, relb, pool_w, pscale):
    bucket = jnp.asarray(_t5_bucket_table())
    sk = sinks.reshape(NH)
    w1 = weights.ffn1()
    h1 = _norm_fwd(x, g1, "norm1_fwd")
    x1, gate1, up1 = _ffn_fwd(h1, w1, x, "ffn1_fwd")
    h2 = _norm_fwd(x1, gm, "norm2_fwd")
    wint, wout = weights.mix(h2)
    z = _in_proj_fwd(h2, wint, "in_proj_fwd")
    ymix = _mix_fwd(z, qg, kg, sk, relb, bucket, pool_w, pscale, "mix_fwd")
    wout = weights.before_out_proj(wout, ymix)
    x2 = _out_proj_fwd(ymix, wout, x1, "out_proj_fwd")
    h3 = _norm_fwd(x2, g3, "norm3_fwd")
    w2 = weights.ffn2(h3)
    y, gate2, up2 = _ffn_fwd(h3, w2, x2, "ffn2_fwd")
    dy, dyb, loss_lanes = _loss_grad(y, target, "loss_grad")

    dh3, dw2 = _ffn_bwd(dyb, h3, gate2, up2, w2, "ffn2_bwd")
    dx2, dx2b, dg3 = _norm_bwd(dh3, x2, g3, dy, 1.0, "norm3_bwd")
    dymix, dwout = _out_proj_bwd(dx2b, wout, ymix, "out_proj_bwd")
    dz, dqg, dkg, dsk, drb, dpw, dps = _mix_bwd(z, dymix, qg, kg, sk, relb, bucket, pool_w, pscale, "mix_bwd")
    dh2, dwint = _in_proj_bwd(dz, wint, h2, "in_proj_bwd")
    dh2 = weights.mix_ffn2_grads_ready(dwint, dwout, dw2, dh2)
    dx1, dx1b, dgm = _norm_bwd(dh2, x1, gm, dx2, 0.5, "norm2_bwd")
    dx1b = weights.before_ffn1_bwd(dx1b)
    dh1, dw1 = _ffn_bwd(dx1b, h1, gate1, up1, w1, "ffn1_bwd")
    gx, _, dg1 = _norm_bwd(dh1, x, g1, dx1, 1.0, "norm1_bwd")
    small = dict(ffn1_norm=dg1, mix_norm=dgm, ffn2_norm=dg3, pool_scale=dps, q_norm=dqg, k_norm=dkg,
                 attn_sinks=dsk[:, :NH], rel_bias=drb, pool_w=dpw, loss=loss_lanes)
    return gx, (dw1, dwint, dwout, dw2), small


SMALL_NAMES = ("ffn1_norm", "mix_norm", "ffn2_norm", "pool_scale", "q_norm", "k_norm", "attn_sinks", "rel_bias",
               "pool_w", "loss")
SMALL_SHAPES = dict(ffn1_norm=(1, D), mix_norm=(1, D), ffn2_norm=(1, D), pool_scale=(1, DPOOL), q_norm=(1, HD),
                    k_norm=(1, HD), attn_sinks=(1, NH), rel_bias=(NBUCK, NH),
                    pool_w=(1, len(POOL_WINDOWS), PGD, PGD), loss=(1, 128))


def _small_rows(name):
    return -(-int(np.prod(SMALL_SHAPES[name])) // 128)


SMALL_OFF = {}
_r = 0
for _n in SMALL_NAMES:
    SMALL_OFF[_n] = _r
    _r += _small_rows(_n)
SMALL_ROWS = -(-_r // 8) * 8
LOSS_ROW = SMALL_OFF["loss"]


def _pack_small(vals):
    parts = []
    for n in SMALL_NAMES:
        size = _small_rows(n) * 128
        if n in vals:
            flat = vals[n].astype(f32).reshape(-1)
            parts.append(jnp.pad(flat, (0, size - flat.shape[0])))
        else:
            parts.append(jnp.zeros((size,), f32))
    flat = jnp.concatenate(parts)
    flat = jnp.pad(flat, (0, SMALL_ROWS * 128 - flat.shape[0]))
    return flat.reshape(SMALL_ROWS, 128)


def _unpack_small(packed, name):
    size = int(np.prod(SMALL_SHAPES[name]))
    r0 = SMALL_OFF[name]
    return packed[r0:r0 + _small_rows(name)].reshape(-1)[:size].reshape(SMALL_SHAPES[name])


def _position():
    return lax.axis_index("x"), lax.axis_index("y"), lax.axis_index("c")


def _dev_index(x, y, c):
    return 4 * x + 2 * y + c


G1_PIECES, MIX_PIECES, F2_PIECES = (0, 1, 2), (3, 4), (5, 6, 7)


def _group_rows(pieces):
    return sum(PIECE_ROWS[k] for k in pieces)


def _shard_piece(s_ref, k):
    return s_ref.at[pl.ds(PIECE_OFF[k], PIECE_ROWS[k]), :]


def _shard_group(s_ref, pieces):
    return s_ref.at[pl.ds(PIECE_OFF[pieces[0]], _group_rows(pieces)), :]


def _weight_pieces(w1_ref=None, wi_ref=None, wo_ref=None, w2_ref=None):
    arrs = {}
    if w1_ref is not None:
        arrs.update({0: w1_ref.at[0], 1: w1_ref.at[1], 2: w1_ref.at[2]})
    if wi_ref is not None:
        arrs[3] = wi_ref
    if wo_ref is not None:
        arrs[4] = wo_ref
    if w2_ref is not None:
        arrs.update({5: w2_ref.at[0], 6: w2_ref.at[1], 7: w2_ref.at[2]})
    return arrs


def _block_rows(arrs, k, dev):
    r = PIECE_ROWS[k]
    return arrs[k].at[pl.ds(pl.multiple_of(_dev_index(*dev) * r, 16), r), :]


def _all_gather_ffn1(shard):
    pieces = G1_PIECES

    def body(s_ref, w1_ref, send_sems, recv_sems, local_sem):
        x, y, c = _position()
        me, sib = (x, y, c), (x, y, 1 - c)
        chips = [(1 - x, y), (x, 1 - y), (1 - x, 1 - y)]
        arrs = _weight_pieces(w1_ref=w1_ref)

        def copies(rel, block, to, from_shard):
            return [pltpu.make_async_remote_copy(
                src_ref=_shard_piece(s_ref, k) if from_shard else _block_rows(arrs, k, block),
                dst_ref=_block_rows(arrs, k, block),
                send_sem=send_sems.at[rel], recv_sem=recv_sems.at[rel], device_id=to, device_id_type=MESH)
                for k in pieces]

        def whole(rel):
            grp = _shard_group(s_ref, pieces)
            return pltpu.make_async_remote_copy(src_ref=grp, dst_ref=grp, send_sem=send_sems.at[rel],
                                                recv_sem=recv_sems.at[rel], device_id=me, device_id_type=MESH)

        mine = [pltpu.make_async_copy(_shard_piece(s_ref, k), _block_rows(arrs, k, me), local_sem) for k in pieces]
        for cp in mine:
            cp.start()
        for cp in copies(0, me, sib, True):
            cp.start()
        for j, chip in enumerate(chips):
            for cp in copies(1 + j, me, (*chip, c), True):
                cp.start()
        for j, chip in enumerate(chips):
            whole(1 + j).wait_recv()
            for cp in copies(4 + j, (*chip, c), sib, False):
                cp.start()
        whole(0).wait_recv()
        for j in range(3):
            whole(4 + j).wait_recv()
        for rel in range(7):
            whole(rel).wait_send()
        grp = _shard_group(s_ref, pieces)
        pltpu.make_async_copy(grp, grp, local_sem).wait()

    hbm = pl.BlockSpec(memory_space=pl.ANY)
    return pl.pallas_call(
        body, in_specs=[hbm], out_specs=hbm, out_shape=SDS((3, F, D), bf16),
        scratch_shapes=[pltpu.SemaphoreType.DMA((7,)), pltpu.SemaphoreType.DMA((7,)), pltpu.SemaphoreType.DMA],
        compiler_params=pltpu.CompilerParams(has_side_effects=True),
        name="all_gather_ffn1")(shard)


HBM_SPEC = pl.BlockSpec(memory_space=pltpu.HBM)
SEM_SPEC = pl.BlockSpec(memory_space=pltpu.SEMAPHORE)
ANY_SPEC = pl.BlockSpec(memory_space=pl.ANY)
SPLIT_EFFECT = pltpu.SideEffectType.DATAFLOW_SIDE_EFFECTING


def _in_hbm(a):
    return pltpu.with_memory_space_constraint(a, pltpu.HBM)


def _hbm_like(a):
    return pltpu.HBM(a.shape, a.dtype)


def _place_own_rows(shard):
    pieces = MIX_PIECES + F2_PIECES

    def body(s_ref, wi_ref, wo_ref, w2_ref, buf, sems):
        x, y, c = _position()
        arrs = _weight_pieces(wi_ref=wi_ref, wo_ref=wo_ref, w2_ref=w2_ref)
        grp = _shard_group(s_ref, pieces)
        load = pltpu.make_async_copy(grp, buf, sems.at[0])
        load.start()
        load.wait()
        base = PIECE_OFF[pieces[0]]
        for k in pieces:
            pltpu.make_async_copy(buf.at[pl.ds(PIECE_OFF[k] - base, PIECE_ROWS[k]), :],
                                  _block_rows(arrs, k, (x, y, c)), sems.at[1]).start()
        pltpu.make_async_copy(grp, buf, sems.at[1]).wait()

    return pl.pallas_call(
        body, in_specs=[ANY_SPEC], out_specs=[ANY_SPEC] * 3,
        out_shape=(SDS((DIN, D), bf16), SDS((DMIX, D), bf16), SDS((3, F, D), bf16)),
        scratch_shapes=[pltpu.VMEM((_group_rows(pieces), D), bf16), pltpu.SemaphoreType.DMA((2,))],
        name="place_own_rows")(shard)


def _xor_peer(x, y, c, k):
    return (x ^ (k >> 2), y ^ ((k >> 1) & 1), c ^ (k & 1))


def _gather_rest_start(shard, wi, wo, w2, w1):
    def body(s_ref, wi_ref, wo_ref, w2_ref, w1_ref,
             ssem_m, rsem_m, ssem_f, rsem_f0, rsem_f, s_o, wi_o, wo_o, w2_o, w1_o):
        x, y, c = _position()
        me, sib = (x, y, c), (x, y, 1 - c)
        chips = [(1 - x, y), (x, 1 - y), (1 - x, 1 - y)]
        arrs = _weight_pieces(wi_ref=wi_ref, wo_ref=wo_ref, w2_ref=w2_ref)
        for k in range(1, NDEV):
            for p in MIX_PIECES:
                pltpu.make_async_remote_copy(
                    src_ref=_shard_piece(s_ref, p), dst_ref=_block_rows(arrs, p, me), send_sem=ssem_m.at[k - 1],
                    recv_sem=rsem_m.at[k - 1], device_id=_xor_peer(x, y, c, k), device_id_type=MESH).start()
        for p in F2_PIECES:
            pltpu.make_async_remote_copy(
                src_ref=_shard_piece(s_ref, p), dst_ref=_block_rows(arrs, p, me), send_sem=ssem_f.at[0],
                recv_sem=rsem_f0, device_id=sib, device_id_type=MESH).start()
        for j, chip in enumerate(chips):
            for p in F2_PIECES:
                pltpu.make_async_remote_copy(
                    src_ref=_shard_piece(s_ref, p), dst_ref=_block_rows(arrs, p, me), send_sem=ssem_f.at[1 + j],
                    recv_sem=rsem_f.at[j], device_id=(*chip, c), device_id_type=MESH).start()

    dma = pltpu.SemaphoreType.DMA
    return pl.pallas_call(
        body, name="gather_rest_start",
        out_shape=(dma((7,)), dma((7,)), dma((4,)), dma(()), dma((3,)),
                   _hbm_like(shard), _hbm_like(wi), _hbm_like(wo), _hbm_like(w2), _hbm_like(w1)),
        in_specs=(HBM_SPEC,) * 5, out_specs=(SEM_SPEC,) * 5 + (HBM_SPEC,) * 5,
        input_output_aliases={0: 5, 1: 6, 2: 7, 3: 8, 4: 9},
        compiler_params=pltpu.CompilerParams(has_side_effects=SPLIT_EFFECT),
    )(_in_hbm(shard), _in_hbm(wi), _in_hbm(wo), _in_hbm(w2), _in_hbm(w1))


def _gather_mix_wait(ssem_m, rsem_m, shard, wi, wo, after):
    def body(s_ref, wi_ref, wo_ref, ssem, rsem, after_ref, s_o, wi_o, wo_o):
        x, y, c = _position()
        grp = _shard_group(s_ref, MIX_PIECES)
        for k in range(NDEV - 1):
            d = pltpu.make_async_remote_copy(src_ref=grp, dst_ref=grp, send_sem=ssem.at[k], recv_sem=rsem.at[k],
                                             device_id=(x, y, c), device_id_type=MESH)
            d.wait_recv()
            d.wait_send()

    return pl.pallas_call(
        body, name="gather_mix_wait", out_shape=(_hbm_like(shard), _hbm_like(wi), _hbm_like(wo)),
        in_specs=(HBM_SPEC, HBM_SPEC, HBM_SPEC, SEM_SPEC, SEM_SPEC, ANY_SPEC), out_specs=(HBM_SPEC,) * 3,
        input_output_aliases={0: 0, 1: 1, 2: 2},
        compiler_params=pltpu.CompilerParams(has_side_effects=SPLIT_EFFECT),
    )(shard, wi, wo, ssem_m, rsem_m, after)


def _gather_ffn2_pass_on(rsem_f, w2, wo, after):
    def body(w2_ref, wo_ref, rsem, after_ref, fsend, frecv, w2_o, wo_o):
        x, y, c = _position()
        sib = (x, y, 1 - c)
        chips = [(1 - x, y), (x, 1 - y), (1 - x, 1 - y)]
        arrs = _weight_pieces(w2_ref=w2_ref)
        three = w2_ref.at[0, pl.ds(0, _group_rows(F2_PIECES)), :]
        for j, chip in enumerate(chips):
            pltpu.make_async_remote_copy(src_ref=three, dst_ref=three, send_sem=fsend.at[j], recv_sem=rsem.at[j],
                                         device_id=(x, y, c), device_id_type=MESH).wait_recv()
            for p in F2_PIECES:
                rows = _block_rows(arrs, p, (*chip, c))
                pltpu.make_async_remote_copy(src_ref=rows, dst_ref=rows, send_sem=fsend.at[j], recv_sem=frecv.at[j],
                                             device_id=sib, device_id_type=MESH).start()

    dma = pltpu.SemaphoreType.DMA
    return pl.pallas_call(
        body, name="gather_ffn2_pass_on", out_shape=(dma((3,)), dma((3,)), _hbm_like(w2), _hbm_like(wo)),
        in_specs=(HBM_SPEC, HBM_SPEC, SEM_SPEC, ANY_SPEC), out_specs=(SEM_SPEC, SEM_SPEC, HBM_SPEC, HBM_SPEC),
        input_output_aliases={0: 2, 1: 3},
        compiler_params=pltpu.CompilerParams(has_side_effects=SPLIT_EFFECT),
    )(w2, wo, rsem_f, after)


def _gather_ffn2_wait(ssem_f, rsem_f0, fsend, frecv, shard, w2, after):
    def body(s_ref, w2_ref, ssem, rsem0, fs, fr, after_ref, w2_o):
        x, y, c = _position()
        grp = _shard_group(s_ref, F2_PIECES)

        def waiter(send_sem, recv_sem):
            return pltpu.make_async_remote_copy(src_ref=grp, dst_ref=grp, send_sem=send_sem, recv_sem=recv_sem,
                                                device_id=(x, y, c), device_id_type=MESH)

        waiter(ssem.at[0], rsem0).wait_recv()
        for j in range(3):
            waiter(fs.at[j], fr.at[j]).wait_recv()
        for rel in range(4):
            waiter(ssem.at[rel], rsem0).wait_send()
        for j in range(3):
            waiter(fs.at[j], fr.at[j]).wait_send()

    return pl.pallas_call(
        body, name="gather_ffn2_wait", out_shape=_hbm_like(w2),
        in_specs=(HBM_SPEC, HBM_SPEC, SEM_SPEC, SEM_SPEC, SEM_SPEC, SEM_SPEC, ANY_SPEC), out_specs=HBM_SPEC,
        input_output_aliases={1: 0},
        compiler_params=pltpu.CompilerParams(has_side_effects=SPLIT_EFFECT),
    )(shard, w2, ssem_f, rsem_f0, fsend, frecv, after)


class _GatheredWeights(_LocalWeights):
    def __init__(self, shard):
        w1 = _all_gather_ffn1(shard)
        wi, wo, w2 = _place_own_rows(shard)
        (self.ssem_m, self.rsem_m, self.ssem_f, self.rsem_f0, self.rsem_f,
         self.shard, self.wi, self.wo, self.w2_part, self.w1) = _gather_rest_start(shard, wi, wo, w2, w1)

    def mix(self, after):
        self.shard, wint, wout = _gather_mix_wait(self.ssem_m, self.rsem_m, self.shard, self.wi, self.wo, after)
        return wint, wout

    def before_out_proj(self, wout, after):
        self.fsend, self.frecv, self.w2_part, wout = _gather_ffn2_pass_on(self.rsem_f, self.w2_part, wout, after)
        return wout

    def ffn2(self, after):
        return _gather_ffn2_wait(self.ssem_f, self.rsem_f0, self.fsend, self.frecv, self.shard, self.w2_part, after)

    def mix_ffn2_grads_ready(self, dwint, dwout, dw2, dh2):
        rx1 = lax.empty((4, RSA_ROWS, D), bf16)
        self.sa, self.ra, dwint, dwout, dw2, rx1, dh2 = _rsa_level1_start(dwint, dwout, dw2, rx1, dh2)
        self.level1 = (dwint, dwout, dw2, rx1)
        return dh2

    def before_ffn1_bwd(self, dx1b):
        dwint, dwout, dw2, rx1 = _rsa_level1_wait(self.sa, self.ra, *self.level1, dx1b)
        tx, self.acc = _rsa_chip_sums(dwint, dwout, dw2, rx1)
        rx2 = lax.empty((3, RSA_ROWS, D), bf16)
        self.sb, self.rb, self.tx, self.rx2, dx1b = _rsa_level2_start(tx, rx2, dx1b)
        return dx1b

    def mix_ffn2_grads_total(self, after):
        rx2 = _rsa_level2_wait(self.sb, self.rb, self.tx, self.rx2, after)
        return _rsa_total(self.acc, rx2)


RS_CHUNK = 176


def _reduce_scatter_ffn1(dw1):
    pieces = G1_PIECES
    nrows = _group_rows(pieces)
    nchunk = nrows // RS_CHUNK

    def body(d1_ref, red_ref, rx1_ref, rx2_ref,
             own_buf, rx_buf, tx_buf, acc, sa, ra, sb, rb, lsem):
        x, y, c = _position()
        me, sib = (x, y, c), (x, y, 1 - c)
        rel_chips = [(x, y), (1 - x, y), (x, 1 - y), (1 - x, 1 - y)]
        srcs = _weight_pieces(w1_ref=d1_ref)

        def piece(k, dev):
            r = PIECE_ROWS[k]
            return srcs[k].at[pl.ds(pl.multiple_of(_dev_index(*dev) * r, 16), r), :]

        def packed(ref, k):
            return ref.at[pl.ds(PIECE_OFF[k], PIECE_ROWS[k]), :]

        for j, chip in enumerate(rel_chips):
            for k in pieces:
                pltpu.make_async_remote_copy(
                    src_ref=piece(k, (*chip, 1 - c)), dst_ref=packed(rx1_ref.at[j], k),
                    send_sem=sa.at[j], recv_sem=ra.at[j], device_id=sib, device_id_type=MESH).start()

        def wait_a(j):
            return pltpu.make_async_remote_copy(src_ref=rx1_ref.at[j], dst_ref=rx1_ref.at[j], send_sem=sa.at[j],
                                                recv_sem=ra.at[j], device_id=me, device_id_type=MESH)

        def ici(j):
            return pltpu.make_async_remote_copy(
                src_ref=tx_buf.at[j - 1], dst_ref=rx2_ref.at[j - 1], send_sem=sb.at[j - 1], recv_sem=rb.at[j - 1],
                device_id=(*rel_chips[j], c), device_id_type=MESH)

        for j in (1, 2, 3, 0):
            loads = [pltpu.make_async_copy(piece(k, (*rel_chips[j], c)), packed(own_buf, k), lsem)
                     for k in pieces]
            for cp in loads:
                cp.start()
            wait_a(j).wait_recv()
            got = pltpu.make_async_copy(rx1_ref.at[j], rx_buf, lsem)
            got.start()
            pltpu.make_async_copy(rx_buf, rx_buf, lsem).wait()
            got.wait()

            def add(i, carry, j=j):
                rows = pl.ds(pl.multiple_of(i * RS_CHUNK, 16), RS_CHUNK)
                tot = own_buf[rows, :].astype(f32) + rx_buf[rows, :].astype(f32)
                if j == 0:
                    acc[rows, :] = tot
                else:
                    tx_buf[j - 1, rows, :] = tot.astype(bf16)
                return carry

            lax.fori_loop(0, nchunk, add, 0)
            if j != 0:
                ici(j).start()

        for j in (1, 2, 3):
            ici(j).wait_recv()
            got = pltpu.make_async_copy(rx2_ref.at[j - 1], rx_buf, lsem)
            got.start()
            got.wait()

            def add2(i, carry):
                rows = pl.ds(pl.multiple_of(i * RS_CHUNK, 16), RS_CHUNK)
                acc[rows, :] += rx_buf[rows, :].astype(f32)
                return carry

            lax.fori_loop(0, nchunk, add2, 0)
        out = pltpu.make_async_copy(acc, red_ref, lsem)
        out.start()
        out.wait()
        for j in range(4):
            wait_a(j).wait_send()
        for j in (1, 2, 3):
            ici(j).wait_send()

    hbm = pl.BlockSpec(memory_space=pl.ANY)
    red, _, _ = pl.pallas_call(
        body, in_specs=[hbm], out_specs=[hbm] * 3,
        out_shape=(SDS((nrows, D), f32), SDS((4, nrows, D), bf16), SDS((3, nrows, D), bf16)),
        scratch_shapes=[pltpu.VMEM((nrows, D), bf16), pltpu.VMEM((nrows, D), bf16),
                        pltpu.VMEM((3, nrows, D), bf16), pltpu.VMEM((nrows, D), f32),
                        pltpu.SemaphoreType.DMA((4,)), pltpu.SemaphoreType.DMA((4,)),
                        pltpu.SemaphoreType.DMA((3,)), pltpu.SemaphoreType.DMA((3,)), pltpu.SemaphoreType.DMA],
        compiler_params=pltpu.CompilerParams(has_side_effects=True, vmem_limit_bytes=VMEM_LIMIT_V7X),
        name="reduce_scatter_ffn1")(dw1)
    return red


RSA_PIECES = MIX_PIECES + F2_PIECES
RSA_ROWS = _group_rows(RSA_PIECES)
RSA_OFF = {k: PIECE_OFF[k] - PIECE_OFF[RSA_PIECES[0]] for k in RSA_PIECES}
RSA_BLOCK = 192


def _rsa_rows(ref, k):
    return ref.at[pl.ds(RSA_OFF[k], PIECE_ROWS[k]), :]


def _rsa_level1_start(dwint, dwout, dw2, rx1, thru):
    def body(di_ref, do_ref, d2_ref, rx1_ref, thru_ref, sa, ra, di_o, do_o, d2_o, rx1_o, thru_o):
        x, y, c = _position()
        srcs = _weight_pieces(wi_ref=di_ref, wo_ref=do_ref, w2_ref=d2_ref)
        for j, chip in enumerate([(x, y), (1 - x, y), (x, 1 - y), (1 - x, 1 - y)]):
            for k in RSA_PIECES:
                pltpu.make_async_remote_copy(
                    src_ref=_block_rows(srcs, k, (*chip, 1 - c)), dst_ref=_rsa_rows(rx1_ref.at[j], k),
                    send_sem=sa.at[j], recv_sem=ra.at[j], device_id=(x, y, 1 - c), device_id_type=MESH).start()

    dma = pltpu.SemaphoreType.DMA
    arrs = (dwint, dwout, dw2, rx1, thru)
    return pl.pallas_call(
        body, name="rsa_level1_start", out_shape=(dma((4,)), dma((4,))) + tuple(_hbm_like(a) for a in arrs),
        in_specs=(HBM_SPEC,) * 5, out_specs=(SEM_SPEC,) * 2 + (HBM_SPEC,) * 5,
        input_output_aliases={0: 2, 1: 3, 2: 4, 3: 5, 4: 6},
        compiler_params=pltpu.CompilerParams(has_side_effects=SPLIT_EFFECT),
    )(*[_in_hbm(a) for a in arrs])


def _rsa_level1_wait(sa, ra, dwint, dwout, dw2, rx1, after):
    def body(di_ref, do_ref, d2_ref, rx1_ref, sa_ref, ra_ref, after_ref, di_o, do_o, d2_o, rx1_o):
        x, y, c = _position()
        for j in range(4):
            d = pltpu.make_async_remote_copy(src_ref=rx1_ref.at[j], dst_ref=rx1_ref.at[j], send_sem=sa_ref.at[j],
                                             recv_sem=ra_ref.at[j], device_id=(x, y, c), device_id_type=MESH)
            d.wait_recv()
            d.wait_send()

    arrs = (dwint, dwout, dw2, rx1)
    return pl.pallas_call(
        body, name="rsa_level1_wait", out_shape=tuple(_hbm_like(a) for a in arrs),
        in_specs=(HBM_SPEC,) * 4 + (SEM_SPEC, SEM_SPEC, ANY_SPEC), out_specs=(HBM_SPEC,) * 4,
        input_output_aliases={0: 0, 1: 1, 2: 2, 3: 3},
        compiler_params=pltpu.CompilerParams(has_side_effects=SPLIT_EFFECT),
    )(*arrs, sa, ra, after)


def _rsa_chip_sums(dwint, dwout, dw2, rx1):
    nblk = RSA_ROWS // RSA_BLOCK

    def body(di_ref, do_ref, d2_ref, rx1_ref, tx_ref, acc_ref, own_buf, rx_buf, tx_buf, acc_buf, lsems):
        x, y, c = _position()
        srcs = _weight_pieces(wi_ref=di_ref, wo_ref=do_ref, w2_ref=d2_ref)
        for j, chip in enumerate([(x, y), (1 - x, y), (x, 1 - y), (1 - x, 1 - y)]):
            loads = [pltpu.make_async_copy(_block_rows(srcs, k, (*chip, c)), _rsa_rows(own_buf, k), lsems.at[0])
                     for k in RSA_PIECES]
            got = pltpu.make_async_copy(rx1_ref.at[j], rx_buf, lsems.at[1])
            for cp in loads + [got]:
                cp.start()
            pltpu.make_async_copy(rx_buf, rx_buf, lsems.at[0]).wait()
            got.wait()

            def add(i, carry, j=j):
                rows = pl.ds(pl.multiple_of(i * RSA_BLOCK, 16), RSA_BLOCK)
                tot = own_buf[rows, :].astype(f32) + rx_buf[rows, :].astype(f32)
                if j == 0:
                    acc_buf[rows, :] = tot
                else:
                    tx_buf[rows, :] = tot.astype(bf16)
                return carry

            lax.fori_loop(0, nblk, add, 0)
            out = (pltpu.make_async_copy(acc_buf, acc_ref, lsems.at[2]) if j == 0
                   else pltpu.make_async_copy(tx_buf, tx_ref.at[j - 1], lsems.at[2]))
            out.start()
            out.wait()

    return pl.pallas_call(
        body, in_specs=[ANY_SPEC] * 4, out_specs=[ANY_SPEC] * 2,
        out_shape=(SDS((3, RSA_ROWS, D), bf16), SDS((RSA_ROWS, D), f32)),
        scratch_shapes=[pltpu.VMEM((RSA_ROWS, D), bf16), pltpu.VMEM((RSA_ROWS, D), bf16),
                        pltpu.VMEM((RSA_ROWS, D), bf16), pltpu.VMEM((RSA_ROWS, D), f32),
                        pltpu.SemaphoreType.DMA((3,))],
        compiler_params=_cparams(None, VMEM_LIMIT_V7X), name="rsa_chip_sums")(dwint, dwout, dw2, rx1)


def _rsa_level2_start(tx, rx2, thru):
    def body(tx_ref, rx2_ref, thru_ref, sb, rb, tx_o, rx2_o, thru_o):
        x, y, c = _position()
        for j, chip in enumerate([(1 - x, y), (x, 1 - y), (1 - x, 1 - y)]):
            pltpu.make_async_remote_copy(src_ref=tx_ref.at[j], dst_ref=rx2_ref.at[j], send_sem=sb.at[j],
                                         recv_sem=rb.at[j], device_id=(*chip, c), device_id_type=MESH).start()

    dma = pltpu.SemaphoreType.DMA
    arrs = (tx, rx2, thru)
    return pl.pallas_call(
        body, name="rsa_level2_start", out_shape=(dma((3,)), dma((3,))) + tuple(_hbm_like(a) for a in arrs),
        in_specs=(HBM_SPEC,) * 3, out_specs=(SEM_SPEC,) * 2 + (HBM_SPEC,) * 3,
        input_output_aliases={0: 2, 1: 3, 2: 4},
        compiler_params=pltpu.CompilerParams(has_side_effects=SPLIT_EFFECT),
    )(*[_in_hbm(a) for a in arrs])


def _rsa_level2_wait(sb, rb, tx, rx2, after):
    def body(tx_ref, rx2_ref, sb_ref, rb_ref, after_ref, rx2_o):
        x, y, c = _position()
        for j in range(3):
            d = pltpu.make_async_remote_copy(src_ref=tx_ref.at[j], dst_ref=rx2_ref.at[j], send_sem=sb_ref.at[j],
                                             recv_sem=rb_ref.at[j], device_id=(x, y, c), device_id_type=MESH)
            d.wait_recv()
            d.wait_send()

    return pl.pallas_call(
        body, name="rsa_level2_wait", out_shape=_hbm_like(rx2),
        in_specs=(HBM_SPEC, HBM_SPEC, SEM_SPEC, SEM_SPEC, ANY_SPEC), out_specs=HBM_SPEC,
        input_output_aliases={1: 0},
        compiler_params=pltpu.CompilerParams(has_side_effects=SPLIT_EFFECT),
    )(tx, rx2, sb, rb, after)


def _rsa_total(acc, rx2):
    def body(a_ref, r_ref, o_ref):
        o_ref[...] = ((a_ref[...] + r_ref[0].astype(f32)) + r_ref[1].astype(f32)) + r_ref[2].astype(f32)

    return pl.pallas_call(
        body, grid=(RSA_ROWS // RSA_BLOCK,),
        in_specs=[pl.BlockSpec((RSA_BLOCK, D), lambda i: (i, 0)), pl.BlockSpec((3, RSA_BLOCK, D), lambda i: (0, i, 0))],
        out_specs=pl.BlockSpec((RSA_BLOCK, D), lambda i: (i, 0)),
        out_shape=SDS((RSA_ROWS, D), f32), name="rsa_total")(acc, rx2)


def _all_reduce_small(packed):
    def body(p_ref, o_ref, pair, chips, send_sems, recv_sems):
        x, y, c = _position()
        chip = 2 * x + y
        pair[c] = p_ref[...]
        swap = pltpu.make_async_remote_copy(
            src_ref=p_ref, dst_ref=pair.at[c], send_sem=send_sems.at[0], recv_sem=recv_sems.at[0],
            device_id=(x, y, 1 - c), device_id_type=MESH)
        swap.start()
        swap.wait_recv()
        chips[chip] = pair[0] + pair[1]
        cps = [pltpu.make_async_remote_copy(
            src_ref=chips.at[chip], dst_ref=chips.at[chip], send_sem=send_sems.at[1 + j], recv_sem=recv_sems.at[1 + j],
            device_id=(*other, c), device_id_type=MESH)
            for j, other in enumerate([(1 - x, y), (x, 1 - y), (1 - x, 1 - y)])]
        for cp in cps:
            cp.start()
        for cp in cps:
            cp.wait_recv()
        tot = (chips[0] + chips[1]) + (chips[2] + chips[3])
        o_ref[...] = tot
        loss = jnp.sum(tot[LOSS_ROW:LOSS_ROW + 1, :], axis=-1, keepdims=True)
        o_ref[LOSS_ROW:LOSS_ROW + 1, :] = jnp.broadcast_to(loss, (1, 128))
        swap.wait_send()
        for cp in cps:
            cp.wait_send()

    vm = pl.BlockSpec(memory_space=pltpu.VMEM)
    return pl.pallas_call(
        body, in_specs=[vm], out_specs=vm, out_shape=SDS((SMALL_ROWS, 128), f32),
        scratch_shapes=[pltpu.VMEM((2, SMALL_ROWS, 128), f32), pltpu.VMEM((4, SMALL_ROWS, 128), f32),
                        pltpu.SemaphoreType.DMA((4,)), pltpu.SemaphoreType.DMA((4,))],
        compiler_params=pltpu.CompilerParams(has_side_effects=True),
        name="all_reduce_small")(packed)


def _adamw_math(w, g, m, v):
    m = ADAM_B1 * m + (1.0 - ADAM_B1) * g
    v = ADAM_B2 * v + (1.0 - ADAM_B2) * (g * g)
    m_hat = m / (1.0 - ADAM_B1 ** ADAM_STEP)
    v_hat = v / (1.0 - ADAM_B2 ** ADAM_STEP)
    delta = -ADAM_LR * (m_hat / (jnp.sqrt(v_hat) + ADAM_EPS) + ADAM_WD * w)
    return delta, m, v


def _adamw_big(ws, ms, vs, red1, red_rest):
    npiece = len(BIG)
    rmax = max(PIECE_ROWS)

    def body(*refs):
        ins = (refs[0:npiece], refs[npiece:2 * npiece], refs[2 * npiece:3 * npiece])
        red1_ref, rest_ref = refs[3 * npiece:3 * npiece + 2]
        out_refs = refs[3 * npiece + 2:7 * npiece + 2]
        inb, outb, in_sems, out_sems = refs[7 * npiece + 2:]

        def grad_rows(k):
            if k in G1_PIECES:
                return red1_ref.at[pl.ds(PIECE_OFF[k], PIECE_ROWS[k]), :]
            return _rsa_rows(rest_ref, k)

        def loads(k):
            s, r = k % 2, PIECE_ROWS[k]
            cps = [pltpu.make_async_copy(ins[q][k].at[0], inb.at[s, q, pl.ds(0, r), :], in_sems.at[4 * s + q])
                   for q in range(3)]
            cps.append(pltpu.make_async_copy(grad_rows(k), inb.at[s, 3, pl.ds(0, r), :], in_sems.at[4 * s + 3]))
            return cps

        def stores(k):
            s, r = k % 2, PIECE_ROWS[k]
            return [pltpu.make_async_copy(outb.at[s, q, pl.ds(0, r), :], out_refs[q * npiece + k].at[0],
                                          out_sems.at[4 * s + q]) for q in range(4)]

        for cp in loads(0):
            cp.start()
        for k in range(npiece):
            s, r = k % 2, PIECE_ROWS[k]
            if k + 1 < npiece:
                for cp in loads(k + 1):
                    cp.start()
            for cp in loads(k):
                cp.wait()
            if k >= 2:
                for cp in stores(k - 2):
                    cp.wait()
            g = inb[s, 3, 0:r, :]
            d, nm, nv = _adamw_math(inb[s, 0, 0:r, :], g, inb[s, 1, 0:r, :], inb[s, 2, 0:r, :])
            outb[s, 0, 0:r, :] = g
            outb[s, 1, 0:r, :] = d
            outb[s, 2, 0:r, :] = nm
            outb[s, 3, 0:r, :] = nv
            for cp in stores(k):
                cp.start()
        for k in (npiece - 2, npiece - 1):
            for cp in stores(k):
                cp.wait()

    hbm = pl.BlockSpec(memory_space=pl.ANY)
    outs = pl.pallas_call(
        body, in_specs=[hbm] * (3 * npiece + 2), out_specs=[hbm] * (4 * npiece),
        out_shape=tuple(SDS(w.shape, f32) for _ in range(4) for w in ws),
        scratch_shapes=[pltpu.VMEM((2, 4, rmax, D), f32), pltpu.VMEM((2, 4, rmax, D), f32),
                        pltpu.SemaphoreType.DMA((8,)), pltpu.SemaphoreType.DMA((8,))],
        compiler_params=_cparams(None, VMEM_LIMIT_V7X), name="adamw_big")(*ws, *ms, *vs, red1, red_rest)
    return [list(outs[q * npiece:(q + 1) * npiece]) for q in range(4)]


def _adamw_small(w, m, v, g, name):
    def body(w_ref, m_ref, v_ref, g_ref, d_ref, nm_ref, nv_ref):
        d, nm, nv = _adamw_math(w_ref[...], g_ref[...], m_ref[...], v_ref[...])
        d_ref[...] = d
        nm_ref[...] = nm
        nv_ref[...] = nv

    return pl.pallas_call(
        body, out_shape=tuple(SDS(w.shape, f32) for _ in range(3)), name=name)(w, m, v, g)


WEIGHTS = ("ffn1_norm", "ffn1_w_gate", "ffn1_w_up", "ffn1_w_down", "mix_norm", "w_in", "q_norm", "k_norm",
           "attn_sinks", "rel_bias", "pool_w", "pool_scale", "w_out", "ffn2_norm", "ffn2_w_gate", "ffn2_w_up",
           "ffn2_w_down")
BIG = (("ffn1_w_gate", True), ("ffn1_w_up", True), ("ffn1_w_down", False), ("w_in", True), ("w_out", False),
       ("ffn2_w_gate", True), ("ffn2_w_up", True), ("ffn2_w_down", False))


def kernel(x, ffn1_norm, ffn1_w_gate, ffn1_w_up, ffn1_w_down, mix_norm, w_in, q_norm, k_norm, attn_sinks, rel_bias, pool_w, pool_scale, w_out, ffn2_norm, ffn2_w_gate, ffn2_w_up, ffn2_w_down, loss_target, m_ffn1_norm, m_ffn1_w_gate, m_ffn1_w_up, m_ffn1_w_down, m_mix_norm, m_w_in, m_q_norm, m_k_norm, m_attn_sinks, m_rel_bias, m_pool_w, m_pool_scale, m_w_out, m_ffn2_norm, m_ffn2_w_gate, m_ffn2_w_up, m_ffn2_w_down, v_ffn1_norm, v_ffn1_w_gate, v_ffn1_w_up, v_ffn1_w_down, v_mix_norm, v_w_in, v_q_norm, v_k_norm, v_attn_sinks, v_rel_bias, v_pool_w, v_pool_scale, v_w_out, v_ffn2_norm, v_ffn2_w_gate, v_ffn2_w_up, v_ffn2_w_down):
    args = dict(locals())
    w = {n: args[n] for n in WEIGHTS}
    m = {n: args["m_" + n] for n in WEIGHTS}
    v = {n: args["v_" + n] for n in WEIGHTS}

    as_rows = lambda a, tr: jnp.swapaxes(a, 1, 2) if tr else a
    shard = jnp.concatenate([as_rows(w[n], tr)[0].astype(bf16) for n, tr in BIG], axis=0)
    exchanges = _GatheredWeights(shard)
    gx, (dw1, _, _, _), small = _local_step(
        x[0], loss_target[0], exchanges, ffn1_norm, mix_norm, ffn2_norm, q_norm, k_norm, attn_sinks,
        rel_bias, pool_w[0], pool_scale)

    red1 = _reduce_scatter_ffn1(dw1)
    red_rest = exchanges.mix_ffn2_grads_total(red1)
    small_tot = _all_reduce_small(_pack_small(small))

    grads, deltas, new_m, new_v = {}, {}, {}, {}
    big_out = _adamw_big(*[[as_rows(t[n], tr) for n, tr in BIG] for t in (w, m, v)], red1, red_rest)
    for k, (n, tr) in enumerate(BIG):
        grads[n], deltas[n], new_m[n], new_v[n] = [as_rows(o[k], tr) for o in big_out]
    small_names = [n for n in SMALL_NAMES if n != "loss"]
    ds, nms, nvs = _adamw_small(_pack_small({n: w[n] for n in small_names}), _pack_small({n: m[n] for n in small_names}),
                                _pack_small({n: v[n] for n in small_names}), small_tot, "adamw_small")
    for n in small_names:
        grads[n] = _unpack_small(small_tot, n)
        deltas[n], new_m[n], new_v[n] = _unpack_small(ds, n), _unpack_small(nms, n), _unpack_small(nvs, n)
    loss = small_tot[LOSS_ROW, 0]
    return (loss, gx[None], *[grads[n] for n in WEIGHTS], *[deltas[n] for n in WEIGHTS],
            *[new_m[n] for n in WEIGHTS], *[new_v[n] for n in WEIGHTS])
```

```python
import functools

import jax
import jax.numpy as jnp
import numpy as np
from jax import lax
from jax.experimental import pallas as pl
from jax.experimental.pallas import tpu as pltpu

f32, bf16, i32 = jnp.float32, jnp.bfloat16, jnp.int32
SDS = jax.ShapeDtypeStruct

D = 1024
F = 2816
HD = 64
NH = 8
NKV = 2
GQA = NH // NKV
DATTN = NH * HD
DKV = NKV * HD
DPOOL = 512
POOL_WINDOWS = (2, 4, 8, 16)
PGD = DPOOL // len(POOL_WINDOWS)
DIN = DATTN + 2 * DKV + DPOOL
DMIX = DATTN + DPOOL
BLK = 128
NBUCK = 32
MAX_DISTANCE = 128
EPS = 1e-6
NEG = -1e30
SCALE = HD ** -0.5

ADAM_LR, ADAM_B1, ADAM_B2, ADAM_EPS, ADAM_WD, ADAM_STEP = 0.001, 0.9, 0.999, 1e-08, 0.01, 10

NDEV = 8
FS = F // NDEV
INS = DIN // NDEV
OUTS = DMIX // NDEV
PIECE_ROWS = (FS, FS, FS, INS, OUTS, FS, FS, FS)
PIECE_OFF = tuple(int(v) for v in np.cumsum((0,) + PIECE_ROWS[:-1]))
PACK_ROWS = sum(PIECE_ROWS)

VMEM_LIMIT_V7X = 56 * 1024 * 1024

MESH = pl.DeviceIdType.MESH


def _cparams(sem=None, vmem=None):
    return pltpu.CompilerParams(dimension_semantics=sem, vmem_limit_bytes=vmem)


def _nt(a, b):
    return lax.dot_general(a, b, (((1,), (1,)), ((), ())), preferred_element_type=f32)


def _tn(a, b):
    return lax.dot_general(a, b, (((0,), (0,)), ((), ())), preferred_element_type=f32)


def _nn(a, b):
    return jnp.dot(a, b, preferred_element_type=f32)


def _sigmoid(x):
    return 1.0 / (1.0 + jnp.exp(-x))


def _norm_fwd(x, g, name):
    T = x.shape[0]
    tm = min(512, T)

    def body(x_ref, g_ref, h_ref):
        xv = x_ref[...]
        r = lax.rsqrt(jnp.mean(xv * xv, axis=-1, keepdims=True) + EPS)
        h_ref[...] = (xv * r * g_ref[...]).astype(bf16)

    return pl.pallas_call(
        body, grid=(T // tm,),
        in_specs=[pl.BlockSpec((tm, D), lambda i: (i, 0)), pl.BlockSpec((1, D), lambda i: (0, 0))],
        out_specs=pl.BlockSpec((tm, D), lambda i: (i, 0)),
        out_shape=SDS((T, D), bf16), name=name)(x, g)


def _norm_bwd(dh, x, g, dres, out_scale, name):
    T = x.shape[0]
    tm = min(512, T)

    def body(dh_ref, x_ref, g_ref, dr_ref, dx_ref, dxb_ref, dg_ref):
        i = pl.program_id(0)
        xv = x_ref[...]
        r = lax.rsqrt(jnp.mean(xv * xv, axis=-1, keepdims=True) + EPS)
        xh = xv * r
        dhv = dh_ref[...]
        dxh = dhv * g_ref[...]
        dx = dr_ref[...] + r * (dxh - xh * jnp.mean(dxh * xh, axis=-1, keepdims=True))
        dx_ref[...] = dx
        dxb_ref[...] = (out_scale * dx).astype(bf16)
        dg = jnp.sum(dhv * xh, axis=0, keepdims=True)

        @pl.when(i == 0)
        def _():
            dg_ref[...] = dg

        @pl.when(i > 0)
        def _():
            dg_ref[...] += dg

    tok = pl.BlockSpec((tm, D), lambda i: (i, 0))
    vec = pl.BlockSpec((1, D), lambda i: (0, 0))
    return pl.pallas_call(
        body, grid=(T // tm,),
        in_specs=[tok, tok, vec, tok], out_specs=[tok, tok, vec],
        out_shape=(SDS((T, D), f32), SDS((T, D), bf16), SDS((1, D), f32)),
        compiler_params=_cparams(("arbitrary",)), name=name)(dh, x, g, dres)


FFN_ROW_CHUNK = 256


def _ffn_tiles(T):
    return min(1024, T), 256


def _ffn_fwd(h, w, x, name):
    T = h.shape[0]
    tm, tf = _ffn_tiles(T)
    nf = F // tf

    def body(h_ref, w_ref, x_ref, xo_ref, g_ref, u_ref, acc):
        fi = pl.program_id(1)

        @pl.when(fi == 0)
        def _():
            acc[...] = jnp.zeros_like(acc)

        gu = _nt(h_ref[...], w_ref[0:2].reshape(2 * tf, D))
        gate, up = gu[:, :tf], gu[:, tf:]
        act = gate * _sigmoid(gate) * up
        g_ref[...] = gate.astype(bf16)
        u_ref[...] = up.astype(bf16)
        acc[...] += _nn(act.astype(bf16), w_ref[2])

        @pl.when(fi == nf - 1)
        def _():
            xo_ref[...] = x_ref[...] + 0.5 * acc[...]

    tok = pl.BlockSpec((tm, D), lambda i, f: (i, 0))
    act_spec = pl.BlockSpec((tm, tf), lambda i, f: (i, f))
    return pl.pallas_call(
        body, grid=(T // tm, nf),
        in_specs=[tok, pl.BlockSpec((3, tf, D), lambda i, f: (0, f, 0)), tok],
        out_specs=[tok, act_spec, act_spec],
        out_shape=(SDS((T, D), f32), SDS((T, F), bf16), SDS((T, F), bf16)),
        scratch_shapes=[pltpu.VMEM((tm, D), f32)],
        compiler_params=_cparams(("arbitrary", "arbitrary"), VMEM_LIMIT_V7X), name=name)(h, w, x)


def _ffn_bwd(dob, h, gate, up, w, name):
    T = h.shape[0]
    _, tf = _ffn_tiles(T)
    nf = F // tf

    def body(do_hbm, h_hbm, g_ref, u_ref, w_ref, dh_hbm, dw_ref, do_v, h_v, dh_acc, dgu_s, act_s, sems):
        fi = pl.program_id(0)

        @pl.when(fi == 0)
        def _():
            loads = [pltpu.make_async_copy(do_hbm, do_v, sems.at[0]), pltpu.make_async_copy(h_hbm, h_v, sems.at[1])]
            for cp in loads:
                cp.start()
            dh_acc[...] = jnp.zeros_like(dh_acc)
            for cp in loads:
                cp.wait()

        wgu = w_ref[0:2].reshape(2 * tf, D)
        for r in range(0, T, FFN_ROW_CHUNK):
            rows = slice(r, r + FFN_ROW_CHUNK)
            dov = do_v[rows, :]
            gv = g_ref[rows, :].astype(f32)
            uv = u_ref[rows, :].astype(f32)
            sg = _sigmoid(gv)
            sil = gv * sg
            dact = _nt(dov, w_ref[2])
            dup = dact * sil
            dgate = dact * uv * (sg * (1.0 + gv * (1.0 - sg)))
            dgu = jnp.concatenate([dgate.astype(bf16), dup.astype(bf16)], axis=1)
            dgu_s[rows, :] = dgu
            act_s[rows, :] = (sil * uv).astype(bf16)
            dh_acc[rows, :] += _nn(dgu, wgu)
        dw_ref[0:2] = _tn(dgu_s[...], h_v[...]).reshape(2, tf, D).astype(bf16)
        dw_ref[2] = _tn(act_s[...], do_v[...]).astype(bf16)

        @pl.when(fi == nf - 1)
        def _():
            out = pltpu.make_async_copy(dh_acc, dh_hbm, sems.at[0])
            out.start()
            out.wait()

    act_spec = pl.BlockSpec((T, tf), lambda f: (0, f))
    wspec = pl.BlockSpec((3, tf, D), lambda f: (0, f, 0))
    hbm = pl.BlockSpec(memory_space=pl.ANY)
    return pl.pallas_call(
        body, grid=(nf,),
        in_specs=[hbm, hbm, act_spec, act_spec, wspec],
        out_specs=[hbm, wspec],
        out_shape=(SDS((T, D), f32), SDS((3, F, D), bf16)),
        scratch_shapes=[pltpu.VMEM((T, D), bf16), pltpu.VMEM((T, D), bf16), pltpu.VMEM((T, D), f32),
                        pltpu.VMEM((T, 2 * tf), bf16), pltpu.VMEM((T, tf), bf16), pltpu.SemaphoreType.DMA((2,))],
        compiler_params=_cparams(("arbitrary",), VMEM_LIMIT_V7X), name=name)(dob, h, gate, up, w)


def _loss_grad(y, target, name):
    T = y.shape[0]
    tm = min(512, T)

    def body(y_ref, t_ref, dy_ref, dyb_ref, l_ref):
        i = pl.program_id(0)
        e = y_ref[...] - t_ref[...]
        dy = e * (1.0 / D)
        dy_ref[...] = dy
        dyb_ref[...] = (0.5 * dy).astype(bf16)
        col = jnp.sum(e * e, axis=0, keepdims=True) * (0.5 / D)
        lanes = col[:, 0:128]
        for k in range(1, D // 128):
            lanes = lanes + col[:, 128 * k:128 * (k + 1)]

        @pl.when(i == 0)
        def _():
            l_ref[...] = lanes

        @pl.when(i > 0)
        def _():
            l_ref[...] += lanes

    tok = pl.BlockSpec((tm, D), lambda i: (i, 0))
    return pl.pallas_call(
        body, grid=(T // tm,), in_specs=[tok, tok],
        out_specs=[tok, tok, pl.BlockSpec((1, 128), lambda i: (0, 0))],
        out_shape=(SDS((T, D), f32), SDS((T, D), bf16), SDS((1, 128), f32)),
        compiler_params=_cparams(("arbitrary",)), name=name)(y, target)


def _in_proj_fwd(h, wint, name):
    T = h.shape[0]
    tm = min(512, T)

    def body(h_ref, w_ref, z_ref):
        z_ref[...] = _nt(h_ref[...], w_ref[...])

    return pl.pallas_call(
        body, grid=(T // tm,),
        in_specs=[pl.BlockSpec((tm, D), lambda i: (i, 0)), pl.BlockSpec((DIN, D), lambda i: (0, 0))],
        out_specs=pl.BlockSpec((tm, DIN), lambda i: (i, 0)),
        out_shape=SDS((T, DIN), f32), name=name)(h, wint)


def _in_proj_bwd(dz, wint, h, name):
    T = h.shape[0]
    tm = min(512, T)
    nt = T // tm

    def body(dz_ref, w_ref, h_ref, dh_ref, dw_ref, acc):
        i = pl.program_id(0)
        dzb = dz_ref[...].astype(bf16)
        dh_ref[...] = _nn(dzb, w_ref[...])
        part = _tn(dzb, h_ref[...])

        @pl.when(i == 0)
        def _():
            acc[...] = part

        @pl.when(i > 0)
        def _():
            acc[...] += part

        @pl.when(i == nt - 1)
        def _():
            dw_ref[...] = acc[...].astype(bf16)

    wspec = pl.BlockSpec((DIN, D), lambda i: (0, 0))
    return pl.pallas_call(
        body, grid=(nt,),
        in_specs=[pl.BlockSpec((tm, DIN), lambda i: (i, 0)), wspec, pl.BlockSpec((tm, D), lambda i: (i, 0))],
        out_specs=[pl.BlockSpec((tm, D), lambda i: (i, 0)), wspec],
        out_shape=(SDS((T, D), f32), SDS((DIN, D), bf16)),
        scratch_shapes=[pltpu.VMEM((DIN, D), f32)],
        compiler_params=_cparams(("arbitrary",)), name=name)(dz, wint, h)


def _out_proj_fwd(ymix, wout, x, name):
    T = x.shape[0]
    tm = min(512, T)

    def body(y_ref, w_ref, x_ref, o_ref):
        o_ref[...] = x_ref[...] + _nn(y_ref[...], w_ref[...])

    tok = pl.BlockSpec((tm, D), lambda i: (i, 0))
    return pl.pallas_call(
        body, grid=(T // tm,),
        in_specs=[pl.BlockSpec((tm, DMIX), lambda i: (i, 0)), pl.BlockSpec((DMIX, D), lambda i: (0, 0)), tok],
        out_specs=tok, out_shape=SDS((T, D), f32), name=name)(ymix, wout, x)


def _out_proj_bwd(dxb, wout, ymix, name):
    T = dxb.shape[0]
    tm = min(512, T)
    nt = T // tm

    def body(dx_ref, w_ref, y_ref, dy_ref, dw_ref, acc):
        i = pl.program_id(0)
        dxv = dx_ref[...]
        dy_ref[...] = _nt(dxv, w_ref[...])
        part = _tn(y_ref[...], dxv)

        @pl.when(i == 0)
        def _():
            acc[...] = part

        @pl.when(i > 0)
        def _():
            acc[...] += part

        @pl.when(i == nt - 1)
        def _():
            dw_ref[...] = acc[...].astype(bf16)

    wspec = pl.BlockSpec((DMIX, D), lambda i: (0, 0))
    return pl.pallas_call(
        body, grid=(nt,),
        in_specs=[pl.BlockSpec((tm, D), lambda i: (i, 0)), wspec, pl.BlockSpec((tm, DMIX), lambda i: (i, 0))],
        out_specs=[pl.BlockSpec((tm, DMIX), lambda i: (i, 0)), wspec],
        out_shape=(SDS((T, DMIX), f32), SDS((DMIX, D), bf16)),
        scratch_shapes=[pltpu.VMEM((DMIX, D), f32)],
        compiler_params=_cparams(("arbitrary",)), name=name)(dxb, wout, ymix)


def _t5_bucket_table():
    ql = np.arange(BLK)[:, None]
    kl = np.arange(2 * BLK)[None, :]
    n = np.maximum(ql + BLK - kl, 0)
    max_exact = NBUCK // 2
    large = max_exact + (np.log(np.maximum(n, 1) / max_exact) / np.log(MAX_DISTANCE / max_exact)
                         * (NBUCK - max_exact)).astype(np.int32)
    large = np.minimum(large, NBUCK - 1)
    return np.where(n < max_exact, n, large).astype(np.int32)


def _fill_bias(bk_ref, rb_ref, bias_scr):
    bk = bk_ref[...]
    for h in range(NH):
        def step(b, acc, h=h):
            return acc + jnp.where(bk == b, rb_ref[b, h], 0.0)
        bias_scr[h] = lax.fori_loop(0, NBUCK, step, jnp.zeros((BLK, 2 * BLK), f32))


MIX_SUB = 2


class _Window:
    def __init__(self, zc_ref, zp_ref, n, s):
        self.blk = n * MIX_SUB + s
        self.first_in_step = s == 0
        self.cur = lambda a, b: zc_ref[s * BLK:(s + 1) * BLK, a:b]
        self.prev = (lambda a, b: zp_ref[:, a:b]) if s == 0 else (lambda a, b: zc_ref[(s - 1) * BLK:s * BLK, a:b])


def _attn_probs(win, kh, qg, kg, sk_ref, bias_scr):
    n = win.blk
    kc = DATTN + HD * kh
    vc = DATTN + DKV + HD * kh
    kx = jnp.concatenate([win.prev(kc, kc + HD), win.cur(kc, kc + HD)], axis=0)
    vx = jnp.concatenate([win.prev(vc, vc + HD), win.cur(vc, vc + HD)], axis=0)
    qx = jnp.concatenate([win.cur(HD * (GQA * kh + g), HD * (GQA * kh + g + 1)) for g in range(GQA)], axis=0)
    rq = lax.rsqrt(jnp.mean(qx * qx, axis=-1, keepdims=True) + EPS)
    rk = lax.rsqrt(jnp.mean(kx * kx, axis=-1, keepdims=True) + EPS)
    qhat, khat = qx * rq, kx * rk
    qnb, knb = (qhat * qg).astype(bf16), (khat * kg).astype(bf16)
    s = _nt(qnb, knb) * SCALE + bias_scr[GQA * kh:GQA * (kh + 1)].reshape(GQA * BLK, 2 * BLK)
    row = lax.broadcasted_iota(i32, (GQA * BLK, 2 * BLK), 0) & (BLK - 1)
    col = lax.broadcasted_iota(i32, (GQA * BLK, 2 * BLK), 1)
    mask = (col > row) & (col <= row + BLK) & ((col >= BLK) | (n > 0))
    s = jnp.where(mask, s, NEG)
    ridx = lax.broadcasted_iota(i32, (GQA * BLK, 1), 0)
    sink = jnp.full((GQA * BLK, 1), sk_ref[GQA * kh + GQA - 1], f32)
    for g in range(GQA - 2, -1, -1):
        sink = jnp.where(ridx < (g + 1) * BLK, sk_ref[GQA * kh + g], sink)
    m = jnp.maximum(jnp.max(s, axis=-1, keepdims=True), sink)
    e = jnp.exp(s - m)
    es = jnp.exp(sink - m)
    den = jnp.sum(e, axis=-1, keepdims=True) + es
    return dict(p=e / den, psink=es / den, qhat=qhat, khat=khat, rq=rq, rk=rk, qnb=qnb, knb=knb, vb=vx.astype(bf16))


def _pool_group(win, g, w):
    n = win.blk
    c0 = DATTN + 2 * DKV + PGD * g
    uc = win.cur(c0, c0 + PGD)
    up = jnp.where(n > 0, win.prev(c0, c0 + PGD), 0.0)
    ue = jnp.concatenate([up, uc], axis=0)
    hi = ue.astype(bf16)
    lo = (ue - hi.astype(f32)).astype(bf16)
    t = lax.broadcasted_iota(i32, (BLK, 2 * BLK), 0)
    s = lax.broadcasted_iota(i32, (BLK, 2 * BLK), 1)
    band = jnp.where((s <= t + BLK) & (s > t + BLK - w), 1.0, 0.0).astype(bf16)
    sm = _nn(band, hi) + _nn(band, lo)
    pos = n * BLK + lax.broadcasted_iota(i32, (BLK, 1), 0) + 1
    cnt = jnp.minimum(pos, w).astype(f32)
    return sm / cnt - uc, band, cnt


def _mix_fwd(z, qg, kg, sinks, relb, bucket, pool_w, pscale, name):
    T = z.shape[0]
    step_rows = MIX_SUB * BLK
    nsteps = T // step_rows

    def body(zc_ref, zp_ref, qg_ref, kg_ref, sk_ref, rb_ref, bk_ref, pw_ref, ps_ref, y_ref, bias_scr, yacc):
        n = pl.program_id(0)

        @pl.when(n == 0)
        def _():
            _fill_bias(bk_ref, rb_ref, bias_scr)

        for s in range(MIX_SUB):
            win = _Window(zc_ref, zp_ref, n, s)
            rows = slice(s * BLK, (s + 1) * BLK)
            for kh in range(NKV):
                a = _attn_probs(win, kh, qg_ref[...], kg_ref[...], sk_ref, bias_scr)
                o = _nn(a["p"].astype(bf16), a["vb"])
                for g in range(GQA):
                    hc = HD * (GQA * kh + g)
                    yacc[rows, hc:hc + HD] = o[g * BLK:(g + 1) * BLK]
            for g, w in enumerate(POOL_WINDOWS):
                pooled, _, _ = _pool_group(win, g, w)
                yp = _nn(pooled.astype(bf16), pw_ref[g].astype(bf16)) * ps_ref[:, PGD * g:PGD * (g + 1)]
                yacc[rows, DATTN + PGD * g:DATTN + PGD * (g + 1)] = yp
        y_ref[...] = yacc[...].astype(bf16)

    full = lambda *shape: pl.BlockSpec(shape, lambda n: (0,) * len(shape))
    smem = pl.BlockSpec(memory_space=pltpu.SMEM)
    return pl.pallas_call(
        body, grid=(nsteps,),
        in_specs=[pl.BlockSpec((step_rows, DIN), lambda n: (n, 0)),
                  pl.BlockSpec((BLK, DIN), lambda n: (jnp.maximum(n * MIX_SUB - 1, 0), 0)),
                  full(1, HD), full(1, HD), smem, smem, full(BLK, 2 * BLK),
                  full(len(POOL_WINDOWS), PGD, PGD), full(1, DPOOL)],
        out_specs=pl.BlockSpec((step_rows, DMIX), lambda n: (n, 0)),
        out_shape=SDS((T, DMIX), bf16),
        scratch_shapes=[pltpu.VMEM((NH, BLK, 2 * BLK), f32), pltpu.VMEM((step_rows, DMIX), f32)],
        compiler_params=_cparams(("arbitrary",)), name=name)(z, z, qg, kg, sinks, relb, bucket, pool_w, pscale)


def _mix_bwd(z, dy, qg, kg, sinks, relb, bucket, pool_w, pscale, name):
    T = z.shape[0]
    step_rows = MIX_SUB * BLK
    nsteps = T // step_rows

    def body(zc_ref, zp_ref, dy_ref, qg_ref, kg_ref, sk_ref, rb_ref, bk_ref, pw_ref, ps_ref,
             dz_ref, dqg_ref, dkg_ref, dsk_ref, drb_ref, dpw_ref, dps_ref, bias_scr, dbias_scr):
        n = pl.program_id(0)

        @pl.when(n == 0)
        def _():
            _fill_bias(bk_ref, rb_ref, bias_scr)
            dbias_scr[...] = jnp.zeros_like(dbias_scr)
            dqg_ref[...] = jnp.zeros_like(dqg_ref)
            dkg_ref[...] = jnp.zeros_like(dkg_ref)
            dsk_ref[...] = jnp.zeros_like(dsk_ref)
            dpw_ref[...] = jnp.zeros_like(dpw_ref)
            dps_ref[...] = jnp.zeros_like(dps_ref)

        qg, kg = qg_ref[...], kg_ref[...]
        lane = lax.broadcasted_iota(i32, (1, 128), 1)
        dsk = jnp.zeros((1, 128), f32)
        for s in range(MIX_SUB):
            win = _Window(zc_ref, zp_ref, n, s)
            blk = win.blk
            rows = pl.ds(pl.multiple_of(blk * BLK, BLK), BLK)
            prow = pl.ds(pl.multiple_of(jnp.maximum(blk - 1, 0) * BLK, BLK), BLK)
            dyr = slice(s * BLK, (s + 1) * BLK)

            def into_prev(fn, s=s):
                if s == 0:
                    pl.when(n > 0)(fn)
                else:
                    fn()

            for kh in range(NKV):
                a = _attn_probs(win, kh, qg, kg, sk_ref, bias_scr)
                p = a["p"]
                do = jnp.concatenate([dy_ref[dyr, HD * (GQA * kh + g):HD * (GQA * kh + g + 1)] for g in range(GQA)],
                                     axis=0).astype(bf16)
                dv = _tn(p.astype(bf16), do)
                dp = _nt(do, a["vb"])
                delta = jnp.sum(p * dp, axis=-1, keepdims=True)
                ds = p * (dp - delta)
                sinkterm = a["psink"] * delta
                for g in range(GQA):
                    h = GQA * kh + g
                    dbias_scr[h] += ds[g * BLK:(g + 1) * BLK]
                    tot = jnp.sum(sinkterm[g * BLK:(g + 1) * BLK], axis=0, keepdims=True)
                    dsk = dsk - jnp.where(lane == h, tot, 0.0)
                dsb = ds.astype(bf16)
                dqn = _nn(dsb, a["knb"]) * SCALE
                dkn = _tn(dsb, a["qnb"]) * SCALE
                qhat, khat = a["qhat"], a["khat"]
                dqg_ref[...] += jnp.sum(dqn * qhat, axis=0, keepdims=True)
                dkg_ref[...] += jnp.sum(dkn * khat, axis=0, keepdims=True)
                dqh = dqn * qg
                dq = a["rq"] * (dqh - qhat * jnp.mean(dqh * qhat, axis=-1, keepdims=True))
                dkh = dkn * kg
                dk = a["rk"] * (dkh - khat * jnp.mean(dkh * khat, axis=-1, keepdims=True))
                kc = DATTN + HD * kh
                vc = DATTN + DKV + HD * kh
                for g in range(GQA):
                    hc = HD * (GQA * kh + g)
                    dz_ref[rows, hc:hc + HD] = dq[g * BLK:(g + 1) * BLK]
                dz_ref[rows, kc:kc + HD] = dk[BLK:2 * BLK]
                dz_ref[rows, vc:vc + HD] = dv[BLK:2 * BLK]

                def kv_prev(dk=dk, dv=dv, kc=kc, vc=vc, prow=prow):
                    dz_ref[prow, kc:kc + HD] += dk[0:BLK]
                    dz_ref[prow, vc:vc + HD] += dv[0:BLK]

                into_prev(kv_prev)

            for g, w in enumerate(POOL_WINDOWS):
                c0 = DATTN + 2 * DKV + PGD * g
                pooled, band, cnt = _pool_group(win, g, w)
                pb = pooled.astype(bf16)
                wb = pw_ref[g].astype(bf16)
                dyp = dy_ref[dyr, DATTN + PGD * g:DATTN + PGD * (g + 1)]
                ypre = _nn(pb, wb)
                dps_ref[:, PGD * g:PGD * (g + 1)] += jnp.sum(dyp * ypre, axis=0, keepdims=True)
                dyg = (dyp * ps_ref[:, PGD * g:PGD * (g + 1)]).astype(bf16)
                dpw_ref[g] += _tn(pb, dyg)
                dpooled = _nt(dyg, wb)
                dsm = dpooled / cnt
                hi = dsm.astype(bf16)
                lo = (dsm - hi.astype(f32)).astype(bf16)
                due = _tn(band, hi) + _tn(band, lo)
                dz_ref[rows, c0:c0 + PGD] = due[BLK:2 * BLK] - dpooled

                def pool_prev(due=due, c0=c0, prow=prow):
                    dz_ref[prow, c0:c0 + PGD] += due[0:BLK]

                into_prev(pool_prev)

        dsk_ref[...] += dsk

        @pl.when(n == nsteps - 1)
        def _():
            bk = bk_ref[...]
            ri = lax.broadcasted_iota(i32, (NBUCK, NH), 0)
            ci = lax.broadcasted_iota(i32, (NBUCK, NH), 1)

            def step(b, acc):
                for h in range(NH):
                    sel = jnp.where(bk == b, dbias_scr[h], 0.0)
                    tot = jnp.sum(jnp.sum(sel, axis=1, keepdims=True), axis=0, keepdims=True)
                    acc = acc + jnp.where((ri == b) & (ci == h), tot, 0.0)
                return acc

            drb_ref[...] = lax.fori_loop(0, NBUCK, step, jnp.zeros((NBUCK, NH), f32))

    full = lambda *shape: pl.BlockSpec(shape, lambda n: (0,) * len(shape))
    smem = pl.BlockSpec(memory_space=pltpu.SMEM)
    npg = len(POOL_WINDOWS)
    return pl.pallas_call(
        body, grid=(nsteps,),
        in_specs=[pl.BlockSpec((step_rows, DIN), lambda n: (n, 0)),
                  pl.BlockSpec((BLK, DIN), lambda n: (jnp.maximum(n * MIX_SUB - 1, 0), 0)),
                  pl.BlockSpec((step_rows, DMIX), lambda n: (n, 0)),
                  full(1, HD), full(1, HD), smem, smem, full(BLK, 2 * BLK), full(npg, PGD, PGD), full(1, DPOOL)],
        out_specs=[full(T, DIN), full(1, HD), full(1, HD), full(1, 128), full(NBUCK, NH),
                   full(npg, PGD, PGD), full(1, DPOOL)],
        out_shape=(SDS((T, DIN), f32), SDS((1, HD), f32), SDS((1, HD), f32), SDS((1, 128), f32),
                   SDS((NBUCK, NH), f32), SDS((npg, PGD, PGD), f32), SDS((1, DPOOL), f32)),
        scratch_shapes=[pltpu.VMEM((NH, BLK, 2 * BLK), f32), pltpu.VMEM((NH, BLK, 2 * BLK), f32)],
        compiler_params=_cparams(("arbitrary",), VMEM_LIMIT_V7X),
        name=name)(z, z, dy, qg, kg, sinks, relb, bucket, pool_w, pscale)


class _LocalWeights:
    def __init__(self, w1, wint, wout, w2):
        self.w1, self.wint, self.wout, self.w2 = w1, wint, wout, w2

    def ffn1(self):
        return self.w1

    def mix(self, after):
        return self.wint, self.wout

    def before_out_proj(self, wout, after):
        return wout

    def ffn2(self, after):
        return self.w2

    def mix_ffn2_grads_ready(self, dwint, dwout, dw2, dh2):
        self.grads_rest = (dwint, dwout, dw2)
        return dh2

    def before_ffn1_bwd(self, dx1b):
        return dx1b


def _local_step(x, target, weights, g1, gm, g3, qg, kg, sinks, relb, pool_w, pscale):
    bucket = jnp.asarray(_t5_bucket_table())
    sk = sinks.reshape(NH)
    w1 = weights.ffn1()
    h1 = _norm_fwd(x, g1, "norm1_fwd")
    x1, gate1, up1 = _ffn_fwd(h1, w1, x, "ffn1_fwd")
    h2 = _norm_fwd(x1, gm, "norm2_fwd")
    wint, wout = weights.mix(h2)
    z = _in_proj_fwd(h2, wint, "in_proj_fwd")
    ymix = _mix_fwd(z, qg, kg, sk, relb, bucket, pool_w, pscale, "mix_fwd")
    wout = weights.before_out_proj(wout, ymix)
    x2 = _out_proj_fwd(ymix, wout, x1, "out_proj_fwd")
    h3 = _norm_fwd(x2, g3, "norm3_fwd")
    w2 = weights.ffn2(h3)
    y, gate2, up2 = _ffn_fwd(h3, w2, x2, "ffn2_fwd")
    dy, dyb, loss_lanes = _loss_grad(y, target, "loss_grad")

    dh3, dw2 = _ffn_bwd(dyb, h3, gate2, up2, w2, "ffn2_bwd")
    dx2, dx2b, dg3 = _norm_bwd(dh3, x2, g3, dy, 1.0, "norm3_bwd")
    dymix, dwout = _out_proj_bwd(dx2b, wout, ymix, "out_proj_bwd")
    dz, dqg, dkg, dsk, drb, dpw, dps = _mix_bwd(z, dymix, qg, kg, sk, relb, bucket, pool_w, pscale, "mix_bwd")
    dh2, dwint = _in_proj_bwd(dz, wint, h2, "in_proj_bwd")
    dh2 = weights.mix_ffn2_grads_ready(dwint, dwout, dw2, dh2)
    dx1, dx1b, dgm = _norm_bwd(dh2, x1, gm, dx2, 0.5, "norm2_bwd")
    dx1b = weights.before_ffn1_bwd(dx1b)
    dh1, dw1 = _ffn_bwd(dx1b, h1, gate1, up1, w1, "ffn1_bwd")
    gx, _, dg1 = _norm_bwd(dh1, x, g1, dx1, 1.0, "norm1_bwd")
    small = dict(ffn1_norm=dg1, mix_norm=dgm, ffn2_norm=dg3, pool_scale=dps, q_norm=dqg, k_norm=dkg,
                 attn_sinks=dsk[:, :NH], rel_bias=drb, pool_w=dpw, loss=loss_lanes)
    return gx, (dw1, dwint, dwout, dw2), small


SMALL_NAMES = ("ffn1_norm", "mix_norm", "ffn2_norm", "pool_scale", "q_norm", "k_norm", "attn_sinks", "rel_bias",
               "pool_w", "loss")
SMALL_SHAPES = dict(ffn1_norm=(1, D), mix_norm=(1, D), ffn2_norm=(1, D), pool_scale=(1, DPOOL), q_norm=(1, HD),
                    k_norm=(1, HD), attn_sinks=(1, NH), rel_bias=(NBUCK, NH),
                    pool_w=(1, len(POOL_WINDOWS), PGD, PGD), loss=(1, 128))


def _small_rows(name):
    return -(-int(np.prod(SMALL_SHAPES[name])) // 128)


SMALL_OFF = {}
_r = 0
for _n in SMALL_NAMES:
    SMALL_OFF[_n] = _r
    _r += _small_rows(_n)
SMALL_ROWS = -(-_r // 8) * 8
LOSS_ROW = SMALL_OFF["loss"]


def _pack_small(vals):
    parts = []
    for n in SMALL_NAMES:
        size = _small_rows(n) * 128
        if n in vals:
            flat = vals[n].astype(f32).reshape(-1)
            parts.append(jnp.pad(flat, (0, size - flat.shape[0])))
        else:
            parts.append(jnp.zeros((size,), f32))
    flat = jnp.concatenate(parts)
    flat = jnp.pad(flat, (0, SMALL_ROWS * 128 - flat.shape[0]))
    return flat.reshape(SMALL_ROWS, 128)


def _unpack_small(packed, name):
    size = int(np.prod(SMALL_SHAPES[name]))
    r0 = SMALL_OFF[name]
    return packed[r0:r0 + _small_rows(name)].reshape(-1)[:size].reshape(SMALL_SHAPES[name])


def _position():
    return lax.axis_index("x"), lax.axis_index("y"), lax.axis_index("c")


def _dev_index(x, y, c):
    return 4 * x + 2 * y + c


G1_PIECES, MIX_PIECES, F2_PIECES = (0, 1, 2), (3, 4), (5, 6, 7)


def _group_rows(pieces):
    return sum(PIECE_ROWS[k] for k in pieces)


def _shard_piece(s_ref, k):
    return s_ref.at[pl.ds(PIECE_OFF[k], PIECE_ROWS[k]), :]


def _shard_group(s_ref, pieces):
    return s_ref.at[pl.ds(PIECE_OFF[pieces[0]], _group_rows(pieces)), :]


def _weight_pieces(w1_ref=None, wi_ref=None, wo_ref=None, w2_ref=None):
    arrs = {}
    if w1_ref is not None:
        arrs.update({0: w1_ref.at[0], 1: w1_ref.at[1], 2: w1_ref.at[2]})
    if wi_ref is not None:
        arrs[3] = wi_ref
    if wo_ref is not None:
        arrs[4] = wo_ref
    if w2_ref is not None:
        arrs.update({5: w2_ref.at[0], 6: w2_ref.at[1], 7: w2_ref.at[2]})
    return arrs


def _block_rows(arrs, k, dev):
    r = PIECE_ROWS[k]
    return arrs[k].at[pl.ds(pl.multiple_of(_dev_index(*dev) * r, 16), r), :]


def _all_gather_ffn1(shard):
    pieces = G1_PIECES

    def body(s_ref, w1_ref, send_sems, recv_sems, local_sem):
        x, y, c = _position()
        me, sib = (x, y, c), (x, y, 1 - c)
        chips = [(1 - x, y), (x, 1 - y), (1 - x, 1 - y)]
        arrs = _weight_pieces(w1_ref=w1_ref)

        def copies(rel, block, to, from_shard):
            return [pltpu.make_async_remote_copy(
                src_ref=_shard_piece(s_ref, k) if from_shard else _block_rows(arrs, k, block),
                dst_ref=_block_rows(arrs, k, block),
                send_sem=send_sems.at[rel], recv_sem=recv_sems.at[rel], device_id=to, device_id_type=MESH)
                for k in pieces]

        def whole(rel):
            grp = _shard_group(s_ref, pieces)
            return pltpu.make_async_remote_copy(src_ref=grp, dst_ref=grp, send_sem=send_sems.at[rel],
                                                recv_sem=recv_sems.at[rel], device_id=me, device_id_type=MESH)

        mine = [pltpu.make_async_copy(_shard_piece(s_ref, k), _block_rows(arrs, k, me), local_sem) for k in pieces]
        for cp in mine:
            cp.start()
        for cp in copies(0, me, sib, True):
            cp.start()
        for j, chip in enumerate(chips):
            for cp in copies(1 + j, me, (*chip, c), True):
                cp.start()
        for j, chip in enumerate(chips):
            whole(1 + j).wait_recv()
            for cp in copies(4 + j, (*chip, c), sib, False):
                cp.start()
        whole(0).wait_recv()
        for j in range(3):
            whole(4 + j).wait_recv()
        for rel in range(7):
            whole(rel).wait_send()
        grp = _shard_group(s_ref, pieces)
        pltpu.make_async_copy(grp, grp, local_sem).wait()

    hbm = pl.BlockSpec(memory_space=pl.ANY)
    return pl.pallas_call(
        body, in_specs=[hbm], out_specs=hbm, out_shape=SDS((3, F, D), bf16),
        scratch_shapes=[pltpu.SemaphoreType.DMA((7,)), pltpu.SemaphoreType.DMA((7,)), pltpu.SemaphoreType.DMA],
        compiler_params=pltpu.CompilerParams(has_side_effects=True),
        name="all_gather_ffn1")(shard)


HBM_SPEC = pl.BlockSpec(memory_space=pltpu.HBM)
SEM_SPEC = pl.BlockSpec(memory_space=pltpu.SEMAPHORE)
ANY_SPEC = pl.BlockSpec(memory_space=pl.ANY)
SPLIT_EFFECT = pltpu.SideEffectType.DATAFLOW_SIDE_EFFECTING


def _in_hbm(a):
    return pltpu.with_memory_space_constraint(a, pltpu.HBM)


def _hbm_like(a):
    return pltpu.HBM(a.shape, a.dtype)


def _place_own_rows(shard):
    pieces = MIX_PIECES + F2_PIECES

    def body(s_ref, wi_ref, wo_ref, w2_ref, buf, sems):
        x, y, c = _position()
        arrs = _weight_pieces(wi_ref=wi_ref, wo_ref=wo_ref, w2_ref=w2_ref)
        grp = _shard_group(s_ref, pieces)
        load = pltpu.make_async_copy(grp, buf, sems.at[0])
        load.start()
        load.wait()
        base = PIECE_OFF[pieces[0]]
        for k in pieces:
            pltpu.make_async_copy(buf.at[pl.ds(PIECE_OFF[k] - base, PIECE_ROWS[k]), :],
                                  _block_rows(arrs, k, (x, y, c)), sems.at[1]).start()
        pltpu.make_async_copy(grp, buf, sems.at[1]).wait()

    return pl.pallas_call(
        body, in_specs=[ANY_SPEC], out_specs=[ANY_SPEC] * 3,
        out_shape=(SDS((DIN, D), bf16), SDS((DMIX, D), bf16), SDS((3, F, D), bf16)),
        scratch_shapes=[pltpu.VMEM((_group_rows(pieces), D), bf16), pltpu.SemaphoreType.DMA((2,))],
        name="place_own_rows")(shard)


def _xor_peer(x, y, c, k):
    return (x ^ (k >> 2), y ^ ((k >> 1) & 1), c ^ (k & 1))


def _gather_rest_start(shard, wi, wo, w2, w1):
    def body(s_ref, wi_ref, wo_ref, w2_ref, w1_ref,
             ssem_m, rsem_m, ssem_f, rsem_f0, rsem_f, s_o, wi_o, wo_o, w2_o, w1_o):
        x, y, c = _position()
        me, sib = (x, y, c), (x, y, 1 - c)
        chips = [(1 - x, y), (x, 1 - y), (1 - x, 1 - y)]
        arrs = _weight_pieces(wi_ref=wi_ref, wo_ref=wo_ref, w2_ref=w2_ref)
        for k in range(1, NDEV):
            for p in MIX_PIECES:
                pltpu.make_async_remote_copy(
                    src_ref=_shard_piece(s_ref, p), dst_ref=_block_rows(arrs, p, me), send_sem=ssem_m.at[k - 1],
                    recv_sem=rsem_m.at[k - 1], device_id=_xor_peer(x, y, c, k), device_id_type=MESH).start()
        for p in F2_PIECES:
            pltpu.make_async_remote_copy(
                src_ref=_shard_piece(s_ref, p), dst_ref=_block_rows(arrs, p, me), send_sem=ssem_f.at[0],
                recv_sem=rsem_f0, device_id=sib, device_id_type=MESH).start()
        for j, chip in enumerate(chips):
            for p in F2_PIECES:
                pltpu.make_async_remote_copy(
                    src_ref=_shard_piece(s_ref, p), dst_ref=_block_rows(arrs, p, me), send_sem=ssem_f.at[1 + j],
                    recv_sem=rsem_f.at[j], device_id=(*chip, c), device_id_type=MESH).start()

    dma = pltpu.SemaphoreType.DMA
    return pl.pallas_call(
        body, name="gather_rest_start",
        out_shape=(dma((7,)), dma((7,)), dma((4,)), dma(()), dma((3,)),
                   _hbm_like(shard), _hbm_like(wi), _hbm_like(wo), _hbm_like(w2), _hbm_like(w1)),
        in_specs=(HBM_SPEC,) * 5, out_specs=(SEM_SPEC,) * 5 + (HBM_SPEC,) * 5,
        input_output_aliases={0: 5, 1: 6, 2: 7, 3: 8, 4: 9},
        compiler_params=pltpu.CompilerParams(has_side_effects=SPLIT_EFFECT),
    )(_in_hbm(shard), _in_hbm(wi), _in_hbm(wo), _in_hbm(w2), _in_hbm(w1))


def _gather_mix_wait(ssem_m, rsem_m, shard, wi, wo, after):
    def body(s_ref, wi_ref, wo_ref, ssem, rsem, after_ref, s_o, wi_o, wo_o):
        x, y, c = _position()
        grp = _shard_group(s_ref, MIX_PIECES)
        for k in range(NDEV - 1):
            d = pltpu.make_async_remote_copy(src_ref=grp, dst_ref=grp, send_sem=ssem.at[k], recv_sem=rsem.at[k],
                                             device_id=(x, y, c), device_id_type=MESH)
            d.wait_recv()
            d.wait_send()

    return pl.pallas_call(
        body, name="gather_mix_wait", out_shape=(_hbm_like(shard), _hbm_like(wi), _hbm_like(wo)),
        in_specs=(HBM_SPEC, HBM_SPEC, HBM_SPEC, SEM_SPEC, SEM_SPEC, ANY_SPEC), out_specs=(HBM_SPEC,) * 3,
        input_output_aliases={0: 0, 1: 1, 2: 2},
        compiler_params=pltpu.CompilerParams(has_side_effects=SPLIT_EFFECT),
    )(shard, wi, wo, ssem_m, rsem_m, after)


def _gather_ffn2_pass_on(rsem_f, w2, wo, after):
    def body(w2_ref, wo_ref, rsem, after_ref, fsend, frecv, w2_o, wo_o):
        x, y, c = _position()
        sib = (x, y, 1 - c)
        chips = [(1 - x, y), (x, 1 - y), (1 - x, 1 - y)]
        arrs = _weight_pieces(w2_ref=w2_ref)
        three = w2_ref.at[0, pl.ds(0, _group_rows(F2_PIECES)), :]
        for j, chip in enumerate(chips):
            pltpu.make_async_remote_copy(src_ref=three, dst_ref=three, send_sem=fsend.at[j], recv_sem=rsem.at[j],
                                         device_id=(x, y, c), device_id_type=MESH).wait_recv()
            for p in F2_PIECES:
                rows = _block_rows(arrs, p, (*chip, c))
                pltpu.make_async_remote_copy(src_ref=rows, dst_ref=rows, send_sem=fsend.at[j], recv_sem=frecv.at[j],
                                             device_id=sib, device_id_type=MESH).start()

    dma = pltpu.SemaphoreType.DMA
    return pl.pallas_call(
        body, name="gather_ffn2_pass_on", out_shape=(dma((3,)), dma((3,)), _hbm_like(w2), _hbm_like(wo)),
        in_specs=(HBM_SPEC, HBM_SPEC, SEM_SPEC, ANY_SPEC), out_specs=(SEM_SPEC, SEM_SPEC, HBM_SPEC, HBM_SPEC),
        input_output_aliases={0: 2, 1: 3},
        compiler_params=pltpu.CompilerParams(has_side_effects=SPLIT_EFFECT),
    )(w2, wo, rsem_f, after)


def _gather_ffn2_wait(ssem_f, rsem_f0, fsend, frecv, shard, w2, after):
    def body(s_ref, w2_ref, ssem, rsem0, fs, fr, after_ref, w2_o):
        x, y, c = _position()
        grp = _shard_group(s_ref, F2_PIECES)

        def waiter(send_sem, recv_sem):
            return pltpu.make_async_remote_copy(src_ref=grp, dst_ref=grp, send_sem=send_sem, recv_sem=recv_sem,
                                                device_id=(x, y, c), device_id_type=MESH)

        waiter(ssem.at[0], rsem0).wait_recv()
        for j in range(3):
            waiter(fs.at[j], fr.at[j]).wait_recv()
        for rel in range(4):
            waiter(ssem.at[rel], rsem0).wait_send()
        for j in range(3):
            waiter(fs.at[j], fr.at[j]).wait_send()

    return pl.pallas_call(
        body, name="gather_ffn2_wait", out_shape=_hbm_like(w2),
        in_specs=(HBM_SPEC, HBM_SPEC, SEM_SPEC, SEM_SPEC, SEM_SPEC, SEM_SPEC, ANY_SPEC), out_specs=HBM_SPEC,
        input_output_aliases={1: 0},
        compiler_params=pltpu.CompilerParams(has_side_effects=SPLIT_EFFECT),
    )(shard, w2, ssem_f, rsem_f0, fsend, frecv, after)


class _GatheredWeights(_LocalWeights):
    def __init__(self, shard):
        w1 = _all_gather_ffn1(shard)
        wi, wo, w2 = _place_own_rows(shard)
        (self.ssem_m, self.rsem_m, self.ssem_f, self.rsem_f0, self.rsem_f,
         self.shard, self.wi, self.wo, self.w2_part, self.w1) = _gather_rest_start(shard, wi, wo, w2, w1)

    def mix(self, after):
        self.shard, wint, wout = _gather_mix_wait(self.ssem_m, self.rsem_m, self.shard, self.wi, self.wo, after)
        return wint, wout

    def before_out_proj(self, wout, after):
        self.fsend, self.frecv, self.w2_part, wout = _gather_ffn2_pass_on(self.rsem_f, self.w2_part, wout, after)
        return wout

    def ffn2(self, after):
        return _gather_ffn2_wait(self.ssem_f, self.rsem_f0, self.fsend, self.frecv, self.shard, self.w2_part, after)

    def mix_ffn2_grads_ready(self, dwint, dwout, dw2, dh2):
        rx1 = lax.empty((4, RSA_ROWS, D), bf16)
        self.sa, self.ra, dwint, dwout, dw2, rx1, dh2 = _rsa_level1_start(dwint, dwout, dw2, rx1, dh2)
        self.level1 = (dwint, dwout, dw2, rx1)
        return dh2

    def before_ffn1_bwd(self, dx1b):
        dwint, dwout, dw2, rx1 = _rsa_level1_wait(self.sa, self.ra, *self.level1, dx1b)
        tx, self.acc = _rsa_chip_sums(dwint, dwout, dw2, rx1)
        rx2 = lax.empty((3, RSA_ROWS, D), bf16)
        self.sb, self.rb, self.tx, self.rx2, dx1b = _rsa_level2_start(tx, rx2, dx1b)
        return dx1b

    def mix_ffn2_grads_total(self, after):
        rx2 = _rsa_level2_wait(self.sb, self.rb, self.tx, self.rx2, after)
        return _rsa_total(self.acc, rx2)


RS_CHUNK = 176


def _reduce_scatter_ffn1(dw1):
    pieces = G1_PIECES
    nrows = _group_rows(pieces)
    nchunk = nrows // RS_CHUNK

    def body(d1_ref, red_ref, rx1_ref, rx2_ref,
             own_buf, rx_buf, tx_buf, acc, sa, ra, sb, rb, lsem):
        x, y, c = _position()
        me, sib = (x, y, c), (x, y, 1 - c)
        rel_chips = [(x, y), (1 - x, y), (x, 1 - y), (1 - x, 1 - y)]
        srcs = _weight_pieces(w1_ref=d1_ref)

        def piece(k, dev):
            r = PIECE_ROWS[k]
            return srcs[k].at[pl.ds(pl.multiple_of(_dev_index(*dev) * r, 16), r), :]

        def packed(ref, k):
            return ref.at[pl.ds(PIECE_OFF[k], PIECE_ROWS[k]), :]

        for j, chip in enumerate(rel_chips):
            for k in pieces:
                pltpu.make_async_remote_copy(
                    src_ref=piece(k, (*chip, 1 - c)), dst_ref=packed(rx1_ref.at[j], k),
                    send_sem=sa.at[j], recv_sem=ra.at[j], device_id=sib, device_id_type=MESH).start()

        def wait_a(j):
            return pltpu.make_async_remote_copy(src_ref=rx1_ref.at[j], dst_ref=rx1_ref.at[j], send_sem=sa.at[j],
                                                recv_sem=ra.at[j], device_id=me, device_id_type=MESH)

        def ici(j):
            return pltpu.make_async_remote_copy(
                src_ref=tx_buf.at[j - 1], dst_ref=rx2_ref.at[j - 1], send_sem=sb.at[j - 1], recv_sem=rb.at[j - 1],
                device_id=(*rel_chips[j], c), device_id_type=MESH)

        for j in (1, 2, 3, 0):
            loads = [pltpu.make_async_copy(piece(k, (*rel_chips[j], c)), packed(own_buf, k), lsem)
                     for k in pieces]
            for cp in loads:
                cp.start()
            wait_a(j).wait_recv()
            got = pltpu.make_async_copy(rx1_ref.at[j], rx_buf, lsem)
            got.start()
            pltpu.make_async_copy(rx_buf, rx_buf, lsem).wait()
            got.wait()

            def add(i, carry, j=j):
                rows = pl.ds(pl.multiple_of(i * RS_CHUNK, 16), RS_CHUNK)
                tot = own_buf[rows, :].astype(f32) + rx_buf[rows, :].astype(f32)
                if j == 0:
                    acc[rows, :] = tot
                else:
                    tx_buf[j - 1, rows, :] = tot.astype(bf16)
                return carry

            lax.fori_loop(0, nchunk, add, 0)
            if j != 0:
                ici(j).start()

        for j in (1, 2, 3):
            ici(j).wait_recv()
            got = pltpu.make_async_copy(rx2_ref.at[j - 1], rx_buf, lsem)
            got.start()
            got.wait()

            def add2(i, carry):
                rows = pl.ds(pl.multiple_of(i * RS_CHUNK, 16), RS_CHUNK)
                acc[rows, :] += rx_buf[rows, :].astype(f32)
                return carry

            lax.fori_loop(0, nchunk, add2, 0)
        out = pltpu.make_async_copy(acc, red_ref, lsem)
        out.start()
        out.wait()
        for j in range(4):
            wait_a(j).wait_send()
        for j in (1, 2, 3):
            ici(j).wait_send()

    hbm = pl.BlockSpec(memory_space=pl.ANY)
    red, _, _ = pl.pallas_call(
        body, in_specs=[hbm], out_specs=[hbm] * 3,
        out_shape=(SDS((nrows, D), f32), SDS((4, nrows, D), bf16), SDS((3, nrows, D), bf16)),
        scratch_shapes=[pltpu.VMEM((nrows, D), bf16), pltpu.VMEM((nrows, D), bf16),
                        pltpu.VMEM((3, nrows, D), bf16), pltpu.VMEM((nrows, D), f32),
                        pltpu.SemaphoreType.DMA((4,)), pltpu.SemaphoreType.DMA((4,)),
                        pltpu.SemaphoreType.DMA((3,)), pltpu.SemaphoreType.DMA((3,)), pltpu.SemaphoreType.DMA],
        compiler_params=pltpu.CompilerParams(has_side_effects=True, vmem_limit_bytes=VMEM_LIMIT_V7X),
        name="reduce_scatter_ffn1")(dw1)
    return red


RSA_PIECES = MIX_PIECES + F2_PIECES
RSA_ROWS = _group_rows(RSA_PIECES)
RSA_OFF = {k: PIECE_OFF[k] - PIECE_OFF[RSA_PIECES[0]] for k in RSA_PIECES}
RSA_BLOCK = 192


def _rsa_rows(ref, k):
    return ref.at[pl.ds(RSA_OFF[k], PIECE_ROWS[k]), :]


def _rsa_level1_start(dwint, dwout, dw2, rx1, thru):
    def body(di_ref, do_ref, d2_ref, rx1_ref, thru_ref, sa, ra, di_o, do_o, d2_o, rx1_o, thru_o):
        x, y, c = _position()
        srcs = _weight_pieces(wi_ref=di_ref, wo_ref=do_ref, w2_ref=d2_ref)
        for j, chip in enumerate([(x, y), (1 - x, y), (x, 1 - y), (1 - x, 1 - y)]):
            for k in RSA_PIECES:
                pltpu.make_async_remote_copy(
                    src_ref=_block_rows(srcs, k, (*chip, 1 - c)), dst_ref=_rsa_rows(rx1_ref.at[j], k),
                    send_sem=sa.at[j], recv_sem=ra.at[j], device_id=(x, y, 1 - c), device_id_type=MESH).start()

    dma = pltpu.SemaphoreType.DMA
    arrs = (dwint, dwout, dw2, rx1, thru)
    return pl.pallas_call(
        body, name="rsa_level1_start", out_shape=(dma((4,)), dma((4,))) + tuple(_hbm_like(a) for a in arrs),
        in_specs=(HBM_SPEC,) * 5, out_specs=(SEM_SPEC,) * 2 + (HBM_SPEC,) * 5,
        input_output_aliases={0: 2, 1: 3, 2: 4, 3: 5, 4: 6},
        compiler_params=pltpu.CompilerParams(has_side_effects=SPLIT_EFFECT),
    )(*[_in_hbm(a) for a in arrs])


def _rsa_level1_wait(sa, ra, dwint, dwout, dw2, rx1, after):
    def body(di_ref, do_ref, d2_ref, rx1_ref, sa_ref, ra_ref, after_ref, di_o, do_o, d2_o, rx1_o):
        x, y, c = _position()
        for j in range(4):
            d = pltpu.make_async_remote_copy(src_ref=rx1_ref.at[j], dst_ref=rx1_ref.at[j], send_sem=sa_ref.at[j],
                                             recv_sem=ra_ref.at[j], device_id=(x, y, c), device_id_type=MESH)
            d.wait_recv()
            d.wait_send()

    arrs = (dwint, dwout, dw2, rx1)
    return pl.pallas_call(
        body, name="rsa_level1_wait", out_shape=tuple(_hbm_like(a) for a in arrs),
        in_specs=(HBM_SPEC,) * 4 + (SEM_SPEC, SEM_SPEC, ANY_SPEC), out_specs=(HBM_SPEC,) * 4,
        input_output_aliases={0: 0, 1: 1, 2: 2, 3: 3},
        compiler_params=pltpu.CompilerParams(has_side_effects=SPLIT_EFFECT),
    )(*arrs, sa, ra, after)


def _rsa_chip_sums(dwint, dwout, dw2, rx1):
    nblk = RSA_ROWS // RSA_BLOCK

    def body(di_ref, do_ref, d2_ref, rx1_ref, tx_ref, acc_ref, own_buf, rx_buf, tx_buf, acc_buf, lsems):
        x, y, c = _position()
        srcs = _weight_pieces(wi_ref=di_ref, wo_ref=do_ref, w2_ref=d2_ref)
        for j, chip in enumerate([(x, y), (1 - x, y), (x, 1 - y), (1 - x, 1 - y)]):
            loads = [pltpu.make_async_copy(_block_rows(srcs, k, (*chip, c)), _rsa_rows(own_buf, k), lsems.at[0])
                     for k in RSA_PIECES]
            got = pltpu.make_async_copy(rx1_ref.at[j], rx_buf, lsems.at[1])
            for cp in loads + [got]:
                cp.start()
            pltpu.make_async_copy(rx_buf, rx_buf, lsems.at[0]).wait()
            got.wait()

            def add(i, carry, j=j):
                rows = pl.ds(pl.multiple_of(i * RSA_BLOCK, 16), RSA_BLOCK)
                tot = own_buf[rows, :].astype(f32) + rx_buf[rows, :].astype(f32)
                if j == 0:
                    acc_buf[rows, :] = tot
                else:
                    tx_buf[rows, :] = tot.astype(bf16)
                return carry

            lax.fori_loop(0, nblk, add, 0)
            out = (pltpu.make_async_copy(acc_buf, acc_ref, lsems.at[2]) if j == 0
                   else pltpu.make_async_copy(tx_buf, tx_ref.at[j - 1], lsems.at[2]))
            out.start()
            out.wait()

    return pl.pallas_call(
        body, in_specs=[ANY_SPEC] * 4, out_specs=[ANY_SPEC] * 2,
        out_shape=(SDS((3, RSA_ROWS, D), bf16), SDS((RSA_ROWS, D), f32)),
        scratch_shapes=[pltpu.VMEM((RSA_ROWS, D), bf16), pltpu.VMEM((RSA_ROWS, D), bf16),
                        pltpu.VMEM((RSA_ROWS, D), bf16), pltpu.VMEM((RSA_ROWS, D), f32),
                        pltpu.SemaphoreType.DMA((3,))],
        compiler_params=_cparams(None, VMEM_LIMIT_V7X), name="rsa_chip_sums")(dwint, dwout, dw2, rx1)


def _rsa_level2_start(tx, rx2, thru):
    def body(tx_ref, rx2_ref, thru_ref, sb, rb, tx_o, rx2_o, thru_o):
        x, y, c = _position()
        for j, chip in enumerate([(1 - x, y), (x, 1 - y), (1 - x, 1 - y)]):
            pltpu.make_async_remote_copy(src_ref=tx_ref.at[j], dst_ref=rx2_ref.at[j], send_sem=sb.at[j],
                                         recv_sem=rb.at[j], device_id=(*chip, c), device_id_type=MESH).start()

    dma = pltpu.SemaphoreType.DMA
    arrs = (tx, rx2, thru)
    return pl.pallas_call(
        body, name="rsa_level2_start", out_shape=(dma((3,)), dma((3,))) + tuple(_hbm_like(a) for a in arrs),
        in_specs=(HBM_SPEC,) * 3, out_specs=(SEM_SPEC,) * 2 + (HBM_SPEC,) * 3,
        input_output_aliases={0: 2, 1: 3, 2: 4},
        compiler_params=pltpu.CompilerParams(has_side_effects=SPLIT_EFFECT),
    )(*[_in_hbm(a) for a in arrs])


def _rsa_level2_wait(sb, rb, tx, rx2, after):
    def body(tx_ref, rx2_ref, sb_ref, rb_ref, after_ref, rx2_o):
        x, y, c = _position()
        for j in range(3):
            d = pltpu.make_async_remote_copy(src_ref=tx_ref.at[j], dst_ref=rx2_ref.at[j], send_sem=sb_ref.at[j],
                                             recv_sem=rb_ref.at[j], device_id=(x, y, c), device_id_type=MESH)
            d.wait_recv()
            d.wait_send()

    return pl.pallas_call(
        body, name="rsa_level2_wait", out_shape=_hbm_like(rx2),
        in_specs=(HBM_SPEC, HBM_SPEC, SEM_SPEC, SEM_SPEC, ANY_SPEC), out_specs=HBM_SPEC,
        input_output_aliases={1: 0},
        compiler_params=pltpu.CompilerParams(has_side_effects=SPLIT_EFFECT),
    )(tx, rx2, sb, rb, after)


def _rsa_total(acc, rx2):
    def body(a_ref, r_ref, o_ref):
        o_ref[...] = ((a_ref[...] + r_ref[0].astype(f32)) + r_ref[1].astype(f32)) + r_ref[2].astype(f32)

    return pl.pallas_call(
        body, grid=(RSA_ROWS // RSA_BLOCK,),
        in_specs=[pl.BlockSpec((RSA_BLOCK, D), lambda i: (i, 0)), pl.BlockSpec((3, RSA_BLOCK, D), lambda i: (0, i, 0))],
        out_specs=pl.BlockSpec((RSA_BLOCK, D), lambda i: (i, 0)),
        out_shape=SDS((RSA_ROWS, D), f32), name="rsa_total")(acc, rx2)


def _all_reduce_small(packed):
    def body(p_ref, o_ref, pair, chips, send_sems, recv_sems):
        x, y, c = _position()
        chip = 2 * x + y
        pair[c] = p_ref[...]
        swap = pltpu.make_async_remote_copy(
            src_ref=p_ref, dst_ref=pair.at[c], send_sem=send_sems.at[0], recv_sem=recv_sems.at[0],
            device_id=(x, y, 1 - c), device_id_type=MESH)
        swap.start()
        swap.wait_recv()
        chips[chip] = pair[0] + pair[1]
        cps = [pltpu.make_async_remote_copy(
            src_ref=chips.at[chip], dst_ref=chips.at[chip], send_sem=send_sems.at[1 + j], recv_sem=recv_sems.at[1 + j],
            device_id=(*other, c), device_id_type=MESH)
            for j, other in enumerate([(1 - x, y), (x, 1 - y), (1 - x, 1 - y)])]
        for cp in cps:
            cp.start()
        for cp in cps:
            cp.wait_recv()
        tot = (chips[0] + chips[1]) + (chips[2] + chips[3])
        o_ref[...] = tot
        loss = jnp.sum(tot[LOSS_ROW:LOSS_ROW + 1, :], axis=-1, keepdims=True)
        o_ref[LOSS_ROW:LOSS_ROW + 1, :] = jnp.broadcast_to(loss, (1, 128))
        swap.wait_send()
        for cp in cps:
            cp.wait_send()

    vm = pl.BlockSpec(memory_space=pltpu.VMEM)
    return pl.pallas_call(
        body, in_specs=[vm], out_specs=vm, out_shape=SDS((SMALL_ROWS, 128), f32),
        scratch_shapes=[pltpu.VMEM((2, SMALL_ROWS, 128), f32), pltpu.VMEM((4, SMALL_ROWS, 128), f32),
                        pltpu.SemaphoreType.DMA((4,)), pltpu.SemaphoreType.DMA((4,))],
        compiler_params=pltpu.CompilerParams(has_side_effects=True),
        name="all_reduce_small")(packed)


def _adamw_math(w, g, m, v):
    m = ADAM_B1 * m + (1.0 - ADAM_B1) * g
    v = ADAM_B2 * v + (1.0 - ADAM_B2) * (g * g)
    m_hat = m / (1.0 - ADAM_B1 ** ADAM_STEP)
    v_hat = v / (1.0 - ADAM_B2 ** ADAM_STEP)
    delta = -ADAM_LR * (m_hat / (jnp.sqrt(v_hat) + ADAM_EPS) + ADAM_WD * w)
    return delta, m, v


def _adamw_big(ws, ms, vs, red1, red_rest):
    npiece = len(BIG)
    rmax = max(PIECE_ROWS)

    def body(*refs):
        ins = (refs[0:npiece], refs[npiece:2 * npiece], refs[2 * npiece:3 * npiece])
        red1_ref, rest_ref = refs[3 * npiece:3 * npiece + 2]
        out_refs = refs[3 * npiece + 2:7 * npiece + 2]
        inb, outb, in_sems, out_sems = refs[7 * npiece + 2:]

        def grad_rows(k):
            if k in G1_PIECES:
                return red1_ref.at[pl.ds(PIECE_OFF[k], PIECE_ROWS[k]), :]
            return _rsa_rows(rest_ref, k)

        def loads(k):
            s, r = k % 2, PIECE_ROWS[k]
            cps = [pltpu.make_async_copy(ins[q][k].at[0], inb.at[s, q, pl.ds(0, r), :], in_sems.at[4 * s + q])
                   for q in range(3)]
            cps.append(pltpu.make_async_copy(grad_rows(k), inb.at[s, 3, pl.ds(0, r), :], in_sems.at[4 * s + 3]))
            return cps

        def stores(k):
            s, r = k % 2, PIECE_ROWS[k]
            return [pltpu.make_async_copy(outb.at[s, q, pl.ds(0, r), :], out_refs[q * npiece + k].at[0],
                                          out_sems.at[4 * s + q]) for q in range(4)]

        for cp in loads(0):
            cp.start()
        for k in range(npiece):
            s, r = k % 2, PIECE_ROWS[k]
            if k + 1 < npiece:
                for cp in loads(k + 1):
                    cp.start()
            for cp in loads(k):
                cp.wait()
            if k >= 2:
                for cp in stores(k - 2):
                    cp.wait()
            g = inb[s, 3, 0:r, :]
            d, nm, nv = _adamw_math(inb[s, 0, 0:r, :], g, inb[s, 1, 0:r, :], inb[s, 2, 0:r, :])
            outb[s, 0, 0:r, :] = g
            outb[s, 1, 0:r, :] = d
            outb[s, 2, 0:r, :] = nm
            outb[s, 3, 0:r, :] = nv
            for cp in stores(k):
                cp.start()
        for k in (npiece - 2, npiece - 1):
            for cp in stores(k):
                cp.wait()

    hbm = pl.BlockSpec(memory_space=pl.ANY)
    outs = pl.pallas_call(
        body, in_specs=[hbm] * (3 * npiece + 2), out_specs=[hbm] * (4 * npiece),
        out_shape=tuple(SDS(w.shape, f32) for _ in range(4) for w in ws),
        scratch_shapes=[pltpu.VMEM((2, 4, rmax, D), f32), pltpu.VMEM((2, 4, rmax, D), f32),
                        pltpu.SemaphoreType.DMA((8,)), pltpu.SemaphoreType.DMA((8,))],
        compiler_params=_cparams(None, VMEM_LIMIT_V7X), name="adamw_big")(*ws, *ms, *vs, red1, red_rest)
    return [list(outs[q * npiece:(q + 1) * npiece]) for q in range(4)]


def _adamw_small(w, m, v, g, name):
    def body(w_ref, m_ref, v_ref, g_ref, d_ref, nm_ref, nv_ref):
        d, nm, nv = _adamw_math(w_ref[...], g_ref[...], m_ref[...], v_ref[...])
        d_ref[...] = d
        nm_ref[...] = nm
        nv_ref[...] = nv

    return pl.pallas_call(
        body, out_shape=tuple(SDS(w.shape, f32) for _ in range(3)), name=name)(w, m, v, g)


WEIGHTS = ("ffn1_norm", "ffn1_w_gate", "ffn1_w_up", "ffn1_w_down", "mix_norm", "w_in", "q_norm", "k_norm",
           "attn_sinks", "rel_bias", "pool_w", "pool_scale", "w_out", "ffn2_norm", "ffn2_w_gate", "ffn2_w_up",
           "ffn2_w_down")
BIG = (("ffn1_w_gate", True), ("ffn1_w_up", True), ("ffn1_w_down", False), ("w_in", True), ("w_out", False),
       ("ffn2_w_gate", True), ("ffn2_w_up", True), ("ffn2_w_down", False))


def kernel(x, ffn1_norm, ffn1_w_gate, ffn1_w_up, ffn1_w_down, mix_norm, w_in, q_norm, k_norm, attn_sinks, rel_bias, pool_w, pool_scale, w_out, ffn2_norm, ffn2_w_gate, ffn2_w_up, ffn2_w_down, loss_target, m_ffn1_norm, m_ffn1_w_gate, m_ffn1_w_up, m_ffn1_w_down, m_mix_norm, m_w_in, m_q_norm, m_k_norm, m_attn_sinks, m_rel_bias, m_pool_w, m_pool_scale, m_w_out, m_ffn2_norm, m_ffn2_w_gate, m_ffn2_w_up, m_ffn2_w_down, v_ffn1_norm, v_ffn1_w_gate, v_ffn1_w_up, v_ffn1_w_down, v_mix_norm, v_w_in, v_q_norm, v_k_norm, v_attn_sinks, v_rel_bias, v_pool_w, v_pool_scale, v_w_out, v_ffn2_norm, v_ffn2_w_gate, v_ffn2_w_up, v_ffn2_w_down):
    args = dict(locals())
    w = {n: args[n] for n in WEIGHTS}
    m = {n: args["m_" + n] for n in WEIGHTS}
    v = {n: args["v_" + n] for n in WEIGHTS}

    as_rows = lambda a, tr: jnp.swapaxes(a, 1, 2) if tr else a
    shard = jnp.concatenate([as_rows(w[n], tr)[0].astype(bf16) for n, tr in BIG], axis=0)
    exchanges = _GatheredWeights(shard)
    gx, (dw1, _, _, _), small = _local_step(
        x[0], loss_target[0], exchanges, ffn1_norm, mix_norm, ffn2_norm, q_norm, k_norm, attn_sinks,
        rel_bias, pool_w[0], pool_scale)

    red1 = _reduce_scatter_ffn1(dw1)
    red_rest = exchanges.mix_ffn2_grads_total(red1)
    small_tot = _all_reduce_small(_pack_small(small))

    grads, deltas, new_m, new_v = {}, {}, {}, {}
    big_out = _adamw_big(*[[as_rows(t[n], tr) for n, tr in BIG] for t in (w, m, v)], red1, red_rest)
    for k, (n, tr) in enumerate(BIG):
        grads[n], deltas[n], new_m[n], new_v[n] = [as_rows(o[k], tr) for o in big_out]
    small_names = [n for n in SMALL_NAMES if n != "loss"]
    ds, nms, nvs = _adamw_small(_pack_small({n: w[n] for n in small_names}), _pack_small({n: m[n] for n in small_names}),
                                _pack_small({n: v[n] for n in small_names}), small_tot, "adamw_small")
    for n in small_names:
        grads[n] = _unpack_small(small_tot, n)
        deltas[n], new_m[n], new_v[n] = _unpack_small(ds, n), _unpack_small(nms, n), _unpack_small(nvs, n)
    loss = small_tot[LOSS_ROW, 0]
    return (loss, gx[None], *[grads[n] for n in WEIGHTS], *[deltas[n] for n in WEIGHTS],
            *[new_m[n] for n in WEIGHTS], *[new_v[n] for n in WEIGHTS])
```

```python
import functools

import jax
import jax.numpy as jnp
import numpy as np
from jax import lax
from jax.experimental import pallas as pl
from jax.experimental.pallas import tpu as pltpu

f32, bf16, i32 = jnp.float32, jnp.bfloat16, jnp.int32
SDS = jax.ShapeDtypeStruct

D = 1024
F = 2816
HD = 64
NH = 8
NKV = 2
GQA = NH // NKV
DATTN = NH * HD
DKV = NKV * HD
DPOOL = 512
POOL_WINDOWS = (2, 4, 8, 16)
PGD = DPOOL // len(POOL_WINDOWS)
DIN = DATTN + 2 * DKV + DPOOL
DMIX = DATTN + DPOOL
BLK = 128
NBUCK = 32
MAX_DISTANCE = 128
EPS = 1e-6
NEG = -1e30
SCALE = HD ** -0.5

ADAM_LR, ADAM_B1, ADAM_B2, ADAM_EPS, ADAM_WD, ADAM_STEP = 0.001, 0.9, 0.999, 1e-08, 0.01, 10

NDEV = 8
FS = F // NDEV
INS = DIN // NDEV
OUTS = DMIX // NDEV
PIECE_ROWS = (FS, FS, FS, INS, OUTS, FS, FS, FS)
PIECE_OFF = tuple(int(v) for v in np.cumsum((0,) + PIECE_ROWS[:-1]))
PACK_ROWS = sum(PIECE_ROWS)

VMEM_LIMIT_V7X = 56 * 1024 * 1024

MESH = pl.DeviceIdType.MESH


def _cparams(sem=None, vmem=None):
    return pltpu.CompilerParams(dimension_semantics=sem, vmem_limit_bytes=vmem)


def _nt(a, b):
    return lax.dot_general(a, b, (((1,), (1,)), ((), ())), preferred_element_type=f32)


def _tn(a, b):
    return lax.dot_general(a, b, (((0,), (0,)), ((), ())), preferred_element_type=f32)


def _nn(a, b):
    return jnp.dot(a, b, preferred_element_type=f32)


def _sigmoid(x):
    return 1.0 / (1.0 + jnp.exp(-x))


def _norm_fwd(x, g, name):
    T = x.shape[0]
    tm = min(512, T)

    def body(x_ref, g_ref, h_ref):
        xv = x_ref[...]
        r = lax.rsqrt(jnp.mean(xv * xv, axis=-1, keepdims=True) + EPS)
        h_ref[...] = (xv * r * g_ref[...]).astype(bf16)

    return pl.pallas_call(
        body, grid=(T // tm,),
        in_specs=[pl.BlockSpec((tm, D), lambda i: (i, 0)), pl.BlockSpec((1, D), lambda i: (0, 0))],
        out_specs=pl.BlockSpec((tm, D), lambda i: (i, 0)),
        out_shape=SDS((T, D), bf16), name=name)(x, g)


def _norm_bwd(dh, x, g, dres, out_scale, name):
    T = x.shape[0]
    tm = min(512, T)

    def body(dh_ref, x_ref, g_ref, dr_ref, dx_ref, dxb_ref, dg_ref):
        i = pl.program_id(0)
        xv = x_ref[...]
        r = lax.rsqrt(jnp.mean(xv * xv, axis=-1, keepdims=True) + EPS)
        xh = xv * r
        dhv = dh_ref[...]
        dxh = dhv * g_ref[...]
        dx = dr_ref[...] + r * (dxh - xh * jnp.mean(dxh * xh, axis=-1, keepdims=True))
        dx_ref[...] = dx
        dxb_ref[...] = (out_scale * dx).astype(bf16)
        dg = jnp.sum(dhv * xh, axis=0, keepdims=True)

        @pl.when(i == 0)
        def _():
            dg_ref[...] = dg

        @pl.when(i > 0)
        def _():
            dg_ref[...] += dg

    tok = pl.BlockSpec((tm, D), lambda i: (i, 0))
    vec = pl.BlockSpec((1, D), lambda i: (0, 0))
    return pl.pallas_call(
        body, grid=(T // tm,),
        in_specs=[tok, tok, vec, tok], out_specs=[tok, tok, vec],
        out_shape=(SDS((T, D), f32), SDS((T, D), bf16), SDS((1, D), f32)),
        compiler_params=_cparams(("arbitrary",)), name=name)(dh, x, g, dres)


FFN_ROW_CHUNK = 256


def _ffn_tiles(T):
    return min(1024, T), 256


def _ffn_fwd(h, w, x, name):
    T = h.shape[0]
    tm, tf = _ffn_tiles(T)
    nf = F // tf

    def body(h_ref, w_ref, x_hbm, xo_ref, g_ref, u_ref, sem):
        fi = pl.program_id(0)

        @pl.when(fi == 0)
        def _():
            cp = pltpu.make_async_copy(x_hbm, xo_ref, sem)
            cp.start()
            cp.wait()

        wgu = w_ref[0:2].reshape(2 * tf, D)
        for r in range(0, T, tm):
            rows = slice(r, r + tm)
            gu = _nt(h_ref[rows, :], wgu)
            gate, up = gu[:, :tf], gu[:, tf:]
            act = gate * _sigmoid(gate) * up
            g_ref[rows, :] = gate.astype(bf16)
            u_ref[rows, :] = up.astype(bf16)
            xo_ref[rows, :] += _nn((0.5 * act).astype(bf16), w_ref[2])

    tok = pl.BlockSpec((T, D), lambda f: (0, 0))
    act_spec = pl.BlockSpec((T, tf), lambda f: (0, f))
    return pl.pallas_call(
        body, grid=(nf,),
        in_specs=[tok, pl.BlockSpec((3, tf, D), lambda f: (0, f, 0)), pl.BlockSpec(memory_space=pl.ANY)],
        out_specs=[tok, act_spec, act_spec],
        out_shape=(SDS((T, D), f32), SDS((T, F), bf16), SDS((T, F), bf16)),
        scratch_shapes=[pltpu.SemaphoreType.DMA],
        compiler_params=_cparams(("arbitrary",), VMEM_LIMIT_V7X), name=name)(h, w, x)


def _ffn_bwd(dob, h, gate, up, w, name):
    T = h.shape[0]
    _, tf = _ffn_tiles(T)
    nf = F // tf

    def body(do_hbm, h_hbm, g_ref, u_ref, w_ref, dh_hbm, dw_ref, do_v, h_v, dh_acc, dgu_s, act_s, sems):
        fi = pl.program_id(0)

        @pl.when(fi == 0)
        def _():
            loads = [pltpu.make_async_copy(do_hbm, do_v, sems.at[0]), pltpu.make_async_copy(h_hbm, h_v, sems.at[1])]
            for cp in loads:
                cp.start()
            dh_acc[...] = jnp.zeros_like(dh_acc)
            for cp in loads:
                cp.wait()

        wgu = w_ref[0:2].reshape(2 * tf, D)
        for r in range(0, T, FFN_ROW_CHUNK):
            rows = slice(r, r + FFN_ROW_CHUNK)
            dov = do_v[rows, :]
            gv = g_ref[rows, :].astype(f32)
            uv = u_ref[rows, :].astype(f32)
            sg = _sigmoid(gv)
            sil = gv * sg
            dact = _nt(dov, w_ref[2])
            dup = dact * sil
            dgate = dact * uv * (sg * (1.0 + gv * (1.0 - sg)))
            dgu = jnp.concatenate([dgate.astype(bf16), dup.astype(bf16)], axis=1)
            dgu_s[rows, :] = dgu
            act_s[rows, :] = (sil * uv).astype(bf16)
            dh_acc[rows, :] += _nn(dgu, wgu)
        dw_ref[0:2] = _tn(dgu_s[...], h_v[...]).reshape(2, tf, D).astype(bf16)
        dw_ref[2] = _tn(act_s[...], do_v[...]).astype(bf16)

        @pl.when(fi == nf - 1)
        def _():
            out = pltpu.make_async_copy(dh_acc, dh_hbm, sems.at[0])
            out.start()
            out.wait()

    act_spec = pl.BlockSpec((T, tf), lambda f: (0, f))
    wspec = pl.BlockSpec((3, tf, D), lambda f: (0, f, 0))
    hbm = pl.BlockSpec(memory_space=pl.ANY)
    return pl.pallas_call(
        body, grid=(nf,),
        in_specs=[hbm, hbm, act_spec, act_spec, wspec],
        out_specs=[hbm, wspec],
        out_shape=(SDS((T, D), f32), SDS((3, F, D), bf16)),
        scratch_shapes=[pltpu.VMEM((T, D), bf16), pltpu.VMEM((T, D), bf16), pltpu.VMEM((T, D), f32),
                        pltpu.VMEM((T, 2 * tf), bf16), pltpu.VMEM((T, tf), bf16), pltpu.SemaphoreType.DMA((2,))],
        compiler_params=_cparams(("arbitrary",), VMEM_LIMIT_V7X), name=name)(dob, h, gate, up, w)


def _loss_grad(y, target, name):
    T = y.shape[0]
    tm = min(512, T)

    def body(y_ref, t_ref, dy_ref, dyb_ref, l_ref):
        i = pl.program_id(0)
        e = y_ref[...] - t_ref[...]
        dy = e * (1.0 / D)
        dy_ref[...] = dy
        dyb_ref[...] = (0.5 * dy).astype(bf16)
        col = jnp.sum(e * e, axis=0, keepdims=True) * (0.5 / D)
        lanes = col[:, 0:128]
        for k in range(1, D // 128):
            lanes = lanes + col[:, 128 * k:128 * (k + 1)]

        @pl.when(i == 0)
        def _():
            l_ref[...] = lanes

        @pl.when(i > 0)
        def _():
            l_ref[...] += lanes

    tok = pl.BlockSpec((tm, D), lambda i: (i, 0))
    return pl.pallas_call(
        body, grid=(T // tm,), in_specs=[tok, tok],
        out_specs=[tok, tok, pl.BlockSpec((1, 128), lambda i: (0, 0))],
        out_shape=(SDS((T, D), f32), SDS((T, D), bf16), SDS((1, 128), f32)),
        compiler_params=_cparams(("arbitrary",)), name=name)(y, target)


def _in_proj_fwd(h, wint, name):
    T = h.shape[0]
    tm = min(512, T)

    def body(h_ref, w_ref, z_ref):
        z_ref[...] = _nt(h_ref[...], w_ref[...])

    return pl.pallas_call(
        body, grid=(T // tm,),
        in_specs=[pl.BlockSpec((tm, D), lambda i: (i, 0)), pl.BlockSpec((DIN, D), lambda i: (0, 0))],
        out_specs=pl.BlockSpec((tm, DIN), lambda i: (i, 0)),
        out_shape=SDS((T, DIN), f32), name=name)(h, wint)


def _in_proj_bwd(dz, wint, h, name):
    T = h.shape[0]
    tm = min(512, T)
    nt = T // tm

    def body(dz_ref, w_ref, h_ref, dh_ref, dw_ref, acc):
        i = pl.program_id(0)
        dzb = dz_ref[...].astype(bf16)
        dh_ref[...] = _nn(dzb, w_ref[...])
        part = _tn(dzb, h_ref[...])

        @pl.when(i == 0)
        def _():
            acc[...] = part

        @pl.when(i > 0)
        def _():
            acc[...] += part

        @pl.when(i == nt - 1)
        def _():
            dw_ref[...] = acc[...].astype(bf16)

    wspec = pl.BlockSpec((DIN, D), lambda i: (0, 0))
    return pl.pallas_call(
        body, grid=(nt,),
        in_specs=[pl.BlockSpec((tm, DIN), lambda i: (i, 0)), wspec, pl.BlockSpec((tm, D), lambda i: (i, 0))],
        out_specs=[pl.BlockSpec((tm, D), lambda i: (i, 0)), wspec],
        out_shape=(SDS((T, D), f32), SDS((DIN, D), bf16)),
        scratch_shapes=[pltpu.VMEM((DIN, D), f32)],
        compiler_params=_cparams(("arbitrary",)), name=name)(dz, wint, h)


def _out_proj_fwd(ymix, wout, x, name):
    T = x.shape[0]
    tm = min(512, T)

    def body(y_ref, w_ref, x_ref, o_ref):
        o_ref[...] = x_ref[...] + _nn(y_ref[...], w_ref[...])

    tok = pl.BlockSpec((tm, D), lambda i: (i, 0))
    return pl.pallas_call(
        body, grid=(T // tm,),
        in_specs=[pl.BlockSpec((tm, DMIX), lambda i: (i, 0)), pl.BlockSpec((DMIX, D), lambda i: (0, 0)), tok],
        out_specs=tok, out_shape=SDS((T, D), f32), name=name)(ymix, wout, x)


def _out_proj_bwd(dxb, wout, ymix, name):
    T = dxb.shape[0]
    tm = min(512, T)
    nt = T // tm

    def body(dx_ref, w_ref, y_ref, dy_ref, dw_ref, acc):
        i = pl.program_id(0)
        dxv = dx_ref[...]
        dy_ref[...] = _nt(dxv, w_ref[...])
        part = _tn(y_ref[...], dxv)

        @pl.when(i == 0)
        def _():
            acc[...] = part

        @pl.when(i > 0)
        def _():
            acc[...] += part

        @pl.when(i == nt - 1)
        def _():
            dw_ref[...] = acc[...].astype(bf16)

    wspec = pl.BlockSpec((DMIX, D), lambda i: (0, 0))
    return pl.pallas_call(
        body, grid=(nt,),
        in_specs=[pl.BlockSpec((tm, D), lambda i: (i, 0)), wspec, pl.BlockSpec((tm, DMIX), lambda i: (i, 0))],
        out_specs=[pl.BlockSpec((tm, DMIX), lambda i: (i, 0)), wspec],
        out_shape=(SDS((T, DMIX), f32), SDS((DMIX, D), bf16)),
        scratch_shapes=[pltpu.VMEM((DMIX, D), f32)],
        compiler_params=_cparams(("arbitrary",)), name=name)(dxb, wout, ymix)


def _t5_bucket_table():
    ql = np.arange(BLK)[:, None]
    kl = np.arange(2 * BLK)[None, :]
    n = np.maximum(ql + BLK - kl, 0)
    max_exact = NBUCK // 2
    large = max_exact + (np.log(np.maximum(n, 1) / max_exact) / np.log(MAX_DISTANCE / max_exact)
                         * (NBUCK - max_exact)).astype(np.int32)
    large = np.minimum(large, NBUCK - 1)
    return np.where(n < max_exact, n, large).astype(np.int32)


def _fill_bias(bk_ref, rb_ref, bias_scr):
    bk = bk_ref[...]
    for h in range(NH):
        def step(b, acc, h=h):
            return acc + jnp.where(bk == b, rb_ref[b, h], 0.0)
        bias_scr[h] = lax.fori_loop(0, NBUCK, step, jnp.zeros((BLK, 2 * BLK), f32))


MIX_SUB = 2


class _Window:
    def __init__(self, zc_ref, zp_ref, n, s):
        self.blk = n * MIX_SUB + s
        self.first_in_step = s == 0
        self.cur = lambda a, b: zc_ref[s * BLK:(s + 1) * BLK, a:b]
        self.prev = (lambda a, b: zp_ref[:, a:b]) if s == 0 else (lambda a, b: zc_ref[(s - 1) * BLK:s * BLK, a:b])


def _attn_probs(win, kh, qg, kg, sk_ref, bias_scr):
    n = win.blk
    kc = DATTN + HD * kh
    vc = DATTN + DKV + HD * kh
    kx = jnp.concatenate([win.prev(kc, kc + HD), win.cur(kc, kc + HD)], axis=0)
    vx = jnp.concatenate([win.prev(vc, vc + HD), win.cur(vc, vc + HD)], axis=0)
    qx = jnp.concatenate([win.cur(HD * (GQA * kh + g), HD * (GQA * kh + g + 1)) for g in range(GQA)], axis=0)
    rq = lax.rsqrt(jnp.mean(qx * qx, axis=-1, keepdims=True) + EPS)
    rk = lax.rsqrt(jnp.mean(kx * kx, axis=-1, keepdims=True) + EPS)
    qhat, khat = qx * rq, kx * rk
    qnb, knb = (qhat * qg).astype(bf16), (khat * kg).astype(bf16)
    s = _nt(qnb, knb) * SCALE + bias_scr[GQA * kh:GQA * (kh + 1)].reshape(GQA * BLK, 2 * BLK)
    row = lax.broadcasted_iota(i32, (GQA * BLK, 2 * BLK), 0) & (BLK - 1)
    col = lax.broadcasted_iota(i32, (GQA * BLK, 2 * BLK), 1)
    mask = (col > row) & (col <= row + BLK) & ((col >= BLK) | (n > 0))
    s = jnp.where(mask, s, NEG)
    ridx = lax.broadcasted_iota(i32, (GQA * BLK, 1), 0)
    sink = jnp.full((GQA * BLK, 1), sk_ref[GQA * kh + GQA - 1], f32)
    for g in range(GQA - 2, -1, -1):
        sink = jnp.where(ridx < (g + 1) * BLK, sk_ref[GQA * kh + g], sink)
    m = jnp.maximum(jnp.max(s, axis=-1, keepdims=True), sink)
    e = jnp.exp(s - m)
    es = jnp.exp(sink - m)
    den = jnp.sum(e, axis=-1, keepdims=True) + es
    return dict(p=e / den, psink=es / den, qhat=qhat, khat=khat, rq=rq, rk=rk, qnb=qnb, knb=knb, vb=vx.astype(bf16))


def _pool_group(win, g, w):
    n = win.blk
    c0 = DATTN + 2 * DKV + PGD * g
    uc = win.cur(c0, c0 + PGD)
    up = jnp.where(n > 0, win.prev(c0, c0 + PGD), 0.0)
    ue = jnp.concatenate([up, uc], axis=0)
    hi = ue.astype(bf16)
    lo = (ue - hi.astype(f32)).astype(bf16)
    t = lax.broadcasted_iota(i32, (BLK, 2 * BLK), 0)
    s = lax.broadcasted_iota(i32, (BLK, 2 * BLK), 1)
    band = jnp.where((s <= t + BLK) & (s > t + BLK - w), 1.0, 0.0).astype(bf16)
    sm = _nn(band, hi) + _nn(band, lo)
    pos = n * BLK + lax.broadcasted_iota(i32, (BLK, 1), 0) + 1
    cnt = jnp.minimum(pos, w).astype(f32)
    return sm / cnt - uc, band, cnt


def _mix_fwd(z, qg, kg, sinks, relb, bucket, pool_w, pscale, name):
    T = z.shape[0]
    step_rows = MIX_SUB * BLK
    nsteps = T // step_rows

    def body(zc_ref, zp_ref, qg_ref, kg_ref, sk_ref, rb_ref, bk_ref, pw_ref, ps_ref, y_ref, bias_scr, yacc):
        n = pl.program_id(0)

        @pl.when(n == 0)
        def _():
            _fill_bias(bk_ref, rb_ref, bias_scr)

        for s in range(MIX_SUB):
            win = _Window(zc_ref, zp_ref, n, s)
            rows = slice(s * BLK, (s + 1) * BLK)
            for kh in range(NKV):
                a = _attn_probs(win, kh, qg_ref[...], kg_ref[...], sk_ref, bias_scr)
                o = _nn(a["p"].astype(bf16), a["vb"])
                for g in range(GQA):
                    hc = HD * (GQA * kh + g)
                    yacc[rows, hc:hc + HD] = o[g * BLK:(g + 1) * BLK]
            for g, w in enumerate(POOL_WINDOWS):
                pooled, _, _ = _pool_group(win, g, w)
                yp = _nn(pooled.astype(bf16), pw_ref[g].astype(bf16)) * ps_ref[:, PGD * g:PGD * (g + 1)]
                yacc[rows, DATTN + PGD * g:DATTN + PGD * (g + 1)] = yp
        y_ref[...] = yacc[...].astype(bf16)

    full = lambda *shape: pl.BlockSpec(shape, lambda n: (0,) * len(shape))
    smem = pl.BlockSpec(memory_space=pltpu.SMEM)
    return pl.pallas_call(
        body, grid=(nsteps,),
        in_specs=[pl.BlockSpec((step_rows, DIN), lambda n: (n, 0)),
                  pl.BlockSpec((BLK, DIN), lambda n: (jnp.maximum(n * MIX_SUB - 1, 0), 0)),
                  full(1, HD), full(1, HD), smem, smem, full(BLK, 2 * BLK),
                  full(len(POOL_WINDOWS), PGD, PGD), full(1, DPOOL)],
        out_specs=pl.BlockSpec((step_rows, DMIX), lambda n: (n, 0)),
        out_shape=SDS((T, DMIX), bf16),
        scratch_shapes=[pltpu.VMEM((NH, BLK, 2 * BLK), f32), pltpu.VMEM((step_rows, DMIX), f32)],
        compiler_params=_cparams(("arbitrary",)), name=name)(z, z, qg, kg, sinks, relb, bucket, pool_w, pscale)


def _mix_bwd(z, dy, qg, kg, sinks, relb, bucket, pool_w, pscale, name):
    T = z.shape[0]
    step_rows = MIX_SUB * BLK
    nsteps = T // step_rows

    def body(zc_ref, zp_ref, dy_ref, qg_ref, kg_ref, sk_ref, rb_ref, bk_ref, pw_ref, ps_ref,
             dz_ref, dqg_ref, dkg_ref, dsk_ref, drb_ref, dpw_ref, dps_ref, bias_scr, dbias_scr):
        n = pl.program_id(0)

        @pl.when(n == 0)
        def _():
            _fill_bias(bk_ref, rb_ref, bias_scr)
            dbias_scr[...] = jnp.zeros_like(dbias_scr)
            dqg_ref[...] = jnp.zeros_like(dqg_ref)
            dkg_ref[...] = jnp.zeros_like(dkg_ref)
            dsk_ref[...] = jnp.zeros_like(dsk_ref)
            dpw_ref[...] = jnp.zeros_like(dpw_ref)
            dps_ref[...] = jnp.zeros_like(dps_ref)

        qg, kg = qg_ref[...], kg_ref[...]
        lane = lax.broadcasted_iota(i32, (1, 128), 1)
        dsk = jnp.zeros((1, 128), f32)
        for s in range(MIX_SUB):
            win = _Window(zc_ref, zp_ref, n, s)
            blk = win.blk
            rows = pl.ds(pl.multiple_of(blk * BLK, BLK), BLK)
            prow = pl.ds(pl.multiple_of(jnp.maximum(blk - 1, 0) * BLK, BLK), BLK)
            dyr = slice(s * BLK, (s + 1) * BLK)

            def into_prev(fn, s=s):
                if s == 0:
                    pl.when(n > 0)(fn)
                else:
                    fn()

            for kh in range(NKV):
                a = _attn_probs(win, kh, qg, kg, sk_ref, bias_scr)
                p = a["p"]
                do = jnp.concatenate([dy_ref[dyr, HD * (GQA * kh + g):HD * (GQA * kh + g + 1)] for g in range(GQA)],
                                     axis=0).astype(bf16)
                dv = _tn(p.astype(bf16), do)
                dp = _nt(do, a["vb"])
                delta = jnp.sum(p * dp, axis=-1, keepdims=True)
                ds = p * (dp - delta)
                sinkterm = a["psink"] * delta
                for g in range(GQA):
                    h = GQA * kh + g
                    dbias_scr[h] += ds[g * BLK:(g + 1) * BLK]
                    tot = jnp.sum(sinkterm[g * BLK:(g + 1) * BLK], axis=0, keepdims=True)
                    dsk = dsk - jnp.where(lane == h, tot, 0.0)
                dsb = ds.astype(bf16)
                dqn = _nn(dsb, a["knb"]) * SCALE
                dkn = _tn(dsb, a["qnb"]) * SCALE
                qhat, khat = a["qhat"], a["khat"]
                dqg_ref[...] += jnp.sum(dqn * qhat, axis=0, keepdims=True)
                dkg_ref[...] += jnp.sum(dkn * khat, axis=0, keepdims=True)
                dqh = dqn * qg
                dq = a["rq"] * (dqh - qhat * jnp.mean(dqh * qhat, axis=-1, keepdims=True))
                dkh = dkn * kg
                dk = a["rk"] * (dkh - khat * jnp.mean(dkh * khat, axis=-1, keepdims=True))
                kc = DATTN + HD * kh
                vc = DATTN + DKV + HD * kh
                for g in range(GQA):
                    hc = HD * (GQA * kh + g)
                    dz_ref[rows, hc:hc + HD] = dq[g * BLK:(g + 1) * BLK]
                dz_ref[rows, kc:kc + HD] = dk[BLK:2 * BLK]
                dz_ref[rows, vc:vc + HD] = dv[BLK:2 * BLK]

                def kv_prev(dk=dk, dv=dv, kc=kc, vc=vc, prow=prow):
                    dz_ref[prow, kc:kc + HD] += dk[0:BLK]
                    dz_ref[prow, vc:vc + HD] += dv[0:BLK]

                into_prev(kv_prev)

            for g, w in enumerate(POOL_WINDOWS):
                c0 = DATTN + 2 * DKV + PGD * g
                pooled, band, cnt = _pool_group(win, g, w)
                pb = pooled.astype(bf16)
                wb = pw_ref[g].astype(bf16)
                dyp = dy_ref[dyr, DATTN + PGD * g:DATTN + PGD * (g + 1)]
                ypre = _nn(pb, wb)
                dps_ref[:, PGD * g:PGD * (g + 1)] += jnp.sum(dyp * ypre, axis=0, keepdims=True)
                dyg = (dyp * ps_ref[:, PGD * g:PGD * (g + 1)]).astype(bf16)
                dpw_ref[g] += _tn(pb, dyg)
                dpooled = _nt(dyg, wb)
                dsm = dpooled / cnt
                hi = dsm.astype(bf16)
                lo = (dsm - hi.astype(f32)).astype(bf16)
                due = _tn(band, hi) + _tn(band, lo)
                dz_ref[rows, c0:c0 + PGD] = due[BLK:2 * BLK] - dpooled

                def pool_prev(due=due, c0=c0, prow=prow):
                    dz_ref[prow, c0:c0 + PGD] += due[0:BLK]

                into_prev(pool_prev)

        dsk_ref[...] += dsk

        @pl.when(n == nsteps - 1)
        def _():
            bk = bk_ref[...]
            ri = lax.broadcasted_iota(i32, (NBUCK, NH), 0)
            ci = lax.broadcasted_iota(i32, (NBUCK, NH), 1)

            def step(b, acc):
                for h in range(NH):
                    sel = jnp.where(bk == b, dbias_scr[h], 0.0)
                    tot = jnp.sum(jnp.sum(sel, axis=1, keepdims=True), axis=0, keepdims=True)
                    acc = acc + jnp.where((ri == b) & (ci == h), tot, 0.0)
                return acc

            drb_ref[...] = lax.fori_loop(0, NBUCK, step, jnp.zeros((NBUCK, NH), f32))

    full = lambda *shape: pl.BlockSpec(shape, lambda n: (0,) * len(shape))
    smem = pl.BlockSpec(memory_space=pltpu.SMEM)
    npg = len(POOL_WINDOWS)
    return pl.pallas_call(
        body, grid=(nsteps,),
        in_specs=[pl.BlockSpec((step_rows, DIN), lambda n: (n, 0)),
                  pl.BlockSpec((BLK, DIN), lambda n: (jnp.maximum(n * MIX_SUB - 1, 0), 0)),
                  pl.BlockSpec((step_rows, DMIX), lambda n: (n, 0)),
                  full(1, HD), full(1, HD), smem, smem, full(BLK, 2 * BLK), full(npg, PGD, PGD), full(1, DPOOL)],
        out_specs=[full(T, DIN), full(1, HD), full(1, HD), full(1, 128), full(NBUCK, NH),
                   full(npg, PGD, PGD), full(1, DPOOL)],
        out_shape=(SDS((T, DIN), f32), SDS((1, HD), f32), SDS((1, HD), f32), SDS((1, 128), f32),
                   SDS((NBUCK, NH), f32), SDS((npg, PGD, PGD), f32), SDS((1, DPOOL), f32)),
        scratch_shapes=[pltpu.VMEM((NH, BLK, 2 * BLK), f32), pltpu.VMEM((NH, BLK, 2 * BLK), f32)],
        compiler_params=_cparams(("arbitrary",), VMEM_LIMIT_V7X),
        name=name)(z, z, dy, qg, kg, sinks, relb, bucket, pool_w, pscale)


class _LocalWeights:
    def __init__(self, w1, wint, wout, w2):
        self.w1, self.wint, self.wout, self.w2 = w1, wint, wout, w2

    def ffn1(self):
        return self.w1

    def mix(self, after):
        return self.wint, self.wout

    def before_out_proj(self, wout, after):
        return wout

    def ffn2(self, after):
        return self.w2

    def mix_ffn2_grads_ready(self, dwint, dwout, dw2, dh2):
        self.grads_rest = (dwint, dwout, dw2)
        return dh2

    def before_ffn1_bwd(self, dx1b):
        return dx1b


def _local_step(x, target, weights, g1, gm, g3, qg, kg, sinks, relb, pool_w, pscale):
    bucket = jnp.asarray(_t5_bucket_table())
    sk = sinks.reshape(NH)
    w1 = weights.ffn1()
    h1 = _norm_fwd(x, g1, "norm1_fwd")
    x1, gate1, up1 = _ffn_fwd(h1, w1, x, "ffn1_fwd")
    h2 = _norm_fwd(x1, gm, "norm2_fwd")
    wint, wout = weights.mix(h2)
    z = _in_proj_fwd(h2, wint, "in_proj_fwd")
    ymix = _mix_fwd(z, qg, kg, sk, relb, bucket, pool_w, pscale, "mix_fwd")
    wout = weights.before_out_proj(wout, ymix)
    x2 = _out_proj_fwd(ymix, wout, x1, "out_proj_fwd")
    h3 = _norm_fwd(x2, g3, "norm3_fwd")
    w2 = weights.ffn2(h3)
    y, gate2, up2 = _ffn_fwd(h3, w2, x2, "ffn2_fwd")
    dy, dyb, loss_lanes = _loss_grad(y, target, "loss_grad")

    dh3, dw2 = _ffn_bwd(dyb, h3, gate2, up2, w2, "ffn2_bwd")
    dx2, dx2b, dg3 = _norm_bwd(dh3, x2, g3, dy, 1.0, "norm3_bwd")
    dymix, dwout = _out_proj_bwd(dx2b, wout, ymix, "out_proj_bwd")
    dz, dqg, dkg, dsk, drb, dpw, dps = _mix_bwd(z, dymix, qg, kg, sk, relb, bucket, pool_w, pscale, "mix_bwd")
    dh2, dwint = _in_proj_bwd(dz, wint, h2, "in_proj_bwd")
    dh2 = weights.mix_ffn2_grads_ready(dwint, dwout, dw2, dh2)
    dx1, dx1b, dgm = _norm_bwd(dh2, x1, gm, dx2, 0.5, "norm2_bwd")
    dx1b = weights.before_ffn1_bwd(dx1b)
    dh1, dw1 = _ffn_bwd(dx1b, h1, gate1, up1, w1, "ffn1_bwd")
    gx, _, dg1 = _norm_bwd(dh1, x, g1, dx1, 1.0, "norm1_bwd")
    small = dict(ffn1_norm=dg1, mix_norm=dgm, ffn2_norm=dg3, pool_scale=dps, q_norm=dqg, k_norm=dkg,
                 attn_sinks=dsk[:, :NH], rel_bias=drb, pool_w=dpw, loss=loss_lanes)
    return gx, (dw1, dwint, dwout, dw2), small


SMALL_NAMES = ("ffn1_norm", "mix_norm", "ffn2_norm", "pool_scale", "q_norm", "k_norm", "attn_sinks", "rel_bias",
               "pool_w", "loss")
SMALL_SHAPES = dict(ffn1_norm=(1, D), mix_norm=(1, D), ffn2_norm=(1, D), pool_scale=(1, DPOOL), q_norm=(1, HD),
                    k_norm=(1, HD), attn_sinks=(1, NH), rel_bias=(NBUCK, NH),
                    pool_w=(1, len(POOL_WINDOWS), PGD, PGD), loss=(1, 128))


def _small_rows(name):
    return -(-int(np.prod(SMALL_SHAPES[name])) // 128)


SMALL_OFF = {}
_r = 0
for _n in SMALL_NAMES:
    SMALL_OFF[_n] = _r
    _r += _small_rows(_n)
SMALL_ROWS = -(-_r // 8) * 8
LOSS_ROW = SMALL_OFF["loss"]


def _pack_small(vals):
    parts = []
    for n in SMALL_NAMES:
        size = _small_rows(n) * 128
        if n in vals:
            flat = vals[n].astype(f32).reshape(-1)
            parts.append(jnp.pad(flat, (0, size - flat.shape[0])))
        else:
            parts.append(jnp.zeros((size,), f32))
    flat = jnp.concatenate(parts)
    flat = jnp.pad(flat, (0, SMALL_ROWS * 128 - flat.shape[0]))
    return flat.reshape(SMALL_ROWS, 128)


def _unpack_small(packed, name):
    size = int(np.prod(SMALL_SHAPES[name]))
    r0 = SMALL_OFF[name]
    return packed[r0:r0 + _small_rows(name)].reshape(-1)[:size].reshape(SMALL_SHAPES[name])


def _position():
    return lax.axis_index("x"), lax.axis_index("y"), lax.axis_index("c")


def _dev_index(x, y, c):
    return 4 * x + 2 * y + c


G1_PIECES, MIX_PIECES, F2_PIECES = (0, 1, 2), (3, 4), (5, 6, 7)


def _group_rows(pieces):
    return sum(PIECE_ROWS[k] for k in pieces)


def _shard_piece(s_ref, k):
    return s_ref.at[pl.ds(PIECE_OFF[k], PIECE_ROWS[k]), :]


def _shard_group(s_ref, pieces):
    return s_ref.at[pl.ds(PIECE_OFF[pieces[0]], _group_rows(pieces)), :]


def _weight_pieces(w1_ref=None, wi_ref=None, wo_ref=None, w2_ref=None):
    arrs = {}
    if w1_ref is not None:
        arrs.update({0: w1_ref.at[0], 1: w1_ref.at[1], 2: w1_ref.at[2]})
    if wi_ref is not None:
        arrs[3] = wi_ref
    if wo_ref is not None:
        arrs[4] = wo_ref
    if w2_ref is not None:
        arrs.update({5: w2_ref.at[0], 6: w2_ref.at[1], 7: w2_ref.at[2]})
    return arrs


def _block_rows(arrs, k, dev):
    r = PIECE_ROWS[k]
    return arrs[k].at[pl.ds(pl.multiple_of(_dev_index(*dev) * r, 16), r), :]


def _all_gather_ffn1(shard):
    pieces = G1_PIECES

    def body(s_ref, w1_ref, send_sems, recv_sems, local_sem):
        x, y, c = _position()
        me, sib = (x, y, c), (x, y, 1 - c)
        chips = [(1 - x, y), (x, 1 - y), (1 - x, 1 - y)]
        arrs = _weight_pieces(w1_ref=w1_ref)

        def copies(rel, block, to, from_shard):
            return [pltpu.make_async_remote_copy(
                src_ref=_shard_piece(s_ref, k) if from_shard else _block_rows(arrs, k, block),
                dst_ref=_block_rows(arrs, k, block),
                send_sem=send_sems.at[rel], recv_sem=recv_sems.at[rel], device_id=to, device_id_type=MESH)
                for k in pieces]

        def whole(rel):
            grp = _shard_group(s_ref, pieces)
            return pltpu.make_async_remote_copy(src_ref=grp, dst_ref=grp, send_sem=send_sems.at[rel],
                                                recv_sem=recv_sems.at[rel], device_id=me, device_id_type=MESH)

        mine = [pltpu.make_async_copy(_shard_piece(s_ref, k), _block_rows(arrs, k, me), local_sem) for k in pieces]
        for cp in mine:
            cp.start()
        for cp in copies(0, me, sib, True):
            cp.start()
        for j, chip in enumerate(chips):
            for cp in copies(1 + j, me, (*chip, c), True):
                cp.start()
        for j, chip in enumerate(chips):
            whole(1 + j).wait_recv()
            for cp in copies(4 + j, (*chip, c), sib, False):
                cp.start()
        whole(0).wait_recv()
        for j in range(3):
            whole(4 + j).wait_recv()
        for rel in range(7):
            whole(rel).wait_send()
        grp = _shard_group(s_ref, pieces)
        pltpu.make_async_copy(grp, grp, local_sem).wait()

    hbm = pl.BlockSpec(memory_space=pl.ANY)
    return pl.pallas_call(
        body, in_specs=[hbm], out_specs=hbm, out_shape=SDS((3, F, D), bf16),
        scratch_shapes=[pltpu.SemaphoreType.DMA((7,)), pltpu.SemaphoreType.DMA((7,)), pltpu.SemaphoreType.DMA],
        compiler_params=pltpu.CompilerParams(has_side_effects=True),
        name="all_gather_ffn1")(shard)


HBM_SPEC = pl.BlockSpec(memory_space=pltpu.HBM)
SEM_SPEC = pl.BlockSpec(memory_space=pltpu.SEMAPHORE)
ANY_SPEC = pl.BlockSpec(memory_space=pl.ANY)
SPLIT_EFFECT = pltpu.SideEffectType.DATAFLOW_SIDE_EFFECTING


def _in_hbm(a):
    return pltpu.with_memory_space_constraint(a, pltpu.HBM)


def _hbm_like(a):
    return pltpu.HBM(a.shape, a.dtype)


def _place_own_rows(shard):
    pieces = MIX_PIECES + F2_PIECES

    def body(s_ref, wi_ref, wo_ref, w2_ref, buf, sems):
        x, y, c = _position()
        arrs = _weight_pieces(wi_ref=wi_ref, wo_ref=wo_ref, w2_ref=w2_ref)
        grp = _shard_group(s_ref, pieces)
        load = pltpu.make_async_copy(grp, buf, sems.at[0])
        load.start()
        load.wait()
        base = PIECE_OFF[pieces[0]]
        for k in pieces:
            pltpu.make_async_copy(buf.at[pl.ds(PIECE_OFF[k] - base, PIECE_ROWS[k]), :],
                                  _block_rows(arrs, k, (x, y, c)), sems.at[1]).start()
        pltpu.make_async_copy(grp, buf, sems.at[1]).wait()

    return pl.pallas_call(
        body, in_specs=[ANY_SPEC], out_specs=[ANY_SPEC] * 3,
        out_shape=(SDS((DIN, D), bf16), SDS((DMIX, D), bf16), SDS((3, F, D), bf16)),
        scratch_shapes=[pltpu.VMEM((_group_rows(pieces), D), bf16), pltpu.SemaphoreType.DMA((2,))],
        name="place_own_rows")(shard)


def _xor_peer(x, y, c, k):
    return (x ^ (k >> 2), y ^ ((k >> 1) & 1), c ^ (k & 1))


def _gather_rest_start(shard, wi, wo, w2, w1):
    def body(s_ref, wi_ref, wo_ref, w2_ref, w1_ref,
             ssem_m, rsem_m, ssem_f, rsem_f0, rsem_f, s_o, wi_o, wo_o, w2_o, w1_o):
        x, y, c = _position()
        me, sib = (x, y, c), (x, y, 1 - c)
        chips = [(1 - x, y), (x, 1 - y), (1 - x, 1 - y)]
        arrs = _weight_pieces(wi_ref=wi_ref, wo_ref=wo_ref, w2_ref=w2_ref)
        for k in range(1, NDEV):
            for p in MIX_PIECES:
                pltpu.make_async_remote_copy(
                    src_ref=_shard_piece(s_ref, p), dst_ref=_block_rows(arrs, p, me), send_sem=ssem_m.at[k - 1],
                    recv_sem=rsem_m.at[k - 1], device_id=_xor_peer(x, y, c, k), device_id_type=MESH).start()
        for p in F2_PIECES:
            pltpu.make_async_remote_copy(
                src_ref=_shard_piece(s_ref, p), dst_ref=_block_rows(arrs, p, me), send_sem=ssem_f.at[0],
                recv_sem=rsem_f0, device_id=sib, device_id_type=MESH).start()
        for j, chip in enumerate(chips):
            for p in F2_PIECES:
                pltpu.make_async_remote_copy(
                    src_ref=_shard_piece(s_ref, p), dst_ref=_block_rows(arrs, p, me), send_sem=ssem_f.at[1 + j],
                    recv_sem=rsem_f.at[j], device_id=(*chip, c), device_id_type=MESH).start()

    dma = pltpu.SemaphoreType.DMA
    return pl.pallas_call(
        body, name="gather_rest_start",
        out_shape=(dma((7,)), dma((7,)), dma((4,)), dma(()), dma((3,)),
                   _hbm_like(shard), _hbm_like(wi), _hbm_like(wo), _hbm_like(w2), _hbm_like(w1)),
        in_specs=(HBM_SPEC,) * 5, out_specs=(SEM_SPEC,) * 5 + (HBM_SPEC,) * 5,
        input_output_aliases={0: 5, 1: 6, 2: 7, 3: 8, 4: 9},
        compiler_params=pltpu.CompilerParams(has_side_effects=SPLIT_EFFECT),
    )(_in_hbm(shard), _in_hbm(wi), _in_hbm(wo), _in_hbm(w2), _in_hbm(w1))


def _gather_mix_wait(ssem_m, rsem_m, shard, wi, wo, after):
    def body(s_ref, wi_ref, wo_ref, ssem, rsem, after_ref, s_o, wi_o, wo_o):
        x, y, c = _position()
        grp = _shard_group(s_ref, MIX_PIECES)
        for k in range(NDEV - 1):
            d = pltpu.make_async_remote_copy(src_ref=grp, dst_ref=grp, send_sem=ssem.at[k], recv_sem=rsem.at[k],
                                             device_id=(x, y, c), device_id_type=MESH)
            d.wait_recv()
            d.wait_send()

    return pl.pallas_call(
        body, name="gather_mix_wait", out_shape=(_hbm_like(shard), _hbm_like(wi), _hbm_like(wo)),
        in_specs=(HBM_SPEC, HBM_SPEC, HBM_SPEC, SEM_SPEC, SEM_SPEC, ANY_SPEC), out_specs=(HBM_SPEC,) * 3,
        input_output_aliases={0: 0, 1: 1, 2: 2},
        compiler_params=pltpu.CompilerParams(has_side_effects=SPLIT_EFFECT),
    )(shard, wi, wo, ssem_m, rsem_m, after)


def _gather_ffn2_pass_on(rsem_f, w2, wo, after):
    def body(w2_ref, wo_ref, rsem, after_ref, fsend, frecv, w2_o, wo_o):
        x, y, c = _position()
        sib = (x, y, 1 - c)
        chips = [(1 - x, y), (x, 1 - y), (1 - x, 1 - y)]
        arrs = _weight_pieces(w2_ref=w2_ref)
        three = w2_ref.at[0, pl.ds(0, _group_rows(F2_PIECES)), :]
        for j, chip in enumerate(chips):
            pltpu.make_async_remote_copy(src_ref=three, dst_ref=three, send_sem=fsend.at[j], recv_sem=rsem.at[j],
                                         device_id=(x, y, c), device_id_type=MESH).wait_recv()
            for p in F2_PIECES:
                rows = _block_rows(arrs, p, (*chip, c))
                pltpu.make_async_remote_copy(src_ref=rows, dst_ref=rows, send_sem=fsend.at[j], recv_sem=frecv.at[j],
                                             device_id=sib, device_id_type=MESH).start()

    dma = pltpu.SemaphoreType.DMA
    return pl.pallas_call(
        body, name="gather_ffn2_pass_on", out_shape=(dma((3,)), dma((3,)), _hbm_like(w2), _hbm_like(wo)),
        in_specs=(HBM_SPEC, HBM_SPEC, SEM_SPEC, ANY_SPEC), out_specs=(SEM_SPEC, SEM_SPEC, HBM_SPEC, HBM_SPEC),
        input_output_aliases={0: 2, 1: 3},
        compiler_params=pltpu.CompilerParams(has_side_effects=SPLIT_EFFECT),
    )(w2, wo, rsem_f, after)


def _gather_ffn2_wait(ssem_f, rsem_f0, fsend, frecv, shard, w2, after):
    def body(s_ref, w2_ref, ssem, rsem0, fs, fr, after_ref, w2_o):
        x, y, c = _position()
        grp = _shard_group(s_ref, F2_PIECES)

        def waiter(send_sem, recv_sem):
            return pltpu.make_async_remote_copy(src_ref=grp, dst_ref=grp, send_sem=send_sem, recv_sem=recv_sem,
                                                device_id=(x, y, c), device_id_type=MESH)

        waiter(ssem.at[0], rsem0).wait_recv()
        for j in range(3):
            waiter(fs.at[j], fr.at[j]).wait_recv()
        for rel in range(4):
            waiter(ssem.at[rel], rsem0).wait_send()
        for j in range(3):
            waiter(fs.at[j], fr.at[j]).wait_send()

    return pl.pallas_call(
        body, name="gather_ffn2_wait", out_shape=_hbm_like(w2),
        in_specs=(HBM_SPEC, HBM_SPEC, SEM_SPEC, SEM_SPEC, SEM_SPEC, SEM_SPEC, ANY_SPEC), out_specs=HBM_SPEC,
        input_output_aliases={1: 0},
        compiler_params=pltpu.CompilerParams(has_side_effects=SPLIT_EFFECT),
    )(shard, w2, ssem_f, rsem_f0, fsend, frecv, after)


class _GatheredWeights(_LocalWeights):
    def __init__(self, shard):
        w1 = _all_gather_ffn1(shard)
        wi, wo, w2 = _place_own_rows(shard)
        (self.ssem_m, self.rsem_m, self.ssem_f, self.rsem_f0, self.rsem_f,
         self.shard, self.wi, self.wo, self.w2_part, self.w1) = _gather_rest_start(shard, wi, wo, w2, w1)

    def mix(self, after):
        self.shard, wint, wout = _gather_mix_wait(self.ssem_m, self.rsem_m, self.shard, self.wi, self.wo, after)
        return wint, wout

    def before_out_proj(self, wout, after):
        self.fsend, self.frecv, self.w2_part, wout = _gather_ffn2_pass_on(self.rsem_f, self.w2_part, wout, after)
        return wout

    def ffn2(self, after):
        return _gather_ffn2_wait(self.ssem_f, self.rsem_f0, self.fsend, self.frecv, self.shard, self.w2_part, after)

    def mix_ffn2_grads_ready(self, dwint, dwout, dw2, dh2):
        rx1 = lax.empty((4, RSA_ROWS, D), bf16)
        self.sa, self.ra, dwint, dwout, dw2, rx1, dh2 = _rsa_level1_start(dwint, dwout, dw2, rx1, dh2)
        self.level1 = (dwint, dwout, dw2, rx1)
        return dh2

    def before_ffn1_bwd(self, dx1b):
        dwint, dwout, dw2, rx1 = _rsa_level1_wait(self.sa, self.ra, *self.level1, dx1b)
        tx, self.acc = _rsa_chip_sums(dwint, dwout, dw2, rx1)
        rx2 = lax.empty((3, RSA_ROWS, D), bf16)
        self.sb, self.rb, self.tx, self.rx2, dx1b = _rsa_level2_start(tx, rx2, dx1b)
        return dx1b

    def mix_ffn2_grads_total(self, after):
        rx2 = _rsa_level2_wait(self.sb, self.rb, self.tx, self.rx2, after)
        return _rsa_total(self.acc, rx2)


RS_CHUNK = 176


def _reduce_scatter_ffn1(dw1):
    pieces = G1_PIECES
    nrows = _group_rows(pieces)
    nchunk = nrows // RS_CHUNK

    def body(d1_ref, red_ref, rx1_ref, rx2_ref,
             own_buf, rx_buf, tx_buf, acc, sa, ra, sb, rb, lsem):
        x, y, c = _position()
        me, sib = (x, y, c), (x, y, 1 - c)
        rel_chips = [(x, y), (1 - x, y), (x, 1 - y), (1 - x, 1 - y)]
        srcs = _weight_pieces(w1_ref=d1_ref)

        def piece(k, dev):
            r = PIECE_ROWS[k]
            return srcs[k].at[pl.ds(pl.multiple_of(_dev_index(*dev) * r, 16), r), :]

        def packed(ref, k):
            return ref.at[pl.ds(PIECE_OFF[k], PIECE_ROWS[k]), :]

        for j, chip in enumerate(rel_chips):
            for k in pieces:
                pltpu.make_async_remote_copy(
                    src_ref=piece(k, (*chip, 1 - c)), dst_ref=packed(rx1_ref.at[j], k),
                    send_sem=sa.at[j], recv_sem=ra.at[j], device_id=sib, device_id_type=MESH).start()

        def wait_a(j):
            return pltpu.make_async_remote_copy(src_ref=rx1_ref.at[j], dst_ref=rx1_ref.at[j], send_sem=sa.at[j],
                                                recv_sem=ra.at[j], device_id=me, device_id_type=MESH)

        def ici(j):
            return pltpu.make_async_remote_copy(
                src_ref=tx_buf.at[j - 1], dst_ref=rx2_ref.at[j - 1], send_sem=sb.at[j - 1], recv_sem=rb.at[j - 1],
                device_id=(*rel_chips[j], c), device_id_type=MESH)

        for j in (1, 2, 3, 0):
            loads = [pltpu.make_async_copy(piece(k, (*rel_chips[j], c)), packed(own_buf, k), lsem)
                     for k in pieces]
            for cp in loads:
                cp.start()
            wait_a(j).wait_recv()
            got = pltpu.make_async_copy(rx1_ref.at[j], rx_buf, lsem)
            got.start()
            pltpu.make_async_copy(rx_buf, rx_buf, lsem).wait()
            got.wait()

            def add(i, carry, j=j):
                rows = pl.ds(pl.multiple_of(i * RS_CHUNK, 16), RS_CHUNK)
                tot = own_buf[rows, :].astype(f32) + rx_buf[rows, :].astype(f32)
                if j == 0:
                    acc[rows, :] = tot
                else:
                    tx_buf[j - 1, rows, :] = tot.astype(bf16)
                return carry

            lax.fori_loop(0, nchunk, add, 0)
            if j != 0:
                ici(j).start()

        for j in (1, 2, 3):
            ici(j).wait_recv()
            got = pltpu.make_async_copy(rx2_ref.at[j - 1], rx_buf, lsem)
            got.start()
            got.wait()

            def add2(i, carry):
                rows = pl.ds(pl.multiple_of(i * RS_CHUNK, 16), RS_CHUNK)
                acc[rows, :] += rx_buf[rows, :].astype(f32)
                return carry

            lax.fori_loop(0, nchunk, add2, 0)
        out = pltpu.make_async_copy(acc, red_ref, lsem)
        out.start()
        out.wait()
        for j in range(4):
            wait_a(j).wait_send()
        for j in (1, 2, 3):
            ici(j).wait_send()

    hbm = pl.BlockSpec(memory_space=pl.ANY)
    red, _, _ = pl.pallas_call(
        body, in_specs=[hbm], out_specs=[hbm] * 3,
        out_shape=(SDS((nrows, D), f32), SDS((4, nrows, D), bf16), SDS((3, nrows, D), bf16)),
        scratch_shapes=[pltpu.VMEM((nrows, D), bf16), pltpu.VMEM((nrows, D), bf16),
                        pltpu.VMEM((3, nrows, D), bf16), pltpu.VMEM((nrows, D), f32),
                        pltpu.SemaphoreType.DMA((4,)), pltpu.SemaphoreType.DMA((4,)),
                        pltpu.SemaphoreType.DMA((3,)), pltpu.SemaphoreType.DMA((3,)), pltpu.SemaphoreType.DMA],
        compiler_params=pltpu.CompilerParams(has_side_effects=True, vmem_limit_bytes=VMEM_LIMIT_V7X),
        name="reduce_scatter_ffn1")(dw1)
    return red


RSA_PIECES = MIX_PIECES + F2_PIECES
RSA_ROWS = _group_rows(RSA_PIECES)
RSA_OFF = {k: PIECE_OFF[k] - PIECE_OFF[RSA_PIECES[0]] for k in RSA_PIECES}
RSA_BLOCK = 192


def _rsa_rows(ref, k):
    return ref.at[pl.ds(RSA_OFF[k], PIECE_ROWS[k]), :]


def _rsa_level1_start(dwint, dwout, dw2, rx1, thru):
    def body(di_ref, do_ref, d2_ref, rx1_ref, thru_ref, sa, ra, di_o, do_o, d2_o, rx1_o, thru_o):
        x, y, c = _position()
        srcs = _weight_pieces(wi_ref=di_ref, wo_ref=do_ref, w2_ref=d2_ref)
        for j, chip in enumerate([(x, y), (1 - x, y), (x, 1 - y), (1 - x, 1 - y)]):
            for k in RSA_PIECES:
                pltpu.make_async_remote_copy(
                    src_ref=_block_rows(srcs, k, (*chip, 1 - c)), dst_ref=_rsa_rows(rx1_ref.at[j], k),
                    send_sem=sa.at[j], recv_sem=ra.at[j], device_id=(x, y, 1 - c), device_id_type=MESH).start()

    dma = pltpu.SemaphoreType.DMA
    arrs = (dwint, dwout, dw2, rx1, thru)
    return pl.pallas_call(
        body, name="rsa_level1_start", out_shape=(dma((4,)), dma((4,))) + tuple(_hbm_like(a) for a in arrs),
        in_specs=(HBM_SPEC,) * 5, out_specs=(SEM_SPEC,) * 2 + (HBM_SPEC,) * 5,
        input_output_aliases={0: 2, 1: 3, 2: 4, 3: 5, 4: 6},
        compiler_params=pltpu.CompilerParams(has_side_effects=SPLIT_EFFECT),
    )(*[_in_hbm(a) for a in arrs])


def _rsa_level1_wait(sa, ra, dwint, dwout, dw2, rx1, after):
    def body(di_ref, do_ref, d2_ref, rx1_ref, sa_ref, ra_ref, after_ref, di_o, do_o, d2_o, rx1_o):
        x, y, c = _position()
        for j in range(4):
            d = pltpu.make_async_remote_copy(src_ref=rx1_ref.at[j], dst_ref=rx1_ref.at[j], send_sem=sa_ref.at[j],
                                             recv_sem=ra_ref.at[j], device_id=(x, y, c), device_id_type=MESH)
            d.wait_recv()
            d.wait_send()

    arrs = (dwint, dwout, dw2, rx1)
    return pl.pallas_call(
        body, name="rsa_level1_wait", out_shape=tuple(_hbm_like(a) for a in arrs),
        in_specs=(HBM_SPEC,) * 4 + (SEM_SPEC, SEM_SPEC, ANY_SPEC), out_specs=(HBM_SPEC,) * 4,
        input_output_aliases={0: 0, 1: 1, 2: 2, 3: 3},
        compiler_params=pltpu.CompilerParams(has_side_effects=SPLIT_EFFECT),
    )(*arrs, sa, ra, after)


def _rsa_chip_sums(dwint, dwout, dw2, rx1):
    nblk = RSA_ROWS // RSA_BLOCK

    def body(di_ref, do_ref, d2_ref, rx1_ref, tx_ref, acc_ref, own_buf, rx_buf, tx_buf, acc_buf, lsems):
        x, y, c = _position()
        srcs = _weight_pieces(wi_ref=di_ref, wo_ref=do_ref, w2_ref=d2_ref)
        for j, chip in enumerate([(x, y), (1 - x, y), (x, 1 - y), (1 - x, 1 - y)]):
            loads = [pltpu.make_async_copy(_block_rows(srcs, k, (*chip, c)), _rsa_rows(own_buf, k), lsems.at[0])
                     for k in RSA_PIECES]
            got = pltpu.make_async_copy(rx1_ref.at[j], rx_buf, lsems.at[1])
            for cp in loads + [got]:
                cp.start()
            pltpu.make_async_copy(rx_buf, rx_buf, lsems.at[0]).wait()
            got.wait()

            def add(i, carry, j=j):
                rows = pl.ds(pl.multiple_of(i * RSA_BLOCK, 16), RSA_BLOCK)
                tot = own_buf[rows, :].astype(f32) + rx_buf[rows, :].astype(f32)
                if j == 0:
                    acc_buf[rows, :] = tot
                else:
                    tx_buf[rows, :] = tot.astype(bf16)
                return carry

            lax.fori_loop(0, nblk, add, 0)
            out = (pltpu.make_async_copy(acc_buf, acc_ref, lsems.at[2]) if j == 0
                   else pltpu.make_async_copy(tx_buf, tx_ref.at[j - 1], lsems.at[2]))
            out.start()
            out.wait()

    return pl.pallas_call(
        body, in_specs=[ANY_SPEC] * 4, out_specs=[ANY_SPEC] * 2,
        out_shape=(SDS((3, RSA_ROWS, D), bf16), SDS((RSA_ROWS, D), f32)),
        scratch_shapes=[pltpu.VMEM((RSA_ROWS, D), bf16), pltpu.VMEM((RSA_ROWS, D), bf16),
                        pltpu.VMEM((RSA_ROWS, D), bf16), pltpu.VMEM((RSA_ROWS, D), f32),
                        pltpu.SemaphoreType.DMA((3,))],
        compiler_params=_cparams(None, VMEM_LIMIT_V7X), name="rsa_chip_sums")(dwint, dwout, dw2, rx1)


def _rsa_level2_start(tx, rx2, thru):
    def body(tx_ref, rx2_ref, thru_ref, sb, rb, tx_o, rx2_o, thru_o):
        x, y, c = _position()
        for j, chip in enumerate([(1 - x, y), (x, 1 - y), (1 - x, 1 - y)]):
            pltpu.make_async_remote_copy(src_ref=tx_ref.at[j], dst_ref=rx2_ref.at[j], send_sem=sb.at[j],
                                         recv_sem=rb.at[j], device_id=(*chip, c), device_id_type=MESH).start()

    dma = pltpu.SemaphoreType.DMA
    arrs = (tx, rx2, thru)
    return pl.pallas_call(
        body, name="rsa_level2_start", out_shape=(dma((3,)), dma((3,))) + tuple(_hbm_like(a) for a in arrs),
        in_specs=(HBM_SPEC,) * 3, out_specs=(SEM_SPEC,) * 2 + (HBM_SPEC,) * 3,
        input_output_aliases={0: 2, 1: 3, 2: 4},
        compiler_params=pltpu.CompilerParams(has_side_effects=SPLIT_EFFECT),
    )(*[_in_hbm(a) for a in arrs])


def _rsa_level2_wait(sb, rb, tx, rx2, after):
    def body(tx_ref, rx2_ref, sb_ref, rb_ref, after_ref, rx2_o):
        x, y, c = _position()
        for j in range(3):
            d = pltpu.make_async_remote_copy(src_ref=tx_ref.at[j], dst_ref=rx2_ref.at[j], send_sem=sb_ref.at[j],
                                             recv_sem=rb_ref.at[j], device_id=(x, y, c), device_id_type=MESH)
            d.wait_recv()
            d.wait_send()

    return pl.pallas_call(
        body, name="rsa_level2_wait", out_shape=_hbm_like(rx2),
        in_specs=(HBM_SPEC, HBM_SPEC, SEM_SPEC, SEM_SPEC, ANY_SPEC), out_specs=HBM_SPEC,
        input_output_aliases={1: 0},
        compiler_params=pltpu.CompilerParams(has_side_effects=SPLIT_EFFECT),
    )(tx, rx2, sb, rb, after)


def _rsa_total(acc, rx2):
    def body(a_ref, r_ref, o_ref):
        o_ref[...] = ((a_ref[...] + r_ref[0].astype(f32)) + r_ref[1].astype(f32)) + r_ref[2].astype(f32)

    return pl.pallas_call(
        body, grid=(RSA_ROWS // RSA_BLOCK,),
        in_specs=[pl.BlockSpec((RSA_BLOCK, D), lambda i: (i, 0)), pl.BlockSpec((3, RSA_BLOCK, D), lambda i: (0, i, 0))],
        out_specs=pl.BlockSpec((RSA_BLOCK, D), lambda i: (i, 0)),
        out_shape=SDS((RSA_ROWS, D), f32), name="rsa_total")(acc, rx2)


def _all_reduce_small(packed):
    def body(p_ref, o_ref, pair, chips, send_sems, recv_sems):
        x, y, c = _position()
        chip = 2 * x + y
        pair[c] = p_ref[...]
        swap = pltpu.make_async_remote_copy(
            src_ref=p_ref, dst_ref=pair.at[c], send_sem=send_sems.at[0], recv_sem=recv_sems.at[0],
            device_id=(x, y, 1 - c), device_id_type=MESH)
        swap.start()
        swap.wait_recv()
        chips[chip] = pair[0] + pair[1]
        cps = [pltpu.make_async_remote_copy(
            src_ref=chips.at[chip], dst_ref=chips.at[chip], send_sem=send_sems.at[1 + j], recv_sem=recv_sems.at[1 + j],
            device_id=(*other, c), device_id_type=MESH)
            for j, other in enumerate([(1 - x, y), (x, 1 - y), (1 - x, 1 - y)])]
        for cp in cps:
            cp.start()
        for cp in cps:
            cp.wait_recv()
        tot = (chips[0] + chips[1]) + (chips[2] + chips[3])
        o_ref[...] = tot
        loss = jnp.sum(tot[LOSS_ROW:LOSS_ROW + 1, :], axis=-1, keepdims=True)
        o_ref[LOSS_ROW:LOSS_ROW + 1, :] = jnp.broadcast_to(loss, (1, 128))
        swap.wait_send()
        for cp in cps:
            cp.wait_send()

    vm = pl.BlockSpec(memory_space=pltpu.VMEM)
    return pl.pallas_call(
        body, in_specs=[vm], out_specs=vm, out_shape=SDS((SMALL_ROWS, 128), f32),
        scratch_shapes=[pltpu.VMEM((2, SMALL_ROWS, 128), f32), pltpu.VMEM((4, SMALL_ROWS, 128), f32),
                        pltpu.SemaphoreType.DMA((4,)), pltpu.SemaphoreType.DMA((4,))],
        compiler_params=pltpu.CompilerParams(has_side_effects=True),
        name="all_reduce_small")(packed)


def _adamw_math(w, g, m, v):
    m = ADAM_B1 * m + (1.0 - ADAM_B1) * g
    v = ADAM_B2 * v + (1.0 - ADAM_B2) * (g * g)
    m_hat = m / (1.0 - ADAM_B1 ** ADAM_STEP)
    v_hat = v / (1.0 - ADAM_B2 ** ADAM_STEP)
    delta = -ADAM_LR * (m_hat / (jnp.sqrt(v_hat) + ADAM_EPS) + ADAM_WD * w)
    return delta, m, v


def _adamw_big(ws, ms, vs, red1, red_rest):
    npiece = len(BIG)
    rmax = max(PIECE_ROWS)

    def body(*refs):
        ins = (refs[0:npiece], refs[npiece:2 * npiece], refs[2 * npiece:3 * npiece])
        red1_ref, rest_ref = refs[3 * npiece:3 * npiece + 2]
        out_refs = refs[3 * npiece + 2:7 * npiece + 2]
        inb, outb, in_sems, out_sems = refs[7 * npiece + 2:]

        def grad_rows(k):
            if k in G1_PIECES:
                return red1_ref.at[pl.ds(PIECE_OFF[k], PIECE_ROWS[k]), :]
            return _rsa_rows(rest_ref, k)

        def loads(k):
            s, r = k % 2, PIECE_ROWS[k]
            cps = [pltpu.make_async_copy(ins[q][k].at[0], inb.at[s, q, pl.ds(0, r), :], in_sems.at[4 * s + q])
                   for q in range(3)]
            cps.append(pltpu.make_async_copy(grad_rows(k), inb.at[s, 3, pl.ds(0, r), :], in_sems.at[4 * s + 3]))
            return cps

        def stores(k):
            s, r = k % 2, PIECE_ROWS[k]
            return [pltpu.make_async_copy(outb.at[s, q, pl.ds(0, r), :], out_refs[q * npiece + k].at[0],
                                          out_sems.at[4 * s + q]) for q in range(4)]

        for cp in loads(0):
            cp.start()
        for k in range(npiece):
            s, r = k % 2, PIECE_ROWS[k]
            if k + 1 < npiece:
                for cp in loads(k + 1):
                    cp.start()
            for cp in loads(k):
                cp.wait()
            if k >= 2:
                for cp in stores(k - 2):
                    cp.wait()
            g = inb[s, 3, 0:r, :]
            d, nm, nv = _adamw_math(inb[s, 0, 0:r, :], g, inb[s, 1, 0:r, :], inb[s, 2, 0:r, :])
            outb[s, 0, 0:r, :] = g
            outb[s, 1, 0:r, :] = d
            outb[s, 2, 0:r, :] = nm
            outb[s, 3, 0:r, :] = nv
            for cp in stores(k):
                cp.start()
        for k in (npiece - 2, npiece - 1):
            for cp in stores(k):
                cp.wait()

    hbm = pl.BlockSpec(memory_space=pl.ANY)
    outs = pl.pallas_call(
        body, in_specs=[hbm] * (3 * npiece + 2), out_specs=[hbm] * (4 * npiece),
        out_shape=tuple(SDS(w.shape, f32) for _ in range(4) for w in ws),
        scratch_shapes=[pltpu.VMEM((2, 4, rmax, D), f32), pltpu.VMEM((2, 4, rmax, D), f32),
                        pltpu.SemaphoreType.DMA((8,)), pltpu.SemaphoreType.DMA((8,))],
        compiler_params=_cparams(None, VMEM_LIMIT_V7X), name="adamw_big")(*ws, *ms, *vs, red1, red_rest)
    return [list(outs[q * npiece:(q + 1) * npiece]) for q in range(4)]


def _adamw_small(w, m, v, g, name):
    def body(w_ref, m_ref, v_ref, g_ref, d_ref, nm_ref, nv_ref):
        d, nm, nv = _adamw_math(w_ref[...], g_ref[...], m_ref[...], v_ref[...])
        d_ref[...] = d
        nm_ref[...] = nm
        nv_ref[...] = nv

    return pl.pallas_call(
        body, out_shape=tuple(SDS(w.shape, f32) for _ in range(3)), name=name)(w, m, v, g)


WEIGHTS = ("ffn1_norm", "ffn1_w_gate", "ffn1_w_up", "ffn1_w_down", "mix_norm", "w_in", "q_norm", "k_norm",
           "attn_sinks", "rel_bias", "pool_w", "pool_scale", "w_out", "ffn2_norm", "ffn2_w_gate", "ffn2_w_up",
           "ffn2_w_down")
BIG = (("ffn1_w_gate", True), ("ffn1_w_up", True), ("ffn1_w_down", False), ("w_in", True), ("w_out", False),
       ("ffn2_w_gate", True), ("ffn2_w_up", True), ("ffn2_w_down", False))


def kernel(x, ffn1_norm, ffn1_w_gate, ffn1_w_up, ffn1_w_down, mix_norm, w_in, q_norm, k_norm, attn_sinks, rel_bias, pool_w, pool_scale, w_out, ffn2_norm, ffn2_w_gate, ffn2_w_up, ffn2_w_down, loss_target, m_ffn1_norm, m_ffn1_w_gate, m_ffn1_w_up, m_ffn1_w_down, m_mix_norm, m_w_in, m_q_norm, m_k_norm, m_attn_sinks, m_rel_bias, m_pool_w, m_pool_scale, m_w_out, m_ffn2_norm, m_ffn2_w_gate, m_ffn2_w_up, m_ffn2_w_down, v_ffn1_norm, v_ffn1_w_gate, v_ffn1_w_up, v_ffn1_w_down, v_mix_norm, v_w_in, v_q_norm, v_k_norm, v_attn_sinks, v_rel_bias, v_pool_w, v_pool_scale, v_w_out, v_ffn2_norm, v_ffn2_w_gate, v_ffn2_w_up, v_ffn2_w_down):
    args = dict(locals())
    w = {n: args[n] for n in WEIGHTS}
    m = {n: args["m_" + n] for n in WEIGHTS}
    v = {n: args["v_" + n] for n in WEIGHTS}

    as_rows = lambda a, tr: jnp.swapaxes(a, 1, 2) if tr else a
    shard = jnp.concatenate([as_rows(w[n], tr)[0].astype(bf16) for n, tr in BIG], axis=0)
    exchanges = _GatheredWeights(shard)
    gx, (dw1, _, _, _), small = _local_step(
        x[0], loss_target[0], exchanges, ffn1_norm, mix_norm, ffn2_norm, q_norm, k_norm, attn_sinks,
        rel_bias, pool_w[0], pool_scale)

    red1 = _reduce_scatter_ffn1(dw1)
    red_rest = exchanges.mix_ffn2_grads_total(red1)
    small_tot = _all_reduce_small(_pack_small(small))

    grads, deltas, new_m, new_v = {}, {}, {}, {}
    big_out = _adamw_big(*[[as_rows(t[n], tr) for n, tr in BIG] for t in (w, m, v)], red1, red_rest)
    for k, (n, tr) in enumerate(BIG):
        grads[n], deltas[n], new_m[n], new_v[n] = [as_rows(o[k], tr) for o in big_out]
    small_names = [n for n in SMALL_NAMES if n != "loss"]
    ds, nms, nvs = _adamw_small(_pack_small({n: w[n] for n in small_names}), _pack_small({n: m[n] for n in small_names}),
                                _pack_small({n: v[n] for n in small_names}), small_tot, "adamw_small")
    for n in small_names:
        grads[n] = _unpack_small(small_tot, n)
        deltas[n], new_m[n], new_v[n] = _unpack_small(ds, n), _unpack_small(nms, n), _unpack_small(nvs, n)
    loss = small_tot[LOSS_ROW, 0]
    return (loss, gx[None], *[grads[n] for n in WEIGHTS], *[deltas[n] for n in WEIGHTS],
            *[new_m[n] for n in WEIGHTS], *[new_v[n] for n in WEIGHTS])
```

```python
import functools

import jax
import jax.numpy as jnp
import numpy as np
from jax import lax
from jax.experimental import pallas as pl
from jax.experimental.pallas import tpu as pltpu

f32, bf16, i32 = jnp.float32, jnp.bfloat16, jnp.int32
SDS = jax.ShapeDtypeStruct

D = 1024
F = 2816
HD = 64
NH = 8
NKV = 2
GQA = NH // NKV
DATTN = NH * HD
DKV = NKV * HD
DPOOL = 512
POOL_WINDOWS = (2, 4, 8, 16)
PGD = DPOOL // len(POOL_WINDOWS)
DIN = DATTN + 2 * DKV + DPOOL
DMIX = DATTN + DPOOL
BLK = 128
NBUCK = 32
MAX_DISTANCE = 128
EPS = 1e-6
NEG = -1e30
SCALE = HD ** -0.5

ADAM_LR, ADAM_B1, ADAM_B2, ADAM_EPS, ADAM_WD, ADAM_STEP = 0.001, 0.9, 0.999, 1e-08, 0.01, 10

NDEV = 8
FS = F // NDEV
INS = DIN // NDEV
OUTS = DMIX // NDEV
PIECE_ROWS = (FS, FS, FS, INS, OUTS, FS, FS, FS)
PIECE_OFF = tuple(int(v) for v in np.cumsum((0,) + PIECE_ROWS[:-1]))
PACK_ROWS = sum(PIECE_ROWS)

VMEM_LIMIT_V7X = 56 * 1024 * 1024

MESH = pl.DeviceIdType.MESH


def _cparams(sem=None, vmem=None):
    return pltpu.CompilerParams(dimension_semantics=sem, vmem_limit_bytes=vmem)


def _nt(a, b):
    return lax.dot_general(a, b, (((1,), (1,)), ((), ())), preferred_element_type=f32)


def _tn(a, b):
    return lax.dot_general(a, b, (((0,), (0,)), ((), ())), preferred_element_type=f32)


def _nn(a, b):
    return jnp.dot(a, b, preferred_element_type=f32)


def _sigmoid(x):
    return 1.0 / (1.0 + jnp.exp(-x))


def _norm_fwd(x, g, name):
    T = x.shape[0]
    tm = min(512, T)

    def body(x_ref, g_ref, h_ref):
        xv = x_ref[...]
        r = lax.rsqrt(jnp.mean(xv * xv, axis=-1, keepdims=True) + EPS)
        h_ref[...] = (xv * r * g_ref[...]).astype(bf16)

    return pl.pallas_call(
        body, grid=(T // tm,),
        in_specs=[pl.BlockSpec((tm, D), lambda i: (i, 0)), pl.BlockSpec((1, D), lambda i: (0, 0))],
        out_specs=pl.BlockSpec((tm, D), lambda i: (i, 0)),
        out_shape=SDS((T, D), bf16), name=name)(x, g)


def _norm_bwd(dh, x, g, dres, out_scale, name):
    T = x.shape[0]
    tm = min(512, T)

    def body(dh_ref, x_ref, g_ref, dr_ref, dx_ref, dxb_ref, dg_ref):
        i = pl.program_id(0)
        xv = x_ref[...]
        r = lax.rsqrt(jnp.mean(xv * xv, axis=-1, keepdims=True) + EPS)
        xh = xv * r
        dhv = dh_ref[...]
        dxh = dhv * g_ref[...]
        dx = dr_ref[...] + r * (dxh - xh * jnp.mean(dxh * xh, axis=-1, keepdims=True))
        dx_ref[...] = dx
        dxb_ref[...] = (out_scale * dx).astype(bf16)
        dg = jnp.sum(dhv * xh, axis=0, keepdims=True)

        @pl.when(i == 0)
        def _():
            dg_ref[...] = dg

        @pl.when(i > 0)
        def _():
            dg_ref[...] += dg

    tok = pl.BlockSpec((tm, D), lambda i: (i, 0))
    vec = pl.BlockSpec((1, D), lambda i: (0, 0))
    return pl.pallas_call(
        body, grid=(T // tm,),
        in_specs=[tok, tok, vec, tok], out_specs=[tok, tok, vec],
        out_shape=(SDS((T, D), f32), SDS((T, D), bf16), SDS((1, D), f32)),
        compiler_params=_cparams(("arbitrary",)), name=name)(dh, x, g, dres)


FFN_ROW_CHUNK = 256


def _ffn_tiles(T):
    return min(1024, T), 256


def _ffn_fwd(h, w, x, name):
    T = h.shape[0]
    tm, tf = _ffn_tiles(T)
    nf = F // tf

    def body(h_ref, w_ref, x_hbm, xo_ref, g_ref, u_ref, sem):
        fi = pl.program_id(0)

        @pl.when(fi == 0)
        def _():
            cp = pltpu.make_async_copy(x_hbm, xo_ref, sem)
            cp.start()
            cp.wait()

        wgu = w_ref[0:2].reshape(2 * tf, D)
        for r in range(0, T, tm):
            rows = slice(r, r + tm)
            gu = _nt(h_ref[rows, :], wgu)
            gate, up = gu[:, :tf], gu[:, tf:]
            act = gate * _sigmoid(gate) * up
            g_ref[0, rows, :] = gate.astype(bf16)
            u_ref[0, rows, :] = up.astype(bf16)
            xo_ref[rows, :] += _nn((0.5 * act).astype(bf16), w_ref[2])

    tok = pl.BlockSpec((T, D), lambda f: (0, 0))
    act_spec = pl.BlockSpec((1, T, tf), lambda f: (f, 0, 0))
    return pl.pallas_call(
        body, grid=(nf,),
        in_specs=[tok, pl.BlockSpec((3, tf, D), lambda f: (0, f, 0)), pl.BlockSpec(memory_space=pl.ANY)],
        out_specs=[tok, act_spec, act_spec],
        out_shape=(SDS((T, D), f32), SDS((nf, T, tf), bf16), SDS((nf, T, tf), bf16)),
        scratch_shapes=[pltpu.SemaphoreType.DMA],
        compiler_params=_cparams(("arbitrary",), VMEM_LIMIT_V7X), name=name)(h, w, x)


def _ffn_bwd(dob, h, gate, up, w, name):
    T = h.shape[0]
    _, tf = _ffn_tiles(T)
    nf = F // tf

    def body(do_hbm, h_hbm, g_ref, u_ref, w_ref, dh_hbm, dw_ref, do_v, h_v, dh_acc, dgu_s, act_s, sems):
        fi = pl.program_id(0)

        @pl.when(fi == 0)
        def _():
            loads = [pltpu.make_async_copy(do_hbm, do_v, sems.at[0]), pltpu.make_async_copy(h_hbm, h_v, sems.at[1])]
            for cp in loads:
                cp.start()
            dh_acc[...] = jnp.zeros_like(dh_acc)
            for cp in loads:
                cp.wait()

        wgu = w_ref[0:2].reshape(2 * tf, D)
        for r in range(0, T, FFN_ROW_CHUNK):
            rows = slice(r, r + FFN_ROW_CHUNK)
            dov = do_v[rows, :]
            gv = g_ref[0, rows, :].astype(f32)
            uv = u_ref[0, rows, :].astype(f32)
            sg = _sigmoid(gv)
            sil = gv * sg
            dact = _nt(dov, w_ref[2])
            dup = dact * sil
            dgate = dact * uv * (sg * (1.0 + gv * (1.0 - sg)))
            dgu = jnp.concatenate([dgate.astype(bf16), dup.astype(bf16)], axis=1)
            dgu_s[rows, :] = dgu
            act_s[rows, :] = (sil * uv).astype(bf16)
            dh_acc[rows, :] += _nn(dgu, wgu)
        dw_ref[0:2] = _tn(dgu_s[...], h_v[...]).reshape(2, tf, D).astype(bf16)
        dw_ref[2] = _tn(act_s[...], do_v[...]).astype(bf16)

        @pl.when(fi == nf - 1)
        def _():
            out = pltpu.make_async_copy(dh_acc, dh_hbm, sems.at[0])
            out.start()
            out.wait()

    act_spec = pl.BlockSpec((1, T, tf), lambda f: (f, 0, 0))
    wspec = pl.BlockSpec((3, tf, D), lambda f: (0, f, 0))
    hbm = pl.BlockSpec(memory_space=pl.ANY)
    return pl.pallas_call(
        body, grid=(nf,),
        in_specs=[hbm, hbm, act_spec, act_spec, wspec],
        out_specs=[hbm, wspec],
        out_shape=(SDS((T, D), f32), SDS((3, F, D), bf16)),
        scratch_shapes=[pltpu.VMEM((T, D), bf16), pltpu.VMEM((T, D), bf16), pltpu.VMEM((T, D), f32),
                        pltpu.VMEM((T, 2 * tf), bf16), pltpu.VMEM((T, tf), bf16), pltpu.SemaphoreType.DMA((2,))],
        compiler_params=_cparams(("arbitrary",), VMEM_LIMIT_V7X), name=name)(dob, h, gate, up, w)


def _loss_grad(y, target, name):
    T = y.shape[0]
    tm = min(512, T)

    def body(y_ref, t_ref, dy_ref, dyb_ref, l_ref):
        i = pl.program_id(0)
        e = y_ref[...] - t_ref[...]
        dy = e * (1.0 / D)
        dy_ref[...] = dy
        dyb_ref[...] = (0.5 * dy).astype(bf16)
        col = jnp.sum(e * e, axis=0, keepdims=True) * (0.5 / D)
        lanes = col[:, 0:128]
        for k in range(1, D // 128):
            lanes = lanes + col[:, 128 * k:128 * (k + 1)]

        @pl.when(i == 0)
        def _():
            l_ref[...] = lanes

        @pl.when(i > 0)
        def _():
            l_ref[...] += lanes

    tok = pl.BlockSpec((tm, D), lambda i: (i, 0))
    return pl.pallas_call(
        body, grid=(T // tm,), in_specs=[tok, tok],
        out_specs=[tok, tok, pl.BlockSpec((1, 128), lambda i: (0, 0))],
        out_shape=(SDS((T, D), f32), SDS((T, D), bf16), SDS((1, 128), f32)),
        compiler_params=_cparams(("arbitrary",)), name=name)(y, target)


def _in_proj_fwd(h, wint, name):
    T = h.shape[0]
    tm = min(512, T)

    def body(h_ref, w_ref, z_ref):
        z_ref[...] = _nt(h_ref[...], w_ref[...])

    return pl.pallas_call(
        body, grid=(T // tm,),
        in_specs=[pl.BlockSpec((tm, D), lambda i: (i, 0)), pl.BlockSpec((DIN, D), lambda i: (0, 0))],
        out_specs=pl.BlockSpec((tm, DIN), lambda i: (i, 0)),
        out_shape=SDS((T, DIN), f32), name=name)(h, wint)


def _in_proj_bwd(dz, wint, h, name):
    T = h.shape[0]
    tm = min(512, T)
    nt = T // tm

    def body(dz_ref, w_ref, h_ref, dh_ref, dw_ref, acc):
        i = pl.program_id(0)
        dzb = dz_ref[...].astype(bf16)
        dh_ref[...] = _nn(dzb, w_ref[...])
        part = _tn(dzb, h_ref[...])

        @pl.when(i == 0)
        def _():
            acc[...] = part

        @pl.when(i > 0)
        def _():
            acc[...] += part

        @pl.when(i == nt - 1)
        def _():
            dw_ref[...] = acc[...].astype(bf16)

    wspec = pl.BlockSpec((DIN, D), lambda i: (0, 0))
    return pl.pallas_call(
        body, grid=(nt,),
        in_specs=[pl.BlockSpec((tm, DIN), lambda i: (i, 0)), wspec, pl.BlockSpec((tm, D), lambda i: (i, 0))],
        out_specs=[pl.BlockSpec((tm, D), lambda i: (i, 0)), wspec],
        out_shape=(SDS((T, D), f32), SDS((DIN, D), bf16)),
        scratch_shapes=[pltpu.VMEM((DIN, D), f32)],
        compiler_params=_cparams(("arbitrary",)), name=name)(dz, wint, h)


def _out_proj_fwd(ymix, wout, x, name):
    T = x.shape[0]
    tm = min(512, T)

    def body(y_ref, w_ref, x_ref, o_ref):
        o_ref[...] = x_ref[...] + _nn(y_ref[...], w_ref[...])

    tok = pl.BlockSpec((tm, D), lambda i: (i, 0))
    return pl.pallas_call(
        body, grid=(T // tm,),
        in_specs=[pl.BlockSpec((tm, DMIX), lambda i: (i, 0)), pl.BlockSpec((DMIX, D), lambda i: (0, 0)), tok],
        out_specs=tok, out_shape=SDS((T, D), f32), name=name)(ymix, wout, x)


def _out_proj_bwd(dxb, wout, ymix, name):
    T = dxb.shape[0]
    tm = min(512, T)
    nt = T // tm

    def body(dx_ref, w_ref, y_ref, dy_ref, dw_ref, acc):
        i = pl.program_id(0)
        dxv = dx_ref[...]
        dy_ref[...] = _nt(dxv, w_ref[...])
        part = _tn(y_ref[...], dxv)

        @pl.when(i == 0)
        def _():
            acc[...] = part

        @pl.when(i > 0)
        def _():
            acc[...] += part

        @pl.when(i == nt - 1)
        def _():
            dw_ref[...] = acc[...].astype(bf16)

    wspec = pl.BlockSpec((DMIX, D), lambda i: (0, 0))
    return pl.pallas_call(
        body, grid=(nt,),
        in_specs=[pl.BlockSpec((tm, D), lambda i: (i, 0)), wspec, pl.BlockSpec((tm, DMIX), lambda i: (i, 0))],
        out_specs=[pl.BlockSpec((tm, DMIX), lambda i: (i, 0)), wspec],
        out_shape=(SDS((T, DMIX), f32), SDS((DMIX, D), bf16)),
        scratch_shapes=[pltpu.VMEM((DMIX, D), f32)],
        compiler_params=_cparams(("arbitrary",)), name=name)(dxb, wout, ymix)


def _t5_bucket_table():
    ql = np.arange(BLK)[:, None]
    kl = np.arange(2 * BLK)[None, :]
    n = np.maximum(ql + BLK - kl, 0)
    max_exact = NBUCK // 2
    large = max_exact + (np.log(np.maximum(n, 1) / max_exact) / np.log(MAX_DISTANCE / max_exact)
                         * (NBUCK - max_exact)).astype(np.int32)
    large = np.minimum(large, NBUCK - 1)
    return np.where(n < max_exact, n, large).astype(np.int32)


def _fill_bias(bk_ref, rb_ref, bias_scr):
    bk = bk_ref[...]
    for h in range(NH):
        def step(b, acc, h=h):
            return acc + jnp.where(bk == b, rb_ref[b, h], 0.0)
        bias_scr[h] = lax.fori_loop(0, NBUCK, step, jnp.zeros((BLK, 2 * BLK), f32))


MIX_SUB = 2


class _Window:
    def __init__(self, zc_ref, zp_ref, n, s):
        self.blk = n * MIX_SUB + s
        self.first_in_step = s == 0
        self.cur = lambda a, b: zc_ref[s * BLK:(s + 1) * BLK, a:b]
        self.prev = (lambda a, b: zp_ref[:, a:b]) if s == 0 else (lambda a, b: zc_ref[(s - 1) * BLK:s * BLK, a:b])


def _attn_probs(win, kh, qg, kg, sk_ref, bias_scr):
    n = win.blk
    kc = DATTN + HD * kh
    vc = DATTN + DKV + HD * kh
    kx = jnp.concatenate([win.prev(kc, kc + HD), win.cur(kc, kc + HD)], axis=0)
    vx = jnp.concatenate([win.prev(vc, vc + HD), win.cur(vc, vc + HD)], axis=0)
    qx = jnp.concatenate([win.cur(HD * (GQA * kh + g), HD * (GQA * kh + g + 1)) for g in range(GQA)], axis=0)
    rq = lax.rsqrt(jnp.mean(qx * qx, axis=-1, keepdims=True) + EPS)
    rk = lax.rsqrt(jnp.mean(kx * kx, axis=-1, keepdims=True) + EPS)
    qhat, khat = qx * rq, kx * rk
    qnb, knb = (qhat * qg).astype(bf16), (khat * kg).astype(bf16)
    s = _nt(qnb, knb) * SCALE + bias_scr[GQA * kh:GQA * (kh + 1)].reshape(GQA * BLK, 2 * BLK)
    row = lax.broadcasted_iota(i32, (GQA * BLK, 2 * BLK), 0) & (BLK - 1)
    col = lax.broadcasted_iota(i32, (GQA * BLK, 2 * BLK), 1)
    mask = (col > row) & (col <= row + BLK) & ((col >= BLK) | (n > 0))
    s = jnp.where(mask, s, NEG)
    ridx = lax.broadcasted_iota(i32, (GQA * BLK, 1), 0)
    sink = jnp.full((GQA * BLK, 1), sk_ref[GQA * kh + GQA - 1], f32)
    for g in range(GQA - 2, -1, -1):
        sink = jnp.where(ridx < (g + 1) * BLK, sk_ref[GQA * kh + g], sink)
    m = jnp.maximum(jnp.max(s, axis=-1, keepdims=True), sink)
    e = jnp.exp(s - m)
    es = jnp.exp(sink - m)
    den = jnp.sum(e, axis=-1, keepdims=True) + es
    return dict(p=e / den, psink=es / den, qhat=qhat, khat=khat, rq=rq, rk=rk, qnb=qnb, knb=knb, vb=vx.astype(bf16))


def _pool_group(win, g, w):
    n = win.blk
    c0 = DATTN + 2 * DKV + PGD * g
    uc = win.cur(c0, c0 + PGD)
    up = jnp.where(n > 0, win.prev(c0, c0 + PGD), 0.0)
    ue = jnp.concatenate([up, uc], axis=0)
    hi = ue.astype(bf16)
    lo = (ue - hi.astype(f32)).astype(bf16)
    t = lax.broadcasted_iota(i32, (BLK, 2 * BLK), 0)
    s = lax.broadcasted_iota(i32, (BLK, 2 * BLK), 1)
    band = jnp.where((s <= t + BLK) & (s > t + BLK - w), 1.0, 0.0).astype(bf16)
    sm = _nn(band, hi) + _nn(band, lo)
    pos = n * BLK + lax.broadcasted_iota(i32, (BLK, 1), 0) + 1
    cnt = jnp.minimum(pos, w).astype(f32)
    return sm / cnt - uc, band, cnt


def _mix_fwd(z, qg, kg, sinks, relb, bucket, pool_w, pscale, name):
    T = z.shape[0]
    step_rows = MIX_SUB * BLK
    nsteps = T // step_rows

    def body(zc_ref, zp_ref, qg_ref, kg_ref, sk_ref, rb_ref, bk_ref, pw_ref, ps_ref, y_ref, bias_scr, yacc):
        n = pl.program_id(0)

        @pl.when(n == 0)
        def _():
            _fill_bias(bk_ref, rb_ref, bias_scr)

        for s in range(MIX_SUB):
            win = _Window(zc_ref, zp_ref, n, s)
            rows = slice(s * BLK, (s + 1) * BLK)
            for kh in range(NKV):
                a = _attn_probs(win, kh, qg_ref[...], kg_ref[...], sk_ref, bias_scr)
                o = _nn(a["p"].astype(bf16), a["vb"])
                for g in range(GQA):
                    hc = HD * (GQA * kh + g)
                    yacc[rows, hc:hc + HD] = o[g * BLK:(g + 1) * BLK]
            for g, w in enumerate(POOL_WINDOWS):
                pooled, _, _ = _pool_group(win, g, w)
                yp = _nn(pooled.astype(bf16), pw_ref[g].astype(bf16)) * ps_ref[:, PGD * g:PGD * (g + 1)]
                yacc[rows, DATTN + PGD * g:DATTN + PGD * (g + 1)] = yp
        y_ref[...] = yacc[...].astype(bf16)

    full = lambda *shape: pl.BlockSpec(shape, lambda n: (0,) * len(shape))
    smem = pl.BlockSpec(memory_space=pltpu.SMEM)
    return pl.pallas_call(
        body, grid=(nsteps,),
        in_specs=[pl.BlockSpec((step_rows, DIN), lambda n: (n, 0)),
                  pl.BlockSpec((BLK, DIN), lambda n: (jnp.maximum(n * MIX_SUB - 1, 0), 0)),
                  full(1, HD), full(1, HD), smem, smem, full(BLK, 2 * BLK),
                  full(len(POOL_WINDOWS), PGD, PGD), full(1, DPOOL)],
        out_specs=pl.BlockSpec((step_rows, DMIX), lambda n: (n, 0)),
        out_shape=SDS((T, DMIX), bf16),
        scratch_shapes=[pltpu.VMEM((NH, BLK, 2 * BLK), f32), pltpu.VMEM((step_rows, DMIX), f32)],
        compiler_params=_cparams(("arbitrary",)), name=name)(z, z, qg, kg, sinks, relb, bucket, pool_w, pscale)


def _mix_bwd(z, dy, qg, kg, sinks, relb, bucket, pool_w, pscale, name):
    T = z.shape[0]
    step_rows = MIX_SUB * BLK
    nsteps = T // step_rows

    def body(zc_ref, zp_ref, dy_ref, qg_ref, kg_ref, sk_ref, rb_ref, bk_ref, pw_ref, ps_ref,
             dz_ref, dqg_ref, dkg_ref, dsk_ref, drb_ref, dpw_ref, dps_ref, bias_scr, dbias_scr):
        n = pl.program_id(0)

        @pl.when(n == 0)
        def _():
            _fill_bias(bk_ref, rb_ref, bias_scr)
            dbias_scr[...] = jnp.zeros_like(dbias_scr)
            dqg_ref[...] = jnp.zeros_like(dqg_ref)
            dkg_ref[...] = jnp.zeros_like(dkg_ref)
            dsk_ref[...] = jnp.zeros_like(dsk_ref)
            dpw_ref[...] = jnp.zeros_like(dpw_ref)
            dps_ref[...] = jnp.zeros_like(dps_ref)

        qg, kg = qg_ref[...], kg_ref[...]
        lane = lax.broadcasted_iota(i32, (1, 128), 1)
        dsk = jnp.zeros((1, 128), f32)
        for s in range(MIX_SUB):
            win = _Window(zc_ref, zp_ref, n, s)
            blk = win.blk
            rows = pl.ds(pl.multiple_of(blk * BLK, BLK), BLK)
            prow = pl.ds(pl.multiple_of(jnp.maximum(blk - 1, 0) * BLK, BLK), BLK)
            dyr = slice(s * BLK, (s + 1) * BLK)

            def into_prev(fn, s=s):
                if s == 0:
                    pl.when(n > 0)(fn)
                else:
                    fn()

            for kh in range(NKV):
                a = _attn_probs(win, kh, qg, kg, sk_ref, bias_scr)
                p = a["p"]
                do = jnp.concatenate([dy_ref[dyr, HD * (GQA * kh + g):HD * (GQA * kh + g + 1)] for g in range(GQA)],
                                     axis=0).astype(bf16)
                dv = _tn(p.astype(bf16), do)
                dp = _nt(do, a["vb"])
                delta = jnp.sum(p * dp, axis=-1, keepdims=True)
                ds = p * (dp - delta)
                sinkterm = a["psink"] * delta
                for g in range(GQA):
                    h = GQA * kh + g
                    dbias_scr[h] += ds[g * BLK:(g + 1) * BLK]
                    tot = jnp.sum(sinkterm[g * BLK:(g + 1) * BLK], axis=0, keepdims=True)
                    dsk = dsk - jnp.where(lane == h, tot, 0.0)
                dsb = ds.astype(bf16)
                dqn = _nn(dsb, a["knb"]) * SCALE
                dkn = _tn(dsb, a["qnb"]) * SCALE
                qhat, khat = a["qhat"], a["khat"]
                dqg_ref[...] += jnp.sum(dqn * qhat, axis=0, keepdims=True)
                dkg_ref[...] += jnp.sum(dkn * khat, axis=0, keepdims=True)
                dqh = dqn * qg
                dq = a["rq"] * (dqh - qhat * jnp.mean(dqh * qhat, axis=-1, keepdims=True))
                dkh = dkn * kg
                dk = a["rk"] * (dkh - khat * jnp.mean(dkh * khat, axis=-1, keepdims=True))
                kc = DATTN + HD * kh
                vc = DATTN + DKV + HD * kh
                for g in range(GQA):
                    hc = HD * (GQA * kh + g)
                    dz_ref[rows, hc:hc + HD] = dq[g * BLK:(g + 1) * BLK]
                dz_ref[rows, kc:kc + HD] = dk[BLK:2 * BLK]
                dz_ref[rows, vc:vc + HD] = dv[BLK:2 * BLK]

                def kv_prev(dk=dk, dv=dv, kc=kc, vc=vc, prow=prow):
                    dz_ref[prow, kc:kc + HD] += dk[0:BLK]
                    dz_ref[prow, vc:vc + HD] += dv[0:BLK]

                into_prev(kv_prev)

            for g, w in enumerate(POOL_WINDOWS):
                c0 = DATTN + 2 * DKV + PGD * g
                pooled, band, cnt = _pool_group(win, g, w)
                pb = pooled.astype(bf16)
                wb = pw_ref[g].astype(bf16)
                dyp = dy_ref[dyr, DATTN + PGD * g:DATTN + PGD * (g + 1)]
                ypre = _nn(pb, wb)
                dps_ref[:, PGD * g:PGD * (g + 1)] += jnp.sum(dyp * ypre, axis=0, keepdims=True)
                dyg = (dyp * ps_ref[:, PGD * g:PGD * (g + 1)]).astype(bf16)
                dpw_ref[g] += _tn(pb, dyg)
                dpooled = _nt(dyg, wb)
                dsm = dpooled / cnt
                hi = dsm.astype(bf16)
                lo = (dsm - hi.astype(f32)).astype(bf16)
                due = _tn(band, hi) + _tn(band, lo)
                dz_ref[rows, c0:c0 + PGD] = due[BLK:2 * BLK] - dpooled

                def pool_prev(due=due, c0=c0, prow=prow):
                    dz_ref[prow, c0:c0 + PGD] += due[0:BLK]

                into_prev(pool_prev)

        dsk_ref[...] += dsk

        @pl.when(n == nsteps - 1)
        def _():
            bk = bk_ref[...]
            ri = lax.broadcasted_iota(i32, (NBUCK, NH), 0)
            ci = lax.broadcasted_iota(i32, (NBUCK, NH), 1)

            def step(b, acc):
                for h in range(NH):
                    sel = jnp.where(bk == b, dbias_scr[h], 0.0)
                    tot = jnp.sum(jnp.sum(sel, axis=1, keepdims=True), axis=0, keepdims=True)
                    acc = acc + jnp.where((ri == b) & (ci == h), tot, 0.0)
                return acc

            drb_ref[...] = lax.fori_loop(0, NBUCK, step, jnp.zeros((NBUCK, NH), f32))

    full = lambda *shape: pl.BlockSpec(shape, lambda n: (0,) * len(shape))
    smem = pl.BlockSpec(memory_space=pltpu.SMEM)
    npg = len(POOL_WINDOWS)
    return pl.pallas_call(
        body, grid=(nsteps,),
        in_specs=[pl.BlockSpec((step_rows, DIN), lambda n: (n, 0)),
                  pl.BlockSpec((BLK, DIN), lambda n: (jnp.maximum(n * MIX_SUB - 1, 0), 0)),
                  pl.BlockSpec((step_rows, DMIX), lambda n: (n, 0)),
                  full(1, HD), full(1, HD), smem, smem, full(BLK, 2 * BLK), full(npg, PGD, PGD), full(1, DPOOL)],
        out_specs=[full(T, DIN), full(1, HD), full(1, HD), full(1, 128), full(NBUCK, NH),
                   full(npg, PGD, PGD), full(1, DPOOL)],
        out_shape=(SDS((T, DIN), f32), SDS((1, HD), f32), SDS((1, HD), f32), SDS((1, 128), f32),
                   SDS((NBUCK, NH), f32), SDS((npg, PGD, PGD), f32), SDS((1, DPOOL), f32)),
        scratch_shapes=[pltpu.VMEM((NH, BLK, 2 * BLK), f32), pltpu.VMEM((NH, BLK, 2 * BLK), f32)],
        compiler_params=_cparams(("arbitrary",), VMEM_LIMIT_V7X),
        name=name)(z, z, dy, qg, kg, sinks, relb, bucket, pool_w, pscale)


class _LocalWeights:
    def __init__(self, w1, wint, wout, w2):
        self.w1, self.wint, self.wout, self.w2 = w1, wint, wout, w2

    def ffn1(self):
        return self.w1

    def mix(self, after):
        return self.wint, self.wout

    def before_out_proj(self, wout, after):
        return wout

    def ffn2(self, after):
        return self.w2

    def mix_ffn2_grads_ready(self, dwint, dwout, dw2, dh2):
        self.grads_rest = (dwint, dwout, dw2)
        return dh2

    def before_ffn1_bwd(self, dx1b):
        return dx1b


def _local_step(x, target, weights, g1, gm, g3, qg, kg, sinks, relb, pool_w, pscale):
    bucket = jnp.asarray(_t5_bucket_table())
    sk = sinks.reshape(NH)
    w1 = weights.ffn1()
    h1 = _norm_fwd(x, g1, "norm1_fwd")
    x1, gate1, up1 = _ffn_fwd(h1, w1, x, "ffn1_fwd")
    h2 = _norm_fwd(x1, gm, "norm2_fwd")
    wint, wout = weights.mix(h2)
    z = _in_proj_fwd(h2, wint, "in_proj_fwd")
    ymix = _mix_fwd(z, qg, kg, sk, relb, bucket, pool_w, pscale, "mix_fwd")
    wout = weights.before_out_proj(wout, ymix)
    x2 = _out_proj_fwd(ymix, wout, x1, "out_proj_fwd")
    h3 = _norm_fwd(x2, g3, "norm3_fwd")
    w2 = weights.ffn2(h3)
    y, gate2, up2 = _ffn_fwd(h3, w2, x2, "ffn2_fwd")
    dy, dyb, loss_lanes = _loss_grad(y, target, "loss_grad")

    dh3, dw2 = _ffn_bwd(dyb, h3, gate2, up2, w2, "ffn2_bwd")
    dx2, dx2b, dg3 = _norm_bwd(dh3, x2, g3, dy, 1.0, "norm3_bwd")
    dymix, dwout = _out_proj_bwd(dx2b, wout, ymix, "out_proj_bwd")
    dz, dqg, dkg, dsk, drb, dpw, dps = _mix_bwd(z, dymix, qg, kg, sk, relb, bucket, pool_w, pscale, "mix_bwd")
    dh2, dwint = _in_proj_bwd(dz, wint, h2, "in_proj_bwd")
    dh2 = weights.mix_ffn2_grads_ready(dwint, dwout, dw2, dh2)
    dx1, dx1b, dgm = _norm_bwd(dh2, x1, gm, dx2, 0.5, "norm2_bwd")
    dx1b = weights.before_ffn1_bwd(dx1b)
    dh1, dw1 = _ffn_bwd(dx1b, h1, gate1, up1, w1, "ffn1_bwd")
    gx, _, dg1 = _norm_bwd(dh1, x, g1, dx1, 1.0, "norm1_bwd")
    small = dict(ffn1_norm=dg1, mix_norm=dgm, ffn2_norm=dg3, pool_scale=dps, q_norm=dqg, k_norm=dkg,
                 attn_sinks=dsk[:, :NH], rel_bias=drb, pool_w=dpw, loss=loss_lanes)
    return gx, (dw1, dwint, dwout, dw2), small


SMALL_NAMES = ("ffn1_norm", "mix_norm", "ffn2_norm", "pool_scale", "q_norm", "k_norm", "attn_sinks", "rel_bias",
               "pool_w", "loss")
SMALL_SHAPES = dict(ffn1_norm=(1, D), mix_norm=(1, D), ffn2_norm=(1, D), pool_scale=(1, DPOOL), q_norm=(1, HD),
                    k_norm=(1, HD), attn_sinks=(1, NH), rel_bias=(NBUCK, NH),
                    pool_w=(1, len(POOL_WINDOWS), PGD, PGD), loss=(1, 128))


def _small_rows(name):
    return -(-int(np.prod(SMALL_SHAPES[name])) // 128)


SMALL_OFF = {}
_r = 0
for _n in SMALL_NAMES:
    SMALL_OFF[_n] = _r
    _r += _small_rows(_n)
SMALL_ROWS = -(-_r // 8) * 8
LOSS_ROW = SMALL_OFF["loss"]


def _pack_small(vals):
    parts = []
    for n in SMALL_NAMES:
        size = _small_rows(n) * 128
        if n in vals:
            flat = vals[n].astype(f32).reshape(-1)
            parts.append(jnp.pad(flat, (0, size - flat.shape[0])))
        else:
            parts.append(jnp.zeros((size,), f32))
    flat = jnp.concatenate(parts)
    flat = jnp.pad(flat, (0, SMALL_ROWS * 128 - flat.shape[0]))
    return flat.reshape(SMALL_ROWS, 128)


def _unpack_small(packed, name):
    size = int(np.prod(SMALL_SHAPES[name]))
    r0 = SMALL_OFF[name]
    return packed[r0:r0 + _small_rows(name)].reshape(-1)[:size].reshape(SMALL_SHAPES[name])


def _position():
    return lax.axis_index("x"), lax.axis_index("y"), lax.axis_index("c")


def _dev_index(x, y, c):
    return 4 * x + 2 * y + c


G1_PIECES, MIX_PIECES, F2_PIECES = (0, 1, 2), (3, 4), (5, 6, 7)


def _group_rows(pieces):
    return sum(PIECE_ROWS[k] for k in pieces)


def _shard_piece(s_ref, k):
    return s_ref.at[pl.ds(PIECE_OFF[k], PIECE_ROWS[k]), :]


def _shard_group(s_ref, pieces):
    return s_ref.at[pl.ds(PIECE_OFF[pieces[0]], _group_rows(pieces)), :]


def _weight_pieces(w1_ref=None, wi_ref=None, wo_ref=None, w2_ref=None):
    arrs = {}
    if w1_ref is not None:
        arrs.update({0: w1_ref.at[0], 1: w1_ref.at[1], 2: w1_ref.at[2]})
    if wi_ref is not None:
        arrs[3] = wi_ref
    if wo_ref is not None:
        arrs[4] = wo_ref
    if w2_ref is not None:
        arrs.update({5: w2_ref.at[0], 6: w2_ref.at[1], 7: w2_ref.at[2]})
    return arrs


def _block_rows(arrs, k, dev):
    r = PIECE_ROWS[k]
    return arrs[k].at[pl.ds(pl.multiple_of(_dev_index(*dev) * r, 16), r), :]


def _all_gather_ffn1(shard):
    pieces = G1_PIECES

    def body(s_ref, w1_ref, send_sems, recv_sems, local_sem):
        x, y, c = _position()
        me, sib = (x, y, c), (x, y, 1 - c)
        chips = [(1 - x, y), (x, 1 - y), (1 - x, 1 - y)]
        arrs = _weight_pieces(w1_ref=w1_ref)

        def copies(rel, block, to, from_shard):
            return [pltpu.make_async_remote_copy(
                src_ref=_shard_piece(s_ref, k) if from_shard else _block_rows(arrs, k, block),
                dst_ref=_block_rows(arrs, k, block),
                send_sem=send_sems.at[rel], recv_sem=recv_sems.at[rel], device_id=to, device_id_type=MESH)
                for k in pieces]

        def whole(rel):
            grp = _shard_group(s_ref, pieces)
            return pltpu.make_async_remote_copy(src_ref=grp, dst_ref=grp, send_sem=send_sems.at[rel],
                                                recv_sem=recv_sems.at[rel], device_id=me, device_id_type=MESH)

        mine = [pltpu.make_async_copy(_shard_piece(s_ref, k), _block_rows(arrs, k, me), local_sem) for k in pieces]
        for cp in mine:
            cp.start()
        for cp in copies(0, me, sib, True):
            cp.start()
        for j, chip in enumerate(chips):
            for cp in copies(1 + j, me, (*chip, c), True):
                cp.start()
        for j, chip in enumerate(chips):
            whole(1 + j).wait_recv()
            for cp in copies(4 + j, (*chip, c), sib, False):
                cp.start()
        whole(0).wait_recv()
        for j in range(3):
            whole(4 + j).wait_recv()
        for rel in range(7):
            whole(rel).wait_send()
        grp = _shard_group(s_ref, pieces)
        pltpu.make_async_copy(grp, grp, local_sem).wait()

    hbm = pl.BlockSpec(memory_space=pl.ANY)
    return pl.pallas_call(
        body, in_specs=[hbm], out_specs=hbm, out_shape=SDS((3, F, D), bf16),
        scratch_shapes=[pltpu.SemaphoreType.DMA((7,)), pltpu.SemaphoreType.DMA((7,)), pltpu.SemaphoreType.DMA],
        compiler_params=pltpu.CompilerParams(has_side_effects=True),
        name="all_gather_ffn1")(shard)


HBM_SPEC = pl.BlockSpec(memory_space=pltpu.HBM)
SEM_SPEC = pl.BlockSpec(memory_space=pltpu.SEMAPHORE)
ANY_SPEC = pl.BlockSpec(memory_space=pl.ANY)
SPLIT_EFFECT = pltpu.SideEffectType.DATAFLOW_SIDE_EFFECTING


def _in_hbm(a):
    return pltpu.with_memory_space_constraint(a, pltpu.HBM)


def _hbm_like(a):
    return pltpu.HBM(a.shape, a.dtype)


def _place_own_rows(shard):
    pieces = MIX_PIECES + F2_PIECES

    def body(s_ref, wi_ref, wo_ref, w2_ref, buf, sems):
        x, y, c = _position()
        arrs = _weight_pieces(wi_ref=wi_ref, wo_ref=wo_ref, w2_ref=w2_ref)
        grp = _shard_group(s_ref, pieces)
        load = pltpu.make_async_copy(grp, buf, sems.at[0])
        load.start()
        load.wait()
        base = PIECE_OFF[pieces[0]]
        for k in pieces:
            pltpu.make_async_copy(buf.at[pl.ds(PIECE_OFF[k] - base, PIECE_ROWS[k]), :],
                                  _block_rows(arrs, k, (x, y, c)), sems.at[1]).start()
        pltpu.make_async_copy(grp, buf, sems.at[1]).wait()

    return pl.pallas_call(
        body, in_specs=[ANY_SPEC], out_specs=[ANY_SPEC] * 3,
        out_shape=(SDS((DIN, D), bf16), SDS((DMIX, D), bf16), SDS((3, F, D), bf16)),
        scratch_shapes=[pltpu.VMEM((_group_rows(pieces), D), bf16), pltpu.SemaphoreType.DMA((2,))],
        name="place_own_rows")(shard)


def _xor_peer(x, y, c, k):
    return (x ^ (k >> 2), y ^ ((k >> 1) & 1), c ^ (k & 1))


def _gather_rest_start(shard, wi, wo, w2, w1):
    def body(s_ref, wi_ref, wo_ref, w2_ref, w1_ref,
             ssem_m, rsem_m, ssem_f, rsem_f0, rsem_f, s_o, wi_o, wo_o, w2_o, w1_o):
        x, y, c = _position()
        me, sib = (x, y, c), (x, y, 1 - c)
        chips = [(1 - x, y), (x, 1 - y), (1 - x, 1 - y)]
        arrs = _weight_pieces(wi_ref=wi_ref, wo_ref=wo_ref, w2_ref=w2_ref)
        for k in range(1, NDEV):
            for p in MIX_PIECES:
                pltpu.make_async_remote_copy(
                    src_ref=_shard_piece(s_ref, p), dst_ref=_block_rows(arrs, p, me), send_sem=ssem_m.at[k - 1],
                    recv_sem=rsem_m.at[k - 1], device_id=_xor_peer(x, y, c, k), device_id_type=MESH).start()
        for p in F2_PIECES:
            pltpu.make_async_remote_copy(
                src_ref=_shard_piece(s_ref, p), dst_ref=_block_rows(arrs, p, me), send_sem=ssem_f.at[0],
                recv_sem=rsem_f0, device_id=sib, device_id_type=MESH).start()
        for j, chip in enumerate(chips):
            for p in F2_PIECES:
                pltpu.make_async_remote_copy(
                    src_ref=_shard_piece(s_ref, p), dst_ref=_block_rows(arrs, p, me), send_sem=ssem_f.at[1 + j],
                    recv_sem=rsem_f.at[j], device_id=(*chip, c), device_id_type=MESH).start()

    dma = pltpu.SemaphoreType.DMA
    return pl.pallas_call(
        body, name="gather_rest_start",
        out_shape=(dma((7,)), dma((7,)), dma((4,)), dma(()), dma((3,)),
                   _hbm_like(shard), _hbm_like(wi), _hbm_like(wo), _hbm_like(w2), _hbm_like(w1)),
        in_specs=(HBM_SPEC,) * 5, out_specs=(SEM_SPEC,) * 5 + (HBM_SPEC,) * 5,
        input_output_aliases={0: 5, 1: 6, 2: 7, 3: 8, 4: 9},
        compiler_params=pltpu.CompilerParams(has_side_effects=SPLIT_EFFECT),
    )(_in_hbm(shard), _in_hbm(wi), _in_hbm(wo), _in_hbm(w2), _in_hbm(w1))


def _gather_mix_wait(ssem_m, rsem_m, shard, wi, wo, after):
    def body(s_ref, wi_ref, wo_ref, ssem, rsem, after_ref, s_o, wi_o, wo_o):
        x, y, c = _position()
        grp = _shard_group(s_ref, MIX_PIECES)
        for k in range(NDEV - 1):
            d = pltpu.make_async_remote_copy(src_ref=grp, dst_ref=grp, send_sem=ssem.at[k], recv_sem=rsem.at[k],
                                             device_id=(x, y, c), device_id_type=MESH)
            d.wait_recv()
            d.wait_send()

    return pl.pallas_call(
        body, name="gather_mix_wait", out_shape=(_hbm_like(shard), _hbm_like(wi), _hbm_like(wo)),
        in_specs=(HBM_SPEC, HBM_SPEC, HBM_SPEC, SEM_SPEC, SEM_SPEC, ANY_SPEC), out_specs=(HBM_SPEC,) * 3,
        input_output_aliases={0: 0, 1: 1, 2: 2},
        compiler_params=pltpu.CompilerParams(has_side_effects=SPLIT_EFFECT),
    )(shard, wi, wo, ssem_m, rsem_m, after)


def _gather_ffn2_pass_on(rsem_f, w2, wo, after):
    def body(w2_ref, wo_ref, rsem, after_ref, fsend, frecv, w2_o, wo_o):
        x, y, c = _position()
        sib = (x, y, 1 - c)
        chips = [(1 - x, y), (x, 1 - y), (1 - x, 1 - y)]
        arrs = _weight_pieces(w2_ref=w2_ref)
        three = w2_ref.at[0, pl.ds(0, _group_rows(F2_PIECES)), :]
        for j, chip in enumerate(chips):
            pltpu.make_async_remote_copy(src_ref=three, dst_ref=three, send_sem=fsend.at[j], recv_sem=rsem.at[j],
                                         device_id=(x, y, c), device_id_type=MESH).wait_recv()
            for p in F2_PIECES:
                rows = _block_rows(arrs, p, (*chip, c))
                pltpu.make_async_remote_copy(src_ref=rows, dst_ref=rows, send_sem=fsend.at[j], recv_sem=frecv.at[j],
                                             device_id=sib, device_id_type=MESH).start()

    dma = pltpu.SemaphoreType.DMA
    return pl.pallas_call(
        body, name="gather_ffn2_pass_on", out_shape=(dma((3,)), dma((3,)), _hbm_like(w2), _hbm_like(wo)),
        in_specs=(HBM_SPEC, HBM_SPEC, SEM_SPEC, ANY_SPEC), out_specs=(SEM_SPEC, SEM_SPEC, HBM_SPEC, HBM_SPEC),
        input_output_aliases={0: 2, 1: 3},
        compiler_params=pltpu.CompilerParams(has_side_effects=SPLIT_EFFECT),
    )(w2, wo, rsem_f, after)


def _gather_ffn2_wait(ssem_f, rsem_f0, fsend, frecv, shard, w2, after):
    def body(s_ref, w2_ref, ssem, rsem0, fs, fr, after_ref, w2_o):
        x, y, c = _position()
        grp = _shard_group(s_ref, F2_PIECES)

        def waiter(send_sem, recv_sem):
            return pltpu.make_async_remote_copy(src_ref=grp, dst_ref=grp, send_sem=send_sem, recv_sem=recv_sem,
                                                device_id=(x, y, c), device_id_type=MESH)

        waiter(ssem.at[0], rsem0).wait_recv()
        for j in range(3):
            waiter(fs.at[j], fr.at[j]).wait_recv()
        for rel in range(4):
            waiter(ssem.at[rel], rsem0).wait_send()
        for j in range(3):
            waiter(fs.at[j], fr.at[j]).wait_send()

    return pl.pallas_call(
        body, name="gather_ffn2_wait", out_shape=_hbm_like(w2),
        in_specs=(HBM_SPEC, HBM_SPEC, SEM_SPEC, SEM_SPEC, SEM_SPEC, SEM_SPEC, ANY_SPEC), out_specs=HBM_SPEC,
        input_output_aliases={1: 0},
        compiler_params=pltpu.CompilerParams(has_side_effects=SPLIT_EFFECT),
    )(shard, w2, ssem_f, rsem_f0, fsend, frecv, after)


class _GatheredWeights(_LocalWeights):
    def __init__(self, shard):
        w1 = _all_gather_ffn1(shard)
        wi, wo, w2 = _place_own_rows(shard)
        (self.ssem_m, self.rsem_m, self.ssem_f, self.rsem_f0, self.rsem_f,
         self.shard, self.wi, self.wo, self.w2_part, self.w1) = _gather_rest_start(shard, wi, wo, w2, w1)

    def mix(self, after):
        self.shard, wint, wout = _gather_mix_wait(self.ssem_m, self.rsem_m, self.shard, self.wi, self.wo, after)
        return wint, wout

    def before_out_proj(self, wout, after):
        self.fsend, self.frecv, self.w2_part, wout = _gather_ffn2_pass_on(self.rsem_f, self.w2_part, wout, after)
        return wout

    def ffn2(self, after):
        return _gather_ffn2_wait(self.ssem_f, self.rsem_f0, self.fsend, self.frecv, self.shard, self.w2_part, after)

    def mix_ffn2_grads_ready(self, dwint, dwout, dw2, dh2):
        rx1 = lax.empty((4, RSA_ROWS, D), bf16)
        self.sa, self.ra, dwint, dwout, dw2, rx1, dh2 = _rsa_level1_start(dwint, dwout, dw2, rx1, dh2)
        self.level1 = (dwint, dwout, dw2, rx1)
        return dh2

    def before_ffn1_bwd(self, dx1b):
        dwint, dwout, dw2, rx1 = _rsa_level1_wait(self.sa, self.ra, *self.level1, dx1b)
        tx, self.acc = _rsa_chip_sums(dwint, dwout, dw2, rx1)
        rx2 = lax.empty((3, RSA_ROWS, D), bf16)
        self.sb, self.rb, self.tx, self.rx2, dx1b = _rsa_level2_start(tx, rx2, dx1b)
        return dx1b

    def mix_ffn2_grads_total(self, after):
        rx2 = _rsa_level2_wait(self.sb, self.rb, self.tx, self.rx2, after)
        return _rsa_total(self.acc, rx2)


RS_CHUNK = 176


def _reduce_scatter_ffn1(dw1):
    pieces = G1_PIECES
    nrows = _group_rows(pieces)
    nchunk = nrows // RS_CHUNK

    def body(d1_ref, red_ref, rx1_ref, rx2_ref,
             own_buf, rx_buf, tx_buf, acc, sa, ra, sb, rb, lsem):
        x, y, c = _position()
        me, sib = (x, y, c), (x, y, 1 - c)
        rel_chips = [(x, y), (1 - x, y), (x, 1 - y), (1 - x, 1 - y)]
        srcs = _weight_pieces(w1_ref=d1_ref)

        def piece(k, dev):
            r = PIECE_ROWS[k]
            return srcs[k].at[pl.ds(pl.multiple_of(_dev_index(*dev) * r, 16), r), :]

        def packed(ref, k):
            return ref.at[pl.ds(PIECE_OFF[k], PIECE_ROWS[k]), :]

        for j, chip in enumerate(rel_chips):
            for k in pieces:
                pltpu.make_async_remote_copy(
                    src_ref=piece(k, (*chip, 1 - c)), dst_ref=packed(rx1_ref.at[j], k),
                    send_sem=sa.at[j], recv_sem=ra.at[j], device_id=sib, device_id_type=MESH).start()

        def wait_a(j):
            return pltpu.make_async_remote_copy(src_ref=rx1_ref.at[j], dst_ref=rx1_ref.at[j], send_sem=sa.at[j],
                                                recv_sem=ra.at[j], device_id=me, device_id_type=MESH)

        def ici(j):
            return pltpu.make_async_remote_copy(
                src_ref=tx_buf.at[j - 1], dst_ref=rx2_ref.at[j - 1], send_sem=sb.at[j - 1], recv_sem=rb.at[j - 1],
                device_id=(*rel_chips[j], c), device_id_type=MESH)

        for j in (1, 2, 3, 0):
            loads = [pltpu.make_async_copy(piece(k, (*rel_chips[j], c)), packed(own_buf, k), lsem)
                     for k in pieces]
            for cp in loads:
                cp.start()
            wait_a(j).wait_recv()
            got = pltpu.make_async_copy(rx1_ref.at[j], rx_buf, lsem)
            got.start()
            pltpu.make_async_copy(rx_buf, rx_buf, lsem).wait()
            got.wait()

            def add(i, carry, j=j):
                rows = pl.ds(pl.multiple_of(i * RS_CHUNK, 16), RS_CHUNK)
                tot = own_buf[rows, :].astype(f32) + rx_buf[rows, :].astype(f32)
                if j == 0:
                    acc[rows, :] = tot
                else:
                    tx_buf[j - 1, rows, :] = tot.astype(bf16)
                return carry

            lax.fori_loop(0, nchunk, add, 0)
            if j != 0:
                ici(j).start()

        for j in (1, 2, 3):
            ici(j).wait_recv()
            got = pltpu.make_async_copy(rx2_ref.at[j - 1], rx_buf, lsem)
            got.start()
            got.wait()

            def add2(i, carry):
                rows = pl.ds(pl.multiple_of(i * RS_CHUNK, 16), RS_CHUNK)
                acc[rows, :] += rx_buf[rows, :].astype(f32)
                return carry

            lax.fori_loop(0, nchunk, add2, 0)
        out = pltpu.make_async_copy(acc, red_ref, lsem)
        out.start()
        out.wait()
        for j in range(4):
            wait_a(j).wait_send()
        for j in (1, 2, 3):
            ici(j).wait_send()

    hbm = pl.BlockSpec(memory_space=pl.ANY)
    red, _, _ = pl.pallas_call(
        body, in_specs=[hbm], out_specs=[hbm] * 3,
        out_shape=(SDS((nrows, D), f32), SDS((4, nrows, D), bf16), SDS((3, nrows, D), bf16)),
        scratch_shapes=[pltpu.VMEM((nrows, D), bf16), pltpu.VMEM((nrows, D), bf16),
                        pltpu.VMEM((3, nrows, D), bf16), pltpu.VMEM((nrows, D), f32),
                        pltpu.SemaphoreType.DMA((4,)), pltpu.SemaphoreType.DMA((4,)),
                        pltpu.SemaphoreType.DMA((3,)), pltpu.SemaphoreType.DMA((3,)), pltpu.SemaphoreType.DMA],
        compiler_params=pltpu.CompilerParams(has_side_effects=True, vmem_limit_bytes=VMEM_LIMIT_V7X),
        name="reduce_scatter_ffn1")(dw1)
    return red


RSA_PIECES = MIX_PIECES + F2_PIECES
RSA_ROWS = _group_rows(RSA_PIECES)
RSA_OFF = {k: PIECE_OFF[k] - PIECE_OFF[RSA_PIECES[0]] for k in RSA_PIECES}
RSA_BLOCK = 192


def _rsa_rows(ref, k):
    return ref.at[pl.ds(RSA_OFF[k], PIECE_ROWS[k]), :]


def _rsa_level1_start(dwint, dwout, dw2, rx1, thru):
    def body(di_ref, do_ref, d2_ref, rx1_ref, thru_ref, sa, ra, di_o, do_o, d2_o, rx1_o, thru_o):
        x, y, c = _position()
        srcs = _weight_pieces(wi_ref=di_ref, wo_ref=do_ref, w2_ref=d2_ref)
        for j, chip in enumerate([(x, y), (1 - x, y), (x, 1 - y), (1 - x, 1 - y)]):
            for k in RSA_PIECES:
                pltpu.make_async_remote_copy(
                    src_ref=_block_rows(srcs, k, (*chip, 1 - c)), dst_ref=_rsa_rows(rx1_ref.at[j], k),
                    send_sem=sa.at[j], recv_sem=ra.at[j], device_id=(x, y, 1 - c), device_id_type=MESH).start()

    dma = pltpu.SemaphoreType.DMA
    arrs = (dwint, dwout, dw2, rx1, thru)
    return pl.pallas_call(
        body, name="rsa_level1_start", out_shape=(dma((4,)), dma((4,))) + tuple(_hbm_like(a) for a in arrs),
        in_specs=(HBM_SPEC,) * 5, out_specs=(SEM_SPEC,) * 2 + (HBM_SPEC,) * 5,
        input_output_aliases={0: 2, 1: 3, 2: 4, 3: 5, 4: 6},
        compiler_params=pltpu.CompilerParams(has_side_effects=SPLIT_EFFECT),
    )(*[_in_hbm(a) for a in arrs])


def _rsa_level1_wait(sa, ra, dwint, dwout, dw2, rx1, after):
    def body(di_ref, do_ref, d2_ref, rx1_ref, sa_ref, ra_ref, after_ref, di_o, do_o, d2_o, rx1_o):
        x, y, c = _position()
        for j in range(4):
            d = pltpu.make_async_remote_copy(src_ref=rx1_ref.at[j], dst_ref=rx1_ref.at[j], send_sem=sa_ref.at[j],
                                             recv_sem=ra_ref.at[j], device_id=(x, y, c), device_id_type=MESH)
            d.wait_recv()
            d.wait_send()

    arrs = (dwint, dwout, dw2, rx1)
    return pl.pallas_call(
        body, name="rsa_level1_wait", out_shape=tuple(_hbm_like(a) for a in arrs),
        in_specs=(HBM_SPEC,) * 4 + (SEM_SPEC, SEM_SPEC, ANY_SPEC), out_specs=(HBM_SPEC,) * 4,
        input_output_aliases={0: 0, 1: 1, 2: 2, 3: 3},
        compiler_params=pltpu.CompilerParams(has_side_effects=SPLIT_EFFECT),
    )(*arrs, sa, ra, after)


def _rsa_chip_sums(dwint, dwout, dw2, rx1):
    nblk = RSA_ROWS // RSA_BLOCK

    def body(di_ref, do_ref, d2_ref, rx1_ref, tx_ref, acc_ref, own_buf, rx_buf, tx_buf, acc_buf, lsems):
        x, y, c = _position()
        srcs = _weight_pieces(wi_ref=di_ref, wo_ref=do_ref, w2_ref=d2_ref)
        for j, chip in enumerate([(x, y), (1 - x, y), (x, 1 - y), (1 - x, 1 - y)]):
            loads = [pltpu.make_async_copy(_block_rows(srcs, k, (*chip, c)), _rsa_rows(own_buf, k), lsems.at[0])
                     for k in RSA_PIECES]
            got = pltpu.make_async_copy(rx1_ref.at[j], rx_buf, lsems.at[1])
            for cp in loads + [got]:
                cp.start()
            pltpu.make_async_copy(rx_buf, rx_buf, lsems.at[0]).wait()
            got.wait()

            def add(i, carry, j=j):
                rows = pl.ds(pl.multiple_of(i * RSA_BLOCK, 16), RSA_BLOCK)
                tot = own_buf[rows, :].astype(f32) + rx_buf[rows, :].astype(f32)
                if j == 0:
                    acc_buf[rows, :] = tot
                else:
                    tx_buf[rows, :] = tot.astype(bf16)
                return carry

            lax.fori_loop(0, nblk, add, 0)
            out = (pltpu.make_async_copy(acc_buf, acc_ref, lsems.at[2]) if j == 0
                   else pltpu.make_async_copy(tx_buf, tx_ref.at[j - 1], lsems.at[2]))
            out.start()
            out.wait()

    return pl.pallas_call(
        body, in_specs=[ANY_SPEC] * 4, out_specs=[ANY_SPEC] * 2,
        out_shape=(SDS((3, RSA_ROWS, D), bf16), SDS((RSA_ROWS, D), f32)),
        scratch_shapes=[pltpu.VMEM((RSA_ROWS, D), bf16), pltpu.VMEM((RSA_ROWS, D), bf16),
                        pltpu.VMEM((RSA_ROWS, D), bf16), pltpu.VMEM((RSA_ROWS, D), f32),
                        pltpu.SemaphoreType.DMA((3,))],
        compiler_params=_cparams(None, VMEM_LIMIT_V7X), name="rsa_chip_sums")(dwint, dwout, dw2, rx1)


def _rsa_level2_start(tx, rx2, thru):
    def body(tx_ref, rx2_ref, thru_ref, sb, rb, tx_o, rx2_o, thru_o):
        x, y, c = _position()
        for j, chip in enumerate([(1 - x, y), (x, 1 - y), (1 - x, 1 - y)]):
            pltpu.make_async_remote_copy(src_ref=tx_ref.at[j], dst_ref=rx2_ref.at[j], send_sem=sb.at[j],
                                         recv_sem=rb.at[j], device_id=(*chip, c), device_id_type=MESH).start()

    dma = pltpu.SemaphoreType.DMA
    arrs = (tx, rx2, thru)
    return pl.pallas_call(
        body, name="rsa_level2_start", out_shape=(dma((3,)), dma((3,))) + tuple(_hbm_like(a) for a in arrs),
        in_specs=(HBM_SPEC,) * 3, out_specs=(SEM_SPEC,) * 2 + (HBM_SPEC,) * 3,
        input_output_aliases={0: 2, 1: 3, 2: 4},
        compiler_params=pltpu.CompilerParams(has_side_effects=SPLIT_EFFECT),
    )(*[_in_hbm(a) for a in arrs])


def _rsa_level2_wait(sb, rb, tx, rx2, after):
    def body(tx_ref, rx2_ref, sb_ref, rb_ref, after_ref, rx2_o):
        x, y, c = _position()
        for j in range(3):
            d = pltpu.make_async_remote_copy(src_ref=tx_ref.at[j], dst_ref=rx2_ref.at[j], send_sem=sb_ref.at[j],
                                             recv_sem=rb_ref.at[j], device_id=(x, y, c), device_id_type=MESH)
            d.wait_recv()
            d.wait_send()

    return pl.pallas_call(
        body, name="rsa_level2_wait", out_shape=_hbm_like(rx2),
        in_specs=(HBM_SPEC, HBM_SPEC, SEM_SPEC, SEM_SPEC, ANY_SPEC), out_specs=HBM_SPEC,
        input_output_aliases={1: 0},
        compiler_params=pltpu.CompilerParams(has_side_effects=SPLIT_EFFECT),
    )(tx, rx2, sb, rb, after)


def _rsa_total(acc, rx2):
    def body(a_ref, r_ref, o_ref):
        o_ref[...] = ((a_ref[...] + r_ref[0].astype(f32)) + r_ref[1].astype(f32)) + r_ref[2].astype(f32)

    return pl.pallas_call(
        body, grid=(RSA_ROWS // RSA_BLOCK,),
        in_specs=[pl.BlockSpec((RSA_BLOCK, D), lambda i: (i, 0)), pl.BlockSpec((3, RSA_BLOCK, D), lambda i: (0, i, 0))],
        out_specs=pl.BlockSpec((RSA_BLOCK, D), lambda i: (i, 0)),
        out_shape=SDS((RSA_ROWS, D), f32), name="rsa_total")(acc, rx2)


def _all_reduce_small(packed):
    def body(p_ref, o_ref, pair, chips, send_sems, recv_sems):
        x, y, c = _position()
        chip = 2 * x + y
        pair[c] = p_ref[...]
        swap = pltpu.make_async_remote_copy(
            src_ref=p_ref, dst_ref=pair.at[c], send_sem=send_sems.at[0], recv_sem=recv_sems.at[0],
            device_id=(x, y, 1 - c), device_id_type=MESH)
        swap.start()
        swap.wait_recv()
        chips[chip] = pair[0] + pair[1]
        cps = [pltpu.make_async_remote_copy(
            src_ref=chips.at[chip], dst_ref=chips.at[chip], send_sem=send_sems.at[1 + j], recv_sem=recv_sems.at[1 + j],
            device_id=(*other, c), device_id_type=MESH)
            for j, other in enumerate([(1 - x, y), (x, 1 - y), (1 - x, 1 - y)])]
        for cp in cps:
            cp.start()
        for cp in cps:
            cp.wait_recv()
        tot = (chips[0] + chips[1]) + (chips[2] + chips[3])
        o_ref[...] = tot
        loss = jnp.sum(tot[LOSS_ROW:LOSS_ROW + 1, :], axis=-1, keepdims=True)
        o_ref[LOSS_ROW:LOSS_ROW + 1, :] = jnp.broadcast_to(loss, (1, 128))
        swap.wait_send()
        for cp in cps:
            cp.wait_send()

    vm = pl.BlockSpec(memory_space=pltpu.VMEM)
    return pl.pallas_call(
        body, in_specs=[vm], out_specs=vm, out_shape=SDS((SMALL_ROWS, 128), f32),
        scratch_shapes=[pltpu.VMEM((2, SMALL_ROWS, 128), f32), pltpu.VMEM((4, SMALL_ROWS, 128), f32),
                        pltpu.SemaphoreType.DMA((4,)), pltpu.SemaphoreType.DMA((4,))],
        compiler_params=pltpu.CompilerParams(has_side_effects=True),
        name="all_reduce_small")(packed)


def _adamw_math(w, g, m, v):
    m = ADAM_B1 * m + (1.0 - ADAM_B1) * g
    v = ADAM_B2 * v + (1.0 - ADAM_B2) * (g * g)
    m_hat = m / (1.0 - ADAM_B1 ** ADAM_STEP)
    v_hat = v / (1.0 - ADAM_B2 ** ADAM_STEP)
    delta = -ADAM_LR * (m_hat / (jnp.sqrt(v_hat) + ADAM_EPS) + ADAM_WD * w)
    return delta, m, v


def _adamw_big(ws, ms, vs, red1, red_rest):
    npiece = len(BIG)
    rmax = max(PIECE_ROWS)

    def body(*refs):
        ins = (refs[0:npiece], refs[npiece:2 * npiece], refs[2 * npiece:3 * npiece])
        red1_ref, rest_ref = refs[3 * npiece:3 * npiece + 2]
        out_refs = refs[3 * npiece + 2:7 * npiece + 2]
        inb, outb, in_sems, out_sems = refs[7 * npiece + 2:]

        def grad_rows(k):
            if k in G1_PIECES:
                return red1_ref.at[pl.ds(PIECE_OFF[k], PIECE_ROWS[k]), :]
            return _rsa_rows(rest_ref, k)

        def loads(k):
            s, r = k % 2, PIECE_ROWS[k]
            cps = [pltpu.make_async_copy(ins[q][k].at[0], inb.at[s, q, pl.ds(0, r), :], in_sems.at[4 * s + q])
                   for q in range(3)]
            cps.append(pltpu.make_async_copy(grad_rows(k), inb.at[s, 3, pl.ds(0, r), :], in_sems.at[4 * s + 3]))
            return cps

        def stores(k):
            s, r = k % 2, PIECE_ROWS[k]
            return [pltpu.make_async_copy(outb.at[s, q, pl.ds(0, r), :], out_refs[q * npiece + k].at[0],
                                          out_sems.at[4 * s + q]) for q in range(4)]

        for cp in loads(0):
            cp.start()
        for k in range(npiece):
            s, r = k % 2, PIECE_ROWS[k]
            if k + 1 < npiece:
                for cp in loads(k + 1):
                    cp.start()
            for cp in loads(k):
                cp.wait()
            if k >= 2:
                for cp in stores(k - 2):
                    cp.wait()
            g = inb[s, 3, 0:r, :]
            d, nm, nv = _adamw_math(inb[s, 0, 0:r, :], g, inb[s, 1, 0:r, :], inb[s, 2, 0:r, :])
            outb[s, 0, 0:r, :] = g
            outb[s, 1, 0:r, :] = d
            outb[s, 2, 0:r, :] = nm
            outb[s, 3, 0:r, :] = nv
            for cp in stores(k):
                cp.start()
        for k in (npiece - 2, npiece - 1):
            for cp in stores(k):
                cp.wait()

    hbm = pl.BlockSpec(memory_space=pl.ANY)
    outs = pl.pallas_call(
        body, in_specs=[hbm] * (3 * npiece + 2), out_specs=[hbm] * (4 * npiece),
        out_shape=tuple(SDS(w.shape, f32) for _ in range(4) for w in ws),
        scratch_shapes=[pltpu.VMEM((2, 4, rmax, D), f32), pltpu.VMEM((2, 4, rmax, D), f32),
                        pltpu.SemaphoreType.DMA((8,)), pltpu.SemaphoreType.DMA((8,))],
        compiler_params=_cparams(None, VMEM_LIMIT_V7X), name="adamw_big")(*ws, *ms, *vs, red1, red_rest)
    return [list(outs[q * npiece:(q + 1) * npiece]) for q in range(4)]


def _adamw_small(w, m, v, g, name):
    def body(w_ref, m_ref, v_ref, g_ref, d_ref, nm_ref, nv_ref):
        d, nm, nv = _adamw_math(w_ref[...], g_ref[...], m_ref[...], v_ref[...])
        d_ref[...] = d
        nm_ref[...] = nm
        nv_ref[...] = nv

    return pl.pallas_call(
        body, out_shape=tuple(SDS(w.shape, f32) for _ in range(3)), name=name)(w, m, v, g)


WEIGHTS = ("ffn1_norm", "ffn1_w_gate", "ffn1_w_up", "ffn1_w_down", "mix_norm", "w_in", "q_norm", "k_norm",
           "attn_sinks", "rel_bias", "pool_w", "pool_scale", "w_out", "ffn2_norm", "ffn2_w_gate", "ffn2_w_up",
           "ffn2_w_down")
BIG = (("ffn1_w_gate", True), ("ffn1_w_up", True), ("ffn1_w_down", False), ("w_in", True), ("w_out", False),
       ("ffn2_w_gate", True), ("ffn2_w_up", True), ("ffn2_w_down", False))


def kernel(x, ffn1_norm, ffn1_w_gate, ffn1_w_up, ffn1_w_down, mix_norm, w_in, q_norm, k_norm, attn_sinks, rel_bias, pool_w, pool_scale, w_out, ffn2_norm, ffn2_w_gate, ffn2_w_up, ffn2_w_down, loss_target, m_ffn1_norm, m_ffn1_w_gate, m_ffn1_w_up, m_ffn1_w_down, m_mix_norm, m_w_in, m_q_norm, m_k_norm, m_attn_sinks, m_rel_bias, m_pool_w, m_pool_scale, m_w_out, m_ffn2_norm, m_ffn2_w_gate, m_ffn2_w_up, m_ffn2_w_down, v_ffn1_norm, v_ffn1_w_gate, v_ffn1_w_up, v_ffn1_w_down, v_mix_norm, v_w_in, v_q_norm, v_k_norm, v_attn_sinks, v_rel_bias, v_pool_w, v_pool_scale, v_w_out, v_ffn2_norm, v_ffn2_w_gate, v_ffn2_w_up, v_ffn2_w_down):
    args = dict(locals())
    w = {n: args[n] for n in WEIGHTS}
    m = {n: args["m_" + n] for n in WEIGHTS}
    v = {n: args["v_" + n] for n in WEIGHTS}

    as_rows = lambda a, tr: jnp.swapaxes(a, 1, 2) if tr else a
    shard = jnp.concatenate([as_rows(w[n], tr)[0].astype(bf16) for n, tr in BIG], axis=0)
    exchanges = _GatheredWeights(shard)
    gx, (dw1, _, _, _), small = _local_step(
        x[0], loss_target[0], exchanges, ffn1_norm, mix_norm, ffn2_norm, q_norm, k_norm, attn_sinks,
        rel_bias, pool_w[0], pool_scale)

    red1 = _reduce_scatter_ffn1(dw1)
    red_rest = exchanges.mix_ffn2_grads_total(red1)
    small_tot = _all_reduce_small(_pack_small(small))

    grads, deltas, new_m, new_v = {}, {}, {}, {}
    big_out = _adamw_big(*[[as_rows(t[n], tr) for n, tr in BIG] for t in (w, m, v)], red1, red_rest)
    for k, (n, tr) in enumerate(BIG):
        grads[n], deltas[n], new_m[n], new_v[n] = [as_rows(o[k], tr) for o in big_out]
    small_names = [n for n in SMALL_NAMES if n != "loss"]
    ds, nms, nvs = _adamw_small(_pack_small({n: w[n] for n in small_names}), _pack_small({n: m[n] for n in small_names}),
                                _pack_small({n: v[n] for n in small_names}), small_tot, "adamw_small")
    for n in small_names:
        grads[n] = _unpack_small(small_tot, n)
        deltas[n], new_m[n], new_v[n] = _unpack_small(ds, n), _unpack_small(nms, n), _unpack_small(nvs, n)
    loss = small_tot[LOSS_ROW, 0]
    return (loss, gx[None], *[grads[n] for n in WEIGHTS], *[deltas[n] for n in WEIGHTS],
            *[new_m[n] for n in WEIGHTS], *[new_v[n] for n in WEIGHTS])
```

```python
import functools

import jax
import jax.numpy as jnp
import numpy as np
from jax import lax
from jax.experimental import pallas as pl
from jax.experimental.pallas import tpu as pltpu

f32, bf16, i32 = jnp.float32, jnp.bfloat16, jnp.int32
SDS = jax.ShapeDtypeStruct

D = 1024
F = 2816
HD = 64
NH = 8
NKV = 2
GQA = NH // NKV
DATTN = NH * HD
DKV = NKV * HD
DPOOL = 512
POOL_WINDOWS = (2, 4, 8, 16)
PGD = DPOOL // len(POOL_WINDOWS)
DIN = DATTN + 2 * DKV + DPOOL
DMIX = DATTN + DPOOL
BLK = 128
NBUCK = 32
MAX_DISTANCE = 128
EPS = 1e-6
NEG = -1e30
SCALE = HD ** -0.5

ADAM_LR, ADAM_B1, ADAM_B2, ADAM_EPS, ADAM_WD, ADAM_STEP = 0.001, 0.9, 0.999, 1e-08, 0.01, 10

NDEV = 8
FS = F // NDEV
INS = DIN // NDEV
OUTS = DMIX // NDEV
PIECE_ROWS = (FS, FS, FS, INS, OUTS, FS, FS, FS)
PIECE_OFF = tuple(int(v) for v in np.cumsum((0,) + PIECE_ROWS[:-1]))
PACK_ROWS = sum(PIECE_ROWS)

VMEM_LIMIT_V7X = 56 * 1024 * 1024

MESH = pl.DeviceIdType.MESH


def _cparams(sem=None, vmem=None):
    return pltpu.CompilerParams(dimension_semantics=sem, vmem_limit_bytes=vmem)


def _nt(a, b):
    return lax.dot_general(a, b, (((1,), (1,)), ((), ())), preferred_element_type=f32)


def _tn(a, b):
    return lax.dot_general(a, b, (((0,), (0,)), ((), ())), preferred_element_type=f32)


def _nn(a, b):
    return jnp.dot(a, b, preferred_element_type=f32)


def _sigmoid(x):
    return 1.0 / (1.0 + jnp.exp(-x))


def _norm_fwd(x, g, name):
    T = x.shape[0]
    tm = min(512, T)

    def body(x_ref, g_ref, h_ref):
        xv = x_ref[...]
        r = lax.rsqrt(jnp.mean(xv * xv, axis=-1, keepdims=True) + EPS)
        h_ref[...] = (xv * r * g_ref[...]).astype(bf16)

    return pl.pallas_call(
        body, grid=(T // tm,),
        in_specs=[pl.BlockSpec((tm, D), lambda i: (i, 0)), pl.BlockSpec((1, D), lambda i: (0, 0))],
        out_specs=pl.BlockSpec((tm, D), lambda i: (i, 0)),
        out_shape=SDS((T, D), bf16), name=name)(x, g)


def _norm_bwd(dh, x, g, dres, out_scale, name):
    T = x.shape[0]
    tm = min(512, T)

    def body(dh_ref, x_ref, g_ref, dr_ref, dx_ref, dxb_ref, dg_ref):
        i = pl.program_id(0)
        xv = x_ref[...]
        r = lax.rsqrt(jnp.mean(xv * xv, axis=-1, keepdims=True) + EPS)
        xh = xv * r
        dhv = dh_ref[...]
        dxh = dhv * g_ref[...]
        dx = dr_ref[...] + r * (dxh - xh * jnp.mean(dxh * xh, axis=-1, keepdims=True))
        dx_ref[...] = dx
        dxb_ref[...] = (out_scale * dx).astype(bf16)
        dg = jnp.sum(dhv * xh, axis=0, keepdims=True)

        @pl.when(i == 0)
        def _():
            dg_ref[...] = dg

        @pl.when(i > 0)
        def _():
            dg_ref[...] += dg

    tok = pl.BlockSpec((tm, D), lambda i: (i, 0))
    vec = pl.BlockSpec((1, D), lambda i: (0, 0))
    return pl.pallas_call(
        body, grid=(T // tm,),
        in_specs=[tok, tok, vec, tok], out_specs=[tok, tok, vec],
        out_shape=(SDS((T, D), f32), SDS((T, D), bf16), SDS((1, D), f32)),
        compiler_params=_cparams(("arbitrary",)), name=name)(dh, x, g, dres)


FFN_ROW_CHUNK = 256


def _ffn_tiles(T):
    return min(1024, T), 256


def _ffn_fwd(h, w, x, name):
    T = h.shape[0]
    tm, tf = _ffn_tiles(T)
    nf = F // tf

    def body(h_ref, w_ref, x_hbm, xo_ref, g_ref, u_ref, sem):
        fi = pl.program_id(0)

        @pl.when(fi == 0)
        def _():
            cp = pltpu.make_async_copy(x_hbm, xo_ref, sem)
            cp.start()
            cp.wait()

        wgu = w_ref[0:2].reshape(2 * tf, D)
        for r in range(0, T, tm):
            rows = slice(r, r + tm)
            gu = _nt(h_ref[rows, :], wgu)
            gate, up = gu[:, :tf], gu[:, tf:]
            act = gate * _sigmoid(gate) * up
            g_ref[0, rows, :] = gate.astype(bf16)
            u_ref[0, rows, :] = up.astype(bf16)
            xo_ref[rows, :] += _nn((0.5 * act).astype(bf16), w_ref[2])

    tok = pl.BlockSpec((T, D), lambda f: (0, 0))
    act_spec = pl.BlockSpec((1, T, tf), lambda f: (f, 0, 0))
    return pl.pallas_call(
        body, grid=(nf,),
        in_specs=[tok, pl.BlockSpec((3, tf, D), lambda f: (0, f, 0)), pl.BlockSpec(memory_space=pl.ANY)],
        out_specs=[tok, act_spec, act_spec],
        out_shape=(SDS((T, D), f32), SDS((nf, T, tf), bf16), SDS((nf, T, tf), bf16)),
        scratch_shapes=[pltpu.SemaphoreType.DMA],
        compiler_params=_cparams(("arbitrary",), VMEM_LIMIT_V7X), name=name)(h, w, x)


def _ffn_bwd(dob, h, gate, up, w, name):
    T = h.shape[0]
    _, tf = _ffn_tiles(T)
    nf = F // tf

    def body(do_hbm, h_hbm, g_ref, u_ref, w_ref, dh_hbm, dw_ref, do_v, h_v, dh_acc, dgu_s, act_s, sems):
        fi = pl.program_id(0)

        @pl.when(fi == 0)
        def _():
            loads = [pltpu.make_async_copy(do_hbm, do_v, sems.at[0]), pltpu.make_async_copy(h_hbm, h_v, sems.at[1])]
            for cp in loads:
                cp.start()
            dh_acc[...] = jnp.zeros_like(dh_acc)
            for cp in loads:
                cp.wait()

        wgu = w_ref[0:2].reshape(2 * tf, D)
        for r in range(0, T, FFN_ROW_CHUNK):
            rows = slice(r, r + FFN_ROW_CHUNK)
            dov = do_v[rows, :]
            gv = g_ref[0, rows, :].astype(f32)
            uv = u_ref[0, rows, :].astype(f32)
            sg = _sigmoid(gv)
            sil = gv * sg
            dact = _nt(dov, w_ref[2])
            dup = dact * sil
            dgate = dact * uv * (sg * (1.0 + gv * (1.0 - sg)))
            dgu = jnp.concatenate([dgate.astype(bf16), dup.astype(bf16)], axis=1)
            dgu_s[rows, :] = dgu
            act_s[rows, :] = (sil * uv).astype(bf16)
            dh_acc[rows, :] += _nn(dgu, wgu)
        dw_ref[0:2] = _tn(dgu_s[...], h_v[...]).reshape(2, tf, D).astype(bf16)
        dw_ref[2] = _tn(act_s[...], do_v[...]).astype(bf16)

        @pl.when(fi == nf - 1)
        def _():
            out = pltpu.make_async_copy(dh_acc, dh_hbm, sems.at[0])
            out.start()
            out.wait()

    act_spec = pl.BlockSpec((1, T, tf), lambda f: (f, 0, 0))
    wspec = pl.BlockSpec((3, tf, D), lambda f: (0, f, 0))
    hbm = pl.BlockSpec(memory_space=pl.ANY)
    return pl.pallas_call(
        body, grid=(nf,),
        in_specs=[hbm, hbm, act_spec, act_spec, wspec],
        out_specs=[hbm, wspec],
        out_shape=(SDS((T, D), f32), SDS((3, F, D), bf16)),
        scratch_shapes=[pltpu.VMEM((T, D), bf16), pltpu.VMEM((T, D), bf16), pltpu.VMEM((T, D), f32),
                        pltpu.VMEM((T, 2 * tf), bf16), pltpu.VMEM((T, tf), bf16), pltpu.SemaphoreType.DMA((2,))],
        compiler_params=_cparams(("arbitrary",), VMEM_LIMIT_V7X), name=name)(dob, h, gate, up, w)


def _loss_grad(y, target, name):
    T = y.shape[0]
    tm = min(512, T)

    def body(y_ref, t_ref, dy_ref, dyb_ref, l_ref):
        i = pl.program_id(0)
        e = y_ref[...] - t_ref[...]
        dy = e * (1.0 / D)
        dy_ref[...] = dy
        dyb_ref[...] = (0.5 * dy).astype(bf16)
        col = jnp.sum(e * e, axis=0, keepdims=True) * (0.5 / D)
        lanes = col[:, 0:128]
        for k in range(1, D // 128):
            lanes = lanes + col[:, 128 * k:128 * (k + 1)]

        @pl.when(i == 0)
        def _():
            l_ref[...] = lanes

        @pl.when(i > 0)
        def _():
            l_ref[...] += lanes

    tok = pl.BlockSpec((tm, D), lambda i: (i, 0))
    return pl.pallas_call(
        body, grid=(T // tm,), in_specs=[tok, tok],
        out_specs=[tok, tok, pl.BlockSpec((1, 128), lambda i: (0, 0))],
        out_shape=(SDS((T, D), f32), SDS((T, D), bf16), SDS((1, 128), f32)),
        compiler_params=_cparams(("arbitrary",)), name=name)(y, target)


def _in_proj_fwd(h, wint, name):
    T = h.shape[0]
    tm = min(512, T)

    def body(h_ref, w_ref, z_ref):
        z_ref[...] = _nt(h_ref[...], w_ref[...])

    return pl.pallas_call(
        body, grid=(T // tm,),
        in_specs=[pl.BlockSpec((tm, D), lambda i: (i, 0)), pl.BlockSpec((DIN, D), lambda i: (0, 0))],
        out_specs=pl.BlockSpec((tm, DIN), lambda i: (i, 0)),
        out_shape=SDS((T, DIN), f32), name=name)(h, wint)


def _in_proj_bwd(dz, wint, h, name):
    T = h.shape[0]
    tm = min(512, T)
    nt = T // tm

    def body(dz_ref, w_ref, h_ref, dh_ref, dw_ref, acc):
        i = pl.program_id(0)
        dzb = dz_ref[...].astype(bf16)
        dh_ref[...] = _nn(dzb, w_ref[...])
        part = _tn(dzb, h_ref[...])

        @pl.when(i == 0)
        def _():
            acc[...] = part

        @pl.when(i > 0)
        def _():
            acc[...] += part

        @pl.when(i == nt - 1)
        def _():
            dw_ref[...] = acc[...].astype(bf16)

    wspec = pl.BlockSpec((DIN, D), lambda i: (0, 0))
    return pl.pallas_call(
        body, grid=(nt,),
        in_specs=[pl.BlockSpec((tm, DIN), lambda i: (i, 0)), wspec, pl.BlockSpec((tm, D), lambda i: (i, 0))],
        out_specs=[pl.BlockSpec((tm, D), lambda i: (i, 0)), wspec],
        out_shape=(SDS((T, D), f32), SDS((DIN, D), bf16)),
        scratch_shapes=[pltpu.VMEM((DIN, D), f32)],
        compiler_params=_cparams(("arbitrary",)), name=name)(dz, wint, h)


def _out_proj_fwd(ymix, wout, x, name):
    T = x.shape[0]
    tm = min(512, T)

    def body(y_ref, w_ref, x_ref, o_ref):
        o_ref[...] = x_ref[...] + _nn(y_ref[...], w_ref[...])

    tok = pl.BlockSpec((tm, D), lambda i: (i, 0))
    return pl.pallas_call(
        body, grid=(T // tm,),
        in_specs=[pl.BlockSpec((tm, DMIX), lambda i: (i, 0)), pl.BlockSpec((DMIX, D), lambda i: (0, 0)), tok],
        out_specs=tok, out_shape=SDS((T, D), f32), name=name)(ymix, wout, x)


def _out_proj_bwd(dxb, wout, ymix, name):
    T = dxb.shape[0]
    tm = min(512, T)
    nt = T // tm

    def body(dx_ref, w_ref, y_ref, dy_ref, dw_ref, acc):
        i = pl.program_id(0)
        dxv = dx_ref[...]
        dy_ref[...] = _nt(dxv, w_ref[...])
        part = _tn(y_ref[...], dxv)

        @pl.when(i == 0)
        def _():
            acc[...] = part

        @pl.when(i > 0)
        def _():
            acc[...] += part

        @pl.when(i == nt - 1)
        def _():
            dw_ref[...] = acc[...].astype(bf16)

    wspec = pl.BlockSpec((DMIX, D), lambda i: (0, 0))
    return pl.pallas_call(
        body, grid=(nt,),
        in_specs=[pl.BlockSpec((tm, D), lambda i: (i, 0)), wspec, pl.BlockSpec((tm, DMIX), lambda i: (i, 0))],
        out_specs=[pl.BlockSpec((tm, DMIX), lambda i: (i, 0)), wspec],
        out_shape=(SDS((T, DMIX), f32), SDS((DMIX, D), bf16)),
        scratch_shapes=[pltpu.VMEM((DMIX, D), f32)],
        compiler_params=_cparams(("arbitrary",)), name=name)(dxb, wout, ymix)


def _t5_bucket_table():
    ql = np.arange(BLK)[:, None]
    kl = np.arange(2 * BLK)[None, :]
    n = np.maximum(ql + BLK - kl, 0)
    max_exact = NBUCK // 2
    large = max_exact + (np.log(np.maximum(n, 1) / max_exact) / np.log(MAX_DISTANCE / max_exact)
                         * (NBUCK - max_exact)).astype(np.int32)
    large = np.minimum(large, NBUCK - 1)
    return np.where(n < max_exact, n, large).astype(np.int32)


def _fill_bias(bk_ref, rb_ref, bias_scr):
    bk = bk_ref[...]
    for h in range(NH):
        def step(b, acc, h=h):
            return acc + jnp.where(bk == b, rb_ref[b, h], 0.0)
        bias_scr[h] = lax.fori_loop(0, NBUCK, step, jnp.zeros((BLK, 2 * BLK), f32))


MIX_SUB = 4


class _Window:
    def __init__(self, zc_ref, zp_ref, n, s):
        self.blk = n * MIX_SUB + s
        self.first_in_step = s == 0
        self.cur = lambda a, b: zc_ref[s * BLK:(s + 1) * BLK, a:b]
        self.prev = (lambda a, b: zp_ref[:, a:b]) if s == 0 else (lambda a, b: zc_ref[(s - 1) * BLK:s * BLK, a:b])


def _attn_probs(win, kh, qg, kg, sk_ref, bias_scr):
    n = win.blk
    kc = DATTN + HD * kh
    vc = DATTN + DKV + HD * kh
    kx = jnp.concatenate([win.prev(kc, kc + HD), win.cur(kc, kc + HD)], axis=0)
    vx = jnp.concatenate([win.prev(vc, vc + HD), win.cur(vc, vc + HD)], axis=0)
    qx = jnp.concatenate([win.cur(HD * (GQA * kh + g), HD * (GQA * kh + g + 1)) for g in range(GQA)], axis=0)
    rq = lax.rsqrt(jnp.mean(qx * qx, axis=-1, keepdims=True) + EPS)
    rk = lax.rsqrt(jnp.mean(kx * kx, axis=-1, keepdims=True) + EPS)
    qhat, khat = qx * rq, kx * rk
    qnb, knb = (qhat * qg).astype(bf16), (khat * kg).astype(bf16)
    s = _nt(qnb, knb) * SCALE + bias_scr[GQA * kh:GQA * (kh + 1)].reshape(GQA * BLK, 2 * BLK)
    row = lax.broadcasted_iota(i32, (GQA * BLK, 2 * BLK), 0) & (BLK - 1)
    col = lax.broadcasted_iota(i32, (GQA * BLK, 2 * BLK), 1)
    mask = (col > row) & (col <= row + BLK) & ((col >= BLK) | (n > 0))
    s = jnp.where(mask, s, NEG)
    ridx = lax.broadcasted_iota(i32, (GQA * BLK, 1), 0)
    sink = jnp.full((GQA * BLK, 1), sk_ref[GQA * kh + GQA - 1], f32)
    for g in range(GQA - 2, -1, -1):
        sink = jnp.where(ridx < (g + 1) * BLK, sk_ref[GQA * kh + g], sink)
    m = jnp.maximum(jnp.max(s, axis=-1, keepdims=True), sink)
    e = jnp.exp(s - m)
    es = jnp.exp(sink - m)
    den = jnp.sum(e, axis=-1, keepdims=True) + es
    return dict(p=e / den, psink=es / den, qhat=qhat, khat=khat, rq=rq, rk=rk, qnb=qnb, knb=knb, vb=vx.astype(bf16))


def _pool_group(win, g, w):
    n = win.blk
    c0 = DATTN + 2 * DKV + PGD * g
    uc = win.cur(c0, c0 + PGD)
    up = jnp.where(n > 0, win.prev(c0, c0 + PGD), 0.0)
    ue = jnp.concatenate([up, uc], axis=0)
    hi = ue.astype(bf16)
    lo = (ue - hi.astype(f32)).astype(bf16)
    t = lax.broadcasted_iota(i32, (BLK, 2 * BLK), 0)
    s = lax.broadcasted_iota(i32, (BLK, 2 * BLK), 1)
    band = jnp.where((s <= t + BLK) & (s > t + BLK - w), 1.0, 0.0).astype(bf16)
    sm = _nn(band, hi) + _nn(band, lo)
    pos = n * BLK + lax.broadcasted_iota(i32, (BLK, 1), 0) + 1
    cnt = jnp.minimum(pos, w).astype(f32)
    return sm / cnt - uc, band, cnt


def _mix_fwd(z, qg, kg, sinks, relb, bucket, pool_w, pscale, name):
    T = z.shape[0]
    step_rows = MIX_SUB * BLK
    nsteps = T // step_rows

    def body(zc_ref, zp_ref, qg_ref, kg_ref, sk_ref, rb_ref, bk_ref, pw_ref, ps_ref, y_ref, bias_scr, yacc):
        n = pl.program_id(0)

        @pl.when(n == 0)
        def _():
            _fill_bias(bk_ref, rb_ref, bias_scr)

        for s in range(MIX_SUB):
            win = _Window(zc_ref, zp_ref, n, s)
            rows = slice(s * BLK, (s + 1) * BLK)
            for kh in range(NKV):
                a = _attn_probs(win, kh, qg_ref[...], kg_ref[...], sk_ref, bias_scr)
                o = _nn(a["p"].astype(bf16), a["vb"])
                for g in range(GQA):
                    hc = HD * (GQA * kh + g)
                    yacc[rows, hc:hc + HD] = o[g * BLK:(g + 1) * BLK]
            for g, w in enumerate(POOL_WINDOWS):
                pooled, _, _ = _pool_group(win, g, w)
                yp = _nn(pooled.astype(bf16), pw_ref[g].astype(bf16)) * ps_ref[:, PGD * g:PGD * (g + 1)]
                yacc[rows, DATTN + PGD * g:DATTN + PGD * (g + 1)] = yp
        y_ref[...] = yacc[...].astype(bf16)

    full = lambda *shape: pl.BlockSpec(shape, lambda n: (0,) * len(shape))
    smem = pl.BlockSpec(memory_space=pltpu.SMEM)
    return pl.pallas_call(
        body, grid=(nsteps,),
        in_specs=[pl.BlockSpec((step_rows, DIN), lambda n: (n, 0)),
                  pl.BlockSpec((BLK, DIN), lambda n: (jnp.maximum(n * MIX_SUB - 1, 0), 0)),
                  full(1, HD), full(1, HD), smem, smem, full(BLK, 2 * BLK),
                  full(len(POOL_WINDOWS), PGD, PGD), full(1, DPOOL)],
        out_specs=pl.BlockSpec((step_rows, DMIX), lambda n: (n, 0)),
        out_shape=SDS((T, DMIX), bf16),
        scratch_shapes=[pltpu.VMEM((NH, BLK, 2 * BLK), f32), pltpu.VMEM((step_rows, DMIX), f32)],
        compiler_params=_cparams(("arbitrary",)), name=name)(z, z, qg, kg, sinks, relb, bucket, pool_w, pscale)


def _mix_bwd(z, dy, qg, kg, sinks, relb, bucket, pool_w, pscale, name):
    T = z.shape[0]
    step_rows = MIX_SUB * BLK
    nsteps = T // step_rows

    def body(zc_ref, zp_ref, dy_ref, qg_ref, kg_ref, sk_ref, rb_ref, bk_ref, pw_ref, ps_ref,
             dz_ref, dqg_ref, dkg_ref, dsk_ref, drb_ref, dpw_ref, dps_ref, bias_scr, dbias_scr):
        n = pl.program_id(0)

        @pl.when(n == 0)
        def _():
            _fill_bias(bk_ref, rb_ref, bias_scr)
            dbias_scr[...] = jnp.zeros_like(dbias_scr)
            dqg_ref[...] = jnp.zeros_like(dqg_ref)
            dkg_ref[...] = jnp.zeros_like(dkg_ref)
            dsk_ref[...] = jnp.zeros_like(dsk_ref)
            dpw_ref[...] = jnp.zeros_like(dpw_ref)
            dps_ref[...] = jnp.zeros_like(dps_ref)

        qg, kg = qg_ref[...], kg_ref[...]
        lane = lax.broadcasted_iota(i32, (1, 128), 1)
        dsk = jnp.zeros((1, 128), f32)
        for s in range(MIX_SUB):
            win = _Window(zc_ref, zp_ref, n, s)
            blk = win.blk
            rows = pl.ds(pl.multiple_of(blk * BLK, BLK), BLK)
            prow = pl.ds(pl.multiple_of(jnp.maximum(blk - 1, 0) * BLK, BLK), BLK)
            dyr = slice(s * BLK, (s + 1) * BLK)

            def into_prev(fn, s=s):
                if s == 0:
                    pl.when(n > 0)(fn)
                else:
                    fn()

            for kh in range(NKV):
                a = _attn_probs(win, kh, qg, kg, sk_ref, bias_scr)
                p = a["p"]
                do = jnp.concatenate([dy_ref[dyr, HD * (GQA * kh + g):HD * (GQA * kh + g + 1)] for g in range(GQA)],
                                     axis=0).astype(bf16)
                dv = _tn(p.astype(bf16), do)
                dp = _nt(do, a["vb"])
                delta = jnp.sum(p * dp, axis=-1, keepdims=True)
                ds = p * (dp - delta)
                sinkterm = a["psink"] * delta
                for g in range(GQA):
                    h = GQA * kh + g
                    dbias_scr[h] += ds[g * BLK:(g + 1) * BLK]
                    tot = jnp.sum(sinkterm[g * BLK:(g + 1) * BLK], axis=0, keepdims=True)
                    dsk = dsk - jnp.where(lane == h, tot, 0.0)
                dsb = ds.astype(bf16)
                dqn = _nn(dsb, a["knb"]) * SCALE
                dkn = _tn(dsb, a["qnb"]) * SCALE
                qhat, khat = a["qhat"], a["khat"]
                dqg_ref[...] += jnp.sum(dqn * qhat, axis=0, keepdims=True)
                dkg_ref[...] += jnp.sum(dkn * khat, axis=0, keepdims=True)
                dqh = dqn * qg
                dq = a["rq"] * (dqh - qhat * jnp.mean(dqh * qhat, axis=-1, keepdims=True))
                dkh = dkn * kg
                dk = a["rk"] * (dkh - khat * jnp.mean(dkh * khat, axis=-1, keepdims=True))
                kc = DATTN + HD * kh
                vc = DATTN + DKV + HD * kh
                for g in range(GQA):
                    hc = HD * (GQA * kh + g)
                    dz_ref[rows, hc:hc + HD] = dq[g * BLK:(g + 1) * BLK]
                dz_ref[rows, kc:kc + HD] = dk[BLK:2 * BLK]
                dz_ref[rows, vc:vc + HD] = dv[BLK:2 * BLK]

                def kv_prev(dk=dk, dv=dv, kc=kc, vc=vc, prow=prow):
                    dz_ref[prow, kc:kc + HD] += dk[0:BLK]
                    dz_ref[prow, vc:vc + HD] += dv[0:BLK]

                into_prev(kv_prev)

            for g, w in enumerate(POOL_WINDOWS):
                c0 = DATTN + 2 * DKV + PGD * g
                pooled, band, cnt = _pool_group(win, g, w)
                pb = pooled.astype(bf16)
                wb = pw_ref[g].astype(bf16)
                dyp = dy_ref[dyr, DATTN + PGD * g:DATTN + PGD * (g + 1)]
                ypre = _nn(pb, wb)
                dps_ref[:, PGD * g:PGD * (g + 1)] += jnp.sum(dyp * ypre, axis=0, keepdims=True)
                dyg = (dyp * ps_ref[:, PGD * g:PGD * (g + 1)]).astype(bf16)
                dpw_ref[g] += _tn(pb, dyg)
                dpooled = _nt(dyg, wb)
                dsm = dpooled / cnt
                hi = dsm.astype(bf16)
                lo = (dsm - hi.astype(f32)).astype(bf16)
                due = _tn(band, hi) + _tn(band, lo)
                dz_ref[rows, c0:c0 + PGD] = due[BLK:2 * BLK] - dpooled

                def pool_prev(due=due, c0=c0, prow=prow):
                    dz_ref[prow, c0:c0 + PGD] += due[0:BLK]

                into_prev(pool_prev)

        dsk_ref[...] += dsk

        @pl.when(n == nsteps - 1)
        def _():
            bk = bk_ref[...]
            ri = lax.broadcasted_iota(i32, (NBUCK, NH), 0)
            ci = lax.broadcasted_iota(i32, (NBUCK, NH), 1)

            def step(b, acc):
                for h in range(NH):
                    sel = jnp.where(bk == b, dbias_scr[h], 0.0)
                    tot = jnp.sum(jnp.sum(sel, axis=1, keepdims=True), axis=0, keepdims=True)
                    acc = acc + jnp.where((ri == b) & (ci == h), tot, 0.0)
                return acc

            drb_ref[...] = lax.fori_loop(0, NBUCK, step, jnp.zeros((NBUCK, NH), f32))

    full = lambda *shape: pl.BlockSpec(shape, lambda n: (0,) * len(shape))
    smem = pl.BlockSpec(memory_space=pltpu.SMEM)
    npg = len(POOL_WINDOWS)
    return pl.pallas_call(
        body, grid=(nsteps,),
        in_specs=[pl.BlockSpec((step_rows, DIN), lambda n: (n, 0)),
                  pl.BlockSpec((BLK, DIN), lambda n: (jnp.maximum(n * MIX_SUB - 1, 0), 0)),
                  pl.BlockSpec((step_rows, DMIX), lambda n: (n, 0)),
                  full(1, HD), full(1, HD), smem, smem, full(BLK, 2 * BLK), full(npg, PGD, PGD), full(1, DPOOL)],
        out_specs=[full(T, DIN), full(1, HD), full(1, HD), full(1, 128), full(NBUCK, NH),
                   full(npg, PGD, PGD), full(1, DPOOL)],
        out_shape=(SDS((T, DIN), f32), SDS((1, HD), f32), SDS((1, HD), f32), SDS((1, 128), f32),
                   SDS((NBUCK, NH), f32), SDS((npg, PGD, PGD), f32), SDS((1, DPOOL), f32)),
        scratch_shapes=[pltpu.VMEM((NH, BLK, 2 * BLK), f32), pltpu.VMEM((NH, BLK, 2 * BLK), f32)],
        compiler_params=_cparams(("arbitrary",), VMEM_LIMIT_V7X),
        name=name)(z, z, dy, qg, kg, sinks, relb, bucket, pool_w, pscale)


class _LocalWeights:
    def __init__(self, w1, wint, wout, w2):
        self.w1, self.wint, self.wout, self.w2 = w1, wint, wout, w2

    def ffn1(self):
        return self.w1

    def mix(self, after):
        return self.wint, self.wout

    def before_out_proj(self, wout, after):
        return wout

    def ffn2(self, after):
        return self.w2

    def mix_ffn2_grads_ready(self, dwint, dwout, dw2, dh2):
        self.grads_rest = (dwint, dwout, dw2)
        return dh2

    def before_ffn1_bwd(self, dx1b):
        return dx1b


def _local_step(x, target, weights, g1, gm, g3, qg, kg, sinks, relb, pool_w, pscale):
    bucket = jnp.asarray(_t5_bucket_table())
    sk = sinks.reshape(NH)
    w1 = weights.ffn1()
    h1 = _norm_fwd(x, g1, "norm1_fwd")
    x1, gate1, up1 = _ffn_fwd(h1, w1, x, "ffn1_fwd")
    h2 = _norm_fwd(x1, gm, "norm2_fwd")
    wint, wout = weights.mix(h2)
    z = _in_proj_fwd(h2, wint, "in_proj_fwd")
    ymix = _mix_fwd(z, qg, kg, sk, relb, bucket, pool_w, pscale, "mix_fwd")
    wout = weights.before_out_proj(wout, ymix)
    x2 = _out_proj_fwd(ymix, wout, x1, "out_proj_fwd")
    h3 = _norm_fwd(x2, g3, "norm3_fwd")
    w2 = weights.ffn2(h3)
    y, gate2, up2 = _ffn_fwd(h3, w2, x2, "ffn2_fwd")
    dy, dyb, loss_lanes = _loss_grad(y, target, "loss_grad")

    dh3, dw2 = _ffn_bwd(dyb, h3, gate2, up2, w2, "ffn2_bwd")
    dx2, dx2b, dg3 = _norm_bwd(dh3, x2, g3, dy, 1.0, "norm3_bwd")
    dymix, dwout = _out_proj_bwd(dx2b, wout, ymix, "out_proj_bwd")
    dz, dqg, dkg, dsk, drb, dpw, dps = _mix_bwd(z, dymix, qg, kg, sk, relb, bucket, pool_w, pscale, "mix_bwd")
    dh2, dwint = _in_proj_bwd(dz, wint, h2, "in_proj_bwd")
    dh2 = weights.mix_ffn2_grads_ready(dwint, dwout, dw2, dh2)
    dx1, dx1b, dgm = _norm_bwd(dh2, x1, gm, dx2, 0.5, "norm2_bwd")
    dx1b = weights.before_ffn1_bwd(dx1b)
    dh1, dw1 = _ffn_bwd(dx1b, h1, gate1, up1, w1, "ffn1_bwd")
    gx, _, dg1 = _norm_bwd(dh1, x, g1, dx1, 1.0, "norm1_bwd")
    small = dict(ffn1_norm=dg1, mix_norm=dgm, ffn2_norm=dg3, pool_scale=dps, q_norm=dqg, k_norm=dkg,
                 attn_sinks=dsk[:, :NH], rel_bias=drb, pool_w=dpw, loss=loss_lanes)
    return gx, (dw1, dwint, dwout, dw2), small


SMALL_NAMES = ("ffn1_norm", "mix_norm", "ffn2_norm", "pool_scale", "q_norm", "k_norm", "attn_sinks", "rel_bias",
               "pool_w", "loss")
SMALL_SHAPES = dict(ffn1_norm=(1, D), mix_norm=(1, D), ffn2_norm=(1, D), pool_scale=(1, DPOOL), q_norm=(1, HD),
                    k_norm=(1, HD), attn_sinks=(1, NH), rel_bias=(NBUCK, NH),
                    pool_w=(1, len(POOL_WINDOWS), PGD, PGD), loss=(1, 128))


def _small_rows(name):
    return -(-int(np.prod(SMALL_SHAPES[name])) // 128)


SMALL_OFF = {}
_r = 0
for _n in SMALL_NAMES:
    SMALL_OFF[_n] = _r
    _r += _small_rows(_n)
SMALL_ROWS = -(-_r // 8) * 8
LOSS_ROW = SMALL_OFF["loss"]


def _pack_small(vals):
    parts = []
    for n in SMALL_NAMES:
        size = _small_rows(n) * 128
        if n in vals:
            flat = vals[n].astype(f32).reshape(-1)
            parts.append(jnp.pad(flat, (0, size - flat.shape[0])))
        else:
            parts.append(jnp.zeros((size,), f32))
    flat = jnp.concatenate(parts)
    flat = jnp.pad(flat, (0, SMALL_ROWS * 128 - flat.shape[0]))
    return flat.reshape(SMALL_ROWS, 128)


def _unpack_small(packed, name):
    size = int(np.prod(SMALL_SHAPES[name]))
    r0 = SMALL_OFF[name]
    return packed[r0:r0 + _small_rows(name)].reshape(-1)[:size].reshape(SMALL_SHAPES[name])


def _position():
    return lax.axis_index("x"), lax.axis_index("y"), lax.axis_index("c")


def _dev_index(x, y, c):
    return 4 * x + 2 * y + c


G1_PIECES, MIX_PIECES, F2_PIECES = (0, 1, 2), (3, 4), (5, 6, 7)


def _group_rows(pieces):
    return sum(PIECE_ROWS[k] for k in pieces)


def _shard_piece(s_ref, k):
    return s_ref.at[pl.ds(PIECE_OFF[k], PIECE_ROWS[k]), :]


def _shard_group(s_ref, pieces):
    return s_ref.at[pl.ds(PIECE_OFF[pieces[0]], _group_rows(pieces)), :]


def _weight_pieces(w1_ref=None, wi_ref=None, wo_ref=None, w2_ref=None):
    arrs = {}
    if w1_ref is not None:
        arrs.update({0: w1_ref.at[0], 1: w1_ref.at[1], 2: w1_ref.at[2]})
    if wi_ref is not None:
        arrs[3] = wi_ref
    if wo_ref is not None:
        arrs[4] = wo_ref
    if w2_ref is not None:
        arrs.update({5: w2_ref.at[0], 6: w2_ref.at[1], 7: w2_ref.at[2]})
    return arrs


def _block_rows(arrs, k, dev):
    r = PIECE_ROWS[k]
    return arrs[k].at[pl.ds(pl.multiple_of(_dev_index(*dev) * r, 16), r), :]


def _all_gather_ffn1(shard):
    pieces = G1_PIECES

    def body(s_ref, w1_ref, send_sems, recv_sems, local_sem):
        x, y, c = _position()
        me, sib = (x, y, c), (x, y, 1 - c)
        chips = [(1 - x, y), (x, 1 - y), (1 - x, 1 - y)]
        arrs = _weight_pieces(w1_ref=w1_ref)

        def copies(rel, block, to, from_shard):
            return [pltpu.make_async_remote_copy(
                src_ref=_shard_piece(s_ref, k) if from_shard else _block_rows(arrs, k, block),
                dst_ref=_block_rows(arrs, k, block),
                send_sem=send_sems.at[rel], recv_sem=recv_sems.at[rel], device_id=to, device_id_type=MESH)
                for k in pieces]

        def whole(rel):
            grp = _shard_group(s_ref, pieces)
            return pltpu.make_async_remote_copy(src_ref=grp, dst_ref=grp, send_sem=send_sems.at[rel],
                                                recv_sem=recv_sems.at[rel], device_id=me, device_id_type=MESH)

        mine = [pltpu.make_async_copy(_shard_piece(s_ref, k), _block_rows(arrs, k, me), local_sem) for k in pieces]
        for cp in mine:
            cp.start()
        for cp in copies(0, me, sib, True):
            cp.start()
        for j, chip in enumerate(chips):
            for cp in copies(1 + j, me, (*chip, c), True):
                cp.start()
        for j, chip in enumerate(chips):
            whole(1 + j).wait_recv()
            for cp in copies(4 + j, (*chip, c), sib, False):
                cp.start()
        whole(0).wait_recv()
        for j in range(3):
            whole(4 + j).wait_recv()
        for rel in range(7):
            whole(rel).wait_send()
        grp = _shard_group(s_ref, pieces)
        pltpu.make_async_copy(grp, grp, local_sem).wait()

    hbm = pl.BlockSpec(memory_space=pl.ANY)
    return pl.pallas_call(
        body, in_specs=[hbm], out_specs=hbm, out_shape=SDS((3, F, D), bf16),
        scratch_shapes=[pltpu.SemaphoreType.DMA((7,)), pltpu.SemaphoreType.DMA((7,)), pltpu.SemaphoreType.DMA],
        compiler_params=pltpu.CompilerParams(has_side_effects=True),
        name="all_gather_ffn1")(shard)


HBM_SPEC = pl.BlockSpec(memory_space=pltpu.HBM)
SEM_SPEC = pl.BlockSpec(memory_space=pltpu.SEMAPHORE)
ANY_SPEC = pl.BlockSpec(memory_space=pl.ANY)
SPLIT_EFFECT = pltpu.SideEffectType.DATAFLOW_SIDE_EFFECTING


def _in_hbm(a):
    return pltpu.with_memory_space_constraint(a, pltpu.HBM)


def _hbm_like(a):
    return pltpu.HBM(a.shape, a.dtype)


def _place_own_rows(shard):
    pieces = MIX_PIECES + F2_PIECES

    def body(s_ref, wi_ref, wo_ref, w2_ref, buf, sems):
        x, y, c = _position()
        arrs = _weight_pieces(wi_ref=wi_ref, wo_ref=wo_ref, w2_ref=w2_ref)
        grp = _shard_group(s_ref, pieces)
        load = pltpu.make_async_copy(grp, buf, sems.at[0])
        load.start()
        load.wait()
        base = PIECE_OFF[pieces[0]]
        for k in pieces:
            pltpu.make_async_copy(buf.at[pl.ds(PIECE_OFF[k] - base, PIECE_ROWS[k]), :],
                                  _block_rows(arrs, k, (x, y, c)), sems.at[1]).start()
        pltpu.make_async_copy(grp, buf, sems.at[1]).wait()

    return pl.pallas_call(
        body, in_specs=[ANY_SPEC], out_specs=[ANY_SPEC] * 3,
        out_shape=(SDS((DIN, D), bf16), SDS((DMIX, D), bf16), SDS((3, F, D), bf16)),
        scratch_shapes=[pltpu.VMEM((_group_rows(pieces), D), bf16), pltpu.SemaphoreType.DMA((2,))],
        name="place_own_rows")(shard)


def _xor_peer(x, y, c, k):
    return (x ^ (k >> 2), y ^ ((k >> 1) & 1), c ^ (k & 1))


def _gather_rest_start(shard, wi, wo, w2, w1):
    def body(s_ref, wi_ref, wo_ref, w2_ref, w1_ref,
             ssem_m, rsem_m, ssem_f, rsem_f0, rsem_f, s_o, wi_o, wo_o, w2_o, w1_o):
        x, y, c = _position()
        me, sib = (x, y, c), (x, y, 1 - c)
        chips = [(1 - x, y), (x, 1 - y), (1 - x, 1 - y)]
        arrs = _weight_pieces(wi_ref=wi_ref, wo_ref=wo_ref, w2_ref=w2_ref)
        for k in range(1, NDEV):
            for p in MIX_PIECES:
                pltpu.make_async_remote_copy(
                    src_ref=_shard_piece(s_ref, p), dst_ref=_block_rows(arrs, p, me), send_sem=ssem_m.at[k - 1],
                    recv_sem=rsem_m.at[k - 1], device_id=_xor_peer(x, y, c, k), device_id_type=MESH).start()
        for p in F2_PIECES:
            pltpu.make_async_remote_copy(
                src_ref=_shard_piece(s_ref, p), dst_ref=_block_rows(arrs, p, me), send_sem=ssem_f.at[0],
                recv_sem=rsem_f0, device_id=sib, device_id_type=MESH).start()
        for j, chip in enumerate(chips):
            for p in F2_PIECES:
                pltpu.make_async_remote_copy(
                    src_ref=_shard_piece(s_ref, p), dst_ref=_block_rows(arrs, p, me), send_sem=ssem_f.at[1 + j],
                    recv_sem=rsem_f.at[j], device_id=(*chip, c), device_id_type=MESH).start()

    dma = pltpu.SemaphoreType.DMA
    return pl.pallas_call(
        body, name="gather_rest_start",
        out_shape=(dma((7,)), dma((7,)), dma((4,)), dma(()), dma((3,)),
                   _hbm_like(shard), _hbm_like(wi), _hbm_like(wo), _hbm_like(w2), _hbm_like(w1)),
        in_specs=(HBM_SPEC,) * 5, out_specs=(SEM_SPEC,) * 5 + (HBM_SPEC,) * 5,
        input_output_aliases={0: 5, 1: 6, 2: 7, 3: 8, 4: 9},
        compiler_params=pltpu.CompilerParams(has_side_effects=SPLIT_EFFECT),
    )(_in_hbm(shard), _in_hbm(wi), _in_hbm(wo), _in_hbm(w2), _in_hbm(w1))


def _gather_mix_wait(ssem_m, rsem_m, shard, wi, wo, after):
    def body(s_ref, wi_ref, wo_ref, ssem, rsem, after_ref, s_o, wi_o, wo_o):
        x, y, c = _position()
        grp = _shard_group(s_ref, MIX_PIECES)
        for k in range(NDEV - 1):
            d = pltpu.make_async_remote_copy(src_ref=grp, dst_ref=grp, send_sem=ssem.at[k], recv_sem=rsem.at[k],
                                             device_id=(x, y, c), device_id_type=MESH)
            d.wait_recv()
            d.wait_send()

    return pl.pallas_call(
        body, name="gather_mix_wait", out_shape=(_hbm_like(shard), _hbm_like(wi), _hbm_like(wo)),
        in_specs=(HBM_SPEC, HBM_SPEC, HBM_SPEC, SEM_SPEC, SEM_SPEC, ANY_SPEC), out_specs=(HBM_SPEC,) * 3,
        input_output_aliases={0: 0, 1: 1, 2: 2},
        compiler_params=pltpu.CompilerParams(has_side_effects=SPLIT_EFFECT),
    )(shard, wi, wo, ssem_m, rsem_m, after)


def _gather_ffn2_pass_on(rsem_f, w2, wo, after):
    def body(w2_ref, wo_ref, rsem, after_ref, fsend, frecv, w2_o, wo_o):
        x, y, c = _position()
        sib = (x, y, 1 - c)
        chips = [(1 - x, y), (x, 1 - y), (1 - x, 1 - y)]
        arrs = _weight_pieces(w2_ref=w2_ref)
        three = w2_ref.at[0, pl.ds(0, _group_rows(F2_PIECES)), :]
        for j, chip in enumerate(chips):
            pltpu.make_async_remote_copy(src_ref=three, dst_ref=three, send_sem=fsend.at[j], recv_sem=rsem.at[j],
                                         device_id=(x, y, c), device_id_type=MESH).wait_recv()
            for p in F2_PIECES:
                rows = _block_rows(arrs, p, (*chip, c))
                pltpu.make_async_remote_copy(src_ref=rows, dst_ref=rows, send_sem=fsend.at[j], recv_sem=frecv.at[j],
                                             device_id=sib, device_id_type=MESH).start()

    dma = pltpu.SemaphoreType.DMA
    return pl.pallas_call(
        body, name="gather_ffn2_pass_on", out_shape=(dma((3,)), dma((3,)), _hbm_like(w2), _hbm_like(wo)),
        in_specs=(HBM_SPEC, HBM_SPEC, SEM_SPEC, ANY_SPEC), out_specs=(SEM_SPEC, SEM_SPEC, HBM_SPEC, HBM_SPEC),
        input_output_aliases={0: 2, 1: 3},
        compiler_params=pltpu.CompilerParams(has_side_effects=SPLIT_EFFECT),
    )(w2, wo, rsem_f, after)


def _gather_ffn2_wait(ssem_f, rsem_f0, fsend, frecv, shard, w2, after):
    def body(s_ref, w2_ref, ssem, rsem0, fs, fr, after_ref, w2_o):
        x, y, c = _position()
        grp = _shard_group(s_ref, F2_PIECES)

        def waiter(send_sem, recv_sem):
            return pltpu.make_async_remote_copy(src_ref=grp, dst_ref=grp, send_sem=send_sem, recv_sem=recv_sem,
                                                device_id=(x, y, c), device_id_type=MESH)

        waiter(ssem.at[0], rsem0).wait_recv()
        for j in range(3):
            waiter(fs.at[j], fr.at[j]).wait_recv()
        for rel in range(4):
            waiter(ssem.at[rel], rsem0).wait_send()
        for j in range(3):
            waiter(fs.at[j], fr.at[j]).wait_send()

    return pl.pallas_call(
        body, name="gather_ffn2_wait", out_shape=_hbm_like(w2),
        in_specs=(HBM_SPEC, HBM_SPEC, SEM_SPEC, SEM_SPEC, SEM_SPEC, SEM_SPEC, ANY_SPEC), out_specs=HBM_SPEC,
        input_output_aliases={1: 0},
        compiler_params=pltpu.CompilerParams(has_side_effects=SPLIT_EFFECT),
    )(shard, w2, ssem_f, rsem_f0, fsend, frecv, after)


class _GatheredWeights(_LocalWeights):
    def __init__(self, shard):
        w1 = _all_gather_ffn1(shard)
        wi, wo, w2 = _place_own_rows(shard)
        (self.ssem_m, self.rsem_m, self.ssem_f, self.rsem_f0, self.rsem_f,
         self.shard, self.wi, self.wo, self.w2_part, self.w1) = _gather_rest_start(shard, wi, wo, w2, w1)

    def mix(self, after):
        self.shard, wint, wout = _gather_mix_wait(self.ssem_m, self.rsem_m, self.shard, self.wi, self.wo, after)
        return wint, wout

    def before_out_proj(self, wout, after):
        self.fsend, self.frecv, self.w2_part, wout = _gather_ffn2_pass_on(self.rsem_f, self.w2_part, wout, after)
        return wout

    def ffn2(self, after):
        return _gather_ffn2_wait(self.ssem_f, self.rsem_f0, self.fsend, self.frecv, self.shard, self.w2_part, after)

    def mix_ffn2_grads_ready(self, dwint, dwout, dw2, dh2):
        rx1 = lax.empty((4, RSA_ROWS, D), bf16)
        self.sa, self.ra, dwint, dwout, dw2, rx1, dh2 = _rsa_level1_start(dwint, dwout, dw2, rx1, dh2)
        self.level1 = (dwint, dwout, dw2, rx1)
        return dh2

    def before_ffn1_bwd(self, dx1b):
        dwint, dwout, dw2, rx1 = _rsa_level1_wait(self.sa, self.ra, *self.level1, dx1b)
        tx, self.acc = _rsa_chip_sums(dwint, dwout, dw2, rx1)
        rx2 = lax.empty((3, RSA_ROWS, D), bf16)
        self.sb, self.rb, self.tx, self.rx2, dx1b = _rsa_level2_start(tx, rx2, dx1b)
        return dx1b

    def mix_ffn2_grads_total(self, after):
        rx2 = _rsa_level2_wait(self.sb, self.rb, self.tx, self.rx2, after)
        return _rsa_total(self.acc, rx2)


RS_CHUNK = 176


def _reduce_scatter_ffn1(dw1, small_packed):
    pieces = G1_PIECES
    nrows = _group_rows(pieces)
    nchunk = nrows // RS_CHUNK

    def body(d1_ref, p_ref, red_ref, rx1_ref, rx2_ref, tot_ref,
             own_buf, rx_buf, tx_buf, acc, sa, ra, sb, rb, lsem, pair, chips, small_send, small_recv):
        x, y, c = _position()
        me, sib = (x, y, c), (x, y, 1 - c)
        rel_chips = [(x, y), (1 - x, y), (x, 1 - y), (1 - x, 1 - y)]
        srcs = _weight_pieces(w1_ref=d1_ref)

        my_chip = 2 * x + y
        pair[c] = p_ref[...]
        swap = pltpu.make_async_remote_copy(
            src_ref=p_ref, dst_ref=pair.at[c], send_sem=small_send.at[0], recv_sem=small_recv.at[0],
            device_id=sib, device_id_type=MESH)
        swap.start()
        small = [pltpu.make_async_remote_copy(
            src_ref=chips.at[my_chip], dst_ref=chips.at[my_chip], send_sem=small_send.at[j], recv_sem=small_recv.at[j],
            device_id=(*rel_chips[j], c), device_id_type=MESH) for j in (1, 2, 3)]

        def piece(k, dev):
            r = PIECE_ROWS[k]
            return srcs[k].at[pl.ds(pl.multiple_of(_dev_index(*dev) * r, 16), r), :]

        def packed(ref, k):
            return ref.at[pl.ds(PIECE_OFF[k], PIECE_ROWS[k]), :]

        for j, chip in enumerate(rel_chips):
            for k in pieces:
                pltpu.make_async_remote_copy(
                    src_ref=piece(k, (*chip, 1 - c)), dst_ref=packed(rx1_ref.at[j], k),
                    send_sem=sa.at[j], recv_sem=ra.at[j], device_id=sib, device_id_type=MESH).start()

        def wait_a(j):
            return pltpu.make_async_remote_copy(src_ref=rx1_ref.at[j], dst_ref=rx1_ref.at[j], send_sem=sa.at[j],
                                                recv_sem=ra.at[j], device_id=me, device_id_type=MESH)

        def ici(j):
            return pltpu.make_async_remote_copy(
                src_ref=tx_buf.at[j - 1], dst_ref=rx2_ref.at[j - 1], send_sem=sb.at[j - 1], recv_sem=rb.at[j - 1],
                device_id=(*rel_chips[j], c), device_id_type=MESH)

        swap.wait_recv()
        chips[my_chip] = pair[0] + pair[1]
        for cp in small:
            cp.start()

        for j in (1, 2, 3, 0):
            loads = [pltpu.make_async_copy(piece(k, (*rel_chips[j], c)), packed(own_buf, k), lsem)
                     for k in pieces]
            for cp in loads:
                cp.start()
            wait_a(j).wait_recv()
            got = pltpu.make_async_copy(rx1_ref.at[j], rx_buf, lsem)
            got.start()
            pltpu.make_async_copy(rx_buf, rx_buf, lsem).wait()
            got.wait()

            def add(i, carry, j=j):
                rows = pl.ds(pl.multiple_of(i * RS_CHUNK, 16), RS_CHUNK)
                tot = own_buf[rows, :].astype(f32) + rx_buf[rows, :].astype(f32)
                if j == 0:
                    acc[rows, :] = tot
                else:
                    tx_buf[j - 1, rows, :] = tot.astype(bf16)
                return carry

            lax.fori_loop(0, nchunk, add, 0)
            if j != 0:
                ici(j).start()

        for j in (1, 2, 3):
            ici(j).wait_recv()
            got = pltpu.make_async_copy(rx2_ref.at[j - 1], rx_buf, lsem)
            got.start()
            got.wait()

            def add2(i, carry):
                rows = pl.ds(pl.multiple_of(i * RS_CHUNK, 16), RS_CHUNK)
                acc[rows, :] += rx_buf[rows, :].astype(f32)
                return carry

            lax.fori_loop(0, nchunk, add2, 0)
        out = pltpu.make_async_copy(acc, red_ref, lsem)
        out.start()
        out.wait()
        for cp in small:
            cp.wait_recv()
        tot = (chips[0] + chips[1]) + (chips[2] + chips[3])
        tot_ref[...] = tot
        loss = jnp.sum(tot[LOSS_ROW:LOSS_ROW + 1, :], axis=-1, keepdims=True)
        tot_ref[LOSS_ROW:LOSS_ROW + 1, :] = jnp.broadcast_to(loss, (1, 128))
        for j in range(4):
            wait_a(j).wait_send()
        for j in (1, 2, 3):
            ici(j).wait_send()
        swap.wait_send()
        for cp in small:
            cp.wait_send()

    hbm = pl.BlockSpec(memory_space=pl.ANY)
    vm = pl.BlockSpec(memory_space=pltpu.VMEM)
    red, _, _, small_tot = pl.pallas_call(
        body, in_specs=[hbm, vm], out_specs=[hbm] * 3 + [vm],
        out_shape=(SDS((nrows, D), f32), SDS((4, nrows, D), bf16), SDS((3, nrows, D), bf16),
                   SDS((SMALL_ROWS, 128), f32)),
        scratch_shapes=[pltpu.VMEM((nrows, D), bf16), pltpu.VMEM((nrows, D), bf16),
                        pltpu.VMEM((3, nrows, D), bf16), pltpu.VMEM((nrows, D), f32),
                        pltpu.SemaphoreType.DMA((4,)), pltpu.SemaphoreType.DMA((4,)),
                        pltpu.SemaphoreType.DMA((3,)), pltpu.SemaphoreType.DMA((3,)), pltpu.SemaphoreType.DMA,
                        pltpu.VMEM((2, SMALL_ROWS, 128), f32), pltpu.VMEM((4, SMALL_ROWS, 128), f32),
                        pltpu.SemaphoreType.DMA((4,)), pltpu.SemaphoreType.DMA((4,))],
        compiler_params=pltpu.CompilerParams(has_side_effects=True, vmem_limit_bytes=VMEM_LIMIT_V7X),
        name="reduce_scatter_ffn1")(dw1, small_packed)
    return red, small_tot


RSA_PIECES = MIX_PIECES + F2_PIECES
RSA_ROWS = _group_rows(RSA_PIECES)
RSA_OFF = {k: PIECE_OFF[k] - PIECE_OFF[RSA_PIECES[0]] for k in RSA_PIECES}
RSA_BLOCK = 192


def _rsa_rows(ref, k):
    return ref.at[pl.ds(RSA_OFF[k], PIECE_ROWS[k]), :]


def _rsa_level1_start(dwint, dwout, dw2, rx1, thru):
    def body(di_ref, do_ref, d2_ref, rx1_ref, thru_ref, sa, ra, di_o, do_o, d2_o, rx1_o, thru_o):
        x, y, c = _position()
        srcs = _weight_pieces(wi_ref=di_ref, wo_ref=do_ref, w2_ref=d2_ref)
        for j, chip in enumerate([(x, y), (1 - x, y), (x, 1 - y), (1 - x, 1 - y)]):
            for k in RSA_PIECES:
                pltpu.make_async_remote_copy(
                    src_ref=_block_rows(srcs, k, (*chip, 1 - c)), dst_ref=_rsa_rows(rx1_ref.at[j], k),
                    send_sem=sa.at[j], recv_sem=ra.at[j], device_id=(x, y, 1 - c), device_id_type=MESH).start()

    dma = pltpu.SemaphoreType.DMA
    arrs = (dwint, dwout, dw2, rx1, thru)
    return pl.pallas_call(
        body, name="rsa_level1_start", out_shape=(dma((4,)), dma((4,))) + tuple(_hbm_like(a) for a in arrs),
        in_specs=(HBM_SPEC,) * 5, out_specs=(SEM_SPEC,) * 2 + (HBM_SPEC,) * 5,
        input_output_aliases={0: 2, 1: 3, 2: 4, 3: 5, 4: 6},
        compiler_params=pltpu.CompilerParams(has_side_effects=SPLIT_EFFECT),
    )(*[_in_hbm(a) for a in arrs])


def _rsa_level1_wait(sa, ra, dwint, dwout, dw2, rx1, after):
    def body(di_ref, do_ref, d2_ref, rx1_ref, sa_ref, ra_ref, after_ref, di_o, do_o, d2_o, rx1_o):
        x, y, c = _position()
        for j in range(4):
            d = pltpu.make_async_remote_copy(src_ref=rx1_ref.at[j], dst_ref=rx1_ref.at[j], send_sem=sa_ref.at[j],
                                             recv_sem=ra_ref.at[j], device_id=(x, y, c), device_id_type=MESH)
            d.wait_recv()
            d.wait_send()

    arrs = (dwint, dwout, dw2, rx1)
    return pl.pallas_call(
        body, name="rsa_level1_wait", out_shape=tuple(_hbm_like(a) for a in arrs),
        in_specs=(HBM_SPEC,) * 4 + (SEM_SPEC, SEM_SPEC, ANY_SPEC), out_specs=(HBM_SPEC,) * 4,
        input_output_aliases={0: 0, 1: 1, 2: 2, 3: 3},
        compiler_params=pltpu.CompilerParams(has_side_effects=SPLIT_EFFECT),
    )(*arrs, sa, ra, after)


def _rsa_chip_sums(dwint, dwout, dw2, rx1):
    nblk = RSA_ROWS // RSA_BLOCK

    def body(di_ref, do_ref, d2_ref, rx1_ref, tx_ref, acc_ref, own_buf, rx_buf, tx_buf, acc_buf, lsems):
        x, y, c = _position()
        srcs = _weight_pieces(wi_ref=di_ref, wo_ref=do_ref, w2_ref=d2_ref)
        for j, chip in enumerate([(x, y), (1 - x, y), (x, 1 - y), (1 - x, 1 - y)]):
            loads = [pltpu.make_async_copy(_block_rows(srcs, k, (*chip, c)), _rsa_rows(own_buf, k), lsems.at[0])
                     for k in RSA_PIECES]
            got = pltpu.make_async_copy(rx1_ref.at[j], rx_buf, lsems.at[1])
            for cp in loads + [got]:
                cp.start()
            pltpu.make_async_copy(rx_buf, rx_buf, lsems.at[0]).wait()
            got.wait()

            def add(i, carry, j=j):
                rows = pl.ds(pl.multiple_of(i * RSA_BLOCK, 16), RSA_BLOCK)
                tot = own_buf[rows, :].astype(f32) + rx_buf[rows, :].astype(f32)
                if j == 0:
                    acc_buf[rows, :] = tot
                else:
                    tx_buf[rows, :] = tot.astype(bf16)
                return carry

            lax.fori_loop(0, nblk, add, 0)
            out = (pltpu.make_async_copy(acc_buf, acc_ref, lsems.at[2]) if j == 0
                   else pltpu.make_async_copy(tx_buf, tx_ref.at[j - 1], lsems.at[2]))
            out.start()
            out.wait()

    return pl.pallas_call(
        body, in_specs=[ANY_SPEC] * 4, out_specs=[ANY_SPEC] * 2,
        out_shape=(SDS((3, RSA_ROWS, D), bf16), SDS((RSA_ROWS, D), f32)),
        scratch_shapes=[pltpu.VMEM((RSA_ROWS, D), bf16), pltpu.VMEM((RSA_ROWS, D), bf16),
                        pltpu.VMEM((RSA_ROWS, D), bf16), pltpu.VMEM((RSA_ROWS, D), f32),
                        pltpu.SemaphoreType.DMA((3,))],
        compiler_params=_cparams(None, VMEM_LIMIT_V7X), name="rsa_chip_sums")(dwint, dwout, dw2, rx1)


def _rsa_level2_start(tx, rx2, thru):
    def body(tx_ref, rx2_ref, thru_ref, sb, rb, tx_o, rx2_o, thru_o):
        x, y, c = _position()
        for j, chip in enumerate([(1 - x, y), (x, 1 - y), (1 - x, 1 - y)]):
            pltpu.make_async_remote_copy(src_ref=tx_ref.at[j], dst_ref=rx2_ref.at[j], send_sem=sb.at[j],
                                         recv_sem=rb.at[j], device_id=(*chip, c), device_id_type=MESH).start()

    dma = pltpu.SemaphoreType.DMA
    arrs = (tx, rx2, thru)
    return pl.pallas_call(
        body, name="rsa_level2_start", out_shape=(dma((3,)), dma((3,))) + tuple(_hbm_like(a) for a in arrs),
        in_specs=(HBM_SPEC,) * 3, out_specs=(SEM_SPEC,) * 2 + (HBM_SPEC,) * 3,
        input_output_aliases={0: 2, 1: 3, 2: 4},
        compiler_params=pltpu.CompilerParams(has_side_effects=SPLIT_EFFECT),
    )(*[_in_hbm(a) for a in arrs])


def _rsa_level2_wait(sb, rb, tx, rx2, after):
    def body(tx_ref, rx2_ref, sb_ref, rb_ref, after_ref, rx2_o):
        x, y, c = _position()
        for j in range(3):
            d = pltpu.make_async_remote_copy(src_ref=tx_ref.at[j], dst_ref=rx2_ref.at[j], send_sem=sb_ref.at[j],
                                             recv_sem=rb_ref.at[j], device_id=(x, y, c), device_id_type=MESH)
            d.wait_recv()
            d.wait_send()

    return pl.pallas_call(
        body, name="rsa_level2_wait", out_shape=_hbm_like(rx2),
        in_specs=(HBM_SPEC, HBM_SPEC, SEM_SPEC, SEM_SPEC, ANY_SPEC), out_specs=HBM_SPEC,
        input_output_aliases={1: 0},
        compiler_params=pltpu.CompilerParams(has_side_effects=SPLIT_EFFECT),
    )(tx, rx2, sb, rb, after)


def _rsa_total(acc, rx2):
    def body(a_ref, r_ref, o_ref):
        o_ref[...] = ((a_ref[...] + r_ref[0].astype(f32)) + r_ref[1].astype(f32)) + r_ref[2].astype(f32)

    return pl.pallas_call(
        body, grid=(RSA_ROWS // RSA_BLOCK,),
        in_specs=[pl.BlockSpec((RSA_BLOCK, D), lambda i: (i, 0)), pl.BlockSpec((3, RSA_BLOCK, D), lambda i: (0, i, 0))],
        out_specs=pl.BlockSpec((RSA_BLOCK, D), lambda i: (i, 0)),
        out_shape=SDS((RSA_ROWS, D), f32), name="rsa_total")(acc, rx2)


def _adamw_math(w, g, m, v):
    m = ADAM_B1 * m + (1.0 - ADAM_B1) * g
    v = ADAM_B2 * v + (1.0 - ADAM_B2) * (g * g)
    m_hat = m / (1.0 - ADAM_B1 ** ADAM_STEP)
    v_hat = v / (1.0 - ADAM_B2 ** ADAM_STEP)
    delta = -ADAM_LR * (m_hat / (jnp.sqrt(v_hat) + ADAM_EPS) + ADAM_WD * w)
    return delta, m, v


def _adamw_big(ws, ms, vs, red1, red_rest):
    npiece = len(BIG)
    rmax = max(PIECE_ROWS)

    def body(*refs):
        ins = (refs[0:npiece], refs[npiece:2 * npiece], refs[2 * npiece:3 * npiece])
        red1_ref, rest_ref = refs[3 * npiece:3 * npiece + 2]
        out_refs = refs[3 * npiece + 2:7 * npiece + 2]
        inb, outb, in_sems, out_sems = refs[7 * npiece + 2:]

        def grad_rows(k):
            if k in G1_PIECES:
                return red1_ref.at[pl.ds(PIECE_OFF[k], PIECE_ROWS[k]), :]
            return _rsa_rows(rest_ref, k)

        def loads(k):
            s, r = k % 2, PIECE_ROWS[k]
            cps = [pltpu.make_async_copy(ins[q][k].at[0], inb.at[s, q, pl.ds(0, r), :], in_sems.at[4 * s + q])
                   for q in range(3)]
            cps.append(pltpu.make_async_copy(grad_rows(k), inb.at[s, 3, pl.ds(0, r), :], in_sems.at[4 * s + 3]))
            return cps

        def stores(k):
            s, r = k % 2, PIECE_ROWS[k]
            return [pltpu.make_async_copy(outb.at[s, q, pl.ds(0, r), :], out_refs[q * npiece + k].at[0],
                                          out_sems.at[4 * s + q]) for q in range(4)]

        for cp in loads(0):
            cp.start()
        for k in range(npiece):
            s, r = k % 2, PIECE_ROWS[k]
            if k + 1 < npiece:
                for cp in loads(k + 1):
                    cp.start()
            for cp in loads(k):
                cp.wait()
            if k >= 2:
                for cp in stores(k - 2):
                    cp.wait()
            g = inb[s, 3, 0:r, :]
            d, nm, nv = _adamw_math(inb[s, 0, 0:r, :], g, inb[s, 1, 0:r, :], inb[s, 2, 0:r, :])
            outb[s, 0, 0:r, :] = g
            outb[s, 1, 0:r, :] = d
            outb[s, 2, 0:r, :] = nm
            outb[s, 3, 0:r, :] = nv
            for cp in stores(k):
                cp.start()
        for k in (npiece - 2, npiece - 1):
            for cp in stores(k):
                cp.wait()

    hbm = pl.BlockSpec(memory_space=pl.ANY)
    outs = pl.pallas_call(
        body, in_specs=[hbm] * (3 * npiece + 2), out_specs=[hbm] * (4 * npiece),
        out_shape=tuple(SDS(w.shape, f32) for _ in range(4) for w in ws),
        scratch_shapes=[pltpu.VMEM((2, 4, rmax, D), f32), pltpu.VMEM((2, 4, rmax, D), f32),
                        pltpu.SemaphoreType.DMA((8,)), pltpu.SemaphoreType.DMA((8,))],
        compiler_params=_cparams(None, VMEM_LIMIT_V7X), name="adamw_big")(*ws, *ms, *vs, red1, red_rest)
    return [list(outs[q * npiece:(q + 1) * npiece]) for q in range(4)]


def _adamw_small(w, m, v, g, name):
    def body(w_ref, m_ref, v_ref, g_ref, d_ref, nm_ref, nv_ref):
        d, nm, nv = _adamw_math(w_ref[...], g_ref[...], m_ref[...], v_ref[...])
        d_ref[...] = d
        nm_ref[...] = nm
        nv_ref[...] = nv

    return pl.pallas_call(
        body, out_shape=tuple(SDS(w.shape, f32) for _ in range(3)), name=name)(w, m, v, g)


WEIGHTS = ("ffn1_norm", "ffn1_w_gate", "ffn1_w_up", "ffn1_w_down", "mix_norm", "w_in", "q_norm", "k_norm",
           "attn_sinks", "rel_bias", "pool_w", "pool_scale", "w_out", "ffn2_norm", "ffn2_w_gate", "ffn2_w_up",
           "ffn2_w_down")
BIG = (("ffn1_w_gate", True), ("ffn1_w_up", True), ("ffn1_w_down", False), ("w_in", True), ("w_out", False),
       ("ffn2_w_gate", True), ("ffn2_w_up", True), ("ffn2_w_down", False))


def kernel(x, ffn1_norm, ffn1_w_gate, ffn1_w_up, ffn1_w_down, mix_norm, w_in, q_norm, k_norm, attn_sinks, rel_bias, pool_w, pool_scale, w_out, ffn2_norm, ffn2_w_gate, ffn2_w_up, ffn2_w_down, loss_target, m_ffn1_norm, m_ffn1_w_gate, m_ffn1_w_up, m_ffn1_w_down, m_mix_norm, m_w_in, m_q_norm, m_k_norm, m_attn_sinks, m_rel_bias, m_pool_w, m_pool_scale, m_w_out, m_ffn2_norm, m_ffn2_w_gate, m_ffn2_w_up, m_ffn2_w_down, v_ffn1_norm, v_ffn1_w_gate, v_ffn1_w_up, v_ffn1_w_down, v_mix_norm, v_w_in, v_q_norm, v_k_norm, v_attn_sinks, v_rel_bias, v_pool_w, v_pool_scale, v_w_out, v_ffn2_norm, v_ffn2_w_gate, v_ffn2_w_up, v_ffn2_w_down):
    args = dict(locals())
    w = {n: args[n] for n in WEIGHTS}
    m = {n: args["m_" + n] for n in WEIGHTS}
    v = {n: args["v_" + n] for n in WEIGHTS}

    as_rows = lambda a, tr: jnp.swapaxes(a, 1, 2) if tr else a
    shard = jnp.concatenate([as_rows(w[n], tr)[0].astype(bf16) for n, tr in BIG], axis=0)
    exchanges = _GatheredWeights(shard)
    gx, (dw1, _, _, _), small = _local_step(
        x[0], loss_target[0], exchanges, ffn1_norm, mix_norm, ffn2_norm, q_norm, k_norm, attn_sinks,
        rel_bias, pool_w[0], pool_scale)

    red1, small_tot = _reduce_scatter_ffn1(dw1, _pack_small(small))
    red_rest = exchanges.mix_ffn2_grads_total(red1)

    grads, deltas, new_m, new_v = {}, {}, {}, {}
    big_out = _adamw_big(*[[as_rows(t[n], tr) for n, tr in BIG] for t in (w, m, v)], red1, red_rest)
    for k, (n, tr) in enumerate(BIG):
        grads[n], deltas[n], new_m[n], new_v[n] = [as_rows(o[k], tr) for o in big_out]
    small_names = [n for n in SMALL_NAMES if n != "loss"]
    ds, nms, nvs = _adamw_small(_pack_small({n: w[n] for n in small_names}), _pack_small({n: m[n] for n in small_names}),
                                _pack_small({n: v[n] for n in small_names}), small_tot, "adamw_small")
    for n in small_names:
        grads[n] = _unpack_small(small_tot, n)
        deltas[n], new_m[n], new_v[n] = _unpack_small(ds, n), _unpack_small(nms, n), _unpack_small(nvs, n)
    loss = small_tot[LOSS_ROW, 0]
    return (loss, gx[None], *[grads[n] for n in WEIGHTS], *[deltas[n] for n in WEIGHTS],
            *[new_m[n] for n in WEIGHTS], *[new_v[n] for n in WEIGHTS])
```

```python
import functools

import jax
import jax.numpy as jnp
import numpy as np
from jax import lax
from jax.experimental import pallas as pl
from jax.experimental.pallas import tpu as pltpu

f32, bf16, i32 = jnp.float32, jnp.bfloat16, jnp.int32
SDS = jax.ShapeDtypeStruct

D = 1024
F = 2816
HD = 64
NH = 8
NKV = 2
GQA = NH // NKV
DATTN = NH * HD
DKV = NKV * HD
DPOOL = 512
POOL_WINDOWS = (2, 4, 8, 16)
PGD = DPOOL // len(POOL_WINDOWS)
DIN = DATTN + 2 * DKV + DPOOL
DMIX = DATTN + DPOOL
BLK = 128
NBUCK = 32
MAX_DISTANCE = 128
EPS = 1e-6
NEG = -1e30
SCALE = HD ** -0.5

ADAM_LR, ADAM_B1, ADAM_B2, ADAM_EPS, ADAM_WD, ADAM_STEP = 0.001, 0.9, 0.999, 1e-08, 0.01, 10

NDEV = 8
FS = F // NDEV
INS = DIN // NDEV
OUTS = DMIX // NDEV
PIECE_ROWS = (FS, FS, FS, INS, OUTS, FS, FS, FS)
PIECE_OFF = tuple(int(v) for v in np.cumsum((0,) + PIECE_ROWS[:-1]))
PACK_ROWS = sum(PIECE_ROWS)

VMEM_LIMIT_V7X = 56 * 1024 * 1024

MESH = pl.DeviceIdType.MESH


def _cparams(sem=None, vmem=None):
    return pltpu.CompilerParams(dimension_semantics=sem, vmem_limit_bytes=vmem)


def _nt(a, b):
    return lax.dot_general(a, b, (((1,), (1,)), ((), ())), preferred_element_type=f32)


def _tn(a, b):
    return lax.dot_general(a, b, (((0,), (0,)), ((), ())), preferred_element_type=f32)


def _nn(a, b):
    return jnp.dot(a, b, preferred_element_type=f32)


def _sigmoid(x):
    return 1.0 / (1.0 + jnp.exp(-x))


def _norm_fwd(x, g, name):
    T = x.shape[0]
    tm = min(512, T)

    def body(x_ref, g_ref, h_ref):
        xv = x_ref[...]
        r = lax.rsqrt(jnp.mean(xv * xv, axis=-1, keepdims=True) + EPS)
        h_ref[...] = (xv * r * g_ref[...]).astype(bf16)

    return pl.pallas_call(
        body, grid=(T // tm,),
        in_specs=[pl.BlockSpec((tm, D), lambda i: (i, 0)), pl.BlockSpec((1, D), lambda i: (0, 0))],
        out_specs=pl.BlockSpec((tm, D), lambda i: (i, 0)),
        out_shape=SDS((T, D), bf16), name=name)(x, g)


def _norm_bwd(dh, x, g, dres, out_scale, name):
    T = x.shape[0]
    tm = min(512, T)

    def body(dh_ref, x_ref, g_ref, dr_ref, dx_ref, dxb_ref, dg_ref):
        i = pl.program_id(0)
        xv = x_ref[...]
        r = lax.rsqrt(jnp.mean(xv * xv, axis=-1, keepdims=True) + EPS)
        xh = xv * r
        dhv = dh_ref[...]
        dxh = dhv * g_ref[...]
        dx = dr_ref[...] + r * (dxh - xh * jnp.mean(dxh * xh, axis=-1, keepdims=True))
        dx_ref[...] = dx
        dxb_ref[...] = (out_scale * dx).astype(bf16)
        dg = jnp.sum(dhv * xh, axis=0, keepdims=True)

        @pl.when(i == 0)
        def _():
            dg_ref[...] = dg

        @pl.when(i > 0)
        def _():
            dg_ref[...] += dg

    tok = pl.BlockSpec((tm, D), lambda i: (i, 0))
    vec = pl.BlockSpec((1, D), lambda i: (0, 0))
    return pl.pallas_call(
        body, grid=(T // tm,),
        in_specs=[tok, tok, vec, tok], out_specs=[tok, tok, vec],
        out_shape=(SDS((T, D), f32), SDS((T, D), bf16), SDS((1, D), f32)),
        compiler_params=_cparams(("arbitrary",)), name=name)(dh, x, g, dres)


FFN_ROW_CHUNK = 256


def _ffn_tiles(T):
    return min(1024, T), 256


def _ffn_fwd(h, w, x, target, name):
    T = h.shape[0]
    tm, tf = _ffn_tiles(T)
    nf = F // tf
    with_loss = target is not None

    def body(*refs):
        if with_loss:
            h_ref, w_ref, x_hbm, t_hbm, xo_ref, g_ref, u_ref, dyb_ref, loss_ref, tbuf, sem = refs
        else:
            h_ref, w_ref, x_hbm, xo_ref, g_ref, u_ref, sem = refs
        fi = pl.program_id(0)

        @pl.when(fi == 0)
        def _():
            cp = pltpu.make_async_copy(x_hbm, xo_ref, sem)
            cp.start()
            cp.wait()

        wgu = w_ref[0:2].reshape(2 * tf, D)
        for r in range(0, T, tm):
            rows = slice(r, r + tm)
            gu = _nt(h_ref[rows, :], wgu)
            gate, up = gu[:, :tf], gu[:, tf:]
            act = gate * _sigmoid(gate) * up
            g_ref[0, rows, :] = gate.astype(bf16)
            u_ref[0, rows, :] = up.astype(bf16)
            xo_ref[rows, :] += _nn((0.5 * act).astype(bf16), w_ref[2])

        if with_loss:
            @pl.when(fi == nf - 1)
            def _():
                lanes = jnp.zeros((1, 128), f32)
                for r in range(0, T, tm):
                    rows = slice(r, r + tm)
                    cp = pltpu.make_async_copy(t_hbm.at[pl.ds(r, tm), :], tbuf, sem)
                    cp.start()
                    cp.wait()
                    e = xo_ref[rows, :] - tbuf[...]
                    dy = e * (1.0 / D)
                    xo_ref[rows, :] = dy
                    dyb_ref[rows, :] = (0.5 * dy).astype(bf16)
                    col = jnp.sum(e * e, axis=0, keepdims=True) * (0.5 / D)
                    for k in range(D // 128):
                        lanes = lanes + col[:, 128 * k:128 * (k + 1)]
                loss_ref[...] = lanes

    tok = pl.BlockSpec((T, D), lambda f: (0, 0))
    act_spec = pl.BlockSpec((1, T, tf), lambda f: (f, 0, 0))
    hbm = pl.BlockSpec(memory_space=pl.ANY)
    in_specs = [tok, pl.BlockSpec((3, tf, D), lambda f: (0, f, 0)), hbm]
    out_specs = [tok, act_spec, act_spec]
    out_shape = [SDS((T, D), f32), SDS((nf, T, tf), bf16), SDS((nf, T, tf), bf16)]
    scratch = [pltpu.SemaphoreType.DMA]
    args = [h, w, x]
    if with_loss:
        in_specs.append(hbm)
        args.append(target)
        out_specs += [tok, pl.BlockSpec((1, 128), lambda f: (0, 0))]
        out_shape += [SDS((T, D), bf16), SDS((1, 128), f32)]
        scratch = [pltpu.VMEM((tm, D), f32)] + scratch
    return pl.pallas_call(
        body, grid=(nf,), in_specs=in_specs, out_specs=out_specs, out_shape=tuple(out_shape), scratch_shapes=scratch,
        compiler_params=_cparams(("arbitrary",), VMEM_LIMIT_V7X), name=name)(*args)


def _ffn_bwd(dob, h, gate, up, w, name):
    T = h.shape[0]
    _, tf = _ffn_tiles(T)
    nf = F // tf

    def body(do_hbm, h_hbm, g_ref, u_ref, w_ref, dh_hbm, dw_ref, do_v, h_v, dh_acc, dgu_s, act_s, sems):
        fi = pl.program_id(0)

        @pl.when(fi == 0)
        def _():
            loads = [pltpu.make_async_copy(do_hbm, do_v, sems.at[0]), pltpu.make_async_copy(h_hbm, h_v, sems.at[1])]
            for cp in loads:
                cp.start()
            dh_acc[...] = jnp.zeros_like(dh_acc)
            for cp in loads:
                cp.wait()

        wgu = w_ref[0:2].reshape(2 * tf, D)
        for r in range(0, T, FFN_ROW_CHUNK):
            rows = slice(r, r + FFN_ROW_CHUNK)
            dov = do_v[rows, :]
            gv = g_ref[0, rows, :].astype(f32)
            uv = u_ref[0, rows, :].astype(f32)
            sg = _sigmoid(gv)
            sil = gv * sg
            dact = _nt(dov, w_ref[2])
            dup = dact * sil
            dgate = dact * uv * (sg * (1.0 + gv * (1.0 - sg)))
            dgu = jnp.concatenate([dgate.astype(bf16), dup.astype(bf16)], axis=1)
            dgu_s[rows, :] = dgu
            act_s[rows, :] = (sil * uv).astype(bf16)
            dh_acc[rows, :] += _nn(dgu, wgu)
        dw_ref[0:2] = _tn(dgu_s[...], h_v[...]).reshape(2, tf, D).astype(bf16)
        dw_ref[2] = _tn(act_s[...], do_v[...]).astype(bf16)

        @pl.when(fi == nf - 1)
        def _():
            out = pltpu.make_async_copy(dh_acc, dh_hbm, sems.at[0])
            out.start()
            out.wait()

    act_spec = pl.BlockSpec((1, T, tf), lambda f: (f, 0, 0))
    wspec = pl.BlockSpec((3, tf, D), lambda f: (0, f, 0))
    hbm = pl.BlockSpec(memory_space=pl.ANY)
    return pl.pallas_call(
        body, grid=(nf,),
        in_specs=[hbm, hbm, act_spec, act_spec, wspec],
        out_specs=[hbm, wspec],
        out_shape=(SDS((T, D), f32), SDS((3, F, D), bf16)),
        scratch_shapes=[pltpu.VMEM((T, D), bf16), pltpu.VMEM((T, D), bf16), pltpu.VMEM((T, D), f32),
                        pltpu.VMEM((T, 2 * tf), bf16), pltpu.VMEM((T, tf), bf16), pltpu.SemaphoreType.DMA((2,))],
        compiler_params=_cparams(("arbitrary",), VMEM_LIMIT_V7X), name=name)(dob, h, gate, up, w)


def _in_proj_fwd(h, wint, name):
    T = h.shape[0]
    tm = min(512, T)

    def body(h_ref, w_ref, z_ref):
        z_ref[...] = _nt(h_ref[...], w_ref[...])

    return pl.pallas_call(
        body, grid=(T // tm,),
        in_specs=[pl.BlockSpec((tm, D), lambda i: (i, 0)), pl.BlockSpec((DIN, D), lambda i: (0, 0))],
        out_specs=pl.BlockSpec((tm, DIN), lambda i: (i, 0)),
        out_shape=SDS((T, DIN), f32), name=name)(h, wint)


def _in_proj_bwd(dz, wint, h, name):
    T = h.shape[0]
    tm = min(512, T)
    nt = T // tm

    def body(dz_ref, w_ref, h_ref, dh_ref, dw_ref, acc):
        i = pl.program_id(0)
        dzb = dz_ref[...].astype(bf16)
        dh_ref[...] = _nn(dzb, w_ref[...])
        part = _tn(dzb, h_ref[...])

        @pl.when(i == 0)
        def _():
            acc[...] = part

        @pl.when(i > 0)
        def _():
            acc[...] += part

        @pl.when(i == nt - 1)
        def _():
            dw_ref[...] = acc[...].astype(bf16)

    wspec = pl.BlockSpec((DIN, D), lambda i: (0, 0))
    return pl.pallas_call(
        body, grid=(nt,),
        in_specs=[pl.BlockSpec((tm, DIN), lambda i: (i, 0)), wspec, pl.BlockSpec((tm, D), lambda i: (i, 0))],
        out_specs=[pl.BlockSpec((tm, D), lambda i: (i, 0)), wspec],
        out_shape=(SDS((T, D), f32), SDS((DIN, D), bf16)),
        scratch_shapes=[pltpu.VMEM((DIN, D), f32)],
        compiler_params=_cparams(("arbitrary",)), name=name)(dz, wint, h)


def _out_proj_fwd(ymix, wout, x, g, name):
    T = x.shape[0]
    tm = min(512, T)

    def body(y_ref, w_ref, x_ref, g_ref, o_ref, h_ref):
        o = x_ref[...] + _nn(y_ref[...], w_ref[...])
        o_ref[...] = o
        r = lax.rsqrt(jnp.mean(o * o, axis=-1, keepdims=True) + EPS)
        h_ref[...] = (o * r * g_ref[...]).astype(bf16)

    tok = pl.BlockSpec((tm, D), lambda i: (i, 0))
    return pl.pallas_call(
        body, grid=(T // tm,),
        in_specs=[pl.BlockSpec((tm, DMIX), lambda i: (i, 0)), pl.BlockSpec((DMIX, D), lambda i: (0, 0)), tok,
                  pl.BlockSpec((1, D), lambda i: (0, 0))],
        out_specs=[tok, tok], out_shape=(SDS((T, D), f32), SDS((T, D), bf16)), name=name)(ymix, wout, x, g)


def _out_proj_bwd(dxb, wout, ymix, name):
    T = dxb.shape[0]
    tm = min(512, T)
    nt = T // tm

    def body(dx_ref, w_ref, y_ref, dy_ref, dw_ref, acc):
        i = pl.program_id(0)
        dxv = dx_ref[...]
        dy_ref[...] = _nt(dxv, w_ref[...])
        part = _tn(y_ref[...], dxv)

        @pl.when(i == 0)
        def _():
            acc[...] = part

        @pl.when(i > 0)
        def _():
            acc[...] += part

        @pl.when(i == nt - 1)
        def _():
            dw_ref[...] = acc[...].astype(bf16)

    wspec = pl.BlockSpec((DMIX, D), lambda i: (0, 0))
    return pl.pallas_call(
        body, grid=(nt,),
        in_specs=[pl.BlockSpec((tm, D), lambda i: (i, 0)), wspec, pl.BlockSpec((tm, DMIX), lambda i: (i, 0))],
        out_specs=[pl.BlockSpec((tm, DMIX), lambda i: (i, 0)), wspec],
        out_shape=(SDS((T, DMIX), f32), SDS((DMIX, D), bf16)),
        scratch_shapes=[pltpu.VMEM((DMIX, D), f32)],
        compiler_params=_cparams(("arbitrary",)), name=name)(dxb, wout, ymix)


def _t5_bucket_table():
    ql = np.arange(BLK)[:, None]
    kl = np.arange(2 * BLK)[None, :]
    n = np.maximum(ql + BLK - kl, 0)
    max_exact = NBUCK // 2
    large = max_exact + (np.log(np.maximum(n, 1) / max_exact) / np.log(MAX_DISTANCE / max_exact)
                         * (NBUCK - max_exact)).astype(np.int32)
    large = np.minimum(large, NBUCK - 1)
    return np.where(n < max_exact, n, large).astype(np.int32)


def _fill_bias(bk_ref, rb_ref, bias_scr):
    bk = bk_ref[...]
    for h in range(NH):
        def step(b, acc, h=h):
            return acc + jnp.where(bk == b, rb_ref[b, h], 0.0)
        bias_scr[h] = lax.fori_loop(0, NBUCK, step, jnp.zeros((BLK, 2 * BLK), f32))


MIX_SUB = 4


class _Window:
    def __init__(self, zc_ref, zp_ref, n, s):
        self.blk = n * MIX_SUB + s
        self.first_in_step = s == 0
        self.cur = lambda a, b: zc_ref[s * BLK:(s + 1) * BLK, a:b]
        self.prev = (lambda a, b: zp_ref[:, a:b]) if s == 0 else (lambda a, b: zc_ref[(s - 1) * BLK:s * BLK, a:b])


def _attn_probs(win, kh, qg, kg, sk_ref, bias_scr):
    n = win.blk
    kc = DATTN + HD * kh
    vc = DATTN + DKV + HD * kh
    kx = jnp.concatenate([win.prev(kc, kc + HD), win.cur(kc, kc + HD)], axis=0)
    vx = jnp.concatenate([win.prev(vc, vc + HD), win.cur(vc, vc + HD)], axis=0)
    qx = jnp.concatenate([win.cur(HD * (GQA * kh + g), HD * (GQA * kh + g + 1)) for g in range(GQA)], axis=0)
    rq = lax.rsqrt(jnp.mean(qx * qx, axis=-1, keepdims=True) + EPS)
    rk = lax.rsqrt(jnp.mean(kx * kx, axis=-1, keepdims=True) + EPS)
    qhat, khat = qx * rq, kx * rk
    qnb, knb = (qhat * qg).astype(bf16), (khat * kg).astype(bf16)
    s = _nt(qnb, knb) * SCALE + bias_scr[GQA * kh:GQA * (kh + 1)].reshape(GQA * BLK, 2 * BLK)
    row = lax.broadcasted_iota(i32, (GQA * BLK, 2 * BLK), 0) & (BLK - 1)
    col = lax.broadcasted_iota(i32, (GQA * BLK, 2 * BLK), 1)
    mask = (col > row) & (col <= row + BLK) & ((col >= BLK) | (n > 0))
    s = jnp.where(mask, s, NEG)
    ridx = lax.broadcasted_iota(i32, (GQA * BLK, 1), 0)
    sink = jnp.full((GQA * BLK, 1), sk_ref[GQA * kh + GQA - 1], f32)
    for g in range(GQA - 2, -1, -1):
        sink = jnp.where(ridx < (g + 1) * BLK, sk_ref[GQA * kh + g], sink)
    m = jnp.maximum(jnp.max(s, axis=-1, keepdims=True), sink)
    e = jnp.exp(s - m)
    es = jnp.exp(sink - m)
    den = jnp.sum(e, axis=-1, keepdims=True) + es
    return dict(p=e / den, psink=es / den, qhat=qhat, khat=khat, rq=rq, rk=rk, qnb=qnb, knb=knb, vb=vx.astype(bf16))


def _pool_group(win, g, w):
    n = win.blk
    c0 = DATTN + 2 * DKV + PGD * g
    uc = win.cur(c0, c0 + PGD)
    up = jnp.where(n > 0, win.prev(c0, c0 + PGD), 0.0)
    ue = jnp.concatenate([up, uc], axis=0)
    hi = ue.astype(bf16)
    lo = (ue - hi.astype(f32)).astype(bf16)
    t = lax.broadcasted_iota(i32, (BLK, 2 * BLK), 0)
    s = lax.broadcasted_iota(i32, (BLK, 2 * BLK), 1)
    band = jnp.where((s <= t + BLK) & (s > t + BLK - w), 1.0, 0.0).astype(bf16)
    sm = _nn(band, hi) + _nn(band, lo)
    pos = n * BLK + lax.broadcasted_iota(i32, (BLK, 1), 0) + 1
    cnt = jnp.minimum(pos, w).astype(f32)
    return sm / cnt - uc, band, cnt


def _mix_fwd(z, qg, kg, sinks, relb, bucket, pool_w, pscale, name):
    T = z.shape[0]
    step_rows = MIX_SUB * BLK
    nsteps = T // step_rows

    def body(zc_ref, zp_ref, qg_ref, kg_ref, sk_ref, rb_ref, bk_ref, pw_ref, ps_ref, y_ref, bias_scr, yacc):
        n = pl.program_id(0)

        @pl.when(n == 0)
        def _():
            _fill_bias(bk_ref, rb_ref, bias_scr)

        for s in range(MIX_SUB):
            win = _Window(zc_ref, zp_ref, n, s)
            rows = slice(s * BLK, (s + 1) * BLK)
            for kh in range(NKV):
                a = _attn_probs(win, kh, qg_ref[...], kg_ref[...], sk_ref, bias_scr)
                o = _nn(a["p"].astype(bf16), a["vb"])
                for g in range(GQA):
                    hc = HD * (GQA * kh + g)
                    yacc[rows, hc:hc + HD] = o[g * BLK:(g + 1) * BLK]
            for g, w in enumerate(POOL_WINDOWS):
                pooled, _, _ = _pool_group(win, g, w)
                yp = _nn(pooled.astype(bf16), pw_ref[g].astype(bf16)) * ps_ref[:, PGD * g:PGD * (g + 1)]
                yacc[rows, DATTN + PGD * g:DATTN + PGD * (g + 1)] = yp
        y_ref[...] = yacc[...].astype(bf16)

    full = lambda *shape: pl.BlockSpec(shape, lambda n: (0,) * len(shape))
    smem = pl.BlockSpec(memory_space=pltpu.SMEM)
    return pl.pallas_call(
        body, grid=(nsteps,),
        in_specs=[pl.BlockSpec((step_rows, DIN), lambda n: (n, 0)),
                  pl.BlockSpec((BLK, DIN), lambda n: (jnp.maximum(n * MIX_SUB - 1, 0), 0)),
                  full(1, HD), full(1, HD), smem, smem, full(BLK, 2 * BLK),
                  full(len(POOL_WINDOWS), PGD, PGD), full(1, DPOOL)],
        out_specs=pl.BlockSpec((step_rows, DMIX), lambda n: (n, 0)),
        out_shape=SDS((T, DMIX), bf16),
        scratch_shapes=[pltpu.VMEM((NH, BLK, 2 * BLK), f32), pltpu.VMEM((step_rows, DMIX), f32)],
        compiler_params=_cparams(("arbitrary",)), name=name)(z, z, qg, kg, sinks, relb, bucket, pool_w, pscale)


def _mix_bwd(z, dy, qg, kg, sinks, relb, bucket, pool_w, pscale, name):
    T = z.shape[0]
    step_rows = MIX_SUB * BLK
    nsteps = T // step_rows

    def body(zc_ref, zp_ref, dy_ref, qg_ref, kg_ref, sk_ref, rb_ref, bk_ref, pw_ref, ps_ref,
             dz_ref, dqg_ref, dkg_ref, dsk_ref, drb_ref, dpw_ref, dps_ref, bias_scr, dbias_scr):
        n = pl.program_id(0)

        @pl.when(n == 0)
        def _():
            _fill_bias(bk_ref, rb_ref, bias_scr)
            dbias_scr[...] = jnp.zeros_like(dbias_scr)
            dqg_ref[...] = jnp.zeros_like(dqg_ref)
            dkg_ref[...] = jnp.zeros_like(dkg_ref)
            dsk_ref[...] = jnp.zeros_like(dsk_ref)
            dpw_ref[...] = jnp.zeros_like(dpw_ref)
            dps_ref[...] = jnp.zeros_like(dps_ref)

        qg, kg = qg_ref[...], kg_ref[...]
        lane = lax.broadcasted_iota(i32, (1, 128), 1)
        dsk = jnp.zeros((1, 128), f32)
        for s in range(MIX_SUB):
            win = _Window(zc_ref, zp_ref, n, s)
            blk = win.blk
            rows = pl.ds(pl.multiple_of(blk * BLK, BLK), BLK)
            prow = pl.ds(pl.multiple_of(jnp.maximum(blk - 1, 0) * BLK, BLK), BLK)
            dyr = slice(s * BLK, (s + 1) * BLK)

            def into_prev(fn, s=s):
                if s == 0:
                    pl.when(n > 0)(fn)
                else:
                    fn()

            for kh in range(NKV):
                a = _attn_probs(win, kh, qg, kg, sk_ref, bias_scr)
                p = a["p"]
                do = jnp.concatenate([dy_ref[dyr, HD * (GQA * kh + g):HD * (GQA * kh + g + 1)] for g in range(GQA)],
                                     axis=0).astype(bf16)
                dv = _tn(p.astype(bf16), do)
                dp = _nt(do, a["vb"])
                delta = jnp.sum(p * dp, axis=-1, keepdims=True)
                ds = p * (dp - delta)
                sinkterm = a["psink"] * delta
                for g in range(GQA):
                    h = GQA * kh + g
                    dbias_scr[h] += ds[g * BLK:(g + 1) * BLK]
                    tot = jnp.sum(sinkterm[g * BLK:(g + 1) * BLK], axis=0, keepdims=True)
                    dsk = dsk - jnp.where(lane == h, tot, 0.0)
                dsb = ds.astype(bf16)
                dqn = _nn(dsb, a["knb"]) * SCALE
                dkn = _tn(dsb, a["qnb"]) * SCALE
                qhat, khat = a["qhat"], a["khat"]
                dqg_ref[...] += jnp.sum(dqn * qhat, axis=0, keepdims=True)
                dkg_ref[...] += jnp.sum(dkn * khat, axis=0, keepdims=True)
                dqh = dqn * qg
                dq = a["rq"] * (dqh - qhat * jnp.mean(dqh * qhat, axis=-1, keepdims=True))
                dkh = dkn * kg
                dk = a["rk"] * (dkh - khat * jnp.mean(dkh * khat, axis=-1, keepdims=True))
                kc = DATTN + HD * kh
                vc = DATTN + DKV + HD * kh
                for g in range(GQA):
                    hc = HD * (GQA * kh + g)
                    dz_ref[rows, hc:hc + HD] = dq[g * BLK:(g + 1) * BLK]
                dz_ref[rows, kc:kc + HD] = dk[BLK:2 * BLK]
                dz_ref[rows, vc:vc + HD] = dv[BLK:2 * BLK]

                def kv_prev(dk=dk, dv=dv, kc=kc, vc=vc, prow=prow):
                    dz_ref[prow, kc:kc + HD] += dk[0:BLK]
                    dz_ref[prow, vc:vc + HD] += dv[0:BLK]

                into_prev(kv_prev)

            for g, w in enumerate(POOL_WINDOWS):
                c0 = DATTN + 2 * DKV + PGD * g
                pooled, band, cnt = _pool_group(win, g, w)
                pb = pooled.astype(bf16)
                wb = pw_ref[g].astype(bf16)
                dyp = dy_ref[dyr, DATTN + PGD * g:DATTN + PGD * (g + 1)]
                ypre = _nn(pb, wb)
                dps_ref[:, PGD * g:PGD * (g + 1)] += jnp.sum(dyp * ypre, axis=0, keepdims=True)
                dyg = (dyp * ps_ref[:, PGD * g:PGD * (g + 1)]).astype(bf16)
                dpw_ref[g] += _tn(pb, dyg)
                dpooled = _nt(dyg, wb)
                dsm = dpooled / cnt
                hi = dsm.astype(bf16)
                lo = (dsm - hi.astype(f32)).astype(bf16)
                due = _tn(band, hi) + _tn(band, lo)
                dz_ref[rows, c0:c0 + PGD] = due[BLK:2 * BLK] - dpooled

                def pool_prev(due=due, c0=c0, prow=prow):
                    dz_ref[prow, c0:c0 + PGD] += due[0:BLK]

                into_prev(pool_prev)

        dsk_ref[...] += dsk

        @pl.when(n == nsteps - 1)
        def _():
            bk = bk_ref[...]
            ri = lax.broadcasted_iota(i32, (NBUCK, NH), 0)
            ci = lax.broadcasted_iota(i32, (NBUCK, NH), 1)

            def step(b, acc):
                for h in range(NH):
                    sel = jnp.where(bk == b, dbias_scr[h], 0.0)
                    tot = jnp.sum(jnp.sum(sel, axis=1, keepdims=True), axis=0, keepdims=True)
                    acc = acc + jnp.where((ri == b) & (ci == h), tot, 0.0)
                return acc

            drb_ref[...] = lax.fori_loop(0, NBUCK, step, jnp.zeros((NBUCK, NH), f32))

    full = lambda *shape: pl.BlockSpec(shape, lambda n: (0,) * len(shape))
    smem = pl.BlockSpec(memory_space=pltpu.SMEM)
    npg = len(POOL_WINDOWS)
    return pl.pallas_call(
        body, grid=(nsteps,),
        in_specs=[pl.BlockSpec((step_rows, DIN), lambda n: (n, 0)),
                  pl.BlockSpec((BLK, DIN), lambda n: (jnp.maximum(n * MIX_SUB - 1, 0), 0)),
                  pl.BlockSpec((step_rows, DMIX), lambda n: (n, 0)),
                  full(1, HD), full(1, HD), smem, smem, full(BLK, 2 * BLK), full(npg, PGD, PGD), full(1, DPOOL)],
        out_specs=[full(T, DIN), full(1, HD), full(1, HD), full(1, 128), full(NBUCK, NH),
                   full(npg, PGD, PGD), full(1, DPOOL)],
        out_shape=(SDS((T, DIN), f32), SDS((1, HD), f32), SDS((1, HD), f32), SDS((1, 128), f32),
                   SDS((NBUCK, NH), f32), SDS((npg, PGD, PGD), f32), SDS((1, DPOOL), f32)),
        scratch_shapes=[pltpu.VMEM((NH, BLK, 2 * BLK), f32), pltpu.VMEM((NH, BLK, 2 * BLK), f32)],
        compiler_params=_cparams(("arbitrary",), VMEM_LIMIT_V7X),
        name=name)(z, z, dy, qg, kg, sinks, relb, bucket, pool_w, pscale)


class _LocalWeights:
    def __init__(self, w1, wint, wout, w2):
        self.w1, self.wint, self.wout, self.w2 = w1, wint, wout, w2

    def ffn1(self):
        return self.w1

    def after_ffn1(self, x1):
        return x1

    def mix(self, after):
        return self.wint, self.wout

    def before_out_proj(self, wout, after):
        return wout

    def ffn2(self, after):
        return self.w2

    def mix_ffn2_grads_ready(self, dwint, dwout, dw2, dh2):
        self.grads_rest = (dwint, dwout, dw2)
        return dh2

    def before_ffn1_bwd(self, dx1b):
        return dx1b


def _local_step(x, target, weights, g1, gm, g3, qg, kg, sinks, relb, pool_w, pscale):
    bucket = jnp.asarray(_t5_bucket_table())
    sk = sinks.reshape(NH)
    w1 = weights.ffn1()
    h1 = _norm_fwd(x, g1, "norm1_fwd")
    x1, gate1, up1 = _ffn_fwd(h1, w1, x, None, "ffn1_fwd")
    x1 = weights.after_ffn1(x1)
    h2 = _norm_fwd(x1, gm, "norm2_fwd")
    wint, wout = weights.mix(h2)
    z = _in_proj_fwd(h2, wint, "in_proj_fwd")
    ymix = _mix_fwd(z, qg, kg, sk, relb, bucket, pool_w, pscale, "mix_fwd")
    wout = weights.before_out_proj(wout, ymix)
    x2, h3 = _out_proj_fwd(ymix, wout, x1, g3, "out_proj_fwd")
    w2 = weights.ffn2(h3)
    dy, gate2, up2, dyb, loss_lanes = _ffn_fwd(h3, w2, x2, target, "ffn2_fwd")

    dh3, dw2 = _ffn_bwd(dyb, h3, gate2, up2, w2, "ffn2_bwd")
    dx2, dx2b, dg3 = _norm_bwd(dh3, x2, g3, dy, 1.0, "norm3_bwd")
    dymix, dwout = _out_proj_bwd(dx2b, wout, ymix, "out_proj_bwd")
    dz, dqg, dkg, dsk, drb, dpw, dps = _mix_bwd(z, dymix, qg, kg, sk, relb, bucket, pool_w, pscale, "mix_bwd")
    dh2, dwint = _in_proj_bwd(dz, wint, h2, "in_proj_bwd")
    dh2 = weights.mix_ffn2_grads_ready(dwint, dwout, dw2, dh2)
    dx1, dx1b, dgm = _norm_bwd(dh2, x1, gm, dx2, 0.5, "norm2_bwd")
    dx1b = weights.before_ffn1_bwd(dx1b)
    dh1, dw1 = _ffn_bwd(dx1b, h1, gate1, up1, w1, "ffn1_bwd")
    gx, _, dg1 = _norm_bwd(dh1, x, g1, dx1, 1.0, "norm1_bwd")
    small = dict(ffn1_norm=dg1, mix_norm=dgm, ffn2_norm=dg3, pool_scale=dps, q_norm=dqg, k_norm=dkg,
                 attn_sinks=dsk[:, :NH], rel_bias=drb, pool_w=dpw, loss=loss_lanes)
    return gx, (dw1, dwint, dwout, dw2), small


SMALL_NAMES = ("ffn1_norm", "mix_norm", "ffn2_norm", "pool_scale", "q_norm", "k_norm", "attn_sinks", "rel_bias",
               "pool_w", "loss")
SMALL_SHAPES = dict(ffn1_norm=(1, D), mix_norm=(1, D), ffn2_norm=(1, D), pool_scale=(1, DPOOL), q_norm=(1, HD),
                    k_norm=(1, HD), attn_sinks=(1, NH), rel_bias=(NBUCK, NH),
                    pool_w=(1, len(POOL_WINDOWS), PGD, PGD), loss=(1, 128))


def _small_rows(name):
    return -(-int(np.prod(SMALL_SHAPES[name])) // 128)


SMALL_OFF = {}
_r = 0
for _n in SMALL_NAMES:
    SMALL_OFF[_n] = _r
    _r += _small_rows(_n)
SMALL_ROWS = -(-_r // 8) * 8
LOSS_ROW = SMALL_OFF["loss"]


def _pack_small(vals):
    parts = []
    for n in SMALL_NAMES:
        size = _small_rows(n) * 128
        if n in vals:
            flat = vals[n].astype(f32).reshape(-1)
            parts.append(jnp.pad(flat, (0, size - flat.shape[0])))
        else:
            parts.append(jnp.zeros((size,), f32))
    flat = jnp.concatenate(parts)
    flat = jnp.pad(flat, (0, SMALL_ROWS * 128 - flat.shape[0]))
    return flat.reshape(SMALL_ROWS, 128)


def _unpack_small(packed, name):
    size = int(np.prod(SMALL_SHAPES[name]))
    r0 = SMALL_OFF[name]
    return packed[r0:r0 + _small_rows(name)].reshape(-1)[:size].reshape(SMALL_SHAPES[name])


def _position():
    return lax.axis_index("x"), lax.axis_index("y"), lax.axis_index("c")


def _dev_index(x, y, c):
    return 4 * x + 2 * y + c


G1_PIECES, MIX_PIECES, F2_PIECES = (0, 1, 2), (3, 4), (5, 6, 7)


def _group_rows(pieces):
    return sum(PIECE_ROWS[k] for k in pieces)


def _shard_piece(s_ref, k):
    return s_ref.at[pl.ds(PIECE_OFF[k], PIECE_ROWS[k]), :]


def _shard_group(s_ref, pieces):
    return s_ref.at[pl.ds(PIECE_OFF[pieces[0]], _group_rows(pieces)), :]


def _weight_pieces(w1_ref=None, wi_ref=None, wo_ref=None, w2_ref=None):
    arrs = {}
    if w1_ref is not None:
        arrs.update({0: w1_ref.at[0], 1: w1_ref.at[1], 2: w1_ref.at[2]})
    if wi_ref is not None:
        arrs[3] = wi_ref
    if wo_ref is not None:
        arrs[4] = wo_ref
    if w2_ref is not None:
        arrs.update({5: w2_ref.at[0], 6: w2_ref.at[1], 7: w2_ref.at[2]})
    return arrs


def _block_rows(arrs, k, dev):
    r = PIECE_ROWS[k]
    return arrs[k].at[pl.ds(pl.multiple_of(_dev_index(*dev) * r, 16), r), :]


def _all_gather_ffn1(shard):
    pieces = G1_PIECES

    def body(s_ref, w1_ref, send_sems, recv_sems, local_sem):
        x, y, c = _position()
        me, sib = (x, y, c), (x, y, 1 - c)
        chips = [(1 - x, y), (x, 1 - y), (1 - x, 1 - y)]
        arrs = _weight_pieces(w1_ref=w1_ref)

        def copies(rel, block, to, from_shard):
            return [pltpu.make_async_remote_copy(
                src_ref=_shard_piece(s_ref, k) if from_shard else _block_rows(arrs, k, block),
                dst_ref=_block_rows(arrs, k, block),
                send_sem=send_sems.at[rel], recv_sem=recv_sems.at[rel], device_id=to, device_id_type=MESH)
                for k in pieces]

        def whole(rel):
            grp = _shard_group(s_ref, pieces)
            return pltpu.make_async_remote_copy(src_ref=grp, dst_ref=grp, send_sem=send_sems.at[rel],
                                                recv_sem=recv_sems.at[rel], device_id=me, device_id_type=MESH)

        mine = [pltpu.make_async_copy(_shard_piece(s_ref, k), _block_rows(arrs, k, me), local_sem) for k in pieces]
        for cp in mine:
            cp.start()
        for cp in copies(0, me, sib, True):
            cp.start()
        for j, chip in enumerate(chips):
            for cp in copies(1 + j, me, (*chip, c), True):
                cp.start()
        for j, chip in enumerate(chips):
            whole(1 + j).wait_recv()
            for cp in copies(4 + j, (*chip, c), sib, False):
                cp.start()
        whole(0).wait_recv()
        for j in range(3):
            whole(4 + j).wait_recv()
        for rel in range(7):
            whole(rel).wait_send()
        grp = _shard_group(s_ref, pieces)
        pltpu.make_async_copy(grp, grp, local_sem).wait()

    hbm = pl.BlockSpec(memory_space=pl.ANY)
    return pl.pallas_call(
        body, in_specs=[hbm], out_specs=hbm, out_shape=SDS((3, F, D), bf16),
        scratch_shapes=[pltpu.SemaphoreType.DMA((7,)), pltpu.SemaphoreType.DMA((7,)), pltpu.SemaphoreType.DMA],
        compiler_params=pltpu.CompilerParams(has_side_effects=True),
        name="all_gather_ffn1")(shard)


HBM_SPEC = pl.BlockSpec(memory_space=pltpu.HBM)
SEM_SPEC = pl.BlockSpec(memory_space=pltpu.SEMAPHORE)
ANY_SPEC = pl.BlockSpec(memory_space=pl.ANY)
SPLIT_EFFECT = pltpu.SideEffectType.DATAFLOW_SIDE_EFFECTING


def _in_hbm(a):
    return pltpu.with_memory_space_constraint(a, pltpu.HBM)


def _hbm_like(a):
    return pltpu.HBM(a.shape, a.dtype)


def _place_own_rows(shard):
    pieces = MIX_PIECES + F2_PIECES

    def body(s_ref, wi_ref, wo_ref, w2_ref, buf, sems):
        x, y, c = _position()
        arrs = _weight_pieces(wi_ref=wi_ref, wo_ref=wo_ref, w2_ref=w2_ref)
        grp = _shard_group(s_ref, pieces)
        load = pltpu.make_async_copy(grp, buf, sems.at[0])
        load.start()
        load.wait()
        base = PIECE_OFF[pieces[0]]
        for k in pieces:
            pltpu.make_async_copy(buf.at[pl.ds(PIECE_OFF[k] - base, PIECE_ROWS[k]), :],
                                  _block_rows(arrs, k, (x, y, c)), sems.at[1]).start()
        pltpu.make_async_copy(grp, buf, sems.at[1]).wait()

    return pl.pallas_call(
        body, in_specs=[ANY_SPEC], out_specs=[ANY_SPEC] * 3,
        out_shape=(SDS((DIN, D), bf16), SDS((DMIX, D), bf16), SDS((3, F, D), bf16)),
        scratch_shapes=[pltpu.VMEM((_group_rows(pieces), D), bf16), pltpu.SemaphoreType.DMA((2,))],
        name="place_own_rows")(shard)


def _xor_peer(x, y, c, k):
    return (x ^ (k >> 2), y ^ ((k >> 1) & 1), c ^ (k & 1))


def _gather_rest_start(shard, wi, wo, w2, w1):
    def body(s_ref, wi_ref, wo_ref, w2_ref, w1_ref,
             ssem_m, rsem_m0, rsem_m, ssem_f, rsem_f0, rsem_f, s_o, wi_o, wo_o, w2_o, w1_o):
        x, y, c = _position()
        me, sib = (x, y, c), (x, y, 1 - c)
        chips = [(1 - x, y), (x, 1 - y), (1 - x, 1 - y)]
        arrs = _weight_pieces(wi_ref=wi_ref, wo_ref=wo_ref, w2_ref=w2_ref)
        for pieces, ssem, rsem0, rsem in ((MIX_PIECES, ssem_m, rsem_m0, rsem_m), (F2_PIECES, ssem_f, rsem_f0, rsem_f)):
            for p in pieces:
                pltpu.make_async_remote_copy(
                    src_ref=_shard_piece(s_ref, p), dst_ref=_block_rows(arrs, p, me), send_sem=ssem.at[0],
                    recv_sem=rsem0, device_id=sib, device_id_type=MESH).start()
            for j, chip in enumerate(chips):
                for p in pieces:
                    pltpu.make_async_remote_copy(
                        src_ref=_shard_piece(s_ref, p), dst_ref=_block_rows(arrs, p, me), send_sem=ssem.at[1 + j],
                        recv_sem=rsem.at[j], device_id=(*chip, c), device_id_type=MESH).start()

    dma = pltpu.SemaphoreType.DMA
    return pl.pallas_call(
        body, name="gather_rest_start",
        out_shape=(dma((4,)), dma(()), dma((3,)), dma((4,)), dma(()), dma((3,)),
                   _hbm_like(shard), _hbm_like(wi), _hbm_like(wo), _hbm_like(w2), _hbm_like(w1)),
        in_specs=(HBM_SPEC,) * 5, out_specs=(SEM_SPEC,) * 6 + (HBM_SPEC,) * 5,
        input_output_aliases={0: 6, 1: 7, 2: 8, 3: 9, 4: 10},
        compiler_params=pltpu.CompilerParams(has_side_effects=SPLIT_EFFECT),
    )(_in_hbm(shard), _in_hbm(wi), _in_hbm(wo), _in_hbm(w2), _in_hbm(w1))


def _gather_mix_pass_on(rsem_m, wi, wo, thru, after):
    def body(wi_ref, wo_ref, thru_ref, rsem, after_ref, fsend, frecv, wi_o, wo_o, thru_o):
        x, y, c = _position()
        sib = (x, y, 1 - c)
        arrs = _weight_pieces(wi_ref=wi_ref, wo_ref=wo_ref)
        both = wi_ref.at[pl.ds(0, _group_rows(MIX_PIECES)), :]
        for j, chip in enumerate([(1 - x, y), (x, 1 - y), (1 - x, 1 - y)]):
            pltpu.make_async_remote_copy(src_ref=both, dst_ref=both, send_sem=fsend.at[j], recv_sem=rsem.at[j],
                                         device_id=(x, y, c), device_id_type=MESH).wait_recv()
            for p in MIX_PIECES:
                rows = _block_rows(arrs, p, (*chip, c))
                pltpu.make_async_remote_copy(src_ref=rows, dst_ref=rows, send_sem=fsend.at[j], recv_sem=frecv.at[j],
                                             device_id=sib, device_id_type=MESH).start()

    dma = pltpu.SemaphoreType.DMA
    return pl.pallas_call(
        body, name="gather_mix_pass_on",
        out_shape=(dma((3,)), dma((3,)), _hbm_like(wi), _hbm_like(wo), _hbm_like(thru)),
        in_specs=(HBM_SPEC, HBM_SPEC, HBM_SPEC, SEM_SPEC, ANY_SPEC), out_specs=(SEM_SPEC, SEM_SPEC) + (HBM_SPEC,) * 3,
        input_output_aliases={0: 2, 1: 3, 2: 4},
        compiler_params=pltpu.CompilerParams(has_side_effects=SPLIT_EFFECT),
    )(wi, wo, _in_hbm(thru), rsem_m, after)


def _gather_mix_wait(ssem_m, rsem_m0, fsend, frecv, shard, wi, wo, after):
    def body(s_ref, wi_ref, wo_ref, ssem, rsem0, fs, fr, after_ref, s_o, wi_o, wo_o):
        x, y, c = _position()
        grp = _shard_group(s_ref, MIX_PIECES)

        def waiter(send_sem, recv_sem):
            return pltpu.make_async_remote_copy(src_ref=grp, dst_ref=grp, send_sem=send_sem, recv_sem=recv_sem,
                                                device_id=(x, y, c), device_id_type=MESH)

        waiter(ssem.at[0], rsem0).wait_recv()
        for j in range(3):
            waiter(fs.at[j], fr.at[j]).wait_recv()
        for rel in range(4):
            waiter(ssem.at[rel], rsem0).wait_send()
        for j in range(3):
            waiter(fs.at[j], fr.at[j]).wait_send()

    return pl.pallas_call(
        body, name="gather_mix_wait", out_shape=(_hbm_like(shard), _hbm_like(wi), _hbm_like(wo)),
        in_specs=(HBM_SPEC,) * 3 + (SEM_SPEC,) * 4 + (ANY_SPEC,), out_specs=(HBM_SPEC,) * 3,
        input_output_aliases={0: 0, 1: 1, 2: 2},
        compiler_params=pltpu.CompilerParams(has_side_effects=SPLIT_EFFECT),
    )(shard, wi, wo, ssem_m, rsem_m0, fsend, frecv, after)


def _gather_ffn2_pass_on(rsem_f, w2, wo, after):
    def body(w2_ref, wo_ref, rsem, after_ref, fsend, frecv, w2_o, wo_o):
        x, y, c = _position()
        sib = (x, y, 1 - c)
        chips = [(1 - x, y), (x, 1 - y), (1 - x, 1 - y)]
        arrs = _weight_pieces(w2_ref=w2_ref)
        three = w2_ref.at[0, pl.ds(0, _group_rows(F2_PIECES)), :]
        for j, chip in enumerate(chips):
            pltpu.make_async_remote_copy(src_ref=three, dst_ref=three, send_sem=fsend.at[j], recv_sem=rsem.at[j],
                                         device_id=(x, y, c), device_id_type=MESH).wait_recv()
            for p in F2_PIECES:
                rows = _block_rows(arrs, p, (*chip, c))
                pltpu.make_async_remote_copy(src_ref=rows, dst_ref=rows, send_sem=fsend.at[j], recv_sem=frecv.at[j],
                                             device_id=sib, device_id_type=MESH).start()

    dma = pltpu.SemaphoreType.DMA
    return pl.pallas_call(
        body, name="gather_ffn2_pass_on", out_shape=(dma((3,)), dma((3,)), _hbm_like(w2), _hbm_like(wo)),
        in_specs=(HBM_SPEC, HBM_SPEC, SEM_SPEC, ANY_SPEC), out_specs=(SEM_SPEC, SEM_SPEC, HBM_SPEC, HBM_SPEC),
        input_output_aliases={0: 2, 1: 3},
        compiler_params=pltpu.CompilerParams(has_side_effects=SPLIT_EFFECT),
    )(w2, wo, rsem_f, after)


def _gather_ffn2_wait(ssem_f, rsem_f0, fsend, frecv, shard, w2, after):
    def body(s_ref, w2_ref, ssem, rsem0, fs, fr, after_ref, w2_o):
        x, y, c = _position()
        grp = _shard_group(s_ref, F2_PIECES)

        def waiter(send_sem, recv_sem):
            return pltpu.make_async_remote_copy(src_ref=grp, dst_ref=grp, send_sem=send_sem, recv_sem=recv_sem,
                                                device_id=(x, y, c), device_id_type=MESH)

        waiter(ssem.at[0], rsem0).wait_recv()
        for j in range(3):
            waiter(fs.at[j], fr.at[j]).wait_recv()
        for rel in range(4):
            waiter(ssem.at[rel], rsem0).wait_send()
        for j in range(3):
            waiter(fs.at[j], fr.at[j]).wait_send()

    return pl.pallas_call(
        body, name="gather_ffn2_wait", out_shape=_hbm_like(w2),
        in_specs=(HBM_SPEC, HBM_SPEC, SEM_SPEC, SEM_SPEC, SEM_SPEC, SEM_SPEC, ANY_SPEC), out_specs=HBM_SPEC,
        input_output_aliases={1: 0},
        compiler_params=pltpu.CompilerParams(has_side_effects=SPLIT_EFFECT),
    )(shard, w2, ssem_f, rsem_f0, fsend, frecv, after)


class _GatheredWeights(_LocalWeights):
    def __init__(self, shard):
        w1 = _all_gather_ffn1(shard)
        wi, wo, w2 = _place_own_rows(shard)
        (self.ssem_m, self.rsem_m0, self.rsem_m, self.ssem_f, self.rsem_f0, self.rsem_f,
         self.shard, self.wi, self.wo, self.w2_part, self.w1) = _gather_rest_start(shard, wi, wo, w2, w1)

    def after_ffn1(self, x1):
        self.fsend_m, self.frecv_m, self.wi, self.wo, x1 = _gather_mix_pass_on(self.rsem_m, self.wi, self.wo, x1, x1)
        return x1

    def mix(self, after):
        self.shard, wint, wout = _gather_mix_wait(self.ssem_m, self.rsem_m0, self.fsend_m, self.frecv_m, self.shard,
                                                  self.wi, self.wo, after)
        return wint, wout

    def before_out_proj(self, wout, after):
        self.fsend, self.frecv, self.w2_part, wout = _gather_ffn2_pass_on(self.rsem_f, self.w2_part, wout, after)
        return wout

    def ffn2(self, after):
        return _gather_ffn2_wait(self.ssem_f, self.rsem_f0, self.fsend, self.frecv, self.shard, self.w2_part, after)

    def mix_ffn2_grads_ready(self, dwint, dwout, dw2, dh2):
        rx1 = lax.empty((4, RSA_ROWS, D), bf16)
        self.sa, self.ra, dwint, dwout, dw2, rx1, dh2 = _rsa_level1_start(dwint, dwout, dw2, rx1, dh2)
        self.level1 = (dwint, dwout, dw2, rx1)
        return dh2

    def before_ffn1_bwd(self, dx1b):
        dwint, dwout, dw2, rx1 = _rsa_level1_wait(self.sa, self.ra, *self.level1, dx1b)
        tx, self.acc = _rsa_chip_sums(dwint, dwout, dw2, rx1)
        rx2 = lax.empty((3, RSA_ROWS, D), bf16)
        self.sb, self.rb, self.tx, self.rx2, dx1b = _rsa_level2_start(tx, rx2, dx1b)
        return dx1b

    def mix_ffn2_grads_total(self, after):
        rx2 = _rsa_level2_wait(self.sb, self.rb, self.tx, self.rx2, after)
        return _rsa_total(self.acc, rx2)


RS_CHUNK = 176


def _reduce_scatter_ffn1(dw1, small_packed):
    pieces = G1_PIECES
    nrows = _group_rows(pieces)
    nchunk = nrows // RS_CHUNK

    def body(d1_ref, p_ref, red_ref, rx1_ref, rx2_ref, tot_ref,
             own_buf, rx_buf, tx_buf, acc, sa, ra, sb, rb, lsem, pair, chips, small_send, small_recv):
        x, y, c = _position()
        me, sib = (x, y, c), (x, y, 1 - c)
        rel_chips = [(x, y), (1 - x, y), (x, 1 - y), (1 - x, 1 - y)]
        srcs = _weight_pieces(w1_ref=d1_ref)

        my_chip = 2 * x + y
        pair[c] = p_ref[...]
        swap = pltpu.make_async_remote_copy(
            src_ref=p_ref, dst_ref=pair.at[c], send_sem=small_send.at[0], recv_sem=small_recv.at[0],
            device_id=sib, device_id_type=MESH)
        swap.start()
        small = [pltpu.make_async_remote_copy(
            src_ref=chips.at[my_chip], dst_ref=chips.at[my_chip], send_sem=small_send.at[j], recv_sem=small_recv.at[j],
            device_id=(*rel_chips[j], c), device_id_type=MESH) for j in (1, 2, 3)]

        def piece(k, dev):
            r = PIECE_ROWS[k]
            return srcs[k].at[pl.ds(pl.multiple_of(_dev_index(*dev) * r, 16), r), :]

        def packed(ref, k):
            return ref.at[pl.ds(PIECE_OFF[k], PIECE_ROWS[k]), :]

        for j, chip in enumerate(rel_chips):
            for k in pieces:
                pltpu.make_async_remote_copy(
                    src_ref=piece(k, (*chip, 1 - c)), dst_ref=packed(rx1_ref.at[j], k),
                    send_sem=sa.at[j], recv_sem=ra.at[j], device_id=sib, device_id_type=MESH).start()

        def wait_a(j):
            return pltpu.make_async_remote_copy(src_ref=rx1_ref.at[j], dst_ref=rx1_ref.at[j], send_sem=sa.at[j],
                                                recv_sem=ra.at[j], device_id=me, device_id_type=MESH)

        def ici(j):
            return pltpu.make_async_remote_copy(
                src_ref=tx_buf.at[j - 1], dst_ref=rx2_ref.at[j - 1], send_sem=sb.at[j - 1], recv_sem=rb.at[j - 1],
                device_id=(*rel_chips[j], c), device_id_type=MESH)

        swap.wait_recv()
        chips[my_chip] = pair[0] + pair[1]
        for cp in small:
            cp.start()

        for j in (1, 2, 3, 0):
            loads = [pltpu.make_async_copy(piece(k, (*rel_chips[j], c)), packed(own_buf, k), lsem)
                     for k in pieces]
            for cp in loads:
                cp.start()
            wait_a(j).wait_recv()
            got = pltpu.make_async_copy(rx1_ref.at[j], rx_buf, lsem)
            got.start()
            pltpu.make_async_copy(rx_buf, rx_buf, lsem).wait()
            got.wait()

            def add(i, carry, j=j):
                rows = pl.ds(pl.multiple_of(i * RS_CHUNK, 16), RS_CHUNK)
                tot = own_buf[rows, :].astype(f32) + rx_buf[rows, :].astype(f32)
                if j == 0:
                    acc[rows, :] = tot
                else:
                    tx_buf[j - 1, rows, :] = tot.astype(bf16)
                return carry

            lax.fori_loop(0, nchunk, add, 0)
            if j != 0:
                ici(j).start()

        for j in (1, 2, 3):
            ici(j).wait_recv()
            got = pltpu.make_async_copy(rx2_ref.at[j - 1], rx_buf, lsem)
            got.start()
            got.wait()

            def add2(i, carry):
                rows = pl.ds(pl.multiple_of(i * RS_CHUNK, 16), RS_CHUNK)
                acc[rows, :] += rx_buf[rows, :].astype(f32)
                return carry

            lax.fori_loop(0, nchunk, add2, 0)
        out = pltpu.make_async_copy(acc, red_ref, lsem)
        out.start()
        out.wait()
        for cp in small:
            cp.wait_recv()
        tot = (chips[0] + chips[1]) + (chips[2] + chips[3])
        tot_ref[...] = tot
        loss = jnp.sum(tot[LOSS_ROW:LOSS_ROW + 1, :], axis=-1, keepdims=True)
        tot_ref[LOSS_ROW:LOSS_ROW + 1, :] = jnp.broadcast_to(loss, (1, 128))
        for j in range(4):
            wait_a(j).wait_send()
        for j in (1, 2, 3):
            ici(j).wait_send()
        swap.wait_send()
        for cp in small:
            cp.wait_send()

    hbm = pl.BlockSpec(memory_space=pl.ANY)
    vm = pl.BlockSpec(memory_space=pltpu.VMEM)
    red, _, _, small_tot = pl.pallas_call(
        body, in_specs=[hbm, vm], out_specs=[hbm] * 3 + [vm],
        out_shape=(SDS((nrows, D), f32), SDS((4, nrows, D), bf16), SDS((3, nrows, D), bf16),
                   SDS((SMALL_ROWS, 128), f32)),
        scratch_shapes=[pltpu.VMEM((nrows, D), bf16), pltpu.VMEM((nrows, D), bf16),
                        pltpu.VMEM((3, nrows, D), bf16), pltpu.VMEM((nrows, D), f32),
                        pltpu.SemaphoreType.DMA((4,)), pltpu.SemaphoreType.DMA((4,)),
                        pltpu.SemaphoreType.DMA((3,)), pltpu.SemaphoreType.DMA((3,)), pltpu.SemaphoreType.DMA,
                        pltpu.VMEM((2, SMALL_ROWS, 128), f32), pltpu.VMEM((4, SMALL_ROWS, 128), f32),
                        pltpu.SemaphoreType.DMA((4,)), pltpu.SemaphoreType.DMA((4,))],
        compiler_params=pltpu.CompilerParams(has_side_effects=True, vmem_limit_bytes=VMEM_LIMIT_V7X),
        name="reduce_scatter_ffn1")(dw1, small_packed)
    return red, small_tot


RSA_PIECES = MIX_PIECES + F2_PIECES
RSA_ROWS = _group_rows(RSA_PIECES)
RSA_OFF = {k: PIECE_OFF[k] - PIECE_OFF[RSA_PIECES[0]] for k in RSA_PIECES}
RSA_BLOCK = 192


def _rsa_rows(ref, k):
    return ref.at[pl.ds(RSA_OFF[k], PIECE_ROWS[k]), :]


def _rsa_level1_start(dwint, dwout, dw2, rx1, thru):
    def body(di_ref, do_ref, d2_ref, rx1_ref, thru_ref, sa, ra, di_o, do_o, d2_o, rx1_o, thru_o):
        x, y, c = _position()
        srcs = _weight_pieces(wi_ref=di_ref, wo_ref=do_ref, w2_ref=d2_ref)
        for j, chip in enumerate([(x, y), (1 - x, y), (x, 1 - y), (1 - x, 1 - y)]):
            for k in RSA_PIECES:
                pltpu.make_async_remote_copy(
                    src_ref=_block_rows(srcs, k, (*chip, 1 - c)), dst_ref=_rsa_rows(rx1_ref.at[j], k),
                    send_sem=sa.at[j], recv_sem=ra.at[j], device_id=(x, y, 1 - c), device_id_type=MESH).start()

    dma = pltpu.SemaphoreType.DMA
    arrs = (dwint, dwout, dw2, rx1, thru)
    return pl.pallas_call(
        body, name="rsa_level1_start", out_shape=(dma((4,)), dma((4,))) + tuple(_hbm_like(a) for a in arrs),
        in_specs=(HBM_SPEC,) * 5, out_specs=(SEM_SPEC,) * 2 + (HBM_SPEC,) * 5,
        input_output_aliases={0: 2, 1: 3, 2: 4, 3: 5, 4: 6},
        compiler_params=pltpu.CompilerParams(has_side_effects=SPLIT_EFFECT),
    )(*[_in_hbm(a) for a in arrs])


def _rsa_level1_wait(sa, ra, dwint, dwout, dw2, rx1, after):
    def body(di_ref, do_ref, d2_ref, rx1_ref, sa_ref, ra_ref, after_ref, di_o, do_o, d2_o, rx1_o):
        x, y, c = _position()
        for j in range(4):
            d = pltpu.make_async_remote_copy(src_ref=rx1_ref.at[j], dst_ref=rx1_ref.at[j], send_sem=sa_ref.at[j],
                                             recv_sem=ra_ref.at[j], device_id=(x, y, c), device_id_type=MESH)
            d.wait_recv()
            d.wait_send()

    arrs = (dwint, dwout, dw2, rx1)
    return pl.pallas_call(
        body, name="rsa_level1_wait", out_shape=tuple(_hbm_like(a) for a in arrs),
        in_specs=(HBM_SPEC,) * 4 + (SEM_SPEC, SEM_SPEC, ANY_SPEC), out_specs=(HBM_SPEC,) * 4,
        input_output_aliases={0: 0, 1: 1, 2: 2, 3: 3},
        compiler_params=pltpu.CompilerParams(has_side_effects=SPLIT_EFFECT),
    )(*arrs, sa, ra, after)


def _rsa_chip_sums(dwint, dwout, dw2, rx1):
    nblk = RSA_ROWS // RSA_BLOCK

    def body(di_ref, do_ref, d2_ref, rx1_ref, tx_ref, acc_ref, own_buf, rx_buf, tx_buf, acc_buf, lsems):
        x, y, c = _position()
        srcs = _weight_pieces(wi_ref=di_ref, wo_ref=do_ref, w2_ref=d2_ref)
        for j, chip in enumerate([(x, y), (1 - x, y), (x, 1 - y), (1 - x, 1 - y)]):
            loads = [pltpu.make_async_copy(_block_rows(srcs, k, (*chip, c)), _rsa_rows(own_buf, k), lsems.at[0])
                     for k in RSA_PIECES]
            got = pltpu.make_async_copy(rx1_ref.at[j], rx_buf, lsems.at[1])
            for cp in loads + [got]:
                cp.start()
            pltpu.make_async_copy(rx_buf, rx_buf, lsems.at[0]).wait()
            got.wait()

            def add(i, carry, j=j):
                rows = pl.ds(pl.multiple_of(i * RSA_BLOCK, 16), RSA_BLOCK)
                tot = own_buf[rows, :].astype(f32) + rx_buf[rows, :].astype(f32)
                if j == 0:
                    acc_buf[rows, :] = tot
                else:
                    tx_buf[rows, :] = tot.astype(bf16)
                return carry

            lax.fori_loop(0, nblk, add, 0)
            out = (pltpu.make_async_copy(acc_buf, acc_ref, lsems.at[2]) if j == 0
                   else pltpu.make_async_copy(tx_buf, tx_ref.at[j - 1], lsems.at[2]))
            out.start()
            out.wait()

    return pl.pallas_call(
        body, in_specs=[ANY_SPEC] * 4, out_specs=[ANY_SPEC] * 2,
        out_shape=(SDS((3, RSA_ROWS, D), bf16), SDS((RSA_ROWS, D), f32)),
        scratch_shapes=[pltpu.VMEM((RSA_ROWS, D), bf16), pltpu.VMEM((RSA_ROWS, D), bf16),
                        pltpu.VMEM((RSA_ROWS, D), bf16), pltpu.VMEM((RSA_ROWS, D), f32),
                        pltpu.SemaphoreType.DMA((3,))],
        compiler_params=_cparams(None, VMEM_LIMIT_V7X), name="rsa_chip_sums")(dwint, dwout, dw2, rx1)


def _rsa_level2_start(tx, rx2, thru):
    def body(tx_ref, rx2_ref, thru_ref, sb, rb, tx_o, rx2_o, thru_o):
        x, y, c = _position()
        for j, chip in enumerate([(1 - x, y), (x, 1 - y), (1 - x, 1 - y)]):
            pltpu.make_async_remote_copy(src_ref=tx_ref.at[j], dst_ref=rx2_ref.at[j], send_sem=sb.at[j],
                                         recv_sem=rb.at[j], device_id=(*chip, c), device_id_type=MESH).start()

    dma = pltpu.SemaphoreType.DMA
    arrs = (tx, rx2, thru)
    return pl.pallas_call(
        body, name="rsa_level2_start", out_shape=(dma((3,)), dma((3,))) + tuple(_hbm_like(a) for a in arrs),
        in_specs=(HBM_SPEC,) * 3, out_specs=(SEM_SPEC,) * 2 + (HBM_SPEC,) * 3,
        input_output_aliases={0: 2, 1: 3, 2: 4},
        compiler_params=pltpu.CompilerParams(has_side_effects=SPLIT_EFFECT),
    )(*[_in_hbm(a) for a in arrs])


def _rsa_level2_wait(sb, rb, tx, rx2, after):
    def body(tx_ref, rx2_ref, sb_ref, rb_ref, after_ref, rx2_o):
        x, y, c = _position()
        for j in range(3):
            d = pltpu.make_async_remote_copy(src_ref=tx_ref.at[j], dst_ref=rx2_ref.at[j], send_sem=sb_ref.at[j],
                                             recv_sem=rb_ref.at[j], device_id=(x, y, c), device_id_type=MESH)
            d.wait_recv()
            d.wait_send()

    return pl.pallas_call(
        body, name="rsa_level2_wait", out_shape=_hbm_like(rx2),
        in_specs=(HBM_SPEC, HBM_SPEC, SEM_SPEC, SEM_SPEC, ANY_SPEC), out_specs=HBM_SPEC,
        input_output_aliases={1: 0},
        compiler_params=pltpu.CompilerParams(has_side_effects=SPLIT_EFFECT),
    )(tx, rx2, sb, rb, after)


def _rsa_total(acc, rx2):
    def body(a_ref, r_ref, o_ref):
        o_ref[...] = ((a_ref[...] + r_ref[0].astype(f32)) + r_ref[1].astype(f32)) + r_ref[2].astype(f32)

    return pl.pallas_call(
        body, grid=(RSA_ROWS // RSA_BLOCK,),
        in_specs=[pl.BlockSpec((RSA_BLOCK, D), lambda i: (i, 0)), pl.BlockSpec((3, RSA_BLOCK, D), lambda i: (0, i, 0))],
        out_specs=pl.BlockSpec((RSA_BLOCK, D), lambda i: (i, 0)),
        out_shape=SDS((RSA_ROWS, D), f32), name="rsa_total")(acc, rx2)


def _adamw_math(w, g, m, v):
    m = ADAM_B1 * m + (1.0 - ADAM_B1) * g
    v = ADAM_B2 * v + (1.0 - ADAM_B2) * (g * g)
    m_hat = m / (1.0 - ADAM_B1 ** ADAM_STEP)
    v_hat = v / (1.0 - ADAM_B2 ** ADAM_STEP)
    delta = -ADAM_LR * (m_hat / (jnp.sqrt(v_hat) + ADAM_EPS) + ADAM_WD * w)
    return delta, m, v


def _adamw_big(ws, ms, vs, red1, red_rest):
    npiece = len(BIG)
    rmax = max(PIECE_ROWS)

    def body(*refs):
        ins = (refs[0:npiece], refs[npiece:2 * npiece], refs[2 * npiece:3 * npiece])
        red1_ref, rest_ref = refs[3 * npiece:3 * npiece + 2]
        out_refs = refs[3 * npiece + 2:7 * npiece + 2]
        inb, outb, in_sems, out_sems = refs[7 * npiece + 2:]

        def grad_rows(k):
            if k in G1_PIECES:
                return red1_ref.at[pl.ds(PIECE_OFF[k], PIECE_ROWS[k]), :]
            return _rsa_rows(rest_ref, k)

        def loads(k):
            s, r = k % 2, PIECE_ROWS[k]
            cps = [pltpu.make_async_copy(ins[q][k].at[0], inb.at[s, q, pl.ds(0, r), :], in_sems.at[4 * s + q])
                   for q in range(3)]
            cps.append(pltpu.make_async_copy(grad_rows(k), inb.at[s, 3, pl.ds(0, r), :], in_sems.at[4 * s + 3]))
            return cps

        def stores(k):
            s, r = k % 2, PIECE_ROWS[k]
            return [pltpu.make_async_copy(outb.at[s, q, pl.ds(0, r), :], out_refs[q * npiece + k].at[0],
                                          out_sems.at[4 * s + q]) for q in range(4)]

        for cp in loads(0):
            cp.start()
        for k in range(npiece):
            s, r = k % 2, PIECE_ROWS[k]
            if k + 1 < npiece:
                for cp in loads(k + 1):
                    cp.start()
            for cp in loads(k):
                cp.wait()
            if k >= 2:
                for cp in stores(k - 2):
                    cp.wait()
            g = inb[s, 3, 0:r, :]
            d, nm, nv = _adamw_math(inb[s, 0, 0:r, :], g, inb[s, 1, 0:r, :], inb[s, 2, 0:r, :])
            outb[s, 0, 0:r, :] = g
            outb[s, 1, 0:r, :] = d
            outb[s, 2, 0:r, :] = nm
            outb[s, 3, 0:r, :] = nv
            for cp in stores(k):
                cp.start()
        for k in (npiece - 2, npiece - 1):
            for cp in stores(k):
                cp.wait()

    hbm = pl.BlockSpec(memory_space=pl.ANY)
    outs = pl.pallas_call(
        body, in_specs=[hbm] * (3 * npiece + 2), out_specs=[hbm] * (4 * npiece),
        out_shape=tuple(SDS(w.shape, f32) for _ in range(4) for w in ws),
        scratch_shapes=[pltpu.VMEM((2, 4, rmax, D), f32), pltpu.VMEM((2, 4, rmax, D), f32),
                        pltpu.SemaphoreType.DMA((8,)), pltpu.SemaphoreType.DMA((8,))],
        compiler_params=_cparams(None, VMEM_LIMIT_V7X), name="adamw_big")(*ws, *ms, *vs, red1, red_rest)
    return [list(outs[q * npiece:(q + 1) * npiece]) for q in range(4)]


def _adamw_small(w, m, v, g, name):
    def body(w_ref, m_ref, v_ref, g_ref, d_ref, nm_ref, nv_ref):
        d, nm, nv = _adamw_math(w_ref[...], g_ref[...], m_ref[...], v_ref[...])
        d_ref[...] = d
        nm_ref[...] = nm
        nv_ref[...] = nv

    return pl.pallas_call(
        body, out_shape=tuple(SDS(w.shape, f32) for _ in range(3)), name=name)(w, m, v, g)


WEIGHTS = ("ffn1_norm", "ffn1_w_gate", "ffn1_w_up", "ffn1_w_down", "mix_norm", "w_in", "q_norm", "k_norm",
           "attn_sinks", "rel_bias", "pool_w", "pool_scale", "w_out", "ffn2_norm", "ffn2_w_gate", "ffn2_w_up",
           "ffn2_w_down")
BIG = (("ffn1_w_gate", True), ("ffn1_w_up", True), ("ffn1_w_down", False), ("w_in", True), ("w_out", False),
       ("ffn2_w_gate", True), ("ffn2_w_up", True), ("ffn2_w_down", False))


def kernel(x, ffn1_norm, ffn1_w_gate, ffn1_w_up, ffn1_w_down, mix_norm, w_in, q_norm, k_norm, attn_sinks, rel_bias, pool_w, pool_scale, w_out, ffn2_norm, ffn2_w_gate, ffn2_w_up, ffn2_w_down, loss_target, m_ffn1_norm, m_ffn1_w_gate, m_ffn1_w_up, m_ffn1_w_down, m_mix_norm, m_w_in, m_q_norm, m_k_norm, m_attn_sinks, m_rel_bias, m_pool_w, m_pool_scale, m_w_out, m_ffn2_norm, m_ffn2_w_gate, m_ffn2_w_up, m_ffn2_w_down, v_ffn1_norm, v_ffn1_w_gate, v_ffn1_w_up, v_ffn1_w_down, v_mix_norm, v_w_in, v_q_norm, v_k_norm, v_attn_sinks, v_rel_bias, v_pool_w, v_pool_scale, v_w_out, v_ffn2_norm, v_ffn2_w_gate, v_ffn2_w_up, v_ffn2_w_down):
    args = dict(locals())
    w = {n: args[n] for n in WEIGHTS}
    m = {n: args["m_" + n] for n in WEIGHTS}
    v = {n: args["v_" + n] for n in WEIGHTS}

    as_rows = lambda a, tr: jnp.swapaxes(a, 1, 2) if tr else a
    shard = jnp.concatenate([as_rows(w[n], tr)[0].astype(bf16) for n, tr in BIG], axis=0)
    exchanges = _GatheredWeights(shard)
    gx, (dw1, _, _, _), small = _local_step(
        x[0], loss_target[0], exchanges, ffn1_norm, mix_norm, ffn2_norm, q_norm, k_norm, attn_sinks,
        rel_bias, pool_w[0], pool_scale)

    red1, small_tot = _reduce_scatter_ffn1(dw1, _pack_small(small))
    red_rest = exchanges.mix_ffn2_grads_total(red1)

    grads, deltas, new_m, new_v = {}, {}, {}, {}
    big_out = _adamw_big(*[[as_rows(t[n], tr) for n, tr in BIG] for t in (w, m, v)], red1, red_rest)
    for k, (n, tr) in enumerate(BIG):
        grads[n], deltas[n], new_m[n], new_v[n] = [as_rows(o[k], tr) for o in big_out]
    small_names = [n for n in SMALL_NAMES if n != "loss"]
    ds, nms, nvs = _adamw_small(_pack_small({n: w[n] for n in small_names}), _pack_small({n: m[n] for n in small_names}),
                                _pack_small({n: v[n] for n in small_names}), small_tot, "adamw_small")
    for n in small_names:
        grads[n] = _unpack_small(small_tot, n)
        deltas[n], new_m[n], new_v[n] = _unpack_small(ds, n), _unpack_small(nms, n), _unpack_small(nvs, n)
    loss = small_tot[LOSS_ROW, 0]
    return (loss, gx[None], *[grads[n] for n in WEIGHTS], *[deltas[n] for n in WEIGHTS],
            *[new_m[n] for n in WEIGHTS], *[new_v[n] for n in WEIGHTS])
```

```python
import functools

import jax
import jax.numpy as jnp
import numpy as np
from jax import lax
from jax.experimental import pallas as pl
from jax.experimental.pallas import tpu as pltpu

f32, bf16, i32 = jnp.float32, jnp.bfloat16, jnp.int32
SDS = jax.ShapeDtypeStruct

D = 1024
F = 2816
HD = 64
NH = 8
NKV = 2
GQA = NH // NKV
DATTN = NH * HD
DKV = NKV * HD
DPOOL = 512
POOL_WINDOWS = (2, 4, 8, 16)
PGD = DPOOL // len(POOL_WINDOWS)
DIN = DATTN + 2 * DKV + DPOOL
DMIX = DATTN + DPOOL
BLK = 128
NBUCK = 32
MAX_DISTANCE = 128
EPS = 1e-6
NEG = -1e30
SCALE = HD ** -0.5

ADAM_LR, ADAM_B1, ADAM_B2, ADAM_EPS, ADAM_WD, ADAM_STEP = 0.001, 0.9, 0.999, 1e-08, 0.01, 10

NDEV = 8
FS = F // NDEV
INS = DIN // NDEV
OUTS = DMIX // NDEV
PIECE_ROWS = (FS, FS, FS, INS, OUTS, FS, FS, FS)
PIECE_OFF = tuple(int(v) for v in np.cumsum((0,) + PIECE_ROWS[:-1]))
PACK_ROWS = sum(PIECE_ROWS)

VMEM_LIMIT_V7X = 56 * 1024 * 1024

MESH = pl.DeviceIdType.MESH


def _cparams(sem=None, vmem=None):
    return pltpu.CompilerParams(dimension_semantics=sem, vmem_limit_bytes=vmem)


def _nt(a, b):
    return lax.dot_general(a, b, (((1,), (1,)), ((), ())), preferred_element_type=f32)


def _tn(a, b):
    return lax.dot_general(a, b, (((0,), (0,)), ((), ())), preferred_element_type=f32)


def _nn(a, b):
    return jnp.dot(a, b, preferred_element_type=f32)


def _sigmoid(x):
    return 1.0 / (1.0 + jnp.exp(-x))


def _norm_fwd(x, g, name):
    T = x.shape[0]
    tm = min(512, T)

    def body(x_ref, g_ref, h_ref):
        xv = x_ref[...]
        r = lax.rsqrt(jnp.mean(xv * xv, axis=-1, keepdims=True) + EPS)
        h_ref[...] = (xv * r * g_ref[...]).astype(bf16)

    return pl.pallas_call(
        body, grid=(T // tm,),
        in_specs=[pl.BlockSpec((tm, D), lambda i: (i, 0)), pl.BlockSpec((1, D), lambda i: (0, 0))],
        out_specs=pl.BlockSpec((tm, D), lambda i: (i, 0)),
        out_shape=SDS((T, D), bf16), name=name)(x, g)


def _norm_bwd(dh, x, g, dres, out_scale, name):
    T = x.shape[0]
    tm = min(512, T)

    def body(dh_ref, x_ref, g_ref, dr_ref, dx_ref, dxb_ref, dg_ref):
        i = pl.program_id(0)
        xv = x_ref[...]
        r = lax.rsqrt(jnp.mean(xv * xv, axis=-1, keepdims=True) + EPS)
        xh = xv * r
        dhv = dh_ref[...]
        dxh = dhv * g_ref[...]
        dx = dr_ref[...] + r * (dxh - xh * jnp.mean(dxh * xh, axis=-1, keepdims=True))
        dx_ref[...] = dx
        dxb_ref[...] = (out_scale * dx).astype(bf16)
        dg = jnp.sum(dhv * xh, axis=0, keepdims=True)

        @pl.when(i == 0)
        def _():
            dg_ref[...] = dg

        @pl.when(i > 0)
        def _():
            dg_ref[...] += dg

    tok = pl.BlockSpec((tm, D), lambda i: (i, 0))
    vec = pl.BlockSpec((1, D), lambda i: (0, 0))
    return pl.pallas_call(
        body, grid=(T // tm,),
        in_specs=[tok, tok, vec, tok], out_specs=[tok, tok, vec],
        out_shape=(SDS((T, D), f32), SDS((T, D), bf16), SDS((1, D), f32)),
        compiler_params=_cparams(("arbitrary",)), name=name)(dh, x, g, dres)


FFN_ROW_CHUNK = 256


def _ffn_tiles(T):
    return min(1024, T), 256


def _ffn_fwd(h, w, x, target, name):
    T = h.shape[0]
    tm, tf = _ffn_tiles(T)
    nf = F // tf
    with_loss = target is not None

    def body(*refs):
        if with_loss:
            h_ref, w_ref, x_hbm, t_hbm, xo_ref, g_ref, u_ref, dyb_ref, loss_ref, tbuf, sem = refs
        else:
            h_ref, w_ref, x_hbm, xo_ref, g_ref, u_ref, sem = refs
        fi = pl.program_id(0)

        @pl.when(fi == 0)
        def _():
            cp = pltpu.make_async_copy(x_hbm, xo_ref, sem)
            cp.start()
            cp.wait()

        wgu = w_ref[0:2].reshape(2 * tf, D)
        for r in range(0, T, tm):
            rows = slice(r, r + tm)
            gu = _nt(h_ref[rows, :], wgu)
            gate, up = gu[:, :tf], gu[:, tf:]
            act = gate * _sigmoid(gate) * up
            g_ref[0, rows, :] = gate.astype(bf16)
            u_ref[0, rows, :] = up.astype(bf16)
            xo_ref[rows, :] += _nn((0.5 * act).astype(bf16), w_ref[2])

        if with_loss:
            @pl.when(fi == nf - 1)
            def _():
                lanes = jnp.zeros((1, 128), f32)
                for r in range(0, T, tm):
                    rows = slice(r, r + tm)
                    cp = pltpu.make_async_copy(t_hbm.at[pl.ds(r, tm), :], tbuf, sem)
                    cp.start()
                    cp.wait()
                    e = xo_ref[rows, :] - tbuf[...]
                    dy = e * (1.0 / D)
                    xo_ref[rows, :] = dy
                    dyb_ref[rows, :] = (0.5 * dy).astype(bf16)
                    col = jnp.sum(e * e, axis=0, keepdims=True) * (0.5 / D)
                    for k in range(D // 128):
                        lanes = lanes + col[:, 128 * k:128 * (k + 1)]
                loss_ref[...] = lanes

    tok = pl.BlockSpec((T, D), lambda f: (0, 0))
    act_spec = pl.BlockSpec((1, T, tf), lambda f: (f, 0, 0))
    hbm = pl.BlockSpec(memory_space=pl.ANY)
    in_specs = [tok, pl.BlockSpec((3, tf, D), lambda f: (0, f, 0)), hbm]
    out_specs = [tok, act_spec, act_spec]
    out_shape = [SDS((T, D), f32), SDS((nf, T, tf), bf16), SDS((nf, T, tf), bf16)]
    scratch = [pltpu.SemaphoreType.DMA]
    args = [h, w, x]
    if with_loss:
        in_specs.append(hbm)
        args.append(target)
        out_specs += [tok, pl.BlockSpec((1, 128), lambda f: (0, 0))]
        out_shape += [SDS((T, D), bf16), SDS((1, 128), f32)]
        scratch = [pltpu.VMEM((tm, D), f32)] + scratch
    return pl.pallas_call(
        body, grid=(nf,), in_specs=in_specs, out_specs=out_specs, out_shape=tuple(out_shape), scratch_shapes=scratch,
        compiler_params=_cparams(("arbitrary",), VMEM_LIMIT_V7X), name=name)(*args)


def _ffn_bwd(dob, h, gate, up, w, name):
    T = h.shape[0]
    _, tf = _ffn_tiles(T)
    nf = F // tf

    def body(do_hbm, h_hbm, g_ref, u_ref, w_ref, dh_hbm, dw_ref, do_v, h_v, dh_acc, dgu_s, act_s, sems):
        fi = pl.program_id(0)

        @pl.when(fi == 0)
        def _():
            loads = [pltpu.make_async_copy(do_hbm, do_v, sems.at[0]), pltpu.make_async_copy(h_hbm, h_v, sems.at[1])]
            for cp in loads:
                cp.start()
            dh_acc[...] = jnp.zeros_like(dh_acc)
            for cp in loads:
                cp.wait()

        wgu = w_ref[0:2].reshape(2 * tf, D)
        for r in range(0, T, FFN_ROW_CHUNK):
            rows = slice(r, r + FFN_ROW_CHUNK)
            dov = do_v[rows, :]
            gv = g_ref[0, rows, :].astype(f32)
            uv = u_ref[0, rows, :].astype(f32)
            sg = _sigmoid(gv)
            sil = gv * sg
            dact = _nt(dov, w_ref[2])
            dup = dact * sil
            dgate = dact * uv * (sg * (1.0 + gv * (1.0 - sg)))
            dgu = jnp.concatenate([dgate.astype(bf16), dup.astype(bf16)], axis=1)
            dgu_s[rows, :] = dgu
            act_s[rows, :] = (sil * uv).astype(bf16)
            dh_acc[rows, :] += _nn(dgu, wgu)
        dw_ref[0:2] = _tn(dgu_s[...], h_v[...]).reshape(2, tf, D).astype(bf16)
        dw_ref[2] = _tn(act_s[...], do_v[...]).astype(bf16)

        @pl.when(fi == nf - 1)
        def _():
            out = pltpu.make_async_copy(dh_acc, dh_hbm, sems.at[0])
            out.start()
            out.wait()

    act_spec = pl.BlockSpec((1, T, tf), lambda f: (f, 0, 0))
    wspec = pl.BlockSpec((3, tf, D), lambda f: (0, f, 0))
    hbm = pl.BlockSpec(memory_space=pl.ANY)
    return pl.pallas_call(
        body, grid=(nf,),
        in_specs=[hbm, hbm, act_spec, act_spec, wspec],
        out_specs=[hbm, wspec],
        out_shape=(SDS((T, D), f32), SDS((3, F, D), bf16)),
        scratch_shapes=[pltpu.VMEM((T, D), bf16), pltpu.VMEM((T, D), bf16), pltpu.VMEM((T, D), f32),
                        pltpu.VMEM((T, 2 * tf), bf16), pltpu.VMEM((T, tf), bf16), pltpu.SemaphoreType.DMA((2,))],
        compiler_params=_cparams(("arbitrary",), VMEM_LIMIT_V7X), name=name)(dob, h, gate, up, w)


def _in_proj_fwd(h, wint, name):
    T = h.shape[0]
    tm = min(512, T)

    def body(h_ref, w_ref, z_ref):
        z_ref[...] = _nt(h_ref[...], w_ref[...])

    return pl.pallas_call(
        body, grid=(T // tm,),
        in_specs=[pl.BlockSpec((tm, D), lambda i: (i, 0)), pl.BlockSpec((DIN, D), lambda i: (0, 0))],
        out_specs=pl.BlockSpec((tm, DIN), lambda i: (i, 0)),
        out_shape=SDS((T, DIN), f32), name=name)(h, wint)


def _in_proj_bwd(dz, wint, h, name):
    T = h.shape[0]
    tm = min(512, T)
    nt = T // tm

    def body(dz_ref, w_ref, h_ref, dh_ref, dw_ref, acc):
        i = pl.program_id(0)
        dzb = dz_ref[...].astype(bf16)
        dh_ref[...] = _nn(dzb, w_ref[...])
        part = _tn(dzb, h_ref[...])

        @pl.when(i == 0)
        def _():
            acc[...] = part

        @pl.when(i > 0)
        def _():
            acc[...] += part

        @pl.when(i == nt - 1)
        def _():
            dw_ref[...] = acc[...].astype(bf16)

    wspec = pl.BlockSpec((DIN, D), lambda i: (0, 0))
    return pl.pallas_call(
        body, grid=(nt,),
        in_specs=[pl.BlockSpec((tm, DIN), lambda i: (i, 0)), wspec, pl.BlockSpec((tm, D), lambda i: (i, 0))],
        out_specs=[pl.BlockSpec((tm, D), lambda i: (i, 0)), wspec],
        out_shape=(SDS((T, D), f32), SDS((DIN, D), bf16)),
        scratch_shapes=[pltpu.VMEM((DIN, D), f32)],
        compiler_params=_cparams(("arbitrary",)), name=name)(dz, wint, h)


def _out_proj_fwd(ymix, wout, x, g, name):
    T = x.shape[0]
    tm = min(512, T)

    def body(y_ref, w_ref, x_ref, g_ref, o_ref, h_ref):
        o = x_ref[...] + _nn(y_ref[...], w_ref[...])
        o_ref[...] = o
        r = lax.rsqrt(jnp.mean(o * o, axis=-1, keepdims=True) + EPS)
        h_ref[...] = (o * r * g_ref[...]).astype(bf16)

    tok = pl.BlockSpec((tm, D), lambda i: (i, 0))
    return pl.pallas_call(
        body, grid=(T // tm,),
        in_specs=[pl.BlockSpec((tm, DMIX), lambda i: (i, 0)), pl.BlockSpec((DMIX, D), lambda i: (0, 0)), tok,
                  pl.BlockSpec((1, D), lambda i: (0, 0))],
        out_specs=[tok, tok], out_shape=(SDS((T, D), f32), SDS((T, D), bf16)), name=name)(ymix, wout, x, g)


def _out_proj_bwd(dxb, wout, ymix, name):
    T = dxb.shape[0]
    tm = min(512, T)
    nt = T // tm

    def body(dx_ref, w_ref, y_ref, dy_ref, dw_ref, acc):
        i = pl.program_id(0)
        dxv = dx_ref[...]
        dy_ref[...] = _nt(dxv, w_ref[...])
        part = _tn(y_ref[...], dxv)

        @pl.when(i == 0)
        def _():
            acc[...] = part

        @pl.when(i > 0)
        def _():
            acc[...] += part

        @pl.when(i == nt - 1)
        def _():
            dw_ref[...] = acc[...].astype(bf16)

    wspec = pl.BlockSpec((DMIX, D), lambda i: (0, 0))
    return pl.pallas_call(
        body, grid=(nt,),
        in_specs=[pl.BlockSpec((tm, D), lambda i: (i, 0)), wspec, pl.BlockSpec((tm, DMIX), lambda i: (i, 0))],
        out_specs=[pl.BlockSpec((tm, DMIX), lambda i: (i, 0)), wspec],
        out_shape=(SDS((T, DMIX), f32), SDS((DMIX, D), bf16)),
        scratch_shapes=[pltpu.VMEM((DMIX, D), f32)],
        compiler_params=_cparams(("arbitrary",)), name=name)(dxb, wout, ymix)


def _t5_bucket_table():
    ql = np.arange(BLK)[:, None]
    kl = np.arange(2 * BLK)[None, :]
    n = np.maximum(ql + BLK - kl, 0)
    max_exact = NBUCK // 2
    large = max_exact + (np.log(np.maximum(n, 1) / max_exact) / np.log(MAX_DISTANCE / max_exact)
                         * (NBUCK - max_exact)).astype(np.int32)
    large = np.minimum(large, NBUCK - 1)
    return np.where(n < max_exact, n, large).astype(np.int32)


def _fill_bias(bk_ref, rb_ref, bias_scr):
    bk = bk_ref[...]
    for h in range(NH):
        def step(b, acc, h=h):
            return acc + jnp.where(bk == b, rb_ref[b, h], 0.0)
        bias_scr[h] = lax.fori_loop(0, NBUCK, step, jnp.zeros((BLK, 2 * BLK), f32))


MIX_SUB = 4


class _Window:
    def __init__(self, zc_ref, zp_ref, n, s):
        self.blk = n * MIX_SUB + s
        self.first_in_step = s == 0
        self.cur = lambda a, b: zc_ref[s * BLK:(s + 1) * BLK, a:b]
        self.prev = (lambda a, b: zp_ref[:, a:b]) if s == 0 else (lambda a, b: zc_ref[(s - 1) * BLK:s * BLK, a:b])


def _attn_probs(win, kh, qg, kg, sk_ref, bias_scr):
    n = win.blk
    kc = DATTN + HD * kh
    vc = DATTN + DKV + HD * kh
    kx = jnp.concatenate([win.prev(kc, kc + HD), win.cur(kc, kc + HD)], axis=0)
    vx = jnp.concatenate([win.prev(vc, vc + HD), win.cur(vc, vc + HD)], axis=0)
    qx = jnp.concatenate([win.cur(HD * (GQA * kh + g), HD * (GQA * kh + g + 1)) for g in range(GQA)], axis=0)
    rq = lax.rsqrt(jnp.mean(qx * qx, axis=-1, keepdims=True) + EPS)
    rk = lax.rsqrt(jnp.mean(kx * kx, axis=-1, keepdims=True) + EPS)
    qhat, khat = qx * rq, kx * rk
    qnb, knb = (qhat * qg).astype(bf16), (khat * kg).astype(bf16)
    s = _nt(qnb, knb) * SCALE + bias_scr[GQA * kh:GQA * (kh + 1)].reshape(GQA * BLK, 2 * BLK)
    row = lax.broadcasted_iota(i32, (GQA * BLK, 2 * BLK), 0) & (BLK - 1)
    col = lax.broadcasted_iota(i32, (GQA * BLK, 2 * BLK), 1)
    mask = (col > row) & (col <= row + BLK) & ((col >= BLK) | (n > 0))
    s = jnp.where(mask, s, NEG)
    ridx = lax.broadcasted_iota(i32, (GQA * BLK, 1), 0)
    sink = jnp.full((GQA * BLK, 1), sk_ref[GQA * kh + GQA - 1], f32)
    for g in range(GQA - 2, -1, -1):
        sink = jnp.where(ridx < (g + 1) * BLK, sk_ref[GQA * kh + g], sink)
    m = jnp.maximum(jnp.max(s, axis=-1, keepdims=True), sink)
    e = jnp.exp(s - m)
    es = jnp.exp(sink - m)
    den = jnp.sum(e, axis=-1, keepdims=True) + es
    return dict(p=e / den, psink=es / den, qhat=qhat, khat=khat, rq=rq, rk=rk, qnb=qnb, knb=knb, vb=vx.astype(bf16))


def _pool_group(win, g, w):
    n = win.blk
    c0 = DATTN + 2 * DKV + PGD * g
    uc = win.cur(c0, c0 + PGD)
    up = jnp.where(n > 0, win.prev(c0, c0 + PGD), 0.0)
    ue = jnp.concatenate([up, uc], axis=0)
    hi = ue.astype(bf16)
    lo = (ue - hi.astype(f32)).astype(bf16)
    t = lax.broadcasted_iota(i32, (BLK, 2 * BLK), 0)
    s = lax.broadcasted_iota(i32, (BLK, 2 * BLK), 1)
    band = jnp.where((s <= t + BLK) & (s > t + BLK - w), 1.0, 0.0).astype(bf16)
    sm = _nn(band, hi) + _nn(band, lo)
    pos = n * BLK + lax.broadcasted_iota(i32, (BLK, 1), 0) + 1
    cnt = jnp.minimum(pos, w).astype(f32)
    return sm / cnt - uc, band, cnt


def _mix_fwd(z, qg, kg, sinks, relb, bucket, pool_w, pscale, name):
    T = z.shape[0]
    step_rows = MIX_SUB * BLK
    nsteps = T // step_rows

    def body(zc_ref, zp_ref, qg_ref, kg_ref, sk_ref, rb_ref, bk_ref, pw_ref, ps_ref, y_ref, bias_scr, yacc):
        n = pl.program_id(0)

        @pl.when(n == 0)
        def _():
            _fill_bias(bk_ref, rb_ref, bias_scr)

        for s in range(MIX_SUB):
            win = _Window(zc_ref, zp_ref, n, s)
            rows = slice(s * BLK, (s + 1) * BLK)
            for kh in range(NKV):
                a = _attn_probs(win, kh, qg_ref[...], kg_ref[...], sk_ref, bias_scr)
                o = _nn(a["p"].astype(bf16), a["vb"])
                for g in range(GQA):
                    hc = HD * (GQA * kh + g)
                    yacc[rows, hc:hc + HD] = o[g * BLK:(g + 1) * BLK]
            for g, w in enumerate(POOL_WINDOWS):
                pooled, _, _ = _pool_group(win, g, w)
                yp = _nn(pooled.astype(bf16), pw_ref[g].astype(bf16)) * ps_ref[:, PGD * g:PGD * (g + 1)]
                yacc[rows, DATTN + PGD * g:DATTN + PGD * (g + 1)] = yp
        y_ref[...] = yacc[...].astype(bf16)

    full = lambda *shape: pl.BlockSpec(shape, lambda n: (0,) * len(shape))
    smem = pl.BlockSpec(memory_space=pltpu.SMEM)
    return pl.pallas_call(
        body, grid=(nsteps,),
        in_specs=[pl.BlockSpec((step_rows, DIN), lambda n: (n, 0)),
                  pl.BlockSpec((BLK, DIN), lambda n: (jnp.maximum(n * MIX_SUB - 1, 0), 0)),
                  full(1, HD), full(1, HD), smem, smem, full(BLK, 2 * BLK),
                  full(len(POOL_WINDOWS), PGD, PGD), full(1, DPOOL)],
        out_specs=pl.BlockSpec((step_rows, DMIX), lambda n: (n, 0)),
        out_shape=SDS((T, DMIX), bf16),
        scratch_shapes=[pltpu.VMEM((NH, BLK, 2 * BLK), f32), pltpu.VMEM((step_rows, DMIX), f32)],
        compiler_params=_cparams(("arbitrary",)), name=name)(z, z, qg, kg, sinks, relb, bucket, pool_w, pscale)


def _mix_bwd(z, dy, qg, kg, sinks, relb, bucket, pool_w, pscale, name):
    T = z.shape[0]
    step_rows = MIX_SUB * BLK
    nsteps = T // step_rows

    def body(zc_ref, zp_ref, dy_ref, qg_ref, kg_ref, sk_ref, rb_ref, bk_ref, pw_ref, ps_ref,
             dz_ref, dqg_ref, dkg_ref, dsk_ref, drb_ref, dpw_ref, dps_ref, bias_scr, dbias_scr):
        n = pl.program_id(0)

        @pl.when(n == 0)
        def _():
            _fill_bias(bk_ref, rb_ref, bias_scr)
            dbias_scr[...] = jnp.zeros_like(dbias_scr)
            dqg_ref[...] = jnp.zeros_like(dqg_ref)
            dkg_ref[...] = jnp.zeros_like(dkg_ref)
            dsk_ref[...] = jnp.zeros_like(dsk_ref)
            dpw_ref[...] = jnp.zeros_like(dpw_ref)
            dps_ref[...] = jnp.zeros_like(dps_ref)

        qg, kg = qg_ref[...], kg_ref[...]
        lane = lax.broadcasted_iota(i32, (1, 128), 1)
        dsk = jnp.zeros((1, 128), f32)
        for s in range(MIX_SUB):
            win = _Window(zc_ref, zp_ref, n, s)
            blk = win.blk
            rows = pl.ds(pl.multiple_of(blk * BLK, BLK), BLK)
            prow = pl.ds(pl.multiple_of(jnp.maximum(blk - 1, 0) * BLK, BLK), BLK)
            dyr = slice(s * BLK, (s + 1) * BLK)

            def into_prev(fn, s=s):
                if s == 0:
                    pl.when(n > 0)(fn)
                else:
                    fn()

            for kh in range(NKV):
                a = _attn_probs(win, kh, qg, kg, sk_ref, bias_scr)
                p = a["p"]
                do = jnp.concatenate([dy_ref[dyr, HD * (GQA * kh + g):HD * (GQA * kh + g + 1)] for g in range(GQA)],
                                     axis=0).astype(bf16)
                dv = _tn(p.astype(bf16), do)
                dp = _nt(do, a["vb"])
                delta = jnp.sum(p * dp, axis=-1, keepdims=True)
                ds = p * (dp - delta)
                sinkterm = a["psink"] * delta
                for g in range(GQA):
                    h = GQA * kh + g
                    dbias_scr[h] += ds[g * BLK:(g + 1) * BLK]
                    tot = jnp.sum(sinkterm[g * BLK:(g + 1) * BLK], axis=0, keepdims=True)
                    dsk = dsk - jnp.where(lane == h, tot, 0.0)
                dsb = ds.astype(bf16)
                dqn = _nn(dsb, a["knb"]) * SCALE
                dkn = _tn(dsb, a["qnb"]) * SCALE
                qhat, khat = a["qhat"], a["khat"]
                dqg_ref[...] += jnp.sum(dqn * qhat, axis=0, keepdims=True)
                dkg_ref[...] += jnp.sum(dkn * khat, axis=0, keepdims=True)
                dqh = dqn * qg
                dq = a["rq"] * (dqh - qhat * jnp.mean(dqh * qhat, axis=-1, keepdims=True))
                dkh = dkn * kg
                dk = a["rk"] * (dkh - khat * jnp.mean(dkh * khat, axis=-1, keepdims=True))
                kc = DATTN + HD * kh
                vc = DATTN + DKV + HD * kh
                for g in range(GQA):
                    hc = HD * (GQA * kh + g)
                    dz_ref[rows, hc:hc + HD] = dq[g * BLK:(g + 1) * BLK]
                dz_ref[rows, kc:kc + HD] = dk[BLK:2 * BLK]
                dz_ref[rows, vc:vc + HD] = dv[BLK:2 * BLK]

                def kv_prev(dk=dk, dv=dv, kc=kc, vc=vc, prow=prow):
                    dz_ref[prow, kc:kc + HD] += dk[0:BLK]
                    dz_ref[prow, vc:vc + HD] += dv[0:BLK]

                into_prev(kv_prev)

            for g, w in enumerate(POOL_WINDOWS):
                c0 = DATTN + 2 * DKV + PGD * g
                pooled, band, cnt = _pool_group(win, g, w)
                pb = pooled.astype(bf16)
                wb = pw_ref[g].astype(bf16)
                dyp = dy_ref[dyr, DATTN + PGD * g:DATTN + PGD * (g + 1)]
                ypre = _nn(pb, wb)
                dps_ref[:, PGD * g:PGD * (g + 1)] += jnp.sum(dyp * ypre, axis=0, keepdims=True)
                dyg = (dyp * ps_ref[:, PGD * g:PGD * (g + 1)]).astype(bf16)
                dpw_ref[g] += _tn(pb, dyg)
                dpooled = _nt(dyg, wb)
                dsm = dpooled / cnt
                hi = dsm.astype(bf16)
                lo = (dsm - hi.astype(f32)).astype(bf16)
                due = _tn(band, hi) + _tn(band, lo)
                dz_ref[rows, c0:c0 + PGD] = due[BLK:2 * BLK] - dpooled

                def pool_prev(due=due, c0=c0, prow=prow):
                    dz_ref[prow, c0:c0 + PGD] += due[0:BLK]

                into_prev(pool_prev)

        dsk_ref[...] += dsk

        @pl.when(n == nsteps - 1)
        def _():
            bk = bk_ref[...]
            ri = lax.broadcasted_iota(i32, (NBUCK, NH), 0)
            ci = lax.broadcasted_iota(i32, (NBUCK, NH), 1)

            def step(b, acc):
                for h in range(NH):
                    sel = jnp.where(bk == b, dbias_scr[h], 0.0)
                    tot = jnp.sum(jnp.sum(sel, axis=1, keepdims=True), axis=0, keepdims=True)
                    acc = acc + jnp.where((ri == b) & (ci == h), tot, 0.0)
                return acc

            drb_ref[...] = lax.fori_loop(0, NBUCK, step, jnp.zeros((NBUCK, NH), f32))

    full = lambda *shape: pl.BlockSpec(shape, lambda n: (0,) * len(shape))
    smem = pl.BlockSpec(memory_space=pltpu.SMEM)
    npg = len(POOL_WINDOWS)
    return pl.pallas_call(
        body, grid=(nsteps,),
        in_specs=[pl.BlockSpec((step_rows, DIN), lambda n: (n, 0)),
                  pl.BlockSpec((BLK, DIN), lambda n: (jnp.maximum(n * MIX_SUB - 1, 0), 0)),
                  pl.BlockSpec((step_rows, DMIX), lambda n: (n, 0)),
                  full(1, HD), full(1, HD), smem, smem, full(BLK, 2 * BLK), full(npg, PGD, PGD), full(1, DPOOL)],
        out_specs=[full(T, DIN), full(1, HD), full(1, HD), full(1, 128), full(NBUCK, NH),
                   full(npg, PGD, PGD), full(1, DPOOL)],
        out_shape=(SDS((T, DIN), f32), SDS((1, HD), f32), SDS((1, HD), f32), SDS((1, 128), f32),
                   SDS((NBUCK, NH), f32), SDS((npg, PGD, PGD), f32), SDS((1, DPOOL), f32)),
        scratch_shapes=[pltpu.VMEM((NH, BLK, 2 * BLK), f32), pltpu.VMEM((NH, BLK, 2 * BLK), f32)],
        compiler_params=_cparams(("arbitrary",), VMEM_LIMIT_V7X),
        name=name)(z, z, dy, qg, kg, sinks, relb, bucket, pool_w, pscale)


class _LocalWeights:
    def __init__(self, w1, wint, wout, w2):
        self.w1, self.wint, self.wout, self.w2 = w1, wint, wout, w2

    def ffn1(self):
        return self.w1

    def after_ffn1(self, gain, x1):
        return gain

    def mix(self, after):
        return self.wint, self.wout

    def before_out_proj(self, wout, after):
        return wout

    def ffn2(self, after):
        return self.w2

    def mix_ffn2_grads_ready(self, dwint, dwout, dw2, dh2):
        self.grads_rest = (dwint, dwout, dw2)
        return dh2

    def before_ffn1_bwd(self, dx1b):
        return dx1b


def _local_step(x, target, weights, g1, gm, g3, qg, kg, sinks, relb, pool_w, pscale):
    bucket = jnp.asarray(_t5_bucket_table())
    sk = sinks.reshape(NH)
    w1 = weights.ffn1()
    h1 = _norm_fwd(x, g1, "norm1_fwd")
    x1, gate1, up1 = _ffn_fwd(h1, w1, x, None, "ffn1_fwd")
    h2 = _norm_fwd(x1, weights.after_ffn1(gm, x1), "norm2_fwd")
    wint, wout = weights.mix(h2)
    z = _in_proj_fwd(h2, wint, "in_proj_fwd")
    ymix = _mix_fwd(z, qg, kg, sk, relb, bucket, pool_w, pscale, "mix_fwd")
    wout = weights.before_out_proj(wout, ymix)
    x2, h3 = _out_proj_fwd(ymix, wout, x1, g3, "out_proj_fwd")
    w2 = weights.ffn2(h3)
    dy, gate2, up2, dyb, loss_lanes = _ffn_fwd(h3, w2, x2, target, "ffn2_fwd")

    dh3, dw2 = _ffn_bwd(dyb, h3, gate2, up2, w2, "ffn2_bwd")
    dx2, dx2b, dg3 = _norm_bwd(dh3, x2, g3, dy, 1.0, "norm3_bwd")
    dymix, dwout = _out_proj_bwd(dx2b, wout, ymix, "out_proj_bwd")
    dz, dqg, dkg, dsk, drb, dpw, dps = _mix_bwd(z, dymix, qg, kg, sk, relb, bucket, pool_w, pscale, "mix_bwd")
    dh2, dwint = _in_proj_bwd(dz, wint, h2, "in_proj_bwd")
    dh2 = weights.mix_ffn2_grads_ready(dwint, dwout, dw2, dh2)
    dx1, dx1b, dgm = _norm_bwd(dh2, x1, gm, dx2, 0.5, "norm2_bwd")
    dx1b = weights.before_ffn1_bwd(dx1b)
    dh1, dw1 = _ffn_bwd(dx1b, h1, gate1, up1, w1, "ffn1_bwd")
    gx, _, dg1 = _norm_bwd(dh1, x, g1, dx1, 1.0, "norm1_bwd")
    small = dict(ffn1_norm=dg1, mix_norm=dgm, ffn2_norm=dg3, pool_scale=dps, q_norm=dqg, k_norm=dkg,
                 attn_sinks=dsk[:, :NH], rel_bias=drb, pool_w=dpw, loss=loss_lanes)
    return gx, (dw1, dwint, dwout, dw2), small


SMALL_NAMES = ("ffn1_norm", "mix_norm", "ffn2_norm", "pool_scale", "q_norm", "k_norm", "attn_sinks", "rel_bias",
               "pool_w", "loss")
SMALL_SHAPES = dict(ffn1_norm=(1, D), mix_norm=(1, D), ffn2_norm=(1, D), pool_scale=(1, DPOOL), q_norm=(1, HD),
                    k_norm=(1, HD), attn_sinks=(1, NH), rel_bias=(NBUCK, NH),
                    pool_w=(1, len(POOL_WINDOWS), PGD, PGD), loss=(1, 128))


def _small_rows(name):
    return -(-int(np.prod(SMALL_SHAPES[name])) // 128)


SMALL_OFF = {}
_r = 0
for _n in SMALL_NAMES:
    SMALL_OFF[_n] = _r
    _r += _small_rows(_n)
SMALL_ROWS = -(-_r // 8) * 8
LOSS_ROW = SMALL_OFF["loss"]


def _pack_small(vals):
    parts = []
    for n in SMALL_NAMES:
        size = _small_rows(n) * 128
        if n in vals:
            flat = vals[n].astype(f32).reshape(-1)
            parts.append(jnp.pad(flat, (0, size - flat.shape[0])))
        else:
            parts.append(jnp.zeros((size,), f32))
    flat = jnp.concatenate(parts)
    flat = jnp.pad(flat, (0, SMALL_ROWS * 128 - flat.shape[0]))
    return flat.reshape(SMALL_ROWS, 128)


def _unpack_small(packed, name):
    size = int(np.prod(SMALL_SHAPES[name]))
    r0 = SMALL_OFF[name]
    return packed[r0:r0 + _small_rows(name)].reshape(-1)[:size].reshape(SMALL_SHAPES[name])


def _position():
    return lax.axis_index("x"), lax.axis_index("y"), lax.axis_index("c")


def _dev_index(x, y, c):
    return 4 * x + 2 * y + c


G1_PIECES, MIX_PIECES, F2_PIECES = (0, 1, 2), (3, 4), (5, 6, 7)


def _group_rows(pieces):
    return sum(PIECE_ROWS[k] for k in pieces)


def _shard_piece(s_ref, k):
    return s_ref.at[pl.ds(PIECE_OFF[k], PIECE_ROWS[k]), :]


def _shard_group(s_ref, pieces):
    return s_ref.at[pl.ds(PIECE_OFF[pieces[0]], _group_rows(pieces)), :]


def _weight_pieces(w1_ref=None, wi_ref=None, wo_ref=None, w2_ref=None):
    arrs = {}
    if w1_ref is not None:
        arrs.update({0: w1_ref.at[0], 1: w1_ref.at[1], 2: w1_ref.at[2]})
    if wi_ref is not None:
        arrs[3] = wi_ref
    if wo_ref is not None:
        arrs[4] = wo_ref
    if w2_ref is not None:
        arrs.update({5: w2_ref.at[0], 6: w2_ref.at[1], 7: w2_ref.at[2]})
    return arrs


def _block_rows(arrs, k, dev):
    r = PIECE_ROWS[k]
    return arrs[k].at[pl.ds(pl.multiple_of(_dev_index(*dev) * r, 16), r), :]


def _all_gather_ffn1(shard):
    pieces = G1_PIECES

    def body(s_ref, w1_ref, send_sems, recv_sems, local_sem):
        x, y, c = _position()
        me, sib = (x, y, c), (x, y, 1 - c)
        chips = [(1 - x, y), (x, 1 - y), (1 - x, 1 - y)]
        arrs = _weight_pieces(w1_ref=w1_ref)

        def copies(rel, block, to, from_shard):
            return [pltpu.make_async_remote_copy(
                src_ref=_shard_piece(s_ref, k) if from_shard else _block_rows(arrs, k, block),
                dst_ref=_block_rows(arrs, k, block),
                send_sem=send_sems.at[rel], recv_sem=recv_sems.at[rel], device_id=to, device_id_type=MESH)
                for k in pieces]

        def whole(rel):
            grp = _shard_group(s_ref, pieces)
            return pltpu.make_async_remote_copy(src_ref=grp, dst_ref=grp, send_sem=send_sems.at[rel],
                                                recv_sem=recv_sems.at[rel], device_id=me, device_id_type=MESH)

        mine = [pltpu.make_async_copy(_shard_piece(s_ref, k), _block_rows(arrs, k, me), local_sem) for k in pieces]
        for cp in mine:
            cp.start()
        for cp in copies(0, me, sib, True):
            cp.start()
        for j, chip in enumerate(chips):
            for cp in copies(1 + j, me, (*chip, c), True):
                cp.start()
        for j, chip in enumerate(chips):
            whole(1 + j).wait_recv()
            for cp in copies(4 + j, (*chip, c), sib, False):
                cp.start()
        whole(0).wait_recv()
        for j in range(3):
            whole(4 + j).wait_recv()
        for rel in range(7):
            whole(rel).wait_send()
        grp = _shard_group(s_ref, pieces)
        pltpu.make_async_copy(grp, grp, local_sem).wait()

    hbm = pl.BlockSpec(memory_space=pl.ANY)
    return pl.pallas_call(
        body, in_specs=[hbm], out_specs=hbm, out_shape=SDS((3, F, D), bf16),
        scratch_shapes=[pltpu.SemaphoreType.DMA((7,)), pltpu.SemaphoreType.DMA((7,)), pltpu.SemaphoreType.DMA],
        compiler_params=pltpu.CompilerParams(has_side_effects=True),
        name="all_gather_ffn1")(shard)


HBM_SPEC = pl.BlockSpec(memory_space=pltpu.HBM)
SEM_SPEC = pl.BlockSpec(memory_space=pltpu.SEMAPHORE)
ANY_SPEC = pl.BlockSpec(memory_space=pl.ANY)
SPLIT_EFFECT = pltpu.SideEffectType.DATAFLOW_SIDE_EFFECTING


def _in_hbm(a):
    return pltpu.with_memory_space_constraint(a, pltpu.HBM)


def _hbm_like(a):
    return pltpu.HBM(a.shape, a.dtype)


def _place_own_rows(shard):
    pieces = MIX_PIECES + F2_PIECES

    def body(s_ref, wi_ref, wo_ref, w2_ref, buf, sems):
        x, y, c = _position()
        arrs = _weight_pieces(wi_ref=wi_ref, wo_ref=wo_ref, w2_ref=w2_ref)
        grp = _shard_group(s_ref, pieces)
        load = pltpu.make_async_copy(grp, buf, sems.at[0])
        load.start()
        load.wait()
        base = PIECE_OFF[pieces[0]]
        for k in pieces:
            pltpu.make_async_copy(buf.at[pl.ds(PIECE_OFF[k] - base, PIECE_ROWS[k]), :],
                                  _block_rows(arrs, k, (x, y, c)), sems.at[1]).start()
        pltpu.make_async_copy(grp, buf, sems.at[1]).wait()

    return pl.pallas_call(
        body, in_specs=[ANY_SPEC], out_specs=[ANY_SPEC] * 3,
        out_shape=(SDS((DIN, D), bf16), SDS((DMIX, D), bf16), SDS((3, F, D), bf16)),
        scratch_shapes=[pltpu.VMEM((_group_rows(pieces), D), bf16), pltpu.SemaphoreType.DMA((2,))],
        name="place_own_rows")(shard)


def _xor_peer(x, y, c, k):
    return (x ^ (k >> 2), y ^ ((k >> 1) & 1), c ^ (k & 1))


def _gather_rest_start(shard, wi, wo, w2, w1):
    def body(s_ref, wi_ref, wo_ref, w2_ref, w1_ref,
             ssem_m, rsem_m0, rsem_m, ssem_f, rsem_f0, rsem_f, s_o, wi_o, wo_o, w2_o, w1_o):
        x, y, c = _position()
        me, sib = (x, y, c), (x, y, 1 - c)
        chips = [(1 - x, y), (x, 1 - y), (1 - x, 1 - y)]
        arrs = _weight_pieces(wi_ref=wi_ref, wo_ref=wo_ref, w2_ref=w2_ref)
        for pieces, ssem, rsem0, rsem in ((MIX_PIECES, ssem_m, rsem_m0, rsem_m), (F2_PIECES, ssem_f, rsem_f0, rsem_f)):
            for p in pieces:
                pltpu.make_async_remote_copy(
                    src_ref=_shard_piece(s_ref, p), dst_ref=_block_rows(arrs, p, me), send_sem=ssem.at[0],
                    recv_sem=rsem0, device_id=sib, device_id_type=MESH).start()
            for j, chip in enumerate(chips):
                for p in pieces:
                    pltpu.make_async_remote_copy(
                        src_ref=_shard_piece(s_ref, p), dst_ref=_block_rows(arrs, p, me), send_sem=ssem.at[1 + j],
                        recv_sem=rsem.at[j], device_id=(*chip, c), device_id_type=MESH).start()

    dma = pltpu.SemaphoreType.DMA
    return pl.pallas_call(
        body, name="gather_rest_start",
        out_shape=(dma((4,)), dma(()), dma((3,)), dma((4,)), dma(()), dma((3,)),
                   _hbm_like(shard), _hbm_like(wi), _hbm_like(wo), _hbm_like(w2), _hbm_like(w1)),
        in_specs=(HBM_SPEC,) * 5, out_specs=(SEM_SPEC,) * 6 + (HBM_SPEC,) * 5,
        input_output_aliases={0: 6, 1: 7, 2: 8, 3: 9, 4: 10},
        compiler_params=pltpu.CompilerParams(has_side_effects=SPLIT_EFFECT),
    )(_in_hbm(shard), _in_hbm(wi), _in_hbm(wo), _in_hbm(w2), _in_hbm(w1))


def _gather_mix_pass_on(rsem_m, wi, wo, thru, after):
    def body(wi_ref, wo_ref, thru_ref, rsem, after_ref, fsend, frecv, wi_o, wo_o, thru_o):
        x, y, c = _position()
        sib = (x, y, 1 - c)
        arrs = _weight_pieces(wi_ref=wi_ref, wo_ref=wo_ref)
        both = wi_ref.at[pl.ds(0, _group_rows(MIX_PIECES)), :]
        for j, chip in enumerate([(1 - x, y), (x, 1 - y), (1 - x, 1 - y)]):
            pltpu.make_async_remote_copy(src_ref=both, dst_ref=both, send_sem=fsend.at[j], recv_sem=rsem.at[j],
                                         device_id=(x, y, c), device_id_type=MESH).wait_recv()
            for p in MIX_PIECES:
                rows = _block_rows(arrs, p, (*chip, c))
                pltpu.make_async_remote_copy(src_ref=rows, dst_ref=rows, send_sem=fsend.at[j], recv_sem=frecv.at[j],
                                             device_id=sib, device_id_type=MESH).start()

    dma = pltpu.SemaphoreType.DMA
    return pl.pallas_call(
        body, name="gather_mix_pass_on",
        out_shape=(dma((3,)), dma((3,)), _hbm_like(wi), _hbm_like(wo), _hbm_like(thru)),
        in_specs=(HBM_SPEC, HBM_SPEC, HBM_SPEC, SEM_SPEC, ANY_SPEC), out_specs=(SEM_SPEC, SEM_SPEC) + (HBM_SPEC,) * 3,
        input_output_aliases={0: 2, 1: 3, 2: 4},
        compiler_params=pltpu.CompilerParams(has_side_effects=SPLIT_EFFECT),
    )(wi, wo, _in_hbm(thru), rsem_m, after)


def _gather_mix_wait(ssem_m, rsem_m0, fsend, frecv, shard, wi, wo, after):
    def body(s_ref, wi_ref, wo_ref, ssem, rsem0, fs, fr, after_ref, s_o, wi_o, wo_o):
        x, y, c = _position()
        grp = _shard_group(s_ref, MIX_PIECES)

        def waiter(send_sem, recv_sem):
            return pltpu.make_async_remote_copy(src_ref=grp, dst_ref=grp, send_sem=send_sem, recv_sem=recv_sem,
                                                device_id=(x, y, c), device_id_type=MESH)

        waiter(ssem.at[0], rsem0).wait_recv()
        for j in range(3):
            waiter(fs.at[j], fr.at[j]).wait_recv()
        for rel in range(4):
            waiter(ssem.at[rel], rsem0).wait_send()
        for j in range(3):
            waiter(fs.at[j], fr.at[j]).wait_send()

    return pl.pallas_call(
        body, name="gather_mix_wait", out_shape=(_hbm_like(shard), _hbm_like(wi), _hbm_like(wo)),
        in_specs=(HBM_SPEC,) * 3 + (SEM_SPEC,) * 4 + (ANY_SPEC,), out_specs=(HBM_SPEC,) * 3,
        input_output_aliases={0: 0, 1: 1, 2: 2},
        compiler_params=pltpu.CompilerParams(has_side_effects=SPLIT_EFFECT),
    )(shard, wi, wo, ssem_m, rsem_m0, fsend, frecv, after)


def _gather_ffn2_pass_on(rsem_f, w2, wo, after):
    def body(w2_ref, wo_ref, rsem, after_ref, fsend, frecv, w2_o, wo_o):
        x, y, c = _position()
        sib = (x, y, 1 - c)
        chips = [(1 - x, y), (x, 1 - y), (1 - x, 1 - y)]
        arrs = _weight_pieces(w2_ref=w2_ref)
        three = w2_ref.at[0, pl.ds(0, _group_rows(F2_PIECES)), :]
        for j, chip in enumerate(chips):
            pltpu.make_async_remote_copy(src_ref=three, dst_ref=three, send_sem=fsend.at[j], recv_sem=rsem.at[j],
                                         device_id=(x, y, c), device_id_type=MESH).wait_recv()
            for p in F2_PIECES:
                rows = _block_rows(arrs, p, (*chip, c))
                pltpu.make_async_remote_copy(src_ref=rows, dst_ref=rows, send_sem=fsend.at[j], recv_sem=frecv.at[j],
                                             device_id=sib, device_id_type=MESH).start()

    dma = pltpu.SemaphoreType.DMA
    return pl.pallas_call(
        body, name="gather_ffn2_pass_on", out_shape=(dma((3,)), dma((3,)), _hbm_like(w2), _hbm_like(wo)),
        in_specs=(HBM_SPEC, HBM_SPEC, SEM_SPEC, ANY_SPEC), out_specs=(SEM_SPEC, SEM_SPEC, HBM_SPEC, HBM_SPEC),
        input_output_aliases={0: 2, 1: 3},
        compiler_params=pltpu.CompilerParams(has_side_effects=SPLIT_EFFECT),
    )(w2, wo, rsem_f, after)


def _gather_ffn2_wait(ssem_f, rsem_f0, fsend, frecv, shard, w2, after):
    def body(s_ref, w2_ref, ssem, rsem0, fs, fr, after_ref, w2_o):
        x, y, c = _position()
        grp = _shard_group(s_ref, F2_PIECES)

        def waiter(send_sem, recv_sem):
            return pltpu.make_async_remote_copy(src_ref=grp, dst_ref=grp, send_sem=send_sem, recv_sem=recv_sem,
                                                device_id=(x, y, c), device_id_type=MESH)

        waiter(ssem.at[0], rsem0).wait_recv()
        for j in range(3):
            waiter(fs.at[j], fr.at[j]).wait_recv()
        for rel in range(4):
            waiter(ssem.at[rel], rsem0).wait_send()
        for j in range(3):
            waiter(fs.at[j], fr.at[j]).wait_send()

    return pl.pallas_call(
        body, name="gather_ffn2_wait", out_shape=_hbm_like(w2),
        in_specs=(HBM_SPEC, HBM_SPEC, SEM_SPEC, SEM_SPEC, SEM_SPEC, SEM_SPEC, ANY_SPEC), out_specs=HBM_SPEC,
        input_output_aliases={1: 0},
        compiler_params=pltpu.CompilerParams(has_side_effects=SPLIT_EFFECT),
    )(shard, w2, ssem_f, rsem_f0, fsend, frecv, after)


class _GatheredWeights(_LocalWeights):
    def __init__(self, shard):
        w1 = _all_gather_ffn1(shard)
        wi, wo, w2 = _place_own_rows(shard)
        (self.ssem_m, self.rsem_m0, self.rsem_m, self.ssem_f, self.rsem_f0, self.rsem_f,
         self.shard, self.wi, self.wo, self.w2_part, self.w1) = _gather_rest_start(shard, wi, wo, w2, w1)

    def after_ffn1(self, gain, x1):
        self.fsend_m, self.frecv_m, self.wi, self.wo, gain = _gather_mix_pass_on(self.rsem_m, self.wi, self.wo, gain, x1)
        return gain

    def mix(self, after):
        self.shard, wint, wout = _gather_mix_wait(self.ssem_m, self.rsem_m0, self.fsend_m, self.frecv_m, self.shard,
                                                  self.wi, self.wo, after)
        return wint, wout

    def before_out_proj(self, wout, after):
        self.fsend, self.frecv, self.w2_part, wout = _gather_ffn2_pass_on(self.rsem_f, self.w2_part, wout, after)
        return wout

    def ffn2(self, after):
        return _gather_ffn2_wait(self.ssem_f, self.rsem_f0, self.fsend, self.frecv, self.shard, self.w2_part, after)

    def mix_ffn2_grads_ready(self, dwint, dwout, dw2, dh2):
        rx1 = lax.empty((4, RSA_ROWS, D), bf16)
        self.sa, self.ra, dwint, dwout, dw2, rx1, dh2 = _rsa_level1_start(dwint, dwout, dw2, rx1, dh2)
        self.level1 = (dwint, dwout, dw2, rx1)
        return dh2

    def before_ffn1_bwd(self, dx1b):
        dwint, dwout, dw2, rx1 = _rsa_level1_wait(self.sa, self.ra, *self.level1, dx1b)
        tx, self.acc = _rsa_chip_sums(dwint, dwout, dw2, rx1)
        rx2 = lax.empty((3, RSA_ROWS, D), bf16)
        self.sb, self.rb, self.tx, self.rx2, dx1b = _rsa_level2_start(tx, rx2, dx1b)
        return dx1b

    def mix_ffn2_grads_total(self, after):
        rx2 = _rsa_level2_wait(self.sb, self.rb, self.tx, self.rx2, after)
        return _rsa_total(self.acc, rx2)


RS_CHUNK = 176


def _reduce_scatter_ffn1(dw1, small_packed):
    pieces = G1_PIECES
    nrows = _group_rows(pieces)
    nchunk = nrows // RS_CHUNK

    def body(d1_ref, p_ref, red_ref, rx1_ref, rx2_ref, tot_ref,
             own_buf, rx_buf, tx_buf, acc, sa, ra, sb, rb, lsem, pair, chips, small_send, small_recv):
        x, y, c = _position()
        me, sib = (x, y, c), (x, y, 1 - c)
        rel_chips = [(x, y), (1 - x, y), (x, 1 - y), (1 - x, 1 - y)]
        srcs = _weight_pieces(w1_ref=d1_ref)

        my_chip = 2 * x + y
        pair[c] = p_ref[...]
        swap = pltpu.make_async_remote_copy(
            src_ref=p_ref, dst_ref=pair.at[c], send_sem=small_send.at[0], recv_sem=small_recv.at[0],
            device_id=sib, device_id_type=MESH)
        swap.start()
        small = [pltpu.make_async_remote_copy(
            src_ref=chips.at[my_chip], dst_ref=chips.at[my_chip], send_sem=small_send.at[j], recv_sem=small_recv.at[j],
            device_id=(*rel_chips[j], c), device_id_type=MESH) for j in (1, 2, 3)]

        def piece(k, dev):
            r = PIECE_ROWS[k]
            return srcs[k].at[pl.ds(pl.multiple_of(_dev_index(*dev) * r, 16), r), :]

        def packed(ref, k):
            return ref.at[pl.ds(PIECE_OFF[k], PIECE_ROWS[k]), :]

        for j, chip in enumerate(rel_chips):
            for k in pieces:
                pltpu.make_async_remote_copy(
                    src_ref=piece(k, (*chip, 1 - c)), dst_ref=packed(rx1_ref.at[j], k),
                    send_sem=sa.at[j], recv_sem=ra.at[j], device_id=sib, device_id_type=MESH).start()

        def wait_a(j):
            return pltpu.make_async_remote_copy(src_ref=rx1_ref.at[j], dst_ref=rx1_ref.at[j], send_sem=sa.at[j],
                                                recv_sem=ra.at[j], device_id=me, device_id_type=MESH)

        def ici(j):
            return pltpu.make_async_remote_copy(
                src_ref=tx_buf.at[j - 1], dst_ref=rx2_ref.at[j - 1], send_sem=sb.at[j - 1], recv_sem=rb.at[j - 1],
                device_id=(*rel_chips[j], c), device_id_type=MESH)

        swap.wait_recv()
        chips[my_chip] = pair[0] + pair[1]
        for cp in small:
            cp.start()

        for j in (1, 2, 3, 0):
            loads = [pltpu.make_async_copy(piece(k, (*rel_chips[j], c)), packed(own_buf, k), lsem)
                     for k in pieces]
            for cp in loads:
                cp.start()
            wait_a(j).wait_recv()
            got = pltpu.make_async_copy(rx1_ref.at[j], rx_buf, lsem)
            got.start()
            pltpu.make_async_copy(rx_buf, rx_buf, lsem).wait()
            got.wait()

            def add(i, carry, j=j):
                rows = pl.ds(pl.multiple_of(i * RS_CHUNK, 16), RS_CHUNK)
                tot = own_buf[rows, :].astype(f32) + rx_buf[rows, :].astype(f32)
                if j == 0:
                    acc[rows, :] = tot
                else:
                    tx_buf[j - 1, rows, :] = tot.astype(bf16)
                return carry

            lax.fori_loop(0, nchunk, add, 0)
            if j != 0:
                ici(j).start()

        for j in (1, 2, 3):
            ici(j).wait_recv()
            got = pltpu.make_async_copy(rx2_ref.at[j - 1], rx_buf, lsem)
            got.start()
            got.wait()

            def add2(i, carry):
                rows = pl.ds(pl.multiple_of(i * RS_CHUNK, 16), RS_CHUNK)
                acc[rows, :] += rx_buf[rows, :].astype(f32)
                return carry

            lax.fori_loop(0, nchunk, add2, 0)
        out = pltpu.make_async_copy(acc, red_ref, lsem)
        out.start()
        out.wait()
        for cp in small:
            cp.wait_recv()
        tot = (chips[0] + chips[1]) + (chips[2] + chips[3])
        tot_ref[...] = tot
        loss = jnp.sum(tot[LOSS_ROW:LOSS_ROW + 1, :], axis=-1, keepdims=True)
        tot_ref[LOSS_ROW:LOSS_ROW + 1, :] = jnp.broadcast_to(loss, (1, 128))
        for j in range(4):
            wait_a(j).wait_send()
        for j in (1, 2, 3):
            ici(j).wait_send()
        swap.wait_send()
        for cp in small:
            cp.wait_send()

    hbm = pl.BlockSpec(memory_space=pl.ANY)
    vm = pl.BlockSpec(memory_space=pltpu.VMEM)
    red, _, _, small_tot = pl.pallas_call(
        body, in_specs=[hbm, vm], out_specs=[hbm] * 3 + [vm],
        out_shape=(SDS((nrows, D), f32), SDS((4, nrows, D), bf16), SDS((3, nrows, D), bf16),
                   SDS((SMALL_ROWS, 128), f32)),
        scratch_shapes=[pltpu.VMEM((nrows, D), bf16), pltpu.VMEM((nrows, D), bf16),
                        pltpu.VMEM((3, nrows, D), bf16), pltpu.VMEM((nrows, D), f32),
                        pltpu.SemaphoreType.DMA((4,)), pltpu.SemaphoreType.DMA((4,)),
                        pltpu.SemaphoreType.DMA((3,)), pltpu.SemaphoreType.DMA((3,)), pltpu.SemaphoreType.DMA,
                        pltpu.VMEM((2, SMALL_ROWS, 128), f32), pltpu.VMEM((4, SMALL_ROWS, 128), f32),
                        pltpu.SemaphoreType.DMA((4,)), pltpu.SemaphoreType.DMA((4,))],
        compiler_params=pltpu.CompilerParams(has_side_effects=True, vmem_limit_bytes=VMEM_LIMIT_V7X),
        name="reduce_scatter_ffn1")(dw1, small_packed)
    return red, small_tot


RSA_PIECES = MIX_PIECES + F2_PIECES
RSA_ROWS = _group_rows(RSA_PIECES)
RSA_OFF = {k: PIECE_OFF[k] - PIECE_OFF[RSA_PIECES[0]] for k in RSA_PIECES}
RSA_BLOCK = 192


def _rsa_rows(ref, k):
    return ref.at[pl.ds(RSA_OFF[k], PIECE_ROWS[k]), :]


def _rsa_level1_start(dwint, dwout, dw2, rx1, thru):
    def body(di_ref, do_ref, d2_ref, rx1_ref, thru_ref, sa, ra, di_o, do_o, d2_o, rx1_o, thru_o):
        x, y, c = _position()
        srcs = _weight_pieces(wi_ref=di_ref, wo_ref=do_ref, w2_ref=d2_ref)
        for j, chip in enumerate([(x, y), (1 - x, y), (x, 1 - y), (1 - x, 1 - y)]):
            for k in RSA_PIECES:
                pltpu.make_async_remote_copy(
                    src_ref=_block_rows(srcs, k, (*chip, 1 - c)), dst_ref=_rsa_rows(rx1_ref.at[j], k),
                    send_sem=sa.at[j], recv_sem=ra.at[j], device_id=(x, y, 1 - c), device_id_type=MESH).start()

    dma = pltpu.SemaphoreType.DMA
    arrs = (dwint, dwout, dw2, rx1, thru)
    return pl.pallas_call(
        body, name="rsa_level1_start", out_shape=(dma((4,)), dma((4,))) + tuple(_hbm_like(a) for a in arrs),
        in_specs=(HBM_SPEC,) * 5, out_specs=(SEM_SPEC,) * 2 + (HBM_SPEC,) * 5,
        input_output_aliases={0: 2, 1: 3, 2: 4, 3: 5, 4: 6},
        compiler_params=pltpu.CompilerParams(has_side_effects=SPLIT_EFFECT),
    )(*[_in_hbm(a) for a in arrs])


def _rsa_level1_wait(sa, ra, dwint, dwout, dw2, rx1, after):
    def body(di_ref, do_ref, d2_ref, rx1_ref, sa_ref, ra_ref, after_ref, di_o, do_o, d2_o, rx1_o):
        x, y, c = _position()
        for j in range(4):
            d = pltpu.make_async_remote_copy(src_ref=rx1_ref.at[j], dst_ref=rx1_ref.at[j], send_sem=sa_ref.at[j],
                                             recv_sem=ra_ref.at[j], device_id=(x, y, c), device_id_type=MESH)
            d.wait_recv()
            d.wait_send()

    arrs = (dwint, dwout, dw2, rx1)
    return pl.pallas_call(
        body, name="rsa_level1_wait", out_shape=tuple(_hbm_like(a) for a in arrs),
        in_specs=(HBM_SPEC,) * 4 + (SEM_SPEC, SEM_SPEC, ANY_SPEC), out_specs=(HBM_SPEC,) * 4,
        input_output_aliases={0: 0, 1: 1, 2: 2, 3: 3},
        compiler_params=pltpu.CompilerParams(has_side_effects=SPLIT_EFFECT),
    )(*arrs, sa, ra, after)


def _rsa_chip_sums(dwint, dwout, dw2, rx1):
    nblk = RSA_ROWS // RSA_BLOCK

    def body(di_ref, do_ref, d2_ref, rx1_ref, tx_ref, acc_ref, own_buf, rx_buf, tx_buf, acc_buf, lsems):
        x, y, c = _position()
        srcs = _weight_pieces(wi_ref=di_ref, wo_ref=do_ref, w2_ref=d2_ref)
        for j, chip in enumerate([(x, y), (1 - x, y), (x, 1 - y), (1 - x, 1 - y)]):
            loads = [pltpu.make_async_copy(_block_rows(srcs, k, (*chip, c)), _rsa_rows(own_buf, k), lsems.at[0])
                     for k in RSA_PIECES]
            got = pltpu.make_async_copy(rx1_ref.at[j], rx_buf, lsems.at[1])
            for cp in loads + [got]:
                cp.start()
            pltpu.make_async_copy(rx_buf, rx_buf, lsems.at[0]).wait()
            got.wait()

            def add(i, carry, j=j):
                rows = pl.ds(pl.multiple_of(i * RSA_BLOCK, 16), RSA_BLOCK)
                tot = own_buf[rows, :].astype(f32) + rx_buf[rows, :].astype(f32)
                if j == 0:
                    acc_buf[rows, :] = tot
                else:
                    tx_buf[rows, :] = tot.astype(bf16)
                return carry

            lax.fori_loop(0, nblk, add, 0)
            out = (pltpu.make_async_copy(acc_buf, acc_ref, lsems.at[2]) if j == 0
                   else pltpu.make_async_copy(tx_buf, tx_ref.at[j - 1], lsems.at[2]))
            out.start()
            out.wait()

    return pl.pallas_call(
        body, in_specs=[ANY_SPEC] * 4, out_specs=[ANY_SPEC] * 2,
        out_shape=(SDS((3, RSA_ROWS, D), bf16), SDS((RSA_ROWS, D), f32)),
        scratch_shapes=[pltpu.VMEM((RSA_ROWS, D), bf16), pltpu.VMEM((RSA_ROWS, D), bf16),
                        pltpu.VMEM((RSA_ROWS, D), bf16), pltpu.VMEM((RSA_ROWS, D), f32),
                        pltpu.SemaphoreType.DMA((3,))],
        compiler_params=_cparams(None, VMEM_LIMIT_V7X), name="rsa_chip_sums")(dwint, dwout, dw2, rx1)


def _rsa_level2_start(tx, rx2, thru):
    def body(tx_ref, rx2_ref, thru_ref, sb, rb, tx_o, rx2_o, thru_o):
        x, y, c = _position()
        for j, chip in enumerate([(1 - x, y), (x, 1 - y), (1 - x, 1 - y)]):
            pltpu.make_async_remote_copy(src_ref=tx_ref.at[j], dst_ref=rx2_ref.at[j], send_sem=sb.at[j],
                                         recv_sem=rb.at[j], device_id=(*chip, c), device_id_type=MESH).start()

    dma = pltpu.SemaphoreType.DMA
    arrs = (tx, rx2, thru)
    return pl.pallas_call(
        body, name="rsa_level2_start", out_shape=(dma((3,)), dma((3,))) + tuple(_hbm_like(a) for a in arrs),
        in_specs=(HBM_SPEC,) * 3, out_specs=(SEM_SPEC,) * 2 + (HBM_SPEC,) * 3,
        input_output_aliases={0: 2, 1: 3, 2: 4},
        compiler_params=pltpu.CompilerParams(has_side_effects=SPLIT_EFFECT),
    )(*[_in_hbm(a) for a in arrs])


def _rsa_level2_wait(sb, rb, tx, rx2, after):
    def body(tx_ref, rx2_ref, sb_ref, rb_ref, after_ref, rx2_o):
        x, y, c = _position()
        for j in range(3):
            d = pltpu.make_async_remote_copy(src_ref=tx_ref.at[j], dst_ref=rx2_ref.at[j], send_sem=sb_ref.at[j],
                                             recv_sem=rb_ref.at[j], device_id=(x, y, c), device_id_type=MESH)
            d.wait_recv()
            d.wait_send()

    return pl.pallas_call(
        body, name="rsa_level2_wait", out_shape=_hbm_like(rx2),
        in_specs=(HBM_SPEC, HBM_SPEC, SEM_SPEC, SEM_SPEC, ANY_SPEC), out_specs=HBM_SPEC,
        input_output_aliases={1: 0},
        compiler_params=pltpu.CompilerParams(has_side_effects=SPLIT_EFFECT),
    )(tx, rx2, sb, rb, after)


def _rsa_total(acc, rx2):
    def body(a_ref, r_ref, o_ref):
        o_ref[...] = ((a_ref[...] + r_ref[0].astype(f32)) + r_ref[1].astype(f32)) + r_ref[2].astype(f32)

    return pl.pallas_call(
        body, grid=(RSA_ROWS // RSA_BLOCK,),
        in_specs=[pl.BlockSpec((RSA_BLOCK, D), lambda i: (i, 0)), pl.BlockSpec((3, RSA_BLOCK, D), lambda i: (0, i, 0))],
        out_specs=pl.BlockSpec((RSA_BLOCK, D), lambda i: (i, 0)),
        out_shape=SDS((RSA_ROWS, D), f32), name="rsa_total")(acc, rx2)


def _adamw_math(w, g, m, v):
    m = ADAM_B1 * m + (1.0 - ADAM_B1) * g
    v = ADAM_B2 * v + (1.0 - ADAM_B2) * (g * g)
    m_hat = m / (1.0 - ADAM_B1 ** ADAM_STEP)
    v_hat = v / (1.0 - ADAM_B2 ** ADAM_STEP)
    delta = -ADAM_LR * (m_hat / (jnp.sqrt(v_hat) + ADAM_EPS) + ADAM_WD * w)
    return delta, m, v


def _adamw_big(ws, ms, vs, red1, red_rest):
    npiece = len(BIG)
    rmax = max(PIECE_ROWS)

    def body(*refs):
        ins = (refs[0:npiece], refs[npiece:2 * npiece], refs[2 * npiece:3 * npiece])
        red1_ref, rest_ref = refs[3 * npiece:3 * npiece + 2]
        out_refs = refs[3 * npiece + 2:7 * npiece + 2]
        inb, outb, in_sems, out_sems = refs[7 * npiece + 2:]

        def grad_rows(k):
            if k in G1_PIECES:
                return red1_ref.at[pl.ds(PIECE_OFF[k], PIECE_ROWS[k]), :]
            return _rsa_rows(rest_ref, k)

        def loads(k):
            s, r = k % 2, PIECE_ROWS[k]
            cps = [pltpu.make_async_copy(ins[q][k].at[0], inb.at[s, q, pl.ds(0, r), :], in_sems.at[4 * s + q])
                   for q in range(3)]
            cps.append(pltpu.make_async_copy(grad_rows(k), inb.at[s, 3, pl.ds(0, r), :], in_sems.at[4 * s + 3]))
            return cps

        def stores(k):
            s, r = k % 2, PIECE_ROWS[k]
            return [pltpu.make_async_copy(outb.at[s, q, pl.ds(0, r), :], out_refs[q * npiece + k].at[0],
                                          out_sems.at[4 * s + q]) for q in range(4)]

        for cp in loads(0):
            cp.start()
        for k in range(npiece):
            s, r = k % 2, PIECE_ROWS[k]
            if k + 1 < npiece:
                for cp in loads(k + 1):
                    cp.start()
            for cp in loads(k):
                cp.wait()
            if k >= 2:
                for cp in stores(k - 2):
                    cp.wait()
            g = inb[s, 3, 0:r, :]
            d, nm, nv = _adamw_math(inb[s, 0, 0:r, :], g, inb[s, 1, 0:r, :], inb[s, 2, 0:r, :])
            outb[s, 0, 0:r, :] = g
            outb[s, 1, 0:r, :] = d
            outb[s, 2, 0:r, :] = nm
            outb[s, 3, 0:r, :] = nv
            for cp in stores(k):
                cp.start()
        for k in (npiece - 2, npiece - 1):
            for cp in stores(k):
                cp.wait()

    hbm = pl.BlockSpec(memory_space=pl.ANY)
    outs = pl.pallas_call(
        body, in_specs=[hbm] * (3 * npiece + 2), out_specs=[hbm] * (4 * npiece),
        out_shape=tuple(SDS(w.shape, f32) for _ in range(4) for w in ws),
        scratch_shapes=[pltpu.VMEM((2, 4, rmax, D), f32), pltpu.VMEM((2, 4, rmax, D), f32),
                        pltpu.SemaphoreType.DMA((8,)), pltpu.SemaphoreType.DMA((8,))],
        compiler_params=_cparams(None, VMEM_LIMIT_V7X), name="adamw_big")(*ws, *ms, *vs, red1, red_rest)
    return [list(outs[q * npiece:(q + 1) * npiece]) for q in range(4)]


def _adamw_small(w, m, v, g, name):
    def body(w_ref, m_ref, v_ref, g_ref, d_ref, nm_ref, nv_ref):
        d, nm, nv = _adamw_math(w_ref[...], g_ref[...], m_ref[...], v_ref[...])
        d_ref[...] = d
        nm_ref[...] = nm
        nv_ref[...] = nv

    return pl.pallas_call(
        body, out_shape=tuple(SDS(w.shape, f32) for _ in range(3)), name=name)(w, m, v, g)


WEIGHTS = ("ffn1_norm", "ffn1_w_gate", "ffn1_w_up", "ffn1_w_down", "mix_norm", "w_in", "q_norm", "k_norm",
           "attn_sinks", "rel_bias", "pool_w", "pool_scale", "w_out", "ffn2_norm", "ffn2_w_gate", "ffn2_w_up",
           "ffn2_w_down")
BIG = (("ffn1_w_gate", True), ("ffn1_w_up", True), ("ffn1_w_down", False), ("w_in", True), ("w_out", False),
       ("ffn2_w_gate", True), ("ffn2_w_up", True), ("ffn2_w_down", False))


def kernel(x, ffn1_norm, ffn1_w_gate, ffn1_w_up, ffn1_w_down, mix_norm, w_in, q_norm, k_norm, attn_sinks, rel_bias, pool_w, pool_scale, w_out, ffn2_norm, ffn2_w_gate, ffn2_w_up, ffn2_w_down, loss_target, m_ffn1_norm, m_ffn1_w_gate, m_ffn1_w_up, m_ffn1_w_down, m_mix_norm, m_w_in, m_q_norm, m_k_norm, m_attn_sinks, m_rel_bias, m_pool_w, m_pool_scale, m_w_out, m_ffn2_norm, m_ffn2_w_gate, m_ffn2_w_up, m_ffn2_w_down, v_ffn1_norm, v_ffn1_w_gate, v_ffn1_w_up, v_ffn1_w_down, v_mix_norm, v_w_in, v_q_norm, v_k_norm, v_attn_sinks, v_rel_bias, v_pool_w, v_pool_scale, v_w_out, v_ffn2_norm, v_ffn2_w_gate, v_ffn2_w_up, v_ffn2_w_down):
    args = dict(locals())
    w = {n: args[n] for n in WEIGHTS}
    m = {n: args["m_" + n] for n in WEIGHTS}
    v = {n: args["v_" + n] for n in WEIGHTS}

    as_rows = lambda a, tr: jnp.swapaxes(a, 1, 2) if tr else a
    shard = jnp.concatenate([as_rows(w[n], tr)[0].astype(bf16) for n, tr in BIG], axis=0)
    exchanges = _GatheredWeights(shard)
    gx, (dw1, _, _, _), small = _local_step(
        x[0], loss_target[0], exchanges, ffn1_norm, mix_norm, ffn2_norm, q_norm, k_norm, attn_sinks,
        rel_bias, pool_w[0], pool_scale)

    red1, small_tot = _reduce_scatter_ffn1(dw1, _pack_small(small))
    red_rest = exchanges.mix_ffn2_grads_total(red1)

    grads, deltas, new_m, new_v = {}, {}, {}, {}
    big_out = _adamw_big(*[[as_rows(t[n], tr) for n, tr in BIG] for t in (w, m, v)], red1, red_rest)
    for k, (n, tr) in enumerate(BIG):
        grads[n], deltas[n], new_m[n], new_v[n] = [as_rows(o[k], tr) for o in big_out]
    small_names = [n for n in SMALL_NAMES if n != "loss"]
    ds, nms, nvs = _adamw_small(_pack_small({n: w[n] for n in small_names}), _pack_small({n: m[n] for n in small_names}),
                                _pack_small({n: v[n] for n in small_names}), small_tot, "adamw_small")
    for n in small_names:
        grads[n] = _unpack_small(small_tot, n)
        deltas[n], new_m[n], new_v[n] = _unpack_small(ds, n), _unpack_small(nms, n), _unpack_small(nvs, n)
    loss = small_tot[LOSS_ROW, 0]
    return (loss, gx[None], *[grads[n] for n in WEIGHTS], *[deltas[n] for n in WEIGHTS],
            *[new_m[n] for n in WEIGHTS], *[new_v[n] for n in WEIGHTS])
```

```python
import functools

import jax
import jax.numpy as jnp
import numpy as np
from jax import lax
from jax.experimental import pallas as pl
from jax.experimental.pallas import tpu as pltpu

f32, bf16, i32 = jnp.float32, jnp.bfloat16, jnp.int32
SDS = jax.ShapeDtypeStruct

D = 1024
F = 2816
HD = 64
NH = 8
NKV = 2
GQA = NH // NKV
DATTN = NH * HD
DKV = NKV * HD
DPOOL = 512
POOL_WINDOWS = (2, 4, 8, 16)
PGD = DPOOL // len(POOL_WINDOWS)
DIN = DATTN + 2 * DKV + DPOOL
DMIX = DATTN + DPOOL
BLK = 128
NBUCK = 32
MAX_DISTANCE = 128
EPS = 1e-6
NEG = -1e30
SCALE = HD ** -0.5

ADAM_LR, ADAM_B1, ADAM_B2, ADAM_EPS, ADAM_WD, ADAM_STEP = 0.001, 0.9, 0.999, 1e-08, 0.01, 10

NDEV = 8
FS = F // NDEV
INS = DIN // NDEV
OUTS = DMIX // NDEV
PIECE_ROWS = (FS, FS, FS, INS, OUTS, FS, FS, FS)
PIECE_OFF = tuple(int(v) for v in np.cumsum((0,) + PIECE_ROWS[:-1]))
PACK_ROWS = sum(PIECE_ROWS)

VMEM_LIMIT_V7X = 56 * 1024 * 1024

MESH = pl.DeviceIdType.MESH


def _cparams(sem=None, vmem=None):
    return pltpu.CompilerParams(dimension_semantics=sem, vmem_limit_bytes=vmem)


def _nt(a, b):
    return lax.dot_general(a, b, (((1,), (1,)), ((), ())), preferred_element_type=f32)


def _tn(a, b):
    return lax.dot_general(a, b, (((0,), (0,)), ((), ())), preferred_element_type=f32)


def _nn(a, b):
    return jnp.dot(a, b, preferred_element_type=f32)


def _sigmoid(x):
    return 1.0 / (1.0 + jnp.exp(-x))


def _norm_fwd(x, g, name):
    T = x.shape[0]
    tm = min(512, T)

    def body(x_ref, g_ref, h_ref):
        xv = x_ref[...]
        r = lax.rsqrt(jnp.mean(xv * xv, axis=-1, keepdims=True) + EPS)
        h_ref[...] = (xv * r * g_ref[...]).astype(bf16)

    return pl.pallas_call(
        body, grid=(T // tm,),
        in_specs=[pl.BlockSpec((tm, D), lambda i: (i, 0)), pl.BlockSpec((1, D), lambda i: (0, 0))],
        out_specs=pl.BlockSpec((tm, D), lambda i: (i, 0)),
        out_shape=SDS((T, D), bf16), name=name)(x, g)


def _norm_bwd(dh, x, g, dres, out_scale, name):
    T = x.shape[0]
    tm = min(512, T)

    def body(dh_ref, x_ref, g_ref, dr_ref, dx_ref, dxb_ref, dg_ref):
        i = pl.program_id(0)
        xv = x_ref[...]
        r = lax.rsqrt(jnp.mean(xv * xv, axis=-1, keepdims=True) + EPS)
        xh = xv * r
        dhv = dh_ref[...]
        dxh = dhv * g_ref[...]
        dx = dr_ref[...] + r * (dxh - xh * jnp.mean(dxh * xh, axis=-1, keepdims=True))
        dx_ref[...] = dx
        dxb_ref[...] = (out_scale * dx).astype(bf16)
        dg = jnp.sum(dhv * xh, axis=0, keepdims=True)

        @pl.when(i == 0)
        def _():
            dg_ref[...] = dg

        @pl.when(i > 0)
        def _():
            dg_ref[...] += dg

    tok = pl.BlockSpec((tm, D), lambda i: (i, 0))
    vec = pl.BlockSpec((1, D), lambda i: (0, 0))
    return pl.pallas_call(
        body, grid=(T // tm,),
        in_specs=[tok, tok, vec, tok], out_specs=[tok, tok, vec],
        out_shape=(SDS((T, D), f32), SDS((T, D), bf16), SDS((1, D), f32)),
        compiler_params=_cparams(("arbitrary",)), name=name)(dh, x, g, dres)


FFN_ROW_CHUNK = 256


def _ffn_tiles(T):
    return min(1024, T), 256


def _ffn_fwd(h, w, x, target, name):
    T = h.shape[0]
    tm, tf = _ffn_tiles(T)
    nf = F // tf
    with_loss = target is not None

    def body(*refs):
        if with_loss:
            h_ref, w_ref, x_hbm, t_hbm, xo_ref, g_ref, u_ref, dyb_ref, loss_ref, tbuf, sem = refs
        else:
            h_ref, w_ref, x_hbm, xo_ref, g_ref, u_ref, sem = refs
        fi = pl.program_id(0)

        @pl.when(fi == 0)
        def _():
            cp = pltpu.make_async_copy(x_hbm, xo_ref, sem)
            cp.start()
            cp.wait()

        wgu = w_ref[0:2].reshape(2 * tf, D)
        for r in range(0, T, tm):
            rows = slice(r, r + tm)
            gu = _nt(h_ref[rows, :], wgu)
            gate, up = gu[:, :tf], gu[:, tf:]
            act = gate * _sigmoid(gate) * up
            g_ref[0, rows, :] = gate.astype(bf16)
            u_ref[0, rows, :] = up.astype(bf16)
            xo_ref[rows, :] += _nn((0.5 * act).astype(bf16), w_ref[2])

        if with_loss:
            @pl.when(fi == nf - 1)
            def _():
                lanes = jnp.zeros((1, 128), f32)
                for r in range(0, T, tm):
                    rows = slice(r, r + tm)
                    cp = pltpu.make_async_copy(t_hbm.at[pl.ds(r, tm), :], tbuf, sem)
                    cp.start()
                    cp.wait()
                    e = xo_ref[rows, :] - tbuf[...]
                    dy = e * (1.0 / D)
                    xo_ref[rows, :] = dy
                    dyb_ref[rows, :] = (0.5 * dy).astype(bf16)
                    col = jnp.sum(e * e, axis=0, keepdims=True) * (0.5 / D)
                    for k in range(D // 128):
                        lanes = lanes + col[:, 128 * k:128 * (k + 1)]
                loss_ref[...] = lanes

    tok = pl.BlockSpec((T, D), lambda f: (0, 0))
    act_spec = pl.BlockSpec((1, T, tf), lambda f: (f, 0, 0))
    hbm = pl.BlockSpec(memory_space=pl.ANY)
    in_specs = [tok, pl.BlockSpec((3, tf, D), lambda f: (0, f, 0)), hbm]
    out_specs = [tok, act_spec, act_spec]
    out_shape = [SDS((T, D), f32), SDS((nf, T, tf), bf16), SDS((nf, T, tf), bf16)]
    scratch = [pltpu.SemaphoreType.DMA]
    args = [h, w, x]
    if with_loss:
        in_specs.append(hbm)
        args.append(target)
        out_specs += [tok, pl.BlockSpec((1, 128), lambda f: (0, 0))]
        out_shape += [SDS((T, D), bf16), SDS((1, 128), f32)]
        scratch = [pltpu.VMEM((tm, D), f32)] + scratch
    return pl.pallas_call(
        body, grid=(nf,), in_specs=in_specs, out_specs=out_specs, out_shape=tuple(out_shape), scratch_shapes=scratch,
        compiler_params=_cparams(("arbitrary",), VMEM_LIMIT_V7X), name=name)(*args)


def _ffn_bwd(dob, h, gate, up, w, name):
    T = h.shape[0]
    _, tf = _ffn_tiles(T)
    nf = F // tf

    def body(do_hbm, h_hbm, g_ref, u_ref, w_ref, dh_hbm, dw_ref, do_v, h_v, dh_acc, dgu_s, act_s, sems):
        fi = pl.program_id(0)

        @pl.when(fi == 0)
        def _():
            loads = [pltpu.make_async_copy(do_hbm, do_v, sems.at[0]), pltpu.make_async_copy(h_hbm, h_v, sems.at[1])]
            for cp in loads:
                cp.start()
            dh_acc[...] = jnp.zeros_like(dh_acc)
            for cp in loads:
                cp.wait()

        wgu = w_ref[0:2].reshape(2 * tf, D)
        for r in range(0, T, FFN_ROW_CHUNK):
            rows = slice(r, r + FFN_ROW_CHUNK)
            dov = do_v[rows, :]
            gv = g_ref[0, rows, :].astype(f32)
            uv = u_ref[0, rows, :].astype(f32)
            sg = _sigmoid(gv)
            sil = gv * sg
            dact = _nt(dov, w_ref[2])
            dup = dact * sil
            dgate = dact * uv * (sg * (1.0 + gv * (1.0 - sg)))
            dgu = jnp.concatenate([dgate.astype(bf16), dup.astype(bf16)], axis=1)
            dgu_s[rows, :] = dgu
            act_s[rows, :] = (sil * uv).astype(bf16)
            dh_acc[rows, :] += _nn(dgu, wgu)
        dw_ref[0:2] = _tn(dgu_s[...], h_v[...]).reshape(2, tf, D).astype(bf16)
        dw_ref[2] = _tn(act_s[...], do_v[...]).astype(bf16)

        @pl.when(fi == nf - 1)
        def _():
            out = pltpu.make_async_copy(dh_acc, dh_hbm, sems.at[0])
            out.start()
            out.wait()

    act_spec = pl.BlockSpec((1, T, tf), lambda f: (f, 0, 0))
    wspec = pl.BlockSpec((3, tf, D), lambda f: (0, f, 0))
    hbm = pl.BlockSpec(memory_space=pl.ANY)
    return pl.pallas_call(
        body, grid=(nf,),
        in_specs=[hbm, hbm, act_spec, act_spec, wspec],
        out_specs=[hbm, wspec],
        out_shape=(SDS((T, D), f32), SDS((3, F, D), bf16)),
        scratch_shapes=[pltpu.VMEM((T, D), bf16), pltpu.VMEM((T, D), bf16), pltpu.VMEM((T, D), f32),
                        pltpu.VMEM((T, 2 * tf), bf16), pltpu.VMEM((T, tf), bf16), pltpu.SemaphoreType.DMA((2,))],
        compiler_params=_cparams(("arbitrary",), VMEM_LIMIT_V7X), name=name)(dob, h, gate, up, w)


def _in_proj_fwd(h, wint, name):
    T = h.shape[0]
    tm = min(512, T)

    def body(h_ref, w_ref, z_ref):
        z_ref[...] = _nt(h_ref[...], w_ref[...])

    return pl.pallas_call(
        body, grid=(T // tm,),
        in_specs=[pl.BlockSpec((tm, D), lambda i: (i, 0)), pl.BlockSpec((DIN, D), lambda i: (0, 0))],
        out_specs=pl.BlockSpec((tm, DIN), lambda i: (i, 0)),
        out_shape=SDS((T, DIN), f32), name=name)(h, wint)


def _in_proj_bwd(dz, wint, h, name):
    T = h.shape[0]
    tm = min(512, T)
    nt = T // tm

    def body(dz_ref, w_ref, h_ref, dh_ref, dw_ref, acc):
        i = pl.program_id(0)
        dzb = dz_ref[...].astype(bf16)
        dh_ref[...] = _nn(dzb, w_ref[...])
        part = _tn(dzb, h_ref[...])

        @pl.when(i == 0)
        def _():
            acc[...] = part

        @pl.when(i > 0)
        def _():
            acc[...] += part

        @pl.when(i == nt - 1)
        def _():
            dw_ref[...] = acc[...].astype(bf16)

    wspec = pl.BlockSpec((DIN, D), lambda i: (0, 0))
    return pl.pallas_call(
        body, grid=(nt,),
        in_specs=[pl.BlockSpec((tm, DIN), lambda i: (i, 0)), wspec, pl.BlockSpec((tm, D), lambda i: (i, 0))],
        out_specs=[pl.BlockSpec((tm, D), lambda i: (i, 0)), wspec],
        out_shape=(SDS((T, D), f32), SDS((DIN, D), bf16)),
        scratch_shapes=[pltpu.VMEM((DIN, D), f32)],
        compiler_params=_cparams(("arbitrary",)), name=name)(dz, wint, h)


def _out_proj_fwd(ymix, wout, x, g, name):
    T = x.shape[0]
    tm = min(512, T)

    def body(y_ref, w_ref, x_ref, g_ref, o_ref, h_ref):
        o = x_ref[...] + _nn(y_ref[...], w_ref[...])
        o_ref[...] = o
        r = lax.rsqrt(jnp.mean(o * o, axis=-1, keepdims=True) + EPS)
        h_ref[...] = (o * r * g_ref[...]).astype(bf16)

    tok = pl.BlockSpec((tm, D), lambda i: (i, 0))
    return pl.pallas_call(
        body, grid=(T // tm,),
        in_specs=[pl.BlockSpec((tm, DMIX), lambda i: (i, 0)), pl.BlockSpec((DMIX, D), lambda i: (0, 0)), tok,
                  pl.BlockSpec((1, D), lambda i: (0, 0))],
        out_specs=[tok, tok], out_shape=(SDS((T, D), f32), SDS((T, D), bf16)), name=name)(ymix, wout, x, g)


def _out_proj_bwd(dxb, wout, ymix, name):
    T = dxb.shape[0]
    tm = min(512, T)
    nt = T // tm

    def body(dx_ref, w_ref, y_ref, dy_ref, dw_ref, acc):
        i = pl.program_id(0)
        dxv = dx_ref[...]
        dy_ref[...] = _nt(dxv, w_ref[...])
        part = _tn(y_ref[...], dxv)

        @pl.when(i == 0)
        def _():
            acc[...] = part

        @pl.when(i > 0)
        def _():
            acc[...] += part

        @pl.when(i == nt - 1)
        def _():
            dw_ref[...] = acc[...].astype(bf16)

    wspec = pl.BlockSpec((DMIX, D), lambda i: (0, 0))
    return pl.pallas_call(
        body, grid=(nt,),
        in_specs=[pl.BlockSpec((tm, D), lambda i: (i, 0)), wspec, pl.BlockSpec((tm, DMIX), lambda i: (i, 0))],
        out_specs=[pl.BlockSpec((tm, DMIX), lambda i: (i, 0)), wspec],
        out_shape=(SDS((T, DMIX), f32), SDS((DMIX, D), bf16)),
        scratch_shapes=[pltpu.VMEM((DMIX, D), f32)],
        compiler_params=_cparams(("arbitrary",)), name=name)(dxb, wout, ymix)


def _t5_bucket_table():
    ql = np.arange(BLK)[:, None]
    kl = np.arange(2 * BLK)[None, :]
    n = np.maximum(ql + BLK - kl, 0)
    max_exact = NBUCK // 2
    large = max_exact + (np.log(np.maximum(n, 1) / max_exact) / np.log(MAX_DISTANCE / max_exact)
                         * (NBUCK - max_exact)).astype(np.int32)
    large = np.minimum(large, NBUCK - 1)
    return np.where(n < max_exact, n, large).astype(np.int32)


def _fill_bias(bk_ref, rb_ref, bias_scr):
    bk = bk_ref[...]
    for h in range(NH):
        def step(b, acc, h=h):
            return acc + jnp.where(bk == b, rb_ref[b, h], 0.0)
        bias_scr[h] = lax.fori_loop(0, NBUCK, step, jnp.zeros((BLK, 2 * BLK), f32))


MIX_SUB = 4


class _Window:
    def __init__(self, zc_ref, zp_ref, n, s):
        self.blk = n * MIX_SUB + s
        self.first_in_step = s == 0
        self.cur = lambda a, b: zc_ref[s * BLK:(s + 1) * BLK, a:b]
        self.prev = (lambda a, b: zp_ref[:, a:b]) if s == 0 else (lambda a, b: zc_ref[(s - 1) * BLK:s * BLK, a:b])


def _attn_qkv(win, kh, qg, kg):
    kc = DATTN + HD * kh
    vc = DATTN + DKV + HD * kh
    kx = jnp.concatenate([win.prev(kc, kc + HD), win.cur(kc, kc + HD)], axis=0)
    vx = jnp.concatenate([win.prev(vc, vc + HD), win.cur(vc, vc + HD)], axis=0)
    qx = jnp.concatenate([win.cur(HD * (GQA * kh + g), HD * (GQA * kh + g + 1)) for g in range(GQA)], axis=0)
    rq = lax.rsqrt(jnp.mean(qx * qx, axis=-1, keepdims=True) + EPS)
    rk = lax.rsqrt(jnp.mean(kx * kx, axis=-1, keepdims=True) + EPS)
    qhat, khat = qx * rq, kx * rk
    return dict(qhat=qhat, khat=khat, rq=rq, rk=rk, qnb=(qhat * qg).astype(bf16), knb=(khat * kg).astype(bf16),
                vb=vx.astype(bf16))


def _attn_probs(a, kh, sk_ref, bias_scr, n):
    s = _nt(a["qnb"], a["knb"]) * SCALE + bias_scr[GQA * kh:GQA * (kh + 1)].reshape(GQA * BLK, 2 * BLK)
    row = lax.broadcasted_iota(i32, (GQA * BLK, 2 * BLK), 0) & (BLK - 1)
    col = lax.broadcasted_iota(i32, (GQA * BLK, 2 * BLK), 1)
    mask = (col > row) & (col <= row + BLK) & ((col >= BLK) | (n > 0))
    s = jnp.where(mask, s, NEG)
    ridx = lax.broadcasted_iota(i32, (GQA * BLK, 1), 0)
    sink = jnp.full((GQA * BLK, 1), sk_ref[GQA * kh + GQA - 1], f32)
    for g in range(GQA - 2, -1, -1):
        sink = jnp.where(ridx < (g + 1) * BLK, sk_ref[GQA * kh + g], sink)
    m = jnp.maximum(jnp.max(s, axis=-1, keepdims=True), sink)
    e = jnp.exp(s - m)
    den = jnp.sum(e, axis=-1, keepdims=True) + jnp.exp(sink - m)
    return e / den


POOL_STEPS = {2: (1,), 4: (1, 2), 8: (1, 2, 4), 16: (1, 2, 4, 8)}


def _pool_group(win, g, w):
    n = win.blk
    c0 = DATTN + 2 * DKV + PGD * g
    uc = win.cur(c0, c0 + PGD)
    up = jnp.where(n > 0, win.prev(c0, c0 + PGD), 0.0)
    sm = jnp.concatenate([up, uc], axis=0)
    for k in POOL_STEPS[w]:
        sm = sm + pltpu.roll(sm, k, axis=0)
    pos = n * BLK + lax.broadcasted_iota(i32, (BLK, 1), 0) + 1
    cnt = jnp.minimum(pos, w).astype(f32)
    return sm[BLK:2 * BLK] / cnt - uc, cnt


def _mix_fwd(z, qg, kg, sinks, relb, bucket, pool_w, pscale, name):
    T = z.shape[0]
    step_rows = MIX_SUB * BLK
    nsteps = T // step_rows

    def body(zc_ref, zp_ref, qg_ref, kg_ref, sk_ref, rb_ref, bk_ref, pw_ref, ps_ref, y_ref, p_ref, bias_scr, yacc):
        n = pl.program_id(0)

        @pl.when(n == 0)
        def _():
            _fill_bias(bk_ref, rb_ref, bias_scr)

        for s in range(MIX_SUB):
            win = _Window(zc_ref, zp_ref, n, s)
            rows = slice(s * BLK, (s + 1) * BLK)
            for kh in range(NKV):
                a = _attn_qkv(win, kh, qg_ref[...], kg_ref[...])
                pb = _attn_probs(a, kh, sk_ref, bias_scr, win.blk).astype(bf16)
                p_ref[s, GQA * kh:GQA * (kh + 1)] = pb.reshape(GQA, BLK, 2 * BLK)
                o = _nn(pb, a["vb"])
                for g in range(GQA):
                    hc = HD * (GQA * kh + g)
                    yacc[rows, hc:hc + HD] = o[g * BLK:(g + 1) * BLK]
            for g, w in enumerate(POOL_WINDOWS):
                pooled, _ = _pool_group(win, g, w)
                yp = _nn(pooled.astype(bf16), pw_ref[g].astype(bf16)) * ps_ref[:, PGD * g:PGD * (g + 1)]
                yacc[rows, DATTN + PGD * g:DATTN + PGD * (g + 1)] = yp
        y_ref[...] = yacc[...].astype(bf16)

    full = lambda *shape: pl.BlockSpec(shape, lambda n: (0,) * len(shape))
    smem = pl.BlockSpec(memory_space=pltpu.SMEM)
    return pl.pallas_call(
        body, grid=(nsteps,),
        in_specs=[pl.BlockSpec((step_rows, DIN), lambda n: (n, 0)),
                  pl.BlockSpec((BLK, DIN), lambda n: (jnp.maximum(n * MIX_SUB - 1, 0), 0)),
                  full(1, HD), full(1, HD), smem, smem, full(BLK, 2 * BLK),
                  full(len(POOL_WINDOWS), PGD, PGD), full(1, DPOOL)],
        out_specs=[pl.BlockSpec((step_rows, DMIX), lambda n: (n, 0)),
                   pl.BlockSpec((MIX_SUB, NH, BLK, 2 * BLK), lambda n: (n, 0, 0, 0))],
        out_shape=(SDS((T, DMIX), bf16), SDS((T // BLK, NH, BLK, 2 * BLK), bf16)),
        scratch_shapes=[pltpu.VMEM((NH, BLK, 2 * BLK), f32), pltpu.VMEM((step_rows, DMIX), f32)],
        compiler_params=_cparams(("arbitrary",)), name=name)(z, z, qg, kg, sinks, relb, bucket, pool_w, pscale)


def _mix_bwd(z, dy, probs, qg, kg, relb, bucket, pool_w, pscale, name):
    T = z.shape[0]
    step_rows = MIX_SUB * BLK
    nsteps = T // step_rows

    def body(zc_ref, zp_ref, dy_ref, p_ref, qg_ref, kg_ref, bk_ref, pw_ref, ps_ref,
             dz_ref, dqg_ref, dkg_ref, dsk_ref, drb_ref, dpw_ref, dps_ref, dbias_scr):
        n = pl.program_id(0)

        @pl.when(n == 0)
        def _():
            dbias_scr[...] = jnp.zeros_like(dbias_scr)
            dqg_ref[...] = jnp.zeros_like(dqg_ref)
            dkg_ref[...] = jnp.zeros_like(dkg_ref)
            dpw_ref[...] = jnp.zeros_like(dpw_ref)
            dps_ref[...] = jnp.zeros_like(dps_ref)

        qg, kg = qg_ref[...], kg_ref[...]
        for s in range(MIX_SUB):
            win = _Window(zc_ref, zp_ref, n, s)
            blk = win.blk
            rows = pl.ds(pl.multiple_of(blk * BLK, BLK), BLK)
            prow = pl.ds(pl.multiple_of(jnp.maximum(blk - 1, 0) * BLK, BLK), BLK)
            dyr = slice(s * BLK, (s + 1) * BLK)

            def into_prev(fn, s=s):
                if s == 0:
                    pl.when(n > 0)(fn)
                else:
                    fn()

            for kh in range(NKV):
                a = _attn_qkv(win, kh, qg, kg)
                pb = p_ref[s, GQA * kh:GQA * (kh + 1)].reshape(GQA * BLK, 2 * BLK)
                p = pb.astype(f32)
                do = jnp.concatenate([dy_ref[dyr, HD * (GQA * kh + g):HD * (GQA * kh + g + 1)] for g in range(GQA)],
                                     axis=0).astype(bf16)
                dv = _tn(pb, do)
                dp = _nt(do, a["vb"])
                delta = jnp.sum(p * dp, axis=-1, keepdims=True)
                ds = p * (dp - delta)
                for g in range(GQA):
                    dbias_scr[GQA * kh + g] += ds[g * BLK:(g + 1) * BLK]
                dsb = ds.astype(bf16)
                dqn = _nn(dsb, a["knb"]) * SCALE
                dkn = _tn(dsb, a["qnb"]) * SCALE
                qhat, khat = a["qhat"], a["khat"]
                dqg_ref[...] += jnp.sum(dqn * qhat, axis=0, keepdims=True)
                dkg_ref[...] += jnp.sum(dkn * khat, axis=0, keepdims=True)
                dqh = dqn * qg
                dq = a["rq"] * (dqh - qhat * jnp.mean(dqh * qhat, axis=-1, keepdims=True))
                dkh = dkn * kg
                dk = a["rk"] * (dkh - khat * jnp.mean(dkh * khat, axis=-1, keepdims=True))
                kc = DATTN + HD * kh
                vc = DATTN + DKV + HD * kh
                for g in range(GQA):
                    hc = HD * (GQA * kh + g)
                    dz_ref[rows, hc:hc + HD] = dq[g * BLK:(g + 1) * BLK]
                dz_ref[rows, kc:kc + HD] = dk[BLK:2 * BLK]
                dz_ref[rows, vc:vc + HD] = dv[BLK:2 * BLK]

                def kv_prev(dk=dk, dv=dv, kc=kc, vc=vc, prow=prow):
                    dz_ref[prow, kc:kc + HD] += dk[0:BLK]
                    dz_ref[prow, vc:vc + HD] += dv[0:BLK]

                into_prev(kv_prev)

            for g, w in enumerate(POOL_WINDOWS):
                c0 = DATTN + 2 * DKV + PGD * g
                pooled, cnt = _pool_group(win, g, w)
                pb = pooled.astype(bf16)
                wb = pw_ref[g].astype(bf16)
                dyp = dy_ref[dyr, DATTN + PGD * g:DATTN + PGD * (g + 1)]
                ypre = _nn(pb, wb)
                dps_ref[:, PGD * g:PGD * (g + 1)] += jnp.sum(dyp * ypre, axis=0, keepdims=True)
                dyg = (dyp * ps_ref[:, PGD * g:PGD * (g + 1)]).astype(bf16)
                dpw_ref[g] += _tn(pb, dyg)
                dpooled = _nt(dyg, wb)
                due = jnp.concatenate([jnp.zeros((BLK, PGD), f32), dpooled / cnt], axis=0)
                for k in POOL_STEPS[w]:
                    due = due + pltpu.roll(due, 2 * BLK - k, axis=0)
                dz_ref[rows, c0:c0 + PGD] = due[BLK:2 * BLK] - dpooled

                def pool_prev(due=due, c0=c0, prow=prow):
                    dz_ref[prow, c0:c0 + PGD] += due[0:BLK]

                into_prev(pool_prev)

        @pl.when(n == nsteps - 1)
        def _():
            bk = bk_ref[...]
            ri = lax.broadcasted_iota(i32, (NBUCK, NH), 0)
            ci = lax.broadcasted_iota(i32, (NBUCK, NH), 1)

            def step(b, acc):
                for h in range(NH):
                    sel = jnp.where(bk == b, dbias_scr[h], 0.0)
                    tot = jnp.sum(jnp.sum(sel, axis=1, keepdims=True), axis=0, keepdims=True)
                    acc = acc + jnp.where((ri == b) & (ci == h), tot, 0.0)
                return acc

            drb_ref[...] = lax.fori_loop(0, NBUCK, step, jnp.zeros((NBUCK, NH), f32))
            lane = lax.broadcasted_iota(i32, (1, 128), 1)
            dsk = jnp.zeros((1, 128), f32)
            for h in range(NH):
                tot = jnp.sum(jnp.sum(dbias_scr[h], axis=1, keepdims=True), axis=0, keepdims=True)
                dsk = dsk - jnp.where(lane == h, tot, 0.0)
            dsk_ref[...] = dsk

    full = lambda *shape: pl.BlockSpec(shape, lambda n: (0,) * len(shape))
    npg = len(POOL_WINDOWS)
    return pl.pallas_call(
        body, grid=(nsteps,),
        in_specs=[pl.BlockSpec((step_rows, DIN), lambda n: (n, 0)),
                  pl.BlockSpec((BLK, DIN), lambda n: (jnp.maximum(n * MIX_SUB - 1, 0), 0)),
                  pl.BlockSpec((step_rows, DMIX), lambda n: (n, 0)),
                  pl.BlockSpec((MIX_SUB, NH, BLK, 2 * BLK), lambda n: (n, 0, 0, 0)),
                  full(1, HD), full(1, HD), full(BLK, 2 * BLK), full(npg, PGD, PGD), full(1, DPOOL)],
        out_specs=[full(T, DIN), full(1, HD), full(1, HD), full(1, 128), full(NBUCK, NH),
                   full(npg, PGD, PGD), full(1, DPOOL)],
        out_shape=(SDS((T, DIN), f32), SDS((1, HD), f32), SDS((1, HD), f32), SDS((1, 128), f32),
                   SDS((NBUCK, NH), f32), SDS((npg, PGD, PGD), f32), SDS((1, DPOOL), f32)),
        scratch_shapes=[pltpu.VMEM((NH, BLK, 2 * BLK), f32)],
        compiler_params=_cparams(("arbitrary",), VMEM_LIMIT_V7X),
        name=name)(z, z, dy, probs, qg, kg, bucket, pool_w, pscale)


class _LocalWeights:
    def __init__(self, w1, wint, wout, w2):
        self.w1, self.wint, self.wout, self.w2 = w1, wint, wout, w2

    def ffn1(self):
        return self.w1

    def after_ffn1(self, gain, x1):
        return gain

    def mix(self, after):
        return self.wint, self.wout

    def before_out_proj(self, wout, after):
        return wout

    def ffn2(self, after):
        return self.w2

    def mix_ffn2_grads_ready(self, dwint, dwout, dw2, dh2):
        self.grads_rest = (dwint, dwout, dw2)
        return dh2

    def before_ffn1_bwd(self, dx1b):
        return dx1b


def _local_step(x, target, weights, g1, gm, g3, qg, kg, sinks, relb, pool_w, pscale):
    bucket = jnp.asarray(_t5_bucket_table())
    sk = sinks.reshape(NH)
    w1 = weights.ffn1()
    h1 = _norm_fwd(x, g1, "norm1_fwd")
    x1, gate1, up1 = _ffn_fwd(h1, w1, x, None, "ffn1_fwd")
    h2 = _norm_fwd(x1, weights.after_ffn1(gm, x1), "norm2_fwd")
    wint, wout = weights.mix(h2)
    z = _in_proj_fwd(h2, wint, "in_proj_fwd")
    ymix, probs = _mix_fwd(z, qg, kg, sk, relb, bucket, pool_w, pscale, "mix_fwd")
    wout = weights.before_out_proj(wout, ymix)
    x2, h3 = _out_proj_fwd(ymix, wout, x1, g3, "out_proj_fwd")
    w2 = weights.ffn2(h3)
    dy, gate2, up2, dyb, loss_lanes = _ffn_fwd(h3, w2, x2, target, "ffn2_fwd")

    dh3, dw2 = _ffn_bwd(dyb, h3, gate2, up2, w2, "ffn2_bwd")
    dx2, dx2b, dg3 = _norm_bwd(dh3, x2, g3, dy, 1.0, "norm3_bwd")
    dymix, dwout = _out_proj_bwd(dx2b, wout, ymix, "out_proj_bwd")
    dz, dqg, dkg, dsk, drb, dpw, dps = _mix_bwd(z, dymix, probs, qg, kg, relb, bucket, pool_w, pscale, "mix_bwd")
    dh2, dwint = _in_proj_bwd(dz, wint, h2, "in_proj_bwd")
    dh2 = weights.mix_ffn2_grads_ready(dwint, dwout, dw2, dh2)
    dx1, dx1b, dgm = _norm_bwd(dh2, x1, gm, dx2, 0.5, "norm2_bwd")
    dx1b = weights.before_ffn1_bwd(dx1b)
    dh1, dw1 = _ffn_bwd(dx1b, h1, gate1, up1, w1, "ffn1_bwd")
    gx, _, dg1 = _norm_bwd(dh1, x, g1, dx1, 1.0, "norm1_bwd")
    small = dict(ffn1_norm=dg1, mix_norm=dgm, ffn2_norm=dg3, pool_scale=dps, q_norm=dqg, k_norm=dkg,
                 attn_sinks=dsk[:, :NH], rel_bias=drb, pool_w=dpw, loss=loss_lanes)
    return gx, (dw1, dwint, dwout, dw2), small


SMALL_NAMES = ("ffn1_norm", "mix_norm", "ffn2_norm", "pool_scale", "q_norm", "k_norm", "attn_sinks", "rel_bias",
               "pool_w", "loss")
SMALL_SHAPES = dict(ffn1_norm=(1, D), mix_norm=(1, D), ffn2_norm=(1, D), pool_scale=(1, DPOOL), q_norm=(1, HD),
                    k_norm=(1, HD), attn_sinks=(1, NH), rel_bias=(NBUCK, NH),
                    pool_w=(1, len(POOL_WINDOWS), PGD, PGD), loss=(1, 128))


def _small_rows(name):
    return -(-int(np.prod(SMALL_SHAPES[name])) // 128)


SMALL_OFF = {}
_r = 0
for _n in SMALL_NAMES:
    SMALL_OFF[_n] = _r
    _r += _small_rows(_n)
SMALL_ROWS = -(-_r // 8) * 8
LOSS_ROW = SMALL_OFF["loss"]


def _pack_small(vals):
    parts = []
    for n in SMALL_NAMES:
        size = _small_rows(n) * 128
        if n in vals:
            flat = vals[n].astype(f32).reshape(-1)
            parts.append(jnp.pad(flat, (0, size - flat.shape[0])))
        else:
            parts.append(jnp.zeros((size,), f32))
    flat = jnp.concatenate(parts)
    flat = jnp.pad(flat, (0, SMALL_ROWS * 128 - flat.shape[0]))
    return flat.reshape(SMALL_ROWS, 128)


def _unpack_small(packed, name):
    size = int(np.prod(SMALL_SHAPES[name]))
    r0 = SMALL_OFF[name]
    return packed[r0:r0 + _small_rows(name)].reshape(-1)[:size].reshape(SMALL_SHAPES[name])


def _position():
    return lax.axis_index("x"), lax.axis_index("y"), lax.axis_index("c")


def _dev_index(x, y, c):
    return 4 * x + 2 * y + c


G1_PIECES, MIX_PIECES, F2_PIECES = (0, 1, 2), (3, 4), (5, 6, 7)


def _group_rows(pieces):
    return sum(PIECE_ROWS[k] for k in pieces)


def _shard_piece(s_ref, k):
    return s_ref.at[pl.ds(PIECE_OFF[k], PIECE_ROWS[k]), :]


def _shard_group(s_ref, pieces):
    return s_ref.at[pl.ds(PIECE_OFF[pieces[0]], _group_rows(pieces)), :]


def _weight_pieces(w1_ref=None, wi_ref=None, wo_ref=None, w2_ref=None):
    arrs = {}
    if w1_ref is not None:
        arrs.update({0: w1_ref.at[0], 1: w1_ref.at[1], 2: w1_ref.at[2]})
    if wi_ref is not None:
        arrs[3] = wi_ref
    if wo_ref is not None:
        arrs[4] = wo_ref
    if w2_ref is not None:
        arrs.update({5: w2_ref.at[0], 6: w2_ref.at[1], 7: w2_ref.at[2]})
    return arrs


def _block_rows(arrs, k, dev):
    r = PIECE_ROWS[k]
    return arrs[k].at[pl.ds(pl.multiple_of(_dev_index(*dev) * r, 16), r), :]


def _all_gather_ffn1(shard):
    pieces = G1_PIECES

    def body(s_ref, w1_ref, send_sems, recv_sems, local_sem):
        x, y, c = _position()
        me, sib = (x, y, c), (x, y, 1 - c)
        chips = [(1 - x, y), (x, 1 - y), (1 - x, 1 - y)]
        arrs = _weight_pieces(w1_ref=w1_ref)

        def copies(rel, block, to, from_shard):
            return [pltpu.make_async_remote_copy(
                src_ref=_shard_piece(s_ref, k) if from_shard else _block_rows(arrs, k, block),
                dst_ref=_block_rows(arrs, k, block),
                send_sem=send_sems.at[rel], recv_sem=recv_sems.at[rel], device_id=to, device_id_type=MESH)
                for k in pieces]

        def whole(rel):
            grp = _shard_group(s_ref, pieces)
            return pltpu.make_async_remote_copy(src_ref=grp, dst_ref=grp, send_sem=send_sems.at[rel],
                                                recv_sem=recv_sems.at[rel], device_id=me, device_id_type=MESH)

        mine = [pltpu.make_async_copy(_shard_piece(s_ref, k), _block_rows(arrs, k, me), local_sem) for k in pieces]
        for cp in mine:
            cp.start()
        for cp in copies(0, me, sib, True):
            cp.start()
        for j, chip in enumerate(chips):
            for cp in copies(1 + j, me, (*chip, c), True):
                cp.start()
        for j, chip in enumerate(chips):
            whole(1 + j).wait_recv()
            for cp in copies(4 + j, (*chip, c), sib, False):
                cp.start()
        whole(0).wait_recv()
        for j in range(3):
            whole(4 + j).wait_recv()
        for rel in range(7):
            whole(rel).wait_send()
        grp = _shard_group(s_ref, pieces)
        pltpu.make_async_copy(grp, grp, local_sem).wait()

    hbm = pl.BlockSpec(memory_space=pl.ANY)
    return pl.pallas_call(
        body, in_specs=[hbm], out_specs=hbm, out_shape=SDS((3, F, D), bf16),
        scratch_shapes=[pltpu.SemaphoreType.DMA((7,)), pltpu.SemaphoreType.DMA((7,)), pltpu.SemaphoreType.DMA],
        compiler_params=pltpu.CompilerParams(has_side_effects=True),
        name="all_gather_ffn1")(shard)


HBM_SPEC = pl.BlockSpec(memory_space=pltpu.HBM)
SEM_SPEC = pl.BlockSpec(memory_space=pltpu.SEMAPHORE)
ANY_SPEC = pl.BlockSpec(memory_space=pl.ANY)
SPLIT_EFFECT = pltpu.SideEffectType.DATAFLOW_SIDE_EFFECTING


def _in_hbm(a):
    return pltpu.with_memory_space_constraint(a, pltpu.HBM)


def _hbm_like(a):
    return pltpu.HBM(a.shape, a.dtype)


def _place_own_rows(shard):
    pieces = MIX_PIECES + F2_PIECES

    def body(s_ref, wi_ref, wo_ref, w2_ref, buf, sems):
        x, y, c = _position()
        arrs = _weight_pieces(wi_ref=wi_ref, wo_ref=wo_ref, w2_ref=w2_ref)
        grp = _shard_group(s_ref, pieces)
        load = pltpu.make_async_copy(grp, buf, sems.at[0])
        load.start()
        load.wait()
        base = PIECE_OFF[pieces[0]]
        for k in pieces:
            pltpu.make_async_copy(buf.at[pl.ds(PIECE_OFF[k] - base, PIECE_ROWS[k]), :],
                                  _block_rows(arrs, k, (x, y, c)), sems.at[1]).start()
        pltpu.make_async_copy(grp, buf, sems.at[1]).wait()

    return pl.pallas_call(
        body, in_specs=[ANY_SPEC], out_specs=[ANY_SPEC] * 3,
        out_shape=(SDS((DIN, D), bf16), SDS((DMIX, D), bf16), SDS((3, F, D), bf16)),
        scratch_shapes=[pltpu.VMEM((_group_rows(pieces), D), bf16), pltpu.SemaphoreType.DMA((2,))],
        name="place_own_rows")(shard)


def _xor_peer(x, y, c, k):
    return (x ^ (k >> 2), y ^ ((k >> 1) & 1), c ^ (k & 1))


def _gather_rest_start(shard, wi, wo, w2, w1):
    def body(s_ref, wi_ref, wo_ref, w2_ref, w1_ref,
             ssem_m, rsem_m0, rsem_m, ssem_f, rsem_f0, rsem_f, s_o, wi_o, wo_o, w2_o, w1_o):
        x, y, c = _position()
        me, sib = (x, y, c), (x, y, 1 - c)
        chips = [(1 - x, y), (x, 1 - y), (1 - x, 1 - y)]
        arrs = _weight_pieces(wi_ref=wi_ref, wo_ref=wo_ref, w2_ref=w2_ref)
        for pieces, ssem, rsem0, rsem in ((MIX_PIECES, ssem_m, rsem_m0, rsem_m), (F2_PIECES, ssem_f, rsem_f0, rsem_f)):
            for p in pieces:
                pltpu.make_async_remote_copy(
                    src_ref=_shard_piece(s_ref, p), dst_ref=_block_rows(arrs, p, me), send_sem=ssem.at[0],
                    recv_sem=rsem0, device_id=sib, device_id_type=MESH).start()
            for j, chip in enumerate(chips):
                for p in pieces:
                    pltpu.make_async_remote_copy(
                        src_ref=_shard_piece(s_ref, p), dst_ref=_block_rows(arrs, p, me), send_sem=ssem.at[1 + j],
                        recv_sem=rsem.at[j], device_id=(*chip, c), device_id_type=MESH).start()

    dma = pltpu.SemaphoreType.DMA
    return pl.pallas_call(
        body, name="gather_rest_start",
        out_shape=(dma((4,)), dma(()), dma((3,)), dma((4,)), dma(()), dma((3,)),
                   _hbm_like(shard), _hbm_like(wi), _hbm_like(wo), _hbm_like(w2), _hbm_like(w1)),
        in_specs=(HBM_SPEC,) * 5, out_specs=(SEM_SPEC,) * 6 + (HBM_SPEC,) * 5,
        input_output_aliases={0: 6, 1: 7, 2: 8, 3: 9, 4: 10},
        compiler_params=pltpu.CompilerParams(has_side_effects=SPLIT_EFFECT),
    )(_in_hbm(shard), _in_hbm(wi), _in_hbm(wo), _in_hbm(w2), _in_hbm(w1))


def _gather_mix_pass_on(rsem_m, wi, wo, thru, after):
    def body(wi_ref, wo_ref, thru_ref, rsem, after_ref, fsend, frecv, wi_o, wo_o, thru_o):
        x, y, c = _position()
        sib = (x, y, 1 - c)
        arrs = _weight_pieces(wi_ref=wi_ref, wo_ref=wo_ref)
        both = wi_ref.at[pl.ds(0, _group_rows(MIX_PIECES)), :]
        for j, chip in enumerate([(1 - x, y), (x, 1 - y), (1 - x, 1 - y)]):
            pltpu.make_async_remote_copy(src_ref=both, dst_ref=both, send_sem=fsend.at[j], recv_sem=rsem.at[j],
                                         device_id=(x, y, c), device_id_type=MESH).wait_recv()
            for p in MIX_PIECES:
                rows = _block_rows(arrs, p, (*chip, c))
                pltpu.make_async_remote_copy(src_ref=rows, dst_ref=rows, send_sem=fsend.at[j], recv_sem=frecv.at[j],
                                             device_id=sib, device_id_type=MESH).start()

    dma = pltpu.SemaphoreType.DMA
    return pl.pallas_call(
        body, name="gather_mix_pass_on",
        out_shape=(dma((3,)), dma((3,)), _hbm_like(wi), _hbm_like(wo), _hbm_like(thru)),
        in_specs=(HBM_SPEC, HBM_SPEC, HBM_SPEC, SEM_SPEC, ANY_SPEC), out_specs=(SEM_SPEC, SEM_SPEC) + (HBM_SPEC,) * 3,
        input_output_aliases={0: 2, 1: 3, 2: 4},
        compiler_params=pltpu.CompilerParams(has_side_effects=SPLIT_EFFECT),
    )(wi, wo, _in_hbm(thru), rsem_m, after)


def _gather_mix_wait(ssem_m, rsem_m0, fsend, frecv, shard, wi, wo, after):
    def body(s_ref, wi_ref, wo_ref, ssem, rsem0, fs, fr, after_ref, s_o, wi_o, wo_o):
        x, y, c = _position()
        grp = _shard_group(s_ref, MIX_PIECES)

        def waiter(send_sem, recv_sem):
            return pltpu.make_async_remote_copy(src_ref=grp, dst_ref=grp, send_sem=send_sem, recv_sem=recv_sem,
                                                device_id=(x, y, c), device_id_type=MESH)

        waiter(ssem.at[0], rsem0).wait_recv()
        for j in range(3):
            waiter(fs.at[j], fr.at[j]).wait_recv()
        for rel in range(4):
            waiter(ssem.at[rel], rsem0).wait_send()
        for j in range(3):
            waiter(fs.at[j], fr.at[j]).wait_send()

    return pl.pallas_call(
        body, name="gather_mix_wait", out_shape=(_hbm_like(shard), _hbm_like(wi), _hbm_like(wo)),
        in_specs=(HBM_SPEC,) * 3 + (SEM_SPEC,) * 4 + (ANY_SPEC,), out_specs=(HBM_SPEC,) * 3,
        input_output_aliases={0: 0, 1: 1, 2: 2},
        compiler_params=pltpu.CompilerParams(has_side_effects=SPLIT_EFFECT),
    )(shard, wi, wo, ssem_m, rsem_m0, fsend, frecv, after)


def _gather_ffn2_pass_on(rsem_f, w2, wo, after):
    def body(w2_ref, wo_ref, rsem, after_ref, fsend, frecv, w2_o, wo_o):
        x, y, c = _position()
        sib = (x, y, 1 - c)
        chips = [(1 - x, y), (x, 1 - y), (1 - x, 1 - y)]
        arrs = _weight_pieces(w2_ref=w2_ref)
        three = w2_ref.at[0, pl.ds(0, _group_rows(F2_PIECES)), :]
        for j, chip in enumerate(chips):
            pltpu.make_async_remote_copy(src_ref=three, dst_ref=three, send_sem=fsend.at[j], recv_sem=rsem.at[j],
                                         device_id=(x, y, c), device_id_type=MESH).wait_recv()
            for p in F2_PIECES:
                rows = _block_rows(arrs, p, (*chip, c))
                pltpu.make_async_remote_copy(src_ref=rows, dst_ref=rows, send_sem=fsend.at[j], recv_sem=frecv.at[j],
                                             device_id=sib, device_id_type=MESH).start()

    dma = pltpu.SemaphoreType.DMA
    return pl.pallas_call(
        body, name="gather_ffn2_pass_on", out_shape=(dma((3,)), dma((3,)), _hbm_like(w2), _hbm_like(wo)),
        in_specs=(HBM_SPEC, HBM_SPEC, SEM_SPEC, ANY_SPEC), out_specs=(SEM_SPEC, SEM_SPEC, HBM_SPEC, HBM_SPEC),
        input_output_aliases={0: 2, 1: 3},
        compiler_params=pltpu.CompilerParams(has_side_effects=SPLIT_EFFECT),
    )(w2, wo, rsem_f, after)


def _gather_ffn2_wait(ssem_f, rsem_f0, fsend, frecv, shard, w2, after):
    def body(s_ref, w2_ref, ssem, rsem0, fs, fr, after_ref, w2_o):
        x, y, c = _position()
        grp = _shard_group(s_ref, F2_PIECES)

        def waiter(send_sem, recv_sem):
            return pltpu.make_async_remote_copy(src_ref=grp, dst_ref=grp, send_sem=send_sem, recv_sem=recv_sem,
                                                device_id=(x, y, c), device_id_type=MESH)

        waiter(ssem.at[0], rsem0).wait_recv()
        for j in range(3):
            waiter(fs.at[j], fr.at[j]).wait_recv()
        for rel in range(4):
            waiter(ssem.at[rel], rsem0).wait_send()
        for j in range(3):
            waiter(fs.at[j], fr.at[j]).wait_send()

    return pl.pallas_call(
        body, name="gather_ffn2_wait", out_shape=_hbm_like(w2),
        in_specs=(HBM_SPEC, HBM_SPEC, SEM_SPEC, SEM_SPEC, SEM_SPEC, SEM_SPEC, ANY_SPEC), out_specs=HBM_SPEC,
        input_output_aliases={1: 0},
        compiler_params=pltpu.CompilerParams(has_side_effects=SPLIT_EFFECT),
    )(shard, w2, ssem_f, rsem_f0, fsend, frecv, after)


class _GatheredWeights(_LocalWeights):
    def __init__(self, shard):
        w1 = _all_gather_ffn1(shard)
        wi, wo, w2 = _place_own_rows(shard)
        (self.ssem_m, self.rsem_m0, self.rsem_m, self.ssem_f, self.rsem_f0, self.rsem_f,
         self.shard, self.wi, self.wo, self.w2_part, self.w1) = _gather_rest_start(shard, wi, wo, w2, w1)

    def after_ffn1(self, gain, x1):
        self.fsend_m, self.frecv_m, self.wi, self.wo, gain = _gather_mix_pass_on(self.rsem_m, self.wi, self.wo, gain, x1)
        return gain

    def mix(self, after):
        self.shard, wint, wout = _gather_mix_wait(self.ssem_m, self.rsem_m0, self.fsend_m, self.frecv_m, self.shard,
                                                  self.wi, self.wo, after)
        return wint, wout

    def before_out_proj(self, wout, after):
        self.fsend, self.frecv, self.w2_part, wout = _gather_ffn2_pass_on(self.rsem_f, self.w2_part, wout, after)
        return wout

    def ffn2(self, after):
        return _gather_ffn2_wait(self.ssem_f, self.rsem_f0, self.fsend, self.frecv, self.shard, self.w2_part, after)

    def mix_ffn2_grads_ready(self, dwint, dwout, dw2, dh2):
        rx1 = lax.empty((4, RSA_ROWS, D), bf16)
        self.sa, self.ra, dwint, dwout, dw2, rx1, dh2 = _rsa_level1_start(dwint, dwout, dw2, rx1, dh2)
        self.level1 = (dwint, dwout, dw2, rx1)
        return dh2

    def before_ffn1_bwd(self, dx1b):
        dwint, dwout, dw2, rx1 = _rsa_level1_wait(self.sa, self.ra, *self.level1, dx1b)
        tx, self.acc = _rsa_chip_sums(dwint, dwout, dw2, rx1)
        rx2 = lax.empty((3, RSA_ROWS, D), bf16)
        self.sb, self.rb, self.tx, self.rx2, dx1b = _rsa_level2_start(tx, rx2, dx1b)
        return dx1b

    def mix_ffn2_grads_total(self, after):
        rx2 = _rsa_level2_wait(self.sb, self.rb, self.tx, self.rx2, after)
        return _rsa_total(self.acc, rx2)


RS_CHUNK = 176


def _reduce_scatter_ffn1(dw1, small_packed):
    pieces = G1_PIECES
    nrows = _group_rows(pieces)
    nchunk = nrows // RS_CHUNK

    def body(d1_ref, p_ref, red_ref, rx1_ref, rx2_ref, tot_ref,
             own_buf, rx_buf, tx_buf, acc, sa, ra, sb, rb, lsem, pair, chips, small_send, small_recv):
        x, y, c = _position()
        me, sib = (x, y, c), (x, y, 1 - c)
        rel_chips = [(x, y), (1 - x, y), (x, 1 - y), (1 - x, 1 - y)]
        srcs = _weight_pieces(w1_ref=d1_ref)

        my_chip = 2 * x + y
        pair[c] = p_ref[...]
        swap = pltpu.make_async_remote_copy(
            src_ref=p_ref, dst_ref=pair.at[c], send_sem=small_send.at[0], recv_sem=small_recv.at[0],
            device_id=sib, device_id_type=MESH)
        swap.start()
        small = [pltpu.make_async_remote_copy(
            src_ref=chips.at[my_chip], dst_ref=chips.at[my_chip], send_sem=small_send.at[j], recv_sem=small_recv.at[j],
            device_id=(*rel_chips[j], c), device_id_type=MESH) for j in (1, 2, 3)]

        def piece(k, dev):
            r = PIECE_ROWS[k]
            return srcs[k].at[pl.ds(pl.multiple_of(_dev_index(*dev) * r, 16), r), :]

        def packed(ref, k):
            return ref.at[pl.ds(PIECE_OFF[k], PIECE_ROWS[k]), :]

        for j, chip in enumerate(rel_chips):
            for k in pieces:
                pltpu.make_async_remote_copy(
                    src_ref=piece(k, (*chip, 1 - c)), dst_ref=packed(rx1_ref.at[j], k),
                    send_sem=sa.at[j], recv_sem=ra.at[j], device_id=sib, device_id_type=MESH).start()

        def wait_a(j):
            return pltpu.make_async_remote_copy(src_ref=rx1_ref.at[j], dst_ref=rx1_ref.at[j], send_sem=sa.at[j],
                                                recv_sem=ra.at[j], device_id=me, device_id_type=MESH)

        def ici(j):
            return pltpu.make_async_remote_copy(
                src_ref=tx_buf.at[j - 1], dst_ref=rx2_ref.at[j - 1], send_sem=sb.at[j - 1], recv_sem=rb.at[j - 1],
                device_id=(*rel_chips[j], c), device_id_type=MESH)

        swap.wait_recv()
        chips[my_chip] = pair[0] + pair[1]
        for cp in small:
            cp.start()

        for j in (1, 2, 3, 0):
            loads = [pltpu.make_async_copy(piece(k, (*rel_chips[j], c)), packed(own_buf, k), lsem)
                     for k in pieces]
            for cp in loads:
                cp.start()
            wait_a(j).wait_recv()
            got = pltpu.make_async_copy(rx1_ref.at[j], rx_buf, lsem)
            got.start()
            pltpu.make_async_copy(rx_buf, rx_buf, lsem).wait()
            got.wait()

            def add(i, carry, j=j):
                rows = pl.ds(pl.multiple_of(i * RS_CHUNK, 16), RS_CHUNK)
                tot = own_buf[rows, :].astype(f32) + rx_buf[rows, :].astype(f32)
                if j == 0:
                    acc[rows, :] = tot
                else:
                    tx_buf[j - 1, rows, :] = tot.astype(bf16)
                return carry

            lax.fori_loop(0, nchunk, add, 0)
            if j != 0:
                ici(j).start()

        for j in (1, 2, 3):
            ici(j).wait_recv()
            got = pltpu.make_async_copy(rx2_ref.at[j - 1], rx_buf, lsem)
            got.start()
            got.wait()

            def add2(i, carry):
                rows = pl.ds(pl.multiple_of(i * RS_CHUNK, 16), RS_CHUNK)
                acc[rows, :] += rx_buf[rows, :].astype(f32)
                return carry

            lax.fori_loop(0, nchunk, add2, 0)
        out = pltpu.make_async_copy(acc, red_ref, lsem)
        out.start()
        out.wait()
        for cp in small:
            cp.wait_recv()
        tot = (chips[0] + chips[1]) + (chips[2] + chips[3])
        tot_ref[...] = tot
        loss = jnp.sum(tot[LOSS_ROW:LOSS_ROW + 1, :], axis=-1, keepdims=True)
        tot_ref[LOSS_ROW:LOSS_ROW + 1, :] = jnp.broadcast_to(loss, (1, 128))
        for j in range(4):
            wait_a(j).wait_send()
        for j in (1, 2, 3):
            ici(j).wait_send()
        swap.wait_send()
        for cp in small:
            cp.wait_send()

    hbm = pl.BlockSpec(memory_space=pl.ANY)
    vm = pl.BlockSpec(memory_space=pltpu.VMEM)
    red, _, _, small_tot = pl.pallas_call(
        body, in_specs=[hbm, vm], out_specs=[hbm] * 3 + [vm],
        out_shape=(SDS((nrows, D), f32), SDS((4, nrows, D), bf16), SDS((3, nrows, D), bf16),
                   SDS((SMALL_ROWS, 128), f32)),
        scratch_shapes=[pltpu.VMEM((nrows, D), bf16), pltpu.VMEM((nrows, D), bf16),
                        pltpu.VMEM((3, nrows, D), bf16), pltpu.VMEM((nrows, D), f32),
                        pltpu.SemaphoreType.DMA((4,)), pltpu.SemaphoreType.DMA((4,)),
                        pltpu.SemaphoreType.DMA((3,)), pltpu.SemaphoreType.DMA((3,)), pltpu.SemaphoreType.DMA,
                        pltpu.VMEM((2, SMALL_ROWS, 128), f32), pltpu.VMEM((4, SMALL_ROWS, 128), f32),
                        pltpu.SemaphoreType.DMA((4,)), pltpu.SemaphoreType.DMA((4,))],
        compiler_params=pltpu.CompilerParams(has_side_effects=True, vmem_limit_bytes=VMEM_LIMIT_V7X),
        name="reduce_scatter_ffn1")(dw1, small_packed)
    return red, small_tot


RSA_PIECES = MIX_PIECES + F2_PIECES
RSA_ROWS = _group_rows(RSA_PIECES)
RSA_OFF = {k: PIECE_OFF[k] - PIECE_OFF[RSA_PIECES[0]] for k in RSA_PIECES}
RSA_BLOCK = 192


def _rsa_rows(ref, k):
    return ref.at[pl.ds(RSA_OFF[k], PIECE_ROWS[k]), :]


def _rsa_level1_start(dwint, dwout, dw2, rx1, thru):
    def body(di_ref, do_ref, d2_ref, rx1_ref, thru_ref, sa, ra, di_o, do_o, d2_o, rx1_o, thru_o):
        x, y, c = _position()
        srcs = _weight_pieces(wi_ref=di_ref, wo_ref=do_ref, w2_ref=d2_ref)
        for j, chip in enumerate([(x, y), (1 - x, y), (x, 1 - y), (1 - x, 1 - y)]):
            for k in RSA_PIECES:
                pltpu.make_async_remote_copy(
                    src_ref=_block_rows(srcs, k, (*chip, 1 - c)), dst_ref=_rsa_rows(rx1_ref.at[j], k),
                    send_sem=sa.at[j], recv_sem=ra.at[j], device_id=(x, y, 1 - c), device_id_type=MESH).start()

    dma = pltpu.SemaphoreType.DMA
    arrs = (dwint, dwout, dw2, rx1, thru)
    return pl.pallas_call(
        body, name="rsa_level1_start", out_shape=(dma((4,)), dma((4,))) + tuple(_hbm_like(a) for a in arrs),
        in_specs=(HBM_SPEC,) * 5, out_specs=(SEM_SPEC,) * 2 + (HBM_SPEC,) * 5,
        input_output_aliases={0: 2, 1: 3, 2: 4, 3: 5, 4: 6},
        compiler_params=pltpu.CompilerParams(has_side_effects=SPLIT_EFFECT),
    )(*[_in_hbm(a) for a in arrs])


def _rsa_level1_wait(sa, ra, dwint, dwout, dw2, rx1, after):
    def body(di_ref, do_ref, d2_ref, rx1_ref, sa_ref, ra_ref, after_ref, di_o, do_o, d2_o, rx1_o):
        x, y, c = _position()
        for j in range(4):
            d = pltpu.make_async_remote_copy(src_ref=rx1_ref.at[j], dst_ref=rx1_ref.at[j], send_sem=sa_ref.at[j],
                                             recv_sem=ra_ref.at[j], device_id=(x, y, c), device_id_type=MESH)
            d.wait_recv()
            d.wait_send()

    arrs = (dwint, dwout, dw2, rx1)
    return pl.pallas_call(
        body, name="rsa_level1_wait", out_shape=tuple(_hbm_like(a) for a in arrs),
        in_specs=(HBM_SPEC,) * 4 + (SEM_SPEC, SEM_SPEC, ANY_SPEC), out_specs=(HBM_SPEC,) * 4,
        input_output_aliases={0: 0, 1: 1, 2: 2, 3: 3},
        compiler_params=pltpu.CompilerParams(has_side_effects=SPLIT_EFFECT),
    )(*arrs, sa, ra, after)


def _rsa_chip_sums(dwint, dwout, dw2, rx1):
    nblk = RSA_ROWS // RSA_BLOCK

    def body(di_ref, do_ref, d2_ref, rx1_ref, tx_ref, acc_ref, own_buf, rx_buf, tx_buf, acc_buf, lsems):
        x, y, c = _position()
        srcs = _weight_pieces(wi_ref=di_ref, wo_ref=do_ref, w2_ref=d2_ref)
        for j, chip in enumerate([(x, y), (1 - x, y), (x, 1 - y), (1 - x, 1 - y)]):
            loads = [pltpu.make_async_copy(_block_rows(srcs, k, (*chip, c)), _rsa_rows(own_buf, k), lsems.at[0])
                     for k in RSA_PIECES]
            got = pltpu.make_async_copy(rx1_ref.at[j], rx_buf, lsems.at[1])
            for cp in loads + [got]:
                cp.start()
            pltpu.make_async_copy(rx_buf, rx_buf, lsems.at[0]).wait()
            got.wait()

            def add(i, carry, j=j):
                rows = pl.ds(pl.multiple_of(i * RSA_BLOCK, 16), RSA_BLOCK)
                tot = own_buf[rows, :].astype(f32) + rx_buf[rows, :].astype(f32)
                if j == 0:
                    acc_buf[rows, :] = tot
                else:
                    tx_buf[rows, :] = tot.astype(bf16)
                return carry

            lax.fori_loop(0, nblk, add, 0)
            out = (pltpu.make_async_copy(acc_buf, acc_ref, lsems.at[2]) if j == 0
                   else pltpu.make_async_copy(tx_buf, tx_ref.at[j - 1], lsems.at[2]))
            out.start()
            out.wait()

    return pl.pallas_call(
        body, in_specs=[ANY_SPEC] * 4, out_specs=[ANY_SPEC] * 2,
        out_shape=(SDS((3, RSA_ROWS, D), bf16), SDS((RSA_ROWS, D), f32)),
        scratch_shapes=[pltpu.VMEM((RSA_ROWS, D), bf16), pltpu.VMEM((RSA_ROWS, D), bf16),
                        pltpu.VMEM((RSA_ROWS, D), bf16), pltpu.VMEM((RSA_ROWS, D), f32),
                        pltpu.SemaphoreType.DMA((3,))],
        compiler_params=_cparams(None, VMEM_LIMIT_V7X), name="rsa_chip_sums")(dwint, dwout, dw2, rx1)


def _rsa_level2_start(tx, rx2, thru):
    def body(tx_ref, rx2_ref, thru_ref, sb, rb, tx_o, rx2_o, thru_o):
        x, y, c = _position()
        for j, chip in enumerate([(1 - x, y), (x, 1 - y), (1 - x, 1 - y)]):
            pltpu.make_async_remote_copy(src_ref=tx_ref.at[j], dst_ref=rx2_ref.at[j], send_sem=sb.at[j],
                                         recv_sem=rb.at[j], device_id=(*chip, c), device_id_type=MESH).start()

    dma = pltpu.SemaphoreType.DMA
    arrs = (tx, rx2, thru)
    return pl.pallas_call(
        body, name="rsa_level2_start", out_shape=(dma((3,)), dma((3,))) + tuple(_hbm_like(a) for a in arrs),
        in_specs=(HBM_SPEC,) * 3, out_specs=(SEM_SPEC,) * 2 + (HBM_SPEC,) * 3,
        input_output_aliases={0: 2, 1: 3, 2: 4},
        compiler_params=pltpu.CompilerParams(has_side_effects=SPLIT_EFFECT),
    )(*[_in_hbm(a) for a in arrs])


def _rsa_level2_wait(sb, rb, tx, rx2, after):
    def body(tx_ref, rx2_ref, sb_ref, rb_ref, after_ref, rx2_o):
        x, y, c = _position()
        for j in range(3):
            d = pltpu.make_async_remote_copy(src_ref=tx_ref.at[j], dst_ref=rx2_ref.at[j], send_sem=sb_ref.at[j],
                                             recv_sem=rb_ref.at[j], device_id=(x, y, c), device_id_type=MESH)
            d.wait_recv()
            d.wait_send()

    return pl.pallas_call(
        body, name="rsa_level2_wait", out_shape=_hbm_like(rx2),
        in_specs=(HBM_SPEC, HBM_SPEC, SEM_SPEC, SEM_SPEC, ANY_SPEC), out_specs=HBM_SPEC,
        input_output_aliases={1: 0},
        compiler_params=pltpu.CompilerParams(has_side_effects=SPLIT_EFFECT),
    )(tx, rx2, sb, rb, after)


def _rsa_total(acc, rx2):
    def body(a_ref, r_ref, o_ref):
        o_ref[...] = ((a_ref[...] + r_ref[0].astype(f32)) + r_ref[1].astype(f32)) + r_ref[2].astype(f32)

    return pl.pallas_call(
        body, grid=(RSA_ROWS // RSA_BLOCK,),
        in_specs=[pl.BlockSpec((RSA_BLOCK, D), lambda i: (i, 0)), pl.BlockSpec((3, RSA_BLOCK, D), lambda i: (0, i, 0))],
        out_specs=pl.BlockSpec((RSA_BLOCK, D), lambda i: (i, 0)),
        out_shape=SDS((RSA_ROWS, D), f32), name="rsa_total")(acc, rx2)


def _adamw_math(w, g, m, v):
    m = ADAM_B1 * m + (1.0 - ADAM_B1) * g
    v = ADAM_B2 * v + (1.0 - ADAM_B2) * (g * g)
    m_hat = m / (1.0 - ADAM_B1 ** ADAM_STEP)
    v_hat = v / (1.0 - ADAM_B2 ** ADAM_STEP)
    delta = -ADAM_LR * (m_hat / (jnp.sqrt(v_hat) + ADAM_EPS) + ADAM_WD * w)
    return delta, m, v


def _adamw_big(ws, ms, vs, red1, red_rest):
    npiece = len(BIG)
    rmax = max(PIECE_ROWS)

    def body(*refs):
        ins = (refs[0:npiece], refs[npiece:2 * npiece], refs[2 * npiece:3 * npiece])
        red1_ref, rest_ref = refs[3 * npiece:3 * npiece + 2]
        out_refs = refs[3 * npiece + 2:7 * npiece + 2]
        inb, outb, in_sems, out_sems = refs[7 * npiece + 2:]

        def grad_rows(k):
            if k in G1_PIECES:
                return red1_ref.at[pl.ds(PIECE_OFF[k], PIECE_ROWS[k]), :]
            return _rsa_rows(rest_ref, k)

        def loads(k):
            s, r = k % 2, PIECE_ROWS[k]
            cps = [pltpu.make_async_copy(ins[q][k].at[0], inb.at[s, q, pl.ds(0, r), :], in_sems.at[4 * s + q])
                   for q in range(3)]
            cps.append(pltpu.make_async_copy(grad_rows(k), inb.at[s, 3, pl.ds(0, r), :], in_sems.at[4 * s + 3]))
            return cps

        def stores(k):
            s, r = k % 2, PIECE_ROWS[k]
            return [pltpu.make_async_copy(outb.at[s, q, pl.ds(0, r), :], out_refs[q * npiece + k].at[0],
                                          out_sems.at[4 * s + q]) for q in range(4)]

        for cp in loads(0):
            cp.start()
        for k in range(npiece):
            s, r = k % 2, PIECE_ROWS[k]
            if k + 1 < npiece:
                for cp in loads(k + 1):
                    cp.start()
            for cp in loads(k):
                cp.wait()
            if k >= 2:
                for cp in stores(k - 2):
                    cp.wait()
            g = inb[s, 3, 0:r, :]
            d, nm, nv = _adamw_math(inb[s, 0, 0:r, :], g, inb[s, 1, 0:r, :], inb[s, 2, 0:r, :])
            outb[s, 0, 0:r, :] = g
            outb[s, 1, 0:r, :] = d
            outb[s, 2, 0:r, :] = nm
            outb[s, 3, 0:r, :] = nv
            for cp in stores(k):
                cp.start()
        for k in (npiece - 2, npiece - 1):
            for cp in stores(k):
                cp.wait()

    hbm = pl.BlockSpec(memory_space=pl.ANY)
    outs = pl.pallas_call(
        body, in_specs=[hbm] * (3 * npiece + 2), out_specs=[hbm] * (4 * npiece),
        out_shape=tuple(SDS(w.shape, f32) for _ in range(4) for w in ws),
        scratch_shapes=[pltpu.VMEM((2, 4, rmax, D), f32), pltpu.VMEM((2, 4, rmax, D), f32),
                        pltpu.SemaphoreType.DMA((8,)), pltpu.SemaphoreType.DMA((8,))],
        compiler_params=_cparams(None, VMEM_LIMIT_V7X), name="adamw_big")(*ws, *ms, *vs, red1, red_rest)
    return [list(outs[q * npiece:(q + 1) * npiece]) for q in range(4)]


def _adamw_small(w, m, v, g, name):
    def body(w_ref, m_ref, v_ref, g_ref, d_ref, nm_ref, nv_ref):
        d, nm, nv = _adamw_math(w_ref[...], g_ref[...], m_ref[...], v_ref[...])
        d_ref[...] = d
        nm_ref[...] = nm
        nv_ref[...] = nv

    return pl.pallas_call(
        body, out_shape=tuple(SDS(w.shape, f32) for _ in range(3)), name=name)(w, m, v, g)


WEIGHTS = ("ffn1_norm", "ffn1_w_gate", "ffn1_w_up", "ffn1_w_down", "mix_norm", "w_in", "q_norm", "k_norm",
           "attn_sinks", "rel_bias", "pool_w", "pool_scale", "w_out", "ffn2_norm", "ffn2_w_gate", "ffn2_w_up",
           "ffn2_w_down")
BIG = (("ffn1_w_gate", True), ("ffn1_w_up", True), ("ffn1_w_down", False), ("w_in", True), ("w_out", False),
       ("ffn2_w_gate", True), ("ffn2_w_up", True), ("ffn2_w_down", False))


def kernel(x, ffn1_norm, ffn1_w_gate, ffn1_w_up, ffn1_w_down, mix_norm, w_in, q_norm, k_norm, attn_sinks, rel_bias, pool_w, pool_scale, w_out, ffn2_norm, ffn2_w_gate, ffn2_w_up, ffn2_w_down, loss_target, m_ffn1_norm, m_ffn1_w_gate, m_ffn1_w_up, m_ffn1_w_down, m_mix_norm, m_w_in, m_q_norm, m_k_norm, m_attn_sinks, m_rel_bias, m_pool_w, m_pool_scale, m_w_out, m_ffn2_norm, m_ffn2_w_gate, m_ffn2_w_up, m_ffn2_w_down, v_ffn1_norm, v_ffn1_w_gate, v_ffn1_w_up, v_ffn1_w_down, v_mix_norm, v_w_in, v_q_norm, v_k_norm, v_attn_sinks, v_rel_bias, v_pool_w, v_pool_scale, v_w_out, v_ffn2_norm, v_ffn2_w_gate, v_ffn2_w_up, v_ffn2_w_down):
    args = dict(locals())
    w = {n: args[n] for n in WEIGHTS}
    m = {n: args["m_" + n] for n in WEIGHTS}
    v = {n: args["v_" + n] for n in WEIGHTS}

    as_rows = lambda a, tr: jnp.swapaxes(a, 1, 2) if tr else a
    shard = jnp.concatenate([as_rows(w[n], tr)[0].astype(bf16) for n, tr in BIG], axis=0)
    exchanges = _GatheredWeights(shard)
    gx, (dw1, _, _, _), small = _local_step(
        x[0], loss_target[0], exchanges, ffn1_norm, mix_norm, ffn2_norm, q_norm, k_norm, attn_sinks,
        rel_bias, pool_w[0], pool_scale)

    red1, small_tot = _reduce_scatter_ffn1(dw1, _pack_small(small))
    red_rest = exchanges.mix_ffn2_grads_total(red1)

    grads, deltas, new_m, new_v = {}, {}, {}, {}
    big_out = _adamw_big(*[[as_rows(t[n], tr) for n, tr in BIG] for t in (w, m, v)], red1, red_rest)
    for k, (n, tr) in enumerate(BIG):
        grads[n], deltas[n], new_m[n], new_v[n] = [as_rows(o[k], tr) for o in big_out]
    small_names = [n for n in SMALL_NAMES if n != "loss"]
    ds, nms, nvs = _adamw_small(_pack_small({n: w[n] for n in small_names}), _pack_small({n: m[n] for n in small_names}),
                                _pack_small({n: v[n] for n in small_names}), small_tot, "adamw_small")
    for n in small_names:
        grads[n] = _unpack_small(small_tot, n)
        deltas[n], new_m[n], new_v[n] = _unpack_small(ds, n), _unpack_small(nms, n), _unpack_small(nvs, n)
    loss = small_tot[LOSS_ROW, 0]
    return (loss, gx[None], *[grads[n] for n in WEIGHTS], *[deltas[n] for n in WEIGHTS],
            *[new_m[n] for n in WEIGHTS], *[new_v[n] for n in WEIGHTS])
```

```python
import functools

import jax
import jax.numpy as jnp
import numpy as np
from jax import lax
from jax.experimental import pallas as pl
from jax.experimental.pallas import tpu as pltpu

f32, bf16, i32 = jnp.float32, jnp.bfloat16, jnp.int32
SDS = jax.ShapeDtypeStruct

D = 1024
F = 2816
HD = 64
NH = 8
NKV = 2
GQA = NH // NKV
DATTN = NH * HD
DKV = NKV * HD
DPOOL = 512
POOL_WINDOWS = (2, 4, 8, 16)
PGD = DPOOL // len(POOL_WINDOWS)
DIN = DATTN + 2 * DKV + DPOOL
DMIX = DATTN + DPOOL
BLK = 128
NBUCK = 32
MAX_DISTANCE = 128
EPS = 1e-6
NEG = -1e30
SCALE = HD ** -0.5

ADAM_LR, ADAM_B1, ADAM_B2, ADAM_EPS, ADAM_WD, ADAM_STEP = 0.001, 0.9, 0.999, 1e-08, 0.01, 10

NDEV = 8
FS = F // NDEV
INS = DIN // NDEV
OUTS = DMIX // NDEV
PIECE_ROWS = (FS, FS, FS, INS, OUTS, FS, FS, FS)
PIECE_OFF = tuple(int(v) for v in np.cumsum((0,) + PIECE_ROWS[:-1]))
PACK_ROWS = sum(PIECE_ROWS)

VMEM_LIMIT_V7X = 56 * 1024 * 1024

MESH = pl.DeviceIdType.MESH


def _cparams(sem=None, vmem=None):
    return pltpu.CompilerParams(dimension_semantics=sem, vmem_limit_bytes=vmem)


def _nt(a, b):
    return lax.dot_general(a, b, (((1,), (1,)), ((), ())), preferred_element_type=f32)


def _tn(a, b):
    return lax.dot_general(a, b, (((0,), (0,)), ((), ())), preferred_element_type=f32)


def _nn(a, b):
    return jnp.dot(a, b, preferred_element_type=f32)


def _sigmoid(x):
    return 1.0 / (1.0 + jnp.exp(-x))


def _norm_fwd(x, g, name):
    T = x.shape[0]
    tm = min(512, T)

    def body(x_ref, g_ref, h_ref):
        xv = x_ref[...]
        r = lax.rsqrt(jnp.mean(xv * xv, axis=-1, keepdims=True) + EPS)
        h_ref[...] = (xv * r * g_ref[...]).astype(bf16)

    return pl.pallas_call(
        body, grid=(T // tm,),
        in_specs=[pl.BlockSpec((tm, D), lambda i: (i, 0)), pl.BlockSpec((1, D), lambda i: (0, 0))],
        out_specs=pl.BlockSpec((tm, D), lambda i: (i, 0)),
        out_shape=SDS((T, D), bf16), name=name)(x, g)


def _norm_bwd(dh, x, g, dres, out_scale, name):
    T = x.shape[0]
    tm = min(512, T)

    def body(dh_ref, x_ref, g_ref, dr_ref, dx_ref, dxb_ref, dg_ref):
        i = pl.program_id(0)
        xv = x_ref[...]
        r = lax.rsqrt(jnp.mean(xv * xv, axis=-1, keepdims=True) + EPS)
        xh = xv * r
        dhv = dh_ref[...]
        dxh = dhv * g_ref[...]
        dx = dr_ref[...] + r * (dxh - xh * jnp.mean(dxh * xh, axis=-1, keepdims=True))
        dx_ref[...] = dx
        dxb_ref[...] = (out_scale * dx).astype(bf16)
        dg = jnp.sum(dhv * xh, axis=0, keepdims=True)

        @pl.when(i == 0)
        def _():
            dg_ref[...] = dg

        @pl.when(i > 0)
        def _():
            dg_ref[...] += dg

    tok = pl.BlockSpec((tm, D), lambda i: (i, 0))
    vec = pl.BlockSpec((1, D), lambda i: (0, 0))
    return pl.pallas_call(
        body, grid=(T // tm,),
        in_specs=[tok, tok, vec, tok], out_specs=[tok, tok, vec],
        out_shape=(SDS((T, D), f32), SDS((T, D), bf16), SDS((1, D), f32)),
        compiler_params=_cparams(("arbitrary",)), name=name)(dh, x, g, dres)


FFN_ROW_CHUNK = 256


def _ffn_tiles(T):
    return min(1024, T), 256


def _ffn_fwd(h, w, x, target, name):
    T = h.shape[0]
    tm, tf = _ffn_tiles(T)
    nf = F // tf
    with_loss = target is not None

    def body(*refs):
        if with_loss:
            h_ref, w_ref, x_hbm, t_hbm, xo_ref, g_ref, u_ref, dyb_ref, loss_ref, tbuf, sem = refs
        else:
            h_ref, w_ref, x_hbm, xo_ref, g_ref, u_ref, sem = refs
        fi = pl.program_id(0)

        @pl.when(fi == 0)
        def _():
            cp = pltpu.make_async_copy(x_hbm, xo_ref, sem)
            cp.start()
            cp.wait()

        wgu = w_ref[0:2].reshape(2 * tf, D)
        for r in range(0, T, tm):
            rows = slice(r, r + tm)
            gu = _nt(h_ref[rows, :], wgu)
            gate, up = gu[:, :tf], gu[:, tf:]
            act = gate * _sigmoid(gate) * up
            g_ref[0, rows, :] = gate.astype(bf16)
            u_ref[0, rows, :] = up.astype(bf16)
            xo_ref[rows, :] += _nn((0.5 * act).astype(bf16), w_ref[2])

        if with_loss:
            @pl.when(fi == nf - 1)
            def _():
                lanes = jnp.zeros((1, 128), f32)
                for r in range(0, T, tm):
                    rows = slice(r, r + tm)
                    cp = pltpu.make_async_copy(t_hbm.at[pl.ds(r, tm), :], tbuf, sem)
                    cp.start()
                    cp.wait()
                    e = xo_ref[rows, :] - tbuf[...]
                    dy = e * (1.0 / D)
                    xo_ref[rows, :] = dy
                    dyb_ref[rows, :] = (0.5 * dy).astype(bf16)
                    col = jnp.sum(e * e, axis=0, keepdims=True) * (0.5 / D)
                    for k in range(D // 128):
                        lanes = lanes + col[:, 128 * k:128 * (k + 1)]
                loss_ref[...] = lanes

    tok = pl.BlockSpec((T, D), lambda f: (0, 0))
    act_spec = pl.BlockSpec((1, T, tf), lambda f: (f, 0, 0))
    hbm = pl.BlockSpec(memory_space=pl.ANY)
    in_specs = [tok, pl.BlockSpec((3, tf, D), lambda f: (0, f, 0)), hbm]
    out_specs = [tok, act_spec, act_spec]
    out_shape = [SDS((T, D), f32), SDS((nf, T, tf), bf16), SDS((nf, T, tf), bf16)]
    scratch = [pltpu.SemaphoreType.DMA]
    args = [h, w, x]
    if with_loss:
        in_specs.append(hbm)
        args.append(target)
        out_specs += [tok, pl.BlockSpec((1, 128), lambda f: (0, 0))]
        out_shape += [SDS((T, D), bf16), SDS((1, 128), f32)]
        scratch = [pltpu.VMEM((tm, D), f32)] + scratch
    return pl.pallas_call(
        body, grid=(nf,), in_specs=in_specs, out_specs=out_specs, out_shape=tuple(out_shape), scratch_shapes=scratch,
        compiler_params=_cparams(("arbitrary",), VMEM_LIMIT_V7X), name=name)(*args)


def _ffn_bwd(dob, h, gate, up, w, name):
    T = h.shape[0]
    _, tf = _ffn_tiles(T)
    nf = F // tf

    def body(do_hbm, h_hbm, g_ref, u_ref, w_ref, dh_hbm, dw_ref, do_v, h_v, dh_acc, dgu_s, act_s, sems):
        fi = pl.program_id(0)

        @pl.when(fi == 0)
        def _():
            loads = [pltpu.make_async_copy(do_hbm, do_v, sems.at[0]), pltpu.make_async_copy(h_hbm, h_v, sems.at[1])]
            for cp in loads:
                cp.start()
            dh_acc[...] = jnp.zeros_like(dh_acc)
            for cp in loads:
                cp.wait()

        wgu = w_ref[0:2].reshape(2 * tf, D)
        for r in range(0, T, FFN_ROW_CHUNK):
            rows = slice(r, r + FFN_ROW_CHUNK)
            dov = do_v[rows, :]
            gv = g_ref[0, rows, :].astype(f32)
            uv = u_ref[0, rows, :].astype(f32)
            sg = _sigmoid(gv)
            sil = gv * sg
            dact = _nt(dov, w_ref[2])
            dup = dact * sil
            dgate = dact * uv * (sg * (1.0 + gv * (1.0 - sg)))
            dgu = jnp.concatenate([dgate.astype(bf16), dup.astype(bf16)], axis=1)
            dgu_s[rows, :] = dgu
            act_s[rows, :] = (sil * uv).astype(bf16)
            dh_acc[rows, :] += _nn(dgu, wgu)
        dw_ref[0:2] = _tn(dgu_s[...], h_v[...]).reshape(2, tf, D).astype(bf16)
        dw_ref[2] = _tn(act_s[...], do_v[...]).astype(bf16)

        @pl.when(fi == nf - 1)
        def _():
            out = pltpu.make_async_copy(dh_acc, dh_hbm, sems.at[0])
            out.start()
            out.wait()

    act_spec = pl.BlockSpec((1, T, tf), lambda f: (f, 0, 0))
    wspec = pl.BlockSpec((3, tf, D), lambda f: (0, f, 0))
    hbm = pl.BlockSpec(memory_space=pl.ANY)
    return pl.pallas_call(
        body, grid=(nf,),
        in_specs=[hbm, hbm, act_spec, act_spec, wspec],
        out_specs=[hbm, wspec],
        out_shape=(SDS((T, D), f32), SDS((3, F, D), bf16)),
        scratch_shapes=[pltpu.VMEM((T, D), bf16), pltpu.VMEM((T, D), bf16), pltpu.VMEM((T, D), f32),
                        pltpu.VMEM((T, 2 * tf), bf16), pltpu.VMEM((T, tf), bf16), pltpu.SemaphoreType.DMA((2,))],
        compiler_params=_cparams(("arbitrary",), VMEM_LIMIT_V7X), name=name)(dob, h, gate, up, w)


def _in_proj_fwd(h, wint, name):
    T = h.shape[0]
    tm = min(512, T)

    def body(h_ref, w_ref, z_ref):
        z_ref[...] = _nt(h_ref[...], w_ref[...])

    return pl.pallas_call(
        body, grid=(T // tm,),
        in_specs=[pl.BlockSpec((tm, D), lambda i: (i, 0)), pl.BlockSpec((DIN, D), lambda i: (0, 0))],
        out_specs=pl.BlockSpec((tm, DIN), lambda i: (i, 0)),
        out_shape=SDS((T, DIN), f32), name=name)(h, wint)


def _in_proj_bwd(dz, wint, h, name):
    T = h.shape[0]
    tm = min(512, T)
    nt = T // tm

    def body(dz_ref, w_ref, h_ref, dh_ref, dw_ref, acc):
        i = pl.program_id(0)
        dzb = dz_ref[...].astype(bf16)
        dh_ref[...] = _nn(dzb, w_ref[...])
        part = _tn(dzb, h_ref[...])

        @pl.when(i == 0)
        def _():
            acc[...] = part

        @pl.when(i > 0)
        def _():
            acc[...] += part

        @pl.when(i == nt - 1)
        def _():
            dw_ref[...] = acc[...].astype(bf16)

    wspec = pl.BlockSpec((DIN, D), lambda i: (0, 0))
    return pl.pallas_call(
        body, grid=(nt,),
        in_specs=[pl.BlockSpec((tm, DIN), lambda i: (i, 0)), wspec, pl.BlockSpec((tm, D), lambda i: (i, 0))],
        out_specs=[pl.BlockSpec((tm, D), lambda i: (i, 0)), wspec],
        out_shape=(SDS((T, D), f32), SDS((DIN, D), bf16)),
        scratch_shapes=[pltpu.VMEM((DIN, D), f32)],
        compiler_params=_cparams(("arbitrary",)), name=name)(dz, wint, h)


def _out_proj_fwd(ymix, wout, x, g, name):
    T = x.shape[0]
    tm = min(512, T)

    def body(y_ref, w_ref, x_ref, g_ref, o_ref, h_ref):
        o = x_ref[...] + _nn(y_ref[...], w_ref[...])
        o_ref[...] = o
        r = lax.rsqrt(jnp.mean(o * o, axis=-1, keepdims=True) + EPS)
        h_ref[...] = (o * r * g_ref[...]).astype(bf16)

    tok = pl.BlockSpec((tm, D), lambda i: (i, 0))
    return pl.pallas_call(
        body, grid=(T // tm,),
        in_specs=[pl.BlockSpec((tm, DMIX), lambda i: (i, 0)), pl.BlockSpec((DMIX, D), lambda i: (0, 0)), tok,
                  pl.BlockSpec((1, D), lambda i: (0, 0))],
        out_specs=[tok, tok], out_shape=(SDS((T, D), f32), SDS((T, D), bf16)), name=name)(ymix, wout, x, g)


def _out_proj_bwd(dxb, wout, ymix, name):
    T = dxb.shape[0]
    tm = min(512, T)
    nt = T // tm

    def body(dx_ref, w_ref, y_ref, dy_ref, dw_ref, acc):
        i = pl.program_id(0)
        dxv = dx_ref[...]
        dy_ref[...] = _nt(dxv, w_ref[...])
        part = _tn(y_ref[...], dxv)

        @pl.when(i == 0)
        def _():
            acc[...] = part

        @pl.when(i > 0)
        def _():
            acc[...] += part

        @pl.when(i == nt - 1)
        def _():
            dw_ref[...] = acc[...].astype(bf16)

    wspec = pl.BlockSpec((DMIX, D), lambda i: (0, 0))
    return pl.pallas_call(
        body, grid=(nt,),
        in_specs=[pl.BlockSpec((tm, D), lambda i: (i, 0)), wspec, pl.BlockSpec((tm, DMIX), lambda i: (i, 0))],
        out_specs=[pl.BlockSpec((tm, DMIX), lambda i: (i, 0)), wspec],
        out_shape=(SDS((T, DMIX), f32), SDS((DMIX, D), bf16)),
        scratch_shapes=[pltpu.VMEM((DMIX, D), f32)],
        compiler_params=_cparams(("arbitrary",)), name=name)(dxb, wout, ymix)


def _t5_bucket_table():
    ql = np.arange(BLK)[:, None]
    kl = np.arange(2 * BLK)[None, :]
    n = np.maximum(ql + BLK - kl, 0)
    max_exact = NBUCK // 2
    large = max_exact + (np.log(np.maximum(n, 1) / max_exact) / np.log(MAX_DISTANCE / max_exact)
                         * (NBUCK - max_exact)).astype(np.int32)
    large = np.minimum(large, NBUCK - 1)
    return np.where(n < max_exact, n, large).astype(np.int32)


def _fill_bias(bk_ref, rb_ref, bias_scr):
    bk = bk_ref[...]
    for h in range(NH):
        def step(b, acc, h=h):
            return acc + jnp.where(bk == b, rb_ref[b, h], 0.0)
        bias_scr[h] = lax.fori_loop(0, NBUCK, step, jnp.zeros((BLK, 2 * BLK), f32))


MIX_SUB = 4


class _Window:
    def __init__(self, zc_ref, zp_ref, n, s):
        self.blk = n * MIX_SUB + s
        self.first_in_step = s == 0
        self.cur = lambda a, b: zc_ref[s * BLK:(s + 1) * BLK, a:b]
        self.prev = (lambda a, b: zp_ref[:, a:b]) if s == 0 else (lambda a, b: zc_ref[(s - 1) * BLK:s * BLK, a:b])


def _attn_qkv(win, kh, qg, kg):
    kc = DATTN + HD * kh
    vc = DATTN + DKV + HD * kh
    kx = jnp.concatenate([win.prev(kc, kc + HD), win.cur(kc, kc + HD)], axis=0)
    vx = jnp.concatenate([win.prev(vc, vc + HD), win.cur(vc, vc + HD)], axis=0)
    qx = jnp.concatenate([win.cur(HD * (GQA * kh + g), HD * (GQA * kh + g + 1)) for g in range(GQA)], axis=0)
    rq = lax.rsqrt(jnp.mean(qx * qx, axis=-1, keepdims=True) + EPS)
    rk = lax.rsqrt(jnp.mean(kx * kx, axis=-1, keepdims=True) + EPS)
    qhat, khat = qx * rq, kx * rk
    return dict(qhat=qhat, khat=khat, rq=rq, rk=rk, qnb=(qhat * qg).astype(bf16), knb=(khat * kg).astype(bf16),
                vb=vx.astype(bf16))


def _attn_probs(a, kh, sk_ref, bias_scr, n):
    s = _nt(a["qnb"], a["knb"]) * SCALE + bias_scr[GQA * kh:GQA * (kh + 1)].reshape(GQA * BLK, 2 * BLK)
    row = lax.broadcasted_iota(i32, (GQA * BLK, 2 * BLK), 0) & (BLK - 1)
    col = lax.broadcasted_iota(i32, (GQA * BLK, 2 * BLK), 1)
    mask = (col > row) & (col <= row + BLK) & ((col >= BLK) | (n > 0))
    s = jnp.where(mask, s, NEG)
    ridx = lax.broadcasted_iota(i32, (GQA * BLK, 1), 0)
    sink = jnp.full((GQA * BLK, 1), sk_ref[GQA * kh + GQA - 1], f32)
    for g in range(GQA - 2, -1, -1):
        sink = jnp.where(ridx < (g + 1) * BLK, sk_ref[GQA * kh + g], sink)
    m = jnp.maximum(jnp.max(s, axis=-1, keepdims=True), sink)
    e = jnp.exp(s - m)
    den = jnp.sum(e, axis=-1, keepdims=True) + jnp.exp(sink - m)
    return e / den


POOL_STEPS = {2: (1,), 4: (1, 2), 8: (1, 2, 4), 16: (1, 2, 4, 8)}


def _pool_group(win, g, w):
    n = win.blk
    c0 = DATTN + 2 * DKV + PGD * g
    uc = win.cur(c0, c0 + PGD)
    up = jnp.where(n > 0, win.prev(c0, c0 + PGD), 0.0)
    sm = jnp.concatenate([up, uc], axis=0)
    for k in POOL_STEPS[w]:
        sm = sm + pltpu.roll(sm, k, axis=0)
    pos = n * BLK + lax.broadcasted_iota(i32, (BLK, 1), 0) + 1
    cnt = jnp.minimum(pos, w).astype(f32)
    return sm[BLK:2 * BLK] / cnt - uc, cnt


def _mix_fwd(z, qg, kg, sinks, relb, bucket, pool_w, pscale, name):
    T = z.shape[0]
    step_rows = MIX_SUB * BLK
    nsteps = T // step_rows

    def body(zc_ref, zp_ref, qg_ref, kg_ref, sk_ref, rb_ref, bk_ref, pw_ref, ps_ref, y_ref, p_ref, bias_scr, yacc):
        n = pl.program_id(0)

        @pl.when(n == 0)
        def _():
            _fill_bias(bk_ref, rb_ref, bias_scr)

        for s in range(MIX_SUB):
            win = _Window(zc_ref, zp_ref, n, s)
            rows = slice(s * BLK, (s + 1) * BLK)
            for kh in range(NKV):
                a = _attn_qkv(win, kh, qg_ref[...], kg_ref[...])
                pb = _attn_probs(a, kh, sk_ref, bias_scr, win.blk).astype(bf16)
                p_ref[s, GQA * kh:GQA * (kh + 1)] = pb.reshape(GQA, BLK, 2 * BLK)
                o = _nn(pb, a["vb"])
                for g in range(GQA):
                    hc = HD * (GQA * kh + g)
                    yacc[rows, hc:hc + HD] = o[g * BLK:(g + 1) * BLK]
            for g, w in enumerate(POOL_WINDOWS):
                pooled, _ = _pool_group(win, g, w)
                yp = _nn(pooled.astype(bf16), pw_ref[g].astype(bf16)) * ps_ref[:, PGD * g:PGD * (g + 1)]
                yacc[rows, DATTN + PGD * g:DATTN + PGD * (g + 1)] = yp
        y_ref[...] = yacc[...].astype(bf16)

    full = lambda *shape: pl.BlockSpec(shape, lambda n: (0,) * len(shape))
    smem = pl.BlockSpec(memory_space=pltpu.SMEM)
    return pl.pallas_call(
        body, grid=(nsteps,),
        in_specs=[pl.BlockSpec((step_rows, DIN), lambda n: (n, 0)),
                  pl.BlockSpec((BLK, DIN), lambda n: (jnp.maximum(n * MIX_SUB - 1, 0), 0)),
                  full(1, HD), full(1, HD), smem, smem, full(BLK, 2 * BLK),
                  full(len(POOL_WINDOWS), PGD, PGD), full(1, DPOOL)],
        out_specs=[pl.BlockSpec((step_rows, DMIX), lambda n: (n, 0)),
                   pl.BlockSpec((MIX_SUB, NH, BLK, 2 * BLK), lambda n: (n, 0, 0, 0))],
        out_shape=(SDS((T, DMIX), bf16), SDS((T // BLK, NH, BLK, 2 * BLK), bf16)),
        scratch_shapes=[pltpu.VMEM((NH, BLK, 2 * BLK), f32), pltpu.VMEM((step_rows, DMIX), f32)],
        compiler_params=_cparams(("arbitrary",)), name=name)(z, z, qg, kg, sinks, relb, bucket, pool_w, pscale)


def _mix_bwd(z, dy, probs, qg, kg, relb, bucket, pool_w, pscale, name):
    T = z.shape[0]
    step_rows = MIX_SUB * BLK
    nsteps = T // step_rows

    def body(zc_ref, zp_ref, dy_ref, p_ref, qg_ref, kg_ref, bk_ref, pw_ref, ps_ref,
             dz_ref, dqg_ref, dkg_ref, dsk_ref, drb_ref, dpw_ref, dps_ref, dbias_scr):
        n = pl.program_id(0)

        @pl.when(n == 0)
        def _():
            dbias_scr[...] = jnp.zeros_like(dbias_scr)
            dqg_ref[...] = jnp.zeros_like(dqg_ref)
            dkg_ref[...] = jnp.zeros_like(dkg_ref)
            dpw_ref[...] = jnp.zeros_like(dpw_ref)
            dps_ref[...] = jnp.zeros_like(dps_ref)

        qg, kg = qg_ref[...], kg_ref[...]
        for s in range(MIX_SUB):
            win = _Window(zc_ref, zp_ref, n, s)
            blk = win.blk
            rows = pl.ds(pl.multiple_of(blk * BLK, BLK), BLK)
            prow = pl.ds(pl.multiple_of(jnp.maximum(blk - 1, 0) * BLK, BLK), BLK)
            dyr = slice(s * BLK, (s + 1) * BLK)

            def into_prev(fn, s=s):
                if s == 0:
                    pl.when(n > 0)(fn)
                else:
                    fn()

            for kh in range(NKV):
                a = _attn_qkv(win, kh, qg, kg)
                pb = p_ref[s, GQA * kh:GQA * (kh + 1)].reshape(GQA * BLK, 2 * BLK)
                p = pb.astype(f32)
                do = jnp.concatenate([dy_ref[dyr, HD * (GQA * kh + g):HD * (GQA * kh + g + 1)] for g in range(GQA)],
                                     axis=0).astype(bf16)
                dv = _tn(pb, do)
                dp = _nt(do, a["vb"])
                delta = jnp.sum(p * dp, axis=-1, keepdims=True)
                ds = p * (dp - delta)
                for g in range(GQA):
                    dbias_scr[GQA * kh + g] += ds[g * BLK:(g + 1) * BLK]
                dsb = ds.astype(bf16)
                dqn = _nn(dsb, a["knb"]) * SCALE
                dkn = _tn(dsb, a["qnb"]) * SCALE
                qhat, khat = a["qhat"], a["khat"]
                dqg_ref[...] += jnp.sum(dqn * qhat, axis=0, keepdims=True)
                dkg_ref[...] += jnp.sum(dkn * khat, axis=0, keepdims=True)
                dqh = dqn * qg
                dq = a["rq"] * (dqh - qhat * jnp.mean(dqh * qhat, axis=-1, keepdims=True))
                dkh = dkn * kg
                dk = a["rk"] * (dkh - khat * jnp.mean(dkh * khat, axis=-1, keepdims=True))
                kc = DATTN + HD * kh
                vc = DATTN + DKV + HD * kh
                for g in range(GQA):
                    hc = HD * (GQA * kh + g)
                    dz_ref[rows, hc:hc + HD] = dq[g * BLK:(g + 1) * BLK]
                dz_ref[rows, kc:kc + HD] = dk[BLK:2 * BLK]
                dz_ref[rows, vc:vc + HD] = dv[BLK:2 * BLK]

                def kv_prev(dk=dk, dv=dv, kc=kc, vc=vc, prow=prow):
                    dz_ref[prow, kc:kc + HD] += dk[0:BLK]
                    dz_ref[prow, vc:vc + HD] += dv[0:BLK]

                into_prev(kv_prev)

            for g, w in enumerate(POOL_WINDOWS):
                c0 = DATTN + 2 * DKV + PGD * g
                pooled, cnt = _pool_group(win, g, w)
                pb = pooled.astype(bf16)
                wb = pw_ref[g].astype(bf16)
                dyp = dy_ref[dyr, DATTN + PGD * g:DATTN + PGD * (g + 1)]
                ypre = _nn(pb, wb)
                dps_ref[:, PGD * g:PGD * (g + 1)] += jnp.sum(dyp * ypre, axis=0, keepdims=True)
                dyg = (dyp * ps_ref[:, PGD * g:PGD * (g + 1)]).astype(bf16)
                dpw_ref[g] += _tn(pb, dyg)
                dpooled = _nt(dyg, wb)
                due = jnp.concatenate([jnp.zeros((BLK, PGD), f32), dpooled / cnt], axis=0)
                for k in POOL_STEPS[w]:
                    due = due + pltpu.roll(due, 2 * BLK - k, axis=0)
                dz_ref[rows, c0:c0 + PGD] = due[BLK:2 * BLK] - dpooled

                def pool_prev(due=due, c0=c0, prow=prow):
                    dz_ref[prow, c0:c0 + PGD] += due[0:BLK]

                into_prev(pool_prev)

        @pl.when(n == nsteps - 1)
        def _():
            bk = bk_ref[...]
            ri = lax.broadcasted_iota(i32, (NBUCK, NH), 0)
            ci = lax.broadcasted_iota(i32, (NBUCK, NH), 1)

            def step(b, acc):
                for h in range(NH):
                    sel = jnp.where(bk == b, dbias_scr[h], 0.0)
                    tot = jnp.sum(jnp.sum(sel, axis=1, keepdims=True), axis=0, keepdims=True)
                    acc = acc + jnp.where((ri == b) & (ci == h), tot, 0.0)
                return acc

            drb_ref[...] = lax.fori_loop(0, NBUCK, step, jnp.zeros((NBUCK, NH), f32))
            lane = lax.broadcasted_iota(i32, (1, 128), 1)
            dsk = jnp.zeros((1, 128), f32)
            for h in range(NH):
                tot = jnp.sum(jnp.sum(dbias_scr[h], axis=1, keepdims=True), axis=0, keepdims=True)
                dsk = dsk - jnp.where(lane == h, tot, 0.0)
            dsk_ref[...] = dsk

    full = lambda *shape: pl.BlockSpec(shape, lambda n: (0,) * len(shape))
    npg = len(POOL_WINDOWS)
    return pl.pallas_call(
        body, grid=(nsteps,),
        in_specs=[pl.BlockSpec((step_rows, DIN), lambda n: (n, 0)),
                  pl.BlockSpec((BLK, DIN), lambda n: (jnp.maximum(n * MIX_SUB - 1, 0), 0)),
                  pl.BlockSpec((step_rows, DMIX), lambda n: (n, 0)),
                  pl.BlockSpec((MIX_SUB, NH, BLK, 2 * BLK), lambda n: (n, 0, 0, 0)),
                  full(1, HD), full(1, HD), full(BLK, 2 * BLK), full(npg, PGD, PGD), full(1, DPOOL)],
        out_specs=[full(T, DIN), full(1, HD), full(1, HD), full(1, 128), full(NBUCK, NH),
                   full(npg, PGD, PGD), full(1, DPOOL)],
        out_shape=(SDS((T, DIN), f32), SDS((1, HD), f32), SDS((1, HD), f32), SDS((1, 128), f32),
                   SDS((NBUCK, NH), f32), SDS((npg, PGD, PGD), f32), SDS((1, DPOOL), f32)),
        scratch_shapes=[pltpu.VMEM((NH, BLK, 2 * BLK), f32)],
        compiler_params=_cparams(("arbitrary",), VMEM_LIMIT_V7X),
        name=name)(z, z, dy, probs, qg, kg, bucket, pool_w, pscale)


class _LocalWeights:
    def __init__(self, w1, wint, wout, w2):
        self.w1, self.wint, self.wout, self.w2 = w1, wint, wout, w2

    def ffn1(self):
        return self.w1

    def after_ffn1(self, gain, x1):
        return gain

    def mix(self, after):
        return self.wint, self.wout

    def before_out_proj(self, wout, after):
        return wout

    def ffn2(self, after):
        return self.w2

    def mix_ffn2_grads_ready(self, dwint, dwout, dw2, dh2):
        self.grads_rest = (dwint, dwout, dw2)
        return dh2

    def before_ffn1_bwd(self, dx1b):
        return dx1b


def _local_step(x, target, weights, g1, gm, g3, qg, kg, sinks, relb, pool_w, pscale):
    bucket = jnp.asarray(_t5_bucket_table())
    sk = sinks.reshape(NH)
    w1 = weights.ffn1()
    h1 = _norm_fwd(x, g1, "norm1_fwd")
    x1, gate1, up1 = _ffn_fwd(h1, w1, x, None, "ffn1_fwd")
    h2 = _norm_fwd(x1, weights.after_ffn1(gm, x1), "norm2_fwd")
    wint, wout = weights.mix(h2)
    z = _in_proj_fwd(h2, wint, "in_proj_fwd")
    ymix, probs = _mix_fwd(z, qg, kg, sk, relb, bucket, pool_w, pscale, "mix_fwd")
    wout = weights.before_out_proj(wout, ymix)
    x2, h3 = _out_proj_fwd(ymix, wout, x1, g3, "out_proj_fwd")
    w2 = weights.ffn2(h3)
    dy, gate2, up2, dyb, loss_lanes = _ffn_fwd(h3, w2, x2, target, "ffn2_fwd")

    dh3, dw2 = _ffn_bwd(dyb, h3, gate2, up2, w2, "ffn2_bwd")
    dx2, dx2b, dg3 = _norm_bwd(dh3, x2, g3, dy, 1.0, "norm3_bwd")
    dymix, dwout = _out_proj_bwd(dx2b, wout, ymix, "out_proj_bwd")
    dz, dqg, dkg, dsk, drb, dpw, dps = _mix_bwd(z, dymix, probs, qg, kg, relb, bucket, pool_w, pscale, "mix_bwd")
    dh2, dwint = _in_proj_bwd(dz, wint, h2, "in_proj_bwd")
    dh2 = weights.mix_ffn2_grads_ready(dwint, dwout, dw2, dh2)
    dx1, dx1b, dgm = _norm_bwd(dh2, x1, gm, dx2, 0.5, "norm2_bwd")
    dx1b = weights.before_ffn1_bwd(dx1b)
    dh1, dw1 = _ffn_bwd(dx1b, h1, gate1, up1, w1, "ffn1_bwd")
    gx, _, dg1 = _norm_bwd(dh1, x, g1, dx1, 1.0, "norm1_bwd")
    small = dict(ffn1_norm=dg1, mix_norm=dgm, ffn2_norm=dg3, pool_scale=dps, q_norm=dqg, k_norm=dkg,
                 attn_sinks=dsk[:, :NH], rel_bias=drb, pool_w=dpw, loss=loss_lanes)
    return gx, (dw1, dwint, dwout, dw2), small


SMALL_NAMES = ("ffn1_norm", "mix_norm", "ffn2_norm", "pool_scale", "q_norm", "k_norm", "attn_sinks", "rel_bias",
               "pool_w", "loss")
SMALL_SHAPES = dict(ffn1_norm=(1, D), mix_norm=(1, D), ffn2_norm=(1, D), pool_scale=(1, DPOOL), q_norm=(1, HD),
                    k_norm=(1, HD), attn_sinks=(1, NH), rel_bias=(NBUCK, NH),
                    pool_w=(1, len(POOL_WINDOWS), PGD, PGD), loss=(1, 128))


def _small_rows(name):
    return -(-int(np.prod(SMALL_SHAPES[name])) // 128)


SMALL_OFF = {}
_r = 0
for _n in SMALL_NAMES:
    SMALL_OFF[_n] = _r
    _r += _small_rows(_n)
SMALL_ROWS = -(-_r // 8) * 8
LOSS_ROW = SMALL_OFF["loss"]


def _pack_small(vals):
    parts = []
    for n in SMALL_NAMES:
        size = _small_rows(n) * 128
        if n in vals:
            flat = vals[n].astype(f32).reshape(-1)
            parts.append(jnp.pad(flat, (0, size - flat.shape[0])))
        else:
            parts.append(jnp.zeros((size,), f32))
    flat = jnp.concatenate(parts)
    flat = jnp.pad(flat, (0, SMALL_ROWS * 128 - flat.shape[0]))
    return flat.reshape(SMALL_ROWS, 128)


def _unpack_small(packed, name):
    size = int(np.prod(SMALL_SHAPES[name]))
    r0 = SMALL_OFF[name]
    return packed[r0:r0 + _small_rows(name)].reshape(-1)[:size].reshape(SMALL_SHAPES[name])


def _position():
    return lax.axis_index("x"), lax.axis_index("y"), lax.axis_index("c")


def _dev_index(x, y, c):
    return 4 * x + 2 * y + c


G1_PIECES, MIX_PIECES, F2_PIECES = (0, 1, 2), (3, 4), (5, 6, 7)


def _group_rows(pieces):
    return sum(PIECE_ROWS[k] for k in pieces)


def _shard_piece(s_ref, k):
    return s_ref.at[pl.ds(PIECE_OFF[k], PIECE_ROWS[k]), :]


def _shard_group(s_ref, pieces):
    return s_ref.at[pl.ds(PIECE_OFF[pieces[0]], _group_rows(pieces)), :]


def _weight_pieces(w1_ref=None, wi_ref=None, wo_ref=None, w2_ref=None):
    arrs = {}
    if w1_ref is not None:
        arrs.update({0: w1_ref.at[0], 1: w1_ref.at[1], 2: w1_ref.at[2]})
    if wi_ref is not None:
        arrs[3] = wi_ref
    if wo_ref is not None:
        arrs[4] = wo_ref
    if w2_ref is not None:
        arrs.update({5: w2_ref.at[0], 6: w2_ref.at[1], 7: w2_ref.at[2]})
    return arrs


def _block_rows(arrs, k, dev):
    r = PIECE_ROWS[k]
    return arrs[k].at[pl.ds(pl.multiple_of(_dev_index(*dev) * r, 16), r), :]


def _all_gather_ffn1(shard):
    pieces = G1_PIECES
    half = FS // 2
    SIB, X0, X1, Y0, Y1, RELAY_Y, RELAY_X, ON_X, ON_Y, ON_D0, ON_D1 = range(11)

    def body(s_ref, w1_ref, send_sems, recv_sems, local_sem):
        x, y, c = _position()
        me, sib = (x, y, c), (x, y, 1 - c)
        xn, yn, dg = (1 - x, y, c), (x, 1 - y, c), (1 - x, 1 - y, c)
        arrs = _weight_pieces(w1_ref=w1_ref)

        def rows_of(k, block, hf):
            r = PIECE_ROWS[k]
            start, size = (0, r) if hf is None else (hf * half, half)
            return arrs[k].at[pl.ds(pl.multiple_of(_dev_index(*block) * r + start, 16), size), :]

        def copies(rel, block, hf, to, from_shard=False):
            def src(k):
                if not from_shard:
                    return rows_of(k, block, hf)
                start, size = (0, PIECE_ROWS[k]) if hf is None else (hf * half, half)
                return s_ref.at[pl.ds(PIECE_OFF[k] + start, size), :]
            return [pltpu.make_async_remote_copy(
                src_ref=src(k), dst_ref=rows_of(k, block, hf), send_sem=send_sems.at[rel], recv_sem=recv_sems.at[rel],
                device_id=to, device_id_type=MESH) for k in pieces]

        def waiter(rel, hf):
            nrows = len(pieces) * (FS if hf is None else half)
            grp = s_ref.at[pl.ds(0, nrows), :]
            return pltpu.make_async_remote_copy(src_ref=grp, dst_ref=grp, send_sem=send_sems.at[rel],
                                                recv_sem=recv_sems.at[rel], device_id=me, device_id_type=MESH)

        def start(cps):
            for cp in cps:
                cp.start()

        mine = [pltpu.make_async_copy(_shard_piece(s_ref, k), _block_rows(arrs, k, me), local_sem) for k in pieces]
        start(mine)
        start(copies(SIB, me, None, sib, True))
        start(copies(X0, me, 0, xn, True))
        start(copies(Y1, me, 1, yn, True))
        start(copies(X1, me, 1, xn, True))
        start(copies(Y0, me, 0, yn, True))
        waiter(X0, 0).wait_recv()
        start(copies(RELAY_Y, xn, 0, yn))
        waiter(Y1, 1).wait_recv()
        start(copies(RELAY_X, yn, 1, xn))
        waiter(X1, 1).wait_recv()
        start(copies(ON_X, xn, None, sib))
        waiter(Y0, 0).wait_recv()
        start(copies(ON_Y, yn, None, sib))
        waiter(RELAY_Y, 0).wait_recv()
        start(copies(ON_D0, dg, 0, sib))
        waiter(RELAY_X, 1).wait_recv()
        start(copies(ON_D1, dg, 1, sib))
        waiter(SIB, None).wait_recv()
        waiter(ON_X, None).wait_recv()
        waiter(ON_Y, None).wait_recv()
        waiter(ON_D0, 0).wait_recv()
        waiter(ON_D1, 1).wait_recv()
        for rel, hf in ((SIB, None), (X0, 0), (X1, 1), (Y0, 0), (Y1, 1), (RELAY_Y, 0), (RELAY_X, 1),
                        (ON_X, None), (ON_Y, None), (ON_D0, 0), (ON_D1, 1)):
            waiter(rel, hf).wait_send()
        grp = _shard_group(s_ref, pieces)
        pltpu.make_async_copy(grp, grp, local_sem).wait()

    hbm = pl.BlockSpec(memory_space=pl.ANY)
    return pl.pallas_call(
        body, in_specs=[hbm], out_specs=hbm, out_shape=SDS((3, F, D), bf16),
        scratch_shapes=[pltpu.SemaphoreType.DMA((11,)), pltpu.SemaphoreType.DMA((11,)), pltpu.SemaphoreType.DMA],
        compiler_params=pltpu.CompilerParams(has_side_effects=True),
        name="all_gather_ffn1")(shard)


HBM_SPEC = pl.BlockSpec(memory_space=pltpu.HBM)
SEM_SPEC = pl.BlockSpec(memory_space=pltpu.SEMAPHORE)
ANY_SPEC = pl.BlockSpec(memory_space=pl.ANY)
SPLIT_EFFECT = pltpu.SideEffectType.DATAFLOW_SIDE_EFFECTING


def _in_hbm(a):
    return pltpu.with_memory_space_constraint(a, pltpu.HBM)


def _hbm_like(a):
    return pltpu.HBM(a.shape, a.dtype)


def _place_own_rows(shard):
    pieces = MIX_PIECES + F2_PIECES

    def body(s_ref, wi_ref, wo_ref, w2_ref, buf, sems):
        x, y, c = _position()
        arrs = _weight_pieces(wi_ref=wi_ref, wo_ref=wo_ref, w2_ref=w2_ref)
        grp = _shard_group(s_ref, pieces)
        load = pltpu.make_async_copy(grp, buf, sems.at[0])
        load.start()
        load.wait()
        base = PIECE_OFF[pieces[0]]
        for k in pieces:
            pltpu.make_async_copy(buf.at[pl.ds(PIECE_OFF[k] - base, PIECE_ROWS[k]), :],
                                  _block_rows(arrs, k, (x, y, c)), sems.at[1]).start()
        pltpu.make_async_copy(grp, buf, sems.at[1]).wait()

    return pl.pallas_call(
        body, in_specs=[ANY_SPEC], out_specs=[ANY_SPEC] * 3,
        out_shape=(SDS((DIN, D), bf16), SDS((DMIX, D), bf16), SDS((3, F, D), bf16)),
        scratch_shapes=[pltpu.VMEM((_group_rows(pieces), D), bf16), pltpu.SemaphoreType.DMA((2,))],
        name="place_own_rows")(shard)


def _xor_peer(x, y, c, k):
    return (x ^ (k >> 2), y ^ ((k >> 1) & 1), c ^ (k & 1))


def _gather_rest_start(shard, wi, wo, w2, w1):
    def body(s_ref, wi_ref, wo_ref, w2_ref, w1_ref,
             ssem_m, rsem_m0, rsem_m, ssem_f, rsem_f0, rsem_f, s_o, wi_o, wo_o, w2_o, w1_o):
        x, y, c = _position()
        me, sib = (x, y, c), (x, y, 1 - c)
        chips = [(1 - x, y), (x, 1 - y), (1 - x, 1 - y)]
        arrs = _weight_pieces(wi_ref=wi_ref, wo_ref=wo_ref, w2_ref=w2_ref)
        for pieces, ssem, rsem0, rsem in ((MIX_PIECES, ssem_m, rsem_m0, rsem_m), (F2_PIECES, ssem_f, rsem_f0, rsem_f)):
            for p in pieces:
                pltpu.make_async_remote_copy(
                    src_ref=_shard_piece(s_ref, p), dst_ref=_block_rows(arrs, p, me), send_sem=ssem.at[0],
                    recv_sem=rsem0, device_id=sib, device_id_type=MESH).start()
            for j, chip in enumerate(chips):
                for p in pieces:
                    pltpu.make_async_remote_copy(
                        src_ref=_shard_piece(s_ref, p), dst_ref=_block_rows(arrs, p, me), send_sem=ssem.at[1 + j],
                        recv_sem=rsem.at[j], device_id=(*chip, c), device_id_type=MESH).start()

    dma = pltpu.SemaphoreType.DMA
    return pl.pallas_call(
        body, name="gather_rest_start",
        out_shape=(dma((4,)), dma(()), dma((3,)), dma((4,)), dma(()), dma((3,)),
                   _hbm_like(shard), _hbm_like(wi), _hbm_like(wo), _hbm_like(w2), _hbm_like(w1)),
        in_specs=(HBM_SPEC,) * 5, out_specs=(SEM_SPEC,) * 6 + (HBM_SPEC,) * 5,
        input_output_aliases={0: 6, 1: 7, 2: 8, 3: 9, 4: 10},
        compiler_params=pltpu.CompilerParams(has_side_effects=SPLIT_EFFECT),
    )(_in_hbm(shard), _in_hbm(wi), _in_hbm(wo), _in_hbm(w2), _in_hbm(w1))


def _gather_mix_pass_on(rsem_m, wi, wo, thru, after):
    def body(wi_ref, wo_ref, thru_ref, rsem, after_ref, fsend, frecv, wi_o, wo_o, thru_o):
        x, y, c = _position()
        sib = (x, y, 1 - c)
        arrs = _weight_pieces(wi_ref=wi_ref, wo_ref=wo_ref)
        both = wi_ref.at[pl.ds(0, _group_rows(MIX_PIECES)), :]
        for j, chip in enumerate([(1 - x, y), (x, 1 - y), (1 - x, 1 - y)]):
            pltpu.make_async_remote_copy(src_ref=both, dst_ref=both, send_sem=fsend.at[j], recv_sem=rsem.at[j],
                                         device_id=(x, y, c), device_id_type=MESH).wait_recv()
            for p in MIX_PIECES:
                rows = _block_rows(arrs, p, (*chip, c))
                pltpu.make_async_remote_copy(src_ref=rows, dst_ref=rows, send_sem=fsend.at[j], recv_sem=frecv.at[j],
                                             device_id=sib, device_id_type=MESH).start()

    dma = pltpu.SemaphoreType.DMA
    return pl.pallas_call(
        body, name="gather_mix_pass_on",
        out_shape=(dma((3,)), dma((3,)), _hbm_like(wi), _hbm_like(wo), _hbm_like(thru)),
        in_specs=(HBM_SPEC, HBM_SPEC, HBM_SPEC, SEM_SPEC, ANY_SPEC), out_specs=(SEM_SPEC, SEM_SPEC) + (HBM_SPEC,) * 3,
        input_output_aliases={0: 2, 1: 3, 2: 4},
        compiler_params=pltpu.CompilerParams(has_side_effects=SPLIT_EFFECT),
    )(wi, wo, _in_hbm(thru), rsem_m, after)


def _gather_mix_wait(ssem_m, rsem_m0, fsend, frecv, shard, wi, wo, after):
    def body(s_ref, wi_ref, wo_ref, ssem, rsem0, fs, fr, after_ref, s_o, wi_o, wo_o):
        x, y, c = _position()
        grp = _shard_group(s_ref, MIX_PIECES)

        def waiter(send_sem, recv_sem):
            return pltpu.make_async_remote_copy(src_ref=grp, dst_ref=grp, send_sem=send_sem, recv_sem=recv_sem,
                                                device_id=(x, y, c), device_id_type=MESH)

        waiter(ssem.at[0], rsem0).wait_recv()
        for j in range(3):
            waiter(fs.at[j], fr.at[j]).wait_recv()
        for rel in range(4):
            waiter(ssem.at[rel], rsem0).wait_send()
        for j in range(3):
            waiter(fs.at[j], fr.at[j]).wait_send()

    return pl.pallas_call(
        body, name="gather_mix_wait", out_shape=(_hbm_like(shard), _hbm_like(wi), _hbm_like(wo)),
        in_specs=(HBM_SPEC,) * 3 + (SEM_SPEC,) * 4 + (ANY_SPEC,), out_specs=(HBM_SPEC,) * 3,
        input_output_aliases={0: 0, 1: 1, 2: 2},
        compiler_params=pltpu.CompilerParams(has_side_effects=SPLIT_EFFECT),
    )(shard, wi, wo, ssem_m, rsem_m0, fsend, frecv, after)


def _gather_ffn2_pass_on(rsem_f, w2, wo, after):
    def body(w2_ref, wo_ref, rsem, after_ref, fsend, frecv, w2_o, wo_o):
        x, y, c = _position()
        sib = (x, y, 1 - c)
        chips = [(1 - x, y), (x, 1 - y), (1 - x, 1 - y)]
        arrs = _weight_pieces(w2_ref=w2_ref)
        three = w2_ref.at[0, pl.ds(0, _group_rows(F2_PIECES)), :]
        for j, chip in enumerate(chips):
            pltpu.make_async_remote_copy(src_ref=three, dst_ref=three, send_sem=fsend.at[j], recv_sem=rsem.at[j],
                                         device_id=(x, y, c), device_id_type=MESH).wait_recv()
            for p in F2_PIECES:
                rows = _block_rows(arrs, p, (*chip, c))
                pltpu.make_async_remote_copy(src_ref=rows, dst_ref=rows, send_sem=fsend.at[j], recv_sem=frecv.at[j],
                                             device_id=sib, device_id_type=MESH).start()

    dma = pltpu.SemaphoreType.DMA
    return pl.pallas_call(
        body, name="gather_ffn2_pass_on", out_shape=(dma((3,)), dma((3,)), _hbm_like(w2), _hbm_like(wo)),
        in_specs=(HBM_SPEC, HBM_SPEC, SEM_SPEC, ANY_SPEC), out_specs=(SEM_SPEC, SEM_SPEC, HBM_SPEC, HBM_SPEC),
        input_output_aliases={0: 2, 1: 3},
        compiler_params=pltpu.CompilerParams(has_side_effects=SPLIT_EFFECT),
    )(w2, wo, rsem_f, after)


def _gather_ffn2_wait(ssem_f, rsem_f0, fsend, frecv, shard, w2, after):
    def body(s_ref, w2_ref, ssem, rsem0, fs, fr, after_ref, w2_o):
        x, y, c = _position()
        grp = _shard_group(s_ref, F2_PIECES)

        def waiter(send_sem, recv_sem):
            return pltpu.make_async_remote_copy(src_ref=grp, dst_ref=grp, send_sem=send_sem, recv_sem=recv_sem,
                                                device_id=(x, y, c), device_id_type=MESH)

        waiter(ssem.at[0], rsem0).wait_recv()
        for j in range(3):
            waiter(fs.at[j], fr.at[j]).wait_recv()
        for rel in range(4):
            waiter(ssem.at[rel], rsem0).wait_send()
        for j in range(3):
            waiter(fs.at[j], fr.at[j]).wait_send()

    return pl.pallas_call(
        body, name="gather_ffn2_wait", out_shape=_hbm_like(w2),
        in_specs=(HBM_SPEC, HBM_SPEC, SEM_SPEC, SEM_SPEC, SEM_SPEC, SEM_SPEC, ANY_SPEC), out_specs=HBM_SPEC,
        input_output_aliases={1: 0},
        compiler_params=pltpu.CompilerParams(has_side_effects=SPLIT_EFFECT),
    )(shard, w2, ssem_f, rsem_f0, fsend, frecv, after)


class _GatheredWeights(_LocalWeights):
    def __init__(self, shard):
        w1 = _all_gather_ffn1(shard)
        wi, wo, w2 = _place_own_rows(shard)
        (self.ssem_m, self.rsem_m0, self.rsem_m, self.ssem_f, self.rsem_f0, self.rsem_f,
         self.shard, self.wi, self.wo, self.w2_part, self.w1) = _gather_rest_start(shard, wi, wo, w2, w1)

    def after_ffn1(self, gain, x1):
        self.fsend_m, self.frecv_m, self.wi, self.wo, gain = _gather_mix_pass_on(self.rsem_m, self.wi, self.wo, gain, x1)
        return gain

    def mix(self, after):
        self.shard, wint, wout = _gather_mix_wait(self.ssem_m, self.rsem_m0, self.fsend_m, self.frecv_m, self.shard,
                                                  self.wi, self.wo, after)
        return wint, wout

    def before_out_proj(self, wout, after):
        self.fsend, self.frecv, self.w2_part, wout = _gather_ffn2_pass_on(self.rsem_f, self.w2_part, wout, after)
        return wout

    def ffn2(self, after):
        return _gather_ffn2_wait(self.ssem_f, self.rsem_f0, self.fsend, self.frecv, self.shard, self.w2_part, after)

    def mix_ffn2_grads_ready(self, dwint, dwout, dw2, dh2):
        rx1 = lax.empty((4, RSA_ROWS, D), bf16)
        self.sa, self.ra, dwint, dwout, dw2, rx1, dh2 = _rsa_level1_start(dwint, dwout, dw2, rx1, dh2)
        self.level1 = (dwint, dwout, dw2, rx1)
        return dh2

    def before_ffn1_bwd(self, dx1b):
        dwint, dwout, dw2, rx1 = _rsa_level1_wait(self.sa, self.ra, *self.level1, dx1b)
        tx, self.acc = _rsa_chip_sums(dwint, dwout, dw2, rx1)
        rx2 = lax.empty((3, RSA_ROWS, D), bf16)
        self.sb, self.rb, self.tx, self.rx2, dx1b = _rsa_level2_start(tx, rx2, dx1b)
        return dx1b

    def mix_ffn2_grads_total(self, after):
        rx2 = _rsa_level2_wait(self.sb, self.rb, self.tx, self.rx2, after)
        return _rsa_total(self.acc, rx2)


RS_CHUNK = 176


def _reduce_scatter_ffn1(dw1, small_packed):
    pieces = G1_PIECES
    nrows = _group_rows(pieces)
    nchunk = nrows // RS_CHUNK

    def body(d1_ref, p_ref, red_ref, rx1_ref, rx2_ref, tot_ref,
             own_buf, rx_buf, tx_buf, acc, sa, ra, sb, rb, lsem, pair, chips, small_send, small_recv):
        x, y, c = _position()
        me, sib = (x, y, c), (x, y, 1 - c)
        rel_chips = [(x, y), (1 - x, y), (x, 1 - y), (1 - x, 1 - y)]
        srcs = _weight_pieces(w1_ref=d1_ref)

        my_chip = 2 * x + y
        pair[c] = p_ref[...]
        swap = pltpu.make_async_remote_copy(
            src_ref=p_ref, dst_ref=pair.at[c], send_sem=small_send.at[0], recv_sem=small_recv.at[0],
            device_id=sib, device_id_type=MESH)
        swap.start()
        small = [pltpu.make_async_remote_copy(
            src_ref=chips.at[my_chip], dst_ref=chips.at[my_chip], send_sem=small_send.at[j], recv_sem=small_recv.at[j],
            device_id=(*rel_chips[j], c), device_id_type=MESH) for j in (1, 2, 3)]

        def piece(k, dev):
            r = PIECE_ROWS[k]
            return srcs[k].at[pl.ds(pl.multiple_of(_dev_index(*dev) * r, 16), r), :]

        def packed(ref, k):
            return ref.at[pl.ds(PIECE_OFF[k], PIECE_ROWS[k]), :]

        for j, chip in enumerate(rel_chips):
            for k in pieces:
                pltpu.make_async_remote_copy(
                    src_ref=piece(k, (*chip, 1 - c)), dst_ref=packed(rx1_ref.at[j], k),
                    send_sem=sa.at[j], recv_sem=ra.at[j], device_id=sib, device_id_type=MESH).start()

        def wait_a(j):
            return pltpu.make_async_remote_copy(src_ref=rx1_ref.at[j], dst_ref=rx1_ref.at[j], send_sem=sa.at[j],
                                                recv_sem=ra.at[j], device_id=me, device_id_type=MESH)

        def ici(j):
            return pltpu.make_async_remote_copy(
                src_ref=tx_buf.at[j - 1], dst_ref=rx2_ref.at[j - 1], send_sem=sb.at[j - 1], recv_sem=rb.at[j - 1],
                device_id=(*rel_chips[j], c), device_id_type=MESH)

        swap.wait_recv()
        chips[my_chip] = pair[0] + pair[1]
        for cp in small:
            cp.start()

        for j in (1, 2, 3, 0):
            loads = [pltpu.make_async_copy(piece(k, (*rel_chips[j], c)), packed(own_buf, k), lsem)
                     for k in pieces]
            for cp in loads:
                cp.start()
            wait_a(j).wait_recv()
            got = pltpu.make_async_copy(rx1_ref.at[j], rx_buf, lsem)
            got.start()
            pltpu.make_async_copy(rx_buf, rx_buf, lsem).wait()
            got.wait()

            def add(i, carry, j=j):
                rows = pl.ds(pl.multiple_of(i * RS_CHUNK, 16), RS_CHUNK)
                tot = own_buf[rows, :].astype(f32) + rx_buf[rows, :].astype(f32)
                if j == 0:
                    acc[rows, :] = tot
                else:
                    tx_buf[j - 1, rows, :] = tot.astype(bf16)
                return carry

            lax.fori_loop(0, nchunk, add, 0)
            if j != 0:
                ici(j).start()

        for j in (1, 2, 3):
            ici(j).wait_recv()
            got = pltpu.make_async_copy(rx2_ref.at[j - 1], rx_buf, lsem)
            got.start()
            got.wait()

            def add2(i, carry):
                rows = pl.ds(pl.multiple_of(i * RS_CHUNK, 16), RS_CHUNK)
                acc[rows, :] += rx_buf[rows, :].astype(f32)
                return carry

            lax.fori_loop(0, nchunk, add2, 0)
        out = pltpu.make_async_copy(acc, red_ref, lsem)
        out.start()
        out.wait()
        for cp in small:
            cp.wait_recv()
        tot = (chips[0] + chips[1]) + (chips[2] + chips[3])
        tot_ref[...] = tot
        loss = jnp.sum(tot[LOSS_ROW:LOSS_ROW + 1, :], axis=-1, keepdims=True)
        tot_ref[LOSS_ROW:LOSS_ROW + 1, :] = jnp.broadcast_to(loss, (1, 128))
        for j in range(4):
            wait_a(j).wait_send()
        for j in (1, 2, 3):
            ici(j).wait_send()
        swap.wait_send()
        for cp in small:
            cp.wait_send()

    hbm = pl.BlockSpec(memory_space=pl.ANY)
    vm = pl.BlockSpec(memory_space=pltpu.VMEM)
    red, _, _, small_tot = pl.pallas_call(
        body, in_specs=[hbm, vm], out_specs=[hbm] * 3 + [vm],
        out_shape=(SDS((nrows, D), f32), SDS((4, nrows, D), bf16), SDS((3, nrows, D), bf16),
                   SDS((SMALL_ROWS, 128), f32)),
        scratch_shapes=[pltpu.VMEM((nrows, D), bf16), pltpu.VMEM((nrows, D), bf16),
                        pltpu.VMEM((3, nrows, D), bf16), pltpu.VMEM((nrows, D), f32),
                        pltpu.SemaphoreType.DMA((4,)), pltpu.SemaphoreType.DMA((4,)),
                        pltpu.SemaphoreType.DMA((3,)), pltpu.SemaphoreType.DMA((3,)), pltpu.SemaphoreType.DMA,
                        pltpu.VMEM((2, SMALL_ROWS, 128), f32), pltpu.VMEM((4, SMALL_ROWS, 128), f32),
                        pltpu.SemaphoreType.DMA((4,)), pltpu.SemaphoreType.DMA((4,))],
        compiler_params=pltpu.CompilerParams(has_side_effects=True, vmem_limit_bytes=VMEM_LIMIT_V7X),
        name="reduce_scatter_ffn1")(dw1, small_packed)
    return red, small_tot


RSA_PIECES = MIX_PIECES + F2_PIECES
RSA_ROWS = _group_rows(RSA_PIECES)
RSA_OFF = {k: PIECE_OFF[k] - PIECE_OFF[RSA_PIECES[0]] for k in RSA_PIECES}
RSA_BLOCK = 192


def _rsa_rows(ref, k):
    return ref.at[pl.ds(RSA_OFF[k], PIECE_ROWS[k]), :]


def _rsa_level1_start(dwint, dwout, dw2, rx1, thru):
    def body(di_ref, do_ref, d2_ref, rx1_ref, thru_ref, sa, ra, di_o, do_o, d2_o, rx1_o, thru_o):
        x, y, c = _position()
        srcs = _weight_pieces(wi_ref=di_ref, wo_ref=do_ref, w2_ref=d2_ref)
        for j, chip in enumerate([(x, y), (1 - x, y), (x, 1 - y), (1 - x, 1 - y)]):
            for k in RSA_PIECES:
                pltpu.make_async_remote_copy(
                    src_ref=_block_rows(srcs, k, (*chip, 1 - c)), dst_ref=_rsa_rows(rx1_ref.at[j], k),
                    send_sem=sa.at[j], recv_sem=ra.at[j], device_id=(x, y, 1 - c), device_id_type=MESH).start()

    dma = pltpu.SemaphoreType.DMA
    arrs = (dwint, dwout, dw2, rx1, thru)
    return pl.pallas_call(
        body, name="rsa_level1_start", out_shape=(dma((4,)), dma((4,))) + tuple(_hbm_like(a) for a in arrs),
        in_specs=(HBM_SPEC,) * 5, out_specs=(SEM_SPEC,) * 2 + (HBM_SPEC,) * 5,
        input_output_aliases={0: 2, 1: 3, 2: 4, 3: 5, 4: 6},
        compiler_params=pltpu.CompilerParams(has_side_effects=SPLIT_EFFECT),
    )(*[_in_hbm(a) for a in arrs])


def _rsa_level1_wait(sa, ra, dwint, dwout, dw2, rx1, after):
    def body(di_ref, do_ref, d2_ref, rx1_ref, sa_ref, ra_ref, after_ref, di_o, do_o, d2_o, rx1_o):
        x, y, c = _position()
        for j in range(4):
            d = pltpu.make_async_remote_copy(src_ref=rx1_ref.at[j], dst_ref=rx1_ref.at[j], send_sem=sa_ref.at[j],
                                             recv_sem=ra_ref.at[j], device_id=(x, y, c), device_id_type=MESH)
            d.wait_recv()
            d.wait_send()

    arrs = (dwint, dwout, dw2, rx1)
    return pl.pallas_call(
        body, name="rsa_level1_wait", out_shape=tuple(_hbm_like(a) for a in arrs),
        in_specs=(HBM_SPEC,) * 4 + (SEM_SPEC, SEM_SPEC, ANY_SPEC), out_specs=(HBM_SPEC,) * 4,
        input_output_aliases={0: 0, 1: 1, 2: 2, 3: 3},
        compiler_params=pltpu.CompilerParams(has_side_effects=SPLIT_EFFECT),
    )(*arrs, sa, ra, after)


def _rsa_chip_sums(dwint, dwout, dw2, rx1):
    nblk = RSA_ROWS // RSA_BLOCK

    def body(di_ref, do_ref, d2_ref, rx1_ref, tx_ref, acc_ref, own_buf, rx_buf, tx_buf, acc_buf, lsems):
        x, y, c = _position()
        srcs = _weight_pieces(wi_ref=di_ref, wo_ref=do_ref, w2_ref=d2_ref)
        for j, chip in enumerate([(x, y), (1 - x, y), (x, 1 - y), (1 - x, 1 - y)]):
            loads = [pltpu.make_async_copy(_block_rows(srcs, k, (*chip, c)), _rsa_rows(own_buf, k), lsems.at[0])
                     for k in RSA_PIECES]
            got = pltpu.make_async_copy(rx1_ref.at[j], rx_buf, lsems.at[1])
            for cp in loads + [got]:
                cp.start()
            pltpu.make_async_copy(rx_buf, rx_buf, lsems.at[0]).wait()
            got.wait()

            def add(i, carry, j=j):
                rows = pl.ds(pl.multiple_of(i * RSA_BLOCK, 16), RSA_BLOCK)
                tot = own_buf[rows, :].astype(f32) + rx_buf[rows, :].astype(f32)
                if j == 0:
                    acc_buf[rows, :] = tot
                else:
                    tx_buf[rows, :] = tot.astype(bf16)
                return carry

            lax.fori_loop(0, nblk, add, 0)
            out = (pltpu.make_async_copy(acc_buf, acc_ref, lsems.at[2]) if j == 0
                   else pltpu.make_async_copy(tx_buf, tx_ref.at[j - 1], lsems.at[2]))
            out.start()
            out.wait()

    return pl.pallas_call(
        body, in_specs=[ANY_SPEC] * 4, out_specs=[ANY_SPEC] * 2,
        out_shape=(SDS((3, RSA_ROWS, D), bf16), SDS((RSA_ROWS, D), f32)),
        scratch_shapes=[pltpu.VMEM((RSA_ROWS, D), bf16), pltpu.VMEM((RSA_ROWS, D), bf16),
                        pltpu.VMEM((RSA_ROWS, D), bf16), pltpu.VMEM((RSA_ROWS, D), f32),
                        pltpu.SemaphoreType.DMA((3,))],
        compiler_params=_cparams(None, VMEM_LIMIT_V7X), name="rsa_chip_sums")(dwint, dwout, dw2, rx1)


def _rsa_level2_start(tx, rx2, thru):
    def body(tx_ref, rx2_ref, thru_ref, sb, rb, tx_o, rx2_o, thru_o):
        x, y, c = _position()
        for j, chip in enumerate([(1 - x, y), (x, 1 - y), (1 - x, 1 - y)]):
            pltpu.make_async_remote_copy(src_ref=tx_ref.at[j], dst_ref=rx2_ref.at[j], send_sem=sb.at[j],
                                         recv_sem=rb.at[j], device_id=(*chip, c), device_id_type=MESH).start()

    dma = pltpu.SemaphoreType.DMA
    arrs = (tx, rx2, thru)
    return pl.pallas_call(
        body, name="rsa_level2_start", out_shape=(dma((3,)), dma((3,))) + tuple(_hbm_like(a) for a in arrs),
        in_specs=(HBM_SPEC,) * 3, out_specs=(SEM_SPEC,) * 2 + (HBM_SPEC,) * 3,
        input_output_aliases={0: 2, 1: 3, 2: 4},
        compiler_params=pltpu.CompilerParams(has_side_effects=SPLIT_EFFECT),
    )(*[_in_hbm(a) for a in arrs])


def _rsa_level2_wait(sb, rb, tx, rx2, after):
    def body(tx_ref, rx2_ref, sb_ref, rb_ref, after_ref, rx2_o):
        x, y, c = _position()
        for j in range(3):
            d = pltpu.make_async_remote_copy(src_ref=tx_ref.at[j], dst_ref=rx2_ref.at[j], send_sem=sb_ref.at[j],
                                             recv_sem=rb_ref.at[j], device_id=(x, y, c), device_id_type=MESH)
            d.wait_recv()
            d.wait_send()

    return pl.pallas_call(
        body, name="rsa_level2_wait", out_shape=_hbm_like(rx2),
        in_specs=(HBM_SPEC, HBM_SPEC, SEM_SPEC, SEM_SPEC, ANY_SPEC), out_specs=HBM_SPEC,
        input_output_aliases={1: 0},
        compiler_params=pltpu.CompilerParams(has_side_effects=SPLIT_EFFECT),
    )(tx, rx2, sb, rb, after)


def _rsa_total(acc, rx2):
    def body(a_ref, r_ref, o_ref):
        o_ref[...] = ((a_ref[...] + r_ref[0].astype(f32)) + r_ref[1].astype(f32)) + r_ref[2].astype(f32)

    return pl.pallas_call(
        body, grid=(RSA_ROWS // RSA_BLOCK,),
        in_specs=[pl.BlockSpec((RSA_BLOCK, D), lambda i: (i, 0)), pl.BlockSpec((3, RSA_BLOCK, D), lambda i: (0, i, 0))],
        out_specs=pl.BlockSpec((RSA_BLOCK, D), lambda i: (i, 0)),
        out_shape=SDS((RSA_ROWS, D), f32), name="rsa_total")(acc, rx2)


def _adamw_math(w, g, m, v):
    m = ADAM_B1 * m + (1.0 - ADAM_B1) * g
    v = ADAM_B2 * v + (1.0 - ADAM_B2) * (g * g)
    m_hat = m / (1.0 - ADAM_B1 ** ADAM_STEP)
    v_hat = v / (1.0 - ADAM_B2 ** ADAM_STEP)
    delta = -ADAM_LR * (m_hat / (jnp.sqrt(v_hat) + ADAM_EPS) + ADAM_WD * w)
    return delta, m, v


def _adamw_big(ws, ms, vs, red1, red_rest):
    npiece = len(BIG)
    rmax = max(PIECE_ROWS)

    def body(*refs):
        ins = (refs[0:npiece], refs[npiece:2 * npiece], refs[2 * npiece:3 * npiece])
        red1_ref, rest_ref = refs[3 * npiece:3 * npiece + 2]
        out_refs = refs[3 * npiece + 2:7 * npiece + 2]
        inb, outb, in_sems, out_sems = refs[7 * npiece + 2:]

        def grad_rows(k):
            if k in G1_PIECES:
                return red1_ref.at[pl.ds(PIECE_OFF[k], PIECE_ROWS[k]), :]
            return _rsa_rows(rest_ref, k)

        def loads(k):
            s, r = k % 2, PIECE_ROWS[k]
            cps = [pltpu.make_async_copy(ins[q][k].at[0], inb.at[s, q, pl.ds(0, r), :], in_sems.at[4 * s + q])
                   for q in range(3)]
            cps.append(pltpu.make_async_copy(grad_rows(k), inb.at[s, 3, pl.ds(0, r), :], in_sems.at[4 * s + 3]))
            return cps

        def stores(k):
            s, r = k % 2, PIECE_ROWS[k]
            return [pltpu.make_async_copy(outb.at[s, q, pl.ds(0, r), :], out_refs[q * npiece + k].at[0],
                                          out_sems.at[4 * s + q]) for q in range(4)]

        for cp in loads(0):
            cp.start()
        for k in range(npiece):
            s, r = k % 2, PIECE_ROWS[k]
            if k + 1 < npiece:
                for cp in loads(k + 1):
                    cp.start()
            for cp in loads(k):
                cp.wait()
            if k >= 2:
                for cp in stores(k - 2):
                    cp.wait()
            g = inb[s, 3, 0:r, :]
            d, nm, nv = _adamw_math(inb[s, 0, 0:r, :], g, inb[s, 1, 0:r, :], inb[s, 2, 0:r, :])
            outb[s, 0, 0:r, :] = g
            outb[s, 1, 0:r, :] = d
            outb[s, 2, 0:r, :] = nm
            outb[s, 3, 0:r, :] = nv
            for cp in stores(k):
                cp.start()
        for k in (npiece - 2, npiece - 1):
            for cp in stores(k):
                cp.wait()

    hbm = pl.BlockSpec(memory_space=pl.ANY)
    outs = pl.pallas_call(
        body, in_specs=[hbm] * (3 * npiece + 2), out_specs=[hbm] * (4 * npiece),
        out_shape=tuple(SDS(w.shape, f32) for _ in range(4) for w in ws),
        scratch_shapes=[pltpu.VMEM((2, 4, rmax, D), f32), pltpu.VMEM((2, 4, rmax, D), f32),
                        pltpu.SemaphoreType.DMA((8,)), pltpu.SemaphoreType.DMA((8,))],
        compiler_params=_cparams(None, VMEM_LIMIT_V7X), name="adamw_big")(*ws, *ms, *vs, red1, red_rest)
    return [list(outs[q * npiece:(q + 1) * npiece]) for q in range(4)]


def _adamw_small(w, m, v, g, name):
    def body(w_ref, m_ref, v_ref, g_ref, d_ref, nm_ref, nv_ref):
        d, nm, nv = _adamw_math(w_ref[...], g_ref[...], m_ref[...], v_ref[...])
        d_ref[...] = d
        nm_ref[...] = nm
        nv_ref[...] = nv

    return pl.pallas_call(
        body, out_shape=tuple(SDS(w.shape, f32) for _ in range(3)), name=name)(w, m, v, g)


WEIGHTS = ("ffn1_norm", "ffn1_w_gate", "ffn1_w_up", "ffn1_w_down", "mix_norm", "w_in", "q_norm", "k_norm",
           "attn_sinks", "rel_bias", "pool_w", "pool_scale", "w_out", "ffn2_norm", "ffn2_w_gate", "ffn2_w_up",
           "ffn2_w_down")
BIG = (("ffn1_w_gate", True), ("ffn1_w_up", True), ("ffn1_w_down", False), ("w_in", True), ("w_out", False),
       ("ffn2_w_gate", True), ("ffn2_w_up", True), ("ffn2_w_down", False))


def kernel(x, ffn1_norm, ffn1_w_gate, ffn1_w_up, ffn1_w_down, mix_norm, w_in, q_norm, k_norm, attn_sinks, rel_bias, pool_w, pool_scale, w_out, ffn2_norm, ffn2_w_gate, ffn2_w_up, ffn2_w_down, loss_target, m_ffn1_norm, m_ffn1_w_gate, m_ffn1_w_up, m_ffn1_w_down, m_mix_norm, m_w_in, m_q_norm, m_k_norm, m_attn_sinks, m_rel_bias, m_pool_w, m_pool_scale, m_w_out, m_ffn2_norm, m_ffn2_w_gate, m_ffn2_w_up, m_ffn2_w_down, v_ffn1_norm, v_ffn1_w_gate, v_ffn1_w_up, v_ffn1_w_down, v_mix_norm, v_w_in, v_q_norm, v_k_norm, v_attn_sinks, v_rel_bias, v_pool_w, v_pool_scale, v_w_out, v_ffn2_norm, v_ffn2_w_gate, v_ffn2_w_up, v_ffn2_w_down):
    args = dict(locals())
    w = {n: args[n] for n in WEIGHTS}
    m = {n: args["m_" + n] for n in WEIGHTS}
    v = {n: args["v_" + n] for n in WEIGHTS}

    as_rows = lambda a, tr: jnp.swapaxes(a, 1, 2) if tr else a
    shard = jnp.concatenate([as_rows(w[n], tr)[0].astype(bf16) for n, tr in BIG], axis=0)
    exchanges = _GatheredWeights(shard)
    gx, (dw1, _, _, _), small = _local_step(
        x[0], loss_target[0], exchanges, ffn1_norm, mix_norm, ffn2_norm, q_norm, k_norm, attn_sinks,
        rel_bias, pool_w[0], pool_scale)

    red1, small_tot = _reduce_scatter_ffn1(dw1, _pack_small(small))
    red_rest = exchanges.mix_ffn2_grads_total(red1)

    grads, deltas, new_m, new_v = {}, {}, {}, {}
    big_out = _adamw_big(*[[as_rows(t[n], tr) for n, tr in BIG] for t in (w, m, v)], red1, red_rest)
    for k, (n, tr) in enumerate(BIG):
        grads[n], deltas[n], new_m[n], new_v[n] = [as_rows(o[k], tr) for o in big_out]
    small_names = [n for n in SMALL_NAMES if n != "loss"]
    ds, nms, nvs = _adamw_small(_pack_small({n: w[n] for n in small_names}), _pack_small({n: m[n] for n in small_names}),
                                _pack_small({n: v[n] for n in small_names}), small_tot, "adamw_small")
    for n in small_names:
        grads[n] = _unpack_small(small_tot, n)
        deltas[n], new_m[n], new_v[n] = _unpack_small(ds, n), _unpack_small(nms, n), _unpack_small(nvs, n)
    loss = small_tot[LOSS_ROW, 0]
    return (loss, gx[None], *[grads[n] for n in WEIGHTS], *[deltas[n] for n in WEIGHTS],
            *[new_m[n] for n in WEIGHTS], *[new_v[n] for n in WEIGHTS])
```

```python
import functools

import jax
import jax.numpy as jnp
import numpy as np
from jax import lax
from jax.experimental import pallas as pl
from jax.experimental.pallas import tpu as pltpu

f32, bf16, i32 = jnp.float32, jnp.bfloat16, jnp.int32
SDS = jax.ShapeDtypeStruct

D = 1024
F = 2816
HD = 64
NH = 8
NKV = 2
GQA = NH // NKV
DATTN = NH * HD
DKV = NKV * HD
DPOOL = 512
POOL_WINDOWS = (2, 4, 8, 16)
PGD = DPOOL // len(POOL_WINDOWS)
DIN = DATTN + 2 * DKV + DPOOL
DMIX = DATTN + DPOOL
BLK = 128
NBUCK = 32
MAX_DISTANCE = 128
EPS = 1e-6
NEG = -1e30
SCALE = HD ** -0.5

ADAM_LR, ADAM_B1, ADAM_B2, ADAM_EPS, ADAM_WD, ADAM_STEP = 0.001, 0.9, 0.999, 1e-08, 0.01, 10

NDEV = 8
FS = F // NDEV
INS = DIN // NDEV
OUTS = DMIX // NDEV
PIECE_ROWS = (FS, FS, FS, INS, OUTS, FS, FS, FS)
PIECE_OFF = tuple(int(v) for v in np.cumsum((0,) + PIECE_ROWS[:-1]))
PACK_ROWS = sum(PIECE_ROWS)

VMEM_LIMIT_V7X = 56 * 1024 * 1024

MESH = pl.DeviceIdType.MESH


def _cparams(sem=None, vmem=None):
    return pltpu.CompilerParams(dimension_semantics=sem, vmem_limit_bytes=vmem)


def _nt(a, b):
    return lax.dot_general(a, b, (((1,), (1,)), ((), ())), preferred_element_type=f32)


def _tn(a, b):
    return lax.dot_general(a, b, (((0,), (0,)), ((), ())), preferred_element_type=f32)


def _nn(a, b):
    return jnp.dot(a, b, preferred_element_type=f32)


def _sigmoid(x):
    return 1.0 / (1.0 + jnp.exp(-x))


def _norm_fwd(x, g, name):
    T = x.shape[0]
    tm = min(512, T)

    def body(x_ref, g_ref, h_ref):
        xv = x_ref[...]
        r = lax.rsqrt(jnp.mean(xv * xv, axis=-1, keepdims=True) + EPS)
        h_ref[...] = (xv * r * g_ref[...]).astype(bf16)

    return pl.pallas_call(
        body, grid=(T // tm,),
        in_specs=[pl.BlockSpec((tm, D), lambda i: (i, 0)), pl.BlockSpec((1, D), lambda i: (0, 0))],
        out_specs=pl.BlockSpec((tm, D), lambda i: (i, 0)),
        out_shape=SDS((T, D), bf16), name=name)(x, g)


def _norm_bwd(dh, x, g, dres, out_scale, name):
    T = x.shape[0]
    tm = min(512, T)

    def body(dh_ref, x_ref, g_ref, dr_ref, dx_ref, dxb_ref, dg_ref):
        i = pl.program_id(0)
        xv = x_ref[...]
        r = lax.rsqrt(jnp.mean(xv * xv, axis=-1, keepdims=True) + EPS)
        xh = xv * r
        dhv = dh_ref[...]
        dxh = dhv * g_ref[...]
        dx = dr_ref[...] + r * (dxh - xh * jnp.mean(dxh * xh, axis=-1, keepdims=True))
        dx_ref[...] = dx
        dxb_ref[...] = (out_scale * dx).astype(bf16)
        dg = jnp.sum(dhv * xh, axis=0, keepdims=True)

        @pl.when(i == 0)
        def _():
            dg_ref[...] = dg

        @pl.when(i > 0)
        def _():
            dg_ref[...] += dg

    tok = pl.BlockSpec((tm, D), lambda i: (i, 0))
    vec = pl.BlockSpec((1, D), lambda i: (0, 0))
    return pl.pallas_call(
        body, grid=(T // tm,),
        in_specs=[tok, tok, vec, tok], out_specs=[tok, tok, vec],
        out_shape=(SDS((T, D), f32), SDS((T, D), bf16), SDS((1, D), f32)),
        compiler_params=_cparams(("arbitrary",)), name=name)(dh, x, g, dres)


FFN_ROW_CHUNK = 256


def _ffn_tiles(T):
    return min(1024, T), 256


def _ffn_fwd(h, w, x, target, name):
    T = h.shape[0]
    tm, tf = _ffn_tiles(T)
    nf = F // tf
    with_loss = target is not None

    def body(*refs):
        if with_loss:
            h_ref, w_ref, x_hbm, t_hbm, xo_ref, g_ref, u_ref, dyb_ref, loss_ref, tbuf, sem = refs
        else:
            h_ref, w_ref, x_hbm, xo_ref, g_ref, u_ref, sem = refs
        fi = pl.program_id(0)

        @pl.when(fi == 0)
        def _():
            cp = pltpu.make_async_copy(x_hbm, xo_ref, sem)
            cp.start()
            cp.wait()

        wgu = w_ref[0:2].reshape(2 * tf, D)
        for r in range(0, T, tm):
            rows = slice(r, r + tm)
            gu = _nt(h_ref[rows, :], wgu)
            gate, up = gu[:, :tf], gu[:, tf:]
            act = gate * _sigmoid(gate) * up
            g_ref[0, rows, :] = gate.astype(bf16)
            u_ref[0, rows, :] = up.astype(bf16)
            xo_ref[rows, :] += _nn((0.5 * act).astype(bf16), w_ref[2])

        if with_loss:
            @pl.when(fi == nf - 1)
            def _():
                lanes = jnp.zeros((1, 128), f32)
                for r in range(0, T, tm):
                    rows = slice(r, r + tm)
                    cp = pltpu.make_async_copy(t_hbm.at[pl.ds(r, tm), :], tbuf, sem)
                    cp.start()
                    cp.wait()
                    e = xo_ref[rows, :] - tbuf[...]
                    dy = e * (1.0 / D)
                    xo_ref[rows, :] = dy
                    dyb_ref[rows, :] = (0.5 * dy).astype(bf16)
                    col = jnp.sum(e * e, axis=0, keepdims=True) * (0.5 / D)
                    for k in range(D // 128):
                        lanes = lanes + col[:, 128 * k:128 * (k + 1)]
                loss_ref[...] = lanes

    tok = pl.BlockSpec((T, D), lambda f: (0, 0))
    act_spec = pl.BlockSpec((1, T, tf), lambda f: (f, 0, 0))
    hbm = pl.BlockSpec(memory_space=pl.ANY)
    in_specs = [tok, pl.BlockSpec((3, tf, D), lambda f: (0, f, 0)), hbm]
    out_specs = [tok, act_spec, act_spec]
    out_shape = [SDS((T, D), f32), SDS((nf, T, tf), bf16), SDS((nf, T, tf), bf16)]
    scratch = [pltpu.SemaphoreType.DMA]
    args = [h, w, x]
    if with_loss:
        in_specs.append(hbm)
        args.append(target)
        out_specs += [tok, pl.BlockSpec((1, 128), lambda f: (0, 0))]
        out_shape += [SDS((T, D), bf16), SDS((1, 128), f32)]
        scratch = [pltpu.VMEM((tm, D), f32)] + scratch
    return pl.pallas_call(
        body, grid=(nf,), in_specs=in_specs, out_specs=out_specs, out_shape=tuple(out_shape), scratch_shapes=scratch,
        compiler_params=_cparams(("arbitrary",), VMEM_LIMIT_V7X), name=name)(*args)


def _ffn_bwd(dob, h, gate, up, w, name):
    T = h.shape[0]
    _, tf = _ffn_tiles(T)
    nf = F // tf

    def body(do_hbm, h_hbm, g_ref, u_ref, w_ref, dh_hbm, dw_ref, do_v, h_v, dh_acc, dgu_s, act_s, sems):
        fi = pl.program_id(0)

        @pl.when(fi == 0)
        def _():
            loads = [pltpu.make_async_copy(do_hbm, do_v, sems.at[0]), pltpu.make_async_copy(h_hbm, h_v, sems.at[1])]
            for cp in loads:
                cp.start()
            dh_acc[...] = jnp.zeros_like(dh_acc)
            for cp in loads:
                cp.wait()

        wgu = w_ref[0:2].reshape(2 * tf, D)
        for r in range(0, T, FFN_ROW_CHUNK):
            rows = slice(r, r + FFN_ROW_CHUNK)
            dov = do_v[rows, :]
            gv = g_ref[0, rows, :].astype(f32)
            uv = u_ref[0, rows, :].astype(f32)
            sg = _sigmoid(gv)
            sil = gv * sg
            dact = _nt(dov, w_ref[2])
            dup = dact * sil
            dgate = dact * uv * (sg * (1.0 + gv * (1.0 - sg)))
            dgu = jnp.concatenate([dgate.astype(bf16), dup.astype(bf16)], axis=1)
            dgu_s[rows, :] = dgu
            act_s[rows, :] = (sil * uv).astype(bf16)
            dh_acc[rows, :] += _nn(dgu, wgu)
        dw_ref[0:2] = _tn(dgu_s[...], h_v[...]).reshape(2, tf, D).astype(bf16)
        dw_ref[2] = _tn(act_s[...], do_v[...]).astype(bf16)

        @pl.when(fi == nf - 1)
        def _():
            out = pltpu.make_async_copy(dh_acc, dh_hbm, sems.at[0])
            out.start()
            out.wait()

    act_spec = pl.BlockSpec((1, T, tf), lambda f: (f, 0, 0))
    wspec = pl.BlockSpec((3, tf, D), lambda f: (0, f, 0))
    hbm = pl.BlockSpec(memory_space=pl.ANY)
    return pl.pallas_call(
        body, grid=(nf,),
        in_specs=[hbm, hbm, act_spec, act_spec, wspec],
        out_specs=[hbm, wspec],
        out_shape=(SDS((T, D), f32), SDS((3, F, D), bf16)),
        scratch_shapes=[pltpu.VMEM((T, D), bf16), pltpu.VMEM((T, D), bf16), pltpu.VMEM((T, D), f32),
                        pltpu.VMEM((T, 2 * tf), bf16), pltpu.VMEM((T, tf), bf16), pltpu.SemaphoreType.DMA((2,))],
        compiler_params=_cparams(("arbitrary",), VMEM_LIMIT_V7X), name=name)(dob, h, gate, up, w)


def _in_proj_fwd(h, wint, name):
    T = h.shape[0]
    tm = min(512, T)

    def body(h_ref, w_ref, z_ref):
        z_ref[...] = _nt(h_ref[...], w_ref[...])

    return pl.pallas_call(
        body, grid=(T // tm,),
        in_specs=[pl.BlockSpec((tm, D), lambda i: (i, 0)), pl.BlockSpec((DIN, D), lambda i: (0, 0))],
        out_specs=pl.BlockSpec((tm, DIN), lambda i: (i, 0)),
        out_shape=SDS((T, DIN), f32), name=name)(h, wint)


def _in_proj_bwd(dz, wint, h, name):
    T = h.shape[0]
    tm = min(512, T)
    nt = T // tm

    def body(dz_ref, w_ref, h_ref, dh_ref, dw_ref, acc):
        i = pl.program_id(0)
        dzb = dz_ref[...].astype(bf16)
        dh_ref[...] = _nn(dzb, w_ref[...])
        part = _tn(dzb, h_ref[...])

        @pl.when(i == 0)
        def _():
            acc[...] = part

        @pl.when(i > 0)
        def _():
            acc[...] += part

        @pl.when(i == nt - 1)
        def _():
            dw_ref[...] = acc[...].astype(bf16)

    wspec = pl.BlockSpec((DIN, D), lambda i: (0, 0))
    return pl.pallas_call(
        body, grid=(nt,),
        in_specs=[pl.BlockSpec((tm, DIN), lambda i: (i, 0)), wspec, pl.BlockSpec((tm, D), lambda i: (i, 0))],
        out_specs=[pl.BlockSpec((tm, D), lambda i: (i, 0)), wspec],
        out_shape=(SDS((T, D), f32), SDS((DIN, D), bf16)),
        scratch_shapes=[pltpu.VMEM((DIN, D), f32)],
        compiler_params=_cparams(("arbitrary",)), name=name)(dz, wint, h)


def _out_proj_fwd(ymix, wout, x, g, name):
    T = x.shape[0]
    tm = min(512, T)

    def body(y_ref, w_ref, x_ref, g_ref, o_ref, h_ref):
        o = x_ref[...] + _nn(y_ref[...], w_ref[...])
        o_ref[...] = o
        r = lax.rsqrt(jnp.mean(o * o, axis=-1, keepdims=True) + EPS)
        h_ref[...] = (o * r * g_ref[...]).astype(bf16)

    tok = pl.BlockSpec((tm, D), lambda i: (i, 0))
    return pl.pallas_call(
        body, grid=(T // tm,),
        in_specs=[pl.BlockSpec((tm, DMIX), lambda i: (i, 0)), pl.BlockSpec((DMIX, D), lambda i: (0, 0)), tok,
                  pl.BlockSpec((1, D), lambda i: (0, 0))],
        out_specs=[tok, tok], out_shape=(SDS((T, D), f32), SDS((T, D), bf16)), name=name)(ymix, wout, x, g)


def _out_proj_bwd(dxb, wout, ymix, name):
    T = dxb.shape[0]
    tm = min(512, T)
    nt = T // tm

    def body(dx_ref, w_ref, y_ref, dy_ref, dw_ref, acc):
        i = pl.program_id(0)
        dxv = dx_ref[...]
        dy_ref[...] = _nt(dxv, w_ref[...])
        part = _tn(y_ref[...], dxv)

        @pl.when(i == 0)
        def _():
            acc[...] = part

        @pl.when(i > 0)
        def _():
            acc[...] += part

        @pl.when(i == nt - 1)
        def _():
            dw_ref[...] = acc[...].astype(bf16)

    wspec = pl.BlockSpec((DMIX, D), lambda i: (0, 0))
    return pl.pallas_call(
        body, grid=(nt,),
        in_specs=[pl.BlockSpec((tm, D), lambda i: (i, 0)), wspec, pl.BlockSpec((tm, DMIX), lambda i: (i, 0))],
        out_specs=[pl.BlockSpec((tm, DMIX), lambda i: (i, 0)), wspec],
        out_shape=(SDS((T, DMIX), f32), SDS((DMIX, D), bf16)),
        scratch_shapes=[pltpu.VMEM((DMIX, D), f32)],
        compiler_params=_cparams(("arbitrary",)), name=name)(dxb, wout, ymix)


def _t5_bucket_table():
    ql = np.arange(BLK)[:, None]
    kl = np.arange(2 * BLK)[None, :]
    n = np.maximum(ql + BLK - kl, 0)
    max_exact = NBUCK // 2
    large = max_exact + (np.log(np.maximum(n, 1) / max_exact) / np.log(MAX_DISTANCE / max_exact)
                         * (NBUCK - max_exact)).astype(np.int32)
    large = np.minimum(large, NBUCK - 1)
    return np.where(n < max_exact, n, large).astype(np.int32)


def _fill_bias(bk_ref, rb_ref, bias_scr):
    bk = bk_ref[...]
    for h in range(NH):
        def step(b, acc, h=h):
            return acc + jnp.where(bk == b, rb_ref[b, h], 0.0)
        bias_scr[h] = lax.fori_loop(0, NBUCK, step, jnp.zeros((BLK, 2 * BLK), f32))


MIX_SUB = 4


class _Window:
    def __init__(self, zc_ref, zp_ref, n, s):
        self.blk = n * MIX_SUB + s
        self.first_in_step = s == 0
        self.cur = lambda a, b: zc_ref[s * BLK:(s + 1) * BLK, a:b]
        self.prev = (lambda a, b: zp_ref[:, a:b]) if s == 0 else (lambda a, b: zc_ref[(s - 1) * BLK:s * BLK, a:b])


def _attn_qkv(win, kh, qg, kg):
    kc = DATTN + HD * kh
    vc = DATTN + DKV + HD * kh
    kx = jnp.concatenate([win.prev(kc, kc + HD), win.cur(kc, kc + HD)], axis=0)
    vx = jnp.concatenate([win.prev(vc, vc + HD), win.cur(vc, vc + HD)], axis=0)
    qx = jnp.concatenate([win.cur(HD * (GQA * kh + g), HD * (GQA * kh + g + 1)) for g in range(GQA)], axis=0)
    rq = lax.rsqrt(jnp.mean(qx * qx, axis=-1, keepdims=True) + EPS)
    rk = lax.rsqrt(jnp.mean(kx * kx, axis=-1, keepdims=True) + EPS)
    qhat, khat = qx * rq, kx * rk
    return dict(qhat=qhat, khat=khat, rq=rq, rk=rk, qnb=(qhat * qg).astype(bf16), knb=(khat * kg).astype(bf16),
                vb=vx.astype(bf16))


def _attn_probs(a, kh, sk_ref, bias_scr, n):
    s = _nt(a["qnb"], a["knb"]) * SCALE + bias_scr[GQA * kh:GQA * (kh + 1)].reshape(GQA * BLK, 2 * BLK)
    row = lax.broadcasted_iota(i32, (GQA * BLK, 2 * BLK), 0) & (BLK - 1)
    col = lax.broadcasted_iota(i32, (GQA * BLK, 2 * BLK), 1)
    mask = (col > row) & (col <= row + BLK) & ((col >= BLK) | (n > 0))
    s = jnp.where(mask, s, NEG)
    ridx = lax.broadcasted_iota(i32, (GQA * BLK, 1), 0)
    sink = jnp.full((GQA * BLK, 1), sk_ref[GQA * kh + GQA - 1], f32)
    for g in range(GQA - 2, -1, -1):
        sink = jnp.where(ridx < (g + 1) * BLK, sk_ref[GQA * kh + g], sink)
    m = jnp.maximum(jnp.max(s, axis=-1, keepdims=True), sink)
    e = jnp.exp(s - m)
    den = jnp.sum(e, axis=-1, keepdims=True) + jnp.exp(sink - m)
    return e / den


POOL_STEPS = {2: (1,), 4: (1, 2), 8: (1, 2, 4), 16: (1, 2, 4, 8)}


def _pool_group(win, g, w):
    n = win.blk
    c0 = DATTN + 2 * DKV + PGD * g
    uc = win.cur(c0, c0 + PGD)
    up = jnp.where(n > 0, win.prev(c0, c0 + PGD), 0.0)
    sm = jnp.concatenate([up, uc], axis=0)
    for k in POOL_STEPS[w]:
        sm = sm + pltpu.roll(sm, k, axis=0)
    pos = n * BLK + lax.broadcasted_iota(i32, (BLK, 1), 0) + 1
    cnt = jnp.minimum(pos, w).astype(f32)
    return sm[BLK:2 * BLK] / cnt - uc, cnt


def _mix_fwd(z, qg, kg, sinks, relb, bucket, pool_w, pscale, name):
    T = z.shape[0]
    step_rows = MIX_SUB * BLK
    nsteps = T // step_rows

    def body(zc_ref, zp_ref, qg_ref, kg_ref, sk_ref, rb_ref, bk_ref, pw_ref, ps_ref, y_ref, p_ref, bias_scr, yacc):
        n = pl.program_id(0)

        @pl.when(n == 0)
        def _():
            _fill_bias(bk_ref, rb_ref, bias_scr)

        for s in range(MIX_SUB):
            win = _Window(zc_ref, zp_ref, n, s)
            rows = slice(s * BLK, (s + 1) * BLK)
            for kh in range(NKV):
                a = _attn_qkv(win, kh, qg_ref[...], kg_ref[...])
                pb = _attn_probs(a, kh, sk_ref, bias_scr, win.blk).astype(bf16)
                p_ref[s, GQA * kh:GQA * (kh + 1)] = pb.reshape(GQA, BLK, 2 * BLK)
                o = _nn(pb, a["vb"])
                for g in range(GQA):
                    hc = HD * (GQA * kh + g)
                    yacc[rows, hc:hc + HD] = o[g * BLK:(g + 1) * BLK]
            for g, w in enumerate(POOL_WINDOWS):
                pooled, _ = _pool_group(win, g, w)
                yp = _nn(pooled.astype(bf16), pw_ref[g].astype(bf16)) * ps_ref[:, PGD * g:PGD * (g + 1)]
                yacc[rows, DATTN + PGD * g:DATTN + PGD * (g + 1)] = yp
        y_ref[...] = yacc[...].astype(bf16)

    full = lambda *shape: pl.BlockSpec(shape, lambda n: (0,) * len(shape))
    smem = pl.BlockSpec(memory_space=pltpu.SMEM)
    return pl.pallas_call(
        body, grid=(nsteps,),
        in_specs=[pl.BlockSpec((step_rows, DIN), lambda n: (n, 0)),
                  pl.BlockSpec((BLK, DIN), lambda n: (jnp.maximum(n * MIX_SUB - 1, 0), 0)),
                  full(1, HD), full(1, HD), smem, smem, full(BLK, 2 * BLK),
                  full(len(POOL_WINDOWS), PGD, PGD), full(1, DPOOL)],
        out_specs=[pl.BlockSpec((step_rows, DMIX), lambda n: (n, 0)),
                   pl.BlockSpec((MIX_SUB, NH, BLK, 2 * BLK), lambda n: (n, 0, 0, 0))],
        out_shape=(SDS((T, DMIX), bf16), SDS((T // BLK, NH, BLK, 2 * BLK), bf16)),
        scratch_shapes=[pltpu.VMEM((NH, BLK, 2 * BLK), f32), pltpu.VMEM((step_rows, DMIX), f32)],
        compiler_params=_cparams(("arbitrary",)), name=name)(z, z, qg, kg, sinks, relb, bucket, pool_w, pscale)


def _mix_bwd(z, dy, probs, qg, kg, relb, bucket, pool_w, pscale, name):
    T = z.shape[0]
    step_rows = MIX_SUB * BLK
    nsteps = T // step_rows

    def body(zc_ref, zp_ref, dy_ref, p_ref, qg_ref, kg_ref, bk_ref, pw_ref, ps_ref,
             dz_ref, dqg_ref, dkg_ref, dsk_ref, drb_ref, dpw_ref, dps_ref, dbias_scr):
        n = pl.program_id(0)

        @pl.when(n == 0)
        def _():
            dbias_scr[...] = jnp.zeros_like(dbias_scr)
            dqg_ref[...] = jnp.zeros_like(dqg_ref)
            dkg_ref[...] = jnp.zeros_like(dkg_ref)
            dpw_ref[...] = jnp.zeros_like(dpw_ref)
            dps_ref[...] = jnp.zeros_like(dps_ref)

        qg, kg = qg_ref[...], kg_ref[...]
        for s in range(MIX_SUB):
            win = _Window(zc_ref, zp_ref, n, s)
            blk = win.blk
            rows = pl.ds(pl.multiple_of(blk * BLK, BLK), BLK)
            prow = pl.ds(pl.multiple_of(jnp.maximum(blk - 1, 0) * BLK, BLK), BLK)
            dyr = slice(s * BLK, (s + 1) * BLK)

            def into_prev(fn, s=s):
                if s == 0:
                    pl.when(n > 0)(fn)
                else:
                    fn()

            for kh in range(NKV):
                a = _attn_qkv(win, kh, qg, kg)
                pb = p_ref[s, GQA * kh:GQA * (kh + 1)].reshape(GQA * BLK, 2 * BLK)
                p = pb.astype(f32)
                do = jnp.concatenate([dy_ref[dyr, HD * (GQA * kh + g):HD * (GQA * kh + g + 1)] for g in range(GQA)],
                                     axis=0).astype(bf16)
                dv = _tn(pb, do)
                dp = _nt(do, a["vb"])
                delta = jnp.sum(p * dp, axis=-1, keepdims=True)
                ds = p * (dp - delta)
                for g in range(GQA):
                    dbias_scr[GQA * kh + g] += ds[g * BLK:(g + 1) * BLK]
                dsb = ds.astype(bf16)
                dqn = _nn(dsb, a["knb"]) * SCALE
                dkn = _tn(dsb, a["qnb"]) * SCALE
                qhat, khat = a["qhat"], a["khat"]
                dqg_ref[...] += jnp.sum(dqn * qhat, axis=0, keepdims=True)
                dkg_ref[...] += jnp.sum(dkn * khat, axis=0, keepdims=True)
                dqh = dqn * qg
                dq = a["rq"] * (dqh - qhat * jnp.mean(dqh * qhat, axis=-1, keepdims=True))
                dkh = dkn * kg
                dk = a["rk"] * (dkh - khat * jnp.mean(dkh * khat, axis=-1, keepdims=True))
                kc = DATTN + HD * kh
                vc = DATTN + DKV + HD * kh
                for g in range(GQA):
                    hc = HD * (GQA * kh + g)
                    dz_ref[rows, hc:hc + HD] = dq[g * BLK:(g + 1) * BLK]
                dz_ref[rows, kc:kc + HD] = dk[BLK:2 * BLK]
                dz_ref[rows, vc:vc + HD] = dv[BLK:2 * BLK]

                def kv_prev(dk=dk, dv=dv, kc=kc, vc=vc, prow=prow):
                    dz_ref[prow, kc:kc + HD] += dk[0:BLK]
                    dz_ref[prow, vc:vc + HD] += dv[0:BLK]

                into_prev(kv_prev)

            for g, w in enumerate(POOL_WINDOWS):
                c0 = DATTN + 2 * DKV + PGD * g
                pooled, cnt = _pool_group(win, g, w)
                pb = pooled.astype(bf16)
                wb = pw_ref[g].astype(bf16)
                dyp = dy_ref[dyr, DATTN + PGD * g:DATTN + PGD * (g + 1)]
                ypre = _nn(pb, wb)
                dps_ref[:, PGD * g:PGD * (g + 1)] += jnp.sum(dyp * ypre, axis=0, keepdims=True)
                dyg = (dyp * ps_ref[:, PGD * g:PGD * (g + 1)]).astype(bf16)
                dpw_ref[g] += _tn(pb, dyg)
                dpooled = _nt(dyg, wb)
                due = jnp.concatenate([jnp.zeros((BLK, PGD), f32), dpooled / cnt], axis=0)
                for k in POOL_STEPS[w]:
                    due = due + pltpu.roll(due, 2 * BLK - k, axis=0)
                dz_ref[rows, c0:c0 + PGD] = due[BLK:2 * BLK] - dpooled

                def pool_prev(due=due, c0=c0, prow=prow):
                    dz_ref[prow, c0:c0 + PGD] += due[0:BLK]

                into_prev(pool_prev)

        @pl.when(n == nsteps - 1)
        def _():
            bk = bk_ref[...]
            ri = lax.broadcasted_iota(i32, (NBUCK, NH), 0)
            ci = lax.broadcasted_iota(i32, (NBUCK, NH), 1)

            def step(b, acc):
                for h in range(NH):
                    sel = jnp.where(bk == b, dbias_scr[h], 0.0)
                    tot = jnp.sum(jnp.sum(sel, axis=1, keepdims=True), axis=0, keepdims=True)
                    acc = acc + jnp.where((ri == b) & (ci == h), tot, 0.0)
                return acc

            drb_ref[...] = lax.fori_loop(0, NBUCK, step, jnp.zeros((NBUCK, NH), f32))
            lane = lax.broadcasted_iota(i32, (1, 128), 1)
            dsk = jnp.zeros((1, 128), f32)
            for h in range(NH):
                tot = jnp.sum(jnp.sum(dbias_scr[h], axis=1, keepdims=True), axis=0, keepdims=True)
                dsk = dsk - jnp.where(lane == h, tot, 0.0)
            dsk_ref[...] = dsk

    full = lambda *shape: pl.BlockSpec(shape, lambda n: (0,) * len(shape))
    npg = len(POOL_WINDOWS)
    return pl.pallas_call(
        body, grid=(nsteps,),
        in_specs=[pl.BlockSpec((step_rows, DIN), lambda n: (n, 0)),
                  pl.BlockSpec((BLK, DIN), lambda n: (jnp.maximum(n * MIX_SUB - 1, 0), 0)),
                  pl.BlockSpec((step_rows, DMIX), lambda n: (n, 0)),
                  pl.BlockSpec((MIX_SUB, NH, BLK, 2 * BLK), lambda n: (n, 0, 0, 0)),
                  full(1, HD), full(1, HD), full(BLK, 2 * BLK), full(npg, PGD, PGD), full(1, DPOOL)],
        out_specs=[full(T, DIN), full(1, HD), full(1, HD), full(1, 128), full(NBUCK, NH),
                   full(npg, PGD, PGD), full(1, DPOOL)],
        out_shape=(SDS((T, DIN), f32), SDS((1, HD), f32), SDS((1, HD), f32), SDS((1, 128), f32),
                   SDS((NBUCK, NH), f32), SDS((npg, PGD, PGD), f32), SDS((1, DPOOL), f32)),
        scratch_shapes=[pltpu.VMEM((NH, BLK, 2 * BLK), f32)],
        compiler_params=_cparams(("arbitrary",), VMEM_LIMIT_V7X),
        name=name)(z, z, dy, probs, qg, kg, bucket, pool_w, pscale)


class _LocalWeights:
    def __init__(self, w1, wint, wout, w2):
        self.w1, self.wint, self.wout, self.w2 = w1, wint, wout, w2

    def ffn1(self):
        return self.w1

    def after_ffn1(self, gain, x1):
        return gain

    def mix(self, after):
        return self.wint, self.wout

    def before_out_proj(self, wout, after):
        return wout

    def ffn2(self, after):
        return self.w2

    def mix_ffn2_grads_ready(self, dwint, dwout, dw2, dh2):
        self.grads_rest = (dwint, dwout, dw2)
        return dh2

    def before_ffn1_bwd(self, dx1b):
        return dx1b


def _local_step(x, target, weights, g1, gm, g3, qg, kg, sinks, relb, pool_w, pscale):
    bucket = jnp.asarray(_t5_bucket_table())
    sk = sinks.reshape(NH)
    w1 = weights.ffn1()
    h1 = _norm_fwd(x, g1, "norm1_fwd")
    x1, gate1, up1 = _ffn_fwd(h1, w1, x, None, "ffn1_fwd")
    h2 = _norm_fwd(x1, weights.after_ffn1(gm, x1), "norm2_fwd")
    wint, wout = weights.mix(h2)
    z = _in_proj_fwd(h2, wint, "in_proj_fwd")
    ymix, probs = _mix_fwd(z, qg, kg, sk, relb, bucket, pool_w, pscale, "mix_fwd")
    wout = weights.before_out_proj(wout, ymix)
    x2, h3 = _out_proj_fwd(ymix, wout, x1, g3, "out_proj_fwd")
    w2 = weights.ffn2(h3)
    dy, gate2, up2, dyb, loss_lanes = _ffn_fwd(h3, w2, x2, target, "ffn2_fwd")

    dh3, dw2 = _ffn_bwd(dyb, h3, gate2, up2, w2, "ffn2_bwd")
    dx2, dx2b, dg3 = _norm_bwd(dh3, x2, g3, dy, 1.0, "norm3_bwd")
    dymix, dwout = _out_proj_bwd(dx2b, wout, ymix, "out_proj_bwd")
    dz, dqg, dkg, dsk, drb, dpw, dps = _mix_bwd(z, dymix, probs, qg, kg, relb, bucket, pool_w, pscale, "mix_bwd")
    dh2, dwint = _in_proj_bwd(dz, wint, h2, "in_proj_bwd")
    dh2 = weights.mix_ffn2_grads_ready(dwint, dwout, dw2, dh2)
    dx1, dx1b, dgm = _norm_bwd(dh2, x1, gm, dx2, 0.5, "norm2_bwd")
    dx1b = weights.before_ffn1_bwd(dx1b)
    dh1, dw1 = _ffn_bwd(dx1b, h1, gate1, up1, w1, "ffn1_bwd")
    gx, _, dg1 = _norm_bwd(dh1, x, g1, dx1, 1.0, "norm1_bwd")
    small = dict(ffn1_norm=dg1, mix_norm=dgm, ffn2_norm=dg3, pool_scale=dps, q_norm=dqg, k_norm=dkg,
                 attn_sinks=dsk[:, :NH], rel_bias=drb, pool_w=dpw, loss=loss_lanes)
    return gx, (dw1, dwint, dwout, dw2), small


SMALL_NAMES = ("ffn1_norm", "mix_norm", "ffn2_norm", "pool_scale", "q_norm", "k_norm", "attn_sinks", "rel_bias",
               "pool_w", "loss")
SMALL_SHAPES = dict(ffn1_norm=(1, D), mix_norm=(1, D), ffn2_norm=(1, D), pool_scale=(1, DPOOL), q_norm=(1, HD),
                    k_norm=(1, HD), attn_sinks=(1, NH), rel_bias=(NBUCK, NH),
                    pool_w=(1, len(POOL_WINDOWS), PGD, PGD), loss=(1, 128))


def _small_rows(name):
    return -(-int(np.prod(SMALL_SHAPES[name])) // 128)


SMALL_OFF = {}
_r = 0
for _n in SMALL_NAMES:
    SMALL_OFF[_n] = _r
    _r += _small_rows(_n)
SMALL_ROWS = -(-_r // 8) * 8
LOSS_ROW = SMALL_OFF["loss"]


def _pack_small(vals):
    parts = []
    for n in SMALL_NAMES:
        size = _small_rows(n) * 128
        if n in vals:
            flat = vals[n].astype(f32).reshape(-1)
            parts.append(jnp.pad(flat, (0, size - flat.shape[0])))
        else:
            parts.append(jnp.zeros((size,), f32))
    flat = jnp.concatenate(parts)
    flat = jnp.pad(flat, (0, SMALL_ROWS * 128 - flat.shape[0]))
    return flat.reshape(SMALL_ROWS, 128)


def _unpack_small(packed, name):
    size = int(np.prod(SMALL_SHAPES[name]))
    r0 = SMALL_OFF[name]
    return packed[r0:r0 + _small_rows(name)].reshape(-1)[:size].reshape(SMALL_SHAPES[name])


def _position():
    return lax.axis_index("x"), lax.axis_index("y"), lax.axis_index("c")


def _dev_index(x, y, c):
    return 4 * x + 2 * y + c


G1_PIECES, MIX_PIECES, F2_PIECES = (0, 1, 2), (3, 4), (5, 6, 7)


def _group_rows(pieces):
    return sum(PIECE_ROWS[k] for k in pieces)


def _shard_piece(s_ref, k):
    return s_ref.at[pl.ds(PIECE_OFF[k], PIECE_ROWS[k]), :]


def _shard_group(s_ref, pieces):
    return s_ref.at[pl.ds(PIECE_OFF[pieces[0]], _group_rows(pieces)), :]


def _weight_pieces(w1_ref=None, wi_ref=None, wo_ref=None, w2_ref=None):
    arrs = {}
    if w1_ref is not None:
        arrs.update({0: w1_ref.at[0], 1: w1_ref.at[1], 2: w1_ref.at[2]})
    if wi_ref is not None:
        arrs[3] = wi_ref
    if wo_ref is not None:
        arrs[4] = wo_ref
    if w2_ref is not None:
        arrs.update({5: w2_ref.at[0], 6: w2_ref.at[1], 7: w2_ref.at[2]})
    return arrs


def _block_rows(arrs, k, dev):
    r = PIECE_ROWS[k]
    return arrs[k].at[pl.ds(pl.multiple_of(_dev_index(*dev) * r, 16), r), :]


def _all_gather_ffn1(shard):
    pieces = G1_PIECES
    half = FS // 2
    SIB, X0, X1, Y0, Y1, RELAY_Y, RELAY_X, ON_X, ON_Y, ON_D0, ON_D1 = range(11)

    def body(s_ref, w1_ref, send_sems, recv_sems, local_sem):
        x, y, c = _position()
        me, sib = (x, y, c), (x, y, 1 - c)
        xn, yn, dg = (1 - x, y, c), (x, 1 - y, c), (1 - x, 1 - y, c)
        arrs = _weight_pieces(w1_ref=w1_ref)

        def rows_of(k, block, hf):
            r = PIECE_ROWS[k]
            start, size = (0, r) if hf is None else (hf * half, half)
            return arrs[k].at[pl.ds(pl.multiple_of(_dev_index(*block) * r + start, 16), size), :]

        def copies(rel, block, hf, to, from_shard=False):
            def src(k):
                if not from_shard:
                    return rows_of(k, block, hf)
                start, size = (0, PIECE_ROWS[k]) if hf is None else (hf * half, half)
                return s_ref.at[pl.ds(PIECE_OFF[k] + start, size), :]
            return [pltpu.make_async_remote_copy(
                src_ref=src(k), dst_ref=rows_of(k, block, hf), send_sem=send_sems.at[rel], recv_sem=recv_sems.at[rel],
                device_id=to, device_id_type=MESH) for k in pieces]

        def waiter(rel, hf):
            nrows = len(pieces) * (FS if hf is None else half)
            grp = s_ref.at[pl.ds(0, nrows), :]
            return pltpu.make_async_remote_copy(src_ref=grp, dst_ref=grp, send_sem=send_sems.at[rel],
                                                recv_sem=recv_sems.at[rel], device_id=me, device_id_type=MESH)

        def start(cps):
            for cp in cps:
                cp.start()

        mine = [pltpu.make_async_copy(_shard_piece(s_ref, k), _block_rows(arrs, k, me), local_sem) for k in pieces]
        start(mine)
        start(copies(SIB, me, None, sib, True))
        start(copies(X0, me, 0, xn, True))
        start(copies(Y1, me, 1, yn, True))
        start(copies(X1, me, 1, xn, True))
        start(copies(Y0, me, 0, yn, True))
        waiter(X0, 0).wait_recv()
        start(copies(RELAY_Y, xn, 0, yn))
        waiter(Y1, 1).wait_recv()
        start(copies(RELAY_X, yn, 1, xn))
        waiter(X1, 1).wait_recv()
        start(copies(ON_X, xn, None, sib))
        waiter(Y0, 0).wait_recv()
        start(copies(ON_Y, yn, None, sib))
        waiter(RELAY_Y, 0).wait_recv()
        start(copies(ON_D0, dg, 0, sib))
        waiter(RELAY_X, 1).wait_recv()
        start(copies(ON_D1, dg, 1, sib))
        waiter(SIB, None).wait_recv()
        waiter(ON_X, None).wait_recv()
        waiter(ON_Y, None).wait_recv()
        waiter(ON_D0, 0).wait_recv()
        waiter(ON_D1, 1).wait_recv()
        for rel, hf in ((SIB, None), (X0, 0), (X1, 1), (Y0, 0), (Y1, 1), (RELAY_Y, 0), (RELAY_X, 1),
                        (ON_X, None), (ON_Y, None), (ON_D0, 0), (ON_D1, 1)):
            waiter(rel, hf).wait_send()
        grp = _shard_group(s_ref, pieces)
        pltpu.make_async_copy(grp, grp, local_sem).wait()

    hbm = pl.BlockSpec(memory_space=pl.ANY)
    return pl.pallas_call(
        body, in_specs=[hbm], out_specs=hbm, out_shape=SDS((3, F, D), bf16),
        scratch_shapes=[pltpu.SemaphoreType.DMA((11,)), pltpu.SemaphoreType.DMA((11,)), pltpu.SemaphoreType.DMA],
        compiler_params=pltpu.CompilerParams(has_side_effects=True),
        name="all_gather_ffn1")(shard)


HBM_SPEC = pl.BlockSpec(memory_space=pltpu.HBM)
SEM_SPEC = pl.BlockSpec(memory_space=pltpu.SEMAPHORE)
ANY_SPEC = pl.BlockSpec(memory_space=pl.ANY)
SPLIT_EFFECT = pltpu.SideEffectType.DATAFLOW_SIDE_EFFECTING


def _in_hbm(a):
    return pltpu.with_memory_space_constraint(a, pltpu.HBM)


def _hbm_like(a):
    return pltpu.HBM(a.shape, a.dtype)


def _place_own_rows(shard):
    pieces = MIX_PIECES + F2_PIECES

    def body(s_ref, wi_ref, wo_ref, w2_ref, buf, sems):
        x, y, c = _position()
        arrs = _weight_pieces(wi_ref=wi_ref, wo_ref=wo_ref, w2_ref=w2_ref)
        grp = _shard_group(s_ref, pieces)
        load = pltpu.make_async_copy(grp, buf, sems.at[0])
        load.start()
        load.wait()
        base = PIECE_OFF[pieces[0]]
        for k in pieces:
            pltpu.make_async_copy(buf.at[pl.ds(PIECE_OFF[k] - base, PIECE_ROWS[k]), :],
                                  _block_rows(arrs, k, (x, y, c)), sems.at[1]).start()
        pltpu.make_async_copy(grp, buf, sems.at[1]).wait()

    return pl.pallas_call(
        body, in_specs=[ANY_SPEC], out_specs=[ANY_SPEC] * 3,
        out_shape=(SDS((DIN, D), bf16), SDS((DMIX, D), bf16), SDS((3, F, D), bf16)),
        scratch_shapes=[pltpu.VMEM((_group_rows(pieces), D), bf16), pltpu.SemaphoreType.DMA((2,))],
        name="place_own_rows")(shard)


def _xor_peer(x, y, c, k):
    return (x ^ (k >> 2), y ^ ((k >> 1) & 1), c ^ (k & 1))


def _gather_rest_start(shard, wi, wo, w2, w1):
    def body(s_ref, wi_ref, wo_ref, w2_ref, w1_ref,
             ssem_m, rsem_m0, rsem_m, ssem_f, rsem_f0, rsem_f, s_o, wi_o, wo_o, w2_o, w1_o):
        x, y, c = _position()
        me, sib = (x, y, c), (x, y, 1 - c)
        chips = [(1 - x, y), (x, 1 - y), (1 - x, 1 - y)]
        arrs = _weight_pieces(wi_ref=wi_ref, wo_ref=wo_ref, w2_ref=w2_ref)
        for pieces, ssem, rsem0, rsem in ((MIX_PIECES, ssem_m, rsem_m0, rsem_m), (F2_PIECES, ssem_f, rsem_f0, rsem_f)):
            for p in pieces:
                pltpu.make_async_remote_copy(
                    src_ref=_shard_piece(s_ref, p), dst_ref=_block_rows(arrs, p, me), send_sem=ssem.at[0],
                    recv_sem=rsem0, device_id=sib, device_id_type=MESH).start()
            for j, chip in enumerate(chips):
                for p in pieces:
                    pltpu.make_async_remote_copy(
                        src_ref=_shard_piece(s_ref, p), dst_ref=_block_rows(arrs, p, me), send_sem=ssem.at[1 + j],
                        recv_sem=rsem.at[j], device_id=(*chip, c), device_id_type=MESH).start()

    dma = pltpu.SemaphoreType.DMA
    return pl.pallas_call(
        body, name="gather_rest_start",
        out_shape=(dma((4,)), dma(()), dma((3,)), dma((4,)), dma(()), dma((3,)),
                   _hbm_like(shard), _hbm_like(wi), _hbm_like(wo), _hbm_like(w2), _hbm_like(w1)),
        in_specs=(HBM_SPEC,) * 5, out_specs=(SEM_SPEC,) * 6 + (HBM_SPEC,) * 5,
        input_output_aliases={0: 6, 1: 7, 2: 8, 3: 9, 4: 10},
        compiler_params=pltpu.CompilerParams(has_side_effects=SPLIT_EFFECT),
    )(_in_hbm(shard), _in_hbm(wi), _in_hbm(wo), _in_hbm(w2), _in_hbm(w1))


def _gather_mix_pass_on(rsem_m, wi, wo, thru, after):
    def body(wi_ref, wo_ref, thru_ref, rsem, after_ref, fsend, frecv, wi_o, wo_o, thru_o):
        x, y, c = _position()
        sib = (x, y, 1 - c)
        arrs = _weight_pieces(wi_ref=wi_ref, wo_ref=wo_ref)
        both = wi_ref.at[pl.ds(0, _group_rows(MIX_PIECES)), :]
        for j, chip in enumerate([(1 - x, y), (x, 1 - y), (1 - x, 1 - y)]):
            pltpu.make_async_remote_copy(src_ref=both, dst_ref=both, send_sem=fsend.at[j], recv_sem=rsem.at[j],
                                         device_id=(x, y, c), device_id_type=MESH).wait_recv()
            for p in MIX_PIECES:
                rows = _block_rows(arrs, p, (*chip, c))
                pltpu.make_async_remote_copy(src_ref=rows, dst_ref=rows, send_sem=fsend.at[j], recv_sem=frecv.at[j],
                                             device_id=sib, device_id_type=MESH).start()

    dma = pltpu.SemaphoreType.DMA
    return pl.pallas_call(
        body, name="gather_mix_pass_on",
        out_shape=(dma((3,)), dma((3,)), _hbm_like(wi), _hbm_like(wo), _hbm_like(thru)),
        in_specs=(HBM_SPEC, HBM_SPEC, HBM_SPEC, SEM_SPEC, ANY_SPEC), out_specs=(SEM_SPEC, SEM_SPEC) + (HBM_SPEC,) * 3,
        input_output_aliases={0: 2, 1: 3, 2: 4},
        compiler_params=pltpu.CompilerParams(has_side_effects=SPLIT_EFFECT),
    )(wi, wo, _in_hbm(thru), rsem_m, after)


def _gather_mix_wait(ssem_m, rsem_m0, fsend, frecv, shard, wi, wo, after):
    def body(s_ref, wi_ref, wo_ref, ssem, rsem0, fs, fr, after_ref, s_o, wi_o, wo_o):
        x, y, c = _position()
        grp = _shard_group(s_ref, MIX_PIECES)

        def waiter(send_sem, recv_sem):
            return pltpu.make_async_remote_copy(src_ref=grp, dst_ref=grp, send_sem=send_sem, recv_sem=recv_sem,
                                                device_id=(x, y, c), device_id_type=MESH)

        waiter(ssem.at[0], rsem0).wait_recv()
        for j in range(3):
            waiter(fs.at[j], fr.at[j]).wait_recv()
        for rel in range(4):
            waiter(ssem.at[rel], rsem0).wait_send()
        for j in range(3):
            waiter(fs.at[j], fr.at[j]).wait_send()

    return pl.pallas_call(
        body, name="gather_mix_wait", out_shape=(_hbm_like(shard), _hbm_like(wi), _hbm_like(wo)),
        in_specs=(HBM_SPEC,) * 3 + (SEM_SPEC,) * 4 + (ANY_SPEC,), out_specs=(HBM_SPEC,) * 3,
        input_output_aliases={0: 0, 1: 1, 2: 2},
        compiler_params=pltpu.CompilerParams(has_side_effects=SPLIT_EFFECT),
    )(shard, wi, wo, ssem_m, rsem_m0, fsend, frecv, after)


def _gather_ffn2_pass_on(rsem_f, w2, wo, after):
    def body(w2_ref, wo_ref, rsem, after_ref, fsend, frecv, w2_o, wo_o):
        x, y, c = _position()
        sib = (x, y, 1 - c)
        chips = [(1 - x, y), (x, 1 - y), (1 - x, 1 - y)]
        arrs = _weight_pieces(w2_ref=w2_ref)
        three = w2_ref.at[0, pl.ds(0, _group_rows(F2_PIECES)), :]
        for j, chip in enumerate(chips):
            pltpu.make_async_remote_copy(src_ref=three, dst_ref=three, send_sem=fsend.at[j], recv_sem=rsem.at[j],
                                         device_id=(x, y, c), device_id_type=MESH).wait_recv()
            for p in F2_PIECES:
                rows = _block_rows(arrs, p, (*chip, c))
                pltpu.make_async_remote_copy(src_ref=rows, dst_ref=rows, send_sem=fsend.at[j], recv_sem=frecv.at[j],
                                             device_id=sib, device_id_type=MESH).start()

    dma = pltpu.SemaphoreType.DMA
    return pl.pallas_call(
        body, name="gather_ffn2_pass_on", out_shape=(dma((3,)), dma((3,)), _hbm_like(w2), _hbm_like(wo)),
        in_specs=(HBM_SPEC, HBM_SPEC, SEM_SPEC, ANY_SPEC), out_specs=(SEM_SPEC, SEM_SPEC, HBM_SPEC, HBM_SPEC),
        input_output_aliases={0: 2, 1: 3},
        compiler_params=pltpu.CompilerParams(has_side_effects=SPLIT_EFFECT),
    )(w2, wo, rsem_f, after)


def _gather_ffn2_wait(ssem_f, rsem_f0, fsend, frecv, shard, w2, after):
    def body(s_ref, w2_ref, ssem, rsem0, fs, fr, after_ref, w2_o):
        x, y, c = _position()
        grp = _shard_group(s_ref, F2_PIECES)

        def waiter(send_sem, recv_sem):
            return pltpu.make_async_remote_copy(src_ref=grp, dst_ref=grp, send_sem=send_sem, recv_sem=recv_sem,
                                                device_id=(x, y, c), device_id_type=MESH)

        waiter(ssem.at[0], rsem0).wait_recv()
        for j in range(3):
            waiter(fs.at[j], fr.at[j]).wait_recv()
        for rel in range(4):
            waiter(ssem.at[rel], rsem0).wait_send()
        for j in range(3):
            waiter(fs.at[j], fr.at[j]).wait_send()

    return pl.pallas_call(
        body, name="gather_ffn2_wait", out_shape=_hbm_like(w2),
        in_specs=(HBM_SPEC, HBM_SPEC, SEM_SPEC, SEM_SPEC, SEM_SPEC, SEM_SPEC, ANY_SPEC), out_specs=HBM_SPEC,
        input_output_aliases={1: 0},
        compiler_params=pltpu.CompilerParams(has_side_effects=SPLIT_EFFECT),
    )(shard, w2, ssem_f, rsem_f0, fsend, frecv, after)


class _GatheredWeights(_LocalWeights):
    def __init__(self, shard):
        w1 = _all_gather_ffn1(shard)
        wi, wo, w2 = _place_own_rows(shard)
        (self.ssem_m, self.rsem_m0, self.rsem_m, self.ssem_f, self.rsem_f0, self.rsem_f,
         self.shard, self.wi, self.wo, self.w2_part, self.w1) = _gather_rest_start(shard, wi, wo, w2, w1)

    def after_ffn1(self, gain, x1):
        self.fsend_m, self.frecv_m, self.wi, self.wo, gain = _gather_mix_pass_on(self.rsem_m, self.wi, self.wo, gain, x1)
        return gain

    def mix(self, after):
        self.shard, wint, wout = _gather_mix_wait(self.ssem_m, self.rsem_m0, self.fsend_m, self.frecv_m, self.shard,
                                                  self.wi, self.wo, after)
        return wint, wout

    def before_out_proj(self, wout, after):
        self.fsend, self.frecv, self.w2_part, wout = _gather_ffn2_pass_on(self.rsem_f, self.w2_part, wout, after)
        return wout

    def ffn2(self, after):
        return _gather_ffn2_wait(self.ssem_f, self.rsem_f0, self.fsend, self.frecv, self.shard, self.w2_part, after)

    def mix_ffn2_grads_ready(self, dwint, dwout, dw2, dh2):
        rx1 = lax.empty((4, RSA_ROWS, D), bf16)
        self.sa, self.ra, dwint, dwout, dw2, rx1, dh2 = _rsa_level1_start(dwint, dwout, dw2, rx1, dh2)
        self.level1 = (dwint, dwout, dw2, rx1)
        return dh2

    def before_ffn1_bwd(self, dx1b):
        dwint, dwout, dw2, rx1 = _rsa_level1_wait(self.sa, self.ra, *self.level1, dx1b)
        tx, self.acc = _rsa_chip_sums(dwint, dwout, dw2, rx1)
        rx2 = lax.empty((3, RSA_ROWS, D), bf16)
        self.sb, self.rb, self.tx, self.rx2, dx1b = _rsa_level2_start(tx, rx2, dx1b)
        return dx1b

    def mix_ffn2_grads_total(self, after):
        rx2 = _rsa_level2_wait(self.sb, self.rb, self.tx, self.rx2, after)
        return _rsa_total(self.acc, rx2)


def _reduce_scatter_ffn1(dw1, small_packed):
    pieces = G1_PIECES
    half = FS // 2
    hrows = len(pieces) * half
    nrows = 2 * hrows
    FROM_X_OWN0, X_RELAY, FROM_Y_OWN1, Y_RELAY, FROM_Y_OWN0, FROM_X_OWN1 = range(6)

    def body(d1_ref, p_ref, red_ref, rx1_ref, rxx_ref, rxy_ref, relx_ref, rely_ref, tot_ref,
             own_buf, rx_buf, tx1, tx2, tx3, acc, sa, ra, sb, rb, lsem, pair, chips, small_send, small_recv):
        x, y, c = _position()
        me, sib = (x, y, c), (x, y, 1 - c)
        xn, yn = (1 - x, y, c), (x, 1 - y, c)
        rel_chips = [(x, y), (1 - x, y), (x, 1 - y), (1 - x, 1 - y)]
        srcs = _weight_pieces(w1_ref=d1_ref)

        my_chip = 2 * x + y
        pair[c] = p_ref[...]
        swap = pltpu.make_async_remote_copy(
            src_ref=p_ref, dst_ref=pair.at[c], send_sem=small_send.at[0], recv_sem=small_recv.at[0],
            device_id=sib, device_id_type=MESH)
        swap.start()
        small = [pltpu.make_async_remote_copy(
            src_ref=chips.at[my_chip], dst_ref=chips.at[my_chip], send_sem=small_send.at[j], recv_sem=small_recv.at[j],
            device_id=(*rel_chips[j], c), device_id_type=MESH) for j in (1, 2, 3)]

        def part(k, dev, hf):
            r = PIECE_ROWS[k]
            return srcs[k].at[pl.ds(pl.multiple_of(_dev_index(*dev) * r + hf * half, 16), half), :]

        def slot(ref, k, hf):
            return ref.at[pl.ds(hf * hrows + k * half, half), :]

        halves = [(k, hf) for hf in (0, 1) for k in pieces]

        for j in (3, 1, 2, 0):
            for k, hf in halves:
                pltpu.make_async_remote_copy(
                    src_ref=part(k, (*rel_chips[j], 1 - c), hf), dst_ref=slot(rx1_ref.at[j], k, hf),
                    send_sem=sa.at[j], recv_sem=ra.at[j], device_id=sib, device_id_type=MESH).start()

        def wait_a(j):
            return pltpu.make_async_remote_copy(src_ref=rx1_ref.at[j], dst_ref=rx1_ref.at[j], send_sem=sa.at[j],
                                                recv_sem=ra.at[j], device_id=me, device_id_type=MESH)

        def ici(rel, src, dst, to):
            return pltpu.make_async_remote_copy(src_ref=src, dst_ref=dst, send_sem=sb.at[rel], recv_sem=rb.at[rel],
                                                device_id=to, device_id_type=MESH)

        first, second = pl.ds(0, hrows), pl.ds(hrows, hrows)
        sends = {
            X_RELAY: ici(X_RELAY, tx3.at[first, :], relx_ref, xn),
            Y_RELAY: ici(Y_RELAY, tx3.at[second, :], rely_ref, yn),
            FROM_X_OWN0: ici(FROM_X_OWN0, tx1.at[first, :], rxx_ref.at[first, :], xn),
            FROM_Y_OWN1: ici(FROM_Y_OWN1, tx2.at[second, :], rxy_ref.at[second, :], yn),
            FROM_Y_OWN0: ici(FROM_Y_OWN0, tx2.at[first, :], rxy_ref.at[first, :], yn),
            FROM_X_OWN1: ici(FROM_X_OWN1, tx1.at[second, :], rxx_ref.at[second, :], xn),
        }

        swap.wait_recv()
        chips[my_chip] = pair[0] + pair[1]
        for cp in small:
            cp.start()

        def chip_sum(j, dst):
            loads = [pltpu.make_async_copy(part(k, (*rel_chips[j], c), hf), slot(own_buf, k, hf), lsem.at[0])
                     for k, hf in halves]
            for cp in loads:
                cp.start()
            wait_a(j).wait_recv()
            got = pltpu.make_async_copy(rx1_ref.at[j], rx_buf, lsem.at[1])
            got.start()
            pltpu.make_async_copy(rx_buf, rx_buf, lsem.at[0]).wait()
            got.wait()

            def add(i, carry):
                rows = pl.ds(pl.multiple_of(i * half, 16), half)
                tot = own_buf[rows, :].astype(f32) + rx_buf[rows, :].astype(f32)
                dst[rows, :] = tot.astype(dst.dtype)
                return carry

            lax.fori_loop(0, nrows // half, add, 0)

        def add_landed(landed, dst, rows0, nrows_):
            got = pltpu.make_async_copy(landed, rx_buf.at[pl.ds(0, nrows_), :], lsem.at[1])
            got.start()
            got.wait()

            def add(i, carry):
                src_rows = pl.ds(pl.multiple_of(i * half, 16), half)
                dst_rows = pl.ds(pl.multiple_of(rows0 + i * half, 16), half)
                dst[dst_rows, :] = (dst[dst_rows, :].astype(f32) + rx_buf[src_rows, :].astype(f32)).astype(dst.dtype)
                return carry

            lax.fori_loop(0, nrows_ // half, add, 0)

        chip_sum(3, tx3)
        sends[X_RELAY].start()
        sends[Y_RELAY].start()
        chip_sum(1, tx1)
        sends[FROM_X_OWN0].start()
        chip_sum(2, tx2)
        sends[FROM_Y_OWN1].start()
        sends[X_RELAY].wait_recv()
        add_landed(relx_ref, tx2, 0, hrows)
        sends[FROM_Y_OWN0].start()
        sends[Y_RELAY].wait_recv()
        add_landed(rely_ref, tx1, hrows, hrows)
        sends[FROM_X_OWN1].start()
        chip_sum(0, acc)
        sends[FROM_X_OWN0].wait_recv()
        sends[FROM_X_OWN1].wait_recv()
        add_landed(rxx_ref, acc, 0, nrows)
        sends[FROM_Y_OWN0].wait_recv()
        sends[FROM_Y_OWN1].wait_recv()
        add_landed(rxy_ref, acc, 0, nrows)
        outs = [pltpu.make_async_copy(slot(acc, k, hf), red_ref.at[pl.ds(PIECE_OFF[k] + hf * half, half), :], lsem.at[0])
                for k, hf in halves]
        for cp in outs:
            cp.start()
        pltpu.make_async_copy(acc, acc, lsem.at[0]).wait()
        for cp in small:
            cp.wait_recv()
        tot = (chips[0] + chips[1]) + (chips[2] + chips[3])
        tot_ref[...] = tot
        loss = jnp.sum(tot[LOSS_ROW:LOSS_ROW + 1, :], axis=-1, keepdims=True)
        tot_ref[LOSS_ROW:LOSS_ROW + 1, :] = jnp.broadcast_to(loss, (1, 128))
        for j in range(4):
            wait_a(j).wait_send()
        for cp in sends.values():
            cp.wait_send()
        swap.wait_send()
        for cp in small:
            cp.wait_send()

    hbm = pl.BlockSpec(memory_space=pl.ANY)
    vm = pl.BlockSpec(memory_space=pltpu.VMEM)
    outs = pl.pallas_call(
        body, in_specs=[hbm, vm], out_specs=[hbm] * 6 + [vm],
        out_shape=(SDS((nrows, D), f32), SDS((4, nrows, D), bf16), SDS((nrows, D), bf16), SDS((nrows, D), bf16),
                   SDS((hrows, D), bf16), SDS((hrows, D), bf16), SDS((SMALL_ROWS, 128), f32)),
        scratch_shapes=[pltpu.VMEM((nrows, D), bf16), pltpu.VMEM((nrows, D), bf16),
                        pltpu.VMEM((nrows, D), bf16), pltpu.VMEM((nrows, D), bf16), pltpu.VMEM((nrows, D), bf16),
                        pltpu.VMEM((nrows, D), f32),
                        pltpu.SemaphoreType.DMA((4,)), pltpu.SemaphoreType.DMA((4,)),
                        pltpu.SemaphoreType.DMA((6,)), pltpu.SemaphoreType.DMA((6,)), pltpu.SemaphoreType.DMA((2,)),
                        pltpu.VMEM((2, SMALL_ROWS, 128), f32), pltpu.VMEM((4, SMALL_ROWS, 128), f32),
                        pltpu.SemaphoreType.DMA((4,)), pltpu.SemaphoreType.DMA((4,))],
        compiler_params=pltpu.CompilerParams(has_side_effects=True, vmem_limit_bytes=VMEM_LIMIT_V7X),
        name="reduce_scatter_ffn1")(dw1, small_packed)
    return outs[0], outs[-1]


RSA_PIECES = MIX_PIECES + F2_PIECES
RSA_ROWS = _group_rows(RSA_PIECES)
RSA_OFF = {k: PIECE_OFF[k] - PIECE_OFF[RSA_PIECES[0]] for k in RSA_PIECES}
RSA_BLOCK = 192


def _rsa_rows(ref, k):
    return ref.at[pl.ds(RSA_OFF[k], PIECE_ROWS[k]), :]


def _rsa_level1_start(dwint, dwout, dw2, rx1, thru):
    def body(di_ref, do_ref, d2_ref, rx1_ref, thru_ref, sa, ra, di_o, do_o, d2_o, rx1_o, thru_o):
        x, y, c = _position()
        srcs = _weight_pieces(wi_ref=di_ref, wo_ref=do_ref, w2_ref=d2_ref)
        for j, chip in enumerate([(x, y), (1 - x, y), (x, 1 - y), (1 - x, 1 - y)]):
            for k in RSA_PIECES:
                pltpu.make_async_remote_copy(
                    src_ref=_block_rows(srcs, k, (*chip, 1 - c)), dst_ref=_rsa_rows(rx1_ref.at[j], k),
                    send_sem=sa.at[j], recv_sem=ra.at[j], device_id=(x, y, 1 - c), device_id_type=MESH).start()

    dma = pltpu.SemaphoreType.DMA
    arrs = (dwint, dwout, dw2, rx1, thru)
    return pl.pallas_call(
        body, name="rsa_level1_start", out_shape=(dma((4,)), dma((4,))) + tuple(_hbm_like(a) for a in arrs),
        in_specs=(HBM_SPEC,) * 5, out_specs=(SEM_SPEC,) * 2 + (HBM_SPEC,) * 5,
        input_output_aliases={0: 2, 1: 3, 2: 4, 3: 5, 4: 6},
        compiler_params=pltpu.CompilerParams(has_side_effects=SPLIT_EFFECT),
    )(*[_in_hbm(a) for a in arrs])


def _rsa_level1_wait(sa, ra, dwint, dwout, dw2, rx1, after):
    def body(di_ref, do_ref, d2_ref, rx1_ref, sa_ref, ra_ref, after_ref, di_o, do_o, d2_o, rx1_o):
        x, y, c = _position()
        for j in range(4):
            d = pltpu.make_async_remote_copy(src_ref=rx1_ref.at[j], dst_ref=rx1_ref.at[j], send_sem=sa_ref.at[j],
                                             recv_sem=ra_ref.at[j], device_id=(x, y, c), device_id_type=MESH)
            d.wait_recv()
            d.wait_send()

    arrs = (dwint, dwout, dw2, rx1)
    return pl.pallas_call(
        body, name="rsa_level1_wait", out_shape=tuple(_hbm_like(a) for a in arrs),
        in_specs=(HBM_SPEC,) * 4 + (SEM_SPEC, SEM_SPEC, ANY_SPEC), out_specs=(HBM_SPEC,) * 4,
        input_output_aliases={0: 0, 1: 1, 2: 2, 3: 3},
        compiler_params=pltpu.CompilerParams(has_side_effects=SPLIT_EFFECT),
    )(*arrs, sa, ra, after)


def _rsa_chip_sums(dwint, dwout, dw2, rx1):
    nblk = RSA_ROWS // RSA_BLOCK

    def body(di_ref, do_ref, d2_ref, rx1_ref, tx_ref, acc_ref, own_buf, rx_buf, tx_buf, acc_buf, lsems):
        x, y, c = _position()
        srcs = _weight_pieces(wi_ref=di_ref, wo_ref=do_ref, w2_ref=d2_ref)
        for j, chip in enumerate([(x, y), (1 - x, y), (x, 1 - y), (1 - x, 1 - y)]):
            loads = [pltpu.make_async_copy(_block_rows(srcs, k, (*chip, c)), _rsa_rows(own_buf, k), lsems.at[0])
                     for k in RSA_PIECES]
            got = pltpu.make_async_copy(rx1_ref.at[j], rx_buf, lsems.at[1])
            for cp in loads + [got]:
                cp.start()
            pltpu.make_async_copy(rx_buf, rx_buf, lsems.at[0]).wait()
            got.wait()

            def add(i, carry, j=j):
                rows = pl.ds(pl.multiple_of(i * RSA_BLOCK, 16), RSA_BLOCK)
                tot = own_buf[rows, :].astype(f32) + rx_buf[rows, :].astype(f32)
                if j == 0:
                    acc_buf[rows, :] = tot
                else:
                    tx_buf[rows, :] = tot.astype(bf16)
                return carry

            lax.fori_loop(0, nblk, add, 0)
            out = (pltpu.make_async_copy(acc_buf, acc_ref, lsems.at[2]) if j == 0
                   else pltpu.make_async_copy(tx_buf, tx_ref.at[j - 1], lsems.at[2]))
            out.start()
            out.wait()

    return pl.pallas_call(
        body, in_specs=[ANY_SPEC] * 4, out_specs=[ANY_SPEC] * 2,
        out_shape=(SDS((3, RSA_ROWS, D), bf16), SDS((RSA_ROWS, D), f32)),
        scratch_shapes=[pltpu.VMEM((RSA_ROWS, D), bf16), pltpu.VMEM((RSA_ROWS, D), bf16),
                        pltpu.VMEM((RSA_ROWS, D), bf16), pltpu.VMEM((RSA_ROWS, D), f32),
                        pltpu.SemaphoreType.DMA((3,))],
        compiler_params=_cparams(None, VMEM_LIMIT_V7X), name="rsa_chip_sums")(dwint, dwout, dw2, rx1)


def _rsa_level2_start(tx, rx2, thru):
    def body(tx_ref, rx2_ref, thru_ref, sb, rb, tx_o, rx2_o, thru_o):
        x, y, c = _position()
        for j, chip in enumerate([(1 - x, y), (x, 1 - y), (1 - x, 1 - y)]):
            pltpu.make_async_remote_copy(src_ref=tx_ref.at[j], dst_ref=rx2_ref.at[j], send_sem=sb.at[j],
                                         recv_sem=rb.at[j], device_id=(*chip, c), device_id_type=MESH).start()

    dma = pltpu.SemaphoreType.DMA
    arrs = (tx, rx2, thru)
    return pl.pallas_call(
        body, name="rsa_level2_start", out_shape=(dma((3,)), dma((3,))) + tuple(_hbm_like(a) for a in arrs),
        in_specs=(HBM_SPEC,) * 3, out_specs=(SEM_SPEC,) * 2 + (HBM_SPEC,) * 3,
        input_output_aliases={0: 2, 1: 3, 2: 4},
        compiler_params=pltpu.CompilerParams(has_side_effects=SPLIT_EFFECT),
    )(*[_in_hbm(a) for a in arrs])


def _rsa_level2_wait(sb, rb, tx, rx2, after):
    def body(tx_ref, rx2_ref, sb_ref, rb_ref, after_ref, rx2_o):
        x, y, c = _position()
        for j in range(3):
            d = pltpu.make_async_remote_copy(src_ref=tx_ref.at[j], dst_ref=rx2_ref.at[j], send_sem=sb_ref.at[j],
                                             recv_sem=rb_ref.at[j], device_id=(x, y, c), device_id_type=MESH)
            d.wait_recv()
            d.wait_send()

    return pl.pallas_call(
        body, name="rsa_level2_wait", out_shape=_hbm_like(rx2),
        in_specs=(HBM_SPEC, HBM_SPEC, SEM_SPEC, SEM_SPEC, ANY_SPEC), out_specs=HBM_SPEC,
        input_output_aliases={1: 0},
        compiler_params=pltpu.CompilerParams(has_side_effects=SPLIT_EFFECT),
    )(tx, rx2, sb, rb, after)


def _rsa_total(acc, rx2):
    def body(a_ref, r_ref, o_ref):
        o_ref[...] = ((a_ref[...] + r_ref[0].astype(f32)) + r_ref[1].astype(f32)) + r_ref[2].astype(f32)

    return pl.pallas_call(
        body, grid=(RSA_ROWS // RSA_BLOCK,),
        in_specs=[pl.BlockSpec((RSA_BLOCK, D), lambda i: (i, 0)), pl.BlockSpec((3, RSA_BLOCK, D), lambda i: (0, i, 0))],
        out_specs=pl.BlockSpec((RSA_BLOCK, D), lambda i: (i, 0)),
        out_shape=SDS((RSA_ROWS, D), f32), name="rsa_total")(acc, rx2)


def _adamw_math(w, g, m, v):
    m = ADAM_B1 * m + (1.0 - ADAM_B1) * g
    v = ADAM_B2 * v + (1.0 - ADAM_B2) * (g * g)
    m_hat = m / (1.0 - ADAM_B1 ** ADAM_STEP)
    v_hat = v / (1.0 - ADAM_B2 ** ADAM_STEP)
    delta = -ADAM_LR * (m_hat / (jnp.sqrt(v_hat) + ADAM_EPS) + ADAM_WD * w)
    return delta, m, v


def _adamw_big(ws, ms, vs, red1, red_rest):
    npiece = len(BIG)
    rmax = max(PIECE_ROWS)

    def body(*refs):
        ins = (refs[0:npiece], refs[npiece:2 * npiece], refs[2 * npiece:3 * npiece])
        red1_ref, rest_ref = refs[3 * npiece:3 * npiece + 2]
        out_refs = refs[3 * npiece + 2:7 * npiece + 2]
        inb, outb, in_sems, out_sems = refs[7 * npiece + 2:]

        def grad_rows(k):
            if k in G1_PIECES:
                return red1_ref.at[pl.ds(PIECE_OFF[k], PIECE_ROWS[k]), :]
            return _rsa_rows(rest_ref, k)

        def loads(k):
            s, r = k % 2, PIECE_ROWS[k]
            cps = [pltpu.make_async_copy(ins[q][k].at[0], inb.at[s, q, pl.ds(0, r), :], in_sems.at[4 * s + q])
                   for q in range(3)]
            cps.append(pltpu.make_async_copy(grad_rows(k), inb.at[s, 3, pl.ds(0, r), :], in_sems.at[4 * s + 3]))
            return cps

        def stores(k):
            s, r = k % 2, PIECE_ROWS[k]
            return [pltpu.make_async_copy(outb.at[s, q, pl.ds(0, r), :], out_refs[q * npiece + k].at[0],
                                          out_sems.at[4 * s + q]) for q in range(4)]

        for cp in loads(0):
            cp.start()
        for k in range(npiece):
            s, r = k % 2, PIECE_ROWS[k]
            if k + 1 < npiece:
                for cp in loads(k + 1):
                    cp.start()
            for cp in loads(k):
                cp.wait()
            if k >= 2:
                for cp in stores(k - 2):
                    cp.wait()
            g = inb[s, 3, 0:r, :]
            d, nm, nv = _adamw_math(inb[s, 0, 0:r, :], g, inb[s, 1, 0:r, :], inb[s, 2, 0:r, :])
            outb[s, 0, 0:r, :] = g
            outb[s, 1, 0:r, :] = d
            outb[s, 2, 0:r, :] = nm
            outb[s, 3, 0:r, :] = nv
            for cp in stores(k):
                cp.start()
        for k in (npiece - 2, npiece - 1):
            for cp in stores(k):
                cp.wait()

    hbm = pl.BlockSpec(memory_space=pl.ANY)
    outs = pl.pallas_call(
        body, in_specs=[hbm] * (3 * npiece + 2), out_specs=[hbm] * (4 * npiece),
        out_shape=tuple(SDS(w.shape, f32) for _ in range(4) for w in ws),
        scratch_shapes=[pltpu.VMEM((2, 4, rmax, D), f32), pltpu.VMEM((2, 4, rmax, D), f32),
                        pltpu.SemaphoreType.DMA((8,)), pltpu.SemaphoreType.DMA((8,))],
        compiler_params=_cparams(None, VMEM_LIMIT_V7X), name="adamw_big")(*ws, *ms, *vs, red1, red_rest)
    return [list(outs[q * npiece:(q + 1) * npiece]) for q in range(4)]


def _adamw_small(w, m, v, g, name):
    def body(w_ref, m_ref, v_ref, g_ref, d_ref, nm_ref, nv_ref):
        d, nm, nv = _adamw_math(w_ref[...], g_ref[...], m_ref[...], v_ref[...])
        d_ref[...] = d
        nm_ref[...] = nm
        nv_ref[...] = nv

    return pl.pallas_call(
        body, out_shape=tuple(SDS(w.shape, f32) for _ in range(3)), name=name)(w, m, v, g)


WEIGHTS = ("ffn1_norm", "ffn1_w_gate", "ffn1_w_up", "ffn1_w_down", "mix_norm", "w_in", "q_norm", "k_norm",
           "attn_sinks", "rel_bias", "pool_w", "pool_scale", "w_out", "ffn2_norm", "ffn2_w_gate", "ffn2_w_up",
           "ffn2_w_down")
BIG = (("ffn1_w_gate", True), ("ffn1_w_up", True), ("ffn1_w_down", False), ("w_in", True), ("w_out", False),
       ("ffn2_w_gate", True), ("ffn2_w_up", True), ("ffn2_w_down", False))


def kernel(x, ffn1_norm, ffn1_w_gate, ffn1_w_up, ffn1_w_down, mix_norm, w_in, q_norm, k_norm, attn_sinks, rel_bias, pool_w, pool_scale, w_out, ffn2_norm, ffn2_w_gate, ffn2_w_up, ffn2_w_down, loss_target, m_ffn1_norm, m_ffn1_w_gate, m_ffn1_w_up, m_ffn1_w_down, m_mix_norm, m_w_in, m_q_norm, m_k_norm, m_attn_sinks, m_rel_bias, m_pool_w, m_pool_scale, m_w_out, m_ffn2_norm, m_ffn2_w_gate, m_ffn2_w_up, m_ffn2_w_down, v_ffn1_norm, v_ffn1_w_gate, v_ffn1_w_up, v_ffn1_w_down, v_mix_norm, v_w_in, v_q_norm, v_k_norm, v_attn_sinks, v_rel_bias, v_pool_w, v_pool_scale, v_w_out, v_ffn2_norm, v_ffn2_w_gate, v_ffn2_w_up, v_ffn2_w_down):
    args = dict(locals())
    w = {n: args[n] for n in WEIGHTS}
    m = {n: args["m_" + n] for n in WEIGHTS}
    v = {n: args["v_" + n] for n in WEIGHTS}

    as_rows = lambda a, tr: jnp.swapaxes(a, 1, 2) if tr else a
    shard = jnp.concatenate([as_rows(w[n], tr)[0].astype(bf16) for n, tr in BIG], axis=0)
    exchanges = _GatheredWeights(shard)
    gx, (dw1, _, _, _), small = _local_step(
        x[0], loss_target[0], exchanges, ffn1_norm, mix_norm, ffn2_norm, q_norm, k_norm, attn_sinks,
        rel_bias, pool_w[0], pool_scale)

    red1, small_tot = _reduce_scatter_ffn1(dw1, _pack_small(small))
    red_rest = exchanges.mix_ffn2_grads_total(red1)

    grads, deltas, new_m, new_v = {}, {}, {}, {}
    big_out = _adamw_big(*[[as_rows(t[n], tr) for n, tr in BIG] for t in (w, m, v)], red1, red_rest)
    for k, (n, tr) in enumerate(BIG):
        grads[n], deltas[n], new_m[n], new_v[n] = [as_rows(o[k], tr) for o in big_out]
    small_names = [n for n in SMALL_NAMES if n != "loss"]
    ds, nms, nvs = _adamw_small(_pack_small({n: w[n] for n in small_names}), _pack_small({n: m[n] for n in small_names}),
                                _pack_small({n: v[n] for n in small_names}), small_tot, "adamw_small")
    for n in small_names:
        grads[n] = _unpack_small(small_tot, n)
        deltas[n], new_m[n], new_v[n] = _unpack_small(ds, n), _unpack_small(nms, n), _unpack_small(nvs, n)
    loss = small_tot[LOSS_ROW, 0]
    return (loss, gx[None], *[grads[n] for n in WEIGHTS], *[deltas[n] for n in WEIGHTS],
            *[new_m[n] for n in WEIGHTS], *[new_v[n] for n in WEIGHTS])
```

```python
import functools

import jax
import jax.numpy as jnp
import numpy as np
from jax import lax
from jax.experimental import pallas as pl
from jax.experimental.pallas import tpu as pltpu

f32, bf16, i32 = jnp.float32, jnp.bfloat16, jnp.int32
SDS = jax.ShapeDtypeStruct

D = 1024
F = 2816
HD = 64
NH = 8
NKV = 2
GQA = NH // NKV
DATTN = NH * HD
DKV = NKV * HD
DPOOL = 512
POOL_WINDOWS = (2, 4, 8, 16)
PGD = DPOOL // len(POOL_WINDOWS)
DIN = DATTN + 2 * DKV + DPOOL
DMIX = DATTN + DPOOL
BLK = 128
NBUCK = 32
MAX_DISTANCE = 128
EPS = 1e-6
NEG = -1e30
SCALE = HD ** -0.5

ADAM_LR, ADAM_B1, ADAM_B2, ADAM_EPS, ADAM_WD, ADAM_STEP = 0.001, 0.9, 0.999, 1e-08, 0.01, 10

NDEV = 8
FS = F // NDEV
INS = DIN // NDEV
OUTS = DMIX // NDEV
PIECE_ROWS = (FS, FS, FS, INS, OUTS, FS, FS, FS)
PIECE_OFF = tuple(int(v) for v in np.cumsum((0,) + PIECE_ROWS[:-1]))
PACK_ROWS = sum(PIECE_ROWS)

VMEM_LIMIT_V7X = 56 * 1024 * 1024

MESH = pl.DeviceIdType.MESH


def _cparams(sem=None, vmem=None):
    return pltpu.CompilerParams(dimension_semantics=sem, vmem_limit_bytes=vmem)


def _nt(a, b):
    return lax.dot_general(a, b, (((1,), (1,)), ((), ())), preferred_element_type=f32)


def _tn(a, b):
    return lax.dot_general(a, b, (((0,), (0,)), ((), ())), preferred_element_type=f32)


def _nn(a, b):
    return jnp.dot(a, b, preferred_element_type=f32)


def _sigmoid(x):
    return 1.0 / (1.0 + jnp.exp(-x))


def _norm_fwd(x, g, name):
    T = x.shape[0]
    tm = min(512, T)

    def body(x_ref, g_ref, h_ref):
        xv = x_ref[...]
        r = lax.rsqrt(jnp.mean(xv * xv, axis=-1, keepdims=True) + EPS)
        h_ref[...] = (xv * r * g_ref[...]).astype(bf16)

    return pl.pallas_call(
        body, grid=(T // tm,),
        in_specs=[pl.BlockSpec((tm, D), lambda i: (i, 0)), pl.BlockSpec((1, D), lambda i: (0, 0))],
        out_specs=pl.BlockSpec((tm, D), lambda i: (i, 0)),
        out_shape=SDS((T, D), bf16), name=name)(x, g)


def _norm_bwd(dh, x, g, dres, out_scale, name):
    T = x.shape[0]
    tm = min(512, T)

    def body(dh_ref, x_ref, g_ref, dr_ref, dx_ref, dxb_ref, dg_ref):
        i = pl.program_id(0)
        xv = x_ref[...]
        r = lax.rsqrt(jnp.mean(xv * xv, axis=-1, keepdims=True) + EPS)
        xh = xv * r
        dhv = dh_ref[...]
        dxh = dhv * g_ref[...]
        dx = dr_ref[...] + r * (dxh - xh * jnp.mean(dxh * xh, axis=-1, keepdims=True))
        dx_ref[...] = dx
        dxb_ref[...] = (out_scale * dx).astype(bf16)
        dg = jnp.sum(dhv * xh, axis=0, keepdims=True)

        @pl.when(i == 0)
        def _():
            dg_ref[...] = dg

        @pl.when(i > 0)
        def _():
            dg_ref[...] += dg

    tok = pl.BlockSpec((tm, D), lambda i: (i, 0))
    vec = pl.BlockSpec((1, D), lambda i: (0, 0))
    return pl.pallas_call(
        body, grid=(T // tm,),
        in_specs=[tok, tok, vec, tok], out_specs=[tok, tok, vec],
        out_shape=(SDS((T, D), f32), SDS((T, D), bf16), SDS((1, D), f32)),
        compiler_params=_cparams(("arbitrary",)), name=name)(dh, x, g, dres)


def _gain_grad(dh, x, name):
    T = x.shape[0]
    tm = min(512, T)

    def body(dh_ref, x_ref, dg_ref):
        i = pl.program_id(0)
        xv = x_ref[...]
        r = lax.rsqrt(jnp.mean(xv * xv, axis=-1, keepdims=True) + EPS)
        dg = jnp.sum(dh_ref[...] * (xv * r), axis=0, keepdims=True)

        @pl.when(i == 0)
        def _():
            dg_ref[...] = dg

        @pl.when(i > 0)
        def _():
            dg_ref[...] += dg

    tok = pl.BlockSpec((tm, D), lambda i: (i, 0))
    return pl.pallas_call(
        body, grid=(T // tm,), in_specs=[tok, tok], out_specs=pl.BlockSpec((1, D), lambda i: (0, 0)),
        out_shape=SDS((1, D), f32), compiler_params=_cparams(("arbitrary",)), name=name)(dh, x)


FFN_ROW_CHUNK = 256


def _ffn_tiles(T):
    return min(1024, T), 256


def _ffn_fwd(h, w, x, target, name):
    T = h.shape[0]
    tm, tf = _ffn_tiles(T)
    nf = F // tf
    with_loss = target is not None

    def body(*refs):
        if with_loss:
            h_ref, w_ref, x_hbm, t_hbm, xo_ref, g_ref, u_ref, dyb_ref, loss_ref, tbuf, sem = refs
        else:
            h_ref, w_ref, x_hbm, xo_ref, g_ref, u_ref, sem = refs
        fi = pl.program_id(0)

        @pl.when(fi == 0)
        def _():
            cp = pltpu.make_async_copy(x_hbm, xo_ref, sem)
            cp.start()
            cp.wait()

        wgu = w_ref[0:2].reshape(2 * tf, D)
        for r in range(0, T, tm):
            rows = slice(r, r + tm)
            gu = _nt(h_ref[rows, :], wgu)
            gate, up = gu[:, :tf], gu[:, tf:]
            act = gate * _sigmoid(gate) * up
            g_ref[0, rows, :] = gate.astype(bf16)
            u_ref[0, rows, :] = up.astype(bf16)
            xo_ref[rows, :] += _nn((0.5 * act).astype(bf16), w_ref[2])

        if with_loss:
            @pl.when(fi == nf - 1)
            def _():
                lanes = jnp.zeros((1, 128), f32)
                for r in range(0, T, tm):
                    rows = slice(r, r + tm)
                    cp = pltpu.make_async_copy(t_hbm.at[pl.ds(r, tm), :], tbuf, sem)
                    cp.start()
                    cp.wait()
                    e = xo_ref[rows, :] - tbuf[...]
                    dy = e * (1.0 / D)
                    xo_ref[rows, :] = dy
                    dyb_ref[rows, :] = (0.5 * dy).astype(bf16)
                    col = jnp.sum(e * e, axis=0, keepdims=True) * (0.5 / D)
                    for k in range(D // 128):
                        lanes = lanes + col[:, 128 * k:128 * (k + 1)]
                loss_ref[...] = lanes

    tok = pl.BlockSpec((T, D), lambda f: (0, 0))
    act_spec = pl.BlockSpec((1, T, tf), lambda f: (f, 0, 0))
    hbm = pl.BlockSpec(memory_space=pl.ANY)
    in_specs = [tok, pl.BlockSpec((3, tf, D), lambda f: (0, f, 0)), hbm]
    out_specs = [tok, act_spec, act_spec]
    out_shape = [SDS((T, D), f32), SDS((nf, T, tf), bf16), SDS((nf, T, tf), bf16)]
    scratch = [pltpu.SemaphoreType.DMA]
    args = [h, w, x]
    if with_loss:
        in_specs.append(hbm)
        args.append(target)
        out_specs += [tok, pl.BlockSpec((1, 128), lambda f: (0, 0))]
        out_shape += [SDS((T, D), bf16), SDS((1, 128), f32)]
        scratch = [pltpu.VMEM((tm, D), f32)] + scratch
    return pl.pallas_call(
        body, grid=(nf,), in_specs=in_specs, out_specs=out_specs, out_shape=tuple(out_shape), scratch_shapes=scratch,
        compiler_params=_cparams(("arbitrary",), VMEM_LIMIT_V7X), name=name)(*args)


def _ffn_bwd(dob, h, gate, up, w, name):
    T = h.shape[0]
    _, tf = _ffn_tiles(T)
    nf = F // tf

    def body(do_hbm, h_hbm, g_ref, u_ref, w_ref, dh_hbm, dw_ref, do_v, h_v, dh_acc, dgu_s, act_s, sems):
        fi = pl.program_id(0)

        @pl.when(fi == 0)
        def _():
            loads = [pltpu.make_async_copy(do_hbm, do_v, sems.at[0]), pltpu.make_async_copy(h_hbm, h_v, sems.at[1])]
            for cp in loads:
                cp.start()
            dh_acc[...] = jnp.zeros_like(dh_acc)
            for cp in loads:
                cp.wait()

        wgu = w_ref[0:2].reshape(2 * tf, D)
        for r in range(0, T, FFN_ROW_CHUNK):
            rows = slice(r, r + FFN_ROW_CHUNK)
            dov = do_v[rows, :]
            gv = g_ref[0, rows, :].astype(f32)
            uv = u_ref[0, rows, :].astype(f32)
            sg = _sigmoid(gv)
            sil = gv * sg
            dact = _nt(dov, w_ref[2])
            dup = dact * sil
            dgate = dact * uv * (sg * (1.0 + gv * (1.0 - sg)))
            dgu = jnp.concatenate([dgate.astype(bf16), dup.astype(bf16)], axis=1)
            dgu_s[rows, :] = dgu
            act_s[rows, :] = (sil * uv).astype(bf16)
            dh_acc[rows, :] += _nn(dgu, wgu)
        dw_ref[0:2] = _tn(dgu_s[...], h_v[...]).reshape(2, tf, D).astype(bf16)
        dw_ref[2] = _tn(act_s[...], do_v[...]).astype(bf16)

        @pl.when(fi == nf - 1)
        def _():
            out = pltpu.make_async_copy(dh_acc, dh_hbm, sems.at[0])
            out.start()
            out.wait()

    act_spec = pl.BlockSpec((1, T, tf), lambda f: (f, 0, 0))
    wspec = pl.BlockSpec((3, tf, D), lambda f: (0, f, 0))
    hbm = pl.BlockSpec(memory_space=pl.ANY)
    return pl.pallas_call(
        body, grid=(nf,),
        in_specs=[hbm, hbm, act_spec, act_spec, wspec],
        out_specs=[hbm, wspec],
        out_shape=(SDS((T, D), f32), SDS((3, F, D), bf16)),
        scratch_shapes=[pltpu.VMEM((T, D), bf16), pltpu.VMEM((T, D), bf16), pltpu.VMEM((T, D), f32),
                        pltpu.VMEM((T, 2 * tf), bf16), pltpu.VMEM((T, tf), bf16), pltpu.SemaphoreType.DMA((2,))],
        compiler_params=_cparams(("arbitrary",), VMEM_LIMIT_V7X), name=name)(dob, h, gate, up, w)


def _in_proj_fwd(h, wint, name):
    T = h.shape[0]
    tm = min(512, T)

    def body(h_ref, w_ref, z_ref):
        z_ref[...] = _nt(h_ref[...], w_ref[...])

    return pl.pallas_call(
        body, grid=(T // tm,),
        in_specs=[pl.BlockSpec((tm, D), lambda i: (i, 0)), pl.BlockSpec((DIN, D), lambda i: (0, 0))],
        out_specs=pl.BlockSpec((tm, DIN), lambda i: (i, 0)),
        out_shape=SDS((T, DIN), f32), name=name)(h, wint)


def _in_proj_bwd(dz, wint, h, name):
    T = h.shape[0]
    tm = min(512, T)
    nt = T // tm

    def body(dz_ref, w_ref, h_ref, dh_ref, dw_ref, acc):
        i = pl.program_id(0)
        dzb = dz_ref[...].astype(bf16)
        dh_ref[...] = _nn(dzb, w_ref[...])
        part = _tn(dzb, h_ref[...])

        @pl.when(i == 0)
        def _():
            acc[...] = part

        @pl.when(i > 0)
        def _():
            acc[...] += part

        @pl.when(i == nt - 1)
        def _():
            dw_ref[...] = acc[...].astype(bf16)

    wspec = pl.BlockSpec((DIN, D), lambda i: (0, 0))
    return pl.pallas_call(
        body, grid=(nt,),
        in_specs=[pl.BlockSpec((tm, DIN), lambda i: (i, 0)), wspec, pl.BlockSpec((tm, D), lambda i: (i, 0))],
        out_specs=[pl.BlockSpec((tm, D), lambda i: (i, 0)), wspec],
        out_shape=(SDS((T, D), f32), SDS((DIN, D), bf16)),
        scratch_shapes=[pltpu.VMEM((DIN, D), f32)],
        compiler_params=_cparams(("arbitrary",)), name=name)(dz, wint, h)


def _out_proj_fwd(ymix, wout, x, g, name):
    T = x.shape[0]
    tm = min(512, T)

    def body(y_ref, w_ref, x_ref, g_ref, o_ref, h_ref):
        o = x_ref[...] + _nn(y_ref[...], w_ref[...])
        o_ref[...] = o
        r = lax.rsqrt(jnp.mean(o * o, axis=-1, keepdims=True) + EPS)
        h_ref[...] = (o * r * g_ref[...]).astype(bf16)

    tok = pl.BlockSpec((tm, D), lambda i: (i, 0))
    return pl.pallas_call(
        body, grid=(T // tm,),
        in_specs=[pl.BlockSpec((tm, DMIX), lambda i: (i, 0)), pl.BlockSpec((DMIX, D), lambda i: (0, 0)), tok,
                  pl.BlockSpec((1, D), lambda i: (0, 0))],
        out_specs=[tok, tok], out_shape=(SDS((T, D), f32), SDS((T, D), bf16)), name=name)(ymix, wout, x, g)


def _out_proj_bwd(dxb, wout, ymix, name):
    T = dxb.shape[0]
    tm = min(512, T)
    nt = T // tm

    def body(dx_ref, w_ref, y_ref, dy_ref, dw_ref, acc):
        i = pl.program_id(0)
        dxv = dx_ref[...]
        dy_ref[...] = _nt(dxv, w_ref[...])
        part = _tn(y_ref[...], dxv)

        @pl.when(i == 0)
        def _():
            acc[...] = part

        @pl.when(i > 0)
        def _():
            acc[...] += part

        @pl.when(i == nt - 1)
        def _():
            dw_ref[...] = acc[...].astype(bf16)

    wspec = pl.BlockSpec((DMIX, D), lambda i: (0, 0))
    return pl.pallas_call(
        body, grid=(nt,),
        in_specs=[pl.BlockSpec((tm, D), lambda i: (i, 0)), wspec, pl.BlockSpec((tm, DMIX), lambda i: (i, 0))],
        out_specs=[pl.BlockSpec((tm, DMIX), lambda i: (i, 0)), wspec],
        out_shape=(SDS((T, DMIX), f32), SDS((DMIX, D), bf16)),
        scratch_shapes=[pltpu.VMEM((DMIX, D), f32)],
        compiler_params=_cparams(("arbitrary",)), name=name)(dxb, wout, ymix)


def _t5_bucket_table():
    ql = np.arange(BLK)[:, None]
    kl = np.arange(2 * BLK)[None, :]
    n = np.maximum(ql + BLK - kl, 0)
    max_exact = NBUCK // 2
    large = max_exact + (np.log(np.maximum(n, 1) / max_exact) / np.log(MAX_DISTANCE / max_exact)
                         * (NBUCK - max_exact)).astype(np.int32)
    large = np.minimum(large, NBUCK - 1)
    return np.where(n < max_exact, n, large).astype(np.int32)


def _fill_bias(bk_ref, rb_ref, bias_scr):
    bk = bk_ref[...]
    for h in range(NH):
        def step(b, acc, h=h):
            return acc + jnp.where(bk == b, rb_ref[b, h], 0.0)
        bias_scr[h] = lax.fori_loop(0, NBUCK, step, jnp.zeros((BLK, 2 * BLK), f32))


MIX_SUB = 4


class _Window:
    def __init__(self, zc_ref, zp_ref, n, s):
        self.blk = n * MIX_SUB + s
        self.first_in_step = s == 0
        self.cur = lambda a, b: zc_ref[s * BLK:(s + 1) * BLK, a:b]
        self.prev = (lambda a, b: zp_ref[:, a:b]) if s == 0 else (lambda a, b: zc_ref[(s - 1) * BLK:s * BLK, a:b])


def _attn_qkv(win, kh, qg, kg):
    kc = DATTN + HD * kh
    vc = DATTN + DKV + HD * kh
    kx = jnp.concatenate([win.prev(kc, kc + HD), win.cur(kc, kc + HD)], axis=0)
    vx = jnp.concatenate([win.prev(vc, vc + HD), win.cur(vc, vc + HD)], axis=0)
    qx = jnp.concatenate([win.cur(HD * (GQA * kh + g), HD * (GQA * kh + g + 1)) for g in range(GQA)], axis=0)
    rq = lax.rsqrt(jnp.mean(qx * qx, axis=-1, keepdims=True) + EPS)
    rk = lax.rsqrt(jnp.mean(kx * kx, axis=-1, keepdims=True) + EPS)
    qhat, khat = qx * rq, kx * rk
    return dict(qhat=qhat, khat=khat, rq=rq, rk=rk, qnb=(qhat * qg).astype(bf16), knb=(khat * kg).astype(bf16),
                vb=vx.astype(bf16))


def _attn_probs(a, kh, sk_ref, bias_scr, n):
    s = _nt(a["qnb"], a["knb"]) * SCALE + bias_scr[GQA * kh:GQA * (kh + 1)].reshape(GQA * BLK, 2 * BLK)
    row = lax.broadcasted_iota(i32, (GQA * BLK, 2 * BLK), 0) & (BLK - 1)
    col = lax.broadcasted_iota(i32, (GQA * BLK, 2 * BLK), 1)
    mask = (col > row) & (col <= row + BLK) & ((col >= BLK) | (n > 0))
    s = jnp.where(mask, s, NEG)
    ridx = lax.broadcasted_iota(i32, (GQA * BLK, 1), 0)
    sink = jnp.full((GQA * BLK, 1), sk_ref[GQA * kh + GQA - 1], f32)
    for g in range(GQA - 2, -1, -1):
        sink = jnp.where(ridx < (g + 1) * BLK, sk_ref[GQA * kh + g], sink)
    m = jnp.maximum(jnp.max(s, axis=-1, keepdims=True), sink)
    e = jnp.exp(s - m)
    den = jnp.sum(e, axis=-1, keepdims=True) + jnp.exp(sink - m)
    return e / den


POOL_STEPS = {2: (1,), 4: (1, 2), 8: (1, 2, 4), 16: (1, 2, 4, 8)}


def _pool_group(win, g, w):
    n = win.blk
    c0 = DATTN + 2 * DKV + PGD * g
    uc = win.cur(c0, c0 + PGD)
    up = jnp.where(n > 0, win.prev(c0, c0 + PGD), 0.0)
    sm = jnp.concatenate([up, uc], axis=0)
    for k in POOL_STEPS[w]:
        sm = sm + pltpu.roll(sm, k, axis=0)
    pos = n * BLK + lax.broadcasted_iota(i32, (BLK, 1), 0) + 1
    cnt = jnp.minimum(pos, w).astype(f32)
    return sm[BLK:2 * BLK] / cnt - uc, cnt


def _mix_fwd(z, qg, kg, sinks, relb, bucket, pool_w, pscale, name):
    T = z.shape[0]
    step_rows = MIX_SUB * BLK
    nsteps = T // step_rows

    def body(zc_ref, zp_ref, qg_ref, kg_ref, sk_ref, rb_ref, bk_ref, pw_ref, ps_ref, y_ref, p_ref, bias_scr, yacc):
        n = pl.program_id(0)

        @pl.when(n == 0)
        def _():
            _fill_bias(bk_ref, rb_ref, bias_scr)

        for s in range(MIX_SUB):
            win = _Window(zc_ref, zp_ref, n, s)
            rows = slice(s * BLK, (s + 1) * BLK)
            for kh in range(NKV):
                a = _attn_qkv(win, kh, qg_ref[...], kg_ref[...])
                pb = _attn_probs(a, kh, sk_ref, bias_scr, win.blk).astype(bf16)
                p_ref[s, GQA * kh:GQA * (kh + 1)] = pb.reshape(GQA, BLK, 2 * BLK)
                o = _nn(pb, a["vb"])
                for g in range(GQA):
                    hc = HD * (GQA * kh + g)
                    yacc[rows, hc:hc + HD] = o[g * BLK:(g + 1) * BLK]
            for g, w in enumerate(POOL_WINDOWS):
                pooled, _ = _pool_group(win, g, w)
                yp = _nn(pooled.astype(bf16), pw_ref[g].astype(bf16)) * ps_ref[:, PGD * g:PGD * (g + 1)]
                yacc[rows, DATTN + PGD * g:DATTN + PGD * (g + 1)] = yp
        y_ref[...] = yacc[...].astype(bf16)

    full = lambda *shape: pl.BlockSpec(shape, lambda n: (0,) * len(shape))
    smem = pl.BlockSpec(memory_space=pltpu.SMEM)
    return pl.pallas_call(
        body, grid=(nsteps,),
        in_specs=[pl.BlockSpec((step_rows, DIN), lambda n: (n, 0)),
                  pl.BlockSpec((BLK, DIN), lambda n: (jnp.maximum(n * MIX_SUB - 1, 0), 0)),
                  full(1, HD), full(1, HD), smem, smem, full(BLK, 2 * BLK),
                  full(len(POOL_WINDOWS), PGD, PGD), full(1, DPOOL)],
        out_specs=[pl.BlockSpec((step_rows, DMIX), lambda n: (n, 0)),
                   pl.BlockSpec((MIX_SUB, NH, BLK, 2 * BLK), lambda n: (n, 0, 0, 0))],
        out_shape=(SDS((T, DMIX), bf16), SDS((T // BLK, NH, BLK, 2 * BLK), bf16)),
        scratch_shapes=[pltpu.VMEM((NH, BLK, 2 * BLK), f32), pltpu.VMEM((step_rows, DMIX), f32)],
        compiler_params=_cparams(("arbitrary",)), name=name)(z, z, qg, kg, sinks, relb, bucket, pool_w, pscale)


def _mix_bwd(z, dy, probs, qg, kg, relb, bucket, pool_w, pscale, name):
    T = z.shape[0]
    step_rows = MIX_SUB * BLK
    nsteps = T // step_rows

    def body(zc_ref, zp_ref, dy_ref, p_ref, qg_ref, kg_ref, bk_ref, pw_ref, ps_ref,
             dz_ref, dqg_ref, dkg_ref, dsk_ref, drb_ref, dpw_ref, dps_ref, dbias_scr):
        n = pl.program_id(0)

        @pl.when(n == 0)
        def _():
            dbias_scr[...] = jnp.zeros_like(dbias_scr)
            dqg_ref[...] = jnp.zeros_like(dqg_ref)
            dkg_ref[...] = jnp.zeros_like(dkg_ref)
            dpw_ref[...] = jnp.zeros_like(dpw_ref)
            dps_ref[...] = jnp.zeros_like(dps_ref)

        qg, kg = qg_ref[...], kg_ref[...]
        for s in range(MIX_SUB):
            win = _Window(zc_ref, zp_ref, n, s)
            blk = win.blk
            rows = pl.ds(pl.multiple_of(blk * BLK, BLK), BLK)
            prow = pl.ds(pl.multiple_of(jnp.maximum(blk - 1, 0) * BLK, BLK), BLK)
            dyr = slice(s * BLK, (s + 1) * BLK)

            def into_prev(fn, s=s):
                if s == 0:
                    pl.when(n > 0)(fn)
                else:
                    fn()

            for kh in range(NKV):
                a = _attn_qkv(win, kh, qg, kg)
                pb = p_ref[s, GQA * kh:GQA * (kh + 1)].reshape(GQA * BLK, 2 * BLK)
                p = pb.astype(f32)
                do = jnp.concatenate([dy_ref[dyr, HD * (GQA * kh + g):HD * (GQA * kh + g + 1)] for g in range(GQA)],
                                     axis=0).astype(bf16)
                dv = _tn(pb, do)
                dp = _nt(do, a["vb"])
                delta = jnp.sum(p * dp, axis=-1, keepdims=True)
                ds = p * (dp - delta)
                for g in range(GQA):
                    dbias_scr[GQA * kh + g] += ds[g * BLK:(g + 1) * BLK]
                dsb = ds.astype(bf16)
                dqn = _nn(dsb, a["knb"]) * SCALE
                dkn = _tn(dsb, a["qnb"]) * SCALE
                qhat, khat = a["qhat"], a["khat"]
                dqg_ref[...] += jnp.sum(dqn * qhat, axis=0, keepdims=True)
                dkg_ref[...] += jnp.sum(dkn * khat, axis=0, keepdims=True)
                dqh = dqn * qg
                dq = a["rq"] * (dqh - qhat * jnp.mean(dqh * qhat, axis=-1, keepdims=True))
                dkh = dkn * kg
                dk = a["rk"] * (dkh - khat * jnp.mean(dkh * khat, axis=-1, keepdims=True))
                kc = DATTN + HD * kh
                vc = DATTN + DKV + HD * kh
                for g in range(GQA):
                    hc = HD * (GQA * kh + g)
                    dz_ref[rows, hc:hc + HD] = dq[g * BLK:(g + 1) * BLK]
                dz_ref[rows, kc:kc + HD] = dk[BLK:2 * BLK]
                dz_ref[rows, vc:vc + HD] = dv[BLK:2 * BLK]

                def kv_prev(dk=dk, dv=dv, kc=kc, vc=vc, prow=prow):
                    dz_ref[prow, kc:kc + HD] += dk[0:BLK]
                    dz_ref[prow, vc:vc + HD] += dv[0:BLK]

                into_prev(kv_prev)

            for g, w in enumerate(POOL_WINDOWS):
                c0 = DATTN + 2 * DKV + PGD * g
                pooled, cnt = _pool_group(win, g, w)
                pb = pooled.astype(bf16)
                wb = pw_ref[g].astype(bf16)
                dyp = dy_ref[dyr, DATTN + PGD * g:DATTN + PGD * (g + 1)]
                ypre = _nn(pb, wb)
                dps_ref[:, PGD * g:PGD * (g + 1)] += jnp.sum(dyp * ypre, axis=0, keepdims=True)
                dyg = (dyp * ps_ref[:, PGD * g:PGD * (g + 1)]).astype(bf16)
                dpw_ref[g] += _tn(pb, dyg)
                dpooled = _nt(dyg, wb)
                due = jnp.concatenate([jnp.zeros((BLK, PGD), f32), dpooled / cnt], axis=0)
                for k in POOL_STEPS[w]:
                    due = due + pltpu.roll(due, 2 * BLK - k, axis=0)
                dz_ref[rows, c0:c0 + PGD] = due[BLK:2 * BLK] - dpooled

                def pool_prev(due=due, c0=c0, prow=prow):
                    dz_ref[prow, c0:c0 + PGD] += due[0:BLK]

                into_prev(pool_prev)

        @pl.when(n == nsteps - 1)
        def _():
            bk = bk_ref[...]
            ri = lax.broadcasted_iota(i32, (NBUCK, NH), 0)
            ci = lax.broadcasted_iota(i32, (NBUCK, NH), 1)

            def step(b, acc):
                for h in range(NH):
                    sel = jnp.where(bk == b, dbias_scr[h], 0.0)
                    tot = jnp.sum(jnp.sum(sel, axis=1, keepdims=True), axis=0, keepdims=True)
                    acc = acc + jnp.where((ri == b) & (ci == h), tot, 0.0)
                return acc

            drb_ref[...] = lax.fori_loop(0, NBUCK, step, jnp.zeros((NBUCK, NH), f32))
            lane = lax.broadcasted_iota(i32, (1, 128), 1)
            dsk = jnp.zeros((1, 128), f32)
            for h in range(NH):
                tot = jnp.sum(jnp.sum(dbias_scr[h], axis=1, keepdims=True), axis=0, keepdims=True)
                dsk = dsk - jnp.where(lane == h, tot, 0.0)
            dsk_ref[...] = dsk

    full = lambda *shape: pl.BlockSpec(shape, lambda n: (0,) * len(shape))
    npg = len(POOL_WINDOWS)
    return pl.pallas_call(
        body, grid=(nsteps,),
        in_specs=[pl.BlockSpec((step_rows, DIN), lambda n: (n, 0)),
                  pl.BlockSpec((BLK, DIN), lambda n: (jnp.maximum(n * MIX_SUB - 1, 0), 0)),
                  pl.BlockSpec((step_rows, DMIX), lambda n: (n, 0)),
                  pl.BlockSpec((MIX_SUB, NH, BLK, 2 * BLK), lambda n: (n, 0, 0, 0)),
                  full(1, HD), full(1, HD), full(BLK, 2 * BLK), full(npg, PGD, PGD), full(1, DPOOL)],
        out_specs=[full(T, DIN), full(1, HD), full(1, HD), full(1, 128), full(NBUCK, NH),
                   full(npg, PGD, PGD), full(1, DPOOL)],
        out_shape=(SDS((T, DIN), f32), SDS((1, HD), f32), SDS((1, HD), f32), SDS((1, 128), f32),
                   SDS((NBUCK, NH), f32), SDS((npg, PGD, PGD), f32), SDS((1, DPOOL), f32)),
        scratch_shapes=[pltpu.VMEM((NH, BLK, 2 * BLK), f32)],
        compiler_params=_cparams(("arbitrary",), VMEM_LIMIT_V7X),
        name=name)(z, z, dy, probs, qg, kg, bucket, pool_w, pscale)


class _LocalWeights:
    def __init__(self, w1, wint, wout, w2):
        self.w1, self.wint, self.wout, self.w2 = w1, wint, wout, w2

    def ffn1(self):
        return self.w1

    def after_ffn1(self, gain, x1):
        return gain

    def mix(self, after):
        return self.wint, self.wout

    def before_out_proj(self, wout, after):
        return wout

    def ffn2(self, after):
        return self.w2

    def mix_ffn2_grads_ready(self, dwint, dwout, dw2, dh2):
        self.grads_rest = (dwint, dwout, dw2)
        return dh2

    def before_ffn1_bwd(self, dx1b):
        return dx1b


def _local_step(x, target, weights, g1, gm, g3, qg, kg, sinks, relb, pool_w, pscale):
    bucket = jnp.asarray(_t5_bucket_table())
    sk = sinks.reshape(NH)
    w1 = weights.ffn1()
    h1 = _norm_fwd(x, g1, "norm1_fwd")
    x1, gate1, up1 = _ffn_fwd(h1, w1, x, None, "ffn1_fwd")
    h2 = _norm_fwd(x1, weights.after_ffn1(gm, x1), "norm2_fwd")
    wint, wout = weights.mix(h2)
    z = _in_proj_fwd(h2, wint, "in_proj_fwd")
    ymix, probs = _mix_fwd(z, qg, kg, sk, relb, bucket, pool_w, pscale, "mix_fwd")
    wout = weights.before_out_proj(wout, ymix)
    x2, h3 = _out_proj_fwd(ymix, wout, x1, g3, "out_proj_fwd")
    w2 = weights.ffn2(h3)
    dy, gate2, up2, dyb, loss_lanes = _ffn_fwd(h3, w2, x2, target, "ffn2_fwd")

    dh3, dw2 = _ffn_bwd(dyb, h3, gate2, up2, w2, "ffn2_bwd")
    dx2, dx2b, dg3 = _norm_bwd(dh3, x2, g3, dy, 1.0, "norm3_bwd")
    dymix, dwout = _out_proj_bwd(dx2b, wout, ymix, "out_proj_bwd")
    dz, dqg, dkg, dsk, drb, dpw, dps = _mix_bwd(z, dymix, probs, qg, kg, relb, bucket, pool_w, pscale, "mix_bwd")
    dh2, dwint = _in_proj_bwd(dz, wint, h2, "in_proj_bwd")
    dh2 = weights.mix_ffn2_grads_ready(dwint, dwout, dw2, dh2)
    dx1, dx1b, dgm = _norm_bwd(dh2, x1, gm, dx2, 0.5, "norm2_bwd")
    dx1b = weights.before_ffn1_bwd(dx1b)
    dh1, dw1 = _ffn_bwd(dx1b, h1, gate1, up1, w1, "ffn1_bwd")
    dg1 = _gain_grad(dh1, x, "norm1_gain_grad")
    small = dict(ffn1_norm=dg1, mix_norm=dgm, ffn2_norm=dg3, pool_scale=dps, q_norm=dqg, k_norm=dkg,
                 attn_sinks=dsk[:, :NH], rel_bias=drb, pool_w=dpw, loss=loss_lanes)
    return (dh1, dx1), (dw1, dwint, dwout, dw2), small


SMALL_NAMES = ("ffn1_norm", "mix_norm", "ffn2_norm", "pool_scale", "q_norm", "k_norm", "attn_sinks", "rel_bias",
               "pool_w", "loss")
SMALL_SHAPES = dict(ffn1_norm=(1, D), mix_norm=(1, D), ffn2_norm=(1, D), pool_scale=(1, DPOOL), q_norm=(1, HD),
                    k_norm=(1, HD), attn_sinks=(1, NH), rel_bias=(NBUCK, NH),
                    pool_w=(1, len(POOL_WINDOWS), PGD, PGD), loss=(1, 128))


def _small_rows(name):
    return -(-int(np.prod(SMALL_SHAPES[name])) // 128)


SMALL_OFF = {}
_r = 0
for _n in SMALL_NAMES:
    SMALL_OFF[_n] = _r
    _r += _small_rows(_n)
SMALL_ROWS = -(-_r // 8) * 8
LOSS_ROW = SMALL_OFF["loss"]


def _pack_small(vals):
    parts = []
    for n in SMALL_NAMES:
        size = _small_rows(n) * 128
        if n in vals:
            flat = vals[n].astype(f32).reshape(-1)
            parts.append(jnp.pad(flat, (0, size - flat.shape[0])))
        else:
            parts.append(jnp.zeros((size,), f32))
    flat = jnp.concatenate(parts)
    flat = jnp.pad(flat, (0, SMALL_ROWS * 128 - flat.shape[0]))
    return flat.reshape(SMALL_ROWS, 128)


def _unpack_small(packed, name):
    size = int(np.prod(SMALL_SHAPES[name]))
    r0 = SMALL_OFF[name]
    return packed[r0:r0 + _small_rows(name)].reshape(-1)[:size].reshape(SMALL_SHAPES[name])


def _position():
    return lax.axis_index("x"), lax.axis_index("y"), lax.axis_index("c")


def _dev_index(x, y, c):
    return 4 * x + 2 * y + c


G1_PIECES, MIX_PIECES, F2_PIECES = (0, 1, 2), (3, 4), (5, 6, 7)


def _group_rows(pieces):
    return sum(PIECE_ROWS[k] for k in pieces)


def _shard_piece(s_ref, k):
    return s_ref.at[pl.ds(PIECE_OFF[k], PIECE_ROWS[k]), :]


def _shard_group(s_ref, pieces):
    return s_ref.at[pl.ds(PIECE_OFF[pieces[0]], _group_rows(pieces)), :]


def _weight_pieces(w1_ref=None, wi_ref=None, wo_ref=None, w2_ref=None):
    arrs = {}
    if w1_ref is not None:
        arrs.update({0: w1_ref.at[0], 1: w1_ref.at[1], 2: w1_ref.at[2]})
    if wi_ref is not None:
        arrs[3] = wi_ref
    if wo_ref is not None:
        arrs[4] = wo_ref
    if w2_ref is not None:
        arrs.update({5: w2_ref.at[0], 6: w2_ref.at[1], 7: w2_ref.at[2]})
    return arrs


def _block_rows(arrs, k, dev):
    r = PIECE_ROWS[k]
    return arrs[k].at[pl.ds(pl.multiple_of(_dev_index(*dev) * r, 16), r), :]


def _all_gather_ffn1(shard):
    pieces = G1_PIECES
    half = FS // 2
    SIB, X0, X1, Y0, Y1, RELAY_Y, RELAY_X, ON_X, ON_Y, ON_D0, ON_D1 = range(11)

    def body(s_ref, w1_ref, send_sems, recv_sems, local_sem):
        x, y, c = _position()
        me, sib = (x, y, c), (x, y, 1 - c)
        xn, yn, dg = (1 - x, y, c), (x, 1 - y, c), (1 - x, 1 - y, c)
        arrs = _weight_pieces(w1_ref=w1_ref)

        def rows_of(k, block, hf):
            r = PIECE_ROWS[k]
            start, size = (0, r) if hf is None else (hf * half, half)
            return arrs[k].at[pl.ds(pl.multiple_of(_dev_index(*block) * r + start, 16), size), :]

        def copies(rel, block, hf, to, from_shard=False):
            def src(k):
                if not from_shard:
                    return rows_of(k, block, hf)
                start, size = (0, PIECE_ROWS[k]) if hf is None else (hf * half, half)
                return s_ref.at[pl.ds(PIECE_OFF[k] + start, size), :]
            return [pltpu.make_async_remote_copy(
                src_ref=src(k), dst_ref=rows_of(k, block, hf), send_sem=send_sems.at[rel], recv_sem=recv_sems.at[rel],
                device_id=to, device_id_type=MESH) for k in pieces]

        def waiter(rel, hf):
            nrows = len(pieces) * (FS if hf is None else half)
            grp = s_ref.at[pl.ds(0, nrows), :]
            return pltpu.make_async_remote_copy(src_ref=grp, dst_ref=grp, send_sem=send_sems.at[rel],
                                                recv_sem=recv_sems.at[rel], device_id=me, device_id_type=MESH)

        def start(cps):
            for cp in cps:
                cp.start()

        mine = [pltpu.make_async_copy(_shard_piece(s_ref, k), _block_rows(arrs, k, me), local_sem) for k in pieces]
        start(mine)
        start(copies(SIB, me, None, sib, True))
        start(copies(X0, me, 0, xn, True))
        start(copies(Y1, me, 1, yn, True))
        start(copies(X1, me, 1, xn, True))
        start(copies(Y0, me, 0, yn, True))
        waiter(X0, 0).wait_recv()
        start(copies(RELAY_Y, xn, 0, yn))
        waiter(Y1, 1).wait_recv()
        start(copies(RELAY_X, yn, 1, xn))
        waiter(X1, 1).wait_recv()
        start(copies(ON_X, xn, None, sib))
        waiter(Y0, 0).wait_recv()
        start(copies(ON_Y, yn, None, sib))
        waiter(RELAY_Y, 0).wait_recv()
        start(copies(ON_D0, dg, 0, sib))
        waiter(RELAY_X, 1).wait_recv()
        start(copies(ON_D1, dg, 1, sib))
        waiter(SIB, None).wait_recv()
        waiter(ON_X, None).wait_recv()
        waiter(ON_Y, None).wait_recv()
        waiter(ON_D0, 0).wait_recv()
        waiter(ON_D1, 1).wait_recv()
        for rel, hf in ((SIB, None), (X0, 0), (X1, 1), (Y0, 0), (Y1, 1), (RELAY_Y, 0), (RELAY_X, 1),
                        (ON_X, None), (ON_Y, None), (ON_D0, 0), (ON_D1, 1)):
            waiter(rel, hf).wait_send()
        grp = _shard_group(s_ref, pieces)
        pltpu.make_async_copy(grp, grp, local_sem).wait()

    hbm = pl.BlockSpec(memory_space=pl.ANY)
    return pl.pallas_call(
        body, in_specs=[hbm], out_specs=hbm, out_shape=SDS((3, F, D), bf16),
        scratch_shapes=[pltpu.SemaphoreType.DMA((11,)), pltpu.SemaphoreType.DMA((11,)), pltpu.SemaphoreType.DMA],
        compiler_params=pltpu.CompilerParams(has_side_effects=True),
        name="all_gather_ffn1")(shard)


HBM_SPEC = pl.BlockSpec(memory_space=pltpu.HBM)
SEM_SPEC = pl.BlockSpec(memory_space=pltpu.SEMAPHORE)
ANY_SPEC = pl.BlockSpec(memory_space=pl.ANY)
SPLIT_EFFECT = pltpu.SideEffectType.DATAFLOW_SIDE_EFFECTING


def _in_hbm(a):
    return pltpu.with_memory_space_constraint(a, pltpu.HBM)


def _hbm_like(a):
    return pltpu.HBM(a.shape, a.dtype)


def _place_own_rows(shard):
    pieces = MIX_PIECES + F2_PIECES

    def body(s_ref, wi_ref, wo_ref, w2_ref, buf, sems):
        x, y, c = _position()
        arrs = _weight_pieces(wi_ref=wi_ref, wo_ref=wo_ref, w2_ref=w2_ref)
        grp = _shard_group(s_ref, pieces)
        load = pltpu.make_async_copy(grp, buf, sems.at[0])
        load.start()
        load.wait()
        base = PIECE_OFF[pieces[0]]
        for k in pieces:
            pltpu.make_async_copy(buf.at[pl.ds(PIECE_OFF[k] - base, PIECE_ROWS[k]), :],
                                  _block_rows(arrs, k, (x, y, c)), sems.at[1]).start()
        pltpu.make_async_copy(grp, buf, sems.at[1]).wait()

    return pl.pallas_call(
        body, in_specs=[ANY_SPEC], out_specs=[ANY_SPEC] * 3,
        out_shape=(SDS((DIN, D), bf16), SDS((DMIX, D), bf16), SDS((3, F, D), bf16)),
        scratch_shapes=[pltpu.VMEM((_group_rows(pieces), D), bf16), pltpu.SemaphoreType.DMA((2,))],
        name="place_own_rows")(shard)


def _xor_peer(x, y, c, k):
    return (x ^ (k >> 2), y ^ ((k >> 1) & 1), c ^ (k & 1))


def _gather_rest_start(shard, wi, wo, w2, w1):
    def body(s_ref, wi_ref, wo_ref, w2_ref, w1_ref,
             ssem_m, rsem_m0, rsem_m, ssem_f, rsem_f0, rsem_f, s_o, wi_o, wo_o, w2_o, w1_o):
        x, y, c = _position()
        me, sib = (x, y, c), (x, y, 1 - c)
        chips = [(1 - x, y), (x, 1 - y), (1 - x, 1 - y)]
        arrs = _weight_pieces(wi_ref=wi_ref, wo_ref=wo_ref, w2_ref=w2_ref)
        for pieces, ssem, rsem0, rsem in ((MIX_PIECES, ssem_m, rsem_m0, rsem_m), (F2_PIECES, ssem_f, rsem_f0, rsem_f)):
            for p in pieces:
                pltpu.make_async_remote_copy(
                    src_ref=_shard_piece(s_ref, p), dst_ref=_block_rows(arrs, p, me), send_sem=ssem.at[0],
                    recv_sem=rsem0, device_id=sib, device_id_type=MESH).start()
            for j, chip in enumerate(chips):
                for p in pieces:
                    pltpu.make_async_remote_copy(
                        src_ref=_shard_piece(s_ref, p), dst_ref=_block_rows(arrs, p, me), send_sem=ssem.at[1 + j],
                        recv_sem=rsem.at[j], device_id=(*chip, c), device_id_type=MESH).start()

    dma = pltpu.SemaphoreType.DMA
    return pl.pallas_call(
        body, name="gather_rest_start",
        out_shape=(dma((4,)), dma(()), dma((3,)), dma((4,)), dma(()), dma((3,)),
                   _hbm_like(shard), _hbm_like(wi), _hbm_like(wo), _hbm_like(w2), _hbm_like(w1)),
        in_specs=(HBM_SPEC,) * 5, out_specs=(SEM_SPEC,) * 6 + (HBM_SPEC,) * 5,
        input_output_aliases={0: 6, 1: 7, 2: 8, 3: 9, 4: 10},
        compiler_params=pltpu.CompilerParams(has_side_effects=SPLIT_EFFECT),
    )(_in_hbm(shard), _in_hbm(wi), _in_hbm(wo), _in_hbm(w2), _in_hbm(w1))


def _gather_mix_pass_on(rsem_m, wi, wo, thru, after):
    def body(wi_ref, wo_ref, thru_ref, rsem, after_ref, fsend, frecv, wi_o, wo_o, thru_o):
        x, y, c = _position()
        sib = (x, y, 1 - c)
        arrs = _weight_pieces(wi_ref=wi_ref, wo_ref=wo_ref)
        both = wi_ref.at[pl.ds(0, _group_rows(MIX_PIECES)), :]
        for j, chip in enumerate([(1 - x, y), (x, 1 - y), (1 - x, 1 - y)]):
            pltpu.make_async_remote_copy(src_ref=both, dst_ref=both, send_sem=fsend.at[j], recv_sem=rsem.at[j],
                                         device_id=(x, y, c), device_id_type=MESH).wait_recv()
            for p in MIX_PIECES:
                rows = _block_rows(arrs, p, (*chip, c))
                pltpu.make_async_remote_copy(src_ref=rows, dst_ref=rows, send_sem=fsend.at[j], recv_sem=frecv.at[j],
                                             device_id=sib, device_id_type=MESH).start()

    dma = pltpu.SemaphoreType.DMA
    return pl.pallas_call(
        body, name="gather_mix_pass_on",
        out_shape=(dma((3,)), dma((3,)), _hbm_like(wi), _hbm_like(wo), _hbm_like(thru)),
        in_specs=(HBM_SPEC, HBM_SPEC, HBM_SPEC, SEM_SPEC, ANY_SPEC), out_specs=(SEM_SPEC, SEM_SPEC) + (HBM_SPEC,) * 3,
        input_output_aliases={0: 2, 1: 3, 2: 4},
        compiler_params=pltpu.CompilerParams(has_side_effects=SPLIT_EFFECT),
    )(wi, wo, _in_hbm(thru), rsem_m, after)


def _gather_mix_wait(ssem_m, rsem_m0, fsend, frecv, shard, wi, wo, after):
    def body(s_ref, wi_ref, wo_ref, ssem, rsem0, fs, fr, after_ref, s_o, wi_o, wo_o):
        x, y, c = _position()
        grp = _shard_group(s_ref, MIX_PIECES)

        def waiter(send_sem, recv_sem):
            return pltpu.make_async_remote_copy(src_ref=grp, dst_ref=grp, send_sem=send_sem, recv_sem=recv_sem,
                                                device_id=(x, y, c), device_id_type=MESH)

        waiter(ssem.at[0], rsem0).wait_recv()
        for j in range(3):
            waiter(fs.at[j], fr.at[j]).wait_recv()
        for rel in range(4):
            waiter(ssem.at[rel], rsem0).wait_send()
        for j in range(3):
            waiter(fs.at[j], fr.at[j]).wait_send()

    return pl.pallas_call(
        body, name="gather_mix_wait", out_shape=(_hbm_like(shard), _hbm_like(wi), _hbm_like(wo)),
        in_specs=(HBM_SPEC,) * 3 + (SEM_SPEC,) * 4 + (ANY_SPEC,), out_specs=(HBM_SPEC,) * 3,
        input_output_aliases={0: 0, 1: 1, 2: 2},
        compiler_params=pltpu.CompilerParams(has_side_effects=SPLIT_EFFECT),
    )(shard, wi, wo, ssem_m, rsem_m0, fsend, frecv, after)


def _gather_ffn2_pass_on(rsem_f, w2, wo, after):
    def body(w2_ref, wo_ref, rsem, after_ref, fsend, frecv, w2_o, wo_o):
        x, y, c = _position()
        sib = (x, y, 1 - c)
        chips = [(1 - x, y), (x, 1 - y), (1 - x, 1 - y)]
        arrs = _weight_pieces(w2_ref=w2_ref)
        three = w2_ref.at[0, pl.ds(0, _group_rows(F2_PIECES)), :]
        for j, chip in enumerate(chips):
            pltpu.make_async_remote_copy(src_ref=three, dst_ref=three, send_sem=fsend.at[j], recv_sem=rsem.at[j],
                                         device_id=(x, y, c), device_id_type=MESH).wait_recv()
            for p in F2_PIECES:
                rows = _block_rows(arrs, p, (*chip, c))
                pltpu.make_async_remote_copy(src_ref=rows, dst_ref=rows, send_sem=fsend.at[j], recv_sem=frecv.at[j],
                                             device_id=sib, device_id_type=MESH).start()

    dma = pltpu.SemaphoreType.DMA
    return pl.pallas_call(
        body, name="gather_ffn2_pass_on", out_shape=(dma((3,)), dma((3,)), _hbm_like(w2), _hbm_like(wo)),
        in_specs=(HBM_SPEC, HBM_SPEC, SEM_SPEC, ANY_SPEC), out_specs=(SEM_SPEC, SEM_SPEC, HBM_SPEC, HBM_SPEC),
        input_output_aliases={0: 2, 1: 3},
        compiler_params=pltpu.CompilerParams(has_side_effects=SPLIT_EFFECT),
    )(w2, wo, rsem_f, after)


def _gather_ffn2_wait(ssem_f, rsem_f0, fsend, frecv, shard, w2, after):
    def body(s_ref, w2_ref, ssem, rsem0, fs, fr, after_ref, w2_o):
        x, y, c = _position()
        grp = _shard_group(s_ref, F2_PIECES)

        def waiter(send_sem, recv_sem):
            return pltpu.make_async_remote_copy(src_ref=grp, dst_ref=grp, send_sem=send_sem, recv_sem=recv_sem,
                                                device_id=(x, y, c), device_id_type=MESH)

        waiter(ssem.at[0], rsem0).wait_recv()
        for j in range(3):
            waiter(fs.at[j], fr.at[j]).wait_recv()
        for rel in range(4):
            waiter(ssem.at[rel], rsem0).wait_send()
        for j in range(3):
            waiter(fs.at[j], fr.at[j]).wait_send()

    return pl.pallas_call(
        body, name="gather_ffn2_wait", out_shape=_hbm_like(w2),
        in_specs=(HBM_SPEC, HBM_SPEC, SEM_SPEC, SEM_SPEC, SEM_SPEC, SEM_SPEC, ANY_SPEC), out_specs=HBM_SPEC,
        input_output_aliases={1: 0},
        compiler_params=pltpu.CompilerParams(has_side_effects=SPLIT_EFFECT),
    )(shard, w2, ssem_f, rsem_f0, fsend, frecv, after)


class _GatheredWeights(_LocalWeights):
    def __init__(self, shard):
        w1 = _all_gather_ffn1(shard)
        wi, wo, w2 = _place_own_rows(shard)
        (self.ssem_m, self.rsem_m0, self.rsem_m, self.ssem_f, self.rsem_f0, self.rsem_f,
         self.shard, self.wi, self.wo, self.w2_part, self.w1) = _gather_rest_start(shard, wi, wo, w2, w1)

    def after_ffn1(self, gain, x1):
        self.fsend_m, self.frecv_m, self.wi, self.wo, gain = _gather_mix_pass_on(self.rsem_m, self.wi, self.wo, gain, x1)
        return gain

    def mix(self, after):
        self.shard, wint, wout = _gather_mix_wait(self.ssem_m, self.rsem_m0, self.fsend_m, self.frecv_m, self.shard,
                                                  self.wi, self.wo, after)
        return wint, wout

    def before_out_proj(self, wout, after):
        self.fsend, self.frecv, self.w2_part, wout = _gather_ffn2_pass_on(self.rsem_f, self.w2_part, wout, after)
        return wout

    def ffn2(self, after):
        return _gather_ffn2_wait(self.ssem_f, self.rsem_f0, self.fsend, self.frecv, self.shard, self.w2_part, after)

    def mix_ffn2_grads_ready(self, dwint, dwout, dw2, dh2):
        rx1 = lax.empty((4, RSA_ROWS, D), bf16)
        self.sa, self.ra, dwint, dwout, dw2, rx1, dh2 = _rsa_level1_start(dwint, dwout, dw2, rx1, dh2)
        self.level1 = (dwint, dwout, dw2, rx1)
        return dh2

    def before_ffn1_bwd(self, dx1b):
        dwint, dwout, dw2, rx1 = _rsa_level1_wait(self.sa, self.ra, *self.level1, dx1b)
        tx, self.acc = _rsa_chip_sums(dwint, dwout, dw2, rx1)
        rx2 = lax.empty((3, RSA_ROWS, D), bf16)
        self.sb, self.rb, self.tx, self.rx2, dx1b = _rsa_level2_start(tx, rx2, dx1b)
        return dx1b

    def mix_ffn2_grads_total(self, after):
        rx2 = _rsa_level2_wait(self.sb, self.rb, self.tx, self.rx2, after)
        return _rsa_total(self.acc, rx2)


def _reduce_scatter_ffn1_head(dw1, small_packed):
    pieces = G1_PIECES
    half = FS // 2
    hrows = len(pieces) * half
    nrows = 2 * hrows
    X_RELAY, Y_RELAY = range(2)

    def body(d1_ref, p_ref, forx_ref, fory_ref, own_ref, rx1_ref, relx_ref, rely_ref, tot_ref,
             own_buf, rx_buf, tx1, tx2, tx3, acc, sa, ra, sb, rb, lsem, pair, chips, small_send, small_recv):
        x, y, c = _position()
        me, sib = (x, y, c), (x, y, 1 - c)
        xn, yn = (1 - x, y, c), (x, 1 - y, c)
        rel_chips = [(x, y), (1 - x, y), (x, 1 - y), (1 - x, 1 - y)]
        srcs = _weight_pieces(w1_ref=d1_ref)

        my_chip = 2 * x + y
        pair[c] = p_ref[...]
        swap = pltpu.make_async_remote_copy(
            src_ref=p_ref, dst_ref=pair.at[c], send_sem=small_send.at[0], recv_sem=small_recv.at[0],
            device_id=sib, device_id_type=MESH)
        swap.start()
        small = [pltpu.make_async_remote_copy(
            src_ref=chips.at[my_chip], dst_ref=chips.at[my_chip], send_sem=small_send.at[j], recv_sem=small_recv.at[j],
            device_id=(*rel_chips[j], c), device_id_type=MESH) for j in (1, 2, 3)]

        def part(k, dev, hf):
            r = PIECE_ROWS[k]
            return srcs[k].at[pl.ds(pl.multiple_of(_dev_index(*dev) * r + hf * half, 16), half), :]

        def slot(ref, k, hf):
            return ref.at[pl.ds(hf * hrows + k * half, half), :]

        halves = [(k, hf) for hf in (0, 1) for k in pieces]

        for j in (3, 1, 2, 0):
            for k, hf in halves:
                pltpu.make_async_remote_copy(
                    src_ref=part(k, (*rel_chips[j], 1 - c), hf), dst_ref=slot(rx1_ref.at[j], k, hf),
                    send_sem=sa.at[j], recv_sem=ra.at[j], device_id=sib, device_id_type=MESH).start()

        def wait_a(j):
            return pltpu.make_async_remote_copy(src_ref=rx1_ref.at[j], dst_ref=rx1_ref.at[j], send_sem=sa.at[j],
                                                recv_sem=ra.at[j], device_id=me, device_id_type=MESH)

        def ici(rel, src, dst, to):
            return pltpu.make_async_remote_copy(src_ref=src, dst_ref=dst, send_sem=sb.at[rel], recv_sem=rb.at[rel],
                                                device_id=to, device_id_type=MESH)

        first, second = pl.ds(0, hrows), pl.ds(hrows, hrows)
        sends = {
            X_RELAY: ici(X_RELAY, tx3.at[first, :], relx_ref, xn),
            Y_RELAY: ici(Y_RELAY, tx3.at[second, :], rely_ref, yn),
        }

        swap.wait_recv()
        chips[my_chip] = pair[0] + pair[1]
        for cp in small:
            cp.start()

        def chip_sum(j, dst):
            loads = [pltpu.make_async_copy(part(k, (*rel_chips[j], c), hf), slot(own_buf, k, hf), lsem.at[0])
                     for k, hf in halves]
            for cp in loads:
                cp.start()
            wait_a(j).wait_recv()
            got = pltpu.make_async_copy(rx1_ref.at[j], rx_buf, lsem.at[1])
            got.start()
            pltpu.make_async_copy(rx_buf, rx_buf, lsem.at[0]).wait()
            got.wait()

            def add(i, carry):
                rows = pl.ds(pl.multiple_of(i * half, 16), half)
                tot = own_buf[rows, :].astype(f32) + rx_buf[rows, :].astype(f32)
                dst[rows, :] = tot.astype(dst.dtype)
                return carry

            lax.fori_loop(0, nrows // half, add, 0)

        def add_landed(landed, dst, rows0, nrows_):
            got = pltpu.make_async_copy(landed, rx_buf.at[pl.ds(0, nrows_), :], lsem.at[1])
            got.start()
            got.wait()

            def add(i, carry):
                src_rows = pl.ds(pl.multiple_of(i * half, 16), half)
                dst_rows = pl.ds(pl.multiple_of(rows0 + i * half, 16), half)
                dst[dst_rows, :] = (dst[dst_rows, :].astype(f32) + rx_buf[src_rows, :].astype(f32)).astype(dst.dtype)
                return carry

            lax.fori_loop(0, nrows_ // half, add, 0)

        chip_sum(3, tx3)
        sends[X_RELAY].start()
        sends[Y_RELAY].start()
        chip_sum(1, tx1)
        chip_sum(2, tx2)
        chip_sum(0, acc)
        own_out = pltpu.make_async_copy(acc, own_ref, lsem.at[0])
        own_out.start()
        sends[X_RELAY].wait_recv()
        add_landed(relx_ref, tx2, 0, hrows)
        sends[Y_RELAY].wait_recv()
        add_landed(rely_ref, tx1, hrows, hrows)
        own_out.wait()
        outs = [pltpu.make_async_copy(tx1, forx_ref, lsem.at[0]), pltpu.make_async_copy(tx2, fory_ref, lsem.at[1])]
        for cp in outs:
            cp.start()
        for cp in outs:
            cp.wait()
        for cp in small:
            cp.wait_recv()
        tot = (chips[0] + chips[1]) + (chips[2] + chips[3])
        tot_ref[...] = tot
        loss = jnp.sum(tot[LOSS_ROW:LOSS_ROW + 1, :], axis=-1, keepdims=True)
        tot_ref[LOSS_ROW:LOSS_ROW + 1, :] = jnp.broadcast_to(loss, (1, 128))
        for j in range(4):
            wait_a(j).wait_send()
        for cp in sends.values():
            cp.wait_send()
        swap.wait_send()
        for cp in small:
            cp.wait_send()

    hbm = pl.BlockSpec(memory_space=pl.ANY)
    vm = pl.BlockSpec(memory_space=pltpu.VMEM)
    outs = pl.pallas_call(
        body, in_specs=[hbm, vm], out_specs=[hbm] * 6 + [vm],
        out_shape=(SDS((nrows, D), bf16), SDS((nrows, D), bf16), SDS((nrows, D), f32), SDS((4, nrows, D), bf16),
                   SDS((hrows, D), bf16), SDS((hrows, D), bf16), SDS((SMALL_ROWS, 128), f32)),
        scratch_shapes=[pltpu.VMEM((nrows, D), bf16), pltpu.VMEM((nrows, D), bf16),
                        pltpu.VMEM((nrows, D), bf16), pltpu.VMEM((nrows, D), bf16), pltpu.VMEM((nrows, D), bf16),
                        pltpu.VMEM((nrows, D), f32),
                        pltpu.SemaphoreType.DMA((4,)), pltpu.SemaphoreType.DMA((4,)),
                        pltpu.SemaphoreType.DMA((2,)), pltpu.SemaphoreType.DMA((2,)), pltpu.SemaphoreType.DMA((2,)),
                        pltpu.VMEM((2, SMALL_ROWS, 128), f32), pltpu.VMEM((4, SMALL_ROWS, 128), f32),
                        pltpu.SemaphoreType.DMA((4,)), pltpu.SemaphoreType.DMA((4,))],
        compiler_params=pltpu.CompilerParams(has_side_effects=True, vmem_limit_bytes=VMEM_LIMIT_V7X),
        name="reduce_scatter_ffn1_head")(dw1, small_packed)
    return outs[0], outs[1], outs[2], outs[-1]


def _rs1_tail_start(for_x, for_y, from_x, from_y, thru):
    def body(fx_ref, fy_ref, lx_ref, ly_ref, thru_ref, ssem, rsem, fx_o, fy_o, lx_o, ly_o, thru_o):
        x, y, c = _position()
        pltpu.make_async_remote_copy(src_ref=fx_ref, dst_ref=lx_ref, send_sem=ssem.at[0], recv_sem=rsem.at[0],
                                     device_id=(1 - x, y, c), device_id_type=MESH).start()
        pltpu.make_async_remote_copy(src_ref=fy_ref, dst_ref=ly_ref, send_sem=ssem.at[1], recv_sem=rsem.at[1],
                                     device_id=(x, 1 - y, c), device_id_type=MESH).start()

    dma = pltpu.SemaphoreType.DMA
    arrs = (for_x, for_y, from_x, from_y, thru)
    return pl.pallas_call(
        body, name="rs1_tail_start", out_shape=(dma((2,)), dma((2,))) + tuple(_hbm_like(a) for a in arrs),
        in_specs=(HBM_SPEC,) * 5, out_specs=(SEM_SPEC,) * 2 + (HBM_SPEC,) * 5,
        input_output_aliases={0: 2, 1: 3, 2: 4, 3: 5, 4: 6},
        compiler_params=pltpu.CompilerParams(has_side_effects=SPLIT_EFFECT),
    )(*[_in_hbm(a) for a in arrs])


def _rs1_tail_wait(ssem, rsem, for_x, for_y, from_x, from_y, after):
    def body(fx_ref, fy_ref, lx_ref, ly_ref, ssem_ref, rsem_ref, after_ref, lx_o, ly_o):
        x, y, c = _position()
        for j, (src, dst) in enumerate(((fx_ref, lx_ref), (fy_ref, ly_ref))):
            d = pltpu.make_async_remote_copy(src_ref=src, dst_ref=dst, send_sem=ssem_ref.at[j], recv_sem=rsem_ref.at[j],
                                             device_id=(x, y, c), device_id_type=MESH)
            d.wait_recv()
            d.wait_send()

    return pl.pallas_call(
        body, name="rs1_tail_wait", out_shape=(_hbm_like(from_x), _hbm_like(from_y)),
        in_specs=(HBM_SPEC,) * 4 + (SEM_SPEC, SEM_SPEC, ANY_SPEC), out_specs=(HBM_SPEC, HBM_SPEC),
        input_output_aliases={2: 0, 3: 1},
        compiler_params=pltpu.CompilerParams(has_side_effects=SPLIT_EFFECT),
    )(for_x, for_y, from_x, from_y, ssem, rsem, after)


def _rs1_total(own, from_x, from_y):
    half = FS // 2
    npiece = len(G1_PIECES)

    def body(a_ref, x_ref, y_ref, o_ref):
        o_ref[...] = (a_ref[...] + x_ref[...].astype(f32)) + y_ref[...].astype(f32)

    blk = pl.BlockSpec((half, D), lambda i: (i, 0))
    return pl.pallas_call(
        body, grid=(2 * npiece,), in_specs=[blk, blk, blk],
        out_specs=pl.BlockSpec((half, D), lambda i: (2 * (i % npiece) + i // npiece, 0)),
        out_shape=SDS((npiece * FS, D), f32), name="rs1_total")(own, from_x, from_y)


RSA_PIECES = MIX_PIECES + F2_PIECES
RSA_ROWS = _group_rows(RSA_PIECES)
RSA_OFF = {k: PIECE_OFF[k] - PIECE_OFF[RSA_PIECES[0]] for k in RSA_PIECES}
RSA_BLOCK = 192


def _rsa_rows(ref, k):
    return ref.at[pl.ds(RSA_OFF[k], PIECE_ROWS[k]), :]


def _rsa_level1_start(dwint, dwout, dw2, rx1, thru):
    def body(di_ref, do_ref, d2_ref, rx1_ref, thru_ref, sa, ra, di_o, do_o, d2_o, rx1_o, thru_o):
        x, y, c = _position()
        srcs = _weight_pieces(wi_ref=di_ref, wo_ref=do_ref, w2_ref=d2_ref)
        for j, chip in enumerate([(x, y), (1 - x, y), (x, 1 - y), (1 - x, 1 - y)]):
            for k in RSA_PIECES:
                pltpu.make_async_remote_copy(
                    src_ref=_block_rows(srcs, k, (*chip, 1 - c)), dst_ref=_rsa_rows(rx1_ref.at[j], k),
                    send_sem=sa.at[j], recv_sem=ra.at[j], device_id=(x, y, 1 - c), device_id_type=MESH).start()

    dma = pltpu.SemaphoreType.DMA
    arrs = (dwint, dwout, dw2, rx1, thru)
    return pl.pallas_call(
        body, name="rsa_level1_start", out_shape=(dma((4,)), dma((4,))) + tuple(_hbm_like(a) for a in arrs),
        in_specs=(HBM_SPEC,) * 5, out_specs=(SEM_SPEC,) * 2 + (HBM_SPEC,) * 5,
        input_output_aliases={0: 2, 1: 3, 2: 4, 3: 5, 4: 6},
        compiler_params=pltpu.CompilerParams(has_side_effects=SPLIT_EFFECT),
    )(*[_in_hbm(a) for a in arrs])


def _rsa_level1_wait(sa, ra, dwint, dwout, dw2, rx1, after):
    def body(di_ref, do_ref, d2_ref, rx1_ref, sa_ref, ra_ref, after_ref, di_o, do_o, d2_o, rx1_o):
        x, y, c = _position()
        for j in range(4):
            d = pltpu.make_async_remote_copy(src_ref=rx1_ref.at[j], dst_ref=rx1_ref.at[j], send_sem=sa_ref.at[j],
                                             recv_sem=ra_ref.at[j], device_id=(x, y, c), device_id_type=MESH)
            d.wait_recv()
            d.wait_send()

    arrs = (dwint, dwout, dw2, rx1)
    return pl.pallas_call(
        body, name="rsa_level1_wait", out_shape=tuple(_hbm_like(a) for a in arrs),
        in_specs=(HBM_SPEC,) * 4 + (SEM_SPEC, SEM_SPEC, ANY_SPEC), out_specs=(HBM_SPEC,) * 4,
        input_output_aliases={0: 0, 1: 1, 2: 2, 3: 3},
        compiler_params=pltpu.CompilerParams(has_side_effects=SPLIT_EFFECT),
    )(*arrs, sa, ra, after)


def _rsa_chip_sums(dwint, dwout, dw2, rx1):
    nblk = RSA_ROWS // RSA_BLOCK

    def body(di_ref, do_ref, d2_ref, rx1_ref, tx_ref, acc_ref, own_buf, rx_buf, tx_buf, acc_buf, lsems):
        x, y, c = _position()
        srcs = _weight_pieces(wi_ref=di_ref, wo_ref=do_ref, w2_ref=d2_ref)
        for j, chip in enumerate([(x, y), (1 - x, y), (x, 1 - y), (1 - x, 1 - y)]):
            loads = [pltpu.make_async_copy(_block_rows(srcs, k, (*chip, c)), _rsa_rows(own_buf, k), lsems.at[0])
                     for k in RSA_PIECES]
            got = pltpu.make_async_copy(rx1_ref.at[j], rx_buf, lsems.at[1])
            for cp in loads + [got]:
                cp.start()
            pltpu.make_async_copy(rx_buf, rx_buf, lsems.at[0]).wait()
            got.wait()

            def add(i, carry, j=j):
                rows = pl.ds(pl.multiple_of(i * RSA_BLOCK, 16), RSA_BLOCK)
                tot = own_buf[rows, :].astype(f32) + rx_buf[rows, :].astype(f32)
                if j == 0:
                    acc_buf[rows, :] = tot
                else:
                    tx_buf[rows, :] = tot.astype(bf16)
                return carry

            lax.fori_loop(0, nblk, add, 0)
            out = (pltpu.make_async_copy(acc_buf, acc_ref, lsems.at[2]) if j == 0
                   else pltpu.make_async_copy(tx_buf, tx_ref.at[j - 1], lsems.at[2]))
            out.start()
            out.wait()

    return pl.pallas_call(
        body, in_specs=[ANY_SPEC] * 4, out_specs=[ANY_SPEC] * 2,
        out_shape=(SDS((3, RSA_ROWS, D), bf16), SDS((RSA_ROWS, D), f32)),
        scratch_shapes=[pltpu.VMEM((RSA_ROWS, D), bf16), pltpu.VMEM((RSA_ROWS, D), bf16),
                        pltpu.VMEM((RSA_ROWS, D), bf16), pltpu.VMEM((RSA_ROWS, D), f32),
                        pltpu.SemaphoreType.DMA((3,))],
        compiler_params=_cparams(None, VMEM_LIMIT_V7X), name="rsa_chip_sums")(dwint, dwout, dw2, rx1)


def _rsa_level2_start(tx, rx2, thru):
    def body(tx_ref, rx2_ref, thru_ref, sb, rb, tx_o, rx2_o, thru_o):
        x, y, c = _position()
        for j, chip in enumerate([(1 - x, y), (x, 1 - y), (1 - x, 1 - y)]):
            pltpu.make_async_remote_copy(src_ref=tx_ref.at[j], dst_ref=rx2_ref.at[j], send_sem=sb.at[j],
                                         recv_sem=rb.at[j], device_id=(*chip, c), device_id_type=MESH).start()

    dma = pltpu.SemaphoreType.DMA
    arrs = (tx, rx2, thru)
    return pl.pallas_call(
        body, name="rsa_level2_start", out_shape=(dma((3,)), dma((3,))) + tuple(_hbm_like(a) for a in arrs),
        in_specs=(HBM_SPEC,) * 3, out_specs=(SEM_SPEC,) * 2 + (HBM_SPEC,) * 3,
        input_output_aliases={0: 2, 1: 3, 2: 4},
        compiler_params=pltpu.CompilerParams(has_side_effects=SPLIT_EFFECT),
    )(*[_in_hbm(a) for a in arrs])


def _rsa_level2_wait(sb, rb, tx, rx2, after):
    def body(tx_ref, rx2_ref, sb_ref, rb_ref, after_ref, rx2_o):
        x, y, c = _position()
        for j in range(3):
            d = pltpu.make_async_remote_copy(src_ref=tx_ref.at[j], dst_ref=rx2_ref.at[j], send_sem=sb_ref.at[j],
                                             recv_sem=rb_ref.at[j], device_id=(x, y, c), device_id_type=MESH)
            d.wait_recv()
            d.wait_send()

    return pl.pallas_call(
        body, name="rsa_level2_wait", out_shape=_hbm_like(rx2),
        in_specs=(HBM_SPEC, HBM_SPEC, SEM_SPEC, SEM_SPEC, ANY_SPEC), out_specs=HBM_SPEC,
        input_output_aliases={1: 0},
        compiler_params=pltpu.CompilerParams(has_side_effects=SPLIT_EFFECT),
    )(tx, rx2, sb, rb, after)


def _rsa_total(acc, rx2):
    def body(a_ref, r_ref, o_ref):
        o_ref[...] = ((a_ref[...] + r_ref[0].astype(f32)) + r_ref[1].astype(f32)) + r_ref[2].astype(f32)

    return pl.pallas_call(
        body, grid=(RSA_ROWS // RSA_BLOCK,),
        in_specs=[pl.BlockSpec((RSA_BLOCK, D), lambda i: (i, 0)), pl.BlockSpec((3, RSA_BLOCK, D), lambda i: (0, i, 0))],
        out_specs=pl.BlockSpec((RSA_BLOCK, D), lambda i: (i, 0)),
        out_shape=SDS((RSA_ROWS, D), f32), name="rsa_total")(acc, rx2)


def _adamw_math(w, g, m, v):
    m = ADAM_B1 * m + (1.0 - ADAM_B1) * g
    v = ADAM_B2 * v + (1.0 - ADAM_B2) * (g * g)
    m_hat = m / (1.0 - ADAM_B1 ** ADAM_STEP)
    v_hat = v / (1.0 - ADAM_B2 ** ADAM_STEP)
    delta = -ADAM_LR * (m_hat / (jnp.sqrt(v_hat) + ADAM_EPS) + ADAM_WD * w)
    return delta, m, v


def _adamw_big(pieces, ws, ms, vs, red, name):
    npiece = len(pieces)
    rmax = max(PIECE_ROWS[k] for k in pieces)

    def body(*refs):
        ins = (refs[0:npiece], refs[npiece:2 * npiece], refs[2 * npiece:3 * npiece])
        red_ref = refs[3 * npiece]
        out_refs = refs[3 * npiece + 1:7 * npiece + 1]
        inb, outb, in_sems, out_sems = refs[7 * npiece + 1:]

        def grad_rows(k):
            if k in G1_PIECES:
                return red_ref.at[pl.ds(PIECE_OFF[k], PIECE_ROWS[k]), :]
            return _rsa_rows(red_ref, k)

        def loads(i):
            s, k = i % 2, pieces[i]
            r = PIECE_ROWS[k]
            cps = [pltpu.make_async_copy(ins[q][i].at[0], inb.at[s, q, pl.ds(0, r), :], in_sems.at[4 * s + q])
                   for q in range(3)]
            cps.append(pltpu.make_async_copy(grad_rows(k), inb.at[s, 3, pl.ds(0, r), :], in_sems.at[4 * s + 3]))
            return cps

        def stores(i):
            s, r = i % 2, PIECE_ROWS[pieces[i]]
            return [pltpu.make_async_copy(outb.at[s, q, pl.ds(0, r), :], out_refs[q * npiece + i].at[0],
                                          out_sems.at[4 * s + q]) for q in range(4)]

        for cp in loads(0):
            cp.start()
        for i in range(npiece):
            s, r = i % 2, PIECE_ROWS[pieces[i]]
            if i + 1 < npiece:
                for cp in loads(i + 1):
                    cp.start()
            for cp in loads(i):
                cp.wait()
            if i >= 2:
                for cp in stores(i - 2):
                    cp.wait()
            g = inb[s, 3, 0:r, :]
            d, nm, nv = _adamw_math(inb[s, 0, 0:r, :], g, inb[s, 1, 0:r, :], inb[s, 2, 0:r, :])
            outb[s, 0, 0:r, :] = g
            outb[s, 1, 0:r, :] = d
            outb[s, 2, 0:r, :] = nm
            outb[s, 3, 0:r, :] = nv
            for cp in stores(i):
                cp.start()
        for i in range(max(npiece - 2, 0), npiece):
            for cp in stores(i):
                cp.wait()

    hbm = pl.BlockSpec(memory_space=pl.ANY)
    outs = pl.pallas_call(
        body, in_specs=[hbm] * (3 * npiece + 1), out_specs=[hbm] * (4 * npiece),
        out_shape=tuple(SDS(w.shape, f32) for _ in range(4) for w in ws),
        scratch_shapes=[pltpu.VMEM((2, 4, rmax, D), f32), pltpu.VMEM((2, 4, rmax, D), f32),
                        pltpu.SemaphoreType.DMA((8,)), pltpu.SemaphoreType.DMA((8,))],
        compiler_params=_cparams(None, VMEM_LIMIT_V7X), name=name)(*ws, *ms, *vs, red)
    return [list(outs[q * npiece:(q + 1) * npiece]) for q in range(4)]


def _adamw_small(w, m, v, g, name):
    def body(w_ref, m_ref, v_ref, g_ref, d_ref, nm_ref, nv_ref):
        d, nm, nv = _adamw_math(w_ref[...], g_ref[...], m_ref[...], v_ref[...])
        d_ref[...] = d
        nm_ref[...] = nm
        nv_ref[...] = nv

    return pl.pallas_call(
        body, out_shape=tuple(SDS(w.shape, f32) for _ in range(3)), name=name)(w, m, v, g)


WEIGHTS = ("ffn1_norm", "ffn1_w_gate", "ffn1_w_up", "ffn1_w_down", "mix_norm", "w_in", "q_norm", "k_norm",
           "attn_sinks", "rel_bias", "pool_w", "pool_scale", "w_out", "ffn2_norm", "ffn2_w_gate", "ffn2_w_up",
           "ffn2_w_down")
BIG = (("ffn1_w_gate", True), ("ffn1_w_up", True), ("ffn1_w_down", False), ("w_in", True), ("w_out", False),
       ("ffn2_w_gate", True), ("ffn2_w_up", True), ("ffn2_w_down", False))


def kernel(x, ffn1_norm, ffn1_w_gate, ffn1_w_up, ffn1_w_down, mix_norm, w_in, q_norm, k_norm, attn_sinks, rel_bias, pool_w, pool_scale, w_out, ffn2_norm, ffn2_w_gate, ffn2_w_up, ffn2_w_down, loss_target, m_ffn1_norm, m_ffn1_w_gate, m_ffn1_w_up, m_ffn1_w_down, m_mix_norm, m_w_in, m_q_norm, m_k_norm, m_attn_sinks, m_rel_bias, m_pool_w, m_pool_scale, m_w_out, m_ffn2_norm, m_ffn2_w_gate, m_ffn2_w_up, m_ffn2_w_down, v_ffn1_norm, v_ffn1_w_gate, v_ffn1_w_up, v_ffn1_w_down, v_mix_norm, v_w_in, v_q_norm, v_k_norm, v_attn_sinks, v_rel_bias, v_pool_w, v_pool_scale, v_w_out, v_ffn2_norm, v_ffn2_w_gate, v_ffn2_w_up, v_ffn2_w_down):
    args = dict(locals())
    w = {n: args[n] for n in WEIGHTS}
    m = {n: args["m_" + n] for n in WEIGHTS}
    v = {n: args["v_" + n] for n in WEIGHTS}

    as_rows = lambda a, tr: jnp.swapaxes(a, 1, 2) if tr else a
    shard = jnp.concatenate([as_rows(w[n], tr)[0].astype(bf16) for n, tr in BIG], axis=0)
    exchanges = _GatheredWeights(shard)
    (dh1, dx1), (dw1, _, _, _), small = _local_step(
        x[0], loss_target[0], exchanges, ffn1_norm, mix_norm, ffn2_norm, q_norm, k_norm, attn_sinks,
        rel_bias, pool_w[0], pool_scale)

    nrows1 = len(G1_PIECES) * FS
    for_x, for_y, own1, small_tot = _reduce_scatter_ffn1_head(dw1, _pack_small(small))
    ssem, rsem, for_x, for_y, from_x, from_y, g1 = _rs1_tail_start(
        for_x, for_y, lax.empty((nrows1, D), bf16), lax.empty((nrows1, D), bf16), ffn1_norm)
    gx, _, _ = _norm_bwd(dh1, x[0], g1, dx1, 1.0, "norm1_bwd")
    red_rest = exchanges.mix_ffn2_grads_total(gx)

    grads, deltas, new_m, new_v = {}, {}, {}, {}
    rest = [k for k in range(len(BIG)) if k not in G1_PIECES]
    rows_of = lambda t, ks: [as_rows(t[BIG[k][0]], BIG[k][1]) for k in ks]
    rest_out = _adamw_big(rest, rows_of(w, rest), rows_of(m, rest), rows_of(v, rest), red_rest, "adamw_rest")
    from_x, from_y = _rs1_tail_wait(ssem, rsem, for_x, for_y, from_x, from_y, rest_out[0][0])
    red1 = _rs1_total(own1, from_x, from_y)
    ffn1 = list(G1_PIECES)
    ffn1_out = _adamw_big(ffn1, rows_of(w, ffn1), rows_of(m, ffn1), rows_of(v, ffn1), red1, "adamw_ffn1")
    for ks, out in ((rest, rest_out), (ffn1, ffn1_out)):
        for i, k in enumerate(ks):
            n, tr = BIG[k]
            grads[n], deltas[n], new_m[n], new_v[n] = [as_rows(o[i], tr) for o in out]
    small_names = [n for n in SMALL_NAMES if n != "loss"]
    ds, nms, nvs = _adamw_small(_pack_small({n: w[n] for n in small_names}), _pack_small({n: m[n] for n in small_names}),
                                _pack_small({n: v[n] for n in small_names}), small_tot, "adamw_small")
    for n in small_names:
        grads[n] = _unpack_small(small_tot, n)
        deltas[n], new_m[n], new_v[n] = _unpack_small(ds, n), _unpack_small(nms, n), _unpack_small(nvs, n)
    loss = small_tot[LOSS_ROW, 0]
    return (loss, gx[None], *[grads[n] for n in WEIGHTS], *[deltas[n] for n in WEIGHTS],
            *[new_m[n] for n in WEIGHTS], *[new_v[n] for n in WEIGHTS])
```

```python
import functools

import jax
import jax.numpy as jnp
import numpy as np
from jax import lax
from jax.experimental import pallas as pl
from jax.experimental.pallas import tpu as pltpu

f32, bf16, i32 = jnp.float32, jnp.bfloat16, jnp.int32
SDS = jax.ShapeDtypeStruct

D = 1024
F = 2816
HD = 64
NH = 8
NKV = 2
GQA = NH // NKV
DATTN = NH * HD
DKV = NKV * HD
DPOOL = 512
POOL_WINDOWS = (2, 4, 8, 16)
PGD = DPOOL // len(POOL_WINDOWS)
DIN = DATTN + 2 * DKV + DPOOL
DMIX = DATTN + DPOOL
BLK = 128
NBUCK = 32
MAX_DISTANCE = 128
EPS = 1e-6
NEG = -1e30
SCALE = HD ** -0.5

ADAM_LR, ADAM_B1, ADAM_B2, ADAM_EPS, ADAM_WD, ADAM_STEP = 0.001, 0.9, 0.999, 1e-08, 0.01, 10

NDEV = 8
FS = F // NDEV
INS = DIN // NDEV
OUTS = DMIX // NDEV
PIECE_ROWS = (FS, FS, FS, INS, OUTS, FS, FS, FS)
PIECE_OFF = tuple(int(v) for v in np.cumsum((0,) + PIECE_ROWS[:-1]))
PACK_ROWS = sum(PIECE_ROWS)

VMEM_LIMIT_V7X = 56 * 1024 * 1024

MESH = pl.DeviceIdType.MESH


def _cparams(sem=None, vmem=None):
    return pltpu.CompilerParams(dimension_semantics=sem, vmem_limit_bytes=vmem)


def _nt(a, b):
    return lax.dot_general(a, b, (((1,), (1,)), ((), ())), preferred_element_type=f32)


def _tn(a, b):
    return lax.dot_general(a, b, (((0,), (0,)), ((), ())), preferred_element_type=f32)


def _nn(a, b):
    return jnp.dot(a, b, preferred_element_type=f32)


def _sigmoid(x):
    return 1.0 / (1.0 + jnp.exp(-x))


def _norm_fwd(x, g, name):
    T = x.shape[0]
    tm = min(512, T)

    def body(x_ref, g_ref, h_ref):
        xv = x_ref[...]
        r = lax.rsqrt(jnp.mean(xv * xv, axis=-1, keepdims=True) + EPS)
        h_ref[...] = (xv * r * g_ref[...]).astype(bf16)

    return pl.pallas_call(
        body, grid=(T // tm,),
        in_specs=[pl.BlockSpec((tm, D), lambda i: (i, 0)), pl.BlockSpec((1, D), lambda i: (0, 0))],
        out_specs=pl.BlockSpec((tm, D), lambda i: (i, 0)),
        out_shape=SDS((T, D), bf16), name=name)(x, g)


def _norm_bwd(dh, x, g, dres, out_scale, name):
    T = x.shape[0]
    tm = min(512, T)

    def body(dh_ref, x_ref, g_ref, dr_ref, dx_ref, dxb_ref, dg_ref):
        i = pl.program_id(0)
        xv = x_ref[...]
        r = lax.rsqrt(jnp.mean(xv * xv, axis=-1, keepdims=True) + EPS)
        xh = xv * r
        dhv = dh_ref[...]
        dxh = dhv * g_ref[...]
        dx = dr_ref[...] + r * (dxh - xh * jnp.mean(dxh * xh, axis=-1, keepdims=True))
        dx_ref[...] = dx
        dxb_ref[...] = (out_scale * dx).astype(bf16)
        dg = jnp.sum(dhv * xh, axis=0, keepdims=True)

        @pl.when(i == 0)
        def _():
            dg_ref[...] = dg

        @pl.when(i > 0)
        def _():
            dg_ref[...] += dg

    tok = pl.BlockSpec((tm, D), lambda i: (i, 0))
    vec = pl.BlockSpec((1, D), lambda i: (0, 0))
    return pl.pallas_call(
        body, grid=(T // tm,),
        in_specs=[tok, tok, vec, tok], out_specs=[tok, tok, vec],
        out_shape=(SDS((T, D), f32), SDS((T, D), bf16), SDS((1, D), f32)),
        compiler_params=_cparams(("arbitrary",)), name=name)(dh, x, g, dres)


def _gain_grad(dh, x, name):
    T = x.shape[0]
    tm = min(512, T)

    def body(dh_ref, x_ref, dg_ref):
        i = pl.program_id(0)
        xv = x_ref[...]
        r = lax.rsqrt(jnp.mean(xv * xv, axis=-1, keepdims=True) + EPS)
        dg = jnp.sum(dh_ref[...] * (xv * r), axis=0, keepdims=True)

        @pl.when(i == 0)
        def _():
            dg_ref[...] = dg

        @pl.when(i > 0)
        def _():
            dg_ref[...] += dg

    tok = pl.BlockSpec((tm, D), lambda i: (i, 0))
    return pl.pallas_call(
        body, grid=(T // tm,), in_specs=[tok, tok], out_specs=pl.BlockSpec((1, D), lambda i: (0, 0)),
        out_shape=SDS((1, D), f32), compiler_params=_cparams(("arbitrary",)), name=name)(dh, x)


FFN_ROW_CHUNK = 256


def _ffn_tiles(T):
    return min(1024, T), 256


def _ffn_fwd(h, w, x, target, name):
    T = h.shape[0]
    tm, tf = _ffn_tiles(T)
    nf = F // tf
    with_loss = target is not None

    def body(*refs):
        if with_loss:
            h_ref, w_ref, x_hbm, t_hbm, xo_ref, g_ref, u_ref, dyb_ref, loss_ref, tbuf, sem = refs
        else:
            h_ref, w_ref, x_hbm, xo_ref, g_ref, u_ref, sem = refs
        fi = pl.program_id(0)

        @pl.when(fi == 0)
        def _():
            cp = pltpu.make_async_copy(x_hbm, xo_ref, sem)
            cp.start()
            cp.wait()

        wgu = w_ref[0:2].reshape(2 * tf, D)
        for r in range(0, T, tm):
            rows = slice(r, r + tm)
            gu = _nt(h_ref[rows, :], wgu)
            gate, up = gu[:, :tf], gu[:, tf:]
            act = gate * _sigmoid(gate) * up
            g_ref[0, rows, :] = gate.astype(bf16)
            u_ref[0, rows, :] = up.astype(bf16)
            xo_ref[rows, :] += _nn((0.5 * act).astype(bf16), w_ref[2])

        if with_loss:
            @pl.when(fi == nf - 1)
            def _():
                lanes = jnp.zeros((1, 128), f32)
                for r in range(0, T, tm):
                    rows = slice(r, r + tm)
                    cp = pltpu.make_async_copy(t_hbm.at[pl.ds(r, tm), :], tbuf, sem)
                    cp.start()
                    cp.wait()
                    e = xo_ref[rows, :] - tbuf[...]
                    dy = e * (1.0 / D)
                    xo_ref[rows, :] = dy
                    dyb_ref[rows, :] = (0.5 * dy).astype(bf16)
                    col = jnp.sum(e * e, axis=0, keepdims=True) * (0.5 / D)
                    for k in range(D // 128):
                        lanes = lanes + col[:, 128 * k:128 * (k + 1)]
                loss_ref[...] = lanes

    tok = pl.BlockSpec((T, D), lambda f: (0, 0))
    act_spec = pl.BlockSpec((1, T, tf), lambda f: (f, 0, 0))
    hbm = pl.BlockSpec(memory_space=pl.ANY)
    in_specs = [tok, pl.BlockSpec((3, tf, D), lambda f: (0, f, 0)), hbm]
    out_specs = [tok, act_spec, act_spec]
    out_shape = [SDS((T, D), f32), SDS((nf, T, tf), bf16), SDS((nf, T, tf), bf16)]
    scratch = [pltpu.SemaphoreType.DMA]
    args = [h, w, x]
    if with_loss:
        in_specs.append(hbm)
        args.append(target)
        out_specs += [tok, pl.BlockSpec((1, 128), lambda f: (0, 0))]
        out_shape += [SDS((T, D), bf16), SDS((1, 128), f32)]
        scratch = [pltpu.VMEM((tm, D), f32)] + scratch
    return pl.pallas_call(
        body, grid=(nf,), in_specs=in_specs, out_specs=out_specs, out_shape=tuple(out_shape), scratch_shapes=scratch,
        compiler_params=_cparams(("arbitrary",), VMEM_LIMIT_V7X), name=name)(*args)


def _ffn_bwd(dob, h, gate, up, w, name):
    T = h.shape[0]
    _, tf = _ffn_tiles(T)
    nf = F // tf

    def body(do_hbm, h_hbm, g_ref, u_ref, w_ref, dh_hbm, dw_ref, do_v, h_v, dh_acc, dgu_s, act_s, sems):
        fi = pl.program_id(0)

        @pl.when(fi == 0)
        def _():
            loads = [pltpu.make_async_copy(do_hbm, do_v, sems.at[0]), pltpu.make_async_copy(h_hbm, h_v, sems.at[1])]
            for cp in loads:
                cp.start()
            dh_acc[...] = jnp.zeros_like(dh_acc)
            for cp in loads:
                cp.wait()

        wgu = w_ref[0:2].reshape(2 * tf, D)
        for r in range(0, T, FFN_ROW_CHUNK):
            rows = slice(r, r + FFN_ROW_CHUNK)
            dov = do_v[rows, :]
            gv = g_ref[0, rows, :].astype(f32)
            uv = u_ref[0, rows, :].astype(f32)
            sg = _sigmoid(gv)
            sil = gv * sg
            dact = _nt(dov, w_ref[2])
            dup = dact * sil
            dgate = dact * uv * (sg * (1.0 + gv * (1.0 - sg)))
            dgu = jnp.concatenate([dgate.astype(bf16), dup.astype(bf16)], axis=1)
            dgu_s[rows, :] = dgu
            act_s[rows, :] = (sil * uv).astype(bf16)
            dh_acc[rows, :] += _nn(dgu, wgu)
        dw_ref[0:2] = _tn(dgu_s[...], h_v[...]).reshape(2, tf, D).astype(bf16)
        dw_ref[2] = _tn(act_s[...], do_v[...]).astype(bf16)

        @pl.when(fi == nf - 1)
        def _():
            out = pltpu.make_async_copy(dh_acc, dh_hbm, sems.at[0])
            out.start()
            out.wait()

    act_spec = pl.BlockSpec((1, T, tf), lambda f: (f, 0, 0))
    wspec = pl.BlockSpec((3, tf, D), lambda f: (0, f, 0))
    hbm = pl.BlockSpec(memory_space=pl.ANY)
    return pl.pallas_call(
        body, grid=(nf,),
        in_specs=[hbm, hbm, act_spec, act_spec, wspec],
        out_specs=[hbm, wspec],
        out_shape=(SDS((T, D), f32), SDS((3, F, D), bf16)),
        scratch_shapes=[pltpu.VMEM((T, D), bf16), pltpu.VMEM((T, D), bf16), pltpu.VMEM((T, D), f32),
                        pltpu.VMEM((T, 2 * tf), bf16), pltpu.VMEM((T, tf), bf16), pltpu.SemaphoreType.DMA((2,))],
        compiler_params=_cparams(("arbitrary",), VMEM_LIMIT_V7X), name=name)(dob, h, gate, up, w)


def _in_proj_fwd(h, wint, name):
    T = h.shape[0]
    tm = min(512, T)

    def body(h_ref, w_ref, z_ref):
        z_ref[...] = _nt(h_ref[...], w_ref[...])

    return pl.pallas_call(
        body, grid=(T // tm,),
        in_specs=[pl.BlockSpec((tm, D), lambda i: (i, 0)), pl.BlockSpec((DIN, D), lambda i: (0, 0))],
        out_specs=pl.BlockSpec((tm, DIN), lambda i: (i, 0)),
        out_shape=SDS((T, DIN), f32), name=name)(h, wint)


def _in_proj_bwd(dz, wint, h, name):
    T = h.shape[0]
    tm = min(512, T)
    nt = T // tm

    def body(dz_ref, w_ref, h_ref, dh_ref, dw_ref, acc):
        i = pl.program_id(0)
        dzb = dz_ref[...].astype(bf16)
        dh_ref[...] = _nn(dzb, w_ref[...])
        part = _tn(dzb, h_ref[...])

        @pl.when(i == 0)
        def _():
            acc[...] = part

        @pl.when(i > 0)
        def _():
            acc[...] += part

        @pl.when(i == nt - 1)
        def _():
            dw_ref[...] = acc[...].astype(bf16)

    wspec = pl.BlockSpec((DIN, D), lambda i: (0, 0))
    return pl.pallas_call(
        body, grid=(nt,),
        in_specs=[pl.BlockSpec((tm, DIN), lambda i: (i, 0)), wspec, pl.BlockSpec((tm, D), lambda i: (i, 0))],
        out_specs=[pl.BlockSpec((tm, D), lambda i: (i, 0)), wspec],
        out_shape=(SDS((T, D), f32), SDS((DIN, D), bf16)),
        scratch_shapes=[pltpu.VMEM((DIN, D), f32)],
        compiler_params=_cparams(("arbitrary",)), name=name)(dz, wint, h)


def _out_proj_fwd(ymix, wout, x, g, name):
    T = x.shape[0]
    tm = min(512, T)

    def body(y_ref, w_ref, x_ref, g_ref, o_ref, h_ref):
        o = x_ref[...] + _nn(y_ref[...], w_ref[...])
        o_ref[...] = o
        r = lax.rsqrt(jnp.mean(o * o, axis=-1, keepdims=True) + EPS)
        h_ref[...] = (o * r * g_ref[...]).astype(bf16)

    tok = pl.BlockSpec((tm, D), lambda i: (i, 0))
    return pl.pallas_call(
        body, grid=(T // tm,),
        in_specs=[pl.BlockSpec((tm, DMIX), lambda i: (i, 0)), pl.BlockSpec((DMIX, D), lambda i: (0, 0)), tok,
                  pl.BlockSpec((1, D), lambda i: (0, 0))],
        out_specs=[tok, tok], out_shape=(SDS((T, D), f32), SDS((T, D), bf16)), name=name)(ymix, wout, x, g)


def _out_proj_bwd(dxb, wout, ymix, name):
    T = dxb.shape[0]
    tm = min(512, T)
    nt = T // tm

    def body(dx_ref, w_ref, y_ref, dy_ref, dw_ref, acc):
        i = pl.program_id(0)
        dxv = dx_ref[...]
        dy_ref[...] = _nt(dxv, w_ref[...])
        part = _tn(y_ref[...], dxv)

        @pl.when(i == 0)
        def _():
            acc[...] = part

        @pl.when(i > 0)
        def _():
            acc[...] += part

        @pl.when(i == nt - 1)
        def _():
            dw_ref[...] = acc[...].astype(bf16)

    wspec = pl.BlockSpec((DMIX, D), lambda i: (0, 0))
    return pl.pallas_call(
        body, grid=(nt,),
        in_specs=[pl.BlockSpec((tm, D), lambda i: (i, 0)), wspec, pl.BlockSpec((tm, DMIX), lambda i: (i, 0))],
        out_specs=[pl.BlockSpec((tm, DMIX), lambda i: (i, 0)), wspec],
        out_shape=(SDS((T, DMIX), f32), SDS((DMIX, D), bf16)),
        scratch_shapes=[pltpu.VMEM((DMIX, D), f32)],
        compiler_params=_cparams(("arbitrary",)), name=name)(dxb, wout, ymix)


def _t5_bucket_table():
    ql = np.arange(BLK)[:, None]
    kl = np.arange(2 * BLK)[None, :]
    n = np.maximum(ql + BLK - kl, 0)
    max_exact = NBUCK // 2
    large = max_exact + (np.log(np.maximum(n, 1) / max_exact) / np.log(MAX_DISTANCE / max_exact)
                         * (NBUCK - max_exact)).astype(np.int32)
    large = np.minimum(large, NBUCK - 1)
    return np.where(n < max_exact, n, large).astype(np.int32)


def _fill_bias(bk_ref, rb_ref, bias_scr):
    bk = bk_ref[...]
    for h in range(NH):
        def step(b, acc, h=h):
            return acc + jnp.where(bk == b, rb_ref[b, h], 0.0)
        bias_scr[h] = lax.fori_loop(0, NBUCK, step, jnp.zeros((BLK, 2 * BLK), f32))


MIX_SUB = 4


class _Window:
    def __init__(self, zc_ref, zp_ref, n, s):
        self.blk = n * MIX_SUB + s
        self.first_in_step = s == 0
        self.cur = lambda a, b: zc_ref[s * BLK:(s + 1) * BLK, a:b]
        self.prev = (lambda a, b: zp_ref[:, a:b]) if s == 0 else (lambda a, b: zc_ref[(s - 1) * BLK:s * BLK, a:b])


def _attn_qkv(win, kh, qg, kg):
    kc = DATTN + HD * kh
    vc = DATTN + DKV + HD * kh
    kx = jnp.concatenate([win.prev(kc, kc + HD), win.cur(kc, kc + HD)], axis=0)
    vx = jnp.concatenate([win.prev(vc, vc + HD), win.cur(vc, vc + HD)], axis=0)
    qx = jnp.concatenate([win.cur(HD * (GQA * kh + g), HD * (GQA * kh + g + 1)) for g in range(GQA)], axis=0)
    rq = lax.rsqrt(jnp.mean(qx * qx, axis=-1, keepdims=True) + EPS)
    rk = lax.rsqrt(jnp.mean(kx * kx, axis=-1, keepdims=True) + EPS)
    qhat, khat = qx * rq, kx * rk
    return dict(qhat=qhat, khat=khat, rq=rq, rk=rk, qnb=(qhat * qg).astype(bf16), knb=(khat * kg).astype(bf16),
                vb=vx.astype(bf16))


def _attn_probs(a, kh, sk_ref, bias_scr, n):
    s = _nt(a["qnb"], a["knb"]) * SCALE + bias_scr[GQA * kh:GQA * (kh + 1)].reshape(GQA * BLK, 2 * BLK)
    row = lax.broadcasted_iota(i32, (GQA * BLK, 2 * BLK), 0) & (BLK - 1)
    col = lax.broadcasted_iota(i32, (GQA * BLK, 2 * BLK), 1)
    mask = (col > row) & (col <= row + BLK) & ((col >= BLK) | (n > 0))
    s = jnp.where(mask, s, NEG)
    ridx = lax.broadcasted_iota(i32, (GQA * BLK, 1), 0)
    sink = jnp.full((GQA * BLK, 1), sk_ref[GQA * kh + GQA - 1], f32)
    for g in range(GQA - 2, -1, -1):
        sink = jnp.where(ridx < (g + 1) * BLK, sk_ref[GQA * kh + g], sink)
    m = jnp.maximum(jnp.max(s, axis=-1, keepdims=True), sink)
    e = jnp.exp(s - m)
    den = jnp.sum(e, axis=-1, keepdims=True) + jnp.exp(sink - m)
    return e / den


POOL_STEPS = {2: (1,), 4: (1, 2), 8: (1, 2, 4), 16: (1, 2, 4, 8)}


def _pool_group(win, g, w):
    n = win.blk
    c0 = DATTN + 2 * DKV + PGD * g
    uc = win.cur(c0, c0 + PGD)
    up = jnp.where(n > 0, win.prev(c0, c0 + PGD), 0.0)
    sm = jnp.concatenate([up, uc], axis=0)
    for k in POOL_STEPS[w]:
        sm = sm + pltpu.roll(sm, k, axis=0)
    pos = n * BLK + lax.broadcasted_iota(i32, (BLK, 1), 0) + 1
    cnt = jnp.minimum(pos, w).astype(f32)
    return sm[BLK:2 * BLK] / cnt - uc, cnt


def _mix_fwd(z, qg, kg, sinks, relb, bucket, pool_w, pscale, name):
    T = z.shape[0]
    step_rows = MIX_SUB * BLK
    nsteps = T // step_rows

    def body(zc_ref, zp_ref, qg_ref, kg_ref, sk_ref, rb_ref, bk_ref, pw_ref, ps_ref, y_ref, p_ref, bias_scr, yacc):
        n = pl.program_id(0)

        @pl.when(n == 0)
        def _():
            _fill_bias(bk_ref, rb_ref, bias_scr)

        for s in range(MIX_SUB):
            win = _Window(zc_ref, zp_ref, n, s)
            rows = slice(s * BLK, (s + 1) * BLK)
            for kh in range(NKV):
                a = _attn_qkv(win, kh, qg_ref[...], kg_ref[...])
                pb = _attn_probs(a, kh, sk_ref, bias_scr, win.blk).astype(bf16)
                p_ref[s, GQA * kh:GQA * (kh + 1)] = pb.reshape(GQA, BLK, 2 * BLK)
                o = _nn(pb, a["vb"])
                for g in range(GQA):
                    hc = HD * (GQA * kh + g)
                    yacc[rows, hc:hc + HD] = o[g * BLK:(g + 1) * BLK]
            for g, w in enumerate(POOL_WINDOWS):
                pooled, _ = _pool_group(win, g, w)
                yp = _nn(pooled.astype(bf16), pw_ref[g].astype(bf16)) * ps_ref[:, PGD * g:PGD * (g + 1)]
                yacc[rows, DATTN + PGD * g:DATTN + PGD * (g + 1)] = yp
        y_ref[...] = yacc[...].astype(bf16)

    full = lambda *shape: pl.BlockSpec(shape, lambda n: (0,) * len(shape))
    smem = pl.BlockSpec(memory_space=pltpu.SMEM)
    return pl.pallas_call(
        body, grid=(nsteps,),
        in_specs=[pl.BlockSpec((step_rows, DIN), lambda n: (n, 0)),
                  pl.BlockSpec((BLK, DIN), lambda n: (jnp.maximum(n * MIX_SUB - 1, 0), 0)),
                  full(1, HD), full(1, HD), smem, smem, full(BLK, 2 * BLK),
                  full(len(POOL_WINDOWS), PGD, PGD), full(1, DPOOL)],
        out_specs=[pl.BlockSpec((step_rows, DMIX), lambda n: (n, 0)),
                   pl.BlockSpec((MIX_SUB, NH, BLK, 2 * BLK), lambda n: (n, 0, 0, 0))],
        out_shape=(SDS((T, DMIX), bf16), SDS((T // BLK, NH, BLK, 2 * BLK), bf16)),
        scratch_shapes=[pltpu.VMEM((NH, BLK, 2 * BLK), f32), pltpu.VMEM((step_rows, DMIX), f32)],
        compiler_params=_cparams(("arbitrary",)), name=name)(z, z, qg, kg, sinks, relb, bucket, pool_w, pscale)


def _mix_bwd(z, dy, probs, qg, kg, relb, bucket, pool_w, pscale, name):
    T = z.shape[0]
    step_rows = MIX_SUB * BLK
    nsteps = T // step_rows

    def body(zc_ref, zp_ref, dy_ref, p_ref, qg_ref, kg_ref, bk_ref, pw_ref, ps_ref,
             dz_ref, dqg_ref, dkg_ref, dsk_ref, drb_ref, dpw_ref, dps_ref, dbias_scr):
        n = pl.program_id(0)

        @pl.when(n == 0)
        def _():
            dbias_scr[...] = jnp.zeros_like(dbias_scr)
            dqg_ref[...] = jnp.zeros_like(dqg_ref)
            dkg_ref[...] = jnp.zeros_like(dkg_ref)
            dpw_ref[...] = jnp.zeros_like(dpw_ref)
            dps_ref[...] = jnp.zeros_like(dps_ref)

        qg, kg = qg_ref[...], kg_ref[...]
        for s in range(MIX_SUB):
            win = _Window(zc_ref, zp_ref, n, s)
            blk = win.blk
            rows = pl.ds(pl.multiple_of(blk * BLK, BLK), BLK)
            prow = pl.ds(pl.multiple_of(jnp.maximum(blk - 1, 0) * BLK, BLK), BLK)
            dyr = slice(s * BLK, (s + 1) * BLK)

            def into_prev(fn, s=s):
                if s == 0:
                    pl.when(n > 0)(fn)
                else:
                    fn()

            for kh in range(NKV):
                a = _attn_qkv(win, kh, qg, kg)
                pb = p_ref[s, GQA * kh:GQA * (kh + 1)].reshape(GQA * BLK, 2 * BLK)
                p = pb.astype(f32)
                do = jnp.concatenate([dy_ref[dyr, HD * (GQA * kh + g):HD * (GQA * kh + g + 1)] for g in range(GQA)],
                                     axis=0).astype(bf16)
                dv = _tn(pb, do)
                dp = _nt(do, a["vb"])
                delta = jnp.sum(p * dp, axis=-1, keepdims=True)
                ds = p * (dp - delta)
                for g in range(GQA):
                    dbias_scr[GQA * kh + g] += ds[g * BLK:(g + 1) * BLK]
                dsb = ds.astype(bf16)
                dqn = _nn(dsb, a["knb"]) * SCALE
                dkn = _tn(dsb, a["qnb"]) * SCALE
                qhat, khat = a["qhat"], a["khat"]
                dqg_ref[...] += jnp.sum(dqn * qhat, axis=0, keepdims=True)
                dkg_ref[...] += jnp.sum(dkn * khat, axis=0, keepdims=True)
                dqh = dqn * qg
                dq = a["rq"] * (dqh - qhat * jnp.mean(dqh * qhat, axis=-1, keepdims=True))
                dkh = dkn * kg
                dk = a["rk"] * (dkh - khat * jnp.mean(dkh * khat, axis=-1, keepdims=True))
                kc = DATTN + HD * kh
                vc = DATTN + DKV + HD * kh
                for g in range(GQA):
                    hc = HD * (GQA * kh + g)
                    dz_ref[rows, hc:hc + HD] = dq[g * BLK:(g + 1) * BLK]
                dz_ref[rows, kc:kc + HD] = dk[BLK:2 * BLK]
                dz_ref[rows, vc:vc + HD] = dv[BLK:2 * BLK]

                def kv_prev(dk=dk, dv=dv, kc=kc, vc=vc, prow=prow):
                    dz_ref[prow, kc:kc + HD] += dk[0:BLK]
                    dz_ref[prow, vc:vc + HD] += dv[0:BLK]

                into_prev(kv_prev)

            for g, w in enumerate(POOL_WINDOWS):
                c0 = DATTN + 2 * DKV + PGD * g
                pooled, cnt = _pool_group(win, g, w)
                pb = pooled.astype(bf16)
                wb = pw_ref[g].astype(bf16)
                dyp = dy_ref[dyr, DATTN + PGD * g:DATTN + PGD * (g + 1)]
                ypre = _nn(pb, wb)
                dps_ref[:, PGD * g:PGD * (g + 1)] += jnp.sum(dyp * ypre, axis=0, keepdims=True)
                dyg = (dyp * ps_ref[:, PGD * g:PGD * (g + 1)]).astype(bf16)
                dpw_ref[g] += _tn(pb, dyg)
                dpooled = _nt(dyg, wb)
                due = jnp.concatenate([jnp.zeros((BLK, PGD), f32), dpooled / cnt], axis=0)
                for k in POOL_STEPS[w]:
                    due = due + pltpu.roll(due, 2 * BLK - k, axis=0)
                dz_ref[rows, c0:c0 + PGD] = due[BLK:2 * BLK] - dpooled

                def pool_prev(due=due, c0=c0, prow=prow):
                    dz_ref[prow, c0:c0 + PGD] += due[0:BLK]

                into_prev(pool_prev)

        @pl.when(n == nsteps - 1)
        def _():
            bk = bk_ref[...]
            ri = lax.broadcasted_iota(i32, (NBUCK, NH), 0)
            ci = lax.broadcasted_iota(i32, (NBUCK, NH), 1)

            def step(b, acc):
                for h in range(NH):
                    sel = jnp.where(bk == b, dbias_scr[h], 0.0)
                    tot = jnp.sum(jnp.sum(sel, axis=1, keepdims=True), axis=0, keepdims=True)
                    acc = acc + jnp.where((ri == b) & (ci == h), tot, 0.0)
                return acc

            drb_ref[...] = lax.fori_loop(0, NBUCK, step, jnp.zeros((NBUCK, NH), f32))
            lane = lax.broadcasted_iota(i32, (1, 128), 1)
            dsk = jnp.zeros((1, 128), f32)
            for h in range(NH):
                tot = jnp.sum(jnp.sum(dbias_scr[h], axis=1, keepdims=True), axis=0, keepdims=True)
                dsk = dsk - jnp.where(lane == h, tot, 0.0)
            dsk_ref[...] = dsk

    full = lambda *shape: pl.BlockSpec(shape, lambda n: (0,) * len(shape))
    npg = len(POOL_WINDOWS)
    return pl.pallas_call(
        body, grid=(nsteps,),
        in_specs=[pl.BlockSpec((step_rows, DIN), lambda n: (n, 0)),
                  pl.BlockSpec((BLK, DIN), lambda n: (jnp.maximum(n * MIX_SUB - 1, 0), 0)),
                  pl.BlockSpec((step_rows, DMIX), lambda n: (n, 0)),
                  pl.BlockSpec((MIX_SUB, NH, BLK, 2 * BLK), lambda n: (n, 0, 0, 0)),
                  full(1, HD), full(1, HD), full(BLK, 2 * BLK), full(npg, PGD, PGD), full(1, DPOOL)],
        out_specs=[full(T, DIN), full(1, HD), full(1, HD), full(1, 128), full(NBUCK, NH),
                   full(npg, PGD, PGD), full(1, DPOOL)],
        out_shape=(SDS((T, DIN), f32), SDS((1, HD), f32), SDS((1, HD), f32), SDS((1, 128), f32),
                   SDS((NBUCK, NH), f32), SDS((npg, PGD, PGD), f32), SDS((1, DPOOL), f32)),
        scratch_shapes=[pltpu.VMEM((NH, BLK, 2 * BLK), f32)],
        compiler_params=_cparams(("arbitrary",), VMEM_LIMIT_V7X),
        name=name)(z, z, dy, probs, qg, kg, bucket, pool_w, pscale)


class _LocalWeights:
    def __init__(self, w1, wint, wout, w2):
        self.w1, self.wint, self.wout, self.w2 = w1, wint, wout, w2

    def ffn1(self):
        return self.w1

    def first_norm(self, x, gain):
        return _norm_fwd(x, gain, "norm1_fwd")

    def after_ffn1(self, gain, x1):
        return gain

    def mix(self, after):
        return self.wint, self.wout

    def before_out_proj(self, wout, after):
        return wout

    def ffn2(self, after):
        return self.w2

    def mix_ffn2_grads_ready(self, dwint, dwout, dw2, dh2):
        self.grads_rest = (dwint, dwout, dw2)
        return dh2

    def before_ffn1_bwd(self, dx1b):
        return dx1b


def _local_step(x, target, weights, g1, gm, g3, qg, kg, sinks, relb, pool_w, pscale):
    bucket = jnp.asarray(_t5_bucket_table())
    sk = sinks.reshape(NH)
    w1 = weights.ffn1()
    h1 = weights.first_norm(x, g1)
    x1, gate1, up1 = _ffn_fwd(h1, w1, x, None, "ffn1_fwd")
    h2 = _norm_fwd(x1, weights.after_ffn1(gm, x1), "norm2_fwd")
    wint, wout = weights.mix(h2)
    z = _in_proj_fwd(h2, wint, "in_proj_fwd")
    ymix, probs = _mix_fwd(z, qg, kg, sk, relb, bucket, pool_w, pscale, "mix_fwd")
    wout = weights.before_out_proj(wout, ymix)
    x2, h3 = _out_proj_fwd(ymix, wout, x1, g3, "out_proj_fwd")
    w2 = weights.ffn2(h3)
    dy, gate2, up2, dyb, loss_lanes = _ffn_fwd(h3, w2, x2, target, "ffn2_fwd")

    dh3, dw2 = _ffn_bwd(dyb, h3, gate2, up2, w2, "ffn2_bwd")
    dx2, dx2b, dg3 = _norm_bwd(dh3, x2, g3, dy, 1.0, "norm3_bwd")
    dymix, dwout = _out_proj_bwd(dx2b, wout, ymix, "out_proj_bwd")
    dz, dqg, dkg, dsk, drb, dpw, dps = _mix_bwd(z, dymix, probs, qg, kg, relb, bucket, pool_w, pscale, "mix_bwd")
    dh2, dwint = _in_proj_bwd(dz, wint, h2, "in_proj_bwd")
    dh2 = weights.mix_ffn2_grads_ready(dwint, dwout, dw2, dh2)
    dx1, dx1b, dgm = _norm_bwd(dh2, x1, gm, dx2, 0.5, "norm2_bwd")
    dx1b = weights.before_ffn1_bwd(dx1b)
    dh1, dw1 = _ffn_bwd(dx1b, h1, gate1, up1, w1, "ffn1_bwd")
    dg1 = _gain_grad(dh1, x, "norm1_gain_grad")
    small = dict(ffn1_norm=dg1, mix_norm=dgm, ffn2_norm=dg3, pool_scale=dps, q_norm=dqg, k_norm=dkg,
                 attn_sinks=dsk[:, :NH], rel_bias=drb, pool_w=dpw, loss=loss_lanes)
    return (dh1, dx1), (dw1, dwint, dwout, dw2), small


SMALL_NAMES = ("ffn1_norm", "mix_norm", "ffn2_norm", "pool_scale", "q_norm", "k_norm", "attn_sinks", "rel_bias",
               "pool_w", "loss")
SMALL_SHAPES = dict(ffn1_norm=(1, D), mix_norm=(1, D), ffn2_norm=(1, D), pool_scale=(1, DPOOL), q_norm=(1, HD),
                    k_norm=(1, HD), attn_sinks=(1, NH), rel_bias=(NBUCK, NH),
                    pool_w=(1, len(POOL_WINDOWS), PGD, PGD), loss=(1, 128))


def _small_rows(name):
    return -(-int(np.prod(SMALL_SHAPES[name])) // 128)


SMALL_OFF = {}
_r = 0
for _n in SMALL_NAMES:
    SMALL_OFF[_n] = _r
    _r += _small_rows(_n)
SMALL_ROWS = -(-_r // 8) * 8
LOSS_ROW = SMALL_OFF["loss"]


def _pack_small(vals):
    parts = []
    for n in SMALL_NAMES:
        size = _small_rows(n) * 128
        if n in vals:
            flat = vals[n].astype(f32).reshape(-1)
            parts.append(jnp.pad(flat, (0, size - flat.shape[0])))
        else:
            parts.append(jnp.zeros((size,), f32))
    flat = jnp.concatenate(parts)
    flat = jnp.pad(flat, (0, SMALL_ROWS * 128 - flat.shape[0]))
    return flat.reshape(SMALL_ROWS, 128)


def _unpack_small(packed, name):
    size = int(np.prod(SMALL_SHAPES[name]))
    r0 = SMALL_OFF[name]
    return packed[r0:r0 + _small_rows(name)].reshape(-1)[:size].reshape(SMALL_SHAPES[name])


def _position():
    return lax.axis_index("x"), lax.axis_index("y"), lax.axis_index("c")


def _dev_index(x, y, c):
    return 4 * x + 2 * y + c


G1_PIECES, MIX_PIECES, F2_PIECES = (0, 1, 2), (3, 4), (5, 6, 7)


def _group_rows(pieces):
    return sum(PIECE_ROWS[k] for k in pieces)


def _shard_piece(s_ref, k):
    return s_ref.at[pl.ds(PIECE_OFF[k], PIECE_ROWS[k]), :]


def _shard_group(s_ref, pieces):
    return s_ref.at[pl.ds(PIECE_OFF[pieces[0]], _group_rows(pieces)), :]


def _weight_pieces(w1_ref=None, wi_ref=None, wo_ref=None, w2_ref=None):
    arrs = {}
    if w1_ref is not None:
        arrs.update({0: w1_ref.at[0], 1: w1_ref.at[1], 2: w1_ref.at[2]})
    if wi_ref is not None:
        arrs[3] = wi_ref
    if wo_ref is not None:
        arrs[4] = wo_ref
    if w2_ref is not None:
        arrs.update({5: w2_ref.at[0], 6: w2_ref.at[1], 7: w2_ref.at[2]})
    return arrs


def _block_rows(arrs, k, dev):
    r = PIECE_ROWS[k]
    return arrs[k].at[pl.ds(pl.multiple_of(_dev_index(*dev) * r, 16), r), :]


NORM_ROWS = 512


def _all_gather_ffn1(shard, x, gain):
    pieces = G1_PIECES
    half = FS // 2
    T = x.shape[0]
    SIB, X0, X1, Y0, Y1, RELAY_Y, RELAY_X, ON_X, ON_Y, ON_D0, ON_D1 = range(11)

    def body(s_ref, x_ref, g_ref, w1_ref, h_ref, xbuf, hbuf, send_sems, recv_sems, local_sem, norm_sems):
        x, y, c = _position()
        me, sib = (x, y, c), (x, y, 1 - c)
        xn, yn, dg = (1 - x, y, c), (x, 1 - y, c), (1 - x, 1 - y, c)
        arrs = _weight_pieces(w1_ref=w1_ref)

        def first_norm():
            for r in range(0, T, NORM_ROWS):
                load = pltpu.make_async_copy(x_ref.at[pl.ds(r, NORM_ROWS), :], xbuf, norm_sems.at[0])
                load.start()
                load.wait()
                xv = xbuf[...]
                rs = lax.rsqrt(jnp.mean(xv * xv, axis=-1, keepdims=True) + EPS)
                hbuf[...] = (xv * rs * g_ref[...]).astype(bf16)
                store = pltpu.make_async_copy(hbuf, h_ref.at[pl.ds(r, NORM_ROWS), :], norm_sems.at[1])
                store.start()
                store.wait()

        def rows_of(k, block, hf):
            r = PIECE_ROWS[k]
            start, size = (0, r) if hf is None else (hf * half, half)
            return arrs[k].at[pl.ds(pl.multiple_of(_dev_index(*block) * r + start, 16), size), :]

        def copies(rel, block, hf, to, from_shard=False):
            def src(k):
                if not from_shard:
                    return rows_of(k, block, hf)
                start, size = (0, PIECE_ROWS[k]) if hf is None else (hf * half, half)
                return s_ref.at[pl.ds(PIECE_OFF[k] + start, size), :]
            return [pltpu.make_async_remote_copy(
                src_ref=src(k), dst_ref=rows_of(k, block, hf), send_sem=send_sems.at[rel], recv_sem=recv_sems.at[rel],
                device_id=to, device_id_type=MESH) for k in pieces]

        def waiter(rel, hf):
            nrows = len(pieces) * (FS if hf is None else half)
            grp = s_ref.at[pl.ds(0, nrows), :]
            return pltpu.make_async_remote_copy(src_ref=grp, dst_ref=grp, send_sem=send_sems.at[rel],
                                                recv_sem=recv_sems.at[rel], device_id=me, device_id_type=MESH)

        def start(cps):
            for cp in cps:
                cp.start()

        mine = [pltpu.make_async_copy(_shard_piece(s_ref, k), _block_rows(arrs, k, me), local_sem) for k in pieces]
        start(mine)
        start(copies(SIB, me, None, sib, True))
        start(copies(X0, me, 0, xn, True))
        start(copies(Y1, me, 1, yn, True))
        start(copies(X1, me, 1, xn, True))
        start(copies(Y0, me, 0, yn, True))
        first_norm()
        waiter(X0, 0).wait_recv()
        start(copies(RELAY_Y, xn, 0, yn))
        waiter(Y1, 1).wait_recv()
        start(copies(RELAY_X, yn, 1, xn))
        waiter(X1, 1).wait_recv()
        start(copies(ON_X, xn, None, sib))
        waiter(Y0, 0).wait_recv()
        start(copies(ON_Y, yn, None, sib))
        waiter(RELAY_Y, 0).wait_recv()
        start(copies(ON_D0, dg, 0, sib))
        waiter(RELAY_X, 1).wait_recv()
        start(copies(ON_D1, dg, 1, sib))
        waiter(SIB, None).wait_recv()
        waiter(ON_X, None).wait_recv()
        waiter(ON_Y, None).wait_recv()
        waiter(ON_D0, 0).wait_recv()
        waiter(ON_D1, 1).wait_recv()
        for rel, hf in ((SIB, None), (X0, 0), (X1, 1), (Y0, 0), (Y1, 1), (RELAY_Y, 0), (RELAY_X, 1),
                        (ON_X, None), (ON_Y, None), (ON_D0, 0), (ON_D1, 1)):
            waiter(rel, hf).wait_send()
        grp = _shard_group(s_ref, pieces)
        pltpu.make_async_copy(grp, grp, local_sem).wait()

    hbm = pl.BlockSpec(memory_space=pl.ANY)
    return pl.pallas_call(
        body, in_specs=[hbm, hbm, pl.BlockSpec(memory_space=pltpu.VMEM)], out_specs=[hbm, hbm],
        out_shape=(SDS((3, F, D), bf16), SDS((T, D), bf16)),
        scratch_shapes=[pltpu.VMEM((NORM_ROWS, D), f32), pltpu.VMEM((NORM_ROWS, D), bf16),
                        pltpu.SemaphoreType.DMA((11,)), pltpu.SemaphoreType.DMA((11,)), pltpu.SemaphoreType.DMA,
                        pltpu.SemaphoreType.DMA((2,))],
        compiler_params=pltpu.CompilerParams(has_side_effects=True),
        name="all_gather_ffn1")(shard, x, gain)


HBM_SPEC = pl.BlockSpec(memory_space=pltpu.HBM)
SEM_SPEC = pl.BlockSpec(memory_space=pltpu.SEMAPHORE)
ANY_SPEC = pl.BlockSpec(memory_space=pl.ANY)
SPLIT_EFFECT = pltpu.SideEffectType.DATAFLOW_SIDE_EFFECTING


def _in_hbm(a):
    return pltpu.with_memory_space_constraint(a, pltpu.HBM)


def _hbm_like(a):
    return pltpu.HBM(a.shape, a.dtype)


def _place_own_rows(shard):
    pieces = MIX_PIECES + F2_PIECES

    def body(s_ref, wi_ref, wo_ref, w2_ref, buf, sems):
        x, y, c = _position()
        arrs = _weight_pieces(wi_ref=wi_ref, wo_ref=wo_ref, w2_ref=w2_ref)
        grp = _shard_group(s_ref, pieces)
        load = pltpu.make_async_copy(grp, buf, sems.at[0])
        load.start()
        load.wait()
        base = PIECE_OFF[pieces[0]]
        for k in pieces:
            pltpu.make_async_copy(buf.at[pl.ds(PIECE_OFF[k] - base, PIECE_ROWS[k]), :],
                                  _block_rows(arrs, k, (x, y, c)), sems.at[1]).start()
        pltpu.make_async_copy(grp, buf, sems.at[1]).wait()

    return pl.pallas_call(
        body, in_specs=[ANY_SPEC], out_specs=[ANY_SPEC] * 3,
        out_shape=(SDS((DIN, D), bf16), SDS((DMIX, D), bf16), SDS((3, F, D), bf16)),
        scratch_shapes=[pltpu.VMEM((_group_rows(pieces), D), bf16), pltpu.SemaphoreType.DMA((2,))],
        name="place_own_rows")(shard)


def _xor_peer(x, y, c, k):
    return (x ^ (k >> 2), y ^ ((k >> 1) & 1), c ^ (k & 1))


def _gather_rest_start(shard, wi, wo, w2, w1):
    def body(s_ref, wi_ref, wo_ref, w2_ref, w1_ref,
             ssem_m, rsem_m0, rsem_m, ssem_f, rsem_f0, rsem_f, s_o, wi_o, wo_o, w2_o, w1_o):
        x, y, c = _position()
        me, sib = (x, y, c), (x, y, 1 - c)
        chips = [(1 - x, y), (x, 1 - y), (1 - x, 1 - y)]
        arrs = _weight_pieces(wi_ref=wi_ref, wo_ref=wo_ref, w2_ref=w2_ref)
        for pieces, ssem, rsem0, rsem in ((MIX_PIECES, ssem_m, rsem_m0, rsem_m), (F2_PIECES, ssem_f, rsem_f0, rsem_f)):
            for p in pieces:
                pltpu.make_async_remote_copy(
                    src_ref=_shard_piece(s_ref, p), dst_ref=_block_rows(arrs, p, me), send_sem=ssem.at[0],
                    recv_sem=rsem0, device_id=sib, device_id_type=MESH).start()
            for j, chip in enumerate(chips):
                for p in pieces:
                    pltpu.make_async_remote_copy(
                        src_ref=_shard_piece(s_ref, p), dst_ref=_block_rows(arrs, p, me), send_sem=ssem.at[1 + j],
                        recv_sem=rsem.at[j], device_id=(*chip, c), device_id_type=MESH).start()

    dma = pltpu.SemaphoreType.DMA
    return pl.pallas_call(
        body, name="gather_rest_start",
        out_shape=(dma((4,)), dma(()), dma((3,)), dma((4,)), dma(()), dma((3,)),
                   _hbm_like(shard), _hbm_like(wi), _hbm_like(wo), _hbm_like(w2), _hbm_like(w1)),
        in_specs=(HBM_SPEC,) * 5, out_specs=(SEM_SPEC,) * 6 + (HBM_SPEC,) * 5,
        input_output_aliases={0: 6, 1: 7, 2: 8, 3: 9, 4: 10},
        compiler_params=pltpu.CompilerParams(has_side_effects=SPLIT_EFFECT),
    )(_in_hbm(shard), _in_hbm(wi), _in_hbm(wo), _in_hbm(w2), _in_hbm(w1))


def _gather_mix_pass_on(rsem_m, wi, wo, thru, after):
    def body(wi_ref, wo_ref, thru_ref, rsem, after_ref, fsend, frecv, wi_o, wo_o, thru_o):
        x, y, c = _position()
        sib = (x, y, 1 - c)
        arrs = _weight_pieces(wi_ref=wi_ref, wo_ref=wo_ref)
        both = wi_ref.at[pl.ds(0, _group_rows(MIX_PIECES)), :]
        for j, chip in enumerate([(1 - x, y), (x, 1 - y), (1 - x, 1 - y)]):
            pltpu.make_async_remote_copy(src_ref=both, dst_ref=both, send_sem=fsend.at[j], recv_sem=rsem.at[j],
                                         device_id=(x, y, c), device_id_type=MESH).wait_recv()
            for p in MIX_PIECES:
                rows = _block_rows(arrs, p, (*chip, c))
                pltpu.make_async_remote_copy(src_ref=rows, dst_ref=rows, send_sem=fsend.at[j], recv_sem=frecv.at[j],
                                             device_id=sib, device_id_type=MESH).start()

    dma = pltpu.SemaphoreType.DMA
    return pl.pallas_call(
        body, name="gather_mix_pass_on",
        out_shape=(dma((3,)), dma((3,)), _hbm_like(wi), _hbm_like(wo), _hbm_like(thru)),
        in_specs=(HBM_SPEC, HBM_SPEC, HBM_SPEC, SEM_SPEC, ANY_SPEC), out_specs=(SEM_SPEC, SEM_SPEC) + (HBM_SPEC,) * 3,
        input_output_aliases={0: 2, 1: 3, 2: 4},
        compiler_params=pltpu.CompilerParams(has_side_effects=SPLIT_EFFECT),
    )(wi, wo, _in_hbm(thru), rsem_m, after)


def _gather_mix_wait(ssem_m, rsem_m0, fsend, frecv, shard, wi, wo, after):
    def body(s_ref, wi_ref, wo_ref, ssem, rsem0, fs, fr, after_ref, s_o, wi_o, wo_o):
        x, y, c = _position()
        grp = _shard_group(s_ref, MIX_PIECES)

        def waiter(send_sem, recv_sem):
            return pltpu.make_async_remote_copy(src_ref=grp, dst_ref=grp, send_sem=send_sem, recv_sem=recv_sem,
                                                device_id=(x, y, c), device_id_type=MESH)

        waiter(ssem.at[0], rsem0).wait_recv()
        for j in range(3):
            waiter(fs.at[j], fr.at[j]).wait_recv()
        for rel in range(4):
            waiter(ssem.at[rel], rsem0).wait_send()
        for j in range(3):
            waiter(fs.at[j], fr.at[j]).wait_send()

    return pl.pallas_call(
        body, name="gather_mix_wait", out_shape=(_hbm_like(shard), _hbm_like(wi), _hbm_like(wo)),
        in_specs=(HBM_SPEC,) * 3 + (SEM_SPEC,) * 4 + (ANY_SPEC,), out_specs=(HBM_SPEC,) * 3,
        input_output_aliases={0: 0, 1: 1, 2: 2},
        compiler_params=pltpu.CompilerParams(has_side_effects=SPLIT_EFFECT),
    )(shard, wi, wo, ssem_m, rsem_m0, fsend, frecv, after)


def _gather_ffn2_pass_on(rsem_f, w2, wo, after):
    def body(w2_ref, wo_ref, rsem, after_ref, fsend, frecv, w2_o, wo_o):
        x, y, c = _position()
        sib = (x, y, 1 - c)
        chips = [(1 - x, y), (x, 1 - y), (1 - x, 1 - y)]
        arrs = _weight_pieces(w2_ref=w2_ref)
        three = w2_ref.at[0, pl.ds(0, _group_rows(F2_PIECES)), :]
        for j, chip in enumerate(chips):
            pltpu.make_async_remote_copy(src_ref=three, dst_ref=three, send_sem=fsend.at[j], recv_sem=rsem.at[j],
                                         device_id=(x, y, c), device_id_type=MESH).wait_recv()
            for p in F2_PIECES:
                rows = _block_rows(arrs, p, (*chip, c))
                pltpu.make_async_remote_copy(src_ref=rows, dst_ref=rows, send_sem=fsend.at[j], recv_sem=frecv.at[j],
                                             device_id=sib, device_id_type=MESH).start()

    dma = pltpu.SemaphoreType.DMA
    return pl.pallas_call(
        body, name="gather_ffn2_pass_on", out_shape=(dma((3,)), dma((3,)), _hbm_like(w2), _hbm_like(wo)),
        in_specs=(HBM_SPEC, HBM_SPEC, SEM_SPEC, ANY_SPEC), out_specs=(SEM_SPEC, SEM_SPEC, HBM_SPEC, HBM_SPEC),
        input_output_aliases={0: 2, 1: 3},
        compiler_params=pltpu.CompilerParams(has_side_effects=SPLIT_EFFECT),
    )(w2, wo, rsem_f, after)


def _gather_ffn2_wait(ssem_f, rsem_f0, fsend, frecv, shard, w2, after):
    def body(s_ref, w2_ref, ssem, rsem0, fs, fr, after_ref, w2_o):
        x, y, c = _position()
        grp = _shard_group(s_ref, F2_PIECES)

        def waiter(send_sem, recv_sem):
            return pltpu.make_async_remote_copy(src_ref=grp, dst_ref=grp, send_sem=send_sem, recv_sem=recv_sem,
                                                device_id=(x, y, c), device_id_type=MESH)

        waiter(ssem.at[0], rsem0).wait_recv()
        for j in range(3):
            waiter(fs.at[j], fr.at[j]).wait_recv()
        for rel in range(4):
            waiter(ssem.at[rel], rsem0).wait_send()
        for j in range(3):
            waiter(fs.at[j], fr.at[j]).wait_send()

    return pl.pallas_call(
        body, name="gather_ffn2_wait", out_shape=_hbm_like(w2),
        in_specs=(HBM_SPEC, HBM_SPEC, SEM_SPEC, SEM_SPEC, SEM_SPEC, SEM_SPEC, ANY_SPEC), out_specs=HBM_SPEC,
        input_output_aliases={1: 0},
        compiler_params=pltpu.CompilerParams(has_side_effects=SPLIT_EFFECT),
    )(shard, w2, ssem_f, rsem_f0, fsend, frecv, after)


class _GatheredWeights(_LocalWeights):
    def __init__(self, shard, x, gain1):
        w1, self.h1 = _all_gather_ffn1(shard, x, gain1)
        wi, wo, w2 = _place_own_rows(shard)
        (self.ssem_m, self.rsem_m0, self.rsem_m, self.ssem_f, self.rsem_f0, self.rsem_f,
         self.shard, self.wi, self.wo, self.w2_part, self.w1) = _gather_rest_start(shard, wi, wo, w2, w1)

    def first_norm(self, x, gain):
        return self.h1

    def after_ffn1(self, gain, x1):
        self.fsend_m, self.frecv_m, self.wi, self.wo, gain = _gather_mix_pass_on(self.rsem_m, self.wi, self.wo, gain, x1)
        return gain

    def mix(self, after):
        self.shard, wint, wout = _gather_mix_wait(self.ssem_m, self.rsem_m0, self.fsend_m, self.frecv_m, self.shard,
                                                  self.wi, self.wo, after)
        return wint, wout

    def before_out_proj(self, wout, after):
        self.fsend, self.frecv, self.w2_part, wout = _gather_ffn2_pass_on(self.rsem_f, self.w2_part, wout, after)
        return wout

    def ffn2(self, after):
        return _gather_ffn2_wait(self.ssem_f, self.rsem_f0, self.fsend, self.frecv, self.shard, self.w2_part, after)

    def mix_ffn2_grads_ready(self, dwint, dwout, dw2, dh2):
        rx1 = lax.empty((4, RSA_ROWS, D), bf16)
        self.sa, self.ra, dwint, dwout, dw2, rx1, dh2 = _rsa_level1_start(dwint, dwout, dw2, rx1, dh2)
        self.level1 = (dwint, dwout, dw2, rx1)
        return dh2

    def before_ffn1_bwd(self, dx1b):
        dwint, dwout, dw2, rx1 = _rsa_level1_wait(self.sa, self.ra, *self.level1, dx1b)
        tx, self.acc = _rsa_chip_sums(dwint, dwout, dw2, rx1)
        rx2 = lax.empty((3, RSA_ROWS, D), bf16)
        self.sb, self.rb, self.tx, self.rx2, dx1b = _rsa_level2_start(tx, rx2, dx1b)
        return dx1b

    def mix_ffn2_grads_total(self, after):
        rx2 = _rsa_level2_wait(self.sb, self.rb, self.tx, self.rx2, after)
        return _rsa_total(self.acc, rx2)


def _reduce_scatter_ffn1_head(dw1, small_packed):
    pieces = G1_PIECES
    half = FS // 2
    hrows = len(pieces) * half
    nrows = 2 * hrows
    X_RELAY, Y_RELAY = range(2)

    def body(d1_ref, p_ref, forx_ref, fory_ref, own_ref, rx1_ref, relx_ref, rely_ref, tot_ref,
             own_buf, rx_buf, tx1, tx2, tx3, acc, sa, ra, sb, rb, lsem, pair, chips, small_send, small_recv):
        x, y, c = _position()
        me, sib = (x, y, c), (x, y, 1 - c)
        xn, yn = (1 - x, y, c), (x, 1 - y, c)
        rel_chips = [(x, y), (1 - x, y), (x, 1 - y), (1 - x, 1 - y)]
        srcs = _weight_pieces(w1_ref=d1_ref)

        my_chip = 2 * x + y
        pair[c] = p_ref[...]
        swap = pltpu.make_async_remote_copy(
            src_ref=p_ref, dst_ref=pair.at[c], send_sem=small_send.at[0], recv_sem=small_recv.at[0],
            device_id=sib, device_id_type=MESH)
        swap.start()
        small = [pltpu.make_async_remote_copy(
            src_ref=chips.at[my_chip], dst_ref=chips.at[my_chip], send_sem=small_send.at[j], recv_sem=small_recv.at[j],
            device_id=(*rel_chips[j], c), device_id_type=MESH) for j in (1, 2, 3)]

        def part(k, dev, hf):
            r = PIECE_ROWS[k]
            return srcs[k].at[pl.ds(pl.multiple_of(_dev_index(*dev) * r + hf * half, 16), half), :]

        def slot(ref, k, hf):
            return ref.at[pl.ds(hf * hrows + k * half, half), :]

        halves = [(k, hf) for hf in (0, 1) for k in pieces]

        for j in (3, 1, 2, 0):
            for k, hf in halves:
                pltpu.make_async_remote_copy(
                    src_ref=part(k, (*rel_chips[j], 1 - c), hf), dst_ref=slot(rx1_ref.at[j], k, hf),
                    send_sem=sa.at[j], recv_sem=ra.at[j], device_id=sib, device_id_type=MESH).start()

        def wait_a(j):
            return pltpu.make_async_remote_copy(src_ref=rx1_ref.at[j], dst_ref=rx1_ref.at[j], send_sem=sa.at[j],
                                                recv_sem=ra.at[j], device_id=me, device_id_type=MESH)

        def ici(rel, src, dst, to):
            return pltpu.make_async_remote_copy(src_ref=src, dst_ref=dst, send_sem=sb.at[rel], recv_sem=rb.at[rel],
                                                device_id=to, device_id_type=MESH)

        first, second = pl.ds(0, hrows), pl.ds(hrows, hrows)
        sends = {
            X_RELAY: ici(X_RELAY, tx3.at[first, :], relx_ref, xn),
            Y_RELAY: ici(Y_RELAY, tx3.at[second, :], rely_ref, yn),
        }

        swap.wait_recv()
        chips[my_chip] = pair[0] + pair[1]
        for cp in small:
            cp.start()

        def chip_sum(j, dst):
            loads = [pltpu.make_async_copy(part(k, (*rel_chips[j], c), hf), slot(own_buf, k, hf), lsem.at[0])
                     for k, hf in halves]
            for cp in loads:
                cp.start()
            wait_a(j).wait_recv()
            got = pltpu.make_async_copy(rx1_ref.at[j], rx_buf, lsem.at[1])
            got.start()
            pltpu.make_async_copy(rx_buf, rx_buf, lsem.at[0]).wait()
            got.wait()

            def add(i, carry):
                rows = pl.ds(pl.multiple_of(i * half, 16), half)
                tot = own_buf[rows, :].astype(f32) + rx_buf[rows, :].astype(f32)
                dst[rows, :] = tot.astype(dst.dtype)
                return carry

            lax.fori_loop(0, nrows // half, add, 0)

        def add_landed(landed, dst, rows0, nrows_):
            got = pltpu.make_async_copy(landed, rx_buf.at[pl.ds(0, nrows_), :], lsem.at[1])
            got.start()
            got.wait()

            def add(i, carry):
                src_rows = pl.ds(pl.multiple_of(i * half, 16), half)
                dst_rows = pl.ds(pl.multiple_of(rows0 + i * half, 16), half)
                dst[dst_rows, :] = (dst[dst_rows, :].astype(f32) + rx_buf[src_rows, :].astype(f32)).astype(dst.dtype)
                return carry

            lax.fori_loop(0, nrows_ // half, add, 0)

        chip_sum(3, tx3)
        sends[X_RELAY].start()
        sends[Y_RELAY].start()
        chip_sum(1, tx1)
        chip_sum(2, tx2)
        chip_sum(0, acc)
        own_out = pltpu.make_async_copy(acc, own_ref, lsem.at[0])
        own_out.start()
        sends[X_RELAY].wait_recv()
        add_landed(relx_ref, tx2, 0, hrows)
        sends[Y_RELAY].wait_recv()
        add_landed(rely_ref, tx1, hrows, hrows)
        own_out.wait()
        outs = [pltpu.make_async_copy(tx1, forx_ref, lsem.at[0]), pltpu.make_async_copy(tx2, fory_ref, lsem.at[1])]
        for cp in outs:
            cp.start()
        for cp in outs:
            cp.wait()
        for cp in small:
            cp.wait_recv()
        tot = (chips[0] + chips[1]) + (chips[2] + chips[3])
        tot_ref[...] = tot
        loss = jnp.sum(tot[LOSS_ROW:LOSS_ROW + 1, :], axis=-1, keepdims=True)
        tot_ref[LOSS_ROW:LOSS_ROW + 1, :] = jnp.broadcast_to(loss, (1, 128))
        for j in range(4):
            wait_a(j).wait_send()
        for cp in sends.values():
            cp.wait_send()
        swap.wait_send()
        for cp in small:
            cp.wait_send()

    hbm = pl.BlockSpec(memory_space=pl.ANY)
    vm = pl.BlockSpec(memory_space=pltpu.VMEM)
    outs = pl.pallas_call(
        body, in_specs=[hbm, vm], out_specs=[hbm] * 6 + [vm],
        out_shape=(SDS((nrows, D), bf16), SDS((nrows, D), bf16), SDS((nrows, D), f32), SDS((4, nrows, D), bf16),
                   SDS((hrows, D), bf16), SDS((hrows, D), bf16), SDS((SMALL_ROWS, 128), f32)),
        scratch_shapes=[pltpu.VMEM((nrows, D), bf16), pltpu.VMEM((nrows, D), bf16),
                        pltpu.VMEM((nrows, D), bf16), pltpu.VMEM((nrows, D), bf16), pltpu.VMEM((nrows, D), bf16),
                        pltpu.VMEM((nrows, D), f32),
                        pltpu.SemaphoreType.DMA((4,)), pltpu.SemaphoreType.DMA((4,)),
                        pltpu.SemaphoreType.DMA((2,)), pltpu.SemaphoreType.DMA((2,)), pltpu.SemaphoreType.DMA((2,)),
                        pltpu.VMEM((2, SMALL_ROWS, 128), f32), pltpu.VMEM((4, SMALL_ROWS, 128), f32),
                        pltpu.SemaphoreType.DMA((4,)), pltpu.SemaphoreType.DMA((4,))],
        compiler_params=pltpu.CompilerParams(has_side_effects=True, vmem_limit_bytes=VMEM_LIMIT_V7X),
        name="reduce_scatter_ffn1_head")(dw1, small_packed)
    return outs[0], outs[1], outs[2], outs[-1]


def _rs1_tail_start(for_x, for_y, from_x, from_y, thru_a, thru_b):
    def body(fx_ref, fy_ref, lx_ref, ly_ref, ta_ref, tb_ref, ssem, rsem, fx_o, fy_o, lx_o, ly_o, ta_o, tb_o):
        x, y, c = _position()
        pltpu.make_async_remote_copy(src_ref=fx_ref, dst_ref=lx_ref, send_sem=ssem.at[0], recv_sem=rsem.at[0],
                                     device_id=(1 - x, y, c), device_id_type=MESH).start()
        pltpu.make_async_remote_copy(src_ref=fy_ref, dst_ref=ly_ref, send_sem=ssem.at[1], recv_sem=rsem.at[1],
                                     device_id=(x, 1 - y, c), device_id_type=MESH).start()

    dma = pltpu.SemaphoreType.DMA
    arrs = (for_x, for_y, from_x, from_y, thru_a, thru_b)
    return pl.pallas_call(
        body, name="rs1_tail_start", out_shape=(dma((2,)), dma((2,))) + tuple(_hbm_like(a) for a in arrs),
        in_specs=(HBM_SPEC,) * 6, out_specs=(SEM_SPEC,) * 2 + (HBM_SPEC,) * 6,
        input_output_aliases={0: 2, 1: 3, 2: 4, 3: 5, 4: 6, 5: 7},
        compiler_params=pltpu.CompilerParams(has_side_effects=SPLIT_EFFECT),
    )(*[_in_hbm(a) for a in arrs])


def _rs1_tail_wait(ssem, rsem, for_x, for_y, from_x, from_y, after):
    def body(fx_ref, fy_ref, lx_ref, ly_ref, ssem_ref, rsem_ref, after_ref, lx_o, ly_o):
        x, y, c = _position()
        for j, (src, dst) in enumerate(((fx_ref, lx_ref), (fy_ref, ly_ref))):
            d = pltpu.make_async_remote_copy(src_ref=src, dst_ref=dst, send_sem=ssem_ref.at[j], recv_sem=rsem_ref.at[j],
                                             device_id=(x, y, c), device_id_type=MESH)
            d.wait_recv()
            d.wait_send()

    return pl.pallas_call(
        body, name="rs1_tail_wait", out_shape=(_hbm_like(from_x), _hbm_like(from_y)),
        in_specs=(HBM_SPEC,) * 4 + (SEM_SPEC, SEM_SPEC, ANY_SPEC), out_specs=(HBM_SPEC, HBM_SPEC),
        input_output_aliases={2: 0, 3: 1},
        compiler_params=pltpu.CompilerParams(has_side_effects=SPLIT_EFFECT),
    )(for_x, for_y, from_x, from_y, ssem, rsem, after)


def _rs1_total(own, from_x, from_y):
    half = FS // 2
    npiece = len(G1_PIECES)

    def body(a_ref, x_ref, y_ref, o_ref):
        o_ref[...] = (a_ref[...] + x_ref[...].astype(f32)) + y_ref[...].astype(f32)

    blk = pl.BlockSpec((half, D), lambda i: (i, 0))
    return pl.pallas_call(
        body, grid=(2 * npiece,), in_specs=[blk, blk, blk],
        out_specs=pl.BlockSpec((half, D), lambda i: (2 * (i % npiece) + i // npiece, 0)),
        out_shape=SDS((npiece * FS, D), f32), name="rs1_total")(own, from_x, from_y)


RSA_PIECES = MIX_PIECES + F2_PIECES
RSA_ROWS = _group_rows(RSA_PIECES)
RSA_OFF = {k: PIECE_OFF[k] - PIECE_OFF[RSA_PIECES[0]] for k in RSA_PIECES}
RSA_BLOCK = 192


def _rsa_rows(ref, k):
    return ref.at[pl.ds(RSA_OFF[k], PIECE_ROWS[k]), :]


def _rsa_level1_start(dwint, dwout, dw2, rx1, thru):
    def body(di_ref, do_ref, d2_ref, rx1_ref, thru_ref, sa, ra, di_o, do_o, d2_o, rx1_o, thru_o):
        x, y, c = _position()
        srcs = _weight_pieces(wi_ref=di_ref, wo_ref=do_ref, w2_ref=d2_ref)
        for j, chip in enumerate([(x, y), (1 - x, y), (x, 1 - y), (1 - x, 1 - y)]):
            for k in RSA_PIECES:
                pltpu.make_async_remote_copy(
                    src_ref=_block_rows(srcs, k, (*chip, 1 - c)), dst_ref=_rsa_rows(rx1_ref.at[j], k),
                    send_sem=sa.at[j], recv_sem=ra.at[j], device_id=(x, y, 1 - c), device_id_type=MESH).start()

    dma = pltpu.SemaphoreType.DMA
    arrs = (dwint, dwout, dw2, rx1, thru)
    return pl.pallas_call(
        body, name="rsa_level1_start", out_shape=(dma((4,)), dma((4,))) + tuple(_hbm_like(a) for a in arrs),
        in_specs=(HBM_SPEC,) * 5, out_specs=(SEM_SPEC,) * 2 + (HBM_SPEC,) * 5,
        input_output_aliases={0: 2, 1: 3, 2: 4, 3: 5, 4: 6},
        compiler_params=pltpu.CompilerParams(has_side_effects=SPLIT_EFFECT),
    )(*[_in_hbm(a) for a in arrs])


def _rsa_level1_wait(sa, ra, dwint, dwout, dw2, rx1, after):
    def body(di_ref, do_ref, d2_ref, rx1_ref, sa_ref, ra_ref, after_ref, di_o, do_o, d2_o, rx1_o):
        x, y, c = _position()
        for j in range(4):
            d = pltpu.make_async_remote_copy(src_ref=rx1_ref.at[j], dst_ref=rx1_ref.at[j], send_sem=sa_ref.at[j],
                                             recv_sem=ra_ref.at[j], device_id=(x, y, c), device_id_type=MESH)
            d.wait_recv()
            d.wait_send()

    arrs = (dwint, dwout, dw2, rx1)
    return pl.pallas_call(
        body, name="rsa_level1_wait", out_shape=tuple(_hbm_like(a) for a in arrs),
        in_specs=(HBM_SPEC,) * 4 + (SEM_SPEC, SEM_SPEC, ANY_SPEC), out_specs=(HBM_SPEC,) * 4,
        input_output_aliases={0: 0, 1: 1, 2: 2, 3: 3},
        compiler_params=pltpu.CompilerParams(has_side_effects=SPLIT_EFFECT),
    )(*arrs, sa, ra, after)


def _rsa_chip_sums(dwint, dwout, dw2, rx1):
    nblk = RSA_ROWS // RSA_BLOCK

    def body(di_ref, do_ref, d2_ref, rx1_ref, tx_ref, acc_ref, own_buf, rx_buf, tx_buf, acc_buf, lsems):
        x, y, c = _position()
        srcs = _weight_pieces(wi_ref=di_ref, wo_ref=do_ref, w2_ref=d2_ref)
        for j, chip in enumerate([(x, y), (1 - x, y), (x, 1 - y), (1 - x, 1 - y)]):
            loads = [pltpu.make_async_copy(_block_rows(srcs, k, (*chip, c)), _rsa_rows(own_buf, k), lsems.at[0])
                     for k in RSA_PIECES]
            got = pltpu.make_async_copy(rx1_ref.at[j], rx_buf, lsems.at[1])
            for cp in loads + [got]:
                cp.start()
            pltpu.make_async_copy(rx_buf, rx_buf, lsems.at[0]).wait()
            got.wait()

            def add(i, carry, j=j):
                rows = pl.ds(pl.multiple_of(i * RSA_BLOCK, 16), RSA_BLOCK)
                tot = own_buf[rows, :].astype(f32) + rx_buf[rows, :].astype(f32)
                if j == 0:
                    acc_buf[rows, :] = tot
                else:
                    tx_buf[rows, :] = tot.astype(bf16)
                return carry

            lax.fori_loop(0, nblk, add, 0)
            out = (pltpu.make_async_copy(acc_buf, acc_ref, lsems.at[2]) if j == 0
                   else pltpu.make_async_copy(tx_buf, tx_ref.at[j - 1], lsems.at[2]))
            out.start()
            out.wait()

    return pl.pallas_call(
        body, in_specs=[ANY_SPEC] * 4, out_specs=[ANY_SPEC] * 2,
        out_shape=(SDS((3, RSA_ROWS, D), bf16), SDS((RSA_ROWS, D), f32)),
        scratch_shapes=[pltpu.VMEM((RSA_ROWS, D), bf16), pltpu.VMEM((RSA_ROWS, D), bf16),
                        pltpu.VMEM((RSA_ROWS, D), bf16), pltpu.VMEM((RSA_ROWS, D), f32),
                        pltpu.SemaphoreType.DMA((3,))],
        compiler_params=_cparams(None, VMEM_LIMIT_V7X), name="rsa_chip_sums")(dwint, dwout, dw2, rx1)


def _rsa_level2_start(tx, rx2, thru):
    def body(tx_ref, rx2_ref, thru_ref, sb, rb, tx_o, rx2_o, thru_o):
        x, y, c = _position()
        for j, chip in enumerate([(1 - x, y), (x, 1 - y), (1 - x, 1 - y)]):
            pltpu.make_async_remote_copy(src_ref=tx_ref.at[j], dst_ref=rx2_ref.at[j], send_sem=sb.at[j],
                                         recv_sem=rb.at[j], device_id=(*chip, c), device_id_type=MESH).start()

    dma = pltpu.SemaphoreType.DMA
    arrs = (tx, rx2, thru)
    return pl.pallas_call(
        body, name="rsa_level2_start", out_shape=(dma((3,)), dma((3,))) + tuple(_hbm_like(a) for a in arrs),
        in_specs=(HBM_SPEC,) * 3, out_specs=(SEM_SPEC,) * 2 + (HBM_SPEC,) * 3,
        input_output_aliases={0: 2, 1: 3, 2: 4},
        compiler_params=pltpu.CompilerParams(has_side_effects=SPLIT_EFFECT),
    )(*[_in_hbm(a) for a in arrs])


def _rsa_level2_wait(sb, rb, tx, rx2, after):
    def body(tx_ref, rx2_ref, sb_ref, rb_ref, after_ref, rx2_o):
        x, y, c = _position()
        for j in range(3):
            d = pltpu.make_async_remote_copy(src_ref=tx_ref.at[j], dst_ref=rx2_ref.at[j], send_sem=sb_ref.at[j],
                                             recv_sem=rb_ref.at[j], device_id=(x, y, c), device_id_type=MESH)
            d.wait_recv()
            d.wait_send()

    return pl.pallas_call(
        body, name="rsa_level2_wait", out_shape=_hbm_like(rx2),
        in_specs=(HBM_SPEC, HBM_SPEC, SEM_SPEC, SEM_SPEC, ANY_SPEC), out_specs=HBM_SPEC,
        input_output_aliases={1: 0},
        compiler_params=pltpu.CompilerParams(has_side_effects=SPLIT_EFFECT),
    )(tx, rx2, sb, rb, after)


def _rsa_total(acc, rx2):
    def body(a_ref, r_ref, o_ref):
        o_ref[...] = ((a_ref[...] + r_ref[0].astype(f32)) + r_ref[1].astype(f32)) + r_ref[2].astype(f32)

    return pl.pallas_call(
        body, grid=(RSA_ROWS // RSA_BLOCK,),
        in_specs=[pl.BlockSpec((RSA_BLOCK, D), lambda i: (i, 0)), pl.BlockSpec((3, RSA_BLOCK, D), lambda i: (0, i, 0))],
        out_specs=pl.BlockSpec((RSA_BLOCK, D), lambda i: (i, 0)),
        out_shape=SDS((RSA_ROWS, D), f32), name="rsa_total")(acc, rx2)


def _adamw_math(w, g, m, v):
    m = ADAM_B1 * m + (1.0 - ADAM_B1) * g
    v = ADAM_B2 * v + (1.0 - ADAM_B2) * (g * g)
    m_hat = m / (1.0 - ADAM_B1 ** ADAM_STEP)
    v_hat = v / (1.0 - ADAM_B2 ** ADAM_STEP)
    delta = -ADAM_LR * (m_hat / (jnp.sqrt(v_hat) + ADAM_EPS) + ADAM_WD * w)
    return delta, m, v


def _adamw_big(pieces, ws, ms, vs, red, name):
    npiece = len(pieces)
    rmax = max(PIECE_ROWS[k] for k in pieces)

    def body(*refs):
        ins = (refs[0:npiece], refs[npiece:2 * npiece], refs[2 * npiece:3 * npiece])
        red_ref = refs[3 * npiece]
        out_refs = refs[3 * npiece + 1:7 * npiece + 1]
        inb, outb, in_sems, out_sems = refs[7 * npiece + 1:]

        def grad_rows(k):
            if k in G1_PIECES:
                return red_ref.at[pl.ds(PIECE_OFF[k], PIECE_ROWS[k]), :]
            return _rsa_rows(red_ref, k)

        def loads(i):
            s, k = i % 2, pieces[i]
            r = PIECE_ROWS[k]
            cps = [pltpu.make_async_copy(ins[q][i].at[0], inb.at[s, q, pl.ds(0, r), :], in_sems.at[4 * s + q])
                   for q in range(3)]
            cps.append(pltpu.make_async_copy(grad_rows(k), inb.at[s, 3, pl.ds(0, r), :], in_sems.at[4 * s + 3]))
            return cps

        def stores(i):
            s, r = i % 2, PIECE_ROWS[pieces[i]]
            return [pltpu.make_async_copy(outb.at[s, q, pl.ds(0, r), :], out_refs[q * npiece + i].at[0],
                                          out_sems.at[4 * s + q]) for q in range(4)]

        for cp in loads(0):
            cp.start()
        for i in range(npiece):
            s, r = i % 2, PIECE_ROWS[pieces[i]]
            if i + 1 < npiece:
                for cp in loads(i + 1):
                    cp.start()
            for cp in loads(i):
                cp.wait()
            if i >= 2:
                for cp in stores(i - 2):
                    cp.wait()
            g = inb[s, 3, 0:r, :]
            d, nm, nv = _adamw_math(inb[s, 0, 0:r, :], g, inb[s, 1, 0:r, :], inb[s, 2, 0:r, :])
            outb[s, 0, 0:r, :] = g
            outb[s, 1, 0:r, :] = d
            outb[s, 2, 0:r, :] = nm
            outb[s, 3, 0:r, :] = nv
            for cp in stores(i):
                cp.start()
        for i in range(max(npiece - 2, 0), npiece):
            for cp in stores(i):
                cp.wait()

    hbm = pl.BlockSpec(memory_space=pl.ANY)
    outs = pl.pallas_call(
        body, in_specs=[hbm] * (3 * npiece + 1), out_specs=[hbm] * (4 * npiece),
        out_shape=tuple(SDS(w.shape, f32) for _ in range(4) for w in ws),
        scratch_shapes=[pltpu.VMEM((2, 4, rmax, D), f32), pltpu.VMEM((2, 4, rmax, D), f32),
                        pltpu.SemaphoreType.DMA((8,)), pltpu.SemaphoreType.DMA((8,))],
        compiler_params=_cparams(None, VMEM_LIMIT_V7X), name=name)(*ws, *ms, *vs, red)
    return [list(outs[q * npiece:(q + 1) * npiece]) for q in range(4)]


def _adamw_small(w, m, v, g, name):
    def body(w_ref, m_ref, v_ref, g_ref, d_ref, nm_ref, nv_ref):
        d, nm, nv = _adamw_math(w_ref[...], g_ref[...], m_ref[...], v_ref[...])
        d_ref[...] = d
        nm_ref[...] = nm
        nv_ref[...] = nv

    return pl.pallas_call(
        body, out_shape=tuple(SDS(w.shape, f32) for _ in range(3)), name=name)(w, m, v, g)


WEIGHTS = ("ffn1_norm", "ffn1_w_gate", "ffn1_w_up", "ffn1_w_down", "mix_norm", "w_in", "q_norm", "k_norm",
           "attn_sinks", "rel_bias", "pool_w", "pool_scale", "w_out", "ffn2_norm", "ffn2_w_gate", "ffn2_w_up",
           "ffn2_w_down")
BIG = (("ffn1_w_gate", True), ("ffn1_w_up", True), ("ffn1_w_down", False), ("w_in", True), ("w_out", False),
       ("ffn2_w_gate", True), ("ffn2_w_up", True), ("ffn2_w_down", False))


def kernel(x, ffn1_norm, ffn1_w_gate, ffn1_w_up, ffn1_w_down, mix_norm, w_in, q_norm, k_norm, attn_sinks, rel_bias, pool_w, pool_scale, w_out, ffn2_norm, ffn2_w_gate, ffn2_w_up, ffn2_w_down, loss_target, m_ffn1_norm, m_ffn1_w_gate, m_ffn1_w_up, m_ffn1_w_down, m_mix_norm, m_w_in, m_q_norm, m_k_norm, m_attn_sinks, m_rel_bias, m_pool_w, m_pool_scale, m_w_out, m_ffn2_norm, m_ffn2_w_gate, m_ffn2_w_up, m_ffn2_w_down, v_ffn1_norm, v_ffn1_w_gate, v_ffn1_w_up, v_ffn1_w_down, v_mix_norm, v_w_in, v_q_norm, v_k_norm, v_attn_sinks, v_rel_bias, v_pool_w, v_pool_scale, v_w_out, v_ffn2_norm, v_ffn2_w_gate, v_ffn2_w_up, v_ffn2_w_down):
    args = dict(locals())
    w = {n: args[n] for n in WEIGHTS}
    m = {n: args["m_" + n] for n in WEIGHTS}
    v = {n: args["v_" + n] for n in WEIGHTS}

    as_rows = lambda a, tr: jnp.swapaxes(a, 1, 2) if tr else a
    shard = jnp.concatenate([as_rows(w[n], tr)[0].astype(bf16) for n, tr in BIG], axis=0)
    exchanges = _GatheredWeights(shard, x[0], ffn1_norm)
    (dh1, dx1), (dw1, _, _, _), small = _local_step(
        x[0], loss_target[0], exchanges, ffn1_norm, mix_norm, ffn2_norm, q_norm, k_norm, attn_sinks,
        rel_bias, pool_w[0], pool_scale)

    nrows1 = len(G1_PIECES) * FS
    for_x, for_y, own1, small_tot = _reduce_scatter_ffn1_head(dw1, _pack_small(small))
    ssem, rsem, for_x, for_y, from_x, from_y, g1, small_tot = _rs1_tail_start(
        for_x, for_y, lax.empty((nrows1, D), bf16), lax.empty((nrows1, D), bf16), ffn1_norm, small_tot)
    gx, _, _ = _norm_bwd(dh1, x[0], g1, dx1, 1.0, "norm1_bwd")
    red_rest = exchanges.mix_ffn2_grads_total(gx)

    grads, deltas, new_m, new_v = {}, {}, {}, {}
    rest = [k for k in range(len(BIG)) if k not in G1_PIECES]
    rows_of = lambda t, ks: [as_rows(t[BIG[k][0]], BIG[k][1]) for k in ks]
    rest_out = _adamw_big(rest, rows_of(w, rest), rows_of(m, rest), rows_of(v, rest), red_rest, "adamw_rest")
    from_x, from_y = _rs1_tail_wait(ssem, rsem, for_x, for_y, from_x, from_y, rest_out[0][0])
    red1 = _rs1_total(own1, from_x, from_y)
    ffn1 = list(G1_PIECES)
    ffn1_out = _adamw_big(ffn1, rows_of(w, ffn1), rows_of(m, ffn1), rows_of(v, ffn1), red1, "adamw_ffn1")
    for ks, out in ((rest, rest_out), (ffn1, ffn1_out)):
        for i, k in enumerate(ks):
            n, tr = BIG[k]
            grads[n], deltas[n], new_m[n], new_v[n] = [as_rows(o[i], tr) for o in out]
    small_names = [n for n in SMALL_NAMES if n != "loss"]
    ds, nms, nvs = _adamw_small(_pack_small({n: w[n] for n in small_names}), _pack_small({n: m[n] for n in small_names}),
                                _pack_small({n: v[n] for n in small_names}), small_tot, "adamw_small")
    for n in small_names:
        grads[n] = _unpack_small(small_tot, n)
        deltas[n], new_m[n], new_v[n] = _unpack_small(ds, n), _unpack_small(nms, n), _unpack_small(nvs, n)
    loss = small_tot[LOSS_ROW, 0]
    return (loss, gx[None], *[grads[n] for n in WEIGHTS], *[deltas[n] for n in WEIGHTS],
            *[new_m[n] for n in WEIGHTS], *[new_v[n] for n in WEIGHTS])
```

```python
import functools

import jax
import jax.numpy as jnp
import numpy as np
from jax import lax
from jax.experimental import pallas as pl
from jax.experimental.pallas import tpu as pltpu

f32, bf16, i32 = jnp.float32, jnp.bfloat16, jnp.int32
SDS = jax.ShapeDtypeStruct

D = 1024
F = 2816
HD = 64
NH = 8
NKV = 2
GQA = NH // NKV
DATTN = NH * HD
DKV = NKV * HD
DPOOL = 512
POOL_WINDOWS = (2, 4, 8, 16)
PGD = DPOOL // len(POOL_WINDOWS)
DIN = DATTN + 2 * DKV + DPOOL
DMIX = DATTN + DPOOL
BLK = 128
NBUCK = 32
MAX_DISTANCE = 128
EPS = 1e-6
NEG = -1e30
SCALE = HD ** -0.5

ADAM_LR, ADAM_B1, ADAM_B2, ADAM_EPS, ADAM_WD, ADAM_STEP = 0.001, 0.9, 0.999, 1e-08, 0.01, 10

NDEV = 8
FS = F // NDEV
INS = DIN // NDEV
OUTS = DMIX // NDEV
PIECE_ROWS = (FS, FS, FS, INS, OUTS, FS, FS, FS)
PIECE_OFF = tuple(int(v) for v in np.cumsum((0,) + PIECE_ROWS[:-1]))
PACK_ROWS = sum(PIECE_ROWS)

VMEM_LIMIT_V7X = 56 * 1024 * 1024

MESH = pl.DeviceIdType.MESH


def _cparams(sem=None, vmem=None):
    return pltpu.CompilerParams(dimension_semantics=sem, vmem_limit_bytes=vmem)


def _nt(a, b):
    return lax.dot_general(a, b, (((1,), (1,)), ((), ())), preferred_element_type=f32)


def _tn(a, b):
    return lax.dot_general(a, b, (((0,), (0,)), ((), ())), preferred_element_type=f32)


def _nn(a, b):
    return jnp.dot(a, b, preferred_element_type=f32)


def _sigmoid(x):
    return 1.0 / (1.0 + jnp.exp(-x))


def _norm_fwd(x, g, name):
    T = x.shape[0]
    tm = min(512, T)

    def body(x_ref, g_ref, h_ref):
        xv = x_ref[...]
        r = lax.rsqrt(jnp.mean(xv * xv, axis=-1, keepdims=True) + EPS)
        h_ref[...] = (xv * r * g_ref[...]).astype(bf16)

    return pl.pallas_call(
        body, grid=(T // tm,),
        in_specs=[pl.BlockSpec((tm, D), lambda i: (i, 0)), pl.BlockSpec((1, D), lambda i: (0, 0))],
        out_specs=pl.BlockSpec((tm, D), lambda i: (i, 0)),
        out_shape=SDS((T, D), bf16), name=name)(x, g)


def _norm_bwd(dh, x, g, dres, out_scale, name):
    T = x.shape[0]
    tm = min(512, T)

    def body(dh_ref, x_ref, g_ref, dr_ref, dx_ref, dxb_ref, dg_ref):
        i = pl.program_id(0)
        xv = x_ref[...]
        r = lax.rsqrt(jnp.mean(xv * xv, axis=-1, keepdims=True) + EPS)
        xh = xv * r
        dhv = dh_ref[...]
        dxh = dhv * g_ref[...]
        dx = dr_ref[...] + r * (dxh - xh * jnp.mean(dxh * xh, axis=-1, keepdims=True))
        dx_ref[...] = dx
        dxb_ref[...] = (out_scale * dx).astype(bf16)
        dg = jnp.sum(dhv * xh, axis=0, keepdims=True)

        @pl.when(i == 0)
        def _():
            dg_ref[...] = dg

        @pl.when(i > 0)
        def _():
            dg_ref[...] += dg

    tok = pl.BlockSpec((tm, D), lambda i: (i, 0))
    vec = pl.BlockSpec((1, D), lambda i: (0, 0))
    return pl.pallas_call(
        body, grid=(T // tm,),
        in_specs=[tok, tok, vec, tok], out_specs=[tok, tok, vec],
        out_shape=(SDS((T, D), f32), SDS((T, D), bf16), SDS((1, D), f32)),
        compiler_params=_cparams(("arbitrary",)), name=name)(dh, x, g, dres)


def _gain_grad(dh, x, name):
    T = x.shape[0]
    tm = min(512, T)

    def body(dh_ref, x_ref, dg_ref):
        i = pl.program_id(0)
        xv = x_ref[...]
        r = lax.rsqrt(jnp.mean(xv * xv, axis=-1, keepdims=True) + EPS)
        dg = jnp.sum(dh_ref[...] * (xv * r), axis=0, keepdims=True)

        @pl.when(i == 0)
        def _():
            dg_ref[...] = dg

        @pl.when(i > 0)
        def _():
            dg_ref[...] += dg

    tok = pl.BlockSpec((tm, D), lambda i: (i, 0))
    return pl.pallas_call(
        body, grid=(T // tm,), in_specs=[tok, tok], out_specs=pl.BlockSpec((1, D), lambda i: (0, 0)),
        out_shape=SDS((1, D), f32), compiler_params=_cparams(("arbitrary",)), name=name)(dh, x)


FFN_ROW_CHUNK = 256


def _ffn_tiles(T):
    return min(1024, T), 256


def _ffn_fwd(h, w, x, target, name):
    T = h.shape[0]
    tm, tf = _ffn_tiles(T)
    nf = F // tf
    with_loss = target is not None

    def body(*refs):
        if with_loss:
            h_ref, w_ref, x_hbm, t_hbm, xo_ref, g_ref, u_ref, dyb_ref, loss_ref, tbuf, sem = refs
        else:
            h_ref, w_ref, x_hbm, xo_ref, g_ref, u_ref, sem = refs
        fi = pl.program_id(0)

        @pl.when(fi == 0)
        def _():
            cp = pltpu.make_async_copy(x_hbm, xo_ref, sem)
            cp.start()
            cp.wait()

        wgu = w_ref[0:2].reshape(2 * tf, D)
        for r in range(0, T, tm):
            rows = slice(r, r + tm)
            gu = _nt(h_ref[rows, :], wgu)
            gate, up = gu[:, :tf], gu[:, tf:]
            act = gate * _sigmoid(gate) * up
            g_ref[0, rows, :] = gate.astype(bf16)
            u_ref[0, rows, :] = up.astype(bf16)
            xo_ref[rows, :] += _nn((0.5 * act).astype(bf16), w_ref[2])

        if with_loss:
            @pl.when(fi == nf - 1)
            def _():
                lanes = jnp.zeros((1, 128), f32)
                for r in range(0, T, tm):
                    rows = slice(r, r + tm)
                    cp = pltpu.make_async_copy(t_hbm.at[pl.ds(r, tm), :], tbuf, sem)
                    cp.start()
                    cp.wait()
                    e = xo_ref[rows, :] - tbuf[...]
                    dy = e * (1.0 / D)
                    xo_ref[rows, :] = dy
                    dyb_ref[rows, :] = (0.5 * dy).astype(bf16)
                    col = jnp.sum(e * e, axis=0, keepdims=True) * (0.5 / D)
                    for k in range(D // 128):
                        lanes = lanes + col[:, 128 * k:128 * (k + 1)]
                loss_ref[...] = lanes

    tok = pl.BlockSpec((T, D), lambda f: (0, 0))
    act_spec = pl.BlockSpec((1, T, tf), lambda f: (f, 0, 0))
    hbm = pl.BlockSpec(memory_space=pl.ANY)
    in_specs = [tok, pl.BlockSpec((3, tf, D), lambda f: (0, f, 0)), hbm]
    out_specs = [tok, act_spec, act_spec]
    out_shape = [SDS((T, D), f32), SDS((nf, T, tf), bf16), SDS((nf, T, tf), bf16)]
    scratch = [pltpu.SemaphoreType.DMA]
    args = [h, w, x]
    if with_loss:
        in_specs.append(hbm)
        args.append(target)
        out_specs += [tok, pl.BlockSpec((1, 128), lambda f: (0, 0))]
        out_shape += [SDS((T, D), bf16), SDS((1, 128), f32)]
        scratch = [pltpu.VMEM((tm, D), f32)] + scratch
    return pl.pallas_call(
        body, grid=(nf,), in_specs=in_specs, out_specs=out_specs, out_shape=tuple(out_shape), scratch_shapes=scratch,
        compiler_params=_cparams(("arbitrary",), VMEM_LIMIT_V7X), name=name)(*args)


def _ffn_bwd(dob, h, gate, up, w, name):
    T = h.shape[0]
    _, tf = _ffn_tiles(T)
    nf = F // tf

    def body(do_hbm, h_hbm, g_ref, u_ref, w_ref, dh_hbm, dw_ref, do_v, h_v, dh_acc, dgu_s, act_s, sems):
        fi = pl.program_id(0)

        @pl.when(fi == 0)
        def _():
            loads = [pltpu.make_async_copy(do_hbm, do_v, sems.at[0]), pltpu.make_async_copy(h_hbm, h_v, sems.at[1])]
            for cp in loads:
                cp.start()
            dh_acc[...] = jnp.zeros_like(dh_acc)
            for cp in loads:
                cp.wait()

        wgu = w_ref[0:2].reshape(2 * tf, D)
        for r in range(0, T, FFN_ROW_CHUNK):
            rows = slice(r, r + FFN_ROW_CHUNK)
            dov = do_v[rows, :]
            gv = g_ref[0, rows, :].astype(f32)
            uv = u_ref[0, rows, :].astype(f32)
            sg = _sigmoid(gv)
            sil = gv * sg
            dact = _nt(dov, w_ref[2])
            dup = dact * sil
            dgate = dact * uv * (sg * (1.0 + gv * (1.0 - sg)))
            dgu = jnp.concatenate([dgate.astype(bf16), dup.astype(bf16)], axis=1)
            dgu_s[rows, :] = dgu
            act_s[rows, :] = (sil * uv).astype(bf16)
            dh_acc[rows, :] += _nn(dgu, wgu)
        dw_ref[0:2] = _tn(dgu_s[...], h_v[...]).reshape(2, tf, D).astype(bf16)
        dw_ref[2] = _tn(act_s[...], do_v[...]).astype(bf16)

        @pl.when(fi == nf - 1)
        def _():
            out = pltpu.make_async_copy(dh_acc, dh_hbm, sems.at[0])
            out.start()
            out.wait()

    act_spec = pl.BlockSpec((1, T, tf), lambda f: (f, 0, 0))
    wspec = pl.BlockSpec((3, tf, D), lambda f: (0, f, 0))
    hbm = pl.BlockSpec(memory_space=pl.ANY)
    return pl.pallas_call(
        body, grid=(nf,),
        in_specs=[hbm, hbm, act_spec, act_spec, wspec],
        out_specs=[hbm, wspec],
        out_shape=(SDS((T, D), f32), SDS((3, F, D), bf16)),
        scratch_shapes=[pltpu.VMEM((T, D), bf16), pltpu.VMEM((T, D), bf16), pltpu.VMEM((T, D), f32),
                        pltpu.VMEM((T, 2 * tf), bf16), pltpu.VMEM((T, tf), bf16), pltpu.SemaphoreType.DMA((2,))],
        compiler_params=_cparams(("arbitrary",), VMEM_LIMIT_V7X), name=name)(dob, h, gate, up, w)


def _in_proj_fwd(h, wint, name):
    T = h.shape[0]
    tm = min(512, T)

    def body(h_ref, w_ref, z_ref):
        z_ref[...] = _nt(h_ref[...], w_ref[...])

    return pl.pallas_call(
        body, grid=(T // tm,),
        in_specs=[pl.BlockSpec((tm, D), lambda i: (i, 0)), pl.BlockSpec((DIN, D), lambda i: (0, 0))],
        out_specs=pl.BlockSpec((tm, DIN), lambda i: (i, 0)),
        out_shape=SDS((T, DIN), f32), name=name)(h, wint)


def _in_proj_bwd(dz, wint, h, name):
    T = h.shape[0]
    tm = min(512, T)
    nt = T // tm

    def body(dz_ref, w_ref, h_ref, dh_ref, dw_ref, acc):
        i = pl.program_id(0)
        dzb = dz_ref[...].astype(bf16)
        dh_ref[...] = _nn(dzb, w_ref[...])
        part = _tn(dzb, h_ref[...])

        @pl.when(i == 0)
        def _():
            acc[...] = part

        @pl.when(i > 0)
        def _():
            acc[...] += part

        @pl.when(i == nt - 1)
        def _():
            dw_ref[...] = acc[...].astype(bf16)

    wspec = pl.BlockSpec((DIN, D), lambda i: (0, 0))
    return pl.pallas_call(
        body, grid=(nt,),
        in_specs=[pl.BlockSpec((tm, DIN), lambda i: (i, 0)), wspec, pl.BlockSpec((tm, D), lambda i: (i, 0))],
        out_specs=[pl.BlockSpec((tm, D), lambda i: (i, 0)), wspec],
        out_shape=(SDS((T, D), f32), SDS((DIN, D), bf16)),
        scratch_shapes=[pltpu.VMEM((DIN, D), f32)],
        compiler_params=_cparams(("arbitrary",)), name=name)(dz, wint, h)


def _out_proj_fwd(ymix, wout, x, g, name):
    T = x.shape[0]
    tm = min(512, T)

    def body(y_ref, w_ref, x_ref, g_ref, o_ref, h_ref):
        o = x_ref[...] + _nn(y_ref[...], w_ref[...])
        o_ref[...] = o
        r = lax.rsqrt(jnp.mean(o * o, axis=-1, keepdims=True) + EPS)
        h_ref[...] = (o * r * g_ref[...]).astype(bf16)

    tok = pl.BlockSpec((tm, D), lambda i: (i, 0))
    return pl.pallas_call(
        body, grid=(T // tm,),
        in_specs=[pl.BlockSpec((tm, DMIX), lambda i: (i, 0)), pl.BlockSpec((DMIX, D), lambda i: (0, 0)), tok,
                  pl.BlockSpec((1, D), lambda i: (0, 0))],
        out_specs=[tok, tok], out_shape=(SDS((T, D), f32), SDS((T, D), bf16)), name=name)(ymix, wout, x, g)


def _out_proj_bwd(dxb, wout, ymix, name):
    T = dxb.shape[0]
    tm = min(512, T)
    nt = T // tm

    def body(dx_ref, w_ref, y_ref, dy_ref, dw_ref, acc):
        i = pl.program_id(0)
        dxv = dx_ref[...]
        dy_ref[...] = _nt(dxv, w_ref[...])
        part = _tn(y_ref[...], dxv)

        @pl.when(i == 0)
        def _():
            acc[...] = part

        @pl.when(i > 0)
        def _():
            acc[...] += part

        @pl.when(i == nt - 1)
        def _():
            dw_ref[...] = acc[...].astype(bf16)

    wspec = pl.BlockSpec((DMIX, D), lambda i: (0, 0))
    return pl.pallas_call(
        body, grid=(nt,),
        in_specs=[pl.BlockSpec((tm, D), lambda i: (i, 0)), wspec, pl.BlockSpec((tm, DMIX), lambda i: (i, 0))],
        out_specs=[pl.BlockSpec((tm, DMIX), lambda i: (i, 0)), wspec],
        out_shape=(SDS((T, DMIX), f32), SDS((DMIX, D), bf16)),
        scratch_shapes=[pltpu.VMEM((DMIX, D), f32)],
        compiler_params=_cparams(("arbitrary",)), name=name)(dxb, wout, ymix)


def _t5_bucket_table():
    ql = np.arange(BLK)[:, None]
    kl = np.arange(2 * BLK)[None, :]
    n = np.maximum(ql + BLK - kl, 0)
    max_exact = NBUCK // 2
    large = max_exact + (np.log(np.maximum(n, 1) / max_exact) / np.log(MAX_DISTANCE / max_exact)
                         * (NBUCK - max_exact)).astype(np.int32)
    large = np.minimum(large, NBUCK - 1)
    return np.where(n < max_exact, n, large).astype(np.int32)


def _fill_bias(bk_ref, rb_ref, bias_scr):
    bk = bk_ref[...]
    for h in range(NH):
        def step(b, acc, h=h):
            return acc + jnp.where(bk == b, rb_ref[b, h], 0.0)
        bias_scr[h] = lax.fori_loop(0, NBUCK, step, jnp.zeros((BLK, 2 * BLK), f32))


MIX_SUB = 4


class _Window:
    def __init__(self, zc_ref, zp_ref, n, s):
        self.blk = n * MIX_SUB + s
        self.first_in_step = s == 0
        self.cur = lambda a, b: zc_ref[s * BLK:(s + 1) * BLK, a:b]
        self.prev = (lambda a, b: zp_ref[:, a:b]) if s == 0 else (lambda a, b: zc_ref[(s - 1) * BLK:s * BLK, a:b])


def _attn_qkv(win, kh, qg, kg):
    kc = DATTN + HD * kh
    vc = DATTN + DKV + HD * kh
    kx = jnp.concatenate([win.prev(kc, kc + HD), win.cur(kc, kc + HD)], axis=0)
    vx = jnp.concatenate([win.prev(vc, vc + HD), win.cur(vc, vc + HD)], axis=0)
    qx = jnp.concatenate([win.cur(HD * (GQA * kh + g), HD * (GQA * kh + g + 1)) for g in range(GQA)], axis=0)
    rq = lax.rsqrt(jnp.mean(qx * qx, axis=-1, keepdims=True) + EPS)
    rk = lax.rsqrt(jnp.mean(kx * kx, axis=-1, keepdims=True) + EPS)
    qhat, khat = qx * rq, kx * rk
    return dict(qhat=qhat, khat=khat, rq=rq, rk=rk, qnb=(qhat * qg).astype(bf16), knb=(khat * kg).astype(bf16),
                vb=vx.astype(bf16))


def _attn_probs(a, kh, sk_ref, bias_scr, n):
    s = _nt(a["qnb"], a["knb"]) * SCALE + bias_scr[GQA * kh:GQA * (kh + 1)].reshape(GQA * BLK, 2 * BLK)
    row = lax.broadcasted_iota(i32, (GQA * BLK, 2 * BLK), 0) & (BLK - 1)
    col = lax.broadcasted_iota(i32, (GQA * BLK, 2 * BLK), 1)
    mask = (col > row) & (col <= row + BLK) & ((col >= BLK) | (n > 0))
    s = jnp.where(mask, s, NEG)
    ridx = lax.broadcasted_iota(i32, (GQA * BLK, 1), 0)
    sink = jnp.full((GQA * BLK, 1), sk_ref[GQA * kh + GQA - 1], f32)
    for g in range(GQA - 2, -1, -1):
        sink = jnp.where(ridx < (g + 1) * BLK, sk_ref[GQA * kh + g], sink)
    m = jnp.maximum(jnp.max(s, axis=-1, keepdims=True), sink)
    e = jnp.exp(s - m)
    den = jnp.sum(e, axis=-1, keepdims=True) + jnp.exp(sink - m)
    return e / den


POOL_STEPS = {2: (1,), 4: (1, 2), 8: (1, 2, 4), 16: (1, 2, 4, 8)}


def _pool_group(win, g, w):
    n = win.blk
    c0 = DATTN + 2 * DKV + PGD * g
    uc = win.cur(c0, c0 + PGD)
    up = jnp.where(n > 0, win.prev(c0, c0 + PGD), 0.0)
    sm = jnp.concatenate([up, uc], axis=0)
    for k in POOL_STEPS[w]:
        sm = sm + pltpu.roll(sm, k, axis=0)
    pos = n * BLK + lax.broadcasted_iota(i32, (BLK, 1), 0) + 1
    cnt = jnp.minimum(pos, w).astype(f32)
    return sm[BLK:2 * BLK] / cnt - uc, cnt


def _mix_fwd(z, qg, kg, sinks, relb, bucket, pool_w, pscale, name):
    T = z.shape[0]
    step_rows = MIX_SUB * BLK
    nsteps = T // step_rows

    def body(zc_ref, zp_ref, qg_ref, kg_ref, sk_ref, rb_ref, bk_ref, pw_ref, ps_ref, y_ref, p_ref, bias_scr, yacc):
        n = pl.program_id(0)

        @pl.when(n == 0)
        def _():
            _fill_bias(bk_ref, rb_ref, bias_scr)

        for s in range(MIX_SUB):
            win = _Window(zc_ref, zp_ref, n, s)
            rows = slice(s * BLK, (s + 1) * BLK)
            for kh in range(NKV):
                a = _attn_qkv(win, kh, qg_ref[...], kg_ref[...])
                pb = _attn_probs(a, kh, sk_ref, bias_scr, win.blk).astype(bf16)
                p_ref[s, GQA * kh:GQA * (kh + 1)] = pb.reshape(GQA, BLK, 2 * BLK)
                o = _nn(pb, a["vb"])
                for g in range(GQA):
                    hc = HD * (GQA * kh + g)
                    yacc[rows, hc:hc + HD] = o[g * BLK:(g + 1) * BLK]
            for g, w in enumerate(POOL_WINDOWS):
                pooled, _ = _pool_group(win, g, w)
                yp = _nn(pooled.astype(bf16), pw_ref[g].astype(bf16)) * ps_ref[:, PGD * g:PGD * (g + 1)]
                yacc[rows, DATTN + PGD * g:DATTN + PGD * (g + 1)] = yp
        y_ref[...] = yacc[...].astype(bf16)

    full = lambda *shape: pl.BlockSpec(shape, lambda n: (0,) * len(shape))
    smem = pl.BlockSpec(memory_space=pltpu.SMEM)
    return pl.pallas_call(
        body, grid=(nsteps,),
        in_specs=[pl.BlockSpec((step_rows, DIN), lambda n: (n, 0)),
                  pl.BlockSpec((BLK, DIN), lambda n: (jnp.maximum(n * MIX_SUB - 1, 0), 0)),
                  full(1, HD), full(1, HD), smem, smem, full(BLK, 2 * BLK),
                  full(len(POOL_WINDOWS), PGD, PGD), full(1, DPOOL)],
        out_specs=[pl.BlockSpec((step_rows, DMIX), lambda n: (n, 0)),
                   pl.BlockSpec((MIX_SUB, NH, BLK, 2 * BLK), lambda n: (n, 0, 0, 0))],
        out_shape=(SDS((T, DMIX), bf16), SDS((T // BLK, NH, BLK, 2 * BLK), bf16)),
        scratch_shapes=[pltpu.VMEM((NH, BLK, 2 * BLK), f32), pltpu.VMEM((step_rows, DMIX), f32)],
        compiler_params=_cparams(("arbitrary",)), name=name)(z, z, qg, kg, sinks, relb, bucket, pool_w, pscale)


def _mix_bwd(z, dy, probs, qg, kg, relb, bucket, pool_w, pscale, name):
    T = z.shape[0]
    step_rows = MIX_SUB * BLK
    nsteps = T // step_rows

    def body(zc_ref, zp_ref, dy_ref, p_ref, qg_ref, kg_ref, bk_ref, pw_ref, ps_ref,
             dz_ref, dqg_ref, dkg_ref, dsk_ref, drb_ref, dpw_ref, dps_ref, dbias_scr):
        n = pl.program_id(0)

        @pl.when(n == 0)
        def _():
            dbias_scr[...] = jnp.zeros_like(dbias_scr)
            dqg_ref[...] = jnp.zeros_like(dqg_ref)
            dkg_ref[...] = jnp.zeros_like(dkg_ref)
            dpw_ref[...] = jnp.zeros_like(dpw_ref)
            dps_ref[...] = jnp.zeros_like(dps_ref)

        qg, kg = qg_ref[...], kg_ref[...]
        for s in range(MIX_SUB):
            win = _Window(zc_ref, zp_ref, n, s)
            blk = win.blk
            rows = pl.ds(pl.multiple_of(blk * BLK, BLK), BLK)
            prow = pl.ds(pl.multiple_of(jnp.maximum(blk - 1, 0) * BLK, BLK), BLK)
            dyr = slice(s * BLK, (s + 1) * BLK)

            def into_prev(fn, s=s):
                if s == 0:
                    pl.when(n > 0)(fn)
                else:
                    fn()

            for kh in range(NKV):
                a = _attn_qkv(win, kh, qg, kg)
                pb = p_ref[s, GQA * kh:GQA * (kh + 1)].reshape(GQA * BLK, 2 * BLK)
                p = pb.astype(f32)
                do = jnp.concatenate([dy_ref[dyr, HD * (GQA * kh + g):HD * (GQA * kh + g + 1)] for g in range(GQA)],
                                     axis=0).astype(bf16)
                dv = _tn(pb, do)
                dp = _nt(do, a["vb"])
                delta = jnp.sum(p * dp, axis=-1, keepdims=True)
                ds = p * (dp - delta)
                for g in range(GQA):
                    dbias_scr[GQA * kh + g] += ds[g * BLK:(g + 1) * BLK]
                dsb = ds.astype(bf16)
                dqn = _nn(dsb, a["knb"]) * SCALE
                dkn = _tn(dsb, a["qnb"]) * SCALE
                qhat, khat = a["qhat"], a["khat"]
                dqg_ref[...] += jnp.sum(dqn * qhat, axis=0, keepdims=True)
                dkg_ref[...] += jnp.sum(dkn * khat, axis=0, keepdims=True)
                dqh = dqn * qg
                dq = a["rq"] * (dqh - qhat * jnp.mean(dqh * qhat, axis=-1, keepdims=True))
                dkh = dkn * kg
                dk = a["rk"] * (dkh - khat * jnp.mean(dkh * khat, axis=-1, keepdims=True))
                kc = DATTN + HD * kh
                vc = DATTN + DKV + HD * kh
                for g in range(GQA):
                    hc = HD * (GQA * kh + g)
                    dz_ref[rows, hc:hc + HD] = dq[g * BLK:(g + 1) * BLK]
                dz_ref[rows, kc:kc + HD] = dk[BLK:2 * BLK]
                dz_ref[rows, vc:vc + HD] = dv[BLK:2 * BLK]

                def kv_prev(dk=dk, dv=dv, kc=kc, vc=vc, prow=prow):
                    dz_ref[prow, kc:kc + HD] += dk[0:BLK]
                    dz_ref[prow, vc:vc + HD] += dv[0:BLK]

                into_prev(kv_prev)

            for g, w in enumerate(POOL_WINDOWS):
                c0 = DATTN + 2 * DKV + PGD * g
                pooled, cnt = _pool_group(win, g, w)
                pb = pooled.astype(bf16)
                wb = pw_ref[g].astype(bf16)
                dyp = dy_ref[dyr, DATTN + PGD * g:DATTN + PGD * (g + 1)]
                ypre = _nn(pb, wb)
                dps_ref[:, PGD * g:PGD * (g + 1)] += jnp.sum(dyp * ypre, axis=0, keepdims=True)
                dyg = (dyp * ps_ref[:, PGD * g:PGD * (g + 1)]).astype(bf16)
                dpw_ref[g] += _tn(pb, dyg)
                dpooled = _nt(dyg, wb)
                due = jnp.concatenate([jnp.zeros((BLK, PGD), f32), dpooled / cnt], axis=0)
                for k in POOL_STEPS[w]:
                    due = due + pltpu.roll(due, 2 * BLK - k, axis=0)
                dz_ref[rows, c0:c0 + PGD] = due[BLK:2 * BLK] - dpooled

                def pool_prev(due=due, c0=c0, prow=prow):
                    dz_ref[prow, c0:c0 + PGD] += due[0:BLK]

                into_prev(pool_prev)

        @pl.when(n == nsteps - 1)
        def _():
            bk = bk_ref[...]
            ri = lax.broadcasted_iota(i32, (NBUCK, NH), 0)
            ci = lax.broadcasted_iota(i32, (NBUCK, NH), 1)

            def step(b, acc):
                for h in range(NH):
                    sel = jnp.where(bk == b, dbias_scr[h], 0.0)
                    tot = jnp.sum(jnp.sum(sel, axis=1, keepdims=True), axis=0, keepdims=True)
                    acc = acc + jnp.where((ri == b) & (ci == h), tot, 0.0)
                return acc

            drb_ref[...] = lax.fori_loop(0, NBUCK, step, jnp.zeros((NBUCK, NH), f32))
            lane = lax.broadcasted_iota(i32, (1, 128), 1)
            dsk = jnp.zeros((1, 128), f32)
            for h in range(NH):
                tot = jnp.sum(jnp.sum(dbias_scr[h], axis=1, keepdims=True), axis=0, keepdims=True)
                dsk = dsk - jnp.where(lane == h, tot, 0.0)
            dsk_ref[...] = dsk

    full = lambda *shape: pl.BlockSpec(shape, lambda n: (0,) * len(shape))
    npg = len(POOL_WINDOWS)
    return pl.pallas_call(
        body, grid=(nsteps,),
        in_specs=[pl.BlockSpec((step_rows, DIN), lambda n: (n, 0)),
                  pl.BlockSpec((BLK, DIN), lambda n: (jnp.maximum(n * MIX_SUB - 1, 0), 0)),
                  pl.BlockSpec((step_rows, DMIX), lambda n: (n, 0)),
                  pl.BlockSpec((MIX_SUB, NH, BLK, 2 * BLK), lambda n: (n, 0, 0, 0)),
                  full(1, HD), full(1, HD), full(BLK, 2 * BLK), full(npg, PGD, PGD), full(1, DPOOL)],
        out_specs=[full(T, DIN), full(1, HD), full(1, HD), full(1, 128), full(NBUCK, NH),
                   full(npg, PGD, PGD), full(1, DPOOL)],
        out_shape=(SDS((T, DIN), f32), SDS((1, HD), f32), SDS((1, HD), f32), SDS((1, 128), f32),
                   SDS((NBUCK, NH), f32), SDS((npg, PGD, PGD), f32), SDS((1, DPOOL), f32)),
        scratch_shapes=[pltpu.VMEM((NH, BLK, 2 * BLK), f32)],
        compiler_params=_cparams(("arbitrary",), VMEM_LIMIT_V7X),
        name=name)(z, z, dy, probs, qg, kg, bucket, pool_w, pscale)


class _LocalWeights:
    def __init__(self, w1, wint, wout, w2):
        self.w1, self.wint, self.wout, self.w2 = w1, wint, wout, w2

    def ffn1(self):
        return self.w1

    def first_norm(self, x, gain):
        return _norm_fwd(x, gain, "norm1_fwd")

    def after_ffn1(self, gain, x1):
        return gain

    def mix(self, after):
        return self.wint, self.wout

    def before_out_proj(self, wout, after):
        return wout

    def ffn2(self, after):
        return self.w2

    def mix_ffn2_grads_ready(self, dwint, dwout, dw2, dh2):
        self.grads_rest = (dwint, dwout, dw2)
        return dh2

    def before_ffn1_bwd(self, dx1b):
        return dx1b


def _local_step(x, target, weights, g1, gm, g3, qg, kg, sinks, relb, pool_w, pscale):
    bucket = jnp.asarray(_t5_bucket_table())
    sk = sinks.reshape(NH)
    w1 = weights.ffn1()
    h1 = weights.first_norm(x, g1)
    x1, gate1, up1 = _ffn_fwd(h1, w1, x, None, "ffn1_fwd")
    h2 = _norm_fwd(x1, weights.after_ffn1(gm, x1), "norm2_fwd")
    wint, wout = weights.mix(h2)
    z = _in_proj_fwd(h2, wint, "in_proj_fwd")
    ymix, probs = _mix_fwd(z, qg, kg, sk, relb, bucket, pool_w, pscale, "mix_fwd")
    wout = weights.before_out_proj(wout, ymix)
    x2, h3 = _out_proj_fwd(ymix, wout, x1, g3, "out_proj_fwd")
    w2 = weights.ffn2(h3)
    dy, gate2, up2, dyb, loss_lanes = _ffn_fwd(h3, w2, x2, target, "ffn2_fwd")

    dh3, dw2 = _ffn_bwd(dyb, h3, gate2, up2, w2, "ffn2_bwd")
    dx2, dx2b, dg3 = _norm_bwd(dh3, x2, g3, dy, 1.0, "norm3_bwd")
    dymix, dwout = _out_proj_bwd(dx2b, wout, ymix, "out_proj_bwd")
    dz, dqg, dkg, dsk, drb, dpw, dps = _mix_bwd(z, dymix, probs, qg, kg, relb, bucket, pool_w, pscale, "mix_bwd")
    dh2, dwint = _in_proj_bwd(dz, wint, h2, "in_proj_bwd")
    dh2 = weights.mix_ffn2_grads_ready(dwint, dwout, dw2, dh2)
    dx1, dx1b, dgm = _norm_bwd(dh2, x1, gm, dx2, 0.5, "norm2_bwd")
    dx1b = weights.before_ffn1_bwd(dx1b)
    dh1, dw1 = _ffn_bwd(dx1b, h1, gate1, up1, w1, "ffn1_bwd")
    dg1 = _gain_grad(dh1, x, "norm1_gain_grad")
    small = dict(ffn1_norm=dg1, mix_norm=dgm, ffn2_norm=dg3, pool_scale=dps, q_norm=dqg, k_norm=dkg,
                 attn_sinks=dsk[:, :NH], rel_bias=drb, pool_w=dpw, loss=loss_lanes)
    return (dh1, dx1), (dw1, dwint, dwout, dw2), small


SMALL_NAMES = ("ffn1_norm", "mix_norm", "ffn2_norm", "pool_scale", "q_norm", "k_norm", "attn_sinks", "rel_bias",
               "pool_w", "loss")
SMALL_SHAPES = dict(ffn1_norm=(1, D), mix_norm=(1, D), ffn2_norm=(1, D), pool_scale=(1, DPOOL), q_norm=(1, HD),
                    k_norm=(1, HD), attn_sinks=(1, NH), rel_bias=(NBUCK, NH),
                    pool_w=(1, len(POOL_WINDOWS), PGD, PGD), loss=(1, 128))


def _small_rows(name):
    return -(-int(np.prod(SMALL_SHAPES[name])) // 128)


SMALL_OFF = {}
_r = 0
for _n in SMALL_NAMES:
    SMALL_OFF[_n] = _r
    _r += _small_rows(_n)
SMALL_ROWS = -(-_r // 8) * 8
LOSS_ROW = SMALL_OFF["loss"]


def _pack_small(vals):
    parts = []
    for n in SMALL_NAMES:
        size = _small_rows(n) * 128
        if n in vals:
            flat = vals[n].astype(f32).reshape(-1)
            parts.append(jnp.pad(flat, (0, size - flat.shape[0])))
        else:
            parts.append(jnp.zeros((size,), f32))
    flat = jnp.concatenate(parts)
    flat = jnp.pad(flat, (0, SMALL_ROWS * 128 - flat.shape[0]))
    return flat.reshape(SMALL_ROWS, 128)


def _unpack_small(packed, name):
    size = int(np.prod(SMALL_SHAPES[name]))
    r0 = SMALL_OFF[name]
    return packed[r0:r0 + _small_rows(name)].reshape(-1)[:size].reshape(SMALL_SHAPES[name])


def _position():
    return lax.axis_index("x"), lax.axis_index("y"), lax.axis_index("c")


def _dev_index(x, y, c):
    return 4 * x + 2 * y + c


G1_PIECES, MIX_PIECES, F2_PIECES = (0, 1, 2), (3, 4), (5, 6, 7)


def _group_rows(pieces):
    return sum(PIECE_ROWS[k] for k in pieces)


def _shard_piece(s_ref, k):
    return s_ref.at[pl.ds(PIECE_OFF[k], PIECE_ROWS[k]), :]


def _shard_group(s_ref, pieces):
    return s_ref.at[pl.ds(PIECE_OFF[pieces[0]], _group_rows(pieces)), :]


def _weight_pieces(w1_ref=None, wi_ref=None, wo_ref=None, w2_ref=None):
    arrs = {}
    if w1_ref is not None:
        arrs.update({0: w1_ref.at[0], 1: w1_ref.at[1], 2: w1_ref.at[2]})
    if wi_ref is not None:
        arrs[3] = wi_ref
    if wo_ref is not None:
        arrs[4] = wo_ref
    if w2_ref is not None:
        arrs.update({5: w2_ref.at[0], 6: w2_ref.at[1], 7: w2_ref.at[2]})
    return arrs


def _block_rows(arrs, k, dev):
    r = PIECE_ROWS[k]
    return arrs[k].at[pl.ds(pl.multiple_of(_dev_index(*dev) * r, 16), r), :]


NORM_ROWS = 512


def _all_gather_ffn1(shard, x, gain):
    pieces = G1_PIECES
    half = FS // 2
    T = x.shape[0]
    SIB, X0, X1, Y0, Y1, RELAY_Y, RELAY_X, ON_X, ON_Y, ON_D0, ON_D1 = range(11)

    def body(s_ref, x_ref, g_ref, w1_ref, h_ref, xbuf, hbuf, send_sems, recv_sems, local_sem, norm_sems):
        x, y, c = _position()
        me, sib = (x, y, c), (x, y, 1 - c)
        xn, yn, dg = (1 - x, y, c), (x, 1 - y, c), (1 - x, 1 - y, c)
        arrs = _weight_pieces(w1_ref=w1_ref)

        def first_norm():
            for r in range(0, T, NORM_ROWS):
                load = pltpu.make_async_copy(x_ref.at[pl.ds(r, NORM_ROWS), :], xbuf, norm_sems.at[0])
                load.start()
                load.wait()
                xv = xbuf[...]
                rs = lax.rsqrt(jnp.mean(xv * xv, axis=-1, keepdims=True) + EPS)
                hbuf[...] = (xv * rs * g_ref[...]).astype(bf16)
                store = pltpu.make_async_copy(hbuf, h_ref.at[pl.ds(r, NORM_ROWS), :], norm_sems.at[1])
                store.start()
                store.wait()

        def rows_of(k, block, hf):
            r = PIECE_ROWS[k]
            start, size = (0, r) if hf is None else (hf * half, half)
            return arrs[k].at[pl.ds(pl.multiple_of(_dev_index(*block) * r + start, 16), size), :]

        def copies(rel, block, hf, to, from_shard=False):
            def src(k):
                if not from_shard:
                    return rows_of(k, block, hf)
                start, size = (0, PIECE_ROWS[k]) if hf is None else (hf * half, half)
                return s_ref.at[pl.ds(PIECE_OFF[k] + start, size), :]
            return [pltpu.make_async_remote_copy(
                src_ref=src(k), dst_ref=rows_of(k, block, hf), send_sem=send_sems.at[rel], recv_sem=recv_sems.at[rel],
                device_id=to, device_id_type=MESH) for k in pieces]

        def waiter(rel, hf):
            nrows = len(pieces) * (FS if hf is None else half)
            grp = s_ref.at[pl.ds(0, nrows), :]
            return pltpu.make_async_remote_copy(src_ref=grp, dst_ref=grp, send_sem=send_sems.at[rel],
                                                recv_sem=recv_sems.at[rel], device_id=me, device_id_type=MESH)

        def start(cps):
            for cp in cps:
                cp.start()

        mine = [pltpu.make_async_copy(_shard_piece(s_ref, k), _block_rows(arrs, k, me), local_sem) for k in pieces]
        start(mine)
        start(copies(SIB, me, None, sib, True))
        start(copies(X0, me, 0, xn, True))
        start(copies(Y1, me, 1, yn, True))
        start(copies(X1, me, 1, xn, True))
        start(copies(Y0, me, 0, yn, True))
        first_norm()
        waiter(X0, 0).wait_recv()
        start(copies(RELAY_Y, xn, 0, yn))
        waiter(Y1, 1).wait_recv()
        start(copies(RELAY_X, yn, 1, xn))
        waiter(X1, 1).wait_recv()
        start(copies(ON_X, xn, None, sib))
        waiter(Y0, 0).wait_recv()
        start(copies(ON_Y, yn, None, sib))
        waiter(RELAY_Y, 0).wait_recv()
        start(copies(ON_D0, dg, 0, sib))
        waiter(RELAY_X, 1).wait_recv()
        start(copies(ON_D1, dg, 1, sib))
        waiter(SIB, None).wait_recv()
        waiter(ON_X, None).wait_recv()
        waiter(ON_Y, None).wait_recv()
        waiter(ON_D0, 0).wait_recv()
        waiter(ON_D1, 1).wait_recv()
        for rel, hf in ((SIB, None), (X0, 0), (X1, 1), (Y0, 0), (Y1, 1), (RELAY_Y, 0), (RELAY_X, 1),
                        (ON_X, None), (ON_Y, None), (ON_D0, 0), (ON_D1, 1)):
            waiter(rel, hf).wait_send()
        grp = _shard_group(s_ref, pieces)
        pltpu.make_async_copy(grp, grp, local_sem).wait()

    hbm = pl.BlockSpec(memory_space=pl.ANY)
    return pl.pallas_call(
        body, in_specs=[hbm, hbm, pl.BlockSpec(memory_space=pltpu.VMEM)], out_specs=[hbm, hbm],
        out_shape=(SDS((3, F, D), bf16), SDS((T, D), bf16)),
        scratch_shapes=[pltpu.VMEM((NORM_ROWS, D), f32), pltpu.VMEM((NORM_ROWS, D), bf16),
                        pltpu.SemaphoreType.DMA((11,)), pltpu.SemaphoreType.DMA((11,)), pltpu.SemaphoreType.DMA,
                        pltpu.SemaphoreType.DMA((2,))],
        compiler_params=pltpu.CompilerParams(has_side_effects=True),
        name="all_gather_ffn1")(shard, x, gain)


HBM_SPEC = pl.BlockSpec(memory_space=pltpu.HBM)
SEM_SPEC = pl.BlockSpec(memory_space=pltpu.SEMAPHORE)
ANY_SPEC = pl.BlockSpec(memory_space=pl.ANY)
SPLIT_EFFECT = pltpu.SideEffectType.DATAFLOW_SIDE_EFFECTING


def _in_hbm(a):
    return pltpu.with_memory_space_constraint(a, pltpu.HBM)


def _hbm_like(a):
    return pltpu.HBM(a.shape, a.dtype)


def _place_own_rows(shard):
    pieces = MIX_PIECES + F2_PIECES

    def body(s_ref, wi_ref, wo_ref, w2_ref, buf, sems):
        x, y, c = _position()
        arrs = _weight_pieces(wi_ref=wi_ref, wo_ref=wo_ref, w2_ref=w2_ref)
        grp = _shard_group(s_ref, pieces)
        load = pltpu.make_async_copy(grp, buf, sems.at[0])
        load.start()
        load.wait()
        base = PIECE_OFF[pieces[0]]
        for k in pieces:
            pltpu.make_async_copy(buf.at[pl.ds(PIECE_OFF[k] - base, PIECE_ROWS[k]), :],
                                  _block_rows(arrs, k, (x, y, c)), sems.at[1]).start()
        pltpu.make_async_copy(grp, buf, sems.at[1]).wait()

    return pl.pallas_call(
        body, in_specs=[ANY_SPEC], out_specs=[ANY_SPEC] * 3,
        out_shape=(SDS((DIN, D), bf16), SDS((DMIX, D), bf16), SDS((3, F, D), bf16)),
        scratch_shapes=[pltpu.VMEM((_group_rows(pieces), D), bf16), pltpu.SemaphoreType.DMA((2,))],
        name="place_own_rows")(shard)


def _xor_peer(x, y, c, k):
    return (x ^ (k >> 2), y ^ ((k >> 1) & 1), c ^ (k & 1))


def _gather_rest_start(shard, wi, wo, w2, w1):
    def body(s_ref, wi_ref, wo_ref, w2_ref, w1_ref,
             ssem_m, rsem_m0, rsem_m, ssem_f, rsem_f0, rsem_f, s_o, wi_o, wo_o, w2_o, w1_o):
        x, y, c = _position()
        me, sib = (x, y, c), (x, y, 1 - c)
        chips = [(1 - x, y), (x, 1 - y), (1 - x, 1 - y)]
        arrs = _weight_pieces(wi_ref=wi_ref, wo_ref=wo_ref, w2_ref=w2_ref)
        for pieces, ssem, rsem0, rsem in ((MIX_PIECES, ssem_m, rsem_m0, rsem_m), (F2_PIECES, ssem_f, rsem_f0, rsem_f)):
            for p in pieces:
                pltpu.make_async_remote_copy(
                    src_ref=_shard_piece(s_ref, p), dst_ref=_block_rows(arrs, p, me), send_sem=ssem.at[0],
                    recv_sem=rsem0, device_id=sib, device_id_type=MESH).start()
            for j, chip in enumerate(chips):
                for p in pieces:
                    pltpu.make_async_remote_copy(
                        src_ref=_shard_piece(s_ref, p), dst_ref=_block_rows(arrs, p, me), send_sem=ssem.at[1 + j],
                        recv_sem=rsem.at[j], device_id=(*chip, c), device_id_type=MESH).start()

    dma = pltpu.SemaphoreType.DMA
    return pl.pallas_call(
        body, name="gather_rest_start",
        out_shape=(dma((4,)), dma(()), dma((3,)), dma((4,)), dma(()), dma((3,)),
                   _hbm_like(shard), _hbm_like(wi), _hbm_like(wo), _hbm_like(w2), _hbm_like(w1)),
        in_specs=(HBM_SPEC,) * 5, out_specs=(SEM_SPEC,) * 6 + (HBM_SPEC,) * 5,
        input_output_aliases={0: 6, 1: 7, 2: 8, 3: 9, 4: 10},
        compiler_params=pltpu.CompilerParams(has_side_effects=SPLIT_EFFECT),
    )(_in_hbm(shard), _in_hbm(wi), _in_hbm(wo), _in_hbm(w2), _in_hbm(w1))


def _gather_mix_pass_on(rsem_m, wi, wo, thru, after):
    def body(wi_ref, wo_ref, thru_ref, rsem, after_ref, fsend, frecv, wi_o, wo_o, thru_o):
        x, y, c = _position()
        sib = (x, y, 1 - c)
        arrs = _weight_pieces(wi_ref=wi_ref, wo_ref=wo_ref)
        both = wi_ref.at[pl.ds(0, _group_rows(MIX_PIECES)), :]
        for j, chip in enumerate([(1 - x, y), (x, 1 - y), (1 - x, 1 - y)]):
            pltpu.make_async_remote_copy(src_ref=both, dst_ref=both, send_sem=fsend.at[j], recv_sem=rsem.at[j],
                                         device_id=(x, y, c), device_id_type=MESH).wait_recv()
            for p in MIX_PIECES:
                rows = _block_rows(arrs, p, (*chip, c))
                pltpu.make_async_remote_copy(src_ref=rows, dst_ref=rows, send_sem=fsend.at[j], recv_sem=frecv.at[j],
                                             device_id=sib, device_id_type=MESH).start()

    dma = pltpu.SemaphoreType.DMA
    return pl.pallas_call(
        body, name="gather_mix_pass_on",
        out_shape=(dma((3,)), dma((3,)), _hbm_like(wi), _hbm_like(wo), _hbm_like(thru)),
        in_specs=(HBM_SPEC, HBM_SPEC, HBM_SPEC, SEM_SPEC, ANY_SPEC), out_specs=(SEM_SPEC, SEM_SPEC) + (HBM_SPEC,) * 3,
        input_output_aliases={0: 2, 1: 3, 2: 4},
        compiler_params=pltpu.CompilerParams(has_side_effects=SPLIT_EFFECT),
    )(wi, wo, _in_hbm(thru), rsem_m, after)


def _gather_mix_wait(ssem_m, rsem_m0, fsend, frecv, shard, wi, wo, after):
    def body(s_ref, wi_ref, wo_ref, ssem, rsem0, fs, fr, after_ref, s_o, wi_o, wo_o):
        x, y, c = _position()
        grp = _shard_group(s_ref, MIX_PIECES)

        def waiter(send_sem, recv_sem):
            return pltpu.make_async_remote_copy(src_ref=grp, dst_ref=grp, send_sem=send_sem, recv_sem=recv_sem,
                                                device_id=(x, y, c), device_id_type=MESH)

        waiter(ssem.at[0], rsem0).wait_recv()
        for j in range(3):
            waiter(fs.at[j], fr.at[j]).wait_recv()
        for rel in range(4):
            waiter(ssem.at[rel], rsem0).wait_send()
        for j in range(3):
            waiter(fs.at[j], fr.at[j]).wait_send()

    return pl.pallas_call(
        body, name="gather_mix_wait", out_shape=(_hbm_like(shard), _hbm_like(wi), _hbm_like(wo)),
        in_specs=(HBM_SPEC,) * 3 + (SEM_SPEC,) * 4 + (ANY_SPEC,), out_specs=(HBM_SPEC,) * 3,
        input_output_aliases={0: 0, 1: 1, 2: 2},
        compiler_params=pltpu.CompilerParams(has_side_effects=SPLIT_EFFECT),
    )(shard, wi, wo, ssem_m, rsem_m0, fsend, frecv, after)


def _gather_ffn2_pass_on(rsem_f, w2, wo, after):
    def body(w2_ref, wo_ref, rsem, after_ref, fsend, frecv, w2_o, wo_o):
        x, y, c = _position()
        sib = (x, y, 1 - c)
        chips = [(1 - x, y), (x, 1 - y), (1 - x, 1 - y)]
        arrs = _weight_pieces(w2_ref=w2_ref)
        three = w2_ref.at[0, pl.ds(0, _group_rows(F2_PIECES)), :]
        for j, chip in enumerate(chips):
            pltpu.make_async_remote_copy(src_ref=three, dst_ref=three, send_sem=fsend.at[j], recv_sem=rsem.at[j],
                                         device_id=(x, y, c), device_id_type=MESH).wait_recv()
            for p in F2_PIECES:
                rows = _block_rows(arrs, p, (*chip, c))
                pltpu.make_async_remote_copy(src_ref=rows, dst_ref=rows, send_sem=fsend.at[j], recv_sem=frecv.at[j],
                                             device_id=sib, device_id_type=MESH).start()

    dma = pltpu.SemaphoreType.DMA
    return pl.pallas_call(
        body, name="gather_ffn2_pass_on", out_shape=(dma((3,)), dma((3,)), _hbm_like(w2), _hbm_like(wo)),
        in_specs=(HBM_SPEC, HBM_SPEC, SEM_SPEC, ANY_SPEC), out_specs=(SEM_SPEC, SEM_SPEC, HBM_SPEC, HBM_SPEC),
        input_output_aliases={0: 2, 1: 3},
        compiler_params=pltpu.CompilerParams(has_side_effects=SPLIT_EFFECT),
    )(w2, wo, rsem_f, after)


def _gather_ffn2_wait(ssem_f, rsem_f0, fsend, frecv, shard, w2, after):
    def body(s_ref, w2_ref, ssem, rsem0, fs, fr, after_ref, w2_o):
        x, y, c = _position()
        grp = _shard_group(s_ref, F2_PIECES)

        def waiter(send_sem, recv_sem):
            return pltpu.make_async_remote_copy(src_ref=grp, dst_ref=grp, send_sem=send_sem, recv_sem=recv_sem,
                                                device_id=(x, y, c), device_id_type=MESH)

        waiter(ssem.at[0], rsem0).wait_recv()
        for j in range(3):
            waiter(fs.at[j], fr.at[j]).wait_recv()
        for rel in range(4):
            waiter(ssem.at[rel], rsem0).wait_send()
        for j in range(3):
            waiter(fs.at[j], fr.at[j]).wait_send()

    return pl.pallas_call(
        body, name="gather_ffn2_wait", out_shape=_hbm_like(w2),
        in_specs=(HBM_SPEC, HBM_SPEC, SEM_SPEC, SEM_SPEC, SEM_SPEC, SEM_SPEC, ANY_SPEC), out_specs=HBM_SPEC,
        input_output_aliases={1: 0},
        compiler_params=pltpu.CompilerParams(has_side_effects=SPLIT_EFFECT),
    )(shard, w2, ssem_f, rsem_f0, fsend, frecv, after)


class _GatheredWeights(_LocalWeights):
    def __init__(self, shard, x, gain1):
        w1, self.h1 = _all_gather_ffn1(shard, x, gain1)
        wi, wo, w2 = _place_own_rows(shard)
        (self.ssem_m, self.rsem_m0, self.rsem_m, self.ssem_f, self.rsem_f0, self.rsem_f,
         self.shard, self.wi, self.wo, self.w2_part, self.w1) = _gather_rest_start(shard, wi, wo, w2, w1)

    def first_norm(self, x, gain):
        return self.h1

    def after_ffn1(self, gain, x1):
        self.fsend_m, self.frecv_m, self.wi, self.wo, gain = _gather_mix_pass_on(self.rsem_m, self.wi, self.wo, gain, x1)
        return gain

    def mix(self, after):
        self.shard, wint, wout = _gather_mix_wait(self.ssem_m, self.rsem_m0, self.fsend_m, self.frecv_m, self.shard,
                                                  self.wi, self.wo, after)
        return wint, wout

    def before_out_proj(self, wout, after):
        self.fsend, self.frecv, self.w2_part, wout = _gather_ffn2_pass_on(self.rsem_f, self.w2_part, wout, after)
        return wout

    def ffn2(self, after):
        return _gather_ffn2_wait(self.ssem_f, self.rsem_f0, self.fsend, self.frecv, self.shard, self.w2_part, after)

    def mix_ffn2_grads_ready(self, dwint, dwout, dw2, dh2):
        rx1 = lax.empty((4, RSA_ROWS, D), bf16)
        self.sa, self.ra, dwint, dwout, dw2, rx1, dh2 = _rsa_level1_start(dwint, dwout, dw2, rx1, dh2)
        self.level1 = (dwint, dwout, dw2, rx1)
        return dh2

    def before_ffn1_bwd(self, dx1b):
        dwint, dwout, dw2, rx1 = _rsa_level1_wait(self.sa, self.ra, *self.level1, dx1b)
        tx, self.acc = _rsa_chip_sums(dwint, dwout, dw2, rx1)
        rx2 = lax.empty((3, RSA_ROWS, D), bf16)
        self.sb, self.rb, self.tx, self.rx2, dx1b = _rsa_level2_start(tx, rx2, dx1b)
        return dx1b

    def mix_ffn2_grads_total(self, after):
        rx2 = _rsa_level2_wait(self.sb, self.rb, self.tx, self.rx2, after)
        return _rsa_total(self.acc, rx2)


def _reduce_scatter_ffn1_head(dw1, small_packed):
    pieces = G1_PIECES
    half = FS // 2
    hrows = len(pieces) * half
    nrows = 2 * hrows
    X_RELAY, Y_RELAY = range(2)

    def body(d1_ref, p_ref, forx_ref, fory_ref, own_ref, rx1_ref, relx_ref, rely_ref, tot_ref,
             own_buf, rx_buf, tx1, tx2, tx3, acc, sa, ra, sb, rb, lsem, pair, chips, small_send, small_recv):
        x, y, c = _position()
        me, sib = (x, y, c), (x, y, 1 - c)
        xn, yn = (1 - x, y, c), (x, 1 - y, c)
        rel_chips = [(x, y), (1 - x, y), (x, 1 - y), (1 - x, 1 - y)]
        srcs = _weight_pieces(w1_ref=d1_ref)

        my_chip = 2 * x + y
        pair[c] = p_ref[...]
        swap = pltpu.make_async_remote_copy(
            src_ref=p_ref, dst_ref=pair.at[c], send_sem=small_send.at[0], recv_sem=small_recv.at[0],
            device_id=sib, device_id_type=MESH)
        swap.start()
        small = [pltpu.make_async_remote_copy(
            src_ref=chips.at[my_chip], dst_ref=chips.at[my_chip], send_sem=small_send.at[j], recv_sem=small_recv.at[j],
            device_id=(*rel_chips[j], c), device_id_type=MESH) for j in (1, 2, 3)]

        def part(k, dev, hf):
            r = PIECE_ROWS[k]
            return srcs[k].at[pl.ds(pl.multiple_of(_dev_index(*dev) * r + hf * half, 16), half), :]

        def slot(ref, k, hf):
            return ref.at[pl.ds(hf * hrows + k * half, half), :]

        halves = [(k, hf) for hf in (0, 1) for k in pieces]

        for j in (3, 1, 2, 0):
            for k, hf in halves:
                pltpu.make_async_remote_copy(
                    src_ref=part(k, (*rel_chips[j], 1 - c), hf), dst_ref=slot(rx1_ref.at[j], k, hf),
                    send_sem=sa.at[j], recv_sem=ra.at[j], device_id=sib, device_id_type=MESH).start()

        def wait_a(j):
            return pltpu.make_async_remote_copy(src_ref=rx1_ref.at[j], dst_ref=rx1_ref.at[j], send_sem=sa.at[j],
                                                recv_sem=ra.at[j], device_id=me, device_id_type=MESH)

        def ici(rel, src, dst, to):
            return pltpu.make_async_remote_copy(src_ref=src, dst_ref=dst, send_sem=sb.at[rel], recv_sem=rb.at[rel],
                                                device_id=to, device_id_type=MESH)

        first, second = pl.ds(0, hrows), pl.ds(hrows, hrows)
        sends = {
            X_RELAY: ici(X_RELAY, tx3.at[first, :], relx_ref, xn),
            Y_RELAY: ici(Y_RELAY, tx3.at[second, :], rely_ref, yn),
        }

        swap.wait_recv()
        chips[my_chip] = pair[0] + pair[1]
        for cp in small:
            cp.start()

        def chip_sum(j, dst):
            loads = [pltpu.make_async_copy(part(k, (*rel_chips[j], c), hf), slot(own_buf, k, hf), lsem.at[0])
                     for k, hf in halves]
            for cp in loads:
                cp.start()
            wait_a(j).wait_recv()
            got = pltpu.make_async_copy(rx1_ref.at[j], rx_buf, lsem.at[1])
            got.start()
            pltpu.make_async_copy(rx_buf, rx_buf, lsem.at[0]).wait()
            got.wait()

            def add(i, carry):
                rows = pl.ds(pl.multiple_of(i * half, 16), half)
                tot = own_buf[rows, :].astype(f32) + rx_buf[rows, :].astype(f32)
                dst[rows, :] = tot.astype(dst.dtype)
                return carry

            lax.fori_loop(0, nrows // half, add, 0)

        def add_landed(landed, dst, rows0, nrows_):
            got = pltpu.make_async_copy(landed, rx_buf.at[pl.ds(0, nrows_), :], lsem.at[1])
            got.start()
            got.wait()

            def add(i, carry):
                src_rows = pl.ds(pl.multiple_of(i * half, 16), half)
                dst_rows = pl.ds(pl.multiple_of(rows0 + i * half, 16), half)
                dst[dst_rows, :] = (dst[dst_rows, :].astype(f32) + rx_buf[src_rows, :].astype(f32)).astype(dst.dtype)
                return carry

            lax.fori_loop(0, nrows_ // half, add, 0)

        chip_sum(3, tx3)
        sends[X_RELAY].start()
        sends[Y_RELAY].start()
        chip_sum(1, tx1)
        chip_sum(2, tx2)
        chip_sum(0, acc)
        own_out = pltpu.make_async_copy(acc, own_ref, lsem.at[0])
        own_out.start()
        sends[X_RELAY].wait_recv()
        add_landed(relx_ref, tx2, 0, hrows)
        sends[Y_RELAY].wait_recv()
        add_landed(rely_ref, tx1, hrows, hrows)
        own_out.wait()
        outs = [pltpu.make_async_copy(tx1, forx_ref, lsem.at[0]), pltpu.make_async_copy(tx2, fory_ref, lsem.at[1])]
        for cp in outs:
            cp.start()
        for cp in outs:
            cp.wait()
        for cp in small:
            cp.wait_recv()
        tot = (chips[0] + chips[1]) + (chips[2] + chips[3])
        tot_ref[...] = tot
        loss = jnp.sum(tot[LOSS_ROW:LOSS_ROW + 1, :], axis=-1, keepdims=True)
        tot_ref[LOSS_ROW:LOSS_ROW + 1, :] = jnp.broadcast_to(loss, (1, 128))
        for j in range(4):
            wait_a(j).wait_send()
        for cp in sends.values():
            cp.wait_send()
        swap.wait_send()
        for cp in small:
            cp.wait_send()

    hbm = pl.BlockSpec(memory_space=pl.ANY)
    vm = pl.BlockSpec(memory_space=pltpu.VMEM)
    outs = pl.pallas_call(
        body, in_specs=[hbm, vm], out_specs=[hbm] * 6 + [vm],
        out_shape=(SDS((nrows, D), bf16), SDS((nrows, D), bf16), SDS((nrows, D), f32), SDS((4, nrows, D), bf16),
                   SDS((hrows, D), bf16), SDS((hrows, D), bf16), SDS((SMALL_ROWS, 128), f32)),
        scratch_shapes=[pltpu.VMEM((nrows, D), bf16), pltpu.VMEM((nrows, D), bf16),
                        pltpu.VMEM((nrows, D), bf16), pltpu.VMEM((nrows, D), bf16), pltpu.VMEM((nrows, D), bf16),
                        pltpu.VMEM((nrows, D), f32),
                        pltpu.SemaphoreType.DMA((4,)), pltpu.SemaphoreType.DMA((4,)),
                        pltpu.SemaphoreType.DMA((2,)), pltpu.SemaphoreType.DMA((2,)), pltpu.SemaphoreType.DMA((2,)),
                        pltpu.VMEM((2, SMALL_ROWS, 128), f32), pltpu.VMEM((4, SMALL_ROWS, 128), f32),
                        pltpu.SemaphoreType.DMA((4,)), pltpu.SemaphoreType.DMA((4,))],
        compiler_params=pltpu.CompilerParams(has_side_effects=True, vmem_limit_bytes=VMEM_LIMIT_V7X),
        name="reduce_scatter_ffn1_head")(dw1, small_packed)
    return outs[0], outs[1], outs[2], outs[-1]


def _rs1_tail_start(for_x, for_y, from_x, from_y, thru_a, thru_b):
    def body(fx_ref, fy_ref, lx_ref, ly_ref, ta_ref, tb_ref, ssem, rsem, fx_o, fy_o, lx_o, ly_o, ta_o, tb_o):
        x, y, c = _position()
        pltpu.make_async_remote_copy(src_ref=fx_ref, dst_ref=lx_ref, send_sem=ssem.at[0], recv_sem=rsem.at[0],
                                     device_id=(1 - x, y, c), device_id_type=MESH).start()
        pltpu.make_async_remote_copy(src_ref=fy_ref, dst_ref=ly_ref, send_sem=ssem.at[1], recv_sem=rsem.at[1],
                                     device_id=(x, 1 - y, c), device_id_type=MESH).start()

    dma = pltpu.SemaphoreType.DMA
    arrs = (for_x, for_y, from_x, from_y, thru_a, thru_b)
    return pl.pallas_call(
        body, name="rs1_tail_start", out_shape=(dma((2,)), dma((2,))) + tuple(_hbm_like(a) for a in arrs),
        in_specs=(HBM_SPEC,) * 6, out_specs=(SEM_SPEC,) * 2 + (HBM_SPEC,) * 6,
        input_output_aliases={0: 2, 1: 3, 2: 4, 3: 5, 4: 6, 5: 7},
        compiler_params=pltpu.CompilerParams(has_side_effects=SPLIT_EFFECT),
    )(*[_in_hbm(a) for a in arrs])


def _rs1_tail_wait(ssem, rsem, for_x, for_y, from_x, from_y, after):
    def body(fx_ref, fy_ref, lx_ref, ly_ref, ssem_ref, rsem_ref, after_ref, lx_o, ly_o):
        x, y, c = _position()
        for j, (src, dst) in enumerate(((fx_ref, lx_ref), (fy_ref, ly_ref))):
            d = pltpu.make_async_remote_copy(src_ref=src, dst_ref=dst, send_sem=ssem_ref.at[j], recv_sem=rsem_ref.at[j],
                                             device_id=(x, y, c), device_id_type=MESH)
            d.wait_recv()
            d.wait_send()

    return pl.pallas_call(
        body, name="rs1_tail_wait", out_shape=(_hbm_like(from_x), _hbm_like(from_y)),
        in_specs=(HBM_SPEC,) * 4 + (SEM_SPEC, SEM_SPEC, ANY_SPEC), out_specs=(HBM_SPEC, HBM_SPEC),
        input_output_aliases={2: 0, 3: 1},
        compiler_params=pltpu.CompilerParams(has_side_effects=SPLIT_EFFECT),
    )(for_x, for_y, from_x, from_y, ssem, rsem, after)


def _rs1_total(own, from_x, from_y):
    half = FS // 2
    npiece = len(G1_PIECES)

    def body(a_ref, x_ref, y_ref, o_ref):
        o_ref[...] = (a_ref[...] + x_ref[...].astype(f32)) + y_ref[...].astype(f32)

    blk = pl.BlockSpec((half, D), lambda i: (i, 0))
    return pl.pallas_call(
        body, grid=(2 * npiece,), in_specs=[blk, blk, blk],
        out_specs=pl.BlockSpec((half, D), lambda i: (2 * (i % npiece) + i // npiece, 0)),
        out_shape=SDS((npiece * FS, D), f32), name="rs1_total")(own, from_x, from_y)


RSA_PIECES = MIX_PIECES + F2_PIECES
RSA_ROWS = _group_rows(RSA_PIECES)
RSA_OFF = {k: PIECE_OFF[k] - PIECE_OFF[RSA_PIECES[0]] for k in RSA_PIECES}
RSA_BLOCK = 192


def _rsa_rows(ref, k):
    return ref.at[pl.ds(RSA_OFF[k], PIECE_ROWS[k]), :]


def _rsa_level1_start(dwint, dwout, dw2, rx1, thru):
    def body(di_ref, do_ref, d2_ref, rx1_ref, thru_ref, sa, ra, di_o, do_o, d2_o, rx1_o, thru_o):
        x, y, c = _position()
        srcs = _weight_pieces(wi_ref=di_ref, wo_ref=do_ref, w2_ref=d2_ref)
        for j, chip in enumerate([(x, y), (1 - x, y), (x, 1 - y), (1 - x, 1 - y)]):
            for k in RSA_PIECES:
                pltpu.make_async_remote_copy(
                    src_ref=_block_rows(srcs, k, (*chip, 1 - c)), dst_ref=_rsa_rows(rx1_ref.at[j], k),
                    send_sem=sa.at[j], recv_sem=ra.at[j], device_id=(x, y, 1 - c), device_id_type=MESH).start()

    dma = pltpu.SemaphoreType.DMA
    arrs = (dwint, dwout, dw2, rx1, thru)
    return pl.pallas_call(
        body, name="rsa_level1_start", out_shape=(dma((4,)), dma((4,))) + tuple(_hbm_like(a) for a in arrs),
        in_specs=(HBM_SPEC,) * 5, out_specs=(SEM_SPEC,) * 2 + (HBM_SPEC,) * 5,
        input_output_aliases={0: 2, 1: 3, 2: 4, 3: 5, 4: 6},
        compiler_params=pltpu.CompilerParams(has_side_effects=SPLIT_EFFECT),
    )(*[_in_hbm(a) for a in arrs])


def _rsa_level1_wait(sa, ra, dwint, dwout, dw2, rx1, after):
    def body(di_ref, do_ref, d2_ref, rx1_ref, sa_ref, ra_ref, after_ref, di_o, do_o, d2_o, rx1_o):
        x, y, c = _position()
        for j in range(4):
            d = pltpu.make_async_remote_copy(src_ref=rx1_ref.at[j], dst_ref=rx1_ref.at[j], send_sem=sa_ref.at[j],
                                             recv_sem=ra_ref.at[j], device_id=(x, y, c), device_id_type=MESH)
            d.wait_recv()
            d.wait_send()

    arrs = (dwint, dwout, dw2, rx1)
    return pl.pallas_call(
        body, name="rsa_level1_wait", out_shape=tuple(_hbm_like(a) for a in arrs),
        in_specs=(HBM_SPEC,) * 4 + (SEM_SPEC, SEM_SPEC, ANY_SPEC), out_specs=(HBM_SPEC,) * 4,
        input_output_aliases={0: 0, 1: 1, 2: 2, 3: 3},
        compiler_params=pltpu.CompilerParams(has_side_effects=SPLIT_EFFECT),
    )(*arrs, sa, ra, after)


def _rsa_chip_sums(dwint, dwout, dw2, rx1):
    nblk = RSA_ROWS // RSA_BLOCK

    def body(di_ref, do_ref, d2_ref, rx1_ref, tx_ref, acc_ref, own_buf, rx_buf, tx_buf, acc_buf, in_sems, out_sems):
        x, y, c = _position()
        srcs = _weight_pieces(wi_ref=di_ref, wo_ref=do_ref, w2_ref=d2_ref)
        chips = [(x, y), (1 - x, y), (x, 1 - y), (1 - x, 1 - y)]

        def start_loads(j):
            s = j % 2
            for k in RSA_PIECES:
                pltpu.make_async_copy(_block_rows(srcs, k, (*chips[j], c)), _rsa_rows(own_buf.at[s], k),
                                      in_sems.at[2 * s]).start()
            pltpu.make_async_copy(rx1_ref.at[j], rx_buf.at[s], in_sems.at[2 * s + 1]).start()

        def wait_loads(j):
            s = j % 2
            pltpu.make_async_copy(rx1_ref.at[j], own_buf.at[s], in_sems.at[2 * s]).wait()
            pltpu.make_async_copy(rx1_ref.at[j], rx_buf.at[s], in_sems.at[2 * s + 1]).wait()

        def store(j):
            if j == 0:
                return pltpu.make_async_copy(acc_buf, acc_ref, out_sems.at[2])
            return pltpu.make_async_copy(tx_buf.at[j % 2], tx_ref.at[j - 1], out_sems.at[j % 2])

        start_loads(0)
        for j in range(4):
            s = j % 2
            if j + 1 < 4:
                start_loads(j + 1)
            wait_loads(j)
            if j == 3:
                store(1).wait()

            def add(i, carry, j=j, s=s):
                rows = pl.ds(pl.multiple_of(i * RSA_BLOCK, 16), RSA_BLOCK)
                tot = own_buf[s, rows, :].astype(f32) + rx_buf[s, rows, :].astype(f32)
                if j == 0:
                    acc_buf[rows, :] = tot
                else:
                    tx_buf[s, rows, :] = tot.astype(bf16)
                return carry

            lax.fori_loop(0, nblk, add, 0)
            store(j).start()
        store(0).wait()
        store(2).wait()
        store(3).wait()

    return pl.pallas_call(
        body, in_specs=[ANY_SPEC] * 4, out_specs=[ANY_SPEC] * 2,
        out_shape=(SDS((3, RSA_ROWS, D), bf16), SDS((RSA_ROWS, D), f32)),
        scratch_shapes=[pltpu.VMEM((2, RSA_ROWS, D), bf16), pltpu.VMEM((2, RSA_ROWS, D), bf16),
                        pltpu.VMEM((2, RSA_ROWS, D), bf16), pltpu.VMEM((RSA_ROWS, D), f32),
                        pltpu.SemaphoreType.DMA((4,)), pltpu.SemaphoreType.DMA((3,))],
        compiler_params=_cparams(None, VMEM_LIMIT_V7X), name="rsa_chip_sums")(dwint, dwout, dw2, rx1)


def _rsa_level2_start(tx, rx2, thru):
    def body(tx_ref, rx2_ref, thru_ref, sb, rb, tx_o, rx2_o, thru_o):
        x, y, c = _position()
        for j, chip in enumerate([(1 - x, y), (x, 1 - y), (1 - x, 1 - y)]):
            pltpu.make_async_remote_copy(src_ref=tx_ref.at[j], dst_ref=rx2_ref.at[j], send_sem=sb.at[j],
                                         recv_sem=rb.at[j], device_id=(*chip, c), device_id_type=MESH).start()

    dma = pltpu.SemaphoreType.DMA
    arrs = (tx, rx2, thru)
    return pl.pallas_call(
        body, name="rsa_level2_start", out_shape=(dma((3,)), dma((3,))) + tuple(_hbm_like(a) for a in arrs),
        in_specs=(HBM_SPEC,) * 3, out_specs=(SEM_SPEC,) * 2 + (HBM_SPEC,) * 3,
        input_output_aliases={0: 2, 1: 3, 2: 4},
        compiler_params=pltpu.CompilerParams(has_side_effects=SPLIT_EFFECT),
    )(*[_in_hbm(a) for a in arrs])


def _rsa_level2_wait(sb, rb, tx, rx2, after):
    def body(tx_ref, rx2_ref, sb_ref, rb_ref, after_ref, rx2_o):
        x, y, c = _position()
        for j in range(3):
            d = pltpu.make_async_remote_copy(src_ref=tx_ref.at[j], dst_ref=rx2_ref.at[j], send_sem=sb_ref.at[j],
                                             recv_sem=rb_ref.at[j], device_id=(x, y, c), device_id_type=MESH)
            d.wait_recv()
            d.wait_send()

    return pl.pallas_call(
        body, name="rsa_level2_wait", out_shape=_hbm_like(rx2),
        in_specs=(HBM_SPEC, HBM_SPEC, SEM_SPEC, SEM_SPEC, ANY_SPEC), out_specs=HBM_SPEC,
        input_output_aliases={1: 0},
        compiler_params=pltpu.CompilerParams(has_side_effects=SPLIT_EFFECT),
    )(tx, rx2, sb, rb, after)


def _rsa_total(acc, rx2):
    def body(a_ref, r_ref, o_ref):
        o_ref[...] = ((a_ref[...] + r_ref[0].astype(f32)) + r_ref[1].astype(f32)) + r_ref[2].astype(f32)

    return pl.pallas_call(
        body, grid=(RSA_ROWS // RSA_BLOCK,),
        in_specs=[pl.BlockSpec((RSA_BLOCK, D), lambda i: (i, 0)), pl.BlockSpec((3, RSA_BLOCK, D), lambda i: (0, i, 0))],
        out_specs=pl.BlockSpec((RSA_BLOCK, D), lambda i: (i, 0)),
        out_shape=SDS((RSA_ROWS, D), f32), name="rsa_total")(acc, rx2)


def _adamw_math(w, g, m, v):
    m = ADAM_B1 * m + (1.0 - ADAM_B1) * g
    v = ADAM_B2 * v + (1.0 - ADAM_B2) * (g * g)
    m_hat = m / (1.0 - ADAM_B1 ** ADAM_STEP)
    v_hat = v / (1.0 - ADAM_B2 ** ADAM_STEP)
    delta = -ADAM_LR * (m_hat / (jnp.sqrt(v_hat) + ADAM_EPS) + ADAM_WD * w)
    return delta, m, v


def _adamw_big(pieces, ws, ms, vs, red, name):
    npiece = len(pieces)
    rmax = max(PIECE_ROWS[k] for k in pieces)

    def body(*refs):
        ins = (refs[0:npiece], refs[npiece:2 * npiece], refs[2 * npiece:3 * npiece])
        red_ref = refs[3 * npiece]
        out_refs = refs[3 * npiece + 1:7 * npiece + 1]
        inb, outb, in_sems, out_sems = refs[7 * npiece + 1:]

        def grad_rows(k):
            if k in G1_PIECES:
                return red_ref.at[pl.ds(PIECE_OFF[k], PIECE_ROWS[k]), :]
            return _rsa_rows(red_ref, k)

        def loads(i):
            s, k = i % 2, pieces[i]
            r = PIECE_ROWS[k]
            cps = [pltpu.make_async_copy(ins[q][i].at[0], inb.at[s, q, pl.ds(0, r), :], in_sems.at[4 * s + q])
                   for q in range(3)]
            cps.append(pltpu.make_async_copy(grad_rows(k), inb.at[s, 3, pl.ds(0, r), :], in_sems.at[4 * s + 3]))
            return cps

        def stores(i):
            s, r = i % 2, PIECE_ROWS[pieces[i]]
            return [pltpu.make_async_copy(outb.at[s, q, pl.ds(0, r), :], out_refs[q * npiece + i].at[0],
                                          out_sems.at[4 * s + q]) for q in range(4)]

        for cp in loads(0):
            cp.start()
        for i in range(npiece):
            s, r = i % 2, PIECE_ROWS[pieces[i]]
            if i + 1 < npiece:
                for cp in loads(i + 1):
                    cp.start()
            for cp in loads(i):
                cp.wait()
            if i >= 2:
                for cp in stores(i - 2):
                    cp.wait()
            g = inb[s, 3, 0:r, :]
            d, nm, nv = _adamw_math(inb[s, 0, 0:r, :], g, inb[s, 1, 0:r, :], inb[s, 2, 0:r, :])
            outb[s, 0, 0:r, :] = g
            outb[s, 1, 0:r, :] = d
            outb[s, 2, 0:r, :] = nm
            outb[s, 3, 0:r, :] = nv
            for cp in stores(i):
                cp.start()
        for i in range(max(npiece - 2, 0), npiece):
            for cp in stores(i):
                cp.wait()

    hbm = pl.BlockSpec(memory_space=pl.ANY)
    outs = pl.pallas_call(
        body, in_specs=[hbm] * (3 * npiece + 1), out_specs=[hbm] * (4 * npiece),
        out_shape=tuple(SDS(w.shape, f32) for _ in range(4) for w in ws),
        scratch_shapes=[pltpu.VMEM((2, 4, rmax, D), f32), pltpu.VMEM((2, 4, rmax, D), f32),
                        pltpu.SemaphoreType.DMA((8,)), pltpu.SemaphoreType.DMA((8,))],
        compiler_params=_cparams(None, VMEM_LIMIT_V7X), name=name)(*ws, *ms, *vs, red)
    return [list(outs[q * npiece:(q + 1) * npiece]) for q in range(4)]


def _adamw_small(ws, ms, vs, gs, name):
    n = len(ws)

    def body(*refs):
        w_refs, m_refs, v_refs, g_refs = refs[0:n], refs[n:2 * n], refs[2 * n:3 * n], refs[3 * n:4 * n]
        outs = refs[4 * n:]
        for i in range(n):
            d, nm, nv = _adamw_math(w_refs[i][...], g_refs[i][...], m_refs[i][...], v_refs[i][...])
            outs[i][...] = d
            outs[n + i][...] = nm
            outs[2 * n + i][...] = nv

    outs = pl.pallas_call(
        body, out_shape=tuple(SDS(w.shape, f32) for _ in range(3) for w in ws), name=name)(*ws, *ms, *vs, *gs)
    return [list(outs[q * n:(q + 1) * n]) for q in range(3)]


WEIGHTS = ("ffn1_norm", "ffn1_w_gate", "ffn1_w_up", "ffn1_w_down", "mix_norm", "w_in", "q_norm", "k_norm",
           "attn_sinks", "rel_bias", "pool_w", "pool_scale", "w_out", "ffn2_norm", "ffn2_w_gate", "ffn2_w_up",
           "ffn2_w_down")
BIG = (("ffn1_w_gate", True), ("ffn1_w_up", True), ("ffn1_w_down", False), ("w_in", True), ("w_out", False),
       ("ffn2_w_gate", True), ("ffn2_w_up", True), ("ffn2_w_down", False))


def kernel(x, ffn1_norm, ffn1_w_gate, ffn1_w_up, ffn1_w_down, mix_norm, w_in, q_norm, k_norm, attn_sinks, rel_bias, pool_w, pool_scale, w_out, ffn2_norm, ffn2_w_gate, ffn2_w_up, ffn2_w_down, loss_target, m_ffn1_norm, m_ffn1_w_gate, m_ffn1_w_up, m_ffn1_w_down, m_mix_norm, m_w_in, m_q_norm, m_k_norm, m_attn_sinks, m_rel_bias, m_pool_w, m_pool_scale, m_w_out, m_ffn2_norm, m_ffn2_w_gate, m_ffn2_w_up, m_ffn2_w_down, v_ffn1_norm, v_ffn1_w_gate, v_ffn1_w_up, v_ffn1_w_down, v_mix_norm, v_w_in, v_q_norm, v_k_norm, v_attn_sinks, v_rel_bias, v_pool_w, v_pool_scale, v_w_out, v_ffn2_norm, v_ffn2_w_gate, v_ffn2_w_up, v_ffn2_w_down):
    args = dict(locals())
    w = {n: args[n] for n in WEIGHTS}
    m = {n: args["m_" + n] for n in WEIGHTS}
    v = {n: args["v_" + n] for n in WEIGHTS}

    as_rows = lambda a, tr: jnp.swapaxes(a, 1, 2) if tr else a
    shard = jnp.concatenate([as_rows(w[n], tr)[0].astype(bf16) for n, tr in BIG], axis=0)
    exchanges = _GatheredWeights(shard, x[0], ffn1_norm)
    (dh1, dx1), (dw1, _, _, _), small = _local_step(
        x[0], loss_target[0], exchanges, ffn1_norm, mix_norm, ffn2_norm, q_norm, k_norm, attn_sinks,
        rel_bias, pool_w[0], pool_scale)

    nrows1 = len(G1_PIECES) * FS
    for_x, for_y, own1, small_tot = _reduce_scatter_ffn1_head(dw1, _pack_small(small))
    ssem, rsem, for_x, for_y, from_x, from_y, g1, small_tot = _rs1_tail_start(
        for_x, for_y, lax.empty((nrows1, D), bf16), lax.empty((nrows1, D), bf16), ffn1_norm, small_tot)
    gx, _, _ = _norm_bwd(dh1, x[0], g1, dx1, 1.0, "norm1_bwd")
    red_rest = exchanges.mix_ffn2_grads_total(gx)

    grads, deltas, new_m, new_v = {}, {}, {}, {}
    rest = [k for k in range(len(BIG)) if k not in G1_PIECES]
    rows_of = lambda t, ks: [as_rows(t[BIG[k][0]], BIG[k][1]) for k in ks]
    rest_out = _adamw_big(rest, rows_of(w, rest), rows_of(m, rest), rows_of(v, rest), red_rest, "adamw_rest")
    from_x, from_y = _rs1_tail_wait(ssem, rsem, for_x, for_y, from_x, from_y, rest_out[0][0])
    red1 = _rs1_total(own1, from_x, from_y)
    ffn1 = list(G1_PIECES)
    ffn1_out = _adamw_big(ffn1, rows_of(w, ffn1), rows_of(m, ffn1), rows_of(v, ffn1), red1, "adamw_ffn1")
    for ks, out in ((rest, rest_out), (ffn1, ffn1_out)):
        for i, k in enumerate(ks):
            n, tr = BIG[k]
            grads[n], deltas[n], new_m[n], new_v[n] = [as_rows(o[i], tr) for o in out]
    small_names = [n for n in SMALL_NAMES if n != "loss"]
    for n in small_names:
        grads[n] = _unpack_small(small_tot, n)
    ds, nms, nvs = _adamw_small([w[n] for n in small_names], [m[n] for n in small_names], [v[n] for n in small_names],
                                [grads[n] for n in small_names], "adamw_small")
    for i, n in enumerate(small_names):
        deltas[n], new_m[n], new_v[n] = ds[i], nms[i], nvs[i]
    loss = small_tot[LOSS_ROW, 0]
    return (loss, gx[None], *[grads[n] for n in WEIGHTS], *[deltas[n] for n in WEIGHTS],
            *[new_m[n] for n in WEIGHTS], *[new_v[n] for n in WEIGHTS])
```

```python
import functools

import jax
import jax.numpy as jnp
import numpy as np
from jax import lax
from jax.experimental import pallas as pl
from jax.experimental.pallas import tpu as pltpu

f32, bf16, i32 = jnp.float32, jnp.bfloat16, jnp.int32
SDS = jax.ShapeDtypeStruct

D = 1024
F = 2816
HD = 64
NH = 8
NKV = 2
GQA = NH // NKV
DATTN = NH * HD
DKV = NKV * HD
DPOOL = 512
POOL_WINDOWS = (2, 4, 8, 16)
PGD = DPOOL // len(POOL_WINDOWS)
DIN = DATTN + 2 * DKV + DPOOL
DMIX = DATTN + DPOOL
BLK = 128
NBUCK = 32
MAX_DISTANCE = 128
EPS = 1e-6
NEG = -1e30
SCALE = HD ** -0.5

ADAM_LR, ADAM_B1, ADAM_B2, ADAM_EPS, ADAM_WD, ADAM_STEP = 0.001, 0.9, 0.999, 1e-08, 0.01, 10

NDEV = 8
FS = F // NDEV
INS = DIN // NDEV
OUTS = DMIX // NDEV
PIECE_ROWS = (FS, FS, FS, INS, OUTS, FS, FS, FS)
PIECE_OFF = tuple(int(v) for v in np.cumsum((0,) + PIECE_ROWS[:-1]))
PACK_ROWS = sum(PIECE_ROWS)

VMEM_LIMIT_V7X = 56 * 1024 * 1024

MESH = pl.DeviceIdType.MESH


def _cparams(sem=None, vmem=None):
    return pltpu.CompilerParams(dimension_semantics=sem, vmem_limit_bytes=vmem)


def _nt(a, b):
    return lax.dot_general(a, b, (((1,), (1,)), ((), ())), preferred_element_type=f32)


def _tn(a, b):
    return lax.dot_general(a, b, (((0,), (0,)), ((), ())), preferred_element_type=f32)


def _nn(a, b):
    return jnp.dot(a, b, preferred_element_type=f32)


def _sigmoid(x):
    return 1.0 / (1.0 + jnp.exp(-x))


def _norm_fwd(x, g, name):
    T = x.shape[0]
    tm = min(512, T)

    def body(x_ref, g_ref, h_ref):
        xv = x_ref[...]
        r = lax.rsqrt(jnp.mean(xv * xv, axis=-1, keepdims=True) + EPS)
        h_ref[...] = (xv * r * g_ref[...]).astype(bf16)

    return pl.pallas_call(
        body, grid=(T // tm,),
        in_specs=[pl.BlockSpec((tm, D), lambda i: (i, 0)), pl.BlockSpec((1, D), lambda i: (0, 0))],
        out_specs=pl.BlockSpec((tm, D), lambda i: (i, 0)),
        out_shape=SDS((T, D), bf16), name=name)(x, g)


def _norm_bwd(dh, x, g, dres, out_scale, name):
    T = x.shape[0]
    tm = min(512, T)

    def body(dh_ref, x_ref, g_ref, dr_ref, dx_ref, dxb_ref, dg_ref):
        i = pl.program_id(0)
        xv = x_ref[...]
        r = lax.rsqrt(jnp.mean(xv * xv, axis=-1, keepdims=True) + EPS)
        xh = xv * r
        dhv = dh_ref[...]
        dxh = dhv * g_ref[...]
        dx = dr_ref[...] + r * (dxh - xh * jnp.mean(dxh * xh, axis=-1, keepdims=True))
        dx_ref[...] = dx
        dxb_ref[...] = (out_scale * dx).astype(bf16)
        dg = jnp.sum(dhv * xh, axis=0, keepdims=True)

        @pl.when(i == 0)
        def _():
            dg_ref[...] = dg

        @pl.when(i > 0)
        def _():
            dg_ref[...] += dg

    tok = pl.BlockSpec((tm, D), lambda i: (i, 0))
    vec = pl.BlockSpec((1, D), lambda i: (0, 0))
    return pl.pallas_call(
        body, grid=(T // tm,),
        in_specs=[tok, tok, vec, tok], out_specs=[tok, tok, vec],
        out_shape=(SDS((T, D), f32), SDS((T, D), bf16), SDS((1, D), f32)),
        compiler_params=_cparams(("arbitrary",)), name=name)(dh, x, g, dres)


def _gain_grad(dh, x, name):
    T = x.shape[0]
    tm = min(512, T)

    def body(dh_ref, x_ref, dg_ref):
        i = pl.program_id(0)
        xv = x_ref[...]
        r = lax.rsqrt(jnp.mean(xv * xv, axis=-1, keepdims=True) + EPS)
        dg = jnp.sum(dh_ref[...] * (xv * r), axis=0, keepdims=True)

        @pl.when(i == 0)
        def _():
            dg_ref[...] = dg

        @pl.when(i > 0)
        def _():
            dg_ref[...] += dg

    tok = pl.BlockSpec((tm, D), lambda i: (i, 0))
    return pl.pallas_call(
        body, grid=(T // tm,), in_specs=[tok, tok], out_specs=pl.BlockSpec((1, D), lambda i: (0, 0)),
        out_shape=SDS((1, D), f32), compiler_params=_cparams(("arbitrary",)), name=name)(dh, x)


FFN_ROW_CHUNK = 256


def _ffn_tiles(T):
    return min(1024, T), 256


def _ffn_fwd(h, w, x, target, name):
    T = h.shape[0]
    tm, tf = _ffn_tiles(T)
    nf = F // tf
    with_loss = target is not None

    def body(*refs):
        if with_loss:
            h_ref, w_ref, x_hbm, t_hbm, xo_ref, g_ref, u_ref, dyb_ref, loss_ref, tbuf, sem = refs
        else:
            h_ref, w_ref, x_hbm, xo_ref, g_ref, u_ref, sem = refs
        fi = pl.program_id(0)

        @pl.when(fi == 0)
        def _():
            cp = pltpu.make_async_copy(x_hbm, xo_ref, sem)
            cp.start()
            cp.wait()

        wgu = w_ref[0:2].reshape(2 * tf, D)
        for r in range(0, T, tm):
            rows = slice(r, r + tm)
            gu = _nt(h_ref[rows, :], wgu)
            gate, up = gu[:, :tf], gu[:, tf:]
            act = gate * _sigmoid(gate) * up
            g_ref[0, rows, :] = gate.astype(bf16)
            u_ref[0, rows, :] = up.astype(bf16)
            xo_ref[rows, :] += _nn((0.5 * act).astype(bf16), w_ref[2])

        if with_loss:
            @pl.when(fi == nf - 1)
            def _():
                lanes = jnp.zeros((1, 128), f32)
                for r in range(0, T, tm):
                    rows = slice(r, r + tm)
                    cp = pltpu.make_async_copy(t_hbm.at[pl.ds(r, tm), :], tbuf, sem)
                    cp.start()
                    cp.wait()
                    e = xo_ref[rows, :] - tbuf[...]
                    dy = e * (1.0 / D)
                    xo_ref[rows, :] = dy
                    dyb_ref[rows, :] = (0.5 * dy).astype(bf16)
                    col = jnp.sum(e * e, axis=0, keepdims=True) * (0.5 / D)
                    for k in range(D // 128):
                        lanes = lanes + col[:, 128 * k:128 * (k + 1)]
                loss_ref[...] = lanes

    tok = pl.BlockSpec((T, D), lambda f: (0, 0))
    act_spec = pl.BlockSpec((1, T, tf), lambda f: (f, 0, 0))
    hbm = pl.BlockSpec(memory_space=pl.ANY)
    in_specs = [tok, pl.BlockSpec((3, tf, D), lambda f: (0, f, 0)), hbm]
    out_specs = [tok, act_spec, act_spec]
    out_shape = [SDS((T, D), f32), SDS((nf, T, tf), bf16), SDS((nf, T, tf), bf16)]
    scratch = [pltpu.SemaphoreType.DMA]
    args = [h, w, x]
    if with_loss:
        in_specs.append(hbm)
        args.append(target)
        out_specs += [tok, pl.BlockSpec((1, 128), lambda f: (0, 0))]
        out_shape += [SDS((T, D), bf16), SDS((1, 128), f32)]
        scratch = [pltpu.VMEM((tm, D), f32)] + scratch
    return pl.pallas_call(
        body, grid=(nf,), in_specs=in_specs, out_specs=out_specs, out_shape=tuple(out_shape), scratch_shapes=scratch,
        compiler_params=_cparams(("arbitrary",), VMEM_LIMIT_V7X), name=name)(*args)


def _ffn_bwd(dob, h, gate, up, w, name):
    T = h.shape[0]
    _, tf = _ffn_tiles(T)
    nf = F // tf

    def body(do_hbm, h_hbm, g_ref, u_ref, w_ref, dh_hbm, dw_ref, do_v, h_v, dh_acc, dgu_s, act_s, sems):
        fi = pl.program_id(0)

        @pl.when(fi == 0)
        def _():
            loads = [pltpu.make_async_copy(do_hbm, do_v, sems.at[0]), pltpu.make_async_copy(h_hbm, h_v, sems.at[1])]
            for cp in loads:
                cp.start()
            dh_acc[...] = jnp.zeros_like(dh_acc)
            for cp in loads:
                cp.wait()

        wgu = w_ref[0:2].reshape(2 * tf, D)
        for r in range(0, T, FFN_ROW_CHUNK):
            rows = slice(r, r + FFN_ROW_CHUNK)
            dov = do_v[rows, :]
            gv = g_ref[0, rows, :].astype(f32)
            uv = u_ref[0, rows, :].astype(f32)
            sg = _sigmoid(gv)
            sil = gv * sg
            dact = _nt(dov, w_ref[2])
            dup = dact * sil
            dgate = dact * uv * (sg * (1.0 + gv * (1.0 - sg)))
            dgu = jnp.concatenate([dgate.astype(bf16), dup.astype(bf16)], axis=1)
            dgu_s[rows, :] = dgu
            act_s[rows, :] = (sil * uv).astype(bf16)
            dh_acc[rows, :] += _nn(dgu, wgu)
        dw_ref[0:2] = _tn(dgu_s[...], h_v[...]).reshape(2, tf, D).astype(bf16)
        dw_ref[2] = _tn(act_s[...], do_v[...]).astype(bf16)

        @pl.when(fi == nf - 1)
        def _():
            out = pltpu.make_async_copy(dh_acc, dh_hbm, sems.at[0])
            out.start()
            out.wait()

    act_spec = pl.BlockSpec((1, T, tf), lambda f: (f, 0, 0))
    wspec = pl.BlockSpec((3, tf, D), lambda f: (0, f, 0))
    hbm = pl.BlockSpec(memory_space=pl.ANY)
    return pl.pallas_call(
        body, grid=(nf,),
        in_specs=[hbm, hbm, act_spec, act_spec, wspec],
        out_specs=[hbm, wspec],
        out_shape=(SDS((T, D), f32), SDS((3, F, D), bf16)),
        scratch_shapes=[pltpu.VMEM((T, D), bf16), pltpu.VMEM((T, D), bf16), pltpu.VMEM((T, D), f32),
                        pltpu.VMEM((T, 2 * tf), bf16), pltpu.VMEM((T, tf), bf16), pltpu.SemaphoreType.DMA((2,))],
        compiler_params=_cparams(("arbitrary",), VMEM_LIMIT_V7X), name=name)(dob, h, gate, up, w)


def _in_proj_fwd(h, wint, name):
    T = h.shape[0]
    tm = min(512, T)

    def body(h_ref, w_ref, z_ref):
        z_ref[...] = _nt(h_ref[...], w_ref[...])

    return pl.pallas_call(
        body, grid=(T // tm,),
        in_specs=[pl.BlockSpec((tm, D), lambda i: (i, 0)), pl.BlockSpec((DIN, D), lambda i: (0, 0))],
        out_specs=pl.BlockSpec((tm, DIN), lambda i: (i, 0)),
        out_shape=SDS((T, DIN), f32), name=name)(h, wint)


def _in_proj_bwd(dz, wint, h, name):
    T = h.shape[0]
    tm = min(512, T)
    nt = T // tm

    def body(dz_ref, w_ref, h_ref, dh_ref, dw_ref, acc):
        i = pl.program_id(0)
        dzb = dz_ref[...].astype(bf16)
        dh_ref[...] = _nn(dzb, w_ref[...])
        part = _tn(dzb, h_ref[...])

        @pl.when(i == 0)
        def _():
            acc[...] = part

        @pl.when(i > 0)
        def _():
            acc[...] += part

        @pl.when(i == nt - 1)
        def _():
            dw_ref[...] = acc[...].astype(bf16)

    wspec = pl.BlockSpec((DIN, D), lambda i: (0, 0))
    return pl.pallas_call(
        body, grid=(nt,),
        in_specs=[pl.BlockSpec((tm, DIN), lambda i: (i, 0)), wspec, pl.BlockSpec((tm, D), lambda i: (i, 0))],
        out_specs=[pl.BlockSpec((tm, D), lambda i: (i, 0)), wspec],
        out_shape=(SDS((T, D), f32), SDS((DIN, D), bf16)),
        scratch_shapes=[pltpu.VMEM((DIN, D), f32)],
        compiler_params=_cparams(("arbitrary",)), name=name)(dz, wint, h)


def _out_proj_fwd(ymix, wout, x, g, name):
    T = x.shape[0]
    tm = min(512, T)

    def body(y_ref, w_ref, x_ref, g_ref, o_ref, h_ref):
        o = x_ref[...] + _nn(y_ref[...], w_ref[...])
        o_ref[...] = o
        r = lax.rsqrt(jnp.mean(o * o, axis=-1, keepdims=True) + EPS)
        h_ref[...] = (o * r * g_ref[...]).astype(bf16)

    tok = pl.BlockSpec((tm, D), lambda i: (i, 0))
    return pl.pallas_call(
        body, grid=(T // tm,),
        in_specs=[pl.BlockSpec((tm, DMIX), lambda i: (i, 0)), pl.BlockSpec((DMIX, D), lambda i: (0, 0)), tok,
                  pl.BlockSpec((1, D), lambda i: (0, 0))],
        out_specs=[tok, tok], out_shape=(SDS((T, D), f32), SDS((T, D), bf16)), name=name)(ymix, wout, x, g)


def _out_proj_bwd(dxb, wout, ymix, name):
    T = dxb.shape[0]
    tm = min(512, T)
    nt = T // tm

    def body(dx_ref, w_ref, y_ref, dy_ref, dw_ref, acc):
        i = pl.program_id(0)
        dxv = dx_ref[...]
        dy_ref[...] = _nt(dxv, w_ref[...])
        part = _tn(y_ref[...], dxv)

        @pl.when(i == 0)
        def _():
            acc[...] = part

        @pl.when(i > 0)
        def _():
            acc[...] += part

        @pl.when(i == nt - 1)
        def _():
            dw_ref[...] = acc[...].astype(bf16)

    wspec = pl.BlockSpec((DMIX, D), lambda i: (0, 0))
    return pl.pallas_call(
        body, grid=(nt,),
        in_specs=[pl.BlockSpec((tm, D), lambda i: (i, 0)), wspec, pl.BlockSpec((tm, DMIX), lambda i: (i, 0))],
        out_specs=[pl.BlockSpec((tm, DMIX), lambda i: (i, 0)), wspec],
        out_shape=(SDS((T, DMIX), f32), SDS((DMIX, D), bf16)),
        scratch_shapes=[pltpu.VMEM((DMIX, D), f32)],
        compiler_params=_cparams(("arbitrary",)), name=name)(dxb, wout, ymix)


def _t5_bucket_table():
    ql = np.arange(BLK)[:, None]
    kl = np.arange(2 * BLK)[None, :]
    n = np.maximum(ql + BLK - kl, 0)
    max_exact = NBUCK // 2
    large = max_exact + (np.log(np.maximum(n, 1) / max_exact) / np.log(MAX_DISTANCE / max_exact)
                         * (NBUCK - max_exact)).astype(np.int32)
    large = np.minimum(large, NBUCK - 1)
    return np.where(n < max_exact, n, large).astype(np.int32)


def _fill_bias(bk_ref, rb_ref, bias_scr):
    bk = bk_ref[...]
    for h in range(NH):
        def step(b, acc, h=h):
            return acc + jnp.where(bk == b, rb_ref[b, h], 0.0)
        bias_scr[h] = lax.fori_loop(0, NBUCK, step, jnp.zeros((BLK, 2 * BLK), f32))


MIX_SUB = 4


class _Window:
    def __init__(self, zc_ref, zp_ref, n, s):
        self.blk = n * MIX_SUB + s
        self.first_in_step = s == 0
        self.cur = lambda a, b: zc_ref[s * BLK:(s + 1) * BLK, a:b]
        self.prev = (lambda a, b: zp_ref[:, a:b]) if s == 0 else (lambda a, b: zc_ref[(s - 1) * BLK:s * BLK, a:b])


def _attn_qkv(win, kh, qg, kg):
    kc = DATTN + HD * kh
    vc = DATTN + DKV + HD * kh
    kx = jnp.concatenate([win.prev(kc, kc + HD), win.cur(kc, kc + HD)], axis=0)
    vx = jnp.concatenate([win.prev(vc, vc + HD), win.cur(vc, vc + HD)], axis=0)
    qx = jnp.concatenate([win.cur(HD * (GQA * kh + g), HD * (GQA * kh + g + 1)) for g in range(GQA)], axis=0)
    rq = lax.rsqrt(jnp.mean(qx * qx, axis=-1, keepdims=True) + EPS)
    rk = lax.rsqrt(jnp.mean(kx * kx, axis=-1, keepdims=True) + EPS)
    qhat, khat = qx * rq, kx * rk
    return dict(qhat=qhat, khat=khat, rq=rq, rk=rk, qsb=(qhat * (qg * SCALE)).astype(bf16),
                knb=(khat * kg).astype(bf16), vb=vx.astype(bf16))


def _window_masks(n):
    row = lax.broadcasted_iota(i32, (GQA * BLK, 2 * BLK), 0) & (BLK - 1)
    col = lax.broadcasted_iota(i32, (GQA * BLK, 2 * BLK), 1)
    band = (col > row) & (col <= row + BLK)
    return band & ((col >= BLK) | (n > 0)), band


def _attn_probs(a, kh, sk_ref, bias_scr, mask):
    s = _nt(a["qsb"], a["knb"]) + bias_scr[GQA * kh:GQA * (kh + 1)].reshape(GQA * BLK, 2 * BLK)
    s = jnp.where(mask, s, NEG)
    ridx = lax.broadcasted_iota(i32, (GQA * BLK, 1), 0)
    sink = jnp.full((GQA * BLK, 1), sk_ref[GQA * kh + GQA - 1], f32)
    for g in range(GQA - 2, -1, -1):
        sink = jnp.where(ridx < (g + 1) * BLK, sk_ref[GQA * kh + g], sink)
    m = jnp.maximum(jnp.max(s, axis=-1, keepdims=True), sink)
    e = jnp.exp(s - m)
    den = jnp.sum(e, axis=-1, keepdims=True) + jnp.exp(sink - m)
    return e / den


POOL_STEPS = {2: (1,), 4: (1, 2), 8: (1, 2, 4), 16: (1, 2, 4, 8)}


def _pool_group(win, g, w):
    n = win.blk
    c0 = DATTN + 2 * DKV + PGD * g
    uc = win.cur(c0, c0 + PGD)
    up = jnp.where(n > 0, win.prev(c0, c0 + PGD), 0.0)
    sm = jnp.concatenate([up, uc], axis=0)
    for k in POOL_STEPS[w]:
        sm = sm + pltpu.roll(sm, k, axis=0)
    pos = n * BLK + lax.broadcasted_iota(i32, (BLK, 1), 0) + 1
    cnt = jnp.minimum(pos, w).astype(f32)
    return sm[BLK:2 * BLK] / cnt - uc, cnt


def _mix_fwd(z, qg, kg, sinks, relb, bucket, pool_w, pscale, name):
    T = z.shape[0]
    step_rows = MIX_SUB * BLK
    nsteps = T // step_rows

    def body(zc_ref, zp_ref, qg_ref, kg_ref, sk_ref, rb_ref, bk_ref, pw_ref, ps_ref, y_ref, p_ref, bias_scr, yacc):
        n = pl.program_id(0)

        @pl.when(n == 0)
        def _():
            _fill_bias(bk_ref, rb_ref, bias_scr)

        first_mask, mask = _window_masks(n)
        for s in range(MIX_SUB):
            win = _Window(zc_ref, zp_ref, n, s)
            rows = slice(s * BLK, (s + 1) * BLK)
            for kh in range(NKV):
                a = _attn_qkv(win, kh, qg_ref[...], kg_ref[...])
                pb = _attn_probs(a, kh, sk_ref, bias_scr, first_mask if s == 0 else mask).astype(bf16)
                p_ref[s, GQA * kh:GQA * (kh + 1)] = pb.reshape(GQA, BLK, 2 * BLK)
                o = _nn(pb, a["vb"])
                for g in range(GQA):
                    hc = HD * (GQA * kh + g)
                    yacc[rows, hc:hc + HD] = o[g * BLK:(g + 1) * BLK]
            for g, w in enumerate(POOL_WINDOWS):
                pooled, _ = _pool_group(win, g, w)
                yp = _nn(pooled.astype(bf16), pw_ref[g].astype(bf16)) * ps_ref[:, PGD * g:PGD * (g + 1)]
                yacc[rows, DATTN + PGD * g:DATTN + PGD * (g + 1)] = yp
        y_ref[...] = yacc[...].astype(bf16)

    full = lambda *shape: pl.BlockSpec(shape, lambda n: (0,) * len(shape))
    smem = pl.BlockSpec(memory_space=pltpu.SMEM)
    return pl.pallas_call(
        body, grid=(nsteps,),
        in_specs=[pl.BlockSpec((step_rows, DIN), lambda n: (n, 0)),
                  pl.BlockSpec((BLK, DIN), lambda n: (jnp.maximum(n * MIX_SUB - 1, 0), 0)),
                  full(1, HD), full(1, HD), smem, smem, full(BLK, 2 * BLK),
                  full(len(POOL_WINDOWS), PGD, PGD), full(1, DPOOL)],
        out_specs=[pl.BlockSpec((step_rows, DMIX), lambda n: (n, 0)),
                   pl.BlockSpec((MIX_SUB, NH, BLK, 2 * BLK), lambda n: (n, 0, 0, 0))],
        out_shape=(SDS((T, DMIX), bf16), SDS((T // BLK, NH, BLK, 2 * BLK), bf16)),
        scratch_shapes=[pltpu.VMEM((NH, BLK, 2 * BLK), f32), pltpu.VMEM((step_rows, DMIX), f32)],
        compiler_params=_cparams(("arbitrary",)), name=name)(z, z, qg, kg, sinks, relb, bucket, pool_w, pscale)


def _mix_bwd(z, dy, probs, qg, kg, relb, bucket, pool_w, pscale, name):
    T = z.shape[0]
    step_rows = MIX_SUB * BLK
    nsteps = T // step_rows

    def body(zc_ref, zp_ref, dy_ref, p_ref, qg_ref, kg_ref, bk_ref, pw_ref, ps_ref,
             dz_ref, dqg_ref, dkg_ref, dsk_ref, drb_ref, dpw_ref, dps_ref, dbias_scr):
        n = pl.program_id(0)

        @pl.when(n == 0)
        def _():
            dbias_scr[...] = jnp.zeros_like(dbias_scr)
            dqg_ref[...] = jnp.zeros_like(dqg_ref)
            dkg_ref[...] = jnp.zeros_like(dkg_ref)
            dpw_ref[...] = jnp.zeros_like(dpw_ref)
            dps_ref[...] = jnp.zeros_like(dps_ref)

        qg, kg = qg_ref[...], kg_ref[...]
        for s in range(MIX_SUB):
            win = _Window(zc_ref, zp_ref, n, s)
            blk = win.blk
            rows = pl.ds(pl.multiple_of(blk * BLK, BLK), BLK)
            prow = pl.ds(pl.multiple_of(jnp.maximum(blk - 1, 0) * BLK, BLK), BLK)
            dyr = slice(s * BLK, (s + 1) * BLK)

            def into_prev(fn, s=s):
                if s == 0:
                    pl.when(n > 0)(fn)
                else:
                    fn()

            for kh in range(NKV):
                a = _attn_qkv(win, kh, qg, kg)
                pb = p_ref[s, GQA * kh:GQA * (kh + 1)].reshape(GQA * BLK, 2 * BLK)
                p = pb.astype(f32)
                do = jnp.concatenate([dy_ref[dyr, HD * (GQA * kh + g):HD * (GQA * kh + g + 1)] for g in range(GQA)],
                                     axis=0).astype(bf16)
                dv = _tn(pb, do)
                dp = _nt(do, a["vb"])
                delta = jnp.sum(p * dp, axis=-1, keepdims=True)
                ds = p * (dp - delta)
                for g in range(GQA):
                    dbias_scr[GQA * kh + g] += ds[g * BLK:(g + 1) * BLK]
                dsb = ds.astype(bf16)
                dqn = _nn(dsb, a["knb"]) * SCALE
                dkn = _tn(dsb, a["qsb"])
                qhat, khat = a["qhat"], a["khat"]
                dqg_ref[...] += jnp.sum(dqn * qhat, axis=0, keepdims=True)
                dkg_ref[...] += jnp.sum(dkn * khat, axis=0, keepdims=True)
                dqh = dqn * qg
                dq = a["rq"] * (dqh - qhat * jnp.mean(dqh * qhat, axis=-1, keepdims=True))
                dkh = dkn * kg
                dk = a["rk"] * (dkh - khat * jnp.mean(dkh * khat, axis=-1, keepdims=True))
                kc = DATTN + HD * kh
                vc = DATTN + DKV + HD * kh
                for g in range(GQA):
                    hc = HD * (GQA * kh + g)
                    dz_ref[rows, hc:hc + HD] = dq[g * BLK:(g + 1) * BLK]
                dz_ref[rows, kc:kc + HD] = dk[BLK:2 * BLK]
                dz_ref[rows, vc:vc + HD] = dv[BLK:2 * BLK]

                def kv_prev(dk=dk, dv=dv, kc=kc, vc=vc, prow=prow):
                    dz_ref[prow, kc:kc + HD] += dk[0:BLK]
                    dz_ref[prow, vc:vc + HD] += dv[0:BLK]

                into_prev(kv_prev)

            for g, w in enumerate(POOL_WINDOWS):
                c0 = DATTN + 2 * DKV + PGD * g
                pooled, cnt = _pool_group(win, g, w)
                pb = pooled.astype(bf16)
                wb = pw_ref[g].astype(bf16)
                dyp = dy_ref[dyr, DATTN + PGD * g:DATTN + PGD * (g + 1)]
                ypre = _nn(pb, wb)
                dps_ref[:, PGD * g:PGD * (g + 1)] += jnp.sum(dyp * ypre, axis=0, keepdims=True)
                dyg = (dyp * ps_ref[:, PGD * g:PGD * (g + 1)]).astype(bf16)
                dpw_ref[g] += _tn(pb, dyg)
                dpooled = _nt(dyg, wb)
                due = jnp.concatenate([jnp.zeros((BLK, PGD), f32), dpooled / cnt], axis=0)
                for k in POOL_STEPS[w]:
                    due = due + pltpu.roll(due, 2 * BLK - k, axis=0)
                dz_ref[rows, c0:c0 + PGD] = due[BLK:2 * BLK] - dpooled

                def pool_prev(due=due, c0=c0, prow=prow):
                    dz_ref[prow, c0:c0 + PGD] += due[0:BLK]

                into_prev(pool_prev)

        @pl.when(n == nsteps - 1)
        def _():
            bk = bk_ref[...]
            ri = lax.broadcasted_iota(i32, (NBUCK, NH), 0)
            ci = lax.broadcasted_iota(i32, (NBUCK, NH), 1)

            def step(b, acc):
                for h in range(NH):
                    sel = jnp.where(bk == b, dbias_scr[h], 0.0)
                    tot = jnp.sum(jnp.sum(sel, axis=1, keepdims=True), axis=0, keepdims=True)
                    acc = acc + jnp.where((ri == b) & (ci == h), tot, 0.0)
                return acc

            drb_ref[...] = lax.fori_loop(0, NBUCK, step, jnp.zeros((NBUCK, NH), f32))
            lane = lax.broadcasted_iota(i32, (1, 128), 1)
            dsk = jnp.zeros((1, 128), f32)
            for h in range(NH):
                tot = jnp.sum(jnp.sum(dbias_scr[h], axis=1, keepdims=True), axis=0, keepdims=True)
                dsk = dsk - jnp.where(lane == h, tot, 0.0)
            dsk_ref[...] = dsk

    full = lambda *shape: pl.BlockSpec(shape, lambda n: (0,) * len(shape))
    npg = len(POOL_WINDOWS)
    return pl.pallas_call(
        body, grid=(nsteps,),
        in_specs=[pl.BlockSpec((step_rows, DIN), lambda n: (n, 0)),
                  pl.BlockSpec((BLK, DIN), lambda n: (jnp.maximum(n * MIX_SUB - 1, 0), 0)),
                  pl.BlockSpec((step_rows, DMIX), lambda n: (n, 0)),
                  pl.BlockSpec((MIX_SUB, NH, BLK, 2 * BLK), lambda n: (n, 0, 0, 0)),
                  full(1, HD), full(1, HD), full(BLK, 2 * BLK), full(npg, PGD, PGD), full(1, DPOOL)],
        out_specs=[full(T, DIN), full(1, HD), full(1, HD), full(1, 128), full(NBUCK, NH),
                   full(npg, PGD, PGD), full(1, DPOOL)],
        out_shape=(SDS((T, DIN), f32), SDS((1, HD), f32), SDS((1, HD), f32), SDS((1, 128), f32),
                   SDS((NBUCK, NH), f32), SDS((npg, PGD, PGD), f32), SDS((1, DPOOL), f32)),
        scratch_shapes=[pltpu.VMEM((NH, BLK, 2 * BLK), f32)],
        compiler_params=_cparams(("arbitrary",), VMEM_LIMIT_V7X),
        name=name)(z, z, dy, probs, qg, kg, bucket, pool_w, pscale)


class _LocalWeights:
    def __init__(self, w1, wint, wout, w2):
        self.w1, self.wint, self.wout, self.w2 = w1, wint, wout, w2

    def ffn1(self):
        return self.w1

    def first_norm(self, x, gain):
        return _norm_fwd(x, gain, "norm1_fwd")

    def after_ffn1(self, gain, x1):
        return gain

    def mix(self, after):
        return self.wint, self.wout

    def before_out_proj(self, wout, after):
        return wout

    def ffn2(self, after):
        return self.w2

    def mix_ffn2_grads_ready(self, dwint, dwout, dw2, dh2):
        self.grads_rest = (dwint, dwout, dw2)
        return dh2

    def before_ffn1_bwd(self, dx1b):
        return dx1b


def _local_step(x, target, weights, g1, gm, g3, qg, kg, sinks, relb, pool_w, pscale):
    bucket = jnp.asarray(_t5_bucket_table())
    sk = sinks.reshape(NH)
    w1 = weights.ffn1()
    h1 = weights.first_norm(x, g1)
    x1, gate1, up1 = _ffn_fwd(h1, w1, x, None, "ffn1_fwd")
    h2 = _norm_fwd(x1, weights.after_ffn1(gm, x1), "norm2_fwd")
    wint, wout = weights.mix(h2)
    z = _in_proj_fwd(h2, wint, "in_proj_fwd")
    ymix, probs = _mix_fwd(z, qg, kg, sk, relb, bucket, pool_w, pscale, "mix_fwd")
    wout = weights.before_out_proj(wout, ymix)
    x2, h3 = _out_proj_fwd(ymix, wout, x1, g3, "out_proj_fwd")
    w2 = weights.ffn2(h3)
    dy, gate2, up2, dyb, loss_lanes = _ffn_fwd(h3, w2, x2, target, "ffn2_fwd")

    dh3, dw2 = _ffn_bwd(dyb, h3, gate2, up2, w2, "ffn2_bwd")
    dx2, dx2b, dg3 = _norm_bwd(dh3, x2, g3, dy, 1.0, "norm3_bwd")
    dymix, dwout = _out_proj_bwd(dx2b, wout, ymix, "out_proj_bwd")
    dz, dqg, dkg, dsk, drb, dpw, dps = _mix_bwd(z, dymix, probs, qg, kg, relb, bucket, pool_w, pscale, "mix_bwd")
    dh2, dwint = _in_proj_bwd(dz, wint, h2, "in_proj_bwd")
    dh2 = weights.mix_ffn2_grads_ready(dwint, dwout, dw2, dh2)
    dx1, dx1b, dgm = _norm_bwd(dh2, x1, gm, dx2, 0.5, "norm2_bwd")
    dx1b = weights.before_ffn1_bwd(dx1b)
    dh1, dw1 = _ffn_bwd(dx1b, h1, gate1, up1, w1, "ffn1_bwd")
    dg1 = _gain_grad(dh1, x, "norm1_gain_grad")
    small = dict(ffn1_norm=dg1, mix_norm=dgm, ffn2_norm=dg3, pool_scale=dps, q_norm=dqg, k_norm=dkg,
                 attn_sinks=dsk[:, :NH], rel_bias=drb, pool_w=dpw, loss=loss_lanes)
    return (dh1, dx1), (dw1, dwint, dwout, dw2), small


SMALL_NAMES = ("ffn1_norm", "mix_norm", "ffn2_norm", "pool_scale", "q_norm", "k_norm", "attn_sinks", "rel_bias",
               "pool_w", "loss")
SMALL_SHAPES = dict(ffn1_norm=(1, D), mix_norm=(1, D), ffn2_norm=(1, D), pool_scale=(1, DPOOL), q_norm=(1, HD),
                    k_norm=(1, HD), attn_sinks=(1, NH), rel_bias=(NBUCK, NH),
                    pool_w=(1, len(POOL_WINDOWS), PGD, PGD), loss=(1, 128))


def _small_rows(name):
    return -(-int(np.prod(SMALL_SHAPES[name])) // 128)


SMALL_OFF = {}
_r = 0
for _n in SMALL_NAMES:
    SMALL_OFF[_n] = _r
    _r += _small_rows(_n)
SMALL_ROWS = -(-_r // 8) * 8
LOSS_ROW = SMALL_OFF["loss"]


def _pack_small(vals):
    parts = []
    for n in SMALL_NAMES:
        size = _small_rows(n) * 128
        if n in vals:
            flat = vals[n].astype(f32).reshape(-1)
            parts.append(jnp.pad(flat, (0, size - flat.shape[0])))
        else:
            parts.append(jnp.zeros((size,), f32))
    flat = jnp.concatenate(parts)
    flat = jnp.pad(flat, (0, SMALL_ROWS * 128 - flat.shape[0]))
    return flat.reshape(SMALL_ROWS, 128)


def _unpack_small(packed, name):
    size = int(np.prod(SMALL_SHAPES[name]))
    r0 = SMALL_OFF[name]
    return packed[r0:r0 + _small_rows(name)].reshape(-1)[:size].reshape(SMALL_SHAPES[name])


def _position():
    return lax.axis_index("x"), lax.axis_index("y"), lax.axis_index("c")


def _dev_index(x, y, c):
    return 4 * x + 2 * y + c


G1_PIECES, MIX_PIECES, F2_PIECES = (0, 1, 2), (3, 4), (5, 6, 7)


def _group_rows(pieces):
    return sum(PIECE_ROWS[k] for k in pieces)


def _shard_piece(s_ref, k):
    return s_ref.at[pl.ds(PIECE_OFF[k], PIECE_ROWS[k]), :]


def _shard_group(s_ref, pieces):
    return s_ref.at[pl.ds(PIECE_OFF[pieces[0]], _group_rows(pieces)), :]


def _weight_pieces(w1_ref=None, wi_ref=None, wo_ref=None, w2_ref=None):
    arrs = {}
    if w1_ref is not None:
        arrs.update({0: w1_ref.at[0], 1: w1_ref.at[1], 2: w1_ref.at[2]})
    if wi_ref is not None:
        arrs[3] = wi_ref
    if wo_ref is not None:
        arrs[4] = wo_ref
    if w2_ref is not None:
        arrs.update({5: w2_ref.at[0], 6: w2_ref.at[1], 7: w2_ref.at[2]})
    return arrs


def _block_rows(arrs, k, dev):
    r = PIECE_ROWS[k]
    return arrs[k].at[pl.ds(pl.multiple_of(_dev_index(*dev) * r, 16), r), :]


NORM_ROWS = 512


def _all_gather_ffn1(shard, x, gain):
    pieces = G1_PIECES
    rest_pieces = MIX_PIECES + F2_PIECES
    half = FS // 2
    T = x.shape[0]
    SIB, X0, X1, Y0, Y1, RELAY_Y, RELAY_X, ON_X, ON_Y, ON_D0, ON_D1 = range(11)

    def body(s_ref, x_ref, g_ref, w1_ref, h_ref, wi_ref, wo_ref, w2_ref, xbuf, hbuf, rest_buf,
             send_sems, recv_sems, local_sem, norm_sems):
        x, y, c = _position()
        me, sib = (x, y, c), (x, y, 1 - c)
        xn, yn, dg = (1 - x, y, c), (x, 1 - y, c), (1 - x, 1 - y, c)
        arrs = _weight_pieces(w1_ref=w1_ref)

        def place_rest():
            rest = _weight_pieces(wi_ref=wi_ref, wo_ref=wo_ref, w2_ref=w2_ref)
            grp = _shard_group(s_ref, rest_pieces)
            load = pltpu.make_async_copy(grp, rest_buf, norm_sems.at[0])
            load.start()
            load.wait()
            base = PIECE_OFF[rest_pieces[0]]
            for k in rest_pieces:
                pltpu.make_async_copy(rest_buf.at[pl.ds(PIECE_OFF[k] - base, PIECE_ROWS[k]), :],
                                      _block_rows(rest, k, me), norm_sems.at[1]).start()
            pltpu.make_async_copy(grp, rest_buf, norm_sems.at[1]).wait()

        def first_norm():
            for r in range(0, T, NORM_ROWS):
                load = pltpu.make_async_copy(x_ref.at[pl.ds(r, NORM_ROWS), :], xbuf, norm_sems.at[0])
                load.start()
                load.wait()
                xv = xbuf[...]
                rs = lax.rsqrt(jnp.mean(xv * xv, axis=-1, keepdims=True) + EPS)
                hbuf[...] = (xv * rs * g_ref[...]).astype(bf16)
                store = pltpu.make_async_copy(hbuf, h_ref.at[pl.ds(r, NORM_ROWS), :], norm_sems.at[1])
                store.start()
                store.wait()

        def rows_of(k, block, hf):
            r = PIECE_ROWS[k]
            start, size = (0, r) if hf is None else (hf * half, half)
            return arrs[k].at[pl.ds(pl.multiple_of(_dev_index(*block) * r + start, 16), size), :]

        def copies(rel, block, hf, to, from_shard=False):
            def src(k):
                if not from_shard:
                    return rows_of(k, block, hf)
                start, size = (0, PIECE_ROWS[k]) if hf is None else (hf * half, half)
                return s_ref.at[pl.ds(PIECE_OFF[k] + start, size), :]
            return [pltpu.make_async_remote_copy(
                src_ref=src(k), dst_ref=rows_of(k, block, hf), send_sem=send_sems.at[rel], recv_sem=recv_sems.at[rel],
                device_id=to, device_id_type=MESH) for k in pieces]

        def waiter(rel, hf):
            nrows = len(pieces) * (FS if hf is None else half)
            grp = s_ref.at[pl.ds(0, nrows), :]
            return pltpu.make_async_remote_copy(src_ref=grp, dst_ref=grp, send_sem=send_sems.at[rel],
                                                recv_sem=recv_sems.at[rel], device_id=me, device_id_type=MESH)

        def start(cps):
            for cp in cps:
                cp.start()

        mine = [pltpu.make_async_copy(_shard_piece(s_ref, k), _block_rows(arrs, k, me), local_sem) for k in pieces]
        start(mine)
        start(copies(SIB, me, None, sib, True))
        start(copies(X0, me, 0, xn, True))
        start(copies(Y1, me, 1, yn, True))
        start(copies(X1, me, 1, xn, True))
        start(copies(Y0, me, 0, yn, True))
        first_norm()
        place_rest()
        waiter(X0, 0).wait_recv()
        start(copies(RELAY_Y, xn, 0, yn))
        waiter(Y1, 1).wait_recv()
        start(copies(RELAY_X, yn, 1, xn))
        waiter(X1, 1).wait_recv()
        start(copies(ON_X, xn, None, sib))
        waiter(Y0, 0).wait_recv()
        start(copies(ON_Y, yn, None, sib))
        waiter(RELAY_Y, 0).wait_recv()
        start(copies(ON_D0, dg, 0, sib))
        waiter(RELAY_X, 1).wait_recv()
        start(copies(ON_D1, dg, 1, sib))
        waiter(SIB, None).wait_recv()
        waiter(ON_X, None).wait_recv()
        waiter(ON_Y, None).wait_recv()
        waiter(ON_D0, 0).wait_recv()
        waiter(ON_D1, 1).wait_recv()
        for rel, hf in ((SIB, None), (X0, 0), (X1, 1), (Y0, 0), (Y1, 1), (RELAY_Y, 0), (RELAY_X, 1),
                        (ON_X, None), (ON_Y, None), (ON_D0, 0), (ON_D1, 1)):
            waiter(rel, hf).wait_send()
        grp = _shard_group(s_ref, pieces)
        pltpu.make_async_copy(grp, grp, local_sem).wait()

    hbm = pl.BlockSpec(memory_space=pl.ANY)
    return pl.pallas_call(
        body, in_specs=[hbm, hbm, pl.BlockSpec(memory_space=pltpu.VMEM)], out_specs=[hbm] * 5,
        out_shape=(SDS((3, F, D), bf16), SDS((T, D), bf16),
                   SDS((DIN, D), bf16), SDS((DMIX, D), bf16), SDS((3, F, D), bf16)),
        scratch_shapes=[pltpu.VMEM((NORM_ROWS, D), f32), pltpu.VMEM((NORM_ROWS, D), bf16),
                        pltpu.VMEM((_group_rows(rest_pieces), D), bf16),
                        pltpu.SemaphoreType.DMA((11,)), pltpu.SemaphoreType.DMA((11,)), pltpu.SemaphoreType.DMA,
                        pltpu.SemaphoreType.DMA((2,))],
        compiler_params=pltpu.CompilerParams(has_side_effects=True),
        name="all_gather_ffn1")(shard, x, gain)


HBM_SPEC = pl.BlockSpec(memory_space=pltpu.HBM)
SEM_SPEC = pl.BlockSpec(memory_space=pltpu.SEMAPHORE)
ANY_SPEC = pl.BlockSpec(memory_space=pl.ANY)
SPLIT_EFFECT = pltpu.SideEffectType.DATAFLOW_SIDE_EFFECTING


def _in_hbm(a):
    return pltpu.with_memory_space_constraint(a, pltpu.HBM)


def _hbm_like(a):
    return pltpu.HBM(a.shape, a.dtype)


def _xor_peer(x, y, c, k):
    return (x ^ (k >> 2), y ^ ((k >> 1) & 1), c ^ (k & 1))


def _gather_rest_start(shard, wi, wo, w2, w1):
    def body(s_ref, wi_ref, wo_ref, w2_ref, w1_ref,
             ssem_m, rsem_m0, rsem_m, ssem_f, rsem_f0, rsem_f, s_o, wi_o, wo_o, w2_o, w1_o):
        x, y, c = _position()
        me, sib = (x, y, c), (x, y, 1 - c)
        chips = [(1 - x, y), (x, 1 - y), (1 - x, 1 - y)]
        arrs = _weight_pieces(wi_ref=wi_ref, wo_ref=wo_ref, w2_ref=w2_ref)
        for pieces, ssem, rsem0, rsem in ((MIX_PIECES, ssem_m, rsem_m0, rsem_m), (F2_PIECES, ssem_f, rsem_f0, rsem_f)):
            for p in pieces:
                pltpu.make_async_remote_copy(
                    src_ref=_shard_piece(s_ref, p), dst_ref=_block_rows(arrs, p, me), send_sem=ssem.at[0],
                    recv_sem=rsem0, device_id=sib, device_id_type=MESH).start()
            for j, chip in enumerate(chips):
                for p in pieces:
                    pltpu.make_async_remote_copy(
                        src_ref=_shard_piece(s_ref, p), dst_ref=_block_rows(arrs, p, me), send_sem=ssem.at[1 + j],
                        recv_sem=rsem.at[j], device_id=(*chip, c), device_id_type=MESH).start()

    dma = pltpu.SemaphoreType.DMA
    return pl.pallas_call(
        body, name="gather_rest_start",
        out_shape=(dma((4,)), dma(()), dma((3,)), dma((4,)), dma(()), dma((3,)),
                   _hbm_like(shard), _hbm_like(wi), _hbm_like(wo), _hbm_like(w2), _hbm_like(w1)),
        in_specs=(HBM_SPEC,) * 5, out_specs=(SEM_SPEC,) * 6 + (HBM_SPEC,) * 5,
        input_output_aliases={0: 6, 1: 7, 2: 8, 3: 9, 4: 10},
        compiler_params=pltpu.CompilerParams(has_side_effects=SPLIT_EFFECT),
    )(_in_hbm(shard), _in_hbm(wi), _in_hbm(wo), _in_hbm(w2), _in_hbm(w1))


def _gather_mix_pass_on(rsem_m, wi, wo, thru, after):
    def body(wi_ref, wo_ref, thru_ref, rsem, after_ref, fsend, frecv, wi_o, wo_o, thru_o):
        x, y, c = _position()
        sib = (x, y, 1 - c)
        arrs = _weight_pieces(wi_ref=wi_ref, wo_ref=wo_ref)
        both = wi_ref.at[pl.ds(0, _group_rows(MIX_PIECES)), :]
        for j, chip in enumerate([(1 - x, y), (x, 1 - y), (1 - x, 1 - y)]):
            pltpu.make_async_remote_copy(src_ref=both, dst_ref=both, send_sem=fsend.at[j], recv_sem=rsem.at[j],
                                         device_id=(x, y, c), device_id_type=MESH).wait_recv()
            for p in MIX_PIECES:
                rows = _block_rows(arrs, p, (*chip, c))
                pltpu.make_async_remote_copy(src_ref=rows, dst_ref=rows, send_sem=fsend.at[j], recv_sem=frecv.at[j],
                                             device_id=sib, device_id_type=MESH).start()

    dma = pltpu.SemaphoreType.DMA
    return pl.pallas_call(
        body, name="gather_mix_pass_on",
        out_shape=(dma((3,)), dma((3,)), _hbm_like(wi), _hbm_like(wo), _hbm_like(thru)),
        in_specs=(HBM_SPEC, HBM_SPEC, HBM_SPEC, SEM_SPEC, ANY_SPEC), out_specs=(SEM_SPEC, SEM_SPEC) + (HBM_SPEC,) * 3,
        input_output_aliases={0: 2, 1: 3, 2: 4},
        compiler_params=pltpu.CompilerParams(has_side_effects=SPLIT_EFFECT),
    )(wi, wo, _in_hbm(thru), rsem_m, after)


def _gather_mix_wait(ssem_m, rsem_m0, fsend, frecv, shard, wi, wo, after):
    def body(s_ref, wi_ref, wo_ref, ssem, rsem0, fs, fr, after_ref, s_o, wi_o, wo_o):
        x, y, c = _position()
        grp = _shard_group(s_ref, MIX_PIECES)

        def waiter(send_sem, recv_sem):
            return pltpu.make_async_remote_copy(src_ref=grp, dst_ref=grp, send_sem=send_sem, recv_sem=recv_sem,
                                                device_id=(x, y, c), device_id_type=MESH)

        waiter(ssem.at[0], rsem0).wait_recv()
        for j in range(3):
            waiter(fs.at[j], fr.at[j]).wait_recv()
        for rel in range(4):
            waiter(ssem.at[rel], rsem0).wait_send()
        for j in range(3):
            waiter(fs.at[j], fr.at[j]).wait_send()

    return pl.pallas_call(
        body, name="gather_mix_wait", out_shape=(_hbm_like(shard), _hbm_like(wi), _hbm_like(wo)),
        in_specs=(HBM_SPEC,) * 3 + (SEM_SPEC,) * 4 + (ANY_SPEC,), out_specs=(HBM_SPEC,) * 3,
        input_output_aliases={0: 0, 1: 1, 2: 2},
        compiler_params=pltpu.CompilerParams(has_side_effects=SPLIT_EFFECT),
    )(shard, wi, wo, ssem_m, rsem_m0, fsend, frecv, after)


def _gather_ffn2_pass_on(rsem_f, w2, wo, after):
    def body(w2_ref, wo_ref, rsem, after_ref, fsend, frecv, w2_o, wo_o):
        x, y, c = _position()
        sib = (x, y, 1 - c)
        chips = [(1 - x, y), (x, 1 - y), (1 - x, 1 - y)]
        arrs = _weight_pieces(w2_ref=w2_ref)
        three = w2_ref.at[0, pl.ds(0, _group_rows(F2_PIECES)), :]
        for j, chip in enumerate(chips):
            pltpu.make_async_remote_copy(src_ref=three, dst_ref=three, send_sem=fsend.at[j], recv_sem=rsem.at[j],
                                         device_id=(x, y, c), device_id_type=MESH).wait_recv()
            for p in F2_PIECES:
                rows = _block_rows(arrs, p, (*chip, c))
                pltpu.make_async_remote_copy(src_ref=rows, dst_ref=rows, send_sem=fsend.at[j], recv_sem=frecv.at[j],
                                             device_id=sib, device_id_type=MESH).start()

    dma = pltpu.SemaphoreType.DMA
    return pl.pallas_call(
        body, name="gather_ffn2_pass_on", out_shape=(dma((3,)), dma((3,)), _hbm_like(w2), _hbm_like(wo)),
        in_specs=(HBM_SPEC, HBM_SPEC, SEM_SPEC, ANY_SPEC), out_specs=(SEM_SPEC, SEM_SPEC, HBM_SPEC, HBM_SPEC),
        input_output_aliases={0: 2, 1: 3},
        compiler_params=pltpu.CompilerParams(has_side_effects=SPLIT_EFFECT),
    )(w2, wo, rsem_f, after)


def _gather_ffn2_wait(ssem_f, rsem_f0, fsend, frecv, shard, w2, after):
    def body(s_ref, w2_ref, ssem, rsem0, fs, fr, after_ref, w2_o):
        x, y, c = _position()
        grp = _shard_group(s_ref, F2_PIECES)

        def waiter(send_sem, recv_sem):
            return pltpu.make_async_remote_copy(src_ref=grp, dst_ref=grp, send_sem=send_sem, recv_sem=recv_sem,
                                                device_id=(x, y, c), device_id_type=MESH)

        waiter(ssem.at[0], rsem0).wait_recv()
        for j in range(3):
            waiter(fs.at[j], fr.at[j]).wait_recv()
        for rel in range(4):
            waiter(ssem.at[rel], rsem0).wait_send()
        for j in range(3):
            waiter(fs.at[j], fr.at[j]).wait_send()

    return pl.pallas_call(
        body, name="gather_ffn2_wait", out_shape=_hbm_like(w2),
        in_specs=(HBM_SPEC, HBM_SPEC, SEM_SPEC, SEM_SPEC, SEM_SPEC, SEM_SPEC, ANY_SPEC), out_specs=HBM_SPEC,
        input_output_aliases={1: 0},
        compiler_params=pltpu.CompilerParams(has_side_effects=SPLIT_EFFECT),
    )(shard, w2, ssem_f, rsem_f0, fsend, frecv, after)


class _GatheredWeights(_LocalWeights):
    def __init__(self, shard, x, gain1):
        w1, self.h1, wi, wo, w2 = _all_gather_ffn1(shard, x, gain1)
        (self.ssem_m, self.rsem_m0, self.rsem_m, self.ssem_f, self.rsem_f0, self.rsem_f,
         self.shard, self.wi, self.wo, self.w2_part, self.w1) = _gather_rest_start(shard, wi, wo, w2, w1)

    def first_norm(self, x, gain):
        return self.h1

    def after_ffn1(self, gain, x1):
        self.fsend_m, self.frecv_m, self.wi, self.wo, gain = _gather_mix_pass_on(self.rsem_m, self.wi, self.wo, gain, x1)
        return gain

    def mix(self, after):
        self.shard, wint, wout = _gather_mix_wait(self.ssem_m, self.rsem_m0, self.fsend_m, self.frecv_m, self.shard,
                                                  self.wi, self.wo, after)
        return wint, wout

    def before_out_proj(self, wout, after):
        self.fsend, self.frecv, self.w2_part, wout = _gather_ffn2_pass_on(self.rsem_f, self.w2_part, wout, after)
        return wout

    def ffn2(self, after):
        return _gather_ffn2_wait(self.ssem_f, self.rsem_f0, self.fsend, self.frecv, self.shard, self.w2_part, after)

    def mix_ffn2_grads_ready(self, dwint, dwout, dw2, dh2):
        rx1 = lax.empty((4, RSA_ROWS, D), bf16)
        self.sa, self.ra, dwint, dwout, dw2, rx1, dh2 = _rsa_level1_start(dwint, dwout, dw2, rx1, dh2)
        self.level1 = (dwint, dwout, dw2, rx1)
        return dh2

    def before_ffn1_bwd(self, dx1b):
        dwint, dwout, dw2, rx1 = _rsa_level1_wait(self.sa, self.ra, *self.level1, dx1b)
        tx, self.acc = _rsa_chip_sums(dwint, dwout, dw2, rx1)
        rx2 = lax.empty((3, RSA_ROWS, D), bf16)
        self.sb, self.rb, self.tx, self.rx2, dx1b = _rsa_level2_start(tx, rx2, dx1b)
        return dx1b

    def mix_ffn2_grads_total(self, after):
        rx2 = _rsa_level2_wait(self.sb, self.rb, self.tx, self.rx2, after)
        return _rsa_total(self.acc, rx2)


def _reduce_scatter_ffn1_head(dw1, small_packed):
    pieces = G1_PIECES
    half = FS // 2
    hrows = len(pieces) * half
    nrows = 2 * hrows
    X_RELAY, Y_RELAY = range(2)

    def body(d1_ref, p_ref, forx_ref, fory_ref, own_ref, rx1_ref, relx_ref, rely_ref, tot_ref,
             own_buf, rx_buf, tx1, tx2, tx3, acc, sa, ra, sb, rb, lsem, pair, chips, small_send, small_recv):
        x, y, c = _position()
        me, sib = (x, y, c), (x, y, 1 - c)
        xn, yn = (1 - x, y, c), (x, 1 - y, c)
        rel_chips = [(x, y), (1 - x, y), (x, 1 - y), (1 - x, 1 - y)]
        srcs = _weight_pieces(w1_ref=d1_ref)

        my_chip = 2 * x + y
        pair[c] = p_ref[...]
        swap = pltpu.make_async_remote_copy(
            src_ref=p_ref, dst_ref=pair.at[c], send_sem=small_send.at[0], recv_sem=small_recv.at[0],
            device_id=sib, device_id_type=MESH)
        swap.start()
        small = [pltpu.make_async_remote_copy(
            src_ref=chips.at[my_chip], dst_ref=chips.at[my_chip], send_sem=small_send.at[j], recv_sem=small_recv.at[j],
            device_id=(*rel_chips[j], c), device_id_type=MESH) for j in (1, 2, 3)]

        def part(k, dev, hf):
            r = PIECE_ROWS[k]
            return srcs[k].at[pl.ds(pl.multiple_of(_dev_index(*dev) * r + hf * half, 16), half), :]

        def slot(ref, k, hf):
            return ref.at[pl.ds(hf * hrows + k * half, half), :]

        halves = [(k, hf) for hf in (0, 1) for k in pieces]

        for j in (3, 1, 2, 0):
            for k, hf in halves:
                pltpu.make_async_remote_copy(
                    src_ref=part(k, (*rel_chips[j], 1 - c), hf), dst_ref=slot(rx1_ref.at[j], k, hf),
                    send_sem=sa.at[j], recv_sem=ra.at[j], device_id=sib, device_id_type=MESH).start()

        def wait_a(j):
            return pltpu.make_async_remote_copy(src_ref=rx1_ref.at[j], dst_ref=rx1_ref.at[j], send_sem=sa.at[j],
                                                recv_sem=ra.at[j], device_id=me, device_id_type=MESH)

        def ici(rel, src, dst, to):
            return pltpu.make_async_remote_copy(src_ref=src, dst_ref=dst, send_sem=sb.at[rel], recv_sem=rb.at[rel],
                                                device_id=to, device_id_type=MESH)

        first, second = pl.ds(0, hrows), pl.ds(hrows, hrows)
        sends = {
            X_RELAY: ici(X_RELAY, tx3.at[first, :], relx_ref, xn),
            Y_RELAY: ici(Y_RELAY, tx3.at[second, :], rely_ref, yn),
        }

        swap.wait_recv()
        chips[my_chip] = pair[0] + pair[1]
        for cp in small:
            cp.start()

        def chip_sum(j, dst):
            loads = [pltpu.make_async_copy(part(k, (*rel_chips[j], c), hf), slot(own_buf, k, hf), lsem.at[0])
                     for k, hf in halves]
            for cp in loads:
                cp.start()
            wait_a(j).wait_recv()
            got = pltpu.make_async_copy(rx1_ref.at[j], rx_buf, lsem.at[1])
            got.start()
            pltpu.make_async_copy(rx_buf, rx_buf, lsem.at[0]).wait()
            got.wait()

            def add(i, carry):
                rows = pl.ds(pl.multiple_of(i * half, 16), half)
                tot = own_buf[rows, :].astype(f32) + rx_buf[rows, :].astype(f32)
                dst[rows, :] = tot.astype(dst.dtype)
                return carry

            lax.fori_loop(0, nrows // half, add, 0)

        def add_landed(landed, dst, rows0, nrows_):
            got = pltpu.make_async_copy(landed, rx_buf.at[pl.ds(0, nrows_), :], lsem.at[1])
            got.start()
            got.wait()

            def add(i, carry):
                src_rows = pl.ds(pl.multiple_of(i * half, 16), half)
                dst_rows = pl.ds(pl.multiple_of(rows0 + i * half, 16), half)
                dst[dst_rows, :] = (dst[dst_rows, :].astype(f32) + rx_buf[src_rows, :].astype(f32)).astype(dst.dtype)
                return carry

            lax.fori_loop(0, nrows_ // half, add, 0)

        chip_sum(3, tx3)
        sends[X_RELAY].start()
        sends[Y_RELAY].start()
        chip_sum(1, tx1)
        chip_sum(2, tx2)
        chip_sum(0, acc)
        own_out = pltpu.make_async_copy(acc, own_ref, lsem.at[0])
        own_out.start()
        sends[X_RELAY].wait_recv()
        add_landed(relx_ref, tx2, 0, hrows)
        sends[Y_RELAY].wait_recv()
        add_landed(rely_ref, tx1, hrows, hrows)
        own_out.wait()
        outs = [pltpu.make_async_copy(tx1, forx_ref, lsem.at[0]), pltpu.make_async_copy(tx2, fory_ref, lsem.at[1])]
        for cp in outs:
            cp.start()
        for cp in outs:
            cp.wait()
        for cp in small:
            cp.wait_recv()
        tot = (chips[0] + chips[1]) + (chips[2] + chips[3])
        tot_ref[...] = tot
        loss = jnp.sum(tot[LOSS_ROW:LOSS_ROW + 1, :], axis=-1, keepdims=True)
        tot_ref[LOSS_ROW:LOSS_ROW + 1, :] = jnp.broadcast_to(loss, (1, 128))
        for j in range(4):
            wait_a(j).wait_send()
        for cp in sends.values():
            cp.wait_send()
        swap.wait_send()
        for cp in small:
            cp.wait_send()

    hbm = pl.BlockSpec(memory_space=pl.ANY)
    vm = pl.BlockSpec(memory_space=pltpu.VMEM)
    outs = pl.pallas_call(
        body, in_specs=[hbm, vm], out_specs=[hbm] * 6 + [vm],
        out_shape=(SDS((nrows, D), bf16), SDS((nrows, D), bf16), SDS((nrows, D), f32), SDS((4, nrows, D), bf16),
                   SDS((hrows, D), bf16), SDS((hrows, D), bf16), SDS((SMALL_ROWS, 128), f32)),
        scratch_shapes=[pltpu.VMEM((nrows, D), bf16), pltpu.VMEM((nrows, D), bf16),
                        pltpu.VMEM((nrows, D), bf16), pltpu.VMEM((nrows, D), bf16), pltpu.VMEM((nrows, D), bf16),
                        pltpu.VMEM((nrows, D), f32),
                        pltpu.SemaphoreType.DMA((4,)), pltpu.SemaphoreType.DMA((4,)),
                        pltpu.SemaphoreType.DMA((2,)), pltpu.SemaphoreType.DMA((2,)), pltpu.SemaphoreType.DMA((2,)),
                        pltpu.VMEM((2, SMALL_ROWS, 128), f32), pltpu.VMEM((4, SMALL_ROWS, 128), f32),
                        pltpu.SemaphoreType.DMA((4,)), pltpu.SemaphoreType.DMA((4,))],
        compiler_params=pltpu.CompilerParams(has_side_effects=True, vmem_limit_bytes=VMEM_LIMIT_V7X),
        name="reduce_scatter_ffn1_head")(dw1, small_packed)
    return outs[0], outs[1], outs[2], outs[-1]


def _rs1_tail_start(for_x, for_y, from_x, from_y, thru_a, thru_b):
    def body(fx_ref, fy_ref, lx_ref, ly_ref, ta_ref, tb_ref, ssem, rsem, fx_o, fy_o, lx_o, ly_o, ta_o, tb_o):
        x, y, c = _position()
        pltpu.make_async_remote_copy(src_ref=fx_ref, dst_ref=lx_ref, send_sem=ssem.at[0], recv_sem=rsem.at[0],
                                     device_id=(1 - x, y, c), device_id_type=MESH).start()
        pltpu.make_async_remote_copy(src_ref=fy_ref, dst_ref=ly_ref, send_sem=ssem.at[1], recv_sem=rsem.at[1],
                                     device_id=(x, 1 - y, c), device_id_type=MESH).start()

    dma = pltpu.SemaphoreType.DMA
    arrs = (for_x, for_y, from_x, from_y, thru_a, thru_b)
    return pl.pallas_call(
        body, name="rs1_tail_start", out_shape=(dma((2,)), dma((2,))) + tuple(_hbm_like(a) for a in arrs),
        in_specs=(HBM_SPEC,) * 6, out_specs=(SEM_SPEC,) * 2 + (HBM_SPEC,) * 6,
        input_output_aliases={0: 2, 1: 3, 2: 4, 3: 5, 4: 6, 5: 7},
        compiler_params=pltpu.CompilerParams(has_side_effects=SPLIT_EFFECT),
    )(*[_in_hbm(a) for a in arrs])


def _rs1_tail_wait(ssem, rsem, for_x, for_y, from_x, from_y, after):
    def body(fx_ref, fy_ref, lx_ref, ly_ref, ssem_ref, rsem_ref, after_ref, lx_o, ly_o):
        x, y, c = _position()
        for j, (src, dst) in enumerate(((fx_ref, lx_ref), (fy_ref, ly_ref))):
            d = pltpu.make_async_remote_copy(src_ref=src, dst_ref=dst, send_sem=ssem_ref.at[j], recv_sem=rsem_ref.at[j],
                                             device_id=(x, y, c), device_id_type=MESH)
            d.wait_recv()
            d.wait_send()

    return pl.pallas_call(
        body, name="rs1_tail_wait", out_shape=(_hbm_like(from_x), _hbm_like(from_y)),
        in_specs=(HBM_SPEC,) * 4 + (SEM_SPEC, SEM_SPEC, ANY_SPEC), out_specs=(HBM_SPEC, HBM_SPEC),
        input_output_aliases={2: 0, 3: 1},
        compiler_params=pltpu.CompilerParams(has_side_effects=SPLIT_EFFECT),
    )(for_x, for_y, from_x, from_y, ssem, rsem, after)


def _rs1_total(own, from_x, from_y):
    half = FS // 2
    npiece = len(G1_PIECES)

    def body(a_ref, x_ref, y_ref, o_ref):
        o_ref[...] = (a_ref[...] + x_ref[...].astype(f32)) + y_ref[...].astype(f32)

    blk = pl.BlockSpec((half, D), lambda i: (i, 0))
    return pl.pallas_call(
        body, grid=(2 * npiece,), in_specs=[blk, blk, blk],
        out_specs=pl.BlockSpec((half, D), lambda i: (2 * (i % npiece) + i // npiece, 0)),
        out_shape=SDS((npiece * FS, D), f32), name="rs1_total")(own, from_x, from_y)


RSA_PIECES = MIX_PIECES + F2_PIECES
RSA_ROWS = _group_rows(RSA_PIECES)
RSA_OFF = {k: PIECE_OFF[k] - PIECE_OFF[RSA_PIECES[0]] for k in RSA_PIECES}
RSA_BLOCK = 192


def _rsa_rows(ref, k):
    return ref.at[pl.ds(RSA_OFF[k], PIECE_ROWS[k]), :]


def _rsa_level1_start(dwint, dwout, dw2, rx1, thru):
    def body(di_ref, do_ref, d2_ref, rx1_ref, thru_ref, sa, ra, di_o, do_o, d2_o, rx1_o, thru_o):
        x, y, c = _position()
        srcs = _weight_pieces(wi_ref=di_ref, wo_ref=do_ref, w2_ref=d2_ref)
        for j, chip in enumerate([(x, y), (1 - x, y), (x, 1 - y), (1 - x, 1 - y)]):
            for k in RSA_PIECES:
                pltpu.make_async_remote_copy(
                    src_ref=_block_rows(srcs, k, (*chip, 1 - c)), dst_ref=_rsa_rows(rx1_ref.at[j], k),
                    send_sem=sa.at[j], recv_sem=ra.at[j], device_id=(x, y, 1 - c), device_id_type=MESH).start()

    dma = pltpu.SemaphoreType.DMA
    arrs = (dwint, dwout, dw2, rx1, thru)
    return pl.pallas_call(
        body, name="rsa_level1_start", out_shape=(dma((4,)), dma((4,))) + tuple(_hbm_like(a) for a in arrs),
        in_specs=(HBM_SPEC,) * 5, out_specs=(SEM_SPEC,) * 2 + (HBM_SPEC,) * 5,
        input_output_aliases={0: 2, 1: 3, 2: 4, 3: 5, 4: 6},
        compiler_params=pltpu.CompilerParams(has_side_effects=SPLIT_EFFECT),
    )(*[_in_hbm(a) for a in arrs])


def _rsa_level1_wait(sa, ra, dwint, dwout, dw2, rx1, after):
    def body(di_ref, do_ref, d2_ref, rx1_ref, sa_ref, ra_ref, after_ref, di_o, do_o, d2_o, rx1_o):
        x, y, c = _position()
        for j in range(4):
            d = pltpu.make_async_remote_copy(src_ref=rx1_ref.at[j], dst_ref=rx1_ref.at[j], send_sem=sa_ref.at[j],
                                             recv_sem=ra_ref.at[j], device_id=(x, y, c), device_id_type=MESH)
            d.wait_recv()
            d.wait_send()

    arrs = (dwint, dwout, dw2, rx1)
    return pl.pallas_call(
        body, name="rsa_level1_wait", out_shape=tuple(_hbm_like(a) for a in arrs),
        in_specs=(HBM_SPEC,) * 4 + (SEM_SPEC, SEM_SPEC, ANY_SPEC), out_specs=(HBM_SPEC,) * 4,
        input_output_aliases={0: 0, 1: 1, 2: 2, 3: 3},
        compiler_params=pltpu.CompilerParams(has_side_effects=SPLIT_EFFECT),
    )(*arrs, sa, ra, after)


def _rsa_chip_sums(dwint, dwout, dw2, rx1):
    nblk = RSA_ROWS // RSA_BLOCK

    def body(di_ref, do_ref, d2_ref, rx1_ref, tx_ref, acc_ref, own_buf, rx_buf, tx_buf, acc_buf, in_sems, out_sems):
        x, y, c = _position()
        srcs = _weight_pieces(wi_ref=di_ref, wo_ref=do_ref, w2_ref=d2_ref)
        chips = [(x, y), (1 - x, y), (x, 1 - y), (1 - x, 1 - y)]

        def start_loads(j):
            s = j % 2
            for k in RSA_PIECES:
                pltpu.make_async_copy(_block_rows(srcs, k, (*chips[j], c)), _rsa_rows(own_buf.at[s], k),
                                      in_sems.at[2 * s]).start()
            pltpu.make_async_copy(rx1_ref.at[j], rx_buf.at[s], in_sems.at[2 * s + 1]).start()

        def wait_loads(j):
            s = j % 2
            pltpu.make_async_copy(rx1_ref.at[j], own_buf.at[s], in_sems.at[2 * s]).wait()
            pltpu.make_async_copy(rx1_ref.at[j], rx_buf.at[s], in_sems.at[2 * s + 1]).wait()

        def store(j):
            if j == 0:
                return pltpu.make_async_copy(acc_buf, acc_ref, out_sems.at[2])
            return pltpu.make_async_copy(tx_buf.at[j % 2], tx_ref.at[j - 1], out_sems.at[j % 2])

        start_loads(0)
        for j in range(4):
            s = j % 2
            if j + 1 < 4:
                start_loads(j + 1)
            wait_loads(j)
            if j == 3:
                store(1).wait()

            def add(i, carry, j=j, s=s):
                rows = pl.ds(pl.multiple_of(i * RSA_BLOCK, 16), RSA_BLOCK)
                tot = own_buf[s, rows, :].astype(f32) + rx_buf[s, rows, :].astype(f32)
                if j == 0:
                    acc_buf[rows, :] = tot
                else:
                    tx_buf[s, rows, :] = tot.astype(bf16)
                return carry

            lax.fori_loop(0, nblk, add, 0)
            store(j).start()
        store(0).wait()
        store(2).wait()
        store(3).wait()

    return pl.pallas_call(
        body, in_specs=[ANY_SPEC] * 4, out_specs=[ANY_SPEC] * 2,
        out_shape=(SDS((3, RSA_ROWS, D), bf16), SDS((RSA_ROWS, D), f32)),
        scratch_shapes=[pltpu.VMEM((2, RSA_ROWS, D), bf16), pltpu.VMEM((2, RSA_ROWS, D), bf16),
                        pltpu.VMEM((2, RSA_ROWS, D), bf16), pltpu.VMEM((RSA_ROWS, D), f32),
                        pltpu.SemaphoreType.DMA((4,)), pltpu.SemaphoreType.DMA((3,))],
        compiler_params=_cparams(None, VMEM_LIMIT_V7X), name="rsa_chip_sums")(dwint, dwout, dw2, rx1)


def _rsa_level2_start(tx, rx2, thru):
    def body(tx_ref, rx2_ref, thru_ref, sb, rb, tx_o, rx2_o, thru_o):
        x, y, c = _position()
        for j, chip in enumerate([(1 - x, y), (x, 1 - y), (1 - x, 1 - y)]):
            pltpu.make_async_remote_copy(src_ref=tx_ref.at[j], dst_ref=rx2_ref.at[j], send_sem=sb.at[j],
                                         recv_sem=rb.at[j], device_id=(*chip, c), device_id_type=MESH).start()

    dma = pltpu.SemaphoreType.DMA
    arrs = (tx, rx2, thru)
    return pl.pallas_call(
        body, name="rsa_level2_start", out_shape=(dma((3,)), dma((3,))) + tuple(_hbm_like(a) for a in arrs),
        in_specs=(HBM_SPEC,) * 3, out_specs=(SEM_SPEC,) * 2 + (HBM_SPEC,) * 3,
        input_output_aliases={0: 2, 1: 3, 2: 4},
        compiler_params=pltpu.CompilerParams(has_side_effects=SPLIT_EFFECT),
    )(*[_in_hbm(a) for a in arrs])


def _rsa_level2_wait(sb, rb, tx, rx2, after):
    def body(tx_ref, rx2_ref, sb_ref, rb_ref, after_ref, rx2_o):
        x, y, c = _position()
        for j in range(3):
            d = pltpu.make_async_remote_copy(src_ref=tx_ref.at[j], dst_ref=rx2_ref.at[j], send_sem=sb_ref.at[j],
                                             recv_sem=rb_ref.at[j], device_id=(x, y, c), device_id_type=MESH)
            d.wait_recv()
            d.wait_send()

    return pl.pallas_call(
        body, name="rsa_level2_wait", out_shape=_hbm_like(rx2),
        in_specs=(HBM_SPEC, HBM_SPEC, SEM_SPEC, SEM_SPEC, ANY_SPEC), out_specs=HBM_SPEC,
        input_output_aliases={1: 0},
        compiler_params=pltpu.CompilerParams(has_side_effects=SPLIT_EFFECT),
    )(tx, rx2, sb, rb, after)


def _rsa_total(acc, rx2):
    def body(a_ref, r_ref, o_ref):
        o_ref[...] = ((a_ref[...] + r_ref[0].astype(f32)) + r_ref[1].astype(f32)) + r_ref[2].astype(f32)

    return pl.pallas_call(
        body, grid=(RSA_ROWS // RSA_BLOCK,),
        in_specs=[pl.BlockSpec((RSA_BLOCK, D), lambda i: (i, 0)), pl.BlockSpec((3, RSA_BLOCK, D), lambda i: (0, i, 0))],
        out_specs=pl.BlockSpec((RSA_BLOCK, D), lambda i: (i, 0)),
        out_shape=SDS((RSA_ROWS, D), f32), name="rsa_total")(acc, rx2)


def _adamw_math(w, g, m, v):
    m = ADAM_B1 * m + (1.0 - ADAM_B1) * g
    v = ADAM_B2 * v + (1.0 - ADAM_B2) * (g * g)
    m_hat = m / (1.0 - ADAM_B1 ** ADAM_STEP)
    v_hat = v / (1.0 - ADAM_B2 ** ADAM_STEP)
    delta = -ADAM_LR * (m_hat / (jnp.sqrt(v_hat) + ADAM_EPS) + ADAM_WD * w)
    return delta, m, v


def _adamw_big(pieces, ws, ms, vs, red, name):
    npiece = len(pieces)
    rmax = max(PIECE_ROWS[k] for k in pieces)

    def body(*refs):
        ins = (refs[0:npiece], refs[npiece:2 * npiece], refs[2 * npiece:3 * npiece])
        red_ref = refs[3 * npiece]
        out_refs = refs[3 * npiece + 1:7 * npiece + 1]
        inb, outb, in_sems, out_sems = refs[7 * npiece + 1:]

        def grad_rows(k):
            if k in G1_PIECES:
                return red_ref.at[pl.ds(PIECE_OFF[k], PIECE_ROWS[k]), :]
            return _rsa_rows(red_ref, k)

        def loads(i):
            s, k = i % 2, pieces[i]
            r = PIECE_ROWS[k]
            cps = [pltpu.make_async_copy(ins[q][i].at[0], inb.at[s, q, pl.ds(0, r), :], in_sems.at[4 * s + q])
                   for q in range(3)]
            cps.append(pltpu.make_async_copy(grad_rows(k), inb.at[s, 3, pl.ds(0, r), :], in_sems.at[4 * s + 3]))
            return cps

        def stores(i):
            s, r = i % 2, PIECE_ROWS[pieces[i]]
            return [pltpu.make_async_copy(outb.at[s, q, pl.ds(0, r), :], out_refs[q * npiece + i].at[0],
                                          out_sems.at[4 * s + q]) for q in range(4)]

        for cp in loads(0):
            cp.start()
        for i in range(npiece):
            s, r = i % 2, PIECE_ROWS[pieces[i]]
            if i + 1 < npiece:
                for cp in loads(i + 1):
                    cp.start()
            for cp in loads(i):
                cp.wait()
            if i >= 2:
                for cp in stores(i - 2):
                    cp.wait()
            g = inb[s, 3, 0:r, :]
            d, nm, nv = _adamw_math(inb[s, 0, 0:r, :], g, inb[s, 1, 0:r, :], inb[s, 2, 0:r, :])
            outb[s, 0, 0:r, :] = g
            outb[s, 1, 0:r, :] = d
            outb[s, 2, 0:r, :] = nm
            outb[s, 3, 0:r, :] = nv
            for cp in stores(i):
                cp.start()
        for i in range(max(npiece - 2, 0), npiece):
            for cp in stores(i):
                cp.wait()

    hbm = pl.BlockSpec(memory_space=pl.ANY)
    outs = pl.pallas_call(
        body, in_specs=[hbm] * (3 * npiece + 1), out_specs=[hbm] * (4 * npiece),
        out_shape=tuple(SDS(w.shape, f32) for _ in range(4) for w in ws),
        scratch_shapes=[pltpu.VMEM((2, 4, rmax, D), f32), pltpu.VMEM((2, 4, rmax, D), f32),
                        pltpu.SemaphoreType.DMA((8,)), pltpu.SemaphoreType.DMA((8,))],
        compiler_params=_cparams(None, VMEM_LIMIT_V7X), name=name)(*ws, *ms, *vs, red)
    return [list(outs[q * npiece:(q + 1) * npiece]) for q in range(4)]


def _adamw_small(ws, ms, vs, gs, name):
    n = len(ws)

    def body(*refs):
        w_refs, m_refs, v_refs, g_refs = refs[0:n], refs[n:2 * n], refs[2 * n:3 * n], refs[3 * n:4 * n]
        outs = refs[4 * n:]
        for i in range(n):
            d, nm, nv = _adamw_math(w_refs[i][...], g_refs[i][...], m_refs[i][...], v_refs[i][...])
            outs[i][...] = d
            outs[n + i][...] = nm
            outs[2 * n + i][...] = nv

    outs = pl.pallas_call(
        body, out_shape=tuple(SDS(w.shape, f32) for _ in range(3) for w in ws), name=name)(*ws, *ms, *vs, *gs)
    return [list(outs[q * n:(q + 1) * n]) for q in range(3)]


WEIGHTS = ("ffn1_norm", "ffn1_w_gate", "ffn1_w_up", "ffn1_w_down", "mix_norm", "w_in", "q_norm", "k_norm",
           "attn_sinks", "rel_bias", "pool_w", "pool_scale", "w_out", "ffn2_norm", "ffn2_w_gate", "ffn2_w_up",
           "ffn2_w_down")
BIG = (("ffn1_w_gate", True), ("ffn1_w_up", True), ("ffn1_w_down", False), ("w_in", True), ("w_out", False),
       ("ffn2_w_gate", True), ("ffn2_w_up", True), ("ffn2_w_down", False))


def kernel(x, ffn1_norm, ffn1_w_gate, ffn1_w_up, ffn1_w_down, mix_norm, w_in, q_norm, k_norm, attn_sinks, rel_bias, pool_w, pool_scale, w_out, ffn2_norm, ffn2_w_gate, ffn2_w_up, ffn2_w_down, loss_target, m_ffn1_norm, m_ffn1_w_gate, m_ffn1_w_up, m_ffn1_w_down, m_mix_norm, m_w_in, m_q_norm, m_k_norm, m_attn_sinks, m_rel_bias, m_pool_w, m_pool_scale, m_w_out, m_ffn2_norm, m_ffn2_w_gate, m_ffn2_w_up, m_ffn2_w_down, v_ffn1_norm, v_ffn1_w_gate, v_ffn1_w_up, v_ffn1_w_down, v_mix_norm, v_w_in, v_q_norm, v_k_norm, v_attn_sinks, v_rel_bias, v_pool_w, v_pool_scale, v_w_out, v_ffn2_norm, v_ffn2_w_gate, v_ffn2_w_up, v_ffn2_w_down):
    args = dict(locals())
    w = {n: args[n] for n in WEIGHTS}
    m = {n: args["m_" + n] for n in WEIGHTS}
    v = {n: args["v_" + n] for n in WEIGHTS}

    as_rows = lambda a, tr: jnp.swapaxes(a, 1, 2) if tr else a
    shard = jnp.concatenate([as_rows(w[n], tr)[0].astype(bf16) for n, tr in BIG], axis=0)
    exchanges = _GatheredWeights(shard, x[0], ffn1_norm)
    (dh1, dx1), (dw1, _, _, _), small = _local_step(
        x[0], loss_target[0], exchanges, ffn1_norm, mix_norm, ffn2_norm, q_norm, k_norm, attn_sinks,
        rel_bias, pool_w[0], pool_scale)

    nrows1 = len(G1_PIECES) * FS
    for_x, for_y, own1, small_tot = _reduce_scatter_ffn1_head(dw1, _pack_small(small))
    ssem, rsem, for_x, for_y, from_x, from_y, g1, small_tot = _rs1_tail_start(
        for_x, for_y, lax.empty((nrows1, D), bf16), lax.empty((nrows1, D), bf16), ffn1_norm, small_tot)
    gx, _, _ = _norm_bwd(dh1, x[0], g1, dx1, 1.0, "norm1_bwd")
    red_rest = exchanges.mix_ffn2_grads_total(gx)

    grads, deltas, new_m, new_v = {}, {}, {}, {}
    rest = [k for k in range(len(BIG)) if k not in G1_PIECES]
    rows_of = lambda t, ks: [as_rows(t[BIG[k][0]], BIG[k][1]) for k in ks]
    rest_out = _adamw_big(rest, rows_of(w, rest), rows_of(m, rest), rows_of(v, rest), red_rest, "adamw_rest")
    from_x, from_y = _rs1_tail_wait(ssem, rsem, for_x, for_y, from_x, from_y, rest_out[0][0])
    red1 = _rs1_total(own1, from_x, from_y)
    ffn1 = list(G1_PIECES)
    ffn1_out = _adamw_big(ffn1, rows_of(w, ffn1), rows_of(m, ffn1), rows_of(v, ffn1), red1, "adamw_ffn1")
    for ks, out in ((rest, rest_out), (ffn1, ffn1_out)):
        for i, k in enumerate(ks):
            n, tr = BIG[k]
            grads[n], deltas[n], new_m[n], new_v[n] = [as_rows(o[i], tr) for o in out]
    small_names = [n for n in SMALL_NAMES if n != "loss"]
    for n in small_names:
        grads[n] = _unpack_small(small_tot, n)
    ds, nms, nvs = _adamw_small([w[n] for n in small_names], [m[n] for n in small_names], [v[n] for n in small_names],
                                [grads[n] for n in small_names], "adamw_small")
    for i, n in enumerate(small_names):
        deltas[n], new_m[n], new_v[n] = ds[i], nms[i], nvs[i]
    loss = small_tot[LOSS_ROW, 0]
    return (loss, gx[None], *[grads[n] for n in WEIGHTS], *[deltas[n] for n in WEIGHTS],
            *[new_m[n] for n in WEIGHTS], *[new_v[n] for n in WEIGHTS])
```

```python
import jax
import jax.numpy as jnp
import numpy as np
from jax import lax
from jax.experimental import pallas as pl
from jax.experimental.pallas import tpu as pltpu

f32, bf16, i32 = jnp.float32, jnp.bfloat16, jnp.int32
SDS = jax.ShapeDtypeStruct

D = 1024
F = 2816
HD = 64
NH = 8
NKV = 2
GQA = NH // NKV
DATTN = NH * HD
DKV = NKV * HD
DPOOL = 512
POOL_WINDOWS = (2, 4, 8, 16)
PGD = DPOOL // len(POOL_WINDOWS)
DIN = DATTN + 2 * DKV + DPOOL
DMIX = DATTN + DPOOL
BLK = 128
NBUCK = 32
MAX_DISTANCE = 128
EPS = 1e-6
NEG = -1e30
SCALE = HD ** -0.5

ADAM_LR, ADAM_B1, ADAM_B2, ADAM_EPS, ADAM_WD, ADAM_STEP = 0.001, 0.9, 0.999, 1e-08, 0.01, 10

NDEV = 8
FS = F // NDEV
INS = DIN // NDEV
OUTS = DMIX // NDEV
PIECE_ROWS = (FS, FS, FS, INS, OUTS, FS, FS, FS)
PIECE_OFF = tuple(int(v) for v in np.cumsum((0,) + PIECE_ROWS[:-1]))
PACK_ROWS = sum(PIECE_ROWS)

VMEM_LIMIT_V7X = 56 * 1024 * 1024

MESH = pl.DeviceIdType.MESH


def _cparams(sem=None, vmem=None):
    return pltpu.CompilerParams(dimension_semantics=sem, vmem_limit_bytes=vmem)


def _nt(a, b):
    return lax.dot_general(a, b, (((1,), (1,)), ((), ())), preferred_element_type=f32)


def _tn(a, b):
    return lax.dot_general(a, b, (((0,), (0,)), ((), ())), preferred_element_type=f32)


def _nn(a, b):
    return jnp.dot(a, b, preferred_element_type=f32)


def _sigmoid(x):
    return 1.0 / (1.0 + jnp.exp(-x))


def _norm_fwd(x, g, name):
    T = x.shape[0]
    tm = min(512, T)

    def body(x_ref, g_ref, h_ref):
        xv = x_ref[...]
        r = lax.rsqrt(jnp.mean(xv * xv, axis=-1, keepdims=True) + EPS)
        h_ref[...] = (xv * r * g_ref[...]).astype(bf16)

    return pl.pallas_call(
        body, grid=(T // tm,),
        in_specs=[pl.BlockSpec((tm, D), lambda i: (i, 0)), pl.BlockSpec((1, D), lambda i: (0, 0))],
        out_specs=pl.BlockSpec((tm, D), lambda i: (i, 0)),
        out_shape=SDS((T, D), bf16), name=name)(x, g)


def _norm_bwd(dh, x, g, dres, out_scale, name):
    T = x.shape[0]
    tm = min(512, T)

    def body(dh_ref, x_ref, g_ref, dr_ref, dx_ref, dxb_ref, dg_ref):
        i = pl.program_id(0)
        xv = x_ref[...]
        r = lax.rsqrt(jnp.mean(xv * xv, axis=-1, keepdims=True) + EPS)
        xh = xv * r
        dhv = dh_ref[...]
        dxh = dhv * g_ref[...]
        dx = dr_ref[...] + r * (dxh - xh * jnp.mean(dxh * xh, axis=-1, keepdims=True))
        dx_ref[...] = dx
        dxb_ref[...] = (out_scale * dx).astype(bf16)
        dg = jnp.sum(dhv * xh, axis=0, keepdims=True)

        @pl.when(i == 0)
        def _():
            dg_ref[...] = dg

        @pl.when(i > 0)
        def _():
            dg_ref[...] += dg

    tok = pl.BlockSpec((tm, D), lambda i: (i, 0))
    vec = pl.BlockSpec((1, D), lambda i: (0, 0))
    return pl.pallas_call(
        body, grid=(T // tm,),
        in_specs=[tok, tok, vec, tok], out_specs=[tok, tok, vec],
        out_shape=(SDS((T, D), f32), SDS((T, D), bf16), SDS((1, D), f32)),
        compiler_params=_cparams(("arbitrary",)), name=name)(dh, x, g, dres)


def _gain_grad(dh, x, name):
    T = x.shape[0]
    tm = min(512, T)

    def body(dh_ref, x_ref, dg_ref):
        i = pl.program_id(0)
        xv = x_ref[...]
        r = lax.rsqrt(jnp.mean(xv * xv, axis=-1, keepdims=True) + EPS)
        dg = jnp.sum(dh_ref[...] * (xv * r), axis=0, keepdims=True)

        @pl.when(i == 0)
        def _():
            dg_ref[...] = dg

        @pl.when(i > 0)
        def _():
            dg_ref[...] += dg

    tok = pl.BlockSpec((tm, D), lambda i: (i, 0))
    return pl.pallas_call(
        body, grid=(T // tm,), in_specs=[tok, tok], out_specs=pl.BlockSpec((1, D), lambda i: (0, 0)),
        out_shape=SDS((1, D), f32), compiler_params=_cparams(("arbitrary",)), name=name)(dh, x)


FFN_ROW_CHUNK = 256


def _ffn_tiles(T):
    return min(1024, T), 256


def _ffn_fwd(h, w, x, target, name):
    T = h.shape[0]
    tm, tf = _ffn_tiles(T)
    nf = F // tf
    with_loss = target is not None

    def body(*refs):
        if with_loss:
            h_ref, w_ref, x_hbm, t_hbm, xo_ref, g_ref, u_ref, dyb_ref, loss_ref, tbuf, sem = refs
        else:
            h_ref, w_ref, x_hbm, xo_ref, g_ref, u_ref, sem = refs
        fi = pl.program_id(0)

        @pl.when(fi == 0)
        def _():
            cp = pltpu.make_async_copy(x_hbm, xo_ref, sem)
            cp.start()
            cp.wait()

        wgu = w_ref[0:2].reshape(2 * tf, D)
        for r in range(0, T, tm):
            rows = slice(r, r + tm)
            gu = _nt(h_ref[rows, :], wgu)
            gate, up = gu[:, :tf], gu[:, tf:]
            act = gate * _sigmoid(gate) * up
            g_ref[0, rows, :] = gate.astype(bf16)
            u_ref[0, rows, :] = up.astype(bf16)
            xo_ref[rows, :] += _nn((0.5 * act).astype(bf16), w_ref[2])

        if with_loss:
            @pl.when(fi == nf - 1)
            def _():
                lanes = jnp.zeros((1, 128), f32)
                for r in range(0, T, tm):
                    rows = slice(r, r + tm)
                    cp = pltpu.make_async_copy(t_hbm.at[pl.ds(r, tm), :], tbuf, sem)
                    cp.start()
                    cp.wait()
                    e = xo_ref[rows, :] - tbuf[...]
                    dy = e * (1.0 / D)
                    xo_ref[rows, :] = dy
                    dyb_ref[rows, :] = (0.5 * dy).astype(bf16)
                    col = jnp.sum(e * e, axis=0, keepdims=True) * (0.5 / D)
                    for k in range(D // 128):
                        lanes = lanes + col[:, 128 * k:128 * (k + 1)]
                loss_ref[...] = lanes

    tok = pl.BlockSpec((T, D), lambda f: (0, 0))
    act_spec = pl.BlockSpec((1, T, tf), lambda f: (f, 0, 0))
    hbm = pl.BlockSpec(memory_space=pl.ANY)
    in_specs = [tok, pl.BlockSpec((3, tf, D), lambda f: (0, f, 0)), hbm]
    out_specs = [tok, act_spec, act_spec]
    out_shape = [SDS((T, D), f32), SDS((nf, T, tf), bf16), SDS((nf, T, tf), bf16)]
    scratch = [pltpu.SemaphoreType.DMA]
    args = [h, w, x]
    if with_loss:
        in_specs.append(hbm)
        args.append(target)
        out_specs += [tok, pl.BlockSpec((1, 128), lambda f: (0, 0))]
        out_shape += [SDS((T, D), bf16), SDS((1, 128), f32)]
        scratch = [pltpu.VMEM((tm, D), f32)] + scratch
    return pl.pallas_call(
        body, grid=(nf,), in_specs=in_specs, out_specs=out_specs, out_shape=tuple(out_shape), scratch_shapes=scratch,
        compiler_params=_cparams(("arbitrary",), VMEM_LIMIT_V7X), name=name)(*args)


def _ffn_bwd(dob, h, gate, up, w, name):
    T = h.shape[0]
    _, tf = _ffn_tiles(T)
    nf = F // tf

    def body(do_hbm, h_hbm, g_ref, u_ref, w_ref, dh_hbm, dw_ref, do_v, h_v, dh_acc, dgu_s, act_s, sems):
        fi = pl.program_id(0)

        @pl.when(fi == 0)
        def _():
            loads = [pltpu.make_async_copy(do_hbm, do_v, sems.at[0]), pltpu.make_async_copy(h_hbm, h_v, sems.at[1])]
            for cp in loads:
                cp.start()
            dh_acc[...] = jnp.zeros_like(dh_acc)
            for cp in loads:
                cp.wait()

        wgu = w_ref[0:2].reshape(2 * tf, D)
        for r in range(0, T, FFN_ROW_CHUNK):
            rows = slice(r, r + FFN_ROW_CHUNK)
            dov = do_v[rows, :]
            gv = g_ref[0, rows, :].astype(f32)
            uv = u_ref[0, rows, :].astype(f32)
            sg = _sigmoid(gv)
            sil = gv * sg
            dact = _nt(dov, w_ref[2])
            dup = dact * sil
            dgate = dact * uv * (sg * (1.0 + gv * (1.0 - sg)))
            dgu = jnp.concatenate([dgate.astype(bf16), dup.astype(bf16)], axis=1)
            dgu_s[rows, :] = dgu
            act_s[rows, :] = (sil * uv).astype(bf16)
            dh_acc[rows, :] += _nn(dgu, wgu)
        dw_ref[0:2] = _tn(dgu_s[...], h_v[...]).reshape(2, tf, D).astype(bf16)
        dw_ref[2] = _tn(act_s[...], do_v[...]).astype(bf16)

        @pl.when(fi == nf - 1)
        def _():
            out = pltpu.make_async_copy(dh_acc, dh_hbm, sems.at[0])
            out.start()
            out.wait()

    act_spec = pl.BlockSpec((1, T, tf), lambda f: (f, 0, 0))
    wspec = pl.BlockSpec((3, tf, D), lambda f: (0, f, 0))
    hbm = pl.BlockSpec(memory_space=pl.ANY)
    return pl.pallas_call(
        body, grid=(nf,),
        in_specs=[hbm, hbm, act_spec, act_spec, wspec],
        out_specs=[hbm, wspec],
        out_shape=(SDS((T, D), f32), SDS((3, F, D), bf16)),
        scratch_shapes=[pltpu.VMEM((T, D), bf16), pltpu.VMEM((T, D), bf16), pltpu.VMEM((T, D), f32),
                        pltpu.VMEM((T, 2 * tf), bf16), pltpu.VMEM((T, tf), bf16), pltpu.SemaphoreType.DMA((2,))],
        compiler_params=_cparams(("arbitrary",), VMEM_LIMIT_V7X), name=name)(dob, h, gate, up, w)


def _in_proj_fwd(h, wint, name):
    T = h.shape[0]
    tm = min(512, T)

    def body(h_ref, w_ref, z_ref):
        z_ref[...] = _nt(h_ref[...], w_ref[...])

    return pl.pallas_call(
        body, grid=(T // tm,),
        in_specs=[pl.BlockSpec((tm, D), lambda i: (i, 0)), pl.BlockSpec((DIN, D), lambda i: (0, 0))],
        out_specs=pl.BlockSpec((tm, DIN), lambda i: (i, 0)),
        out_shape=SDS((T, DIN), f32), name=name)(h, wint)


def _in_proj_bwd(dz, wint, h, name):
    T = h.shape[0]
    tm = min(512, T)
    nt = T // tm

    def body(dz_ref, w_ref, h_ref, dh_ref, dw_ref, dz_all):
        i = pl.program_id(0)
        dzb = dz_ref[...].astype(bf16)
        dz_all[pl.ds(pl.multiple_of(i * tm, tm), tm), :] = dzb
        dh_ref[...] = _nn(dzb, w_ref[...])

        @pl.when(i == nt - 1)
        def _():
            dw_ref[...] = _tn(dz_all[...], h_ref[...]).astype(bf16)

    wspec = pl.BlockSpec((DIN, D), lambda i: (0, 0))
    return pl.pallas_call(
        body, grid=(nt,),
        in_specs=[pl.BlockSpec((tm, DIN), lambda i: (i, 0)), wspec, pl.BlockSpec((T, D), lambda i: (0, 0))],
        out_specs=[pl.BlockSpec((tm, D), lambda i: (i, 0)), wspec],
        out_shape=(SDS((T, D), f32), SDS((DIN, D), bf16)),
        scratch_shapes=[pltpu.VMEM((T, DIN), bf16)],
        compiler_params=_cparams(("arbitrary",), VMEM_LIMIT_V7X), name=name)(dz, wint, h)


def _out_proj_fwd(ymix, wout, x, g, name):
    T = x.shape[0]
    tm = min(512, T)

    def body(y_ref, w_ref, x_ref, g_ref, o_ref, h_ref):
        o = x_ref[...] + _nn(y_ref[...], w_ref[...])
        o_ref[...] = o
        r = lax.rsqrt(jnp.mean(o * o, axis=-1, keepdims=True) + EPS)
        h_ref[...] = (o * r * g_ref[...]).astype(bf16)

    tok = pl.BlockSpec((tm, D), lambda i: (i, 0))
    return pl.pallas_call(
        body, grid=(T // tm,),
        in_specs=[pl.BlockSpec((tm, DMIX), lambda i: (i, 0)), pl.BlockSpec((DMIX, D), lambda i: (0, 0)), tok,
                  pl.BlockSpec((1, D), lambda i: (0, 0))],
        out_specs=[tok, tok], out_shape=(SDS((T, D), f32), SDS((T, D), bf16)), name=name)(ymix, wout, x, g)


def _out_proj_bwd(dxb, wout, ymix, name):
    T = dxb.shape[0]
    tm = min(512, T)
    nt = T // tm

    def body(dx_ref, w_ref, y_ref, dy_ref, dw_ref):
        i = pl.program_id(0)
        rows = pl.ds(pl.multiple_of(i * tm, tm), tm)
        dy_ref[...] = _nt(dx_ref[rows, :], w_ref[...])

        @pl.when(i == nt - 1)
        def _():
            dw_ref[...] = _tn(y_ref[...], dx_ref[...]).astype(bf16)

    wspec = pl.BlockSpec((DMIX, D), lambda i: (0, 0))
    return pl.pallas_call(
        body, grid=(nt,),
        in_specs=[pl.BlockSpec((T, D), lambda i: (0, 0)), wspec, pl.BlockSpec((T, DMIX), lambda i: (0, 0))],
        out_specs=[pl.BlockSpec((tm, DMIX), lambda i: (i, 0)), wspec],
        out_shape=(SDS((T, DMIX), f32), SDS((DMIX, D), bf16)),
        compiler_params=_cparams(("arbitrary",), VMEM_LIMIT_V7X), name=name)(dxb, wout, ymix)


def _t5_bucket_table():
    ql = np.arange(BLK)[:, None]
    kl = np.arange(2 * BLK)[None, :]
    n = np.maximum(ql + BLK - kl, 0)
    max_exact = NBUCK // 2
    large = max_exact + (np.log(np.maximum(n, 1) / max_exact) / np.log(MAX_DISTANCE / max_exact)
                         * (NBUCK - max_exact)).astype(np.int32)
    large = np.minimum(large, NBUCK - 1)
    return np.where(n < max_exact, n, large).astype(np.int32)


def _fill_bias(bk_ref, rb_ref, bias_scr):
    bk = bk_ref[...]
    for h in range(NH):
        def step(b, acc, h=h):
            return acc + jnp.where(bk == b, rb_ref[b, h], 0.0)
        bias_scr[h] = lax.fori_loop(0, NBUCK, step, jnp.zeros((BLK, 2 * BLK), f32))


MIX_SUB = 4


class _Window:
    def __init__(self, zc_ref, zp_ref, n, s):
        self.blk = n * MIX_SUB + s
        self.cur = lambda a, b: zc_ref[s * BLK:(s + 1) * BLK, a:b]
        self.prev = (lambda a, b: zp_ref[:, a:b]) if s == 0 else (lambda a, b: zc_ref[(s - 1) * BLK:s * BLK, a:b])


def _attn_qkv(win, kh, qg, kg):
    kc = DATTN + HD * kh
    vc = DATTN + DKV + HD * kh
    kx = jnp.concatenate([win.prev(kc, kc + HD), win.cur(kc, kc + HD)], axis=0)
    vx = jnp.concatenate([win.prev(vc, vc + HD), win.cur(vc, vc + HD)], axis=0)
    qx = jnp.concatenate([win.cur(HD * (GQA * kh + g), HD * (GQA * kh + g + 1)) for g in range(GQA)], axis=0)
    rq = lax.rsqrt(jnp.mean(qx * qx, axis=-1, keepdims=True) + EPS)
    rk = lax.rsqrt(jnp.mean(kx * kx, axis=-1, keepdims=True) + EPS)
    qhat, khat = qx * rq, kx * rk
    return dict(qhat=qhat, khat=khat, rq=rq, rk=rk, qsb=(qhat * (qg * SCALE)).astype(bf16),
                knb=(khat * kg).astype(bf16), vb=vx.astype(bf16))


def _window_masks(n):
    row = lax.broadcasted_iota(i32, (GQA * BLK, 2 * BLK), 0) & (BLK - 1)
    col = lax.broadcasted_iota(i32, (GQA * BLK, 2 * BLK), 1)
    band = (col > row) & (col <= row + BLK)
    return band & ((col >= BLK) | (n > 0)), band


def _attn_probs(a, kh, sk_ref, bias_scr, mask):
    s = _nt(a["qsb"], a["knb"]) + bias_scr[GQA * kh:GQA * (kh + 1)].reshape(GQA * BLK, 2 * BLK)
    s = jnp.where(mask, s, NEG)
    ridx = lax.broadcasted_iota(i32, (GQA * BLK, 1), 0)
    sink = jnp.full((GQA * BLK, 1), sk_ref[GQA * kh + GQA - 1], f32)
    for g in range(GQA - 2, -1, -1):
        sink = jnp.where(ridx < (g + 1) * BLK, sk_ref[GQA * kh + g], sink)
    m = jnp.maximum(jnp.max(s, axis=-1, keepdims=True), sink)
    e = jnp.exp(s - m)
    den = jnp.sum(e, axis=-1, keepdims=True) + jnp.exp(sink - m)
    return e / den


POOL_STEPS = {2: (1,), 4: (1, 2), 8: (1, 2, 4), 16: (1, 2, 4, 8)}


def _pool_group(win, g, w):
    n = win.blk
    c0 = DATTN + 2 * DKV + PGD * g
    uc = win.cur(c0, c0 + PGD)
    up = jnp.where(n > 0, win.prev(c0, c0 + PGD), 0.0)
    sm = jnp.concatenate([up, uc], axis=0)
    for k in POOL_STEPS[w]:
        sm = sm + pltpu.roll(sm, k, axis=0)
    pos = n * BLK + lax.broadcasted_iota(i32, (BLK, 1), 0) + 1
    cnt = jnp.minimum(pos, w).astype(f32)
    return sm[BLK:2 * BLK] / cnt - uc, cnt


def _mix_fwd(z, qg, kg, sinks, relb, bucket, pool_w, pscale, name):
    T = z.shape[0]
    step_rows = MIX_SUB * BLK
    nsteps = T // step_rows

    def body(zc_ref, zp_ref, qg_ref, kg_ref, sk_ref, rb_ref, bk_ref, pw_ref, ps_ref, y_ref, p_ref, bias_scr, yacc):
        n = pl.program_id(0)

        @pl.when(n == 0)
        def _():
            _fill_bias(bk_ref, rb_ref, bias_scr)

        first_mask, mask = _window_masks(n)
        for s in range(MIX_SUB):
            win = _Window(zc_ref, zp_ref, n, s)
            rows = slice(s * BLK, (s + 1) * BLK)
            for kh in range(NKV):
                a = _attn_qkv(win, kh, qg_ref[...], kg_ref[...])
                pb = _attn_probs(a, kh, sk_ref, bias_scr, first_mask if s == 0 else mask).astype(bf16)
                p_ref[s, GQA * kh:GQA * (kh + 1)] = pb.reshape(GQA, BLK, 2 * BLK)
                o = _nn(pb, a["vb"])
                for g in range(GQA):
                    hc = HD * (GQA * kh + g)
                    yacc[rows, hc:hc + HD] = o[g * BLK:(g + 1) * BLK]
            for g, w in enumerate(POOL_WINDOWS):
                pooled, _ = _pool_group(win, g, w)
                yp = _nn(pooled.astype(bf16), pw_ref[g].astype(bf16)) * ps_ref[:, PGD * g:PGD * (g + 1)]
                yacc[rows, DATTN + PGD * g:DATTN + PGD * (g + 1)] = yp
        y_ref[...] = yacc[...].astype(bf16)

    full = lambda *shape: pl.BlockSpec(shape, lambda n: (0,) * len(shape))
    smem = pl.BlockSpec(memory_space=pltpu.SMEM)
    return pl.pallas_call(
        body, grid=(nsteps,),
        in_specs=[pl.BlockSpec((step_rows, DIN), lambda n: (n, 0)),
                  pl.BlockSpec((BLK, DIN), lambda n: (jnp.maximum(n * MIX_SUB - 1, 0), 0)),
                  full(1, HD), full(1, HD), smem, smem, full(BLK, 2 * BLK),
                  full(len(POOL_WINDOWS), PGD, PGD), full(1, DPOOL)],
        out_specs=[pl.BlockSpec((step_rows, DMIX), lambda n: (n, 0)),
                   pl.BlockSpec((MIX_SUB, NH, BLK, 2 * BLK), lambda n: (n, 0, 0, 0))],
        out_shape=(SDS((T, DMIX), bf16), SDS((T // BLK, NH, BLK, 2 * BLK), bf16)),
        scratch_shapes=[pltpu.VMEM((NH, BLK, 2 * BLK), f32), pltpu.VMEM((step_rows, DMIX), f32)],
        compiler_params=_cparams(("arbitrary",)), name=name)(z, z, qg, kg, sinks, relb, bucket, pool_w, pscale)


def _mix_bwd(z, dy, probs, qg, kg, relb, bucket, pool_w, pscale, name):
    T = z.shape[0]
    step_rows = MIX_SUB * BLK
    nsteps = T // step_rows

    def body(zc_ref, zp_ref, dy_ref, p_ref, qg_ref, kg_ref, bk_ref, pw_ref, ps_ref,
             dz_ref, dqg_ref, dkg_ref, dsk_ref, drb_ref, dpw_ref, dps_ref, dbias_scr):
        n = pl.program_id(0)

        @pl.when(n == 0)
        def _():
            dbias_scr[...] = jnp.zeros_like(dbias_scr)
            dqg_ref[...] = jnp.zeros_like(dqg_ref)
            dkg_ref[...] = jnp.zeros_like(dkg_ref)
            dpw_ref[...] = jnp.zeros_like(dpw_ref)
            dps_ref[...] = jnp.zeros_like(dps_ref)

        qg, kg = qg_ref[...], kg_ref[...]
        for s in range(MIX_SUB):
            win = _Window(zc_ref, zp_ref, n, s)
            blk = win.blk
            rows = pl.ds(pl.multiple_of(blk * BLK, BLK), BLK)
            prow = pl.ds(pl.multiple_of(jnp.maximum(blk - 1, 0) * BLK, BLK), BLK)
            dyr = slice(s * BLK, (s + 1) * BLK)

            def into_prev(fn, s=s):
                if s == 0:
                    pl.when(n > 0)(fn)
                else:
                    fn()

            for kh in range(NKV):
                a = _attn_qkv(win, kh, qg, kg)
                pb = p_ref[s, GQA * kh:GQA * (kh + 1)].reshape(GQA * BLK, 2 * BLK)
                p = pb.astype(f32)
                do = jnp.concatenate([dy_ref[dyr, HD * (GQA * kh + g):HD * (GQA * kh + g + 1)] for g in range(GQA)],
                                     axis=0).astype(bf16)
                dv = _tn(pb, do)
                dp = _nt(do, a["vb"])
                delta = jnp.sum(p * dp, axis=-1, keepdims=True)
                ds = p * (dp - delta)
                for g in range(GQA):
                    dbias_scr[GQA * kh + g] += ds[g * BLK:(g + 1) * BLK]
                dsb = ds.astype(bf16)
                dqn = _nn(dsb, a["knb"]) * SCALE
                dkn = _tn(dsb, a["qsb"])
                qhat, khat = a["qhat"], a["khat"]
                dqg_ref[...] += jnp.sum(dqn * qhat, axis=0, keepdims=True)
                dkg_ref[...] += jnp.sum(dkn * khat, axis=0, keepdims=True)
                dqh = dqn * qg
                dq = a["rq"] * (dqh - qhat * jnp.mean(dqh * qhat, axis=-1, keepdims=True))
                dkh = dkn * kg
                dk = a["rk"] * (dkh - khat * jnp.mean(dkh * khat, axis=-1, keepdims=True))
                kc = DATTN + HD * kh
                vc = DATTN + DKV + HD * kh
                for g in range(GQA):
                    hc = HD * (GQA * kh + g)
                    dz_ref[rows, hc:hc + HD] = dq[g * BLK:(g + 1) * BLK]
                dz_ref[rows, kc:kc + HD] = dk[BLK:2 * BLK]
                dz_ref[rows, vc:vc + HD] = dv[BLK:2 * BLK]

                def kv_prev(dk=dk, dv=dv, kc=kc, vc=vc, prow=prow):
                    dz_ref[prow, kc:kc + HD] += dk[0:BLK]
                    dz_ref[prow, vc:vc + HD] += dv[0:BLK]

                into_prev(kv_prev)

            for g, w in enumerate(POOL_WINDOWS):
                c0 = DATTN + 2 * DKV + PGD * g
                pooled, cnt = _pool_group(win, g, w)
                pb = pooled.astype(bf16)
                wb = pw_ref[g].astype(bf16)
                dyp = dy_ref[dyr, DATTN + PGD * g:DATTN + PGD * (g + 1)]
                ypre = _nn(pb, wb)
                dps_ref[:, PGD * g:PGD * (g + 1)] += jnp.sum(dyp * ypre, axis=0, keepdims=True)
                dyg = (dyp * ps_ref[:, PGD * g:PGD * (g + 1)]).astype(bf16)
                dpw_ref[g] += _tn(pb, dyg)
                dpooled = _nt(dyg, wb)
                due = jnp.concatenate([jnp.zeros((BLK, PGD), f32), dpooled / cnt], axis=0)
                for k in POOL_STEPS[w]:
                    due = due + pltpu.roll(due, 2 * BLK - k, axis=0)
                dz_ref[rows, c0:c0 + PGD] = due[BLK:2 * BLK] - dpooled

                def pool_prev(due=due, c0=c0, prow=prow):
                    dz_ref[prow, c0:c0 + PGD] += due[0:BLK]

                into_prev(pool_prev)

        @pl.when(n == nsteps - 1)
        def _():
            bk = bk_ref[...]
            ri = lax.broadcasted_iota(i32, (NBUCK, NH), 0)
            ci = lax.broadcasted_iota(i32, (NBUCK, NH), 1)

            def step(b, acc):
                for h in range(NH):
                    sel = jnp.where(bk == b, dbias_scr[h], 0.0)
                    tot = jnp.sum(jnp.sum(sel, axis=1, keepdims=True), axis=0, keepdims=True)
                    acc = acc + jnp.where((ri == b) & (ci == h), tot, 0.0)
                return acc

            drb_ref[...] = lax.fori_loop(0, NBUCK, step, jnp.zeros((NBUCK, NH), f32))
            lane = lax.broadcasted_iota(i32, (1, 128), 1)
            dsk = jnp.zeros((1, 128), f32)
            for h in range(NH):
                tot = jnp.sum(jnp.sum(dbias_scr[h], axis=1, keepdims=True), axis=0, keepdims=True)
                dsk = dsk - jnp.where(lane == h, tot, 0.0)
            dsk_ref[...] = dsk

    full = lambda *shape: pl.BlockSpec(shape, lambda n: (0,) * len(shape))
    npg = len(POOL_WINDOWS)
    return pl.pallas_call(
        body, grid=(nsteps,),
        in_specs=[pl.BlockSpec((step_rows, DIN), lambda n: (n, 0)),
                  pl.BlockSpec((BLK, DIN), lambda n: (jnp.maximum(n * MIX_SUB - 1, 0), 0)),
                  pl.BlockSpec((step_rows, DMIX), lambda n: (n, 0)),
                  pl.BlockSpec((MIX_SUB, NH, BLK, 2 * BLK), lambda n: (n, 0, 0, 0)),
                  full(1, HD), full(1, HD), full(BLK, 2 * BLK), full(npg, PGD, PGD), full(1, DPOOL)],
        out_specs=[full(T, DIN), full(1, HD), full(1, HD), full(1, 128), full(NBUCK, NH),
                   full(npg, PGD, PGD), full(1, DPOOL)],
        out_shape=(SDS((T, DIN), f32), SDS((1, HD), f32), SDS((1, HD), f32), SDS((1, 128), f32),
                   SDS((NBUCK, NH), f32), SDS((npg, PGD, PGD), f32), SDS((1, DPOOL), f32)),
        scratch_shapes=[pltpu.VMEM((NH, BLK, 2 * BLK), f32)],
        compiler_params=_cparams(("arbitrary",), VMEM_LIMIT_V7X),
        name=name)(z, z, dy, probs, qg, kg, bucket, pool_w, pscale)


class _LocalWeights:
    def __init__(self, w1, wint, wout, w2):
        self.w1, self.wint, self.wout, self.w2 = w1, wint, wout, w2

    def ffn1(self):
        return self.w1

    def first_norm(self, x, gain):
        return _norm_fwd(x, gain, "norm1_fwd")

    def after_ffn1(self, gain, x1):
        return gain

    def mix(self, after):
        return self.wint, self.wout

    def before_out_proj(self, wout, after):
        return wout

    def ffn2(self, after):
        return self.w2

    def mix_ffn2_grads_ready(self, dwint, dwout, dw2, dh2):
        self.grads_rest = (dwint, dwout, dw2)
        return dh2

    def before_ffn1_bwd(self, dx1b):
        return dx1b


def _local_step(x, target, weights, g1, gm, g3, qg, kg, sinks, relb, pool_w, pscale):
    bucket = jnp.asarray(_t5_bucket_table())
    sk = sinks.reshape(NH)
    w1 = weights.ffn1()
    h1 = weights.first_norm(x, g1)
    x1, gate1, up1 = _ffn_fwd(h1, w1, x, None, "ffn1_fwd")
    h2 = _norm_fwd(x1, weights.after_ffn1(gm, x1), "norm2_fwd")
    wint, wout = weights.mix(h2)
    z = _in_proj_fwd(h2, wint, "in_proj_fwd")
    ymix, probs = _mix_fwd(z, qg, kg, sk, relb, bucket, pool_w, pscale, "mix_fwd")
    wout = weights.before_out_proj(wout, ymix)
    x2, h3 = _out_proj_fwd(ymix, wout, x1, g3, "out_proj_fwd")
    w2 = weights.ffn2(h3)
    dy, gate2, up2, dyb, loss_lanes = _ffn_fwd(h3, w2, x2, target, "ffn2_fwd")

    dh3, dw2 = _ffn_bwd(dyb, h3, gate2, up2, w2, "ffn2_bwd")
    dx2, dx2b, dg3 = _norm_bwd(dh3, x2, g3, dy, 1.0, "norm3_bwd")
    dymix, dwout = _out_proj_bwd(dx2b, wout, ymix, "out_proj_bwd")
    dz, dqg, dkg, dsk, drb, dpw, dps = _mix_bwd(z, dymix, probs, qg, kg, relb, bucket, pool_w, pscale, "mix_bwd")
    dh2, dwint = _in_proj_bwd(dz, wint, h2, "in_proj_bwd")
    dh2 = weights.mix_ffn2_grads_ready(dwint, dwout, dw2, dh2)
    dx1, dx1b, dgm = _norm_bwd(dh2, x1, gm, dx2, 0.5, "norm2_bwd")
    dx1b = weights.before_ffn1_bwd(dx1b)
    dh1, dw1 = _ffn_bwd(dx1b, h1, gate1, up1, w1, "ffn1_bwd")
    dg1 = _gain_grad(dh1, x, "norm1_gain_grad")
    small = dict(ffn1_norm=dg1, mix_norm=dgm, ffn2_norm=dg3, pool_scale=dps, q_norm=dqg, k_norm=dkg,
                 attn_sinks=dsk[:, :NH], rel_bias=drb, pool_w=dpw, loss=loss_lanes)
    return (dh1, dx1), (dw1, dwint, dwout, dw2), small


SMALL_NAMES = ("ffn1_norm", "mix_norm", "ffn2_norm", "pool_scale", "q_norm", "k_norm", "attn_sinks", "rel_bias",
               "pool_w", "loss")
SMALL_SHAPES = dict(ffn1_norm=(1, D), mix_norm=(1, D), ffn2_norm=(1, D), pool_scale=(1, DPOOL), q_norm=(1, HD),
                    k_norm=(1, HD), attn_sinks=(1, NH), rel_bias=(NBUCK, NH),
                    pool_w=(1, len(POOL_WINDOWS), PGD, PGD), loss=(1, 128))


def _small_rows(name):
    return -(-int(np.prod(SMALL_SHAPES[name])) // 128)


SMALL_OFF = {}
_r = 0
for _n in SMALL_NAMES:
    SMALL_OFF[_n] = _r
    _r += _small_rows(_n)
SMALL_ROWS = -(-_r // 8) * 8
LOSS_ROW = SMALL_OFF["loss"]


def _pack_small(vals):
    parts = []
    for n in SMALL_NAMES:
        size = _small_rows(n) * 128
        if n in vals:
            flat = vals[n].astype(f32).reshape(-1)
            parts.append(jnp.pad(flat, (0, size - flat.shape[0])))
        else:
            parts.append(jnp.zeros((size,), f32))
    flat = jnp.concatenate(parts)
    flat = jnp.pad(flat, (0, SMALL_ROWS * 128 - flat.shape[0]))
    return flat.reshape(SMALL_ROWS, 128)


def _unpack_small(packed, name):
    size = int(np.prod(SMALL_SHAPES[name]))
    r0 = SMALL_OFF[name]
    return packed[r0:r0 + _small_rows(name)].reshape(-1)[:size].reshape(SMALL_SHAPES[name])


def _position():
    return lax.axis_index("x"), lax.axis_index("y"), lax.axis_index("c")


def _dev_index(x, y, c):
    return 4 * x + 2 * y + c


G1_PIECES, MIX_PIECES, F2_PIECES = (0, 1, 2), (3, 4), (5, 6, 7)


def _group_rows(pieces):
    return sum(PIECE_ROWS[k] for k in pieces)


def _shard_piece(s_ref, k):
    return s_ref.at[pl.ds(PIECE_OFF[k], PIECE_ROWS[k]), :]


def _shard_group(s_ref, pieces):
    return s_ref.at[pl.ds(PIECE_OFF[pieces[0]], _group_rows(pieces)), :]


def _weight_pieces(w1_ref=None, wi_ref=None, wo_ref=None, w2_ref=None):
    arrs = {}
    if w1_ref is not None:
        arrs.update({0: w1_ref.at[0], 1: w1_ref.at[1], 2: w1_ref.at[2]})
    if wi_ref is not None:
        arrs[3] = wi_ref
    if wo_ref is not None:
        arrs[4] = wo_ref
    if w2_ref is not None:
        arrs.update({5: w2_ref.at[0], 6: w2_ref.at[1], 7: w2_ref.at[2]})
    return arrs


def _block_rows(arrs, k, dev):
    r = PIECE_ROWS[k]
    return arrs[k].at[pl.ds(pl.multiple_of(_dev_index(*dev) * r, 16), r), :]


NORM_ROWS = 512


def _all_gather_ffn1(shard, x, gain):
    pieces = G1_PIECES
    rest_pieces = MIX_PIECES + F2_PIECES
    half = FS // 2
    T = x.shape[0]
    SIB, X0, X1, Y0, Y1, RELAY_Y, RELAY_X, ON_X, ON_Y, ON_D0, ON_D1 = range(11)

    def body(s_ref, x_ref, g_ref, w1_ref, h_ref, wi_ref, wo_ref, w2_ref, xbuf, hbuf, rest_buf,
             send_sems, recv_sems, local_sem, norm_sems):
        x, y, c = _position()
        me, sib = (x, y, c), (x, y, 1 - c)
        xn, yn, dg = (1 - x, y, c), (x, 1 - y, c), (1 - x, 1 - y, c)
        arrs = _weight_pieces(w1_ref=w1_ref)

        def place_rest():
            rest = _weight_pieces(wi_ref=wi_ref, wo_ref=wo_ref, w2_ref=w2_ref)
            grp = _shard_group(s_ref, rest_pieces)
            load = pltpu.make_async_copy(grp, rest_buf, norm_sems.at[0])
            load.start()
            load.wait()
            base = PIECE_OFF[rest_pieces[0]]
            for k in rest_pieces:
                pltpu.make_async_copy(rest_buf.at[pl.ds(PIECE_OFF[k] - base, PIECE_ROWS[k]), :],
                                      _block_rows(rest, k, me), norm_sems.at[1]).start()
            pltpu.make_async_copy(grp, rest_buf, norm_sems.at[1]).wait()

        def first_norm():
            for r in range(0, T, NORM_ROWS):
                load = pltpu.make_async_copy(x_ref.at[pl.ds(r, NORM_ROWS), :], xbuf, norm_sems.at[0])
                load.start()
                load.wait()
                xv = xbuf[...]
                rs = lax.rsqrt(jnp.mean(xv * xv, axis=-1, keepdims=True) + EPS)
                hbuf[...] = (xv * rs * g_ref[...]).astype(bf16)
                store = pltpu.make_async_copy(hbuf, h_ref.at[pl.ds(r, NORM_ROWS), :], norm_sems.at[1])
                store.start()
                store.wait()

        def rows_of(k, block, hf):
            r = PIECE_ROWS[k]
            start, size = (0, r) if hf is None else (hf * half, half)
            return arrs[k].at[pl.ds(pl.multiple_of(_dev_index(*block) * r + start, 16), size), :]

        def copies(rel, block, hf, to, from_shard=False):
            def src(k):
                if not from_shard:
                    return rows_of(k, block, hf)
                start, size = (0, PIECE_ROWS[k]) if hf is None else (hf * half, half)
                return s_ref.at[pl.ds(PIECE_OFF[k] + start, size), :]
            return [pltpu.make_async_remote_copy(
                src_ref=src(k), dst_ref=rows_of(k, block, hf), send_sem=send_sems.at[rel], recv_sem=recv_sems.at[rel],
                device_id=to, device_id_type=MESH) for k in pieces]

        def waiter(rel, hf):
            nrows = len(pieces) * (FS if hf is None else half)
            grp = s_ref.at[pl.ds(0, nrows), :]
            return pltpu.make_async_remote_copy(src_ref=grp, dst_ref=grp, send_sem=send_sems.at[rel],
                                                recv_sem=recv_sems.at[rel], device_id=me, device_id_type=MESH)

        def start(cps):
            for cp in cps:
                cp.start()

        mine = [pltpu.make_async_copy(_shard_piece(s_ref, k), _block_rows(arrs, k, me), local_sem) for k in pieces]
        start(mine)
        start(copies(SIB, me, None, sib, True))
        start(copies(X0, me, 0, xn, True))
        start(copies(Y1, me, 1, yn, True))
        start(copies(X1, me, 1, xn, True))
        start(copies(Y0, me, 0, yn, True))
        first_norm()
        place_rest()
        waiter(X0, 0).wait_recv()
        start(copies(RELAY_Y, xn, 0, yn))
        waiter(Y1, 1).wait_recv()
        start(copies(RELAY_X, yn, 1, xn))
        waiter(X1, 1).wait_recv()
        start(copies(ON_X, xn, None, sib))
        waiter(Y0, 0).wait_recv()
        start(copies(ON_Y, yn, None, sib))
        waiter(RELAY_Y, 0).wait_recv()
        start(copies(ON_D0, dg, 0, sib))
        waiter(RELAY_X, 1).wait_recv()
        start(copies(ON_D1, dg, 1, sib))
        waiter(SIB, None).wait_recv()
        waiter(ON_X, None).wait_recv()
        waiter(ON_Y, None).wait_recv()
        waiter(ON_D0, 0).wait_recv()
        waiter(ON_D1, 1).wait_recv()
        for rel, hf in ((SIB, None), (X0, 0), (X1, 1), (Y0, 0), (Y1, 1), (RELAY_Y, 0), (RELAY_X, 1),
                        (ON_X, None), (ON_Y, None), (ON_D0, 0), (ON_D1, 1)):
            waiter(rel, hf).wait_send()
        grp = _shard_group(s_ref, pieces)
        pltpu.make_async_copy(grp, grp, local_sem).wait()

    hbm = pl.BlockSpec(memory_space=pl.ANY)
    return pl.pallas_call(
        body, in_specs=[hbm, hbm, pl.BlockSpec(memory_space=pltpu.VMEM)], out_specs=[hbm] * 5,
        out_shape=(SDS((3, F, D), bf16), SDS((T, D), bf16),
                   SDS((DIN, D), bf16), SDS((DMIX, D), bf16), SDS((3, F, D), bf16)),
        scratch_shapes=[pltpu.VMEM((NORM_ROWS, D), f32), pltpu.VMEM((NORM_ROWS, D), bf16),
                        pltpu.VMEM((_group_rows(rest_pieces), D), bf16),
                        pltpu.SemaphoreType.DMA((11,)), pltpu.SemaphoreType.DMA((11,)), pltpu.SemaphoreType.DMA,
                        pltpu.SemaphoreType.DMA((2,))],
        compiler_params=pltpu.CompilerParams(has_side_effects=True),
        name="all_gather_ffn1")(shard, x, gain)


HBM_SPEC = pl.BlockSpec(memory_space=pltpu.HBM)
SEM_SPEC = pl.BlockSpec(memory_space=pltpu.SEMAPHORE)
ANY_SPEC = pl.BlockSpec(memory_space=pl.ANY)
SPLIT_EFFECT = pltpu.SideEffectType.DATAFLOW_SIDE_EFFECTING


def _in_hbm(a):
    return pltpu.with_memory_space_constraint(a, pltpu.HBM)


def _hbm_like(a):
    return pltpu.HBM(a.shape, a.dtype)


def _gather_rest_start(shard, wi, wo, w2, w1):
    def body(s_ref, wi_ref, wo_ref, w2_ref, w1_ref,
             ssem_m, rsem_m0, rsem_m, ssem_f, rsem_f0, rsem_f, s_o, wi_o, wo_o, w2_o, w1_o):
        x, y, c = _position()
        me, sib = (x, y, c), (x, y, 1 - c)
        chips = [(1 - x, y), (x, 1 - y), (1 - x, 1 - y)]
        arrs = _weight_pieces(wi_ref=wi_ref, wo_ref=wo_ref, w2_ref=w2_ref)
        for pieces, ssem, rsem0, rsem in ((MIX_PIECES, ssem_m, rsem_m0, rsem_m), (F2_PIECES, ssem_f, rsem_f0, rsem_f)):
            for p in pieces:
                pltpu.make_async_remote_copy(
                    src_ref=_shard_piece(s_ref, p), dst_ref=_block_rows(arrs, p, me), send_sem=ssem.at[0],
                    recv_sem=rsem0, device_id=sib, device_id_type=MESH).start()
            for j, chip in enumerate(chips):
                for p in pieces:
                    pltpu.make_async_remote_copy(
                        src_ref=_shard_piece(s_ref, p), dst_ref=_block_rows(arrs, p, me), send_sem=ssem.at[1 + j],
                        recv_sem=rsem.at[j], device_id=(*chip, c), device_id_type=MESH).start()

    dma = pltpu.SemaphoreType.DMA
    return pl.pallas_call(
        body, name="gather_rest_start",
        out_shape=(dma((4,)), dma(()), dma((3,)), dma((4,)), dma(()), dma((3,)),
                   _hbm_like(shard), _hbm_like(wi), _hbm_like(wo), _hbm_like(w2), _hbm_like(w1)),
        in_specs=(HBM_SPEC,) * 5, out_specs=(SEM_SPEC,) * 6 + (HBM_SPEC,) * 5,
        input_output_aliases={0: 6, 1: 7, 2: 8, 3: 9, 4: 10},
        compiler_params=pltpu.CompilerParams(has_side_effects=SPLIT_EFFECT),
    )(_in_hbm(shard), _in_hbm(wi), _in_hbm(wo), _in_hbm(w2), _in_hbm(w1))


def _gather_mix_pass_on(rsem_m, wi, wo, thru, after):
    def body(wi_ref, wo_ref, thru_ref, rsem, after_ref, fsend, frecv, wi_o, wo_o, thru_o):
        x, y, c = _position()
        sib = (x, y, 1 - c)
        arrs = _weight_pieces(wi_ref=wi_ref, wo_ref=wo_ref)
        both = wi_ref.at[pl.ds(0, _group_rows(MIX_PIECES)), :]
        for j, chip in enumerate([(1 - x, y), (x, 1 - y), (1 - x, 1 - y)]):
            pltpu.make_async_remote_copy(src_ref=both, dst_ref=both, send_sem=fsend.at[j], recv_sem=rsem.at[j],
                                         device_id=(x, y, c), device_id_type=MESH).wait_recv()
            for p in MIX_PIECES:
                rows = _block_rows(arrs, p, (*chip, c))
                pltpu.make_async_remote_copy(src_ref=rows, dst_ref=rows, send_sem=fsend.at[j], recv_sem=frecv.at[j],
                                             device_id=sib, device_id_type=MESH).start()

    dma = pltpu.SemaphoreType.DMA
    return pl.pallas_call(
        body, name="gather_mix_pass_on",
        out_shape=(dma((3,)), dma((3,)), _hbm_like(wi), _hbm_like(wo), _hbm_like(thru)),
        in_specs=(HBM_SPEC, HBM_SPEC, HBM_SPEC, SEM_SPEC, ANY_SPEC), out_specs=(SEM_SPEC, SEM_SPEC) + (HBM_SPEC,) * 3,
        input_output_aliases={0: 2, 1: 3, 2: 4},
        compiler_params=pltpu.CompilerParams(has_side_effects=SPLIT_EFFECT),
    )(wi, wo, _in_hbm(thru), rsem_m, after)


def _gather_mix_wait(ssem_m, rsem_m0, fsend, frecv, shard, wi, wo, after):
    def body(s_ref, wi_ref, wo_ref, ssem, rsem0, fs, fr, after_ref, s_o, wi_o, wo_o):
        x, y, c = _position()
        grp = _shard_group(s_ref, MIX_PIECES)

        def waiter(send_sem, recv_sem):
            return pltpu.make_async_remote_copy(src_ref=grp, dst_ref=grp, send_sem=send_sem, recv_sem=recv_sem,
                                                device_id=(x, y, c), device_id_type=MESH)

        waiter(ssem.at[0], rsem0).wait_recv()
        for j in range(3):
            waiter(fs.at[j], fr.at[j]).wait_recv()
        for rel in range(4):
            waiter(ssem.at[rel], rsem0).wait_send()
        for j in range(3):
            waiter(fs.at[j], fr.at[j]).wait_send()

    return pl.pallas_call(
        body, name="gather_mix_wait", out_shape=(_hbm_like(shard), _hbm_like(wi), _hbm_like(wo)),
        in_specs=(HBM_SPEC,) * 3 + (SEM_SPEC,) * 4 + (ANY_SPEC,), out_specs=(HBM_SPEC,) * 3,
        input_output_aliases={0: 0, 1: 1, 2: 2},
        compiler_params=pltpu.CompilerParams(has_side_effects=SPLIT_EFFECT),
    )(shard, wi, wo, ssem_m, rsem_m0, fsend, frecv, after)


def _gather_ffn2_pass_on(rsem_f, w2, wo, after):
    def body(w2_ref, wo_ref, rsem, after_ref, fsend, frecv, w2_o, wo_o):
        x, y, c = _position()
        sib = (x, y, 1 - c)
        chips = [(1 - x, y), (x, 1 - y), (1 - x, 1 - y)]
        arrs = _weight_pieces(w2_ref=w2_ref)
        three = w2_ref.at[0, pl.ds(0, _group_rows(F2_PIECES)), :]
        for j, chip in enumerate(chips):
            pltpu.make_async_remote_copy(src_ref=three, dst_ref=three, send_sem=fsend.at[j], recv_sem=rsem.at[j],
                                         device_id=(x, y, c), device_id_type=MESH).wait_recv()
            for p in F2_PIECES:
                rows = _block_rows(arrs, p, (*chip, c))
                pltpu.make_async_remote_copy(src_ref=rows, dst_ref=rows, send_sem=fsend.at[j], recv_sem=frecv.at[j],
                                             device_id=sib, device_id_type=MESH).start()

    dma = pltpu.SemaphoreType.DMA
    return pl.pallas_call(
        body, name="gather_ffn2_pass_on", out_shape=(dma((3,)), dma((3,)), _hbm_like(w2), _hbm_like(wo)),
        in_specs=(HBM_SPEC, HBM_SPEC, SEM_SPEC, ANY_SPEC), out_specs=(SEM_SPEC, SEM_SPEC, HBM_SPEC, HBM_SPEC),
        input_output_aliases={0: 2, 1: 3},
        compiler_params=pltpu.CompilerParams(has_side_effects=SPLIT_EFFECT),
    )(w2, wo, rsem_f, after)


def _gather_ffn2_wait(ssem_f, rsem_f0, fsend, frecv, shard, w2, after):
    def body(s_ref, w2_ref, ssem, rsem0, fs, fr, after_ref, w2_o):
        x, y, c = _position()
        grp = _shard_group(s_ref, F2_PIECES)

        def waiter(send_sem, recv_sem):
            return pltpu.make_async_remote_copy(src_ref=grp, dst_ref=grp, send_sem=send_sem, recv_sem=recv_sem,
                                                device_id=(x, y, c), device_id_type=MESH)

        waiter(ssem.at[0], rsem0).wait_recv()
        for j in range(3):
            waiter(fs.at[j], fr.at[j]).wait_recv()
        for rel in range(4):
            waiter(ssem.at[rel], rsem0).wait_send()
        for j in range(3):
            waiter(fs.at[j], fr.at[j]).wait_send()

    return pl.pallas_call(
        body, name="gather_ffn2_wait", out_shape=_hbm_like(w2),
        in_specs=(HBM_SPEC, HBM_SPEC, SEM_SPEC, SEM_SPEC, SEM_SPEC, SEM_SPEC, ANY_SPEC), out_specs=HBM_SPEC,
        input_output_aliases={1: 0},
        compiler_params=pltpu.CompilerParams(has_side_effects=SPLIT_EFFECT),
    )(shard, w2, ssem_f, rsem_f0, fsend, frecv, after)


class _GatheredWeights(_LocalWeights):
    def __init__(self, shard, x, gain1):
        w1, self.h1, wi, wo, w2 = _all_gather_ffn1(shard, x, gain1)
        (self.ssem_m, self.rsem_m0, self.rsem_m, self.ssem_f, self.rsem_f0, self.rsem_f,
         self.shard, self.wi, self.wo, self.w2_part, self.w1) = _gather_rest_start(shard, wi, wo, w2, w1)

    def first_norm(self, x, gain):
        return self.h1

    def after_ffn1(self, gain, x1):
        self.fsend_m, self.frecv_m, self.wi, self.wo, gain = _gather_mix_pass_on(self.rsem_m, self.wi, self.wo, gain, x1)
        return gain

    def mix(self, after):
        self.shard, wint, wout = _gather_mix_wait(self.ssem_m, self.rsem_m0, self.fsend_m, self.frecv_m, self.shard,
                                                  self.wi, self.wo, after)
        return wint, wout

    def before_out_proj(self, wout, after):
        self.fsend, self.frecv, self.w2_part, wout = _gather_ffn2_pass_on(self.rsem_f, self.w2_part, wout, after)
        return wout

    def ffn2(self, after):
        return _gather_ffn2_wait(self.ssem_f, self.rsem_f0, self.fsend, self.frecv, self.shard, self.w2_part, after)

    def mix_ffn2_grads_ready(self, dwint, dwout, dw2, dh2):
        rx1 = lax.empty((4, RSA_ROWS, D), bf16)
        self.sa, self.ra, dwint, dwout, dw2, rx1, dh2 = _rsa_level1_start(dwint, dwout, dw2, rx1, dh2)
        self.level1 = (dwint, dwout, dw2, rx1)
        return dh2

    def before_ffn1_bwd(self, dx1b):
        dwint, dwout, dw2, rx1 = _rsa_level1_wait(self.sa, self.ra, *self.level1, dx1b)
        tx, self.acc = _rsa_chip_sums(dwint, dwout, dw2, rx1)
        rx2 = lax.empty((3, RSA_ROWS, D), bf16)
        self.sb, self.rb, self.tx, self.rx2, dx1b = _rsa_level2_start(tx, rx2, dx1b)
        return dx1b

    def mix_ffn2_grads_total(self, after):
        rx2 = _rsa_level2_wait(self.sb, self.rb, self.tx, self.rx2, after)
        return _rsa_total(self.acc, rx2)


def _reduce_scatter_ffn1_head(dw1, small_packed):
    pieces = G1_PIECES
    half = FS // 2
    hrows = len(pieces) * half
    nrows = 2 * hrows
    X_RELAY, Y_RELAY = range(2)

    def body(d1_ref, p_ref, forx_ref, fory_ref, own_ref, rx1_ref, relx_ref, rely_ref, tot_ref,
             own_buf, rx_buf, tx1, tx2, tx3, acc, sa, ra, sb, rb, lsem, pair, chips, small_send, small_recv):
        x, y, c = _position()
        me, sib = (x, y, c), (x, y, 1 - c)
        xn, yn = (1 - x, y, c), (x, 1 - y, c)
        rel_chips = [(x, y), (1 - x, y), (x, 1 - y), (1 - x, 1 - y)]
        srcs = _weight_pieces(w1_ref=d1_ref)

        my_chip = 2 * x + y
        pair[c] = p_ref[...]
        swap = pltpu.make_async_remote_copy(
            src_ref=p_ref, dst_ref=pair.at[c], send_sem=small_send.at[0], recv_sem=small_recv.at[0],
            device_id=sib, device_id_type=MESH)
        swap.start()
        small = [pltpu.make_async_remote_copy(
            src_ref=chips.at[my_chip], dst_ref=chips.at[my_chip], send_sem=small_send.at[j], recv_sem=small_recv.at[j],
            device_id=(*rel_chips[j], c), device_id_type=MESH) for j in (1, 2, 3)]

        def part(k, dev, hf):
            r = PIECE_ROWS[k]
            return srcs[k].at[pl.ds(pl.multiple_of(_dev_index(*dev) * r + hf * half, 16), half), :]

        def slot(ref, k, hf):
            return ref.at[pl.ds(hf * hrows + k * half, half), :]

        halves = [(k, hf) for hf in (0, 1) for k in pieces]

        for j in (3, 1, 2, 0):
            for k, hf in halves:
                pltpu.make_async_remote_copy(
                    src_ref=part(k, (*rel_chips[j], 1 - c), hf), dst_ref=slot(rx1_ref.at[j], k, hf),
                    send_sem=sa.at[j], recv_sem=ra.at[j], device_id=sib, device_id_type=MESH).start()

        def wait_a(j):
            return pltpu.make_async_remote_copy(src_ref=rx1_ref.at[j], dst_ref=rx1_ref.at[j], send_sem=sa.at[j],
                                                recv_sem=ra.at[j], device_id=me, device_id_type=MESH)

        def ici(rel, src, dst, to):
            return pltpu.make_async_remote_copy(src_ref=src, dst_ref=dst, send_sem=sb.at[rel], recv_sem=rb.at[rel],
                                                device_id=to, device_id_type=MESH)

        first, second = pl.ds(0, hrows), pl.ds(hrows, hrows)
        sends = {
            X_RELAY: ici(X_RELAY, tx3.at[first, :], relx_ref, xn),
            Y_RELAY: ici(Y_RELAY, tx3.at[second, :], rely_ref, yn),
        }

        swap.wait_recv()
        chips[my_chip] = pair[0] + pair[1]
        for cp in small:
            cp.start()

        def chip_sum(j, dst):
            loads = [pltpu.make_async_copy(part(k, (*rel_chips[j], c), hf), slot(own_buf, k, hf), lsem.at[0])
                     for k, hf in halves]
            for cp in loads:
                cp.start()
            wait_a(j).wait_recv()
            got = pltpu.make_async_copy(rx1_ref.at[j], rx_buf, lsem.at[1])
            got.start()
            pltpu.make_async_copy(rx_buf, rx_buf, lsem.at[0]).wait()
            got.wait()

            def add(i, carry):
                rows = pl.ds(pl.multiple_of(i * half, 16), half)
                tot = own_buf[rows, :].astype(f32) + rx_buf[rows, :].astype(f32)
                dst[rows, :] = tot.astype(dst.dtype)
                return carry

            lax.fori_loop(0, nrows // half, add, 0)

        def add_landed(landed, dst, rows0, nrows_):
            got = pltpu.make_async_copy(landed, rx_buf.at[pl.ds(0, nrows_), :], lsem.at[1])
            got.start()
            got.wait()

            def add(i, carry):
                src_rows = pl.ds(pl.multiple_of(i * half, 16), half)
                dst_rows = pl.ds(pl.multiple_of(rows0 + i * half, 16), half)
                dst[dst_rows, :] = (dst[dst_rows, :].astype(f32) + rx_buf[src_rows, :].astype(f32)).astype(dst.dtype)
                return carry

            lax.fori_loop(0, nrows_ // half, add, 0)

        chip_sum(3, tx3)
        sends[X_RELAY].start()
        sends[Y_RELAY].start()
        chip_sum(1, tx1)
        chip_sum(2, tx2)
        chip_sum(0, acc)
        own_out = pltpu.make_async_copy(acc, own_ref, lsem.at[0])
        own_out.start()
        sends[X_RELAY].wait_recv()
        add_landed(relx_ref, tx2, 0, hrows)
        sends[Y_RELAY].wait_recv()
        add_landed(rely_ref, tx1, hrows, hrows)
        own_out.wait()
        outs = [pltpu.make_async_copy(tx1, forx_ref, lsem.at[0]), pltpu.make_async_copy(tx2, fory_ref, lsem.at[1])]
        for cp in outs:
            cp.start()
        for cp in outs:
            cp.wait()
        for cp in small:
            cp.wait_recv()
        tot = (chips[0] + chips[1]) + (chips[2] + chips[3])
        tot_ref[...] = tot
        loss = jnp.sum(tot[LOSS_ROW:LOSS_ROW + 1, :], axis=-1, keepdims=True)
        tot_ref[LOSS_ROW:LOSS_ROW + 1, :] = jnp.broadcast_to(loss, (1, 128))
        for j in range(4):
            wait_a(j).wait_send()
        for cp in sends.values():
            cp.wait_send()
        swap.wait_send()
        for cp in small:
            cp.wait_send()

    hbm = pl.BlockSpec(memory_space=pl.ANY)
    vm = pl.BlockSpec(memory_space=pltpu.VMEM)
    outs = pl.pallas_call(
        body, in_specs=[hbm, vm], out_specs=[hbm] * 6 + [vm],
        out_shape=(SDS((nrows, D), bf16), SDS((nrows, D), bf16), SDS((nrows, D), f32), SDS((4, nrows, D), bf16),
                   SDS((hrows, D), bf16), SDS((hrows, D), bf16), SDS((SMALL_ROWS, 128), f32)),
        scratch_shapes=[pltpu.VMEM((nrows, D), bf16), pltpu.VMEM((nrows, D), bf16),
                        pltpu.VMEM((nrows, D), bf16), pltpu.VMEM((nrows, D), bf16), pltpu.VMEM((nrows, D), bf16),
                        pltpu.VMEM((nrows, D), f32),
                        pltpu.SemaphoreType.DMA((4,)), pltpu.SemaphoreType.DMA((4,)),
                        pltpu.SemaphoreType.DMA((2,)), pltpu.SemaphoreType.DMA((2,)), pltpu.SemaphoreType.DMA((2,)),
                        pltpu.VMEM((2, SMALL_ROWS, 128), f32), pltpu.VMEM((4, SMALL_ROWS, 128), f32),
                        pltpu.SemaphoreType.DMA((4,)), pltpu.SemaphoreType.DMA((4,))],
        compiler_params=pltpu.CompilerParams(has_side_effects=True, vmem_limit_bytes=VMEM_LIMIT_V7X),
        name="reduce_scatter_ffn1_head")(dw1, small_packed)
    return outs[0], outs[1], outs[2], outs[-1]


def _rs1_tail_start(for_x, for_y, from_x, from_y, thru_a, thru_b):
    def body(fx_ref, fy_ref, lx_ref, ly_ref, ta_ref, tb_ref, ssem, rsem, fx_o, fy_o, lx_o, ly_o, ta_o, tb_o):
        x, y, c = _position()
        pltpu.make_async_remote_copy(src_ref=fx_ref, dst_ref=lx_ref, send_sem=ssem.at[0], recv_sem=rsem.at[0],
                                     device_id=(1 - x, y, c), device_id_type=MESH).start()
        pltpu.make_async_remote_copy(src_ref=fy_ref, dst_ref=ly_ref, send_sem=ssem.at[1], recv_sem=rsem.at[1],
                                     device_id=(x, 1 - y, c), device_id_type=MESH).start()

    dma = pltpu.SemaphoreType.DMA
    arrs = (for_x, for_y, from_x, from_y, thru_a, thru_b)
    return pl.pallas_call(
        body, name="rs1_tail_start", out_shape=(dma((2,)), dma((2,))) + tuple(_hbm_like(a) for a in arrs),
        in_specs=(HBM_SPEC,) * 6, out_specs=(SEM_SPEC,) * 2 + (HBM_SPEC,) * 6,
        input_output_aliases={0: 2, 1: 3, 2: 4, 3: 5, 4: 6, 5: 7},
        compiler_params=pltpu.CompilerParams(has_side_effects=SPLIT_EFFECT),
    )(*[_in_hbm(a) for a in arrs])


def _rs1_tail_wait(ssem, rsem, for_x, for_y, from_x, from_y, after):
    def body(fx_ref, fy_ref, lx_ref, ly_ref, ssem_ref, rsem_ref, after_ref, lx_o, ly_o):
        x, y, c = _position()
        for j, (src, dst) in enumerate(((fx_ref, lx_ref), (fy_ref, ly_ref))):
            d = pltpu.make_async_remote_copy(src_ref=src, dst_ref=dst, send_sem=ssem_ref.at[j], recv_sem=rsem_ref.at[j],
                                             device_id=(x, y, c), device_id_type=MESH)
            d.wait_recv()
            d.wait_send()

    return pl.pallas_call(
        body, name="rs1_tail_wait", out_shape=(_hbm_like(from_x), _hbm_like(from_y)),
        in_specs=(HBM_SPEC,) * 4 + (SEM_SPEC, SEM_SPEC, ANY_SPEC), out_specs=(HBM_SPEC, HBM_SPEC),
        input_output_aliases={2: 0, 3: 1},
        compiler_params=pltpu.CompilerParams(has_side_effects=SPLIT_EFFECT),
    )(for_x, for_y, from_x, from_y, ssem, rsem, after)


def _rs1_total(own, from_x, from_y):
    half = FS // 2
    npiece = len(G1_PIECES)

    def body(a_ref, x_ref, y_ref, o_ref):
        o_ref[...] = (a_ref[...] + x_ref[...].astype(f32)) + y_ref[...].astype(f32)

    blk = pl.BlockSpec((half, D), lambda i: (i, 0))
    return pl.pallas_call(
        body, grid=(2 * npiece,), in_specs=[blk, blk, blk],
        out_specs=pl.BlockSpec((half, D), lambda i: (2 * (i % npiece) + i // npiece, 0)),
        out_shape=SDS((npiece * FS, D), f32), name="rs1_total")(own, from_x, from_y)


RSA_PIECES = MIX_PIECES + F2_PIECES
RSA_ROWS = _group_rows(RSA_PIECES)
RSA_OFF = {k: PIECE_OFF[k] - PIECE_OFF[RSA_PIECES[0]] for k in RSA_PIECES}
RSA_BLOCK = 192


def _rsa_rows(ref, k):
    return ref.at[pl.ds(RSA_OFF[k], PIECE_ROWS[k]), :]


def _rsa_level1_start(dwint, dwout, dw2, rx1, thru):
    def body(di_ref, do_ref, d2_ref, rx1_ref, thru_ref, sa, ra, di_o, do_o, d2_o, rx1_o, thru_o):
        x, y, c = _position()
        srcs = _weight_pieces(wi_ref=di_ref, wo_ref=do_ref, w2_ref=d2_ref)
        for j, chip in enumerate([(x, y), (1 - x, y), (x, 1 - y), (1 - x, 1 - y)]):
            for k in RSA_PIECES:
                pltpu.make_async_remote_copy(
                    src_ref=_block_rows(srcs, k, (*chip, 1 - c)), dst_ref=_rsa_rows(rx1_ref.at[j], k),
                    send_sem=sa.at[j], recv_sem=ra.at[j], device_id=(x, y, 1 - c), device_id_type=MESH).start()

    dma = pltpu.SemaphoreType.DMA
    arrs = (dwint, dwout, dw2, rx1, thru)
    return pl.pallas_call(
        body, name="rsa_level1_start", out_shape=(dma((4,)), dma((4,))) + tuple(_hbm_like(a) for a in arrs),
        in_specs=(HBM_SPEC,) * 5, out_specs=(SEM_SPEC,) * 2 + (HBM_SPEC,) * 5,
        input_output_aliases={0: 2, 1: 3, 2: 4, 3: 5, 4: 6},
        compiler_params=pltpu.CompilerParams(has_side_effects=SPLIT_EFFECT),
    )(*[_in_hbm(a) for a in arrs])


def _rsa_level1_wait(sa, ra, dwint, dwout, dw2, rx1, after):
    def body(di_ref, do_ref, d2_ref, rx1_ref, sa_ref, ra_ref, after_ref, di_o, do_o, d2_o, rx1_o):
        x, y, c = _position()
        for j in range(4):
            d = pltpu.make_async_remote_copy(src_ref=rx1_ref.at[j], dst_ref=rx1_ref.at[j], send_sem=sa_ref.at[j],
                                             recv_sem=ra_ref.at[j], device_id=(x, y, c), device_id_type=MESH)
            d.wait_recv()
            d.wait_send()

    arrs = (dwint, dwout, dw2, rx1)
    return pl.pallas_call(
        body, name="rsa_level1_wait", out_shape=tuple(_hbm_like(a) for a in arrs),
        in_specs=(HBM_SPEC,) * 4 + (SEM_SPEC, SEM_SPEC, ANY_SPEC), out_specs=(HBM_SPEC,) * 4,
        input_output_aliases={0: 0, 1: 1, 2: 2, 3: 3},
        compiler_params=pltpu.CompilerParams(has_side_effects=SPLIT_EFFECT),
    )(*arrs, sa, ra, after)


def _rsa_chip_sums(dwint, dwout, dw2, rx1):
    nblk = RSA_ROWS // RSA_BLOCK

    def body(di_ref, do_ref, d2_ref, rx1_ref, tx_ref, acc_ref, own_buf, rx_buf, tx_buf, acc_buf, in_sems, out_sems):
        x, y, c = _position()
        srcs = _weight_pieces(wi_ref=di_ref, wo_ref=do_ref, w2_ref=d2_ref)
        chips = [(x, y), (1 - x, y), (x, 1 - y), (1 - x, 1 - y)]

        def start_loads(j):
            s = j % 2
            for k in RSA_PIECES:
                pltpu.make_async_copy(_block_rows(srcs, k, (*chips[j], c)), _rsa_rows(own_buf.at[s], k),
                                      in_sems.at[2 * s]).start()
            pltpu.make_async_copy(rx1_ref.at[j], rx_buf.at[s], in_sems.at[2 * s + 1]).start()

        def wait_loads(j):
            s = j % 2
            pltpu.make_async_copy(rx1_ref.at[j], own_buf.at[s], in_sems.at[2 * s]).wait()
            pltpu.make_async_copy(rx1_ref.at[j], rx_buf.at[s], in_sems.at[2 * s + 1]).wait()

        def store(j):
            if j == 0:
                return pltpu.make_async_copy(acc_buf, acc_ref, out_sems.at[2])
            return pltpu.make_async_copy(tx_buf.at[j % 2], tx_ref.at[j - 1], out_sems.at[j % 2])

        start_loads(0)
        for j in range(4):
            s = j % 2
            if j + 1 < 4:
                start_loads(j + 1)
            wait_loads(j)
            if j == 3:
                store(1).wait()

            def add(i, carry, j=j, s=s):
                rows = pl.ds(pl.multiple_of(i * RSA_BLOCK, 16), RSA_BLOCK)
                tot = own_buf[s, rows, :].astype(f32) + rx_buf[s, rows, :].astype(f32)
                if j == 0:
                    acc_buf[rows, :] = tot
                else:
                    tx_buf[s, rows, :] = tot.astype(bf16)
                return carry

            lax.fori_loop(0, nblk, add, 0)
            store(j).start()
        store(0).wait()
        store(2).wait()
        store(3).wait()

    return pl.pallas_call(
        body, in_specs=[ANY_SPEC] * 4, out_specs=[ANY_SPEC] * 2,
        out_shape=(SDS((3, RSA_ROWS, D), bf16), SDS((RSA_ROWS, D), f32)),
        scratch_shapes=[pltpu.VMEM((2, RSA_ROWS, D), bf16), pltpu.VMEM((2, RSA_ROWS, D), bf16),
                        pltpu.VMEM((2, RSA_ROWS, D), bf16), pltpu.VMEM((RSA_ROWS, D), f32),
                        pltpu.SemaphoreType.DMA((4,)), pltpu.SemaphoreType.DMA((3,))],
        compiler_params=_cparams(None, VMEM_LIMIT_V7X), name="rsa_chip_sums")(dwint, dwout, dw2, rx1)


def _rsa_level2_start(tx, rx2, thru):
    def body(tx_ref, rx2_ref, thru_ref, sb, rb, tx_o, rx2_o, thru_o):
        x, y, c = _position()
        for j, chip in enumerate([(1 - x, y), (x, 1 - y), (1 - x, 1 - y)]):
            pltpu.make_async_remote_copy(src_ref=tx_ref.at[j], dst_ref=rx2_ref.at[j], send_sem=sb.at[j],
                                         recv_sem=rb.at[j], device_id=(*chip, c), device_id_type=MESH).start()

    dma = pltpu.SemaphoreType.DMA
    arrs = (tx, rx2, thru)
    return pl.pallas_call(
        body, name="rsa_level2_start", out_shape=(dma((3,)), dma((3,))) + tuple(_hbm_like(a) for a in arrs),
        in_specs=(HBM_SPEC,) * 3, out_specs=(SEM_SPEC,) * 2 + (HBM_SPEC,) * 3,
        input_output_aliases={0: 2, 1: 3, 2: 4},
        compiler_params=pltpu.CompilerParams(has_side_effects=SPLIT_EFFECT),
    )(*[_in_hbm(a) for a in arrs])


def _rsa_level2_wait(sb, rb, tx, rx2, after):
    def body(tx_ref, rx2_ref, sb_ref, rb_ref, after_ref, rx2_o):
        x, y, c = _position()
        for j in range(3):
            d = pltpu.make_async_remote_copy(src_ref=tx_ref.at[j], dst_ref=rx2_ref.at[j], send_sem=sb_ref.at[j],
                                             recv_sem=rb_ref.at[j], device_id=(x, y, c), device_id_type=MESH)
            d.wait_recv()
            d.wait_send()

    return pl.pallas_call(
        body, name="rsa_level2_wait", out_shape=_hbm_like(rx2),
        in_specs=(HBM_SPEC, HBM_SPEC, SEM_SPEC, SEM_SPEC, ANY_SPEC), out_specs=HBM_SPEC,
        input_output_aliases={1: 0},
        compiler_params=pltpu.CompilerParams(has_side_effects=SPLIT_EFFECT),
    )(tx, rx2, sb, rb, after)


def _rsa_total(acc, rx2):
    def body(a_ref, r_ref, o_ref):
        o_ref[...] = ((a_ref[...] + r_ref[0].astype(f32)) + r_ref[1].astype(f32)) + r_ref[2].astype(f32)

    return pl.pallas_call(
        body, grid=(RSA_ROWS // RSA_BLOCK,),
        in_specs=[pl.BlockSpec((RSA_BLOCK, D), lambda i: (i, 0)), pl.BlockSpec((3, RSA_BLOCK, D), lambda i: (0, i, 0))],
        out_specs=pl.BlockSpec((RSA_BLOCK, D), lambda i: (i, 0)),
        out_shape=SDS((RSA_ROWS, D), f32), name="rsa_total")(acc, rx2)


def _adamw_math(w, g, m, v):
    m = ADAM_B1 * m + (1.0 - ADAM_B1) * g
    v = ADAM_B2 * v + (1.0 - ADAM_B2) * (g * g)
    m_hat = m / (1.0 - ADAM_B1 ** ADAM_STEP)
    v_hat = v / (1.0 - ADAM_B2 ** ADAM_STEP)
    delta = -ADAM_LR * (m_hat / (jnp.sqrt(v_hat) + ADAM_EPS) + ADAM_WD * w)
    return delta, m, v


def _adamw_big(pieces, ws, ms, vs, red, name):
    npiece = len(pieces)
    rmax = max(PIECE_ROWS[k] for k in pieces)

    def body(*refs):
        ins = (refs[0:npiece], refs[npiece:2 * npiece], refs[2 * npiece:3 * npiece])
        red_ref = refs[3 * npiece]
        out_refs = refs[3 * npiece + 1:7 * npiece + 1]
        inb, outb, in_sems, out_sems = refs[7 * npiece + 1:]

        def grad_rows(k):
            if k in G1_PIECES:
                return red_ref.at[pl.ds(PIECE_OFF[k], PIECE_ROWS[k]), :]
            return _rsa_rows(red_ref, k)

        def loads(i):
            s, k = i % 2, pieces[i]
            r = PIECE_ROWS[k]
            cps = [pltpu.make_async_copy(ins[q][i].at[0], inb.at[s, q, pl.ds(0, r), :], in_sems.at[4 * s + q])
                   for q in range(3)]
            cps.append(pltpu.make_async_copy(grad_rows(k), inb.at[s, 3, pl.ds(0, r), :], in_sems.at[4 * s + 3]))
            return cps

        def stores(i):
            s, r = i % 2, PIECE_ROWS[pieces[i]]
            return [pltpu.make_async_copy(outb.at[s, q, pl.ds(0, r), :], out_refs[q * npiece + i].at[0],
                                          out_sems.at[4 * s + q]) for q in range(4)]

        for cp in loads(0):
            cp.start()
        for i in range(npiece):
            s, r = i % 2, PIECE_ROWS[pieces[i]]
            if i + 1 < npiece:
                for cp in loads(i + 1):
                    cp.start()
            for cp in loads(i):
                cp.wait()
            if i >= 2:
                for cp in stores(i - 2):
                    cp.wait()
            g = inb[s, 3, 0:r, :]
            d, nm, nv = _adamw_math(inb[s, 0, 0:r, :], g, inb[s, 1, 0:r, :], inb[s, 2, 0:r, :])
            outb[s, 0, 0:r, :] = g
            outb[s, 1, 0:r, :] = d
            outb[s, 2, 0:r, :] = nm
            outb[s, 3, 0:r, :] = nv
            for cp in stores(i):
                cp.start()
        for i in range(max(npiece - 2, 0), npiece):
            for cp in stores(i):
                cp.wait()

    hbm = pl.BlockSpec(memory_space=pl.ANY)
    outs = pl.pallas_call(
        body, in_specs=[hbm] * (3 * npiece + 1), out_specs=[hbm] * (4 * npiece),
        out_shape=tuple(SDS(w.shape, f32) for _ in range(4) for w in ws),
        scratch_shapes=[pltpu.VMEM((2, 4, rmax, D), f32), pltpu.VMEM((2, 4, rmax, D), f32),
                        pltpu.SemaphoreType.DMA((8,)), pltpu.SemaphoreType.DMA((8,))],
        compiler_params=_cparams(None, VMEM_LIMIT_V7X), name=name)(*ws, *ms, *vs, red)
    return [list(outs[q * npiece:(q + 1) * npiece]) for q in range(4)]


def _adamw_small(ws, ms, vs, gs, name):
    n = len(ws)

    def body(*refs):
        w_refs, m_refs, v_refs, g_refs = refs[0:n], refs[n:2 * n], refs[2 * n:3 * n], refs[3 * n:4 * n]
        outs = refs[4 * n:]
        for i in range(n):
            d, nm, nv = _adamw_math(w_refs[i][...], g_refs[i][...], m_refs[i][...], v_refs[i][...])
            outs[i][...] = d
            outs[n + i][...] = nm
            outs[2 * n + i][...] = nv

    outs = pl.pallas_call(
        body, out_shape=tuple(SDS(w.shape, f32) for _ in range(3) for w in ws), name=name)(*ws, *ms, *vs, *gs)
    return [list(outs[q * n:(q + 1) * n]) for q in range(3)]


WEIGHTS = ("ffn1_norm", "ffn1_w_gate", "ffn1_w_up", "ffn1_w_down", "mix_norm", "w_in", "q_norm", "k_norm",
           "attn_sinks", "rel_bias", "pool_w", "pool_scale", "w_out", "ffn2_norm", "ffn2_w_gate", "ffn2_w_up",
           "ffn2_w_down")
BIG = (("ffn1_w_gate", True), ("ffn1_w_up", True), ("ffn1_w_down", False), ("w_in", True), ("w_out", False),
       ("ffn2_w_gate", True), ("ffn2_w_up", True), ("ffn2_w_down", False))


def kernel(x, ffn1_norm, ffn1_w_gate, ffn1_w_up, ffn1_w_down, mix_norm, w_in, q_norm, k_norm, attn_sinks, rel_bias, pool_w, pool_scale, w_out, ffn2_norm, ffn2_w_gate, ffn2_w_up, ffn2_w_down, loss_target, m_ffn1_norm, m_ffn1_w_gate, m_ffn1_w_up, m_ffn1_w_down, m_mix_norm, m_w_in, m_q_norm, m_k_norm, m_attn_sinks, m_rel_bias, m_pool_w, m_pool_scale, m_w_out, m_ffn2_norm, m_ffn2_w_gate, m_ffn2_w_up, m_ffn2_w_down, v_ffn1_norm, v_ffn1_w_gate, v_ffn1_w_up, v_ffn1_w_down, v_mix_norm, v_w_in, v_q_norm, v_k_norm, v_attn_sinks, v_rel_bias, v_pool_w, v_pool_scale, v_w_out, v_ffn2_norm, v_ffn2_w_gate, v_ffn2_w_up, v_ffn2_w_down):
    args = dict(locals())
    w = {n: args[n] for n in WEIGHTS}
    m = {n: args["m_" + n] for n in WEIGHTS}
    v = {n: args["v_" + n] for n in WEIGHTS}

    as_rows = lambda a, tr: jnp.swapaxes(a, 1, 2) if tr else a
    shard = jnp.concatenate([as_rows(w[n], tr)[0].astype(bf16) for n, tr in BIG], axis=0)
    exchanges = _GatheredWeights(shard, x[0], ffn1_norm)
    (dh1, dx1), (dw1, _, _, _), small = _local_step(
        x[0], loss_target[0], exchanges, ffn1_norm, mix_norm, ffn2_norm, q_norm, k_norm, attn_sinks,
        rel_bias, pool_w[0], pool_scale)

    nrows1 = len(G1_PIECES) * FS
    for_x, for_y, own1, small_tot = _reduce_scatter_ffn1_head(dw1, _pack_small(small))
    ssem, rsem, for_x, for_y, from_x, from_y, g1, small_tot = _rs1_tail_start(
        for_x, for_y, lax.empty((nrows1, D), bf16), lax.empty((nrows1, D), bf16), ffn1_norm, small_tot)
    gx, _, _ = _norm_bwd(dh1, x[0], g1, dx1, 1.0, "norm1_bwd")
    red_rest = exchanges.mix_ffn2_grads_total(gx)

    grads, deltas, new_m, new_v = {}, {}, {}, {}
    rest = [k for k in range(len(BIG)) if k not in G1_PIECES]
    rows_of = lambda t, ks: [as_rows(t[BIG[k][0]], BIG[k][1]) for k in ks]
    rest_out = _adamw_big(rest, rows_of(w, rest), rows_of(m, rest), rows_of(v, rest), red_rest, "adamw_rest")
    from_x, from_y = _rs1_tail_wait(ssem, rsem, for_x, for_y, from_x, from_y, rest_out[0][0])
    red1 = _rs1_total(own1, from_x, from_y)
    ffn1 = list(G1_PIECES)
    ffn1_out = _adamw_big(ffn1, rows_of(w, ffn1), rows_of(m, ffn1), rows_of(v, ffn1), red1, "adamw_ffn1")
    for ks, out in ((rest, rest_out), (ffn1, ffn1_out)):
        for i, k in enumerate(ks):
            n, tr = BIG[k]
            grads[n], deltas[n], new_m[n], new_v[n] = [as_rows(o[i], tr) for o in out]
    small_names = [n for n in SMALL_NAMES if n != "loss"]
    for n in small_names:
        grads[n] = _unpack_small(small_tot, n)
    ds, nms, nvs = _adamw_small([w[n] for n in small_names], [m[n] for n in small_names], [v[n] for n in small_names],
                                [grads[n] for n in small_names], "adamw_small")
    for i, n in enumerate(small_names):
        deltas[n], new_m[n], new_v[n] = ds[i], nms[i], nvs[i]
    loss = small_tot[LOSS_ROW, 0]
    return (loss, gx[None], *[grads[n] for n in WEIGHTS], *[deltas[n] for n in WEIGHTS],
            *[new_m[n] for n in WEIGHTS], *[new_v[n] for n in WEIGHTS])
```

```python
import jax
import jax.numpy as jnp
import numpy as np
from jax import lax
from jax.experimental import pallas as pl
from jax.experimental.pallas import tpu as pltpu

f32, bf16, i32 = jnp.float32, jnp.bfloat16, jnp.int32
SDS = jax.ShapeDtypeStruct

D = 1024
F = 2816
HD = 64
NH = 8
NKV = 2
GQA = NH // NKV
DATTN = NH * HD
DKV = NKV * HD
DPOOL = 512
POOL_WINDOWS = (2, 4, 8, 16)
PGD = DPOOL // len(POOL_WINDOWS)
DIN = DATTN + 2 * DKV + DPOOL
DMIX = DATTN + DPOOL
BLK = 128
NBUCK = 32
MAX_DISTANCE = 128
EPS = 1e-6
NEG = -1e30
SCALE = HD ** -0.5

ADAM_LR, ADAM_B1, ADAM_B2, ADAM_EPS, ADAM_WD, ADAM_STEP = 0.001, 0.9, 0.999, 1e-08, 0.01, 10

NDEV = 8
FS = F // NDEV
INS = DIN // NDEV
OUTS = DMIX // NDEV
PIECE_ROWS = (FS, FS, FS, INS, OUTS, FS, FS, FS)
PIECE_OFF = tuple(int(v) for v in np.cumsum((0,) + PIECE_ROWS[:-1]))
PACK_ROWS = sum(PIECE_ROWS)

VMEM_LIMIT_V7X = 56 * 1024 * 1024

MESH = pl.DeviceIdType.MESH


def _cparams(sem=None, vmem=None):
    return pltpu.CompilerParams(dimension_semantics=sem, vmem_limit_bytes=vmem)


def _nt(a, b):
    return lax.dot_general(a, b, (((1,), (1,)), ((), ())), preferred_element_type=f32)


def _tn(a, b):
    return lax.dot_general(a, b, (((0,), (0,)), ((), ())), preferred_element_type=f32)


def _nn(a, b):
    return jnp.dot(a, b, preferred_element_type=f32)


def _sigmoid(x):
    return 1.0 / (1.0 + jnp.exp(-x))


def _norm_fwd(x, g, name):
    T = x.shape[0]
    tm = min(512, T)

    def body(x_ref, g_ref, h_ref):
        xv = x_ref[...]
        r = lax.rsqrt(jnp.mean(xv * xv, axis=-1, keepdims=True) + EPS)
        h_ref[...] = (xv * r * g_ref[...]).astype(bf16)

    return pl.pallas_call(
        body, grid=(T // tm,),
        in_specs=[pl.BlockSpec((tm, D), lambda i: (i, 0)), pl.BlockSpec((1, D), lambda i: (0, 0))],
        out_specs=pl.BlockSpec((tm, D), lambda i: (i, 0)),
        out_shape=SDS((T, D), bf16), name=name)(x, g)


def _norm_bwd(dh, x, g, dres, out_scale, name):
    T = x.shape[0]
    tm = min(512, T)

    def body(dh_ref, x_ref, g_ref, dr_ref, dx_ref, dxb_ref, dg_ref):
        i = pl.program_id(0)
        xv = x_ref[...]
        r = lax.rsqrt(jnp.mean(xv * xv, axis=-1, keepdims=True) + EPS)
        xh = xv * r
        dhv = dh_ref[...]
        dxh = dhv * g_ref[...]
        dx = dr_ref[...] + r * (dxh - xh * jnp.mean(dxh * xh, axis=-1, keepdims=True))
        dx_ref[...] = dx
        dxb_ref[...] = (out_scale * dx).astype(bf16)
        dg = jnp.sum(dhv * xh, axis=0, keepdims=True)

        @pl.when(i == 0)
        def _():
            dg_ref[...] = dg

        @pl.when(i > 0)
        def _():
            dg_ref[...] += dg

    tok = pl.BlockSpec((tm, D), lambda i: (i, 0))
    vec = pl.BlockSpec((1, D), lambda i: (0, 0))
    return pl.pallas_call(
        body, grid=(T // tm,),
        in_specs=[tok, tok, vec, tok], out_specs=[tok, tok, vec],
        out_shape=(SDS((T, D), f32), SDS((T, D), bf16), SDS((1, D), f32)),
        compiler_params=_cparams(("arbitrary",)), name=name)(dh, x, g, dres)


def _gain_grad(dh, x, name):
    T = x.shape[0]
    tm = min(512, T)

    def body(dh_ref, x_ref, dg_ref):
        i = pl.program_id(0)
        xv = x_ref[...]
        r = lax.rsqrt(jnp.mean(xv * xv, axis=-1, keepdims=True) + EPS)
        dg = jnp.sum(dh_ref[...] * (xv * r), axis=0, keepdims=True)

        @pl.when(i == 0)
        def _():
            dg_ref[...] = dg

        @pl.when(i > 0)
        def _():
            dg_ref[...] += dg

    tok = pl.BlockSpec((tm, D), lambda i: (i, 0))
    return pl.pallas_call(
        body, grid=(T // tm,), in_specs=[tok, tok], out_specs=pl.BlockSpec((1, D), lambda i: (0, 0)),
        out_shape=SDS((1, D), f32), compiler_params=_cparams(("arbitrary",)), name=name)(dh, x)


FFN_ROW_CHUNK = 256


def _ffn_tiles(T):
    return min(1024, T), 256


def _ffn_fwd(h, w, x, target, name):
    T = h.shape[0]
    tm, tf = _ffn_tiles(T)
    nf = F // tf
    with_loss = target is not None

    def body(*refs):
        if with_loss:
            h_ref, w_ref, x_hbm, t_hbm, xo_ref, g_ref, u_ref, dyb_ref, loss_ref, tbuf, sem = refs
        else:
            h_ref, w_ref, x_hbm, xo_ref, g_ref, u_ref, sem = refs
        fi = pl.program_id(0)

        @pl.when(fi == 0)
        def _():
            cp = pltpu.make_async_copy(x_hbm, xo_ref, sem)
            cp.start()
            cp.wait()

        wgu = w_ref[0:2].reshape(2 * tf, D)
        for r in range(0, T, tm):
            rows = slice(r, r + tm)
            gu = _nt(h_ref[rows, :], wgu)
            gate, up = gu[:, :tf], gu[:, tf:]
            act = gate * _sigmoid(gate) * up
            g_ref[0, rows, :] = gate.astype(bf16)
            u_ref[0, rows, :] = up.astype(bf16)
            xo_ref[rows, :] += _nn((0.5 * act).astype(bf16), w_ref[2])

        if with_loss:
            @pl.when(fi == nf - 1)
            def _():
                lanes = jnp.zeros((1, 128), f32)
                for r in range(0, T, tm):
                    rows = slice(r, r + tm)
                    cp = pltpu.make_async_copy(t_hbm.at[pl.ds(r, tm), :], tbuf, sem)
                    cp.start()
                    cp.wait()
                    e = xo_ref[rows, :] - tbuf[...]
                    dy = e * (1.0 / D)
                    xo_ref[rows, :] = dy
                    dyb_ref[rows, :] = (0.5 * dy).astype(bf16)
                    col = jnp.sum(e * e, axis=0, keepdims=True) * (0.5 / D)
                    for k in range(D // 128):
                        lanes = lanes + col[:, 128 * k:128 * (k + 1)]
                loss_ref[...] = lanes

    tok = pl.BlockSpec((T, D), lambda f: (0, 0))
    act_spec = pl.BlockSpec((1, T, tf), lambda f: (f, 0, 0))
    hbm = pl.BlockSpec(memory_space=pl.ANY)
    in_specs = [tok, pl.BlockSpec((3, tf, D), lambda f: (0, f, 0)), hbm]
    out_specs = [tok, act_spec, act_spec]
    out_shape = [SDS((T, D), f32), SDS((nf, T, tf), bf16), SDS((nf, T, tf), bf16)]
    scratch = [pltpu.SemaphoreType.DMA]
    args = [h, w, x]
    if with_loss:
        in_specs.append(hbm)
        args.append(target)
        out_specs += [tok, pl.BlockSpec((1, 128), lambda f: (0, 0))]
        out_shape += [SDS((T, D), bf16), SDS((1, 128), f32)]
        scratch = [pltpu.VMEM((tm, D), f32)] + scratch
    return pl.pallas_call(
        body, grid=(nf,), in_specs=in_specs, out_specs=out_specs, out_shape=tuple(out_shape), scratch_shapes=scratch,
        compiler_params=_cparams(("arbitrary",), VMEM_LIMIT_V7X), name=name)(*args)


def _ffn_bwd(dob, h, gate, up, w, name):
    T = h.shape[0]
    _, tf = _ffn_tiles(T)
    nf = F // tf

    def body(do_hbm, h_hbm, g_ref, u_ref, w_ref, dh_hbm, dw_ref, do_v, h_v, dh_acc, dgu_s, act_s, sems):
        fi = pl.program_id(0)

        @pl.when(fi == 0)
        def _():
            loads = [pltpu.make_async_copy(do_hbm, do_v, sems.at[0]), pltpu.make_async_copy(h_hbm, h_v, sems.at[1])]
            for cp in loads:
                cp.start()
            dh_acc[...] = jnp.zeros_like(dh_acc)
            for cp in loads:
                cp.wait()

        wgu = w_ref[0:2].reshape(2 * tf, D)
        for r in range(0, T, FFN_ROW_CHUNK):
            rows = slice(r, r + FFN_ROW_CHUNK)
            dov = do_v[rows, :]
            gv = g_ref[0, rows, :].astype(f32)
            uv = u_ref[0, rows, :].astype(f32)
            sg = _sigmoid(gv)
            sil = gv * sg
            dact = _nt(dov, w_ref[2])
            dup = dact * sil
            dgate = dact * uv * (sg * (1.0 + gv * (1.0 - sg)))
            dgu = jnp.concatenate([dgate.astype(bf16), dup.astype(bf16)], axis=1)
            dgu_s[rows, :] = dgu
            act_s[rows, :] = (sil * uv).astype(bf16)
            dh_acc[rows, :] += _nn(dgu, wgu)
        dw_ref[0:2] = _tn(dgu_s[...], h_v[...]).reshape(2, tf, D).astype(bf16)
        dw_ref[2] = _tn(act_s[...], do_v[...]).astype(bf16)

        @pl.when(fi == nf - 1)
        def _():
            out = pltpu.make_async_copy(dh_acc, dh_hbm, sems.at[0])
            out.start()
            out.wait()

    act_spec = pl.BlockSpec((1, T, tf), lambda f: (f, 0, 0))
    wspec = pl.BlockSpec((3, tf, D), lambda f: (0, f, 0))
    hbm = pl.BlockSpec(memory_space=pl.ANY)
    return pl.pallas_call(
        body, grid=(nf,),
        in_specs=[hbm, hbm, act_spec, act_spec, wspec],
        out_specs=[hbm, wspec],
        out_shape=(SDS((T, D), f32), SDS((3, F, D), bf16)),
        scratch_shapes=[pltpu.VMEM((T, D), bf16), pltpu.VMEM((T, D), bf16), pltpu.VMEM((T, D), f32),
                        pltpu.VMEM((T, 2 * tf), bf16), pltpu.VMEM((T, tf), bf16), pltpu.SemaphoreType.DMA((2,))],
        compiler_params=_cparams(("arbitrary",), VMEM_LIMIT_V7X), name=name)(dob, h, gate, up, w)


def _in_proj_fwd(h, wint, name):
    T = h.shape[0]
    tm = min(512, T)

    def body(h_ref, w_ref, z_ref):
        z_ref[...] = _nt(h_ref[...], w_ref[...])

    return pl.pallas_call(
        body, grid=(T // tm,),
        in_specs=[pl.BlockSpec((tm, D), lambda i: (i, 0)), pl.BlockSpec((DIN, D), lambda i: (0, 0))],
        out_specs=pl.BlockSpec((tm, DIN), lambda i: (i, 0)),
        out_shape=SDS((T, DIN), f32), name=name)(h, wint)


def _in_proj_bwd(dz, wint, h, name):
    T = h.shape[0]
    tm = min(512, T)
    nt = T // tm

    def body(dz_ref, w_ref, h_ref, dh_ref, dw_ref, acc):
        i = pl.program_id(0)
        dzb = dz_ref[...].astype(bf16)
        dh_ref[...] = _nn(dzb, w_ref[...])
        part = _tn(dzb, h_ref[...])

        @pl.when(i == 0)
        def _():
            acc[...] = part

        @pl.when(i > 0)
        def _():
            acc[...] += part

        @pl.when(i == nt - 1)
        def _():
            dw_ref[...] = acc[...].astype(bf16)

    wspec = pl.BlockSpec((DIN, D), lambda i: (0, 0))
    return pl.pallas_call(
        body, grid=(nt,),
        in_specs=[pl.BlockSpec((tm, DIN), lambda i: (i, 0)), wspec, pl.BlockSpec((tm, D), lambda i: (i, 0))],
        out_specs=[pl.BlockSpec((tm, D), lambda i: (i, 0)), wspec],
        out_shape=(SDS((T, D), f32), SDS((DIN, D), bf16)),
        scratch_shapes=[pltpu.VMEM((DIN, D), f32)],
        compiler_params=_cparams(("arbitrary",)), name=name)(dz, wint, h)


def _out_proj_fwd(ymix, wout, x, g, name):
    T = x.shape[0]
    tm = min(512, T)

    def body(y_ref, w_ref, x_ref, g_ref, o_ref, h_ref):
        o = x_ref[...] + _nn(y_ref[...], w_ref[...])
        o_ref[...] = o
        r = lax.rsqrt(jnp.mean(o * o, axis=-1, keepdims=True) + EPS)
        h_ref[...] = (o * r * g_ref[...]).astype(bf16)

    tok = pl.BlockSpec((tm, D), lambda i: (i, 0))
    return pl.pallas_call(
        body, grid=(T // tm,),
        in_specs=[pl.BlockSpec((tm, DMIX), lambda i: (i, 0)), pl.BlockSpec((DMIX, D), lambda i: (0, 0)), tok,
                  pl.BlockSpec((1, D), lambda i: (0, 0))],
        out_specs=[tok, tok], out_shape=(SDS((T, D), f32), SDS((T, D), bf16)), name=name)(ymix, wout, x, g)


def _out_proj_bwd(dxb, wout, ymix, name):
    T = dxb.shape[0]
    tm = min(512, T)
    nt = T // tm

    def body(dx_ref, w_ref, y_ref, dy_ref, dw_ref, acc):
        i = pl.program_id(0)
        dxv = dx_ref[...]
        dy_ref[...] = _nt(dxv, w_ref[...])
        part = _tn(y_ref[...], dxv)

        @pl.when(i == 0)
        def _():
            acc[...] = part

        @pl.when(i > 0)
        def _():
            acc[...] += part

        @pl.when(i == nt - 1)
        def _():
            dw_ref[...] = acc[...].astype(bf16)

    wspec = pl.BlockSpec((DMIX, D), lambda i: (0, 0))
    return pl.pallas_call(
        body, grid=(nt,),
        in_specs=[pl.BlockSpec((tm, D), lambda i: (i, 0)), wspec, pl.BlockSpec((tm, DMIX), lambda i: (i, 0))],
        out_specs=[pl.BlockSpec((tm, DMIX), lambda i: (i, 0)), wspec],
        out_shape=(SDS((T, DMIX), f32), SDS((DMIX, D), bf16)),
        scratch_shapes=[pltpu.VMEM((DMIX, D), f32)],
        compiler_params=_cparams(("arbitrary",)), name=name)(dxb, wout, ymix)


def _t5_bucket_table():
    ql = np.arange(BLK)[:, None]
    kl = np.arange(2 * BLK)[None, :]
    n = np.maximum(ql + BLK - kl, 0)
    max_exact = NBUCK // 2
    large = max_exact + (np.log(np.maximum(n, 1) / max_exact) / np.log(MAX_DISTANCE / max_exact)
                         * (NBUCK - max_exact)).astype(np.int32)
    large = np.minimum(large, NBUCK - 1)
    return np.where(n < max_exact, n, large).astype(np.int32)


def _fill_bias(bk_ref, rb_ref, bias_scr):
    bk = bk_ref[...]
    for h in range(NH):
        def step(b, acc, h=h):
            return acc + jnp.where(bk == b, rb_ref[b, h], 0.0)
        bias_scr[h] = lax.fori_loop(0, NBUCK, step, jnp.zeros((BLK, 2 * BLK), f32))


MIX_SUB = 4


class _Window:
    def __init__(self, zc_ref, zp_ref, n, s):
        self.blk = n * MIX_SUB + s
        self.cur = lambda a, b: zc_ref[s * BLK:(s + 1) * BLK, a:b]
        self.prev = (lambda a, b: zp_ref[:, a:b]) if s == 0 else (lambda a, b: zc_ref[(s - 1) * BLK:s * BLK, a:b])


def _attn_qkv(win, kh, qg, kg):
    kc = DATTN + HD * kh
    vc = DATTN + DKV + HD * kh
    kx = jnp.concatenate([win.prev(kc, kc + HD), win.cur(kc, kc + HD)], axis=0)
    vx = jnp.concatenate([win.prev(vc, vc + HD), win.cur(vc, vc + HD)], axis=0)
    qx = jnp.concatenate([win.cur(HD * (GQA * kh + g), HD * (GQA * kh + g + 1)) for g in range(GQA)], axis=0)
    rq = lax.rsqrt(jnp.mean(qx * qx, axis=-1, keepdims=True) + EPS)
    rk = lax.rsqrt(jnp.mean(kx * kx, axis=-1, keepdims=True) + EPS)
    qhat, khat = qx * rq, kx * rk
    return dict(qhat=qhat, khat=khat, rq=rq, rk=rk, qsb=(qhat * (qg * SCALE)).astype(bf16),
                knb=(khat * kg).astype(bf16), vb=vx.astype(bf16))


def _window_masks(n):
    row = lax.broadcasted_iota(i32, (GQA * BLK, 2 * BLK), 0) & (BLK - 1)
    col = lax.broadcasted_iota(i32, (GQA * BLK, 2 * BLK), 1)
    band = (col > row) & (col <= row + BLK)
    return band & ((col >= BLK) | (n > 0)), band


def _attn_probs(a, kh, sk_ref, bias_scr, mask):
    s = _nt(a["qsb"], a["knb"]) + bias_scr[GQA * kh:GQA * (kh + 1)].reshape(GQA * BLK, 2 * BLK)
    s = jnp.where(mask, s, NEG)
    ridx = lax.broadcasted_iota(i32, (GQA * BLK, 1), 0)
    sink = jnp.full((GQA * BLK, 1), sk_ref[GQA * kh + GQA - 1], f32)
    for g in range(GQA - 2, -1, -1):
        sink = jnp.where(ridx < (g + 1) * BLK, sk_ref[GQA * kh + g], sink)
    m = jnp.maximum(jnp.max(s, axis=-1, keepdims=True), sink)
    e = jnp.exp(s - m)
    den = jnp.sum(e, axis=-1, keepdims=True) + jnp.exp(sink - m)
    return e / den


POOL_STEPS = {2: (1,), 4: (1, 2), 8: (1, 2, 4), 16: (1, 2, 4, 8)}


def _pool_group(win, g, w):
    n = win.blk
    c0 = DATTN + 2 * DKV + PGD * g
    uc = win.cur(c0, c0 + PGD)
    up = jnp.where(n > 0, win.prev(c0, c0 + PGD), 0.0)
    sm = jnp.concatenate([up, uc], axis=0)
    for k in POOL_STEPS[w]:
        sm = sm + pltpu.roll(sm, k, axis=0)
    pos = n * BLK + lax.broadcasted_iota(i32, (BLK, 1), 0) + 1
    cnt = jnp.minimum(pos, w).astype(f32)
    return sm[BLK:2 * BLK] / cnt - uc, cnt


def _mix_fwd(z, qg, kg, sinks, relb, bucket, pool_w, pscale, name):
    T = z.shape[0]
    step_rows = MIX_SUB * BLK
    nsteps = T // step_rows

    def body(zc_ref, zp_ref, qg_ref, kg_ref, sk_ref, rb_ref, bk_ref, pw_ref, ps_ref, y_ref, p_ref, bias_scr, yacc):
        n = pl.program_id(0)

        @pl.when(n == 0)
        def _():
            _fill_bias(bk_ref, rb_ref, bias_scr)

        first_mask, mask = _window_masks(n)
        for s in range(MIX_SUB):
            win = _Window(zc_ref, zp_ref, n, s)
            rows = slice(s * BLK, (s + 1) * BLK)
            for kh in range(NKV):
                a = _attn_qkv(win, kh, qg_ref[...], kg_ref[...])
                pb = _attn_probs(a, kh, sk_ref, bias_scr, first_mask if s == 0 else mask).astype(bf16)
                p_ref[s, GQA * kh:GQA * (kh + 1)] = pb.reshape(GQA, BLK, 2 * BLK)
                o = _nn(pb, a["vb"])
                for g in range(GQA):
                    hc = HD * (GQA * kh + g)
                    yacc[rows, hc:hc + HD] = o[g * BLK:(g + 1) * BLK]
            for g, w in enumerate(POOL_WINDOWS):
                pooled, _ = _pool_group(win, g, w)
                yp = _nn(pooled.astype(bf16), pw_ref[g].astype(bf16)) * ps_ref[:, PGD * g:PGD * (g + 1)]
                yacc[rows, DATTN + PGD * g:DATTN + PGD * (g + 1)] = yp
        y_ref[...] = yacc[...].astype(bf16)

    full = lambda *shape: pl.BlockSpec(shape, lambda n: (0,) * len(shape))
    smem = pl.BlockSpec(memory_space=pltpu.SMEM)
    return pl.pallas_call(
        body, grid=(nsteps,),
        in_specs=[pl.BlockSpec((step_rows, DIN), lambda n: (n, 0)),
                  pl.BlockSpec((BLK, DIN), lambda n: (jnp.maximum(n * MIX_SUB - 1, 0), 0)),
                  full(1, HD), full(1, HD), smem, smem, full(BLK, 2 * BLK),
                  full(len(POOL_WINDOWS), PGD, PGD), full(1, DPOOL)],
        out_specs=[pl.BlockSpec((step_rows, DMIX), lambda n: (n, 0)),
                   pl.BlockSpec((MIX_SUB, NH, BLK, 2 * BLK), lambda n: (n, 0, 0, 0))],
        out_shape=(SDS((T, DMIX), bf16), SDS((T // BLK, NH, BLK, 2 * BLK), bf16)),
        scratch_shapes=[pltpu.VMEM((NH, BLK, 2 * BLK), f32), pltpu.VMEM((step_rows, DMIX), f32)],
        compiler_params=_cparams(("arbitrary",)), name=name)(z, z, qg, kg, sinks, relb, bucket, pool_w, pscale)


def _mix_bwd(z, dy, probs, qg, kg, relb, bucket, pool_w, pscale, name):
    T = z.shape[0]
    step_rows = MIX_SUB * BLK
    nsteps = T // step_rows

    def body(zc_ref, zp_ref, dy_ref, p_ref, qg_ref, kg_ref, bk_ref, pw_ref, ps_ref,
             dz_ref, dqg_ref, dkg_ref, dsk_ref, drb_ref, dpw_ref, dps_ref, dbias_scr):
        n = pl.program_id(0)

        @pl.when(n == 0)
        def _():
            dbias_scr[...] = jnp.zeros_like(dbias_scr)
            dqg_ref[...] = jnp.zeros_like(dqg_ref)
            dkg_ref[...] = jnp.zeros_like(dkg_ref)
            dpw_ref[...] = jnp.zeros_like(dpw_ref)
            dps_ref[...] = jnp.zeros_like(dps_ref)

        qg, kg = qg_ref[...], kg_ref[...]
        for s in range(MIX_SUB):
            win = _Window(zc_ref, zp_ref, n, s)
            blk = win.blk
            rows = pl.ds(pl.multiple_of(blk * BLK, BLK), BLK)
            prow = pl.ds(pl.multiple_of(jnp.maximum(blk - 1, 0) * BLK, BLK), BLK)
            dyr = slice(s * BLK, (s + 1) * BLK)

            def into_prev(fn, s=s):
                if s == 0:
                    pl.when(n > 0)(fn)
                else:
                    fn()

            for kh in range(NKV):
                a = _attn_qkv(win, kh, qg, kg)
                pb = p_ref[s, GQA * kh:GQA * (kh + 1)].reshape(GQA * BLK, 2 * BLK)
                p = pb.astype(f32)
                do = jnp.concatenate([dy_ref[dyr, HD * (GQA * kh + g):HD * (GQA * kh + g + 1)] for g in range(GQA)],
                                     axis=0).astype(bf16)
                dv = _tn(pb, do)
                dp = _nt(do, a["vb"])
                delta = jnp.sum(p * dp, axis=-1, keepdims=True)
                ds = p * (dp - delta)
                for g in range(GQA):
                    dbias_scr[GQA * kh + g] += ds[g * BLK:(g + 1) * BLK]
                dsb = ds.astype(bf16)
                dqn = _nn(dsb, a["knb"]) * SCALE
                dkn = _tn(dsb, a["qsb"])
                qhat, khat = a["qhat"], a["khat"]
                dqg_ref[...] += jnp.sum(dqn * qhat, axis=0, keepdims=True)
                dkg_ref[...] += jnp.sum(dkn * khat, axis=0, keepdims=True)
                dqh = dqn * qg
                dq = a["rq"] * (dqh - qhat * jnp.mean(dqh * qhat, axis=-1, keepdims=True))
                dkh = dkn * kg
                dk = a["rk"] * (dkh - khat * jnp.mean(dkh * khat, axis=-1, keepdims=True))
                kc = DATTN + HD * kh
                vc = DATTN + DKV + HD * kh
                for g in range(GQA):
                    hc = HD * (GQA * kh + g)
                    dz_ref[rows, hc:hc + HD] = dq[g * BLK:(g + 1) * BLK]
                dz_ref[rows, kc:kc + HD] = dk[BLK:2 * BLK]
                dz_ref[rows, vc:vc + HD] = dv[BLK:2 * BLK]

                def kv_prev(dk=dk, dv=dv, kc=kc, vc=vc, prow=prow):
                    dz_ref[prow, kc:kc + HD] += dk[0:BLK]
                    dz_ref[prow, vc:vc + HD] += dv[0:BLK]

                into_prev(kv_prev)

            for g, w in enumerate(POOL_WINDOWS):
                c0 = DATTN + 2 * DKV + PGD * g
                pooled, cnt = _pool_group(win, g, w)
                pb = pooled.astype(bf16)
                wb = pw_ref[g].astype(bf16)
                dyp = dy_ref[dyr, DATTN + PGD * g:DATTN + PGD * (g + 1)]
                ypre = _nn(pb, wb)
                dps_ref[:, PGD * g:PGD * (g + 1)] += jnp.sum(dyp * ypre, axis=0, keepdims=True)
                dyg = (dyp * ps_ref[:, PGD * g:PGD * (g + 1)]).astype(bf16)
                dpw_ref[g] += _tn(pb, dyg)
                dpooled = _nt(dyg, wb)
                due = jnp.concatenate([jnp.zeros((BLK, PGD), f32), dpooled / cnt], axis=0)
                for k in POOL_STEPS[w]:
                    due = due + pltpu.roll(due, 2 * BLK - k, axis=0)
                dz_ref[rows, c0:c0 + PGD] = due[BLK:2 * BLK] - dpooled

                def pool_prev(due=due, c0=c0, prow=prow):
                    dz_ref[prow, c0:c0 + PGD] += due[0:BLK]

                into_prev(pool_prev)

        @pl.when(n == nsteps - 1)
        def _():
            bk = bk_ref[...]
            ri = lax.broadcasted_iota(i32, (NBUCK, NH), 0)
            ci = lax.broadcasted_iota(i32, (NBUCK, NH), 1)

            def step(b, acc):
                for h in range(NH):
                    sel = jnp.where(bk == b, dbias_scr[h], 0.0)
                    tot = jnp.sum(jnp.sum(sel, axis=1, keepdims=True), axis=0, keepdims=True)
                    acc = acc + jnp.where((ri == b) & (ci == h), tot, 0.0)
                return acc

            drb_ref[...] = lax.fori_loop(0, NBUCK, step, jnp.zeros((NBUCK, NH), f32))
            lane = lax.broadcasted_iota(i32, (1, 128), 1)
            dsk = jnp.zeros((1, 128), f32)
            for h in range(NH):
                tot = jnp.sum(jnp.sum(dbias_scr[h], axis=1, keepdims=True), axis=0, keepdims=True)
                dsk = dsk - jnp.where(lane == h, tot, 0.0)
            dsk_ref[...] = dsk

    full = lambda *shape: pl.BlockSpec(shape, lambda n: (0,) * len(shape))
    npg = len(POOL_WINDOWS)
    return pl.pallas_call(
        body, grid=(nsteps,),
        in_specs=[pl.BlockSpec((step_rows, DIN), lambda n: (n, 0)),
                  pl.BlockSpec((BLK, DIN), lambda n: (jnp.maximum(n * MIX_SUB - 1, 0), 0)),
                  pl.BlockSpec((step_rows, DMIX), lambda n: (n, 0)),
                  pl.BlockSpec((MIX_SUB, NH, BLK, 2 * BLK), lambda n: (n, 0, 0, 0)),
                  full(1, HD), full(1, HD), full(BLK, 2 * BLK), full(npg, PGD, PGD), full(1, DPOOL)],
        out_specs=[full(T, DIN), full(1, HD), full(1, HD), full(1, 128), full(NBUCK, NH),
                   full(npg, PGD, PGD), full(1, DPOOL)],
        out_shape=(SDS((T, DIN), f32), SDS((1, HD), f32), SDS((1, HD), f32), SDS((1, 128), f32),
                   SDS((NBUCK, NH), f32), SDS((npg, PGD, PGD), f32), SDS((1, DPOOL), f32)),
        scratch_shapes=[pltpu.VMEM((NH, BLK, 2 * BLK), f32)],
        compiler_params=_cparams(("arbitrary",), VMEM_LIMIT_V7X),
        name=name)(z, z, dy, probs, qg, kg, bucket, pool_w, pscale)


class _LocalWeights:
    def __init__(self, w1, wint, wout, w2):
        self.w1, self.wint, self.wout, self.w2 = w1, wint, wout, w2

    def ffn1(self):
        return self.w1

    def first_norm(self, x, gain):
        return _norm_fwd(x, gain, "norm1_fwd")

    def after_ffn1(self, gain, x1):
        return gain

    def mix(self, after):
        return self.wint, self.wout

    def before_out_proj(self, wout, after):
        return wout

    def ffn2(self, after):
        return self.w2

    def mix_ffn2_grads_ready(self, dwint, dwout, dw2, dh2):
        self.grads_rest = (dwint, dwout, dw2)
        return dh2

    def before_ffn1_bwd(self, dx1b):
        return dx1b


def _local_step(x, target, weights, g1, gm, g3, qg, kg, sinks, relb, pool_w, pscale):
    bucket = jnp.asarray(_t5_bucket_table())
    sk = sinks.reshape(NH)
    w1 = weights.ffn1()
    h1 = weights.first_norm(x, g1)
    x1, gate1, up1 = _ffn_fwd(h1, w1, x, None, "ffn1_fwd")
    h2 = _norm_fwd(x1, weights.after_ffn1(gm, x1), "norm2_fwd")
    wint, wout = weights.mix(h2)
    z = _in_proj_fwd(h2, wint, "in_proj_fwd")
    ymix, probs = _mix_fwd(z, qg, kg, sk, relb, bucket, pool_w, pscale, "mix_fwd")
    wout = weights.before_out_proj(wout, ymix)
    x2, h3 = _out_proj_fwd(ymix, wout, x1, g3, "out_proj_fwd")
    w2 = weights.ffn2(h3)
    dy, gate2, up2, dyb, loss_lanes = _ffn_fwd(h3, w2, x2, target, "ffn2_fwd")

    dh3, dw2 = _ffn_bwd(dyb, h3, gate2, up2, w2, "ffn2_bwd")
    dx2, dx2b, dg3 = _norm_bwd(dh3, x2, g3, dy, 1.0, "norm3_bwd")
    dymix, dwout = _out_proj_bwd(dx2b, wout, ymix, "out_proj_bwd")
    dz, dqg, dkg, dsk, drb, dpw, dps = _mix_bwd(z, dymix, probs, qg, kg, relb, bucket, pool_w, pscale, "mix_bwd")
    dh2, dwint = _in_proj_bwd(dz, wint, h2, "in_proj_bwd")
    dh2 = weights.mix_ffn2_grads_ready(dwint, dwout, dw2, dh2)
    dx1, dx1b, dgm = _norm_bwd(dh2, x1, gm, dx2, 0.5, "norm2_bwd")
    dx1b = weights.before_ffn1_bwd(dx1b)
    dh1, dw1 = _ffn_bwd(dx1b, h1, gate1, up1, w1, "ffn1_bwd")
    dg1 = _gain_grad(dh1, x, "norm1_gain_grad")
    small = dict(ffn1_norm=dg1, mix_norm=dgm, ffn2_norm=dg3, pool_scale=dps, q_norm=dqg, k_norm=dkg,
                 attn_sinks=dsk[:, :NH], rel_bias=drb, pool_w=dpw, loss=loss_lanes)
    return (dh1, dx1), (dw1, dwint, dwout, dw2), small


SMALL_NAMES = ("ffn1_norm", "mix_norm", "ffn2_norm", "pool_scale", "q_norm", "k_norm", "attn_sinks", "rel_bias",
               "pool_w", "loss")
SMALL_SHAPES = dict(ffn1_norm=(1, D), mix_norm=(1, D), ffn2_norm=(1, D), pool_scale=(1, DPOOL), q_norm=(1, HD),
                    k_norm=(1, HD), attn_sinks=(1, NH), rel_bias=(NBUCK, NH),
                    pool_w=(1, len(POOL_WINDOWS), PGD, PGD), loss=(1, 128))


def _small_rows(name):
    return -(-int(np.prod(SMALL_SHAPES[name])) // 128)


SMALL_OFF = {}
_r = 0
for _n in SMALL_NAMES:
    SMALL_OFF[_n] = _r
    _r += _small_rows(_n)
SMALL_ROWS = -(-_r // 8) * 8
LOSS_ROW = SMALL_OFF["loss"]


def _pack_small(vals):
    parts = []
    for n in SMALL_NAMES:
        size = _small_rows(n) * 128
        if n in vals:
            flat = vals[n].astype(f32).reshape(-1)
            parts.append(jnp.pad(flat, (0, size - flat.shape[0])))
        else:
            parts.append(jnp.zeros((size,), f32))
    flat = jnp.concatenate(parts)
    flat = jnp.pad(flat, (0, SMALL_ROWS * 128 - flat.shape[0]))
    return flat.reshape(SMALL_ROWS, 128)


def _unpack_small(packed, name):
    size = int(np.prod(SMALL_SHAPES[name]))
    r0 = SMALL_OFF[name]
    return packed[r0:r0 + _small_rows(name)].reshape(-1)[:size].reshape(SMALL_SHAPES[name])


def _position():
    return lax.axis_index("x"), lax.axis_index("y"), lax.axis_index("c")


def _dev_index(x, y, c):
    return 4 * x + 2 * y + c


G1_PIECES, MIX_PIECES, F2_PIECES = (0, 1, 2), (3, 4), (5, 6, 7)


def _group_rows(pieces):
    return sum(PIECE_ROWS[k] for k in pieces)


def _shard_piece(s_ref, k):
    return s_ref.at[pl.ds(PIECE_OFF[k], PIECE_ROWS[k]), :]


def _shard_group(s_ref, pieces):
    return s_ref.at[pl.ds(PIECE_OFF[pieces[0]], _group_rows(pieces)), :]


def _weight_pieces(w1_ref=None, wi_ref=None, wo_ref=None, w2_ref=None):
    arrs = {}
    if w1_ref is not None:
        arrs.update({0: w1_ref.at[0], 1: w1_ref.at[1], 2: w1_ref.at[2]})
    if wi_ref is not None:
        arrs[3] = wi_ref
    if wo_ref is not None:
        arrs[4] = wo_ref
    if w2_ref is not None:
        arrs.update({5: w2_ref.at[0], 6: w2_ref.at[1], 7: w2_ref.at[2]})
    return arrs


def _block_rows(arrs, k, dev):
    r = PIECE_ROWS[k]
    return arrs[k].at[pl.ds(pl.multiple_of(_dev_index(*dev) * r, 16), r), :]


NORM_ROWS = 512


def _all_gather_ffn1(shard, x, gain):
    pieces = G1_PIECES
    rest_pieces = MIX_PIECES + F2_PIECES
    half = FS // 2
    T = x.shape[0]
    SIB, X0, X1, Y0, Y1, RELAY_Y, RELAY_X, ON_X, ON_Y, ON_D0, ON_D1 = range(11)

    def body(s_ref, x_ref, g_ref, w1_ref, h_ref, wi_ref, wo_ref, w2_ref, xbuf, hbuf, rest_buf,
             send_sems, recv_sems, local_sem, norm_sems):
        x, y, c = _position()
        me, sib = (x, y, c), (x, y, 1 - c)
        xn, yn, dg = (1 - x, y, c), (x, 1 - y, c), (1 - x, 1 - y, c)
        arrs = _weight_pieces(w1_ref=w1_ref)

        def place_rest():
            rest = _weight_pieces(wi_ref=wi_ref, wo_ref=wo_ref, w2_ref=w2_ref)
            grp = _shard_group(s_ref, rest_pieces)
            load = pltpu.make_async_copy(grp, rest_buf, norm_sems.at[0])
            load.start()
            load.wait()
            base = PIECE_OFF[rest_pieces[0]]
            for k in rest_pieces:
                pltpu.make_async_copy(rest_buf.at[pl.ds(PIECE_OFF[k] - base, PIECE_ROWS[k]), :],
                                      _block_rows(rest, k, me), norm_sems.at[1]).start()
            pltpu.make_async_copy(grp, rest_buf, norm_sems.at[1]).wait()

        def first_norm():
            for r in range(0, T, NORM_ROWS):
                load = pltpu.make_async_copy(x_ref.at[pl.ds(r, NORM_ROWS), :], xbuf, norm_sems.at[0])
                load.start()
                load.wait()
                xv = xbuf[...]
                rs = lax.rsqrt(jnp.mean(xv * xv, axis=-1, keepdims=True) + EPS)
                hbuf[...] = (xv * rs * g_ref[...]).astype(bf16)
                store = pltpu.make_async_copy(hbuf, h_ref.at[pl.ds(r, NORM_ROWS), :], norm_sems.at[1])
                store.start()
                store.wait()

        def rows_of(k, block, hf):
            r = PIECE_ROWS[k]
            start, size = (0, r) if hf is None else (hf * half, half)
            return arrs[k].at[pl.ds(pl.multiple_of(_dev_index(*block) * r + start, 16), size), :]

        def copies(rel, block, hf, to, from_shard=False):
            def src(k):
                if not from_shard:
                    return rows_of(k, block, hf)
                start, size = (0, PIECE_ROWS[k]) if hf is None else (hf * half, half)
                return s_ref.at[pl.ds(PIECE_OFF[k] + start, size), :]
            return [pltpu.make_async_remote_copy(
                src_ref=src(k), dst_ref=rows_of(k, block, hf), send_sem=send_sems.at[rel], recv_sem=recv_sems.at[rel],
                device_id=to, device_id_type=MESH) for k in pieces]

        def waiter(rel, hf):
            nrows = len(pieces) * (FS if hf is None else half)
            grp = s_ref.at[pl.ds(0, nrows), :]
            return pltpu.make_async_remote_copy(src_ref=grp, dst_ref=grp, send_sem=send_sems.at[rel],
                                                recv_sem=recv_sems.at[rel], device_id=me, device_id_type=MESH)

        def start(cps):
            for cp in cps:
                cp.start()

        mine = [pltpu.make_async_copy(_shard_piece(s_ref, k), _block_rows(arrs, k, me), local_sem) for k in pieces]
        start(mine)
        start(copies(SIB, me, None, sib, True))
        start(copies(X0, me, 0, xn, True))
        start(copies(Y1, me, 1, yn, True))
        start(copies(X1, me, 1, xn, True))
        start(copies(Y0, me, 0, yn, True))
        first_norm()
        place_rest()
        waiter(X0, 0).wait_recv()
        start(copies(RELAY_Y, xn, 0, yn))
        waiter(Y1, 1).wait_recv()
        start(copies(RELAY_X, yn, 1, xn))
        waiter(X1, 1).wait_recv()
        start(copies(ON_X, xn, None, sib))
        waiter(Y0, 0).wait_recv()
        start(copies(ON_Y, yn, None, sib))
        waiter(RELAY_Y, 0).wait_recv()
        start(copies(ON_D0, dg, 0, sib))
        waiter(RELAY_X, 1).wait_recv()
        start(copies(ON_D1, dg, 1, sib))
        waiter(SIB, None).wait_recv()
        waiter(ON_X, None).wait_recv()
        waiter(ON_Y, None).wait_recv()
        waiter(ON_D0, 0).wait_recv()
        waiter(ON_D1, 1).wait_recv()
        for rel, hf in ((SIB, None), (X0, 0), (X1, 1), (Y0, 0), (Y1, 1), (RELAY_Y, 0), (RELAY_X, 1),
                        (ON_X, None), (ON_Y, None), (ON_D0, 0), (ON_D1, 1)):
            waiter(rel, hf).wait_send()
        grp = _shard_group(s_ref, pieces)
        pltpu.make_async_copy(grp, grp, local_sem).wait()

    hbm = pl.BlockSpec(memory_space=pl.ANY)
    return pl.pallas_call(
        body, in_specs=[hbm, hbm, pl.BlockSpec(memory_space=pltpu.VMEM)], out_specs=[hbm] * 5,
        out_shape=(SDS((3, F, D), bf16), SDS((T, D), bf16),
                   SDS((DIN, D), bf16), SDS((DMIX, D), bf16), SDS((3, F, D), bf16)),
        scratch_shapes=[pltpu.VMEM((NORM_ROWS, D), f32), pltpu.VMEM((NORM_ROWS, D), bf16),
                        pltpu.VMEM((_group_rows(rest_pieces), D), bf16),
                        pltpu.SemaphoreType.DMA((11,)), pltpu.SemaphoreType.DMA((11,)), pltpu.SemaphoreType.DMA,
                        pltpu.SemaphoreType.DMA((2,))],
        compiler_params=pltpu.CompilerParams(has_side_effects=True),
        name="all_gather_ffn1")(shard, x, gain)


HBM_SPEC = pl.BlockSpec(memory_space=pltpu.HBM)
SEM_SPEC = pl.BlockSpec(memory_space=pltpu.SEMAPHORE)
ANY_SPEC = pl.BlockSpec(memory_space=pl.ANY)
SPLIT_EFFECT = pltpu.SideEffectType.DATAFLOW_SIDE_EFFECTING


def _in_hbm(a):
    return pltpu.with_memory_space_constraint(a, pltpu.HBM)


def _hbm_like(a):
    return pltpu.HBM(a.shape, a.dtype)


def _gather_rest_start(shard, wi, wo, w2, w1):
    def body(s_ref, wi_ref, wo_ref, w2_ref, w1_ref,
             ssem_m, rsem_m0, rsem_m, ssem_f, rsem_f0, rsem_f, s_o, wi_o, wo_o, w2_o, w1_o):
        x, y, c = _position()
        me, sib = (x, y, c), (x, y, 1 - c)
        chips = [(1 - x, y), (x, 1 - y), (1 - x, 1 - y)]
        arrs = _weight_pieces(wi_ref=wi_ref, wo_ref=wo_ref, w2_ref=w2_ref)
        for pieces, ssem, rsem0, rsem in ((MIX_PIECES, ssem_m, rsem_m0, rsem_m), (F2_PIECES, ssem_f, rsem_f0, rsem_f)):
            for p in pieces:
                pltpu.make_async_remote_copy(
                    src_ref=_shard_piece(s_ref, p), dst_ref=_block_rows(arrs, p, me), send_sem=ssem.at[0],
                    recv_sem=rsem0, device_id=sib, device_id_type=MESH).start()
            for j, chip in enumerate(chips):
                for p in pieces:
                    pltpu.make_async_remote_copy(
                        src_ref=_shard_piece(s_ref, p), dst_ref=_block_rows(arrs, p, me), send_sem=ssem.at[1 + j],
                        recv_sem=rsem.at[j], device_id=(*chip, c), device_id_type=MESH).start()

    dma = pltpu.SemaphoreType.DMA
    return pl.pallas_call(
        body, name="gather_rest_start",
        out_shape=(dma((4,)), dma(()), dma((3,)), dma((4,)), dma(()), dma((3,)),
                   _hbm_like(shard), _hbm_like(wi), _hbm_like(wo), _hbm_like(w2), _hbm_like(w1)),
        in_specs=(HBM_SPEC,) * 5, out_specs=(SEM_SPEC,) * 6 + (HBM_SPEC,) * 5,
        input_output_aliases={0: 6, 1: 7, 2: 8, 3: 9, 4: 10},
        compiler_params=pltpu.CompilerParams(has_side_effects=SPLIT_EFFECT),
    )(_in_hbm(shard), _in_hbm(wi), _in_hbm(wo), _in_hbm(w2), _in_hbm(w1))


def _gather_mix_pass_on(rsem_m, wi, wo, thru, after):
    def body(wi_ref, wo_ref, thru_ref, rsem, after_ref, fsend, frecv, wi_o, wo_o, thru_o):
        x, y, c = _position()
        sib = (x, y, 1 - c)
        arrs = _weight_pieces(wi_ref=wi_ref, wo_ref=wo_ref)
        both = wi_ref.at[pl.ds(0, _group_rows(MIX_PIECES)), :]
        for j, chip in enumerate([(1 - x, y), (x, 1 - y), (1 - x, 1 - y)]):
            pltpu.make_async_remote_copy(src_ref=both, dst_ref=both, send_sem=fsend.at[j], recv_sem=rsem.at[j],
                                         device_id=(x, y, c), device_id_type=MESH).wait_recv()
            for p in MIX_PIECES:
                rows = _block_rows(arrs, p, (*chip, c))
                pltpu.make_async_remote_copy(src_ref=rows, dst_ref=rows, send_sem=fsend.at[j], recv_sem=frecv.at[j],
                                             device_id=sib, device_id_type=MESH).start()

    dma = pltpu.SemaphoreType.DMA
    return pl.pallas_call(
        body, name="gather_mix_pass_on",
        out_shape=(dma((3,)), dma((3,)), _hbm_like(wi), _hbm_like(wo), _hbm_like(thru)),
        in_specs=(HBM_SPEC, HBM_SPEC, HBM_SPEC, SEM_SPEC, ANY_SPEC), out_specs=(SEM_SPEC, SEM_SPEC) + (HBM_SPEC,) * 3,
        input_output_aliases={0: 2, 1: 3, 2: 4},
        compiler_params=pltpu.CompilerParams(has_side_effects=SPLIT_EFFECT),
    )(wi, wo, _in_hbm(thru), rsem_m, after)


def _gather_mix_wait(ssem_m, rsem_m0, fsend, frecv, shard, wi, wo, after):
    def body(s_ref, wi_ref, wo_ref, ssem, rsem0, fs, fr, after_ref, s_o, wi_o, wo_o):
        x, y, c = _position()
        grp = _shard_group(s_ref, MIX_PIECES)

        def waiter(send_sem, recv_sem):
            return pltpu.make_async_remote_copy(src_ref=grp, dst_ref=grp, send_sem=send_sem, recv_sem=recv_sem,
                                                device_id=(x, y, c), device_id_type=MESH)

        waiter(ssem.at[0], rsem0).wait_recv()
        for j in range(3):
            waiter(fs.at[j], fr.at[j]).wait_recv()
        for rel in range(4):
            waiter(ssem.at[rel], rsem0).wait_send()
        for j in range(3):
            waiter(fs.at[j], fr.at[j]).wait_send()

    return pl.pallas_call(
        body, name="gather_mix_wait", out_shape=(_hbm_like(shard), _hbm_like(wi), _hbm_like(wo)),
        in_specs=(HBM_SPEC,) * 3 + (SEM_SPEC,) * 4 + (ANY_SPEC,), out_specs=(HBM_SPEC,) * 3,
        input_output_aliases={0: 0, 1: 1, 2: 2},
        compiler_params=pltpu.CompilerParams(has_side_effects=SPLIT_EFFECT),
    )(shard, wi, wo, ssem_m, rsem_m0, fsend, frecv, after)


def _gather_ffn2_pass_on(rsem_f, w2, wo, after):
    def body(w2_ref, wo_ref, rsem, after_ref, fsend, frecv, w2_o, wo_o):
        x, y, c = _position()
        sib = (x, y, 1 - c)
        chips = [(1 - x, y), (x, 1 - y), (1 - x, 1 - y)]
        arrs = _weight_pieces(w2_ref=w2_ref)
        three = w2_ref.at[0, pl.ds(0, _group_rows(F2_PIECES)), :]
        for j, chip in enumerate(chips):
            pltpu.make_async_remote_copy(src_ref=three, dst_ref=three, send_sem=fsend.at[j], recv_sem=rsem.at[j],
                                         device_id=(x, y, c), device_id_type=MESH).wait_recv()
            for p in F2_PIECES:
                rows = _block_rows(arrs, p, (*chip, c))
                pltpu.make_async_remote_copy(src_ref=rows, dst_ref=rows, send_sem=fsend.at[j], recv_sem=frecv.at[j],
                                             device_id=sib, device_id_type=MESH).start()

    dma = pltpu.SemaphoreType.DMA
    return pl.pallas_call(
        body, name="gather_ffn2_pass_on", out_shape=(dma((3,)), dma((3,)), _hbm_like(w2), _hbm_like(wo)),
        in_specs=(HBM_SPEC, HBM_SPEC, SEM_SPEC, ANY_SPEC), out_specs=(SEM_SPEC, SEM_SPEC, HBM_SPEC, HBM_SPEC),
        input_output_aliases={0: 2, 1: 3},
        compiler_params=pltpu.CompilerParams(has_side_effects=SPLIT_EFFECT),
    )(w2, wo, rsem_f, after)


def _gather_ffn2_wait(ssem_f, rsem_f0, fsend, frecv, shard, w2, after):
    def body(s_ref, w2_ref, ssem, rsem0, fs, fr, after_ref, w2_o):
        x, y, c = _position()
        grp = _shard_group(s_ref, F2_PIECES)

        def waiter(send_sem, recv_sem):
            return pltpu.make_async_remote_copy(src_ref=grp, dst_ref=grp, send_sem=send_sem, recv_sem=recv_sem,
                                                device_id=(x, y, c), device_id_type=MESH)

        waiter(ssem.at[0], rsem0).wait_recv()
        for j in range(3):
            waiter(fs.at[j], fr.at[j]).wait_recv()
        for rel in range(4):
            waiter(ssem.at[rel], rsem0).wait_send()
        for j in range(3):
            waiter(fs.at[j], fr.at[j]).wait_send()

    return pl.pallas_call(
        body, name="gather_ffn2_wait", out_shape=_hbm_like(w2),
        in_specs=(HBM_SPEC, HBM_SPEC, SEM_SPEC, SEM_SPEC, SEM_SPEC, SEM_SPEC, ANY_SPEC), out_specs=HBM_SPEC,
        input_output_aliases={1: 0},
        compiler_params=pltpu.CompilerParams(has_side_effects=SPLIT_EFFECT),
    )(shard, w2, ssem_f, rsem_f0, fsend, frecv, after)


class _GatheredWeights(_LocalWeights):
    def __init__(self, shard, x, gain1):
        w1, self.h1, wi, wo, w2 = _all_gather_ffn1(shard, x, gain1)
        (self.ssem_m, self.rsem_m0, self.rsem_m, self.ssem_f, self.rsem_f0, self.rsem_f,
         self.shard, self.wi, self.wo, self.w2_part, self.w1) = _gather_rest_start(shard, wi, wo, w2, w1)

    def first_norm(self, x, gain):
        return self.h1

    def after_ffn1(self, gain, x1):
        self.fsend_m, self.frecv_m, self.wi, self.wo, gain = _gather_mix_pass_on(self.rsem_m, self.wi, self.wo, gain, x1)
        return gain

    def mix(self, after):
        self.shard, wint, wout = _gather_mix_wait(self.ssem_m, self.rsem_m0, self.fsend_m, self.frecv_m, self.shard,
                                                  self.wi, self.wo, after)
        return wint, wout

    def before_out_proj(self, wout, after):
        self.fsend, self.frecv, self.w2_part, wout = _gather_ffn2_pass_on(self.rsem_f, self.w2_part, wout, after)
        return wout

    def ffn2(self, after):
        return _gather_ffn2_wait(self.ssem_f, self.rsem_f0, self.fsend, self.frecv, self.shard, self.w2_part, after)

    def mix_ffn2_grads_ready(self, dwint, dwout, dw2, dh2):
        rx1 = lax.empty((4, RSA_ROWS, D), bf16)
        self.sa, self.ra, dwint, dwout, dw2, rx1, dh2 = _rsa_level1_start(dwint, dwout, dw2, rx1, dh2)
        self.level1 = (dwint, dwout, dw2, rx1)
        return dh2

    def before_ffn1_bwd(self, dx1b):
        dwint, dwout, dw2, rx1 = _rsa_level1_wait(self.sa, self.ra, *self.level1, dx1b)
        tx, self.acc = _rsa_chip_sums(dwint, dwout, dw2, rx1)
        rx2 = lax.empty((3, RSA_ROWS, D), bf16)
        self.sb, self.rb, self.tx, self.rx2, dx1b = _rsa_level2_start(tx, rx2, dx1b)
        return dx1b

    def mix_ffn2_grads_parts(self, after):
        rx2 = _rsa_level2_wait(self.sb, self.rb, self.tx, self.rx2, after)
        return self.acc, rx2


def _reduce_scatter_ffn1_head(dw1, small_packed):
    pieces = G1_PIECES
    half = FS // 2
    hrows = len(pieces) * half
    nrows = 2 * hrows
    X_RELAY, Y_RELAY = range(2)

    def body(d1_ref, p_ref, forx_ref, fory_ref, own_ref, rx1_ref, relx_ref, rely_ref, tot_ref,
             own_buf, rx_buf, tx1, tx2, tx3, acc, sa, ra, sb, rb, lsem, pair, chips, small_send, small_recv):
        x, y, c = _position()
        me, sib = (x, y, c), (x, y, 1 - c)
        xn, yn = (1 - x, y, c), (x, 1 - y, c)
        rel_chips = [(x, y), (1 - x, y), (x, 1 - y), (1 - x, 1 - y)]
        srcs = _weight_pieces(w1_ref=d1_ref)

        my_chip = 2 * x + y
        pair[c] = p_ref[...]
        swap = pltpu.make_async_remote_copy(
            src_ref=p_ref, dst_ref=pair.at[c], send_sem=small_send.at[0], recv_sem=small_recv.at[0],
            device_id=sib, device_id_type=MESH)
        swap.start()
        small = [pltpu.make_async_remote_copy(
            src_ref=chips.at[my_chip], dst_ref=chips.at[my_chip], send_sem=small_send.at[j], recv_sem=small_recv.at[j],
            device_id=(*rel_chips[j], c), device_id_type=MESH) for j in (1, 2, 3)]

        def part(k, dev, hf):
            r = PIECE_ROWS[k]
            return srcs[k].at[pl.ds(pl.multiple_of(_dev_index(*dev) * r + hf * half, 16), half), :]

        def slot(ref, k, hf):
            return ref.at[pl.ds(hf * hrows + k * half, half), :]

        halves = [(k, hf) for hf in (0, 1) for k in pieces]

        for j in (3, 1, 2, 0):
            for k, hf in halves:
                pltpu.make_async_remote_copy(
                    src_ref=part(k, (*rel_chips[j], 1 - c), hf), dst_ref=slot(rx1_ref.at[j], k, hf),
                    send_sem=sa.at[j], recv_sem=ra.at[j], device_id=sib, device_id_type=MESH).start()

        def wait_a(j):
            return pltpu.make_async_remote_copy(src_ref=rx1_ref.at[j], dst_ref=rx1_ref.at[j], send_sem=sa.at[j],
                                                recv_sem=ra.at[j], device_id=me, device_id_type=MESH)

        def ici(rel, src, dst, to):
            return pltpu.make_async_remote_copy(src_ref=src, dst_ref=dst, send_sem=sb.at[rel], recv_sem=rb.at[rel],
                                                device_id=to, device_id_type=MESH)

        first, second = pl.ds(0, hrows), pl.ds(hrows, hrows)
        sends = {
            X_RELAY: ici(X_RELAY, tx3.at[first, :], relx_ref, xn),
            Y_RELAY: ici(Y_RELAY, tx3.at[second, :], rely_ref, yn),
        }

        swap.wait_recv()
        chips[my_chip] = pair[0] + pair[1]
        for cp in small:
            cp.start()

        def chip_sum(j, dst):
            loads = [pltpu.make_async_copy(part(k, (*rel_chips[j], c), hf), slot(own_buf, k, hf), lsem.at[0])
                     for k, hf in halves]
            for cp in loads:
                cp.start()
            wait_a(j).wait_recv()
            got = pltpu.make_async_copy(rx1_ref.at[j], rx_buf, lsem.at[1])
            got.start()
            pltpu.make_async_copy(rx_buf, rx_buf, lsem.at[0]).wait()
            got.wait()

            def add(i, carry):
                rows = pl.ds(pl.multiple_of(i * half, 16), half)
                tot = own_buf[rows, :].astype(f32) + rx_buf[rows, :].astype(f32)
                dst[rows, :] = tot.astype(dst.dtype)
                return carry

            lax.fori_loop(0, nrows // half, add, 0)

        def add_landed(landed, dst, rows0, nrows_):
            got = pltpu.make_async_copy(landed, rx_buf.at[pl.ds(0, nrows_), :], lsem.at[1])
            got.start()
            got.wait()

            def add(i, carry):
                src_rows = pl.ds(pl.multiple_of(i * half, 16), half)
                dst_rows = pl.ds(pl.multiple_of(rows0 + i * half, 16), half)
                dst[dst_rows, :] = (dst[dst_rows, :].astype(f32) + rx_buf[src_rows, :].astype(f32)).astype(dst.dtype)
                return carry

            lax.fori_loop(0, nrows_ // half, add, 0)

        chip_sum(3, tx3)
        sends[X_RELAY].start()
        sends[Y_RELAY].start()
        chip_sum(1, tx1)
        chip_sum(2, tx2)
        chip_sum(0, acc)
        own_out = pltpu.make_async_copy(acc, own_ref, lsem.at[0])
        own_out.start()
        sends[X_RELAY].wait_recv()
        add_landed(relx_ref, tx2, 0, hrows)
        sends[Y_RELAY].wait_recv()
        add_landed(rely_ref, tx1, hrows, hrows)
        own_out.wait()
        outs = [pltpu.make_async_copy(tx1, forx_ref, lsem.at[0]), pltpu.make_async_copy(tx2, fory_ref, lsem.at[1])]
        for cp in outs:
            cp.start()
        for cp in outs:
            cp.wait()
        for cp in small:
            cp.wait_recv()
        tot = (chips[0] + chips[1]) + (chips[2] + chips[3])
        tot_ref[...] = tot
        loss = jnp.sum(tot[LOSS_ROW:LOSS_ROW + 1, :], axis=-1, keepdims=True)
        tot_ref[LOSS_ROW:LOSS_ROW + 1, :] = jnp.broadcast_to(loss, (1, 128))
        for j in range(4):
            wait_a(j).wait_send()
        for cp in sends.values():
            cp.wait_send()
        swap.wait_send()
        for cp in small:
            cp.wait_send()

    hbm = pl.BlockSpec(memory_space=pl.ANY)
    vm = pl.BlockSpec(memory_space=pltpu.VMEM)
    outs = pl.pallas_call(
        body, in_specs=[hbm, vm], out_specs=[hbm] * 6 + [vm],
        out_shape=(SDS((nrows, D), bf16), SDS((nrows, D), bf16), SDS((nrows, D), f32), SDS((4, nrows, D), bf16),
                   SDS((hrows, D), bf16), SDS((hrows, D), bf16), SDS((SMALL_ROWS, 128), f32)),
        scratch_shapes=[pltpu.VMEM((nrows, D), bf16), pltpu.VMEM((nrows, D), bf16),
                        pltpu.VMEM((nrows, D), bf16), pltpu.VMEM((nrows, D), bf16), pltpu.VMEM((nrows, D), bf16),
                        pltpu.VMEM((nrows, D), f32),
                        pltpu.SemaphoreType.DMA((4,)), pltpu.SemaphoreType.DMA((4,)),
                        pltpu.SemaphoreType.DMA((2,)), pltpu.SemaphoreType.DMA((2,)), pltpu.SemaphoreType.DMA((2,)),
                        pltpu.VMEM((2, SMALL_ROWS, 128), f32), pltpu.VMEM((4, SMALL_ROWS, 128), f32),
                        pltpu.SemaphoreType.DMA((4,)), pltpu.SemaphoreType.DMA((4,))],
        compiler_params=pltpu.CompilerParams(has_side_effects=True, vmem_limit_bytes=VMEM_LIMIT_V7X),
        name="reduce_scatter_ffn1_head")(dw1, small_packed)
    return outs[0], outs[1], outs[2], outs[-1]


def _rs1_tail_start(for_x, for_y, from_x, from_y, thru_a, thru_b):
    def body(fx_ref, fy_ref, lx_ref, ly_ref, ta_ref, tb_ref, ssem, rsem, fx_o, fy_o, lx_o, ly_o, ta_o, tb_o):
        x, y, c = _position()
        pltpu.make_async_remote_copy(src_ref=fx_ref, dst_ref=lx_ref, send_sem=ssem.at[0], recv_sem=rsem.at[0],
                                     device_id=(1 - x, y, c), device_id_type=MESH).start()
        pltpu.make_async_remote_copy(src_ref=fy_ref, dst_ref=ly_ref, send_sem=ssem.at[1], recv_sem=rsem.at[1],
                                     device_id=(x, 1 - y, c), device_id_type=MESH).start()

    dma = pltpu.SemaphoreType.DMA
    arrs = (for_x, for_y, from_x, from_y, thru_a, thru_b)
    return pl.pallas_call(
        body, name="rs1_tail_start", out_shape=(dma((2,)), dma((2,))) + tuple(_hbm_like(a) for a in arrs),
        in_specs=(HBM_SPEC,) * 6, out_specs=(SEM_SPEC,) * 2 + (HBM_SPEC,) * 6,
        input_output_aliases={0: 2, 1: 3, 2: 4, 3: 5, 4: 6, 5: 7},
        compiler_params=pltpu.CompilerParams(has_side_effects=SPLIT_EFFECT),
    )(*[_in_hbm(a) for a in arrs])


def _rs1_tail_wait(ssem, rsem, for_x, for_y, from_x, from_y, after):
    def body(fx_ref, fy_ref, lx_ref, ly_ref, ssem_ref, rsem_ref, after_ref, lx_o, ly_o):
        x, y, c = _position()
        for j, (src, dst) in enumerate(((fx_ref, lx_ref), (fy_ref, ly_ref))):
            d = pltpu.make_async_remote_copy(src_ref=src, dst_ref=dst, send_sem=ssem_ref.at[j], recv_sem=rsem_ref.at[j],
                                             device_id=(x, y, c), device_id_type=MESH)
            d.wait_recv()
            d.wait_send()

    return pl.pallas_call(
        body, name="rs1_tail_wait", out_shape=(_hbm_like(from_x), _hbm_like(from_y)),
        in_specs=(HBM_SPEC,) * 4 + (SEM_SPEC, SEM_SPEC, ANY_SPEC), out_specs=(HBM_SPEC, HBM_SPEC),
        input_output_aliases={2: 0, 3: 1},
        compiler_params=pltpu.CompilerParams(has_side_effects=SPLIT_EFFECT),
    )(for_x, for_y, from_x, from_y, ssem, rsem, after)


RSA_PIECES = MIX_PIECES + F2_PIECES
RSA_ROWS = _group_rows(RSA_PIECES)
RSA_OFF = {k: PIECE_OFF[k] - PIECE_OFF[RSA_PIECES[0]] for k in RSA_PIECES}
RSA_BLOCK = 192


def _rsa_rows(ref, k):
    return ref.at[pl.ds(RSA_OFF[k], PIECE_ROWS[k]), :]


def _rsa_level1_start(dwint, dwout, dw2, rx1, thru):
    def body(di_ref, do_ref, d2_ref, rx1_ref, thru_ref, sa, ra, di_o, do_o, d2_o, rx1_o, thru_o):
        x, y, c = _position()
        srcs = _weight_pieces(wi_ref=di_ref, wo_ref=do_ref, w2_ref=d2_ref)
        for j, chip in enumerate([(x, y), (1 - x, y), (x, 1 - y), (1 - x, 1 - y)]):
            for k in RSA_PIECES:
                pltpu.make_async_remote_copy(
                    src_ref=_block_rows(srcs, k, (*chip, 1 - c)), dst_ref=_rsa_rows(rx1_ref.at[j], k),
                    send_sem=sa.at[j], recv_sem=ra.at[j], device_id=(x, y, 1 - c), device_id_type=MESH).start()

    dma = pltpu.SemaphoreType.DMA
    arrs = (dwint, dwout, dw2, rx1, thru)
    return pl.pallas_call(
        body, name="rsa_level1_start", out_shape=(dma((4,)), dma((4,))) + tuple(_hbm_like(a) for a in arrs),
        in_specs=(HBM_SPEC,) * 5, out_specs=(SEM_SPEC,) * 2 + (HBM_SPEC,) * 5,
        input_output_aliases={0: 2, 1: 3, 2: 4, 3: 5, 4: 6},
        compiler_params=pltpu.CompilerParams(has_side_effects=SPLIT_EFFECT),
    )(*[_in_hbm(a) for a in arrs])


def _rsa_level1_wait(sa, ra, dwint, dwout, dw2, rx1, after):
    def body(di_ref, do_ref, d2_ref, rx1_ref, sa_ref, ra_ref, after_ref, di_o, do_o, d2_o, rx1_o):
        x, y, c = _position()
        for j in range(4):
            d = pltpu.make_async_remote_copy(src_ref=rx1_ref.at[j], dst_ref=rx1_ref.at[j], send_sem=sa_ref.at[j],
                                             recv_sem=ra_ref.at[j], device_id=(x, y, c), device_id_type=MESH)
            d.wait_recv()
            d.wait_send()

    arrs = (dwint, dwout, dw2, rx1)
    return pl.pallas_call(
        body, name="rsa_level1_wait", out_shape=tuple(_hbm_like(a) for a in arrs),
        in_specs=(HBM_SPEC,) * 4 + (SEM_SPEC, SEM_SPEC, ANY_SPEC), out_specs=(HBM_SPEC,) * 4,
        input_output_aliases={0: 0, 1: 1, 2: 2, 3: 3},
        compiler_params=pltpu.CompilerParams(has_side_effects=SPLIT_EFFECT),
    )(*arrs, sa, ra, after)


def _rsa_chip_sums(dwint, dwout, dw2, rx1):
    nblk = RSA_ROWS // RSA_BLOCK

    def body(di_ref, do_ref, d2_ref, rx1_ref, tx_ref, acc_ref, own_buf, rx_buf, tx_buf, acc_buf, in_sems, out_sems):
        x, y, c = _position()
        srcs = _weight_pieces(wi_ref=di_ref, wo_ref=do_ref, w2_ref=d2_ref)
        chips = [(x, y), (1 - x, y), (x, 1 - y), (1 - x, 1 - y)]

        def start_loads(j):
            s = j % 2
            for k in RSA_PIECES:
                pltpu.make_async_copy(_block_rows(srcs, k, (*chips[j], c)), _rsa_rows(own_buf.at[s], k),
                                      in_sems.at[2 * s]).start()
            pltpu.make_async_copy(rx1_ref.at[j], rx_buf.at[s], in_sems.at[2 * s + 1]).start()

        def wait_loads(j):
            s = j % 2
            pltpu.make_async_copy(rx1_ref.at[j], own_buf.at[s], in_sems.at[2 * s]).wait()
            pltpu.make_async_copy(rx1_ref.at[j], rx_buf.at[s], in_sems.at[2 * s + 1]).wait()

        def store(j):
            if j == 0:
                return pltpu.make_async_copy(acc_buf, acc_ref, out_sems.at[2])
            return pltpu.make_async_copy(tx_buf.at[j % 2], tx_ref.at[j - 1], out_sems.at[j % 2])

        start_loads(0)
        for j in range(4):
            s = j % 2
            if j + 1 < 4:
                start_loads(j + 1)
            wait_loads(j)
            if j == 3:
                store(1).wait()

            def add(i, carry, j=j, s=s):
                rows = pl.ds(pl.multiple_of(i * RSA_BLOCK, 16), RSA_BLOCK)
                tot = own_buf[s, rows, :].astype(f32) + rx_buf[s, rows, :].astype(f32)
                if j == 0:
                    acc_buf[rows, :] = tot
                else:
                    tx_buf[s, rows, :] = tot.astype(bf16)
                return carry

            lax.fori_loop(0, nblk, add, 0)
            store(j).start()
        store(0).wait()
        store(2).wait()
        store(3).wait()

    return pl.pallas_call(
        body, in_specs=[ANY_SPEC] * 4, out_specs=[ANY_SPEC] * 2,
        out_shape=(SDS((3, RSA_ROWS, D), bf16), SDS((RSA_ROWS, D), f32)),
        scratch_shapes=[pltpu.VMEM((2, RSA_ROWS, D), bf16), pltpu.VMEM((2, RSA_ROWS, D), bf16),
                        pltpu.VMEM((2, RSA_ROWS, D), bf16), pltpu.VMEM((RSA_ROWS, D), f32),
                        pltpu.SemaphoreType.DMA((4,)), pltpu.SemaphoreType.DMA((3,))],
        compiler_params=_cparams(None, VMEM_LIMIT_V7X), name="rsa_chip_sums")(dwint, dwout, dw2, rx1)


def _rsa_level2_start(tx, rx2, thru):
    def body(tx_ref, rx2_ref, thru_ref, sb, rb, tx_o, rx2_o, thru_o):
        x, y, c = _position()
        for j, chip in enumerate([(1 - x, y), (x, 1 - y), (1 - x, 1 - y)]):
            pltpu.make_async_remote_copy(src_ref=tx_ref.at[j], dst_ref=rx2_ref.at[j], send_sem=sb.at[j],
                                         recv_sem=rb.at[j], device_id=(*chip, c), device_id_type=MESH).start()

    dma = pltpu.SemaphoreType.DMA
    arrs = (tx, rx2, thru)
    return pl.pallas_call(
        body, name="rsa_level2_start", out_shape=(dma((3,)), dma((3,))) + tuple(_hbm_like(a) for a in arrs),
        in_specs=(HBM_SPEC,) * 3, out_specs=(SEM_SPEC,) * 2 + (HBM_SPEC,) * 3,
        input_output_aliases={0: 2, 1: 3, 2: 4},
        compiler_params=pltpu.CompilerParams(has_side_effects=SPLIT_EFFECT),
    )(*[_in_hbm(a) for a in arrs])


def _rsa_level2_wait(sb, rb, tx, rx2, after):
    def body(tx_ref, rx2_ref, sb_ref, rb_ref, after_ref, rx2_o):
        x, y, c = _position()
        for j in range(3):
            d = pltpu.make_async_remote_copy(src_ref=tx_ref.at[j], dst_ref=rx2_ref.at[j], send_sem=sb_ref.at[j],
                                             recv_sem=rb_ref.at[j], device_id=(x, y, c), device_id_type=MESH)
            d.wait_recv()
            d.wait_send()

    return pl.pallas_call(
        body, name="rsa_level2_wait", out_shape=_hbm_like(rx2),
        in_specs=(HBM_SPEC, HBM_SPEC, SEM_SPEC, SEM_SPEC, ANY_SPEC), out_specs=HBM_SPEC,
        input_output_aliases={1: 0},
        compiler_params=pltpu.CompilerParams(has_side_effects=SPLIT_EFFECT),
    )(tx, rx2, sb, rb, after)


def _adamw_math(w, g, m, v):
    m = ADAM_B1 * m + (1.0 - ADAM_B1) * g
    v = ADAM_B2 * v + (1.0 - ADAM_B2) * (g * g)
    m_hat = m / (1.0 - ADAM_B1 ** ADAM_STEP)
    v_hat = v / (1.0 - ADAM_B2 ** ADAM_STEP)
    delta = -ADAM_LR * (m_hat / (jnp.sqrt(v_hat) + ADAM_EPS) + ADAM_WD * w)
    return delta, m, v


def _adamw_big(pieces, ws, ms, vs, own, landed, name):
    npiece = len(pieces)
    nland = sum(a.shape[0] if a.ndim == 3 else 1 for a in landed)
    rmax = max(PIECE_ROWS[k] for k in pieces)
    half = FS // 2

    def segments(k):
        if k in G1_PIECES:
            return [(hf * len(G1_PIECES) * half + k * half, hf * half, half) for hf in (0, 1)]
        return [(RSA_OFF[k], 0, PIECE_ROWS[k])]

    def body(*refs):
        ins = (refs[0:npiece], refs[npiece:2 * npiece], refs[2 * npiece:3 * npiece])
        own_ref = refs[3 * npiece]
        nin = 3 * npiece + 1 + len(landed)
        land_refs = []
        for ref, a in zip(refs[3 * npiece + 1:nin], landed):
            land_refs += [ref.at[j] for j in range(a.shape[0])] if a.ndim == 3 else [ref]
        out_refs = refs[nin:nin + 4 * npiece]
        inb, landb, outb, in_sems, land_sems, out_sems = refs[nin + 4 * npiece:]

        def loads(i):
            s, k = i % 2, pieces[i]
            r = PIECE_ROWS[k]
            cps = [pltpu.make_async_copy(ins[q][i].at[0], inb.at[s, q, pl.ds(0, r), :], in_sems.at[4 * s + q])
                   for q in range(3)]
            waits = list(cps)
            for src0, dst0, n in segments(k):
                cps.append(pltpu.make_async_copy(own_ref.at[pl.ds(src0, n), :], inb.at[s, 3, pl.ds(dst0, n), :],
                                                 in_sems.at[4 * s + 3]))
                for p in range(nland):
                    cps.append(pltpu.make_async_copy(land_refs[p].at[pl.ds(src0, n), :],
                                                     landb.at[s, p, pl.ds(dst0, n), :], land_sems.at[nland * s + p]))
            own_rows = inb.at[s, 3, pl.ds(0, r), :]
            waits.append(pltpu.make_async_copy(own_rows, own_rows, in_sems.at[4 * s + 3]))
            for p in range(nland):
                rows = landb.at[s, p, pl.ds(0, r), :]
                waits.append(pltpu.make_async_copy(rows, rows, land_sems.at[nland * s + p]))
            return cps, waits

        def stores(i):
            s, r = i % 2, PIECE_ROWS[pieces[i]]
            return [pltpu.make_async_copy(outb.at[s, q, pl.ds(0, r), :], out_refs[q * npiece + i].at[0],
                                          out_sems.at[4 * s + q]) for q in range(4)]

        for cp in loads(0)[0]:
            cp.start()
        for i in range(npiece):
            s, r = i % 2, PIECE_ROWS[pieces[i]]
            if i + 1 < npiece:
                for cp in loads(i + 1)[0]:
                    cp.start()
            for cp in loads(i)[1]:
                cp.wait()
            if i >= 2:
                for cp in stores(i - 2):
                    cp.wait()
            g = inb[s, 3, 0:r, :]
            for p in range(nland):
                g = g + landb[s, p, 0:r, :].astype(f32)
            d, nm, nv = _adamw_math(inb[s, 0, 0:r, :], g, inb[s, 1, 0:r, :], inb[s, 2, 0:r, :])
            outb[s, 0, 0:r, :] = g
            outb[s, 1, 0:r, :] = d
            outb[s, 2, 0:r, :] = nm
            outb[s, 3, 0:r, :] = nv
            for cp in stores(i):
                cp.start()
        for i in range(max(npiece - 2, 0), npiece):
            for cp in stores(i):
                cp.wait()

    hbm = pl.BlockSpec(memory_space=pl.ANY)
    outs = pl.pallas_call(
        body, in_specs=[hbm] * (3 * npiece + 1 + len(landed)), out_specs=[hbm] * (4 * npiece),
        out_shape=tuple(SDS(w.shape, f32) for _ in range(4) for w in ws),
        scratch_shapes=[pltpu.VMEM((2, 4, rmax, D), f32), pltpu.VMEM((2, nland, rmax, D), bf16),
                        pltpu.VMEM((2, 4, rmax, D), f32),
                        pltpu.SemaphoreType.DMA((8,)), pltpu.SemaphoreType.DMA((2 * nland,)),
                        pltpu.SemaphoreType.DMA((8,))],
        compiler_params=_cparams(None, VMEM_LIMIT_V7X), name=name)(*ws, *ms, *vs, own, *landed)
    return [list(outs[q * npiece:(q + 1) * npiece]) for q in range(4)]


def _adamw_small(ws, ms, vs, gs, name):
    n = len(ws)

    def body(*refs):
        w_refs, m_refs, v_refs, g_refs = refs[0:n], refs[n:2 * n], refs[2 * n:3 * n], refs[3 * n:4 * n]
        outs = refs[4 * n:]
        for i in range(n):
            d, nm, nv = _adamw_math(w_refs[i][...], g_refs[i][...], m_refs[i][...], v_refs[i][...])
            outs[i][...] = d
            outs[n + i][...] = nm
            outs[2 * n + i][...] = nv

    outs = pl.pallas_call(
        body, out_shape=tuple(SDS(w.shape, f32) for _ in range(3) for w in ws), name=name)(*ws, *ms, *vs, *gs)
    return [list(outs[q * n:(q + 1) * n]) for q in range(3)]


WEIGHTS = ("ffn1_norm", "ffn1_w_gate", "ffn1_w_up", "ffn1_w_down", "mix_norm", "w_in", "q_norm", "k_norm",
           "attn_sinks", "rel_bias", "pool_w", "pool_scale", "w_out", "ffn2_norm", "ffn2_w_gate", "ffn2_w_up",
           "ffn2_w_down")
BIG = (("ffn1_w_gate", True), ("ffn1_w_up", True), ("ffn1_w_down", False), ("w_in", True), ("w_out", False),
       ("ffn2_w_gate", True), ("ffn2_w_up", True), ("ffn2_w_down", False))


def kernel(x, ffn1_norm, ffn1_w_gate, ffn1_w_up, ffn1_w_down, mix_norm, w_in, q_norm, k_norm, attn_sinks, rel_bias, pool_w, pool_scale, w_out, ffn2_norm, ffn2_w_gate, ffn2_w_up, ffn2_w_down, loss_target, m_ffn1_norm, m_ffn1_w_gate, m_ffn1_w_up, m_ffn1_w_down, m_mix_norm, m_w_in, m_q_norm, m_k_norm, m_attn_sinks, m_rel_bias, m_pool_w, m_pool_scale, m_w_out, m_ffn2_norm, m_ffn2_w_gate, m_ffn2_w_up, m_ffn2_w_down, v_ffn1_norm, v_ffn1_w_gate, v_ffn1_w_up, v_ffn1_w_down, v_mix_norm, v_w_in, v_q_norm, v_k_norm, v_attn_sinks, v_rel_bias, v_pool_w, v_pool_scale, v_w_out, v_ffn2_norm, v_ffn2_w_gate, v_ffn2_w_up, v_ffn2_w_down):
    args = dict(locals())
    w = {n: args[n] for n in WEIGHTS}
    m = {n: args["m_" + n] for n in WEIGHTS}
    v = {n: args["v_" + n] for n in WEIGHTS}

    as_rows = lambda a, tr: jnp.swapaxes(a, 1, 2) if tr else a
    shard = jnp.concatenate([as_rows(w[n], tr)[0].astype(bf16) for n, tr in BIG], axis=0)
    exchanges = _GatheredWeights(shard, x[0], ffn1_norm)
    (dh1, dx1), (dw1, _, _, _), small = _local_step(
        x[0], loss_target[0], exchanges, ffn1_norm, mix_norm, ffn2_norm, q_norm, k_norm, attn_sinks,
        rel_bias, pool_w[0], pool_scale)

    nrows1 = len(G1_PIECES) * FS
    for_x, for_y, own1, small_tot = _reduce_scatter_ffn1_head(dw1, _pack_small(small))
    ssem, rsem, for_x, for_y, from_x, from_y, g1, small_tot = _rs1_tail_start(
        for_x, for_y, lax.empty((nrows1, D), bf16), lax.empty((nrows1, D), bf16), ffn1_norm, small_tot)
    gx, _, _ = _norm_bwd(dh1, x[0], g1, dx1, 1.0, "norm1_bwd")
    own_rest, landed_rest = exchanges.mix_ffn2_grads_parts(gx)

    grads, deltas, new_m, new_v = {}, {}, {}, {}
    rest = [k for k in range(len(BIG)) if k not in G1_PIECES]
    rows_of = lambda t, ks: [as_rows(t[BIG[k][0]], BIG[k][1]) for k in ks]
    rest_out = _adamw_big(rest, rows_of(w, rest), rows_of(m, rest), rows_of(v, rest), own_rest, [landed_rest],
                          "adamw_rest")
    from_x, from_y = _rs1_tail_wait(ssem, rsem, for_x, for_y, from_x, from_y, rest_out[0][0])
    ffn1 = list(G1_PIECES)
    ffn1_out = _adamw_big(ffn1, rows_of(w, ffn1), rows_of(m, ffn1), rows_of(v, ffn1), own1, [from_x, from_y],
                          "adamw_ffn1")
    for ks, out in ((rest, rest_out), (ffn1, ffn1_out)):
        for i, k in enumerate(ks):
            n, tr = BIG[k]
            grads[n], deltas[n], new_m[n], new_v[n] = [as_rows(o[i], tr) for o in out]
    small_names = [n for n in SMALL_NAMES if n != "loss"]
    for n in small_names:
        grads[n] = _unpack_small(small_tot, n)
    ds, nms, nvs = _adamw_small([w[n] for n in small_names], [m[n] for n in small_names], [v[n] for n in small_names],
                                [grads[n] for n in small_names], "adamw_small")
    for i, n in enumerate(small_names):
        deltas[n], new_m[n], new_v[n] = ds[i], nms[i], nvs[i]
    loss = small_tot[LOSS_ROW, 0]
    return (loss, gx[None], *[grads[n] for n in WEIGHTS], *[deltas[n] for n in WEIGHTS],
            *[new_m[n] for n in WEIGHTS], *[new_v[n] for n in WEIGHTS])
```

```python
import jax
import jax.numpy as jnp
import numpy as np
from jax import lax
from jax.experimental import pallas as pl
from jax.experimental.pallas import tpu as pltpu

f32, bf16, i32 = jnp.float32, jnp.bfloat16, jnp.int32
SDS = jax.ShapeDtypeStruct

D = 1024
F = 2816
HD = 64
NH = 8
NKV = 2
GQA = NH // NKV
DATTN = NH * HD
DKV = NKV * HD
DPOOL = 512
POOL_WINDOWS = (2, 4, 8, 16)
PGD = DPOOL // len(POOL_WINDOWS)
DIN = DATTN + 2 * DKV + DPOOL
DMIX = DATTN + DPOOL
BLK = 128
NBUCK = 32
MAX_DISTANCE = 128
EPS = 1e-6
NEG = -1e30
SCALE = HD ** -0.5

ADAM_LR, ADAM_B1, ADAM_B2, ADAM_EPS, ADAM_WD, ADAM_STEP = 0.001, 0.9, 0.999, 1e-08, 0.01, 10

NDEV = 8
FS = F // NDEV
INS = DIN // NDEV
OUTS = DMIX // NDEV
PIECE_ROWS = (FS, FS, FS, INS, OUTS, FS, FS, FS)
PIECE_OFF = tuple(int(v) for v in np.cumsum((0,) + PIECE_ROWS[:-1]))
PACK_ROWS = sum(PIECE_ROWS)

VMEM_LIMIT_V7X = 56 * 1024 * 1024

MESH = pl.DeviceIdType.MESH


def _cparams(sem=None, vmem=None):
    return pltpu.CompilerParams(dimension_semantics=sem, vmem_limit_bytes=vmem)


def _nt(a, b):
    return lax.dot_general(a, b, (((1,), (1,)), ((), ())), preferred_element_type=f32)


def _tn(a, b):
    return lax.dot_general(a, b, (((0,), (0,)), ((), ())), preferred_element_type=f32)


def _nn(a, b):
    return jnp.dot(a, b, preferred_element_type=f32)


def _sigmoid(x):
    return 1.0 / (1.0 + jnp.exp(-x))


def _norm_fwd(x, g, name):
    T = x.shape[0]
    tm = min(512, T)

    def body(x_ref, g_ref, h_ref):
        xv = x_ref[...]
        r = lax.rsqrt(jnp.mean(xv * xv, axis=-1, keepdims=True) + EPS)
        h_ref[...] = (xv * r * g_ref[...]).astype(bf16)

    return pl.pallas_call(
        body, grid=(T // tm,),
        in_specs=[pl.BlockSpec((tm, D), lambda i: (i, 0)), pl.BlockSpec((1, D), lambda i: (0, 0))],
        out_specs=pl.BlockSpec((tm, D), lambda i: (i, 0)),
        out_shape=SDS((T, D), bf16), name=name)(x, g)


def _norm_bwd(dh, x, g, dres, out_scale, name):
    T = x.shape[0]
    tm = min(512, T)

    def body(dh_ref, x_ref, g_ref, dr_ref, dx_ref, dxb_ref, dg_ref):
        i = pl.program_id(0)
        xv = x_ref[...]
        r = lax.rsqrt(jnp.mean(xv * xv, axis=-1, keepdims=True) + EPS)
        xh = xv * r
        dhv = dh_ref[...]
        dxh = dhv * g_ref[...]
        dx = dr_ref[...] + r * (dxh - xh * jnp.mean(dxh * xh, axis=-1, keepdims=True))
        dx_ref[...] = dx
        dxb_ref[...] = (out_scale * dx).astype(bf16)
        dg = jnp.sum(dhv * xh, axis=0, keepdims=True)

        @pl.when(i == 0)
        def _():
            dg_ref[...] = dg

        @pl.when(i > 0)
        def _():
            dg_ref[...] += dg

    tok = pl.BlockSpec((tm, D), lambda i: (i, 0))
    vec = pl.BlockSpec((1, D), lambda i: (0, 0))
    return pl.pallas_call(
        body, grid=(T // tm,),
        in_specs=[tok, tok, vec, tok], out_specs=[tok, tok, vec],
        out_shape=(SDS((T, D), f32), SDS((T, D), bf16), SDS((1, D), f32)),
        compiler_params=_cparams(("arbitrary",)), name=name)(dh, x, g, dres)


FFN_ROW_CHUNK = 256


def _ffn_tiles(T):
    return min(1024, T), 256


def _ffn_fwd(h, w, x, target, name):
    T = h.shape[0]
    tm, tf = _ffn_tiles(T)
    nf = F // tf
    with_loss = target is not None

    def body(*refs):
        if with_loss:
            h_ref, w_ref, x_hbm, t_hbm, xo_ref, g_ref, u_ref, dyb_ref, loss_ref, tbuf, sem = refs
        else:
            h_ref, w_ref, x_hbm, xo_ref, g_ref, u_ref, sem = refs
        fi = pl.program_id(0)

        @pl.when(fi == 0)
        def _():
            cp = pltpu.make_async_copy(x_hbm, xo_ref, sem)
            cp.start()
            cp.wait()

        wgu = w_ref[0:2].reshape(2 * tf, D)
        for r in range(0, T, tm):
            rows = slice(r, r + tm)
            gu = _nt(h_ref[rows, :], wgu)
            gate, up = gu[:, :tf], gu[:, tf:]
            act = gate * _sigmoid(gate) * up
            g_ref[0, rows, :] = gate.astype(bf16)
            u_ref[0, rows, :] = up.astype(bf16)
            xo_ref[rows, :] += _nn((0.5 * act).astype(bf16), w_ref[2])

        if with_loss:
            @pl.when(fi == nf - 1)
            def _():
                lanes = jnp.zeros((1, 128), f32)
                for r in range(0, T, tm):
                    rows = slice(r, r + tm)
                    cp = pltpu.make_async_copy(t_hbm.at[pl.ds(r, tm), :], tbuf, sem)
                    cp.start()
                    cp.wait()
                    e = xo_ref[rows, :] - tbuf[...]
                    dy = e * (1.0 / D)
                    xo_ref[rows, :] = dy
                    dyb_ref[rows, :] = (0.5 * dy).astype(bf16)
                    col = jnp.sum(e * e, axis=0, keepdims=True) * (0.5 / D)
                    for k in range(D // 128):
                        lanes = lanes + col[:, 128 * k:128 * (k + 1)]
                loss_ref[...] = lanes

    tok = pl.BlockSpec((T, D), lambda f: (0, 0))
    act_spec = pl.BlockSpec((1, T, tf), lambda f: (f, 0, 0))
    hbm = pl.BlockSpec(memory_space=pl.ANY)
    in_specs = [tok, pl.BlockSpec((3, tf, D), lambda f: (0, f, 0)), hbm]
    out_specs = [tok, act_spec, act_spec]
    out_shape = [SDS((T, D), f32), SDS((nf, T, tf), bf16), SDS((nf, T, tf), bf16)]
    scratch = [pltpu.SemaphoreType.DMA]
    args = [h, w, x]
    if with_loss:
        in_specs.append(hbm)
        args.append(target)
        out_specs += [tok, pl.BlockSpec((1, 128), lambda f: (0, 0))]
        out_shape += [SDS((T, D), bf16), SDS((1, 128), f32)]
        scratch = [pltpu.VMEM((tm, D), f32)] + scratch
    return pl.pallas_call(
        body, grid=(nf,), in_specs=in_specs, out_specs=out_specs, out_shape=tuple(out_shape), scratch_shapes=scratch,
        compiler_params=_cparams(("arbitrary",), VMEM_LIMIT_V7X), name=name)(*args)


def _ffn_bwd(dob, h, gate, up, w, name, input_grad=True):
    T = h.shape[0]
    _, tf = _ffn_tiles(T)
    nf = F // tf

    def body(*refs):
        if input_grad:
            do_hbm, h_hbm, g_ref, u_ref, w_ref, dh_hbm, dw_ref, do_v, h_v, act_s, sems, dh_acc, dgu_s = refs
        else:
            do_hbm, h_hbm, g_ref, u_ref, w_ref, dgu_ref, dw_ref, do_v, h_v, act_s, sems = refs
            dgu_s = dgu_ref.at[0]
        fi = pl.program_id(0)

        @pl.when(fi == 0)
        def _():
            loads = [pltpu.make_async_copy(do_hbm, do_v, sems.at[0]), pltpu.make_async_copy(h_hbm, h_v, sems.at[1])]
            for cp in loads:
                cp.start()
            if input_grad:
                dh_acc[...] = jnp.zeros_like(dh_acc)
            for cp in loads:
                cp.wait()

        wgu = w_ref[0:2].reshape(2 * tf, D)
        for r in range(0, T, FFN_ROW_CHUNK):
            rows = slice(r, r + FFN_ROW_CHUNK)
            dov = do_v[rows, :]
            gv = g_ref[0, rows, :].astype(f32)
            uv = u_ref[0, rows, :].astype(f32)
            sg = _sigmoid(gv)
            sil = gv * sg
            dact = _nt(dov, w_ref[2])
            dup = dact * sil
            dgate = dact * uv * (sg * (1.0 + gv * (1.0 - sg)))
            dgu = jnp.concatenate([dgate.astype(bf16), dup.astype(bf16)], axis=1)
            dgu_s[rows, :] = dgu
            act_s[rows, :] = (sil * uv).astype(bf16)
            if input_grad:
                dh_acc[rows, :] += _nn(dgu, wgu)
        dw_ref[0:2] = _tn(dgu_s[...], h_v[...]).reshape(2, tf, D).astype(bf16)
        dw_ref[2] = _tn(act_s[...], do_v[...]).astype(bf16)

        if input_grad:
            @pl.when(fi == nf - 1)
            def _():
                out = pltpu.make_async_copy(dh_acc, dh_hbm, sems.at[0])
                out.start()
                out.wait()

    act_spec = pl.BlockSpec((1, T, tf), lambda f: (f, 0, 0))
    wspec = pl.BlockSpec((3, tf, D), lambda f: (0, f, 0))
    hbm = pl.BlockSpec(memory_space=pl.ANY)
    scratch = [pltpu.VMEM((T, D), bf16), pltpu.VMEM((T, D), bf16), pltpu.VMEM((T, tf), bf16),
               pltpu.SemaphoreType.DMA((2,))]
    if input_grad:
        first_spec, first_shape = hbm, SDS((T, D), f32)
        scratch += [pltpu.VMEM((T, D), f32), pltpu.VMEM((T, 2 * tf), bf16)]
    else:
        first_spec, first_shape = pl.BlockSpec((1, T, 2 * tf), lambda f: (f, 0, 0)), SDS((nf, T, 2 * tf), bf16)
    return pl.pallas_call(
        body, grid=(nf,),
        in_specs=[hbm, hbm, act_spec, act_spec, wspec],
        out_specs=[first_spec, wspec],
        out_shape=(first_shape, SDS((3, F, D), bf16)),
        scratch_shapes=scratch,
        compiler_params=_cparams(("arbitrary",), VMEM_LIMIT_V7X), name=name)(dob, h, gate, up, w)


def _ffn_input_grad(dgu, w, after, name):
    nf, T, tf2 = dgu.shape
    tf = tf2 // 2

    def body(dgu_ref, w_ref, after_ref, dh_ref):
        @pl.when(pl.program_id(0) == 0)
        def _():
            dh_ref[...] = jnp.zeros_like(dh_ref)

        wgu = w_ref[0:2].reshape(2 * tf, D)
        for r in range(0, T, FFN_ROW_CHUNK):
            rows = slice(r, r + FFN_ROW_CHUNK)
            dh_ref[rows, :] += _nn(dgu_ref[0, rows, :], wgu)

    return pl.pallas_call(
        body, grid=(nf,),
        in_specs=[pl.BlockSpec((1, T, tf2), lambda f: (f, 0, 0)), pl.BlockSpec((3, tf, D), lambda f: (0, f, 0)),
                  pl.BlockSpec(memory_space=pl.ANY)],
        out_specs=pl.BlockSpec((T, D), lambda f: (0, 0)), out_shape=SDS((T, D), f32),
        compiler_params=_cparams(("arbitrary",), VMEM_LIMIT_V7X), name=name)(dgu, w, after)


def _in_proj_fwd(h, wint, name):
    T = h.shape[0]
    tm = min(512, T)

    def body(h_ref, w_ref, z_ref):
        z_ref[...] = _nt(h_ref[...], w_ref[...])

    return pl.pallas_call(
        body, grid=(T // tm,),
        in_specs=[pl.BlockSpec((tm, D), lambda i: (i, 0)), pl.BlockSpec((DIN, D), lambda i: (0, 0))],
        out_specs=pl.BlockSpec((tm, DIN), lambda i: (i, 0)),
        out_shape=SDS((T, DIN), f32), name=name)(h, wint)


def _in_proj_bwd(dz, wint, h, name):
    T = h.shape[0]
    tm = min(512, T)
    nt = T // tm

    def body(dz_ref, w_ref, h_ref, dh_ref, dw_ref, acc):
        i = pl.program_id(0)
        dzb = dz_ref[...].astype(bf16)
        dh_ref[...] = _nn(dzb, w_ref[...])
        part = _tn(dzb, h_ref[...])

        @pl.when(i == 0)
        def _():
            acc[...] = part

        @pl.when(i > 0)
        def _():
            acc[...] += part

        @pl.when(i == nt - 1)
        def _():
            dw_ref[...] = acc[...].astype(bf16)

    wspec = pl.BlockSpec((DIN, D), lambda i: (0, 0))
    return pl.pallas_call(
        body, grid=(nt,),
        in_specs=[pl.BlockSpec((tm, DIN), lambda i: (i, 0)), wspec, pl.BlockSpec((tm, D), lambda i: (i, 0))],
        out_specs=[pl.BlockSpec((tm, D), lambda i: (i, 0)), wspec],
        out_shape=(SDS((T, D), f32), SDS((DIN, D), bf16)),
        scratch_shapes=[pltpu.VMEM((DIN, D), f32)],
        compiler_params=_cparams(("arbitrary",)), name=name)(dz, wint, h)


def _out_proj_fwd(ymix, wout, x, g, name):
    T = x.shape[0]
    tm = min(512, T)

    def body(y_ref, w_ref, x_ref, g_ref, o_ref, h_ref):
        o = x_ref[...] + _nn(y_ref[...], w_ref[...])
        o_ref[...] = o
        r = lax.rsqrt(jnp.mean(o * o, axis=-1, keepdims=True) + EPS)
        h_ref[...] = (o * r * g_ref[...]).astype(bf16)

    tok = pl.BlockSpec((tm, D), lambda i: (i, 0))
    return pl.pallas_call(
        body, grid=(T // tm,),
        in_specs=[pl.BlockSpec((tm, DMIX), lambda i: (i, 0)), pl.BlockSpec((DMIX, D), lambda i: (0, 0)), tok,
                  pl.BlockSpec((1, D), lambda i: (0, 0))],
        out_specs=[tok, tok], out_shape=(SDS((T, D), f32), SDS((T, D), bf16)), name=name)(ymix, wout, x, g)


def _out_proj_bwd(dxb, wout, ymix, name):
    T = dxb.shape[0]
    tm = min(512, T)
    nt = T // tm

    def body(dx_ref, w_ref, y_ref, dy_ref, dw_ref, acc):
        i = pl.program_id(0)
        dxv = dx_ref[...]
        dy_ref[...] = _nt(dxv, w_ref[...])
        part = _tn(y_ref[...], dxv)

        @pl.when(i == 0)
        def _():
            acc[...] = part

        @pl.when(i > 0)
        def _():
            acc[...] += part

        @pl.when(i == nt - 1)
        def _():
            dw_ref[...] = acc[...].astype(bf16)

    wspec = pl.BlockSpec((DMIX, D), lambda i: (0, 0))
    return pl.pallas_call(
        body, grid=(nt,),
        in_specs=[pl.BlockSpec((tm, D), lambda i: (i, 0)), wspec, pl.BlockSpec((tm, DMIX), lambda i: (i, 0))],
        out_specs=[pl.BlockSpec((tm, DMIX), lambda i: (i, 0)), wspec],
        out_shape=(SDS((T, DMIX), f32), SDS((DMIX, D), bf16)),
        scratch_shapes=[pltpu.VMEM((DMIX, D), f32)],
        compiler_params=_cparams(("arbitrary",)), name=name)(dxb, wout, ymix)


def _t5_bucket_table():
    ql = np.arange(BLK)[:, None]
    kl = np.arange(2 * BLK)[None, :]
    n = np.maximum(ql + BLK - kl, 0)
    max_exact = NBUCK // 2
    large = max_exact + (np.log(np.maximum(n, 1) / max_exact) / np.log(MAX_DISTANCE / max_exact)
                         * (NBUCK - max_exact)).astype(np.int32)
    large = np.minimum(large, NBUCK - 1)
    return np.where(n < max_exact, n, large).astype(np.int32)


def _fill_bias(bk_ref, rb_ref, bias_scr):
    bk = bk_ref[...]
    for h in range(NH):
        def step(b, acc, h=h):
            return acc + jnp.where(bk == b, rb_ref[b, h], 0.0)
        bias_scr[h] = lax.fori_loop(0, NBUCK, step, jnp.zeros((BLK, 2 * BLK), f32))


MIX_SUB = 4


class _Window:
    def __init__(self, zc_ref, zp_ref, n, s):
        self.blk = n * MIX_SUB + s
        self.cur = lambda a, b: zc_ref[s * BLK:(s + 1) * BLK, a:b]
        self.prev = (lambda a, b: zp_ref[:, a:b]) if s == 0 else (lambda a, b: zc_ref[(s - 1) * BLK:s * BLK, a:b])


def _attn_qkv(win, kh, qg, kg):
    kc = DATTN + HD * kh
    vc = DATTN + DKV + HD * kh
    kx = jnp.concatenate([win.prev(kc, kc + HD), win.cur(kc, kc + HD)], axis=0)
    vx = jnp.concatenate([win.prev(vc, vc + HD), win.cur(vc, vc + HD)], axis=0)
    qx = jnp.concatenate([win.cur(HD * (GQA * kh + g), HD * (GQA * kh + g + 1)) for g in range(GQA)], axis=0)
    rq = lax.rsqrt(jnp.mean(qx * qx, axis=-1, keepdims=True) + EPS)
    rk = lax.rsqrt(jnp.mean(kx * kx, axis=-1, keepdims=True) + EPS)
    qhat, khat = qx * rq, kx * rk
    return dict(qhat=qhat, khat=khat, rq=rq, rk=rk, qsb=(qhat * (qg * SCALE)).astype(bf16),
                knb=(khat * kg).astype(bf16), vb=vx.astype(bf16))


def _window_masks(n):
    row = lax.broadcasted_iota(i32, (GQA * BLK, 2 * BLK), 0) & (BLK - 1)
    col = lax.broadcasted_iota(i32, (GQA * BLK, 2 * BLK), 1)
    band = (col > row) & (col <= row + BLK)
    return band & ((col >= BLK) | (n > 0)), band


def _attn_probs(a, kh, sk_ref, bias_scr, mask):
    s = _nt(a["qsb"], a["knb"]) + bias_scr[GQA * kh:GQA * (kh + 1)].reshape(GQA * BLK, 2 * BLK)
    s = jnp.where(mask, s, NEG)
    ridx = lax.broadcasted_iota(i32, (GQA * BLK, 1), 0)
    sink = jnp.full((GQA * BLK, 1), sk_ref[GQA * kh + GQA - 1], f32)
    for g in range(GQA - 2, -1, -1):
        sink = jnp.where(ridx < (g + 1) * BLK, sk_ref[GQA * kh + g], sink)
    m = jnp.maximum(jnp.max(s, axis=-1, keepdims=True), sink)
    e = jnp.exp(s - m)
    den = jnp.sum(e, axis=-1, keepdims=True) + jnp.exp(sink - m)
    return e / den


POOL_STEPS = {2: (1,), 4: (1, 2), 8: (1, 2, 4), 16: (1, 2, 4, 8)}


def _pool_group(win, g, w):
    n = win.blk
    c0 = DATTN + 2 * DKV + PGD * g
    uc = win.cur(c0, c0 + PGD)
    up = jnp.where(n > 0, win.prev(c0, c0 + PGD), 0.0)
    sm = jnp.concatenate([up, uc], axis=0)
    for k in POOL_STEPS[w]:
        sm = sm + pltpu.roll(sm, k, axis=0)
    pos = n * BLK + lax.broadcasted_iota(i32, (BLK, 1), 0) + 1
    cnt = jnp.minimum(pos, w).astype(f32)
    return sm[BLK:2 * BLK] / cnt - uc, cnt


def _mix_fwd(z, qg, kg, sinks, relb, bucket, pool_w, pscale, name):
    T = z.shape[0]
    step_rows = MIX_SUB * BLK
    nsteps = T // step_rows

    def body(zc_ref, zp_ref, qg_ref, kg_ref, sk_ref, rb_ref, bk_ref, pw_ref, ps_ref, y_ref, p_ref, bias_scr, yacc):
        n = pl.program_id(0)

        @pl.when(n == 0)
        def _():
            _fill_bias(bk_ref, rb_ref, bias_scr)

        first_mask, mask = _window_masks(n)
        for s in range(MIX_SUB):
            win = _Window(zc_ref, zp_ref, n, s)
            rows = slice(s * BLK, (s + 1) * BLK)
            for kh in range(NKV):
                a = _attn_qkv(win, kh, qg_ref[...], kg_ref[...])
                pb = _attn_probs(a, kh, sk_ref, bias_scr, first_mask if s == 0 else mask).astype(bf16)
                p_ref[s, GQA * kh:GQA * (kh + 1)] = pb.reshape(GQA, BLK, 2 * BLK)
                o = _nn(pb, a["vb"])
                for g in range(GQA):
                    hc = HD * (GQA * kh + g)
                    yacc[rows, hc:hc + HD] = o[g * BLK:(g + 1) * BLK]
            for g, w in enumerate(POOL_WINDOWS):
                pooled, _ = _pool_group(win, g, w)
                yp = _nn(pooled.astype(bf16), pw_ref[g].astype(bf16)) * ps_ref[:, PGD * g:PGD * (g + 1)]
                yacc[rows, DATTN + PGD * g:DATTN + PGD * (g + 1)] = yp
        y_ref[...] = yacc[...].astype(bf16)

    full = lambda *shape: pl.BlockSpec(shape, lambda n: (0,) * len(shape))
    smem = pl.BlockSpec(memory_space=pltpu.SMEM)
    return pl.pallas_call(
        body, grid=(nsteps,),
        in_specs=[pl.BlockSpec((step_rows, DIN), lambda n: (n, 0)),
                  pl.BlockSpec((BLK, DIN), lambda n: (jnp.maximum(n * MIX_SUB - 1, 0), 0)),
                  full(1, HD), full(1, HD), smem, smem, full(BLK, 2 * BLK),
                  full(len(POOL_WINDOWS), PGD, PGD), full(1, DPOOL)],
        out_specs=[pl.BlockSpec((step_rows, DMIX), lambda n: (n, 0)),
                   pl.BlockSpec((MIX_SUB, NH, BLK, 2 * BLK), lambda n: (n, 0, 0, 0))],
        out_shape=(SDS((T, DMIX), bf16), SDS((T // BLK, NH, BLK, 2 * BLK), bf16)),
        scratch_shapes=[pltpu.VMEM((NH, BLK, 2 * BLK), f32), pltpu.VMEM((step_rows, DMIX), f32)],
        compiler_params=_cparams(("arbitrary",)), name=name)(z, z, qg, kg, sinks, relb, bucket, pool_w, pscale)


def _mix_bwd(z, dy, probs, qg, kg, relb, bucket, pool_w, pscale, name):
    T = z.shape[0]
    step_rows = MIX_SUB * BLK
    nsteps = T // step_rows

    def body(zc_ref, zp_ref, dy_ref, p_ref, qg_ref, kg_ref, bk_ref, pw_ref, ps_ref,
             dz_ref, dqg_ref, dkg_ref, dsk_ref, drb_ref, dpw_ref, dps_ref, dbias_scr):
        n = pl.program_id(0)

        @pl.when(n == 0)
        def _():
            dbias_scr[...] = jnp.zeros_like(dbias_scr)
            dqg_ref[...] = jnp.zeros_like(dqg_ref)
            dkg_ref[...] = jnp.zeros_like(dkg_ref)
            dpw_ref[...] = jnp.zeros_like(dpw_ref)
            dps_ref[...] = jnp.zeros_like(dps_ref)

        qg, kg = qg_ref[...], kg_ref[...]
        for s in range(MIX_SUB):
            win = _Window(zc_ref, zp_ref, n, s)
            blk = win.blk
            rows = pl.ds(pl.multiple_of(blk * BLK, BLK), BLK)
            prow = pl.ds(pl.multiple_of(jnp.maximum(blk - 1, 0) * BLK, BLK), BLK)
            dyr = slice(s * BLK, (s + 1) * BLK)

            def into_prev(fn, s=s):
                if s == 0:
                    pl.when(n > 0)(fn)
                else:
                    fn()

            for kh in range(NKV):
                a = _attn_qkv(win, kh, qg, kg)
                pb = p_ref[s, GQA * kh:GQA * (kh + 1)].reshape(GQA * BLK, 2 * BLK)
                p = pb.astype(f32)
                do = jnp.concatenate([dy_ref[dyr, HD * (GQA * kh + g):HD * (GQA * kh + g + 1)] for g in range(GQA)],
                                     axis=0).astype(bf16)
                dv = _tn(pb, do)
                dp = _nt(do, a["vb"])
                delta = jnp.sum(p * dp, axis=-1, keepdims=True)
                ds = p * (dp - delta)
                for g in range(GQA):
                    dbias_scr[GQA * kh + g] += ds[g * BLK:(g + 1) * BLK]
                dsb = ds.astype(bf16)
                dqn = _nn(dsb, a["knb"]) * SCALE
                dkn = _tn(dsb, a["qsb"])
                qhat, khat = a["qhat"], a["khat"]
                dqg_ref[...] += jnp.sum(dqn * qhat, axis=0, keepdims=True)
                dkg_ref[...] += jnp.sum(dkn * khat, axis=0, keepdims=True)
                dqh = dqn * qg
                dq = a["rq"] * (dqh - qhat * jnp.mean(dqh * qhat, axis=-1, keepdims=True))
                dkh = dkn * kg
                dk = a["rk"] * (dkh - khat * jnp.mean(dkh * khat, axis=-1, keepdims=True))
                kc = DATTN + HD * kh
                vc = DATTN + DKV + HD * kh
                for g in range(GQA):
                    hc = HD * (GQA * kh + g)
                    dz_ref[rows, hc:hc + HD] = dq[g * BLK:(g + 1) * BLK]
                dz_ref[rows, kc:kc + HD] = dk[BLK:2 * BLK]
                dz_ref[rows, vc:vc + HD] = dv[BLK:2 * BLK]

                def kv_prev(dk=dk, dv=dv, kc=kc, vc=vc, prow=prow):
                    dz_ref[prow, kc:kc + HD] += dk[0:BLK]
                    dz_ref[prow, vc:vc + HD] += dv[0:BLK]

                into_prev(kv_prev)

            for g, w in enumerate(POOL_WINDOWS):
                c0 = DATTN + 2 * DKV + PGD * g
                pooled, cnt = _pool_group(win, g, w)
                pb = pooled.astype(bf16)
                wb = pw_ref[g].astype(bf16)
                dyp = dy_ref[dyr, DATTN + PGD * g:DATTN + PGD * (g + 1)]
                ypre = _nn(pb, wb)
                dps_ref[:, PGD * g:PGD * (g + 1)] += jnp.sum(dyp * ypre, axis=0, keepdims=True)
                dyg = (dyp * ps_ref[:, PGD * g:PGD * (g + 1)]).astype(bf16)
                dpw_ref[g] += _tn(pb, dyg)
                dpooled = _nt(dyg, wb)
                due = jnp.concatenate([jnp.zeros((BLK, PGD), f32), dpooled / cnt], axis=0)
                for k in POOL_STEPS[w]:
                    due = due + pltpu.roll(due, 2 * BLK - k, axis=0)
                dz_ref[rows, c0:c0 + PGD] = due[BLK:2 * BLK] - dpooled

                def pool_prev(due=due, c0=c0, prow=prow):
                    dz_ref[prow, c0:c0 + PGD] += due[0:BLK]

                into_prev(pool_prev)

        @pl.when(n == nsteps - 1)
        def _():
            bk = bk_ref[...]
            ri = lax.broadcasted_iota(i32, (NBUCK, NH), 0)
            ci = lax.broadcasted_iota(i32, (NBUCK, NH), 1)

            def step(b, acc):
                for h in range(NH):
                    sel = jnp.where(bk == b, dbias_scr[h], 0.0)
                    tot = jnp.sum(jnp.sum(sel, axis=1, keepdims=True), axis=0, keepdims=True)
                    acc = acc + jnp.where((ri == b) & (ci == h), tot, 0.0)
                return acc

            drb_ref[...] = lax.fori_loop(0, NBUCK, step, jnp.zeros((NBUCK, NH), f32))
            lane = lax.broadcasted_iota(i32, (1, 128), 1)
            dsk = jnp.zeros((1, 128), f32)
            for h in range(NH):
                tot = jnp.sum(jnp.sum(dbias_scr[h], axis=1, keepdims=True), axis=0, keepdims=True)
                dsk = dsk - jnp.where(lane == h, tot, 0.0)
            dsk_ref[...] = dsk

    full = lambda *shape: pl.BlockSpec(shape, lambda n: (0,) * len(shape))
    npg = len(POOL_WINDOWS)
    return pl.pallas_call(
        body, grid=(nsteps,),
        in_specs=[pl.BlockSpec((step_rows, DIN), lambda n: (n, 0)),
                  pl.BlockSpec((BLK, DIN), lambda n: (jnp.maximum(n * MIX_SUB - 1, 0), 0)),
                  pl.BlockSpec((step_rows, DMIX), lambda n: (n, 0)),
                  pl.BlockSpec((MIX_SUB, NH, BLK, 2 * BLK), lambda n: (n, 0, 0, 0)),
                  full(1, HD), full(1, HD), full(BLK, 2 * BLK), full(npg, PGD, PGD), full(1, DPOOL)],
        out_specs=[full(T, DIN), full(1, HD), full(1, HD), full(1, 128), full(NBUCK, NH),
                   full(npg, PGD, PGD), full(1, DPOOL)],
        out_shape=(SDS((T, DIN), f32), SDS((1, HD), f32), SDS((1, HD), f32), SDS((1, 128), f32),
                   SDS((NBUCK, NH), f32), SDS((npg, PGD, PGD), f32), SDS((1, DPOOL), f32)),
        scratch_shapes=[pltpu.VMEM((NH, BLK, 2 * BLK), f32)],
        compiler_params=_cparams(("arbitrary",), VMEM_LIMIT_V7X),
        name=name)(z, z, dy, probs, qg, kg, bucket, pool_w, pscale)


class _LocalWeights:
    def __init__(self, w1, wint, wout, w2):
        self.w1, self.wint, self.wout, self.w2 = w1, wint, wout, w2

    def ffn1(self):
        return self.w1

    def first_norm(self, x, gain):
        return _norm_fwd(x, gain, "norm1_fwd")

    def after_ffn1(self, gain, x1):
        return gain

    def mix(self, after):
        return self.wint, self.wout

    def before_out_proj(self, wout, after):
        return wout

    def ffn2(self, after):
        return self.w2

    def mix_ffn2_grads_ready(self, dwint, dwout, dw2, dh2):
        self.grads_rest = (dwint, dwout, dw2)
        return dh2

    def before_ffn1_bwd(self, dx1b):
        return dx1b

    def ffn1_grads_ready(self, dw1, small, gain):
        self.dw1 = dw1
        return gain


def _local_step(x, target, weights, g1, gm, g3, qg, kg, sinks, relb, pool_w, pscale):
    bucket = jnp.asarray(_t5_bucket_table())
    sk = sinks.reshape(NH)
    w1 = weights.ffn1()
    h1 = weights.first_norm(x, g1)
    x1, gate1, up1 = _ffn_fwd(h1, w1, x, None, "ffn1_fwd")
    h2 = _norm_fwd(x1, weights.after_ffn1(gm, x1), "norm2_fwd")
    wint, wout = weights.mix(h2)
    z = _in_proj_fwd(h2, wint, "in_proj_fwd")
    ymix, probs = _mix_fwd(z, qg, kg, sk, relb, bucket, pool_w, pscale, "mix_fwd")
    wout = weights.before_out_proj(wout, ymix)
    x2, h3 = _out_proj_fwd(ymix, wout, x1, g3, "out_proj_fwd")
    w2 = weights.ffn2(h3)
    dy, gate2, up2, dyb, loss_lanes = _ffn_fwd(h3, w2, x2, target, "ffn2_fwd")

    dh3, dw2 = _ffn_bwd(dyb, h3, gate2, up2, w2, "ffn2_bwd")
    dx2, dx2b, dg3 = _norm_bwd(dh3, x2, g3, dy, 1.0, "norm3_bwd")
    dymix, dwout = _out_proj_bwd(dx2b, wout, ymix, "out_proj_bwd")
    dz, dqg, dkg, dsk, drb, dpw, dps = _mix_bwd(z, dymix, probs, qg, kg, relb, bucket, pool_w, pscale, "mix_bwd")
    dh2, dwint = _in_proj_bwd(dz, wint, h2, "in_proj_bwd")
    dh2 = weights.mix_ffn2_grads_ready(dwint, dwout, dw2, dh2)
    dx1, dx1b, dgm = _norm_bwd(dh2, x1, gm, dx2, 0.5, "norm2_bwd")
    dx1b = weights.before_ffn1_bwd(dx1b)
    dgu1, dw1 = _ffn_bwd(dx1b, h1, gate1, up1, w1, "ffn1_bwd", input_grad=False)
    small = dict(mix_norm=dgm, ffn2_norm=dg3, pool_scale=dps, q_norm=dqg, k_norm=dkg,
                 attn_sinks=dsk[:, :NH], rel_bias=drb, pool_w=dpw, loss=loss_lanes)
    g1 = weights.ffn1_grads_ready(dw1, small, g1)
    dh1 = _ffn_input_grad(dgu1, w1, g1, "ffn1_bwd_input")
    gx, _, dg1 = _norm_bwd(dh1, x, g1, dx1, 1.0, "norm1_bwd")
    small["ffn1_norm"] = dg1
    return gx, (dw1, dwint, dwout, dw2), small


SMALL_NAMES = ("ffn1_norm", "mix_norm", "ffn2_norm", "pool_scale", "q_norm", "k_norm", "attn_sinks", "rel_bias",
               "pool_w", "loss")
SMALL_SHAPES = dict(ffn1_norm=(1, D), mix_norm=(1, D), ffn2_norm=(1, D), pool_scale=(1, DPOOL), q_norm=(1, HD),
                    k_norm=(1, HD), attn_sinks=(1, NH), rel_bias=(NBUCK, NH),
                    pool_w=(1, len(POOL_WINDOWS), PGD, PGD), loss=(1, 128))


def _small_rows(name):
    return -(-int(np.prod(SMALL_SHAPES[name])) // 128)


SMALL_OFF = {}
_r = 0
for _n in SMALL_NAMES:
    SMALL_OFF[_n] = _r
    _r += _small_rows(_n)
SMALL_ROWS = -(-_r // 8) * 8
LOSS_ROW = SMALL_OFF["loss"]


def _pack_small(vals):
    parts = []
    for n in SMALL_NAMES:
        size = _small_rows(n) * 128
        if n in vals:
            flat = vals[n].astype(f32).reshape(-1)
            parts.append(jnp.pad(flat, (0, size - flat.shape[0])))
        else:
            parts.append(jnp.zeros((size,), f32))
    flat = jnp.concatenate(parts)
    flat = jnp.pad(flat, (0, SMALL_ROWS * 128 - flat.shape[0]))
    return flat.reshape(SMALL_ROWS, 128)


def _unpack_small(packed, name):
    size = int(np.prod(SMALL_SHAPES[name]))
    r0 = SMALL_OFF[name]
    return packed[r0:r0 + _small_rows(name)].reshape(-1)[:size].reshape(SMALL_SHAPES[name])


def _position():
    return lax.axis_index("x"), lax.axis_index("y"), lax.axis_index("c")


def _dev_index(x, y, c):
    return 4 * x + 2 * y + c


G1_PIECES, MIX_PIECES, F2_PIECES = (0, 1, 2), (3, 4), (5, 6, 7)


def _group_rows(pieces):
    return sum(PIECE_ROWS[k] for k in pieces)


def _shard_piece(s_ref, k):
    return s_ref.at[pl.ds(PIECE_OFF[k], PIECE_ROWS[k]), :]


def _shard_group(s_ref, pieces):
    return s_ref.at[pl.ds(PIECE_OFF[pieces[0]], _group_rows(pieces)), :]


def _weight_pieces(w1_ref=None, wi_ref=None, wo_ref=None, w2_ref=None):
    arrs = {}
    if w1_ref is not None:
        arrs.update({0: w1_ref.at[0], 1: w1_ref.at[1], 2: w1_ref.at[2]})
    if wi_ref is not None:
        arrs[3] = wi_ref
    if wo_ref is not None:
        arrs[4] = wo_ref
    if w2_ref is not None:
        arrs.update({5: w2_ref.at[0], 6: w2_ref.at[1], 7: w2_ref.at[2]})
    return arrs


def _block_rows(arrs, k, dev):
    r = PIECE_ROWS[k]
    return arrs[k].at[pl.ds(pl.multiple_of(_dev_index(*dev) * r, 16), r), :]


NORM_ROWS = 512


def _all_gather_ffn1(shard, x, gain):
    pieces = G1_PIECES
    rest_pieces = MIX_PIECES + F2_PIECES
    half = FS // 2
    T = x.shape[0]
    SIB, X0, X1, Y0, Y1, RELAY_Y, RELAY_X, ON_X, ON_Y, ON_D0, ON_D1 = range(11)

    def body(s_ref, x_ref, g_ref, w1_ref, h_ref, wi_ref, wo_ref, w2_ref, xbuf, hbuf, rest_buf,
             send_sems, recv_sems, local_sem, norm_sems):
        x, y, c = _position()
        me, sib = (x, y, c), (x, y, 1 - c)
        xn, yn, dg = (1 - x, y, c), (x, 1 - y, c), (1 - x, 1 - y, c)
        arrs = _weight_pieces(w1_ref=w1_ref)

        def place_rest():
            rest = _weight_pieces(wi_ref=wi_ref, wo_ref=wo_ref, w2_ref=w2_ref)
            grp = _shard_group(s_ref, rest_pieces)
            load = pltpu.make_async_copy(grp, rest_buf, norm_sems.at[0])
            load.start()
            load.wait()
            base = PIECE_OFF[rest_pieces[0]]
            for k in rest_pieces:
                pltpu.make_async_copy(rest_buf.at[pl.ds(PIECE_OFF[k] - base, PIECE_ROWS[k]), :],
                                      _block_rows(rest, k, me), norm_sems.at[1]).start()
            pltpu.make_async_copy(grp, rest_buf, norm_sems.at[1]).wait()

        def first_norm():
            for r in range(0, T, NORM_ROWS):
                load = pltpu.make_async_copy(x_ref.at[pl.ds(r, NORM_ROWS), :], xbuf, norm_sems.at[0])
                load.start()
                load.wait()
                xv = xbuf[...]
                rs = lax.rsqrt(jnp.mean(xv * xv, axis=-1, keepdims=True) + EPS)
                hbuf[...] = (xv * rs * g_ref[...]).astype(bf16)
                store = pltpu.make_async_copy(hbuf, h_ref.at[pl.ds(r, NORM_ROWS), :], norm_sems.at[1])
                store.start()
                store.wait()

        def rows_of(k, block, hf):
            r = PIECE_ROWS[k]
            start, size = (0, r) if hf is None else (hf * half, half)
            return arrs[k].at[pl.ds(pl.multiple_of(_dev_index(*block) * r + start, 16), size), :]

        def copies(rel, block, hf, to, from_shard=False):
            def src(k):
                if not from_shard:
                    return rows_of(k, block, hf)
                start, size = (0, PIECE_ROWS[k]) if hf is None else (hf * half, half)
                return s_ref.at[pl.ds(PIECE_OFF[k] + start, size), :]
            return [pltpu.make_async_remote_copy(
                src_ref=src(k), dst_ref=rows_of(k, block, hf), send_sem=send_sems.at[rel], recv_sem=recv_sems.at[rel],
                device_id=to, device_id_type=MESH) for k in pieces]

        def waiter(rel, hf):
            nrows = len(pieces) * (FS if hf is None else half)
            grp = s_ref.at[pl.ds(0, nrows), :]
            return pltpu.make_async_remote_copy(src_ref=grp, dst_ref=grp, send_sem=send_sems.at[rel],
                                                recv_sem=recv_sems.at[rel], device_id=me, device_id_type=MESH)

        def start(cps):
            for cp in cps:
                cp.start()

        mine = [pltpu.make_async_copy(_shard_piece(s_ref, k), _block_rows(arrs, k, me), local_sem) for k in pieces]
        start(mine)
        start(copies(SIB, me, None, sib, True))
        start(copies(X0, me, 0, xn, True))
        start(copies(Y1, me, 1, yn, True))
        start(copies(X1, me, 1, xn, True))
        start(copies(Y0, me, 0, yn, True))
        first_norm()
        place_rest()
        waiter(X0, 0).wait_recv()
        start(copies(RELAY_Y, xn, 0, yn))
        waiter(Y1, 1).wait_recv()
        start(copies(RELAY_X, yn, 1, xn))
        waiter(X1, 1).wait_recv()
        start(copies(ON_X, xn, None, sib))
        waiter(Y0, 0).wait_recv()
        start(copies(ON_Y, yn, None, sib))
        waiter(RELAY_Y, 0).wait_recv()
        start(copies(ON_D0, dg, 0, sib))
        waiter(RELAY_X, 1).wait_recv()
        start(copies(ON_D1, dg, 1, sib))
        waiter(SIB, None).wait_recv()
        waiter(ON_X, None).wait_recv()
        waiter(ON_Y, None).wait_recv()
        waiter(ON_D0, 0).wait_recv()
        waiter(ON_D1, 1).wait_recv()
        for rel, hf in ((SIB, None), (X0, 0), (X1, 1), (Y0, 0), (Y1, 1), (RELAY_Y, 0), (RELAY_X, 1),
                        (ON_X, None), (ON_Y, None), (ON_D0, 0), (ON_D1, 1)):
            waiter(rel, hf).wait_send()
        grp = _shard_group(s_ref, pieces)
        pltpu.make_async_copy(grp, grp, local_sem).wait()

    hbm = pl.BlockSpec(memory_space=pl.ANY)
    return pl.pallas_call(
        body, in_specs=[hbm, hbm, pl.BlockSpec(memory_space=pltpu.VMEM)], out_specs=[hbm] * 5,
        out_shape=(SDS((3, F, D), bf16), SDS((T, D), bf16),
                   SDS((DIN, D), bf16), SDS((DMIX, D), bf16), SDS((3, F, D), bf16)),
        scratch_shapes=[pltpu.VMEM((NORM_ROWS, D), f32), pltpu.VMEM((NORM_ROWS, D), bf16),
                        pltpu.VMEM((_group_rows(rest_pieces), D), bf16),
                        pltpu.SemaphoreType.DMA((11,)), pltpu.SemaphoreType.DMA((11,)), pltpu.SemaphoreType.DMA,
                        pltpu.SemaphoreType.DMA((2,))],
        compiler_params=pltpu.CompilerParams(has_side_effects=True),
        name="all_gather_ffn1")(shard, x, gain)


HBM_SPEC = pl.BlockSpec(memory_space=pltpu.HBM)
SEM_SPEC = pl.BlockSpec(memory_space=pltpu.SEMAPHORE)
ANY_SPEC = pl.BlockSpec(memory_space=pl.ANY)
SPLIT_EFFECT = pltpu.SideEffectType.DATAFLOW_SIDE_EFFECTING


def _in_hbm(a):
    return pltpu.with_memory_space_constraint(a, pltpu.HBM)


def _hbm_like(a):
    return pltpu.HBM(a.shape, a.dtype)


def _gather_rest_start(shard, wi, wo, w2, w1):
    def body(s_ref, wi_ref, wo_ref, w2_ref, w1_ref,
             ssem_m, rsem_m0, rsem_m, ssem_f, rsem_f0, rsem_f, s_o, wi_o, wo_o, w2_o, w1_o):
        x, y, c = _position()
        me, sib = (x, y, c), (x, y, 1 - c)
        chips = [(1 - x, y), (x, 1 - y), (1 - x, 1 - y)]
        arrs = _weight_pieces(wi_ref=wi_ref, wo_ref=wo_ref, w2_ref=w2_ref)
        for pieces, ssem, rsem0, rsem in ((MIX_PIECES, ssem_m, rsem_m0, rsem_m), (F2_PIECES, ssem_f, rsem_f0, rsem_f)):
            for p in pieces:
                pltpu.make_async_remote_copy(
                    src_ref=_shard_piece(s_ref, p), dst_ref=_block_rows(arrs, p, me), send_sem=ssem.at[0],
                    recv_sem=rsem0, device_id=sib, device_id_type=MESH).start()
            for j, chip in enumerate(chips):
                for p in pieces:
                    pltpu.make_async_remote_copy(
                        src_ref=_shard_piece(s_ref, p), dst_ref=_block_rows(arrs, p, me), send_sem=ssem.at[1 + j],
                        recv_sem=rsem.at[j], device_id=(*chip, c), device_id_type=MESH).start()

    dma = pltpu.SemaphoreType.DMA
    return pl.pallas_call(
        body, name="gather_rest_start",
        out_shape=(dma((4,)), dma(()), dma((3,)), dma((4,)), dma(()), dma((3,)),
                   _hbm_like(shard), _hbm_like(wi), _hbm_like(wo), _hbm_like(w2), _hbm_like(w1)),
        in_specs=(HBM_SPEC,) * 5, out_specs=(SEM_SPEC,) * 6 + (HBM_SPEC,) * 5,
        input_output_aliases={0: 6, 1: 7, 2: 8, 3: 9, 4: 10},
        compiler_params=pltpu.CompilerParams(has_side_effects=SPLIT_EFFECT),
    )(_in_hbm(shard), _in_hbm(wi), _in_hbm(wo), _in_hbm(w2), _in_hbm(w1))


def _gather_mix_pass_on(rsem_m, wi, wo, thru, after):
    def body(wi_ref, wo_ref, thru_ref, rsem, after_ref, fsend, frecv, wi_o, wo_o, thru_o):
        x, y, c = _position()
        sib = (x, y, 1 - c)
        arrs = _weight_pieces(wi_ref=wi_ref, wo_ref=wo_ref)
        both = wi_ref.at[pl.ds(0, _group_rows(MIX_PIECES)), :]
        for j, chip in enumerate([(1 - x, y), (x, 1 - y), (1 - x, 1 - y)]):
            pltpu.make_async_remote_copy(src_ref=both, dst_ref=both, send_sem=fsend.at[j], recv_sem=rsem.at[j],
                                         device_id=(x, y, c), device_id_type=MESH).wait_recv()
            for p in MIX_PIECES:
                rows = _block_rows(arrs, p, (*chip, c))
                pltpu.make_async_remote_copy(src_ref=rows, dst_ref=rows, send_sem=fsend.at[j], recv_sem=frecv.at[j],
                                             device_id=sib, device_id_type=MESH).start()

    dma = pltpu.SemaphoreType.DMA
    return pl.pallas_call(
        body, name="gather_mix_pass_on",
        out_shape=(dma((3,)), dma((3,)), _hbm_like(wi), _hbm_like(wo), _hbm_like(thru)),
        in_specs=(HBM_SPEC, HBM_SPEC, HBM_SPEC, SEM_SPEC, ANY_SPEC), out_specs=(SEM_SPEC, SEM_SPEC) + (HBM_SPEC,) * 3,
        input_output_aliases={0: 2, 1: 3, 2: 4},
        compiler_params=pltpu.CompilerParams(has_side_effects=SPLIT_EFFECT),
    )(wi, wo, _in_hbm(thru), rsem_m, after)


def _gather_mix_wait(ssem_m, rsem_m0, fsend, frecv, shard, wi, wo, after):
    def body(s_ref, wi_ref, wo_ref, ssem, rsem0, fs, fr, after_ref, s_o, wi_o, wo_o):
        x, y, c = _position()
        grp = _shard_group(s_ref, MIX_PIECES)

        def waiter(send_sem, recv_sem):
            return pltpu.make_async_remote_copy(src_ref=grp, dst_ref=grp, send_sem=send_sem, recv_sem=recv_sem,
                                                device_id=(x, y, c), device_id_type=MESH)

        waiter(ssem.at[0], rsem0).wait_recv()
        for j in range(3):
            waiter(fs.at[j], fr.at[j]).wait_recv()
        for rel in range(4):
            waiter(ssem.at[rel], rsem0).wait_send()
        for j in range(3):
            waiter(fs.at[j], fr.at[j]).wait_send()

    return pl.pallas_call(
        body, name="gather_mix_wait", out_shape=(_hbm_like(shard), _hbm_like(wi), _hbm_like(wo)),
        in_specs=(HBM_SPEC,) * 3 + (SEM_SPEC,) * 4 + (ANY_SPEC,), out_specs=(HBM_SPEC,) * 3,
        input_output_aliases={0: 0, 1: 1, 2: 2},
        compiler_params=pltpu.CompilerParams(has_side_effects=SPLIT_EFFECT),
    )(shard, wi, wo, ssem_m, rsem_m0, fsend, frecv, after)


def _gather_ffn2_pass_on(rsem_f, w2, wo, after):
    def body(w2_ref, wo_ref, rsem, after_ref, fsend, frecv, w2_o, wo_o):
        x, y, c = _position()
        sib = (x, y, 1 - c)
        chips = [(1 - x, y), (x, 1 - y), (1 - x, 1 - y)]
        arrs = _weight_pieces(w2_ref=w2_ref)
        three = w2_ref.at[0, pl.ds(0, _group_rows(F2_PIECES)), :]
        for j, chip in enumerate(chips):
            pltpu.make_async_remote_copy(src_ref=three, dst_ref=three, send_sem=fsend.at[j], recv_sem=rsem.at[j],
                                         device_id=(x, y, c), device_id_type=MESH).wait_recv()
            for p in F2_PIECES:
                rows = _block_rows(arrs, p, (*chip, c))
                pltpu.make_async_remote_copy(src_ref=rows, dst_ref=rows, send_sem=fsend.at[j], recv_sem=frecv.at[j],
                                             device_id=sib, device_id_type=MESH).start()

    dma = pltpu.SemaphoreType.DMA
    return pl.pallas_call(
        body, name="gather_ffn2_pass_on", out_shape=(dma((3,)), dma((3,)), _hbm_like(w2), _hbm_like(wo)),
        in_specs=(HBM_SPEC, HBM_SPEC, SEM_SPEC, ANY_SPEC), out_specs=(SEM_SPEC, SEM_SPEC, HBM_SPEC, HBM_SPEC),
        input_output_aliases={0: 2, 1: 3},
        compiler_params=pltpu.CompilerParams(has_side_effects=SPLIT_EFFECT),
    )(w2, wo, rsem_f, after)


def _gather_ffn2_wait(ssem_f, rsem_f0, fsend, frecv, shard, w2, after):
    def body(s_ref, w2_ref, ssem, rsem0, fs, fr, after_ref, w2_o):
        x, y, c = _position()
        grp = _shard_group(s_ref, F2_PIECES)

        def waiter(send_sem, recv_sem):
            return pltpu.make_async_remote_copy(src_ref=grp, dst_ref=grp, send_sem=send_sem, recv_sem=recv_sem,
                                                device_id=(x, y, c), device_id_type=MESH)

        waiter(ssem.at[0], rsem0).wait_recv()
        for j in range(3):
            waiter(fs.at[j], fr.at[j]).wait_recv()
        for rel in range(4):
            waiter(ssem.at[rel], rsem0).wait_send()
        for j in range(3):
            waiter(fs.at[j], fr.at[j]).wait_send()

    return pl.pallas_call(
        body, name="gather_ffn2_wait", out_shape=_hbm_like(w2),
        in_specs=(HBM_SPEC, HBM_SPEC, SEM_SPEC, SEM_SPEC, SEM_SPEC, SEM_SPEC, ANY_SPEC), out_specs=HBM_SPEC,
        input_output_aliases={1: 0},
        compiler_params=pltpu.CompilerParams(has_side_effects=SPLIT_EFFECT),
    )(shard, w2, ssem_f, rsem_f0, fsend, frecv, after)


class _GatheredWeights(_LocalWeights):
    def __init__(self, shard, x, gain1):
        w1, self.h1, wi, wo, w2 = _all_gather_ffn1(shard, x, gain1)
        (self.ssem_m, self.rsem_m0, self.rsem_m, self.ssem_f, self.rsem_f0, self.rsem_f,
         self.shard, self.wi, self.wo, self.w2_part, self.w1) = _gather_rest_start(shard, wi, wo, w2, w1)

    def first_norm(self, x, gain):
        return self.h1

    def after_ffn1(self, gain, x1):
        self.fsend_m, self.frecv_m, self.wi, self.wo, gain = _gather_mix_pass_on(self.rsem_m, self.wi, self.wo, gain, x1)
        return gain

    def mix(self, after):
        self.shard, wint, wout = _gather_mix_wait(self.ssem_m, self.rsem_m0, self.fsend_m, self.frecv_m, self.shard,
                                                  self.wi, self.wo, after)
        return wint, wout

    def before_out_proj(self, wout, after):
        self.fsend, self.frecv, self.w2_part, wout = _gather_ffn2_pass_on(self.rsem_f, self.w2_part, wout, after)
        return wout

    def ffn2(self, after):
        return _gather_ffn2_wait(self.ssem_f, self.rsem_f0, self.fsend, self.frecv, self.shard, self.w2_part, after)

    def mix_ffn2_grads_ready(self, dwint, dwout, dw2, dh2):
        rx1 = lax.empty((4, RSA_ROWS, D), bf16)
        self.sa, self.ra, dwint, dwout, dw2, rx1, dh2 = _rsa_level1_start(dwint, dwout, dw2, rx1, dh2)
        self.level1 = (dwint, dwout, dw2, rx1)
        return dh2

    def before_ffn1_bwd(self, dx1b):
        dwint, dwout, dw2, rx1 = _rsa_level1_wait(self.sa, self.ra, *self.level1, dx1b)
        tx, self.acc = _rsa_chip_sums(dwint, dwout, dw2, rx1)
        rx2 = lax.empty((3, RSA_ROWS, D), bf16)
        self.sb, self.rb, self.tx, self.rx2, dx1b = _rsa_level2_start(tx, rx2, dx1b)
        return dx1b

    def ffn1_grads_ready(self, dw1, small, gain):
        tx, self.own1, self.small_tot = _reduce_scatter_ffn1_head(dw1, _pack_small(small))
        nrows = len(G1_PIECES) * FS
        self.ssem1, self.rsem1, self.tx1, self.landed1, gain, self.small_tot = _rs1_tail_start(
            tx, lax.empty((3, nrows, D), bf16), gain, self.small_tot)
        return gain

    def ffn1_grads_parts(self, after):
        return self.own1, _rs1_tail_wait(self.ssem1, self.rsem1, self.tx1, self.landed1, after)

    def mix_ffn2_grads_parts(self, after):
        rx2 = _rsa_level2_wait(self.sb, self.rb, self.tx, self.rx2, after)
        return self.acc, rx2


def _reduce_scatter_ffn1_head(dw1, small_packed):
    pieces = G1_PIECES
    half = FS // 2
    hrows = len(pieces) * half
    nrows = 2 * hrows

    def body(d1_ref, p_ref, tx_ref, own_ref, rx1_ref, tot_ref,
             own_buf, rx_buf, txb, acc, sa, ra, lsem, osem, pair, chips, small_send, small_recv):
        x, y, c = _position()
        me, sib = (x, y, c), (x, y, 1 - c)
        rel_chips = [(x, y), (1 - x, y), (x, 1 - y), (1 - x, 1 - y)]
        srcs = _weight_pieces(w1_ref=d1_ref)

        my_chip = 2 * x + y
        pair[c] = p_ref[...]
        swap = pltpu.make_async_remote_copy(
            src_ref=p_ref, dst_ref=pair.at[c], send_sem=small_send.at[0], recv_sem=small_recv.at[0],
            device_id=sib, device_id_type=MESH)
        swap.start()
        small = [pltpu.make_async_remote_copy(
            src_ref=chips.at[my_chip], dst_ref=chips.at[my_chip], send_sem=small_send.at[j], recv_sem=small_recv.at[j],
            device_id=(*rel_chips[j], c), device_id_type=MESH) for j in (1, 2, 3)]

        def part(k, dev, hf):
            r = PIECE_ROWS[k]
            return srcs[k].at[pl.ds(pl.multiple_of(_dev_index(*dev) * r + hf * half, 16), half), :]

        def slot(ref, k, hf):
            return ref.at[pl.ds(hf * hrows + k * half, half), :]

        halves = [(k, hf) for hf in (0, 1) for k in pieces]

        for j in (3, 1, 2, 0):
            for k, hf in halves:
                pltpu.make_async_remote_copy(
                    src_ref=part(k, (*rel_chips[j], 1 - c), hf), dst_ref=slot(rx1_ref.at[j], k, hf),
                    send_sem=sa.at[j], recv_sem=ra.at[j], device_id=sib, device_id_type=MESH).start()

        def wait_a(j):
            return pltpu.make_async_remote_copy(src_ref=rx1_ref.at[j], dst_ref=rx1_ref.at[j], send_sem=sa.at[j],
                                                recv_sem=ra.at[j], device_id=me, device_id_type=MESH)

        swap.wait_recv()
        chips[my_chip] = pair[0] + pair[1]
        for cp in small:
            cp.start()

        def chip_sum(j, dst):
            loads = [pltpu.make_async_copy(part(k, (*rel_chips[j], c), hf), slot(own_buf, k, hf), lsem.at[0])
                     for k, hf in halves]
            for cp in loads:
                cp.start()
            wait_a(j).wait_recv()
            got = pltpu.make_async_copy(rx1_ref.at[j], rx_buf, lsem.at[1])
            got.start()
            pltpu.make_async_copy(rx_buf, rx_buf, lsem.at[0]).wait()
            got.wait()

            def add(i, carry):
                rows = pl.ds(pl.multiple_of(i * half, 16), half)
                tot = own_buf[rows, :].astype(f32) + rx_buf[rows, :].astype(f32)
                dst[rows, :] = tot.astype(dst.dtype)
                return carry

            lax.fori_loop(0, nrows // half, add, 0)

        outs = []
        for j in (3, 1, 2):
            chip_sum(j, txb.at[j - 1])
            outs.append(pltpu.make_async_copy(txb.at[j - 1], tx_ref.at[j - 1], osem.at[j - 1]))
            outs[-1].start()
        chip_sum(0, acc)
        outs.append(pltpu.make_async_copy(acc, own_ref, osem.at[3]))
        outs[-1].start()
        for cp in small:
            cp.wait_recv()
        tot = (chips[0] + chips[1]) + (chips[2] + chips[3])
        tot_ref[...] = tot
        loss = jnp.sum(tot[LOSS_ROW:LOSS_ROW + 1, :], axis=-1, keepdims=True)
        tot_ref[LOSS_ROW:LOSS_ROW + 1, :] = jnp.broadcast_to(loss, (1, 128))
        for cp in outs:
            cp.wait()
        for j in range(4):
            wait_a(j).wait_send()
        swap.wait_send()
        for cp in small:
            cp.wait_send()

    hbm = pl.BlockSpec(memory_space=pl.ANY)
    vm = pl.BlockSpec(memory_space=pltpu.VMEM)
    outs = pl.pallas_call(
        body, in_specs=[hbm, vm], out_specs=[hbm] * 3 + [vm],
        out_shape=(SDS((3, nrows, D), bf16), SDS((nrows, D), f32), SDS((4, nrows, D), bf16),
                   SDS((SMALL_ROWS, 128), f32)),
        scratch_shapes=[pltpu.VMEM((nrows, D), bf16), pltpu.VMEM((nrows, D), bf16),
                        pltpu.VMEM((3, nrows, D), bf16), pltpu.VMEM((nrows, D), f32),
                        pltpu.SemaphoreType.DMA((4,)), pltpu.SemaphoreType.DMA((4,)),
                        pltpu.SemaphoreType.DMA((2,)), pltpu.SemaphoreType.DMA((4,)),
                        pltpu.VMEM((2, SMALL_ROWS, 128), f32), pltpu.VMEM((4, SMALL_ROWS, 128), f32),
                        pltpu.SemaphoreType.DMA((4,)), pltpu.SemaphoreType.DMA((4,))],
        compiler_params=pltpu.CompilerParams(has_side_effects=True, vmem_limit_bytes=VMEM_LIMIT_V7X),
        name="reduce_scatter_ffn1_head")(dw1, small_packed)
    return outs[0], outs[1], outs[-1]


def _rs1_peers(x, y, c):
    return (1 - x, y, c), (x, 1 - y, c), (1 - x, 1 - y, c)


def _rs1_tail_start(tx, landed, thru_a, thru_b):
    def body(tx_ref, land_ref, ta_ref, tb_ref, ssem, rsem, tx_o, land_o, ta_o, tb_o):
        peers = _rs1_peers(*_position())
        for j in (2, 0, 1):
            pltpu.make_async_remote_copy(src_ref=tx_ref.at[j], dst_ref=land_ref.at[j], send_sem=ssem.at[j],
                                         recv_sem=rsem.at[j], device_id=peers[j], device_id_type=MESH).start()

    dma = pltpu.SemaphoreType.DMA
    arrs = (tx, landed, thru_a, thru_b)
    return pl.pallas_call(
        body, name="rs1_tail_start", out_shape=(dma((3,)), dma((3,))) + tuple(_hbm_like(a) for a in arrs),
        in_specs=(HBM_SPEC,) * 4, out_specs=(SEM_SPEC,) * 2 + (HBM_SPEC,) * 4,
        input_output_aliases={0: 2, 1: 3, 2: 4, 3: 5},
        compiler_params=pltpu.CompilerParams(has_side_effects=SPLIT_EFFECT),
    )(*[_in_hbm(a) for a in arrs])


def _rs1_tail_wait(ssem, rsem, tx, landed, after):
    def body(tx_ref, land_ref, ssem_ref, rsem_ref, after_ref, land_o):
        me = _position()
        for j in range(3):
            d = pltpu.make_async_remote_copy(src_ref=tx_ref.at[j], dst_ref=land_ref.at[j], send_sem=ssem_ref.at[j],
                                             recv_sem=rsem_ref.at[j], device_id=me, device_id_type=MESH)
            d.wait_recv()
            d.wait_send()

    return pl.pallas_call(
        body, name="rs1_tail_wait", out_shape=_hbm_like(landed),
        in_specs=(HBM_SPEC,) * 2 + (SEM_SPEC, SEM_SPEC, ANY_SPEC), out_specs=HBM_SPEC,
        input_output_aliases={1: 0},
        compiler_params=pltpu.CompilerParams(has_side_effects=SPLIT_EFFECT),
    )(tx, landed, ssem, rsem, after)


def _all_reduce_row(row, name):
    masks = [(a, b, cc) for a in (0, 1) for b in (0, 1) for cc in (0, 1)][1:]

    def body(row_ref, tot_ref, rows, send, recv):
        x, y, c = _position()
        mine = _dev_index(x, y, c)
        rows[mine] = row_ref[...]
        copies = [pltpu.make_async_remote_copy(
            src_ref=rows.at[mine], dst_ref=rows.at[mine], send_sem=send.at[j], recv_sem=recv.at[j],
            device_id=(x ^ a, y ^ b, c ^ cc), device_id_type=MESH) for j, (a, b, cc) in enumerate(masks)]
        for cp in copies:
            cp.start()
        for cp in copies:
            cp.wait_recv()
        tot_ref[...] = ((rows[0] + rows[1]) + (rows[2] + rows[3])) + ((rows[4] + rows[5]) + (rows[6] + rows[7]))
        for cp in copies:
            cp.wait_send()

    return pl.pallas_call(
        body, out_shape=SDS(row.shape, f32),
        scratch_shapes=[pltpu.VMEM((8,) + row.shape, f32), pltpu.SemaphoreType.DMA((7,)),
                        pltpu.SemaphoreType.DMA((7,))],
        compiler_params=pltpu.CompilerParams(has_side_effects=True), name=name)(row)


RSA_PIECES = MIX_PIECES + F2_PIECES
RSA_ROWS = _group_rows(RSA_PIECES)
RSA_OFF = {k: PIECE_OFF[k] - PIECE_OFF[RSA_PIECES[0]] for k in RSA_PIECES}
RSA_BLOCK = 192


def _rsa_rows(ref, k):
    return ref.at[pl.ds(RSA_OFF[k], PIECE_ROWS[k]), :]


def _rsa_level1_start(dwint, dwout, dw2, rx1, thru):
    def body(di_ref, do_ref, d2_ref, rx1_ref, thru_ref, sa, ra, di_o, do_o, d2_o, rx1_o, thru_o):
        x, y, c = _position()
        srcs = _weight_pieces(wi_ref=di_ref, wo_ref=do_ref, w2_ref=d2_ref)
        for j, chip in enumerate([(x, y), (1 - x, y), (x, 1 - y), (1 - x, 1 - y)]):
            for k in RSA_PIECES:
                pltpu.make_async_remote_copy(
                    src_ref=_block_rows(srcs, k, (*chip, 1 - c)), dst_ref=_rsa_rows(rx1_ref.at[j], k),
                    send_sem=sa.at[j], recv_sem=ra.at[j], device_id=(x, y, 1 - c), device_id_type=MESH).start()

    dma = pltpu.SemaphoreType.DMA
    arrs = (dwint, dwout, dw2, rx1, thru)
    return pl.pallas_call(
        body, name="rsa_level1_start", out_shape=(dma((4,)), dma((4,))) + tuple(_hbm_like(a) for a in arrs),
        in_specs=(HBM_SPEC,) * 5, out_specs=(SEM_SPEC,) * 2 + (HBM_SPEC,) * 5,
        input_output_aliases={0: 2, 1: 3, 2: 4, 3: 5, 4: 6},
        compiler_params=pltpu.CompilerParams(has_side_effects=SPLIT_EFFECT),
    )(*[_in_hbm(a) for a in arrs])


def _rsa_level1_wait(sa, ra, dwint, dwout, dw2, rx1, after):
    def body(di_ref, do_ref, d2_ref, rx1_ref, sa_ref, ra_ref, after_ref, di_o, do_o, d2_o, rx1_o):
        x, y, c = _position()
        for j in range(4):
            d = pltpu.make_async_remote_copy(src_ref=rx1_ref.at[j], dst_ref=rx1_ref.at[j], send_sem=sa_ref.at[j],
                                             recv_sem=ra_ref.at[j], device_id=(x, y, c), device_id_type=MESH)
            d.wait_recv()
            d.wait_send()

    arrs = (dwint, dwout, dw2, rx1)
    return pl.pallas_call(
        body, name="rsa_level1_wait", out_shape=tuple(_hbm_like(a) for a in arrs),
        in_specs=(HBM_SPEC,) * 4 + (SEM_SPEC, SEM_SPEC, ANY_SPEC), out_specs=(HBM_SPEC,) * 4,
        input_output_aliases={0: 0, 1: 1, 2: 2, 3: 3},
        compiler_params=pltpu.CompilerParams(has_side_effects=SPLIT_EFFECT),
    )(*arrs, sa, ra, after)


def _rsa_chip_sums(dwint, dwout, dw2, rx1):
    nblk = RSA_ROWS // RSA_BLOCK

    def body(di_ref, do_ref, d2_ref, rx1_ref, tx_ref, acc_ref, own_buf, rx_buf, tx_buf, acc_buf, in_sems, out_sems):
        x, y, c = _position()
        srcs = _weight_pieces(wi_ref=di_ref, wo_ref=do_ref, w2_ref=d2_ref)
        chips = [(x, y), (1 - x, y), (x, 1 - y), (1 - x, 1 - y)]

        def start_loads(j):
            s = j % 2
            for k in RSA_PIECES:
                pltpu.make_async_copy(_block_rows(srcs, k, (*chips[j], c)), _rsa_rows(own_buf.at[s], k),
                                      in_sems.at[2 * s]).start()
            pltpu.make_async_copy(rx1_ref.at[j], rx_buf.at[s], in_sems.at[2 * s + 1]).start()

        def wait_loads(j):
            s = j % 2
            pltpu.make_async_copy(rx1_ref.at[j], own_buf.at[s], in_sems.at[2 * s]).wait()
            pltpu.make_async_copy(rx1_ref.at[j], rx_buf.at[s], in_sems.at[2 * s + 1]).wait()

        def store(j):
            if j == 0:
                return pltpu.make_async_copy(acc_buf, acc_ref, out_sems.at[2])
            return pltpu.make_async_copy(tx_buf.at[j % 2], tx_ref.at[j - 1], out_sems.at[j % 2])

        start_loads(0)
        for j in range(4):
            s = j % 2
            if j + 1 < 4:
                start_loads(j + 1)
            wait_loads(j)
            if j == 3:
                store(1).wait()

            def add(i, carry, j=j, s=s):
                rows = pl.ds(pl.multiple_of(i * RSA_BLOCK, 16), RSA_BLOCK)
                tot = own_buf[s, rows, :].astype(f32) + rx_buf[s, rows, :].astype(f32)
                if j == 0:
                    acc_buf[rows, :] = tot
                else:
                    tx_buf[s, rows, :] = tot.astype(bf16)
                return carry

            lax.fori_loop(0, nblk, add, 0)
            store(j).start()
        store(0).wait()
        store(2).wait()
        store(3).wait()

    return pl.pallas_call(
        body, in_specs=[ANY_SPEC] * 4, out_specs=[ANY_SPEC] * 2,
        out_shape=(SDS((3, RSA_ROWS, D), bf16), SDS((RSA_ROWS, D), f32)),
        scratch_shapes=[pltpu.VMEM((2, RSA_ROWS, D), bf16), pltpu.VMEM((2, RSA_ROWS, D), bf16),
                        pltpu.VMEM((2, RSA_ROWS, D), bf16), pltpu.VMEM((RSA_ROWS, D), f32),
                        pltpu.SemaphoreType.DMA((4,)), pltpu.SemaphoreType.DMA((3,))],
        compiler_params=_cparams(None, VMEM_LIMIT_V7X), name="rsa_chip_sums")(dwint, dwout, dw2, rx1)


def _rsa_level2_start(tx, rx2, thru):
    def body(tx_ref, rx2_ref, thru_ref, sb, rb, tx_o, rx2_o, thru_o):
        x, y, c = _position()
        for j, chip in enumerate([(1 - x, y), (x, 1 - y), (1 - x, 1 - y)]):
            pltpu.make_async_remote_copy(src_ref=tx_ref.at[j], dst_ref=rx2_ref.at[j], send_sem=sb.at[j],
                                         recv_sem=rb.at[j], device_id=(*chip, c), device_id_type=MESH).start()

    dma = pltpu.SemaphoreType.DMA
    arrs = (tx, rx2, thru)
    return pl.pallas_call(
        body, name="rsa_level2_start", out_shape=(dma((3,)), dma((3,))) + tuple(_hbm_like(a) for a in arrs),
        in_specs=(HBM_SPEC,) * 3, out_specs=(SEM_SPEC,) * 2 + (HBM_SPEC,) * 3,
        input_output_aliases={0: 2, 1: 3, 2: 4},
        compiler_params=pltpu.CompilerParams(has_side_effects=SPLIT_EFFECT),
    )(*[_in_hbm(a) for a in arrs])


def _rsa_level2_wait(sb, rb, tx, rx2, after):
    def body(tx_ref, rx2_ref, sb_ref, rb_ref, after_ref, rx2_o):
        x, y, c = _position()
        for j in range(3):
            d = pltpu.make_async_remote_copy(src_ref=tx_ref.at[j], dst_ref=rx2_ref.at[j], send_sem=sb_ref.at[j],
                                             recv_sem=rb_ref.at[j], device_id=(x, y, c), device_id_type=MESH)
            d.wait_recv()
            d.wait_send()

    return pl.pallas_call(
        body, name="rsa_level2_wait", out_shape=_hbm_like(rx2),
        in_specs=(HBM_SPEC, HBM_SPEC, SEM_SPEC, SEM_SPEC, ANY_SPEC), out_specs=HBM_SPEC,
        input_output_aliases={1: 0},
        compiler_params=pltpu.CompilerParams(has_side_effects=SPLIT_EFFECT),
    )(tx, rx2, sb, rb, after)


def _adamw_math(w, g, m, v):
    m = ADAM_B1 * m + (1.0 - ADAM_B1) * g
    v = ADAM_B2 * v + (1.0 - ADAM_B2) * (g * g)
    m_hat = m / (1.0 - ADAM_B1 ** ADAM_STEP)
    v_hat = v / (1.0 - ADAM_B2 ** ADAM_STEP)
    delta = -ADAM_LR * (m_hat / (jnp.sqrt(v_hat) + ADAM_EPS) + ADAM_WD * w)
    return delta, m, v


def _adamw_big(pieces, ws, ms, vs, own, landed, name):
    npiece = len(pieces)
    nland = sum(a.shape[0] if a.ndim == 3 else 1 for a in landed)
    rmax = max(PIECE_ROWS[k] for k in pieces)
    half = FS // 2

    def segments(k):
        if k in G1_PIECES:
            return [(hf * len(G1_PIECES) * half + k * half, hf * half, half) for hf in (0, 1)]
        return [(RSA_OFF[k], 0, PIECE_ROWS[k])]

    def body(*refs):
        ins = (refs[0:npiece], refs[npiece:2 * npiece], refs[2 * npiece:3 * npiece])
        own_ref = refs[3 * npiece]
        nin = 3 * npiece + 1 + len(landed)
        land_refs = []
        for ref, a in zip(refs[3 * npiece + 1:nin], landed):
            land_refs += [ref.at[j] for j in range(a.shape[0])] if a.ndim == 3 else [ref]
        out_refs = refs[nin:nin + 4 * npiece]
        inb, landb, outb, in_sems, land_sems, out_sems = refs[nin + 4 * npiece:]

        def loads(i):
            s, k = i % 2, pieces[i]
            r = PIECE_ROWS[k]
            cps = [pltpu.make_async_copy(ins[q][i].at[0], inb.at[s, q, pl.ds(0, r), :], in_sems.at[4 * s + q])
                   for q in range(3)]
            waits = list(cps)
            for src0, dst0, n in segments(k):
                cps.append(pltpu.make_async_copy(own_ref.at[pl.ds(src0, n), :], inb.at[s, 3, pl.ds(dst0, n), :],
                                                 in_sems.at[4 * s + 3]))
                for p in range(nland):
                    cps.append(pltpu.make_async_copy(land_refs[p].at[pl.ds(src0, n), :],
                                                     landb.at[s, p, pl.ds(dst0, n), :], land_sems.at[nland * s + p]))
            own_rows = inb.at[s, 3, pl.ds(0, r), :]
            waits.append(pltpu.make_async_copy(own_rows, own_rows, in_sems.at[4 * s + 3]))
            for p in range(nland):
                rows = landb.at[s, p, pl.ds(0, r), :]
                waits.append(pltpu.make_async_copy(rows, rows, land_sems.at[nland * s + p]))
            return cps, waits

        def stores(i):
            s, r = i % 2, PIECE_ROWS[pieces[i]]
            return [pltpu.make_async_copy(outb.at[s, q, pl.ds(0, r), :], out_refs[q * npiece + i].at[0],
                                          out_sems.at[4 * s + q]) for q in range(4)]

        for cp in loads(0)[0]:
            cp.start()
        for i in range(npiece):
            s, r = i % 2, PIECE_ROWS[pieces[i]]
            if i + 1 < npiece:
                for cp in loads(i + 1)[0]:
                    cp.start()
            for cp in loads(i)[1]:
                cp.wait()
            if i >= 2:
                for cp in stores(i - 2):
                    cp.wait()
            g = inb[s, 3, 0:r, :]
            for p in range(nland):
                g = g + landb[s, p, 0:r, :].astype(f32)
            d, nm, nv = _adamw_math(inb[s, 0, 0:r, :], g, inb[s, 1, 0:r, :], inb[s, 2, 0:r, :])
            outb[s, 0, 0:r, :] = g
            outb[s, 1, 0:r, :] = d
            outb[s, 2, 0:r, :] = nm
            outb[s, 3, 0:r, :] = nv
            for cp in stores(i):
                cp.start()
        for i in range(max(npiece - 2, 0), npiece):
            for cp in stores(i):
                cp.wait()

    hbm = pl.BlockSpec(memory_space=pl.ANY)
    outs = pl.pallas_call(
        body, in_specs=[hbm] * (3 * npiece + 1 + len(landed)), out_specs=[hbm] * (4 * npiece),
        out_shape=tuple(SDS(w.shape, f32) for _ in range(4) for w in ws),
        scratch_shapes=[pltpu.VMEM((2, 4, rmax, D), f32), pltpu.VMEM((2, nland, rmax, D), bf16),
                        pltpu.VMEM((2, 4, rmax, D), f32),
                        pltpu.SemaphoreType.DMA((8,)), pltpu.SemaphoreType.DMA((2 * nland,)),
                        pltpu.SemaphoreType.DMA((8,))],
        compiler_params=_cparams(None, VMEM_LIMIT_V7X), name=name)(*ws, *ms, *vs, own, *landed)
    return [list(outs[q * npiece:(q + 1) * npiece]) for q in range(4)]


def _adamw_small(ws, ms, vs, gs, name):
    n = len(ws)

    def body(*refs):
        w_refs, m_refs, v_refs, g_refs = refs[0:n], refs[n:2 * n], refs[2 * n:3 * n], refs[3 * n:4 * n]
        outs = refs[4 * n:]
        for i in range(n):
            d, nm, nv = _adamw_math(w_refs[i][...], g_refs[i][...], m_refs[i][...], v_refs[i][...])
            outs[i][...] = d
            outs[n + i][...] = nm
            outs[2 * n + i][...] = nv

    outs = pl.pallas_call(
        body, out_shape=tuple(SDS(w.shape, f32) for _ in range(3) for w in ws), name=name)(*ws, *ms, *vs, *gs)
    return [list(outs[q * n:(q + 1) * n]) for q in range(3)]


WEIGHTS = ("ffn1_norm", "ffn1_w_gate", "ffn1_w_up", "ffn1_w_down", "mix_norm", "w_in", "q_norm", "k_norm",
           "attn_sinks", "rel_bias", "pool_w", "pool_scale", "w_out", "ffn2_norm", "ffn2_w_gate", "ffn2_w_up",
           "ffn2_w_down")
BIG = (("ffn1_w_gate", True), ("ffn1_w_up", True), ("ffn1_w_down", False), ("w_in", True), ("w_out", False),
       ("ffn2_w_gate", True), ("ffn2_w_up", True), ("ffn2_w_down", False))


def kernel(x, ffn1_norm, ffn1_w_gate, ffn1_w_up, ffn1_w_down, mix_norm, w_in, q_norm, k_norm, attn_sinks, rel_bias, pool_w, pool_scale, w_out, ffn2_norm, ffn2_w_gate, ffn2_w_up, ffn2_w_down, loss_target, m_ffn1_norm, m_ffn1_w_gate, m_ffn1_w_up, m_ffn1_w_down, m_mix_norm, m_w_in, m_q_norm, m_k_norm, m_attn_sinks, m_rel_bias, m_pool_w, m_pool_scale, m_w_out, m_ffn2_norm, m_ffn2_w_gate, m_ffn2_w_up, m_ffn2_w_down, v_ffn1_norm, v_ffn1_w_gate, v_ffn1_w_up, v_ffn1_w_down, v_mix_norm, v_w_in, v_q_norm, v_k_norm, v_attn_sinks, v_rel_bias, v_pool_w, v_pool_scale, v_w_out, v_ffn2_norm, v_ffn2_w_gate, v_ffn2_w_up, v_ffn2_w_down):
    args = dict(locals())
    w = {n: args[n] for n in WEIGHTS}
    m = {n: args["m_" + n] for n in WEIGHTS}
    v = {n: args["v_" + n] for n in WEIGHTS}

    as_rows = lambda a, tr: jnp.swapaxes(a, 1, 2) if tr else a
    shard = jnp.concatenate([as_rows(w[n], tr)[0].astype(bf16) for n, tr in BIG], axis=0)
    exchanges = _GatheredWeights(shard, x[0], ffn1_norm)
    gx, _, small = _local_step(
        x[0], loss_target[0], exchanges, ffn1_norm, mix_norm, ffn2_norm, q_norm, k_norm, attn_sinks,
        rel_bias, pool_w[0], pool_scale)

    small_tot = exchanges.small_tot
    dg1 = _all_reduce_row(small["ffn1_norm"], "all_reduce_gain")
    own_rest, landed_rest = exchanges.mix_ffn2_grads_parts(dg1)

    grads, deltas, new_m, new_v = {}, {}, {}, {}
    rest = [k for k in range(len(BIG)) if k not in G1_PIECES]
    rows_of = lambda t, ks: [as_rows(t[BIG[k][0]], BIG[k][1]) for k in ks]
    rest_out = _adamw_big(rest, rows_of(w, rest), rows_of(m, rest), rows_of(v, rest), own_rest, [landed_rest],
                          "adamw_rest")
    own1, landed1 = exchanges.ffn1_grads_parts(rest_out[0][0])
    ffn1 = list(G1_PIECES)
    ffn1_out = _adamw_big(ffn1, rows_of(w, ffn1), rows_of(m, ffn1), rows_of(v, ffn1), own1, [landed1], "adamw_ffn1")
    for ks, out in ((rest, rest_out), (ffn1, ffn1_out)):
        for i, k in enumerate(ks):
            n, tr = BIG[k]
            grads[n], deltas[n], new_m[n], new_v[n] = [as_rows(o[i], tr) for o in out]
    small_names = [n for n in SMALL_NAMES if n != "loss"]
    for n in small_names:
        grads[n] = dg1 if n == "ffn1_norm" else _unpack_small(small_tot, n)
    ds, nms, nvs = _adamw_small([w[n] for n in small_names], [m[n] for n in small_names], [v[n] for n in small_names],
                                [grads[n] for n in small_names], "adamw_small")
    for i, n in enumerate(small_names):
        deltas[n], new_m[n], new_v[n] = ds[i], nms[i], nvs[i]
    loss = small_tot[LOSS_ROW, 0]
    return (loss, gx[None], *[grads[n] for n in WEIGHTS], *[deltas[n] for n in WEIGHTS],
            *[new_m[n] for n in WEIGHTS], *[new_v[n] for n in WEIGHTS])
```

```python
import jax
import jax.numpy as jnp
import numpy as np
from jax import lax
from jax.experimental import pallas as pl
from jax.experimental.pallas import tpu as pltpu

f32, bf16, i32 = jnp.float32, jnp.bfloat16, jnp.int32
SDS = jax.ShapeDtypeStruct

D = 1024
F = 2816
HD = 64
NH = 8
NKV = 2
GQA = NH // NKV
DATTN = NH * HD
DKV = NKV * HD
DPOOL = 512
POOL_WINDOWS = (2, 4, 8, 16)
PGD = DPOOL // len(POOL_WINDOWS)
DIN = DATTN + 2 * DKV + DPOOL
DMIX = DATTN + DPOOL
BLK = 128
NBUCK = 32
MAX_DISTANCE = 128
EPS = 1e-6
NEG = -1e30
SCALE = HD ** -0.5

ADAM_LR, ADAM_B1, ADAM_B2, ADAM_EPS, ADAM_WD, ADAM_STEP = 0.001, 0.9, 0.999, 1e-08, 0.01, 10

NDEV = 8
FS = F // NDEV
INS = DIN // NDEV
OUTS = DMIX // NDEV
PIECE_ROWS = (FS, FS, FS, INS, OUTS, FS, FS, FS)
PIECE_OFF = tuple(int(v) for v in np.cumsum((0,) + PIECE_ROWS[:-1]))
PACK_ROWS = sum(PIECE_ROWS)

VMEM_LIMIT_V7X = 56 * 1024 * 1024

MESH = pl.DeviceIdType.MESH


def _cparams(sem=None, vmem=None):
    return pltpu.CompilerParams(dimension_semantics=sem, vmem_limit_bytes=vmem)


def _nt(a, b):
    return lax.dot_general(a, b, (((1,), (1,)), ((), ())), preferred_element_type=f32)


def _tn(a, b):
    return lax.dot_general(a, b, (((0,), (0,)), ((), ())), preferred_element_type=f32)


def _nn(a, b):
    return jnp.dot(a, b, preferred_element_type=f32)


def _sigmoid(x):
    return 1.0 / (1.0 + jnp.exp(-x))


def _norm_fwd(x, g, name):
    T = x.shape[0]
    tm = min(512, T)

    def body(x_ref, g_ref, h_ref):
        xv = x_ref[...]
        r = lax.rsqrt(jnp.mean(xv * xv, axis=-1, keepdims=True) + EPS)
        h_ref[...] = (xv * r * g_ref[...]).astype(bf16)

    return pl.pallas_call(
        body, grid=(T // tm,),
        in_specs=[pl.BlockSpec((tm, D), lambda i: (i, 0)), pl.BlockSpec((1, D), lambda i: (0, 0))],
        out_specs=pl.BlockSpec((tm, D), lambda i: (i, 0)),
        out_shape=SDS((T, D), bf16), name=name)(x, g)


def _norm_bwd(dh, x, g, dres, out_scale, name):
    T = x.shape[0]
    tm = min(512, T)

    def body(dh_ref, x_ref, g_ref, dr_ref, dx_ref, dxb_ref, dg_ref):
        i = pl.program_id(0)
        xv = x_ref[...]
        r = lax.rsqrt(jnp.mean(xv * xv, axis=-1, keepdims=True) + EPS)
        xh = xv * r
        dhv = dh_ref[...]
        dxh = dhv * g_ref[...]
        dx = dr_ref[...] + r * (dxh - xh * jnp.mean(dxh * xh, axis=-1, keepdims=True))
        dx_ref[...] = dx
        dxb_ref[...] = (out_scale * dx).astype(bf16)
        dg = jnp.sum(dhv * xh, axis=0, keepdims=True)

        @pl.when(i == 0)
        def _():
            dg_ref[...] = dg

        @pl.when(i > 0)
        def _():
            dg_ref[...] += dg

    tok = pl.BlockSpec((tm, D), lambda i: (i, 0))
    vec = pl.BlockSpec((1, D), lambda i: (0, 0))
    return pl.pallas_call(
        body, grid=(T // tm,),
        in_specs=[tok, tok, vec, tok], out_specs=[tok, tok, vec],
        out_shape=(SDS((T, D), f32), SDS((T, D), bf16), SDS((1, D), f32)),
        compiler_params=_cparams(("arbitrary",)), name=name)(dh, x, g, dres)


FFN_ROW_CHUNK = 256


def _ffn_tiles(T):
    return min(1024, T), 256


def _ffn_fwd(h, w, x, target, next_gain, name):
    T = h.shape[0]
    tm, tf = _ffn_tiles(T)
    nf = F // tf
    with_loss = target is not None
    assert with_loss != (next_gain is not None)

    def body(*refs):
        if with_loss:
            h_ref, w_ref, x_hbm, t_hbm, xo_ref, g_ref, u_ref, dyb_ref, loss_ref, tbuf, sem = refs
        else:
            h_ref, w_ref, x_hbm, gain_ref, xo_ref, g_ref, u_ref, hn_ref, sem = refs
        fi = pl.program_id(0)

        @pl.when(fi == 0)
        def _():
            cp = pltpu.make_async_copy(x_hbm, xo_ref, sem)
            cp.start()
            cp.wait()

        wgu = w_ref[0:2].reshape(2 * tf, D)
        for r in range(0, T, tm):
            rows = slice(r, r + tm)
            gu = _nt(h_ref[rows, :], wgu)
            gate, up = gu[:, :tf], gu[:, tf:]
            act = gate * _sigmoid(gate) * up
            g_ref[0, rows, :] = gate.astype(bf16)
            u_ref[0, rows, :] = up.astype(bf16)
            xo_ref[rows, :] += _nn((0.5 * act).astype(bf16), w_ref[2])

        if with_loss:
            @pl.when(fi == nf - 1)
            def _():
                lanes = jnp.zeros((1, 128), f32)
                for r in range(0, T, tm):
                    rows = slice(r, r + tm)
                    cp = pltpu.make_async_copy(t_hbm.at[pl.ds(r, tm), :], tbuf, sem)
                    cp.start()
                    cp.wait()
                    e = xo_ref[rows, :] - tbuf[...]
                    dy = e * (1.0 / D)
                    xo_ref[rows, :] = dy
                    dyb_ref[rows, :] = (0.5 * dy).astype(bf16)
                    col = jnp.sum(e * e, axis=0, keepdims=True) * (0.5 / D)
                    for k in range(D // 128):
                        lanes = lanes + col[:, 128 * k:128 * (k + 1)]
                loss_ref[...] = lanes
        else:
            @pl.when(fi == nf - 1)
            def _():
                for r in range(0, T, FFN_ROW_CHUNK):
                    rows = slice(r, r + FFN_ROW_CHUNK)
                    xv = xo_ref[rows, :]
                    rstd = lax.rsqrt(jnp.mean(xv * xv, axis=-1, keepdims=True) + EPS)
                    hn_ref[rows, :] = (xv * rstd * gain_ref[...]).astype(bf16)

    tok = pl.BlockSpec((T, D), lambda f: (0, 0))
    act_spec = pl.BlockSpec((1, T, tf), lambda f: (f, 0, 0))
    hbm = pl.BlockSpec(memory_space=pl.ANY)
    in_specs = [tok, pl.BlockSpec((3, tf, D), lambda f: (0, f, 0)), hbm]
    out_specs = [tok, act_spec, act_spec]
    out_shape = [SDS((T, D), f32), SDS((nf, T, tf), bf16), SDS((nf, T, tf), bf16)]
    scratch = [pltpu.SemaphoreType.DMA]
    args = [h, w, x]
    if with_loss:
        in_specs.append(hbm)
        args.append(target)
        out_specs += [tok, pl.BlockSpec((1, 128), lambda f: (0, 0))]
        out_shape += [SDS((T, D), bf16), SDS((1, 128), f32)]
        scratch = [pltpu.VMEM((tm, D), f32)] + scratch
    else:
        in_specs.append(pl.BlockSpec((1, D), lambda f: (0, 0)))
        args.append(next_gain)
        out_specs.append(tok)
        out_shape.append(SDS((T, D), bf16))
    return pl.pallas_call(
        body, grid=(nf,), in_specs=in_specs, out_specs=out_specs, out_shape=tuple(out_shape), scratch_shapes=scratch,
        compiler_params=_cparams(("arbitrary",), VMEM_LIMIT_V7X), name=name)(*args)


def _ffn_bwd(dob, h, gate, up, w, norm, name):
    x, g, dres = norm
    T = h.shape[0]
    _, tf = _ffn_tiles(T)
    nf = F // tf
    tm = min(512, T)
    nchunk = T // tm

    def body(do_hbm, h_hbm, g_ref, u_ref, w_ref, x_hbm, gain_ref, dr_hbm, dx_hbm, dxb_hbm, dg_ref, dw_ref,
             do_v, h_v, dh_acc, dgu_s, act_s, xbuf, rbuf, obuf, obb, sems, in_sems, out_sems):
        fi = pl.program_id(0)

        @pl.when(fi == 0)
        def _():
            loads = [pltpu.make_async_copy(do_hbm, do_v, sems.at[0]), pltpu.make_async_copy(h_hbm, h_v, sems.at[1])]
            for cp in loads:
                cp.start()
            dh_acc[...] = jnp.zeros_like(dh_acc)
            for cp in loads:
                cp.wait()

        wgu = w_ref[0:2].reshape(2 * tf, D)
        for r in range(0, T, FFN_ROW_CHUNK):
            rows = slice(r, r + FFN_ROW_CHUNK)
            dov = do_v[rows, :]
            gv = g_ref[0, rows, :].astype(f32)
            uv = u_ref[0, rows, :].astype(f32)
            sg = _sigmoid(gv)
            sil = gv * sg
            dact = _nt(dov, w_ref[2])
            dup = dact * sil
            dgate = dact * uv * (sg * (1.0 + gv * (1.0 - sg)))
            dgu = jnp.concatenate([dgate.astype(bf16), dup.astype(bf16)], axis=1)
            dgu_s[rows, :] = dgu
            act_s[rows, :] = (sil * uv).astype(bf16)
            dh_acc[rows, :] += _nn(dgu, wgu)
        dw_ref[0:2] = _tn(dgu_s[...], h_v[...]).reshape(2, tf, D).astype(bf16)
        dw_ref[2] = _tn(act_s[...], do_v[...]).astype(bf16)

        @pl.when(fi == nf - 1)
        def _():
            def loads(i):
                s, rows = i % 2, pl.ds(i * tm, tm)
                return [pltpu.make_async_copy(x_hbm.at[rows, :], xbuf.at[s], in_sems.at[2 * s]),
                        pltpu.make_async_copy(dr_hbm.at[rows, :], rbuf.at[s], in_sems.at[2 * s + 1])]

            def stores(i):
                s, rows = i % 2, pl.ds(i * tm, tm)
                return [pltpu.make_async_copy(obuf.at[s], dx_hbm.at[rows, :], out_sems.at[2 * s]),
                        pltpu.make_async_copy(obb.at[s], dxb_hbm.at[rows, :], out_sems.at[2 * s + 1])]

            for cp in loads(0):
                cp.start()
            dg = jnp.zeros((1, D), f32)
            for i in range(nchunk):
                s = i % 2
                if i + 1 < nchunk:
                    for cp in loads(i + 1):
                        cp.start()
                for cp in loads(i):
                    cp.wait()
                if i >= 2:
                    for cp in stores(i - 2):
                        cp.wait()
                xv = xbuf[s]
                rstd = lax.rsqrt(jnp.mean(xv * xv, axis=-1, keepdims=True) + EPS)
                xh = xv * rstd
                dhv = dh_acc[i * tm:(i + 1) * tm, :]
                dxh = dhv * gain_ref[...]
                dx = rbuf[s] + rstd * (dxh - xh * jnp.mean(dxh * xh, axis=-1, keepdims=True))
                obuf[s] = dx
                obb[s] = dx.astype(bf16)
                dg = dg + jnp.sum(dhv * xh, axis=0, keepdims=True)
                for cp in stores(i):
                    cp.start()
            dg_ref[...] = dg
            for i in range(max(nchunk - 2, 0), nchunk):
                for cp in stores(i):
                    cp.wait()

    act_spec = pl.BlockSpec((1, T, tf), lambda f: (f, 0, 0))
    wspec = pl.BlockSpec((3, tf, D), lambda f: (0, f, 0))
    vec = pl.BlockSpec((1, D), lambda f: (0, 0))
    hbm = pl.BlockSpec(memory_space=pl.ANY)
    return pl.pallas_call(
        body, grid=(nf,),
        in_specs=[hbm, hbm, act_spec, act_spec, wspec, hbm, vec, hbm],
        out_specs=[hbm, hbm, vec, wspec],
        out_shape=(SDS((T, D), f32), SDS((T, D), bf16), SDS((1, D), f32), SDS((3, F, D), bf16)),
        scratch_shapes=[pltpu.VMEM((T, D), bf16), pltpu.VMEM((T, D), bf16), pltpu.VMEM((T, D), f32),
                        pltpu.VMEM((T, 2 * tf), bf16), pltpu.VMEM((T, tf), bf16),
                        pltpu.VMEM((2, tm, D), f32), pltpu.VMEM((2, tm, D), f32), pltpu.VMEM((2, tm, D), f32),
                        pltpu.VMEM((2, tm, D), bf16),
                        pltpu.SemaphoreType.DMA((2,)), pltpu.SemaphoreType.DMA((4,)), pltpu.SemaphoreType.DMA((4,))],
        compiler_params=_cparams(("arbitrary",), VMEM_LIMIT_V7X), name=name)(dob, h, gate, up, w, x, g, dres)


def _in_proj_fwd(h, wint, name):
    T = h.shape[0]
    tm = min(512, T)

    def body(h_ref, w_ref, z_ref):
        z_ref[...] = _nt(h_ref[...], w_ref[...])

    return pl.pallas_call(
        body, grid=(T // tm,),
        in_specs=[pl.BlockSpec((tm, D), lambda i: (i, 0)), pl.BlockSpec((DIN, D), lambda i: (0, 0))],
        out_specs=pl.BlockSpec((tm, DIN), lambda i: (i, 0)),
        out_shape=SDS((T, DIN), f32), name=name)(h, wint)


def _in_proj_bwd(dz, wint, h, name):
    T = h.shape[0]
    tm = min(512, T)
    nt = T // tm

    def body(dz_ref, w_ref, h_ref, dh_ref, dw_ref, acc):
        i = pl.program_id(0)
        dzb = dz_ref[...].astype(bf16)
        dh_ref[...] = _nn(dzb, w_ref[...])
        part = _tn(dzb, h_ref[...])

        @pl.when(i == 0)
        def _():
            acc[...] = part

        @pl.when(i > 0)
        def _():
            acc[...] += part

        @pl.when(i == nt - 1)
        def _():
            dw_ref[...] = acc[...].astype(bf16)

    wspec = pl.BlockSpec((DIN, D), lambda i: (0, 0))
    return pl.pallas_call(
        body, grid=(nt,),
        in_specs=[pl.BlockSpec((tm, DIN), lambda i: (i, 0)), wspec, pl.BlockSpec((tm, D), lambda i: (i, 0))],
        out_specs=[pl.BlockSpec((tm, D), lambda i: (i, 0)), wspec],
        out_shape=(SDS((T, D), f32), SDS((DIN, D), bf16)),
        scratch_shapes=[pltpu.VMEM((DIN, D), f32)],
        compiler_params=_cparams(("arbitrary",)), name=name)(dz, wint, h)


def _out_proj_fwd(ymix, wout, x, g, name):
    T = x.shape[0]
    tm = min(512, T)

    def body(y_ref, w_ref, x_ref, g_ref, o_ref, h_ref):
        o = x_ref[...] + _nn(y_ref[...], w_ref[...])
        o_ref[...] = o
        r = lax.rsqrt(jnp.mean(o * o, axis=-1, keepdims=True) + EPS)
        h_ref[...] = (o * r * g_ref[...]).astype(bf16)

    tok = pl.BlockSpec((tm, D), lambda i: (i, 0))
    return pl.pallas_call(
        body, grid=(T // tm,),
        in_specs=[pl.BlockSpec((tm, DMIX), lambda i: (i, 0)), pl.BlockSpec((DMIX, D), lambda i: (0, 0)), tok,
                  pl.BlockSpec((1, D), lambda i: (0, 0))],
        out_specs=[tok, tok], out_shape=(SDS((T, D), f32), SDS((T, D), bf16)), name=name)(ymix, wout, x, g)


def _out_proj_bwd(dxb, wout, ymix, name):
    T = dxb.shape[0]
    tm = min(512, T)
    nt = T // tm

    def body(dx_ref, w_ref, y_ref, dy_ref, dw_ref, acc):
        i = pl.program_id(0)
        dxv = dx_ref[...]
        dy_ref[...] = _nt(dxv, w_ref[...])
        part = _tn(y_ref[...], dxv)

        @pl.when(i == 0)
        def _():
            acc[...] = part

        @pl.when(i > 0)
        def _():
            acc[...] += part

        @pl.when(i == nt - 1)
        def _():
            dw_ref[...] = acc[...].astype(bf16)

    wspec = pl.BlockSpec((DMIX, D), lambda i: (0, 0))
    return pl.pallas_call(
        body, grid=(nt,),
        in_specs=[pl.BlockSpec((tm, D), lambda i: (i, 0)), wspec, pl.BlockSpec((tm, DMIX), lambda i: (i, 0))],
        out_specs=[pl.BlockSpec((tm, DMIX), lambda i: (i, 0)), wspec],
        out_shape=(SDS((T, DMIX), f32), SDS((DMIX, D), bf16)),
        scratch_shapes=[pltpu.VMEM((DMIX, D), f32)],
        compiler_params=_cparams(("arbitrary",)), name=name)(dxb, wout, ymix)


def _t5_bucket_table():
    ql = np.arange(BLK)[:, None]
    kl = np.arange(2 * BLK)[None, :]
    n = np.maximum(ql + BLK - kl, 0)
    max_exact = NBUCK // 2
    large = max_exact + (np.log(np.maximum(n, 1) / max_exact) / np.log(MAX_DISTANCE / max_exact)
                         * (NBUCK - max_exact)).astype(np.int32)
    large = np.minimum(large, NBUCK - 1)
    return np.where(n < max_exact, n, large).astype(np.int32)


def _fill_bias(bk_ref, rb_ref, bias_scr):
    bk = bk_ref[...]
    for h in range(NH):
        def step(b, acc, h=h):
            return acc + jnp.where(bk == b, rb_ref[b, h], 0.0)
        bias_scr[h] = lax.fori_loop(0, NBUCK, step, jnp.zeros((BLK, 2 * BLK), f32))


MIX_SUB = 4


class _Window:
    def __init__(self, zc_ref, zp_ref, n, s):
        self.blk = n * MIX_SUB + s
        self.cur = lambda a, b: zc_ref[s * BLK:(s + 1) * BLK, a:b]
        self.prev = (lambda a, b: zp_ref[:, a:b]) if s == 0 else (lambda a, b: zc_ref[(s - 1) * BLK:s * BLK, a:b])


def _attn_qkv(win, kh, qg, kg):
    kc = DATTN + HD * kh
    vc = DATTN + DKV + HD * kh
    kx = jnp.concatenate([win.prev(kc, kc + HD), win.cur(kc, kc + HD)], axis=0)
    vx = jnp.concatenate([win.prev(vc, vc + HD), win.cur(vc, vc + HD)], axis=0)
    qx = jnp.concatenate([win.cur(HD * (GQA * kh + g), HD * (GQA * kh + g + 1)) for g in range(GQA)], axis=0)
    rq = lax.rsqrt(jnp.mean(qx * qx, axis=-1, keepdims=True) + EPS)
    rk = lax.rsqrt(jnp.mean(kx * kx, axis=-1, keepdims=True) + EPS)
    qhat, khat = qx * rq, kx * rk
    return dict(qhat=qhat, khat=khat, rq=rq, rk=rk, qsb=(qhat * (qg * SCALE)).astype(bf16),
                knb=(khat * kg).astype(bf16), vb=vx.astype(bf16))


def _window_masks(n):
    row = lax.broadcasted_iota(i32, (GQA * BLK, 2 * BLK), 0) & (BLK - 1)
    col = lax.broadcasted_iota(i32, (GQA * BLK, 2 * BLK), 1)
    band = (col > row) & (col <= row + BLK)
    return band & ((col >= BLK) | (n > 0)), band


def _attn_probs(a, kh, sk_ref, bias_scr, mask):
    s = _nt(a["qsb"], a["knb"]) + bias_scr[GQA * kh:GQA * (kh + 1)].reshape(GQA * BLK, 2 * BLK)
    s = jnp.where(mask, s, NEG)
    ridx = lax.broadcasted_iota(i32, (GQA * BLK, 1), 0)
    sink = jnp.full((GQA * BLK, 1), sk_ref[GQA * kh + GQA - 1], f32)
    for g in range(GQA - 2, -1, -1):
        sink = jnp.where(ridx < (g + 1) * BLK, sk_ref[GQA * kh + g], sink)
    m = jnp.maximum(jnp.max(s, axis=-1, keepdims=True), sink)
    e = jnp.exp(s - m)
    den = jnp.sum(e, axis=-1, keepdims=True) + jnp.exp(sink - m)
    return e / den


POOL_STEPS = {2: (1,), 4: (1, 2), 8: (1, 2, 4), 16: (1, 2, 4, 8)}


def _pool_group(win, g, w):
    n = win.blk
    c0 = DATTN + 2 * DKV + PGD * g
    uc = win.cur(c0, c0 + PGD)
    up = jnp.where(n > 0, win.prev(c0, c0 + PGD), 0.0)
    sm = jnp.concatenate([up, uc], axis=0)
    for k in POOL_STEPS[w]:
        sm = sm + pltpu.roll(sm, k, axis=0)
    pos = n * BLK + lax.broadcasted_iota(i32, (BLK, 1), 0) + 1
    cnt = jnp.minimum(pos, w).astype(f32)
    return sm[BLK:2 * BLK] / cnt - uc, cnt


def _mix_fwd(z, qg, kg, sinks, relb, bucket, pool_w, pscale, name):
    T = z.shape[0]
    step_rows = MIX_SUB * BLK
    nsteps = T // step_rows

    def body(zc_ref, zp_ref, qg_ref, kg_ref, sk_ref, rb_ref, bk_ref, pw_ref, ps_ref, y_ref, p_ref, bias_scr, yacc):
        n = pl.program_id(0)

        @pl.when(n == 0)
        def _():
            _fill_bias(bk_ref, rb_ref, bias_scr)

        first_mask, mask = _window_masks(n)
        for s in range(MIX_SUB):
            win = _Window(zc_ref, zp_ref, n, s)
            rows = slice(s * BLK, (s + 1) * BLK)
            for kh in range(NKV):
                a = _attn_qkv(win, kh, qg_ref[...], kg_ref[...])
                pb = _attn_probs(a, kh, sk_ref, bias_scr, first_mask if s == 0 else mask).astype(bf16)
                p_ref[s, GQA * kh:GQA * (kh + 1)] = pb.reshape(GQA, BLK, 2 * BLK)
                o = _nn(pb, a["vb"])
                for g in range(GQA):
                    hc = HD * (GQA * kh + g)
                    yacc[rows, hc:hc + HD] = o[g * BLK:(g + 1) * BLK]
            for g, w in enumerate(POOL_WINDOWS):
                pooled, _ = _pool_group(win, g, w)
                yp = _nn(pooled.astype(bf16), pw_ref[g].astype(bf16)) * ps_ref[:, PGD * g:PGD * (g + 1)]
                yacc[rows, DATTN + PGD * g:DATTN + PGD * (g + 1)] = yp
        y_ref[...] = yacc[...].astype(bf16)

    full = lambda *shape: pl.BlockSpec(shape, lambda n: (0,) * len(shape))
    smem = pl.BlockSpec(memory_space=pltpu.SMEM)
    return pl.pallas_call(
        body, grid=(nsteps,),
        in_specs=[pl.BlockSpec((step_rows, DIN), lambda n: (n, 0)),
                  pl.BlockSpec((BLK, DIN), lambda n: (jnp.maximum(n * MIX_SUB - 1, 0), 0)),
                  full(1, HD), full(1, HD), smem, smem, full(BLK, 2 * BLK),
                  full(len(POOL_WINDOWS), PGD, PGD), full(1, DPOOL)],
        out_specs=[pl.BlockSpec((step_rows, DMIX), lambda n: (n, 0)),
                   pl.BlockSpec((MIX_SUB, NH, BLK, 2 * BLK), lambda n: (n, 0, 0, 0))],
        out_shape=(SDS((T, DMIX), bf16), SDS((T // BLK, NH, BLK, 2 * BLK), bf16)),
        scratch_shapes=[pltpu.VMEM((NH, BLK, 2 * BLK), f32), pltpu.VMEM((step_rows, DMIX), f32)],
        compiler_params=_cparams(("arbitrary",)), name=name)(z, z, qg, kg, sinks, relb, bucket, pool_w, pscale)


def _mix_bwd(z, dy, probs, qg, kg, relb, bucket, pool_w, pscale, name):
    T = z.shape[0]
    step_rows = MIX_SUB * BLK
    nsteps = T // step_rows

    def body(zc_ref, zp_ref, dy_ref, p_ref, qg_ref, kg_ref, bk_ref, pw_ref, ps_ref,
             dz_ref, dqg_ref, dkg_ref, dsk_ref, drb_ref, dpw_ref, dps_ref, dbias_scr):
        n = pl.program_id(0)

        @pl.when(n == 0)
        def _():
            dbias_scr[...] = jnp.zeros_like(dbias_scr)
            dqg_ref[...] = jnp.zeros_like(dqg_ref)
            dkg_ref[...] = jnp.zeros_like(dkg_ref)
            dpw_ref[...] = jnp.zeros_like(dpw_ref)
            dps_ref[...] = jnp.zeros_like(dps_ref)

        qg, kg = qg_ref[...], kg_ref[...]
        for s in range(MIX_SUB):
            win = _Window(zc_ref, zp_ref, n, s)
            blk = win.blk
            rows = pl.ds(pl.multiple_of(blk * BLK, BLK), BLK)
            prow = pl.ds(pl.multiple_of(jnp.maximum(blk - 1, 0) * BLK, BLK), BLK)
            dyr = slice(s * BLK, (s + 1) * BLK)

            def into_prev(fn, s=s):
                if s == 0:
                    pl.when(n > 0)(fn)
                else:
                    fn()

            for kh in range(NKV):
                a = _attn_qkv(win, kh, qg, kg)
                pb = p_ref[s, GQA * kh:GQA * (kh + 1)].reshape(GQA * BLK, 2 * BLK)
                p = pb.astype(f32)
                do = jnp.concatenate([dy_ref[dyr, HD * (GQA * kh + g):HD * (GQA * kh + g + 1)] for g in range(GQA)],
                                     axis=0).astype(bf16)
                dv = _tn(pb, do)
                dp = _nt(do, a["vb"])
                delta = jnp.sum(p * dp, axis=-1, keepdims=True)
                ds = p * (dp - delta)
                for g in range(GQA):
                    dbias_scr[GQA * kh + g] += ds[g * BLK:(g + 1) * BLK]
                dsb = ds.astype(bf16)
                dqn = _nn(dsb, a["knb"]) * SCALE
                dkn = _tn(dsb, a["qsb"])
                qhat, khat = a["qhat"], a["khat"]
                dqg_ref[...] += jnp.sum(dqn * qhat, axis=0, keepdims=True)
                dkg_ref[...] += jnp.sum(dkn * khat, axis=0, keepdims=True)
                dqh = dqn * qg
                dq = a["rq"] * (dqh - qhat * jnp.mean(dqh * qhat, axis=-1, keepdims=True))
                dkh = dkn * kg
                dk = a["rk"] * (dkh - khat * jnp.mean(dkh * khat, axis=-1, keepdims=True))
                kc = DATTN + HD * kh
                vc = DATTN + DKV + HD * kh
                for g in range(GQA):
                    hc = HD * (GQA * kh + g)
                    dz_ref[rows, hc:hc + HD] = dq[g * BLK:(g + 1) * BLK]
                dz_ref[rows, kc:kc + HD] = dk[BLK:2 * BLK]
                dz_ref[rows, vc:vc + HD] = dv[BLK:2 * BLK]

                def kv_prev(dk=dk, dv=dv, kc=kc, vc=vc, prow=prow):
                    dz_ref[prow, kc:kc + HD] += dk[0:BLK]
                    dz_ref[prow, vc:vc + HD] += dv[0:BLK]

                into_prev(kv_prev)

            for g, w in enumerate(POOL_WINDOWS):
                c0 = DATTN + 2 * DKV + PGD * g
                pooled, cnt = _pool_group(win, g, w)
                pb = pooled.astype(bf16)
                wb = pw_ref[g].astype(bf16)
                dyp = dy_ref[dyr, DATTN + PGD * g:DATTN + PGD * (g + 1)]
                ypre = _nn(pb, wb)
                dps_ref[:, PGD * g:PGD * (g + 1)] += jnp.sum(dyp * ypre, axis=0, keepdims=True)
                dyg = (dyp * ps_ref[:, PGD * g:PGD * (g + 1)]).astype(bf16)
                dpw_ref[g] += _tn(pb, dyg)
                dpooled = _nt(dyg, wb)
                due = jnp.concatenate([jnp.zeros((BLK, PGD), f32), dpooled / cnt], axis=0)
                for k in POOL_STEPS[w]:
                    due = due + pltpu.roll(due, 2 * BLK - k, axis=0)
                dz_ref[rows, c0:c0 + PGD] = due[BLK:2 * BLK] - dpooled

                def pool_prev(due=due, c0=c0, prow=prow):
                    dz_ref[prow, c0:c0 + PGD] += due[0:BLK]

                into_prev(pool_prev)

        @pl.when(n == nsteps - 1)
        def _():
            bk = bk_ref[...]
            ri = lax.broadcasted_iota(i32, (NBUCK, NH), 0)
            ci = lax.broadcasted_iota(i32, (NBUCK, NH), 1)

            def step(b, acc):
                for h in range(NH):
                    sel = jnp.where(bk == b, dbias_scr[h], 0.0)
                    tot = jnp.sum(jnp.sum(sel, axis=1, keepdims=True), axis=0, keepdims=True)
                    acc = acc + jnp.where((ri == b) & (ci == h), tot, 0.0)
                return acc

            drb_ref[...] = lax.fori_loop(0, NBUCK, step, jnp.zeros((NBUCK, NH), f32))
            lane = lax.broadcasted_iota(i32, (1, 128), 1)
            dsk = jnp.zeros((1, 128), f32)
            for h in range(NH):
                tot = jnp.sum(jnp.sum(dbias_scr[h], axis=1, keepdims=True), axis=0, keepdims=True)
                dsk = dsk - jnp.where(lane == h, tot, 0.0)
            dsk_ref[...] = dsk

    full = lambda *shape: pl.BlockSpec(shape, lambda n: (0,) * len(shape))
    npg = len(POOL_WINDOWS)
    return pl.pallas_call(
        body, grid=(nsteps,),
        in_specs=[pl.BlockSpec((step_rows, DIN), lambda n: (n, 0)),
                  pl.BlockSpec((BLK, DIN), lambda n: (jnp.maximum(n * MIX_SUB - 1, 0), 0)),
                  pl.BlockSpec((step_rows, DMIX), lambda n: (n, 0)),
                  pl.BlockSpec((MIX_SUB, NH, BLK, 2 * BLK), lambda n: (n, 0, 0, 0)),
                  full(1, HD), full(1, HD), full(BLK, 2 * BLK), full(npg, PGD, PGD), full(1, DPOOL)],
        out_specs=[full(T, DIN), full(1, HD), full(1, HD), full(1, 128), full(NBUCK, NH),
                   full(npg, PGD, PGD), full(1, DPOOL)],
        out_shape=(SDS((T, DIN), f32), SDS((1, HD), f32), SDS((1, HD), f32), SDS((1, 128), f32),
                   SDS((NBUCK, NH), f32), SDS((npg, PGD, PGD), f32), SDS((1, DPOOL), f32)),
        scratch_shapes=[pltpu.VMEM((NH, BLK, 2 * BLK), f32)],
        compiler_params=_cparams(("arbitrary",), VMEM_LIMIT_V7X),
        name=name)(z, z, dy, probs, qg, kg, bucket, pool_w, pscale)


class _LocalWeights:
    def __init__(self, w1, wint, wout, w2):
        self.w1, self.wint, self.wout, self.w2 = w1, wint, wout, w2

    def ffn1(self):
        return self.w1

    def first_norm(self, x, gain):
        return _norm_fwd(x, gain, "norm1_fwd")

    def after_ffn1(self, gain, x1):
        return gain

    def mix(self, after):
        return self.wint, self.wout

    def before_out_proj(self, wout, after):
        return wout

    def ffn2(self, after):
        return self.w2

    def mix_ffn2_grads_ready(self, dwint, dwout, dw2, dh2):
        self.grads_rest = (dwint, dwout, dw2)
        return dh2

    def before_ffn1_bwd(self, dx1b):
        return dx1b


def _local_step(x, target, weights, g1, gm, g3, qg, kg, sinks, relb, pool_w, pscale):
    bucket = jnp.asarray(_t5_bucket_table())
    sk = sinks.reshape(NH)
    w1 = weights.ffn1()
    h1 = weights.first_norm(x, g1)
    x1, gate1, up1, h2 = _ffn_fwd(h1, w1, x, None, gm, "ffn1_fwd")
    gm = weights.after_ffn1(gm, x1)
    wint, wout = weights.mix(h2)
    z = _in_proj_fwd(h2, wint, "in_proj_fwd")
    ymix, probs = _mix_fwd(z, qg, kg, sk, relb, bucket, pool_w, pscale, "mix_fwd")
    wout = weights.before_out_proj(wout, ymix)
    x2, h3 = _out_proj_fwd(ymix, wout, x1, g3, "out_proj_fwd")
    w2 = weights.ffn2(h3)
    dy, gate2, up2, dyb, loss_lanes = _ffn_fwd(h3, w2, x2, target, None, "ffn2_fwd")

    dx2, dx2b, dg3, dw2 = _ffn_bwd(dyb, h3, gate2, up2, w2, (x2, g3, dy), "ffn2_bwd")
    dymix, dwout = _out_proj_bwd(dx2b, wout, ymix, "out_proj_bwd")
    dz, dqg, dkg, dsk, drb, dpw, dps = _mix_bwd(z, dymix, probs, qg, kg, relb, bucket, pool_w, pscale, "mix_bwd")
    dh2, dwint = _in_proj_bwd(dz, wint, h2, "in_proj_bwd")
    dh2 = weights.mix_ffn2_grads_ready(dwint, dwout, dw2, dh2)
    dx1, dx1b, dgm = _norm_bwd(dh2, x1, gm, dx2, 0.5, "norm2_bwd")
    dx1b = weights.before_ffn1_bwd(dx1b)
    gx, _, dg1, dw1 = _ffn_bwd(dx1b, h1, gate1, up1, w1, (x, g1, dx1), "ffn1_bwd")
    small = dict(ffn1_norm=dg1, mix_norm=dgm, ffn2_norm=dg3, pool_scale=dps, q_norm=dqg, k_norm=dkg,
                 attn_sinks=dsk[:, :NH], rel_bias=drb, pool_w=dpw, loss=loss_lanes)
    return gx, (dw1, dwint, dwout, dw2), small


SMALL_NAMES = ("ffn1_norm", "mix_norm", "ffn2_norm", "pool_scale", "q_norm", "k_norm", "attn_sinks", "rel_bias",
               "pool_w", "loss")
SMALL_SHAPES = dict(ffn1_norm=(1, D), mix_norm=(1, D), ffn2_norm=(1, D), pool_scale=(1, DPOOL), q_norm=(1, HD),
                    k_norm=(1, HD), attn_sinks=(1, NH), rel_bias=(NBUCK, NH),
                    pool_w=(1, len(POOL_WINDOWS), PGD, PGD), loss=(1, 128))


def _small_rows(name):
    return -(-int(np.prod(SMALL_SHAPES[name])) // 128)


SMALL_OFF = {}
_r = 0
for _n in SMALL_NAMES:
    SMALL_OFF[_n] = _r
    _r += _small_rows(_n)
SMALL_ROWS = -(-_r // 8) * 8
LOSS_ROW = SMALL_OFF["loss"]


def _pack_small(vals):
    parts = []
    for n in SMALL_NAMES:
        size = _small_rows(n) * 128
        if n in vals:
            flat = vals[n].astype(f32).reshape(-1)
            parts.append(jnp.pad(flat, (0, size - flat.shape[0])))
        else:
            parts.append(jnp.zeros((size,), f32))
    flat = jnp.concatenate(parts)
    flat = jnp.pad(flat, (0, SMALL_ROWS * 128 - flat.shape[0]))
    return flat.reshape(SMALL_ROWS, 128)


def _unpack_small(packed, name):
    size = int(np.prod(SMALL_SHAPES[name]))
    r0 = SMALL_OFF[name]
    return packed[r0:r0 + _small_rows(name)].reshape(-1)[:size].reshape(SMALL_SHAPES[name])


def _position():
    return lax.axis_index("x"), lax.axis_index("y"), lax.axis_index("c")


def _dev_index(x, y, c):
    return 4 * x + 2 * y + c


G1_PIECES, MIX_PIECES, F2_PIECES = (0, 1, 2), (3, 4), (5, 6, 7)


def _group_rows(pieces):
    return sum(PIECE_ROWS[k] for k in pieces)


def _shard_piece(s_ref, k):
    return s_ref.at[pl.ds(PIECE_OFF[k], PIECE_ROWS[k]), :]


def _shard_group(s_ref, pieces):
    return s_ref.at[pl.ds(PIECE_OFF[pieces[0]], _group_rows(pieces)), :]


def _weight_pieces(w1_ref=None, wi_ref=None, wo_ref=None, w2_ref=None):
    arrs = {}
    if w1_ref is not None:
        arrs.update({0: w1_ref.at[0], 1: w1_ref.at[1], 2: w1_ref.at[2]})
    if wi_ref is not None:
        arrs[3] = wi_ref
    if wo_ref is not None:
        arrs[4] = wo_ref
    if w2_ref is not None:
        arrs.update({5: w2_ref.at[0], 6: w2_ref.at[1], 7: w2_ref.at[2]})
    return arrs


def _block_rows(arrs, k, dev):
    r = PIECE_ROWS[k]
    return arrs[k].at[pl.ds(pl.multiple_of(_dev_index(*dev) * r, 16), r), :]


NORM_ROWS = 512


def _all_gather_ffn1(shard, x, gain):
    pieces = G1_PIECES
    rest_pieces = MIX_PIECES + F2_PIECES
    half = FS // 2
    T = x.shape[0]
    SIB, X0, X1, Y0, Y1, RELAY_Y, RELAY_X, ON_X, ON_Y, ON_D0, ON_D1 = range(11)

    def body(s_ref, x_ref, g_ref, w1_ref, h_ref, wi_ref, wo_ref, w2_ref, xbuf, hbuf, rest_buf,
             send_sems, recv_sems, local_sem, norm_sems):
        x, y, c = _position()
        me, sib = (x, y, c), (x, y, 1 - c)
        xn, yn, dg = (1 - x, y, c), (x, 1 - y, c), (1 - x, 1 - y, c)
        arrs = _weight_pieces(w1_ref=w1_ref)

        def place_rest():
            rest = _weight_pieces(wi_ref=wi_ref, wo_ref=wo_ref, w2_ref=w2_ref)
            grp = _shard_group(s_ref, rest_pieces)
            load = pltpu.make_async_copy(grp, rest_buf, norm_sems.at[0])
            load.start()
            load.wait()
            base = PIECE_OFF[rest_pieces[0]]
            for k in rest_pieces:
                pltpu.make_async_copy(rest_buf.at[pl.ds(PIECE_OFF[k] - base, PIECE_ROWS[k]), :],
                                      _block_rows(rest, k, me), norm_sems.at[1]).start()
            pltpu.make_async_copy(grp, rest_buf, norm_sems.at[1]).wait()

        def first_norm():
            for r in range(0, T, NORM_ROWS):
                load = pltpu.make_async_copy(x_ref.at[pl.ds(r, NORM_ROWS), :], xbuf, norm_sems.at[0])
                load.start()
                load.wait()
                xv = xbuf[...]
                rs = lax.rsqrt(jnp.mean(xv * xv, axis=-1, keepdims=True) + EPS)
                hbuf[...] = (xv * rs * g_ref[...]).astype(bf16)
                store = pltpu.make_async_copy(hbuf, h_ref.at[pl.ds(r, NORM_ROWS), :], norm_sems.at[1])
                store.start()
                store.wait()

        def rows_of(k, block, hf):
            r = PIECE_ROWS[k]
            start, size = (0, r) if hf is None else (hf * half, half)
            return arrs[k].at[pl.ds(pl.multiple_of(_dev_index(*block) * r + start, 16), size), :]

        def copies(rel, block, hf, to, from_shard=False):
            def src(k):
                if not from_shard:
                    return rows_of(k, block, hf)
                start, size = (0, PIECE_ROWS[k]) if hf is None else (hf * half, half)
                return s_ref.at[pl.ds(PIECE_OFF[k] + start, size), :]
            return [pltpu.make_async_remote_copy(
                src_ref=src(k), dst_ref=rows_of(k, block, hf), send_sem=send_sems.at[rel], recv_sem=recv_sems.at[rel],
                device_id=to, device_id_type=MESH) for k in pieces]

        def waiter(rel, hf):
            nrows = len(pieces) * (FS if hf is None else half)
            grp = s_ref.at[pl.ds(0, nrows), :]
            return pltpu.make_async_remote_copy(src_ref=grp, dst_ref=grp, send_sem=send_sems.at[rel],
                                                recv_sem=recv_sems.at[rel], device_id=me, device_id_type=MESH)

        def start(cps):
            for cp in cps:
                cp.start()

        mine = [pltpu.make_async_copy(_shard_piece(s_ref, k), _block_rows(arrs, k, me), local_sem) for k in pieces]
        start(mine)
        start(copies(SIB, me, None, sib, True))
        start(copies(X0, me, 0, xn, True))
        start(copies(Y1, me, 1, yn, True))
        start(copies(X1, me, 1, xn, True))
        start(copies(Y0, me, 0, yn, True))
        first_norm()
        place_rest()
        waiter(X0, 0).wait_recv()
        start(copies(RELAY_Y, xn, 0, yn))
        waiter(Y1, 1).wait_recv()
        start(copies(RELAY_X, yn, 1, xn))
        waiter(X1, 1).wait_recv()
        start(copies(ON_X, xn, None, sib))
        waiter(Y0, 0).wait_recv()
        start(copies(ON_Y, yn, None, sib))
        waiter(RELAY_Y, 0).wait_recv()
        start(copies(ON_D0, dg, 0, sib))
        waiter(RELAY_X, 1).wait_recv()
        start(copies(ON_D1, dg, 1, sib))
        waiter(SIB, None).wait_recv()
        waiter(ON_X, None).wait_recv()
        waiter(ON_Y, None).wait_recv()
        waiter(ON_D0, 0).wait_recv()
        waiter(ON_D1, 1).wait_recv()
        for rel, hf in ((SIB, None), (X0, 0), (X1, 1), (Y0, 0), (Y1, 1), (RELAY_Y, 0), (RELAY_X, 1),
                        (ON_X, None), (ON_Y, None), (ON_D0, 0), (ON_D1, 1)):
            waiter(rel, hf).wait_send()
        grp = _shard_group(s_ref, pieces)
        pltpu.make_async_copy(grp, grp, local_sem).wait()

    hbm = pl.BlockSpec(memory_space=pl.ANY)
    return pl.pallas_call(
        body, in_specs=[hbm, hbm, pl.BlockSpec(memory_space=pltpu.VMEM)], out_specs=[hbm] * 5,
        out_shape=(SDS((3, F, D), bf16), SDS((T, D), bf16),
                   SDS((DIN, D), bf16), SDS((DMIX, D), bf16), SDS((3, F, D), bf16)),
        scratch_shapes=[pltpu.VMEM((NORM_ROWS, D), f32), pltpu.VMEM((NORM_ROWS, D), bf16),
                        pltpu.VMEM((_group_rows(rest_pieces), D), bf16),
                        pltpu.SemaphoreType.DMA((11,)), pltpu.SemaphoreType.DMA((11,)), pltpu.SemaphoreType.DMA,
                        pltpu.SemaphoreType.DMA((2,))],
        compiler_params=pltpu.CompilerParams(has_side_effects=True),
        name="all_gather_ffn1")(shard, x, gain)


HBM_SPEC = pl.BlockSpec(memory_space=pltpu.HBM)
SEM_SPEC = pl.BlockSpec(memory_space=pltpu.SEMAPHORE)
ANY_SPEC = pl.BlockSpec(memory_space=pl.ANY)
SPLIT_EFFECT = pltpu.SideEffectType.DATAFLOW_SIDE_EFFECTING


def _in_hbm(a):
    return pltpu.with_memory_space_constraint(a, pltpu.HBM)


def _hbm_like(a):
    return pltpu.HBM(a.shape, a.dtype)


def _gather_rest_start(shard, wi, wo, w2, w1):
    def body(s_ref, wi_ref, wo_ref, w2_ref, w1_ref,
             ssem_m, rsem_m0, rsem_m, ssem_f, rsem_f0, rsem_f, s_o, wi_o, wo_o, w2_o, w1_o):
        x, y, c = _position()
        me, sib = (x, y, c), (x, y, 1 - c)
        chips = [(1 - x, y), (x, 1 - y), (1 - x, 1 - y)]
        arrs = _weight_pieces(wi_ref=wi_ref, wo_ref=wo_ref, w2_ref=w2_ref)
        for pieces, ssem, rsem0, rsem in ((MIX_PIECES, ssem_m, rsem_m0, rsem_m), (F2_PIECES, ssem_f, rsem_f0, rsem_f)):
            for p in pieces:
                pltpu.make_async_remote_copy(
                    src_ref=_shard_piece(s_ref, p), dst_ref=_block_rows(arrs, p, me), send_sem=ssem.at[0],
                    recv_sem=rsem0, device_id=sib, device_id_type=MESH).start()
            for j, chip in enumerate(chips):
                for p in pieces:
                    pltpu.make_async_remote_copy(
                        src_ref=_shard_piece(s_ref, p), dst_ref=_block_rows(arrs, p, me), send_sem=ssem.at[1 + j],
                        recv_sem=rsem.at[j], device_id=(*chip, c), device_id_type=MESH).start()

    dma = pltpu.SemaphoreType.DMA
    return pl.pallas_call(
        body, name="gather_rest_start",
        out_shape=(dma((4,)), dma(()), dma((3,)), dma((4,)), dma(()), dma((3,)),
                   _hbm_like(shard), _hbm_like(wi), _hbm_like(wo), _hbm_like(w2), _hbm_like(w1)),
        in_specs=(HBM_SPEC,) * 5, out_specs=(SEM_SPEC,) * 6 + (HBM_SPEC,) * 5,
        input_output_aliases={0: 6, 1: 7, 2: 8, 3: 9, 4: 10},
        compiler_params=pltpu.CompilerParams(has_side_effects=SPLIT_EFFECT),
    )(_in_hbm(shard), _in_hbm(wi), _in_hbm(wo), _in_hbm(w2), _in_hbm(w1))


def _gather_mix_pass_on(rsem_m, wi, wo, thru, after):
    def body(wi_ref, wo_ref, thru_ref, rsem, after_ref, fsend, frecv, wi_o, wo_o, thru_o):
        x, y, c = _position()
        sib = (x, y, 1 - c)
        arrs = _weight_pieces(wi_ref=wi_ref, wo_ref=wo_ref)
        both = wi_ref.at[pl.ds(0, _group_rows(MIX_PIECES)), :]
        for j, chip in enumerate([(1 - x, y), (x, 1 - y), (1 - x, 1 - y)]):
            pltpu.make_async_remote_copy(src_ref=both, dst_ref=both, send_sem=fsend.at[j], recv_sem=rsem.at[j],
                                         device_id=(x, y, c), device_id_type=MESH).wait_recv()
            for p in MIX_PIECES:
                rows = _block_rows(arrs, p, (*chip, c))
                pltpu.make_async_remote_copy(src_ref=rows, dst_ref=rows, send_sem=fsend.at[j], recv_sem=frecv.at[j],
                                             device_id=sib, device_id_type=MESH).start()

    dma = pltpu.SemaphoreType.DMA
    return pl.pallas_call(
        body, name="gather_mix_pass_on",
        out_shape=(dma((3,)), dma((3,)), _hbm_like(wi), _hbm_like(wo), _hbm_like(thru)),
        in_specs=(HBM_SPEC, HBM_SPEC, HBM_SPEC, SEM_SPEC, ANY_SPEC), out_specs=(SEM_SPEC, SEM_SPEC) + (HBM_SPEC,) * 3,
        input_output_aliases={0: 2, 1: 3, 2: 4},
        compiler_params=pltpu.CompilerParams(has_side_effects=SPLIT_EFFECT),
    )(wi, wo, _in_hbm(thru), rsem_m, after)


def _gather_mix_wait(ssem_m, rsem_m0, fsend, frecv, shard, wi, wo, after):
    def body(s_ref, wi_ref, wo_ref, ssem, rsem0, fs, fr, after_ref, s_o, wi_o, wo_o):
        x, y, c = _position()
        grp = _shard_group(s_ref, MIX_PIECES)

        def waiter(send_sem, recv_sem):
            return pltpu.make_async_remote_copy(src_ref=grp, dst_ref=grp, send_sem=send_sem, recv_sem=recv_sem,
                                                device_id=(x, y, c), device_id_type=MESH)

        waiter(ssem.at[0], rsem0).wait_recv()
        for j in range(3):
            waiter(fs.at[j], fr.at[j]).wait_recv()
        for rel in range(4):
            waiter(ssem.at[rel], rsem0).wait_send()
        for j in range(3):
            waiter(fs.at[j], fr.at[j]).wait_send()

    return pl.pallas_call(
        body, name="gather_mix_wait", out_shape=(_hbm_like(shard), _hbm_like(wi), _hbm_like(wo)),
        in_specs=(HBM_SPEC,) * 3 + (SEM_SPEC,) * 4 + (ANY_SPEC,), out_specs=(HBM_SPEC,) * 3,
        input_output_aliases={0: 0, 1: 1, 2: 2},
        compiler_params=pltpu.CompilerParams(has_side_effects=SPLIT_EFFECT),
    )(shard, wi, wo, ssem_m, rsem_m0, fsend, frecv, after)


def _gather_ffn2_pass_on(rsem_f, w2, wo, after):
    def body(w2_ref, wo_ref, rsem, after_ref, fsend, frecv, w2_o, wo_o):
        x, y, c = _position()
        sib = (x, y, 1 - c)
        chips = [(1 - x, y), (x, 1 - y), (1 - x, 1 - y)]
        arrs = _weight_pieces(w2_ref=w2_ref)
        three = w2_ref.at[0, pl.ds(0, _group_rows(F2_PIECES)), :]
        for j, chip in enumerate(chips):
            pltpu.make_async_remote_copy(src_ref=three, dst_ref=three, send_sem=fsend.at[j], recv_sem=rsem.at[j],
                                         device_id=(x, y, c), device_id_type=MESH).wait_recv()
            for p in F2_PIECES:
                rows = _block_rows(arrs, p, (*chip, c))
                pltpu.make_async_remote_copy(src_ref=rows, dst_ref=rows, send_sem=fsend.at[j], recv_sem=frecv.at[j],
                                             device_id=sib, device_id_type=MESH).start()

    dma = pltpu.SemaphoreType.DMA
    return pl.pallas_call(
        body, name="gather_ffn2_pass_on", out_shape=(dma((3,)), dma((3,)), _hbm_like(w2), _hbm_like(wo)),
        in_specs=(HBM_SPEC, HBM_SPEC, SEM_SPEC, ANY_SPEC), out_specs=(SEM_SPEC, SEM_SPEC, HBM_SPEC, HBM_SPEC),
        input_output_aliases={0: 2, 1: 3},
        compiler_params=pltpu.CompilerParams(has_side_effects=SPLIT_EFFECT),
    )(w2, wo, rsem_f, after)


def _gather_ffn2_wait(ssem_f, rsem_f0, fsend, frecv, shard, w2, after):
    def body(s_ref, w2_ref, ssem, rsem0, fs, fr, after_ref, w2_o):
        x, y, c = _position()
        grp = _shard_group(s_ref, F2_PIECES)

        def waiter(send_sem, recv_sem):
            return pltpu.make_async_remote_copy(src_ref=grp, dst_ref=grp, send_sem=send_sem, recv_sem=recv_sem,
                                                device_id=(x, y, c), device_id_type=MESH)

        waiter(ssem.at[0], rsem0).wait_recv()
        for j in range(3):
            waiter(fs.at[j], fr.at[j]).wait_recv()
        for rel in range(4):
            waiter(ssem.at[rel], rsem0).wait_send()
        for j in range(3):
            waiter(fs.at[j], fr.at[j]).wait_send()

    return pl.pallas_call(
        body, name="gather_ffn2_wait", out_shape=_hbm_like(w2),
        in_specs=(HBM_SPEC, HBM_SPEC, SEM_SPEC, SEM_SPEC, SEM_SPEC, SEM_SPEC, ANY_SPEC), out_specs=HBM_SPEC,
        input_output_aliases={1: 0},
        compiler_params=pltpu.CompilerParams(has_side_effects=SPLIT_EFFECT),
    )(shard, w2, ssem_f, rsem_f0, fsend, frecv, after)


class _GatheredWeights(_LocalWeights):
    def __init__(self, shard, x, gain1):
        w1, self.h1, wi, wo, w2 = _all_gather_ffn1(shard, x, gain1)
        (self.ssem_m, self.rsem_m0, self.rsem_m, self.ssem_f, self.rsem_f0, self.rsem_f,
         self.shard, self.wi, self.wo, self.w2_part, self.w1) = _gather_rest_start(shard, wi, wo, w2, w1)

    def first_norm(self, x, gain):
        return self.h1

    def after_ffn1(self, gain, x1):
        self.fsend_m, self.frecv_m, self.wi, self.wo, gain = _gather_mix_pass_on(self.rsem_m, self.wi, self.wo, gain, x1)
        return gain

    def mix(self, after):
        self.shard, wint, wout = _gather_mix_wait(self.ssem_m, self.rsem_m0, self.fsend_m, self.frecv_m, self.shard,
                                                  self.wi, self.wo, after)
        return wint, wout

    def before_out_proj(self, wout, after):
        self.fsend, self.frecv, self.w2_part, wout = _gather_ffn2_pass_on(self.rsem_f, self.w2_part, wout, after)
        return wout

    def ffn2(self, after):
        return _gather_ffn2_wait(self.ssem_f, self.rsem_f0, self.fsend, self.frecv, self.shard, self.w2_part, after)

    def mix_ffn2_grads_ready(self, dwint, dwout, dw2, dh2):
        rx1 = lax.empty((4, RSA_ROWS, D), bf16)
        self.sa, self.ra, dwint, dwout, dw2, rx1, dh2 = _rsa_level1_start(dwint, dwout, dw2, rx1, dh2)
        self.level1 = (dwint, dwout, dw2, rx1)
        return dh2

    def before_ffn1_bwd(self, dx1b):
        dwint, dwout, dw2, rx1 = _rsa_level1_wait(self.sa, self.ra, *self.level1, dx1b)
        tx, self.acc = _rsa_chip_sums(dwint, dwout, dw2, rx1)
        rx2 = lax.empty((3, RSA_ROWS, D), bf16)
        self.sb, self.rb, self.tx, self.rx2, dx1b = _rsa_level2_start(tx, rx2, dx1b)
        return dx1b

    def mix_ffn2_grads_parts(self, after):
        rx2 = _rsa_level2_wait(self.sb, self.rb, self.tx, self.rx2, after)
        return self.acc, rx2


def _reduce_scatter_ffn1_head(dw1, small_packed):
    pieces = G1_PIECES
    half = FS // 2
    hrows = len(pieces) * half
    nrows = 2 * hrows
    X_RELAY, Y_RELAY = range(2)

    def body(d1_ref, p_ref, forx_ref, fory_ref, own_ref, rx1_ref, relx_ref, rely_ref, tot_ref,
             own_buf, rx_buf, tx1, tx2, tx3, acc, sa, ra, sb, rb, lsem, pair, chips, small_send, small_recv):
        x, y, c = _position()
        me, sib = (x, y, c), (x, y, 1 - c)
        xn, yn = (1 - x, y, c), (x, 1 - y, c)
        rel_chips = [(x, y), (1 - x, y), (x, 1 - y), (1 - x, 1 - y)]
        srcs = _weight_pieces(w1_ref=d1_ref)

        my_chip = 2 * x + y
        pair[c] = p_ref[...]
        swap = pltpu.make_async_remote_copy(
            src_ref=p_ref, dst_ref=pair.at[c], send_sem=small_send.at[0], recv_sem=small_recv.at[0],
            device_id=sib, device_id_type=MESH)
        swap.start()
        small = [pltpu.make_async_remote_copy(
            src_ref=chips.at[my_chip], dst_ref=chips.at[my_chip], send_sem=small_send.at[j], recv_sem=small_recv.at[j],
            device_id=(*rel_chips[j], c), device_id_type=MESH) for j in (1, 2, 3)]

        def part(k, dev, hf):
            r = PIECE_ROWS[k]
            return srcs[k].at[pl.ds(pl.multiple_of(_dev_index(*dev) * r + hf * half, 16), half), :]

        def slot(ref, k, hf):
            return ref.at[pl.ds(hf * hrows + k * half, half), :]

        halves = [(k, hf) for hf in (0, 1) for k in pieces]

        for j in (3, 1, 2, 0):
            for k, hf in halves:
                pltpu.make_async_remote_copy(
                    src_ref=part(k, (*rel_chips[j], 1 - c), hf), dst_ref=slot(rx1_ref.at[j], k, hf),
                    send_sem=sa.at[j], recv_sem=ra.at[j], device_id=sib, device_id_type=MESH).start()

        def wait_a(j):
            return pltpu.make_async_remote_copy(src_ref=rx1_ref.at[j], dst_ref=rx1_ref.at[j], send_sem=sa.at[j],
                                                recv_sem=ra.at[j], device_id=me, device_id_type=MESH)

        def ici(rel, src, dst, to):
            return pltpu.make_async_remote_copy(src_ref=src, dst_ref=dst, send_sem=sb.at[rel], recv_sem=rb.at[rel],
                                                device_id=to, device_id_type=MESH)

        first, second = pl.ds(0, hrows), pl.ds(hrows, hrows)
        sends = {
            X_RELAY: ici(X_RELAY, tx3.at[first, :], relx_ref, xn),
            Y_RELAY: ici(Y_RELAY, tx3.at[second, :], rely_ref, yn),
        }

        swap.wait_recv()
        chips[my_chip] = pair[0] + pair[1]
        for cp in small:
            cp.start()

        def chip_sum(j, dst):
            loads = [pltpu.make_async_copy(part(k, (*rel_chips[j], c), hf), slot(own_buf, k, hf), lsem.at[0])
                     for k, hf in halves]
            for cp in loads:
                cp.start()
            wait_a(j).wait_recv()
            got = pltpu.make_async_copy(rx1_ref.at[j], rx_buf, lsem.at[1])
            got.start()
            pltpu.make_async_copy(rx_buf, rx_buf, lsem.at[0]).wait()
            got.wait()

            def add(i, carry):
                rows = pl.ds(pl.multiple_of(i * half, 16), half)
                tot = own_buf[rows, :].astype(f32) + rx_buf[rows, :].astype(f32)
                dst[rows, :] = tot.astype(dst.dtype)
                return carry

            lax.fori_loop(0, nrows // half, add, 0)

        def add_landed(landed, dst, rows0, nrows_):
            got = pltpu.make_async_copy(landed, rx_buf.at[pl.ds(0, nrows_), :], lsem.at[1])
            got.start()
            got.wait()

            def add(i, carry):
                src_rows = pl.ds(pl.multiple_of(i * half, 16), half)
                dst_rows = pl.ds(pl.multiple_of(rows0 + i * half, 16), half)
                dst[dst_rows, :] = (dst[dst_rows, :].astype(f32) + rx_buf[src_rows, :].astype(f32)).astype(dst.dtype)
                return carry

            lax.fori_loop(0, nrows_ // half, add, 0)

        chip_sum(3, tx3)
        sends[X_RELAY].start()
        sends[Y_RELAY].start()
        chip_sum(1, tx1)
        chip_sum(2, tx2)
        chip_sum(0, acc)
        own_out = pltpu.make_async_copy(acc, own_ref, lsem.at[0])
        own_out.start()
        sends[X_RELAY].wait_recv()
        add_landed(relx_ref, tx2, 0, hrows)
        sends[Y_RELAY].wait_recv()
        add_landed(rely_ref, tx1, hrows, hrows)
        own_out.wait()
        outs = [pltpu.make_async_copy(tx1, forx_ref, lsem.at[0]), pltpu.make_async_copy(tx2, fory_ref, lsem.at[1])]
        for cp in outs:
            cp.start()
        for cp in outs:
            cp.wait()
        for cp in small:
            cp.wait_recv()
        tot = (chips[0] + chips[1]) + (chips[2] + chips[3])
        tot_ref[...] = tot
        loss = jnp.sum(tot[LOSS_ROW:LOSS_ROW + 1, :], axis=-1, keepdims=True)
        tot_ref[LOSS_ROW:LOSS_ROW + 1, :] = jnp.broadcast_to(loss, (1, 128))
        for j in range(4):
            wait_a(j).wait_send()
        for cp in sends.values():
            cp.wait_send()
        swap.wait_send()
        for cp in small:
            cp.wait_send()

    hbm = pl.BlockSpec(memory_space=pl.ANY)
    vm = pl.BlockSpec(memory_space=pltpu.VMEM)
    outs = pl.pallas_call(
        body, in_specs=[hbm, vm], out_specs=[hbm] * 6 + [vm],
        out_shape=(SDS((nrows, D), bf16), SDS((nrows, D), bf16), SDS((nrows, D), f32), SDS((4, nrows, D), bf16),
                   SDS((hrows, D), bf16), SDS((hrows, D), bf16), SDS((SMALL_ROWS, 128), f32)),
        scratch_shapes=[pltpu.VMEM((nrows, D), bf16), pltpu.VMEM((nrows, D), bf16),
                        pltpu.VMEM((nrows, D), bf16), pltpu.VMEM((nrows, D), bf16), pltpu.VMEM((nrows, D), bf16),
                        pltpu.VMEM((nrows, D), f32),
                        pltpu.SemaphoreType.DMA((4,)), pltpu.SemaphoreType.DMA((4,)),
                        pltpu.SemaphoreType.DMA((2,)), pltpu.SemaphoreType.DMA((2,)), pltpu.SemaphoreType.DMA((2,)),
                        pltpu.VMEM((2, SMALL_ROWS, 128), f32), pltpu.VMEM((4, SMALL_ROWS, 128), f32),
                        pltpu.SemaphoreType.DMA((4,)), pltpu.SemaphoreType.DMA((4,))],
        compiler_params=pltpu.CompilerParams(has_side_effects=True, vmem_limit_bytes=VMEM_LIMIT_V7X),
        name="reduce_scatter_ffn1_head")(dw1, small_packed)
    return outs[0], outs[1], outs[2], outs[-1]


def _rs1_tail_start(for_x, for_y, from_x, from_y, thru):
    def body(fx_ref, fy_ref, lx_ref, ly_ref, thru_ref, ssem, rsem, fx_o, fy_o, lx_o, ly_o, thru_o):
        x, y, c = _position()
        pltpu.make_async_remote_copy(src_ref=fx_ref, dst_ref=lx_ref, send_sem=ssem.at[0], recv_sem=rsem.at[0],
                                     device_id=(1 - x, y, c), device_id_type=MESH).start()
        pltpu.make_async_remote_copy(src_ref=fy_ref, dst_ref=ly_ref, send_sem=ssem.at[1], recv_sem=rsem.at[1],
                                     device_id=(x, 1 - y, c), device_id_type=MESH).start()

    dma = pltpu.SemaphoreType.DMA
    arrs = (for_x, for_y, from_x, from_y, thru)
    return pl.pallas_call(
        body, name="rs1_tail_start", out_shape=(dma((2,)), dma((2,))) + tuple(_hbm_like(a) for a in arrs),
        in_specs=(HBM_SPEC,) * 5, out_specs=(SEM_SPEC,) * 2 + (HBM_SPEC,) * 5,
        input_output_aliases={0: 2, 1: 3, 2: 4, 3: 5, 4: 6},
        compiler_params=pltpu.CompilerParams(has_side_effects=SPLIT_EFFECT),
    )(*[_in_hbm(a) for a in arrs])


def _rs1_tail_wait(ssem, rsem, for_x, for_y, from_x, from_y, after):
    def body(fx_ref, fy_ref, lx_ref, ly_ref, ssem_ref, rsem_ref, after_ref, lx_o, ly_o):
        x, y, c = _position()
        for j, (src, dst) in enumerate(((fx_ref, lx_ref), (fy_ref, ly_ref))):
            d = pltpu.make_async_remote_copy(src_ref=src, dst_ref=dst, send_sem=ssem_ref.at[j], recv_sem=rsem_ref.at[j],
                                             device_id=(x, y, c), device_id_type=MESH)
            d.wait_recv()
            d.wait_send()

    return pl.pallas_call(
        body, name="rs1_tail_wait", out_shape=(_hbm_like(from_x), _hbm_like(from_y)),
        in_specs=(HBM_SPEC,) * 4 + (SEM_SPEC, SEM_SPEC, ANY_SPEC), out_specs=(HBM_SPEC, HBM_SPEC),
        input_output_aliases={2: 0, 3: 1},
        compiler_params=pltpu.CompilerParams(has_side_effects=SPLIT_EFFECT),
    )(for_x, for_y, from_x, from_y, ssem, rsem, after)


RSA_PIECES = MIX_PIECES + F2_PIECES
RSA_ROWS = _group_rows(RSA_PIECES)
RSA_OFF = {k: PIECE_OFF[k] - PIECE_OFF[RSA_PIECES[0]] for k in RSA_PIECES}
RSA_BLOCK = 192


def _rsa_rows(ref, k):
    return ref.at[pl.ds(RSA_OFF[k], PIECE_ROWS[k]), :]


def _rsa_level1_start(dwint, dwout, dw2, rx1, thru):
    def body(di_ref, do_ref, d2_ref, rx1_ref, thru_ref, sa, ra, di_o, do_o, d2_o, rx1_o, thru_o):
        x, y, c = _position()
        srcs = _weight_pieces(wi_ref=di_ref, wo_ref=do_ref, w2_ref=d2_ref)
        for j, chip in enumerate([(x, y), (1 - x, y), (x, 1 - y), (1 - x, 1 - y)]):
            for k in RSA_PIECES:
                pltpu.make_async_remote_copy(
                    src_ref=_block_rows(srcs, k, (*chip, 1 - c)), dst_ref=_rsa_rows(rx1_ref.at[j], k),
                    send_sem=sa.at[j], recv_sem=ra.at[j], device_id=(x, y, 1 - c), device_id_type=MESH).start()

    dma = pltpu.SemaphoreType.DMA
    arrs = (dwint, dwout, dw2, rx1, thru)
    return pl.pallas_call(
        body, name="rsa_level1_start", out_shape=(dma((4,)), dma((4,))) + tuple(_hbm_like(a) for a in arrs),
        in_specs=(HBM_SPEC,) * 5, out_specs=(SEM_SPEC,) * 2 + (HBM_SPEC,) * 5,
        input_output_aliases={0: 2, 1: 3, 2: 4, 3: 5, 4: 6},
        compiler_params=pltpu.CompilerParams(has_side_effects=SPLIT_EFFECT),
    )(*[_in_hbm(a) for a in arrs])


def _rsa_level1_wait(sa, ra, dwint, dwout, dw2, rx1, after):
    def body(di_ref, do_ref, d2_ref, rx1_ref, sa_ref, ra_ref, after_ref, di_o, do_o, d2_o, rx1_o):
        x, y, c = _position()
        for j in range(4):
            d = pltpu.make_async_remote_copy(src_ref=rx1_ref.at[j], dst_ref=rx1_ref.at[j], send_sem=sa_ref.at[j],
                                             recv_sem=ra_ref.at[j], device_id=(x, y, c), device_id_type=MESH)
            d.wait_recv()
            d.wait_send()

    arrs = (dwint, dwout, dw2, rx1)
    return pl.pallas_call(
        body, name="rsa_level1_wait", out_shape=tuple(_hbm_like(a) for a in arrs),
        in_specs=(HBM_SPEC,) * 4 + (SEM_SPEC, SEM_SPEC, ANY_SPEC), out_specs=(HBM_SPEC,) * 4,
        input_output_aliases={0: 0, 1: 1, 2: 2, 3: 3},
        compiler_params=pltpu.CompilerParams(has_side_effects=SPLIT_EFFECT),
    )(*arrs, sa, ra, after)


def _rsa_chip_sums(dwint, dwout, dw2, rx1):
    nblk = RSA_ROWS // RSA_BLOCK

    def body(di_ref, do_ref, d2_ref, rx1_ref, tx_ref, acc_ref, own_buf, rx_buf, tx_buf, acc_buf, in_sems, out_sems):
        x, y, c = _position()
        srcs = _weight_pieces(wi_ref=di_ref, wo_ref=do_ref, w2_ref=d2_ref)
        chips = [(x, y), (1 - x, y), (x, 1 - y), (1 - x, 1 - y)]

        def start_loads(j):
            s = j % 2
            for k in RSA_PIECES:
                pltpu.make_async_copy(_block_rows(srcs, k, (*chips[j], c)), _rsa_rows(own_buf.at[s], k),
                                      in_sems.at[2 * s]).start()
            pltpu.make_async_copy(rx1_ref.at[j], rx_buf.at[s], in_sems.at[2 * s + 1]).start()

        def wait_loads(j):
            s = j % 2
            pltpu.make_async_copy(rx1_ref.at[j], own_buf.at[s], in_sems.at[2 * s]).wait()
            pltpu.make_async_copy(rx1_ref.at[j], rx_buf.at[s], in_sems.at[2 * s + 1]).wait()

        def store(j):
            if j == 0:
                return pltpu.make_async_copy(acc_buf, acc_ref, out_sems.at[2])
            return pltpu.make_async_copy(tx_buf.at[j % 2], tx_ref.at[j - 1], out_sems.at[j % 2])

        start_loads(0)
        for j in range(4):
            s = j % 2
            if j + 1 < 4:
                start_loads(j + 1)
            wait_loads(j)
            if j == 3:
                store(1).wait()

            def add(i, carry, j=j, s=s):
                rows = pl.ds(pl.multiple_of(i * RSA_BLOCK, 16), RSA_BLOCK)
                tot = own_buf[s, rows, :].astype(f32) + rx_buf[s, rows, :].astype(f32)
                if j == 0:
                    acc_buf[rows, :] = tot
                else:
                    tx_buf[s, rows, :] = tot.astype(bf16)
                return carry

            lax.fori_loop(0, nblk, add, 0)
            store(j).start()
        store(0).wait()
        store(2).wait()
        store(3).wait()

    return pl.pallas_call(
        body, in_specs=[ANY_SPEC] * 4, out_specs=[ANY_SPEC] * 2,
        out_shape=(SDS((3, RSA_ROWS, D), bf16), SDS((RSA_ROWS, D), f32)),
        scratch_shapes=[pltpu.VMEM((2, RSA_ROWS, D), bf16), pltpu.VMEM((2, RSA_ROWS, D), bf16),
                        pltpu.VMEM((2, RSA_ROWS, D), bf16), pltpu.VMEM((RSA_ROWS, D), f32),
                        pltpu.SemaphoreType.DMA((4,)), pltpu.SemaphoreType.DMA((3,))],
        compiler_params=_cparams(None, VMEM_LIMIT_V7X), name="rsa_chip_sums")(dwint, dwout, dw2, rx1)


def _rsa_level2_start(tx, rx2, thru):
    def body(tx_ref, rx2_ref, thru_ref, sb, rb, tx_o, rx2_o, thru_o):
        x, y, c = _position()
        for j, chip in enumerate([(1 - x, y), (x, 1 - y), (1 - x, 1 - y)]):
            pltpu.make_async_remote_copy(src_ref=tx_ref.at[j], dst_ref=rx2_ref.at[j], send_sem=sb.at[j],
                                         recv_sem=rb.at[j], device_id=(*chip, c), device_id_type=MESH).start()

    dma = pltpu.SemaphoreType.DMA
    arrs = (tx, rx2, thru)
    return pl.pallas_call(
        body, name="rsa_level2_start", out_shape=(dma((3,)), dma((3,))) + tuple(_hbm_like(a) for a in arrs),
        in_specs=(HBM_SPEC,) * 3, out_specs=(SEM_SPEC,) * 2 + (HBM_SPEC,) * 3,
        input_output_aliases={0: 2, 1: 3, 2: 4},
        compiler_params=pltpu.CompilerParams(has_side_effects=SPLIT_EFFECT),
    )(*[_in_hbm(a) for a in arrs])


def _rsa_level2_wait(sb, rb, tx, rx2, after):
    def body(tx_ref, rx2_ref, sb_ref, rb_ref, after_ref, rx2_o):
        x, y, c = _position()
        for j in range(3):
            d = pltpu.make_async_remote_copy(src_ref=tx_ref.at[j], dst_ref=rx2_ref.at[j], send_sem=sb_ref.at[j],
                                             recv_sem=rb_ref.at[j], device_id=(x, y, c), device_id_type=MESH)
            d.wait_recv()
            d.wait_send()

    return pl.pallas_call(
        body, name="rsa_level2_wait", out_shape=_hbm_like(rx2),
        in_specs=(HBM_SPEC, HBM_SPEC, SEM_SPEC, SEM_SPEC, ANY_SPEC), out_specs=HBM_SPEC,
        input_output_aliases={1: 0},
        compiler_params=pltpu.CompilerParams(has_side_effects=SPLIT_EFFECT),
    )(tx, rx2, sb, rb, after)


def _adamw_math(w, g, m, v):
    m = ADAM_B1 * m + (1.0 - ADAM_B1) * g
    v = ADAM_B2 * v + (1.0 - ADAM_B2) * (g * g)
    m_hat = m / (1.0 - ADAM_B1 ** ADAM_STEP)
    v_hat = v / (1.0 - ADAM_B2 ** ADAM_STEP)
    delta = -ADAM_LR * (m_hat / (jnp.sqrt(v_hat) + ADAM_EPS) + ADAM_WD * w)
    return delta, m, v


def _adamw_big(pieces, ws, ms, vs, own, landed, name):
    npiece = len(pieces)
    nland = sum(a.shape[0] if a.ndim == 3 else 1 for a in landed)
    rmax = max(PIECE_ROWS[k] for k in pieces)
    half = FS // 2

    def segments(k):
        if k in G1_PIECES:
            return [(hf * len(G1_PIECES) * half + k * half, hf * half, half) for hf in (0, 1)]
        return [(RSA_OFF[k], 0, PIECE_ROWS[k])]

    def body(*refs):
        ins = (refs[0:npiece], refs[npiece:2 * npiece], refs[2 * npiece:3 * npiece])
        own_ref = refs[3 * npiece]
        nin = 3 * npiece + 1 + len(landed)
        land_refs = []
        for ref, a in zip(refs[3 * npiece + 1:nin], landed):
            land_refs += [ref.at[j] for j in range(a.shape[0])] if a.ndim == 3 else [ref]
        out_refs = refs[nin:nin + 4 * npiece]
        inb, landb, outb, in_sems, land_sems, out_sems = refs[nin + 4 * npiece:]

        def loads(i):
            s, k = i % 2, pieces[i]
            r = PIECE_ROWS[k]
            cps = [pltpu.make_async_copy(ins[q][i].at[0], inb.at[s, q, pl.ds(0, r), :], in_sems.at[4 * s + q])
                   for q in range(3)]
            waits = list(cps)
            for src0, dst0, n in segments(k):
                cps.append(pltpu.make_async_copy(own_ref.at[pl.ds(src0, n), :], inb.at[s, 3, pl.ds(dst0, n), :],
                                                 in_sems.at[4 * s + 3]))
                for p in range(nland):
                    cps.append(pltpu.make_async_copy(land_refs[p].at[pl.ds(src0, n), :],
                                                     landb.at[s, p, pl.ds(dst0, n), :], land_sems.at[nland * s + p]))
            own_rows = inb.at[s, 3, pl.ds(0, r), :]
            waits.append(pltpu.make_async_copy(own_rows, own_rows, in_sems.at[4 * s + 3]))
            for p in range(nland):
                rows = landb.at[s, p, pl.ds(0, r), :]
                waits.append(pltpu.make_async_copy(rows, rows, land_sems.at[nland * s + p]))
            return cps, waits

        def stores(i):
            s, r = i % 2, PIECE_ROWS[pieces[i]]
            return [pltpu.make_async_copy(outb.at[s, q, pl.ds(0, r), :], out_refs[q * npiece + i].at[0],
                                          out_sems.at[4 * s + q]) for q in range(4)]

        for cp in loads(0)[0]:
            cp.start()
        for i in range(npiece):
            s, r = i % 2, PIECE_ROWS[pieces[i]]
            if i + 1 < npiece:
                for cp in loads(i + 1)[0]:
                    cp.start()
            for cp in loads(i)[1]:
                cp.wait()
            if i >= 2:
                for cp in stores(i - 2):
                    cp.wait()
            g = inb[s, 3, 0:r, :]
            for p in range(nland):
                g = g + landb[s, p, 0:r, :].astype(f32)
            d, nm, nv = _adamw_math(inb[s, 0, 0:r, :], g, inb[s, 1, 0:r, :], inb[s, 2, 0:r, :])
            outb[s, 0, 0:r, :] = g
            outb[s, 1, 0:r, :] = d
            outb[s, 2, 0:r, :] = nm
            outb[s, 3, 0:r, :] = nv
            for cp in stores(i):
                cp.start()
        for i in range(max(npiece - 2, 0), npiece):
            for cp in stores(i):
                cp.wait()

    hbm = pl.BlockSpec(memory_space=pl.ANY)
    outs = pl.pallas_call(
        body, in_specs=[hbm] * (3 * npiece + 1 + len(landed)), out_specs=[hbm] * (4 * npiece),
        out_shape=tuple(SDS(w.shape, f32) for _ in range(4) for w in ws),
        scratch_shapes=[pltpu.VMEM((2, 4, rmax, D), f32), pltpu.VMEM((2, nland, rmax, D), bf16),
                        pltpu.VMEM((2, 4, rmax, D), f32),
                        pltpu.SemaphoreType.DMA((8,)), pltpu.SemaphoreType.DMA((2 * nland,)),
                        pltpu.SemaphoreType.DMA((8,))],
        compiler_params=_cparams(None, VMEM_LIMIT_V7X), name=name)(*ws, *ms, *vs, own, *landed)
    return [list(outs[q * npiece:(q + 1) * npiece]) for q in range(4)]


def _adamw_small(ws, ms, vs, gs, name):
    n = len(ws)

    def body(*refs):
        w_refs, m_refs, v_refs, g_refs = refs[0:n], refs[n:2 * n], refs[2 * n:3 * n], refs[3 * n:4 * n]
        outs = refs[4 * n:]
        for i in range(n):
            d, nm, nv = _adamw_math(w_refs[i][...], g_refs[i][...], m_refs[i][...], v_refs[i][...])
            outs[i][...] = d
            outs[n + i][...] = nm
            outs[2 * n + i][...] = nv

    outs = pl.pallas_call(
        body, out_shape=tuple(SDS(w.shape, f32) for _ in range(3) for w in ws), name=name)(*ws, *ms, *vs, *gs)
    return [list(outs[q * n:(q + 1) * n]) for q in range(3)]


WEIGHTS = ("ffn1_norm", "ffn1_w_gate", "ffn1_w_up", "ffn1_w_down", "mix_norm", "w_in", "q_norm", "k_norm",
           "attn_sinks", "rel_bias", "pool_w", "pool_scale", "w_out", "ffn2_norm", "ffn2_w_gate", "ffn2_w_up",
           "ffn2_w_down")
BIG = (("ffn1_w_gate", True), ("ffn1_w_up", True), ("ffn1_w_down", False), ("w_in", True), ("w_out", False),
       ("ffn2_w_gate", True), ("ffn2_w_up", True), ("ffn2_w_down", False))


def kernel(x, ffn1_norm, ffn1_w_gate, ffn1_w_up, ffn1_w_down, mix_norm, w_in, q_norm, k_norm, attn_sinks, rel_bias, pool_w, pool_scale, w_out, ffn2_norm, ffn2_w_gate, ffn2_w_up, ffn2_w_down, loss_target, m_ffn1_norm, m_ffn1_w_gate, m_ffn1_w_up, m_ffn1_w_down, m_mix_norm, m_w_in, m_q_norm, m_k_norm, m_attn_sinks, m_rel_bias, m_pool_w, m_pool_scale, m_w_out, m_ffn2_norm, m_ffn2_w_gate, m_ffn2_w_up, m_ffn2_w_down, v_ffn1_norm, v_ffn1_w_gate, v_ffn1_w_up, v_ffn1_w_down, v_mix_norm, v_w_in, v_q_norm, v_k_norm, v_attn_sinks, v_rel_bias, v_pool_w, v_pool_scale, v_w_out, v_ffn2_norm, v_ffn2_w_gate, v_ffn2_w_up, v_ffn2_w_down):
    args = dict(locals())
    w = {n: args[n] for n in WEIGHTS}
    m = {n: args["m_" + n] for n in WEIGHTS}
    v = {n: args["v_" + n] for n in WEIGHTS}

    as_rows = lambda a, tr: jnp.swapaxes(a, 1, 2) if tr else a
    shard = jnp.concatenate([as_rows(w[n], tr)[0].astype(bf16) for n, tr in BIG], axis=0)
    exchanges = _GatheredWeights(shard, x[0], ffn1_norm)
    gx, (dw1, _, _, _), small = _local_step(
        x[0], loss_target[0], exchanges, ffn1_norm, mix_norm, ffn2_norm, q_norm, k_norm, attn_sinks,
        rel_bias, pool_w[0], pool_scale)

    nrows1 = len(G1_PIECES) * FS
    for_x, for_y, own1, small_tot = _reduce_scatter_ffn1_head(dw1, _pack_small(small))
    ssem, rsem, for_x, for_y, from_x, from_y, small_tot = _rs1_tail_start(
        for_x, for_y, lax.empty((nrows1, D), bf16), lax.empty((nrows1, D), bf16), small_tot)
    own_rest, landed_rest = exchanges.mix_ffn2_grads_parts(small_tot)

    grads, deltas, new_m, new_v = {}, {}, {}, {}
    rest = [k for k in range(len(BIG)) if k not in G1_PIECES]
    rows_of = lambda t, ks: [as_rows(t[BIG[k][0]], BIG[k][1]) for k in ks]
    rest_out = _adamw_big(rest, rows_of(w, rest), rows_of(m, rest), rows_of(v, rest), own_rest, [landed_rest],
                          "adamw_rest")
    from_x, from_y = _rs1_tail_wait(ssem, rsem, for_x, for_y, from_x, from_y, rest_out[0][0])
    ffn1 = list(G1_PIECES)
    ffn1_out = _adamw_big(ffn1, rows_of(w, ffn1), rows_of(m, ffn1), rows_of(v, ffn1), own1, [from_x, from_y],
                          "adamw_ffn1")
    for ks, out in ((rest, rest_out), (ffn1, ffn1_out)):
        for i, k in enumerate(ks):
            n, tr = BIG[k]
            grads[n], deltas[n], new_m[n], new_v[n] = [as_rows(o[i], tr) for o in out]
    small_names = [n for n in SMALL_NAMES if n != "loss"]
    for n in small_names:
        grads[n] = _unpack_small(small_tot, n)
    ds, nms, nvs = _adamw_small([w[n] for n in small_names], [m[n] for n in small_names], [v[n] for n in small_names],
                                [grads[n] for n in small_names], "adamw_small")
    for i, n in enumerate(small_names):
        deltas[n], new_m[n], new_v[n] = ds[i], nms[i], nvs[i]
    loss = small_tot[LOSS_ROW, 0]
    return (loss, gx[None], *[grads[n] for n in WEIGHTS], *[deltas[n] for n in WEIGHTS],
            *[new_m[n] for n in WEIGHTS], *[new_v[n] for n in WEIGHTS])
```

```python
import jax
import jax.numpy as jnp
import numpy as np
from jax import lax
from jax.experimental import pallas as pl
from jax.experimental.pallas import tpu as pltpu

f32, bf16, i32 = jnp.float32, jnp.bfloat16, jnp.int32
SDS = jax.ShapeDtypeStruct

D = 1024
F = 2816
HD = 64
NH = 8
NKV = 2
GQA = NH // NKV
DATTN = NH * HD
DKV = NKV * HD
DPOOL = 512
POOL_WINDOWS = (2, 4, 8, 16)
PGD = DPOOL // len(POOL_WINDOWS)
DIN = DATTN + 2 * DKV + DPOOL
DMIX = DATTN + DPOOL
BLK = 128
NBUCK = 32
MAX_DISTANCE = 128
EPS = 1e-6
NEG = -1e30
SCALE = HD ** -0.5

ADAM_LR, ADAM_B1, ADAM_B2, ADAM_EPS, ADAM_WD, ADAM_STEP = 0.001, 0.9, 0.999, 1e-08, 0.01, 10

NDEV = 8
FS = F // NDEV
INS = DIN // NDEV
OUTS = DMIX // NDEV
PIECE_ROWS = (FS, FS, FS, INS, OUTS, FS, FS, FS)
PIECE_OFF = tuple(int(v) for v in np.cumsum((0,) + PIECE_ROWS[:-1]))
PACK_ROWS = sum(PIECE_ROWS)

VMEM_LIMIT_V7X = 56 * 1024 * 1024

MESH = pl.DeviceIdType.MESH


def _cparams(sem=None, vmem=None):
    return pltpu.CompilerParams(dimension_semantics=sem, vmem_limit_bytes=vmem)


def _nt(a, b):
    return lax.dot_general(a, b, (((1,), (1,)), ((), ())), preferred_element_type=f32)


def _tn(a, b):
    return lax.dot_general(a, b, (((0,), (0,)), ((), ())), preferred_element_type=f32)


def _nn(a, b):
    return jnp.dot(a, b, preferred_element_type=f32)


def _sigmoid(x):
    return 1.0 / (1.0 + jnp.exp(-x))


def _norm_fwd(x, g, name):
    T = x.shape[0]
    tm = min(512, T)

    def body(x_ref, g_ref, h_ref):
        xv = x_ref[...]
        r = lax.rsqrt(jnp.mean(xv * xv, axis=-1, keepdims=True) + EPS)
        h_ref[...] = (xv * r * g_ref[...]).astype(bf16)

    return pl.pallas_call(
        body, grid=(T // tm,),
        in_specs=[pl.BlockSpec((tm, D), lambda i: (i, 0)), pl.BlockSpec((1, D), lambda i: (0, 0))],
        out_specs=pl.BlockSpec((tm, D), lambda i: (i, 0)),
        out_shape=SDS((T, D), bf16), name=name)(x, g)


def _norm_bwd(dh, x, g, dres, out_scale, name):
    T = x.shape[0]
    tm = min(512, T)

    def body(dh_ref, x_ref, g_ref, dr_ref, dx_ref, dxb_ref, dg_ref):
        i = pl.program_id(0)
        xv = x_ref[...]
        r = lax.rsqrt(jnp.mean(xv * xv, axis=-1, keepdims=True) + EPS)
        xh = xv * r
        dhv = dh_ref[...]
        dxh = dhv * g_ref[...]
        dx = dr_ref[...] + r * (dxh - xh * jnp.mean(dxh * xh, axis=-1, keepdims=True))
        dx_ref[...] = dx
        dxb_ref[...] = (out_scale * dx).astype(bf16)
        dg = jnp.sum(dhv * xh, axis=0, keepdims=True)

        @pl.when(i == 0)
        def _():
            dg_ref[...] = dg

        @pl.when(i > 0)
        def _():
            dg_ref[...] += dg

    tok = pl.BlockSpec((tm, D), lambda i: (i, 0))
    vec = pl.BlockSpec((1, D), lambda i: (0, 0))
    return pl.pallas_call(
        body, grid=(T // tm,),
        in_specs=[tok, tok, vec, tok], out_specs=[tok, tok, vec],
        out_shape=(SDS((T, D), f32), SDS((T, D), bf16), SDS((1, D), f32)),
        compiler_params=_cparams(("arbitrary",)), name=name)(dh, x, g, dres)


FFN_ROW_CHUNK = 256


def _ffn_tiles(T):
    return min(1024, T), 256


def _ffn_fwd(h, w, x, target, next_gain, name):
    T = h.shape[0]
    tm, tf = _ffn_tiles(T)
    nf = F // tf
    with_loss = target is not None
    assert with_loss != (next_gain is not None)

    def body(*refs):
        if with_loss:
            h_ref, w_ref, x_hbm, t_hbm, xo_ref, g_ref, u_ref, dyb_ref, loss_ref, tbuf, sem = refs
        else:
            h_ref, w_ref, x_hbm, gain_ref, xo_ref, g_ref, u_ref, hn_ref, sem = refs
        fi = pl.program_id(0)

        @pl.when(fi == 0)
        def _():
            cp = pltpu.make_async_copy(x_hbm, xo_ref, sem)
            cp.start()
            cp.wait()

        wgu = w_ref[0:2].reshape(2 * tf, D)
        for r in range(0, T, tm):
            rows = slice(r, r + tm)
            gu = _nt(h_ref[rows, :], wgu)
            gate, up = gu[:, :tf], gu[:, tf:]
            act = gate * _sigmoid(gate) * up
            g_ref[0, rows, :] = gate.astype(bf16)
            u_ref[0, rows, :] = up.astype(bf16)
            xo_ref[rows, :] += _nn((0.5 * act).astype(bf16), w_ref[2])

        if with_loss:
            @pl.when(fi == nf - 1)
            def _():
                lanes = jnp.zeros((1, 128), f32)
                for r in range(0, T, tm):
                    rows = slice(r, r + tm)
                    cp = pltpu.make_async_copy(t_hbm.at[pl.ds(r, tm), :], tbuf, sem)
                    cp.start()
                    cp.wait()
                    e = xo_ref[rows, :] - tbuf[...]
                    dy = e * (1.0 / D)
                    xo_ref[rows, :] = dy
                    dyb_ref[rows, :] = (0.5 * dy).astype(bf16)
                    col = jnp.sum(e * e, axis=0, keepdims=True) * (0.5 / D)
                    for k in range(D // 128):
                        lanes = lanes + col[:, 128 * k:128 * (k + 1)]
                loss_ref[...] = lanes
        else:
            @pl.when(fi == nf - 1)
            def _():
                for r in range(0, T, FFN_ROW_CHUNK):
                    rows = slice(r, r + FFN_ROW_CHUNK)
                    xv = xo_ref[rows, :]
                    rstd = lax.rsqrt(jnp.mean(xv * xv, axis=-1, keepdims=True) + EPS)
                    hn_ref[rows, :] = (xv * rstd * gain_ref[...]).astype(bf16)

    tok = pl.BlockSpec((T, D), lambda f: (0, 0))
    act_spec = pl.BlockSpec((1, T, tf), lambda f: (f, 0, 0))
    hbm = pl.BlockSpec(memory_space=pl.ANY)
    in_specs = [tok, pl.BlockSpec((3, tf, D), lambda f: (0, f, 0)), hbm]
    out_specs = [tok, act_spec, act_spec]
    out_shape = [SDS((T, D), f32), SDS((nf, T, tf), bf16), SDS((nf, T, tf), bf16)]
    scratch = [pltpu.SemaphoreType.DMA]
    args = [h, w, x]
    if with_loss:
        in_specs.append(hbm)
        args.append(target)
        out_specs += [tok, pl.BlockSpec((1, 128), lambda f: (0, 0))]
        out_shape += [SDS((T, D), bf16), SDS((1, 128), f32)]
        scratch = [pltpu.VMEM((tm, D), f32)] + scratch
    else:
        in_specs.append(pl.BlockSpec((1, D), lambda f: (0, 0)))
        args.append(next_gain)
        out_specs.append(tok)
        out_shape.append(SDS((T, D), bf16))
    return pl.pallas_call(
        body, grid=(nf,), in_specs=in_specs, out_specs=out_specs, out_shape=tuple(out_shape), scratch_shapes=scratch,
        compiler_params=_cparams(("arbitrary",), VMEM_LIMIT_V7X), name=name)(*args)


def _ffn_bwd(dob, h, gate, up, w, norm, name):
    x, g, dres = norm
    T = h.shape[0]
    _, tf = _ffn_tiles(T)
    nf = F // tf
    tm = FFN_ROW_CHUNK
    nchunk = T // tm

    def body(do_hbm, h_hbm, g_ref, u_ref, w_ref, x_hbm, gain_ref, dr_hbm, dx_hbm, dxb_hbm, dg_ref, dw_ref,
             do_v, h_v, dh_acc, dgu_s, act_s, xbuf, rbuf, obuf, obb, sems, in_sems, out_sems):
        fi = pl.program_id(0)

        @pl.when(fi == 0)
        def _():
            loads = [pltpu.make_async_copy(do_hbm, do_v, sems.at[0]), pltpu.make_async_copy(h_hbm, h_v, sems.at[1])]
            for cp in loads:
                cp.start()
            dh_acc[...] = jnp.zeros_like(dh_acc)
            for cp in loads:
                cp.wait()

        def loads(i):
            s, rows = i % 2, pl.ds(i * tm, tm)
            return [pltpu.make_async_copy(x_hbm.at[rows, :], xbuf.at[s], in_sems.at[2 * s]),
                    pltpu.make_async_copy(dr_hbm.at[rows, :], rbuf.at[s], in_sems.at[2 * s + 1])]

        def stores(i):
            s, rows = i % 2, pl.ds(i * tm, tm)
            return [pltpu.make_async_copy(obuf.at[s], dx_hbm.at[rows, :], out_sems.at[2 * s]),
                    pltpu.make_async_copy(obb.at[s], dxb_hbm.at[rows, :], out_sems.at[2 * s + 1])]

        def norm_rows(i, dg):
            s = i % 2
            for cp in loads(i):
                cp.wait()
            if i >= 2:
                for cp in stores(i - 2):
                    cp.wait()
            xv = xbuf[s]
            rstd = lax.rsqrt(jnp.mean(xv * xv, axis=-1, keepdims=True) + EPS)
            xh = xv * rstd
            dhv = dh_acc[i * tm:(i + 1) * tm, :]
            dxh = dhv * gain_ref[...]
            dx = rbuf[s] + rstd * (dxh - xh * jnp.mean(dxh * xh, axis=-1, keepdims=True))
            obuf[s] = dx
            obb[s] = dx.astype(bf16)
            for cp in stores(i):
                cp.start()
            if i + 2 < nchunk:
                for cp in loads(i + 2):
                    cp.start()
            return dg + jnp.sum(dhv * xh, axis=0, keepdims=True)

        def tile(last):
            wgu = w_ref[0:2].reshape(2 * tf, D)
            dg = jnp.zeros((1, D), f32)
            if last:
                for i in range(min(2, nchunk)):
                    for cp in loads(i):
                        cp.start()
            for i in range(nchunk):
                rows = slice(i * tm, (i + 1) * tm)
                dov = do_v[rows, :]
                gv = g_ref[0, rows, :].astype(f32)
                uv = u_ref[0, rows, :].astype(f32)
                sg = _sigmoid(gv)
                sil = gv * sg
                dact = _nt(dov, w_ref[2])
                dup = dact * sil
                dgate = dact * uv * (sg * (1.0 + gv * (1.0 - sg)))
                dgu = jnp.concatenate([dgate.astype(bf16), dup.astype(bf16)], axis=1)
                dgu_s[rows, :] = dgu
                act_s[rows, :] = (sil * uv).astype(bf16)
                dh_acc[rows, :] += _nn(dgu, wgu)
                if last:
                    dg = norm_rows(i, dg)
            dw_ref[0:2] = _tn(dgu_s[...], h_v[...]).reshape(2, tf, D).astype(bf16)
            dw_ref[2] = _tn(act_s[...], do_v[...]).astype(bf16)
            if last:
                dg_ref[...] = dg
                for i in range(max(nchunk - 2, 0), nchunk):
                    for cp in stores(i):
                        cp.wait()

        @pl.when(fi < nf - 1)
        def _():
            tile(False)

        @pl.when(fi == nf - 1)
        def _():
            tile(True)

    act_spec = pl.BlockSpec((1, T, tf), lambda f: (f, 0, 0))
    wspec = pl.BlockSpec((3, tf, D), lambda f: (0, f, 0))
    vec = pl.BlockSpec((1, D), lambda f: (0, 0))
    hbm = pl.BlockSpec(memory_space=pl.ANY)
    return pl.pallas_call(
        body, grid=(nf,),
        in_specs=[hbm, hbm, act_spec, act_spec, wspec, hbm, vec, hbm],
        out_specs=[hbm, hbm, vec, wspec],
        out_shape=(SDS((T, D), f32), SDS((T, D), bf16), SDS((1, D), f32), SDS((3, F, D), bf16)),
        scratch_shapes=[pltpu.VMEM((T, D), bf16), pltpu.VMEM((T, D), bf16), pltpu.VMEM((T, D), f32),
                        pltpu.VMEM((T, 2 * tf), bf16), pltpu.VMEM((T, tf), bf16),
                        pltpu.VMEM((2, tm, D), f32), pltpu.VMEM((2, tm, D), f32), pltpu.VMEM((2, tm, D), f32),
                        pltpu.VMEM((2, tm, D), bf16),
                        pltpu.SemaphoreType.DMA((2,)), pltpu.SemaphoreType.DMA((4,)), pltpu.SemaphoreType.DMA((4,))],
        compiler_params=_cparams(("arbitrary",), VMEM_LIMIT_V7X), name=name)(dob, h, gate, up, w, x, g, dres)


def _in_proj_fwd(h, wint, name):
    T = h.shape[0]
    tm = min(512, T)

    def body(h_ref, w_ref, z_ref):
        z_ref[...] = _nt(h_ref[...], w_ref[...])

    return pl.pallas_call(
        body, grid=(T // tm,),
        in_specs=[pl.BlockSpec((tm, D), lambda i: (i, 0)), pl.BlockSpec((DIN, D), lambda i: (0, 0))],
        out_specs=pl.BlockSpec((tm, DIN), lambda i: (i, 0)),
        out_shape=SDS((T, DIN), f32), name=name)(h, wint)


def _in_proj_bwd(dz, wint, h, name):
    T = h.shape[0]
    tm = min(512, T)
    nt = T // tm

    def body(dz_ref, w_ref, h_ref, dh_ref, dw_ref, acc):
        i = pl.program_id(0)
        dzb = dz_ref[...].astype(bf16)
        dh_ref[...] = _nn(dzb, w_ref[...])
        part = _tn(dzb, h_ref[...])

        @pl.when(i == 0)
        def _():
            acc[...] = part

        @pl.when(i > 0)
        def _():
            acc[...] += part

        @pl.when(i == nt - 1)
        def _():
            dw_ref[...] = acc[...].astype(bf16)

    wspec = pl.BlockSpec((DIN, D), lambda i: (0, 0))
    return pl.pallas_call(
        body, grid=(nt,),
        in_specs=[pl.BlockSpec((tm, DIN), lambda i: (i, 0)), wspec, pl.BlockSpec((tm, D), lambda i: (i, 0))],
        out_specs=[pl.BlockSpec((tm, D), lambda i: (i, 0)), wspec],
        out_shape=(SDS((T, D), f32), SDS((DIN, D), bf16)),
        scratch_shapes=[pltpu.VMEM((DIN, D), f32)],
        compiler_params=_cparams(("arbitrary",)), name=name)(dz, wint, h)


def _out_proj_fwd(ymix, wout, x, g, name):
    T = x.shape[0]
    tm = min(512, T)

    def body(y_ref, w_ref, x_ref, g_ref, o_ref, h_ref):
        o = x_ref[...] + _nn(y_ref[...], w_ref[...])
        o_ref[...] = o
        r = lax.rsqrt(jnp.mean(o * o, axis=-1, keepdims=True) + EPS)
        h_ref[...] = (o * r * g_ref[...]).astype(bf16)

    tok = pl.BlockSpec((tm, D), lambda i: (i, 0))
    return pl.pallas_call(
        body, grid=(T // tm,),
        in_specs=[pl.BlockSpec((tm, DMIX), lambda i: (i, 0)), pl.BlockSpec((DMIX, D), lambda i: (0, 0)), tok,
                  pl.BlockSpec((1, D), lambda i: (0, 0))],
        out_specs=[tok, tok], out_shape=(SDS((T, D), f32), SDS((T, D), bf16)), name=name)(ymix, wout, x, g)


def _out_proj_bwd(dxb, wout, ymix, name):
    T = dxb.shape[0]
    tm = min(512, T)
    nt = T // tm

    def body(dx_ref, w_ref, y_ref, dy_ref, dw_ref, acc):
        i = pl.program_id(0)
        dxv = dx_ref[...]
        dy_ref[...] = _nt(dxv, w_ref[...])
        part = _tn(y_ref[...], dxv)

        @pl.when(i == 0)
        def _():
            acc[...] = part

        @pl.when(i > 0)
        def _():
            acc[...] += part

        @pl.when(i == nt - 1)
        def _():
            dw_ref[...] = acc[...].astype(bf16)

    wspec = pl.BlockSpec((DMIX, D), lambda i: (0, 0))
    return pl.pallas_call(
        body, grid=(nt,),
        in_specs=[pl.BlockSpec((tm, D), lambda i: (i, 0)), wspec, pl.BlockSpec((tm, DMIX), lambda i: (i, 0))],
        out_specs=[pl.BlockSpec((tm, DMIX), lambda i: (i, 0)), wspec],
        out_shape=(SDS((T, DMIX), f32), SDS((DMIX, D), bf16)),
        scratch_shapes=[pltpu.VMEM((DMIX, D), f32)],
        compiler_params=_cparams(("arbitrary",)), name=name)(dxb, wout, ymix)


def _t5_bucket_table():
    ql = np.arange(BLK)[:, None]
    kl = np.arange(2 * BLK)[None, :]
    n = np.maximum(ql + BLK - kl, 0)
    max_exact = NBUCK // 2
    large = max_exact + (np.log(np.maximum(n, 1) / max_exact) / np.log(MAX_DISTANCE / max_exact)
                         * (NBUCK - max_exact)).astype(np.int32)
    large = np.minimum(large, NBUCK - 1)
    return np.where(n < max_exact, n, large).astype(np.int32)


def _fill_bias(bk_ref, rb_ref, bias_scr):
    bk = bk_ref[...]
    for h in range(NH):
        def step(b, acc, h=h):
            return acc + jnp.where(bk == b, rb_ref[b, h], 0.0)
        bias_scr[h] = lax.fori_loop(0, NBUCK, step, jnp.zeros((BLK, 2 * BLK), f32))


MIX_SUB = 4


class _Window:
    def __init__(self, zc_ref, zp_ref, n, s):
        self.blk = n * MIX_SUB + s
        self.cur = lambda a, b: zc_ref[s * BLK:(s + 1) * BLK, a:b]
        self.prev = (lambda a, b: zp_ref[:, a:b]) if s == 0 else (lambda a, b: zc_ref[(s - 1) * BLK:s * BLK, a:b])


def _attn_qkv(win, kh, qg, kg):
    kc = DATTN + HD * kh
    vc = DATTN + DKV + HD * kh
    kx = jnp.concatenate([win.prev(kc, kc + HD), win.cur(kc, kc + HD)], axis=0)
    vx = jnp.concatenate([win.prev(vc, vc + HD), win.cur(vc, vc + HD)], axis=0)
    qx = jnp.concatenate([win.cur(HD * (GQA * kh + g), HD * (GQA * kh + g + 1)) for g in range(GQA)], axis=0)
    rq = lax.rsqrt(jnp.mean(qx * qx, axis=-1, keepdims=True) + EPS)
    rk = lax.rsqrt(jnp.mean(kx * kx, axis=-1, keepdims=True) + EPS)
    qhat, khat = qx * rq, kx * rk
    return dict(qhat=qhat, khat=khat, rq=rq, rk=rk, qsb=(qhat * (qg * SCALE)).astype(bf16),
                knb=(khat * kg).astype(bf16), vb=vx.astype(bf16))


def _window_masks(n):
    row = lax.broadcasted_iota(i32, (GQA * BLK, 2 * BLK), 0) & (BLK - 1)
    col = lax.broadcasted_iota(i32, (GQA * BLK, 2 * BLK), 1)
    band = (col > row) & (col <= row + BLK)
    return band & ((col >= BLK) | (n > 0)), band


def _attn_probs(a, kh, sk_ref, bias_scr, mask):
    s = _nt(a["qsb"], a["knb"]) + bias_scr[GQA * kh:GQA * (kh + 1)].reshape(GQA * BLK, 2 * BLK)
    s = jnp.where(mask, s, NEG)
    ridx = lax.broadcasted_iota(i32, (GQA * BLK, 1), 0)
    sink = jnp.full((GQA * BLK, 1), sk_ref[GQA * kh + GQA - 1], f32)
    for g in range(GQA - 2, -1, -1):
        sink = jnp.where(ridx < (g + 1) * BLK, sk_ref[GQA * kh + g], sink)
    m = jnp.maximum(jnp.max(s, axis=-1, keepdims=True), sink)
    e = jnp.exp(s - m)
    den = jnp.sum(e, axis=-1, keepdims=True) + jnp.exp(sink - m)
    return e / den


POOL_STEPS = {2: (1,), 4: (1, 2), 8: (1, 2, 4), 16: (1, 2, 4, 8)}


def _pool_group(win, g, w):
    n = win.blk
    c0 = DATTN + 2 * DKV + PGD * g
    uc = win.cur(c0, c0 + PGD)
    up = jnp.where(n > 0, win.prev(c0, c0 + PGD), 0.0)
    sm = jnp.concatenate([up, uc], axis=0)
    for k in POOL_STEPS[w]:
        sm = sm + pltpu.roll(sm, k, axis=0)
    pos = n * BLK + lax.broadcasted_iota(i32, (BLK, 1), 0) + 1
    cnt = jnp.minimum(pos, w).astype(f32)
    return sm[BLK:2 * BLK] / cnt - uc, cnt


def _mix_fwd(z, qg, kg, sinks, relb, bucket, pool_w, pscale, name):
    T = z.shape[0]
    step_rows = MIX_SUB * BLK
    nsteps = T // step_rows

    def body(zc_ref, zp_ref, qg_ref, kg_ref, sk_ref, rb_ref, bk_ref, pw_ref, ps_ref, y_ref, p_ref, bias_scr, yacc):
        n = pl.program_id(0)

        @pl.when(n == 0)
        def _():
            _fill_bias(bk_ref, rb_ref, bias_scr)

        first_mask, mask = _window_masks(n)
        for s in range(MIX_SUB):
            win = _Window(zc_ref, zp_ref, n, s)
            rows = slice(s * BLK, (s + 1) * BLK)
            for kh in range(NKV):
                a = _attn_qkv(win, kh, qg_ref[...], kg_ref[...])
                pb = _attn_probs(a, kh, sk_ref, bias_scr, first_mask if s == 0 else mask).astype(bf16)
                p_ref[s, GQA * kh:GQA * (kh + 1)] = pb.reshape(GQA, BLK, 2 * BLK)
                o = _nn(pb, a["vb"])
                for g in range(GQA):
                    hc = HD * (GQA * kh + g)
                    yacc[rows, hc:hc + HD] = o[g * BLK:(g + 1) * BLK]
            for g, w in enumerate(POOL_WINDOWS):
                pooled, _ = _pool_group(win, g, w)
                yp = _nn(pooled.astype(bf16), pw_ref[g].astype(bf16)) * ps_ref[:, PGD * g:PGD * (g + 1)]
                yacc[rows, DATTN + PGD * g:DATTN + PGD * (g + 1)] = yp
        y_ref[...] = yacc[...].astype(bf16)

    full = lambda *shape: pl.BlockSpec(shape, lambda n: (0,) * len(shape))
    smem = pl.BlockSpec(memory_space=pltpu.SMEM)
    return pl.pallas_call(
        body, grid=(nsteps,),
        in_specs=[pl.BlockSpec((step_rows, DIN), lambda n: (n, 0)),
                  pl.BlockSpec((BLK, DIN), lambda n: (jnp.maximum(n * MIX_SUB - 1, 0), 0)),
                  full(1, HD), full(1, HD), smem, smem, full(BLK, 2 * BLK),
                  full(len(POOL_WINDOWS), PGD, PGD), full(1, DPOOL)],
        out_specs=[pl.BlockSpec((step_rows, DMIX), lambda n: (n, 0)),
                   pl.BlockSpec((MIX_SUB, NH, BLK, 2 * BLK), lambda n: (n, 0, 0, 0))],
        out_shape=(SDS((T, DMIX), bf16), SDS((T // BLK, NH, BLK, 2 * BLK), bf16)),
        scratch_shapes=[pltpu.VMEM((NH, BLK, 2 * BLK), f32), pltpu.VMEM((step_rows, DMIX), f32)],
        compiler_params=_cparams(("arbitrary",)), name=name)(z, z, qg, kg, sinks, relb, bucket, pool_w, pscale)


def _mix_bwd(z, dy, probs, qg, kg, relb, bucket, pool_w, pscale, name):
    T = z.shape[0]
    step_rows = MIX_SUB * BLK
    nsteps = T // step_rows

    def body(zc_ref, zp_ref, dy_ref, p_ref, qg_ref, kg_ref, bk_ref, pw_ref, ps_ref,
             dz_ref, dqg_ref, dkg_ref, dsk_ref, drb_ref, dpw_ref, dps_ref, dbias_scr):
        n = pl.program_id(0)

        @pl.when(n == 0)
        def _():
            dbias_scr[...] = jnp.zeros_like(dbias_scr)
            dqg_ref[...] = jnp.zeros_like(dqg_ref)
            dkg_ref[...] = jnp.zeros_like(dkg_ref)
            dpw_ref[...] = jnp.zeros_like(dpw_ref)
            dps_ref[...] = jnp.zeros_like(dps_ref)

        qg, kg = qg_ref[...], kg_ref[...]
        for s in range(MIX_SUB):
            win = _Window(zc_ref, zp_ref, n, s)
            blk = win.blk
            rows = pl.ds(pl.multiple_of(blk * BLK, BLK), BLK)
            prow = pl.ds(pl.multiple_of(jnp.maximum(blk - 1, 0) * BLK, BLK), BLK)
            dyr = slice(s * BLK, (s + 1) * BLK)

            def into_prev(fn, s=s):
                if s == 0:
                    pl.when(n > 0)(fn)
                else:
                    fn()

            for kh in range(NKV):
                a = _attn_qkv(win, kh, qg, kg)
                pb = p_ref[s, GQA * kh:GQA * (kh + 1)].reshape(GQA * BLK, 2 * BLK)
                p = pb.astype(f32)
                do = jnp.concatenate([dy_ref[dyr, HD * (GQA * kh + g):HD * (GQA * kh + g + 1)] for g in range(GQA)],
                                     axis=0).astype(bf16)
                dv = _tn(pb, do)
                dp = _nt(do, a["vb"])
                delta = jnp.sum(p * dp, axis=-1, keepdims=True)
                ds = p * (dp - delta)
                for g in range(GQA):
                    dbias_scr[GQA * kh + g] += ds[g * BLK:(g + 1) * BLK]
                dsb = ds.astype(bf16)
                dqn = _nn(dsb, a["knb"]) * SCALE
                dkn = _tn(dsb, a["qsb"])
                qhat, khat = a["qhat"], a["khat"]
                dqg_ref[...] += jnp.sum(dqn * qhat, axis=0, keepdims=True)
                dkg_ref[...] += jnp.sum(dkn * khat, axis=0, keepdims=True)
                dqh = dqn * qg
                dq = a["rq"] * (dqh - qhat * jnp.mean(dqh * qhat, axis=-1, keepdims=True))
                dkh = dkn * kg
                dk = a["rk"] * (dkh - khat * jnp.mean(dkh * khat, axis=-1, keepdims=True))
                kc = DATTN + HD * kh
                vc = DATTN + DKV + HD * kh
                for g in range(GQA):
                    hc = HD * (GQA * kh + g)
                    dz_ref[rows, hc:hc + HD] = dq[g * BLK:(g + 1) * BLK]
                dz_ref[rows, kc:kc + HD] = dk[BLK:2 * BLK]
                dz_ref[rows, vc:vc + HD] = dv[BLK:2 * BLK]

                def kv_prev(dk=dk, dv=dv, kc=kc, vc=vc, prow=prow):
                    dz_ref[prow, kc:kc + HD] += dk[0:BLK]
                    dz_ref[prow, vc:vc + HD] += dv[0:BLK]

                into_prev(kv_prev)

            for g, w in enumerate(POOL_WINDOWS):
                c0 = DATTN + 2 * DKV + PGD * g
                pooled, cnt = _pool_group(win, g, w)
                pb = pooled.astype(bf16)
                wb = pw_ref[g].astype(bf16)
                dyp = dy_ref[dyr, DATTN + PGD * g:DATTN + PGD * (g + 1)]
                ypre = _nn(pb, wb)
                dps_ref[:, PGD * g:PGD * (g + 1)] += jnp.sum(dyp * ypre, axis=0, keepdims=True)
                dyg = (dyp * ps_ref[:, PGD * g:PGD * (g + 1)]).astype(bf16)
                dpw_ref[g] += _tn(pb, dyg)
                dpooled = _nt(dyg, wb)
                due = jnp.concatenate([jnp.zeros((BLK, PGD), f32), dpooled / cnt], axis=0)
                for k in POOL_STEPS[w]:
                    due = due + pltpu.roll(due, 2 * BLK - k, axis=0)
                dz_ref[rows, c0:c0 + PGD] = due[BLK:2 * BLK] - dpooled

                def pool_prev(due=due, c0=c0, prow=prow):
                    dz_ref[prow, c0:c0 + PGD] += due[0:BLK]

                into_prev(pool_prev)

        @pl.when(n == nsteps - 1)
        def _():
            bk = bk_ref[...]
            ri = lax.broadcasted_iota(i32, (NBUCK, NH), 0)
            ci = lax.broadcasted_iota(i32, (NBUCK, NH), 1)

            def step(b, acc):
                for h in range(NH):
                    sel = jnp.where(bk == b, dbias_scr[h], 0.0)
                    tot = jnp.sum(jnp.sum(sel, axis=1, keepdims=True), axis=0, keepdims=True)
                    acc = acc + jnp.where((ri == b) & (ci == h), tot, 0.0)
                return acc

            drb_ref[...] = lax.fori_loop(0, NBUCK, step, jnp.zeros((NBUCK, NH), f32))
            lane = lax.broadcasted_iota(i32, (1, 128), 1)
            dsk = jnp.zeros((1, 128), f32)
            for h in range(NH):
                tot = jnp.sum(jnp.sum(dbias_scr[h], axis=1, keepdims=True), axis=0, keepdims=True)
                dsk = dsk - jnp.where(lane == h, tot, 0.0)
            dsk_ref[...] = dsk

    full = lambda *shape: pl.BlockSpec(shape, lambda n: (0,) * len(shape))
    npg = len(POOL_WINDOWS)
    return pl.pallas_call(
        body, grid=(nsteps,),
        in_specs=[pl.BlockSpec((step_rows, DIN), lambda n: (n, 0)),
                  pl.BlockSpec((BLK, DIN), lambda n: (jnp.maximum(n * MIX_SUB - 1, 0), 0)),
                  pl.BlockSpec((step_rows, DMIX), lambda n: (n, 0)),
                  pl.BlockSpec((MIX_SUB, NH, BLK, 2 * BLK), lambda n: (n, 0, 0, 0)),
                  full(1, HD), full(1, HD), full(BLK, 2 * BLK), full(npg, PGD, PGD), full(1, DPOOL)],
        out_specs=[full(T, DIN), full(1, HD), full(1, HD), full(1, 128), full(NBUCK, NH),
                   full(npg, PGD, PGD), full(1, DPOOL)],
        out_shape=(SDS((T, DIN), f32), SDS((1, HD), f32), SDS((1, HD), f32), SDS((1, 128), f32),
                   SDS((NBUCK, NH), f32), SDS((npg, PGD, PGD), f32), SDS((1, DPOOL), f32)),
        scratch_shapes=[pltpu.VMEM((NH, BLK, 2 * BLK), f32)],
        compiler_params=_cparams(("arbitrary",), VMEM_LIMIT_V7X),
        name=name)(z, z, dy, probs, qg, kg, bucket, pool_w, pscale)


class _LocalWeights:
    def __init__(self, w1, wint, wout, w2):
        self.w1, self.wint, self.wout, self.w2 = w1, wint, wout, w2

    def ffn1(self):
        return self.w1

    def first_norm(self, x, gain):
        return _norm_fwd(x, gain, "norm1_fwd")

    def after_ffn1(self, gain, x1):
        return gain

    def mix(self, after):
        return self.wint, self.wout

    def before_out_proj(self, wout, after):
        return wout

    def ffn2(self, after):
        return self.w2

    def mix_ffn2_grads_ready(self, dwint, dwout, dw2, dh2):
        self.grads_rest = (dwint, dwout, dw2)
        return dh2

    def before_ffn1_bwd(self, dx1b):
        return dx1b


def _local_step(x, target, weights, g1, gm, g3, qg, kg, sinks, relb, pool_w, pscale):
    bucket = jnp.asarray(_t5_bucket_table())
    sk = sinks.reshape(NH)
    w1 = weights.ffn1()
    h1 = weights.first_norm(x, g1)
    x1, gate1, up1, h2 = _ffn_fwd(h1, w1, x, None, gm, "ffn1_fwd")
    gm = weights.after_ffn1(gm, x1)
    wint, wout = weights.mix(h2)
    z = _in_proj_fwd(h2, wint, "in_proj_fwd")
    ymix, probs = _mix_fwd(z, qg, kg, sk, relb, bucket, pool_w, pscale, "mix_fwd")
    wout = weights.before_out_proj(wout, ymix)
    x2, h3 = _out_proj_fwd(ymix, wout, x1, g3, "out_proj_fwd")
    w2 = weights.ffn2(h3)
    dy, gate2, up2, dyb, loss_lanes = _ffn_fwd(h3, w2, x2, target, None, "ffn2_fwd")

    dx2, dx2b, dg3, dw2 = _ffn_bwd(dyb, h3, gate2, up2, w2, (x2, g3, dy), "ffn2_bwd")
    dymix, dwout = _out_proj_bwd(dx2b, wout, ymix, "out_proj_bwd")
    dz, dqg, dkg, dsk, drb, dpw, dps = _mix_bwd(z, dymix, probs, qg, kg, relb, bucket, pool_w, pscale, "mix_bwd")
    dh2, dwint = _in_proj_bwd(dz, wint, h2, "in_proj_bwd")
    dh2 = weights.mix_ffn2_grads_ready(dwint, dwout, dw2, dh2)
    dx1, dx1b, dgm = _norm_bwd(dh2, x1, gm, dx2, 0.5, "norm2_bwd")
    dx1b = weights.before_ffn1_bwd(dx1b)
    gx, _, dg1, dw1 = _ffn_bwd(dx1b, h1, gate1, up1, w1, (x, g1, dx1), "ffn1_bwd")
    small = dict(ffn1_norm=dg1, mix_norm=dgm, ffn2_norm=dg3, pool_scale=dps, q_norm=dqg, k_norm=dkg,
                 attn_sinks=dsk[:, :NH], rel_bias=drb, pool_w=dpw, loss=loss_lanes)
    return gx, (dw1, dwint, dwout, dw2), small


SMALL_NAMES = ("ffn1_norm", "mix_norm", "ffn2_norm", "pool_scale", "q_norm", "k_norm", "attn_sinks", "rel_bias",
               "pool_w", "loss")
SMALL_SHAPES = dict(ffn1_norm=(1, D), mix_norm=(1, D), ffn2_norm=(1, D), pool_scale=(1, DPOOL), q_norm=(1, HD),
                    k_norm=(1, HD), attn_sinks=(1, NH), rel_bias=(NBUCK, NH),
                    pool_w=(1, len(POOL_WINDOWS), PGD, PGD), loss=(1, 128))


def _small_rows(name):
    return -(-int(np.prod(SMALL_SHAPES[name])) // 128)


SMALL_OFF = {}
_r = 0
for _n in SMALL_NAMES:
    SMALL_OFF[_n] = _r
    _r += _small_rows(_n)
SMALL_ROWS = -(-_r // 8) * 8
LOSS_ROW = SMALL_OFF["loss"]


def _pack_small(vals):
    parts = []
    for n in SMALL_NAMES:
        size = _small_rows(n) * 128
        if n in vals:
            flat = vals[n].astype(f32).reshape(-1)
            parts.append(jnp.pad(flat, (0, size - flat.shape[0])))
        else:
            parts.append(jnp.zeros((size,), f32))
    flat = jnp.concatenate(parts)
    flat = jnp.pad(flat, (0, SMALL_ROWS * 128 - flat.shape[0]))
    return flat.reshape(SMALL_ROWS, 128)


def _unpack_small(packed, name):
    size = int(np.prod(SMALL_SHAPES[name]))
    r0 = SMALL_OFF[name]
    return packed[r0:r0 + _small_rows(name)].reshape(-1)[:size].reshape(SMALL_SHAPES[name])


def _position():
    return lax.axis_index("x"), lax.axis_index("y"), lax.axis_index("c")


def _dev_index(x, y, c):
    return 4 * x + 2 * y + c


G1_PIECES, MIX_PIECES, F2_PIECES = (0, 1, 2), (3, 4), (5, 6, 7)


def _group_rows(pieces):
    return sum(PIECE_ROWS[k] for k in pieces)


def _shard_piece(s_ref, k):
    return s_ref.at[pl.ds(PIECE_OFF[k], PIECE_ROWS[k]), :]


def _shard_group(s_ref, pieces):
    return s_ref.at[pl.ds(PIECE_OFF[pieces[0]], _group_rows(pieces)), :]


def _weight_pieces(w1_ref=None, wi_ref=None, wo_ref=None, w2_ref=None):
    arrs = {}
    if w1_ref is not None:
        arrs.update({0: w1_ref.at[0], 1: w1_ref.at[1], 2: w1_ref.at[2]})
    if wi_ref is not None:
        arrs[3] = wi_ref
    if wo_ref is not None:
        arrs[4] = wo_ref
    if w2_ref is not None:
        arrs.update({5: w2_ref.at[0], 6: w2_ref.at[1], 7: w2_ref.at[2]})
    return arrs


def _block_rows(arrs, k, dev):
    r = PIECE_ROWS[k]
    return arrs[k].at[pl.ds(pl.multiple_of(_dev_index(*dev) * r, 16), r), :]


NORM_ROWS = 512


def _all_gather_ffn1(shard, x, gain):
    pieces = G1_PIECES
    rest_pieces = MIX_PIECES + F2_PIECES
    half = FS // 2
    T = x.shape[0]
    SIB, X0, X1, Y0, Y1, RELAY_Y, RELAY_X, ON_X, ON_Y, ON_D0, ON_D1 = range(11)

    def body(s_ref, x_ref, g_ref, w1_ref, h_ref, wi_ref, wo_ref, w2_ref, xbuf, hbuf, rest_buf,
             send_sems, recv_sems, local_sem, norm_sems):
        x, y, c = _position()
        me, sib = (x, y, c), (x, y, 1 - c)
        xn, yn, dg = (1 - x, y, c), (x, 1 - y, c), (1 - x, 1 - y, c)
        arrs = _weight_pieces(w1_ref=w1_ref)

        def place_rest():
            rest = _weight_pieces(wi_ref=wi_ref, wo_ref=wo_ref, w2_ref=w2_ref)
            grp = _shard_group(s_ref, rest_pieces)
            load = pltpu.make_async_copy(grp, rest_buf, norm_sems.at[0])
            load.start()
            load.wait()
            base = PIECE_OFF[rest_pieces[0]]
            for k in rest_pieces:
                pltpu.make_async_copy(rest_buf.at[pl.ds(PIECE_OFF[k] - base, PIECE_ROWS[k]), :],
                                      _block_rows(rest, k, me), norm_sems.at[1]).start()
            pltpu.make_async_copy(grp, rest_buf, norm_sems.at[1]).wait()

        def first_norm():
            for r in range(0, T, NORM_ROWS):
                load = pltpu.make_async_copy(x_ref.at[pl.ds(r, NORM_ROWS), :], xbuf, norm_sems.at[0])
                load.start()
                load.wait()
                xv = xbuf[...]
                rs = lax.rsqrt(jnp.mean(xv * xv, axis=-1, keepdims=True) + EPS)
                hbuf[...] = (xv * rs * g_ref[...]).astype(bf16)
                store = pltpu.make_async_copy(hbuf, h_ref.at[pl.ds(r, NORM_ROWS), :], norm_sems.at[1])
                store.start()
                store.wait()

        def rows_of(k, block, hf):
            r = PIECE_ROWS[k]
            start, size = (0, r) if hf is None else (hf * half, half)
            return arrs[k].at[pl.ds(pl.multiple_of(_dev_index(*block) * r + start, 16), size), :]

        def copies(rel, block, hf, to, from_shard=False):
            def src(k):
                if not from_shard:
                    return rows_of(k, block, hf)
                start, size = (0, PIECE_ROWS[k]) if hf is None else (hf * half, half)
                return s_ref.at[pl.ds(PIECE_OFF[k] + start, size), :]
            return [pltpu.make_async_remote_copy(
                src_ref=src(k), dst_ref=rows_of(k, block, hf), send_sem=send_sems.at[rel], recv_sem=recv_sems.at[rel],
                device_id=to, device_id_type=MESH) for k in pieces]

        def waiter(rel, hf):
            nrows = len(pieces) * (FS if hf is None else half)
            grp = s_ref.at[pl.ds(0, nrows), :]
            return pltpu.make_async_remote_copy(src_ref=grp, dst_ref=grp, send_sem=send_sems.at[rel],
                                                recv_sem=recv_sems.at[rel], device_id=me, device_id_type=MESH)

        def start(cps):
            for cp in cps:
                cp.start()

        mine = [pltpu.make_async_copy(_shard_piece(s_ref, k), _block_rows(arrs, k, me), local_sem) for k in pieces]
        start(mine)
        start(copies(SIB, me, None, sib, True))
        start(copies(X0, me, 0, xn, True))
        start(copies(Y1, me, 1, yn, True))
        start(copies(X1, me, 1, xn, True))
        start(copies(Y0, me, 0, yn, True))
        first_norm()
        place_rest()
        waiter(X0, 0).wait_recv()
        start(copies(RELAY_Y, xn, 0, yn))
        waiter(Y1, 1).wait_recv()
        start(copies(RELAY_X, yn, 1, xn))
        waiter(X1, 1).wait_recv()
        start(copies(ON_X, xn, None, sib))
        waiter(Y0, 0).wait_recv()
        start(copies(ON_Y, yn, None, sib))
        waiter(RELAY_Y, 0).wait_recv()
        start(copies(ON_D0, dg, 0, sib))
        waiter(RELAY_X, 1).wait_recv()
        start(copies(ON_D1, dg, 1, sib))
        waiter(SIB, None).wait_recv()
        waiter(ON_X, None).wait_recv()
        waiter(ON_Y, None).wait_recv()
        waiter(ON_D0, 0).wait_recv()
        waiter(ON_D1, 1).wait_recv()
        for rel, hf in ((SIB, None), (X0, 0), (X1, 1), (Y0, 0), (Y1, 1), (RELAY_Y, 0), (RELAY_X, 1),
                        (ON_X, None), (ON_Y, None), (ON_D0, 0), (ON_D1, 1)):
            waiter(rel, hf).wait_send()
        grp = _shard_group(s_ref, pieces)
        pltpu.make_async_copy(grp, grp, local_sem).wait()

    hbm = pl.BlockSpec(memory_space=pl.ANY)
    return pl.pallas_call(
        body, in_specs=[hbm, hbm, pl.BlockSpec(memory_space=pltpu.VMEM)], out_specs=[hbm] * 5,
        out_shape=(SDS((3, F, D), bf16), SDS((T, D), bf16),
                   SDS((DIN, D), bf16), SDS((DMIX, D), bf16), SDS((3, F, D), bf16)),
        scratch_shapes=[pltpu.VMEM((NORM_ROWS, D), f32), pltpu.VMEM((NORM_ROWS, D), bf16),
                        pltpu.VMEM((_group_rows(rest_pieces), D), bf16),
                        pltpu.SemaphoreType.DMA((11,)), pltpu.SemaphoreType.DMA((11,)), pltpu.SemaphoreType.DMA,
                        pltpu.SemaphoreType.DMA((2,))],
        compiler_params=pltpu.CompilerParams(has_side_effects=True),
        name="all_gather_ffn1")(shard, x, gain)


HBM_SPEC = pl.BlockSpec(memory_space=pltpu.HBM)
SEM_SPEC = pl.BlockSpec(memory_space=pltpu.SEMAPHORE)
ANY_SPEC = pl.BlockSpec(memory_space=pl.ANY)
SPLIT_EFFECT = pltpu.SideEffectType.DATAFLOW_SIDE_EFFECTING


def _in_hbm(a):
    return pltpu.with_memory_space_constraint(a, pltpu.HBM)


def _hbm_like(a):
    return pltpu.HBM(a.shape, a.dtype)


def _gather_rest_start(shard, wi, wo, w2, w1):
    def body(s_ref, wi_ref, wo_ref, w2_ref, w1_ref,
             ssem_m, rsem_m0, rsem_m, ssem_f, rsem_f0, rsem_f, s_o, wi_o, wo_o, w2_o, w1_o):
        x, y, c = _position()
        me, sib = (x, y, c), (x, y, 1 - c)
        chips = [(1 - x, y), (x, 1 - y), (1 - x, 1 - y)]
        arrs = _weight_pieces(wi_ref=wi_ref, wo_ref=wo_ref, w2_ref=w2_ref)
        for pieces, ssem, rsem0, rsem in ((MIX_PIECES, ssem_m, rsem_m0, rsem_m), (F2_PIECES, ssem_f, rsem_f0, rsem_f)):
            for p in pieces:
                pltpu.make_async_remote_copy(
                    src_ref=_shard_piece(s_ref, p), dst_ref=_block_rows(arrs, p, me), send_sem=ssem.at[0],
                    recv_sem=rsem0, device_id=sib, device_id_type=MESH).start()
            for j, chip in enumerate(chips):
                for p in pieces:
                    pltpu.make_async_remote_copy(
                        src_ref=_shard_piece(s_ref, p), dst_ref=_block_rows(arrs, p, me), send_sem=ssem.at[1 + j],
                        recv_sem=rsem.at[j], device_id=(*chip, c), device_id_type=MESH).start()

    dma = pltpu.SemaphoreType.DMA
    return pl.pallas_call(
        body, name="gather_rest_start",
        out_shape=(dma((4,)), dma(()), dma((3,)), dma((4,)), dma(()), dma((3,)),
                   _hbm_like(shard), _hbm_like(wi), _hbm_like(wo), _hbm_like(w2), _hbm_like(w1)),
        in_specs=(HBM_SPEC,) * 5, out_specs=(SEM_SPEC,) * 6 + (HBM_SPEC,) * 5,
        input_output_aliases={0: 6, 1: 7, 2: 8, 3: 9, 4: 10},
        compiler_params=pltpu.CompilerParams(has_side_effects=SPLIT_EFFECT),
    )(_in_hbm(shard), _in_hbm(wi), _in_hbm(wo), _in_hbm(w2), _in_hbm(w1))


def _gather_mix_pass_on(rsem_m, wi, wo, thru, after):
    def body(wi_ref, wo_ref, thru_ref, rsem, after_ref, fsend, frecv, wi_o, wo_o, thru_o):
        x, y, c = _position()
        sib = (x, y, 1 - c)
        arrs = _weight_pieces(wi_ref=wi_ref, wo_ref=wo_ref)
        both = wi_ref.at[pl.ds(0, _group_rows(MIX_PIECES)), :]
        for j, chip in enumerate([(1 - x, y), (x, 1 - y), (1 - x, 1 - y)]):
            pltpu.make_async_remote_copy(src_ref=both, dst_ref=both, send_sem=fsend.at[j], recv_sem=rsem.at[j],
                                         device_id=(x, y, c), device_id_type=MESH).wait_recv()
            for p in MIX_PIECES:
                rows = _block_rows(arrs, p, (*chip, c))
                pltpu.make_async_remote_copy(src_ref=rows, dst_ref=rows, send_sem=fsend.at[j], recv_sem=frecv.at[j],
                                             device_id=sib, device_id_type=MESH).start()

    dma = pltpu.SemaphoreType.DMA
    return pl.pallas_call(
        body, name="gather_mix_pass_on",
        out_shape=(dma((3,)), dma((3,)), _hbm_like(wi), _hbm_like(wo), _hbm_like(thru)),
        in_specs=(HBM_SPEC, HBM_SPEC, HBM_SPEC, SEM_SPEC, ANY_SPEC), out_specs=(SEM_SPEC, SEM_SPEC) + (HBM_SPEC,) * 3,
        input_output_aliases={0: 2, 1: 3, 2: 4},
        compiler_params=pltpu.CompilerParams(has_side_effects=SPLIT_EFFECT),
    )(wi, wo, _in_hbm(thru), rsem_m, after)


def _gather_mix_wait(ssem_m, rsem_m0, fsend, frecv, shard, wi, wo, after):
    def body(s_ref, wi_ref, wo_ref, ssem, rsem0, fs, fr, after_ref, s_o, wi_o, wo_o):
        x, y, c = _position()
        grp = _shard_group(s_ref, MIX_PIECES)

        def waiter(send_sem, recv_sem):
            return pltpu.make_async_remote_copy(src_ref=grp, dst_ref=grp, send_sem=send_sem, recv_sem=recv_sem,
                                                device_id=(x, y, c), device_id_type=MESH)

        waiter(ssem.at[0], rsem0).wait_recv()
        for j in range(3):
            waiter(fs.at[j], fr.at[j]).wait_recv()
        for rel in range(4):
            waiter(ssem.at[rel], rsem0).wait_send()
        for j in range(3):
            waiter(fs.at[j], fr.at[j]).wait_send()

    return pl.pallas_call(
        body, name="gather_mix_wait", out_shape=(_hbm_like(shard), _hbm_like(wi), _hbm_like(wo)),
        in_specs=(HBM_SPEC,) * 3 + (SEM_SPEC,) * 4 + (ANY_SPEC,), out_specs=(HBM_SPEC,) * 3,
        input_output_aliases={0: 0, 1: 1, 2: 2},
        compiler_params=pltpu.CompilerParams(has_side_effects=SPLIT_EFFECT),
    )(shard, wi, wo, ssem_m, rsem_m0, fsend, frecv, after)


def _gather_ffn2_pass_on(rsem_f, w2, wo, after):
    def body(w2_ref, wo_ref, rsem, after_ref, fsend, frecv, w2_o, wo_o):
        x, y, c = _position()
        sib = (x, y, 1 - c)
        chips = [(1 - x, y), (x, 1 - y), (1 - x, 1 - y)]
        arrs = _weight_pieces(w2_ref=w2_ref)
        three = w2_ref.at[0, pl.ds(0, _group_rows(F2_PIECES)), :]
        for j, chip in enumerate(chips):
            pltpu.make_async_remote_copy(src_ref=three, dst_ref=three, send_sem=fsend.at[j], recv_sem=rsem.at[j],
                                         device_id=(x, y, c), device_id_type=MESH).wait_recv()
            for p in F2_PIECES:
                rows = _block_rows(arrs, p, (*chip, c))
                pltpu.make_async_remote_copy(src_ref=rows, dst_ref=rows, send_sem=fsend.at[j], recv_sem=frecv.at[j],
                                             device_id=sib, device_id_type=MESH).start()

    dma = pltpu.SemaphoreType.DMA
    return pl.pallas_call(
        body, name="gather_ffn2_pass_on", out_shape=(dma((3,)), dma((3,)), _hbm_like(w2), _hbm_like(wo)),
        in_specs=(HBM_SPEC, HBM_SPEC, SEM_SPEC, ANY_SPEC), out_specs=(SEM_SPEC, SEM_SPEC, HBM_SPEC, HBM_SPEC),
        input_output_aliases={0: 2, 1: 3},
        compiler_params=pltpu.CompilerParams(has_side_effects=SPLIT_EFFECT),
    )(w2, wo, rsem_f, after)


def _gather_ffn2_wait(ssem_f, rsem_f0, fsend, frecv, shard, w2, after):
    def body(s_ref, w2_ref, ssem, rsem0, fs, fr, after_ref, w2_o):
        x, y, c = _position()
        grp = _shard_group(s_ref, F2_PIECES)

        def waiter(send_sem, recv_sem):
            return pltpu.make_async_remote_copy(src_ref=grp, dst_ref=grp, send_sem=send_sem, recv_sem=recv_sem,
                                                device_id=(x, y, c), device_id_type=MESH)

        waiter(ssem.at[0], rsem0).wait_recv()
        for j in range(3):
            waiter(fs.at[j], fr.at[j]).wait_recv()
        for rel in range(4):
            waiter(ssem.at[rel], rsem0).wait_send()
        for j in range(3):
            waiter(fs.at[j], fr.at[j]).wait_send()

    return pl.pallas_call(
        body, name="gather_ffn2_wait", out_shape=_hbm_like(w2),
        in_specs=(HBM_SPEC, HBM_SPEC, SEM_SPEC, SEM_SPEC, SEM_SPEC, SEM_SPEC, ANY_SPEC), out_specs=HBM_SPEC,
        input_output_aliases={1: 0},
        compiler_params=pltpu.CompilerParams(has_side_effects=SPLIT_EFFECT),
    )(shard, w2, ssem_f, rsem_f0, fsend, frecv, after)


class _GatheredWeights(_LocalWeights):
    def __init__(self, shard, x, gain1):
        w1, self.h1, wi, wo, w2 = _all_gather_ffn1(shard, x, gain1)
        (self.ssem_m, self.rsem_m0, self.rsem_m, self.ssem_f, self.rsem_f0, self.rsem_f,
         self.shard, self.wi, self.wo, self.w2_part, self.w1) = _gather_rest_start(shard, wi, wo, w2, w1)

    def first_norm(self, x, gain):
        return self.h1

    def after_ffn1(self, gain, x1):
        self.fsend_m, self.frecv_m, self.wi, self.wo, gain = _gather_mix_pass_on(self.rsem_m, self.wi, self.wo, gain, x1)
        return gain

    def mix(self, after):
        self.shard, wint, wout = _gather_mix_wait(self.ssem_m, self.rsem_m0, self.fsend_m, self.frecv_m, self.shard,
                                                  self.wi, self.wo, after)
        return wint, wout

    def before_out_proj(self, wout, after):
        self.fsend, self.frecv, self.w2_part, wout = _gather_ffn2_pass_on(self.rsem_f, self.w2_part, wout, after)
        return wout

    def ffn2(self, after):
        return _gather_ffn2_wait(self.ssem_f, self.rsem_f0, self.fsend, self.frecv, self.shard, self.w2_part, after)

    def mix_ffn2_grads_ready(self, dwint, dwout, dw2, dh2):
        rx1 = lax.empty((4, RSA_ROWS, D), bf16)
        self.sa, self.ra, dwint, dwout, dw2, rx1, dh2 = _rsa_level1_start(dwint, dwout, dw2, rx1, dh2)
        self.level1 = (dwint, dwout, dw2, rx1)
        return dh2

    def before_ffn1_bwd(self, dx1b):
        dwint, dwout, dw2, rx1 = _rsa_level1_wait(self.sa, self.ra, *self.level1, dx1b)
        tx, self.acc = _rsa_chip_sums(dwint, dwout, dw2, rx1)
        rx2 = lax.empty((3, RSA_ROWS, D), bf16)
        self.sb, self.rb, self.tx, self.rx2, dx1b = _rsa_level2_start(tx, rx2, dx1b)
        return dx1b

    def mix_ffn2_grads_parts(self, after):
        rx2 = _rsa_level2_wait(self.sb, self.rb, self.tx, self.rx2, after)
        return self.acc, rx2


def _reduce_scatter_ffn1_head(dw1, small_packed):
    pieces = G1_PIECES
    half = FS // 2
    hrows = len(pieces) * half
    nrows = 2 * hrows
    X_RELAY, Y_RELAY = range(2)

    def body(d1_ref, p_ref, forx_ref, fory_ref, own_ref, rx1_ref, relx_ref, rely_ref, tot_ref,
             own_buf, rx_buf, tx1, tx2, tx3, acc, sa, ra, sb, rb, lsem, pair, chips, small_send, small_recv):
        x, y, c = _position()
        me, sib = (x, y, c), (x, y, 1 - c)
        xn, yn = (1 - x, y, c), (x, 1 - y, c)
        rel_chips = [(x, y), (1 - x, y), (x, 1 - y), (1 - x, 1 - y)]
        srcs = _weight_pieces(w1_ref=d1_ref)

        my_chip = 2 * x + y
        pair[c] = p_ref[...]
        swap = pltpu.make_async_remote_copy(
            src_ref=p_ref, dst_ref=pair.at[c], send_sem=small_send.at[0], recv_sem=small_recv.at[0],
            device_id=sib, device_id_type=MESH)
        swap.start()
        small = [pltpu.make_async_remote_copy(
            src_ref=chips.at[my_chip], dst_ref=chips.at[my_chip], send_sem=small_send.at[j], recv_sem=small_recv.at[j],
            device_id=(*rel_chips[j], c), device_id_type=MESH) for j in (1, 2, 3)]

        def part(k, dev, hf):
            r = PIECE_ROWS[k]
            return srcs[k].at[pl.ds(pl.multiple_of(_dev_index(*dev) * r + hf * half, 16), half), :]

        def slot(ref, k, hf):
            return ref.at[pl.ds(hf * hrows + k * half, half), :]

        halves = [(k, hf) for hf in (0, 1) for k in pieces]

        for j in (3, 1, 2, 0):
            for k, hf in halves:
                pltpu.make_async_remote_copy(
                    src_ref=part(k, (*rel_chips[j], 1 - c), hf), dst_ref=slot(rx1_ref.at[j], k, hf),
                    send_sem=sa.at[j], recv_sem=ra.at[j], device_id=sib, device_id_type=MESH).start()

        def wait_a(j):
            return pltpu.make_async_remote_copy(src_ref=rx1_ref.at[j], dst_ref=rx1_ref.at[j], send_sem=sa.at[j],
                                                recv_sem=ra.at[j], device_id=me, device_id_type=MESH)

        def ici(rel, src, dst, to):
            return pltpu.make_async_remote_copy(src_ref=src, dst_ref=dst, send_sem=sb.at[rel], recv_sem=rb.at[rel],
                                                device_id=to, device_id_type=MESH)

        first, second = pl.ds(0, hrows), pl.ds(hrows, hrows)
        sends = {
            X_RELAY: ici(X_RELAY, tx3.at[first, :], relx_ref, xn),
            Y_RELAY: ici(Y_RELAY, tx3.at[second, :], rely_ref, yn),
        }

        swap.wait_recv()
        chips[my_chip] = pair[0] + pair[1]
        for cp in small:
            cp.start()

        def chip_sum(j, dst):
            loads = [pltpu.make_async_copy(part(k, (*rel_chips[j], c), hf), slot(own_buf, k, hf), lsem.at[0])
                     for k, hf in halves]
            for cp in loads:
                cp.start()
            wait_a(j).wait_recv()
            got = pltpu.make_async_copy(rx1_ref.at[j], rx_buf, lsem.at[1])
            got.start()
            pltpu.make_async_copy(rx_buf, rx_buf, lsem.at[0]).wait()
            got.wait()

            def add(i, carry):
                rows = pl.ds(pl.multiple_of(i * half, 16), half)
                tot = own_buf[rows, :].astype(f32) + rx_buf[rows, :].astype(f32)
                dst[rows, :] = tot.astype(dst.dtype)
                return carry

            lax.fori_loop(0, nrows // half, add, 0)

        def add_landed(landed, dst, rows0, nrows_):
            got = pltpu.make_async_copy(landed, rx_buf.at[pl.ds(0, nrows_), :], lsem.at[1])
            got.start()
            got.wait()

            def add(i, carry):
                src_rows = pl.ds(pl.multiple_of(i * half, 16), half)
                dst_rows = pl.ds(pl.multiple_of(rows0 + i * half, 16), half)
                dst[dst_rows, :] = (dst[dst_rows, :].astype(f32) + rx_buf[src_rows, :].astype(f32)).astype(dst.dtype)
                return carry

            lax.fori_loop(0, nrows_ // half, add, 0)

        chip_sum(3, tx3)
        sends[X_RELAY].start()
        sends[Y_RELAY].start()
        chip_sum(1, tx1)
        chip_sum(2, tx2)
        chip_sum(0, acc)
        own_out = pltpu.make_async_copy(acc, own_ref, lsem.at[0])
        own_out.start()
        sends[X_RELAY].wait_recv()
        add_landed(relx_ref, tx2, 0, hrows)
        sends[Y_RELAY].wait_recv()
        add_landed(rely_ref, tx1, hrows, hrows)
        own_out.wait()
        outs = [pltpu.make_async_copy(tx1, forx_ref, lsem.at[0]), pltpu.make_async_copy(tx2, fory_ref, lsem.at[1])]
        for cp in outs:
            cp.start()
        for cp in outs:
            cp.wait()
        for cp in small:
            cp.wait_recv()
        tot = (chips[0] + chips[1]) + (chips[2] + chips[3])
        tot_ref[...] = tot
        loss = jnp.sum(tot[LOSS_ROW:LOSS_ROW + 1, :], axis=-1, keepdims=True)
        tot_ref[LOSS_ROW:LOSS_ROW + 1, :] = jnp.broadcast_to(loss, (1, 128))
        for j in range(4):
            wait_a(j).wait_send()
        for cp in sends.values():
            cp.wait_send()
        swap.wait_send()
        for cp in small:
            cp.wait_send()

    hbm = pl.BlockSpec(memory_space=pl.ANY)
    vm = pl.BlockSpec(memory_space=pltpu.VMEM)
    outs = pl.pallas_call(
        body, in_specs=[hbm, vm], out_specs=[hbm] * 6 + [vm],
        out_shape=(SDS((nrows, D), bf16), SDS((nrows, D), bf16), SDS((nrows, D), f32), SDS((4, nrows, D), bf16),
                   SDS((hrows, D), bf16), SDS((hrows, D), bf16), SDS((SMALL_ROWS, 128), f32)),
        scratch_shapes=[pltpu.VMEM((nrows, D), bf16), pltpu.VMEM((nrows, D), bf16),
                        pltpu.VMEM((nrows, D), bf16), pltpu.VMEM((nrows, D), bf16), pltpu.VMEM((nrows, D), bf16),
                        pltpu.VMEM((nrows, D), f32),
                        pltpu.SemaphoreType.DMA((4,)), pltpu.SemaphoreType.DMA((4,)),
                        pltpu.SemaphoreType.DMA((2,)), pltpu.SemaphoreType.DMA((2,)), pltpu.SemaphoreType.DMA((2,)),
                        pltpu.VMEM((2, SMALL_ROWS, 128), f32), pltpu.VMEM((4, SMALL_ROWS, 128), f32),
                        pltpu.SemaphoreType.DMA((4,)), pltpu.SemaphoreType.DMA((4,))],
        compiler_params=pltpu.CompilerParams(has_side_effects=True, vmem_limit_bytes=VMEM_LIMIT_V7X),
        name="reduce_scatter_ffn1_head")(dw1, small_packed)
    return outs[0], outs[1], outs[2], outs[-1]


def _rs1_tail_start(for_x, for_y, from_x, from_y, thru):
    def body(fx_ref, fy_ref, lx_ref, ly_ref, thru_ref, ssem, rsem, fx_o, fy_o, lx_o, ly_o, thru_o):
        x, y, c = _position()
        pltpu.make_async_remote_copy(src_ref=fx_ref, dst_ref=lx_ref, send_sem=ssem.at[0], recv_sem=rsem.at[0],
                                     device_id=(1 - x, y, c), device_id_type=MESH).start()
        pltpu.make_async_remote_copy(src_ref=fy_ref, dst_ref=ly_ref, send_sem=ssem.at[1], recv_sem=rsem.at[1],
                                     device_id=(x, 1 - y, c), device_id_type=MESH).start()

    dma = pltpu.SemaphoreType.DMA
    arrs = (for_x, for_y, from_x, from_y, thru)
    return pl.pallas_call(
        body, name="rs1_tail_start", out_shape=(dma((2,)), dma((2,))) + tuple(_hbm_like(a) for a in arrs),
        in_specs=(HBM_SPEC,) * 5, out_specs=(SEM_SPEC,) * 2 + (HBM_SPEC,) * 5,
        input_output_aliases={0: 2, 1: 3, 2: 4, 3: 5, 4: 6},
        compiler_params=pltpu.CompilerParams(has_side_effects=SPLIT_EFFECT),
    )(*[_in_hbm(a) for a in arrs])


def _rs1_tail_wait(ssem, rsem, for_x, for_y, from_x, from_y, after):
    def body(fx_ref, fy_ref, lx_ref, ly_ref, ssem_ref, rsem_ref, after_ref, lx_o, ly_o):
        x, y, c = _position()
        for j, (src, dst) in enumerate(((fx_ref, lx_ref), (fy_ref, ly_ref))):
            d = pltpu.make_async_remote_copy(src_ref=src, dst_ref=dst, send_sem=ssem_ref.at[j], recv_sem=rsem_ref.at[j],
                                             device_id=(x, y, c), device_id_type=MESH)
            d.wait_recv()
            d.wait_send()

    return pl.pallas_call(
        body, name="rs1_tail_wait", out_shape=(_hbm_like(from_x), _hbm_like(from_y)),
        in_specs=(HBM_SPEC,) * 4 + (SEM_SPEC, SEM_SPEC, ANY_SPEC), out_specs=(HBM_SPEC, HBM_SPEC),
        input_output_aliases={2: 0, 3: 1},
        compiler_params=pltpu.CompilerParams(has_side_effects=SPLIT_EFFECT),
    )(for_x, for_y, from_x, from_y, ssem, rsem, after)


RSA_PIECES = MIX_PIECES + F2_PIECES
RSA_ROWS = _group_rows(RSA_PIECES)
RSA_OFF = {k: PIECE_OFF[k] - PIECE_OFF[RSA_PIECES[0]] for k in RSA_PIECES}
RSA_BLOCK = 192


def _rsa_rows(ref, k):
    return ref.at[pl.ds(RSA_OFF[k], PIECE_ROWS[k]), :]


def _rsa_level1_start(dwint, dwout, dw2, rx1, thru):
    def body(di_ref, do_ref, d2_ref, rx1_ref, thru_ref, sa, ra, di_o, do_o, d2_o, rx1_o, thru_o):
        x, y, c = _position()
        srcs = _weight_pieces(wi_ref=di_ref, wo_ref=do_ref, w2_ref=d2_ref)
        for j, chip in enumerate([(x, y), (1 - x, y), (x, 1 - y), (1 - x, 1 - y)]):
            for k in RSA_PIECES:
                pltpu.make_async_remote_copy(
                    src_ref=_block_rows(srcs, k, (*chip, 1 - c)), dst_ref=_rsa_rows(rx1_ref.at[j], k),
                    send_sem=sa.at[j], recv_sem=ra.at[j], device_id=(x, y, 1 - c), device_id_type=MESH).start()

    dma = pltpu.SemaphoreType.DMA
    arrs = (dwint, dwout, dw2, rx1, thru)
    return pl.pallas_call(
        body, name="rsa_level1_start", out_shape=(dma((4,)), dma((4,))) + tuple(_hbm_like(a) for a in arrs),
        in_specs=(HBM_SPEC,) * 5, out_specs=(SEM_SPEC,) * 2 + (HBM_SPEC,) * 5,
        input_output_aliases={0: 2, 1: 3, 2: 4, 3: 5, 4: 6},
        compiler_params=pltpu.CompilerParams(has_side_effects=SPLIT_EFFECT),
    )(*[_in_hbm(a) for a in arrs])


def _rsa_level1_wait(sa, ra, dwint, dwout, dw2, rx1, after):
    def body(di_ref, do_ref, d2_ref, rx1_ref, sa_ref, ra_ref, after_ref, di_o, do_o, d2_o, rx1_o):
        x, y, c = _position()
        for j in range(4):
            d = pltpu.make_async_remote_copy(src_ref=rx1_ref.at[j], dst_ref=rx1_ref.at[j], send_sem=sa_ref.at[j],
                                             recv_sem=ra_ref.at[j], device_id=(x, y, c), device_id_type=MESH)
            d.wait_recv()
            d.wait_send()

    arrs = (dwint, dwout, dw2, rx1)
    return pl.pallas_call(
        body, name="rsa_level1_wait", out_shape=tuple(_hbm_like(a) for a in arrs),
        in_specs=(HBM_SPEC,) * 4 + (SEM_SPEC, SEM_SPEC, ANY_SPEC), out_specs=(HBM_SPEC,) * 4,
        input_output_aliases={0: 0, 1: 1, 2: 2, 3: 3},
        compiler_params=pltpu.CompilerParams(has_side_effects=SPLIT_EFFECT),
    )(*arrs, sa, ra, after)


def _rsa_chip_sums(dwint, dwout, dw2, rx1):
    nblk = RSA_ROWS // RSA_BLOCK

    def body(di_ref, do_ref, d2_ref, rx1_ref, tx_ref, acc_ref, own_buf, rx_buf, tx_buf, acc_buf, in_sems, out_sems):
        x, y, c = _position()
        srcs = _weight_pieces(wi_ref=di_ref, wo_ref=do_ref, w2_ref=d2_ref)
        chips = [(x, y), (1 - x, y), (x, 1 - y), (1 - x, 1 - y)]

        def start_loads(j):
            s = j % 2
            for k in RSA_PIECES:
                pltpu.make_async_copy(_block_rows(srcs, k, (*chips[j], c)), _rsa_rows(own_buf.at[s], k),
                                      in_sems.at[2 * s]).start()
            pltpu.make_async_copy(rx1_ref.at[j], rx_buf.at[s], in_sems.at[2 * s + 1]).start()

        def wait_loads(j):
            s = j % 2
            pltpu.make_async_copy(rx1_ref.at[j], own_buf.at[s], in_sems.at[2 * s]).wait()
            pltpu.make_async_copy(rx1_ref.at[j], rx_buf.at[s], in_sems.at[2 * s + 1]).wait()

        def store(j):
            if j == 0:
                return pltpu.make_async_copy(acc_buf, acc_ref, out_sems.at[2])
            return pltpu.make_async_copy(tx_buf.at[j % 2], tx_ref.at[j - 1], out_sems.at[j % 2])

        start_loads(0)
        for j in range(4):
            s = j % 2
            if j + 1 < 4:
                start_loads(j + 1)
            wait_loads(j)
            if j == 3:
                store(1).wait()

            def add(i, carry, j=j, s=s):
                rows = pl.ds(pl.multiple_of(i * RSA_BLOCK, 16), RSA_BLOCK)
                tot = own_buf[s, rows, :].astype(f32) + rx_buf[s, rows, :].astype(f32)
                if j == 0:
                    acc_buf[rows, :] = tot
                else:
                    tx_buf[s, rows, :] = tot.astype(bf16)
                return carry

            lax.fori_loop(0, nblk, add, 0)
            store(j).start()
        store(0).wait()
        store(2).wait()
        store(3).wait()

    return pl.pallas_call(
        body, in_specs=[ANY_SPEC] * 4, out_specs=[ANY_SPEC] * 2,
        out_shape=(SDS((3, RSA_ROWS, D), bf16), SDS((RSA_ROWS, D), f32)),
        scratch_shapes=[pltpu.VMEM((2, RSA_ROWS, D), bf16), pltpu.VMEM((2, RSA_ROWS, D), bf16),
                        pltpu.VMEM((2, RSA_ROWS, D), bf16), pltpu.VMEM((RSA_ROWS, D), f32),
                        pltpu.SemaphoreType.DMA((4,)), pltpu.SemaphoreType.DMA((3,))],
        compiler_params=_cparams(None, VMEM_LIMIT_V7X), name="rsa_chip_sums")(dwint, dwout, dw2, rx1)


def _rsa_level2_start(tx, rx2, thru):
    def body(tx_ref, rx2_ref, thru_ref, sb, rb, tx_o, rx2_o, thru_o):
        x, y, c = _position()
        for j, chip in enumerate([(1 - x, y), (x, 1 - y), (1 - x, 1 - y)]):
            pltpu.make_async_remote_copy(src_ref=tx_ref.at[j], dst_ref=rx2_ref.at[j], send_sem=sb.at[j],
                                         recv_sem=rb.at[j], device_id=(*chip, c), device_id_type=MESH).start()

    dma = pltpu.SemaphoreType.DMA
    arrs = (tx, rx2, thru)
    return pl.pallas_call(
        body, name="rsa_level2_start", out_shape=(dma((3,)), dma((3,))) + tuple(_hbm_like(a) for a in arrs),
        in_specs=(HBM_SPEC,) * 3, out_specs=(SEM_SPEC,) * 2 + (HBM_SPEC,) * 3,
        input_output_aliases={0: 2, 1: 3, 2: 4},
        compiler_params=pltpu.CompilerParams(has_side_effects=SPLIT_EFFECT),
    )(*[_in_hbm(a) for a in arrs])


def _rsa_level2_wait(sb, rb, tx, rx2, after):
    def body(tx_ref, rx2_ref, sb_ref, rb_ref, after_ref, rx2_o):
        x, y, c = _position()
        for j in range(3):
            d = pltpu.make_async_remote_copy(src_ref=tx_ref.at[j], dst_ref=rx2_ref.at[j], send_sem=sb_ref.at[j],
                                             recv_sem=rb_ref.at[j], device_id=(x, y, c), device_id_type=MESH)
            d.wait_recv()
            d.wait_send()

    return pl.pallas_call(
        body, name="rsa_level2_wait", out_shape=_hbm_like(rx2),
        in_specs=(HBM_SPEC, HBM_SPEC, SEM_SPEC, SEM_SPEC, ANY_SPEC), out_specs=HBM_SPEC,
        input_output_aliases={1: 0},
        compiler_params=pltpu.CompilerParams(has_side_effects=SPLIT_EFFECT),
    )(tx, rx2, sb, rb, after)


def _adamw_math(w, g, m, v):
    m = ADAM_B1 * m + (1.0 - ADAM_B1) * g
    v = ADAM_B2 * v + (1.0 - ADAM_B2) * (g * g)
    m_hat = m / (1.0 - ADAM_B1 ** ADAM_STEP)
    v_hat = v / (1.0 - ADAM_B2 ** ADAM_STEP)
    delta = -ADAM_LR * (m_hat / (jnp.sqrt(v_hat) + ADAM_EPS) + ADAM_WD * w)
    return delta, m, v


def _adamw_big(pieces, ws, ms, vs, own, landed, name):
    npiece = len(pieces)
    nland = sum(a.shape[0] if a.ndim == 3 else 1 for a in landed)
    rmax = max(PIECE_ROWS[k] for k in pieces)
    half = FS // 2

    def segments(k):
        if k in G1_PIECES:
            return [(hf * len(G1_PIECES) * half + k * half, hf * half, half) for hf in (0, 1)]
        return [(RSA_OFF[k], 0, PIECE_ROWS[k])]

    def body(*refs):
        ins = (refs[0:npiece], refs[npiece:2 * npiece], refs[2 * npiece:3 * npiece])
        own_ref = refs[3 * npiece]
        nin = 3 * npiece + 1 + len(landed)
        land_refs = []
        for ref, a in zip(refs[3 * npiece + 1:nin], landed):
            land_refs += [ref.at[j] for j in range(a.shape[0])] if a.ndim == 3 else [ref]
        out_refs = refs[nin:nin + 4 * npiece]
        inb, landb, outb, in_sems, land_sems, out_sems = refs[nin + 4 * npiece:]

        def loads(i):
            s, k = i % 2, pieces[i]
            r = PIECE_ROWS[k]
            cps = [pltpu.make_async_copy(ins[q][i].at[0], inb.at[s, q, pl.ds(0, r), :], in_sems.at[4 * s + q])
                   for q in range(3)]
            waits = list(cps)
            for src0, dst0, n in segments(k):
                cps.append(pltpu.make_async_copy(own_ref.at[pl.ds(src0, n), :], inb.at[s, 3, pl.ds(dst0, n), :],
                                                 in_sems.at[4 * s + 3]))
                for p in range(nland):
                    cps.append(pltpu.make_async_copy(land_refs[p].at[pl.ds(src0, n), :],
                                                     landb.at[s, p, pl.ds(dst0, n), :], land_sems.at[nland * s + p]))
            own_rows = inb.at[s, 3, pl.ds(0, r), :]
            waits.append(pltpu.make_async_copy(own_rows, own_rows, in_sems.at[4 * s + 3]))
            for p in range(nland):
                rows = landb.at[s, p, pl.ds(0, r), :]
                waits.append(pltpu.make_async_copy(rows, rows, land_sems.at[nland * s + p]))
            return cps, waits

        def stores(i):
            s, r = i % 2, PIECE_ROWS[pieces[i]]
            return [pltpu.make_async_copy(outb.at[s, q, pl.ds(0, r), :], out_refs[q * npiece + i].at[0],
                                          out_sems.at[4 * s + q]) for q in range(4)]

        for cp in loads(0)[0]:
            cp.start()
        for i in range(npiece):
            s, r = i % 2, PIECE_ROWS[pieces[i]]
            if i + 1 < npiece:
                for cp in loads(i + 1)[0]:
                    cp.start()
            for cp in loads(i)[1]:
                cp.wait()
            if i >= 2:
                for cp in stores(i - 2):
                    cp.wait()
            g = inb[s, 3, 0:r, :]
            for p in range(nland):
                g = g + landb[s, p, 0:r, :].astype(f32)
            d, nm, nv = _adamw_math(inb[s, 0, 0:r, :], g, inb[s, 1, 0:r, :], inb[s, 2, 0:r, :])
            outb[s, 0, 0:r, :] = g
            outb[s, 1, 0:r, :] = d
            outb[s, 2, 0:r, :] = nm
            outb[s, 3, 0:r, :] = nv
            for cp in stores(i):
                cp.start()
        for i in range(max(npiece - 2, 0), npiece):
            for cp in stores(i):
                cp.wait()

    hbm = pl.BlockSpec(memory_space=pl.ANY)
    outs = pl.pallas_call(
        body, in_specs=[hbm] * (3 * npiece + 1 + len(landed)), out_specs=[hbm] * (4 * npiece),
        out_shape=tuple(SDS(w.shape, f32) for _ in range(4) for w in ws),
        scratch_shapes=[pltpu.VMEM((2, 4, rmax, D), f32), pltpu.VMEM((2, nland, rmax, D), bf16),
                        pltpu.VMEM((2, 4, rmax, D), f32),
                        pltpu.SemaphoreType.DMA((8,)), pltpu.SemaphoreType.DMA((2 * nland,)),
                        pltpu.SemaphoreType.DMA((8,))],
        compiler_params=_cparams(None, VMEM_LIMIT_V7X), name=name)(*ws, *ms, *vs, own, *landed)
    return [list(outs[q * npiece:(q + 1) * npiece]) for q in range(4)]


def _adamw_small(ws, ms, vs, gs, name):
    n = len(ws)

    def body(*refs):
        w_refs, m_refs, v_refs, g_refs = refs[0:n], refs[n:2 * n], refs[2 * n:3 * n], refs[3 * n:4 * n]
        outs = refs[4 * n:]
        for i in range(n):
            d, nm, nv = _adamw_math(w_refs[i][...], g_refs[i][...], m_refs[i][...], v_refs[i][...])
            outs[i][...] = d
            outs[n + i][...] = nm
            outs[2 * n + i][...] = nv

    outs = pl.pallas_call(
        body, out_shape=tuple(SDS(w.shape, f32) for _ in range(3) for w in ws), name=name)(*ws, *ms, *vs, *gs)
    return [list(outs[q * n:(q + 1) * n]) for q in range(3)]


WEIGHTS = ("ffn1_norm", "ffn1_w_gate", "ffn1_w_up", "ffn1_w_down", "mix_norm", "w_in", "q_norm", "k_norm",
           "attn_sinks", "rel_bias", "pool_w", "pool_scale", "w_out", "ffn2_norm", "ffn2_w_gate", "ffn2_w_up",
           "ffn2_w_down")
BIG = (("ffn1_w_gate", True), ("ffn1_w_up", True), ("ffn1_w_down", False), ("w_in", True), ("w_out", False),
       ("ffn2_w_gate", True), ("ffn2_w_up", True), ("ffn2_w_down", False))


def kernel(x, ffn1_norm, ffn1_w_gate, ffn1_w_up, ffn1_w_down, mix_norm, w_in, q_norm, k_norm, attn_sinks, rel_bias, pool_w, pool_scale, w_out, ffn2_norm, ffn2_w_gate, ffn2_w_up, ffn2_w_down, loss_target, m_ffn1_norm, m_ffn1_w_gate, m_ffn1_w_up, m_ffn1_w_down, m_mix_norm, m_w_in, m_q_norm, m_k_norm, m_attn_sinks, m_rel_bias, m_pool_w, m_pool_scale, m_w_out, m_ffn2_norm, m_ffn2_w_gate, m_ffn2_w_up, m_ffn2_w_down, v_ffn1_norm, v_ffn1_w_gate, v_ffn1_w_up, v_ffn1_w_down, v_mix_norm, v_w_in, v_q_norm, v_k_norm, v_attn_sinks, v_rel_bias, v_pool_w, v_pool_scale, v_w_out, v_ffn2_norm, v_ffn2_w_gate, v_ffn2_w_up, v_ffn2_w_down):
    args = dict(locals())
    w = {n: args[n] for n in WEIGHTS}
    m = {n: args["m_" + n] for n in WEIGHTS}
    v = {n: args["v_" + n] for n in WEIGHTS}

    as_rows = lambda a, tr: jnp.swapaxes(a, 1, 2) if tr else a
    shard = jnp.concatenate([as_rows(w[n], tr)[0].astype(bf16) for n, tr in BIG], axis=0)
    exchanges = _GatheredWeights(shard, x[0], ffn1_norm)
    gx, (dw1, _, _, _), small = _local_step(
        x[0], loss_target[0], exchanges, ffn1_norm, mix_norm, ffn2_norm, q_norm, k_norm, attn_sinks,
        rel_bias, pool_w[0], pool_scale)

    nrows1 = len(G1_PIECES) * FS
    for_x, for_y, own1, small_tot = _reduce_scatter_ffn1_head(dw1, _pack_small(small))
    ssem, rsem, for_x, for_y, from_x, from_y, small_tot = _rs1_tail_start(
        for_x, for_y, lax.empty((nrows1, D), bf16), lax.empty((nrows1, D), bf16), small_tot)
    own_rest, landed_rest = exchanges.mix_ffn2_grads_parts(small_tot)

    grads, deltas, new_m, new_v = {}, {}, {}, {}
    rest = [k for k in range(len(BIG)) if k not in G1_PIECES]
    rows_of = lambda t, ks: [as_rows(t[BIG[k][0]], BIG[k][1]) for k in ks]
    rest_out = _adamw_big(rest, rows_of(w, rest), rows_of(m, rest), rows_of(v, rest), own_rest, [landed_rest],
                          "adamw_rest")
    from_x, from_y = _rs1_tail_wait(ssem, rsem, for_x, for_y, from_x, from_y, rest_out[0][0])
    ffn1 = list(G1_PIECES)
    ffn1_out = _adamw_big(ffn1, rows_of(w, ffn1), rows_of(m, ffn1), rows_of(v, ffn1), own1, [from_x, from_y],
                          "adamw_ffn1")
    for ks, out in ((rest, rest_out), (ffn1, ffn1_out)):
        for i, k in enumerate(ks):
            n, tr = BIG[k]
            grads[n], deltas[n], new_m[n], new_v[n] = [as_rows(o[i], tr) for o in out]
    small_names = [n for n in SMALL_NAMES if n != "loss"]
    for n in small_names:
        grads[n] = _unpack_small(small_tot, n)
    ds, nms, nvs = _adamw_small([w[n] for n in small_names], [m[n] for n in small_names], [v[n] for n in small_names],
                                [grads[n] for n in small_names], "adamw_small")
    for i, n in enumerate(small_names):
        deltas[n], new_m[n], new_v[n] = ds[i], nms[i], nvs[i]
    loss = small_tot[LOSS_ROW, 0]
    return (loss, gx[None], *[grads[n] for n in WEIGHTS], *[deltas[n] for n in WEIGHTS],
            *[new_m[n] for n in WEIGHTS], *[new_v[n] for n in WEIGHTS])
```

```python
import jax
import jax.numpy as jnp
import numpy as np
from jax import lax
from jax.experimental import pallas as pl
from jax.experimental.pallas import tpu as pltpu

f32, bf16, i32 = jnp.float32, jnp.bfloat16, jnp.int32
SDS = jax.ShapeDtypeStruct

D = 1024
F = 2816
HD = 64
NH = 8
NKV = 2
GQA = NH // NKV
DATTN = NH * HD
DKV = NKV * HD
DPOOL = 512
POOL_WINDOWS = (2, 4, 8, 16)
PGD = DPOOL // len(POOL_WINDOWS)
DIN = DATTN + 2 * DKV + DPOOL
DMIX = DATTN + DPOOL
BLK = 128
NBUCK = 32
MAX_DISTANCE = 128
EPS = 1e-6
NEG = -1e30
SCALE = HD ** -0.5

ADAM_LR, ADAM_B1, ADAM_B2, ADAM_EPS, ADAM_WD, ADAM_STEP = 0.001, 0.9, 0.999, 1e-08, 0.01, 10

NDEV = 8
FS = F // NDEV
INS = DIN // NDEV
OUTS = DMIX // NDEV
PIECE_ROWS = (FS, FS, FS, INS, OUTS, FS, FS, FS)
PIECE_OFF = tuple(int(v) for v in np.cumsum((0,) + PIECE_ROWS[:-1]))
PACK_ROWS = sum(PIECE_ROWS)

VMEM_LIMIT_V7X = 56 * 1024 * 1024

MESH = pl.DeviceIdType.MESH


def _cparams(sem=None, vmem=None):
    return pltpu.CompilerParams(dimension_semantics=sem, vmem_limit_bytes=vmem)


def _nt(a, b):
    return lax.dot_general(a, b, (((1,), (1,)), ((), ())), preferred_element_type=f32)


def _tn(a, b):
    return lax.dot_general(a, b, (((0,), (0,)), ((), ())), preferred_element_type=f32)


def _nn(a, b):
    return jnp.dot(a, b, preferred_element_type=f32)


def _sigmoid(x):
    return 1.0 / (1.0 + jnp.exp(-x))


def _norm_fwd(x, g, name):
    T = x.shape[0]
    tm = min(512, T)

    def body(x_ref, g_ref, h_ref):
        xv = x_ref[...]
        r = lax.rsqrt(jnp.mean(xv * xv, axis=-1, keepdims=True) + EPS)
        h_ref[...] = (xv * r * g_ref[...]).astype(bf16)

    return pl.pallas_call(
        body, grid=(T // tm,),
        in_specs=[pl.BlockSpec((tm, D), lambda i: (i, 0)), pl.BlockSpec((1, D), lambda i: (0, 0))],
        out_specs=pl.BlockSpec((tm, D), lambda i: (i, 0)),
        out_shape=SDS((T, D), bf16), name=name)(x, g)


FFN_ROW_CHUNK = 256


def _ffn_tiles(T):
    return min(1024, T), 256


def _ffn_fwd(h, w, x, target, next_gain, name):
    T = h.shape[0]
    tm, tf = _ffn_tiles(T)
    nf = F // tf
    with_loss = target is not None
    assert with_loss != (next_gain is not None)

    def body(*refs):
        if with_loss:
            h_ref, w_ref, x_hbm, t_hbm, xo_ref, g_ref, u_ref, dyb_ref, loss_ref, tbuf, sem = refs
        else:
            h_ref, w_ref, x_hbm, gain_ref, xo_ref, g_ref, u_ref, hn_ref, sem = refs
        fi = pl.program_id(0)

        @pl.when(fi == 0)
        def _():
            cp = pltpu.make_async_copy(x_hbm, xo_ref, sem)
            cp.start()
            cp.wait()

        wgu = w_ref[0:2].reshape(2 * tf, D)
        for r in range(0, T, tm):
            rows = slice(r, r + tm)
            gu = _nt(h_ref[rows, :], wgu)
            gate, up = gu[:, :tf], gu[:, tf:]
            act = gate * _sigmoid(gate) * up
            g_ref[0, rows, :] = gate.astype(bf16)
            u_ref[0, rows, :] = up.astype(bf16)
            xo_ref[rows, :] += _nn((0.5 * act).astype(bf16), w_ref[2])

        if with_loss:
            @pl.when(fi == nf - 1)
            def _():
                lanes = jnp.zeros((1, 128), f32)
                for r in range(0, T, tm):
                    rows = slice(r, r + tm)
                    cp = pltpu.make_async_copy(t_hbm.at[pl.ds(r, tm), :], tbuf, sem)
                    cp.start()
                    cp.wait()
                    e = xo_ref[rows, :] - tbuf[...]
                    dy = e * (1.0 / D)
                    xo_ref[rows, :] = dy
                    dyb_ref[rows, :] = (0.5 * dy).astype(bf16)
                    col = jnp.sum(e * e, axis=0, keepdims=True) * (0.5 / D)
                    for k in range(D // 128):
                        lanes = lanes + col[:, 128 * k:128 * (k + 1)]
                loss_ref[...] = lanes
        else:
            @pl.when(fi == nf - 1)
            def _():
                for r in range(0, T, FFN_ROW_CHUNK):
                    rows = slice(r, r + FFN_ROW_CHUNK)
                    xv = xo_ref[rows, :]
                    rstd = lax.rsqrt(jnp.mean(xv * xv, axis=-1, keepdims=True) + EPS)
                    hn_ref[rows, :] = (xv * rstd * gain_ref[...]).astype(bf16)

    tok = pl.BlockSpec((T, D), lambda f: (0, 0))
    act_spec = pl.BlockSpec((1, T, tf), lambda f: (f, 0, 0))
    hbm = pl.BlockSpec(memory_space=pl.ANY)
    in_specs = [tok, pl.BlockSpec((3, tf, D), lambda f: (0, f, 0)), hbm]
    out_specs = [tok, act_spec, act_spec]
    out_shape = [SDS((T, D), f32), SDS((nf, T, tf), bf16), SDS((nf, T, tf), bf16)]
    scratch = [pltpu.SemaphoreType.DMA]
    args = [h, w, x]
    if with_loss:
        in_specs.append(hbm)
        args.append(target)
        out_specs += [tok, pl.BlockSpec((1, 128), lambda f: (0, 0))]
        out_shape += [SDS((T, D), bf16), SDS((1, 128), f32)]
        scratch = [pltpu.VMEM((tm, D), f32)] + scratch
    else:
        in_specs.append(pl.BlockSpec((1, D), lambda f: (0, 0)))
        args.append(next_gain)
        out_specs.append(tok)
        out_shape.append(SDS((T, D), bf16))
    return pl.pallas_call(
        body, grid=(nf,), in_specs=in_specs, out_specs=out_specs, out_shape=tuple(out_shape), scratch_shapes=scratch,
        compiler_params=_cparams(("arbitrary",), VMEM_LIMIT_V7X), name=name)(*args)


def _ffn_bwd(dob, h, gate, up, w, norm, name):
    x, g, dres = norm
    T = h.shape[0]
    _, tf = _ffn_tiles(T)
    nf = F // tf
    tm = min(512, T)
    nchunk = T // tm

    def body(do_hbm, h_hbm, g_ref, u_ref, w_ref, x_hbm, gain_ref, dr_hbm, dx_hbm, dxb_hbm, dg_ref, dw_ref,
             do_v, h_v, dh_acc, dgu_s, act_s, xbuf, rbuf, obuf, obb, sems, in_sems, out_sems):
        fi = pl.program_id(0)

        @pl.when(fi == 0)
        def _():
            loads = [pltpu.make_async_copy(do_hbm, do_v, sems.at[0]), pltpu.make_async_copy(h_hbm, h_v, sems.at[1])]
            for cp in loads:
                cp.start()
            dh_acc[...] = jnp.zeros_like(dh_acc)
            for cp in loads:
                cp.wait()

        wgu = w_ref[0:2].reshape(2 * tf, D)
        for r in range(0, T, FFN_ROW_CHUNK):
            rows = slice(r, r + FFN_ROW_CHUNK)
            dov = do_v[rows, :]
            gv = g_ref[0, rows, :].astype(f32)
            uv = u_ref[0, rows, :].astype(f32)
            sg = _sigmoid(gv)
            sil = gv * sg
            dact = _nt(dov, w_ref[2])
            dup = dact * sil
            dgate = dact * uv * (sg * (1.0 + gv * (1.0 - sg)))
            dgu = jnp.concatenate([dgate.astype(bf16), dup.astype(bf16)], axis=1)
            dgu_s[rows, :] = dgu
            act_s[rows, :] = (sil * uv).astype(bf16)
            dh_acc[rows, :] += _nn(dgu, wgu)
        dw_ref[0:2] = _tn(dgu_s[...], h_v[...]).reshape(2, tf, D).astype(bf16)
        dw_ref[2] = _tn(act_s[...], do_v[...]).astype(bf16)

        @pl.when(fi == nf - 1)
        def _():
            def loads(i):
                s, rows = i % 2, pl.ds(i * tm, tm)
                return [pltpu.make_async_copy(x_hbm.at[rows, :], xbuf.at[s], in_sems.at[2 * s]),
                        pltpu.make_async_copy(dr_hbm.at[rows, :], rbuf.at[s], in_sems.at[2 * s + 1])]

            def stores(i):
                s, rows = i % 2, pl.ds(i * tm, tm)
                return [pltpu.make_async_copy(obuf.at[s], dx_hbm.at[rows, :], out_sems.at[2 * s]),
                        pltpu.make_async_copy(obb.at[s], dxb_hbm.at[rows, :], out_sems.at[2 * s + 1])]

            for cp in loads(0):
                cp.start()
            dg = jnp.zeros((1, D), f32)
            for i in range(nchunk):
                s = i % 2
                if i + 1 < nchunk:
                    for cp in loads(i + 1):
                        cp.start()
                for cp in loads(i):
                    cp.wait()
                if i >= 2:
                    for cp in stores(i - 2):
                        cp.wait()
                xv = xbuf[s]
                rstd = lax.rsqrt(jnp.mean(xv * xv, axis=-1, keepdims=True) + EPS)
                xh = xv * rstd
                dhv = dh_acc[i * tm:(i + 1) * tm, :]
                dxh = dhv * gain_ref[...]
                dx = rbuf[s] + rstd * (dxh - xh * jnp.mean(dxh * xh, axis=-1, keepdims=True))
                obuf[s] = dx
                obb[s] = dx.astype(bf16)
                dg = dg + jnp.sum(dhv * xh, axis=0, keepdims=True)
                for cp in stores(i):
                    cp.start()
            dg_ref[...] = dg
            for i in range(max(nchunk - 2, 0), nchunk):
                for cp in stores(i):
                    cp.wait()

    act_spec = pl.BlockSpec((1, T, tf), lambda f: (f, 0, 0))
    wspec = pl.BlockSpec((3, tf, D), lambda f: (0, f, 0))
    vec = pl.BlockSpec((1, D), lambda f: (0, 0))
    hbm = pl.BlockSpec(memory_space=pl.ANY)
    return pl.pallas_call(
        body, grid=(nf,),
        in_specs=[hbm, hbm, act_spec, act_spec, wspec, hbm, vec, hbm],
        out_specs=[hbm, hbm, vec, wspec],
        out_shape=(SDS((T, D), f32), SDS((T, D), bf16), SDS((1, D), f32), SDS((3, F, D), bf16)),
        scratch_shapes=[pltpu.VMEM((T, D), bf16), pltpu.VMEM((T, D), bf16), pltpu.VMEM((T, D), f32),
                        pltpu.VMEM((T, 2 * tf), bf16), pltpu.VMEM((T, tf), bf16),
                        pltpu.VMEM((2, tm, D), f32), pltpu.VMEM((2, tm, D), f32), pltpu.VMEM((2, tm, D), f32),
                        pltpu.VMEM((2, tm, D), bf16),
                        pltpu.SemaphoreType.DMA((2,)), pltpu.SemaphoreType.DMA((4,)), pltpu.SemaphoreType.DMA((4,))],
        compiler_params=_cparams(("arbitrary",), VMEM_LIMIT_V7X), name=name)(dob, h, gate, up, w, x, g, dres)


def _in_proj_fwd(h, wint, name):
    T = h.shape[0]
    tm = min(512, T)

    def body(h_ref, w_ref, z_ref):
        z_ref[...] = _nt(h_ref[...], w_ref[...])

    return pl.pallas_call(
        body, grid=(T // tm,),
        in_specs=[pl.BlockSpec((tm, D), lambda i: (i, 0)), pl.BlockSpec((DIN, D), lambda i: (0, 0))],
        out_specs=pl.BlockSpec((tm, DIN), lambda i: (i, 0)),
        out_shape=SDS((T, DIN), f32), name=name)(h, wint)


def _in_proj_bwd(dz, wint, h, norm, out_scale, name):
    x, g, dres = norm
    T = h.shape[0]
    tm = min(512, T)
    nt = T // tm

    def body(dz_ref, w_ref, h_ref, x_ref, g_ref, dr_ref, dx_ref, dxb_ref, dg_ref, dw_ref, acc):
        i = pl.program_id(0)
        dzb = dz_ref[...].astype(bf16)
        dhv = _nn(dzb, w_ref[...])
        part = _tn(dzb, h_ref[...])
        xv = x_ref[...]
        rstd = lax.rsqrt(jnp.mean(xv * xv, axis=-1, keepdims=True) + EPS)
        xh = xv * rstd
        dxh = dhv * g_ref[...]
        dx = dr_ref[...] + rstd * (dxh - xh * jnp.mean(dxh * xh, axis=-1, keepdims=True))
        dx_ref[...] = dx
        dxb_ref[...] = (out_scale * dx).astype(bf16)
        dg = jnp.sum(dhv * xh, axis=0, keepdims=True)

        @pl.when(i == 0)
        def _():
            acc[...] = part
            dg_ref[...] = dg

        @pl.when(i > 0)
        def _():
            acc[...] += part
            dg_ref[...] += dg

        @pl.when(i == nt - 1)
        def _():
            dw_ref[...] = acc[...].astype(bf16)

    wspec = pl.BlockSpec((DIN, D), lambda i: (0, 0))
    tok = pl.BlockSpec((tm, D), lambda i: (i, 0))
    vec = pl.BlockSpec((1, D), lambda i: (0, 0))
    return pl.pallas_call(
        body, grid=(nt,),
        in_specs=[pl.BlockSpec((tm, DIN), lambda i: (i, 0)), wspec, tok, tok, vec, tok],
        out_specs=[tok, tok, vec, wspec],
        out_shape=(SDS((T, D), f32), SDS((T, D), bf16), SDS((1, D), f32), SDS((DIN, D), bf16)),
        scratch_shapes=[pltpu.VMEM((DIN, D), f32)],
        compiler_params=_cparams(("arbitrary",)), name=name)(dz, wint, h, x, g, dres)


def _out_proj_fwd(ymix, wout, x, g, name):
    T = x.shape[0]
    tm = min(512, T)

    def body(y_ref, w_ref, x_ref, g_ref, o_ref, h_ref):
        o = x_ref[...] + _nn(y_ref[...], w_ref[...])
        o_ref[...] = o
        r = lax.rsqrt(jnp.mean(o * o, axis=-1, keepdims=True) + EPS)
        h_ref[...] = (o * r * g_ref[...]).astype(bf16)

    tok = pl.BlockSpec((tm, D), lambda i: (i, 0))
    return pl.pallas_call(
        body, grid=(T // tm,),
        in_specs=[pl.BlockSpec((tm, DMIX), lambda i: (i, 0)), pl.BlockSpec((DMIX, D), lambda i: (0, 0)), tok,
                  pl.BlockSpec((1, D), lambda i: (0, 0))],
        out_specs=[tok, tok], out_shape=(SDS((T, D), f32), SDS((T, D), bf16)), name=name)(ymix, wout, x, g)


def _out_proj_bwd(dxb, wout, ymix, name):
    T = dxb.shape[0]
    tm = min(512, T)
    nt = T // tm

    def body(dx_ref, w_ref, y_ref, dy_ref, dw_ref, acc):
        i = pl.program_id(0)
        dxv = dx_ref[...]
        dy_ref[...] = _nt(dxv, w_ref[...])
        part = _tn(y_ref[...], dxv)

        @pl.when(i == 0)
        def _():
            acc[...] = part

        @pl.when(i > 0)
        def _():
            acc[...] += part

        @pl.when(i == nt - 1)
        def _():
            dw_ref[...] = acc[...].astype(bf16)

    wspec = pl.BlockSpec((DMIX, D), lambda i: (0, 0))
    return pl.pallas_call(
        body, grid=(nt,),
        in_specs=[pl.BlockSpec((tm, D), lambda i: (i, 0)), wspec, pl.BlockSpec((tm, DMIX), lambda i: (i, 0))],
        out_specs=[pl.BlockSpec((tm, DMIX), lambda i: (i, 0)), wspec],
        out_shape=(SDS((T, DMIX), f32), SDS((DMIX, D), bf16)),
        scratch_shapes=[pltpu.VMEM((DMIX, D), f32)],
        compiler_params=_cparams(("arbitrary",)), name=name)(dxb, wout, ymix)


def _t5_bucket_table():
    ql = np.arange(BLK)[:, None]
    kl = np.arange(2 * BLK)[None, :]
    n = np.maximum(ql + BLK - kl, 0)
    max_exact = NBUCK // 2
    large = max_exact + (np.log(np.maximum(n, 1) / max_exact) / np.log(MAX_DISTANCE / max_exact)
                         * (NBUCK - max_exact)).astype(np.int32)
    large = np.minimum(large, NBUCK - 1)
    return np.where(n < max_exact, n, large).astype(np.int32)


def _fill_bias(bk_ref, rb_ref, bias_scr):
    bk = bk_ref[...]
    for h in range(NH):
        def step(b, acc, h=h):
            return acc + jnp.where(bk == b, rb_ref[b, h], 0.0)
        bias_scr[h] = lax.fori_loop(0, NBUCK, step, jnp.zeros((BLK, 2 * BLK), f32))


MIX_SUB = 4


class _Window:
    def __init__(self, zc_ref, zp_ref, n, s):
        self.blk = n * MIX_SUB + s
        self.cur = lambda a, b: zc_ref[s * BLK:(s + 1) * BLK, a:b]
        self.prev = (lambda a, b: zp_ref[:, a:b]) if s == 0 else (lambda a, b: zc_ref[(s - 1) * BLK:s * BLK, a:b])


def _attn_qkv(win, kh, qg, kg):
    kc = DATTN + HD * kh
    vc = DATTN + DKV + HD * kh
    kx = jnp.concatenate([win.prev(kc, kc + HD), win.cur(kc, kc + HD)], axis=0)
    vx = jnp.concatenate([win.prev(vc, vc + HD), win.cur(vc, vc + HD)], axis=0)
    qx = jnp.concatenate([win.cur(HD * (GQA * kh + g), HD * (GQA * kh + g + 1)) for g in range(GQA)], axis=0)
    rq = lax.rsqrt(jnp.mean(qx * qx, axis=-1, keepdims=True) + EPS)
    rk = lax.rsqrt(jnp.mean(kx * kx, axis=-1, keepdims=True) + EPS)
    qhat, khat = qx * rq, kx * rk
    return dict(qhat=qhat, khat=khat, rq=rq, rk=rk, qsb=(qhat * (qg * SCALE)).astype(bf16),
                knb=(khat * kg).astype(bf16), vb=vx.astype(bf16))


def _window_masks(n):
    row = lax.broadcasted_iota(i32, (GQA * BLK, 2 * BLK), 0) & (BLK - 1)
    col = lax.broadcasted_iota(i32, (GQA * BLK, 2 * BLK), 1)
    band = (col > row) & (col <= row + BLK)
    return band & ((col >= BLK) | (n > 0)), band


def _attn_probs(a, kh, sk_ref, bias_scr, mask):
    s = _nt(a["qsb"], a["knb"]) + bias_scr[GQA * kh:GQA * (kh + 1)].reshape(GQA * BLK, 2 * BLK)
    s = jnp.where(mask, s, NEG)
    ridx = lax.broadcasted_iota(i32, (GQA * BLK, 1), 0)
    sink = jnp.full((GQA * BLK, 1), sk_ref[GQA * kh + GQA - 1], f32)
    for g in range(GQA - 2, -1, -1):
        sink = jnp.where(ridx < (g + 1) * BLK, sk_ref[GQA * kh + g], sink)
    m = jnp.maximum(jnp.max(s, axis=-1, keepdims=True), sink)
    e = jnp.exp(s - m)
    den = jnp.sum(e, axis=-1, keepdims=True) + jnp.exp(sink - m)
    return e / den


POOL_STEPS = {2: (1,), 4: (1, 2), 8: (1, 2, 4), 16: (1, 2, 4, 8)}


def _pool_group(win, g, w):
    n = win.blk
    c0 = DATTN + 2 * DKV + PGD * g
    uc = win.cur(c0, c0 + PGD)
    up = jnp.where(n > 0, win.prev(c0, c0 + PGD), 0.0)
    sm = jnp.concatenate([up, uc], axis=0)
    for k in POOL_STEPS[w]:
        sm = sm + pltpu.roll(sm, k, axis=0)
    pos = n * BLK + lax.broadcasted_iota(i32, (BLK, 1), 0) + 1
    cnt = jnp.minimum(pos, w).astype(f32)
    return sm[BLK:2 * BLK] / cnt - uc, cnt


def _mix_fwd(z, qg, kg, sinks, relb, bucket, pool_w, pscale, name):
    T = z.shape[0]
    step_rows = MIX_SUB * BLK
    nsteps = T // step_rows

    def body(zc_ref, zp_ref, qg_ref, kg_ref, sk_ref, rb_ref, bk_ref, pw_ref, ps_ref, y_ref, p_ref, bias_scr, yacc):
        n = pl.program_id(0)

        @pl.when(n == 0)
        def _():
            _fill_bias(bk_ref, rb_ref, bias_scr)

        first_mask, mask = _window_masks(n)
        for s in range(MIX_SUB):
            win = _Window(zc_ref, zp_ref, n, s)
            rows = slice(s * BLK, (s + 1) * BLK)
            for kh in range(NKV):
                a = _attn_qkv(win, kh, qg_ref[...], kg_ref[...])
                pb = _attn_probs(a, kh, sk_ref, bias_scr, first_mask if s == 0 else mask).astype(bf16)
                p_ref[s, GQA * kh:GQA * (kh + 1)] = pb.reshape(GQA, BLK, 2 * BLK)
                o = _nn(pb, a["vb"])
                for g in range(GQA):
                    hc = HD * (GQA * kh + g)
                    yacc[rows, hc:hc + HD] = o[g * BLK:(g + 1) * BLK]
            for g, w in enumerate(POOL_WINDOWS):
                pooled, _ = _pool_group(win, g, w)
                yp = _nn(pooled.astype(bf16), pw_ref[g].astype(bf16)) * ps_ref[:, PGD * g:PGD * (g + 1)]
                yacc[rows, DATTN + PGD * g:DATTN + PGD * (g + 1)] = yp
        y_ref[...] = yacc[...].astype(bf16)

    full = lambda *shape: pl.BlockSpec(shape, lambda n: (0,) * len(shape))
    smem = pl.BlockSpec(memory_space=pltpu.SMEM)
    return pl.pallas_call(
        body, grid=(nsteps,),
        in_specs=[pl.BlockSpec((step_rows, DIN), lambda n: (n, 0)),
                  pl.BlockSpec((BLK, DIN), lambda n: (jnp.maximum(n * MIX_SUB - 1, 0), 0)),
                  full(1, HD), full(1, HD), smem, smem, full(BLK, 2 * BLK),
                  full(len(POOL_WINDOWS), PGD, PGD), full(1, DPOOL)],
        out_specs=[pl.BlockSpec((step_rows, DMIX), lambda n: (n, 0)),
                   pl.BlockSpec((MIX_SUB, NH, BLK, 2 * BLK), lambda n: (n, 0, 0, 0))],
        out_shape=(SDS((T, DMIX), bf16), SDS((T // BLK, NH, BLK, 2 * BLK), bf16)),
        scratch_shapes=[pltpu.VMEM((NH, BLK, 2 * BLK), f32), pltpu.VMEM((step_rows, DMIX), f32)],
        compiler_params=_cparams(("arbitrary",)), name=name)(z, z, qg, kg, sinks, relb, bucket, pool_w, pscale)


def _mix_bwd(z, dy, probs, qg, kg, relb, bucket, pool_w, pscale, name):
    T = z.shape[0]
    step_rows = MIX_SUB * BLK
    nsteps = T // step_rows

    def body(zc_ref, zp_ref, dy_ref, p_ref, qg_ref, kg_ref, bk_ref, pw_ref, ps_ref,
             dz_ref, dqg_ref, dkg_ref, dsk_ref, drb_ref, dpw_ref, dps_ref, dbias_scr):
        n = pl.program_id(0)

        @pl.when(n == 0)
        def _():
            dbias_scr[...] = jnp.zeros_like(dbias_scr)
            dqg_ref[...] = jnp.zeros_like(dqg_ref)
            dkg_ref[...] = jnp.zeros_like(dkg_ref)
            dpw_ref[...] = jnp.zeros_like(dpw_ref)
            dps_ref[...] = jnp.zeros_like(dps_ref)

        qg, kg = qg_ref[...], kg_ref[...]
        for s in range(MIX_SUB):
            win = _Window(zc_ref, zp_ref, n, s)
            blk = win.blk
            rows = pl.ds(pl.multiple_of(blk * BLK, BLK), BLK)
            prow = pl.ds(pl.multiple_of(jnp.maximum(blk - 1, 0) * BLK, BLK), BLK)
            dyr = slice(s * BLK, (s + 1) * BLK)

            def into_prev(fn, s=s):
                if s == 0:
                    pl.when(n > 0)(fn)
                else:
                    fn()

            for kh in range(NKV):
                a = _attn_qkv(win, kh, qg, kg)
                pb = p_ref[s, GQA * kh:GQA * (kh + 1)].reshape(GQA * BLK, 2 * BLK)
                p = pb.astype(f32)
                do = jnp.concatenate([dy_ref[dyr, HD * (GQA * kh + g):HD * (GQA * kh + g + 1)] for g in range(GQA)],
                                     axis=0).astype(bf16)
                dv = _tn(pb, do)
                dp = _nt(do, a["vb"])
                delta = jnp.sum(p * dp, axis=-1, keepdims=True)
                ds = p * (dp - delta)
                for g in range(GQA):
                    dbias_scr[GQA * kh + g] += ds[g * BLK:(g + 1) * BLK]
                dsb = ds.astype(bf16)
                dqn = _nn(dsb, a["knb"]) * SCALE
                dkn = _tn(dsb, a["qsb"])
                qhat, khat = a["qhat"], a["khat"]
                dqg_ref[...] += jnp.sum(dqn * qhat, axis=0, keepdims=True)
                dkg_ref[...] += jnp.sum(dkn * khat, axis=0, keepdims=True)
                dqh = dqn * qg
                dq = a["rq"] * (dqh - qhat * jnp.mean(dqh * qhat, axis=-1, keepdims=True))
                dkh = dkn * kg
                dk = a["rk"] * (dkh - khat * jnp.mean(dkh * khat, axis=-1, keepdims=True))
                kc = DATTN + HD * kh
                vc = DATTN + DKV + HD * kh
                for g in range(GQA):
                    hc = HD * (GQA * kh + g)
                    dz_ref[rows, hc:hc + HD] = dq[g * BLK:(g + 1) * BLK]
                dz_ref[rows, kc:kc + HD] = dk[BLK:2 * BLK]
                dz_ref[rows, vc:vc + HD] = dv[BLK:2 * BLK]

                def kv_prev(dk=dk, dv=dv, kc=kc, vc=vc, prow=prow):
                    dz_ref[prow, kc:kc + HD] += dk[0:BLK]
                    dz_ref[prow, vc:vc + HD] += dv[0:BLK]

                into_prev(kv_prev)

            for g, w in enumerate(POOL_WINDOWS):
                c0 = DATTN + 2 * DKV + PGD * g
                pooled, cnt = _pool_group(win, g, w)
                pb = pooled.astype(bf16)
                wb = pw_ref[g].astype(bf16)
                dyp = dy_ref[dyr, DATTN + PGD * g:DATTN + PGD * (g + 1)]
                ypre = _nn(pb, wb)
                dps_ref[:, PGD * g:PGD * (g + 1)] += jnp.sum(dyp * ypre, axis=0, keepdims=True)
                dyg = (dyp * ps_ref[:, PGD * g:PGD * (g + 1)]).astype(bf16)
                dpw_ref[g] += _tn(pb, dyg)
                dpooled = _nt(dyg, wb)
                due = jnp.concatenate([jnp.zeros((BLK, PGD), f32), dpooled / cnt], axis=0)
                for k in POOL_STEPS[w]:
                    due = due + pltpu.roll(due, 2 * BLK - k, axis=0)
                dz_ref[rows, c0:c0 + PGD] = due[BLK:2 * BLK] - dpooled

                def pool_prev(due=due, c0=c0, prow=prow):
                    dz_ref[prow, c0:c0 + PGD] += due[0:BLK]

                into_prev(pool_prev)

        @pl.when(n == nsteps - 1)
        def _():
            bk = bk_ref[...]
            ri = lax.broadcasted_iota(i32, (NBUCK, NH), 0)
            ci = lax.broadcasted_iota(i32, (NBUCK, NH), 1)

            def step(b, acc):
                for h in range(NH):
                    sel = jnp.where(bk == b, dbias_scr[h], 0.0)
                    tot = jnp.sum(jnp.sum(sel, axis=1, keepdims=True), axis=0, keepdims=True)
                    acc = acc + jnp.where((ri == b) & (ci == h), tot, 0.0)
                return acc

            drb_ref[...] = lax.fori_loop(0, NBUCK, step, jnp.zeros((NBUCK, NH), f32))
            lane = lax.broadcasted_iota(i32, (1, 128), 1)
            dsk = jnp.zeros((1, 128), f32)
            for h in range(NH):
                tot = jnp.sum(jnp.sum(dbias_scr[h], axis=1, keepdims=True), axis=0, keepdims=True)
                dsk = dsk - jnp.where(lane == h, tot, 0.0)
            dsk_ref[...] = dsk

    full = lambda *shape: pl.BlockSpec(shape, lambda n: (0,) * len(shape))
    npg = len(POOL_WINDOWS)
    return pl.pallas_call(
        body, grid=(nsteps,),
        in_specs=[pl.BlockSpec((step_rows, DIN), lambda n: (n, 0)),
                  pl.BlockSpec((BLK, DIN), lambda n: (jnp.maximum(n * MIX_SUB - 1, 0), 0)),
                  pl.BlockSpec((step_rows, DMIX), lambda n: (n, 0)),
                  pl.BlockSpec((MIX_SUB, NH, BLK, 2 * BLK), lambda n: (n, 0, 0, 0)),
                  full(1, HD), full(1, HD), full(BLK, 2 * BLK), full(npg, PGD, PGD), full(1, DPOOL)],
        out_specs=[full(T, DIN), full(1, HD), full(1, HD), full(1, 128), full(NBUCK, NH),
                   full(npg, PGD, PGD), full(1, DPOOL)],
        out_shape=(SDS((T, DIN), f32), SDS((1, HD), f32), SDS((1, HD), f32), SDS((1, 128), f32),
                   SDS((NBUCK, NH), f32), SDS((npg, PGD, PGD), f32), SDS((1, DPOOL), f32)),
        scratch_shapes=[pltpu.VMEM((NH, BLK, 2 * BLK), f32)],
        compiler_params=_cparams(("arbitrary",), VMEM_LIMIT_V7X),
        name=name)(z, z, dy, probs, qg, kg, bucket, pool_w, pscale)


class _LocalWeights:
    def __init__(self, w1, wint, wout, w2):
        self.w1, self.wint, self.wout, self.w2 = w1, wint, wout, w2

    def ffn1(self):
        return self.w1

    def first_norm(self, x, gain):
        return _norm_fwd(x, gain, "norm1_fwd")

    def after_ffn1(self, gain, x1):
        return gain

    def mix(self, after):
        return self.wint, self.wout

    def before_out_proj(self, wout, after):
        return wout

    def ffn2(self, after):
        return self.w2

    def out_ffn2_grads_ready(self, dwout, dw2, after):
        return after

    def before_ffn1_bwd(self, dwint, dx1b):
        return dx1b


def _local_step(x, target, weights, g1, gm, g3, qg, kg, sinks, relb, pool_w, pscale):
    bucket = jnp.asarray(_t5_bucket_table())
    sk = sinks.reshape(NH)
    w1 = weights.ffn1()
    h1 = weights.first_norm(x, g1)
    x1, gate1, up1, h2 = _ffn_fwd(h1, w1, x, None, gm, "ffn1_fwd")
    gm = weights.after_ffn1(gm, x1)
    wint, wout = weights.mix(h2)
    z = _in_proj_fwd(h2, wint, "in_proj_fwd")
    ymix, probs = _mix_fwd(z, qg, kg, sk, relb, bucket, pool_w, pscale, "mix_fwd")
    wout = weights.before_out_proj(wout, ymix)
    x2, h3 = _out_proj_fwd(ymix, wout, x1, g3, "out_proj_fwd")
    w2 = weights.ffn2(h3)
    dy, gate2, up2, dyb, loss_lanes = _ffn_fwd(h3, w2, x2, target, None, "ffn2_fwd")

    dx2, dx2b, dg3, dw2 = _ffn_bwd(dyb, h3, gate2, up2, w2, (x2, g3, dy), "ffn2_bwd")
    dymix, dwout = _out_proj_bwd(dx2b, wout, ymix, "out_proj_bwd")
    dymix = weights.out_ffn2_grads_ready(dwout, dw2, dymix)
    dz, dqg, dkg, dsk, drb, dpw, dps = _mix_bwd(z, dymix, probs, qg, kg, relb, bucket, pool_w, pscale, "mix_bwd")
    dx1, dx1b, dgm, dwint = _in_proj_bwd(dz, wint, h2, (x1, gm, dx2), 0.5, "in_proj_bwd")
    dx1b = weights.before_ffn1_bwd(dwint, dx1b)
    gx, _, dg1, dw1 = _ffn_bwd(dx1b, h1, gate1, up1, w1, (x, g1, dx1), "ffn1_bwd")
    small = dict(ffn1_norm=dg1, mix_norm=dgm, ffn2_norm=dg3, pool_scale=dps, q_norm=dqg, k_norm=dkg,
                 attn_sinks=dsk[:, :NH], rel_bias=drb, pool_w=dpw, loss=loss_lanes)
    return gx, (dw1, dwint, dwout, dw2), small


SMALL_NAMES = ("ffn1_norm", "mix_norm", "ffn2_norm", "pool_scale", "q_norm", "k_norm", "attn_sinks", "rel_bias",
               "pool_w", "loss")
SMALL_SHAPES = dict(ffn1_norm=(1, D), mix_norm=(1, D), ffn2_norm=(1, D), pool_scale=(1, DPOOL), q_norm=(1, HD),
                    k_norm=(1, HD), attn_sinks=(1, NH), rel_bias=(NBUCK, NH),
                    pool_w=(1, len(POOL_WINDOWS), PGD, PGD), loss=(1, 128))


def _small_rows(name):
    return -(-int(np.prod(SMALL_SHAPES[name])) // 128)


SMALL_OFF = {}
_r = 0
for _n in SMALL_NAMES:
    SMALL_OFF[_n] = _r
    _r += _small_rows(_n)
SMALL_ROWS = -(-_r // 8) * 8
LOSS_ROW = SMALL_OFF["loss"]


def _pack_small(vals):
    parts = []
    for n in SMALL_NAMES:
        size = _small_rows(n) * 128
        if n in vals:
            flat = vals[n].astype(f32).reshape(-1)
            parts.append(jnp.pad(flat, (0, size - flat.shape[0])))
        else:
            parts.append(jnp.zeros((size,), f32))
    flat = jnp.concatenate(parts)
    flat = jnp.pad(flat, (0, SMALL_ROWS * 128 - flat.shape[0]))
    return flat.reshape(SMALL_ROWS, 128)


def _unpack_small(packed, name):
    size = int(np.prod(SMALL_SHAPES[name]))
    r0 = SMALL_OFF[name]
    return packed[r0:r0 + _small_rows(name)].reshape(-1)[:size].reshape(SMALL_SHAPES[name])


def _position():
    return lax.axis_index("x"), lax.axis_index("y"), lax.axis_index("c")


def _dev_index(x, y, c):
    return 4 * x + 2 * y + c


G1_PIECES, MIX_PIECES, F2_PIECES = (0, 1, 2), (3, 4), (5, 6, 7)


def _group_rows(pieces):
    return sum(PIECE_ROWS[k] for k in pieces)


def _shard_piece(s_ref, k):
    return s_ref.at[pl.ds(PIECE_OFF[k], PIECE_ROWS[k]), :]


def _shard_group(s_ref, pieces):
    return s_ref.at[pl.ds(PIECE_OFF[pieces[0]], _group_rows(pieces)), :]


def _weight_pieces(w1_ref=None, wi_ref=None, wo_ref=None, w2_ref=None):
    arrs = {}
    if w1_ref is not None:
        arrs.update({0: w1_ref.at[0], 1: w1_ref.at[1], 2: w1_ref.at[2]})
    if wi_ref is not None:
        arrs[3] = wi_ref
    if wo_ref is not None:
        arrs[4] = wo_ref
    if w2_ref is not None:
        arrs.update({5: w2_ref.at[0], 6: w2_ref.at[1], 7: w2_ref.at[2]})
    return arrs


def _block_rows(arrs, k, dev):
    r = PIECE_ROWS[k]
    return arrs[k].at[pl.ds(pl.multiple_of(_dev_index(*dev) * r, 16), r), :]


NORM_ROWS = 512


def _all_gather_ffn1(shard, x, gain):
    pieces = G1_PIECES
    rest_pieces = MIX_PIECES + F2_PIECES
    half = FS // 2
    T = x.shape[0]
    SIB, X0, X1, Y0, Y1, RELAY_Y, RELAY_X, ON_X, ON_Y, ON_D0, ON_D1 = range(11)

    def body(s_ref, x_ref, g_ref, w1_ref, h_ref, wi_ref, wo_ref, w2_ref, xbuf, hbuf, rest_buf,
             send_sems, recv_sems, local_sem, norm_sems):
        x, y, c = _position()
        me, sib = (x, y, c), (x, y, 1 - c)
        xn, yn, dg = (1 - x, y, c), (x, 1 - y, c), (1 - x, 1 - y, c)
        arrs = _weight_pieces(w1_ref=w1_ref)

        def place_rest():
            rest = _weight_pieces(wi_ref=wi_ref, wo_ref=wo_ref, w2_ref=w2_ref)
            grp = _shard_group(s_ref, rest_pieces)
            load = pltpu.make_async_copy(grp, rest_buf, norm_sems.at[0])
            load.start()
            load.wait()
            base = PIECE_OFF[rest_pieces[0]]
            for k in rest_pieces:
                pltpu.make_async_copy(rest_buf.at[pl.ds(PIECE_OFF[k] - base, PIECE_ROWS[k]), :],
                                      _block_rows(rest, k, me), norm_sems.at[1]).start()
            pltpu.make_async_copy(grp, rest_buf, norm_sems.at[1]).wait()

        def first_norm():
            for r in range(0, T, NORM_ROWS):
                load = pltpu.make_async_copy(x_ref.at[pl.ds(r, NORM_ROWS), :], xbuf, norm_sems.at[0])
                load.start()
                load.wait()
                xv = xbuf[...]
                rs = lax.rsqrt(jnp.mean(xv * xv, axis=-1, keepdims=True) + EPS)
                hbuf[...] = (xv * rs * g_ref[...]).astype(bf16)
                store = pltpu.make_async_copy(hbuf, h_ref.at[pl.ds(r, NORM_ROWS), :], norm_sems.at[1])
                store.start()
                store.wait()

        def rows_of(k, block, hf):
            r = PIECE_ROWS[k]
            start, size = (0, r) if hf is None else (hf * half, half)
            return arrs[k].at[pl.ds(pl.multiple_of(_dev_index(*block) * r + start, 16), size), :]

        def copies(rel, block, hf, to, from_shard=False):
            def src(k):
                if not from_shard:
                    return rows_of(k, block, hf)
                start, size = (0, PIECE_ROWS[k]) if hf is None else (hf * half, half)
                return s_ref.at[pl.ds(PIECE_OFF[k] + start, size), :]
            return [pltpu.make_async_remote_copy(
                src_ref=src(k), dst_ref=rows_of(k, block, hf), send_sem=send_sems.at[rel], recv_sem=recv_sems.at[rel],
                device_id=to, device_id_type=MESH) for k in pieces]

        def waiter(rel, hf):
            nrows = len(pieces) * (FS if hf is None else half)
            grp = s_ref.at[pl.ds(0, nrows), :]
            return pltpu.make_async_remote_copy(src_ref=grp, dst_ref=grp, send_sem=send_sems.at[rel],
                                                recv_sem=recv_sems.at[rel], device_id=me, device_id_type=MESH)

        def start(cps):
            for cp in cps:
                cp.start()

        mine = [pltpu.make_async_copy(_shard_piece(s_ref, k), _block_rows(arrs, k, me), local_sem) for k in pieces]
        start(mine)
        start(copies(SIB, me, None, sib, True))
        start(copies(X0, me, 0, xn, True))
        start(copies(Y1, me, 1, yn, True))
        start(copies(X1, me, 1, xn, True))
        start(copies(Y0, me, 0, yn, True))
        first_norm()
        place_rest()
        waiter(X0, 0).wait_recv()
        start(copies(RELAY_Y, xn, 0, yn))
        waiter(Y1, 1).wait_recv()
        start(copies(RELAY_X, yn, 1, xn))
        waiter(X1, 1).wait_recv()
        start(copies(ON_X, xn, None, sib))
        waiter(Y0, 0).wait_recv()
        start(copies(ON_Y, yn, None, sib))
        waiter(RELAY_Y, 0).wait_recv()
        start(copies(ON_D0, dg, 0, sib))
        waiter(RELAY_X, 1).wait_recv()
        start(copies(ON_D1, dg, 1, sib))
        waiter(SIB, None).wait_recv()
        waiter(ON_X, None).wait_recv()
        waiter(ON_Y, None).wait_recv()
        waiter(ON_D0, 0).wait_recv()
        waiter(ON_D1, 1).wait_recv()
        for rel, hf in ((SIB, None), (X0, 0), (X1, 1), (Y0, 0), (Y1, 1), (RELAY_Y, 0), (RELAY_X, 1),
                        (ON_X, None), (ON_Y, None), (ON_D0, 0), (ON_D1, 1)):
            waiter(rel, hf).wait_send()
        grp = _shard_group(s_ref, pieces)
        pltpu.make_async_copy(grp, grp, local_sem).wait()

    hbm = pl.BlockSpec(memory_space=pl.ANY)
    return pl.pallas_call(
        body, in_specs=[hbm, hbm, pl.BlockSpec(memory_space=pltpu.VMEM)], out_specs=[hbm] * 5,
        out_shape=(SDS((3, F, D), bf16), SDS((T, D), bf16),
                   SDS((DIN, D), bf16), SDS((DMIX, D), bf16), SDS((3, F, D), bf16)),
        scratch_shapes=[pltpu.VMEM((NORM_ROWS, D), f32), pltpu.VMEM((NORM_ROWS, D), bf16),
                        pltpu.VMEM((_group_rows(rest_pieces), D), bf16),
                        pltpu.SemaphoreType.DMA((11,)), pltpu.SemaphoreType.DMA((11,)), pltpu.SemaphoreType.DMA,
                        pltpu.SemaphoreType.DMA((2,))],
        compiler_params=pltpu.CompilerParams(has_side_effects=True),
        name="all_gather_ffn1")(shard, x, gain)


HBM_SPEC = pl.BlockSpec(memory_space=pltpu.HBM)
SEM_SPEC = pl.BlockSpec(memory_space=pltpu.SEMAPHORE)
ANY_SPEC = pl.BlockSpec(memory_space=pl.ANY)
SPLIT_EFFECT = pltpu.SideEffectType.DATAFLOW_SIDE_EFFECTING


def _in_hbm(a):
    return pltpu.with_memory_space_constraint(a, pltpu.HBM)


def _hbm_like(a):
    return pltpu.HBM(a.shape, a.dtype)


def _gather_rest_start(shard, wi, wo, w2, w1):
    def body(s_ref, wi_ref, wo_ref, w2_ref, w1_ref,
             ssem_m, rsem_m0, rsem_m, ssem_f, rsem_f0, rsem_f, s_o, wi_o, wo_o, w2_o, w1_o):
        x, y, c = _position()
        me, sib = (x, y, c), (x, y, 1 - c)
        chips = [(1 - x, y), (x, 1 - y), (1 - x, 1 - y)]
        arrs = _weight_pieces(wi_ref=wi_ref, wo_ref=wo_ref, w2_ref=w2_ref)
        for pieces, ssem, rsem0, rsem in ((MIX_PIECES, ssem_m, rsem_m0, rsem_m), (F2_PIECES, ssem_f, rsem_f0, rsem_f)):
            for p in pieces:
                pltpu.make_async_remote_copy(
                    src_ref=_shard_piece(s_ref, p), dst_ref=_block_rows(arrs, p, me), send_sem=ssem.at[0],
                    recv_sem=rsem0, device_id=sib, device_id_type=MESH).start()
            for j, chip in enumerate(chips):
                for p in pieces:
                    pltpu.make_async_remote_copy(
                        src_ref=_shard_piece(s_ref, p), dst_ref=_block_rows(arrs, p, me), send_sem=ssem.at[1 + j],
                        recv_sem=rsem.at[j], device_id=(*chip, c), device_id_type=MESH).start()

    dma = pltpu.SemaphoreType.DMA
    return pl.pallas_call(
        body, name="gather_rest_start",
        out_shape=(dma((4,)), dma(()), dma((3,)), dma((4,)), dma(()), dma((3,)),
                   _hbm_like(shard), _hbm_like(wi), _hbm_like(wo), _hbm_like(w2), _hbm_like(w1)),
        in_specs=(HBM_SPEC,) * 5, out_specs=(SEM_SPEC,) * 6 + (HBM_SPEC,) * 5,
        input_output_aliases={0: 6, 1: 7, 2: 8, 3: 9, 4: 10},
        compiler_params=pltpu.CompilerParams(has_side_effects=SPLIT_EFFECT),
    )(_in_hbm(shard), _in_hbm(wi), _in_hbm(wo), _in_hbm(w2), _in_hbm(w1))


def _gather_mix_pass_on(rsem_m, wi, wo, thru, after):
    def body(wi_ref, wo_ref, thru_ref, rsem, after_ref, fsend, frecv, wi_o, wo_o, thru_o):
        x, y, c = _position()
        sib = (x, y, 1 - c)
        arrs = _weight_pieces(wi_ref=wi_ref, wo_ref=wo_ref)
        both = wi_ref.at[pl.ds(0, _group_rows(MIX_PIECES)), :]
        for j, chip in enumerate([(1 - x, y), (x, 1 - y), (1 - x, 1 - y)]):
            pltpu.make_async_remote_copy(src_ref=both, dst_ref=both, send_sem=fsend.at[j], recv_sem=rsem.at[j],
                                         device_id=(x, y, c), device_id_type=MESH).wait_recv()
            for p in MIX_PIECES:
                rows = _block_rows(arrs, p, (*chip, c))
                pltpu.make_async_remote_copy(src_ref=rows, dst_ref=rows, send_sem=fsend.at[j], recv_sem=frecv.at[j],
                                             device_id=sib, device_id_type=MESH).start()

    dma = pltpu.SemaphoreType.DMA
    return pl.pallas_call(
        body, name="gather_mix_pass_on",
        out_shape=(dma((3,)), dma((3,)), _hbm_like(wi), _hbm_like(wo), _hbm_like(thru)),
        in_specs=(HBM_SPEC, HBM_SPEC, HBM_SPEC, SEM_SPEC, ANY_SPEC), out_specs=(SEM_SPEC, SEM_SPEC) + (HBM_SPEC,) * 3,
        input_output_aliases={0: 2, 1: 3, 2: 4},
        compiler_params=pltpu.CompilerParams(has_side_effects=SPLIT_EFFECT),
    )(wi, wo, _in_hbm(thru), rsem_m, after)


def _gather_mix_wait(ssem_m, rsem_m0, fsend, frecv, shard, wi, wo, after):
    def body(s_ref, wi_ref, wo_ref, ssem, rsem0, fs, fr, after_ref, s_o, wi_o, wo_o):
        x, y, c = _position()
        grp = _shard_group(s_ref, MIX_PIECES)

        def waiter(send_sem, recv_sem):
            return pltpu.make_async_remote_copy(src_ref=grp, dst_ref=grp, send_sem=send_sem, recv_sem=recv_sem,
                                                device_id=(x, y, c), device_id_type=MESH)

        waiter(ssem.at[0], rsem0).wait_recv()
        for j in range(3):
            waiter(fs.at[j], fr.at[j]).wait_recv()
        for rel in range(4):
            waiter(ssem.at[rel], rsem0).wait_send()
        for j in range(3):
            waiter(fs.at[j], fr.at[j]).wait_send()

    return pl.pallas_call(
        body, name="gather_mix_wait", out_shape=(_hbm_like(shard), _hbm_like(wi), _hbm_like(wo)),
        in_specs=(HBM_SPEC,) * 3 + (SEM_SPEC,) * 4 + (ANY_SPEC,), out_specs=(HBM_SPEC,) * 3,
        input_output_aliases={0: 0, 1: 1, 2: 2},
        compiler_params=pltpu.CompilerParams(has_side_effects=SPLIT_EFFECT),
    )(shard, wi, wo, ssem_m, rsem_m0, fsend, frecv, after)


def _gather_ffn2_pass_on(rsem_f, w2, wo, after):
    def body(w2_ref, wo_ref, rsem, after_ref, fsend, frecv, w2_o, wo_o):
        x, y, c = _position()
        sib = (x, y, 1 - c)
        chips = [(1 - x, y), (x, 1 - y), (1 - x, 1 - y)]
        arrs = _weight_pieces(w2_ref=w2_ref)
        three = w2_ref.at[0, pl.ds(0, _group_rows(F2_PIECES)), :]
        for j, chip in enumerate(chips):
            pltpu.make_async_remote_copy(src_ref=three, dst_ref=three, send_sem=fsend.at[j], recv_sem=rsem.at[j],
                                         device_id=(x, y, c), device_id_type=MESH).wait_recv()
            for p in F2_PIECES:
                rows = _block_rows(arrs, p, (*chip, c))
                pltpu.make_async_remote_copy(src_ref=rows, dst_ref=rows, send_sem=fsend.at[j], recv_sem=frecv.at[j],
                                             device_id=sib, device_id_type=MESH).start()

    dma = pltpu.SemaphoreType.DMA
    return pl.pallas_call(
        body, name="gather_ffn2_pass_on", out_shape=(dma((3,)), dma((3,)), _hbm_like(w2), _hbm_like(wo)),
        in_specs=(HBM_SPEC, HBM_SPEC, SEM_SPEC, ANY_SPEC), out_specs=(SEM_SPEC, SEM_SPEC, HBM_SPEC, HBM_SPEC),
        input_output_aliases={0: 2, 1: 3},
        compiler_params=pltpu.CompilerParams(has_side_effects=SPLIT_EFFECT),
    )(w2, wo, rsem_f, after)


def _gather_ffn2_wait(ssem_f, rsem_f0, fsend, frecv, shard, w2, after):
    def body(s_ref, w2_ref, ssem, rsem0, fs, fr, after_ref, w2_o):
        x, y, c = _position()
        grp = _shard_group(s_ref, F2_PIECES)

        def waiter(send_sem, recv_sem):
            return pltpu.make_async_remote_copy(src_ref=grp, dst_ref=grp, send_sem=send_sem, recv_sem=recv_sem,
                                                device_id=(x, y, c), device_id_type=MESH)

        waiter(ssem.at[0], rsem0).wait_recv()
        for j in range(3):
            waiter(fs.at[j], fr.at[j]).wait_recv()
        for rel in range(4):
            waiter(ssem.at[rel], rsem0).wait_send()
        for j in range(3):
            waiter(fs.at[j], fr.at[j]).wait_send()

    return pl.pallas_call(
        body, name="gather_ffn2_wait", out_shape=_hbm_like(w2),
        in_specs=(HBM_SPEC, HBM_SPEC, SEM_SPEC, SEM_SPEC, SEM_SPEC, SEM_SPEC, ANY_SPEC), out_specs=HBM_SPEC,
        input_output_aliases={1: 0},
        compiler_params=pltpu.CompilerParams(has_side_effects=SPLIT_EFFECT),
    )(shard, w2, ssem_f, rsem_f0, fsend, frecv, after)


class _GatheredWeights(_LocalWeights):
    def __init__(self, shard, x, gain1):
        w1, self.h1, wi, wo, w2 = _all_gather_ffn1(shard, x, gain1)
        (self.ssem_m, self.rsem_m0, self.rsem_m, self.ssem_f, self.rsem_f0, self.rsem_f,
         self.shard, self.wi, self.wo, self.w2_part, self.w1) = _gather_rest_start(shard, wi, wo, w2, w1)

    def first_norm(self, x, gain):
        return self.h1

    def after_ffn1(self, gain, x1):
        self.fsend_m, self.frecv_m, self.wi, self.wo, gain = _gather_mix_pass_on(self.rsem_m, self.wi, self.wo, gain, x1)
        return gain

    def mix(self, after):
        self.shard, wint, wout = _gather_mix_wait(self.ssem_m, self.rsem_m0, self.fsend_m, self.frecv_m, self.shard,
                                                  self.wi, self.wo, after)
        return wint, wout

    def before_out_proj(self, wout, after):
        self.fsend, self.frecv, self.w2_part, wout = _gather_ffn2_pass_on(self.rsem_f, self.w2_part, wout, after)
        return wout

    def ffn2(self, after):
        return _gather_ffn2_wait(self.ssem_f, self.rsem_f0, self.fsend, self.frecv, self.shard, self.w2_part, after)

    def out_ffn2_grads_ready(self, dwout, dw2, after):
        rx1 = lax.empty((4, RSA_ROWS, D), bf16)
        sa, ra, sent, rx1, after = _rsa_level1_start(dict(wo=dwout, w2=dw2), rx1, after, "rsa_level1_start_out_ffn2")
        self.level1 = ((sa, ra), sent, rx1)
        return after

    def before_ffn1_bwd(self, dwint, dx1b):
        early, sent, rx1 = self.level1
        sa, ra, late, rx1, dx1b = _rsa_level1_start(dict(wi=dwint), rx1, dx1b, "rsa_level1_start_in")
        groups = (((MIX_PIECES[1],) + F2_PIECES, *early), ((MIX_PIECES[0],), sa, ra))
        dwint, dwout, dw2, rx1 = _rsa_level1_wait(groups, late["wi"], sent["wo"], sent["w2"], rx1, dx1b)
        tx, self.acc = _rsa_chip_sums(dwint, dwout, dw2, rx1)
        rx2 = lax.empty((3, RSA_ROWS, D), bf16)
        self.sb, self.rb, self.tx, self.rx2, dx1b = _rsa_level2_start(tx, rx2, dx1b)
        return dx1b

    def mix_ffn2_grads_parts(self, after):
        rx2 = _rsa_level2_wait(self.sb, self.rb, self.tx, self.rx2, after)
        return self.acc, rx2


def _reduce_scatter_ffn1_head(dw1, small_packed):
    pieces = G1_PIECES
    half = FS // 2
    hrows = len(pieces) * half
    nrows = 2 * hrows
    X_RELAY, Y_RELAY = range(2)

    def body(d1_ref, p_ref, forx_ref, fory_ref, own_ref, rx1_ref, relx_ref, rely_ref, tot_ref,
             own_buf, rx_buf, tx1, tx2, tx3, acc, sa, ra, sb, rb, lsem, pair, chips, small_send, small_recv):
        x, y, c = _position()
        me, sib = (x, y, c), (x, y, 1 - c)
        xn, yn = (1 - x, y, c), (x, 1 - y, c)
        rel_chips = [(x, y), (1 - x, y), (x, 1 - y), (1 - x, 1 - y)]
        srcs = _weight_pieces(w1_ref=d1_ref)

        my_chip = 2 * x + y
        pair[c] = p_ref[...]
        swap = pltpu.make_async_remote_copy(
            src_ref=p_ref, dst_ref=pair.at[c], send_sem=small_send.at[0], recv_sem=small_recv.at[0],
            device_id=sib, device_id_type=MESH)
        swap.start()
        small = [pltpu.make_async_remote_copy(
            src_ref=chips.at[my_chip], dst_ref=chips.at[my_chip], send_sem=small_send.at[j], recv_sem=small_recv.at[j],
            device_id=(*rel_chips[j], c), device_id_type=MESH) for j in (1, 2, 3)]

        def part(k, dev, hf):
            r = PIECE_ROWS[k]
            return srcs[k].at[pl.ds(pl.multiple_of(_dev_index(*dev) * r + hf * half, 16), half), :]

        def slot(ref, k, hf):
            return ref.at[pl.ds(hf * hrows + k * half, half), :]

        halves = [(k, hf) for hf in (0, 1) for k in pieces]

        for j in (3, 1, 2, 0):
            for k, hf in halves:
                pltpu.make_async_remote_copy(
                    src_ref=part(k, (*rel_chips[j], 1 - c), hf), dst_ref=slot(rx1_ref.at[j], k, hf),
                    send_sem=sa.at[j], recv_sem=ra.at[j], device_id=sib, device_id_type=MESH).start()

        def wait_a(j):
            return pltpu.make_async_remote_copy(src_ref=rx1_ref.at[j], dst_ref=rx1_ref.at[j], send_sem=sa.at[j],
                                                recv_sem=ra.at[j], device_id=me, device_id_type=MESH)

        def ici(rel, src, dst, to):
            return pltpu.make_async_remote_copy(src_ref=src, dst_ref=dst, send_sem=sb.at[rel], recv_sem=rb.at[rel],
                                                device_id=to, device_id_type=MESH)

        first, second = pl.ds(0, hrows), pl.ds(hrows, hrows)
        sends = {
            X_RELAY: ici(X_RELAY, tx3.at[first, :], relx_ref, xn),
            Y_RELAY: ici(Y_RELAY, tx3.at[second, :], rely_ref, yn),
        }

        swap.wait_recv()
        chips[my_chip] = pair[0] + pair[1]
        for cp in small:
            cp.start()

        def chip_sum(j, dst):
            loads = [pltpu.make_async_copy(part(k, (*rel_chips[j], c), hf), slot(own_buf, k, hf), lsem.at[0])
                     for k, hf in halves]
            for cp in loads:
                cp.start()
            wait_a(j).wait_recv()
            got = pltpu.make_async_copy(rx1_ref.at[j], rx_buf, lsem.at[1])
            got.start()
            pltpu.make_async_copy(rx_buf, rx_buf, lsem.at[0]).wait()
            got.wait()

            def add(i, carry):
                rows = pl.ds(pl.multiple_of(i * half, 16), half)
                tot = own_buf[rows, :].astype(f32) + rx_buf[rows, :].astype(f32)
                dst[rows, :] = tot.astype(dst.dtype)
                return carry

            lax.fori_loop(0, nrows // half, add, 0)

        def add_landed(landed, dst, rows0, nrows_):
            got = pltpu.make_async_copy(landed, rx_buf.at[pl.ds(0, nrows_), :], lsem.at[1])
            got.start()
            got.wait()

            def add(i, carry):
                src_rows = pl.ds(pl.multiple_of(i * half, 16), half)
                dst_rows = pl.ds(pl.multiple_of(rows0 + i * half, 16), half)
                dst[dst_rows, :] = (dst[dst_rows, :].astype(f32) + rx_buf[src_rows, :].astype(f32)).astype(dst.dtype)
                return carry

            lax.fori_loop(0, nrows_ // half, add, 0)

        chip_sum(3, tx3)
        sends[X_RELAY].start()
        sends[Y_RELAY].start()
        chip_sum(1, tx1)
        chip_sum(2, tx2)
        chip_sum(0, acc)
        own_out = pltpu.make_async_copy(acc, own_ref, lsem.at[0])
        own_out.start()
        sends[X_RELAY].wait_recv()
        add_landed(relx_ref, tx2, 0, hrows)
        sends[Y_RELAY].wait_recv()
        add_landed(rely_ref, tx1, hrows, hrows)
        own_out.wait()
        outs = [pltpu.make_async_copy(tx1, forx_ref, lsem.at[0]), pltpu.make_async_copy(tx2, fory_ref, lsem.at[1])]
        for cp in outs:
            cp.start()
        for cp in outs:
            cp.wait()
        for cp in small:
            cp.wait_recv()
        tot = (chips[0] + chips[1]) + (chips[2] + chips[3])
        tot_ref[...] = tot
        loss = jnp.sum(tot[LOSS_ROW:LOSS_ROW + 1, :], axis=-1, keepdims=True)
        tot_ref[LOSS_ROW:LOSS_ROW + 1, :] = jnp.broadcast_to(loss, (1, 128))
        for j in range(4):
            wait_a(j).wait_send()
        for cp in sends.values():
            cp.wait_send()
        swap.wait_send()
        for cp in small:
            cp.wait_send()

    hbm = pl.BlockSpec(memory_space=pl.ANY)
    vm = pl.BlockSpec(memory_space=pltpu.VMEM)
    outs = pl.pallas_call(
        body, in_specs=[hbm, vm], out_specs=[hbm] * 6 + [vm],
        out_shape=(SDS((nrows, D), bf16), SDS((nrows, D), bf16), SDS((nrows, D), f32), SDS((4, nrows, D), bf16),
                   SDS((hrows, D), bf16), SDS((hrows, D), bf16), SDS((SMALL_ROWS, 128), f32)),
        scratch_shapes=[pltpu.VMEM((nrows, D), bf16), pltpu.VMEM((nrows, D), bf16),
                        pltpu.VMEM((nrows, D), bf16), pltpu.VMEM((nrows, D), bf16), pltpu.VMEM((nrows, D), bf16),
                        pltpu.VMEM((nrows, D), f32),
                        pltpu.SemaphoreType.DMA((4,)), pltpu.SemaphoreType.DMA((4,)),
                        pltpu.SemaphoreType.DMA((2,)), pltpu.SemaphoreType.DMA((2,)), pltpu.SemaphoreType.DMA((2,)),
                        pltpu.VMEM((2, SMALL_ROWS, 128), f32), pltpu.VMEM((4, SMALL_ROWS, 128), f32),
                        pltpu.SemaphoreType.DMA((4,)), pltpu.SemaphoreType.DMA((4,))],
        compiler_params=pltpu.CompilerParams(has_side_effects=True, vmem_limit_bytes=VMEM_LIMIT_V7X),
        name="reduce_scatter_ffn1_head")(dw1, small_packed)
    return outs[0], outs[1], outs[2], outs[-1]


def _rs1_tail_start(for_x, for_y, from_x, from_y, thru):
    def body(fx_ref, fy_ref, lx_ref, ly_ref, thru_ref, ssem, rsem, fx_o, fy_o, lx_o, ly_o, thru_o):
        x, y, c = _position()
        pltpu.make_async_remote_copy(src_ref=fx_ref, dst_ref=lx_ref, send_sem=ssem.at[0], recv_sem=rsem.at[0],
                                     device_id=(1 - x, y, c), device_id_type=MESH).start()
        pltpu.make_async_remote_copy(src_ref=fy_ref, dst_ref=ly_ref, send_sem=ssem.at[1], recv_sem=rsem.at[1],
                                     device_id=(x, 1 - y, c), device_id_type=MESH).start()

    dma = pltpu.SemaphoreType.DMA
    arrs = (for_x, for_y, from_x, from_y, thru)
    return pl.pallas_call(
        body, name="rs1_tail_start", out_shape=(dma((2,)), dma((2,))) + tuple(_hbm_like(a) for a in arrs),
        in_specs=(HBM_SPEC,) * 5, out_specs=(SEM_SPEC,) * 2 + (HBM_SPEC,) * 5,
        input_output_aliases={0: 2, 1: 3, 2: 4, 3: 5, 4: 6},
        compiler_params=pltpu.CompilerParams(has_side_effects=SPLIT_EFFECT),
    )(*[_in_hbm(a) for a in arrs])


def _rs1_tail_wait(ssem, rsem, for_x, for_y, from_x, from_y, after):
    def body(fx_ref, fy_ref, lx_ref, ly_ref, ssem_ref, rsem_ref, after_ref, lx_o, ly_o):
        x, y, c = _position()
        for j, (src, dst) in enumerate(((fx_ref, lx_ref), (fy_ref, ly_ref))):
            d = pltpu.make_async_remote_copy(src_ref=src, dst_ref=dst, send_sem=ssem_ref.at[j], recv_sem=rsem_ref.at[j],
                                             device_id=(x, y, c), device_id_type=MESH)
            d.wait_recv()
            d.wait_send()

    return pl.pallas_call(
        body, name="rs1_tail_wait", out_shape=(_hbm_like(from_x), _hbm_like(from_y)),
        in_specs=(HBM_SPEC,) * 4 + (SEM_SPEC, SEM_SPEC, ANY_SPEC), out_specs=(HBM_SPEC, HBM_SPEC),
        input_output_aliases={2: 0, 3: 1},
        compiler_params=pltpu.CompilerParams(has_side_effects=SPLIT_EFFECT),
    )(for_x, for_y, from_x, from_y, ssem, rsem, after)


RSA_PIECES = MIX_PIECES + F2_PIECES
RSA_ROWS = _group_rows(RSA_PIECES)
RSA_OFF = {k: PIECE_OFF[k] - PIECE_OFF[RSA_PIECES[0]] for k in RSA_PIECES}
RSA_BLOCK = 192


def _rsa_rows(ref, k):
    return ref.at[pl.ds(RSA_OFF[k], PIECE_ROWS[k]), :]


def _rsa_level1_start(grads, rx1, thru, name):
    keys = sorted(grads)
    n = len(keys)

    def body(*refs):
        srcs = _weight_pieces(**{k + "_ref": ref for k, ref in zip(keys, refs[:n])})
        rx1_ref, sa, ra = refs[n], refs[n + 2], refs[n + 3]
        x, y, c = _position()
        for j, chip in enumerate([(x, y), (1 - x, y), (x, 1 - y), (1 - x, 1 - y)]):
            for k in sorted(srcs):
                pltpu.make_async_remote_copy(
                    src_ref=_block_rows(srcs, k, (*chip, 1 - c)), dst_ref=_rsa_rows(rx1_ref.at[j], k),
                    send_sem=sa.at[j], recv_sem=ra.at[j], device_id=(x, y, 1 - c), device_id_type=MESH).start()

    dma = pltpu.SemaphoreType.DMA
    arrs = tuple(grads[k] for k in keys) + (rx1, thru)
    outs = pl.pallas_call(
        body, name=name, out_shape=(dma((4,)), dma((4,))) + tuple(_hbm_like(a) for a in arrs),
        in_specs=(HBM_SPEC,) * len(arrs), out_specs=(SEM_SPEC,) * 2 + (HBM_SPEC,) * len(arrs),
        input_output_aliases={i: i + 2 for i in range(len(arrs))},
        compiler_params=pltpu.CompilerParams(has_side_effects=SPLIT_EFFECT),
    )(*[_in_hbm(a) for a in arrs])
    return outs[0], outs[1], dict(zip(keys, outs[2:2 + n])), outs[2 + n], outs[3 + n]


def _rsa_level1_wait(groups, dwint, dwout, dw2, rx1, after):
    ngroup = len(groups)

    def body(di_ref, do_ref, d2_ref, rx1_ref, *rest):
        x, y, c = _position()
        for g, (pieces, _, _) in enumerate(groups):
            sa_ref, ra_ref = rest[2 * g], rest[2 * g + 1]
            for j in range(4):
                rows = rx1_ref.at[j, pl.ds(RSA_OFF[pieces[0]], _group_rows(pieces)), :]
                d = pltpu.make_async_remote_copy(src_ref=rows, dst_ref=rows, send_sem=sa_ref.at[j],
                                                 recv_sem=ra_ref.at[j], device_id=(x, y, c), device_id_type=MESH)
                d.wait_recv()
                d.wait_send()

    arrs = (dwint, dwout, dw2, rx1)
    sems = [sem for _, sa, ra in groups for sem in (sa, ra)]
    return pl.pallas_call(
        body, name="rsa_level1_wait", out_shape=tuple(_hbm_like(a) for a in arrs),
        in_specs=(HBM_SPEC,) * 4 + (SEM_SPEC,) * (2 * ngroup) + (ANY_SPEC,), out_specs=(HBM_SPEC,) * 4,
        input_output_aliases={0: 0, 1: 1, 2: 2, 3: 3},
        compiler_params=pltpu.CompilerParams(has_side_effects=SPLIT_EFFECT),
    )(*arrs, *sems, after)


def _rsa_chip_sums(dwint, dwout, dw2, rx1):
    nblk = RSA_ROWS // RSA_BLOCK

    def body(di_ref, do_ref, d2_ref, rx1_ref, tx_ref, acc_ref, own_buf, rx_buf, tx_buf, acc_buf, in_sems, out_sems):
        x, y, c = _position()
        srcs = _weight_pieces(wi_ref=di_ref, wo_ref=do_ref, w2_ref=d2_ref)
        chips = [(x, y), (1 - x, y), (x, 1 - y), (1 - x, 1 - y)]

        def start_loads(j):
            s = j % 2
            for k in RSA_PIECES:
                pltpu.make_async_copy(_block_rows(srcs, k, (*chips[j], c)), _rsa_rows(own_buf.at[s], k),
                                      in_sems.at[2 * s]).start()
            pltpu.make_async_copy(rx1_ref.at[j], rx_buf.at[s], in_sems.at[2 * s + 1]).start()

        def wait_loads(j):
            s = j % 2
            pltpu.make_async_copy(rx1_ref.at[j], own_buf.at[s], in_sems.at[2 * s]).wait()
            pltpu.make_async_copy(rx1_ref.at[j], rx_buf.at[s], in_sems.at[2 * s + 1]).wait()

        def store(j):
            if j == 0:
                return pltpu.make_async_copy(acc_buf, acc_ref, out_sems.at[2])
            return pltpu.make_async_copy(tx_buf.at[j % 2], tx_ref.at[j - 1], out_sems.at[j % 2])

        start_loads(0)
        for j in range(4):
            s = j % 2
            if j + 1 < 4:
                start_loads(j + 1)
            wait_loads(j)
            if j == 3:
                store(1).wait()

            def add(i, carry, j=j, s=s):
                rows = pl.ds(pl.multiple_of(i * RSA_BLOCK, 16), RSA_BLOCK)
                tot = own_buf[s, rows, :].astype(f32) + rx_buf[s, rows, :].astype(f32)
                if j == 0:
                    acc_buf[rows, :] = tot
                else:
                    tx_buf[s, rows, :] = tot.astype(bf16)
                return carry

            lax.fori_loop(0, nblk, add, 0)
            store(j).start()
        store(0).wait()
        store(2).wait()
        store(3).wait()

    return pl.pallas_call(
        body, in_specs=[ANY_SPEC] * 4, out_specs=[ANY_SPEC] * 2,
        out_shape=(SDS((3, RSA_ROWS, D), bf16), SDS((RSA_ROWS, D), f32)),
        scratch_shapes=[pltpu.VMEM((2, RSA_ROWS, D), bf16), pltpu.VMEM((2, RSA_ROWS, D), bf16),
                        pltpu.VMEM((2, RSA_ROWS, D), bf16), pltpu.VMEM((RSA_ROWS, D), f32),
                        pltpu.SemaphoreType.DMA((4,)), pltpu.SemaphoreType.DMA((3,))],
        compiler_params=_cparams(None, VMEM_LIMIT_V7X), name="rsa_chip_sums")(dwint, dwout, dw2, rx1)


def _rsa_level2_start(tx, rx2, thru):
    def body(tx_ref, rx2_ref, thru_ref, sb, rb, tx_o, rx2_o, thru_o):
        x, y, c = _position()
        for j, chip in enumerate([(1 - x, y), (x, 1 - y), (1 - x, 1 - y)]):
            pltpu.make_async_remote_copy(src_ref=tx_ref.at[j], dst_ref=rx2_ref.at[j], send_sem=sb.at[j],
                                         recv_sem=rb.at[j], device_id=(*chip, c), device_id_type=MESH).start()

    dma = pltpu.SemaphoreType.DMA
    arrs = (tx, rx2, thru)
    return pl.pallas_call(
        body, name="rsa_level2_start", out_shape=(dma((3,)), dma((3,))) + tuple(_hbm_like(a) for a in arrs),
        in_specs=(HBM_SPEC,) * 3, out_specs=(SEM_SPEC,) * 2 + (HBM_SPEC,) * 3,
        input_output_aliases={0: 2, 1: 3, 2: 4},
        compiler_params=pltpu.CompilerParams(has_side_effects=SPLIT_EFFECT),
    )(*[_in_hbm(a) for a in arrs])


def _rsa_level2_wait(sb, rb, tx, rx2, after):
    def body(tx_ref, rx2_ref, sb_ref, rb_ref, after_ref, rx2_o):
        x, y, c = _position()
        for j in range(3):
            d = pltpu.make_async_remote_copy(src_ref=tx_ref.at[j], dst_ref=rx2_ref.at[j], send_sem=sb_ref.at[j],
                                             recv_sem=rb_ref.at[j], device_id=(x, y, c), device_id_type=MESH)
            d.wait_recv()
            d.wait_send()

    return pl.pallas_call(
        body, name="rsa_level2_wait", out_shape=_hbm_like(rx2),
        in_specs=(HBM_SPEC, HBM_SPEC, SEM_SPEC, SEM_SPEC, ANY_SPEC), out_specs=HBM_SPEC,
        input_output_aliases={1: 0},
        compiler_params=pltpu.CompilerParams(has_side_effects=SPLIT_EFFECT),
    )(tx, rx2, sb, rb, after)


def _adamw_math(w, g, m, v):
    m = ADAM_B1 * m + (1.0 - ADAM_B1) * g
    v = ADAM_B2 * v + (1.0 - ADAM_B2) * (g * g)
    m_hat = m / (1.0 - ADAM_B1 ** ADAM_STEP)
    v_hat = v / (1.0 - ADAM_B2 ** ADAM_STEP)
    delta = -ADAM_LR * (m_hat / (jnp.sqrt(v_hat) + ADAM_EPS) + ADAM_WD * w)
    return delta, m, v


def _adamw_big(pieces, ws, ms, vs, own, landed, name):
    npiece = len(pieces)
    nland = sum(a.shape[0] if a.ndim == 3 else 1 for a in landed)
    rmax = max(PIECE_ROWS[k] for k in pieces)
    half = FS // 2

    def segments(k):
        if k in G1_PIECES:
            return [(hf * len(G1_PIECES) * half + k * half, hf * half, half) for hf in (0, 1)]
        return [(RSA_OFF[k], 0, PIECE_ROWS[k])]

    def body(*refs):
        ins = (refs[0:npiece], refs[npiece:2 * npiece], refs[2 * npiece:3 * npiece])
        own_ref = refs[3 * npiece]
        nin = 3 * npiece + 1 + len(landed)
        land_refs = []
        for ref, a in zip(refs[3 * npiece + 1:nin], landed):
            land_refs += [ref.at[j] for j in range(a.shape[0])] if a.ndim == 3 else [ref]
        out_refs = refs[nin:nin + 4 * npiece]
        inb, landb, outb, in_sems, land_sems, out_sems = refs[nin + 4 * npiece:]

        def loads(i):
            s, k = i % 2, pieces[i]
            r = PIECE_ROWS[k]
            cps = [pltpu.make_async_copy(ins[q][i].at[0], inb.at[s, q, pl.ds(0, r), :], in_sems.at[4 * s + q])
                   for q in range(3)]
            waits = list(cps)
            for src0, dst0, n in segments(k):
                cps.append(pltpu.make_async_copy(own_ref.at[pl.ds(src0, n), :], inb.at[s, 3, pl.ds(dst0, n), :],
                                                 in_sems.at[4 * s + 3]))
                for p in range(nland):
                    cps.append(pltpu.make_async_copy(land_refs[p].at[pl.ds(src0, n), :],
                                                     landb.at[s, p, pl.ds(dst0, n), :], land_sems.at[nland * s + p]))
            own_rows = inb.at[s, 3, pl.ds(0, r), :]
            waits.append(pltpu.make_async_copy(own_rows, own_rows, in_sems.at[4 * s + 3]))
            for p in range(nland):
                rows = landb.at[s, p, pl.ds(0, r), :]
                waits.append(pltpu.make_async_copy(rows, rows, land_sems.at[nland * s + p]))
            return cps, waits

        def stores(i):
            s, r = i % 2, PIECE_ROWS[pieces[i]]
            return [pltpu.make_async_copy(outb.at[s, q, pl.ds(0, r), :], out_refs[q * npiece + i].at[0],
                                          out_sems.at[4 * s + q]) for q in range(4)]

        for cp in loads(0)[0]:
            cp.start()
        for i in range(npiece):
            s, r = i % 2, PIECE_ROWS[pieces[i]]
            if i + 1 < npiece:
                for cp in loads(i + 1)[0]:
                    cp.start()
            for cp in loads(i)[1]:
                cp.wait()
            if i >= 2:
                for cp in stores(i - 2):
                    cp.wait()
            g = inb[s, 3, 0:r, :]
            for p in range(nland):
                g = g + landb[s, p, 0:r, :].astype(f32)
            d, nm, nv = _adamw_math(inb[s, 0, 0:r, :], g, inb[s, 1, 0:r, :], inb[s, 2, 0:r, :])
            outb[s, 0, 0:r, :] = g
            outb[s, 1, 0:r, :] = d
            outb[s, 2, 0:r, :] = nm
            outb[s, 3, 0:r, :] = nv
            for cp in stores(i):
                cp.start()
        for i in range(max(npiece - 2, 0), npiece):
            for cp in stores(i):
                cp.wait()

    hbm = pl.BlockSpec(memory_space=pl.ANY)
    outs = pl.pallas_call(
        body, in_specs=[hbm] * (3 * npiece + 1 + len(landed)), out_specs=[hbm] * (4 * npiece),
        out_shape=tuple(SDS(w.shape, f32) for _ in range(4) for w in ws),
        scratch_shapes=[pltpu.VMEM((2, 4, rmax, D), f32), pltpu.VMEM((2, nland, rmax, D), bf16),
                        pltpu.VMEM((2, 4, rmax, D), f32),
                        pltpu.SemaphoreType.DMA((8,)), pltpu.SemaphoreType.DMA((2 * nland,)),
                        pltpu.SemaphoreType.DMA((8,))],
        compiler_params=_cparams(None, VMEM_LIMIT_V7X), name=name)(*ws, *ms, *vs, own, *landed)
    return [list(outs[q * npiece:(q + 1) * npiece]) for q in range(4)]


def _adamw_small(ws, ms, vs, gs, name):
    n = len(ws)

    def body(*refs):
        w_refs, m_refs, v_refs, g_refs = refs[0:n], refs[n:2 * n], refs[2 * n:3 * n], refs[3 * n:4 * n]
        outs = refs[4 * n:]
        for i in range(n):
            d, nm, nv = _adamw_math(w_refs[i][...], g_refs[i][...], m_refs[i][...], v_refs[i][...])
            outs[i][...] = d
            outs[n + i][...] = nm
            outs[2 * n + i][...] = nv

    outs = pl.pallas_call(
        body, out_shape=tuple(SDS(w.shape, f32) for _ in range(3) for w in ws), name=name)(*ws, *ms, *vs, *gs)
    return [list(outs[q * n:(q + 1) * n]) for q in range(3)]


WEIGHTS = ("ffn1_norm", "ffn1_w_gate", "ffn1_w_up", "ffn1_w_down", "mix_norm", "w_in", "q_norm", "k_norm",
           "attn_sinks", "rel_bias", "pool_w", "pool_scale", "w_out", "ffn2_norm", "ffn2_w_gate", "ffn2_w_up",
           "ffn2_w_down")
BIG = (("ffn1_w_gate", True), ("ffn1_w_up", True), ("ffn1_w_down", False), ("w_in", True), ("w_out", False),
       ("ffn2_w_gate", True), ("ffn2_w_up", True), ("ffn2_w_down", False))


def kernel(x, ffn1_norm, ffn1_w_gate, ffn1_w_up, ffn1_w_down, mix_norm, w_in, q_norm, k_norm, attn_sinks, rel_bias, pool_w, pool_scale, w_out, ffn2_norm, ffn2_w_gate, ffn2_w_up, ffn2_w_down, loss_target, m_ffn1_norm, m_ffn1_w_gate, m_ffn1_w_up, m_ffn1_w_down, m_mix_norm, m_w_in, m_q_norm, m_k_norm, m_attn_sinks, m_rel_bias, m_pool_w, m_pool_scale, m_w_out, m_ffn2_norm, m_ffn2_w_gate, m_ffn2_w_up, m_ffn2_w_down, v_ffn1_norm, v_ffn1_w_gate, v_ffn1_w_up, v_ffn1_w_down, v_mix_norm, v_w_in, v_q_norm, v_k_norm, v_attn_sinks, v_rel_bias, v_pool_w, v_pool_scale, v_w_out, v_ffn2_norm, v_ffn2_w_gate, v_ffn2_w_up, v_ffn2_w_down):
    args = dict(locals())
    w = {n: args[n] for n in WEIGHTS}
    m = {n: args["m_" + n] for n in WEIGHTS}
    v = {n: args["v_" + n] for n in WEIGHTS}

    as_rows = lambda a, tr: jnp.swapaxes(a, 1, 2) if tr else a
    shard = jnp.concatenate([as_rows(w[n], tr)[0].astype(bf16) for n, tr in BIG], axis=0)
    exchanges = _GatheredWeights(shard, x[0], ffn1_norm)
    gx, (dw1, _, _, _), small = _local_step(
        x[0], loss_target[0], exchanges, ffn1_norm, mix_norm, ffn2_norm, q_norm, k_norm, attn_sinks,
        rel_bias, pool_w[0], pool_scale)

    nrows1 = len(G1_PIECES) * FS
    for_x, for_y, own1, small_tot = _reduce_scatter_ffn1_head(dw1, _pack_small(small))
    ssem, rsem, for_x, for_y, from_x, from_y, small_tot = _rs1_tail_start(
        for_x, for_y, lax.empty((nrows1, D), bf16), lax.empty((nrows1, D), bf16), small_tot)
    own_rest, landed_rest = exchanges.mix_ffn2_grads_parts(small_tot)

    grads, deltas, new_m, new_v = {}, {}, {}, {}
    rest = [k for k in range(len(BIG)) if k not in G1_PIECES]
    rows_of = lambda t, ks: [as_rows(t[BIG[k][0]], BIG[k][1]) for k in ks]
    rest_out = _adamw_big(rest, rows_of(w, rest), rows_of(m, rest), rows_of(v, rest), own_rest, [landed_rest],
                          "adamw_rest")
    from_x, from_y = _rs1_tail_wait(ssem, rsem, for_x, for_y, from_x, from_y, rest_out[0][0])
    ffn1 = list(G1_PIECES)
    ffn1_out = _adamw_big(ffn1, rows_of(w, ffn1), rows_of(m, ffn1), rows_of(v, ffn1), own1, [from_x, from_y],
                          "adamw_ffn1")
    for ks, out in ((rest, rest_out), (ffn1, ffn1_out)):
        for i, k in enumerate(ks):
            n, tr = BIG[k]
            grads[n], deltas[n], new_m[n], new_v[n] = [as_rows(o[i], tr) for o in out]
    small_names = [n for n in SMALL_NAMES if n != "loss"]
    for n in small_names:
        grads[n] = _unpack_small(small_tot, n)
    ds, nms, nvs = _adamw_small([w[n] for n in small_names], [m[n] for n in small_names], [v[n] for n in small_names],
                                [grads[n] for n in small_names], "adamw_small")
    for i, n in enumerate(small_names):
        deltas[n], new_m[n], new_v[n] = ds[i], nms[i], nvs[i]
    loss = small_tot[LOSS_ROW, 0]
    return (loss, gx[None], *[grads[n] for n in WEIGHTS], *[deltas[n] for n in WEIGHTS],
            *[new_m[n] for n in WEIGHTS], *[new_v[n] for n in WEIGHTS])
```

```python
import jax
import jax.numpy as jnp
import numpy as np
from jax import lax
from jax.experimental import pallas as pl
from jax.experimental.pallas import tpu as pltpu

f32, bf16, i32 = jnp.float32, jnp.bfloat16, jnp.int32
SDS = jax.ShapeDtypeStruct

D = 1024
F = 2816
HD = 64
NH = 8
NKV = 2
GQA = NH // NKV
DATTN = NH * HD
DKV = NKV * HD
DPOOL = 512
POOL_WINDOWS = (2, 4, 8, 16)
PGD = DPOOL // len(POOL_WINDOWS)
DIN = DATTN + 2 * DKV + DPOOL
DMIX = DATTN + DPOOL
BLK = 128
NBUCK = 32
MAX_DISTANCE = 128
EPS = 1e-6
NEG = -1e30
SCALE = HD ** -0.5

ADAM_LR, ADAM_B1, ADAM_B2, ADAM_EPS, ADAM_WD, ADAM_STEP = 0.001, 0.9, 0.999, 1e-08, 0.01, 10

NDEV = 8
FS = F // NDEV
INS = DIN // NDEV
OUTS = DMIX // NDEV
PIECE_ROWS = (FS, FS, FS, INS, OUTS, FS, FS, FS)
PIECE_OFF = tuple(int(v) for v in np.cumsum((0,) + PIECE_ROWS[:-1]))
PACK_ROWS = sum(PIECE_ROWS)

VMEM_LIMIT_V7X = 56 * 1024 * 1024

MESH = pl.DeviceIdType.MESH


def _cparams(sem=None, vmem=None):
    return pltpu.CompilerParams(dimension_semantics=sem, vmem_limit_bytes=vmem)


def _nt(a, b):
    return lax.dot_general(a, b, (((1,), (1,)), ((), ())), preferred_element_type=f32)


def _tn(a, b):
    return lax.dot_general(a, b, (((0,), (0,)), ((), ())), preferred_element_type=f32)


def _nn(a, b):
    return jnp.dot(a, b, preferred_element_type=f32)


def _sigmoid(x):
    return 1.0 / (1.0 + jnp.exp(-x))


def _norm_fwd(x, g, name):
    T = x.shape[0]
    tm = min(512, T)

    def body(x_ref, g_ref, h_ref):
        xv = x_ref[...]
        r = lax.rsqrt(jnp.mean(xv * xv, axis=-1, keepdims=True) + EPS)
        h_ref[...] = (xv * r * g_ref[...]).astype(bf16)

    return pl.pallas_call(
        body, grid=(T // tm,),
        in_specs=[pl.BlockSpec((tm, D), lambda i: (i, 0)), pl.BlockSpec((1, D), lambda i: (0, 0))],
        out_specs=pl.BlockSpec((tm, D), lambda i: (i, 0)),
        out_shape=SDS((T, D), bf16), name=name)(x, g)


FFN_ROW_CHUNK = 256


def _ffn_tiles(T):
    return min(1024, T), 256


def _ffn_fwd(h, w, x, target, next_gain, name):
    T = h.shape[0]
    tm, tf = _ffn_tiles(T)
    nf = F // tf
    with_loss = target is not None
    assert with_loss != (next_gain is not None)

    def body(*refs):
        if with_loss:
            h_ref, w_ref, x_hbm, t_hbm, xo_ref, g_ref, u_ref, dyb_ref, loss_ref, tbuf, sem = refs
        else:
            h_ref, w_ref, x_hbm, gain_ref, xo_ref, g_ref, u_ref, hn_ref, sem = refs
        fi = pl.program_id(0)

        @pl.when(fi == 0)
        def _():
            cp = pltpu.make_async_copy(x_hbm, xo_ref, sem)
            cp.start()
            cp.wait()

        wgu = w_ref[0:2].reshape(2 * tf, D)
        for r in range(0, T, tm):
            rows = slice(r, r + tm)
            gu = _nt(h_ref[rows, :], wgu)
            gate, up = gu[:, :tf], gu[:, tf:]
            act = gate * _sigmoid(gate) * up
            g_ref[0, rows, :] = gate.astype(bf16)
            u_ref[0, rows, :] = up.astype(bf16)
            xo_ref[rows, :] += _nn((0.5 * act).astype(bf16), w_ref[2])

        if with_loss:
            @pl.when(fi == nf - 1)
            def _():
                lanes = jnp.zeros((1, 128), f32)
                for r in range(0, T, tm):
                    rows = slice(r, r + tm)
                    cp = pltpu.make_async_copy(t_hbm.at[pl.ds(r, tm), :], tbuf, sem)
                    cp.start()
                    cp.wait()
                    e = xo_ref[rows, :] - tbuf[...]
                    dy = e * (1.0 / D)
                    xo_ref[rows, :] = dy
                    dyb_ref[rows, :] = (0.5 * dy).astype(bf16)
                    col = jnp.sum(e * e, axis=0, keepdims=True) * (0.5 / D)
                    for k in range(D // 128):
                        lanes = lanes + col[:, 128 * k:128 * (k + 1)]
                loss_ref[...] = lanes
        else:
            @pl.when(fi == nf - 1)
            def _():
                for r in range(0, T, FFN_ROW_CHUNK):
                    rows = slice(r, r + FFN_ROW_CHUNK)
                    xv = xo_ref[rows, :]
                    rstd = lax.rsqrt(jnp.mean(xv * xv, axis=-1, keepdims=True) + EPS)
                    hn_ref[rows, :] = (xv * rstd * gain_ref[...]).astype(bf16)

    tok = pl.BlockSpec((T, D), lambda f: (0, 0))
    act_spec = pl.BlockSpec((1, T, tf), lambda f: (f, 0, 0))
    hbm = pl.BlockSpec(memory_space=pl.ANY)
    in_specs = [tok, pl.BlockSpec((3, tf, D), lambda f: (0, f, 0)), hbm]
    out_specs = [tok, act_spec, act_spec]
    out_shape = [SDS((T, D), f32), SDS((nf, T, tf), bf16), SDS((nf, T, tf), bf16)]
    scratch = [pltpu.SemaphoreType.DMA]
    args = [h, w, x]
    if with_loss:
        in_specs.append(hbm)
        args.append(target)
        out_specs += [tok, pl.BlockSpec((1, 128), lambda f: (0, 0))]
        out_shape += [SDS((T, D), bf16), SDS((1, 128), f32)]
        scratch = [pltpu.VMEM((tm, D), f32)] + scratch
    else:
        in_specs.append(pl.BlockSpec((1, D), lambda f: (0, 0)))
        args.append(next_gain)
        out_specs.append(tok)
        out_shape.append(SDS((T, D), bf16))
    return pl.pallas_call(
        body, grid=(nf,), in_specs=in_specs, out_specs=out_specs, out_shape=tuple(out_shape), scratch_shapes=scratch,
        compiler_params=_cparams(("arbitrary",), VMEM_LIMIT_V7X), name=name)(*args)


def _ffn_bwd(dob, h, gate, up, w, norm, name):
    x, g, dres = norm
    T = h.shape[0]
    _, tf = _ffn_tiles(T)
    nf = F // tf
    tm = min(512, T)
    nchunk = T // tm

    def body(do_hbm, h_hbm, g_ref, u_ref, w_ref, x_hbm, gain_ref, dr_hbm, dx_hbm, dxb_hbm, dg_ref, dw_ref,
             do_v, h_v, dh_acc, dgu_s, act_s, xbuf, rbuf, obuf, obb, sems, in_sems, out_sems):
        fi = pl.program_id(0)

        @pl.when(fi == 0)
        def _():
            loads = [pltpu.make_async_copy(do_hbm, do_v, sems.at[0]), pltpu.make_async_copy(h_hbm, h_v, sems.at[1])]
            for cp in loads:
                cp.start()
            dh_acc[...] = jnp.zeros_like(dh_acc)
            for cp in loads:
                cp.wait()

        wgu = w_ref[0:2].reshape(2 * tf, D)
        for r in range(0, T, FFN_ROW_CHUNK):
            rows = slice(r, r + FFN_ROW_CHUNK)
            dov = do_v[rows, :]
            gv = g_ref[0, rows, :].astype(f32)
            uv = u_ref[0, rows, :].astype(f32)
            sg = _sigmoid(gv)
            sil = gv * sg
            dact = _nt(dov, w_ref[2])
            dup = dact * sil
            dgate = dact * uv * (sg * (1.0 + gv * (1.0 - sg)))
            dgu = jnp.concatenate([dgate.astype(bf16), dup.astype(bf16)], axis=1)
            dgu_s[rows, :] = dgu
            act_s[rows, :] = (sil * uv).astype(bf16)
            dh_acc[rows, :] += _nn(dgu, wgu)
        dw_ref[0:2] = _tn(dgu_s[...], h_v[...]).reshape(2, tf, D).astype(bf16)
        dw_ref[2] = _tn(act_s[...], do_v[...]).astype(bf16)

        @pl.when(fi == nf - 1)
        def _():
            def loads(i):
                s, rows = i % 2, pl.ds(i * tm, tm)
                return [pltpu.make_async_copy(x_hbm.at[rows, :], xbuf.at[s], in_sems.at[2 * s]),
                        pltpu.make_async_copy(dr_hbm.at[rows, :], rbuf.at[s], in_sems.at[2 * s + 1])]

            def stores(i):
                s, rows = i % 2, pl.ds(i * tm, tm)
                return [pltpu.make_async_copy(obuf.at[s], dx_hbm.at[rows, :], out_sems.at[2 * s]),
                        pltpu.make_async_copy(obb.at[s], dxb_hbm.at[rows, :], out_sems.at[2 * s + 1])]

            for cp in loads(0):
                cp.start()
            dg = jnp.zeros((1, D), f32)
            for i in range(nchunk):
                s = i % 2
                if i + 1 < nchunk:
                    for cp in loads(i + 1):
                        cp.start()
                for cp in loads(i):
                    cp.wait()
                if i >= 2:
                    for cp in stores(i - 2):
                        cp.wait()
                xv = xbuf[s]
                rstd = lax.rsqrt(jnp.mean(xv * xv, axis=-1, keepdims=True) + EPS)
                xh = xv * rstd
                dhv = dh_acc[i * tm:(i + 1) * tm, :]
                dxh = dhv * gain_ref[...]
                dx = rbuf[s] + rstd * (dxh - xh * jnp.mean(dxh * xh, axis=-1, keepdims=True))
                obuf[s] = dx
                obb[s] = dx.astype(bf16)
                dg = dg + jnp.sum(dhv * xh, axis=0, keepdims=True)
                for cp in stores(i):
                    cp.start()
            dg_ref[...] = dg
            for i in range(max(nchunk - 2, 0), nchunk):
                for cp in stores(i):
                    cp.wait()

    act_spec = pl.BlockSpec((1, T, tf), lambda f: (f, 0, 0))
    wspec = pl.BlockSpec((3, tf, D), lambda f: (0, f, 0))
    vec = pl.BlockSpec((1, D), lambda f: (0, 0))
    hbm = pl.BlockSpec(memory_space=pl.ANY)
    return pl.pallas_call(
        body, grid=(nf,),
        in_specs=[hbm, hbm, act_spec, act_spec, wspec, hbm, vec, hbm],
        out_specs=[hbm, hbm, vec, wspec],
        out_shape=(SDS((T, D), f32), SDS((T, D), bf16), SDS((1, D), f32), SDS((3, F, D), bf16)),
        scratch_shapes=[pltpu.VMEM((T, D), bf16), pltpu.VMEM((T, D), bf16), pltpu.VMEM((T, D), f32),
                        pltpu.VMEM((T, 2 * tf), bf16), pltpu.VMEM((T, tf), bf16),
                        pltpu.VMEM((2, tm, D), f32), pltpu.VMEM((2, tm, D), f32), pltpu.VMEM((2, tm, D), f32),
                        pltpu.VMEM((2, tm, D), bf16),
                        pltpu.SemaphoreType.DMA((2,)), pltpu.SemaphoreType.DMA((4,)), pltpu.SemaphoreType.DMA((4,))],
        compiler_params=_cparams(("arbitrary",), VMEM_LIMIT_V7X), name=name)(dob, h, gate, up, w, x, g, dres)


def _in_proj_fwd(h, wint, name):
    T = h.shape[0]
    tm = min(512, T)

    def body(h_ref, w_ref, z_ref):
        z_ref[...] = _nt(h_ref[...], w_ref[...])

    return pl.pallas_call(
        body, grid=(T // tm,),
        in_specs=[pl.BlockSpec((tm, D), lambda i: (i, 0)), pl.BlockSpec((DIN, D), lambda i: (0, 0))],
        out_specs=pl.BlockSpec((tm, DIN), lambda i: (i, 0)),
        out_shape=SDS((T, DIN), f32), name=name)(h, wint)


def _in_proj_bwd(dz, wint, h, norm, out_scale, name):
    x, g, dres = norm
    T = h.shape[0]
    tm = min(512, T)
    nt = T // tm

    def body(dz_ref, w_ref, h_ref, x_ref, g_ref, dr_ref, dx_ref, dxb_ref, dg_ref, dw_ref, acc):
        i = pl.program_id(0)
        dzb = dz_ref[...].astype(bf16)
        dhv = _nn(dzb, w_ref[...])
        part = _tn(dzb, h_ref[...])
        xv = x_ref[...]
        rstd = lax.rsqrt(jnp.mean(xv * xv, axis=-1, keepdims=True) + EPS)
        xh = xv * rstd
        dxh = dhv * g_ref[...]
        dx = dr_ref[...] + rstd * (dxh - xh * jnp.mean(dxh * xh, axis=-1, keepdims=True))
        dx_ref[...] = dx
        dxb_ref[...] = (out_scale * dx).astype(bf16)
        dg = jnp.sum(dhv * xh, axis=0, keepdims=True)

        @pl.when(i == 0)
        def _():
            acc[...] = part
            dg_ref[...] = dg

        @pl.when(i > 0)
        def _():
            acc[...] += part
            dg_ref[...] += dg

        @pl.when(i == nt - 1)
        def _():
            dw_ref[...] = acc[...].astype(bf16)

    wspec = pl.BlockSpec((DIN, D), lambda i: (0, 0))
    tok = pl.BlockSpec((tm, D), lambda i: (i, 0))
    vec = pl.BlockSpec((1, D), lambda i: (0, 0))
    return pl.pallas_call(
        body, grid=(nt,),
        in_specs=[pl.BlockSpec((tm, DIN), lambda i: (i, 0)), wspec, tok, tok, vec, tok],
        out_specs=[tok, tok, vec, wspec],
        out_shape=(SDS((T, D), f32), SDS((T, D), bf16), SDS((1, D), f32), SDS((DIN, D), bf16)),
        scratch_shapes=[pltpu.VMEM((DIN, D), f32)],
        compiler_params=_cparams(("arbitrary",)), name=name)(dz, wint, h, x, g, dres)


def _out_proj_fwd(ymix, wout, x, g, name):
    T = x.shape[0]
    tm = min(512, T)

    def body(y_ref, w_ref, x_ref, g_ref, o_ref, h_ref):
        o = x_ref[...] + _nn(y_ref[...], w_ref[...])
        o_ref[...] = o
        r = lax.rsqrt(jnp.mean(o * o, axis=-1, keepdims=True) + EPS)
        h_ref[...] = (o * r * g_ref[...]).astype(bf16)

    tok = pl.BlockSpec((tm, D), lambda i: (i, 0))
    return pl.pallas_call(
        body, grid=(T // tm,),
        in_specs=[pl.BlockSpec((tm, DMIX), lambda i: (i, 0)), pl.BlockSpec((DMIX, D), lambda i: (0, 0)), tok,
                  pl.BlockSpec((1, D), lambda i: (0, 0))],
        out_specs=[tok, tok], out_shape=(SDS((T, D), f32), SDS((T, D), bf16)), name=name)(ymix, wout, x, g)


def _out_proj_bwd(dxb, wout, ymix, name):
    T = dxb.shape[0]
    tm = min(512, T)
    nt = T // tm

    def body(dx_ref, w_ref, y_ref, dy_ref, dw_ref, acc):
        i = pl.program_id(0)
        dxv = dx_ref[...]
        dy_ref[...] = _nt(dxv, w_ref[...])
        part = _tn(y_ref[...], dxv)

        @pl.when(i == 0)
        def _():
            acc[...] = part

        @pl.when(i > 0)
        def _():
            acc[...] += part

        @pl.when(i == nt - 1)
        def _():
            dw_ref[...] = acc[...].astype(bf16)

    wspec = pl.BlockSpec((DMIX, D), lambda i: (0, 0))
    return pl.pallas_call(
        body, grid=(nt,),
        in_specs=[pl.BlockSpec((tm, D), lambda i: (i, 0)), wspec, pl.BlockSpec((tm, DMIX), lambda i: (i, 0))],
        out_specs=[pl.BlockSpec((tm, DMIX), lambda i: (i, 0)), wspec],
        out_shape=(SDS((T, DMIX), f32), SDS((DMIX, D), bf16)),
        scratch_shapes=[pltpu.VMEM((DMIX, D), f32)],
        compiler_params=_cparams(("arbitrary",)), name=name)(dxb, wout, ymix)


def _t5_bucket_table():
    ql = np.arange(BLK)[:, None]
    kl = np.arange(2 * BLK)[None, :]
    n = np.maximum(ql + BLK - kl, 0)
    max_exact = NBUCK // 2
    large = max_exact + (np.log(np.maximum(n, 1) / max_exact) / np.log(MAX_DISTANCE / max_exact)
                         * (NBUCK - max_exact)).astype(np.int32)
    large = np.minimum(large, NBUCK - 1)
    return np.where(n < max_exact, n, large).astype(np.int32)


def _fill_bias(bk_ref, rb_ref, bias_scr):
    bk = bk_ref[...]
    for h in range(NH):
        def step(b, acc, h=h):
            return acc + jnp.where(bk == b, rb_ref[b, h], 0.0)
        bias_scr[h] = lax.fori_loop(0, NBUCK, step, jnp.zeros((BLK, 2 * BLK), f32))


MIX_SUB = 4


class _Window:
    def __init__(self, zc_ref, zp_ref, n, s):
        self.blk = n * MIX_SUB + s
        self.cur = lambda a, b: zc_ref[s * BLK:(s + 1) * BLK, a:b]
        self.prev = (lambda a, b: zp_ref[:, a:b]) if s == 0 else (lambda a, b: zc_ref[(s - 1) * BLK:s * BLK, a:b])


def _attn_qkv(win, kh, qg, kg):
    kc = DATTN + HD * kh
    vc = DATTN + DKV + HD * kh
    kx = jnp.concatenate([win.prev(kc, kc + HD), win.cur(kc, kc + HD)], axis=0)
    vx = jnp.concatenate([win.prev(vc, vc + HD), win.cur(vc, vc + HD)], axis=0)
    qx = jnp.concatenate([win.cur(HD * (GQA * kh + g), HD * (GQA * kh + g + 1)) for g in range(GQA)], axis=0)
    rq = lax.rsqrt(jnp.mean(qx * qx, axis=-1, keepdims=True) + EPS)
    rk = lax.rsqrt(jnp.mean(kx * kx, axis=-1, keepdims=True) + EPS)
    qhat, khat = qx * rq, kx * rk
    return dict(qhat=qhat, khat=khat, rq=rq, rk=rk, qsb=(qhat * (qg * SCALE)).astype(bf16),
                knb=(khat * kg).astype(bf16), vb=vx.astype(bf16))


def _window_masks(n):
    row = lax.broadcasted_iota(i32, (GQA * BLK, 2 * BLK), 0) & (BLK - 1)
    col = lax.broadcasted_iota(i32, (GQA * BLK, 2 * BLK), 1)
    band = (col > row) & (col <= row + BLK)
    return band & ((col >= BLK) | (n > 0)), band


def _attn_probs(a, kh, sk_ref, bias_scr, mask):
    s = _nt(a["qsb"], a["knb"]) + bias_scr[GQA * kh:GQA * (kh + 1)].reshape(GQA * BLK, 2 * BLK)
    s = jnp.where(mask, s, NEG)
    ridx = lax.broadcasted_iota(i32, (GQA * BLK, 1), 0)
    sink = jnp.full((GQA * BLK, 1), sk_ref[GQA * kh + GQA - 1], f32)
    for g in range(GQA - 2, -1, -1):
        sink = jnp.where(ridx < (g + 1) * BLK, sk_ref[GQA * kh + g], sink)
    m = jnp.maximum(jnp.max(s, axis=-1, keepdims=True), sink)
    e = jnp.exp(s - m)
    den = jnp.sum(e, axis=-1, keepdims=True) + jnp.exp(sink - m)
    return e / den


POOL_STEPS = {2: (1,), 4: (1, 2), 8: (1, 2, 4), 16: (1, 2, 4, 8)}


def _pool_group(win, g, w):
    n = win.blk
    c0 = DATTN + 2 * DKV + PGD * g
    uc = win.cur(c0, c0 + PGD)
    up = jnp.where(n > 0, win.prev(c0, c0 + PGD), 0.0)
    sm = jnp.concatenate([up, uc], axis=0)
    for k in POOL_STEPS[w]:
        sm = sm + pltpu.roll(sm, k, axis=0)
    pos = n * BLK + lax.broadcasted_iota(i32, (BLK, 1), 0) + 1
    cnt = jnp.minimum(pos, w).astype(f32)
    return sm[BLK:2 * BLK] / cnt - uc, cnt


def _mix_fwd(z, qg, kg, sinks, relb, bucket, pool_w, pscale, name):
    T = z.shape[0]
    step_rows = MIX_SUB * BLK
    nsteps = T // step_rows

    def body(zc_ref, zp_ref, qg_ref, kg_ref, sk_ref, rb_ref, bk_ref, pw_ref, ps_ref, y_ref, p_ref, bias_scr, yacc):
        n = pl.program_id(0)

        @pl.when(n == 0)
        def _():
            _fill_bias(bk_ref, rb_ref, bias_scr)

        first_mask, mask = _window_masks(n)
        for s in range(MIX_SUB):
            win = _Window(zc_ref, zp_ref, n, s)
            rows = slice(s * BLK, (s + 1) * BLK)
            for kh in range(NKV):
                a = _attn_qkv(win, kh, qg_ref[...], kg_ref[...])
                pb = _attn_probs(a, kh, sk_ref, bias_scr, first_mask if s == 0 else mask).astype(bf16)
                p_ref[s, GQA * kh:GQA * (kh + 1)] = pb.reshape(GQA, BLK, 2 * BLK)
                o = _nn(pb, a["vb"])
                for g in range(GQA):
                    hc = HD * (GQA * kh + g)
                    yacc[rows, hc:hc + HD] = o[g * BLK:(g + 1) * BLK]
            for g, w in enumerate(POOL_WINDOWS):
                pooled, _ = _pool_group(win, g, w)
                yp = _nn(pooled.astype(bf16), pw_ref[g].astype(bf16)) * ps_ref[:, PGD * g:PGD * (g + 1)]
                yacc[rows, DATTN + PGD * g:DATTN + PGD * (g + 1)] = yp
        y_ref[...] = yacc[...].astype(bf16)

    full = lambda *shape: pl.BlockSpec(shape, lambda n: (0,) * len(shape))
    smem = pl.BlockSpec(memory_space=pltpu.SMEM)
    return pl.pallas_call(
        body, grid=(nsteps,),
        in_specs=[pl.BlockSpec((step_rows, DIN), lambda n: (n, 0)),
                  pl.BlockSpec((BLK, DIN), lambda n: (jnp.maximum(n * MIX_SUB - 1, 0), 0)),
                  full(1, HD), full(1, HD), smem, smem, full(BLK, 2 * BLK),
                  full(len(POOL_WINDOWS), PGD, PGD), full(1, DPOOL)],
        out_specs=[pl.BlockSpec((step_rows, DMIX), lambda n: (n, 0)),
                   pl.BlockSpec((MIX_SUB, NH, BLK, 2 * BLK), lambda n: (n, 0, 0, 0))],
        out_shape=(SDS((T, DMIX), bf16), SDS((T // BLK, NH, BLK, 2 * BLK), bf16)),
        scratch_shapes=[pltpu.VMEM((NH, BLK, 2 * BLK), f32), pltpu.VMEM((step_rows, DMIX), f32)],
        compiler_params=_cparams(("arbitrary",)), name=name)(z, z, qg, kg, sinks, relb, bucket, pool_w, pscale)


def _mix_bwd(z, dy, probs, qg, kg, relb, bucket, pool_w, pscale, name):
    T = z.shape[0]
    step_rows = MIX_SUB * BLK
    nsteps = T // step_rows

    def body(zc_ref, zp_ref, dy_ref, p_ref, qg_ref, kg_ref, bk_ref, pw_ref, ps_ref,
             dz_ref, dqg_ref, dkg_ref, dsk_ref, drb_ref, dpw_ref, dps_ref, dbias_scr):
        n = pl.program_id(0)

        @pl.when(n == 0)
        def _():
            dbias_scr[...] = jnp.zeros_like(dbias_scr)
            dqg_ref[...] = jnp.zeros_like(dqg_ref)
            dkg_ref[...] = jnp.zeros_like(dkg_ref)
            dpw_ref[...] = jnp.zeros_like(dpw_ref)
            dps_ref[...] = jnp.zeros_like(dps_ref)

        qg, kg = qg_ref[...], kg_ref[...]
        for s in range(MIX_SUB):
            win = _Window(zc_ref, zp_ref, n, s)
            blk = win.blk
            rows = pl.ds(pl.multiple_of(blk * BLK, BLK), BLK)
            prow = pl.ds(pl.multiple_of(jnp.maximum(blk - 1, 0) * BLK, BLK), BLK)
            dyr = slice(s * BLK, (s + 1) * BLK)

            def into_prev(fn, s=s):
                if s == 0:
                    pl.when(n > 0)(fn)
                else:
                    fn()

            for kh in range(NKV):
                a = _attn_qkv(win, kh, qg, kg)
                pb = p_ref[s, GQA * kh:GQA * (kh + 1)].reshape(GQA * BLK, 2 * BLK)
                p = pb.astype(f32)
                do = jnp.concatenate([dy_ref[dyr, HD * (GQA * kh + g):HD * (GQA * kh + g + 1)] for g in range(GQA)],
                                     axis=0).astype(bf16)
                dv = _tn(pb, do)
                dp = _nt(do, a["vb"])
                delta = jnp.sum(p * dp, axis=-1, keepdims=True)
                ds = p * (dp - delta)
                for g in range(GQA):
                    dbias_scr[GQA * kh + g] += ds[g * BLK:(g + 1) * BLK]
                dsb = ds.astype(bf16)
                dqn = _nn(dsb, a["knb"]) * SCALE
                dkn = _tn(dsb, a["qsb"])
                qhat, khat = a["qhat"], a["khat"]
                dqg_ref[...] += jnp.sum(dqn * qhat, axis=0, keepdims=True)
                dkg_ref[...] += jnp.sum(dkn * khat, axis=0, keepdims=True)
                dqh = dqn * qg
                dq = a["rq"] * (dqh - qhat * jnp.mean(dqh * qhat, axis=-1, keepdims=True))
                dkh = dkn * kg
                dk = a["rk"] * (dkh - khat * jnp.mean(dkh * khat, axis=-1, keepdims=True))
                kc = DATTN + HD * kh
                vc = DATTN + DKV + HD * kh
                for g in range(GQA):
                    hc = HD * (GQA * kh + g)
                    dz_ref[rows, hc:hc + HD] = dq[g * BLK:(g + 1) * BLK]
                dz_ref[rows, kc:kc + HD] = dk[BLK:2 * BLK]
                dz_ref[rows, vc:vc + HD] = dv[BLK:2 * BLK]

                def kv_prev(dk=dk, dv=dv, kc=kc, vc=vc, prow=prow):
                    dz_ref[prow, kc:kc + HD] += dk[0:BLK]
                    dz_ref[prow, vc:vc + HD] += dv[0:BLK]

                into_prev(kv_prev)

            for g, w in enumerate(POOL_WINDOWS):
                c0 = DATTN + 2 * DKV + PGD * g
                pooled, cnt = _pool_group(win, g, w)
                pb = pooled.astype(bf16)
                wb = pw_ref[g].astype(bf16)
                dyp = dy_ref[dyr, DATTN + PGD * g:DATTN + PGD * (g + 1)]
                ypre = _nn(pb, wb)
                dps_ref[:, PGD * g:PGD * (g + 1)] += jnp.sum(dyp * ypre, axis=0, keepdims=True)
                dyg = (dyp * ps_ref[:, PGD * g:PGD * (g + 1)]).astype(bf16)
                dpw_ref[g] += _tn(pb, dyg)
                dpooled = _nt(dyg, wb)
                due = jnp.concatenate([jnp.zeros((BLK, PGD), f32), dpooled / cnt], axis=0)
                for k in POOL_STEPS[w]:
                    due = due + pltpu.roll(due, 2 * BLK - k, axis=0)
                dz_ref[rows, c0:c0 + PGD] = due[BLK:2 * BLK] - dpooled

                def pool_prev(due=due, c0=c0, prow=prow):
                    dz_ref[prow, c0:c0 + PGD] += due[0:BLK]

                into_prev(pool_prev)

        @pl.when(n == nsteps - 1)
        def _():
            bk = bk_ref[...]
            ri = lax.broadcasted_iota(i32, (NBUCK, NH), 0)
            ci = lax.broadcasted_iota(i32, (NBUCK, NH), 1)

            def step(b, acc):
                for h in range(NH):
                    sel = jnp.where(bk == b, dbias_scr[h], 0.0)
                    tot = jnp.sum(jnp.sum(sel, axis=1, keepdims=True), axis=0, keepdims=True)
                    acc = acc + jnp.where((ri == b) & (ci == h), tot, 0.0)
                return acc

            drb_ref[...] = lax.fori_loop(0, NBUCK, step, jnp.zeros((NBUCK, NH), f32))
            lane = lax.broadcasted_iota(i32, (1, 128), 1)
            dsk = jnp.zeros((1, 128), f32)
            for h in range(NH):
                tot = jnp.sum(jnp.sum(dbias_scr[h], axis=1, keepdims=True), axis=0, keepdims=True)
                dsk = dsk - jnp.where(lane == h, tot, 0.0)
            dsk_ref[...] = dsk

    full = lambda *shape: pl.BlockSpec(shape, lambda n: (0,) * len(shape))
    npg = len(POOL_WINDOWS)
    return pl.pallas_call(
        body, grid=(nsteps,),
        in_specs=[pl.BlockSpec((step_rows, DIN), lambda n: (n, 0)),
                  pl.BlockSpec((BLK, DIN), lambda n: (jnp.maximum(n * MIX_SUB - 1, 0), 0)),
                  pl.BlockSpec((step_rows, DMIX), lambda n: (n, 0)),
                  pl.BlockSpec((MIX_SUB, NH, BLK, 2 * BLK), lambda n: (n, 0, 0, 0)),
                  full(1, HD), full(1, HD), full(BLK, 2 * BLK), full(npg, PGD, PGD), full(1, DPOOL)],
        out_specs=[full(T, DIN), full(1, HD), full(1, HD), full(1, 128), full(NBUCK, NH),
                   full(npg, PGD, PGD), full(1, DPOOL)],
        out_shape=(SDS((T, DIN), f32), SDS((1, HD), f32), SDS((1, HD), f32), SDS((1, 128), f32),
                   SDS((NBUCK, NH), f32), SDS((npg, PGD, PGD), f32), SDS((1, DPOOL), f32)),
        scratch_shapes=[pltpu.VMEM((NH, BLK, 2 * BLK), f32)],
        compiler_params=_cparams(("arbitrary",), VMEM_LIMIT_V7X),
        name=name)(z, z, dy, probs, qg, kg, bucket, pool_w, pscale)


class _LocalWeights:
    def __init__(self, w1, wint, wout, w2):
        self.w1, self.wint, self.wout, self.w2 = w1, wint, wout, w2

    def ffn1(self):
        return self.w1

    def first_norm(self, x, gain):
        return _norm_fwd(x, gain, "norm1_fwd")

    def after_ffn1(self, gain, x1):
        return gain

    def mix(self, after):
        return self.wint, self.wout

    def before_out_proj(self, wout, after):
        return wout

    def ffn2(self, after):
        return self.w2

    def out_ffn2_grads_ready(self, dwout, dw2, after):
        return after

    def before_ffn1_bwd(self, dwint, dx1b):
        return dx1b


def _local_step(x, target, weights, g1, gm, g3, qg, kg, sinks, relb, pool_w, pscale):
    bucket = jnp.asarray(_t5_bucket_table())
    sk = sinks.reshape(NH)
    w1 = weights.ffn1()
    h1 = weights.first_norm(x, g1)
    x1, gate1, up1, h2 = _ffn_fwd(h1, w1, x, None, gm, "ffn1_fwd")
    gm = weights.after_ffn1(gm, x1)
    wint, wout = weights.mix(h2)
    z = _in_proj_fwd(h2, wint, "in_proj_fwd")
    ymix, probs = _mix_fwd(z, qg, kg, sk, relb, bucket, pool_w, pscale, "mix_fwd")
    wout = weights.before_out_proj(wout, ymix)
    x2, h3 = _out_proj_fwd(ymix, wout, x1, g3, "out_proj_fwd")
    w2 = weights.ffn2(h3)
    dy, gate2, up2, dyb, loss_lanes = _ffn_fwd(h3, w2, x2, target, None, "ffn2_fwd")

    dx2, dx2b, dg3, dw2 = _ffn_bwd(dyb, h3, gate2, up2, w2, (x2, g3, dy), "ffn2_bwd")
    dymix, dwout = _out_proj_bwd(dx2b, wout, ymix, "out_proj_bwd")
    dymix = weights.out_ffn2_grads_ready(dwout, dw2, dymix)
    dz, dqg, dkg, dsk, drb, dpw, dps = _mix_bwd(z, dymix, probs, qg, kg, relb, bucket, pool_w, pscale, "mix_bwd")
    dx1, dx1b, dgm, dwint = _in_proj_bwd(dz, wint, h2, (x1, gm, dx2), 0.5, "in_proj_bwd")
    dx1b = weights.before_ffn1_bwd(dwint, dx1b)
    gx, _, dg1, dw1 = _ffn_bwd(dx1b, h1, gate1, up1, w1, (x, g1, dx1), "ffn1_bwd")
    small = dict(ffn1_norm=dg1, mix_norm=dgm, ffn2_norm=dg3, pool_scale=dps, q_norm=dqg, k_norm=dkg,
                 attn_sinks=dsk[:, :NH], rel_bias=drb, pool_w=dpw, loss=loss_lanes)
    return gx, (dw1, dwint, dwout, dw2), small


SMALL_NAMES = ("ffn1_norm", "mix_norm", "ffn2_norm", "pool_scale", "q_norm", "k_norm", "attn_sinks", "rel_bias",
               "pool_w", "loss")
SMALL_SHAPES = dict(ffn1_norm=(1, D), mix_norm=(1, D), ffn2_norm=(1, D), pool_scale=(1, DPOOL), q_norm=(1, HD),
                    k_norm=(1, HD), attn_sinks=(1, NH), rel_bias=(NBUCK, NH),
                    pool_w=(1, len(POOL_WINDOWS), PGD, PGD), loss=(1, 128))


def _small_rows(name):
    return -(-int(np.prod(SMALL_SHAPES[name])) // 128)


SMALL_OFF = {}
_r = 0
for _n in SMALL_NAMES:
    SMALL_OFF[_n] = _r
    _r += _small_rows(_n)
SMALL_ROWS = -(-_r // 8) * 8
LOSS_ROW = SMALL_OFF["loss"]


def _pack_small(vals):
    parts = []
    for n in SMALL_NAMES:
        size = _small_rows(n) * 128
        if n in vals:
            flat = vals[n].astype(f32).reshape(-1)
            parts.append(jnp.pad(flat, (0, size - flat.shape[0])))
        else:
            parts.append(jnp.zeros((size,), f32))
    flat = jnp.concatenate(parts)
    flat = jnp.pad(flat, (0, SMALL_ROWS * 128 - flat.shape[0]))
    return flat.reshape(SMALL_ROWS, 128)


def _unpack_small(packed, name):
    size = int(np.prod(SMALL_SHAPES[name]))
    r0 = SMALL_OFF[name]
    return packed[r0:r0 + _small_rows(name)].reshape(-1)[:size].reshape(SMALL_SHAPES[name])


def _position():
    return lax.axis_index("x"), lax.axis_index("y"), lax.axis_index("c")


def _dev_index(x, y, c):
    return 4 * x + 2 * y + c


G1_PIECES, MIX_PIECES, F2_PIECES = (0, 1, 2), (3, 4), (5, 6, 7)


def _group_rows(pieces):
    return sum(PIECE_ROWS[k] for k in pieces)


def _shard_piece(s_ref, k):
    return s_ref.at[pl.ds(PIECE_OFF[k], PIECE_ROWS[k]), :]


def _shard_group(s_ref, pieces):
    return s_ref.at[pl.ds(PIECE_OFF[pieces[0]], _group_rows(pieces)), :]


def _weight_pieces(w1_ref=None, wi_ref=None, wo_ref=None, w2_ref=None):
    arrs = {}
    if w1_ref is not None:
        arrs.update({0: w1_ref.at[0], 1: w1_ref.at[1], 2: w1_ref.at[2]})
    if wi_ref is not None:
        arrs[3] = wi_ref
    if wo_ref is not None:
        arrs[4] = wo_ref
    if w2_ref is not None:
        arrs.update({5: w2_ref.at[0], 6: w2_ref.at[1], 7: w2_ref.at[2]})
    return arrs


def _block_rows(arrs, k, dev):
    r = PIECE_ROWS[k]
    return arrs[k].at[pl.ds(pl.multiple_of(_dev_index(*dev) * r, 16), r), :]


NORM_ROWS = 512


def _all_gather_ffn1(shard, x, gain):
    pieces = G1_PIECES
    rest_pieces = MIX_PIECES + F2_PIECES
    half = FS // 2
    T = x.shape[0]
    SIB, X0, X1, Y0, Y1, RELAY_Y, RELAY_X, ON_X, ON_Y, ON_D0, ON_D1 = range(11)

    def body(s_ref, x_ref, g_ref, w1_ref, h_ref, wi_ref, wo_ref, w2_ref, xbuf, hbuf, rest_buf,
             send_sems, recv_sems, local_sem, norm_sems):
        x, y, c = _position()
        me, sib = (x, y, c), (x, y, 1 - c)
        xn, yn, dg = (1 - x, y, c), (x, 1 - y, c), (1 - x, 1 - y, c)
        arrs = _weight_pieces(w1_ref=w1_ref)

        def place_rest():
            rest = _weight_pieces(wi_ref=wi_ref, wo_ref=wo_ref, w2_ref=w2_ref)
            grp = _shard_group(s_ref, rest_pieces)
            load = pltpu.make_async_copy(grp, rest_buf, norm_sems.at[0])
            load.start()
            load.wait()
            base = PIECE_OFF[rest_pieces[0]]
            for k in rest_pieces:
                pltpu.make_async_copy(rest_buf.at[pl.ds(PIECE_OFF[k] - base, PIECE_ROWS[k]), :],
                                      _block_rows(rest, k, me), norm_sems.at[1]).start()
            pltpu.make_async_copy(grp, rest_buf, norm_sems.at[1]).wait()

        def first_norm():
            for r in range(0, T, NORM_ROWS):
                load = pltpu.make_async_copy(x_ref.at[pl.ds(r, NORM_ROWS), :], xbuf, norm_sems.at[0])
                load.start()
                load.wait()
                xv = xbuf[...]
                rs = lax.rsqrt(jnp.mean(xv * xv, axis=-1, keepdims=True) + EPS)
                hbuf[...] = (xv * rs * g_ref[...]).astype(bf16)
                store = pltpu.make_async_copy(hbuf, h_ref.at[pl.ds(r, NORM_ROWS), :], norm_sems.at[1])
                store.start()
                store.wait()

        def rows_of(k, block, hf):
            r = PIECE_ROWS[k]
            start, size = (0, r) if hf is None else (hf * half, half)
            return arrs[k].at[pl.ds(pl.multiple_of(_dev_index(*block) * r + start, 16), size), :]

        def copies(rel, block, hf, to, from_shard=False):
            def src(k):
                if not from_shard:
                    return rows_of(k, block, hf)
                start, size = (0, PIECE_ROWS[k]) if hf is None else (hf * half, half)
                return s_ref.at[pl.ds(PIECE_OFF[k] + start, size), :]
            return [pltpu.make_async_remote_copy(
                src_ref=src(k), dst_ref=rows_of(k, block, hf), send_sem=send_sems.at[rel], recv_sem=recv_sems.at[rel],
                device_id=to, device_id_type=MESH) for k in pieces]

        def waiter(rel, hf):
            nrows = len(pieces) * (FS if hf is None else half)
            grp = s_ref.at[pl.ds(0, nrows), :]
            return pltpu.make_async_remote_copy(src_ref=grp, dst_ref=grp, send_sem=send_sems.at[rel],
                                                recv_sem=recv_sems.at[rel], device_id=me, device_id_type=MESH)

        def start(cps):
            for cp in cps:
                cp.start()

        mine = [pltpu.make_async_copy(_shard_piece(s_ref, k), _block_rows(arrs, k, me), local_sem) for k in pieces]
        start(mine)
        start(copies(SIB, me, None, sib, True))
        start(copies(X0, me, 0, xn, True))
        start(copies(Y1, me, 1, yn, True))
        start(copies(X1, me, 1, xn, True))
        start(copies(Y0, me, 0, yn, True))
        first_norm()
        place_rest()
        waiter(X0, 0).wait_recv()
        start(copies(RELAY_Y, xn, 0, yn))
        waiter(Y1, 1).wait_recv()
        start(copies(RELAY_X, yn, 1, xn))
        waiter(X1, 1).wait_recv()
        start(copies(ON_X, xn, None, sib))
        waiter(Y0, 0).wait_recv()
        start(copies(ON_Y, yn, None, sib))
        waiter(RELAY_Y, 0).wait_recv()
        start(copies(ON_D0, dg, 0, sib))
        waiter(RELAY_X, 1).wait_recv()
        start(copies(ON_D1, dg, 1, sib))
        waiter(SIB, None).wait_recv()
        waiter(ON_X, None).wait_recv()
        waiter(ON_Y, None).wait_recv()
        waiter(ON_D0, 0).wait_recv()
        waiter(ON_D1, 1).wait_recv()
        for rel, hf in ((SIB, None), (X0, 0), (X1, 1), (Y0, 0), (Y1, 1), (RELAY_Y, 0), (RELAY_X, 1),
                        (ON_X, None), (ON_Y, None), (ON_D0, 0), (ON_D1, 1)):
            waiter(rel, hf).wait_send()
        grp = _shard_group(s_ref, pieces)
        pltpu.make_async_copy(grp, grp, local_sem).wait()

    hbm = pl.BlockSpec(memory_space=pl.ANY)
    return pl.pallas_call(
        body, in_specs=[hbm, hbm, pl.BlockSpec(memory_space=pltpu.VMEM)], out_specs=[hbm] * 5,
        out_shape=(SDS((3, F, D), bf16), SDS((T, D), bf16),
                   SDS((DIN, D), bf16), SDS((DMIX, D), bf16), SDS((3, F, D), bf16)),
        scratch_shapes=[pltpu.VMEM((NORM_ROWS, D), f32), pltpu.VMEM((NORM_ROWS, D), bf16),
                        pltpu.VMEM((_group_rows(rest_pieces), D), bf16),
                        pltpu.SemaphoreType.DMA((11,)), pltpu.SemaphoreType.DMA((11,)), pltpu.SemaphoreType.DMA,
                        pltpu.SemaphoreType.DMA((2,))],
        compiler_params=pltpu.CompilerParams(has_side_effects=True),
        name="all_gather_ffn1")(shard, x, gain)


HBM_SPEC = pl.BlockSpec(memory_space=pltpu.HBM)
SEM_SPEC = pl.BlockSpec(memory_space=pltpu.SEMAPHORE)
ANY_SPEC = pl.BlockSpec(memory_space=pl.ANY)
SPLIT_EFFECT = pltpu.SideEffectType.DATAFLOW_SIDE_EFFECTING


def _in_hbm(a):
    return pltpu.with_memory_space_constraint(a, pltpu.HBM)


def _hbm_like(a):
    return pltpu.HBM(a.shape, a.dtype)


def _gather_rest_start(shard, wi, wo, w2, w1):
    def body(s_ref, wi_ref, wo_ref, w2_ref, w1_ref,
             ssem_m, rsem_m0, rsem_m, ssem_f, rsem_f0, rsem_f, s_o, wi_o, wo_o, w2_o, w1_o):
        x, y, c = _position()
        me, sib = (x, y, c), (x, y, 1 - c)
        chips = [(1 - x, y), (x, 1 - y), (1 - x, 1 - y)]
        arrs = _weight_pieces(wi_ref=wi_ref, wo_ref=wo_ref, w2_ref=w2_ref)
        for pieces, ssem, rsem0, rsem in ((MIX_PIECES, ssem_m, rsem_m0, rsem_m), (F2_PIECES, ssem_f, rsem_f0, rsem_f)):
            for p in pieces:
                pltpu.make_async_remote_copy(
                    src_ref=_shard_piece(s_ref, p), dst_ref=_block_rows(arrs, p, me), send_sem=ssem.at[0],
                    recv_sem=rsem0, device_id=sib, device_id_type=MESH).start()
            for j, chip in enumerate(chips):
                for p in pieces:
                    pltpu.make_async_remote_copy(
                        src_ref=_shard_piece(s_ref, p), dst_ref=_block_rows(arrs, p, me), send_sem=ssem.at[1 + j],
                        recv_sem=rsem.at[j], device_id=(*chip, c), device_id_type=MESH).start()

    dma = pltpu.SemaphoreType.DMA
    return pl.pallas_call(
        body, name="gather_rest_start",
        out_shape=(dma((4,)), dma(()), dma((3,)), dma((4,)), dma(()), dma((3,)),
                   _hbm_like(shard), _hbm_like(wi), _hbm_like(wo), _hbm_like(w2), _hbm_like(w1)),
        in_specs=(HBM_SPEC,) * 5, out_specs=(SEM_SPEC,) * 6 + (HBM_SPEC,) * 5,
        input_output_aliases={0: 6, 1: 7, 2: 8, 3: 9, 4: 10},
        compiler_params=pltpu.CompilerParams(has_side_effects=SPLIT_EFFECT),
    )(_in_hbm(shard), _in_hbm(wi), _in_hbm(wo), _in_hbm(w2), _in_hbm(w1))


def _gather_mix_pass_on(rsem_m, wi, wo, thru, after):
    def body(wi_ref, wo_ref, thru_ref, rsem, after_ref, fsend, frecv, wi_o, wo_o, thru_o):
        x, y, c = _position()
        sib = (x, y, 1 - c)
        arrs = _weight_pieces(wi_ref=wi_ref, wo_ref=wo_ref)
        both = wi_ref.at[pl.ds(0, _group_rows(MIX_PIECES)), :]
        for j, chip in enumerate([(1 - x, y), (x, 1 - y), (1 - x, 1 - y)]):
            pltpu.make_async_remote_copy(src_ref=both, dst_ref=both, send_sem=fsend.at[j], recv_sem=rsem.at[j],
                                         device_id=(x, y, c), device_id_type=MESH).wait_recv()
            for p in MIX_PIECES:
                rows = _block_rows(arrs, p, (*chip, c))
                pltpu.make_async_remote_copy(src_ref=rows, dst_ref=rows, send_sem=fsend.at[j], recv_sem=frecv.at[j],
                                             device_id=sib, device_id_type=MESH).start()

    dma = pltpu.SemaphoreType.DMA
    return pl.pallas_call(
        body, name="gather_mix_pass_on",
        out_shape=(dma((3,)), dma((3,)), _hbm_like(wi), _hbm_like(wo), _hbm_like(thru)),
        in_specs=(HBM_SPEC, HBM_SPEC, HBM_SPEC, SEM_SPEC, ANY_SPEC), out_specs=(SEM_SPEC, SEM_SPEC) + (HBM_SPEC,) * 3,
        input_output_aliases={0: 2, 1: 3, 2: 4},
        compiler_params=pltpu.CompilerParams(has_side_effects=SPLIT_EFFECT),
    )(wi, wo, _in_hbm(thru), rsem_m, after)


def _gather_mix_wait(ssem_m, rsem_m0, fsend, frecv, shard, wi, wo, after):
    def body(s_ref, wi_ref, wo_ref, ssem, rsem0, fs, fr, after_ref, s_o, wi_o, wo_o):
        x, y, c = _position()
        grp = _shard_group(s_ref, MIX_PIECES)

        def waiter(send_sem, recv_sem):
            return pltpu.make_async_remote_copy(src_ref=grp, dst_ref=grp, send_sem=send_sem, recv_sem=recv_sem,
                                                device_id=(x, y, c), device_id_type=MESH)

        waiter(ssem.at[0], rsem0).wait_recv()
        for j in range(3):
            waiter(fs.at[j], fr.at[j]).wait_recv()
        for rel in range(4):
            waiter(ssem.at[rel], rsem0).wait_send()
        for j in range(3):
            waiter(fs.at[j], fr.at[j]).wait_send()

    return pl.pallas_call(
        body, name="gather_mix_wait", out_shape=(_hbm_like(shard), _hbm_like(wi), _hbm_like(wo)),
        in_specs=(HBM_SPEC,) * 3 + (SEM_SPEC,) * 4 + (ANY_SPEC,), out_specs=(HBM_SPEC,) * 3,
        input_output_aliases={0: 0, 1: 1, 2: 2},
        compiler_params=pltpu.CompilerParams(has_side_effects=SPLIT_EFFECT),
    )(shard, wi, wo, ssem_m, rsem_m0, fsend, frecv, after)


def _gather_ffn2_pass_on(rsem_f, w2, wo, after):
    def body(w2_ref, wo_ref, rsem, after_ref, fsend, frecv, w2_o, wo_o):
        x, y, c = _position()
        sib = (x, y, 1 - c)
        chips = [(1 - x, y), (x, 1 - y), (1 - x, 1 - y)]
        arrs = _weight_pieces(w2_ref=w2_ref)
        three = w2_ref.at[0, pl.ds(0, _group_rows(F2_PIECES)), :]
        for j, chip in enumerate(chips):
            pltpu.make_async_remote_copy(src_ref=three, dst_ref=three, send_sem=fsend.at[j], recv_sem=rsem.at[j],
                                         device_id=(x, y, c), device_id_type=MESH).wait_recv()
            for p in F2_PIECES:
                rows = _block_rows(arrs, p, (*chip, c))
                pltpu.make_async_remote_copy(src_ref=rows, dst_ref=rows, send_sem=fsend.at[j], recv_sem=frecv.at[j],
                                             device_id=sib, device_id_type=MESH).start()

    dma = pltpu.SemaphoreType.DMA
    return pl.pallas_call(
        body, name="gather_ffn2_pass_on", out_shape=(dma((3,)), dma((3,)), _hbm_like(w2), _hbm_like(wo)),
        in_specs=(HBM_SPEC, HBM_SPEC, SEM_SPEC, ANY_SPEC), out_specs=(SEM_SPEC, SEM_SPEC, HBM_SPEC, HBM_SPEC),
        input_output_aliases={0: 2, 1: 3},
        compiler_params=pltpu.CompilerParams(has_side_effects=SPLIT_EFFECT),
    )(w2, wo, rsem_f, after)


def _gather_ffn2_wait(ssem_f, rsem_f0, fsend, frecv, shard, w2, after):
    def body(s_ref, w2_ref, ssem, rsem0, fs, fr, after_ref, w2_o):
        x, y, c = _position()
        grp = _shard_group(s_ref, F2_PIECES)

        def waiter(send_sem, recv_sem):
            return pltpu.make_async_remote_copy(src_ref=grp, dst_ref=grp, send_sem=send_sem, recv_sem=recv_sem,
                                                device_id=(x, y, c), device_id_type=MESH)

        waiter(ssem.at[0], rsem0).wait_recv()
        for j in range(3):
            waiter(fs.at[j], fr.at[j]).wait_recv()
        for rel in range(4):
            waiter(ssem.at[rel], rsem0).wait_send()
        for j in range(3):
            waiter(fs.at[j], fr.at[j]).wait_send()

    return pl.pallas_call(
        body, name="gather_ffn2_wait", out_shape=_hbm_like(w2),
        in_specs=(HBM_SPEC, HBM_SPEC, SEM_SPEC, SEM_SPEC, SEM_SPEC, SEM_SPEC, ANY_SPEC), out_specs=HBM_SPEC,
        input_output_aliases={1: 0},
        compiler_params=pltpu.CompilerParams(has_side_effects=SPLIT_EFFECT),
    )(shard, w2, ssem_f, rsem_f0, fsend, frecv, after)


class _GatheredWeights(_LocalWeights):
    def __init__(self, shard, x, gain1):
        w1, self.h1, wi, wo, w2 = _all_gather_ffn1(shard, x, gain1)
        (self.ssem_m, self.rsem_m0, self.rsem_m, self.ssem_f, self.rsem_f0, self.rsem_f,
         self.shard, self.wi, self.wo, self.w2_part, self.w1) = _gather_rest_start(shard, wi, wo, w2, w1)

    def first_norm(self, x, gain):
        return self.h1

    def after_ffn1(self, gain, x1):
        self.fsend_m, self.frecv_m, self.wi, self.wo, gain = _gather_mix_pass_on(self.rsem_m, self.wi, self.wo, gain, x1)
        return gain

    def mix(self, after):
        self.shard, wint, wout = _gather_mix_wait(self.ssem_m, self.rsem_m0, self.fsend_m, self.frecv_m, self.shard,
                                                  self.wi, self.wo, after)
        return wint, wout

    def before_out_proj(self, wout, after):
        self.fsend, self.frecv, self.w2_part, wout = _gather_ffn2_pass_on(self.rsem_f, self.w2_part, wout, after)
        return wout

    def ffn2(self, after):
        return _gather_ffn2_wait(self.ssem_f, self.rsem_f0, self.fsend, self.frecv, self.shard, self.w2_part, after)

    def out_ffn2_grads_ready(self, dwout, dw2, after):
        rx1 = lax.empty((4, RSA_ROWS, D), bf16)
        sa, ra, sent, rx1, after = _rsa_level1_start(dict(wo=dwout, w2=dw2), rx1, after, "rsa_level1_start_out_ffn2")
        self.level1 = ((sa, ra), sent, rx1)
        return after

    def before_ffn1_bwd(self, dwint, dx1b):
        early, sent, rx1 = self.level1
        sa, ra, late, rx1, dx1b = _rsa_level1_start(dict(wi=dwint), rx1, dx1b, "rsa_level1_start_in")
        groups = (((MIX_PIECES[1],) + F2_PIECES, *early), ((MIX_PIECES[0],), sa, ra))
        dwint, dwout, dw2, rx1 = _rsa_level1_wait(groups, late["wi"], sent["wo"], sent["w2"], rx1, dx1b)
        tx, self.acc = _rsa_chip_sums(dwint, dwout, dw2, rx1)
        rx2 = lax.empty((3, RSA_ROWS, D), bf16)
        self.sb, self.rb, self.tx, self.rx2, dx1b = _rsa_level2_start(tx, rx2, dx1b)
        return dx1b

    def mix_ffn2_grads_parts(self, after):
        rx2 = _rsa_level2_wait(self.sb, self.rb, self.tx, self.rx2, after)
        return self.acc, rx2


def _reduce_scatter_ffn1_head(dw1, small_packed):
    pieces = G1_PIECES
    half = FS // 2
    hrows = len(pieces) * half
    nrows = 2 * hrows
    X_RELAY, Y_RELAY = range(2)

    def body(d1_ref, p_ref, forx_ref, fory_ref, fromx_ref, fromy_ref, own_ref, rx1_ref, relx_ref, rely_ref, tot_ref,
             own_buf, rx_buf, tx1, tx2, tx3, acc, sa, ra, sb, rb, sc, rc, lsem, pair, chips, small_send, small_recv):
        x, y, c = _position()
        me, sib = (x, y, c), (x, y, 1 - c)
        xn, yn = (1 - x, y, c), (x, 1 - y, c)
        rel_chips = [(x, y), (1 - x, y), (x, 1 - y), (1 - x, 1 - y)]
        srcs = _weight_pieces(w1_ref=d1_ref)

        my_chip = 2 * x + y
        pair[c] = p_ref[...]
        swap = pltpu.make_async_remote_copy(
            src_ref=p_ref, dst_ref=pair.at[c], send_sem=small_send.at[0], recv_sem=small_recv.at[0],
            device_id=sib, device_id_type=MESH)
        swap.start()
        small = [pltpu.make_async_remote_copy(
            src_ref=chips.at[my_chip], dst_ref=chips.at[my_chip], send_sem=small_send.at[j], recv_sem=small_recv.at[j],
            device_id=(*rel_chips[j], c), device_id_type=MESH) for j in (1, 2, 3)]

        def part(k, dev, hf):
            r = PIECE_ROWS[k]
            return srcs[k].at[pl.ds(pl.multiple_of(_dev_index(*dev) * r + hf * half, 16), half), :]

        def slot(ref, k, hf):
            return ref.at[pl.ds(hf * hrows + k * half, half), :]

        halves = [(k, hf) for hf in (0, 1) for k in pieces]

        for j in (3, 1, 2, 0):
            for k, hf in halves:
                pltpu.make_async_remote_copy(
                    src_ref=part(k, (*rel_chips[j], 1 - c), hf), dst_ref=slot(rx1_ref.at[j], k, hf),
                    send_sem=sa.at[j], recv_sem=ra.at[j], device_id=sib, device_id_type=MESH).start()

        def wait_a(j):
            return pltpu.make_async_remote_copy(src_ref=rx1_ref.at[j], dst_ref=rx1_ref.at[j], send_sem=sa.at[j],
                                                recv_sem=ra.at[j], device_id=me, device_id_type=MESH)

        def ici(rel, src, dst, to):
            return pltpu.make_async_remote_copy(src_ref=src, dst_ref=dst, send_sem=sb.at[rel], recv_sem=rb.at[rel],
                                                device_id=to, device_id_type=MESH)

        first, second = pl.ds(0, hrows), pl.ds(hrows, hrows)
        sends = {
            X_RELAY: ici(X_RELAY, tx3.at[first, :], relx_ref, xn),
            Y_RELAY: ici(Y_RELAY, tx3.at[second, :], rely_ref, yn),
        }
        direct = [pltpu.make_async_remote_copy(src_ref=src.at[rows, :], dst_ref=dst.at[rows, :], send_sem=sc.at[rel],
                                               recv_sem=rc.at[rel], device_id=to, device_id_type=MESH)
                  for rel, (src, dst, rows, to) in enumerate(((tx1, fromx_ref, first, xn), (tx2, fromy_ref, second, yn)))]

        swap.wait_recv()
        chips[my_chip] = pair[0] + pair[1]
        for cp in small:
            cp.start()

        def chip_sum(j, dst):
            loads = [pltpu.make_async_copy(part(k, (*rel_chips[j], c), hf), slot(own_buf, k, hf), lsem.at[0])
                     for k, hf in halves]
            for cp in loads:
                cp.start()
            wait_a(j).wait_recv()
            got = pltpu.make_async_copy(rx1_ref.at[j], rx_buf, lsem.at[1])
            got.start()
            pltpu.make_async_copy(rx_buf, rx_buf, lsem.at[0]).wait()
            got.wait()

            def add(i, carry):
                rows = pl.ds(pl.multiple_of(i * half, 16), half)
                tot = own_buf[rows, :].astype(f32) + rx_buf[rows, :].astype(f32)
                dst[rows, :] = tot.astype(dst.dtype)
                return carry

            lax.fori_loop(0, nrows // half, add, 0)

        def add_landed(landed, dst, rows0, nrows_):
            got = pltpu.make_async_copy(landed, rx_buf.at[pl.ds(0, nrows_), :], lsem.at[1])
            got.start()
            got.wait()

            def add(i, carry):
                src_rows = pl.ds(pl.multiple_of(i * half, 16), half)
                dst_rows = pl.ds(pl.multiple_of(rows0 + i * half, 16), half)
                dst[dst_rows, :] = (dst[dst_rows, :].astype(f32) + rx_buf[src_rows, :].astype(f32)).astype(dst.dtype)
                return carry

            lax.fori_loop(0, nrows_ // half, add, 0)

        chip_sum(3, tx3)
        sends[X_RELAY].start()
        sends[Y_RELAY].start()
        chip_sum(1, tx1)
        direct[0].start()
        chip_sum(2, tx2)
        direct[1].start()
        chip_sum(0, acc)
        own_out = pltpu.make_async_copy(acc, own_ref, lsem.at[0])
        own_out.start()
        sends[X_RELAY].wait_recv()
        add_landed(relx_ref, tx2, 0, hrows)
        sends[Y_RELAY].wait_recv()
        add_landed(rely_ref, tx1, hrows, hrows)
        own_out.wait()
        outs = [pltpu.make_async_copy(tx1.at[second, :], forx_ref, lsem.at[0]),
                pltpu.make_async_copy(tx2.at[first, :], fory_ref, lsem.at[1])]
        for cp in outs:
            cp.start()
        for cp in outs:
            cp.wait()
        for cp in small:
            cp.wait_recv()
        tot = (chips[0] + chips[1]) + (chips[2] + chips[3])
        tot_ref[...] = tot
        loss = jnp.sum(tot[LOSS_ROW:LOSS_ROW + 1, :], axis=-1, keepdims=True)
        tot_ref[LOSS_ROW:LOSS_ROW + 1, :] = jnp.broadcast_to(loss, (1, 128))
        for j in range(4):
            wait_a(j).wait_send()
        for cp in sends.values():
            cp.wait_send()
        for cp in direct:
            cp.wait_recv()
            cp.wait_send()
        swap.wait_send()
        for cp in small:
            cp.wait_send()

    hbm = pl.BlockSpec(memory_space=pl.ANY)
    vm = pl.BlockSpec(memory_space=pltpu.VMEM)
    outs = pl.pallas_call(
        body, in_specs=[hbm, vm], out_specs=[hbm] * 8 + [vm],
        out_shape=(SDS((hrows, D), bf16), SDS((hrows, D), bf16), SDS((nrows, D), bf16), SDS((nrows, D), bf16),
                   SDS((nrows, D), f32), SDS((4, nrows, D), bf16),
                   SDS((hrows, D), bf16), SDS((hrows, D), bf16), SDS((SMALL_ROWS, 128), f32)),
        scratch_shapes=[pltpu.VMEM((nrows, D), bf16), pltpu.VMEM((nrows, D), bf16),
                        pltpu.VMEM((nrows, D), bf16), pltpu.VMEM((nrows, D), bf16), pltpu.VMEM((nrows, D), bf16),
                        pltpu.VMEM((nrows, D), f32),
                        pltpu.SemaphoreType.DMA((4,)), pltpu.SemaphoreType.DMA((4,)),
                        pltpu.SemaphoreType.DMA((2,)), pltpu.SemaphoreType.DMA((2,)),
                        pltpu.SemaphoreType.DMA((2,)), pltpu.SemaphoreType.DMA((2,)), pltpu.SemaphoreType.DMA((2,)),
                        pltpu.VMEM((2, SMALL_ROWS, 128), f32), pltpu.VMEM((4, SMALL_ROWS, 128), f32),
                        pltpu.SemaphoreType.DMA((4,)), pltpu.SemaphoreType.DMA((4,))],
        compiler_params=pltpu.CompilerParams(has_side_effects=True, vmem_limit_bytes=VMEM_LIMIT_V7X),
        name="reduce_scatter_ffn1_head")(dw1, small_packed)
    return outs[0], outs[1], outs[2], outs[3], outs[4], outs[-1]


def _rs1_tail_start(for_x, for_y, from_x, from_y, thru):
    def body(fx_ref, fy_ref, lx_ref, ly_ref, thru_ref, ssem, rsem, fx_o, fy_o, lx_o, ly_o, thru_o):
        x, y, c = _position()
        hrows = fx_ref.shape[0]
        pltpu.make_async_remote_copy(src_ref=fx_ref, dst_ref=lx_ref.at[pl.ds(hrows, hrows), :], send_sem=ssem.at[0],
                                     recv_sem=rsem.at[0], device_id=(1 - x, y, c), device_id_type=MESH).start()
        pltpu.make_async_remote_copy(src_ref=fy_ref, dst_ref=ly_ref.at[pl.ds(0, hrows), :], send_sem=ssem.at[1],
                                     recv_sem=rsem.at[1], device_id=(x, 1 - y, c), device_id_type=MESH).start()

    dma = pltpu.SemaphoreType.DMA
    arrs = (for_x, for_y, from_x, from_y, thru)
    return pl.pallas_call(
        body, name="rs1_tail_start", out_shape=(dma((2,)), dma((2,))) + tuple(_hbm_like(a) for a in arrs),
        in_specs=(HBM_SPEC,) * 5, out_specs=(SEM_SPEC,) * 2 + (HBM_SPEC,) * 5,
        input_output_aliases={0: 2, 1: 3, 2: 4, 3: 5, 4: 6},
        compiler_params=pltpu.CompilerParams(has_side_effects=SPLIT_EFFECT),
    )(*[_in_hbm(a) for a in arrs])


def _rs1_tail_wait(ssem, rsem, for_x, for_y, from_x, from_y, after):
    def body(fx_ref, fy_ref, lx_ref, ly_ref, ssem_ref, rsem_ref, after_ref, lx_o, ly_o):
        x, y, c = _position()
        for j, src in enumerate((fx_ref, fy_ref)):
            d = pltpu.make_async_remote_copy(src_ref=src, dst_ref=src, send_sem=ssem_ref.at[j], recv_sem=rsem_ref.at[j],
                                             device_id=(x, y, c), device_id_type=MESH)
            d.wait_recv()
            d.wait_send()

    return pl.pallas_call(
        body, name="rs1_tail_wait", out_shape=(_hbm_like(from_x), _hbm_like(from_y)),
        in_specs=(HBM_SPEC,) * 4 + (SEM_SPEC, SEM_SPEC, ANY_SPEC), out_specs=(HBM_SPEC, HBM_SPEC),
        input_output_aliases={2: 0, 3: 1},
        compiler_params=pltpu.CompilerParams(has_side_effects=SPLIT_EFFECT),
    )(for_x, for_y, from_x, from_y, ssem, rsem, after)


RSA_PIECES = MIX_PIECES + F2_PIECES
RSA_ROWS = _group_rows(RSA_PIECES)
RSA_OFF = {k: PIECE_OFF[k] - PIECE_OFF[RSA_PIECES[0]] for k in RSA_PIECES}
RSA_BLOCK = 192


def _rsa_rows(ref, k):
    return ref.at[pl.ds(RSA_OFF[k], PIECE_ROWS[k]), :]


def _rsa_level1_start(grads, rx1, thru, name):
    keys = sorted(grads)
    n = len(keys)

    def body(*refs):
        srcs = _weight_pieces(**{k + "_ref": ref for k, ref in zip(keys, refs[:n])})
        rx1_ref, sa, ra = refs[n], refs[n + 2], refs[n + 3]
        x, y, c = _position()
        for j, chip in enumerate([(x, y), (1 - x, y), (x, 1 - y), (1 - x, 1 - y)]):
            for k in sorted(srcs):
                pltpu.make_async_remote_copy(
                    src_ref=_block_rows(srcs, k, (*chip, 1 - c)), dst_ref=_rsa_rows(rx1_ref.at[j], k),
                    send_sem=sa.at[j], recv_sem=ra.at[j], device_id=(x, y, 1 - c), device_id_type=MESH).start()

    dma = pltpu.SemaphoreType.DMA
    arrs = tuple(grads[k] for k in keys) + (rx1, thru)
    outs = pl.pallas_call(
        body, name=name, out_shape=(dma((4,)), dma((4,))) + tuple(_hbm_like(a) for a in arrs),
        in_specs=(HBM_SPEC,) * len(arrs), out_specs=(SEM_SPEC,) * 2 + (HBM_SPEC,) * len(arrs),
        input_output_aliases={i: i + 2 for i in range(len(arrs))},
        compiler_params=pltpu.CompilerParams(has_side_effects=SPLIT_EFFECT),
    )(*[_in_hbm(a) for a in arrs])
    return outs[0], outs[1], dict(zip(keys, outs[2:2 + n])), outs[2 + n], outs[3 + n]


def _rsa_level1_wait(groups, dwint, dwout, dw2, rx1, after):
    ngroup = len(groups)

    def body(di_ref, do_ref, d2_ref, rx1_ref, *rest):
        x, y, c = _position()
        for g, (pieces, _, _) in enumerate(groups):
            sa_ref, ra_ref = rest[2 * g], rest[2 * g + 1]
            for j in range(4):
                rows = rx1_ref.at[j, pl.ds(RSA_OFF[pieces[0]], _group_rows(pieces)), :]
                d = pltpu.make_async_remote_copy(src_ref=rows, dst_ref=rows, send_sem=sa_ref.at[j],
                                                 recv_sem=ra_ref.at[j], device_id=(x, y, c), device_id_type=MESH)
                d.wait_recv()
                d.wait_send()

    arrs = (dwint, dwout, dw2, rx1)
    sems = [sem for _, sa, ra in groups for sem in (sa, ra)]
    return pl.pallas_call(
        body, name="rsa_level1_wait", out_shape=tuple(_hbm_like(a) for a in arrs),
        in_specs=(HBM_SPEC,) * 4 + (SEM_SPEC,) * (2 * ngroup) + (ANY_SPEC,), out_specs=(HBM_SPEC,) * 4,
        input_output_aliases={0: 0, 1: 1, 2: 2, 3: 3},
        compiler_params=pltpu.CompilerParams(has_side_effects=SPLIT_EFFECT),
    )(*arrs, *sems, after)


def _rsa_chip_sums(dwint, dwout, dw2, rx1):
    nblk = RSA_ROWS // RSA_BLOCK

    def body(di_ref, do_ref, d2_ref, rx1_ref, tx_ref, acc_ref, own_buf, rx_buf, tx_buf, acc_buf, in_sems, out_sems):
        x, y, c = _position()
        srcs = _weight_pieces(wi_ref=di_ref, wo_ref=do_ref, w2_ref=d2_ref)
        chips = [(x, y), (1 - x, y), (x, 1 - y), (1 - x, 1 - y)]

        def start_loads(j):
            s = j % 2
            for k in RSA_PIECES:
                pltpu.make_async_copy(_block_rows(srcs, k, (*chips[j], c)), _rsa_rows(own_buf.at[s], k),
                                      in_sems.at[2 * s]).start()
            pltpu.make_async_copy(rx1_ref.at[j], rx_buf.at[s], in_sems.at[2 * s + 1]).start()

        def wait_loads(j):
            s = j % 2
            pltpu.make_async_copy(rx1_ref.at[j], own_buf.at[s], in_sems.at[2 * s]).wait()
            pltpu.make_async_copy(rx1_ref.at[j], rx_buf.at[s], in_sems.at[2 * s + 1]).wait()

        def store(j):
            if j == 0:
                return pltpu.make_async_copy(acc_buf, acc_ref, out_sems.at[2])
            return pltpu.make_async_copy(tx_buf.at[j % 2], tx_ref.at[j - 1], out_sems.at[j % 2])

        start_loads(0)
        for j in range(4):
            s = j % 2
            if j + 1 < 4:
                start_loads(j + 1)
            wait_loads(j)
            if j == 3:
                store(1).wait()

            def add(i, carry, j=j, s=s):
                rows = pl.ds(pl.multiple_of(i * RSA_BLOCK, 16), RSA_BLOCK)
                tot = own_buf[s, rows, :].astype(f32) + rx_buf[s, rows, :].astype(f32)
                if j == 0:
                    acc_buf[rows, :] = tot
                else:
                    tx_buf[s, rows, :] = tot.astype(bf16)
                return carry

            lax.fori_loop(0, nblk, add, 0)
            store(j).start()
        store(0).wait()
        store(2).wait()
        store(3).wait()

    return pl.pallas_call(
        body, in_specs=[ANY_SPEC] * 4, out_specs=[ANY_SPEC] * 2,
        out_shape=(SDS((3, RSA_ROWS, D), bf16), SDS((RSA_ROWS, D), f32)),
        scratch_shapes=[pltpu.VMEM((2, RSA_ROWS, D), bf16), pltpu.VMEM((2, RSA_ROWS, D), bf16),
                        pltpu.VMEM((2, RSA_ROWS, D), bf16), pltpu.VMEM((RSA_ROWS, D), f32),
                        pltpu.SemaphoreType.DMA((4,)), pltpu.SemaphoreType.DMA((3,))],
        compiler_params=_cparams(None, VMEM_LIMIT_V7X), name="rsa_chip_sums")(dwint, dwout, dw2, rx1)


def _rsa_level2_start(tx, rx2, thru):
    def body(tx_ref, rx2_ref, thru_ref, sb, rb, tx_o, rx2_o, thru_o):
        x, y, c = _position()
        for j, chip in enumerate([(1 - x, y), (x, 1 - y), (1 - x, 1 - y)]):
            pltpu.make_async_remote_copy(src_ref=tx_ref.at[j], dst_ref=rx2_ref.at[j], send_sem=sb.at[j],
                                         recv_sem=rb.at[j], device_id=(*chip, c), device_id_type=MESH).start()

    dma = pltpu.SemaphoreType.DMA
    arrs = (tx, rx2, thru)
    return pl.pallas_call(
        body, name="rsa_level2_start", out_shape=(dma((3,)), dma((3,))) + tuple(_hbm_like(a) for a in arrs),
        in_specs=(HBM_SPEC,) * 3, out_specs=(SEM_SPEC,) * 2 + (HBM_SPEC,) * 3,
        input_output_aliases={0: 2, 1: 3, 2: 4},
        compiler_params=pltpu.CompilerParams(has_side_effects=SPLIT_EFFECT),
    )(*[_in_hbm(a) for a in arrs])


def _rsa_level2_wait(sb, rb, tx, rx2, after):
    def body(tx_ref, rx2_ref, sb_ref, rb_ref, after_ref, rx2_o):
        x, y, c = _position()
        for j in range(3):
            d = pltpu.make_async_remote_copy(src_ref=tx_ref.at[j], dst_ref=rx2_ref.at[j], send_sem=sb_ref.at[j],
                                             recv_sem=rb_ref.at[j], device_id=(x, y, c), device_id_type=MESH)
            d.wait_recv()
            d.wait_send()

    return pl.pallas_call(
        body, name="rsa_level2_wait", out_shape=_hbm_like(rx2),
        in_specs=(HBM_SPEC, HBM_SPEC, SEM_SPEC, SEM_SPEC, ANY_SPEC), out_specs=HBM_SPEC,
        input_output_aliases={1: 0},
        compiler_params=pltpu.CompilerParams(has_side_effects=SPLIT_EFFECT),
    )(tx, rx2, sb, rb, after)


def _adamw_math(w, g, m, v):
    m = ADAM_B1 * m + (1.0 - ADAM_B1) * g
    v = ADAM_B2 * v + (1.0 - ADAM_B2) * (g * g)
    m_hat = m / (1.0 - ADAM_B1 ** ADAM_STEP)
    v_hat = v / (1.0 - ADAM_B2 ** ADAM_STEP)
    delta = -ADAM_LR * (m_hat / (jnp.sqrt(v_hat) + ADAM_EPS) + ADAM_WD * w)
    return delta, m, v


def _adamw_big(pieces, ws, ms, vs, own, landed, name):
    npiece = len(pieces)
    nland = sum(a.shape[0] if a.ndim == 3 else 1 for a in landed)
    rmax = max(PIECE_ROWS[k] for k in pieces)
    half = FS // 2

    def segments(k):
        if k in G1_PIECES:
            return [(hf * len(G1_PIECES) * half + k * half, hf * half, half) for hf in (0, 1)]
        return [(RSA_OFF[k], 0, PIECE_ROWS[k])]

    def body(*refs):
        ins = (refs[0:npiece], refs[npiece:2 * npiece], refs[2 * npiece:3 * npiece])
        own_ref = refs[3 * npiece]
        nin = 3 * npiece + 1 + len(landed)
        land_refs = []
        for ref, a in zip(refs[3 * npiece + 1:nin], landed):
            land_refs += [ref.at[j] for j in range(a.shape[0])] if a.ndim == 3 else [ref]
        out_refs = refs[nin:nin + 4 * npiece]
        inb, landb, outb, in_sems, land_sems, out_sems = refs[nin + 4 * npiece:]

        def loads(i):
            s, k = i % 2, pieces[i]
            r = PIECE_ROWS[k]
            cps = [pltpu.make_async_copy(ins[q][i].at[0], inb.at[s, q, pl.ds(0, r), :], in_sems.at[4 * s + q])
                   for q in range(3)]
            waits = list(cps)
            for src0, dst0, n in segments(k):
                cps.append(pltpu.make_async_copy(own_ref.at[pl.ds(src0, n), :], inb.at[s, 3, pl.ds(dst0, n), :],
                                                 in_sems.at[4 * s + 3]))
                for p in range(nland):
                    cps.append(pltpu.make_async_copy(land_refs[p].at[pl.ds(src0, n), :],
                                                     landb.at[s, p, pl.ds(dst0, n), :], land_sems.at[nland * s + p]))
            own_rows = inb.at[s, 3, pl.ds(0, r), :]
            waits.append(pltpu.make_async_copy(own_rows, own_rows, in_sems.at[4 * s + 3]))
            for p in range(nland):
                rows = landb.at[s, p, pl.ds(0, r), :]
                waits.append(pltpu.make_async_copy(rows, rows, land_sems.at[nland * s + p]))
            return cps, waits

        def stores(i):
            s, r = i % 2, PIECE_ROWS[pieces[i]]
            return [pltpu.make_async_copy(outb.at[s, q, pl.ds(0, r), :], out_refs[q * npiece + i].at[0],
                                          out_sems.at[4 * s + q]) for q in range(4)]

        for cp in loads(0)[0]:
            cp.start()
        for i in range(npiece):
            s, r = i % 2, PIECE_ROWS[pieces[i]]
            if i + 1 < npiece:
                for cp in loads(i + 1)[0]:
                    cp.start()
            for cp in loads(i)[1]:
                cp.wait()
            if i >= 2:
                for cp in stores(i - 2):
                    cp.wait()
            g = inb[s, 3, 0:r, :]
            for p in range(nland):
                g = g + landb[s, p, 0:r, :].astype(f32)
            d, nm, nv = _adamw_math(inb[s, 0, 0:r, :], g, inb[s, 1, 0:r, :], inb[s, 2, 0:r, :])
            outb[s, 0, 0:r, :] = g
            outb[s, 1, 0:r, :] = d
            outb[s, 2, 0:r, :] = nm
            outb[s, 3, 0:r, :] = nv
            for cp in stores(i):
                cp.start()
        for i in range(max(npiece - 2, 0), npiece):
            for cp in stores(i):
                cp.wait()

    hbm = pl.BlockSpec(memory_space=pl.ANY)
    outs = pl.pallas_call(
        body, in_specs=[hbm] * (3 * npiece + 1 + len(landed)), out_specs=[hbm] * (4 * npiece),
        out_shape=tuple(SDS(w.shape, f32) for _ in range(4) for w in ws),
        scratch_shapes=[pltpu.VMEM((2, 4, rmax, D), f32), pltpu.VMEM((2, nland, rmax, D), bf16),
                        pltpu.VMEM((2, 4, rmax, D), f32),
                        pltpu.SemaphoreType.DMA((8,)), pltpu.SemaphoreType.DMA((2 * nland,)),
                        pltpu.SemaphoreType.DMA((8,))],
        compiler_params=_cparams(None, VMEM_LIMIT_V7X), name=name)(*ws, *ms, *vs, own, *landed)
    return [list(outs[q * npiece:(q + 1) * npiece]) for q in range(4)]


def _adamw_small(ws, ms, vs, gs, name):
    n = len(ws)

    def body(*refs):
        w_refs, m_refs, v_refs, g_refs = refs[0:n], refs[n:2 * n], refs[2 * n:3 * n], refs[3 * n:4 * n]
        outs = refs[4 * n:]
        for i in range(n):
            d, nm, nv = _adamw_math(w_refs[i][...], g_refs[i][...], m_refs[i][...], v_refs[i][...])
            outs[i][...] = d
            outs[n + i][...] = nm
            outs[2 * n + i][...] = nv

    outs = pl.pallas_call(
        body, out_shape=tuple(SDS(w.shape, f32) for _ in range(3) for w in ws), name=name)(*ws, *ms, *vs, *gs)
    return [list(outs[q * n:(q + 1) * n]) for q in range(3)]


WEIGHTS = ("ffn1_norm", "ffn1_w_gate", "ffn1_w_up", "ffn1_w_down", "mix_norm", "w_in", "q_norm", "k_norm",
           "attn_sinks", "rel_bias", "pool_w", "pool_scale", "w_out", "ffn2_norm", "ffn2_w_gate", "ffn2_w_up",
           "ffn2_w_down")
BIG = (("ffn1_w_gate", True), ("ffn1_w_up", True), ("ffn1_w_down", False), ("w_in", True), ("w_out", False),
       ("ffn2_w_gate", True), ("ffn2_w_up", True), ("ffn2_w_down", False))


def kernel(x, ffn1_norm, ffn1_w_gate, ffn1_w_up, ffn1_w_down, mix_norm, w_in, q_norm, k_norm, attn_sinks, rel_bias, pool_w, pool_scale, w_out, ffn2_norm, ffn2_w_gate, ffn2_w_up, ffn2_w_down, loss_target, m_ffn1_norm, m_ffn1_w_gate, m_ffn1_w_up, m_ffn1_w_down, m_mix_norm, m_w_in, m_q_norm, m_k_norm, m_attn_sinks, m_rel_bias, m_pool_w, m_pool_scale, m_w_out, m_ffn2_norm, m_ffn2_w_gate, m_ffn2_w_up, m_ffn2_w_down, v_ffn1_norm, v_ffn1_w_gate, v_ffn1_w_up, v_ffn1_w_down, v_mix_norm, v_w_in, v_q_norm, v_k_norm, v_attn_sinks, v_rel_bias, v_pool_w, v_pool_scale, v_w_out, v_ffn2_norm, v_ffn2_w_gate, v_ffn2_w_up, v_ffn2_w_down):
    args = dict(locals())
    w = {n: args[n] for n in WEIGHTS}
    m = {n: args["m_" + n] for n in WEIGHTS}
    v = {n: args["v_" + n] for n in WEIGHTS}

    as_rows = lambda a, tr: jnp.swapaxes(a, 1, 2) if tr else a
    shard = jnp.concatenate([as_rows(w[n], tr)[0].astype(bf16) for n, tr in BIG], axis=0)
    exchanges = _GatheredWeights(shard, x[0], ffn1_norm)
    gx, (dw1, _, _, _), small = _local_step(
        x[0], loss_target[0], exchanges, ffn1_norm, mix_norm, ffn2_norm, q_norm, k_norm, attn_sinks,
        rel_bias, pool_w[0], pool_scale)

    for_x, for_y, from_x, from_y, own1, small_tot = _reduce_scatter_ffn1_head(dw1, _pack_small(small))
    ssem, rsem, for_x, for_y, from_x, from_y, small_tot = _rs1_tail_start(for_x, for_y, from_x, from_y, small_tot)
    own_rest, landed_rest = exchanges.mix_ffn2_grads_parts(small_tot)

    grads, deltas, new_m, new_v = {}, {}, {}, {}
    rest = [k for k in range(len(BIG)) if k not in G1_PIECES]
    rows_of = lambda t, ks: [as_rows(t[BIG[k][0]], BIG[k][1]) for k in ks]
    rest_out = _adamw_big(rest, rows_of(w, rest), rows_of(m, rest), rows_of(v, rest), own_rest, [landed_rest],
                          "adamw_rest")
    from_x, from_y = _rs1_tail_wait(ssem, rsem, for_x, for_y, from_x, from_y, rest_out[0][0])
    ffn1 = list(G1_PIECES)
    ffn1_out = _adamw_big(ffn1, rows_of(w, ffn1), rows_of(m, ffn1), rows_of(v, ffn1), own1, [from_x, from_y],
                          "adamw_ffn1")
    for ks, out in ((rest, rest_out), (ffn1, ffn1_out)):
        for i, k in enumerate(ks):
            n, tr = BIG[k]
            grads[n], deltas[n], new_m[n], new_v[n] = [as_rows(o[i], tr) for o in out]
    small_names = [n for n in SMALL_NAMES if n != "loss"]
    for n in small_names:
        grads[n] = _unpack_small(small_tot, n)
    ds, nms, nvs = _adamw_small([w[n] for n in small_names], [m[n] for n in small_names], [v[n] for n in small_names],
                                [grads[n] for n in small_names], "adamw_small")
    for i, n in enumerate(small_names):
        deltas[n], new_m[n], new_v[n] = ds[i], nms[i], nvs[i]
    loss = small_tot[LOSS_ROW, 0]
    return (loss, gx[None], *[grads[n] for n in WEIGHTS], *[deltas[n] for n in WEIGHTS],
            *[new_m[n] for n in WEIGHTS], *[new_v[n] for n in WEIGHTS])
```

```python
import jax
import jax.numpy as jnp
import numpy as np
from jax import lax
from jax.experimental import pallas as pl
from jax.experimental.pallas import tpu as pltpu

f32, bf16, i32 = jnp.float32, jnp.bfloat16, jnp.int32
SDS = jax.ShapeDtypeStruct

D = 1024
F = 2816
HD = 64
NH = 8
NKV = 2
GQA = NH // NKV
DATTN = NH * HD
DKV = NKV * HD
DPOOL = 512
POOL_WINDOWS = (2, 4, 8, 16)
PGD = DPOOL // len(POOL_WINDOWS)
DIN = DATTN + 2 * DKV + DPOOL
DMIX = DATTN + DPOOL
BLK = 128
NBUCK = 32
MAX_DISTANCE = 128
EPS = 1e-6
NEG = -1e30
SCALE = HD ** -0.5

ADAM_LR, ADAM_B1, ADAM_B2, ADAM_EPS, ADAM_WD, ADAM_STEP = 0.001, 0.9, 0.999, 1e-08, 0.01, 10

NDEV = 8
FS = F // NDEV
INS = DIN // NDEV
OUTS = DMIX // NDEV
PIECE_ROWS = (FS, FS, FS, INS, OUTS, FS, FS, FS)
PIECE_OFF = tuple(int(v) for v in np.cumsum((0,) + PIECE_ROWS[:-1]))
PACK_ROWS = sum(PIECE_ROWS)

VMEM_LIMIT_V7X = 56 * 1024 * 1024

MESH = pl.DeviceIdType.MESH


def _cparams(sem=None, vmem=None):
    return pltpu.CompilerParams(dimension_semantics=sem, vmem_limit_bytes=vmem)


def _nt(a, b):
    return lax.dot_general(a, b, (((1,), (1,)), ((), ())), preferred_element_type=f32)


def _tn(a, b):
    return lax.dot_general(a, b, (((0,), (0,)), ((), ())), preferred_element_type=f32)


def _nn(a, b):
    return jnp.dot(a, b, preferred_element_type=f32)


def _sigmoid(x):
    return 1.0 / (1.0 + jnp.exp(-x))


def _norm_fwd(x, g, name):
    T = x.shape[0]
    tm = min(512, T)

    def body(x_ref, g_ref, h_ref):
        xv = x_ref[...]
        r = lax.rsqrt(jnp.mean(xv * xv, axis=-1, keepdims=True) + EPS)
        h_ref[...] = (xv * r * g_ref[...]).astype(bf16)

    return pl.pallas_call(
        body, grid=(T // tm,),
        in_specs=[pl.BlockSpec((tm, D), lambda i: (i, 0)), pl.BlockSpec((1, D), lambda i: (0, 0))],
        out_specs=pl.BlockSpec((tm, D), lambda i: (i, 0)),
        out_shape=SDS((T, D), bf16), name=name)(x, g)


FFN_ROW_CHUNK = 256


def _ffn_tiles(T):
    return min(1024, T), 256


def _ffn_fwd(h, w, x, target, next_gain, name):
    T = h.shape[0]
    tm, tf = _ffn_tiles(T)
    nf = F // tf
    with_loss = target is not None
    assert with_loss != (next_gain is not None)

    def body(*refs):
        if with_loss:
            h_ref, w_ref, x_hbm, t_hbm, xo_ref, g_ref, u_ref, dyb_ref, loss_ref, tbuf, sem = refs
        else:
            h_ref, w_ref, x_hbm, gain_ref, xo_ref, g_ref, u_ref, hn_ref, sem = refs
        fi = pl.program_id(0)

        @pl.when(fi == 0)
        def _():
            cp = pltpu.make_async_copy(x_hbm, xo_ref, sem)
            cp.start()
            cp.wait()

        wgu = w_ref[0:2].reshape(2 * tf, D)
        for r in range(0, T, tm):
            rows = slice(r, r + tm)
            gu = _nt(h_ref[rows, :], wgu)
            gate, up = gu[:, :tf], gu[:, tf:]
            act = gate * _sigmoid(gate) * up
            g_ref[0, rows, :] = gate.astype(bf16)
            u_ref[0, rows, :] = up.astype(bf16)
            xo_ref[rows, :] += _nn((0.5 * act).astype(bf16), w_ref[2])

        if with_loss:
            @pl.when(fi == nf - 1)
            def _():
                lanes = jnp.zeros((1, 128), f32)
                for r in range(0, T, tm):
                    rows = slice(r, r + tm)
                    cp = pltpu.make_async_copy(t_hbm.at[pl.ds(r, tm), :], tbuf, sem)
                    cp.start()
                    cp.wait()
                    e = xo_ref[rows, :] - tbuf[...]
                    dy = e * (1.0 / D)
                    xo_ref[rows, :] = dy
                    dyb_ref[rows, :] = (0.5 * dy).astype(bf16)
                    col = jnp.sum(e * e, axis=0, keepdims=True) * (0.5 / D)
                    for k in range(D // 128):
                        lanes = lanes + col[:, 128 * k:128 * (k + 1)]
                loss_ref[...] = lanes
        else:
            @pl.when(fi == nf - 1)
            def _():
                for r in range(0, T, FFN_ROW_CHUNK):
                    rows = slice(r, r + FFN_ROW_CHUNK)
                    xv = xo_ref[rows, :]
                    rstd = lax.rsqrt(jnp.mean(xv * xv, axis=-1, keepdims=True) + EPS)
                    hn_ref[rows, :] = (xv * rstd * gain_ref[...]).astype(bf16)

    tok = pl.BlockSpec((T, D), lambda f: (0, 0))
    act_spec = pl.BlockSpec((1, T, tf), lambda f: (f, 0, 0))
    hbm = pl.BlockSpec(memory_space=pl.ANY)
    in_specs = [tok, pl.BlockSpec((3, tf, D), lambda f: (0, f, 0)), hbm]
    out_specs = [tok, act_spec, act_spec]
    out_shape = [SDS((T, D), f32), SDS((nf, T, tf), bf16), SDS((nf, T, tf), bf16)]
    scratch = [pltpu.SemaphoreType.DMA]
    args = [h, w, x]
    if with_loss:
        in_specs.append(hbm)
        args.append(target)
        out_specs += [tok, pl.BlockSpec((1, 128), lambda f: (0, 0))]
        out_shape += [SDS((T, D), bf16), SDS((1, 128), f32)]
        scratch = [pltpu.VMEM((tm, D), f32)] + scratch
    else:
        in_specs.append(pl.BlockSpec((1, D), lambda f: (0, 0)))
        args.append(next_gain)
        out_specs.append(tok)
        out_shape.append(SDS((T, D), bf16))
    return pl.pallas_call(
        body, grid=(nf,), in_specs=in_specs, out_specs=out_specs, out_shape=tuple(out_shape), scratch_shapes=scratch,
        compiler_params=_cparams(("arbitrary",), VMEM_LIMIT_V7X), name=name)(*args)


def _ffn_bwd(dob, h, gate, up, w, norm, name):
    x, g, dres = norm
    T = h.shape[0]
    _, tf = _ffn_tiles(T)
    nf = F // tf
    tm = min(512, T)
    nchunk = T // tm

    def body(do_hbm, h_hbm, g_ref, u_ref, w_ref, x_hbm, gain_ref, dr_hbm, dx_hbm, dxb_hbm, dg_ref, dw_ref,
             do_v, h_v, dh_acc, dgu_s, act_s, xbuf, rbuf, obuf, obb, sems, in_sems, out_sems):
        fi = pl.program_id(0)

        @pl.when(fi == 0)
        def _():
            loads = [pltpu.make_async_copy(do_hbm, do_v, sems.at[0]), pltpu.make_async_copy(h_hbm, h_v, sems.at[1])]
            for cp in loads:
                cp.start()
            dh_acc[...] = jnp.zeros_like(dh_acc)
            for cp in loads:
                cp.wait()

        wgu = w_ref[0:2].reshape(2 * tf, D)
        for r in range(0, T, FFN_ROW_CHUNK):
            rows = slice(r, r + FFN_ROW_CHUNK)
            dov = do_v[rows, :]
            gv = g_ref[0, rows, :].astype(f32)
            uv = u_ref[0, rows, :].astype(f32)
            sg = _sigmoid(gv)
            sil = gv * sg
            dact = _nt(dov, w_ref[2])
            dup = dact * sil
            dgate = dact * uv * (sg * (1.0 + gv * (1.0 - sg)))
            dgu = jnp.concatenate([dgate.astype(bf16), dup.astype(bf16)], axis=1)
            dgu_s[rows, :] = dgu
            act_s[rows, :] = (sil * uv).astype(bf16)
            dh_acc[rows, :] += _nn(dgu, wgu)
        dw_ref[0:2] = _tn(dgu_s[...], h_v[...]).reshape(2, tf, D).astype(bf16)
        dw_ref[2] = _tn(act_s[...], do_v[...]).astype(bf16)

        @pl.when(fi == nf - 1)
        def _():
            def loads(i):
                s, rows = i % 2, pl.ds(i * tm, tm)
                return [pltpu.make_async_copy(x_hbm.at[rows, :], xbuf.at[s], in_sems.at[2 * s]),
                        pltpu.make_async_copy(dr_hbm.at[rows, :], rbuf.at[s], in_sems.at[2 * s + 1])]

            def stores(i):
                s, rows = i % 2, pl.ds(i * tm, tm)
                return [pltpu.make_async_copy(obuf.at[s], dx_hbm.at[rows, :], out_sems.at[2 * s]),
                        pltpu.make_async_copy(obb.at[s], dxb_hbm.at[rows, :], out_sems.at[2 * s + 1])]

            for cp in loads(0):
                cp.start()
            dg = jnp.zeros((1, D), f32)
            for i in range(nchunk):
                s = i % 2
                if i + 1 < nchunk:
                    for cp in loads(i + 1):
                        cp.start()
                for cp in loads(i):
                    cp.wait()
                if i >= 2:
                    for cp in stores(i - 2):
                        cp.wait()
                xv = xbuf[s]
                rstd = lax.rsqrt(jnp.mean(xv * xv, axis=-1, keepdims=True) + EPS)
                xh = xv * rstd
                dhv = dh_acc[i * tm:(i + 1) * tm, :]
                dxh = dhv * gain_ref[...]
                dx = rbuf[s] + rstd * (dxh - xh * jnp.mean(dxh * xh, axis=-1, keepdims=True))
                obuf[s] = dx
                obb[s] = dx.astype(bf16)
                dg = dg + jnp.sum(dhv * xh, axis=0, keepdims=True)
                for cp in stores(i):
                    cp.start()
            dg_ref[...] = dg
            for i in range(max(nchunk - 2, 0), nchunk):
                for cp in stores(i):
                    cp.wait()

    act_spec = pl.BlockSpec((1, T, tf), lambda f: (f, 0, 0))
    wspec = pl.BlockSpec((3, tf, D), lambda f: (0, f, 0))
    vec = pl.BlockSpec((1, D), lambda f: (0, 0))
    hbm = pl.BlockSpec(memory_space=pl.ANY)
    return pl.pallas_call(
        body, grid=(nf,),
        in_specs=[hbm, hbm, act_spec, act_spec, wspec, hbm, vec, hbm],
        out_specs=[hbm, hbm, vec, wspec],
        out_shape=(SDS((T, D), f32), SDS((T, D), bf16), SDS((1, D), f32), SDS((3, F, D), bf16)),
        scratch_shapes=[pltpu.VMEM((T, D), bf16), pltpu.VMEM((T, D), bf16), pltpu.VMEM((T, D), f32),
                        pltpu.VMEM((T, 2 * tf), bf16), pltpu.VMEM((T, tf), bf16),
                        pltpu.VMEM((2, tm, D), f32), pltpu.VMEM((2, tm, D), f32), pltpu.VMEM((2, tm, D), f32),
                        pltpu.VMEM((2, tm, D), bf16),
                        pltpu.SemaphoreType.DMA((2,)), pltpu.SemaphoreType.DMA((4,)), pltpu.SemaphoreType.DMA((4,))],
        compiler_params=_cparams(("arbitrary",), VMEM_LIMIT_V7X), name=name)(dob, h, gate, up, w, x, g, dres)


def _in_proj_fwd(h, wint, name):
    T = h.shape[0]
    tm = min(512, T)

    def body(h_ref, w_ref, z_ref):
        z_ref[...] = _nt(h_ref[...], w_ref[...])

    return pl.pallas_call(
        body, grid=(T // tm,),
        in_specs=[pl.BlockSpec((tm, D), lambda i: (i, 0)), pl.BlockSpec((DIN, D), lambda i: (0, 0))],
        out_specs=pl.BlockSpec((tm, DIN), lambda i: (i, 0)),
        out_shape=SDS((T, DIN), f32), name=name)(h, wint)


def _in_proj_bwd(dz, wint, h, norm, out_scale, name):
    x, g, dres = norm
    T = h.shape[0]
    tm = min(512, T)
    nt = T // tm

    def body(dz_ref, w_ref, h_ref, x_ref, g_ref, dr_ref, dx_ref, dxb_ref, dg_ref, dw_ref, acc):
        i = pl.program_id(0)
        dzb = dz_ref[...].astype(bf16)
        dhv = _nn(dzb, w_ref[...])
        part = _tn(dzb, h_ref[...])
        xv = x_ref[...]
        rstd = lax.rsqrt(jnp.mean(xv * xv, axis=-1, keepdims=True) + EPS)
        xh = xv * rstd
        dxh = dhv * g_ref[...]
        dx = dr_ref[...] + rstd * (dxh - xh * jnp.mean(dxh * xh, axis=-1, keepdims=True))
        dx_ref[...] = dx
        dxb_ref[...] = (out_scale * dx).astype(bf16)
        dg = jnp.sum(dhv * xh, axis=0, keepdims=True)

        @pl.when(i == 0)
        def _():
            acc[...] = part
            dg_ref[...] = dg

        @pl.when(i > 0)
        def _():
            acc[...] += part
            dg_ref[...] += dg

        @pl.when(i == nt - 1)
        def _():
            dw_ref[...] = acc[...].astype(bf16)

    wspec = pl.BlockSpec((DIN, D), lambda i: (0, 0))
    tok = pl.BlockSpec((tm, D), lambda i: (i, 0))
    vec = pl.BlockSpec((1, D), lambda i: (0, 0))
    return pl.pallas_call(
        body, grid=(nt,),
        in_specs=[pl.BlockSpec((tm, DIN), lambda i: (i, 0)), wspec, tok, tok, vec, tok],
        out_specs=[tok, tok, vec, wspec],
        out_shape=(SDS((T, D), f32), SDS((T, D), bf16), SDS((1, D), f32), SDS((DIN, D), bf16)),
        scratch_shapes=[pltpu.VMEM((DIN, D), f32)],
        compiler_params=_cparams(("arbitrary",)), name=name)(dz, wint, h, x, g, dres)


def _out_proj_fwd(ymix, wout, x, g, name):
    T = x.shape[0]
    tm = min(512, T)

    def body(y_ref, w_ref, x_ref, g_ref, o_ref, h_ref):
        o = x_ref[...] + _nn(y_ref[...], w_ref[...])
        o_ref[...] = o
        r = lax.rsqrt(jnp.mean(o * o, axis=-1, keepdims=True) + EPS)
        h_ref[...] = (o * r * g_ref[...]).astype(bf16)

    tok = pl.BlockSpec((tm, D), lambda i: (i, 0))
    return pl.pallas_call(
        body, grid=(T // tm,),
        in_specs=[pl.BlockSpec((tm, DMIX), lambda i: (i, 0)), pl.BlockSpec((DMIX, D), lambda i: (0, 0)), tok,
                  pl.BlockSpec((1, D), lambda i: (0, 0))],
        out_specs=[tok, tok], out_shape=(SDS((T, D), f32), SDS((T, D), bf16)), name=name)(ymix, wout, x, g)


def _out_proj_bwd(dxb, wout, ymix, name):
    T = dxb.shape[0]
    tm = min(512, T)
    nt = T // tm

    def body(dx_ref, w_ref, y_ref, dy_ref, dw_ref, acc):
        i = pl.program_id(0)
        dxv = dx_ref[...]
        dy_ref[...] = _nt(dxv, w_ref[...])
        part = _tn(y_ref[...], dxv)

        @pl.when(i == 0)
        def _():
            acc[...] = part

        @pl.when(i > 0)
        def _():
            acc[...] += part

        @pl.when(i == nt - 1)
        def _():
            dw_ref[...] = acc[...].astype(bf16)

    wspec = pl.BlockSpec((DMIX, D), lambda i: (0, 0))
    return pl.pallas_call(
        body, grid=(nt,),
        in_specs=[pl.BlockSpec((tm, D), lambda i: (i, 0)), wspec, pl.BlockSpec((tm, DMIX), lambda i: (i, 0))],
        out_specs=[pl.BlockSpec((tm, DMIX), lambda i: (i, 0)), wspec],
        out_shape=(SDS((T, DMIX), f32), SDS((DMIX, D), bf16)),
        scratch_shapes=[pltpu.VMEM((DMIX, D), f32)],
        compiler_params=_cparams(("arbitrary",)), name=name)(dxb, wout, ymix)


def _t5_bucket_table():
    ql = np.arange(BLK)[:, None]
    kl = np.arange(2 * BLK)[None, :]
    n = np.maximum(ql + BLK - kl, 0)
    max_exact = NBUCK // 2
    large = max_exact + (np.log(np.maximum(n, 1) / max_exact) / np.log(MAX_DISTANCE / max_exact)
                         * (NBUCK - max_exact)).astype(np.int32)
    large = np.minimum(large, NBUCK - 1)
    return np.where(n < max_exact, n, large).astype(np.int32)


def _fill_bias(bk_ref, rb_ref, bias_scr):
    bk = bk_ref[...]
    for h in range(NH):
        def step(b, acc, h=h):
            return acc + jnp.where(bk == b, rb_ref[b, h], 0.0)
        bias_scr[h] = lax.fori_loop(0, NBUCK, step, jnp.zeros((BLK, 2 * BLK), f32))


MIX_SUB = 8


class _Window:
    def __init__(self, zc_ref, zp_ref, n, s):
        self.blk = n * MIX_SUB + s
        self.cur = lambda a, b: zc_ref[s * BLK:(s + 1) * BLK, a:b]
        self.prev = (lambda a, b: zp_ref[:, a:b]) if s == 0 else (lambda a, b: zc_ref[(s - 1) * BLK:s * BLK, a:b])


def _attn_qkv(win, kh, qg, kg):
    kc = DATTN + HD * kh
    vc = DATTN + DKV + HD * kh
    kx = jnp.concatenate([win.prev(kc, kc + HD), win.cur(kc, kc + HD)], axis=0)
    vx = jnp.concatenate([win.prev(vc, vc + HD), win.cur(vc, vc + HD)], axis=0)
    qx = jnp.concatenate([win.cur(HD * (GQA * kh + g), HD * (GQA * kh + g + 1)) for g in range(GQA)], axis=0)
    rq = lax.rsqrt(jnp.mean(qx * qx, axis=-1, keepdims=True) + EPS)
    rk = lax.rsqrt(jnp.mean(kx * kx, axis=-1, keepdims=True) + EPS)
    qhat, khat = qx * rq, kx * rk
    return dict(qhat=qhat, khat=khat, rq=rq, rk=rk, qsb=(qhat * (qg * SCALE)).astype(bf16),
                knb=(khat * kg).astype(bf16), vb=vx.astype(bf16))


def _window_masks(n):
    row = lax.broadcasted_iota(i32, (GQA * BLK, 2 * BLK), 0) & (BLK - 1)
    col = lax.broadcasted_iota(i32, (GQA * BLK, 2 * BLK), 1)
    band = (col > row) & (col <= row + BLK)
    return band & ((col >= BLK) | (n > 0)), band


def _attn_probs(a, kh, sk_ref, bias_scr, mask):
    s = _nt(a["qsb"], a["knb"]) + bias_scr[GQA * kh:GQA * (kh + 1)].reshape(GQA * BLK, 2 * BLK)
    s = jnp.where(mask, s, NEG)
    ridx = lax.broadcasted_iota(i32, (GQA * BLK, 1), 0)
    sink = jnp.full((GQA * BLK, 1), sk_ref[GQA * kh + GQA - 1], f32)
    for g in range(GQA - 2, -1, -1):
        sink = jnp.where(ridx < (g + 1) * BLK, sk_ref[GQA * kh + g], sink)
    m = jnp.maximum(jnp.max(s, axis=-1, keepdims=True), sink)
    e = jnp.exp(s - m)
    den = jnp.sum(e, axis=-1, keepdims=True) + jnp.exp(sink - m)
    return e / den


POOL_STEPS = {2: (1,), 4: (1, 2), 8: (1, 2, 4), 16: (1, 2, 4, 8)}


def _pool_group(win, g, w):
    n = win.blk
    c0 = DATTN + 2 * DKV + PGD * g
    uc = win.cur(c0, c0 + PGD)
    up = jnp.where(n > 0, win.prev(c0, c0 + PGD), 0.0)
    sm = jnp.concatenate([up, uc], axis=0)
    for k in POOL_STEPS[w]:
        sm = sm + pltpu.roll(sm, k, axis=0)
    pos = n * BLK + lax.broadcasted_iota(i32, (BLK, 1), 0) + 1
    cnt = jnp.minimum(pos, w).astype(f32)
    return sm[BLK:2 * BLK] / cnt - uc, cnt


def _mix_fwd(z, qg, kg, sinks, relb, bucket, pool_w, pscale, name):
    T = z.shape[0]
    step_rows = MIX_SUB * BLK
    nsteps = T // step_rows

    def body(zc_ref, zp_ref, qg_ref, kg_ref, sk_ref, rb_ref, bk_ref, pw_ref, ps_ref, y_ref, p_ref, bias_scr, yacc):
        n = pl.program_id(0)

        @pl.when(n == 0)
        def _():
            _fill_bias(bk_ref, rb_ref, bias_scr)

        first_mask, mask = _window_masks(n)
        for s in range(MIX_SUB):
            win = _Window(zc_ref, zp_ref, n, s)
            rows = slice(s * BLK, (s + 1) * BLK)
            for kh in range(NKV):
                a = _attn_qkv(win, kh, qg_ref[...], kg_ref[...])
                pb = _attn_probs(a, kh, sk_ref, bias_scr, first_mask if s == 0 else mask).astype(bf16)
                p_ref[s, GQA * kh:GQA * (kh + 1)] = pb.reshape(GQA, BLK, 2 * BLK)
                o = _nn(pb, a["vb"])
                for g in range(GQA):
                    hc = HD * (GQA * kh + g)
                    yacc[rows, hc:hc + HD] = o[g * BLK:(g + 1) * BLK]
            for g, w in enumerate(POOL_WINDOWS):
                pooled, _ = _pool_group(win, g, w)
                yp = _nn(pooled.astype(bf16), pw_ref[g].astype(bf16)) * ps_ref[:, PGD * g:PGD * (g + 1)]
                yacc[rows, DATTN + PGD * g:DATTN + PGD * (g + 1)] = yp
        y_ref[...] = yacc[...].astype(bf16)

    full = lambda *shape: pl.BlockSpec(shape, lambda n: (0,) * len(shape))
    smem = pl.BlockSpec(memory_space=pltpu.SMEM)
    return pl.pallas_call(
        body, grid=(nsteps,),
        in_specs=[pl.BlockSpec((step_rows, DIN), lambda n: (n, 0)),
                  pl.BlockSpec((BLK, DIN), lambda n: (jnp.maximum(n * MIX_SUB - 1, 0), 0)),
                  full(1, HD), full(1, HD), smem, smem, full(BLK, 2 * BLK),
                  full(len(POOL_WINDOWS), PGD, PGD), full(1, DPOOL)],
        out_specs=[pl.BlockSpec((step_rows, DMIX), lambda n: (n, 0)),
                   pl.BlockSpec((MIX_SUB, NH, BLK, 2 * BLK), lambda n: (n, 0, 0, 0))],
        out_shape=(SDS((T, DMIX), bf16), SDS((T // BLK, NH, BLK, 2 * BLK), bf16)),
        scratch_shapes=[pltpu.VMEM((NH, BLK, 2 * BLK), f32), pltpu.VMEM((step_rows, DMIX), f32)],
        compiler_params=_cparams(("arbitrary",)), name=name)(z, z, qg, kg, sinks, relb, bucket, pool_w, pscale)


def _mix_bwd(z, dy, probs, qg, kg, relb, bucket, pool_w, pscale, name):
    T = z.shape[0]
    step_rows = MIX_SUB * BLK
    nsteps = T // step_rows

    def body(zc_ref, zp_ref, dy_ref, p_ref, qg_ref, kg_ref, bk_ref, pw_ref, ps_ref,
             dz_ref, dqg_ref, dkg_ref, dsk_ref, drb_ref, dpw_ref, dps_ref, dbias_scr):
        n = pl.program_id(0)

        @pl.when(n == 0)
        def _():
            dbias_scr[...] = jnp.zeros_like(dbias_scr)
            dqg_ref[...] = jnp.zeros_like(dqg_ref)
            dkg_ref[...] = jnp.zeros_like(dkg_ref)
            dpw_ref[...] = jnp.zeros_like(dpw_ref)
            dps_ref[...] = jnp.zeros_like(dps_ref)

        qg, kg = qg_ref[...], kg_ref[...]
        for s in range(MIX_SUB):
            win = _Window(zc_ref, zp_ref, n, s)
            blk = win.blk
            rows = pl.ds(pl.multiple_of(blk * BLK, BLK), BLK)
            prow = pl.ds(pl.multiple_of(jnp.maximum(blk - 1, 0) * BLK, BLK), BLK)
            dyr = slice(s * BLK, (s + 1) * BLK)

            def into_prev(fn, s=s):
                if s == 0:
                    pl.when(n > 0)(fn)
                else:
                    fn()

            for kh in range(NKV):
                a = _attn_qkv(win, kh, qg, kg)
                pb = p_ref[s, GQA * kh:GQA * (kh + 1)].reshape(GQA * BLK, 2 * BLK)
                p = pb.astype(f32)
                do = jnp.concatenate([dy_ref[dyr, HD * (GQA * kh + g):HD * (GQA * kh + g + 1)] for g in range(GQA)],
                                     axis=0).astype(bf16)
                dv = _tn(pb, do)
                dp = _nt(do, a["vb"])
                delta = jnp.sum(p * dp, axis=-1, keepdims=True)
                ds = p * (dp - delta)
                for g in range(GQA):
                    dbias_scr[GQA * kh + g] += ds[g * BLK:(g + 1) * BLK]
                dsb = ds.astype(bf16)
                dqn = _nn(dsb, a["knb"]) * SCALE
                dkn = _tn(dsb, a["qsb"])
                qhat, khat = a["qhat"], a["khat"]
                dqg_ref[...] += jnp.sum(dqn * qhat, axis=0, keepdims=True)
                dkg_ref[...] += jnp.sum(dkn * khat, axis=0, keepdims=True)
                dqh = dqn * qg
                dq = a["rq"] * (dqh - qhat * jnp.mean(dqh * qhat, axis=-1, keepdims=True))
                dkh = dkn * kg
                dk = a["rk"] * (dkh - khat * jnp.mean(dkh * khat, axis=-1, keepdims=True))
                kc = DATTN + HD * kh
                vc = DATTN + DKV + HD * kh
                for g in range(GQA):
                    hc = HD * (GQA * kh + g)
                    dz_ref[rows, hc:hc + HD] = dq[g * BLK:(g + 1) * BLK]
                dz_ref[rows, kc:kc + HD] = dk[BLK:2 * BLK]
                dz_ref[rows, vc:vc + HD] = dv[BLK:2 * BLK]

                def kv_prev(dk=dk, dv=dv, kc=kc, vc=vc, prow=prow):
                    dz_ref[prow, kc:kc + HD] += dk[0:BLK]
                    dz_ref[prow, vc:vc + HD] += dv[0:BLK]

                into_prev(kv_prev)

            for g, w in enumerate(POOL_WINDOWS):
                c0 = DATTN + 2 * DKV + PGD * g
                pooled, cnt = _pool_group(win, g, w)
                pb = pooled.astype(bf16)
                wb = pw_ref[g].astype(bf16)
                dyp = dy_ref[dyr, DATTN + PGD * g:DATTN + PGD * (g + 1)]
                ypre = _nn(pb, wb)
                dps_ref[:, PGD * g:PGD * (g + 1)] += jnp.sum(dyp * ypre, axis=0, keepdims=True)
                dyg = (dyp * ps_ref[:, PGD * g:PGD * (g + 1)]).astype(bf16)
                dpw_ref[g] += _tn(pb, dyg)
                dpooled = _nt(dyg, wb)
                due = jnp.concatenate([jnp.zeros((BLK, PGD), f32), dpooled / cnt], axis=0)
                for k in POOL_STEPS[w]:
                    due = due + pltpu.roll(due, 2 * BLK - k, axis=0)
                dz_ref[rows, c0:c0 + PGD] = due[BLK:2 * BLK] - dpooled

                def pool_prev(due=due, c0=c0, prow=prow):
                    dz_ref[prow, c0:c0 + PGD] += due[0:BLK]

                into_prev(pool_prev)

        @pl.when(n == nsteps - 1)
        def _():
            bk = bk_ref[...]
            ri = lax.broadcasted_iota(i32, (NBUCK, NH), 0)
            ci = lax.broadcasted_iota(i32, (NBUCK, NH), 1)

            def step(b, acc):
                for h in range(NH):
                    sel = jnp.where(bk == b, dbias_scr[h], 0.0)
                    tot = jnp.sum(jnp.sum(sel, axis=1, keepdims=True), axis=0, keepdims=True)
                    acc = acc + jnp.where((ri == b) & (ci == h), tot, 0.0)
                return acc

            drb_ref[...] = lax.fori_loop(0, NBUCK, step, jnp.zeros((NBUCK, NH), f32))
            lane = lax.broadcasted_iota(i32, (1, 128), 1)
            dsk = jnp.zeros((1, 128), f32)
            for h in range(NH):
                tot = jnp.sum(jnp.sum(dbias_scr[h], axis=1, keepdims=True), axis=0, keepdims=True)
                dsk = dsk - jnp.where(lane == h, tot, 0.0)
            dsk_ref[...] = dsk

    full = lambda *shape: pl.BlockSpec(shape, lambda n: (0,) * len(shape))
    npg = len(POOL_WINDOWS)
    return pl.pallas_call(
        body, grid=(nsteps,),
        in_specs=[pl.BlockSpec((step_rows, DIN), lambda n: (n, 0)),
                  pl.BlockSpec((BLK, DIN), lambda n: (jnp.maximum(n * MIX_SUB - 1, 0), 0)),
                  pl.BlockSpec((step_rows, DMIX), lambda n: (n, 0)),
                  pl.BlockSpec((MIX_SUB, NH, BLK, 2 * BLK), lambda n: (n, 0, 0, 0)),
                  full(1, HD), full(1, HD), full(BLK, 2 * BLK), full(npg, PGD, PGD), full(1, DPOOL)],
        out_specs=[full(T, DIN), full(1, HD), full(1, HD), full(1, 128), full(NBUCK, NH),
                   full(npg, PGD, PGD), full(1, DPOOL)],
        out_shape=(SDS((T, DIN), f32), SDS((1, HD), f32), SDS((1, HD), f32), SDS((1, 128), f32),
                   SDS((NBUCK, NH), f32), SDS((npg, PGD, PGD), f32), SDS((1, DPOOL), f32)),
        scratch_shapes=[pltpu.VMEM((NH, BLK, 2 * BLK), f32)],
        compiler_params=_cparams(("arbitrary",), VMEM_LIMIT_V7X),
        name=name)(z, z, dy, probs, qg, kg, bucket, pool_w, pscale)


class _LocalWeights:
    def __init__(self, w1, wint, wout, w2):
        self.w1, self.wint, self.wout, self.w2 = w1, wint, wout, w2

    def ffn1(self):
        return self.w1

    def first_norm(self, x, gain):
        return _norm_fwd(x, gain, "norm1_fwd")

    def after_ffn1(self, gain, x1):
        return gain

    def mix(self, after):
        return self.wint, self.wout

    def before_out_proj(self, wout, after):
        return wout

    def ffn2(self, after):
        return self.w2

    def out_ffn2_grads_ready(self, dwout, dw2, after):
        return after

    def before_ffn1_bwd(self, dwint, dx1b):
        return dx1b


def _local_step(x, target, weights, g1, gm, g3, qg, kg, sinks, relb, pool_w, pscale):
    bucket = jnp.asarray(_t5_bucket_table())
    sk = sinks.reshape(NH)
    w1 = weights.ffn1()
    h1 = weights.first_norm(x, g1)
    x1, gate1, up1, h2 = _ffn_fwd(h1, w1, x, None, gm, "ffn1_fwd")
    gm = weights.after_ffn1(gm, x1)
    wint, wout = weights.mix(h2)
    z = _in_proj_fwd(h2, wint, "in_proj_fwd")
    ymix, probs = _mix_fwd(z, qg, kg, sk, relb, bucket, pool_w, pscale, "mix_fwd")
    wout = weights.before_out_proj(wout, ymix)
    x2, h3 = _out_proj_fwd(ymix, wout, x1, g3, "out_proj_fwd")
    w2 = weights.ffn2(h3)
    dy, gate2, up2, dyb, loss_lanes = _ffn_fwd(h3, w2, x2, target, None, "ffn2_fwd")

    dx2, dx2b, dg3, dw2 = _ffn_bwd(dyb, h3, gate2, up2, w2, (x2, g3, dy), "ffn2_bwd")
    dymix, dwout = _out_proj_bwd(dx2b, wout, ymix, "out_proj_bwd")
    dymix = weights.out_ffn2_grads_ready(dwout, dw2, dymix)
    dz, dqg, dkg, dsk, drb, dpw, dps = _mix_bwd(z, dymix, probs, qg, kg, relb, bucket, pool_w, pscale, "mix_bwd")
    dx1, dx1b, dgm, dwint = _in_proj_bwd(dz, wint, h2, (x1, gm, dx2), 0.5, "in_proj_bwd")
    dx1b = weights.before_ffn1_bwd(dwint, dx1b)
    gx, _, dg1, dw1 = _ffn_bwd(dx1b, h1, gate1, up1, w1, (x, g1, dx1), "ffn1_bwd")
    small = dict(ffn1_norm=dg1, mix_norm=dgm, ffn2_norm=dg3, pool_scale=dps, q_norm=dqg, k_norm=dkg,
                 attn_sinks=dsk[:, :NH], rel_bias=drb, pool_w=dpw, loss=loss_lanes)
    return gx, (dw1, dwint, dwout, dw2), small


SMALL_NAMES = ("ffn1_norm", "mix_norm", "ffn2_norm", "pool_scale", "q_norm", "k_norm", "attn_sinks", "rel_bias",
               "pool_w", "loss")
SMALL_SHAPES = dict(ffn1_norm=(1, D), mix_norm=(1, D), ffn2_norm=(1, D), pool_scale=(1, DPOOL), q_norm=(1, HD),
                    k_norm=(1, HD), attn_sinks=(1, NH), rel_bias=(NBUCK, NH),
                    pool_w=(1, len(POOL_WINDOWS), PGD, PGD), loss=(1, 128))


def _small_rows(name):
    return -(-int(np.prod(SMALL_SHAPES[name])) // 128)


SMALL_OFF = {}
_r = 0
for _n in SMALL_NAMES:
    SMALL_OFF[_n] = _r
    _r += _small_rows(_n)
SMALL_ROWS = -(-_r // 8) * 8
LOSS_ROW = SMALL_OFF["loss"]


def _pack_small(vals):
    parts = []
    for n in SMALL_NAMES:
        size = _small_rows(n) * 128
        if n in vals:
            flat = vals[n].astype(f32).reshape(-1)
            parts.append(jnp.pad(flat, (0, size - flat.shape[0])))
        else:
            parts.append(jnp.zeros((size,), f32))
    flat = jnp.concatenate(parts)
    flat = jnp.pad(flat, (0, SMALL_ROWS * 128 - flat.shape[0]))
    return flat.reshape(SMALL_ROWS, 128)


def _unpack_small(packed, name):
    size = int(np.prod(SMALL_SHAPES[name]))
    r0 = SMALL_OFF[name]
    return packed[r0:r0 + _small_rows(name)].reshape(-1)[:size].reshape(SMALL_SHAPES[name])


def _position():
    return lax.axis_index("x"), lax.axis_index("y"), lax.axis_index("c")


def _dev_index(x, y, c):
    return 4 * x + 2 * y + c


G1_PIECES, MIX_PIECES, F2_PIECES = (0, 1, 2), (3, 4), (5, 6, 7)


def _group_rows(pieces):
    return sum(PIECE_ROWS[k] for k in pieces)


def _shard_piece(s_ref, k):
    return s_ref.at[pl.ds(PIECE_OFF[k], PIECE_ROWS[k]), :]


def _shard_group(s_ref, pieces):
    return s_ref.at[pl.ds(PIECE_OFF[pieces[0]], _group_rows(pieces)), :]


def _weight_pieces(w1_ref=None, wi_ref=None, wo_ref=None, w2_ref=None):
    arrs = {}
    if w1_ref is not None:
        arrs.update({0: w1_ref.at[0], 1: w1_ref.at[1], 2: w1_ref.at[2]})
    if wi_ref is not None:
        arrs[3] = wi_ref
    if wo_ref is not None:
        arrs[4] = wo_ref
    if w2_ref is not None:
        arrs.update({5: w2_ref.at[0], 6: w2_ref.at[1], 7: w2_ref.at[2]})
    return arrs


def _block_rows(arrs, k, dev):
    r = PIECE_ROWS[k]
    return arrs[k].at[pl.ds(pl.multiple_of(_dev_index(*dev) * r, 16), r), :]


NORM_ROWS = 512


def _all_gather_ffn1(shard, x, gain):
    pieces = G1_PIECES
    rest_pieces = MIX_PIECES + F2_PIECES
    half = FS // 2
    T = x.shape[0]
    SIB, X0, X1, Y0, Y1, RELAY_Y, RELAY_X, ON_X, ON_Y, ON_D0, ON_D1 = range(11)

    def body(s_ref, x_ref, g_ref, w1_ref, h_ref, wi_ref, wo_ref, w2_ref, xbuf, hbuf, rest_buf,
             send_sems, recv_sems, local_sem, norm_sems):
        x, y, c = _position()
        me, sib = (x, y, c), (x, y, 1 - c)
        xn, yn, dg = (1 - x, y, c), (x, 1 - y, c), (1 - x, 1 - y, c)
        arrs = _weight_pieces(w1_ref=w1_ref)

        def place_rest():
            rest = _weight_pieces(wi_ref=wi_ref, wo_ref=wo_ref, w2_ref=w2_ref)
            grp = _shard_group(s_ref, rest_pieces)
            load = pltpu.make_async_copy(grp, rest_buf, norm_sems.at[0])
            load.start()
            load.wait()
            base = PIECE_OFF[rest_pieces[0]]
            for k in rest_pieces:
                pltpu.make_async_copy(rest_buf.at[pl.ds(PIECE_OFF[k] - base, PIECE_ROWS[k]), :],
                                      _block_rows(rest, k, me), norm_sems.at[1]).start()
            pltpu.make_async_copy(grp, rest_buf, norm_sems.at[1]).wait()

        def first_norm():
            for r in range(0, T, NORM_ROWS):
                load = pltpu.make_async_copy(x_ref.at[pl.ds(r, NORM_ROWS), :], xbuf, norm_sems.at[0])
                load.start()
                load.wait()
                xv = xbuf[...]
                rs = lax.rsqrt(jnp.mean(xv * xv, axis=-1, keepdims=True) + EPS)
                hbuf[...] = (xv * rs * g_ref[...]).astype(bf16)
                store = pltpu.make_async_copy(hbuf, h_ref.at[pl.ds(r, NORM_ROWS), :], norm_sems.at[1])
                store.start()
                store.wait()

        def rows_of(k, block, hf):
            r = PIECE_ROWS[k]
            start, size = (0, r) if hf is None else (hf * half, half)
            return arrs[k].at[pl.ds(pl.multiple_of(_dev_index(*block) * r + start, 16), size), :]

        def copies(rel, block, hf, to, from_shard=False):
            def src(k):
                if not from_shard:
                    return rows_of(k, block, hf)
                start, size = (0, PIECE_ROWS[k]) if hf is None else (hf * half, half)
                return s_ref.at[pl.ds(PIECE_OFF[k] + start, size), :]
            return [pltpu.make_async_remote_copy(
                src_ref=src(k), dst_ref=rows_of(k, block, hf), send_sem=send_sems.at[rel], recv_sem=recv_sems.at[rel],
                device_id=to, device_id_type=MESH) for k in pieces]

        def waiter(rel, hf):
            nrows = len(pieces) * (FS if hf is None else half)
            grp = s_ref.at[pl.ds(0, nrows), :]
            return pltpu.make_async_remote_copy(src_ref=grp, dst_ref=grp, send_sem=send_sems.at[rel],
                                                recv_sem=recv_sems.at[rel], device_id=me, device_id_type=MESH)

        def start(cps):
            for cp in cps:
                cp.start()

        mine = [pltpu.make_async_copy(_shard_piece(s_ref, k), _block_rows(arrs, k, me), local_sem) for k in pieces]
        start(mine)
        start(copies(SIB, me, None, sib, True))
        start(copies(X0, me, 0, xn, True))
        start(copies(Y1, me, 1, yn, True))
        start(copies(X1, me, 1, xn, True))
        start(copies(Y0, me, 0, yn, True))
        first_norm()
        place_rest()
        waiter(X0, 0).wait_recv()
        start(copies(RELAY_Y, xn, 0, yn))
        waiter(Y1, 1).wait_recv()
        start(copies(RELAY_X, yn, 1, xn))
        waiter(X1, 1).wait_recv()
        start(copies(ON_X, xn, None, sib))
        waiter(Y0, 0).wait_recv()
        start(copies(ON_Y, yn, None, sib))
        waiter(RELAY_Y, 0).wait_recv()
        start(copies(ON_D0, dg, 0, sib))
        waiter(RELAY_X, 1).wait_recv()
        start(copies(ON_D1, dg, 1, sib))
        waiter(SIB, None).wait_recv()
        waiter(ON_X, None).wait_recv()
        waiter(ON_Y, None).wait_recv()
        waiter(ON_D0, 0).wait_recv()
        waiter(ON_D1, 1).wait_recv()
        for rel, hf in ((SIB, None), (X0, 0), (X1, 1), (Y0, 0), (Y1, 1), (RELAY_Y, 0), (RELAY_X, 1),
                        (ON_X, None), (ON_Y, None), (ON_D0, 0), (ON_D1, 1)):
            waiter(rel, hf).wait_send()
        grp = _shard_group(s_ref, pieces)
        pltpu.make_async_copy(grp, grp, local_sem).wait()

    hbm = pl.BlockSpec(memory_space=pl.ANY)
    return pl.pallas_call(
        body, in_specs=[hbm, hbm, pl.BlockSpec(memory_space=pltpu.VMEM)], out_specs=[hbm] * 5,
        out_shape=(SDS((3, F, D), bf16), SDS((T, D), bf16),
                   SDS((DIN, D), bf16), SDS((DMIX, D), bf16), SDS((3, F, D), bf16)),
        scratch_shapes=[pltpu.VMEM((NORM_ROWS, D), f32), pltpu.VMEM((NORM_ROWS, D), bf16),
                        pltpu.VMEM((_group_rows(rest_pieces), D), bf16),
                        pltpu.SemaphoreType.DMA((11,)), pltpu.SemaphoreType.DMA((11,)), pltpu.SemaphoreType.DMA,
                        pltpu.SemaphoreType.DMA((2,))],
        compiler_params=pltpu.CompilerParams(has_side_effects=True),
        name="all_gather_ffn1")(shard, x, gain)


HBM_SPEC = pl.BlockSpec(memory_space=pltpu.HBM)
SEM_SPEC = pl.BlockSpec(memory_space=pltpu.SEMAPHORE)
ANY_SPEC = pl.BlockSpec(memory_space=pl.ANY)
SPLIT_EFFECT = pltpu.SideEffectType.DATAFLOW_SIDE_EFFECTING


def _in_hbm(a):
    return pltpu.with_memory_space_constraint(a, pltpu.HBM)


def _hbm_like(a):
    return pltpu.HBM(a.shape, a.dtype)


def _gather_rest_start(shard, wi, wo, w2, w1):
    def body(s_ref, wi_ref, wo_ref, w2_ref, w1_ref,
             ssem_m, rsem_m0, rsem_m, ssem_f, rsem_f0, rsem_f, s_o, wi_o, wo_o, w2_o, w1_o):
        x, y, c = _position()
        me, sib = (x, y, c), (x, y, 1 - c)
        chips = [(1 - x, y), (x, 1 - y), (1 - x, 1 - y)]
        arrs = _weight_pieces(wi_ref=wi_ref, wo_ref=wo_ref, w2_ref=w2_ref)
        for pieces, ssem, rsem0, rsem in ((MIX_PIECES, ssem_m, rsem_m0, rsem_m), (F2_PIECES, ssem_f, rsem_f0, rsem_f)):
            for p in pieces:
                pltpu.make_async_remote_copy(
                    src_ref=_shard_piece(s_ref, p), dst_ref=_block_rows(arrs, p, me), send_sem=ssem.at[0],
                    recv_sem=rsem0, device_id=sib, device_id_type=MESH).start()
            for j, chip in enumerate(chips):
                for p in pieces:
                    pltpu.make_async_remote_copy(
                        src_ref=_shard_piece(s_ref, p), dst_ref=_block_rows(arrs, p, me), send_sem=ssem.at[1 + j],
                        recv_sem=rsem.at[j], device_id=(*chip, c), device_id_type=MESH).start()

    dma = pltpu.SemaphoreType.DMA
    return pl.pallas_call(
        body, name="gather_rest_start",
        out_shape=(dma((4,)), dma(()), dma((3,)), dma((4,)), dma(()), dma((3,)),
                   _hbm_like(shard), _hbm_like(wi), _hbm_like(wo), _hbm_like(w2), _hbm_like(w1)),
        in_specs=(HBM_SPEC,) * 5, out_specs=(SEM_SPEC,) * 6 + (HBM_SPEC,) * 5,
        input_output_aliases={0: 6, 1: 7, 2: 8, 3: 9, 4: 10},
        compiler_params=pltpu.CompilerParams(has_side_effects=SPLIT_EFFECT),
    )(_in_hbm(shard), _in_hbm(wi), _in_hbm(wo), _in_hbm(w2), _in_hbm(w1))


def _gather_mix_pass_on(rsem_m, wi, wo, thru, after):
    def body(wi_ref, wo_ref, thru_ref, rsem, after_ref, fsend, frecv, wi_o, wo_o, thru_o):
        x, y, c = _position()
        sib = (x, y, 1 - c)
        arrs = _weight_pieces(wi_ref=wi_ref, wo_ref=wo_ref)
        both = wi_ref.at[pl.ds(0, _group_rows(MIX_PIECES)), :]
        for j, chip in enumerate([(1 - x, y), (x, 1 - y), (1 - x, 1 - y)]):
            pltpu.make_async_remote_copy(src_ref=both, dst_ref=both, send_sem=fsend.at[j], recv_sem=rsem.at[j],
                                         device_id=(x, y, c), device_id_type=MESH).wait_recv()
            for p in MIX_PIECES:
                rows = _block_rows(arrs, p, (*chip, c))
                pltpu.make_async_remote_copy(src_ref=rows, dst_ref=rows, send_sem=fsend.at[j], recv_sem=frecv.at[j],
                                             device_id=sib, device_id_type=MESH).start()

    dma = pltpu.SemaphoreType.DMA
    return pl.pallas_call(
        body, name="gather_mix_pass_on",
        out_shape=(dma((3,)), dma((3,)), _hbm_like(wi), _hbm_like(wo), _hbm_like(thru)),
        in_specs=(HBM_SPEC, HBM_SPEC, HBM_SPEC, SEM_SPEC, ANY_SPEC), out_specs=(SEM_SPEC, SEM_SPEC) + (HBM_SPEC,) * 3,
        input_output_aliases={0: 2, 1: 3, 2: 4},
        compiler_params=pltpu.CompilerParams(has_side_effects=SPLIT_EFFECT),
    )(wi, wo, _in_hbm(thru), rsem_m, after)


def _gather_mix_wait(ssem_m, rsem_m0, fsend, frecv, shard, wi, wo, after):
    def body(s_ref, wi_ref, wo_ref, ssem, rsem0, fs, fr, after_ref, s_o, wi_o, wo_o):
        x, y, c = _position()
        grp = _shard_group(s_ref, MIX_PIECES)

        def waiter(send_sem, recv_sem):
            return pltpu.make_async_remote_copy(src_ref=grp, dst_ref=grp, send_sem=send_sem, recv_sem=recv_sem,
                                                device_id=(x, y, c), device_id_type=MESH)

        waiter(ssem.at[0], rsem0).wait_recv()
        for j in range(3):
            waiter(fs.at[j], fr.at[j]).wait_recv()
        for rel in range(4):
            waiter(ssem.at[rel], rsem0).wait_send()
        for j in range(3):
            waiter(fs.at[j], fr.at[j]).wait_send()

    return pl.pallas_call(
        body, name="gather_mix_wait", out_shape=(_hbm_like(shard), _hbm_like(wi), _hbm_like(wo)),
        in_specs=(HBM_SPEC,) * 3 + (SEM_SPEC,) * 4 + (ANY_SPEC,), out_specs=(HBM_SPEC,) * 3,
        input_output_aliases={0: 0, 1: 1, 2: 2},
        compiler_params=pltpu.CompilerParams(has_side_effects=SPLIT_EFFECT),
    )(shard, wi, wo, ssem_m, rsem_m0, fsend, frecv, after)


def _gather_ffn2_pass_on(rsem_f, w2, wo, after):
    def body(w2_ref, wo_ref, rsem, after_ref, fsend, frecv, w2_o, wo_o):
        x, y, c = _position()
        sib = (x, y, 1 - c)
        chips = [(1 - x, y), (x, 1 - y), (1 - x, 1 - y)]
        arrs = _weight_pieces(w2_ref=w2_ref)
        three = w2_ref.at[0, pl.ds(0, _group_rows(F2_PIECES)), :]
        for j, chip in enumerate(chips):
            pltpu.make_async_remote_copy(src_ref=three, dst_ref=three, send_sem=fsend.at[j], recv_sem=rsem.at[j],
                                         device_id=(x, y, c), device_id_type=MESH).wait_recv()
            for p in F2_PIECES:
                rows = _block_rows(arrs, p, (*chip, c))
                pltpu.make_async_remote_copy(src_ref=rows, dst_ref=rows, send_sem=fsend.at[j], recv_sem=frecv.at[j],
                                             device_id=sib, device_id_type=MESH).start()

    dma = pltpu.SemaphoreType.DMA
    return pl.pallas_call(
        body, name="gather_ffn2_pass_on", out_shape=(dma((3,)), dma((3,)), _hbm_like(w2), _hbm_like(wo)),
        in_specs=(HBM_SPEC, HBM_SPEC, SEM_SPEC, ANY_SPEC), out_specs=(SEM_SPEC, SEM_SPEC, HBM_SPEC, HBM_SPEC),
        input_output_aliases={0: 2, 1: 3},
        compiler_params=pltpu.CompilerParams(has_side_effects=SPLIT_EFFECT),
    )(w2, wo, rsem_f, after)


def _gather_ffn2_wait(ssem_f, rsem_f0, fsend, frecv, shard, w2, after):
    def body(s_ref, w2_ref, ssem, rsem0, fs, fr, after_ref, w2_o):
        x, y, c = _position()
        grp = _shard_group(s_ref, F2_PIECES)

        def waiter(send_sem, recv_sem):
            return pltpu.make_async_remote_copy(src_ref=grp, dst_ref=grp, send_sem=send_sem, recv_sem=recv_sem,
                                                device_id=(x, y, c), device_id_type=MESH)

        waiter(ssem.at[0], rsem0).wait_recv()
        for j in range(3):
            waiter(fs.at[j], fr.at[j]).wait_recv()
        for rel in range(4):
            waiter(ssem.at[rel], rsem0).wait_send()
        for j in range(3):
            waiter(fs.at[j], fr.at[j]).wait_send()

    return pl.pallas_call(
        body, name="gather_ffn2_wait", out_shape=_hbm_like(w2),
        in_specs=(HBM_SPEC, HBM_SPEC, SEM_SPEC, SEM_SPEC, SEM_SPEC, SEM_SPEC, ANY_SPEC), out_specs=HBM_SPEC,
        input_output_aliases={1: 0},
        compiler_params=pltpu.CompilerParams(has_side_effects=SPLIT_EFFECT),
    )(shard, w2, ssem_f, rsem_f0, fsend, frecv, after)


class _GatheredWeights(_LocalWeights):
    def __init__(self, shard, x, gain1):
        w1, self.h1, wi, wo, w2 = _all_gather_ffn1(shard, x, gain1)
        (self.ssem_m, self.rsem_m0, self.rsem_m, self.ssem_f, self.rsem_f0, self.rsem_f,
         self.shard, self.wi, self.wo, self.w2_part, self.w1) = _gather_rest_start(shard, wi, wo, w2, w1)

    def first_norm(self, x, gain):
        return self.h1

    def after_ffn1(self, gain, x1):
        self.fsend_m, self.frecv_m, self.wi, self.wo, gain = _gather_mix_pass_on(self.rsem_m, self.wi, self.wo, gain, x1)
        return gain

    def mix(self, after):
        self.shard, wint, wout = _gather_mix_wait(self.ssem_m, self.rsem_m0, self.fsend_m, self.frecv_m, self.shard,
                                                  self.wi, self.wo, after)
        return wint, wout

    def before_out_proj(self, wout, after):
        self.fsend, self.frecv, self.w2_part, wout = _gather_ffn2_pass_on(self.rsem_f, self.w2_part, wout, after)
        return wout

    def ffn2(self, after):
        return _gather_ffn2_wait(self.ssem_f, self.rsem_f0, self.fsend, self.frecv, self.shard, self.w2_part, after)

    def out_ffn2_grads_ready(self, dwout, dw2, after):
        rx1 = lax.empty((4, RSA_ROWS, D), bf16)
        sa, ra, sent, rx1, after = _rsa_level1_start(dict(wo=dwout, w2=dw2), rx1, after, "rsa_level1_start_out_ffn2")
        self.level1 = ((sa, ra), sent, rx1)
        return after

    def before_ffn1_bwd(self, dwint, dx1b):
        early, sent, rx1 = self.level1
        sa, ra, late, rx1, dx1b = _rsa_level1_start(dict(wi=dwint), rx1, dx1b, "rsa_level1_start_in")
        groups = (((MIX_PIECES[1],) + F2_PIECES, *early), ((MIX_PIECES[0],), sa, ra))
        dwint, dwout, dw2, rx1 = _rsa_level1_wait(groups, late["wi"], sent["wo"], sent["w2"], rx1, dx1b)
        tx, self.acc = _rsa_chip_sums(dwint, dwout, dw2, rx1)
        rx2 = lax.empty((3, RSA_ROWS, D), bf16)
        self.sb, self.rb, self.tx, self.rx2, dx1b = _rsa_level2_start(tx, rx2, dx1b)
        return dx1b

    def mix_ffn2_grads_parts(self, after):
        rx2 = _rsa_level2_wait(self.sb, self.rb, self.tx, self.rx2, after)
        return self.acc, rx2


def _reduce_scatter_ffn1_head(dw1, small_packed):
    pieces = G1_PIECES
    half = FS // 2
    hrows = len(pieces) * half
    nrows = 2 * hrows
    X_RELAY, Y_RELAY = range(2)

    def body(d1_ref, p_ref, forx_ref, fory_ref, own_ref, rx1_ref, relx_ref, rely_ref, tot_ref,
             own_buf, rx_buf, tx1, tx2, tx3, acc, sa, ra, sb, rb, lsem, pair, chips, small_send, small_recv):
        x, y, c = _position()
        me, sib = (x, y, c), (x, y, 1 - c)
        xn, yn = (1 - x, y, c), (x, 1 - y, c)
        rel_chips = [(x, y), (1 - x, y), (x, 1 - y), (1 - x, 1 - y)]
        srcs = _weight_pieces(w1_ref=d1_ref)

        my_chip = 2 * x + y
        pair[c] = p_ref[...]
        swap = pltpu.make_async_remote_copy(
            src_ref=p_ref, dst_ref=pair.at[c], send_sem=small_send.at[0], recv_sem=small_recv.at[0],
            device_id=sib, device_id_type=MESH)
        swap.start()
        small = [pltpu.make_async_remote_copy(
            src_ref=chips.at[my_chip], dst_ref=chips.at[my_chip], send_sem=small_send.at[j], recv_sem=small_recv.at[j],
            device_id=(*rel_chips[j], c), device_id_type=MESH) for j in (1, 2, 3)]

        def part(k, dev, hf):
            r = PIECE_ROWS[k]
            return srcs[k].at[pl.ds(pl.multiple_of(_dev_index(*dev) * r + hf * half, 16), half), :]

        def slot(ref, k, hf):
            return ref.at[pl.ds(hf * hrows + k * half, half), :]

        halves = [(k, hf) for hf in (0, 1) for k in pieces]

        for j in (3, 1, 2, 0):
            for k, hf in halves:
                pltpu.make_async_remote_copy(
                    src_ref=part(k, (*rel_chips[j], 1 - c), hf), dst_ref=slot(rx1_ref.at[j], k, hf),
                    send_sem=sa.at[j], recv_sem=ra.at[j], device_id=sib, device_id_type=MESH).start()

        def wait_a(j):
            return pltpu.make_async_remote_copy(src_ref=rx1_ref.at[j], dst_ref=rx1_ref.at[j], send_sem=sa.at[j],
                                                recv_sem=ra.at[j], device_id=me, device_id_type=MESH)

        def ici(rel, src, dst, to):
            return pltpu.make_async_remote_copy(src_ref=src, dst_ref=dst, send_sem=sb.at[rel], recv_sem=rb.at[rel],
                                                device_id=to, device_id_type=MESH)

        first, second = pl.ds(0, hrows), pl.ds(hrows, hrows)
        sends = {
            X_RELAY: ici(X_RELAY, tx3.at[first, :], relx_ref, xn),
            Y_RELAY: ici(Y_RELAY, tx3.at[second, :], rely_ref, yn),
        }

        swap.wait_recv()
        chips[my_chip] = pair[0] + pair[1]
        for cp in small:
            cp.start()

        def chip_sum(j, dst):
            loads = [pltpu.make_async_copy(part(k, (*rel_chips[j], c), hf), slot(own_buf, k, hf), lsem.at[0])
                     for k, hf in halves]
            for cp in loads:
                cp.start()
            wait_a(j).wait_recv()
            got = pltpu.make_async_copy(rx1_ref.at[j], rx_buf, lsem.at[1])
            got.start()
            pltpu.make_async_copy(rx_buf, rx_buf, lsem.at[0]).wait()
            got.wait()

            def add(i, carry):
                rows = pl.ds(pl.multiple_of(i * half, 16), half)
                tot = own_buf[rows, :].astype(f32) + rx_buf[rows, :].astype(f32)
                dst[rows, :] = tot.astype(dst.dtype)
                return carry

            lax.fori_loop(0, nrows // half, add, 0)

        def add_landed(landed, dst, rows0, nrows_):
            got = pltpu.make_async_copy(landed, rx_buf.at[pl.ds(0, nrows_), :], lsem.at[1])
            got.start()
            got.wait()

            def add(i, carry):
                src_rows = pl.ds(pl.multiple_of(i * half, 16), half)
                dst_rows = pl.ds(pl.multiple_of(rows0 + i * half, 16), half)
                dst[dst_rows, :] = (dst[dst_rows, :].astype(f32) + rx_buf[src_rows, :].astype(f32)).astype(dst.dtype)
                return carry

            lax.fori_loop(0, nrows_ // half, add, 0)

        chip_sum(3, tx3)
        sends[X_RELAY].start()
        sends[Y_RELAY].start()
        chip_sum(1, tx1)
        chip_sum(2, tx2)
        chip_sum(0, acc)
        own_out = pltpu.make_async_copy(acc, own_ref, lsem.at[0])
        own_out.start()
        sends[X_RELAY].wait_recv()
        add_landed(relx_ref, tx2, 0, hrows)
        sends[Y_RELAY].wait_recv()
        add_landed(rely_ref, tx1, hrows, hrows)
        own_out.wait()
        outs = [pltpu.make_async_copy(tx1, forx_ref, lsem.at[0]), pltpu.make_async_copy(tx2, fory_ref, lsem.at[1])]
        for cp in outs:
            cp.start()
        for cp in outs:
            cp.wait()
        for cp in small:
            cp.wait_recv()
        tot = (chips[0] + chips[1]) + (chips[2] + chips[3])
        tot_ref[...] = tot
        loss = jnp.sum(tot[LOSS_ROW:LOSS_ROW + 1, :], axis=-1, keepdims=True)
        tot_ref[LOSS_ROW:LOSS_ROW + 1, :] = jnp.broadcast_to(loss, (1, 128))
        for j in range(4):
            wait_a(j).wait_send()
        for cp in sends.values():
            cp.wait_send()
        swap.wait_send()
        for cp in small:
            cp.wait_send()

    hbm = pl.BlockSpec(memory_space=pl.ANY)
    vm = pl.BlockSpec(memory_space=pltpu.VMEM)
    outs = pl.pallas_call(
        body, in_specs=[hbm, vm], out_specs=[hbm] * 6 + [vm],
        out_shape=(SDS((nrows, D), bf16), SDS((nrows, D), bf16), SDS((nrows, D), f32), SDS((4, nrows, D), bf16),
                   SDS((hrows, D), bf16), SDS((hrows, D), bf16), SDS((SMALL_ROWS, 128), f32)),
        scratch_shapes=[pltpu.VMEM((nrows, D), bf16), pltpu.VMEM((nrows, D), bf16),
                        pltpu.VMEM((nrows, D), bf16), pltpu.VMEM((nrows, D), bf16), pltpu.VMEM((nrows, D), bf16),
                        pltpu.VMEM((nrows, D), f32),
                        pltpu.SemaphoreType.DMA((4,)), pltpu.SemaphoreType.DMA((4,)),
                        pltpu.SemaphoreType.DMA((2,)), pltpu.SemaphoreType.DMA((2,)), pltpu.SemaphoreType.DMA((2,)),
                        pltpu.VMEM((2, SMALL_ROWS, 128), f32), pltpu.VMEM((4, SMALL_ROWS, 128), f32),
                        pltpu.SemaphoreType.DMA((4,)), pltpu.SemaphoreType.DMA((4,))],
        compiler_params=pltpu.CompilerParams(has_side_effects=True, vmem_limit_bytes=VMEM_LIMIT_V7X),
        name="reduce_scatter_ffn1_head")(dw1, small_packed)
    return outs[0], outs[1], outs[2], outs[-1]


def _rs1_tail_start(for_x, for_y, from_x, from_y, thru):
    def body(fx_ref, fy_ref, lx_ref, ly_ref, thru_ref, ssem, rsem, fx_o, fy_o, lx_o, ly_o, thru_o):
        x, y, c = _position()
        pltpu.make_async_remote_copy(src_ref=fx_ref, dst_ref=lx_ref, send_sem=ssem.at[0], recv_sem=rsem.at[0],
                                     device_id=(1 - x, y, c), device_id_type=MESH).start()
        pltpu.make_async_remote_copy(src_ref=fy_ref, dst_ref=ly_ref, send_sem=ssem.at[1], recv_sem=rsem.at[1],
                                     device_id=(x, 1 - y, c), device_id_type=MESH).start()

    dma = pltpu.SemaphoreType.DMA
    arrs = (for_x, for_y, from_x, from_y, thru)
    return pl.pallas_call(
        body, name="rs1_tail_start", out_shape=(dma((2,)), dma((2,))) + tuple(_hbm_like(a) for a in arrs),
        in_specs=(HBM_SPEC,) * 5, out_specs=(SEM_SPEC,) * 2 + (HBM_SPEC,) * 5,
        input_output_aliases={0: 2, 1: 3, 2: 4, 3: 5, 4: 6},
        compiler_params=pltpu.CompilerParams(has_side_effects=SPLIT_EFFECT),
    )(*[_in_hbm(a) for a in arrs])


def _rs1_tail_wait(ssem, rsem, for_x, for_y, from_x, from_y, after):
    def body(fx_ref, fy_ref, lx_ref, ly_ref, ssem_ref, rsem_ref, after_ref, lx_o, ly_o):
        x, y, c = _position()
        for j, (src, dst) in enumerate(((fx_ref, lx_ref), (fy_ref, ly_ref))):
            d = pltpu.make_async_remote_copy(src_ref=src, dst_ref=dst, send_sem=ssem_ref.at[j], recv_sem=rsem_ref.at[j],
                                             device_id=(x, y, c), device_id_type=MESH)
            d.wait_recv()
            d.wait_send()

    return pl.pallas_call(
        body, name="rs1_tail_wait", out_shape=(_hbm_like(from_x), _hbm_like(from_y)),
        in_specs=(HBM_SPEC,) * 4 + (SEM_SPEC, SEM_SPEC, ANY_SPEC), out_specs=(HBM_SPEC, HBM_SPEC),
        input_output_aliases={2: 0, 3: 1},
        compiler_params=pltpu.CompilerParams(has_side_effects=SPLIT_EFFECT),
    )(for_x, for_y, from_x, from_y, ssem, rsem, after)


RSA_PIECES = MIX_PIECES + F2_PIECES
RSA_ROWS = _group_rows(RSA_PIECES)
RSA_OFF = {k: PIECE_OFF[k] - PIECE_OFF[RSA_PIECES[0]] for k in RSA_PIECES}
RSA_BLOCK = 192


def _rsa_rows(ref, k):
    return ref.at[pl.ds(RSA_OFF[k], PIECE_ROWS[k]), :]


def _rsa_level1_start(grads, rx1, thru, name):
    keys = sorted(grads)
    n = len(keys)

    def body(*refs):
        srcs = _weight_pieces(**{k + "_ref": ref for k, ref in zip(keys, refs[:n])})
        rx1_ref, sa, ra = refs[n], refs[n + 2], refs[n + 3]
        x, y, c = _position()
        for j, chip in enumerate([(x, y), (1 - x, y), (x, 1 - y), (1 - x, 1 - y)]):
            for k in sorted(srcs):
                pltpu.make_async_remote_copy(
                    src_ref=_block_rows(srcs, k, (*chip, 1 - c)), dst_ref=_rsa_rows(rx1_ref.at[j], k),
                    send_sem=sa.at[j], recv_sem=ra.at[j], device_id=(x, y, 1 - c), device_id_type=MESH).start()

    dma = pltpu.SemaphoreType.DMA
    arrs = tuple(grads[k] for k in keys) + (rx1, thru)
    outs = pl.pallas_call(
        body, name=name, out_shape=(dma((4,)), dma((4,))) + tuple(_hbm_like(a) for a in arrs),
        in_specs=(HBM_SPEC,) * len(arrs), out_specs=(SEM_SPEC,) * 2 + (HBM_SPEC,) * len(arrs),
        input_output_aliases={i: i + 2 for i in range(len(arrs))},
        compiler_params=pltpu.CompilerParams(has_side_effects=SPLIT_EFFECT),
    )(*[_in_hbm(a) for a in arrs])
    return outs[0], outs[1], dict(zip(keys, outs[2:2 + n])), outs[2 + n], outs[3 + n]


def _rsa_level1_wait(groups, dwint, dwout, dw2, rx1, after):
    ngroup = len(groups)

    def body(di_ref, do_ref, d2_ref, rx1_ref, *rest):
        x, y, c = _position()
        for g, (pieces, _, _) in enumerate(groups):
            sa_ref, ra_ref = rest[2 * g], rest[2 * g + 1]
            for j in range(4):
                rows = rx1_ref.at[j, pl.ds(RSA_OFF[pieces[0]], _group_rows(pieces)), :]
                d = pltpu.make_async_remote_copy(src_ref=rows, dst_ref=rows, send_sem=sa_ref.at[j],
                                                 recv_sem=ra_ref.at[j], device_id=(x, y, c), device_id_type=MESH)
                d.wait_recv()
                d.wait_send()

    arrs = (dwint, dwout, dw2, rx1)
    sems = [sem for _, sa, ra in groups for sem in (sa, ra)]
    return pl.pallas_call(
        body, name="rsa_level1_wait", out_shape=tuple(_hbm_like(a) for a in arrs),
        in_specs=(HBM_SPEC,) * 4 + (SEM_SPEC,) * (2 * ngroup) + (ANY_SPEC,), out_specs=(HBM_SPEC,) * 4,
        input_output_aliases={0: 0, 1: 1, 2: 2, 3: 3},
        compiler_params=pltpu.CompilerParams(has_side_effects=SPLIT_EFFECT),
    )(*arrs, *sems, after)


def _rsa_chip_sums(dwint, dwout, dw2, rx1):
    nblk = RSA_ROWS // RSA_BLOCK

    def body(di_ref, do_ref, d2_ref, rx1_ref, tx_ref, acc_ref, own_buf, rx_buf, tx_buf, acc_buf, in_sems, out_sems):
        x, y, c = _position()
        srcs = _weight_pieces(wi_ref=di_ref, wo_ref=do_ref, w2_ref=d2_ref)
        chips = [(x, y), (1 - x, y), (x, 1 - y), (1 - x, 1 - y)]

        def start_loads(j):
            s = j % 2
            for k in RSA_PIECES:
                pltpu.make_async_copy(_block_rows(srcs, k, (*chips[j], c)), _rsa_rows(own_buf.at[s], k),
                                      in_sems.at[2 * s]).start()
            pltpu.make_async_copy(rx1_ref.at[j], rx_buf.at[s], in_sems.at[2 * s + 1]).start()

        def wait_loads(j):
            s = j % 2
            pltpu.make_async_copy(rx1_ref.at[j], own_buf.at[s], in_sems.at[2 * s]).wait()
            pltpu.make_async_copy(rx1_ref.at[j], rx_buf.at[s], in_sems.at[2 * s + 1]).wait()

        def store(j):
            if j == 0:
                return pltpu.make_async_copy(acc_buf, acc_ref, out_sems.at[2])
            return pltpu.make_async_copy(tx_buf.at[j % 2], tx_ref.at[j - 1], out_sems.at[j % 2])

        start_loads(0)
        for j in range(4):
            s = j % 2
            if j + 1 < 4:
                start_loads(j + 1)
            wait_loads(j)
            if j == 3:
                store(1).wait()

            def add(i, carry, j=j, s=s):
                rows = pl.ds(pl.multiple_of(i * RSA_BLOCK, 16), RSA_BLOCK)
                tot = own_buf[s, rows, :].astype(f32) + rx_buf[s, rows, :].astype(f32)
                if j == 0:
                    acc_buf[rows, :] = tot
                else:
                    tx_buf[s, rows, :] = tot.astype(bf16)
                return carry

            lax.fori_loop(0, nblk, add, 0)
            store(j).start()
        store(0).wait()
        store(2).wait()
        store(3).wait()

    return pl.pallas_call(
        body, in_specs=[ANY_SPEC] * 4, out_specs=[ANY_SPEC] * 2,
        out_shape=(SDS((3, RSA_ROWS, D), bf16), SDS((RSA_ROWS, D), f32)),
        scratch_shapes=[pltpu.VMEM((2, RSA_ROWS, D), bf16), pltpu.VMEM((2, RSA_ROWS, D), bf16),
                        pltpu.VMEM((2, RSA_ROWS, D), bf16), pltpu.VMEM((RSA_ROWS, D), f32),
                        pltpu.SemaphoreType.DMA((4,)), pltpu.SemaphoreType.DMA((3,))],
        compiler_params=_cparams(None, VMEM_LIMIT_V7X), name="rsa_chip_sums")(dwint, dwout, dw2, rx1)


def _rsa_level2_start(tx, rx2, thru):
    def body(tx_ref, rx2_ref, thru_ref, sb, rb, tx_o, rx2_o, thru_o):
        x, y, c = _position()
        for j, chip in enumerate([(1 - x, y), (x, 1 - y), (1 - x, 1 - y)]):
            pltpu.make_async_remote_copy(src_ref=tx_ref.at[j], dst_ref=rx2_ref.at[j], send_sem=sb.at[j],
                                         recv_sem=rb.at[j], device_id=(*chip, c), device_id_type=MESH).start()

    dma = pltpu.SemaphoreType.DMA
    arrs = (tx, rx2, thru)
    return pl.pallas_call(
        body, name="rsa_level2_start", out_shape=(dma((3,)), dma((3,))) + tuple(_hbm_like(a) for a in arrs),
        in_specs=(HBM_SPEC,) * 3, out_specs=(SEM_SPEC,) * 2 + (HBM_SPEC,) * 3,
        input_output_aliases={0: 2, 1: 3, 2: 4},
        compiler_params=pltpu.CompilerParams(has_side_effects=SPLIT_EFFECT),
    )(*[_in_hbm(a) for a in arrs])


def _rsa_level2_wait(sb, rb, tx, rx2, after):
    def body(tx_ref, rx2_ref, sb_ref, rb_ref, after_ref, rx2_o):
        x, y, c = _position()
        for j in range(3):
            d = pltpu.make_async_remote_copy(src_ref=tx_ref.at[j], dst_ref=rx2_ref.at[j], send_sem=sb_ref.at[j],
                                             recv_sem=rb_ref.at[j], device_id=(x, y, c), device_id_type=MESH)
            d.wait_recv()
            d.wait_send()

    return pl.pallas_call(
        body, name="rsa_level2_wait", out_shape=_hbm_like(rx2),
        in_specs=(HBM_SPEC, HBM_SPEC, SEM_SPEC, SEM_SPEC, ANY_SPEC), out_specs=HBM_SPEC,
        input_output_aliases={1: 0},
        compiler_params=pltpu.CompilerParams(has_side_effects=SPLIT_EFFECT),
    )(tx, rx2, sb, rb, after)


def _adamw_math(w, g, m, v):
    m = ADAM_B1 * m + (1.0 - ADAM_B1) * g
    v = ADAM_B2 * v + (1.0 - ADAM_B2) * (g * g)
    m_hat = m / (1.0 - ADAM_B1 ** ADAM_STEP)
    v_hat = v / (1.0 - ADAM_B2 ** ADAM_STEP)
    delta = -ADAM_LR * (m_hat / (jnp.sqrt(v_hat) + ADAM_EPS) + ADAM_WD * w)
    return delta, m, v


def _adamw_big(pieces, ws, ms, vs, own, landed, name):
    npiece = len(pieces)
    nland = sum(a.shape[0] if a.ndim == 3 else 1 for a in landed)
    rmax = max(PIECE_ROWS[k] for k in pieces)
    half = FS // 2

    def segments(k):
        if k in G1_PIECES:
            return [(hf * len(G1_PIECES) * half + k * half, hf * half, half) for hf in (0, 1)]
        return [(RSA_OFF[k], 0, PIECE_ROWS[k])]

    def body(*refs):
        ins = (refs[0:npiece], refs[npiece:2 * npiece], refs[2 * npiece:3 * npiece])
        own_ref = refs[3 * npiece]
        nin = 3 * npiece + 1 + len(landed)
        land_refs = []
        for ref, a in zip(refs[3 * npiece + 1:nin], landed):
            land_refs += [ref.at[j] for j in range(a.shape[0])] if a.ndim == 3 else [ref]
        out_refs = refs[nin:nin + 4 * npiece]
        inb, landb, outb, in_sems, land_sems, out_sems = refs[nin + 4 * npiece:]

        def loads(i):
            s, k = i % 2, pieces[i]
            r = PIECE_ROWS[k]
            cps = [pltpu.make_async_copy(ins[q][i].at[0], inb.at[s, q, pl.ds(0, r), :], in_sems.at[4 * s + q])
                   for q in range(3)]
            waits = list(cps)
            for src0, dst0, n in segments(k):
                cps.append(pltpu.make_async_copy(own_ref.at[pl.ds(src0, n), :], inb.at[s, 3, pl.ds(dst0, n), :],
                                                 in_sems.at[4 * s + 3]))
                for p in range(nland):
                    cps.append(pltpu.make_async_copy(land_refs[p].at[pl.ds(src0, n), :],
                                                     landb.at[s, p, pl.ds(dst0, n), :], land_sems.at[nland * s + p]))
            own_rows = inb.at[s, 3, pl.ds(0, r), :]
            waits.append(pltpu.make_async_copy(own_rows, own_rows, in_sems.at[4 * s + 3]))
            for p in range(nland):
                rows = landb.at[s, p, pl.ds(0, r), :]
                waits.append(pltpu.make_async_copy(rows, rows, land_sems.at[nland * s + p]))
            return cps, waits

        def stores(i):
            s, r = i % 2, PIECE_ROWS[pieces[i]]
            return [pltpu.make_async_copy(outb.at[s, q, pl.ds(0, r), :], out_refs[q * npiece + i].at[0],
                                          out_sems.at[4 * s + q]) for q in range(4)]

        for cp in loads(0)[0]:
            cp.start()
        for i in range(npiece):
            s, r = i % 2, PIECE_ROWS[pieces[i]]
            if i + 1 < npiece:
                for cp in loads(i + 1)[0]:
                    cp.start()
            for cp in loads(i)[1]:
                cp.wait()
            if i >= 2:
                for cp in stores(i - 2):
                    cp.wait()
            g = inb[s, 3, 0:r, :]
            for p in range(nland):
                g = g + landb[s, p, 0:r, :].astype(f32)
            d, nm, nv = _adamw_math(inb[s, 0, 0:r, :], g, inb[s, 1, 0:r, :], inb[s, 2, 0:r, :])
            outb[s, 0, 0:r, :] = g
            outb[s, 1, 0:r, :] = d
            outb[s, 2, 0:r, :] = nm
            outb[s, 3, 0:r, :] = nv
            for cp in stores(i):
                cp.start()
        for i in range(max(npiece - 2, 0), npiece):
            for cp in stores(i):
                cp.wait()

    hbm = pl.BlockSpec(memory_space=pl.ANY)
    outs = pl.pallas_call(
        body, in_specs=[hbm] * (3 * npiece + 1 + len(landed)), out_specs=[hbm] * (4 * npiece),
        out_shape=tuple(SDS(w.shape, f32) for _ in range(4) for w in ws),
        scratch_shapes=[pltpu.VMEM((2, 4, rmax, D), f32), pltpu.VMEM((2, nland, rmax, D), bf16),
                        pltpu.VMEM((2, 4, rmax, D), f32),
                        pltpu.SemaphoreType.DMA((8,)), pltpu.SemaphoreType.DMA((2 * nland,)),
                        pltpu.SemaphoreType.DMA((8,))],
        compiler_params=_cparams(None, VMEM_LIMIT_V7X), name=name)(*ws, *ms, *vs, own, *landed)
    return [list(outs[q * npiece:(q + 1) * npiece]) for q in range(4)]


def _adamw_small(ws, ms, vs, gs, name):
    n = len(ws)

    def body(*refs):
        w_refs, m_refs, v_refs, g_refs = refs[0:n], refs[n:2 * n], refs[2 * n:3 * n], refs[3 * n:4 * n]
        outs = refs[4 * n:]
        for i in range(n):
            d, nm, nv = _adamw_math(w_refs[i][...], g_refs[i][...], m_refs[i][...], v_refs[i][...])
            outs[i][...] = d
            outs[n + i][...] = nm
            outs[2 * n + i][...] = nv

    outs = pl.pallas_call(
        body, out_shape=tuple(SDS(w.shape, f32) for _ in range(3) for w in ws), name=name)(*ws, *ms, *vs, *gs)
    return [list(outs[q * n:(q + 1) * n]) for q in range(3)]


WEIGHTS = ("ffn1_norm", "ffn1_w_gate", "ffn1_w_up", "ffn1_w_down", "mix_norm", "w_in", "q_norm", "k_norm",
           "attn_sinks", "rel_bias", "pool_w", "pool_scale", "w_out", "ffn2_norm", "ffn2_w_gate", "ffn2_w_up",
           "ffn2_w_down")
BIG = (("ffn1_w_gate", True), ("ffn1_w_up", True), ("ffn1_w_down", False), ("w_in", True), ("w_out", False),
       ("ffn2_w_gate", True), ("ffn2_w_up", True), ("ffn2_w_down", False))


def kernel(x, ffn1_norm, ffn1_w_gate, ffn1_w_up, ffn1_w_down, mix_norm, w_in, q_norm, k_norm, attn_sinks, rel_bias, pool_w, pool_scale, w_out, ffn2_norm, ffn2_w_gate, ffn2_w_up, ffn2_w_down, loss_target, m_ffn1_norm, m_ffn1_w_gate, m_ffn1_w_up, m_ffn1_w_down, m_mix_norm, m_w_in, m_q_norm, m_k_norm, m_attn_sinks, m_rel_bias, m_pool_w, m_pool_scale, m_w_out, m_ffn2_norm, m_ffn2_w_gate, m_ffn2_w_up, m_ffn2_w_down, v_ffn1_norm, v_ffn1_w_gate, v_ffn1_w_up, v_ffn1_w_down, v_mix_norm, v_w_in, v_q_norm, v_k_norm, v_attn_sinks, v_rel_bias, v_pool_w, v_pool_scale, v_w_out, v_ffn2_norm, v_ffn2_w_gate, v_ffn2_w_up, v_ffn2_w_down):
    args = dict(locals())
    w = {n: args[n] for n in WEIGHTS}
    m = {n: args["m_" + n] for n in WEIGHTS}
    v = {n: args["v_" + n] for n in WEIGHTS}

    as_rows = lambda a, tr: jnp.swapaxes(a, 1, 2) if tr else a
    shard = jnp.concatenate([as_rows(w[n], tr)[0].astype(bf16) for n, tr in BIG], axis=0)
    exchanges = _GatheredWeights(shard, x[0], ffn1_norm)
    gx, (dw1, _, _, _), small = _local_step(
        x[0], loss_target[0], exchanges, ffn1_norm, mix_norm, ffn2_norm, q_norm, k_norm, attn_sinks,
        rel_bias, pool_w[0], pool_scale)

    nrows1 = len(G1_PIECES) * FS
    for_x, for_y, own1, small_tot = _reduce_scatter_ffn1_head(dw1, _pack_small(small))
    ssem, rsem, for_x, for_y, from_x, from_y, small_tot = _rs1_tail_start(
        for_x, for_y, lax.empty((nrows1, D), bf16), lax.empty((nrows1, D), bf16), small_tot)
    own_rest, landed_rest = exchanges.mix_ffn2_grads_parts(small_tot)

    grads, deltas, new_m, new_v = {}, {}, {}, {}
    rest = [k for k in range(len(BIG)) if k not in G1_PIECES]
    rows_of = lambda t, ks: [as_rows(t[BIG[k][0]], BIG[k][1]) for k in ks]
    rest_out = _adamw_big(rest, rows_of(w, rest), rows_of(m, rest), rows_of(v, rest), own_rest, [landed_rest],
                          "adamw_rest")
    from_x, from_y = _rs1_tail_wait(ssem, rsem, for_x, for_y, from_x, from_y, rest_out[0][0])
    ffn1 = list(G1_PIECES)
    ffn1_out = _adamw_big(ffn1, rows_of(w, ffn1), rows_of(m, ffn1), rows_of(v, ffn1), own1, [from_x, from_y],
                          "adamw_ffn1")
    for ks, out in ((rest, rest_out), (ffn1, ffn1_out)):
        for i, k in enumerate(ks):
            n, tr = BIG[k]
            grads[n], deltas[n], new_m[n], new_v[n] = [as_rows(o[i], tr) for o in out]
    small_names = [n for n in SMALL_NAMES if n != "loss"]
    for n in small_names:
        grads[n] = _unpack_small(small_tot, n)
    ds, nms, nvs = _adamw_small([w[n] for n in small_names], [m[n] for n in small_names], [v[n] for n in small_names],
                                [grads[n] for n in small_names], "adamw_small")
    for i, n in enumerate(small_names):
        deltas[n], new_m[n], new_v[n] = ds[i], nms[i], nvs[i]
    loss = small_tot[LOSS_ROW, 0]
    return (loss, gx[None], *[grads[n] for n in WEIGHTS], *[deltas[n] for n in WEIGHTS],
            *[new_m[n] for n in WEIGHTS], *[new_v[n] for n in WEIGHTS])
```

```python
import jax
import jax.numpy as jnp
import numpy as np
from jax import lax
from jax.experimental import pallas as pl
from jax.experimental.pallas import tpu as pltpu

f32, bf16, i32 = jnp.float32, jnp.bfloat16, jnp.int32
SDS = jax.ShapeDtypeStruct

D = 1024
F = 2816
HD = 64
NH = 8
NKV = 2
GQA = NH // NKV
DATTN = NH * HD
DKV = NKV * HD
DPOOL = 512
POOL_WINDOWS = (2, 4, 8, 16)
PGD = DPOOL // len(POOL_WINDOWS)
DIN = DATTN + 2 * DKV + DPOOL
DMIX = DATTN + DPOOL
BLK = 128
NBUCK = 32
MAX_DISTANCE = 128
EPS = 1e-6
NEG = -1e30
SCALE = HD ** -0.5

ADAM_LR, ADAM_B1, ADAM_B2, ADAM_EPS, ADAM_WD, ADAM_STEP = 0.001, 0.9, 0.999, 1e-08, 0.01, 10

NDEV = 8
FS = F // NDEV
INS = DIN // NDEV
OUTS = DMIX // NDEV
PIECE_ROWS = (FS, FS, FS, INS, OUTS, FS, FS, FS)
PIECE_OFF = tuple(int(v) for v in np.cumsum((0,) + PIECE_ROWS[:-1]))
PACK_ROWS = sum(PIECE_ROWS)

VMEM_LIMIT_V7X = 56 * 1024 * 1024

MESH = pl.DeviceIdType.MESH


def _cparams(sem=None, vmem=None):
    return pltpu.CompilerParams(dimension_semantics=sem, vmem_limit_bytes=vmem)


def _nt(a, b):
    return lax.dot_general(a, b, (((1,), (1,)), ((), ())), preferred_element_type=f32)


def _tn(a, b):
    return lax.dot_general(a, b, (((0,), (0,)), ((), ())), preferred_element_type=f32)


def _nn(a, b):
    return jnp.dot(a, b, preferred_element_type=f32)


def _sigmoid(x):
    return 1.0 / (1.0 + jnp.exp(-x))


def _norm_fwd(x, g, name):
    T = x.shape[0]
    tm = min(512, T)

    def body(x_ref, g_ref, h_ref):
        xv = x_ref[...]
        r = lax.rsqrt(jnp.mean(xv * xv, axis=-1, keepdims=True) + EPS)
        h_ref[...] = (xv * r * g_ref[...]).astype(bf16)

    return pl.pallas_call(
        body, grid=(T // tm,),
        in_specs=[pl.BlockSpec((tm, D), lambda i: (i, 0)), pl.BlockSpec((1, D), lambda i: (0, 0))],
        out_specs=pl.BlockSpec((tm, D), lambda i: (i, 0)),
        out_shape=SDS((T, D), bf16), name=name)(x, g)


FFN_ROW_CHUNK = 256


def _ffn_tiles(T):
    return min(1024, T), 256


def _ffn_fwd(h, w, x, target, next_gain, name):
    T = h.shape[0]
    tm, tf = _ffn_tiles(T)
    nf = F // tf
    with_loss = target is not None
    assert with_loss != (next_gain is not None)

    def body(*refs):
        if with_loss:
            h_ref, w_ref, x_hbm, t_hbm, xo_ref, g_ref, u_ref, dyb_ref, loss_ref, tbuf, sem = refs
        else:
            h_ref, w_ref, x_hbm, gain_ref, xo_ref, g_ref, u_ref, hn_ref, sem = refs
        fi = pl.program_id(0)

        @pl.when(fi == 0)
        def _():
            cp = pltpu.make_async_copy(x_hbm, xo_ref, sem)
            cp.start()
            cp.wait()

        wgu = w_ref[0:2].reshape(2 * tf, D)
        for r in range(0, T, tm):
            rows = slice(r, r + tm)
            gu = _nt(h_ref[rows, :], wgu)
            gate, up = gu[:, :tf], gu[:, tf:]
            act = gate * _sigmoid(gate) * up
            g_ref[0, rows, :] = gate.astype(bf16)
            u_ref[0, rows, :] = up.astype(bf16)
            xo_ref[rows, :] += _nn((0.5 * act).astype(bf16), w_ref[2])

        if with_loss:
            @pl.when(fi == nf - 1)
            def _():
                lanes = jnp.zeros((1, 128), f32)
                for r in range(0, T, tm):
                    rows = slice(r, r + tm)
                    cp = pltpu.make_async_copy(t_hbm.at[pl.ds(r, tm), :], tbuf, sem)
                    cp.start()
                    cp.wait()
                    e = xo_ref[rows, :] - tbuf[...]
                    dy = e * (1.0 / D)
                    xo_ref[rows, :] = dy
                    dyb_ref[rows, :] = (0.5 * dy).astype(bf16)
                    col = jnp.sum(e * e, axis=0, keepdims=True) * (0.5 / D)
                    for k in range(D // 128):
                        lanes = lanes + col[:, 128 * k:128 * (k + 1)]
                loss_ref[...] = lanes
        else:
            @pl.when(fi == nf - 1)
            def _():
                for r in range(0, T, FFN_ROW_CHUNK):
                    rows = slice(r, r + FFN_ROW_CHUNK)
                    xv = xo_ref[rows, :]
                    rstd = lax.rsqrt(jnp.mean(xv * xv, axis=-1, keepdims=True) + EPS)
                    hn_ref[rows, :] = (xv * rstd * gain_ref[...]).astype(bf16)

    tok = pl.BlockSpec((T, D), lambda f: (0, 0))
    act_spec = pl.BlockSpec((1, T, tf), lambda f: (f, 0, 0))
    hbm = pl.BlockSpec(memory_space=pl.ANY)
    in_specs = [tok, pl.BlockSpec((3, tf, D), lambda f: (0, f, 0)), hbm]
    out_specs = [tok, act_spec, act_spec]
    out_shape = [SDS((T, D), f32), SDS((nf, T, tf), bf16), SDS((nf, T, tf), bf16)]
    scratch = [pltpu.SemaphoreType.DMA]
    args = [h, w, x]
    if with_loss:
        in_specs.append(hbm)
        args.append(target)
        out_specs += [tok, pl.BlockSpec((1, 128), lambda f: (0, 0))]
        out_shape += [SDS((T, D), bf16), SDS((1, 128), f32)]
        scratch = [pltpu.VMEM((tm, D), f32)] + scratch
    else:
        in_specs.append(pl.BlockSpec((1, D), lambda f: (0, 0)))
        args.append(next_gain)
        out_specs.append(tok)
        out_shape.append(SDS((T, D), bf16))
    return pl.pallas_call(
        body, grid=(nf,), in_specs=in_specs, out_specs=out_specs, out_shape=tuple(out_shape), scratch_shapes=scratch,
        compiler_params=_cparams(("arbitrary",), VMEM_LIMIT_V7X), name=name)(*args)


def _ffn_bwd(dob, h, gate, up, w, norm, name):
    x, g, dres = norm
    T = h.shape[0]
    _, tf = _ffn_tiles(T)
    nf = F // tf
    tm = min(512, T)
    nchunk = T // tm

    def body(do_hbm, h_hbm, g_ref, u_ref, w_ref, x_hbm, gain_ref, dr_hbm, dx_hbm, dxb_hbm, dg_ref, dw_ref,
             do_v, h_v, dh_acc, dgu_s, act_s, xbuf, rbuf, obuf, obb, sems, in_sems, out_sems):
        fi = pl.program_id(0)

        @pl.when(fi == 0)
        def _():
            loads = [pltpu.make_async_copy(do_hbm, do_v, sems.at[0]), pltpu.make_async_copy(h_hbm, h_v, sems.at[1])]
            for cp in loads:
                cp.start()
            dh_acc[...] = jnp.zeros_like(dh_acc)
            for cp in loads:
                cp.wait()

        wgu = w_ref[0:2].reshape(2 * tf, D)
        for r in range(0, T, FFN_ROW_CHUNK):
            rows = slice(r, r + FFN_ROW_CHUNK)
            dov = do_v[rows, :]
            gv = g_ref[0, rows, :].astype(f32)
            uv = u_ref[0, rows, :].astype(f32)
            sg = _sigmoid(gv)
            sil = gv * sg
            dact = _nt(dov, w_ref[2])
            dup = dact * sil
            dgate = dact * uv * (sg * (1.0 + gv * (1.0 - sg)))
            dgu = jnp.concatenate([dgate.astype(bf16), dup.astype(bf16)], axis=1)
            dgu_s[rows, :] = dgu
            act_s[rows, :] = (sil * uv).astype(bf16)
            dh_acc[rows, :] += _nn(dgu, wgu)
        dw_ref[0:2] = _tn(dgu_s[...], h_v[...]).reshape(2, tf, D).astype(bf16)
        dw_ref[2] = _tn(act_s[...], do_v[...]).astype(bf16)

        @pl.when(fi == nf - 1)
        def _():
            def loads(i):
                s, rows = i % 2, pl.ds(i * tm, tm)
                return [pltpu.make_async_copy(x_hbm.at[rows, :], xbuf.at[s], in_sems.at[2 * s]),
                        pltpu.make_async_copy(dr_hbm.at[rows, :], rbuf.at[s], in_sems.at[2 * s + 1])]

            def stores(i):
                s, rows = i % 2, pl.ds(i * tm, tm)
                return [pltpu.make_async_copy(obuf.at[s], dx_hbm.at[rows, :], out_sems.at[2 * s]),
                        pltpu.make_async_copy(obb.at[s], dxb_hbm.at[rows, :], out_sems.at[2 * s + 1])]

            for cp in loads(0):
                cp.start()
            dg = jnp.zeros((1, D), f32)
            for i in range(nchunk):
                s = i % 2
                if i + 1 < nchunk:
                    for cp in loads(i + 1):
                        cp.start()
                for cp in loads(i):
                    cp.wait()
                if i >= 2:
                    for cp in stores(i - 2):
                        cp.wait()
                xv = xbuf[s]
                rstd = lax.rsqrt(jnp.mean(xv * xv, axis=-1, keepdims=True) + EPS)
                xh = xv * rstd
                dhv = dh_acc[i * tm:(i + 1) * tm, :]
                dxh = dhv * gain_ref[...]
                dx = rbuf[s] + rstd * (dxh - xh * jnp.mean(dxh * xh, axis=-1, keepdims=True))
                obuf[s] = dx
                obb[s] = dx.astype(bf16)
                dg = dg + jnp.sum(dhv * xh, axis=0, keepdims=True)
                for cp in stores(i):
                    cp.start()
            dg_ref[...] = dg
            for i in range(max(nchunk - 2, 0), nchunk):
                for cp in stores(i):
                    cp.wait()

    act_spec = pl.BlockSpec((1, T, tf), lambda f: (f, 0, 0))
    wspec = pl.BlockSpec((3, tf, D), lambda f: (0, f, 0))
    vec = pl.BlockSpec((1, D), lambda f: (0, 0))
    hbm = pl.BlockSpec(memory_space=pl.ANY)
    return pl.pallas_call(
        body, grid=(nf,),
        in_specs=[hbm, hbm, act_spec, act_spec, wspec, hbm, vec, hbm],
        out_specs=[hbm, hbm, vec, wspec],
        out_shape=(SDS((T, D), f32), SDS((T, D), bf16), SDS((1, D), f32), SDS((3, F, D), bf16)),
        scratch_shapes=[pltpu.VMEM((T, D), bf16), pltpu.VMEM((T, D), bf16), pltpu.VMEM((T, D), f32),
                        pltpu.VMEM((T, 2 * tf), bf16), pltpu.VMEM((T, tf), bf16),
                        pltpu.VMEM((2, tm, D), f32), pltpu.VMEM((2, tm, D), f32), pltpu.VMEM((2, tm, D), f32),
                        pltpu.VMEM((2, tm, D), bf16),
                        pltpu.SemaphoreType.DMA((2,)), pltpu.SemaphoreType.DMA((4,)), pltpu.SemaphoreType.DMA((4,))],
        compiler_params=_cparams(("arbitrary",), VMEM_LIMIT_V7X), name=name)(dob, h, gate, up, w, x, g, dres)


def _in_proj_fwd(h, wint, name):
    T = h.shape[0]
    tm = min(512, T)

    def body(h_ref, w_ref, z_ref):
        z_ref[...] = _nt(h_ref[...], w_ref[...])

    return pl.pallas_call(
        body, grid=(T // tm,),
        in_specs=[pl.BlockSpec((tm, D), lambda i: (i, 0)), pl.BlockSpec((DIN, D), lambda i: (0, 0))],
        out_specs=pl.BlockSpec((tm, DIN), lambda i: (i, 0)),
        out_shape=SDS((T, DIN), f32), name=name)(h, wint)


def _in_proj_bwd(dz, wint, h, norm, out_scale, name):
    x, g, dres = norm
    T = h.shape[0]
    tm = min(512, T)
    nt = T // tm

    def body(dz_ref, w_ref, h_ref, x_ref, g_ref, dr_ref, dx_ref, dxb_ref, dg_ref, dw_ref, acc):
        i = pl.program_id(0)
        dzb = dz_ref[...].astype(bf16)
        dhv = _nn(dzb, w_ref[...])
        part = _tn(dzb, h_ref[...])
        xv = x_ref[...]
        rstd = lax.rsqrt(jnp.mean(xv * xv, axis=-1, keepdims=True) + EPS)
        xh = xv * rstd
        dxh = dhv * g_ref[...]
        dx = dr_ref[...] + rstd * (dxh - xh * jnp.mean(dxh * xh, axis=-1, keepdims=True))
        dx_ref[...] = dx
        dxb_ref[...] = (out_scale * dx).astype(bf16)
        dg = jnp.sum(dhv * xh, axis=0, keepdims=True)

        @pl.when(i == 0)
        def _():
            acc[...] = part
            dg_ref[...] = dg

        @pl.when(i > 0)
        def _():
            acc[...] += part
            dg_ref[...] += dg

        @pl.when(i == nt - 1)
        def _():
            dw_ref[...] = acc[...].astype(bf16)

    wspec = pl.BlockSpec((DIN, D), lambda i: (0, 0))
    tok = pl.BlockSpec((tm, D), lambda i: (i, 0))
    vec = pl.BlockSpec((1, D), lambda i: (0, 0))
    return pl.pallas_call(
        body, grid=(nt,),
        in_specs=[pl.BlockSpec((tm, DIN), lambda i: (i, 0)), wspec, tok, tok, vec, tok],
        out_specs=[tok, tok, vec, wspec],
        out_shape=(SDS((T, D), f32), SDS((T, D), bf16), SDS((1, D), f32), SDS((DIN, D), bf16)),
        scratch_shapes=[pltpu.VMEM((DIN, D), f32)],
        compiler_params=_cparams(("arbitrary",)), name=name)(dz, wint, h, x, g, dres)


def _out_proj_fwd(ymix, wout, x, g, name):
    T = x.shape[0]
    tm = min(512, T)

    def body(y_ref, w_ref, x_ref, g_ref, o_ref, h_ref):
        o = x_ref[...] + _nn(y_ref[...], w_ref[...])
        o_ref[...] = o
        r = lax.rsqrt(jnp.mean(o * o, axis=-1, keepdims=True) + EPS)
        h_ref[...] = (o * r * g_ref[...]).astype(bf16)

    tok = pl.BlockSpec((tm, D), lambda i: (i, 0))
    return pl.pallas_call(
        body, grid=(T // tm,),
        in_specs=[pl.BlockSpec((tm, DMIX), lambda i: (i, 0)), pl.BlockSpec((DMIX, D), lambda i: (0, 0)), tok,
                  pl.BlockSpec((1, D), lambda i: (0, 0))],
        out_specs=[tok, tok], out_shape=(SDS((T, D), f32), SDS((T, D), bf16)), name=name)(ymix, wout, x, g)


def _out_proj_bwd(dxb, wout, ymix, name):
    T = dxb.shape[0]
    tm = min(512, T)
    nt = T // tm

    def body(dx_ref, w_ref, y_ref, dy_ref, dw_ref, acc):
        i = pl.program_id(0)
        dxv = dx_ref[...]
        dy_ref[...] = _nt(dxv, w_ref[...])
        part = _tn(y_ref[...], dxv)

        @pl.when(i == 0)
        def _():
            acc[...] = part

        @pl.when(i > 0)
        def _():
            acc[...] += part

        @pl.when(i == nt - 1)
        def _():
            dw_ref[...] = acc[...].astype(bf16)

    wspec = pl.BlockSpec((DMIX, D), lambda i: (0, 0))
    return pl.pallas_call(
        body, grid=(nt,),
        in_specs=[pl.BlockSpec((tm, D), lambda i: (i, 0)), wspec, pl.BlockSpec((tm, DMIX), lambda i: (i, 0))],
        out_specs=[pl.BlockSpec((tm, DMIX), lambda i: (i, 0)), wspec],
        out_shape=(SDS((T, DMIX), f32), SDS((DMIX, D), bf16)),
        scratch_shapes=[pltpu.VMEM((DMIX, D), f32)],
        compiler_params=_cparams(("arbitrary",)), name=name)(dxb, wout, ymix)


def _t5_bucket_table():
    ql = np.arange(BLK)[:, None]
    kl = np.arange(2 * BLK)[None, :]
    n = np.maximum(ql + BLK - kl, 0)
    max_exact = NBUCK // 2
    large = max_exact + (np.log(np.maximum(n, 1) / max_exact) / np.log(MAX_DISTANCE / max_exact)
                         * (NBUCK - max_exact)).astype(np.int32)
    large = np.minimum(large, NBUCK - 1)
    return np.where(n < max_exact, n, large).astype(np.int32)


def _fill_bias(bk_ref, rb_ref, bias_scr):
    bk = bk_ref[...]
    for h in range(NH):
        def step(b, acc, h=h):
            return acc + jnp.where(bk == b, rb_ref[b, h], 0.0)
        bias_scr[h] = lax.fori_loop(0, NBUCK, step, jnp.zeros((BLK, 2 * BLK), f32))


MIX_SUB = 4


class _Window:
    def __init__(self, zc_ref, zp_ref, n, s):
        self.blk = n * MIX_SUB + s
        self.cur = lambda a, b: zc_ref[s * BLK:(s + 1) * BLK, a:b]
        self.prev = (lambda a, b: zp_ref[:, a:b]) if s == 0 else (lambda a, b: zc_ref[(s - 1) * BLK:s * BLK, a:b])


def _attn_qkv(win, kh, qg, kg):
    kc = DATTN + HD * kh
    vc = DATTN + DKV + HD * kh
    kx = jnp.concatenate([win.prev(kc, kc + HD), win.cur(kc, kc + HD)], axis=0)
    vx = jnp.concatenate([win.prev(vc, vc + HD), win.cur(vc, vc + HD)], axis=0)
    qx = jnp.concatenate([win.cur(HD * (GQA * kh + g), HD * (GQA * kh + g + 1)) for g in range(GQA)], axis=0)
    rq = lax.rsqrt(jnp.mean(qx * qx, axis=-1, keepdims=True) + EPS)
    rk = lax.rsqrt(jnp.mean(kx * kx, axis=-1, keepdims=True) + EPS)
    qhat, khat = qx * rq, kx * rk
    return dict(qhat=qhat, khat=khat, rq=rq, rk=rk, qsb=(qhat * (qg * SCALE)).astype(bf16),
                knb=(khat * kg).astype(bf16), vb=vx.astype(bf16))


def _window_masks(n):
    row = lax.broadcasted_iota(i32, (GQA * BLK, 2 * BLK), 0) & (BLK - 1)
    col = lax.broadcasted_iota(i32, (GQA * BLK, 2 * BLK), 1)
    band = (col > row) & (col <= row + BLK)
    return band & ((col >= BLK) | (n > 0)), band


def _attn_probs(a, kh, sk_ref, bias_scr, mask):
    s = _nt(a["qsb"], a["knb"]) + bias_scr[GQA * kh:GQA * (kh + 1)].reshape(GQA * BLK, 2 * BLK)
    s = jnp.where(mask, s, NEG)
    ridx = lax.broadcasted_iota(i32, (GQA * BLK, 1), 0)
    sink = jnp.full((GQA * BLK, 1), sk_ref[GQA * kh + GQA - 1], f32)
    for g in range(GQA - 2, -1, -1):
        sink = jnp.where(ridx < (g + 1) * BLK, sk_ref[GQA * kh + g], sink)
    m = jnp.maximum(jnp.max(s, axis=-1, keepdims=True), sink)
    e = jnp.exp(s - m)
    den = jnp.sum(e, axis=-1, keepdims=True) + jnp.exp(sink - m)
    return e / den


POOL_STEPS = {2: (1,), 4: (1, 2), 8: (1, 2, 4), 16: (1, 2, 4, 8)}


def _pool_group(win, g, w):
    n = win.blk
    c0 = DATTN + 2 * DKV + PGD * g
    uc = win.cur(c0, c0 + PGD)
    up = jnp.where(n > 0, win.prev(c0, c0 + PGD), 0.0)
    sm = jnp.concatenate([up, uc], axis=0)
    for k in POOL_STEPS[w]:
        sm = sm + pltpu.roll(sm, k, axis=0)
    pos = n * BLK + lax.broadcasted_iota(i32, (BLK, 1), 0) + 1
    cnt = jnp.minimum(pos, w).astype(f32)
    return sm[BLK:2 * BLK] / cnt - uc, cnt


def _mix_fwd(z, qg, kg, sinks, relb, bucket, pool_w, pscale, name):
    T = z.shape[0]
    step_rows = MIX_SUB * BLK
    nsteps = T // step_rows

    def body(zc_ref, zp_ref, qg_ref, kg_ref, sk_ref, rb_ref, bk_ref, pw_ref, ps_ref, y_ref, p_ref, bias_scr, yacc):
        n = pl.program_id(0)

        @pl.when(n == 0)
        def _():
            _fill_bias(bk_ref, rb_ref, bias_scr)

        first_mask, mask = _window_masks(n)
        for s in range(MIX_SUB):
            win = _Window(zc_ref, zp_ref, n, s)
            rows = slice(s * BLK, (s + 1) * BLK)
            for kh in range(NKV):
                a = _attn_qkv(win, kh, qg_ref[...], kg_ref[...])
                pb = _attn_probs(a, kh, sk_ref, bias_scr, first_mask if s == 0 else mask).astype(bf16)
                p_ref[s, GQA * kh:GQA * (kh + 1)] = pb.reshape(GQA, BLK, 2 * BLK)
                o = _nn(pb, a["vb"])
                for g in range(GQA):
                    hc = HD * (GQA * kh + g)
                    yacc[rows, hc:hc + HD] = o[g * BLK:(g + 1) * BLK]
            for g, w in enumerate(POOL_WINDOWS):
                pooled, _ = _pool_group(win, g, w)
                yp = _nn(pooled.astype(bf16), pw_ref[g].astype(bf16)) * ps_ref[:, PGD * g:PGD * (g + 1)]
                yacc[rows, DATTN + PGD * g:DATTN + PGD * (g + 1)] = yp
        y_ref[...] = yacc[...].astype(bf16)

    full = lambda *shape: pl.BlockSpec(shape, lambda n: (0,) * len(shape))
    smem = pl.BlockSpec(memory_space=pltpu.SMEM)
    return pl.pallas_call(
        body, grid=(nsteps,),
        in_specs=[pl.BlockSpec((step_rows, DIN), lambda n: (n, 0)),
                  pl.BlockSpec((BLK, DIN), lambda n: (jnp.maximum(n * MIX_SUB - 1, 0), 0)),
                  full(1, HD), full(1, HD), smem, smem, full(BLK, 2 * BLK),
                  full(len(POOL_WINDOWS), PGD, PGD), full(1, DPOOL)],
        out_specs=[pl.BlockSpec((step_rows, DMIX), lambda n: (n, 0)),
                   pl.BlockSpec((MIX_SUB, NH, BLK, 2 * BLK), lambda n: (n, 0, 0, 0))],
        out_shape=(SDS((T, DMIX), bf16), SDS((T // BLK, NH, BLK, 2 * BLK), bf16)),
        scratch_shapes=[pltpu.VMEM((NH, BLK, 2 * BLK), f32), pltpu.VMEM((step_rows, DMIX), f32)],
        compiler_params=_cparams(("arbitrary",)), name=name)(z, z, qg, kg, sinks, relb, bucket, pool_w, pscale)


def _mix_bwd(z, dy, probs, qg, kg, relb, bucket, pool_w, pscale, name):
    T = z.shape[0]
    step_rows = MIX_SUB * BLK
    nsteps = T // step_rows

    def body(zc_ref, zp_ref, dy_ref, p_ref, qg_ref, kg_ref, bk_ref, pw_ref, ps_ref,
             dz_ref, dqg_ref, dkg_ref, dsk_ref, drb_ref, dpw_ref, dps_ref, dbias_scr):
        n = pl.program_id(0)

        @pl.when(n == 0)
        def _():
            dbias_scr[...] = jnp.zeros_like(dbias_scr)
            dqg_ref[...] = jnp.zeros_like(dqg_ref)
            dkg_ref[...] = jnp.zeros_like(dkg_ref)
            dpw_ref[...] = jnp.zeros_like(dpw_ref)
            dps_ref[...] = jnp.zeros_like(dps_ref)

        qg, kg = qg_ref[...], kg_ref[...]
        for s in range(MIX_SUB):
            win = _Window(zc_ref, zp_ref, n, s)
            blk = win.blk
            rows = pl.ds(pl.multiple_of(blk * BLK, BLK), BLK)
            prow = pl.ds(pl.multiple_of(jnp.maximum(blk - 1, 0) * BLK, BLK), BLK)
            dyr = slice(s * BLK, (s + 1) * BLK)

            def into_prev(fn, s=s):
                if s == 0:
                    pl.when(n > 0)(fn)
                else:
                    fn()

            for kh in range(NKV):
                a = _attn_qkv(win, kh, qg, kg)
                pb = p_ref[s, GQA * kh:GQA * (kh + 1)].reshape(GQA * BLK, 2 * BLK)
                p = pb.astype(f32)
                do = jnp.concatenate([dy_ref[dyr, HD * (GQA * kh + g):HD * (GQA * kh + g + 1)] for g in range(GQA)],
                                     axis=0).astype(bf16)
                dv = _tn(pb, do)
                dp = _nt(do, a["vb"])
                delta = jnp.sum(p * dp, axis=-1, keepdims=True)
                ds = p * (dp - delta)
                for g in range(GQA):
                    dbias_scr[GQA * kh + g] += ds[g * BLK:(g + 1) * BLK]
                dsb = ds.astype(bf16)
                dqn = _nn(dsb, a["knb"]) * SCALE
                dkn = _tn(dsb, a["qsb"])
                qhat, khat = a["qhat"], a["khat"]
                dqg_ref[...] += jnp.sum(dqn * qhat, axis=0, keepdims=True)
                dkg_ref[...] += jnp.sum(dkn * khat, axis=0, keepdims=True)
                dqh = dqn * qg
                dq = a["rq"] * (dqh - qhat * jnp.mean(dqh * qhat, axis=-1, keepdims=True))
                dkh = dkn * kg
                dk = a["rk"] * (dkh - khat * jnp.mean(dkh * khat, axis=-1, keepdims=True))
                kc = DATTN + HD * kh
                vc = DATTN + DKV + HD * kh
                for g in range(GQA):
                    hc = HD * (GQA * kh + g)
                    dz_ref[rows, hc:hc + HD] = dq[g * BLK:(g + 1) * BLK]
                dz_ref[rows, kc:kc + HD] = dk[BLK:2 * BLK]
                dz_ref[rows, vc:vc + HD] = dv[BLK:2 * BLK]

                def kv_prev(dk=dk, dv=dv, kc=kc, vc=vc, prow=prow):
                    dz_ref[prow, kc:kc + HD] += dk[0:BLK]
                    dz_ref[prow, vc:vc + HD] += dv[0:BLK]

                into_prev(kv_prev)

            for g, w in enumerate(POOL_WINDOWS):
                c0 = DATTN + 2 * DKV + PGD * g
                pooled, cnt = _pool_group(win, g, w)
                pb = pooled.astype(bf16)
                wb = pw_ref[g].astype(bf16)
                dyp = dy_ref[dyr, DATTN + PGD * g:DATTN + PGD * (g + 1)]
                ypre = _nn(pb, wb)
                dps_ref[:, PGD * g:PGD * (g + 1)] += jnp.sum(dyp * ypre, axis=0, keepdims=True)
                dyg = (dyp * ps_ref[:, PGD * g:PGD * (g + 1)]).astype(bf16)
                dpw_ref[g] += _tn(pb, dyg)
                dpooled = _nt(dyg, wb)
                due = jnp.concatenate([jnp.zeros((BLK, PGD), f32), dpooled / cnt], axis=0)
                for k in POOL_STEPS[w]:
                    due = due + pltpu.roll(due, 2 * BLK - k, axis=0)
                dz_ref[rows, c0:c0 + PGD] = due[BLK:2 * BLK] - dpooled

                def pool_prev(due=due, c0=c0, prow=prow):
                    dz_ref[prow, c0:c0 + PGD] += due[0:BLK]

                into_prev(pool_prev)

        @pl.when(n == nsteps - 1)
        def _():
            bk = bk_ref[...]
            ri = lax.broadcasted_iota(i32, (NBUCK, NH), 0)
            ci = lax.broadcasted_iota(i32, (NBUCK, NH), 1)

            def step(b, acc):
                for h in range(NH):
                    sel = jnp.where(bk == b, dbias_scr[h], 0.0)
                    tot = jnp.sum(jnp.sum(sel, axis=1, keepdims=True), axis=0, keepdims=True)
                    acc = acc + jnp.where((ri == b) & (ci == h), tot, 0.0)
                return acc

            drb_ref[...] = lax.fori_loop(0, NBUCK, step, jnp.zeros((NBUCK, NH), f32))
            lane = lax.broadcasted_iota(i32, (1, 128), 1)
            dsk = jnp.zeros((1, 128), f32)
            for h in range(NH):
                tot = jnp.sum(jnp.sum(dbias_scr[h], axis=1, keepdims=True), axis=0, keepdims=True)
                dsk = dsk - jnp.where(lane == h, tot, 0.0)
            dsk_ref[...] = dsk

    full = lambda *shape: pl.BlockSpec(shape, lambda n: (0,) * len(shape))
    npg = len(POOL_WINDOWS)
    return pl.pallas_call(
        body, grid=(nsteps,),
        in_specs=[pl.BlockSpec((step_rows, DIN), lambda n: (n, 0)),
                  pl.BlockSpec((BLK, DIN), lambda n: (jnp.maximum(n * MIX_SUB - 1, 0), 0)),
                  pl.BlockSpec((step_rows, DMIX), lambda n: (n, 0)),
                  pl.BlockSpec((MIX_SUB, NH, BLK, 2 * BLK), lambda n: (n, 0, 0, 0)),
                  full(1, HD), full(1, HD), full(BLK, 2 * BLK), full(npg, PGD, PGD), full(1, DPOOL)],
        out_specs=[full(T, DIN), full(1, HD), full(1, HD), full(1, 128), full(NBUCK, NH),
                   full(npg, PGD, PGD), full(1, DPOOL)],
        out_shape=(SDS((T, DIN), f32), SDS((1, HD), f32), SDS((1, HD), f32), SDS((1, 128), f32),
                   SDS((NBUCK, NH), f32), SDS((npg, PGD, PGD), f32), SDS((1, DPOOL), f32)),
        scratch_shapes=[pltpu.VMEM((NH, BLK, 2 * BLK), f32)],
        compiler_params=_cparams(("arbitrary",), VMEM_LIMIT_V7X),
        name=name)(z, z, dy, probs, qg, kg, bucket, pool_w, pscale)


class _LocalWeights:
    def __init__(self, w1, wint, wout, w2):
        self.w1, self.wint, self.wout, self.w2 = w1, wint, wout, w2

    def ffn1(self):
        return self.w1

    def first_norm(self, x, gain):
        return _norm_fwd(x, gain, "norm1_fwd")

    def after_ffn1(self, gain, x1):
        return gain

    def mix(self, after):
        return self.wint, self.wout

    def before_out_proj(self, wout, after):
        return wout

    def ffn2(self, after):
        return self.w2

    def out_ffn2_grads_ready(self, dwout, dw2, after):
        return after

    def before_ffn1_bwd(self, dwint, dx1b):
        return dx1b


def _local_step(x, target, weights, g1, gm, g3, qg, kg, sinks, relb, pool_w, pscale):
    bucket = jnp.asarray(_t5_bucket_table())
    sk = sinks.reshape(NH)
    w1 = weights.ffn1()
    h1 = weights.first_norm(x, g1)
    x1, gate1, up1, h2 = _ffn_fwd(h1, w1, x, None, gm, "ffn1_fwd")
    gm = weights.after_ffn1(gm, x1)
    wint, wout = weights.mix(h2)
    z = _in_proj_fwd(h2, wint, "in_proj_fwd")
    ymix, probs = _mix_fwd(z, qg, kg, sk, relb, bucket, pool_w, pscale, "mix_fwd")
    wout = weights.before_out_proj(wout, ymix)
    x2, h3 = _out_proj_fwd(ymix, wout, x1, g3, "out_proj_fwd")
    w2 = weights.ffn2(h3)
    dy, gate2, up2, dyb, loss_lanes = _ffn_fwd(h3, w2, x2, target, None, "ffn2_fwd")

    dx2, dx2b, dg3, dw2 = _ffn_bwd(dyb, h3, gate2, up2, w2, (x2, g3, dy), "ffn2_bwd")
    dymix, dwout = _out_proj_bwd(dx2b, wout, ymix, "out_proj_bwd")
    dymix = weights.out_ffn2_grads_ready(dwout, dw2, dymix)
    dz, dqg, dkg, dsk, drb, dpw, dps = _mix_bwd(z, dymix, probs, qg, kg, relb, bucket, pool_w, pscale, "mix_bwd")
    dx1, dx1b, dgm, dwint = _in_proj_bwd(dz, wint, h2, (x1, gm, dx2), 0.5, "in_proj_bwd")
    dx1b = weights.before_ffn1_bwd(dwint, dx1b)
    gx, _, dg1, dw1 = _ffn_bwd(dx1b, h1, gate1, up1, w1, (x, g1, dx1), "ffn1_bwd")
    small = dict(ffn1_norm=dg1, mix_norm=dgm, ffn2_norm=dg3, pool_scale=dps, q_norm=dqg, k_norm=dkg,
                 attn_sinks=dsk[:, :NH], rel_bias=drb, pool_w=dpw, loss=loss_lanes)
    return gx, (dw1, dwint, dwout, dw2), small


SMALL_NAMES = ("ffn1_norm", "mix_norm", "ffn2_norm", "pool_scale", "q_norm", "k_norm", "attn_sinks", "rel_bias",
               "pool_w", "loss")
SMALL_SHAPES = dict(ffn1_norm=(1, D), mix_norm=(1, D), ffn2_norm=(1, D), pool_scale=(1, DPOOL), q_norm=(1, HD),
                    k_norm=(1, HD), attn_sinks=(1, NH), rel_bias=(NBUCK, NH),
                    pool_w=(1, len(POOL_WINDOWS), PGD, PGD), loss=(1, 128))


def _small_rows(name):
    return -(-int(np.prod(SMALL_SHAPES[name])) // 128)


SMALL_OFF = {}
_r = 0
for _n in SMALL_NAMES:
    SMALL_OFF[_n] = _r
    _r += _small_rows(_n)
SMALL_ROWS = -(-_r // 8) * 8
LOSS_ROW = SMALL_OFF["loss"]


def _pack_small(vals):
    parts = []
    for n in SMALL_NAMES:
        size = _small_rows(n) * 128
        if n in vals:
            flat = vals[n].astype(f32).reshape(-1)
            parts.append(jnp.pad(flat, (0, size - flat.shape[0])))
        else:
            parts.append(jnp.zeros((size,), f32))
    flat = jnp.concatenate(parts)
    flat = jnp.pad(flat, (0, SMALL_ROWS * 128 - flat.shape[0]))
    return flat.reshape(SMALL_ROWS, 128)


def _unpack_small(packed, name):
    size = int(np.prod(SMALL_SHAPES[name]))
    r0 = SMALL_OFF[name]
    return packed[r0:r0 + _small_rows(name)].reshape(-1)[:size].reshape(SMALL_SHAPES[name])


def _position():
    return lax.axis_index("x"), lax.axis_index("y"), lax.axis_index("c")


def _dev_index(x, y, c):
    return 4 * x + 2 * y + c


G1_PIECES, MIX_PIECES, F2_PIECES = (0, 1, 2), (3, 4), (5, 6, 7)


def _group_rows(pieces):
    return sum(PIECE_ROWS[k] for k in pieces)


def _shard_piece(s_ref, k):
    return s_ref.at[pl.ds(PIECE_OFF[k], PIECE_ROWS[k]), :]


def _shard_group(s_ref, pieces):
    return s_ref.at[pl.ds(PIECE_OFF[pieces[0]], _group_rows(pieces)), :]


def _weight_pieces(w1_ref=None, wi_ref=None, wo_ref=None, w2_ref=None):
    arrs = {}
    if w1_ref is not None:
        arrs.update({0: w1_ref.at[0], 1: w1_ref.at[1], 2: w1_ref.at[2]})
    if wi_ref is not None:
        arrs[3] = wi_ref
    if wo_ref is not None:
        arrs[4] = wo_ref
    if w2_ref is not None:
        arrs.update({5: w2_ref.at[0], 6: w2_ref.at[1], 7: w2_ref.at[2]})
    return arrs


def _block_rows(arrs, k, dev):
    r = PIECE_ROWS[k]
    return arrs[k].at[pl.ds(pl.multiple_of(_dev_index(*dev) * r, 16), r), :]


NORM_ROWS = 512


def _all_gather_ffn1(shard, x, gain):
    pieces = G1_PIECES
    rest_pieces = MIX_PIECES + F2_PIECES
    half = FS // 2
    T = x.shape[0]
    SIB, X0, X1, Y0, Y1, RELAY_Y, RELAY_X, ON_X, ON_Y, ON_D0, ON_D1 = range(11)

    def body(s_ref, x_ref, g_ref, w1_ref, h_ref, wi_ref, wo_ref, w2_ref, xbuf, hbuf, rest_buf,
             send_sems, recv_sems, local_sem, norm_sems):
        x, y, c = _position()
        me, sib = (x, y, c), (x, y, 1 - c)
        xn, yn, dg = (1 - x, y, c), (x, 1 - y, c), (1 - x, 1 - y, c)
        arrs = _weight_pieces(w1_ref=w1_ref)

        def place_rest():
            rest = _weight_pieces(wi_ref=wi_ref, wo_ref=wo_ref, w2_ref=w2_ref)
            grp = _shard_group(s_ref, rest_pieces)
            load = pltpu.make_async_copy(grp, rest_buf, norm_sems.at[0])
            load.start()
            load.wait()
            base = PIECE_OFF[rest_pieces[0]]
            for k in rest_pieces:
                pltpu.make_async_copy(rest_buf.at[pl.ds(PIECE_OFF[k] - base, PIECE_ROWS[k]), :],
                                      _block_rows(rest, k, me), norm_sems.at[1]).start()
            pltpu.make_async_copy(grp, rest_buf, norm_sems.at[1]).wait()

        def first_norm():
            for r in range(0, T, NORM_ROWS):
                load = pltpu.make_async_copy(x_ref.at[pl.ds(r, NORM_ROWS), :], xbuf, norm_sems.at[0])
                load.start()
                load.wait()
                xv = xbuf[...]
                rs = lax.rsqrt(jnp.mean(xv * xv, axis=-1, keepdims=True) + EPS)
                hbuf[...] = (xv * rs * g_ref[...]).astype(bf16)
                store = pltpu.make_async_copy(hbuf, h_ref.at[pl.ds(r, NORM_ROWS), :], norm_sems.at[1])
                store.start()
                store.wait()

        def rows_of(k, block, hf):
            r = PIECE_ROWS[k]
            start, size = (0, r) if hf is None else (hf * half, half)
            return arrs[k].at[pl.ds(pl.multiple_of(_dev_index(*block) * r + start, 16), size), :]

        def copies(rel, block, hf, to, from_shard=False):
            def src(k):
                if not from_shard:
                    return rows_of(k, block, hf)
                start, size = (0, PIECE_ROWS[k]) if hf is None else (hf * half, half)
                return s_ref.at[pl.ds(PIECE_OFF[k] + start, size), :]
            return [pltpu.make_async_remote_copy(
                src_ref=src(k), dst_ref=rows_of(k, block, hf), send_sem=send_sems.at[rel], recv_sem=recv_sems.at[rel],
                device_id=to, device_id_type=MESH) for k in pieces]

        def waiter(rel, hf):
            nrows = len(pieces) * (FS if hf is None else half)
            grp = s_ref.at[pl.ds(0, nrows), :]
            return pltpu.make_async_remote_copy(src_ref=grp, dst_ref=grp, send_sem=send_sems.at[rel],
                                                recv_sem=recv_sems.at[rel], device_id=me, device_id_type=MESH)

        def start(cps):
            for cp in cps:
                cp.start()

        mine = [pltpu.make_async_copy(_shard_piece(s_ref, k), _block_rows(arrs, k, me), local_sem) for k in pieces]
        start(mine)
        start(copies(SIB, me, None, sib, True))
        start(copies(X0, me, 0, xn, True))
        start(copies(Y1, me, 1, yn, True))
        start(copies(X1, me, 1, xn, True))
        start(copies(Y0, me, 0, yn, True))
        first_norm()
        place_rest()
        waiter(X0, 0).wait_recv()
        start(copies(RELAY_Y, xn, 0, yn))
        waiter(Y1, 1).wait_recv()
        start(copies(RELAY_X, yn, 1, xn))
        waiter(X1, 1).wait_recv()
        start(copies(ON_X, xn, None, sib))
        waiter(Y0, 0).wait_recv()
        start(copies(ON_Y, yn, None, sib))
        waiter(RELAY_Y, 0).wait_recv()
        start(copies(ON_D0, dg, 0, sib))
        waiter(RELAY_X, 1).wait_recv()
        start(copies(ON_D1, dg, 1, sib))
        waiter(SIB, None).wait_recv()
        waiter(ON_X, None).wait_recv()
        waiter(ON_Y, None).wait_recv()
        waiter(ON_D0, 0).wait_recv()
        waiter(ON_D1, 1).wait_recv()
        for rel, hf in ((SIB, None), (X0, 0), (X1, 1), (Y0, 0), (Y1, 1), (RELAY_Y, 0), (RELAY_X, 1),
                        (ON_X, None), (ON_Y, None), (ON_D0, 0), (ON_D1, 1)):
            waiter(rel, hf).wait_send()
        grp = _shard_group(s_ref, pieces)
        pltpu.make_async_copy(grp, grp, local_sem).wait()

    hbm = pl.BlockSpec(memory_space=pl.ANY)
    return pl.pallas_call(
        body, in_specs=[hbm, hbm, pl.BlockSpec(memory_space=pltpu.VMEM)], out_specs=[hbm] * 5,
        out_shape=(SDS((3, F, D), bf16), SDS((T, D), bf16),
                   SDS((DIN, D), bf16), SDS((DMIX, D), bf16), SDS((3, F, D), bf16)),
        scratch_shapes=[pltpu.VMEM((NORM_ROWS, D), f32), pltpu.VMEM((NORM_ROWS, D), bf16),
                        pltpu.VMEM((_group_rows(rest_pieces), D), bf16),
                        pltpu.SemaphoreType.DMA((11,)), pltpu.SemaphoreType.DMA((11,)), pltpu.SemaphoreType.DMA,
                        pltpu.SemaphoreType.DMA((2,))],
        compiler_params=pltpu.CompilerParams(has_side_effects=True),
        name="all_gather_ffn1")(shard, x, gain)


HBM_SPEC = pl.BlockSpec(memory_space=pltpu.HBM)
SEM_SPEC = pl.BlockSpec(memory_space=pltpu.SEMAPHORE)
ANY_SPEC = pl.BlockSpec(memory_space=pl.ANY)
SPLIT_EFFECT = pltpu.SideEffectType.DATAFLOW_SIDE_EFFECTING


def _in_hbm(a):
    return pltpu.with_memory_space_constraint(a, pltpu.HBM)


def _hbm_like(a):
    return pltpu.HBM(a.shape, a.dtype)


def _gather_rest_start(shard, wi, wo, w2, w1):
    def body(s_ref, wi_ref, wo_ref, w2_ref, w1_ref,
             ssem_m, rsem_m0, rsem_m, ssem_f, rsem_f0, rsem_f, s_o, wi_o, wo_o, w2_o, w1_o):
        x, y, c = _position()
        me, sib = (x, y, c), (x, y, 1 - c)
        chips = [(1 - x, y), (x, 1 - y), (1 - x, 1 - y)]
        arrs = _weight_pieces(wi_ref=wi_ref, wo_ref=wo_ref, w2_ref=w2_ref)
        for pieces, ssem, rsem0, rsem in ((MIX_PIECES, ssem_m, rsem_m0, rsem_m), (F2_PIECES, ssem_f, rsem_f0, rsem_f)):
            for p in pieces:
                pltpu.make_async_remote_copy(
                    src_ref=_shard_piece(s_ref, p), dst_ref=_block_rows(arrs, p, me), send_sem=ssem.at[0],
                    recv_sem=rsem0, device_id=sib, device_id_type=MESH).start()
            for j, chip in enumerate(chips):
                for p in pieces:
                    pltpu.make_async_remote_copy(
                        src_ref=_shard_piece(s_ref, p), dst_ref=_block_rows(arrs, p, me), send_sem=ssem.at[1 + j],
                        recv_sem=rsem.at[j], device_id=(*chip, c), device_id_type=MESH).start()

    dma = pltpu.SemaphoreType.DMA
    return pl.pallas_call(
        body, name="gather_rest_start",
        out_shape=(dma((4,)), dma(()), dma((3,)), dma((4,)), dma(()), dma((3,)),
                   _hbm_like(shard), _hbm_like(wi), _hbm_like(wo), _hbm_like(w2), _hbm_like(w1)),
        in_specs=(HBM_SPEC,) * 5, out_specs=(SEM_SPEC,) * 6 + (HBM_SPEC,) * 5,
        input_output_aliases={0: 6, 1: 7, 2: 8, 3: 9, 4: 10},
        compiler_params=pltpu.CompilerParams(has_side_effects=SPLIT_EFFECT),
    )(_in_hbm(shard), _in_hbm(wi), _in_hbm(wo), _in_hbm(w2), _in_hbm(w1))


def _gather_mix_pass_on(rsem_m, wi, wo, thru, after):
    def body(wi_ref, wo_ref, thru_ref, rsem, after_ref, fsend, frecv, wi_o, wo_o, thru_o):
        x, y, c = _position()
        sib = (x, y, 1 - c)
        arrs = _weight_pieces(wi_ref=wi_ref, wo_ref=wo_ref)
        both = wi_ref.at[pl.ds(0, _group_rows(MIX_PIECES)), :]
        for j, chip in enumerate([(1 - x, y), (x, 1 - y), (1 - x, 1 - y)]):
            pltpu.make_async_remote_copy(src_ref=both, dst_ref=both, send_sem=fsend.at[j], recv_sem=rsem.at[j],
                                         device_id=(x, y, c), device_id_type=MESH).wait_recv()
            for p in MIX_PIECES:
                rows = _block_rows(arrs, p, (*chip, c))
                pltpu.make_async_remote_copy(src_ref=rows, dst_ref=rows, send_sem=fsend.at[j], recv_sem=frecv.at[j],
                                             device_id=sib, device_id_type=MESH).start()

    dma = pltpu.SemaphoreType.DMA
    return pl.pallas_call(
        body, name="gather_mix_pass_on",
        out_shape=(dma((3,)), dma((3,)), _hbm_like(wi), _hbm_like(wo), _hbm_like(thru)),
        in_specs=(HBM_SPEC, HBM_SPEC, HBM_SPEC, SEM_SPEC, ANY_SPEC), out_specs=(SEM_SPEC, SEM_SPEC) + (HBM_SPEC,) * 3,
        input_output_aliases={0: 2, 1: 3, 2: 4},
        compiler_params=pltpu.CompilerParams(has_side_effects=SPLIT_EFFECT),
    )(wi, wo, _in_hbm(thru), rsem_m, after)


def _gather_mix_wait(ssem_m, rsem_m0, fsend, frecv, shard, wi, wo, after):
    def body(s_ref, wi_ref, wo_ref, ssem, rsem0, fs, fr, after_ref, s_o, wi_o, wo_o):
        x, y, c = _position()
        grp = _shard_group(s_ref, MIX_PIECES)

        def waiter(send_sem, recv_sem):
            return pltpu.make_async_remote_copy(src_ref=grp, dst_ref=grp, send_sem=send_sem, recv_sem=recv_sem,
                                                device_id=(x, y, c), device_id_type=MESH)

        waiter(ssem.at[0], rsem0).wait_recv()
        for j in range(3):
            waiter(fs.at[j], fr.at[j]).wait_recv()
        for rel in range(4):
            waiter(ssem.at[rel], rsem0).wait_send()
        for j in range(3):
            waiter(fs.at[j], fr.at[j]).wait_send()

    return pl.pallas_call(
        body, name="gather_mix_wait", out_shape=(_hbm_like(shard), _hbm_like(wi), _hbm_like(wo)),
        in_specs=(HBM_SPEC,) * 3 + (SEM_SPEC,) * 4 + (ANY_SPEC,), out_specs=(HBM_SPEC,) * 3,
        input_output_aliases={0: 0, 1: 1, 2: 2},
        compiler_params=pltpu.CompilerParams(has_side_effects=SPLIT_EFFECT),
    )(shard, wi, wo, ssem_m, rsem_m0, fsend, frecv, after)


def _gather_ffn2_pass_on(rsem_f, w2, wo, after):
    def body(w2_ref, wo_ref, rsem, after_ref, fsend, frecv, w2_o, wo_o):
        x, y, c = _position()
        sib = (x, y, 1 - c)
        chips = [(1 - x, y), (x, 1 - y), (1 - x, 1 - y)]
        arrs = _weight_pieces(w2_ref=w2_ref)
        three = w2_ref.at[0, pl.ds(0, _group_rows(F2_PIECES)), :]
        for j, chip in enumerate(chips):
            pltpu.make_async_remote_copy(src_ref=three, dst_ref=three, send_sem=fsend.at[j], recv_sem=rsem.at[j],
                                         device_id=(x, y, c), device_id_type=MESH).wait_recv()
            for p in F2_PIECES:
                rows = _block_rows(arrs, p, (*chip, c))
                pltpu.make_async_remote_copy(src_ref=rows, dst_ref=rows, send_sem=fsend.at[j], recv_sem=frecv.at[j],
                                             device_id=sib, device_id_type=MESH).start()

    dma = pltpu.SemaphoreType.DMA
    return pl.pallas_call(
        body, name="gather_ffn2_pass_on", out_shape=(dma((3,)), dma((3,)), _hbm_like(w2), _hbm_like(wo)),
        in_specs=(HBM_SPEC, HBM_SPEC, SEM_SPEC, ANY_SPEC), out_specs=(SEM_SPEC, SEM_SPEC, HBM_SPEC, HBM_SPEC),
        input_output_aliases={0: 2, 1: 3},
        compiler_params=pltpu.CompilerParams(has_side_effects=SPLIT_EFFECT),
    )(w2, wo, rsem_f, after)


def _gather_ffn2_wait(ssem_f, rsem_f0, fsend, frecv, shard, w2, after):
    def body(s_ref, w2_ref, ssem, rsem0, fs, fr, after_ref, w2_o):
        x, y, c = _position()
        grp = _shard_group(s_ref, F2_PIECES)

        def waiter(send_sem, recv_sem):
            return pltpu.make_async_remote_copy(src_ref=grp, dst_ref=grp, send_sem=send_sem, recv_sem=recv_sem,
                                                device_id=(x, y, c), device_id_type=MESH)

        waiter(ssem.at[0], rsem0).wait_recv()
        for j in range(3):
            waiter(fs.at[j], fr.at[j]).wait_recv()
        for rel in range(4):
            waiter(ssem.at[rel], rsem0).wait_send()
        for j in range(3):
            waiter(fs.at[j], fr.at[j]).wait_send()

    return pl.pallas_call(
        body, name="gather_ffn2_wait", out_shape=_hbm_like(w2),
        in_specs=(HBM_SPEC, HBM_SPEC, SEM_SPEC, SEM_SPEC, SEM_SPEC, SEM_SPEC, ANY_SPEC), out_specs=HBM_SPEC,
        input_output_aliases={1: 0},
        compiler_params=pltpu.CompilerParams(has_side_effects=SPLIT_EFFECT),
    )(shard, w2, ssem_f, rsem_f0, fsend, frecv, after)


class _GatheredWeights(_LocalWeights):
    def __init__(self, shard, x, gain1):
        w1, self.h1, wi, wo, w2 = _all_gather_ffn1(shard, x, gain1)
        (self.ssem_m, self.rsem_m0, self.rsem_m, self.ssem_f, self.rsem_f0, self.rsem_f,
         self.shard, self.wi, self.wo, self.w2_part, self.w1) = _gather_rest_start(shard, wi, wo, w2, w1)

    def first_norm(self, x, gain):
        return self.h1

    def after_ffn1(self, gain, x1):
        self.fsend_m, self.frecv_m, self.wi, self.wo, gain = _gather_mix_pass_on(self.rsem_m, self.wi, self.wo, gain, x1)
        return gain

    def mix(self, after):
        self.shard, wint, wout = _gather_mix_wait(self.ssem_m, self.rsem_m0, self.fsend_m, self.frecv_m, self.shard,
                                                  self.wi, self.wo, after)
        return wint, wout

    def before_out_proj(self, wout, after):
        self.fsend, self.frecv, self.w2_part, wout = _gather_ffn2_pass_on(self.rsem_f, self.w2_part, wout, after)
        return wout

    def ffn2(self, after):
        return _gather_ffn2_wait(self.ssem_f, self.rsem_f0, self.fsend, self.frecv, self.shard, self.w2_part, after)

    def out_ffn2_grads_ready(self, dwout, dw2, after):
        rx1 = lax.empty((4, RSA_ROWS, D), bf16)
        sa, ra, sent, rx1, after = _rsa_level1_start(dict(wo=dwout, w2=dw2), rx1, after, "rsa_level1_start_out_ffn2")
        self.level1 = ((sa, ra), sent, rx1)
        return after

    def before_ffn1_bwd(self, dwint, dx1b):
        early, sent, rx1 = self.level1
        sa, ra, late, rx1, dx1b = _rsa_level1_start(dict(wi=dwint), rx1, dx1b, "rsa_level1_start_in")
        started = (((MIX_PIECES[1],) + F2_PIECES, *early), ((MIX_PIECES[0],), sa, ra))
        rx2 = lax.empty((3, RSA_ROWS, D), bf16)
        self.sb, self.rb, self.tx, self.acc, self.rx2, dx1b = _rsa_sums_and_send(
            started, late["wi"], sent["wo"], sent["w2"], rx1, rx2, dx1b)
        return dx1b

    def mix_ffn2_grads_parts(self, after):
        rx2 = _rsa_level2_wait(self.sb, self.rb, self.tx, self.rx2, after)
        return self.acc, rx2


def _reduce_scatter_ffn1_head(dw1, small_packed):
    pieces = G1_PIECES
    half = FS // 2
    hrows = len(pieces) * half
    nrows = 2 * hrows
    X_RELAY, Y_RELAY = range(2)

    def body(d1_ref, p_ref, forx_ref, fory_ref, own_ref, rx1_ref, relx_ref, rely_ref, tot_ref,
             own_buf, rx_buf, tx1, tx2, tx3, acc, sa, ra, sb, rb, lsem, pair, chips, small_send, small_recv):
        x, y, c = _position()
        me, sib = (x, y, c), (x, y, 1 - c)
        xn, yn = (1 - x, y, c), (x, 1 - y, c)
        rel_chips = [(x, y), (1 - x, y), (x, 1 - y), (1 - x, 1 - y)]
        srcs = _weight_pieces(w1_ref=d1_ref)

        my_chip = 2 * x + y
        pair[c] = p_ref[...]
        swap = pltpu.make_async_remote_copy(
            src_ref=p_ref, dst_ref=pair.at[c], send_sem=small_send.at[0], recv_sem=small_recv.at[0],
            device_id=sib, device_id_type=MESH)
        swap.start()
        small = [pltpu.make_async_remote_copy(
            src_ref=chips.at[my_chip], dst_ref=chips.at[my_chip], send_sem=small_send.at[j], recv_sem=small_recv.at[j],
            device_id=(*rel_chips[j], c), device_id_type=MESH) for j in (1, 2, 3)]

        def part(k, dev, hf):
            r = PIECE_ROWS[k]
            return srcs[k].at[pl.ds(pl.multiple_of(_dev_index(*dev) * r + hf * half, 16), half), :]

        def slot(ref, k, hf):
            return ref.at[pl.ds(hf * hrows + k * half, half), :]

        halves = [(k, hf) for hf in (0, 1) for k in pieces]

        for j in (3, 1, 2, 0):
            for k, hf in halves:
                pltpu.make_async_remote_copy(
                    src_ref=part(k, (*rel_chips[j], 1 - c), hf), dst_ref=slot(rx1_ref.at[j], k, hf),
                    send_sem=sa.at[j], recv_sem=ra.at[j], device_id=sib, device_id_type=MESH).start()

        def wait_a(j):
            return pltpu.make_async_remote_copy(src_ref=rx1_ref.at[j], dst_ref=rx1_ref.at[j], send_sem=sa.at[j],
                                                recv_sem=ra.at[j], device_id=me, device_id_type=MESH)

        def ici(rel, src, dst, to):
            return pltpu.make_async_remote_copy(src_ref=src, dst_ref=dst, send_sem=sb.at[rel], recv_sem=rb.at[rel],
                                                device_id=to, device_id_type=MESH)

        first, second = pl.ds(0, hrows), pl.ds(hrows, hrows)
        sends = {
            X_RELAY: ici(X_RELAY, tx3.at[first, :], relx_ref, xn),
            Y_RELAY: ici(Y_RELAY, tx3.at[second, :], rely_ref, yn),
        }

        swap.wait_recv()
        chips[my_chip] = pair[0] + pair[1]
        for cp in small:
            cp.start()

        def chip_sum(j, dst):
            loads = [pltpu.make_async_copy(part(k, (*rel_chips[j], c), hf), slot(own_buf, k, hf), lsem.at[0])
                     for k, hf in halves]
            for cp in loads:
                cp.start()
            wait_a(j).wait_recv()
            got = pltpu.make_async_copy(rx1_ref.at[j], rx_buf, lsem.at[1])
            got.start()
            pltpu.make_async_copy(rx_buf, rx_buf, lsem.at[0]).wait()
            got.wait()

            def add(i, carry):
                rows = pl.ds(pl.multiple_of(i * half, 16), half)
                tot = own_buf[rows, :].astype(f32) + rx_buf[rows, :].astype(f32)
                dst[rows, :] = tot.astype(dst.dtype)
                return carry

            lax.fori_loop(0, nrows // half, add, 0)

        def add_landed(landed, dst, rows0, nrows_):
            got = pltpu.make_async_copy(landed, rx_buf.at[pl.ds(0, nrows_), :], lsem.at[1])
            got.start()
            got.wait()

            def add(i, carry):
                src_rows = pl.ds(pl.multiple_of(i * half, 16), half)
                dst_rows = pl.ds(pl.multiple_of(rows0 + i * half, 16), half)
                dst[dst_rows, :] = (dst[dst_rows, :].astype(f32) + rx_buf[src_rows, :].astype(f32)).astype(dst.dtype)
                return carry

            lax.fori_loop(0, nrows_ // half, add, 0)

        chip_sum(3, tx3)
        sends[X_RELAY].start()
        sends[Y_RELAY].start()
        chip_sum(1, tx1)
        chip_sum(2, tx2)
        chip_sum(0, acc)
        own_out = pltpu.make_async_copy(acc, own_ref, lsem.at[0])
        own_out.start()
        sends[X_RELAY].wait_recv()
        add_landed(relx_ref, tx2, 0, hrows)
        sends[Y_RELAY].wait_recv()
        add_landed(rely_ref, tx1, hrows, hrows)
        own_out.wait()
        outs = [pltpu.make_async_copy(tx1, forx_ref, lsem.at[0]), pltpu.make_async_copy(tx2, fory_ref, lsem.at[1])]
        for cp in outs:
            cp.start()
        for cp in outs:
            cp.wait()
        for cp in small:
            cp.wait_recv()
        tot = (chips[0] + chips[1]) + (chips[2] + chips[3])
        tot_ref[...] = tot
        loss = jnp.sum(tot[LOSS_ROW:LOSS_ROW + 1, :], axis=-1, keepdims=True)
        tot_ref[LOSS_ROW:LOSS_ROW + 1, :] = jnp.broadcast_to(loss, (1, 128))
        for j in range(4):
            wait_a(j).wait_send()
        for cp in sends.values():
            cp.wait_send()
        swap.wait_send()
        for cp in small:
            cp.wait_send()

    hbm = pl.BlockSpec(memory_space=pl.ANY)
    vm = pl.BlockSpec(memory_space=pltpu.VMEM)
    outs = pl.pallas_call(
        body, in_specs=[hbm, vm], out_specs=[hbm] * 6 + [vm],
        out_shape=(SDS((nrows, D), bf16), SDS((nrows, D), bf16), SDS((nrows, D), f32), SDS((4, nrows, D), bf16),
                   SDS((hrows, D), bf16), SDS((hrows, D), bf16), SDS((SMALL_ROWS, 128), f32)),
        scratch_shapes=[pltpu.VMEM((nrows, D), bf16), pltpu.VMEM((nrows, D), bf16),
                        pltpu.VMEM((nrows, D), bf16), pltpu.VMEM((nrows, D), bf16), pltpu.VMEM((nrows, D), bf16),
                        pltpu.VMEM((nrows, D), f32),
                        pltpu.SemaphoreType.DMA((4,)), pltpu.SemaphoreType.DMA((4,)),
                        pltpu.SemaphoreType.DMA((2,)), pltpu.SemaphoreType.DMA((2,)), pltpu.SemaphoreType.DMA((2,)),
                        pltpu.VMEM((2, SMALL_ROWS, 128), f32), pltpu.VMEM((4, SMALL_ROWS, 128), f32),
                        pltpu.SemaphoreType.DMA((4,)), pltpu.SemaphoreType.DMA((4,))],
        compiler_params=pltpu.CompilerParams(has_side_effects=True, vmem_limit_bytes=VMEM_LIMIT_V7X),
        name="reduce_scatter_ffn1_head")(dw1, small_packed)
    return outs[0], outs[1], outs[2], outs[-1]


def _rs1_tail_start(for_x, for_y, from_x, from_y, thru):
    def body(fx_ref, fy_ref, lx_ref, ly_ref, thru_ref, ssem, rsem, fx_o, fy_o, lx_o, ly_o, thru_o):
        x, y, c = _position()
        pltpu.make_async_remote_copy(src_ref=fx_ref, dst_ref=lx_ref, send_sem=ssem.at[0], recv_sem=rsem.at[0],
                                     device_id=(1 - x, y, c), device_id_type=MESH).start()
        pltpu.make_async_remote_copy(src_ref=fy_ref, dst_ref=ly_ref, send_sem=ssem.at[1], recv_sem=rsem.at[1],
                                     device_id=(x, 1 - y, c), device_id_type=MESH).start()

    dma = pltpu.SemaphoreType.DMA
    arrs = (for_x, for_y, from_x, from_y, thru)
    return pl.pallas_call(
        body, name="rs1_tail_start", out_shape=(dma((2,)), dma((2,))) + tuple(_hbm_like(a) for a in arrs),
        in_specs=(HBM_SPEC,) * 5, out_specs=(SEM_SPEC,) * 2 + (HBM_SPEC,) * 5,
        input_output_aliases={0: 2, 1: 3, 2: 4, 3: 5, 4: 6},
        compiler_params=pltpu.CompilerParams(has_side_effects=SPLIT_EFFECT),
    )(*[_in_hbm(a) for a in arrs])


def _rs1_tail_wait(ssem, rsem, for_x, for_y, from_x, from_y, after):
    def body(fx_ref, fy_ref, lx_ref, ly_ref, ssem_ref, rsem_ref, after_ref, lx_o, ly_o):
        x, y, c = _position()
        for j, (src, dst) in enumerate(((fx_ref, lx_ref), (fy_ref, ly_ref))):
            d = pltpu.make_async_remote_copy(src_ref=src, dst_ref=dst, send_sem=ssem_ref.at[j], recv_sem=rsem_ref.at[j],
                                             device_id=(x, y, c), device_id_type=MESH)
            d.wait_recv()
            d.wait_send()

    return pl.pallas_call(
        body, name="rs1_tail_wait", out_shape=(_hbm_like(from_x), _hbm_like(from_y)),
        in_specs=(HBM_SPEC,) * 4 + (SEM_SPEC, SEM_SPEC, ANY_SPEC), out_specs=(HBM_SPEC, HBM_SPEC),
        input_output_aliases={2: 0, 3: 1},
        compiler_params=pltpu.CompilerParams(has_side_effects=SPLIT_EFFECT),
    )(for_x, for_y, from_x, from_y, ssem, rsem, after)


RSA_PIECES = MIX_PIECES + F2_PIECES
RSA_ROWS = _group_rows(RSA_PIECES)
RSA_OFF = {k: PIECE_OFF[k] - PIECE_OFF[RSA_PIECES[0]] for k in RSA_PIECES}
RSA_BLOCK = 192


def _rsa_rows(ref, k):
    return ref.at[pl.ds(RSA_OFF[k], PIECE_ROWS[k]), :]


def _rsa_level1_start(grads, rx1, thru, name):
    keys = sorted(grads)
    n = len(keys)

    def body(*refs):
        srcs = _weight_pieces(**{k + "_ref": ref for k, ref in zip(keys, refs[:n])})
        rx1_ref, sa, ra = refs[n], refs[n + 2], refs[n + 3]
        x, y, c = _position()
        for j, chip in enumerate([(x, y), (1 - x, y), (x, 1 - y), (1 - x, 1 - y)]):
            for k in sorted(srcs):
                pltpu.make_async_remote_copy(
                    src_ref=_block_rows(srcs, k, (*chip, 1 - c)), dst_ref=_rsa_rows(rx1_ref.at[j], k),
                    send_sem=sa.at[j], recv_sem=ra.at[j], device_id=(x, y, 1 - c), device_id_type=MESH).start()

    dma = pltpu.SemaphoreType.DMA
    arrs = tuple(grads[k] for k in keys) + (rx1, thru)
    outs = pl.pallas_call(
        body, name=name, out_shape=(dma((4,)), dma((4,))) + tuple(_hbm_like(a) for a in arrs),
        in_specs=(HBM_SPEC,) * len(arrs), out_specs=(SEM_SPEC,) * 2 + (HBM_SPEC,) * len(arrs),
        input_output_aliases={i: i + 2 for i in range(len(arrs))},
        compiler_params=pltpu.CompilerParams(has_side_effects=SPLIT_EFFECT),
    )(*[_in_hbm(a) for a in arrs])
    return outs[0], outs[1], dict(zip(keys, outs[2:2 + n])), outs[2 + n], outs[3 + n]


def _rsa_sums_and_send(started, dwint, dwout, dw2, rx1, rx2, thru):
    nblk = RSA_ROWS // RSA_BLOCK
    nstart = len(started)

    def body(*refs):
        di_ref, do_ref, d2_ref, rx1_ref, rx2_ref = refs[:5]
        l1_sems = refs[6:6 + 2 * nstart]
        sb, rb, tx_ref, acc_ref = refs[6 + 2 * nstart:10 + 2 * nstart]
        own_buf, rx_buf, tx_buf, acc_buf, in_sems, out_sems = refs[13 + 2 * nstart:]
        x, y, c = _position()
        srcs = _weight_pieces(wi_ref=di_ref, wo_ref=do_ref, w2_ref=d2_ref)
        chips = [(x, y), (1 - x, y), (x, 1 - y), (1 - x, 1 - y)]

        for g, (pieces, _, _) in enumerate(started):
            ssem, rsem = l1_sems[2 * g], l1_sems[2 * g + 1]
            for j in range(4):
                rows = rx1_ref.at[j, pl.ds(RSA_OFF[pieces[0]], _group_rows(pieces)), :]
                d = pltpu.make_async_remote_copy(src_ref=rows, dst_ref=rows, send_sem=ssem.at[j], recv_sem=rsem.at[j],
                                                 device_id=(x, y, c), device_id_type=MESH)
                d.wait_recv()
                d.wait_send()

        def start_loads(j):
            s = j % 2
            for k in RSA_PIECES:
                pltpu.make_async_copy(_block_rows(srcs, k, (*chips[j], c)), _rsa_rows(own_buf.at[s], k),
                                      in_sems.at[2 * s]).start()
            pltpu.make_async_copy(rx1_ref.at[j], rx_buf.at[s], in_sems.at[2 * s + 1]).start()

        def wait_loads(j):
            s = j % 2
            pltpu.make_async_copy(rx1_ref.at[j], own_buf.at[s], in_sems.at[2 * s]).wait()
            pltpu.make_async_copy(rx1_ref.at[j], rx_buf.at[s], in_sems.at[2 * s + 1]).wait()

        def store(j):
            if j == 0:
                return pltpu.make_async_copy(acc_buf, acc_ref, out_sems.at[2])
            return pltpu.make_async_copy(tx_buf.at[j % 2], tx_ref.at[j - 1], out_sems.at[j % 2])

        def send(j):
            return pltpu.make_async_remote_copy(src_ref=tx_ref.at[j - 1], dst_ref=rx2_ref.at[j - 1], send_sem=sb.at[j - 1],
                                                recv_sem=rb.at[j - 1], device_id=(*chips[j], c), device_id_type=MESH)

        start_loads(0)
        for j in range(4):
            s = j % 2
            if j + 1 < 4:
                start_loads(j + 1)
            wait_loads(j)
            if j == 3:
                store(1).wait()
                send(1).start()

            def add(i, carry, j=j, s=s):
                rows = pl.ds(pl.multiple_of(i * RSA_BLOCK, 16), RSA_BLOCK)
                tot = own_buf[s, rows, :].astype(f32) + rx_buf[s, rows, :].astype(f32)
                if j == 0:
                    acc_buf[rows, :] = tot
                else:
                    tx_buf[s, rows, :] = tot.astype(bf16)
                return carry

            lax.fori_loop(0, nblk, add, 0)
            store(j).start()
        store(0).wait()
        for j in (2, 3):
            store(j).wait()
            send(j).start()

    dma = pltpu.SemaphoreType.DMA
    passed = (rx1, rx2, thru)
    outs = pl.pallas_call(
        body, name="rsa_sums_and_send",
        in_specs=(HBM_SPEC,) * 6 + (SEM_SPEC,) * (2 * nstart),
        out_specs=(SEM_SPEC,) * 2 + (HBM_SPEC,) * 5,
        out_shape=(dma((3,)), dma((3,)), pltpu.HBM((3, RSA_ROWS, D), bf16), pltpu.HBM((RSA_ROWS, D), f32))
        + tuple(_hbm_like(a) for a in passed),
        input_output_aliases={3: 4, 4: 5, 5: 6},
        scratch_shapes=[pltpu.VMEM((2, RSA_ROWS, D), bf16), pltpu.VMEM((2, RSA_ROWS, D), bf16),
                        pltpu.VMEM((2, RSA_ROWS, D), bf16), pltpu.VMEM((RSA_ROWS, D), f32),
                        dma((4,)), dma((3,))],
        compiler_params=pltpu.CompilerParams(has_side_effects=SPLIT_EFFECT, vmem_limit_bytes=VMEM_LIMIT_V7X),
    )(*[_in_hbm(a) for a in (dwint, dwout, dw2) + passed], *[sem for _, sa, ra in started for sem in (sa, ra)])
    sb, rb, tx, acc, _, rx2, thru = outs
    return sb, rb, tx, acc, rx2, thru


def _rsa_level2_wait(sb, rb, tx, rx2, after):
    def body(tx_ref, rx2_ref, sb_ref, rb_ref, after_ref, rx2_o):
        x, y, c = _position()
        for j in range(3):
            d = pltpu.make_async_remote_copy(src_ref=tx_ref.at[j], dst_ref=rx2_ref.at[j], send_sem=sb_ref.at[j],
                                             recv_sem=rb_ref.at[j], device_id=(x, y, c), device_id_type=MESH)
            d.wait_recv()
            d.wait_send()

    return pl.pallas_call(
        body, name="rsa_level2_wait", out_shape=_hbm_like(rx2),
        in_specs=(HBM_SPEC, HBM_SPEC, SEM_SPEC, SEM_SPEC, ANY_SPEC), out_specs=HBM_SPEC,
        input_output_aliases={1: 0},
        compiler_params=pltpu.CompilerParams(has_side_effects=SPLIT_EFFECT),
    )(tx, rx2, sb, rb, after)


def _adamw_math(w, g, m, v):
    m = ADAM_B1 * m + (1.0 - ADAM_B1) * g
    v = ADAM_B2 * v + (1.0 - ADAM_B2) * (g * g)
    m_hat = m / (1.0 - ADAM_B1 ** ADAM_STEP)
    v_hat = v / (1.0 - ADAM_B2 ** ADAM_STEP)
    delta = -ADAM_LR * (m_hat / (jnp.sqrt(v_hat) + ADAM_EPS) + ADAM_WD * w)
    return delta, m, v


def _adamw_big(pieces, ws, ms, vs, own, landed, name):
    npiece = len(pieces)
    nland = sum(a.shape[0] if a.ndim == 3 else 1 for a in landed)
    rmax = max(PIECE_ROWS[k] for k in pieces)
    half = FS // 2

    def segments(k):
        if k in G1_PIECES:
            return [(hf * len(G1_PIECES) * half + k * half, hf * half, half) for hf in (0, 1)]
        return [(RSA_OFF[k], 0, PIECE_ROWS[k])]

    def body(*refs):
        ins = (refs[0:npiece], refs[npiece:2 * npiece], refs[2 * npiece:3 * npiece])
        own_ref = refs[3 * npiece]
        nin = 3 * npiece + 1 + len(landed)
        land_refs = []
        for ref, a in zip(refs[3 * npiece + 1:nin], landed):
            land_refs += [ref.at[j] for j in range(a.shape[0])] if a.ndim == 3 else [ref]
        out_refs = refs[nin:nin + 4 * npiece]
        inb, landb, outb, in_sems, land_sems, out_sems = refs[nin + 4 * npiece:]

        def loads(i):
            s, k = i % 2, pieces[i]
            r = PIECE_ROWS[k]
            cps = [pltpu.make_async_copy(ins[q][i].at[0], inb.at[s, q, pl.ds(0, r), :], in_sems.at[4 * s + q])
                   for q in range(3)]
            waits = list(cps)
            for src0, dst0, n in segments(k):
                cps.append(pltpu.make_async_copy(own_ref.at[pl.ds(src0, n), :], inb.at[s, 3, pl.ds(dst0, n), :],
                                                 in_sems.at[4 * s + 3]))
                for p in range(nland):
                    cps.append(pltpu.make_async_copy(land_refs[p].at[pl.ds(src0, n), :],
                                                     landb.at[s, p, pl.ds(dst0, n), :], land_sems.at[nland * s + p]))
            own_rows = inb.at[s, 3, pl.ds(0, r), :]
            waits.append(pltpu.make_async_copy(own_rows, own_rows, in_sems.at[4 * s + 3]))
            for p in range(nland):
                rows = landb.at[s, p, pl.ds(0, r), :]
                waits.append(pltpu.make_async_copy(rows, rows, land_sems.at[nland * s + p]))
            return cps, waits

        def stores(i):
            s, r = i % 2, PIECE_ROWS[pieces[i]]
            return [pltpu.make_async_copy(outb.at[s, q, pl.ds(0, r), :], out_refs[q * npiece + i].at[0],
                                          out_sems.at[4 * s + q]) for q in range(4)]

        for cp in loads(0)[0]:
            cp.start()
        for i in range(npiece):
            s, r = i % 2, PIECE_ROWS[pieces[i]]
            if i + 1 < npiece:
                for cp in loads(i + 1)[0]:
                    cp.start()
            for cp in loads(i)[1]:
                cp.wait()
            if i >= 2:
                for cp in stores(i - 2):
                    cp.wait()
            g = inb[s, 3, 0:r, :]
            for p in range(nland):
                g = g + landb[s, p, 0:r, :].astype(f32)
            d, nm, nv = _adamw_math(inb[s, 0, 0:r, :], g, inb[s, 1, 0:r, :], inb[s, 2, 0:r, :])
            outb[s, 0, 0:r, :] = g
            outb[s, 1, 0:r, :] = d
            outb[s, 2, 0:r, :] = nm
            outb[s, 3, 0:r, :] = nv
            for cp in stores(i):
                cp.start()
        for i in range(max(npiece - 2, 0), npiece):
            for cp in stores(i):
                cp.wait()

    hbm = pl.BlockSpec(memory_space=pl.ANY)
    outs = pl.pallas_call(
        body, in_specs=[hbm] * (3 * npiece + 1 + len(landed)), out_specs=[hbm] * (4 * npiece),
        out_shape=tuple(SDS(w.shape, f32) for _ in range(4) for w in ws),
        scratch_shapes=[pltpu.VMEM((2, 4, rmax, D), f32), pltpu.VMEM((2, nland, rmax, D), bf16),
                        pltpu.VMEM((2, 4, rmax, D), f32),
                        pltpu.SemaphoreType.DMA((8,)), pltpu.SemaphoreType.DMA((2 * nland,)),
                        pltpu.SemaphoreType.DMA((8,))],
        compiler_params=_cparams(None, VMEM_LIMIT_V7X), name=name)(*ws, *ms, *vs, own, *landed)
    return [list(outs[q * npiece:(q + 1) * npiece]) for q in range(4)]


def _adamw_small(ws, ms, vs, gs, name):
    n = len(ws)

    def body(*refs):
        w_refs, m_refs, v_refs, g_refs = refs[0:n], refs[n:2 * n], refs[2 * n:3 * n], refs[3 * n:4 * n]
        outs = refs[4 * n:]
        for i in range(n):
            d, nm, nv = _adamw_math(w_refs[i][...], g_refs[i][...], m_refs[i][...], v_refs[i][...])
            outs[i][...] = d
            outs[n + i][...] = nm
            outs[2 * n + i][...] = nv

    outs = pl.pallas_call(
        body, out_shape=tuple(SDS(w.shape, f32) for _ in range(3) for w in ws), name=name)(*ws, *ms, *vs, *gs)
    return [list(outs[q * n:(q + 1) * n]) for q in range(3)]


WEIGHTS = ("ffn1_norm", "ffn1_w_gate", "ffn1_w_up", "ffn1_w_down", "mix_norm", "w_in", "q_norm", "k_norm",
           "attn_sinks", "rel_bias", "pool_w", "pool_scale", "w_out", "ffn2_norm", "ffn2_w_gate", "ffn2_w_up",
           "ffn2_w_down")
BIG = (("ffn1_w_gate", True), ("ffn1_w_up", True), ("ffn1_w_down", False), ("w_in", True), ("w_out", False),
       ("ffn2_w_gate", True), ("ffn2_w_up", True), ("ffn2_w_down", False))


def kernel(x, ffn1_norm, ffn1_w_gate, ffn1_w_up, ffn1_w_down, mix_norm, w_in, q_norm, k_norm, attn_sinks, rel_bias, pool_w, pool_scale, w_out, ffn2_norm, ffn2_w_gate, ffn2_w_up, ffn2_w_down, loss_target, m_ffn1_norm, m_ffn1_w_gate, m_ffn1_w_up, m_ffn1_w_down, m_mix_norm, m_w_in, m_q_norm, m_k_norm, m_attn_sinks, m_rel_bias, m_pool_w, m_pool_scale, m_w_out, m_ffn2_norm, m_ffn2_w_gate, m_ffn2_w_up, m_ffn2_w_down, v_ffn1_norm, v_ffn1_w_gate, v_ffn1_w_up, v_ffn1_w_down, v_mix_norm, v_w_in, v_q_norm, v_k_norm, v_attn_sinks, v_rel_bias, v_pool_w, v_pool_scale, v_w_out, v_ffn2_norm, v_ffn2_w_gate, v_ffn2_w_up, v_ffn2_w_down):
    args = dict(locals())
    w = {n: args[n] for n in WEIGHTS}
    m = {n: args["m_" + n] for n in WEIGHTS}
    v = {n: args["v_" + n] for n in WEIGHTS}

    as_rows = lambda a, tr: jnp.swapaxes(a, 1, 2) if tr else a
    shard = jnp.concatenate([as_rows(w[n], tr)[0].astype(bf16) for n, tr in BIG], axis=0)
    exchanges = _GatheredWeights(shard, x[0], ffn1_norm)
    gx, (dw1, _, _, _), small = _local_step(
        x[0], loss_target[0], exchanges, ffn1_norm, mix_norm, ffn2_norm, q_norm, k_norm, attn_sinks,
        rel_bias, pool_w[0], pool_scale)

    nrows1 = len(G1_PIECES) * FS
    for_x, for_y, own1, small_tot = _reduce_scatter_ffn1_head(dw1, _pack_small(small))
    ssem, rsem, for_x, for_y, from_x, from_y, small_tot = _rs1_tail_start(
        for_x, for_y, lax.empty((nrows1, D), bf16), lax.empty((nrows1, D), bf16), small_tot)
    own_rest, landed_rest = exchanges.mix_ffn2_grads_parts(small_tot)

    grads, deltas, new_m, new_v = {}, {}, {}, {}
    rest = [k for k in range(len(BIG)) if k not in G1_PIECES]
    rows_of = lambda t, ks: [as_rows(t[BIG[k][0]], BIG[k][1]) for k in ks]
    rest_out = _adamw_big(rest, rows_of(w, rest), rows_of(m, rest), rows_of(v, rest), own_rest, [landed_rest],
                          "adamw_rest")
    from_x, from_y = _rs1_tail_wait(ssem, rsem, for_x, for_y, from_x, from_y, rest_out[0][0])
    ffn1 = list(G1_PIECES)
    ffn1_out = _adamw_big(ffn1, rows_of(w, ffn1), rows_of(m, ffn1), rows_of(v, ffn1), own1, [from_x, from_y],
                          "adamw_ffn1")
    for ks, out in ((rest, rest_out), (ffn1, ffn1_out)):
        for i, k in enumerate(ks):
            n, tr = BIG[k]
            grads[n], deltas[n], new_m[n], new_v[n] = [as_rows(o[i], tr) for o in out]
    small_names = [n for n in SMALL_NAMES if n != "loss"]
    for n in small_names:
        grads[n] = _unpack_small(small_tot, n)
    ds, nms, nvs = _adamw_small([w[n] for n in small_names], [m[n] for n in small_names], [v[n] for n in small_names],
                                [grads[n] for n in small_names], "adamw_small")
    for i, n in enumerate(small_names):
        deltas[n], new_m[n], new_v[n] = ds[i], nms[i], nvs[i]
    loss = small_tot[LOSS_ROW, 0]
    return (loss, gx[None], *[grads[n] for n in WEIGHTS], *[deltas[n] for n in WEIGHTS],
            *[new_m[n] for n in WEIGHTS], *[new_v[n] for n in WEIGHTS])
```

```python
import jax
import jax.numpy as jnp
import numpy as np
from jax import lax
from jax.experimental import pallas as pl
from jax.experimental.pallas import tpu as pltpu

f32, bf16, i32 = jnp.float32, jnp.bfloat16, jnp.int32
SDS = jax.ShapeDtypeStruct

D = 1024
F = 2816
HD = 64
NH = 8
NKV = 2
GQA = NH // NKV
DATTN = NH * HD
DKV = NKV * HD
DPOOL = 512
POOL_WINDOWS = (2, 4, 8, 16)
PGD = DPOOL // len(POOL_WINDOWS)
DIN = DATTN + 2 * DKV + DPOOL
DMIX = DATTN + DPOOL
BLK = 128
NBUCK = 32
MAX_DISTANCE = 128
EPS = 1e-6
NEG = -1e30
SCALE = HD ** -0.5

ADAM_LR, ADAM_B1, ADAM_B2, ADAM_EPS, ADAM_WD, ADAM_STEP = 0.001, 0.9, 0.999, 1e-08, 0.01, 10

NDEV = 8
FS = F // NDEV
INS = DIN // NDEV
OUTS = DMIX // NDEV
PIECE_ROWS = (FS, FS, FS, INS, OUTS, FS, FS, FS)
PIECE_OFF = tuple(int(v) for v in np.cumsum((0,) + PIECE_ROWS[:-1]))
PACK_ROWS = sum(PIECE_ROWS)

VMEM_LIMIT_V7X = 56 * 1024 * 1024

MESH = pl.DeviceIdType.MESH


def _cparams(sem=None, vmem=None):
    return pltpu.CompilerParams(dimension_semantics=sem, vmem_limit_bytes=vmem)


def _nt(a, b):
    return lax.dot_general(a, b, (((1,), (1,)), ((), ())), preferred_element_type=f32)


def _tn(a, b):
    return lax.dot_general(a, b, (((0,), (0,)), ((), ())), preferred_element_type=f32)


def _nn(a, b):
    return jnp.dot(a, b, preferred_element_type=f32)


def _sigmoid(x):
    return 1.0 / (1.0 + jnp.exp(-x))


def _norm_fwd(x, g, name):
    T = x.shape[0]
    tm = min(512, T)

    def body(x_ref, g_ref, h_ref):
        xv = x_ref[...]
        r = lax.rsqrt(jnp.mean(xv * xv, axis=-1, keepdims=True) + EPS)
        h_ref[...] = (xv * r * g_ref[...]).astype(bf16)

    return pl.pallas_call(
        body, grid=(T // tm,),
        in_specs=[pl.BlockSpec((tm, D), lambda i: (i, 0)), pl.BlockSpec((1, D), lambda i: (0, 0))],
        out_specs=pl.BlockSpec((tm, D), lambda i: (i, 0)),
        out_shape=SDS((T, D), bf16), name=name)(x, g)


FFN_ROW_CHUNK = 256


def _ffn_tiles(T):
    return min(1024, T), 256


def _ffn_fwd(h, w, x, target, next_gain, name):
    T = h.shape[0]
    tm, tf = _ffn_tiles(T)
    nf = F // tf
    with_loss = target is not None
    assert with_loss != (next_gain is not None)

    def body(*refs):
        if with_loss:
            h_ref, w_ref, x_hbm, t_hbm, xo_ref, g_ref, u_ref, dyb_ref, loss_ref, tbuf, sem = refs
        else:
            h_ref, w_ref, x_hbm, gain_ref, xo_ref, g_ref, u_ref, hn_ref, sem = refs
        fi = pl.program_id(0)

        @pl.when(fi == 0)
        def _():
            cp = pltpu.make_async_copy(x_hbm, xo_ref, sem)
            cp.start()
            cp.wait()

        wgu = w_ref[0:2].reshape(2 * tf, D)
        for r in range(0, T, tm):
            rows = slice(r, r + tm)
            gu = _nt(h_ref[rows, :], wgu)
            gate, up = gu[:, :tf], gu[:, tf:]
            act = gate * _sigmoid(gate) * up
            g_ref[0, rows, :] = gate.astype(bf16)
            u_ref[0, rows, :] = up.astype(bf16)
            xo_ref[rows, :] += _nn((0.5 * act).astype(bf16), w_ref[2])

        if with_loss:
            @pl.when(fi == nf - 1)
            def _():
                lanes = jnp.zeros((1, 128), f32)
                for r in range(0, T, tm):
                    rows = slice(r, r + tm)
                    cp = pltpu.make_async_copy(t_hbm.at[pl.ds(r, tm), :], tbuf, sem)
                    cp.start()
                    cp.wait()
                    e = xo_ref[rows, :] - tbuf[...]
                    dy = e * (1.0 / D)
                    xo_ref[rows, :] = dy
                    dyb_ref[rows, :] = (0.5 * dy).astype(bf16)
                    col = jnp.sum(e * e, axis=0, keepdims=True) * (0.5 / D)
                    for k in range(D // 128):
                        lanes = lanes + col[:, 128 * k:128 * (k + 1)]
                loss_ref[...] = lanes
        else:
            @pl.when(fi == nf - 1)
            def _():
                for r in range(0, T, FFN_ROW_CHUNK):
                    rows = slice(r, r + FFN_ROW_CHUNK)
                    xv = xo_ref[rows, :]
                    rstd = lax.rsqrt(jnp.mean(xv * xv, axis=-1, keepdims=True) + EPS)
                    hn_ref[rows, :] = (xv * rstd * gain_ref[...]).astype(bf16)

    tok = pl.BlockSpec((T, D), lambda f: (0, 0))
    act_spec = pl.BlockSpec((1, T, tf), lambda f: (f, 0, 0))
    hbm = pl.BlockSpec(memory_space=pl.ANY)
    in_specs = [tok, pl.BlockSpec((3, tf, D), lambda f: (0, f, 0)), hbm]
    out_specs = [tok, act_spec, act_spec]
    out_shape = [SDS((T, D), f32), SDS((nf, T, tf), bf16), SDS((nf, T, tf), bf16)]
    scratch = [pltpu.SemaphoreType.DMA]
    args = [h, w, x]
    if with_loss:
        in_specs.append(hbm)
        args.append(target)
        out_specs += [tok, pl.BlockSpec((1, 128), lambda f: (0, 0))]
        out_shape += [SDS((T, D), bf16), SDS((1, 128), f32)]
        scratch = [pltpu.VMEM((tm, D), f32)] + scratch
    else:
        in_specs.append(pl.BlockSpec((1, D), lambda f: (0, 0)))
        args.append(next_gain)
        out_specs.append(tok)
        out_shape.append(SDS((T, D), bf16))
    return pl.pallas_call(
        body, grid=(nf,), in_specs=in_specs, out_specs=out_specs, out_shape=tuple(out_shape), scratch_shapes=scratch,
        compiler_params=_cparams(("arbitrary",), VMEM_LIMIT_V7X), name=name)(*args)


def _ffn_bwd(dob, h, gate, up, w, norm, name):
    x, g, dres = norm
    T = h.shape[0]
    _, tf = _ffn_tiles(T)
    nf = F // tf
    tm = min(512, T)
    nchunk = T // tm

    def body(do_hbm, h_hbm, g_ref, u_ref, w_ref, x_hbm, gain_ref, dr_hbm, dx_hbm, dxb_hbm, dg_ref, dw_ref,
             do_v, h_v, dh_acc, dgu_s, act_s, xbuf, rbuf, obuf, obb, sems, in_sems, out_sems):
        fi = pl.program_id(0)

        @pl.when(fi == 0)
        def _():
            loads = [pltpu.make_async_copy(do_hbm, do_v, sems.at[0]), pltpu.make_async_copy(h_hbm, h_v, sems.at[1])]
            for cp in loads:
                cp.start()
            dh_acc[...] = jnp.zeros_like(dh_acc)
            for cp in loads:
                cp.wait()

        wgu = w_ref[0:2].reshape(2 * tf, D)
        for r in range(0, T, FFN_ROW_CHUNK):
            rows = slice(r, r + FFN_ROW_CHUNK)
            dov = do_v[rows, :]
            gv = g_ref[0, rows, :].astype(f32)
            uv = u_ref[0, rows, :].astype(f32)
            sg = _sigmoid(gv)
            sil = gv * sg
            dact = _nt(dov, w_ref[2])
            dup = dact * sil
            dgate = dact * uv * (sg * (1.0 + gv * (1.0 - sg)))
            dgu = jnp.concatenate([dgate.astype(bf16), dup.astype(bf16)], axis=1)
            dgu_s[rows, :] = dgu
            act_s[rows, :] = (sil * uv).astype(bf16)
            dh_acc[rows, :] += _nn(dgu, wgu)
        dw_ref[0:2] = _tn(dgu_s[...], h_v[...]).reshape(2, tf, D).astype(bf16)
        dw_ref[2] = _tn(act_s[...], do_v[...]).astype(bf16)

        @pl.when(fi == nf - 1)
        def _():
            def loads(i):
                s, rows = i % 2, pl.ds(i * tm, tm)
                return [pltpu.make_async_copy(x_hbm.at[rows, :], xbuf.at[s], in_sems.at[2 * s]),
                        pltpu.make_async_copy(dr_hbm.at[rows, :], rbuf.at[s], in_sems.at[2 * s + 1])]

            def stores(i):
                s, rows = i % 2, pl.ds(i * tm, tm)
                return [pltpu.make_async_copy(obuf.at[s], dx_hbm.at[rows, :], out_sems.at[2 * s]),
                        pltpu.make_async_copy(obb.at[s], dxb_hbm.at[rows, :], out_sems.at[2 * s + 1])]

            for cp in loads(0):
                cp.start()
            dg = jnp.zeros((1, D), f32)
            for i in range(nchunk):
                s = i % 2
                if i + 1 < nchunk:
                    for cp in loads(i + 1):
                        cp.start()
                for cp in loads(i):
                    cp.wait()
                if i >= 2:
                    for cp in stores(i - 2):
                        cp.wait()
                xv = xbuf[s]
                rstd = lax.rsqrt(jnp.mean(xv * xv, axis=-1, keepdims=True) + EPS)
                xh = xv * rstd
                dhv = dh_acc[i * tm:(i + 1) * tm, :]
                dxh = dhv * gain_ref[...]
                dx = rbuf[s] + rstd * (dxh - xh * jnp.mean(dxh * xh, axis=-1, keepdims=True))
                obuf[s] = dx
                obb[s] = dx.astype(bf16)
                dg = dg + jnp.sum(dhv * xh, axis=0, keepdims=True)
                for cp in stores(i):
                    cp.start()
            dg_ref[...] = dg
            for i in range(max(nchunk - 2, 0), nchunk):
                for cp in stores(i):
                    cp.wait()

    act_spec = pl.BlockSpec((1, T, tf), lambda f: (f, 0, 0))
    wspec = pl.BlockSpec((3, tf, D), lambda f: (0, f, 0))
    vec = pl.BlockSpec((1, D), lambda f: (0, 0))
    hbm = pl.BlockSpec(memory_space=pl.ANY)
    return pl.pallas_call(
        body, grid=(nf,),
        in_specs=[hbm, hbm, act_spec, act_spec, wspec, hbm, vec, hbm],
        out_specs=[hbm, hbm, vec, wspec],
        out_shape=(SDS((T, D), f32), SDS((T, D), bf16), SDS((1, D), f32), SDS((3, F, D), bf16)),
        scratch_shapes=[pltpu.VMEM((T, D), bf16), pltpu.VMEM((T, D), bf16), pltpu.VMEM((T, D), f32),
                        pltpu.VMEM((T, 2 * tf), bf16), pltpu.VMEM((T, tf), bf16),
                        pltpu.VMEM((2, tm, D), f32), pltpu.VMEM((2, tm, D), f32), pltpu.VMEM((2, tm, D), f32),
                        pltpu.VMEM((2, tm, D), bf16),
                        pltpu.SemaphoreType.DMA((2,)), pltpu.SemaphoreType.DMA((4,)), pltpu.SemaphoreType.DMA((4,))],
        compiler_params=_cparams(("arbitrary",), VMEM_LIMIT_V7X), name=name)(dob, h, gate, up, w, x, g, dres)


def _in_proj_fwd(h, wint, name):
    T = h.shape[0]
    tm = min(512, T)

    def body(h_ref, w_ref, z_ref):
        z_ref[...] = _nt(h_ref[...], w_ref[...])

    return pl.pallas_call(
        body, grid=(T // tm,),
        in_specs=[pl.BlockSpec((tm, D), lambda i: (i, 0)), pl.BlockSpec((DIN, D), lambda i: (0, 0))],
        out_specs=pl.BlockSpec((tm, DIN), lambda i: (i, 0)),
        out_shape=SDS((T, DIN), f32), name=name)(h, wint)


def _in_proj_bwd(dz, wint, h, norm, out_scale, name):
    x, g, dres = norm
    T = h.shape[0]
    tm = min(512, T)
    nt = T // tm

    def body(dz_ref, w_ref, h_ref, x_ref, g_ref, dr_ref, dx_ref, dxb_ref, dg_ref, dw_ref, acc):
        i = pl.program_id(0)
        dzb = dz_ref[...].astype(bf16)
        dhv = _nn(dzb, w_ref[...])
        part = _tn(dzb, h_ref[...])
        xv = x_ref[...]
        rstd = lax.rsqrt(jnp.mean(xv * xv, axis=-1, keepdims=True) + EPS)
        xh = xv * rstd
        dxh = dhv * g_ref[...]
        dx = dr_ref[...] + rstd * (dxh - xh * jnp.mean(dxh * xh, axis=-1, keepdims=True))
        dx_ref[...] = dx
        dxb_ref[...] = (out_scale * dx).astype(bf16)
        dg = jnp.sum(dhv * xh, axis=0, keepdims=True)

        @pl.when(i == 0)
        def _():
            acc[...] = part
            dg_ref[...] = dg

        @pl.when(i > 0)
        def _():
            acc[...] += part
            dg_ref[...] += dg

        @pl.when(i == nt - 1)
        def _():
            dw_ref[...] = acc[...].astype(bf16)

    wspec = pl.BlockSpec((DIN, D), lambda i: (0, 0))
    tok = pl.BlockSpec((tm, D), lambda i: (i, 0))
    vec = pl.BlockSpec((1, D), lambda i: (0, 0))
    return pl.pallas_call(
        body, grid=(nt,),
        in_specs=[pl.BlockSpec((tm, DIN), lambda i: (i, 0)), wspec, tok, tok, vec, tok],
        out_specs=[tok, tok, vec, wspec],
        out_shape=(SDS((T, D), f32), SDS((T, D), bf16), SDS((1, D), f32), SDS((DIN, D), bf16)),
        scratch_shapes=[pltpu.VMEM((DIN, D), f32)],
        compiler_params=_cparams(("arbitrary",)), name=name)(dz, wint, h, x, g, dres)


def _out_proj_fwd(ymix, wout, x, g, name):
    T = x.shape[0]
    tm = min(512, T)

    def body(y_ref, w_ref, x_ref, g_ref, o_ref, h_ref):
        o = x_ref[...] + _nn(y_ref[...], w_ref[...])
        o_ref[...] = o
        r = lax.rsqrt(jnp.mean(o * o, axis=-1, keepdims=True) + EPS)
        h_ref[...] = (o * r * g_ref[...]).astype(bf16)

    tok = pl.BlockSpec((tm, D), lambda i: (i, 0))
    return pl.pallas_call(
        body, grid=(T // tm,),
        in_specs=[pl.BlockSpec((tm, DMIX), lambda i: (i, 0)), pl.BlockSpec((DMIX, D), lambda i: (0, 0)), tok,
                  pl.BlockSpec((1, D), lambda i: (0, 0))],
        out_specs=[tok, tok], out_shape=(SDS((T, D), f32), SDS((T, D), bf16)), name=name)(ymix, wout, x, g)


def _out_proj_bwd(dxb, wout, ymix, name):
    T = dxb.shape[0]
    tm = min(512, T)
    nt = T // tm

    def body(dx_ref, w_ref, y_ref, dy_ref, dw_ref, acc):
        i = pl.program_id(0)
        dxv = dx_ref[...]
        dy_ref[...] = _nt(dxv, w_ref[...])
        part = _tn(y_ref[...], dxv)

        @pl.when(i == 0)
        def _():
            acc[...] = part

        @pl.when(i > 0)
        def _():
            acc[...] += part

        @pl.when(i == nt - 1)
        def _():
            dw_ref[...] = acc[...].astype(bf16)

    wspec = pl.BlockSpec((DMIX, D), lambda i: (0, 0))
    return pl.pallas_call(
        body, grid=(nt,),
        in_specs=[pl.BlockSpec((tm, D), lambda i: (i, 0)), wspec, pl.BlockSpec((tm, DMIX), lambda i: (i, 0))],
        out_specs=[pl.BlockSpec((tm, DMIX), lambda i: (i, 0)), wspec],
        out_shape=(SDS((T, DMIX), f32), SDS((DMIX, D), bf16)),
        scratch_shapes=[pltpu.VMEM((DMIX, D), f32)],
        compiler_params=_cparams(("arbitrary",)), name=name)(dxb, wout, ymix)


def _t5_bucket_table():
    ql = np.arange(BLK)[:, None]
    kl = np.arange(2 * BLK)[None, :]
    n = np.maximum(ql + BLK - kl, 0)
    max_exact = NBUCK // 2
    large = max_exact + (np.log(np.maximum(n, 1) / max_exact) / np.log(MAX_DISTANCE / max_exact)
                         * (NBUCK - max_exact)).astype(np.int32)
    large = np.minimum(large, NBUCK - 1)
    return np.where(n < max_exact, n, large).astype(np.int32)


def _fill_bias(bk_ref, rb_ref, bias_scr):
    bk = bk_ref[...]
    for h in range(NH):
        def step(b, acc, h=h):
            return acc + jnp.where(bk == b, rb_ref[b, h], 0.0)
        bias_scr[h] = lax.fori_loop(0, NBUCK, step, jnp.zeros((BLK, 2 * BLK), f32))


MIX_SUB = 4


class _Window:
    def __init__(self, zc_ref, zp_ref, n, s):
        self.blk = n * MIX_SUB + s
        self.cur = lambda a, b: zc_ref[s * BLK:(s + 1) * BLK, a:b]
        self.prev = (lambda a, b: zp_ref[:, a:b]) if s == 0 else (lambda a, b: zc_ref[(s - 1) * BLK:s * BLK, a:b])


def _attn_qkv(win, kh, qg, kg):
    kc = DATTN + HD * kh
    vc = DATTN + DKV + HD * kh
    kx = jnp.concatenate([win.prev(kc, kc + HD), win.cur(kc, kc + HD)], axis=0)
    vx = jnp.concatenate([win.prev(vc, vc + HD), win.cur(vc, vc + HD)], axis=0)
    qx = jnp.concatenate([win.cur(HD * (GQA * kh + g), HD * (GQA * kh + g + 1)) for g in range(GQA)], axis=0)
    rq = lax.rsqrt(jnp.mean(qx * qx, axis=-1, keepdims=True) + EPS)
    rk = lax.rsqrt(jnp.mean(kx * kx, axis=-1, keepdims=True) + EPS)
    qhat, khat = qx * rq, kx * rk
    return dict(qhat=qhat, khat=khat, rq=rq, rk=rk, qsb=(qhat * (qg * SCALE)).astype(bf16),
                knb=(khat * kg).astype(bf16), vb=vx.astype(bf16))


def _window_masks(n):
    row = lax.broadcasted_iota(i32, (GQA * BLK, 2 * BLK), 0) & (BLK - 1)
    col = lax.broadcasted_iota(i32, (GQA * BLK, 2 * BLK), 1)
    band = (col > row) & (col <= row + BLK)
    return band & ((col >= BLK) | (n > 0)), band


def _attn_probs(a, kh, sk_ref, bias_scr, mask):
    s = _nt(a["qsb"], a["knb"]) + bias_scr[GQA * kh:GQA * (kh + 1)].reshape(GQA * BLK, 2 * BLK)
    s = jnp.where(mask, s, NEG)
    ridx = lax.broadcasted_iota(i32, (GQA * BLK, 1), 0)
    sink = jnp.full((GQA * BLK, 1), sk_ref[GQA * kh + GQA - 1], f32)
    for g in range(GQA - 2, -1, -1):
        sink = jnp.where(ridx < (g + 1) * BLK, sk_ref[GQA * kh + g], sink)
    m = jnp.maximum(jnp.max(s, axis=-1, keepdims=True), sink)
    e = jnp.exp(s - m)
    den = jnp.sum(e, axis=-1, keepdims=True) + jnp.exp(sink - m)
    return e / den


POOL_STEPS = {2: (1,), 4: (1, 2), 8: (1, 2, 4), 16: (1, 2, 4, 8)}


def _pool_group(win, g, w):
    n = win.blk
    c0 = DATTN + 2 * DKV + PGD * g
    uc = win.cur(c0, c0 + PGD)
    up = jnp.where(n > 0, win.prev(c0, c0 + PGD), 0.0)
    sm = jnp.concatenate([up, uc], axis=0)
    for k in POOL_STEPS[w]:
        sm = sm + pltpu.roll(sm, k, axis=0)
    pos = n * BLK + lax.broadcasted_iota(i32, (BLK, 1), 0) + 1
    cnt = jnp.minimum(pos, w).astype(f32)
    return sm[BLK:2 * BLK] / cnt - uc, cnt


def _mix_fwd(z, qg, kg, sinks, relb, bucket, pool_w, pscale, name):
    T = z.shape[0]
    step_rows = MIX_SUB * BLK
    nsteps = T // step_rows

    def body(zc_ref, zp_ref, qg_ref, kg_ref, sk_ref, rb_ref, bk_ref, pw_ref, ps_ref, y_ref, p_ref, bias_scr, yacc):
        n = pl.program_id(0)

        @pl.when(n == 0)
        def _():
            _fill_bias(bk_ref, rb_ref, bias_scr)

        first_mask, mask = _window_masks(n)
        for s in range(MIX_SUB):
            win = _Window(zc_ref, zp_ref, n, s)
            rows = slice(s * BLK, (s + 1) * BLK)
            for kh in range(NKV):
                a = _attn_qkv(win, kh, qg_ref[...], kg_ref[...])
                pb = _attn_probs(a, kh, sk_ref, bias_scr, first_mask if s == 0 else mask).astype(bf16)
                p_ref[s, GQA * kh:GQA * (kh + 1)] = pb.reshape(GQA, BLK, 2 * BLK)
                o = _nn(pb, a["vb"])
                for g in range(GQA):
                    hc = HD * (GQA * kh + g)
                    yacc[rows, hc:hc + HD] = o[g * BLK:(g + 1) * BLK]
            for g, w in enumerate(POOL_WINDOWS):
                pooled, _ = _pool_group(win, g, w)
                yp = _nn(pooled.astype(bf16), pw_ref[g].astype(bf16)) * ps_ref[:, PGD * g:PGD * (g + 1)]
                yacc[rows, DATTN + PGD * g:DATTN + PGD * (g + 1)] = yp
        y_ref[...] = yacc[...].astype(bf16)

    full = lambda *shape: pl.BlockSpec(shape, lambda n: (0,) * len(shape))
    smem = pl.BlockSpec(memory_space=pltpu.SMEM)
    return pl.pallas_call(
        body, grid=(nsteps,),
        in_specs=[pl.BlockSpec((step_rows, DIN), lambda n: (n, 0)),
                  pl.BlockSpec((BLK, DIN), lambda n: (jnp.maximum(n * MIX_SUB - 1, 0), 0)),
                  full(1, HD), full(1, HD), smem, smem, full(BLK, 2 * BLK),
                  full(len(POOL_WINDOWS), PGD, PGD), full(1, DPOOL)],
        out_specs=[pl.BlockSpec((step_rows, DMIX), lambda n: (n, 0)),
                   pl.BlockSpec((MIX_SUB, NH, BLK, 2 * BLK), lambda n: (n, 0, 0, 0))],
        out_shape=(SDS((T, DMIX), bf16), SDS((T // BLK, NH, BLK, 2 * BLK), bf16)),
        scratch_shapes=[pltpu.VMEM((NH, BLK, 2 * BLK), f32), pltpu.VMEM((step_rows, DMIX), f32)],
        compiler_params=_cparams(("arbitrary",)), name=name)(z, z, qg, kg, sinks, relb, bucket, pool_w, pscale)


def _mix_bwd(z, dy, probs, qg, kg, relb, bucket, pool_w, pscale, name):
    T = z.shape[0]
    step_rows = MIX_SUB * BLK
    nsteps = T // step_rows

    def body(zc_ref, zp_ref, dy_ref, p_ref, qg_ref, kg_ref, bk_ref, pw_ref, ps_ref,
             dz_ref, dqg_ref, dkg_ref, dsk_ref, drb_ref, dpw_ref, dps_ref, dbias_scr):
        n = pl.program_id(0)

        @pl.when(n == 0)
        def _():
            dbias_scr[...] = jnp.zeros_like(dbias_scr)
            dqg_ref[...] = jnp.zeros_like(dqg_ref)
            dkg_ref[...] = jnp.zeros_like(dkg_ref)
            dpw_ref[...] = jnp.zeros_like(dpw_ref)
            dps_ref[...] = jnp.zeros_like(dps_ref)

        qg, kg = qg_ref[...], kg_ref[...]
        for s in range(MIX_SUB):
            win = _Window(zc_ref, zp_ref, n, s)
            blk = win.blk
            rows = pl.ds(pl.multiple_of(blk * BLK, BLK), BLK)
            prow = pl.ds(pl.multiple_of(jnp.maximum(blk - 1, 0) * BLK, BLK), BLK)
            dyr = slice(s * BLK, (s + 1) * BLK)

            def into_prev(fn, s=s):
                if s == 0:
                    pl.when(n > 0)(fn)
                else:
                    fn()

            for kh in range(NKV):
                a = _attn_qkv(win, kh, qg, kg)
                pb = p_ref[s, GQA * kh:GQA * (kh + 1)].reshape(GQA * BLK, 2 * BLK)
                p = pb.astype(f32)
                do = jnp.concatenate([dy_ref[dyr, HD * (GQA * kh + g):HD * (GQA * kh + g + 1)] for g in range(GQA)],
                                     axis=0).astype(bf16)
                dv = _tn(pb, do)
                dp = _nt(do, a["vb"])
                delta = jnp.sum(p * dp, axis=-1, keepdims=True)
                ds = p * (dp - delta)
                for g in range(GQA):
                    dbias_scr[GQA * kh + g] += ds[g * BLK:(g + 1) * BLK]
                dsb = ds.astype(bf16)
                dqn = _nn(dsb, a["knb"]) * SCALE
                dkn = _tn(dsb, a["qsb"])
                qhat, khat = a["qhat"], a["khat"]
                dqg_ref[...] += jnp.sum(dqn * qhat, axis=0, keepdims=True)
                dkg_ref[...] += jnp.sum(dkn * khat, axis=0, keepdims=True)
                dqh = dqn * qg
                dq = a["rq"] * (dqh - qhat * jnp.mean(dqh * qhat, axis=-1, keepdims=True))
                dkh = dkn * kg
                dk = a["rk"] * (dkh - khat * jnp.mean(dkh * khat, axis=-1, keepdims=True))
                kc = DATTN + HD * kh
                vc = DATTN + DKV + HD * kh
                for g in range(GQA):
                    hc = HD * (GQA * kh + g)
                    dz_ref[rows, hc:hc + HD] = dq[g * BLK:(g + 1) * BLK]
                dz_ref[rows, kc:kc + HD] = dk[BLK:2 * BLK]
                dz_ref[rows, vc:vc + HD] = dv[BLK:2 * BLK]

                def kv_prev(dk=dk, dv=dv, kc=kc, vc=vc, prow=prow):
                    dz_ref[prow, kc:kc + HD] += dk[0:BLK]
                    dz_ref[prow, vc:vc + HD] += dv[0:BLK]

                into_prev(kv_prev)

            for g, w in enumerate(POOL_WINDOWS):
                c0 = DATTN + 2 * DKV + PGD * g
                pooled, cnt = _pool_group(win, g, w)
                pb = pooled.astype(bf16)
                wb = pw_ref[g].astype(bf16)
                dyp = dy_ref[dyr, DATTN + PGD * g:DATTN + PGD * (g + 1)]
                ypre = _nn(pb, wb)
                dps_ref[:, PGD * g:PGD * (g + 1)] += jnp.sum(dyp * ypre, axis=0, keepdims=True)
                dyg = (dyp * ps_ref[:, PGD * g:PGD * (g + 1)]).astype(bf16)
                dpw_ref[g] += _tn(pb, dyg)
                dpooled = _nt(dyg, wb)
                due = jnp.concatenate([jnp.zeros((BLK, PGD), f32), dpooled / cnt], axis=0)
                for k in POOL_STEPS[w]:
                    due = due + pltpu.roll(due, 2 * BLK - k, axis=0)
                dz_ref[rows, c0:c0 + PGD] = due[BLK:2 * BLK] - dpooled

                def pool_prev(due=due, c0=c0, prow=prow):
                    dz_ref[prow, c0:c0 + PGD] += due[0:BLK]

                into_prev(pool_prev)

        @pl.when(n == nsteps - 1)
        def _():
            bk = bk_ref[...]
            ri = lax.broadcasted_iota(i32, (NBUCK, NH), 0)
            ci = lax.broadcasted_iota(i32, (NBUCK, NH), 1)

            def step(b, acc):
                for h in range(NH):
                    sel = jnp.where(bk == b, dbias_scr[h], 0.0)
                    tot = jnp.sum(jnp.sum(sel, axis=1, keepdims=True), axis=0, keepdims=True)
                    acc = acc + jnp.where((ri == b) & (ci == h), tot, 0.0)
                return acc

            drb_ref[...] = lax.fori_loop(0, NBUCK, step, jnp.zeros((NBUCK, NH), f32))
            lane = lax.broadcasted_iota(i32, (1, 128), 1)
            dsk = jnp.zeros((1, 128), f32)
            for h in range(NH):
                tot = jnp.sum(jnp.sum(dbias_scr[h], axis=1, keepdims=True), axis=0, keepdims=True)
                dsk = dsk - jnp.where(lane == h, tot, 0.0)
            dsk_ref[...] = dsk

    full = lambda *shape: pl.BlockSpec(shape, lambda n: (0,) * len(shape))
    npg = len(POOL_WINDOWS)
    return pl.pallas_call(
        body, grid=(nsteps,),
        in_specs=[pl.BlockSpec((step_rows, DIN), lambda n: (n, 0)),
                  pl.BlockSpec((BLK, DIN), lambda n: (jnp.maximum(n * MIX_SUB - 1, 0), 0)),
                  pl.BlockSpec((step_rows, DMIX), lambda n: (n, 0)),
                  pl.BlockSpec((MIX_SUB, NH, BLK, 2 * BLK), lambda n: (n, 0, 0, 0)),
                  full(1, HD), full(1, HD), full(BLK, 2 * BLK), full(npg, PGD, PGD), full(1, DPOOL)],
        out_specs=[full(T, DIN), full(1, HD), full(1, HD), full(1, 128), full(NBUCK, NH),
                   full(npg, PGD, PGD), full(1, DPOOL)],
        out_shape=(SDS((T, DIN), f32), SDS((1, HD), f32), SDS((1, HD), f32), SDS((1, 128), f32),
                   SDS((NBUCK, NH), f32), SDS((npg, PGD, PGD), f32), SDS((1, DPOOL), f32)),
        scratch_shapes=[pltpu.VMEM((NH, BLK, 2 * BLK), f32)],
        compiler_params=_cparams(("arbitrary",), VMEM_LIMIT_V7X),
        name=name)(z, z, dy, probs, qg, kg, bucket, pool_w, pscale)


class _LocalWeights:
    def __init__(self, w1, wint, wout, w2):
        self.w1, self.wint, self.wout, self.w2 = w1, wint, wout, w2

    def ffn1(self):
        return self.w1

    def first_norm(self, x, gain):
        return _norm_fwd(x, gain, "norm1_fwd")

    def after_ffn1(self, gain, x1):
        return gain

    def mix(self, after):
        return self.wint, self.wout

    def before_out_proj(self, wout, after):
        return wout

    def ffn2(self, after):
        return self.w2

    def out_ffn2_grads_ready(self, dwout, dw2, after):
        return after

    def before_ffn1_bwd(self, dwint, dx1b):
        return dx1b


def _local_step(x, target, weights, g1, gm, g3, qg, kg, sinks, relb, pool_w, pscale):
    bucket = jnp.asarray(_t5_bucket_table())
    sk = sinks.reshape(NH)
    w1 = weights.ffn1()
    h1 = weights.first_norm(x, g1)
    x1, gate1, up1, h2 = _ffn_fwd(h1, w1, x, None, gm, "ffn1_fwd")
    gm = weights.after_ffn1(gm, x1)
    wint, wout = weights.mix(h2)
    z = _in_proj_fwd(h2, wint, "in_proj_fwd")
    ymix, probs = _mix_fwd(z, qg, kg, sk, relb, bucket, pool_w, pscale, "mix_fwd")
    wout = weights.before_out_proj(wout, ymix)
    x2, h3 = _out_proj_fwd(ymix, wout, x1, g3, "out_proj_fwd")
    w2 = weights.ffn2(h3)
    dy, gate2, up2, dyb, loss_lanes = _ffn_fwd(h3, w2, x2, target, None, "ffn2_fwd")

    dx2, dx2b, dg3, dw2 = _ffn_bwd(dyb, h3, gate2, up2, w2, (x2, g3, dy), "ffn2_bwd")
    dymix, dwout = _out_proj_bwd(dx2b, wout, ymix, "out_proj_bwd")
    dymix = weights.out_ffn2_grads_ready(dwout, dw2, dymix)
    dz, dqg, dkg, dsk, drb, dpw, dps = _mix_bwd(z, dymix, probs, qg, kg, relb, bucket, pool_w, pscale, "mix_bwd")
    dx1, dx1b, dgm, dwint = _in_proj_bwd(dz, wint, h2, (x1, gm, dx2), 0.5, "in_proj_bwd")
    dx1b = weights.before_ffn1_bwd(dwint, dx1b)
    gx, _, dg1, dw1 = _ffn_bwd(dx1b, h1, gate1, up1, w1, (x, g1, dx1), "ffn1_bwd")
    small = dict(ffn1_norm=dg1, mix_norm=dgm, ffn2_norm=dg3, pool_scale=dps, q_norm=dqg, k_norm=dkg,
                 attn_sinks=dsk[:, :NH], rel_bias=drb, pool_w=dpw, loss=loss_lanes)
    return gx, (dw1, dwint, dwout, dw2), small


SMALL_NAMES = ("ffn1_norm", "mix_norm", "ffn2_norm", "pool_scale", "q_norm", "k_norm", "attn_sinks", "rel_bias",
               "pool_w", "loss")
SMALL_SHAPES = dict(ffn1_norm=(1, D), mix_norm=(1, D), ffn2_norm=(1, D), pool_scale=(1, DPOOL), q_norm=(1, HD),
                    k_norm=(1, HD), attn_sinks=(1, NH), rel_bias=(NBUCK, NH),
                    pool_w=(1, len(POOL_WINDOWS), PGD, PGD), loss=(1, 128))


def _small_rows(name):
    return -(-int(np.prod(SMALL_SHAPES[name])) // 128)


SMALL_OFF = {}
_r = 0
for _n in SMALL_NAMES:
    SMALL_OFF[_n] = _r
    _r += _small_rows(_n)
SMALL_ROWS = -(-_r // 16) * 16
LOSS_ROW = SMALL_OFF["loss"]


def _pack_small(vals):
    parts = []
    for n in SMALL_NAMES:
        size = _small_rows(n) * 128
        if n in vals:
            flat = vals[n].astype(f32).reshape(-1)
            parts.append(jnp.pad(flat, (0, size - flat.shape[0])))
        else:
            parts.append(jnp.zeros((size,), f32))
    flat = jnp.concatenate(parts)
    flat = jnp.pad(flat, (0, SMALL_ROWS * 128 - flat.shape[0]))
    return flat.reshape(SMALL_ROWS, 128)


def _unpack_small(packed, name):
    size = int(np.prod(SMALL_SHAPES[name]))
    r0 = SMALL_OFF[name]
    return packed[r0:r0 + _small_rows(name)].reshape(-1)[:size].reshape(SMALL_SHAPES[name])


def _position():
    return lax.axis_index("x"), lax.axis_index("y"), lax.axis_index("c")


def _dev_index(x, y, c):
    return 4 * x + 2 * y + c


G1_PIECES, MIX_PIECES, F2_PIECES = (0, 1, 2), (3, 4), (5, 6, 7)


def _group_rows(pieces):
    return sum(PIECE_ROWS[k] for k in pieces)


def _shard_piece(s_ref, k):
    return s_ref.at[pl.ds(PIECE_OFF[k], PIECE_ROWS[k]), :]


def _shard_group(s_ref, pieces):
    return s_ref.at[pl.ds(PIECE_OFF[pieces[0]], _group_rows(pieces)), :]


def _weight_pieces(w1_ref=None, wi_ref=None, wo_ref=None, w2_ref=None):
    arrs = {}
    if w1_ref is not None:
        arrs.update({0: w1_ref.at[0], 1: w1_ref.at[1], 2: w1_ref.at[2]})
    if wi_ref is not None:
        arrs[3] = wi_ref
    if wo_ref is not None:
        arrs[4] = wo_ref
    if w2_ref is not None:
        arrs.update({5: w2_ref.at[0], 6: w2_ref.at[1], 7: w2_ref.at[2]})
    return arrs


def _block_rows(arrs, k, dev):
    r = PIECE_ROWS[k]
    return arrs[k].at[pl.ds(pl.multiple_of(_dev_index(*dev) * r, 16), r), :]


NORM_ROWS = 512


def _all_gather_ffn1(shard, x, gain):
    pieces = G1_PIECES
    rest_pieces = MIX_PIECES + F2_PIECES
    half = FS // 2
    T = x.shape[0]
    SIB, X0, X1, Y0, Y1, RELAY_Y, RELAY_X, ON_X, ON_Y, ON_D0, ON_D1 = range(11)

    def body(s_ref, x_ref, g_ref, w1_ref, h_ref, wi_ref, wo_ref, w2_ref, xbuf, hbuf, rest_buf,
             send_sems, recv_sems, local_sem, norm_sems):
        x, y, c = _position()
        me, sib = (x, y, c), (x, y, 1 - c)
        xn, yn, dg = (1 - x, y, c), (x, 1 - y, c), (1 - x, 1 - y, c)
        arrs = _weight_pieces(w1_ref=w1_ref)

        def place_rest():
            rest = _weight_pieces(wi_ref=wi_ref, wo_ref=wo_ref, w2_ref=w2_ref)
            grp = _shard_group(s_ref, rest_pieces)
            load = pltpu.make_async_copy(grp, rest_buf, norm_sems.at[0])
            load.start()
            load.wait()
            base = PIECE_OFF[rest_pieces[0]]
            for k in rest_pieces:
                pltpu.make_async_copy(rest_buf.at[pl.ds(PIECE_OFF[k] - base, PIECE_ROWS[k]), :],
                                      _block_rows(rest, k, me), norm_sems.at[1]).start()
            pltpu.make_async_copy(grp, rest_buf, norm_sems.at[1]).wait()

        def first_norm():
            for r in range(0, T, NORM_ROWS):
                load = pltpu.make_async_copy(x_ref.at[pl.ds(r, NORM_ROWS), :], xbuf, norm_sems.at[0])
                load.start()
                load.wait()
                xv = xbuf[...]
                rs = lax.rsqrt(jnp.mean(xv * xv, axis=-1, keepdims=True) + EPS)
                hbuf[...] = (xv * rs * g_ref[...]).astype(bf16)
                store = pltpu.make_async_copy(hbuf, h_ref.at[pl.ds(r, NORM_ROWS), :], norm_sems.at[1])
                store.start()
                store.wait()

        def rows_of(k, block, hf):
            r = PIECE_ROWS[k]
            start, size = (0, r) if hf is None else (hf * half, half)
            return arrs[k].at[pl.ds(pl.multiple_of(_dev_index(*block) * r + start, 16), size), :]

        def copies(rel, block, hf, to, from_shard=False):
            def src(k):
                if not from_shard:
                    return rows_of(k, block, hf)
                start, size = (0, PIECE_ROWS[k]) if hf is None else (hf * half, half)
                return s_ref.at[pl.ds(PIECE_OFF[k] + start, size), :]
            return [pltpu.make_async_remote_copy(
                src_ref=src(k), dst_ref=rows_of(k, block, hf), send_sem=send_sems.at[rel], recv_sem=recv_sems.at[rel],
                device_id=to, device_id_type=MESH) for k in pieces]

        def waiter(rel, hf):
            nrows = len(pieces) * (FS if hf is None else half)
            grp = s_ref.at[pl.ds(0, nrows), :]
            return pltpu.make_async_remote_copy(src_ref=grp, dst_ref=grp, send_sem=send_sems.at[rel],
                                                recv_sem=recv_sems.at[rel], device_id=me, device_id_type=MESH)

        def start(cps):
            for cp in cps:
                cp.start()

        mine = [pltpu.make_async_copy(_shard_piece(s_ref, k), _block_rows(arrs, k, me), local_sem) for k in pieces]
        start(mine)
        start(copies(SIB, me, None, sib, True))
        start(copies(X0, me, 0, xn, True))
        start(copies(Y1, me, 1, yn, True))
        start(copies(X1, me, 1, xn, True))
        start(copies(Y0, me, 0, yn, True))
        first_norm()
        place_rest()
        waiter(X0, 0).wait_recv()
        start(copies(RELAY_Y, xn, 0, yn))
        waiter(Y1, 1).wait_recv()
        start(copies(RELAY_X, yn, 1, xn))
        waiter(X1, 1).wait_recv()
        start(copies(ON_X, xn, None, sib))
        waiter(Y0, 0).wait_recv()
        start(copies(ON_Y, yn, None, sib))
        waiter(RELAY_Y, 0).wait_recv()
        start(copies(ON_D0, dg, 0, sib))
        waiter(RELAY_X, 1).wait_recv()
        start(copies(ON_D1, dg, 1, sib))
        waiter(SIB, None).wait_recv()
        waiter(ON_X, None).wait_recv()
        waiter(ON_Y, None).wait_recv()
        waiter(ON_D0, 0).wait_recv()
        waiter(ON_D1, 1).wait_recv()
        for rel, hf in ((SIB, None), (X0, 0), (X1, 1), (Y0, 0), (Y1, 1), (RELAY_Y, 0), (RELAY_X, 1),
                        (ON_X, None), (ON_Y, None), (ON_D0, 0), (ON_D1, 1)):
            waiter(rel, hf).wait_send()
        grp = _shard_group(s_ref, pieces)
        pltpu.make_async_copy(grp, grp, local_sem).wait()

    hbm = pl.BlockSpec(memory_space=pl.ANY)
    return pl.pallas_call(
        body, in_specs=[hbm, hbm, pl.BlockSpec(memory_space=pltpu.VMEM)], out_specs=[hbm] * 5,
        out_shape=(SDS((3, F, D), bf16), SDS((T, D), bf16),
                   SDS((DIN, D), bf16), SDS((DMIX, D), bf16), SDS((3, F, D), bf16)),
        scratch_shapes=[pltpu.VMEM((NORM_ROWS, D), f32), pltpu.VMEM((NORM_ROWS, D), bf16),
                        pltpu.VMEM((_group_rows(rest_pieces), D), bf16),
                        pltpu.SemaphoreType.DMA((11,)), pltpu.SemaphoreType.DMA((11,)), pltpu.SemaphoreType.DMA,
                        pltpu.SemaphoreType.DMA((2,))],
        compiler_params=pltpu.CompilerParams(has_side_effects=True),
        name="all_gather_ffn1")(shard, x, gain)


HBM_SPEC = pl.BlockSpec(memory_space=pltpu.HBM)
SEM_SPEC = pl.BlockSpec(memory_space=pltpu.SEMAPHORE)
ANY_SPEC = pl.BlockSpec(memory_space=pl.ANY)
SPLIT_EFFECT = pltpu.SideEffectType.DATAFLOW_SIDE_EFFECTING


def _in_hbm(a):
    return pltpu.with_memory_space_constraint(a, pltpu.HBM)


def _hbm_like(a):
    return pltpu.HBM(a.shape, a.dtype)


def _gather_rest_start(shard, wi, wo, w2, w1):
    def body(s_ref, wi_ref, wo_ref, w2_ref, w1_ref,
             ssem_m, rsem_m0, rsem_m, ssem_f, rsem_f0, rsem_f, s_o, wi_o, wo_o, w2_o, w1_o):
        x, y, c = _position()
        me, sib = (x, y, c), (x, y, 1 - c)
        chips = [(1 - x, y), (x, 1 - y), (1 - x, 1 - y)]
        arrs = _weight_pieces(wi_ref=wi_ref, wo_ref=wo_ref, w2_ref=w2_ref)
        for pieces, ssem, rsem0, rsem in ((MIX_PIECES, ssem_m, rsem_m0, rsem_m), (F2_PIECES, ssem_f, rsem_f0, rsem_f)):
            for p in pieces:
                pltpu.make_async_remote_copy(
                    src_ref=_shard_piece(s_ref, p), dst_ref=_block_rows(arrs, p, me), send_sem=ssem.at[0],
                    recv_sem=rsem0, device_id=sib, device_id_type=MESH).start()
            for j, chip in enumerate(chips):
                for p in pieces:
                    pltpu.make_async_remote_copy(
                        src_ref=_shard_piece(s_ref, p), dst_ref=_block_rows(arrs, p, me), send_sem=ssem.at[1 + j],
                        recv_sem=rsem.at[j], device_id=(*chip, c), device_id_type=MESH).start()

    dma = pltpu.SemaphoreType.DMA
    return pl.pallas_call(
        body, name="gather_rest_start",
        out_shape=(dma((4,)), dma(()), dma((3,)), dma((4,)), dma(()), dma((3,)),
                   _hbm_like(shard), _hbm_like(wi), _hbm_like(wo), _hbm_like(w2), _hbm_like(w1)),
        in_specs=(HBM_SPEC,) * 5, out_specs=(SEM_SPEC,) * 6 + (HBM_SPEC,) * 5,
        input_output_aliases={0: 6, 1: 7, 2: 8, 3: 9, 4: 10},
        compiler_params=pltpu.CompilerParams(has_side_effects=SPLIT_EFFECT),
    )(_in_hbm(shard), _in_hbm(wi), _in_hbm(wo), _in_hbm(w2), _in_hbm(w1))


def _gather_mix_pass_on(rsem_m, wi, wo, thru, after):
    def body(wi_ref, wo_ref, thru_ref, rsem, after_ref, fsend, frecv, wi_o, wo_o, thru_o):
        x, y, c = _position()
        sib = (x, y, 1 - c)
        arrs = _weight_pieces(wi_ref=wi_ref, wo_ref=wo_ref)
        both = wi_ref.at[pl.ds(0, _group_rows(MIX_PIECES)), :]
        for j, chip in enumerate([(1 - x, y), (x, 1 - y), (1 - x, 1 - y)]):
            pltpu.make_async_remote_copy(src_ref=both, dst_ref=both, send_sem=fsend.at[j], recv_sem=rsem.at[j],
                                         device_id=(x, y, c), device_id_type=MESH).wait_recv()
            for p in MIX_PIECES:
                rows = _block_rows(arrs, p, (*chip, c))
                pltpu.make_async_remote_copy(src_ref=rows, dst_ref=rows, send_sem=fsend.at[j], recv_sem=frecv.at[j],
                                             device_id=sib, device_id_type=MESH).start()

    dma = pltpu.SemaphoreType.DMA
    return pl.pallas_call(
        body, name="gather_mix_pass_on",
        out_shape=(dma((3,)), dma((3,)), _hbm_like(wi), _hbm_like(wo), _hbm_like(thru)),
        in_specs=(HBM_SPEC, HBM_SPEC, HBM_SPEC, SEM_SPEC, ANY_SPEC), out_specs=(SEM_SPEC, SEM_SPEC) + (HBM_SPEC,) * 3,
        input_output_aliases={0: 2, 1: 3, 2: 4},
        compiler_params=pltpu.CompilerParams(has_side_effects=SPLIT_EFFECT),
    )(wi, wo, _in_hbm(thru), rsem_m, after)


def _gather_mix_wait(ssem_m, rsem_m0, fsend, frecv, shard, wi, wo, after):
    def body(s_ref, wi_ref, wo_ref, ssem, rsem0, fs, fr, after_ref, s_o, wi_o, wo_o):
        x, y, c = _position()
        grp = _shard_group(s_ref, MIX_PIECES)

        def waiter(send_sem, recv_sem):
            return pltpu.make_async_remote_copy(src_ref=grp, dst_ref=grp, send_sem=send_sem, recv_sem=recv_sem,
                                                device_id=(x, y, c), device_id_type=MESH)

        waiter(ssem.at[0], rsem0).wait_recv()
        for j in range(3):
            waiter(fs.at[j], fr.at[j]).wait_recv()
        for rel in range(4):
            waiter(ssem.at[rel], rsem0).wait_send()
        for j in range(3):
            waiter(fs.at[j], fr.at[j]).wait_send()

    return pl.pallas_call(
        body, name="gather_mix_wait", out_shape=(_hbm_like(shard), _hbm_like(wi), _hbm_like(wo)),
        in_specs=(HBM_SPEC,) * 3 + (SEM_SPEC,) * 4 + (ANY_SPEC,), out_specs=(HBM_SPEC,) * 3,
        input_output_aliases={0: 0, 1: 1, 2: 2},
        compiler_params=pltpu.CompilerParams(has_side_effects=SPLIT_EFFECT),
    )(shard, wi, wo, ssem_m, rsem_m0, fsend, frecv, after)


def _gather_ffn2_pass_on(rsem_f, w2, wo, after):
    def body(w2_ref, wo_ref, rsem, after_ref, fsend, frecv, w2_o, wo_o):
        x, y, c = _position()
        sib = (x, y, 1 - c)
        chips = [(1 - x, y), (x, 1 - y), (1 - x, 1 - y)]
        arrs = _weight_pieces(w2_ref=w2_ref)
        three = w2_ref.at[0, pl.ds(0, _group_rows(F2_PIECES)), :]
        for j, chip in enumerate(chips):
            pltpu.make_async_remote_copy(src_ref=three, dst_ref=three, send_sem=fsend.at[j], recv_sem=rsem.at[j],
                                         device_id=(x, y, c), device_id_type=MESH).wait_recv()
            for p in F2_PIECES:
                rows = _block_rows(arrs, p, (*chip, c))
                pltpu.make_async_remote_copy(src_ref=rows, dst_ref=rows, send_sem=fsend.at[j], recv_sem=frecv.at[j],
                                             device_id=sib, device_id_type=MESH).start()

    dma = pltpu.SemaphoreType.DMA
    return pl.pallas_call(
        body, name="gather_ffn2_pass_on", out_shape=(dma((3,)), dma((3,)), _hbm_like(w2), _hbm_like(wo)),
        in_specs=(HBM_SPEC, HBM_SPEC, SEM_SPEC, ANY_SPEC), out_specs=(SEM_SPEC, SEM_SPEC, HBM_SPEC, HBM_SPEC),
        input_output_aliases={0: 2, 1: 3},
        compiler_params=pltpu.CompilerParams(has_side_effects=SPLIT_EFFECT),
    )(w2, wo, rsem_f, after)


def _gather_ffn2_wait(ssem_f, rsem_f0, fsend, frecv, shard, w2, after):
    def body(s_ref, w2_ref, ssem, rsem0, fs, fr, after_ref, w2_o):
        x, y, c = _position()
        grp = _shard_group(s_ref, F2_PIECES)

        def waiter(send_sem, recv_sem):
            return pltpu.make_async_remote_copy(src_ref=grp, dst_ref=grp, send_sem=send_sem, recv_sem=recv_sem,
                                                device_id=(x, y, c), device_id_type=MESH)

        waiter(ssem.at[0], rsem0).wait_recv()
        for j in range(3):
            waiter(fs.at[j], fr.at[j]).wait_recv()
        for rel in range(4):
            waiter(ssem.at[rel], rsem0).wait_send()
        for j in range(3):
            waiter(fs.at[j], fr.at[j]).wait_send()

    return pl.pallas_call(
        body, name="gather_ffn2_wait", out_shape=_hbm_like(w2),
        in_specs=(HBM_SPEC, HBM_SPEC, SEM_SPEC, SEM_SPEC, SEM_SPEC, SEM_SPEC, ANY_SPEC), out_specs=HBM_SPEC,
        input_output_aliases={1: 0},
        compiler_params=pltpu.CompilerParams(has_side_effects=SPLIT_EFFECT),
    )(shard, w2, ssem_f, rsem_f0, fsend, frecv, after)


class _GatheredWeights(_LocalWeights):
    def __init__(self, shard, x, gain1):
        w1, self.h1, wi, wo, w2 = _all_gather_ffn1(shard, x, gain1)
        (self.ssem_m, self.rsem_m0, self.rsem_m, self.ssem_f, self.rsem_f0, self.rsem_f,
         self.shard, self.wi, self.wo, self.w2_part, self.w1) = _gather_rest_start(shard, wi, wo, w2, w1)

    def first_norm(self, x, gain):
        return self.h1

    def after_ffn1(self, gain, x1):
        self.fsend_m, self.frecv_m, self.wi, self.wo, gain = _gather_mix_pass_on(self.rsem_m, self.wi, self.wo, gain, x1)
        return gain

    def mix(self, after):
        self.shard, wint, wout = _gather_mix_wait(self.ssem_m, self.rsem_m0, self.fsend_m, self.frecv_m, self.shard,
                                                  self.wi, self.wo, after)
        return wint, wout

    def before_out_proj(self, wout, after):
        self.fsend, self.frecv, self.w2_part, wout = _gather_ffn2_pass_on(self.rsem_f, self.w2_part, wout, after)
        return wout

    def ffn2(self, after):
        return _gather_ffn2_wait(self.ssem_f, self.rsem_f0, self.fsend, self.frecv, self.shard, self.w2_part, after)

    def out_ffn2_grads_ready(self, dwout, dw2, after):
        rx1 = lax.empty((4, RSA_ROWS, D), bf16)
        sa, ra, sent, rx1, after = _rsa_level1_start(dict(wo=dwout, w2=dw2), rx1, after, "rsa_level1_start_out_ffn2")
        self.level1 = ((sa, ra), sent, rx1)
        return after

    def before_ffn1_bwd(self, dwint, dx1b):
        early, sent, rx1 = self.level1
        sa, ra, late, rx1, dx1b = _rsa_level1_start(dict(wi=dwint), rx1, dx1b, "rsa_level1_start_in")
        started = (((MIX_PIECES[1],) + F2_PIECES, *early), ((MIX_PIECES[0],), sa, ra))
        rx2 = lax.empty((3, RSA_ROWS, D), bf16)
        self.sb, self.rb, self.tx, self.acc, self.rx2, dx1b = _rsa_sums_and_send(
            started, late["wi"], sent["wo"], sent["w2"], rx1, rx2, dx1b)
        return dx1b

    def mix_ffn2_grads_parts(self, after):
        rx2 = _rsa_level2_wait(self.sb, self.rb, self.tx, self.rx2, after)
        return self.acc, rx2


def _reduce_scatter_ffn1_head(dw1, small_packed):
    pieces = G1_PIECES
    half = FS // 2
    hrows = len(pieces) * half
    nrows = 2 * hrows
    X_RELAY, Y_RELAY = range(2)

    def body(d1_ref, p_ref, forx_ref, fory_ref, own_ref, rx1_ref, relx_ref, rely_ref, tot_ref,
             own_buf, rx_buf, tx1, tx2, tx3, acc, sa, ra, sb, rb, lsem, pair, chips, small_tot, small_send, small_recv):
        x, y, c = _position()
        me, sib = (x, y, c), (x, y, 1 - c)
        xn, yn = (1 - x, y, c), (x, 1 - y, c)
        rel_chips = [(x, y), (1 - x, y), (x, 1 - y), (1 - x, 1 - y)]
        srcs = _weight_pieces(w1_ref=d1_ref)

        my_chip = 2 * x + y
        pair[c] = p_ref[...]
        swap = pltpu.make_async_remote_copy(
            src_ref=p_ref, dst_ref=pair.at[c], send_sem=small_send.at[0], recv_sem=small_recv.at[0],
            device_id=sib, device_id_type=MESH)
        swap.start()
        mine = pl.ds(pl.multiple_of(c * (SMALL_ROWS // 2), 8), SMALL_ROWS // 2)
        small = [pltpu.make_async_remote_copy(
            src_ref=chips.at[my_chip, mine, :], dst_ref=chips.at[my_chip, mine, :], send_sem=small_send.at[j],
            recv_sem=small_recv.at[j], device_id=(*rel_chips[j], c), device_id_type=MESH) for j in (1, 2, 3)]
        give = pltpu.make_async_remote_copy(
            src_ref=small_tot.at[mine, :], dst_ref=small_tot.at[mine, :], send_sem=small_send.at[4],
            recv_sem=small_recv.at[4], device_id=sib, device_id_type=MESH)

        def part(k, dev, hf):
            r = PIECE_ROWS[k]
            return srcs[k].at[pl.ds(pl.multiple_of(_dev_index(*dev) * r + hf * half, 16), half), :]

        def slot(ref, k, hf):
            return ref.at[pl.ds(hf * hrows + k * half, half), :]

        halves = [(k, hf) for hf in (0, 1) for k in pieces]

        for j in (3, 1, 2, 0):
            for k, hf in halves:
                pltpu.make_async_remote_copy(
                    src_ref=part(k, (*rel_chips[j], 1 - c), hf), dst_ref=slot(rx1_ref.at[j], k, hf),
                    send_sem=sa.at[j], recv_sem=ra.at[j], device_id=sib, device_id_type=MESH).start()

        def wait_a(j):
            return pltpu.make_async_remote_copy(src_ref=rx1_ref.at[j], dst_ref=rx1_ref.at[j], send_sem=sa.at[j],
                                                recv_sem=ra.at[j], device_id=me, device_id_type=MESH)

        def ici(rel, src, dst, to):
            return pltpu.make_async_remote_copy(src_ref=src, dst_ref=dst, send_sem=sb.at[rel], recv_sem=rb.at[rel],
                                                device_id=to, device_id_type=MESH)

        first, second = pl.ds(0, hrows), pl.ds(hrows, hrows)
        sends = {
            X_RELAY: ici(X_RELAY, tx3.at[first, :], relx_ref, xn),
            Y_RELAY: ici(Y_RELAY, tx3.at[second, :], rely_ref, yn),
        }

        swap.wait_recv()
        chips[my_chip] = pair[0] + pair[1]
        for cp in small:
            cp.start()

        def chip_sum(j, dst):
            loads = [pltpu.make_async_copy(part(k, (*rel_chips[j], c), hf), slot(own_buf, k, hf), lsem.at[0])
                     for k, hf in halves]
            for cp in loads:
                cp.start()
            wait_a(j).wait_recv()
            got = pltpu.make_async_copy(rx1_ref.at[j], rx_buf, lsem.at[1])
            got.start()
            pltpu.make_async_copy(rx_buf, rx_buf, lsem.at[0]).wait()
            got.wait()

            def add(i, carry):
                rows = pl.ds(pl.multiple_of(i * half, 16), half)
                tot = own_buf[rows, :].astype(f32) + rx_buf[rows, :].astype(f32)
                dst[rows, :] = tot.astype(dst.dtype)
                return carry

            lax.fori_loop(0, nrows // half, add, 0)

        def add_landed(landed, dst, rows0, nrows_):
            got = pltpu.make_async_copy(landed, rx_buf.at[pl.ds(0, nrows_), :], lsem.at[1])
            got.start()
            got.wait()

            def add(i, carry):
                src_rows = pl.ds(pl.multiple_of(i * half, 16), half)
                dst_rows = pl.ds(pl.multiple_of(rows0 + i * half, 16), half)
                dst[dst_rows, :] = (dst[dst_rows, :].astype(f32) + rx_buf[src_rows, :].astype(f32)).astype(dst.dtype)
                return carry

            lax.fori_loop(0, nrows_ // half, add, 0)

        chip_sum(3, tx3)
        sends[X_RELAY].start()
        sends[Y_RELAY].start()
        chip_sum(1, tx1)
        chip_sum(2, tx2)
        chip_sum(0, acc)
        own_out = pltpu.make_async_copy(acc, own_ref, lsem.at[0])
        own_out.start()
        sends[X_RELAY].wait_recv()
        add_landed(relx_ref, tx2, 0, hrows)
        sends[Y_RELAY].wait_recv()
        add_landed(rely_ref, tx1, hrows, hrows)
        own_out.wait()
        outs = [pltpu.make_async_copy(tx1, forx_ref, lsem.at[0]), pltpu.make_async_copy(tx2, fory_ref, lsem.at[1])]
        for cp in outs:
            cp.start()
        for cp in outs:
            cp.wait()
        for cp in small:
            cp.wait_recv()
        small_tot[mine, :] = (chips[0, mine, :] + chips[1, mine, :]) + (chips[2, mine, :] + chips[3, mine, :])
        give.start()
        give.wait_recv()
        tot = small_tot[...]
        tot_ref[...] = tot
        loss = jnp.sum(tot[LOSS_ROW:LOSS_ROW + 1, :], axis=-1, keepdims=True)
        tot_ref[LOSS_ROW:LOSS_ROW + 1, :] = jnp.broadcast_to(loss, (1, 128))
        for j in range(4):
            wait_a(j).wait_send()
        for cp in sends.values():
            cp.wait_send()
        swap.wait_send()
        for cp in small + [give]:
            cp.wait_send()

    hbm = pl.BlockSpec(memory_space=pl.ANY)
    vm = pl.BlockSpec(memory_space=pltpu.VMEM)
    outs = pl.pallas_call(
        body, in_specs=[hbm, vm], out_specs=[hbm] * 6 + [vm],
        out_shape=(SDS((nrows, D), bf16), SDS((nrows, D), bf16), SDS((nrows, D), f32), SDS((4, nrows, D), bf16),
                   SDS((hrows, D), bf16), SDS((hrows, D), bf16), SDS((SMALL_ROWS, 128), f32)),
        scratch_shapes=[pltpu.VMEM((nrows, D), bf16), pltpu.VMEM((nrows, D), bf16),
                        pltpu.VMEM((nrows, D), bf16), pltpu.VMEM((nrows, D), bf16), pltpu.VMEM((nrows, D), bf16),
                        pltpu.VMEM((nrows, D), f32),
                        pltpu.SemaphoreType.DMA((4,)), pltpu.SemaphoreType.DMA((4,)),
                        pltpu.SemaphoreType.DMA((2,)), pltpu.SemaphoreType.DMA((2,)), pltpu.SemaphoreType.DMA((2,)),
                        pltpu.VMEM((2, SMALL_ROWS, 128), f32), pltpu.VMEM((4, SMALL_ROWS, 128), f32),
                        pltpu.VMEM((SMALL_ROWS, 128), f32),
                        pltpu.SemaphoreType.DMA((5,)), pltpu.SemaphoreType.DMA((5,))],
        compiler_params=pltpu.CompilerParams(has_side_effects=True, vmem_limit_bytes=VMEM_LIMIT_V7X),
        name="reduce_scatter_ffn1_head")(dw1, small_packed)
    return outs[0], outs[1], outs[2], outs[-1]


def _rs1_tail_start(for_x, for_y, from_x, from_y, thru):
    def body(fx_ref, fy_ref, lx_ref, ly_ref, thru_ref, ssem, rsem, fx_o, fy_o, lx_o, ly_o, thru_o):
        x, y, c = _position()
        pltpu.make_async_remote_copy(src_ref=fx_ref, dst_ref=lx_ref, send_sem=ssem.at[0], recv_sem=rsem.at[0],
                                     device_id=(1 - x, y, c), device_id_type=MESH).start()
        pltpu.make_async_remote_copy(src_ref=fy_ref, dst_ref=ly_ref, send_sem=ssem.at[1], recv_sem=rsem.at[1],
                                     device_id=(x, 1 - y, c), device_id_type=MESH).start()

    dma = pltpu.SemaphoreType.DMA
    arrs = (for_x, for_y, from_x, from_y, thru)
    return pl.pallas_call(
        body, name="rs1_tail_start", out_shape=(dma((2,)), dma((2,))) + tuple(_hbm_like(a) for a in arrs),
        in_specs=(HBM_SPEC,) * 5, out_specs=(SEM_SPEC,) * 2 + (HBM_SPEC,) * 5,
        input_output_aliases={0: 2, 1: 3, 2: 4, 3: 5, 4: 6},
        compiler_params=pltpu.CompilerParams(has_side_effects=SPLIT_EFFECT),
    )(*[_in_hbm(a) for a in arrs])


def _rs1_tail_wait(ssem, rsem, for_x, for_y, from_x, from_y, after):
    def body(fx_ref, fy_ref, lx_ref, ly_ref, ssem_ref, rsem_ref, after_ref, lx_o, ly_o):
        x, y, c = _position()
        for j, (src, dst) in enumerate(((fx_ref, lx_ref), (fy_ref, ly_ref))):
            d = pltpu.make_async_remote_copy(src_ref=src, dst_ref=dst, send_sem=ssem_ref.at[j], recv_sem=rsem_ref.at[j],
                                             device_id=(x, y, c), device_id_type=MESH)
            d.wait_recv()
            d.wait_send()

    return pl.pallas_call(
        body, name="rs1_tail_wait", out_shape=(_hbm_like(from_x), _hbm_like(from_y)),
        in_specs=(HBM_SPEC,) * 4 + (SEM_SPEC, SEM_SPEC, ANY_SPEC), out_specs=(HBM_SPEC, HBM_SPEC),
        input_output_aliases={2: 0, 3: 1},
        compiler_params=pltpu.CompilerParams(has_side_effects=SPLIT_EFFECT),
    )(for_x, for_y, from_x, from_y, ssem, rsem, after)


RSA_PIECES = MIX_PIECES + F2_PIECES
RSA_ROWS = _group_rows(RSA_PIECES)
RSA_OFF = {k: PIECE_OFF[k] - PIECE_OFF[RSA_PIECES[0]] for k in RSA_PIECES}
RSA_BLOCK = 192


def _rsa_rows(ref, k):
    return ref.at[pl.ds(RSA_OFF[k], PIECE_ROWS[k]), :]


def _rsa_level1_start(grads, rx1, thru, name):
    keys = sorted(grads)
    n = len(keys)

    def body(*refs):
        srcs = _weight_pieces(**{k + "_ref": ref for k, ref in zip(keys, refs[:n])})
        rx1_ref, sa, ra = refs[n], refs[n + 2], refs[n + 3]
        x, y, c = _position()
        for j, chip in enumerate([(x, y), (1 - x, y), (x, 1 - y), (1 - x, 1 - y)]):
            for k in sorted(srcs):
                pltpu.make_async_remote_copy(
                    src_ref=_block_rows(srcs, k, (*chip, 1 - c)), dst_ref=_rsa_rows(rx1_ref.at[j], k),
                    send_sem=sa.at[j], recv_sem=ra.at[j], device_id=(x, y, 1 - c), device_id_type=MESH).start()

    dma = pltpu.SemaphoreType.DMA
    arrs = tuple(grads[k] for k in keys) + (rx1, thru)
    outs = pl.pallas_call(
        body, name=name, out_shape=(dma((4,)), dma((4,))) + tuple(_hbm_like(a) for a in arrs),
        in_specs=(HBM_SPEC,) * len(arrs), out_specs=(SEM_SPEC,) * 2 + (HBM_SPEC,) * len(arrs),
        input_output_aliases={i: i + 2 for i in range(len(arrs))},
        compiler_params=pltpu.CompilerParams(has_side_effects=SPLIT_EFFECT),
    )(*[_in_hbm(a) for a in arrs])
    return outs[0], outs[1], dict(zip(keys, outs[2:2 + n])), outs[2 + n], outs[3 + n]


def _rsa_sums_and_send(started, dwint, dwout, dw2, rx1, rx2, thru):
    nblk = RSA_ROWS // RSA_BLOCK
    nstart = len(started)

    def body(*refs):
        di_ref, do_ref, d2_ref, rx1_ref, rx2_ref = refs[:5]
        l1_sems = refs[6:6 + 2 * nstart]
        sb, rb, tx_ref, acc_ref = refs[6 + 2 * nstart:10 + 2 * nstart]
        own_buf, rx_buf, tx_buf, acc_buf, in_sems, out_sems = refs[13 + 2 * nstart:]
        x, y, c = _position()
        srcs = _weight_pieces(wi_ref=di_ref, wo_ref=do_ref, w2_ref=d2_ref)
        chips = [(x, y), (1 - x, y), (x, 1 - y), (1 - x, 1 - y)]

        for g, (pieces, _, _) in enumerate(started):
            ssem, rsem = l1_sems[2 * g], l1_sems[2 * g + 1]
            for j in range(4):
                rows = rx1_ref.at[j, pl.ds(RSA_OFF[pieces[0]], _group_rows(pieces)), :]
                d = pltpu.make_async_remote_copy(src_ref=rows, dst_ref=rows, send_sem=ssem.at[j], recv_sem=rsem.at[j],
                                                 device_id=(x, y, c), device_id_type=MESH)
                d.wait_recv()
                d.wait_send()

        def start_loads(j):
            s = j % 2
            for k in RSA_PIECES:
                pltpu.make_async_copy(_block_rows(srcs, k, (*chips[j], c)), _rsa_rows(own_buf.at[s], k),
                                      in_sems.at[2 * s]).start()
            pltpu.make_async_copy(rx1_ref.at[j], rx_buf.at[s], in_sems.at[2 * s + 1]).start()

        def wait_loads(j):
            s = j % 2
            pltpu.make_async_copy(rx1_ref.at[j], own_buf.at[s], in_sems.at[2 * s]).wait()
            pltpu.make_async_copy(rx1_ref.at[j], rx_buf.at[s], in_sems.at[2 * s + 1]).wait()

        def store(j):
            if j == 0:
                return pltpu.make_async_copy(acc_buf, acc_ref, out_sems.at[2])
            return pltpu.make_async_copy(tx_buf.at[j % 2], tx_ref.at[j - 1], out_sems.at[j % 2])

        def send(j):
            return pltpu.make_async_remote_copy(src_ref=tx_ref.at[j - 1], dst_ref=rx2_ref.at[j - 1], send_sem=sb.at[j - 1],
                                                recv_sem=rb.at[j - 1], device_id=(*chips[j], c), device_id_type=MESH)

        start_loads(0)
        for j in range(4):
            s = j % 2
            if j + 1 < 4:
                start_loads(j + 1)
            wait_loads(j)
            if j == 3:
                store(1).wait()
                send(1).start()

            def add(i, carry, j=j, s=s):
                rows = pl.ds(pl.multiple_of(i * RSA_BLOCK, 16), RSA_BLOCK)
                tot = own_buf[s, rows, :].astype(f32) + rx_buf[s, rows, :].astype(f32)
                if j == 0:
                    acc_buf[rows, :] = tot
                else:
                    tx_buf[s, rows, :] = tot.astype(bf16)
                return carry

            lax.fori_loop(0, nblk, add, 0)
            store(j).start()
        store(0).wait()
        for j in (2, 3):
            store(j).wait()
            send(j).start()

    dma = pltpu.SemaphoreType.DMA
    passed = (rx1, rx2, thru)
    outs = pl.pallas_call(
        body, name="rsa_sums_and_send",
        in_specs=(HBM_SPEC,) * 6 + (SEM_SPEC,) * (2 * nstart),
        out_specs=(SEM_SPEC,) * 2 + (HBM_SPEC,) * 5,
        out_shape=(dma((3,)), dma((3,)), pltpu.HBM((3, RSA_ROWS, D), bf16), pltpu.HBM((RSA_ROWS, D), f32))
        + tuple(_hbm_like(a) for a in passed),
        input_output_aliases={3: 4, 4: 5, 5: 6},
        scratch_shapes=[pltpu.VMEM((2, RSA_ROWS, D), bf16), pltpu.VMEM((2, RSA_ROWS, D), bf16),
                        pltpu.VMEM((2, RSA_ROWS, D), bf16), pltpu.VMEM((RSA_ROWS, D), f32),
                        dma((4,)), dma((3,))],
        compiler_params=pltpu.CompilerParams(has_side_effects=SPLIT_EFFECT, vmem_limit_bytes=VMEM_LIMIT_V7X),
    )(*[_in_hbm(a) for a in (dwint, dwout, dw2) + passed], *[sem for _, sa, ra in started for sem in (sa, ra)])
    sb, rb, tx, acc, _, rx2, thru = outs
    return sb, rb, tx, acc, rx2, thru


def _rsa_level2_wait(sb, rb, tx, rx2, after):
    def body(tx_ref, rx2_ref, sb_ref, rb_ref, after_ref, rx2_o):
        x, y, c = _position()
        for j in range(3):
            d = pltpu.make_async_remote_copy(src_ref=tx_ref.at[j], dst_ref=rx2_ref.at[j], send_sem=sb_ref.at[j],
                                             recv_sem=rb_ref.at[j], device_id=(x, y, c), device_id_type=MESH)
            d.wait_recv()
            d.wait_send()

    return pl.pallas_call(
        body, name="rsa_level2_wait", out_shape=_hbm_like(rx2),
        in_specs=(HBM_SPEC, HBM_SPEC, SEM_SPEC, SEM_SPEC, ANY_SPEC), out_specs=HBM_SPEC,
        input_output_aliases={1: 0},
        compiler_params=pltpu.CompilerParams(has_side_effects=SPLIT_EFFECT),
    )(tx, rx2, sb, rb, after)


def _adamw_math(w, g, m, v):
    m = ADAM_B1 * m + (1.0 - ADAM_B1) * g
    v = ADAM_B2 * v + (1.0 - ADAM_B2) * (g * g)
    m_hat = m / (1.0 - ADAM_B1 ** ADAM_STEP)
    v_hat = v / (1.0 - ADAM_B2 ** ADAM_STEP)
    delta = -ADAM_LR * (m_hat / (jnp.sqrt(v_hat) + ADAM_EPS) + ADAM_WD * w)
    return delta, m, v


def _adamw_big(pieces, ws, ms, vs, own, landed, name):
    npiece = len(pieces)
    nland = sum(a.shape[0] if a.ndim == 3 else 1 for a in landed)
    rmax = max(PIECE_ROWS[k] for k in pieces)
    half = FS // 2

    def segments(k):
        if k in G1_PIECES:
            return [(hf * len(G1_PIECES) * half + k * half, hf * half, half) for hf in (0, 1)]
        return [(RSA_OFF[k], 0, PIECE_ROWS[k])]

    def body(*refs):
        ins = (refs[0:npiece], refs[npiece:2 * npiece], refs[2 * npiece:3 * npiece])
        own_ref = refs[3 * npiece]
        nin = 3 * npiece + 1 + len(landed)
        land_refs = []
        for ref, a in zip(refs[3 * npiece + 1:nin], landed):
            land_refs += [ref.at[j] for j in range(a.shape[0])] if a.ndim == 3 else [ref]
        out_refs = refs[nin:nin + 4 * npiece]
        inb, landb, outb, in_sems, land_sems, out_sems = refs[nin + 4 * npiece:]

        def loads(i):
            s, k = i % 2, pieces[i]
            r = PIECE_ROWS[k]
            cps = [pltpu.make_async_copy(ins[q][i].at[0], inb.at[s, q, pl.ds(0, r), :], in_sems.at[4 * s + q])
                   for q in range(3)]
            waits = list(cps)
            for src0, dst0, n in segments(k):
                cps.append(pltpu.make_async_copy(own_ref.at[pl.ds(src0, n), :], inb.at[s, 3, pl.ds(dst0, n), :],
                                                 in_sems.at[4 * s + 3]))
                for p in range(nland):
                    cps.append(pltpu.make_async_copy(land_refs[p].at[pl.ds(src0, n), :],
                                                     landb.at[s, p, pl.ds(dst0, n), :], land_sems.at[nland * s + p]))
            own_rows = inb.at[s, 3, pl.ds(0, r), :]
            waits.append(pltpu.make_async_copy(own_rows, own_rows, in_sems.at[4 * s + 3]))
            for p in range(nland):
                rows = landb.at[s, p, pl.ds(0, r), :]
                waits.append(pltpu.make_async_copy(rows, rows, land_sems.at[nland * s + p]))
            return cps, waits

        def stores(i):
            s, r = i % 2, PIECE_ROWS[pieces[i]]
            return [pltpu.make_async_copy(outb.at[s, q, pl.ds(0, r), :], out_refs[q * npiece + i].at[0],
                                          out_sems.at[4 * s + q]) for q in range(4)]

        for cp in loads(0)[0]:
            cp.start()
        for i in range(npiece):
            s, r = i % 2, PIECE_ROWS[pieces[i]]
            if i + 1 < npiece:
                for cp in loads(i + 1)[0]:
                    cp.start()
            for cp in loads(i)[1]:
                cp.wait()
            if i >= 2:
                for cp in stores(i - 2):
                    cp.wait()
            g = inb[s, 3, 0:r, :]
            for p in range(nland):
                g = g + landb[s, p, 0:r, :].astype(f32)
            d, nm, nv = _adamw_math(inb[s, 0, 0:r, :], g, inb[s, 1, 0:r, :], inb[s, 2, 0:r, :])
            outb[s, 0, 0:r, :] = g
            outb[s, 1, 0:r, :] = d
            outb[s, 2, 0:r, :] = nm
            outb[s, 3, 0:r, :] = nv
            for cp in stores(i):
                cp.start()
        for i in range(max(npiece - 2, 0), npiece):
            for cp in stores(i):
                cp.wait()

    hbm = pl.BlockSpec(memory_space=pl.ANY)
    outs = pl.pallas_call(
        body, in_specs=[hbm] * (3 * npiece + 1 + len(landed)), out_specs=[hbm] * (4 * npiece),
        out_shape=tuple(SDS(w.shape, f32) for _ in range(4) for w in ws),
        scratch_shapes=[pltpu.VMEM((2, 4, rmax, D), f32), pltpu.VMEM((2, nland, rmax, D), bf16),
                        pltpu.VMEM((2, 4, rmax, D), f32),
                        pltpu.SemaphoreType.DMA((8,)), pltpu.SemaphoreType.DMA((2 * nland,)),
                        pltpu.SemaphoreType.DMA((8,))],
        compiler_params=_cparams(None, VMEM_LIMIT_V7X), name=name)(*ws, *ms, *vs, own, *landed)
    return [list(outs[q * npiece:(q + 1) * npiece]) for q in range(4)]


def _adamw_small(ws, ms, vs, gs, name):
    n = len(ws)

    def body(*refs):
        w_refs, m_refs, v_refs, g_refs = refs[0:n], refs[n:2 * n], refs[2 * n:3 * n], refs[3 * n:4 * n]
        outs = refs[4 * n:]
        for i in range(n):
            d, nm, nv = _adamw_math(w_refs[i][...], g_refs[i][...], m_refs[i][...], v_refs[i][...])
            outs[i][...] = d
            outs[n + i][...] = nm
            outs[2 * n + i][...] = nv

    outs = pl.pallas_call(
        body, out_shape=tuple(SDS(w.shape, f32) for _ in range(3) for w in ws), name=name)(*ws, *ms, *vs, *gs)
    return [list(outs[q * n:(q + 1) * n]) for q in range(3)]


WEIGHTS = ("ffn1_norm", "ffn1_w_gate", "ffn1_w_up", "ffn1_w_down", "mix_norm", "w_in", "q_norm", "k_norm",
           "attn_sinks", "rel_bias", "pool_w", "pool_scale", "w_out", "ffn2_norm", "ffn2_w_gate", "ffn2_w_up",
           "ffn2_w_down")
BIG = (("ffn1_w_gate", True), ("ffn1_w_up", True), ("ffn1_w_down", False), ("w_in", True), ("w_out", False),
       ("ffn2_w_gate", True), ("ffn2_w_up", True), ("ffn2_w_down", False))


def kernel(x, ffn1_norm, ffn1_w_gate, ffn1_w_up, ffn1_w_down, mix_norm, w_in, q_norm, k_norm, attn_sinks, rel_bias, pool_w, pool_scale, w_out, ffn2_norm, ffn2_w_gate, ffn2_w_up, ffn2_w_down, loss_target, m_ffn1_norm, m_ffn1_w_gate, m_ffn1_w_up, m_ffn1_w_down, m_mix_norm, m_w_in, m_q_norm, m_k_norm, m_attn_sinks, m_rel_bias, m_pool_w, m_pool_scale, m_w_out, m_ffn2_norm, m_ffn2_w_gate, m_ffn2_w_up, m_ffn2_w_down, v_ffn1_norm, v_ffn1_w_gate, v_ffn1_w_up, v_ffn1_w_down, v_mix_norm, v_w_in, v_q_norm, v_k_norm, v_attn_sinks, v_rel_bias, v_pool_w, v_pool_scale, v_w_out, v_ffn2_norm, v_ffn2_w_gate, v_ffn2_w_up, v_ffn2_w_down):
    args = dict(locals())
    w = {n: args[n] for n in WEIGHTS}
    m = {n: args["m_" + n] for n in WEIGHTS}
    v = {n: args["v_" + n] for n in WEIGHTS}

    as_rows = lambda a, tr: jnp.swapaxes(a, 1, 2) if tr else a
    shard = jnp.concatenate([as_rows(w[n], tr)[0].astype(bf16) for n, tr in BIG], axis=0)
    exchanges = _GatheredWeights(shard, x[0], ffn1_norm)
    gx, (dw1, _, _, _), small = _local_step(
        x[0], loss_target[0], exchanges, ffn1_norm, mix_norm, ffn2_norm, q_norm, k_norm, attn_sinks,
        rel_bias, pool_w[0], pool_scale)

    nrows1 = len(G1_PIECES) * FS
    for_x, for_y, own1, small_tot = _reduce_scatter_ffn1_head(dw1, _pack_small(small))
    ssem, rsem, for_x, for_y, from_x, from_y, small_tot = _rs1_tail_start(
        for_x, for_y, lax.empty((nrows1, D), bf16), lax.empty((nrows1, D), bf16), small_tot)
    own_rest, landed_rest = exchanges.mix_ffn2_grads_parts(small_tot)

    grads, deltas, new_m, new_v = {}, {}, {}, {}
    rest = [k for k in range(len(BIG)) if k not in G1_PIECES]
    rows_of = lambda t, ks: [as_rows(t[BIG[k][0]], BIG[k][1]) for k in ks]
    rest_out = _adamw_big(rest, rows_of(w, rest), rows_of(m, rest), rows_of(v, rest), own_rest, [landed_rest],
                          "adamw_rest")
    from_x, from_y = _rs1_tail_wait(ssem, rsem, for_x, for_y, from_x, from_y, rest_out[0][0])
    ffn1 = list(G1_PIECES)
    ffn1_out = _adamw_big(ffn1, rows_of(w, ffn1), rows_of(m, ffn1), rows_of(v, ffn1), own1, [from_x, from_y],
                          "adamw_ffn1")
    for ks, out in ((rest, rest_out), (ffn1, ffn1_out)):
        for i, k in enumerate(ks):
            n, tr = BIG[k]
            grads[n], deltas[n], new_m[n], new_v[n] = [as_rows(o[i], tr) for o in out]
    small_names = [n for n in SMALL_NAMES if n != "loss"]
    for n in small_names:
        grads[n] = _unpack_small(small_tot, n)
    ds, nms, nvs = _adamw_small([w[n] for n in small_names], [m[n] for n in small_names], [v[n] for n in small_names],
                                [grads[n] for n in small_names], "adamw_small")
    for i, n in enumerate(small_names):
        deltas[n], new_m[n], new_v[n] = ds[i], nms[i], nvs[i]
    loss = small_tot[LOSS_ROW, 0]
    return (loss, gx[None], *[grads[n] for n in WEIGHTS], *[deltas[n] for n in WEIGHTS],
            *[new_m[n] for n in WEIGHTS], *[new_v[n] for n in WEIGHTS])
```

```python
import jax
import jax.numpy as jnp
import numpy as np
from jax import lax
from jax.experimental import pallas as pl
from jax.experimental.pallas import tpu as pltpu

f32, bf16, i32 = jnp.float32, jnp.bfloat16, jnp.int32
SDS = jax.ShapeDtypeStruct

D = 1024
F = 2816
HD = 64
NH = 8
NKV = 2
GQA = NH // NKV
DATTN = NH * HD
DKV = NKV * HD
DPOOL = 512
POOL_WINDOWS = (2, 4, 8, 16)
PGD = DPOOL // len(POOL_WINDOWS)
DIN = DATTN + 2 * DKV + DPOOL
DMIX = DATTN + DPOOL
BLK = 128
NBUCK = 32
MAX_DISTANCE = 128
EPS = 1e-6
NEG = -1e30
SCALE = HD ** -0.5

ADAM_LR, ADAM_B1, ADAM_B2, ADAM_EPS, ADAM_WD, ADAM_STEP = 0.001, 0.9, 0.999, 1e-08, 0.01, 10

NDEV = 8
FS = F // NDEV
INS = DIN // NDEV
OUTS = DMIX // NDEV
PIECE_ROWS = (FS, FS, FS, INS, OUTS, FS, FS, FS)
PIECE_OFF = tuple(int(v) for v in np.cumsum((0,) + PIECE_ROWS[:-1]))
PACK_ROWS = sum(PIECE_ROWS)

VMEM_LIMIT_V7X = 56 * 1024 * 1024

MESH = pl.DeviceIdType.MESH


def _cparams(sem=None, vmem=None):
    return pltpu.CompilerParams(dimension_semantics=sem, vmem_limit_bytes=vmem)


def _nt(a, b):
    return lax.dot_general(a, b, (((1,), (1,)), ((), ())), preferred_element_type=f32)


def _tn(a, b):
    return lax.dot_general(a, b, (((0,), (0,)), ((), ())), preferred_element_type=f32)


def _nn(a, b):
    return jnp.dot(a, b, preferred_element_type=f32)


def _sigmoid(x):
    return 1.0 / (1.0 + jnp.exp(-x))


def _norm_fwd(x, g, name):
    T = x.shape[0]
    tm = min(512, T)

    def body(x_ref, g_ref, h_ref):
        xv = x_ref[...]
        r = lax.rsqrt(jnp.mean(xv * xv, axis=-1, keepdims=True) + EPS)
        h_ref[...] = (xv * r * g_ref[...]).astype(bf16)

    return pl.pallas_call(
        body, grid=(T // tm,),
        in_specs=[pl.BlockSpec((tm, D), lambda i: (i, 0)), pl.BlockSpec((1, D), lambda i: (0, 0))],
        out_specs=pl.BlockSpec((tm, D), lambda i: (i, 0)),
        out_shape=SDS((T, D), bf16), name=name)(x, g)


FFN_ROW_CHUNK = 256


def _ffn_tiles(T):
    return min(1024, T), 256


def _ffn_fwd(h, w, x, target, next_gain, name):
    T = h.shape[0]
    tm, tf = _ffn_tiles(T)
    nf = F // tf
    with_loss = target is not None
    assert with_loss != (next_gain is not None)

    def body(*refs):
        if with_loss:
            h_ref, w_ref, x_hbm, t_hbm, xo_ref, g_ref, u_ref, dyb_ref, loss_ref, tbuf, sem = refs
        else:
            h_ref, w_ref, x_hbm, gain_ref, xo_ref, g_ref, u_ref, hn_ref, sem = refs
        fi = pl.program_id(0)

        @pl.when(fi == 0)
        def _():
            cp = pltpu.make_async_copy(x_hbm, xo_ref, sem)
            cp.start()
            cp.wait()

        wgu = w_ref[0:2].reshape(2 * tf, D)
        for r in range(0, T, tm):
            rows = slice(r, r + tm)
            gu = _nt(h_ref[rows, :], wgu)
            gate, up = gu[:, :tf], gu[:, tf:]
            act = gate * _sigmoid(gate) * up
            g_ref[0, rows, :] = gate.astype(bf16)
            u_ref[0, rows, :] = up.astype(bf16)
            xo_ref[rows, :] += _nn((0.5 * act).astype(bf16), w_ref[2])

        if with_loss:
            @pl.when(fi == nf - 1)
            def _():
                lanes = jnp.zeros((1, 128), f32)
                for r in range(0, T, tm):
                    rows = slice(r, r + tm)
                    cp = pltpu.make_async_copy(t_hbm.at[pl.ds(r, tm), :], tbuf, sem)
                    cp.start()
                    cp.wait()
                    e = xo_ref[rows, :] - tbuf[...]
                    dy = e * (1.0 / D)
                    xo_ref[rows, :] = dy
                    dyb_ref[rows, :] = (0.5 * dy).astype(bf16)
                    col = jnp.sum(e * e, axis=0, keepdims=True) * (0.5 / D)
                    for k in range(D // 128):
                        lanes = lanes + col[:, 128 * k:128 * (k + 1)]
                loss_ref[...] = lanes
        else:
            @pl.when(fi == nf - 1)
            def _():
                for r in range(0, T, FFN_ROW_CHUNK):
                    rows = slice(r, r + FFN_ROW_CHUNK)
                    xv = xo_ref[rows, :]
                    rstd = lax.rsqrt(jnp.mean(xv * xv, axis=-1, keepdims=True) + EPS)
                    hn_ref[rows, :] = (xv * rstd * gain_ref[...]).astype(bf16)

    tok = pl.BlockSpec((T, D), lambda f: (0, 0))
    act_spec = pl.BlockSpec((1, T, tf), lambda f: (f, 0, 0))
    hbm = pl.BlockSpec(memory_space=pl.ANY)
    in_specs = [tok, pl.BlockSpec((3, tf, D), lambda f: (0, f, 0)), hbm]
    out_specs = [tok, act_spec, act_spec]
    out_shape = [SDS((T, D), f32), SDS((nf, T, tf), bf16), SDS((nf, T, tf), bf16)]
    scratch = [pltpu.SemaphoreType.DMA]
    args = [h, w, x]
    if with_loss:
        in_specs.append(hbm)
        args.append(target)
        out_specs += [tok, pl.BlockSpec((1, 128), lambda f: (0, 0))]
        out_shape += [SDS((T, D), bf16), SDS((1, 128), f32)]
        scratch = [pltpu.VMEM((tm, D), f32)] + scratch
    else:
        in_specs.append(pl.BlockSpec((1, D), lambda f: (0, 0)))
        args.append(next_gain)
        out_specs.append(tok)
        out_shape.append(SDS((T, D), bf16))
    return pl.pallas_call(
        body, grid=(nf,), in_specs=in_specs, out_specs=out_specs, out_shape=tuple(out_shape), scratch_shapes=scratch,
        compiler_params=_cparams(("arbitrary",), VMEM_LIMIT_V7X), name=name)(*args)


def _ffn_bwd(dob, h, gate, up, w, norm, name):
    x, g, dres = norm
    T = h.shape[0]
    _, tf = _ffn_tiles(T)
    nf = F // tf
    tm = min(512, T)
    nchunk = T // tm

    def body(do_hbm, h_hbm, g_ref, u_ref, w_ref, x_hbm, gain_ref, dr_hbm, dx_hbm, dxb_hbm, dg_ref, dw_ref,
             do_v, h_v, dh_acc, dgu_s, act_s, xbuf, rbuf, obuf, obb, sems, in_sems, out_sems):
        fi = pl.program_id(0)

        @pl.when(fi == 0)
        def _():
            loads = [pltpu.make_async_copy(do_hbm, do_v, sems.at[0]), pltpu.make_async_copy(h_hbm, h_v, sems.at[1])]
            for cp in loads:
                cp.start()
            dh_acc[...] = jnp.zeros_like(dh_acc)
            for cp in loads:
                cp.wait()

        wgu = w_ref[0:2].reshape(2 * tf, D)
        for r in range(0, T, FFN_ROW_CHUNK):
            rows = slice(r, r + FFN_ROW_CHUNK)
            dov = do_v[rows, :]
            gv = g_ref[0, rows, :].astype(f32)
            uv = u_ref[0, rows, :].astype(f32)
            sg = _sigmoid(gv)
            sil = gv * sg
            dact = _nt(dov, w_ref[2])
            dup = dact * sil
            dgate = dact * uv * (sg * (1.0 + gv * (1.0 - sg)))
            dgu = jnp.concatenate([dgate.astype(bf16), dup.astype(bf16)], axis=1)
            dgu_s[rows, :] = dgu
            act_s[rows, :] = (sil * uv).astype(bf16)
            dh_acc[rows, :] += _nn(dgu, wgu)
        dw_ref[0:2] = _tn(dgu_s[...], h_v[...]).reshape(2, tf, D).astype(bf16)
        dw_ref[2] = _tn(act_s[...], do_v[...]).astype(bf16)

        @pl.when(fi == nf - 1)
        def _():
            def loads(i):
                s, rows = i % 2, pl.ds(i * tm, tm)
                return [pltpu.make_async_copy(x_hbm.at[rows, :], xbuf.at[s], in_sems.at[2 * s]),
                        pltpu.make_async_copy(dr_hbm.at[rows, :], rbuf.at[s], in_sems.at[2 * s + 1])]

            def stores(i):
                s, rows = i % 2, pl.ds(i * tm, tm)
                return [pltpu.make_async_copy(obuf.at[s], dx_hbm.at[rows, :], out_sems.at[2 * s]),
                        pltpu.make_async_copy(obb.at[s], dxb_hbm.at[rows, :], out_sems.at[2 * s + 1])]

            for cp in loads(0):
                cp.start()
            dg = jnp.zeros((1, D), f32)
            for i in range(nchunk):
                s = i % 2
                if i + 1 < nchunk:
                    for cp in loads(i + 1):
                        cp.start()
                for cp in loads(i):
                    cp.wait()
                if i >= 2:
                    for cp in stores(i - 2):
                        cp.wait()
                xv = xbuf[s]
                rstd = lax.rsqrt(jnp.mean(xv * xv, axis=-1, keepdims=True) + EPS)
                xh = xv * rstd
                dhv = dh_acc[i * tm:(i + 1) * tm, :]
                dxh = dhv * gain_ref[...]
                dx = rbuf[s] + rstd * (dxh - xh * jnp.mean(dxh * xh, axis=-1, keepdims=True))
                obuf[s] = dx
                obb[s] = dx.astype(bf16)
                dg = dg + jnp.sum(dhv * xh, axis=0, keepdims=True)
                for cp in stores(i):
                    cp.start()
            dg_ref[...] = dg
            for i in range(max(nchunk - 2, 0), nchunk):
                for cp in stores(i):
                    cp.wait()

    act_spec = pl.BlockSpec((1, T, tf), lambda f: (f, 0, 0))
    wspec = pl.BlockSpec((3, tf, D), lambda f: (0, f, 0))
    vec = pl.BlockSpec((1, D), lambda f: (0, 0))
    hbm = pl.BlockSpec(memory_space=pl.ANY)
    return pl.pallas_call(
        body, grid=(nf,),
        in_specs=[hbm, hbm, act_spec, act_spec, wspec, hbm, vec, hbm],
        out_specs=[hbm, hbm, vec, wspec],
        out_shape=(SDS((T, D), f32), SDS((T, D), bf16), SDS((1, D), f32), SDS((3, F, D), bf16)),
        scratch_shapes=[pltpu.VMEM((T, D), bf16), pltpu.VMEM((T, D), bf16), pltpu.VMEM((T, D), f32),
                        pltpu.VMEM((T, 2 * tf), bf16), pltpu.VMEM((T, tf), bf16),
                        pltpu.VMEM((2, tm, D), f32), pltpu.VMEM((2, tm, D), f32), pltpu.VMEM((2, tm, D), f32),
                        pltpu.VMEM((2, tm, D), bf16),
                        pltpu.SemaphoreType.DMA((2,)), pltpu.SemaphoreType.DMA((4,)), pltpu.SemaphoreType.DMA((4,))],
        compiler_params=_cparams(("arbitrary",), VMEM_LIMIT_V7X), name=name)(dob, h, gate, up, w, x, g, dres)


def _in_proj_fwd(h, wint, name):
    T = h.shape[0]
    tm = min(512, T)

    def body(h_ref, w_ref, z_ref):
        z_ref[...] = _nt(h_ref[...], w_ref[...])

    return pl.pallas_call(
        body, grid=(T // tm,),
        in_specs=[pl.BlockSpec((tm, D), lambda i: (i, 0)), pl.BlockSpec((DIN, D), lambda i: (0, 0))],
        out_specs=pl.BlockSpec((tm, DIN), lambda i: (i, 0)),
        out_shape=SDS((T, DIN), f32), name=name)(h, wint)


def _in_proj_bwd(dz, wint, h, norm, out_scale, name):
    x, g, dres = norm
    T = h.shape[0]
    tm = min(512, T)
    nt = T // tm

    def body(dz_ref, w_ref, h_ref, x_ref, g_ref, dr_ref, dx_ref, dxb_ref, dg_ref, dw_ref, acc):
        i = pl.program_id(0)
        dzb = dz_ref[...].astype(bf16)
        dhv = _nn(dzb, w_ref[...])
        part = _tn(dzb, h_ref[...])
        xv = x_ref[...]
        rstd = lax.rsqrt(jnp.mean(xv * xv, axis=-1, keepdims=True) + EPS)
        xh = xv * rstd
        dxh = dhv * g_ref[...]
        dx = dr_ref[...] + rstd * (dxh - xh * jnp.mean(dxh * xh, axis=-1, keepdims=True))
        dx_ref[...] = dx
        dxb_ref[...] = (out_scale * dx).astype(bf16)
        dg = jnp.sum(dhv * xh, axis=0, keepdims=True)

        @pl.when(i == 0)
        def _():
            acc[...] = part
            dg_ref[...] = dg

        @pl.when(i > 0)
        def _():
            acc[...] += part
            dg_ref[...] += dg

        @pl.when(i == nt - 1)
        def _():
            dw_ref[...] = acc[...].astype(bf16)

    wspec = pl.BlockSpec((DIN, D), lambda i: (0, 0))
    tok = pl.BlockSpec((tm, D), lambda i: (i, 0))
    vec = pl.BlockSpec((1, D), lambda i: (0, 0))
    return pl.pallas_call(
        body, grid=(nt,),
        in_specs=[pl.BlockSpec((tm, DIN), lambda i: (i, 0)), wspec, tok, tok, vec, tok],
        out_specs=[tok, tok, vec, wspec],
        out_shape=(SDS((T, D), f32), SDS((T, D), bf16), SDS((1, D), f32), SDS((DIN, D), bf16)),
        scratch_shapes=[pltpu.VMEM((DIN, D), f32)],
        compiler_params=_cparams(("arbitrary",)), name=name)(dz, wint, h, x, g, dres)


def _out_proj_fwd(ymix, wout, x, g, name):
    T = x.shape[0]
    tm = min(512, T)

    def body(y_ref, w_ref, x_ref, g_ref, o_ref, h_ref):
        o = x_ref[...] + _nn(y_ref[...], w_ref[...])
        o_ref[...] = o
        r = lax.rsqrt(jnp.mean(o * o, axis=-1, keepdims=True) + EPS)
        h_ref[...] = (o * r * g_ref[...]).astype(bf16)

    tok = pl.BlockSpec((tm, D), lambda i: (i, 0))
    return pl.pallas_call(
        body, grid=(T // tm,),
        in_specs=[pl.BlockSpec((tm, DMIX), lambda i: (i, 0)), pl.BlockSpec((DMIX, D), lambda i: (0, 0)), tok,
                  pl.BlockSpec((1, D), lambda i: (0, 0))],
        out_specs=[tok, tok], out_shape=(SDS((T, D), f32), SDS((T, D), bf16)), name=name)(ymix, wout, x, g)


def _out_proj_bwd(dxb, wout, ymix, name):
    T = dxb.shape[0]
    tm = min(512, T)
    nt = T // tm

    def body(dx_ref, w_ref, y_ref, dy_ref, dw_ref, acc):
        i = pl.program_id(0)
        dxv = dx_ref[...]
        dy_ref[...] = _nt(dxv, w_ref[...])
        part = _tn(y_ref[...], dxv)

        @pl.when(i == 0)
        def _():
            acc[...] = part

        @pl.when(i > 0)
        def _():
            acc[...] += part

        @pl.when(i == nt - 1)
        def _():
            dw_ref[...] = acc[...].astype(bf16)

    wspec = pl.BlockSpec((DMIX, D), lambda i: (0, 0))
    return pl.pallas_call(
        body, grid=(nt,),
        in_specs=[pl.BlockSpec((tm, D), lambda i: (i, 0)), wspec, pl.BlockSpec((tm, DMIX), lambda i: (i, 0))],
        out_specs=[pl.BlockSpec((tm, DMIX), lambda i: (i, 0)), wspec],
        out_shape=(SDS((T, DMIX), f32), SDS((DMIX, D), bf16)),
        scratch_shapes=[pltpu.VMEM((DMIX, D), f32)],
        compiler_params=_cparams(("arbitrary",)), name=name)(dxb, wout, ymix)


def _t5_bucket_table():
    ql = np.arange(BLK)[:, None]
    kl = np.arange(2 * BLK)[None, :]
    n = np.maximum(ql + BLK - kl, 0)
    max_exact = NBUCK // 2
    large = max_exact + (np.log(np.maximum(n, 1) / max_exact) / np.log(MAX_DISTANCE / max_exact)
                         * (NBUCK - max_exact)).astype(np.int32)
    large = np.minimum(large, NBUCK - 1)
    return np.where(n < max_exact, n, large).astype(np.int32)


def _fill_bias(bk_ref, rb_ref, bias_scr):
    bk = bk_ref[...]
    for h in range(NH):
        def step(b, acc, h=h):
            return acc + jnp.where(bk == b, rb_ref[b, h], 0.0)
        bias_scr[h] = lax.fori_loop(0, NBUCK, step, jnp.zeros((BLK, 2 * BLK), f32))


MIX_SUB = 4


class _Window:
    def __init__(self, zc_ref, zp_ref, n, s):
        self.blk = n * MIX_SUB + s
        self.cur = lambda a, b: zc_ref[s * BLK:(s + 1) * BLK, a:b]
        self.prev = (lambda a, b: zp_ref[:, a:b]) if s == 0 else (lambda a, b: zc_ref[(s - 1) * BLK:s * BLK, a:b])


def _attn_qkv(win, kh, qg, kg):
    kc = DATTN + HD * kh
    vc = DATTN + DKV + HD * kh
    kx = jnp.concatenate([win.prev(kc, kc + HD), win.cur(kc, kc + HD)], axis=0)
    vx = jnp.concatenate([win.prev(vc, vc + HD), win.cur(vc, vc + HD)], axis=0)
    qx = jnp.concatenate([win.cur(HD * (GQA * kh + g), HD * (GQA * kh + g + 1)) for g in range(GQA)], axis=0)
    rq = lax.rsqrt(jnp.mean(qx * qx, axis=-1, keepdims=True) + EPS)
    rk = lax.rsqrt(jnp.mean(kx * kx, axis=-1, keepdims=True) + EPS)
    qhat, khat = qx * rq, kx * rk
    return dict(qhat=qhat, khat=khat, rq=rq, rk=rk, qsb=(qhat * (qg * SCALE)).astype(bf16),
                knb=(khat * kg).astype(bf16), vb=vx.astype(bf16))


def _window_masks(n):
    row = lax.broadcasted_iota(i32, (GQA * BLK, 2 * BLK), 0) & (BLK - 1)
    col = lax.broadcasted_iota(i32, (GQA * BLK, 2 * BLK), 1)
    band = (col > row) & (col <= row + BLK)
    return band & ((col >= BLK) | (n > 0)), band


def _attn_probs(a, kh, sk_ref, bias_scr, mask):
    s = _nt(a["qsb"], a["knb"]) + bias_scr[GQA * kh:GQA * (kh + 1)].reshape(GQA * BLK, 2 * BLK)
    s = jnp.where(mask, s, NEG)
    ridx = lax.broadcasted_iota(i32, (GQA * BLK, 1), 0)
    sink = jnp.full((GQA * BLK, 1), sk_ref[GQA * kh + GQA - 1], f32)
    for g in range(GQA - 2, -1, -1):
        sink = jnp.where(ridx < (g + 1) * BLK, sk_ref[GQA * kh + g], sink)
    m = jnp.maximum(jnp.max(s, axis=-1, keepdims=True), sink)
    e = jnp.exp(s - m)
    den = jnp.sum(e, axis=-1, keepdims=True) + jnp.exp(sink - m)
    return e / den


POOL_STEPS = {2: (1,), 4: (1, 2), 8: (1, 2, 4), 16: (1, 2, 4, 8)}


def _pool_group(win, g, w):
    n = win.blk
    c0 = DATTN + 2 * DKV + PGD * g
    uc = win.cur(c0, c0 + PGD)
    up = jnp.where(n > 0, win.prev(c0, c0 + PGD), 0.0)
    sm = jnp.concatenate([up, uc], axis=0)
    for k in POOL_STEPS[w]:
        sm = sm + pltpu.roll(sm, k, axis=0)
    pos = n * BLK + lax.broadcasted_iota(i32, (BLK, 1), 0) + 1
    cnt = jnp.minimum(pos, w).astype(f32)
    return sm[BLK:2 * BLK] / cnt - uc, cnt


def _mix_fwd(z, qg, kg, sinks, relb, bucket, pool_w, pscale, name):
    T = z.shape[0]
    step_rows = MIX_SUB * BLK
    nsteps = T // step_rows

    def body(zc_ref, zp_ref, qg_ref, kg_ref, sk_ref, rb_ref, bk_ref, pw_ref, ps_ref, y_ref, p_ref, bias_scr, yacc):
        n = pl.program_id(0)

        @pl.when(n == 0)
        def _():
            _fill_bias(bk_ref, rb_ref, bias_scr)

        first_mask, mask = _window_masks(n)
        for s in range(MIX_SUB):
            win = _Window(zc_ref, zp_ref, n, s)
            rows = slice(s * BLK, (s + 1) * BLK)
            for kh in range(NKV):
                a = _attn_qkv(win, kh, qg_ref[...], kg_ref[...])
                pb = _attn_probs(a, kh, sk_ref, bias_scr, first_mask if s == 0 else mask).astype(bf16)
                p_ref[s, GQA * kh:GQA * (kh + 1)] = pb.reshape(GQA, BLK, 2 * BLK)
                o = _nn(pb, a["vb"])
                for g in range(GQA):
                    hc = HD * (GQA * kh + g)
                    yacc[rows, hc:hc + HD] = o[g * BLK:(g + 1) * BLK]
            for g, w in enumerate(POOL_WINDOWS):
                pooled, _ = _pool_group(win, g, w)
                yp = _nn(pooled.astype(bf16), pw_ref[g].astype(bf16)) * ps_ref[:, PGD * g:PGD * (g + 1)]
                yacc[rows, DATTN + PGD * g:DATTN + PGD * (g + 1)] = yp
        y_ref[...] = yacc[...].astype(bf16)

    full = lambda *shape: pl.BlockSpec(shape, lambda n: (0,) * len(shape))
    smem = pl.BlockSpec(memory_space=pltpu.SMEM)
    return pl.pallas_call(
        body, grid=(nsteps,),
        in_specs=[pl.BlockSpec((step_rows, DIN), lambda n: (n, 0)),
                  pl.BlockSpec((BLK, DIN), lambda n: (jnp.maximum(n * MIX_SUB - 1, 0), 0)),
                  full(1, HD), full(1, HD), smem, smem, full(BLK, 2 * BLK),
                  full(len(POOL_WINDOWS), PGD, PGD), full(1, DPOOL)],
        out_specs=[pl.BlockSpec((step_rows, DMIX), lambda n: (n, 0)),
                   pl.BlockSpec((MIX_SUB, NH, BLK, 2 * BLK), lambda n: (n, 0, 0, 0))],
        out_shape=(SDS((T, DMIX), bf16), SDS((T // BLK, NH, BLK, 2 * BLK), bf16)),
        scratch_shapes=[pltpu.VMEM((NH, BLK, 2 * BLK), f32), pltpu.VMEM((step_rows, DMIX), f32)],
        compiler_params=_cparams(("arbitrary",)), name=name)(z, z, qg, kg, sinks, relb, bucket, pool_w, pscale)


def _mix_bwd(z, dy, probs, qg, kg, relb, bucket, pool_w, pscale, name):
    T = z.shape[0]
    step_rows = MIX_SUB * BLK
    nsteps = T // step_rows

    def body(zc_ref, zp_ref, dy_ref, p_ref, qg_ref, kg_ref, bk_ref, pw_ref, ps_ref,
             dz_ref, dqg_ref, dkg_ref, dsk_ref, drb_ref, dpw_ref, dps_ref, dbias_scr):
        n = pl.program_id(0)

        @pl.when(n == 0)
        def _():
            dbias_scr[...] = jnp.zeros_like(dbias_scr)
            dqg_ref[...] = jnp.zeros_like(dqg_ref)
            dkg_ref[...] = jnp.zeros_like(dkg_ref)
            dpw_ref[...] = jnp.zeros_like(dpw_ref)
            dps_ref[...] = jnp.zeros_like(dps_ref)

        qg, kg = qg_ref[...], kg_ref[...]
        for s in range(MIX_SUB):
            win = _Window(zc_ref, zp_ref, n, s)
            blk = win.blk
            rows = pl.ds(pl.multiple_of(blk * BLK, BLK), BLK)
            prow = pl.ds(pl.multiple_of(jnp.maximum(blk - 1, 0) * BLK, BLK), BLK)
            dyr = slice(s * BLK, (s + 1) * BLK)

            def into_prev(fn, s=s):
                if s == 0:
                    pl.when(n > 0)(fn)
                else:
                    fn()

            for kh in range(NKV):
                a = _attn_qkv(win, kh, qg, kg)
                pb = p_ref[s, GQA * kh:GQA * (kh + 1)].reshape(GQA * BLK, 2 * BLK)
                p = pb.astype(f32)
                do = jnp.concatenate([dy_ref[dyr, HD * (GQA * kh + g):HD * (GQA * kh + g + 1)] for g in range(GQA)],
                                     axis=0).astype(bf16)
                dv = _tn(pb, do)
                dp = _nt(do, a["vb"])
                delta = jnp.sum(p * dp, axis=-1, keepdims=True)
                ds = p * (dp - delta)
                for g in range(GQA):
                    dbias_scr[GQA * kh + g] += ds[g * BLK:(g + 1) * BLK]
                dsb = ds.astype(bf16)
                dqn = _nn(dsb, a["knb"]) * SCALE
                dkn = _tn(dsb, a["qsb"])
                qhat, khat = a["qhat"], a["khat"]
                dqg_ref[...] += jnp.sum(dqn * qhat, axis=0, keepdims=True)
                dkg_ref[...] += jnp.sum(dkn * khat, axis=0, keepdims=True)
                dqh = dqn * qg
                dq = a["rq"] * (dqh - qhat * jnp.mean(dqh * qhat, axis=-1, keepdims=True))
                dkh = dkn * kg
                dk = a["rk"] * (dkh - khat * jnp.mean(dkh * khat, axis=-1, keepdims=True))
                kc = DATTN + HD * kh
                vc = DATTN + DKV + HD * kh
                for g in range(GQA):
                    hc = HD * (GQA * kh + g)
                    dz_ref[rows, hc:hc + HD] = dq[g * BLK:(g + 1) * BLK]
                dz_ref[rows, kc:kc + HD] = dk[BLK:2 * BLK]
                dz_ref[rows, vc:vc + HD] = dv[BLK:2 * BLK]

                def kv_prev(dk=dk, dv=dv, kc=kc, vc=vc, prow=prow):
                    dz_ref[prow, kc:kc + HD] += dk[0:BLK]
                    dz_ref[prow, vc:vc + HD] += dv[0:BLK]

                into_prev(kv_prev)

            for g, w in enumerate(POOL_WINDOWS):
                c0 = DATTN + 2 * DKV + PGD * g
                pooled, cnt = _pool_group(win, g, w)
                pb = pooled.astype(bf16)
                wb = pw_ref[g].astype(bf16)
                dyp = dy_ref[dyr, DATTN + PGD * g:DATTN + PGD * (g + 1)]
                ypre = _nn(pb, wb)
                dps_ref[:, PGD * g:PGD * (g + 1)] += jnp.sum(dyp * ypre, axis=0, keepdims=True)
                dyg = (dyp * ps_ref[:, PGD * g:PGD * (g + 1)]).astype(bf16)
                dpw_ref[g] += _tn(pb, dyg)
                dpooled = _nt(dyg, wb)
                due = jnp.concatenate([jnp.zeros((BLK, PGD), f32), dpooled / cnt], axis=0)
                for k in POOL_STEPS[w]:
                    due = due + pltpu.roll(due, 2 * BLK - k, axis=0)
                dz_ref[rows, c0:c0 + PGD] = due[BLK:2 * BLK] - dpooled

                def pool_prev(due=due, c0=c0, prow=prow):
                    dz_ref[prow, c0:c0 + PGD] += due[0:BLK]

                into_prev(pool_prev)

        @pl.when(n == nsteps - 1)
        def _():
            bk = bk_ref[...]
            ri = lax.broadcasted_iota(i32, (NBUCK, NH), 0)
            ci = lax.broadcasted_iota(i32, (NBUCK, NH), 1)

            def step(b, acc):
                for h in range(NH):
                    sel = jnp.where(bk == b, dbias_scr[h], 0.0)
                    tot = jnp.sum(jnp.sum(sel, axis=1, keepdims=True), axis=0, keepdims=True)
                    acc = acc + jnp.where((ri == b) & (ci == h), tot, 0.0)
                return acc

            drb_ref[...] = lax.fori_loop(0, NBUCK, step, jnp.zeros((NBUCK, NH), f32))
            lane = lax.broadcasted_iota(i32, (1, 128), 1)
            dsk = jnp.zeros((1, 128), f32)
            for h in range(NH):
                tot = jnp.sum(jnp.sum(dbias_scr[h], axis=1, keepdims=True), axis=0, keepdims=True)
                dsk = dsk - jnp.where(lane == h, tot, 0.0)
            dsk_ref[...] = dsk

    full = lambda *shape: pl.BlockSpec(shape, lambda n: (0,) * len(shape))
    npg = len(POOL_WINDOWS)
    return pl.pallas_call(
        body, grid=(nsteps,),
        in_specs=[pl.BlockSpec((step_rows, DIN), lambda n: (n, 0)),
                  pl.BlockSpec((BLK, DIN), lambda n: (jnp.maximum(n * MIX_SUB - 1, 0), 0)),
                  pl.BlockSpec((step_rows, DMIX), lambda n: (n, 0)),
                  pl.BlockSpec((MIX_SUB, NH, BLK, 2 * BLK), lambda n: (n, 0, 0, 0)),
                  full(1, HD), full(1, HD), full(BLK, 2 * BLK), full(npg, PGD, PGD), full(1, DPOOL)],
        out_specs=[full(T, DIN), full(1, HD), full(1, HD), full(1, 128), full(NBUCK, NH),
                   full(npg, PGD, PGD), full(1, DPOOL)],
        out_shape=(SDS((T, DIN), f32), SDS((1, HD), f32), SDS((1, HD), f32), SDS((1, 128), f32),
                   SDS((NBUCK, NH), f32), SDS((npg, PGD, PGD), f32), SDS((1, DPOOL), f32)),
        scratch_shapes=[pltpu.VMEM((NH, BLK, 2 * BLK), f32)],
        compiler_params=_cparams(("arbitrary",), VMEM_LIMIT_V7X),
        name=name)(z, z, dy, probs, qg, kg, bucket, pool_w, pscale)


class _LocalWeights:
    def __init__(self, w1, wint, wout, w2):
        self.w1, self.wint, self.wout, self.w2 = w1, wint, wout, w2

    def ffn1(self):
        return self.w1

    def first_norm(self, x, gain):
        return _norm_fwd(x, gain, "norm1_fwd")

    def after_ffn1(self, gain, x1):
        return gain

    def mix(self, after):
        return self.wint, self.wout

    def before_out_proj(self, wout, after):
        return wout

    def ffn2(self, after):
        return self.w2

    def out_ffn2_grads_ready(self, dwout, dw2, after):
        return after

    def before_ffn1_bwd(self, dwint, dx1b):
        return dx1b


def _local_step(x, target, weights, g1, gm, g3, qg, kg, sinks, relb, pool_w, pscale):
    bucket = jnp.asarray(_t5_bucket_table())
    sk = sinks.reshape(NH)
    w1 = weights.ffn1()
    h1 = weights.first_norm(x, g1)
    x1, gate1, up1, h2 = _ffn_fwd(h1, w1, x, None, gm, "ffn1_fwd")
    gm = weights.after_ffn1(gm, x1)
    wint, wout = weights.mix(h2)
    z = _in_proj_fwd(h2, wint, "in_proj_fwd")
    ymix, probs = _mix_fwd(z, qg, kg, sk, relb, bucket, pool_w, pscale, "mix_fwd")
    wout = weights.before_out_proj(wout, ymix)
    x2, h3 = _out_proj_fwd(ymix, wout, x1, g3, "out_proj_fwd")
    w2 = weights.ffn2(h3)
    dy, gate2, up2, dyb, loss_lanes = _ffn_fwd(h3, w2, x2, target, None, "ffn2_fwd")

    dx2, dx2b, dg3, dw2 = _ffn_bwd(dyb, h3, gate2, up2, w2, (x2, g3, dy), "ffn2_bwd")
    dymix, dwout = _out_proj_bwd(dx2b, wout, ymix, "out_proj_bwd")
    dymix = weights.out_ffn2_grads_ready(dwout, dw2, dymix)
    dz, dqg, dkg, dsk, drb, dpw, dps = _mix_bwd(z, dymix, probs, qg, kg, relb, bucket, pool_w, pscale, "mix_bwd")
    dx1, dx1b, dgm, dwint = _in_proj_bwd(dz, wint, h2, (x1, gm, dx2), 0.5, "in_proj_bwd")
    dx1b = weights.before_ffn1_bwd(dwint, dx1b)
    gx, _, dg1, dw1 = _ffn_bwd(dx1b, h1, gate1, up1, w1, (x, g1, dx1), "ffn1_bwd")
    small = dict(ffn1_norm=dg1, mix_norm=dgm, ffn2_norm=dg3, pool_scale=dps, q_norm=dqg, k_norm=dkg,
                 attn_sinks=dsk[:, :NH], rel_bias=drb, pool_w=dpw, loss=loss_lanes)
    return gx, (dw1, dwint, dwout, dw2), small


SMALL_NAMES = ("ffn1_norm", "mix_norm", "ffn2_norm", "pool_scale", "q_norm", "k_norm", "attn_sinks", "rel_bias",
               "pool_w", "loss")
SMALL_SHAPES = dict(ffn1_norm=(1, D), mix_norm=(1, D), ffn2_norm=(1, D), pool_scale=(1, DPOOL), q_norm=(1, HD),
                    k_norm=(1, HD), attn_sinks=(1, NH), rel_bias=(NBUCK, NH),
                    pool_w=(1, len(POOL_WINDOWS), PGD, PGD), loss=(1, 128))


def _small_rows(name):
    return -(-int(np.prod(SMALL_SHAPES[name])) // 128)


SMALL_OFF = {}
_r = 0
for _n in SMALL_NAMES:
    SMALL_OFF[_n] = _r
    _r += _small_rows(_n)
SMALL_ROWS = -(-_r // 16) * 16
LOSS_ROW = SMALL_OFF["loss"]


def _pack_small(vals):
    parts = []
    for n in SMALL_NAMES:
        size = _small_rows(n) * 128
        if n in vals:
            flat = vals[n].astype(f32).reshape(-1)
            parts.append(jnp.pad(flat, (0, size - flat.shape[0])))
        else:
            parts.append(jnp.zeros((size,), f32))
    flat = jnp.concatenate(parts)
    flat = jnp.pad(flat, (0, SMALL_ROWS * 128 - flat.shape[0]))
    return flat.reshape(SMALL_ROWS, 128)


def _unpack_small(packed, name):
    size = int(np.prod(SMALL_SHAPES[name]))
    r0 = SMALL_OFF[name]
    return packed[r0:r0 + _small_rows(name)].reshape(-1)[:size].reshape(SMALL_SHAPES[name])


def _position():
    return lax.axis_index("x"), lax.axis_index("y"), lax.axis_index("c")


def _dev_index(x, y, c):
    return 4 * x + 2 * y + c


G1_PIECES, MIX_PIECES, F2_PIECES = (0, 1, 2), (3, 4), (5, 6, 7)


def _group_rows(pieces):
    return sum(PIECE_ROWS[k] for k in pieces)


def _shard_piece(s_ref, k):
    return s_ref.at[pl.ds(PIECE_OFF[k], PIECE_ROWS[k]), :]


def _shard_group(s_ref, pieces):
    return s_ref.at[pl.ds(PIECE_OFF[pieces[0]], _group_rows(pieces)), :]


def _weight_pieces(w1_ref=None, wi_ref=None, wo_ref=None, w2_ref=None):
    arrs = {}
    if w1_ref is not None:
        arrs.update({0: w1_ref.at[0], 1: w1_ref.at[1], 2: w1_ref.at[2]})
    if wi_ref is not None:
        arrs[3] = wi_ref
    if wo_ref is not None:
        arrs[4] = wo_ref
    if w2_ref is not None:
        arrs.update({5: w2_ref.at[0], 6: w2_ref.at[1], 7: w2_ref.at[2]})
    return arrs


def _block_rows(arrs, k, dev):
    r = PIECE_ROWS[k]
    return arrs[k].at[pl.ds(pl.multiple_of(_dev_index(*dev) * r, 16), r), :]


NORM_ROWS = 512


def _all_gather_ffn1(shard, x, gain):
    pieces = G1_PIECES
    rest_pieces = MIX_PIECES + F2_PIECES
    half = FS // 2
    T = x.shape[0]
    SIB, X0, X1, Y0, Y1, RELAY_Y, RELAY_X, ON_X, ON_Y, ON_D0, ON_D1 = range(11)

    def body(s_ref, x_ref, g_ref, w1_ref, h_ref, wi_ref, wo_ref, w2_ref, xbuf, hbuf, rest_buf,
             send_sems, recv_sems, local_sem, norm_sems):
        x, y, c = _position()
        me, sib = (x, y, c), (x, y, 1 - c)
        xn, yn, dg = (1 - x, y, c), (x, 1 - y, c), (1 - x, 1 - y, c)
        arrs = _weight_pieces(w1_ref=w1_ref)

        def place_rest():
            rest = _weight_pieces(wi_ref=wi_ref, wo_ref=wo_ref, w2_ref=w2_ref)
            grp = _shard_group(s_ref, rest_pieces)
            load = pltpu.make_async_copy(grp, rest_buf, norm_sems.at[0])
            load.start()
            load.wait()
            base = PIECE_OFF[rest_pieces[0]]
            for k in rest_pieces:
                pltpu.make_async_copy(rest_buf.at[pl.ds(PIECE_OFF[k] - base, PIECE_ROWS[k]), :],
                                      _block_rows(rest, k, me), norm_sems.at[1]).start()
            pltpu.make_async_copy(grp, rest_buf, norm_sems.at[1]).wait()

        def first_norm():
            for r in range(0, T, NORM_ROWS):
                load = pltpu.make_async_copy(x_ref.at[pl.ds(r, NORM_ROWS), :], xbuf, norm_sems.at[0])
                load.start()
                load.wait()
                xv = xbuf[...]
                rs = lax.rsqrt(jnp.mean(xv * xv, axis=-1, keepdims=True) + EPS)
                hbuf[...] = (xv * rs * g_ref[...]).astype(bf16)
                store = pltpu.make_async_copy(hbuf, h_ref.at[pl.ds(r, NORM_ROWS), :], norm_sems.at[1])
                store.start()
                store.wait()

        def rows_of(k, block, hf):
            r = PIECE_ROWS[k]
            start, size = (0, r) if hf is None else (hf * half, half)
            return arrs[k].at[pl.ds(pl.multiple_of(_dev_index(*block) * r + start, 16), size), :]

        def copies(rel, block, hf, to, from_shard=False):
            def src(k):
                if not from_shard:
                    return rows_of(k, block, hf)
                start, size = (0, PIECE_ROWS[k]) if hf is None else (hf * half, half)
                return s_ref.at[pl.ds(PIECE_OFF[k] + start, size), :]
            return [pltpu.make_async_remote_copy(
                src_ref=src(k), dst_ref=rows_of(k, block, hf), send_sem=send_sems.at[rel], recv_sem=recv_sems.at[rel],
                device_id=to, device_id_type=MESH) for k in pieces]

        def waiter(rel, hf):
            nrows = len(pieces) * (FS if hf is None else half)
            grp = s_ref.at[pl.ds(0, nrows), :]
            return pltpu.make_async_remote_copy(src_ref=grp, dst_ref=grp, send_sem=send_sems.at[rel],
                                                recv_sem=recv_sems.at[rel], device_id=me, device_id_type=MESH)

        def start(cps):
            for cp in cps:
                cp.start()

        mine = [pltpu.make_async_copy(_shard_piece(s_ref, k), _block_rows(arrs, k, me), local_sem) for k in pieces]
        start(mine)
        start(copies(SIB, me, None, sib, True))
        start(copies(X0, me, 0, xn, True))
        start(copies(Y1, me, 1, yn, True))
        start(copies(X1, me, 1, xn, True))
        start(copies(Y0, me, 0, yn, True))
        first_norm()
        place_rest()
        waiter(X0, 0).wait_recv()
        start(copies(RELAY_Y, xn, 0, yn))
        waiter(Y1, 1).wait_recv()
        start(copies(RELAY_X, yn, 1, xn))
        waiter(X1, 1).wait_recv()
        start(copies(ON_X, xn, None, sib))
        waiter(Y0, 0).wait_recv()
        start(copies(ON_Y, yn, None, sib))
        waiter(RELAY_Y, 0).wait_recv()
        start(copies(ON_D0, dg, 0, sib))
        waiter(RELAY_X, 1).wait_recv()
        start(copies(ON_D1, dg, 1, sib))
        waiter(SIB, None).wait_recv()
        waiter(ON_X, None).wait_recv()
        waiter(ON_Y, None).wait_recv()
        waiter(ON_D0, 0).wait_recv()
        waiter(ON_D1, 1).wait_recv()
        for rel, hf in ((SIB, None), (X0, 0), (X1, 1), (Y0, 0), (Y1, 1), (RELAY_Y, 0), (RELAY_X, 1),
                        (ON_X, None), (ON_Y, None), (ON_D0, 0), (ON_D1, 1)):
            waiter(rel, hf).wait_send()
        grp = _shard_group(s_ref, pieces)
        pltpu.make_async_copy(grp, grp, local_sem).wait()

    hbm = pl.BlockSpec(memory_space=pl.ANY)
    return pl.pallas_call(
        body, in_specs=[hbm, hbm, pl.BlockSpec(memory_space=pltpu.VMEM)], out_specs=[hbm] * 5,
        out_shape=(SDS((3, F, D), bf16), SDS((T, D), bf16),
                   SDS((DIN, D), bf16), SDS((DMIX, D), bf16), SDS((3, F, D), bf16)),
        scratch_shapes=[pltpu.VMEM((NORM_ROWS, D), f32), pltpu.VMEM((NORM_ROWS, D), bf16),
                        pltpu.VMEM((_group_rows(rest_pieces), D), bf16),
                        pltpu.SemaphoreType.DMA((11,)), pltpu.SemaphoreType.DMA((11,)), pltpu.SemaphoreType.DMA,
                        pltpu.SemaphoreType.DMA((2,))],
        compiler_params=pltpu.CompilerParams(has_side_effects=True),
        name="all_gather_ffn1")(shard, x, gain)


HBM_SPEC = pl.BlockSpec(memory_space=pltpu.HBM)
SEM_SPEC = pl.BlockSpec(memory_space=pltpu.SEMAPHORE)
ANY_SPEC = pl.BlockSpec(memory_space=pl.ANY)
SPLIT_EFFECT = pltpu.SideEffectType.DATAFLOW_SIDE_EFFECTING


def _in_hbm(a):
    return pltpu.with_memory_space_constraint(a, pltpu.HBM)


def _hbm_like(a):
    return pltpu.HBM(a.shape, a.dtype)


def _gather_rest_start(shard, wi, wo, w2, w1):
    def body(s_ref, wi_ref, wo_ref, w2_ref, w1_ref,
             ssem_m, rsem_m0, rsem_m, ssem_f, rsem_f0, rsem_f, s_o, wi_o, wo_o, w2_o, w1_o):
        x, y, c = _position()
        me, sib = (x, y, c), (x, y, 1 - c)
        chips = [(1 - x, y), (x, 1 - y), (1 - x, 1 - y)]
        arrs = _weight_pieces(wi_ref=wi_ref, wo_ref=wo_ref, w2_ref=w2_ref)
        for pieces, ssem, rsem0, rsem in ((MIX_PIECES, ssem_m, rsem_m0, rsem_m), (F2_PIECES, ssem_f, rsem_f0, rsem_f)):
            for p in pieces:
                pltpu.make_async_remote_copy(
                    src_ref=_shard_piece(s_ref, p), dst_ref=_block_rows(arrs, p, me), send_sem=ssem.at[0],
                    recv_sem=rsem0, device_id=sib, device_id_type=MESH).start()
            for j, chip in enumerate(chips):
                for p in pieces:
                    pltpu.make_async_remote_copy(
                        src_ref=_shard_piece(s_ref, p), dst_ref=_block_rows(arrs, p, me), send_sem=ssem.at[1 + j],
                        recv_sem=rsem.at[j], device_id=(*chip, c), device_id_type=MESH).start()

    dma = pltpu.SemaphoreType.DMA
    return pl.pallas_call(
        body, name="gather_rest_start",
        out_shape=(dma((4,)), dma(()), dma((3,)), dma((4,)), dma(()), dma((3,)),
                   _hbm_like(shard), _hbm_like(wi), _hbm_like(wo), _hbm_like(w2), _hbm_like(w1)),
        in_specs=(HBM_SPEC,) * 5, out_specs=(SEM_SPEC,) * 6 + (HBM_SPEC,) * 5,
        input_output_aliases={0: 6, 1: 7, 2: 8, 3: 9, 4: 10},
        compiler_params=pltpu.CompilerParams(has_side_effects=SPLIT_EFFECT),
    )(_in_hbm(shard), _in_hbm(wi), _in_hbm(wo), _in_hbm(w2), _in_hbm(w1))


def _gather_mix_pass_on(rsem_m, wi, wo, thru, after):
    def body(wi_ref, wo_ref, thru_ref, rsem, after_ref, fsend, frecv, wi_o, wo_o, thru_o):
        x, y, c = _position()
        sib = (x, y, 1 - c)
        arrs = _weight_pieces(wi_ref=wi_ref, wo_ref=wo_ref)
        both = wi_ref.at[pl.ds(0, _group_rows(MIX_PIECES)), :]
        for j, chip in enumerate([(1 - x, y), (x, 1 - y), (1 - x, 1 - y)]):
            pltpu.make_async_remote_copy(src_ref=both, dst_ref=both, send_sem=fsend.at[j], recv_sem=rsem.at[j],
                                         device_id=(x, y, c), device_id_type=MESH).wait_recv()
            for p in MIX_PIECES:
                rows = _block_rows(arrs, p, (*chip, c))
                pltpu.make_async_remote_copy(src_ref=rows, dst_ref=rows, send_sem=fsend.at[j], recv_sem=frecv.at[j],
                                             device_id=sib, device_id_type=MESH).start()

    dma = pltpu.SemaphoreType.DMA
    return pl.pallas_call(
        body, name="gather_mix_pass_on",
        out_shape=(dma((3,)), dma((3,)), _hbm_like(wi), _hbm_like(wo), _hbm_like(thru)),
        in_specs=(HBM_SPEC, HBM_SPEC, HBM_SPEC, SEM_SPEC, ANY_SPEC), out_specs=(SEM_SPEC, SEM_SPEC) + (HBM_SPEC,) * 3,
        input_output_aliases={0: 2, 1: 3, 2: 4},
        compiler_params=pltpu.CompilerParams(has_side_effects=SPLIT_EFFECT),
    )(wi, wo, _in_hbm(thru), rsem_m, after)


def _gather_mix_wait(ssem_m, rsem_m0, fsend, frecv, shard, wi, wo, after):
    def body(s_ref, wi_ref, wo_ref, ssem, rsem0, fs, fr, after_ref, s_o, wi_o, wo_o):
        x, y, c = _position()
        grp = _shard_group(s_ref, MIX_PIECES)

        def waiter(send_sem, recv_sem):
            return pltpu.make_async_remote_copy(src_ref=grp, dst_ref=grp, send_sem=send_sem, recv_sem=recv_sem,
                                                device_id=(x, y, c), device_id_type=MESH)

        waiter(ssem.at[0], rsem0).wait_recv()
        for j in range(3):
            waiter(fs.at[j], fr.at[j]).wait_recv()
        for rel in range(4):
            waiter(ssem.at[rel], rsem0).wait_send()
        for j in range(3):
            waiter(fs.at[j], fr.at[j]).wait_send()

    return pl.pallas_call(
        body, name="gather_mix_wait", out_shape=(_hbm_like(shard), _hbm_like(wi), _hbm_like(wo)),
        in_specs=(HBM_SPEC,) * 3 + (SEM_SPEC,) * 4 + (ANY_SPEC,), out_specs=(HBM_SPEC,) * 3,
        input_output_aliases={0: 0, 1: 1, 2: 2},
        compiler_params=pltpu.CompilerParams(has_side_effects=SPLIT_EFFECT),
    )(shard, wi, wo, ssem_m, rsem_m0, fsend, frecv, after)


def _gather_ffn2_pass_on(rsem_f, w2, wo, after):
    def body(w2_ref, wo_ref, rsem, after_ref, fsend, frecv, w2_o, wo_o):
        x, y, c = _position()
        sib = (x, y, 1 - c)
        chips = [(1 - x, y), (x, 1 - y), (1 - x, 1 - y)]
        arrs = _weight_pieces(w2_ref=w2_ref)
        three = w2_ref.at[0, pl.ds(0, _group_rows(F2_PIECES)), :]
        for j, chip in enumerate(chips):
            pltpu.make_async_remote_copy(src_ref=three, dst_ref=three, send_sem=fsend.at[j], recv_sem=rsem.at[j],
                                         device_id=(x, y, c), device_id_type=MESH).wait_recv()
            for p in F2_PIECES:
                rows = _block_rows(arrs, p, (*chip, c))
                pltpu.make_async_remote_copy(src_ref=rows, dst_ref=rows, send_sem=fsend.at[j], recv_sem=frecv.at[j],
                                             device_id=sib, device_id_type=MESH).start()

    dma = pltpu.SemaphoreType.DMA
    return pl.pallas_call(
        body, name="gather_ffn2_pass_on", out_shape=(dma((3,)), dma((3,)), _hbm_like(w2), _hbm_like(wo)),
        in_specs=(HBM_SPEC, HBM_SPEC, SEM_SPEC, ANY_SPEC), out_specs=(SEM_SPEC, SEM_SPEC, HBM_SPEC, HBM_SPEC),
        input_output_aliases={0: 2, 1: 3},
        compiler_params=pltpu.CompilerParams(has_side_effects=SPLIT_EFFECT),
    )(w2, wo, rsem_f, after)


def _gather_ffn2_wait(ssem_f, rsem_f0, fsend, frecv, shard, w2, after):
    def body(s_ref, w2_ref, ssem, rsem0, fs, fr, after_ref, w2_o):
        x, y, c = _position()
        grp = _shard_group(s_ref, F2_PIECES)

        def waiter(send_sem, recv_sem):
            return pltpu.make_async_remote_copy(src_ref=grp, dst_ref=grp, send_sem=send_sem, recv_sem=recv_sem,
                                                device_id=(x, y, c), device_id_type=MESH)

        waiter(ssem.at[0], rsem0).wait_recv()
        for j in range(3):
            waiter(fs.at[j], fr.at[j]).wait_recv()
        for rel in range(4):
            waiter(ssem.at[rel], rsem0).wait_send()
        for j in range(3):
            waiter(fs.at[j], fr.at[j]).wait_send()

    return pl.pallas_call(
        body, name="gather_ffn2_wait", out_shape=_hbm_like(w2),
        in_specs=(HBM_SPEC, HBM_SPEC, SEM_SPEC, SEM_SPEC, SEM_SPEC, SEM_SPEC, ANY_SPEC), out_specs=HBM_SPEC,
        input_output_aliases={1: 0},
        compiler_params=pltpu.CompilerParams(has_side_effects=SPLIT_EFFECT),
    )(shard, w2, ssem_f, rsem_f0, fsend, frecv, after)


class _GatheredWeights(_LocalWeights):
    def __init__(self, shard, x, gain1):
        w1, self.h1, wi, wo, w2 = _all_gather_ffn1(shard, x, gain1)
        (self.ssem_m, self.rsem_m0, self.rsem_m, self.ssem_f, self.rsem_f0, self.rsem_f,
         self.shard, self.wi, self.wo, self.w2_part, self.w1) = _gather_rest_start(shard, wi, wo, w2, w1)

    def first_norm(self, x, gain):
        return self.h1

    def after_ffn1(self, gain, x1):
        self.fsend_m, self.frecv_m, self.wi, self.wo, gain = _gather_mix_pass_on(self.rsem_m, self.wi, self.wo, gain, x1)
        return gain

    def mix(self, after):
        self.shard, wint, wout = _gather_mix_wait(self.ssem_m, self.rsem_m0, self.fsend_m, self.frecv_m, self.shard,
                                                  self.wi, self.wo, after)
        return wint, wout

    def before_out_proj(self, wout, after):
        self.fsend, self.frecv, self.w2_part, wout = _gather_ffn2_pass_on(self.rsem_f, self.w2_part, wout, after)
        return wout

    def ffn2(self, after):
        return _gather_ffn2_wait(self.ssem_f, self.rsem_f0, self.fsend, self.frecv, self.shard, self.w2_part, after)

    def out_ffn2_grads_ready(self, dwout, dw2, after):
        rx1 = lax.empty((4, RSA_ROWS, D), bf16)
        sa, ra, sent, rx1, after = _rsa_level1_start(dict(wo=dwout, w2=dw2), rx1, after, "rsa_level1_start_out_ffn2")
        self.level1 = ((sa, ra), sent, rx1)
        return after

    def before_ffn1_bwd(self, dwint, dx1b):
        early, sent, rx1 = self.level1
        sa, ra, late, rx1, dx1b = _rsa_level1_start(dict(wi=dwint), rx1, dx1b, "rsa_level1_start_in")
        started = (((MIX_PIECES[1],) + F2_PIECES, *early), ((MIX_PIECES[0],), sa, ra))
        rx2 = lax.empty((3, RSA_ROWS, D), bf16)
        self.sb, self.rb, self.tx, self.acc, self.rx2, dx1b = _rsa_sums_and_send(
            started, late["wi"], sent["wo"], sent["w2"], rx1, rx2, dx1b)
        return dx1b

    def mix_ffn2_grads_parts(self, after):
        rx2 = _rsa_level2_wait(self.sb, self.rb, self.tx, self.rx2, after)
        return self.acc, rx2


def _reduce_scatter_ffn1_head(dw1, small_packed):
    pieces = G1_PIECES
    half = FS // 2
    hrows = len(pieces) * half
    nrows = 2 * hrows
    X_RELAY, Y_RELAY = range(2)

    def body(d1_ref, p_ref, forx_ref, fory_ref, own_ref, rx1_ref, relx_ref, rely_ref, tot_ref,
             own_buf, rx_buf, tx1, tx2, tx3, acc, sa, ra, sb, rb, lsem, pair, chips, small_tot, small_send, small_recv):
        x, y, c = _position()
        me, sib = (x, y, c), (x, y, 1 - c)
        xn, yn = (1 - x, y, c), (x, 1 - y, c)
        rel_chips = [(x, y), (1 - x, y), (x, 1 - y), (1 - x, 1 - y)]
        srcs = _weight_pieces(w1_ref=d1_ref)

        my_chip = 2 * x + y
        pair[c] = p_ref[...]
        swap = pltpu.make_async_remote_copy(
            src_ref=p_ref, dst_ref=pair.at[c], send_sem=small_send.at[0], recv_sem=small_recv.at[0],
            device_id=sib, device_id_type=MESH)
        swap.start()
        mine = pl.ds(pl.multiple_of(c * (SMALL_ROWS // 2), 8), SMALL_ROWS // 2)
        small = [pltpu.make_async_remote_copy(
            src_ref=chips.at[my_chip, mine, :], dst_ref=chips.at[my_chip, mine, :], send_sem=small_send.at[j],
            recv_sem=small_recv.at[j], device_id=(*rel_chips[j], c), device_id_type=MESH) for j in (1, 2, 3)]
        give = pltpu.make_async_remote_copy(
            src_ref=small_tot.at[mine, :], dst_ref=small_tot.at[mine, :], send_sem=small_send.at[4],
            recv_sem=small_recv.at[4], device_id=sib, device_id_type=MESH)

        def part(k, dev, hf):
            r = PIECE_ROWS[k]
            return srcs[k].at[pl.ds(pl.multiple_of(_dev_index(*dev) * r + hf * half, 16), half), :]

        def slot(ref, k, hf):
            return ref.at[pl.ds(hf * hrows + k * half, half), :]

        halves = [(k, hf) for hf in (0, 1) for k in pieces]

        for j in (3, 1, 2, 0):
            for k, hf in halves:
                pltpu.make_async_remote_copy(
                    src_ref=part(k, (*rel_chips[j], 1 - c), hf), dst_ref=slot(rx1_ref.at[j], k, hf),
                    send_sem=sa.at[j], recv_sem=ra.at[j], device_id=sib, device_id_type=MESH).start()

        def wait_a(j):
            return pltpu.make_async_remote_copy(src_ref=rx1_ref.at[j], dst_ref=rx1_ref.at[j], send_sem=sa.at[j],
                                                recv_sem=ra.at[j], device_id=me, device_id_type=MESH)

        def ici(rel, src, dst, to):
            return pltpu.make_async_remote_copy(src_ref=src, dst_ref=dst, send_sem=sb.at[rel], recv_sem=rb.at[rel],
                                                device_id=to, device_id_type=MESH)

        first, second = pl.ds(0, hrows), pl.ds(hrows, hrows)
        sends = {
            X_RELAY: ici(X_RELAY, tx3.at[first, :], relx_ref, xn),
            Y_RELAY: ici(Y_RELAY, tx3.at[second, :], rely_ref, yn),
        }

        def chip_sum(j, dst):
            loads = [pltpu.make_async_copy(part(k, (*rel_chips[j], c), hf), slot(own_buf, k, hf), lsem.at[0])
                     for k, hf in halves]
            for cp in loads:
                cp.start()
            wait_a(j).wait_recv()
            got = pltpu.make_async_copy(rx1_ref.at[j], rx_buf, lsem.at[1])
            got.start()
            pltpu.make_async_copy(rx_buf, rx_buf, lsem.at[0]).wait()
            got.wait()

            def add(i, carry):
                rows = pl.ds(pl.multiple_of(i * half, 16), half)
                tot = own_buf[rows, :].astype(f32) + rx_buf[rows, :].astype(f32)
                dst[rows, :] = tot.astype(dst.dtype)
                return carry

            lax.fori_loop(0, nrows // half, add, 0)

        def add_landed(landed, dst, rows0, nrows_):
            got = pltpu.make_async_copy(landed, rx_buf.at[pl.ds(0, nrows_), :], lsem.at[1])
            got.start()
            got.wait()

            def add(i, carry):
                src_rows = pl.ds(pl.multiple_of(i * half, 16), half)
                dst_rows = pl.ds(pl.multiple_of(rows0 + i * half, 16), half)
                dst[dst_rows, :] = (dst[dst_rows, :].astype(f32) + rx_buf[src_rows, :].astype(f32)).astype(dst.dtype)
                return carry

            lax.fori_loop(0, nrows_ // half, add, 0)

        chip_sum(3, tx3)
        sends[X_RELAY].start()
        sends[Y_RELAY].start()
        swap.wait_recv()
        chips[my_chip] = pair[0] + pair[1]
        for cp in small:
            cp.start()
        chip_sum(1, tx1)
        chip_sum(2, tx2)
        chip_sum(0, acc)
        own_out = pltpu.make_async_copy(acc, own_ref, lsem.at[0])
        own_out.start()
        sends[X_RELAY].wait_recv()
        add_landed(relx_ref, tx2, 0, hrows)
        sends[Y_RELAY].wait_recv()
        add_landed(rely_ref, tx1, hrows, hrows)
        own_out.wait()
        outs = [pltpu.make_async_copy(tx1, forx_ref, lsem.at[0]), pltpu.make_async_copy(tx2, fory_ref, lsem.at[1])]
        for cp in outs:
            cp.start()
        for cp in outs:
            cp.wait()
        for cp in small:
            cp.wait_recv()
        small_tot[mine, :] = (chips[0, mine, :] + chips[1, mine, :]) + (chips[2, mine, :] + chips[3, mine, :])
        give.start()
        give.wait_recv()
        tot = small_tot[...]
        tot_ref[...] = tot
        loss = jnp.sum(tot[LOSS_ROW:LOSS_ROW + 1, :], axis=-1, keepdims=True)
        tot_ref[LOSS_ROW:LOSS_ROW + 1, :] = jnp.broadcast_to(loss, (1, 128))
        for j in range(4):
            wait_a(j).wait_send()
        for cp in sends.values():
            cp.wait_send()
        swap.wait_send()
        for cp in small + [give]:
            cp.wait_send()

    hbm = pl.BlockSpec(memory_space=pl.ANY)
    vm = pl.BlockSpec(memory_space=pltpu.VMEM)
    outs = pl.pallas_call(
        body, in_specs=[hbm, vm], out_specs=[hbm] * 6 + [vm],
        out_shape=(SDS((nrows, D), bf16), SDS((nrows, D), bf16), SDS((nrows, D), f32), SDS((4, nrows, D), bf16),
                   SDS((hrows, D), bf16), SDS((hrows, D), bf16), SDS((SMALL_ROWS, 128), f32)),
        scratch_shapes=[pltpu.VMEM((nrows, D), bf16), pltpu.VMEM((nrows, D), bf16),
                        pltpu.VMEM((nrows, D), bf16), pltpu.VMEM((nrows, D), bf16), pltpu.VMEM((nrows, D), bf16),
                        pltpu.VMEM((nrows, D), f32),
                        pltpu.SemaphoreType.DMA((4,)), pltpu.SemaphoreType.DMA((4,)),
                        pltpu.SemaphoreType.DMA((2,)), pltpu.SemaphoreType.DMA((2,)), pltpu.SemaphoreType.DMA((2,)),
                        pltpu.VMEM((2, SMALL_ROWS, 128), f32), pltpu.VMEM((4, SMALL_ROWS, 128), f32),
                        pltpu.VMEM((SMALL_ROWS, 128), f32),
                        pltpu.SemaphoreType.DMA((5,)), pltpu.SemaphoreType.DMA((5,))],
        compiler_params=pltpu.CompilerParams(has_side_effects=True, vmem_limit_bytes=VMEM_LIMIT_V7X),
        name="reduce_scatter_ffn1_head")(dw1, small_packed)
    return outs[0], outs[1], outs[2], outs[-1]


def _rs1_tail_start(for_x, for_y, from_x, from_y, thru):
    def body(fx_ref, fy_ref, lx_ref, ly_ref, thru_ref, ssem, rsem, fx_o, fy_o, lx_o, ly_o, thru_o):
        x, y, c = _position()
        pltpu.make_async_remote_copy(src_ref=fx_ref, dst_ref=lx_ref, send_sem=ssem.at[0], recv_sem=rsem.at[0],
                                     device_id=(1 - x, y, c), device_id_type=MESH).start()
        pltpu.make_async_remote_copy(src_ref=fy_ref, dst_ref=ly_ref, send_sem=ssem.at[1], recv_sem=rsem.at[1],
                                     device_id=(x, 1 - y, c), device_id_type=MESH).start()

    dma = pltpu.SemaphoreType.DMA
    arrs = (for_x, for_y, from_x, from_y, thru)
    return pl.pallas_call(
        body, name="rs1_tail_start", out_shape=(dma((2,)), dma((2,))) + tuple(_hbm_like(a) for a in arrs),
        in_specs=(HBM_SPEC,) * 5, out_specs=(SEM_SPEC,) * 2 + (HBM_SPEC,) * 5,
        input_output_aliases={0: 2, 1: 3, 2: 4, 3: 5, 4: 6},
        compiler_params=pltpu.CompilerParams(has_side_effects=SPLIT_EFFECT),
    )(*[_in_hbm(a) for a in arrs])


def _rs1_tail_wait(ssem, rsem, for_x, for_y, from_x, from_y, after):
    def body(fx_ref, fy_ref, lx_ref, ly_ref, ssem_ref, rsem_ref, after_ref, lx_o, ly_o):
        x, y, c = _position()
        for j, (src, dst) in enumerate(((fx_ref, lx_ref), (fy_ref, ly_ref))):
            d = pltpu.make_async_remote_copy(src_ref=src, dst_ref=dst, send_sem=ssem_ref.at[j], recv_sem=rsem_ref.at[j],
                                             device_id=(x, y, c), device_id_type=MESH)
            d.wait_recv()
            d.wait_send()

    return pl.pallas_call(
        body, name="rs1_tail_wait", out_shape=(_hbm_like(from_x), _hbm_like(from_y)),
        in_specs=(HBM_SPEC,) * 4 + (SEM_SPEC, SEM_SPEC, ANY_SPEC), out_specs=(HBM_SPEC, HBM_SPEC),
        input_output_aliases={2: 0, 3: 1},
        compiler_params=pltpu.CompilerParams(has_side_effects=SPLIT_EFFECT),
    )(for_x, for_y, from_x, from_y, ssem, rsem, after)


RSA_PIECES = MIX_PIECES + F2_PIECES
RSA_ROWS = _group_rows(RSA_PIECES)
RSA_OFF = {k: PIECE_OFF[k] - PIECE_OFF[RSA_PIECES[0]] for k in RSA_PIECES}
RSA_BLOCK = 192


def _rsa_rows(ref, k):
    return ref.at[pl.ds(RSA_OFF[k], PIECE_ROWS[k]), :]


def _rsa_level1_start(grads, rx1, thru, name):
    keys = sorted(grads)
    n = len(keys)

    def body(*refs):
        srcs = _weight_pieces(**{k + "_ref": ref for k, ref in zip(keys, refs[:n])})
        rx1_ref, sa, ra = refs[n], refs[n + 2], refs[n + 3]
        x, y, c = _position()
        for j, chip in enumerate([(x, y), (1 - x, y), (x, 1 - y), (1 - x, 1 - y)]):
            for k in sorted(srcs):
                pltpu.make_async_remote_copy(
                    src_ref=_block_rows(srcs, k, (*chip, 1 - c)), dst_ref=_rsa_rows(rx1_ref.at[j], k),
                    send_sem=sa.at[j], recv_sem=ra.at[j], device_id=(x, y, 1 - c), device_id_type=MESH).start()

    dma = pltpu.SemaphoreType.DMA
    arrs = tuple(grads[k] for k in keys) + (rx1, thru)
    outs = pl.pallas_call(
        body, name=name, out_shape=(dma((4,)), dma((4,))) + tuple(_hbm_like(a) for a in arrs),
        in_specs=(HBM_SPEC,) * len(arrs), out_specs=(SEM_SPEC,) * 2 + (HBM_SPEC,) * len(arrs),
        input_output_aliases={i: i + 2 for i in range(len(arrs))},
        compiler_params=pltpu.CompilerParams(has_side_effects=SPLIT_EFFECT),
    )(*[_in_hbm(a) for a in arrs])
    return outs[0], outs[1], dict(zip(keys, outs[2:2 + n])), outs[2 + n], outs[3 + n]


def _rsa_sums_and_send(started, dwint, dwout, dw2, rx1, rx2, thru):
    nblk = RSA_ROWS // RSA_BLOCK
    nstart = len(started)

    def body(*refs):
        di_ref, do_ref, d2_ref, rx1_ref, rx2_ref = refs[:5]
        l1_sems = refs[6:6 + 2 * nstart]
        sb, rb, tx_ref, acc_ref = refs[6 + 2 * nstart:10 + 2 * nstart]
        own_buf, rx_buf, tx_buf, acc_buf, in_sems, out_sems = refs[13 + 2 * nstart:]
        x, y, c = _position()
        srcs = _weight_pieces(wi_ref=di_ref, wo_ref=do_ref, w2_ref=d2_ref)
        chips = [(x, y), (1 - x, y), (x, 1 - y), (1 - x, 1 - y)]

        for g, (pieces, _, _) in enumerate(started):
            ssem, rsem = l1_sems[2 * g], l1_sems[2 * g + 1]
            for j in range(4):
                rows = rx1_ref.at[j, pl.ds(RSA_OFF[pieces[0]], _group_rows(pieces)), :]
                d = pltpu.make_async_remote_copy(src_ref=rows, dst_ref=rows, send_sem=ssem.at[j], recv_sem=rsem.at[j],
                                                 device_id=(x, y, c), device_id_type=MESH)
                d.wait_recv()
                d.wait_send()

        def start_loads(j):
            s = j % 2
            for k in RSA_PIECES:
                pltpu.make_async_copy(_block_rows(srcs, k, (*chips[j], c)), _rsa_rows(own_buf.at[s], k),
                                      in_sems.at[2 * s]).start()
            pltpu.make_async_copy(rx1_ref.at[j], rx_buf.at[s], in_sems.at[2 * s + 1]).start()

        def wait_loads(j):
            s = j % 2
            pltpu.make_async_copy(rx1_ref.at[j], own_buf.at[s], in_sems.at[2 * s]).wait()
            pltpu.make_async_copy(rx1_ref.at[j], rx_buf.at[s], in_sems.at[2 * s + 1]).wait()

        def store(j):
            if j == 0:
                return pltpu.make_async_copy(acc_buf, acc_ref, out_sems.at[2])
            return pltpu.make_async_copy(tx_buf.at[j % 2], tx_ref.at[j - 1], out_sems.at[j % 2])

        def send(j):
            return pltpu.make_async_remote_copy(src_ref=tx_ref.at[j - 1], dst_ref=rx2_ref.at[j - 1], send_sem=sb.at[j - 1],
                                                recv_sem=rb.at[j - 1], device_id=(*chips[j], c), device_id_type=MESH)

        start_loads(0)
        for j in range(4):
            s = j % 2
            if j + 1 < 4:
                start_loads(j + 1)
            wait_loads(j)
            if j == 3:
                store(1).wait()
                send(1).start()

            def add(i, carry, j=j, s=s):
                rows = pl.ds(pl.multiple_of(i * RSA_BLOCK, 16), RSA_BLOCK)
                tot = own_buf[s, rows, :].astype(f32) + rx_buf[s, rows, :].astype(f32)
                if j == 0:
                    acc_buf[rows, :] = tot
                else:
                    tx_buf[s, rows, :] = tot.astype(bf16)
                return carry

            lax.fori_loop(0, nblk, add, 0)
            store(j).start()
        store(0).wait()
        for j in (2, 3):
            store(j).wait()
            send(j).start()

    dma = pltpu.SemaphoreType.DMA
    passed = (rx1, rx2, thru)
    outs = pl.pallas_call(
        body, name="rsa_sums_and_send",
        in_specs=(HBM_SPEC,) * 6 + (SEM_SPEC,) * (2 * nstart),
        out_specs=(SEM_SPEC,) * 2 + (HBM_SPEC,) * 5,
        out_shape=(dma((3,)), dma((3,)), pltpu.HBM((3, RSA_ROWS, D), bf16), pltpu.HBM((RSA_ROWS, D), f32))
        + tuple(_hbm_like(a) for a in passed),
        input_output_aliases={3: 4, 4: 5, 5: 6},
        scratch_shapes=[pltpu.VMEM((2, RSA_ROWS, D), bf16), pltpu.VMEM((2, RSA_ROWS, D), bf16),
                        pltpu.VMEM((2, RSA_ROWS, D), bf16), pltpu.VMEM((RSA_ROWS, D), f32),
                        dma((4,)), dma((3,))],
        compiler_params=pltpu.CompilerParams(has_side_effects=SPLIT_EFFECT, vmem_limit_bytes=VMEM_LIMIT_V7X),
    )(*[_in_hbm(a) for a in (dwint, dwout, dw2) + passed], *[sem for _, sa, ra in started for sem in (sa, ra)])
    sb, rb, tx, acc, _, rx2, thru = outs
    return sb, rb, tx, acc, rx2, thru


def _rsa_level2_wait(sb, rb, tx, rx2, after):
    def body(tx_ref, rx2_ref, sb_ref, rb_ref, after_ref, rx2_o):
        x, y, c = _position()
        for j in range(3):
            d = pltpu.make_async_remote_copy(src_ref=tx_ref.at[j], dst_ref=rx2_ref.at[j], send_sem=sb_ref.at[j],
                                             recv_sem=rb_ref.at[j], device_id=(x, y, c), device_id_type=MESH)
            d.wait_recv()
            d.wait_send()

    return pl.pallas_call(
        body, name="rsa_level2_wait", out_shape=_hbm_like(rx2),
        in_specs=(HBM_SPEC, HBM_SPEC, SEM_SPEC, SEM_SPEC, ANY_SPEC), out_specs=HBM_SPEC,
        input_output_aliases={1: 0},
        compiler_params=pltpu.CompilerParams(has_side_effects=SPLIT_EFFECT),
    )(tx, rx2, sb, rb, after)


def _adamw_math(w, g, m, v):
    m = ADAM_B1 * m + (1.0 - ADAM_B1) * g
    v = ADAM_B2 * v + (1.0 - ADAM_B2) * (g * g)
    m_hat = m / (1.0 - ADAM_B1 ** ADAM_STEP)
    v_hat = v / (1.0 - ADAM_B2 ** ADAM_STEP)
    delta = -ADAM_LR * (m_hat / (jnp.sqrt(v_hat) + ADAM_EPS) + ADAM_WD * w)
    return delta, m, v


def _adamw_big(pieces, ws, ms, vs, own, landed, name):
    npiece = len(pieces)
    nland = sum(a.shape[0] if a.ndim == 3 else 1 for a in landed)
    rmax = max(PIECE_ROWS[k] for k in pieces)
    half = FS // 2

    def segments(k):
        if k in G1_PIECES:
            return [(hf * len(G1_PIECES) * half + k * half, hf * half, half) for hf in (0, 1)]
        return [(RSA_OFF[k], 0, PIECE_ROWS[k])]

    def body(*refs):
        ins = (refs[0:npiece], refs[npiece:2 * npiece], refs[2 * npiece:3 * npiece])
        own_ref = refs[3 * npiece]
        nin = 3 * npiece + 1 + len(landed)
        land_refs = []
        for ref, a in zip(refs[3 * npiece + 1:nin], landed):
            land_refs += [ref.at[j] for j in range(a.shape[0])] if a.ndim == 3 else [ref]
        out_refs = refs[nin:nin + 4 * npiece]
        inb, landb, outb, in_sems, land_sems, out_sems = refs[nin + 4 * npiece:]

        def loads(i):
            s, k = i % 2, pieces[i]
            r = PIECE_ROWS[k]
            cps = [pltpu.make_async_copy(ins[q][i].at[0], inb.at[s, q, pl.ds(0, r), :], in_sems.at[4 * s + q])
                   for q in range(3)]
            waits = list(cps)
            for src0, dst0, n in segments(k):
                cps.append(pltpu.make_async_copy(own_ref.at[pl.ds(src0, n), :], inb.at[s, 3, pl.ds(dst0, n), :],
                                                 in_sems.at[4 * s + 3]))
                for p in range(nland):
                    cps.append(pltpu.make_async_copy(land_refs[p].at[pl.ds(src0, n), :],
                                                     landb.at[s, p, pl.ds(dst0, n), :], land_sems.at[nland * s + p]))
            own_rows = inb.at[s, 3, pl.ds(0, r), :]
            waits.append(pltpu.make_async_copy(own_rows, own_rows, in_sems.at[4 * s + 3]))
            for p in range(nland):
                rows = landb.at[s, p, pl.ds(0, r), :]
                waits.append(pltpu.make_async_copy(rows, rows, land_sems.at[nland * s + p]))
            return cps, waits

        def stores(i):
            s, r = i % 2, PIECE_ROWS[pieces[i]]
            return [pltpu.make_async_copy(outb.at[s, q, pl.ds(0, r), :], out_refs[q * npiece + i].at[0],
                                          out_sems.at[4 * s + q]) for q in range(4)]

        for cp in loads(0)[0]:
            cp.start()
        for i in range(npiece):
            s, r = i % 2, PIECE_ROWS[pieces[i]]
            if i + 1 < npiece:
                for cp in loads(i + 1)[0]:
                    cp.start()
            for cp in loads(i)[1]:
                cp.wait()
            if i >= 2:
                for cp in stores(i - 2):
                    cp.wait()
            g = inb[s, 3, 0:r, :]
            for p in range(nland):
                g = g + landb[s, p, 0:r, :].astype(f32)
            d, nm, nv = _adamw_math(inb[s, 0, 0:r, :], g, inb[s, 1, 0:r, :], inb[s, 2, 0:r, :])
            outb[s, 0, 0:r, :] = g
            outb[s, 1, 0:r, :] = d
            outb[s, 2, 0:r, :] = nm
            outb[s, 3, 0:r, :] = nv
            for cp in stores(i):
                cp.start()
        for i in range(max(npiece - 2, 0), npiece):
            for cp in stores(i):
                cp.wait()

    hbm = pl.BlockSpec(memory_space=pl.ANY)
    outs = pl.pallas_call(
        body, in_specs=[hbm] * (3 * npiece + 1 + len(landed)), out_specs=[hbm] * (4 * npiece),
        out_shape=tuple(SDS(w.shape, f32) for _ in range(4) for w in ws),
        scratch_shapes=[pltpu.VMEM((2, 4, rmax, D), f32), pltpu.VMEM((2, nland, rmax, D), bf16),
                        pltpu.VMEM((2, 4, rmax, D), f32),
                        pltpu.SemaphoreType.DMA((8,)), pltpu.SemaphoreType.DMA((2 * nland,)),
                        pltpu.SemaphoreType.DMA((8,))],
        compiler_params=_cparams(None, VMEM_LIMIT_V7X), name=name)(*ws, *ms, *vs, own, *landed)
    return [list(outs[q * npiece:(q + 1) * npiece]) for q in range(4)]


def _adamw_small(ws, ms, vs, gs, name):
    n = len(ws)

    def body(*refs):
        w_refs, m_refs, v_refs, g_refs = refs[0:n], refs[n:2 * n], refs[2 * n:3 * n], refs[3 * n:4 * n]
        outs = refs[4 * n:]
        for i in range(n):
            d, nm, nv = _adamw_math(w_refs[i][...], g_refs[i][...], m_refs[i][...], v_refs[i][...])
            outs[i][...] = d
            outs[n + i][...] = nm
            outs[2 * n + i][...] = nv

    outs = pl.pallas_call(
        body, out_shape=tuple(SDS(w.shape, f32) for _ in range(3) for w in ws), name=name)(*ws, *ms, *vs, *gs)
    return [list(outs[q * n:(q + 1) * n]) for q in range(3)]


WEIGHTS = ("ffn1_norm", "ffn1_w_gate", "ffn1_w_up", "ffn1_w_down", "mix_norm", "w_in", "q_norm", "k_norm",
           "attn_sinks", "rel_bias", "pool_w", "pool_scale", "w_out", "ffn2_norm", "ffn2_w_gate", "ffn2_w_up",
           "ffn2_w_down")
BIG = (("ffn1_w_gate", True), ("ffn1_w_up", True), ("ffn1_w_down", False), ("w_in", True), ("w_out", False),
       ("ffn2_w_gate", True), ("ffn2_w_up", True), ("ffn2_w_down", False))


def kernel(x, ffn1_norm, ffn1_w_gate, ffn1_w_up, ffn1_w_down, mix_norm, w_in, q_norm, k_norm, attn_sinks, rel_bias, pool_w, pool_scale, w_out, ffn2_norm, ffn2_w_gate, ffn2_w_up, ffn2_w_down, loss_target, m_ffn1_norm, m_ffn1_w_gate, m_ffn1_w_up, m_ffn1_w_down, m_mix_norm, m_w_in, m_q_norm, m_k_norm, m_attn_sinks, m_rel_bias, m_pool_w, m_pool_scale, m_w_out, m_ffn2_norm, m_ffn2_w_gate, m_ffn2_w_up, m_ffn2_w_down, v_ffn1_norm, v_ffn1_w_gate, v_ffn1_w_up, v_ffn1_w_down, v_mix_norm, v_w_in, v_q_norm, v_k_norm, v_attn_sinks, v_rel_bias, v_pool_w, v_pool_scale, v_w_out, v_ffn2_norm, v_ffn2_w_gate, v_ffn2_w_up, v_ffn2_w_down):
    args = dict(locals())
    w = {n: args[n] for n in WEIGHTS}
    m = {n: args["m_" + n] for n in WEIGHTS}
    v = {n: args["v_" + n] for n in WEIGHTS}

    as_rows = lambda a, tr: jnp.swapaxes(a, 1, 2) if tr else a
    shard = jnp.concatenate([as_rows(w[n], tr)[0].astype(bf16) for n, tr in BIG], axis=0)
    exchanges = _GatheredWeights(shard, x[0], ffn1_norm)
    gx, (dw1, _, _, _), small = _local_step(
        x[0], loss_target[0], exchanges, ffn1_norm, mix_norm, ffn2_norm, q_norm, k_norm, attn_sinks,
        rel_bias, pool_w[0], pool_scale)

    nrows1 = len(G1_PIECES) * FS
    for_x, for_y, own1, small_tot = _reduce_scatter_ffn1_head(dw1, _pack_small(small))
    ssem, rsem, for_x, for_y, from_x, from_y, small_tot = _rs1_tail_start(
        for_x, for_y, lax.empty((nrows1, D), bf16), lax.empty((nrows1, D), bf16), small_tot)
    own_rest, landed_rest = exchanges.mix_ffn2_grads_parts(small_tot)

    grads, deltas, new_m, new_v = {}, {}, {}, {}
    rest = [k for k in range(len(BIG)) if k not in G1_PIECES]
    rows_of = lambda t, ks: [as_rows(t[BIG[k][0]], BIG[k][1]) for k in ks]
    rest_out = _adamw_big(rest, rows_of(w, rest), rows_of(m, rest), rows_of(v, rest), own_rest, [landed_rest],
                          "adamw_rest")
    from_x, from_y = _rs1_tail_wait(ssem, rsem, for_x, for_y, from_x, from_y, rest_out[0][0])
    ffn1 = list(G1_PIECES)
    ffn1_out = _adamw_big(ffn1, rows_of(w, ffn1), rows_of(m, ffn1), rows_of(v, ffn1), own1, [from_x, from_y],
                          "adamw_ffn1")
    for ks, out in ((rest, rest_out), (ffn1, ffn1_out)):
        for i, k in enumerate(ks):
            n, tr = BIG[k]
            grads[n], deltas[n], new_m[n], new_v[n] = [as_rows(o[i], tr) for o in out]
    small_names = [n for n in SMALL_NAMES if n != "loss"]
    for n in small_names:
        grads[n] = _unpack_small(small_tot, n)
    ds, nms, nvs = _adamw_small([w[n] for n in small_names], [m[n] for n in small_names], [v[n] for n in small_names],
                                [grads[n] for n in small_names], "adamw_small")
    for i, n in enumerate(small_names):
        deltas[n], new_m[n], new_v[n] = ds[i], nms[i], nvs[i]
    loss = small_tot[LOSS_ROW, 0]
    return (loss, gx[None], *[grads[n] for n in WEIGHTS], *[deltas[n] for n in WEIGHTS],
            *[new_m[n] for n in WEIGHTS], *[new_v[n] for n in WEIGHTS])
```

```python
import jax
import jax.numpy as jnp
import numpy as np
from jax import lax
from jax.experimental import pallas as pl
from jax.experimental.pallas import tpu as pltpu

f32, bf16, i32 = jnp.float32, jnp.bfloat16, jnp.int32
SDS = jax.ShapeDtypeStruct

D = 1024
F = 2816
HD = 64
NH = 8
NKV = 2
GQA = NH // NKV
DATTN = NH * HD
DKV = NKV * HD
DPOOL = 512
POOL_WINDOWS = (2, 4, 8, 16)
PGD = DPOOL // len(POOL_WINDOWS)
DIN = DATTN + 2 * DKV + DPOOL
DMIX = DATTN + DPOOL
BLK = 128
NBUCK = 32
MAX_DISTANCE = 128
EPS = 1e-6
NEG = -1e30
SCALE = HD ** -0.5

ADAM_LR, ADAM_B1, ADAM_B2, ADAM_EPS, ADAM_WD, ADAM_STEP = 0.001, 0.9, 0.999, 1e-08, 0.01, 10

NDEV = 8
FS = F // NDEV
INS = DIN // NDEV
OUTS = DMIX // NDEV
PIECE_ROWS = (FS, FS, FS, INS, OUTS, FS, FS, FS)
PIECE_OFF = tuple(int(v) for v in np.cumsum((0,) + PIECE_ROWS[:-1]))
PACK_ROWS = sum(PIECE_ROWS)

VMEM_LIMIT_V7X = 56 * 1024 * 1024

MESH = pl.DeviceIdType.MESH


def _cparams(sem=None, vmem=None):
    return pltpu.CompilerParams(dimension_semantics=sem, vmem_limit_bytes=vmem)


def _nt(a, b):
    return lax.dot_general(a, b, (((1,), (1,)), ((), ())), preferred_element_type=f32)


def _tn(a, b):
    return lax.dot_general(a, b, (((0,), (0,)), ((), ())), preferred_element_type=f32)


def _nn(a, b):
    return jnp.dot(a, b, preferred_element_type=f32)


def _sigmoid(x):
    return 1.0 / (1.0 + jnp.exp(-x))


def _norm_fwd(x, g, name):
    T = x.shape[0]
    tm = min(512, T)

    def body(x_ref, g_ref, h_ref):
        xv = x_ref[...]
        r = lax.rsqrt(jnp.mean(xv * xv, axis=-1, keepdims=True) + EPS)
        h_ref[...] = (xv * r * g_ref[...]).astype(bf16)

    return pl.pallas_call(
        body, grid=(T // tm,),
        in_specs=[pl.BlockSpec((tm, D), lambda i: (i, 0)), pl.BlockSpec((1, D), lambda i: (0, 0))],
        out_specs=pl.BlockSpec((tm, D), lambda i: (i, 0)),
        out_shape=SDS((T, D), bf16), name=name)(x, g)


FFN_ROW_CHUNK = 256


def _ffn_tiles(T):
    return min(1024, T), 256


def _ffn_fwd(h, w, x, target, next_gain, name):
    T = h.shape[0]
    tm, tf = _ffn_tiles(T)
    nf = F // tf
    with_loss = target is not None
    assert with_loss != (next_gain is not None)

    def body(*refs):
        if with_loss:
            h_ref, w_ref, x_hbm, t_hbm, xo_ref, g_ref, u_ref, dyb_ref, loss_ref, tbuf, sem = refs
        else:
            h_ref, w_ref, x_hbm, gain_ref, xo_ref, g_ref, u_ref, hn_ref, sem = refs
        fi = pl.program_id(0)

        @pl.when(fi == 0)
        def _():
            cp = pltpu.make_async_copy(x_hbm, xo_ref, sem)
            cp.start()
            cp.wait()

        wgu = w_ref[0:2].reshape(2 * tf, D)
        for r in range(0, T, tm):
            rows = slice(r, r + tm)
            gu = _nt(h_ref[rows, :], wgu)
            gate, up = gu[:, :tf], gu[:, tf:]
            act = gate * _sigmoid(gate) * up
            g_ref[0, rows, :] = gate.astype(bf16)
            u_ref[0, rows, :] = up.astype(bf16)
            xo_ref[rows, :] += _nn((0.5 * act).astype(bf16), w_ref[2])

        if with_loss:
            @pl.when(fi == nf - 1)
            def _():
                lanes = jnp.zeros((1, 128), f32)
                for r in range(0, T, tm):
                    rows = slice(r, r + tm)
                    cp = pltpu.make_async_copy(t_hbm.at[pl.ds(r, tm), :], tbuf, sem)
                    cp.start()
                    cp.wait()
                    e = xo_ref[rows, :] - tbuf[...]
                    dy = e * (1.0 / D)
                    xo_ref[rows, :] = dy
                    dyb_ref[rows, :] = (0.5 * dy).astype(bf16)
                    col = jnp.sum(e * e, axis=0, keepdims=True) * (0.5 / D)
                    for k in range(D // 128):
                        lanes = lanes + col[:, 128 * k:128 * (k + 1)]
                loss_ref[...] = lanes
        else:
            @pl.when(fi == nf - 1)
            def _():
                for r in range(0, T, FFN_ROW_CHUNK):
                    rows = slice(r, r + FFN_ROW_CHUNK)
                    xv = xo_ref[rows, :]
                    rstd = lax.rsqrt(jnp.mean(xv * xv, axis=-1, keepdims=True) + EPS)
                    hn_ref[rows, :] = (xv * rstd * gain_ref[...]).astype(bf16)

    tok = pl.BlockSpec((T, D), lambda f: (0, 0))
    act_spec = pl.BlockSpec((1, T, tf), lambda f: (f, 0, 0))
    hbm = pl.BlockSpec(memory_space=pl.ANY)
    in_specs = [tok, pl.BlockSpec((3, tf, D), lambda f: (0, f, 0)), hbm]
    out_specs = [tok, act_spec, act_spec]
    out_shape = [SDS((T, D), f32), SDS((nf, T, tf), bf16), SDS((nf, T, tf), bf16)]
    scratch = [pltpu.SemaphoreType.DMA]
    args = [h, w, x]
    if with_loss:
        in_specs.append(hbm)
        args.append(target)
        out_specs += [tok, pl.BlockSpec((1, 128), lambda f: (0, 0))]
        out_shape += [SDS((T, D), bf16), SDS((1, 128), f32)]
        scratch = [pltpu.VMEM((tm, D), f32)] + scratch
    else:
        in_specs.append(pl.BlockSpec((1, D), lambda f: (0, 0)))
        args.append(next_gain)
        out_specs.append(tok)
        out_shape.append(SDS((T, D), bf16))
    return pl.pallas_call(
        body, grid=(nf,), in_specs=in_specs, out_specs=out_specs, out_shape=tuple(out_shape), scratch_shapes=scratch,
        compiler_params=_cparams(("arbitrary",), VMEM_LIMIT_V7X), name=name)(*args)


def _ffn_bwd(dob, h, gate, up, w, norm, name):
    x, g, dres = norm
    T = h.shape[0]
    _, tf = _ffn_tiles(T)
    nf = F // tf
    tm = min(512, T)
    nchunk = T // tm

    def body(do_hbm, h_hbm, g_ref, u_ref, w_ref, x_hbm, gain_ref, dr_hbm, dx_hbm, dxb_hbm, dg_ref, dw_ref,
             do_v, h_v, dh_acc, dgu_s, act_s, xbuf, rbuf, obuf, obb, sems, in_sems, out_sems):
        fi = pl.program_id(0)

        @pl.when(fi == 0)
        def _():
            loads = [pltpu.make_async_copy(do_hbm, do_v, sems.at[0]), pltpu.make_async_copy(h_hbm, h_v, sems.at[1])]
            for cp in loads:
                cp.start()
            dh_acc[...] = jnp.zeros_like(dh_acc)
            for cp in loads:
                cp.wait()

        wgu = w_ref[0:2].reshape(2 * tf, D)
        for r in range(0, T, FFN_ROW_CHUNK):
            rows = slice(r, r + FFN_ROW_CHUNK)
            dov = do_v[rows, :]
            gv = g_ref[0, rows, :].astype(f32)
            uv = u_ref[0, rows, :].astype(f32)
            sg = _sigmoid(gv)
            sil = gv * sg
            dact = _nt(dov, w_ref[2])
            dup = dact * sil
            dgate = dact * uv * (sg * (1.0 + gv * (1.0 - sg)))
            dgu = jnp.concatenate([dgate.astype(bf16), dup.astype(bf16)], axis=1)
            dgu_s[rows, :] = dgu
            act_s[rows, :] = (sil * uv).astype(bf16)
            dh_acc[rows, :] += _nn(dgu, wgu)
        dw_ref[0:2] = _tn(dgu_s[...], h_v[...]).reshape(2, tf, D).astype(bf16)
        dw_ref[2] = _tn(act_s[...], do_v[...]).astype(bf16)

        @pl.when(fi == nf - 1)
        def _():
            def loads(i):
                s, rows = i % 2, pl.ds(i * tm, tm)
                return [pltpu.make_async_copy(x_hbm.at[rows, :], xbuf.at[s], in_sems.at[2 * s]),
                        pltpu.make_async_copy(dr_hbm.at[rows, :], rbuf.at[s], in_sems.at[2 * s + 1])]

            def stores(i):
                s, rows = i % 2, pl.ds(i * tm, tm)
                return [pltpu.make_async_copy(obuf.at[s], dx_hbm.at[rows, :], out_sems.at[2 * s]),
                        pltpu.make_async_copy(obb.at[s], dxb_hbm.at[rows, :], out_sems.at[2 * s + 1])]

            for cp in loads(0):
                cp.start()
            dg = jnp.zeros((1, D), f32)
            for i in range(nchunk):
                s = i % 2
                if i + 1 < nchunk:
                    for cp in loads(i + 1):
                        cp.start()
                for cp in loads(i):
                    cp.wait()
                if i >= 2:
                    for cp in stores(i - 2):
                        cp.wait()
                xv = xbuf[s]
                rstd = lax.rsqrt(jnp.mean(xv * xv, axis=-1, keepdims=True) + EPS)
                xh = xv * rstd
                dhv = dh_acc[i * tm:(i + 1) * tm, :]
                dxh = dhv * gain_ref[...]
                dx = rbuf[s] + rstd * (dxh - xh * jnp.mean(dxh * xh, axis=-1, keepdims=True))
                obuf[s] = dx
                obb[s] = dx.astype(bf16)
                dg = dg + jnp.sum(dhv * xh, axis=0, keepdims=True)
                for cp in stores(i):
                    cp.start()
            dg_ref[...] = dg
            for i in range(max(nchunk - 2, 0), nchunk):
                for cp in stores(i):
                    cp.wait()

    act_spec = pl.BlockSpec((1, T, tf), lambda f: (f, 0, 0))
    wspec = pl.BlockSpec((3, tf, D), lambda f: (0, f, 0))
    vec = pl.BlockSpec((1, D), lambda f: (0, 0))
    hbm = pl.BlockSpec(memory_space=pl.ANY)
    return pl.pallas_call(
        body, grid=(nf,),
        in_specs=[hbm, hbm, act_spec, act_spec, wspec, hbm, vec, hbm],
        out_specs=[hbm, hbm, vec, wspec],
        out_shape=(SDS((T, D), f32), SDS((T, D), bf16), SDS((1, D), f32), SDS((3, F, D), bf16)),
        scratch_shapes=[pltpu.VMEM((T, D), bf16), pltpu.VMEM((T, D), bf16), pltpu.VMEM((T, D), f32),
                        pltpu.VMEM((T, 2 * tf), bf16), pltpu.VMEM((T, tf), bf16),
                        pltpu.VMEM((2, tm, D), f32), pltpu.VMEM((2, tm, D), f32), pltpu.VMEM((2, tm, D), f32),
                        pltpu.VMEM((2, tm, D), bf16),
                        pltpu.SemaphoreType.DMA((2,)), pltpu.SemaphoreType.DMA((4,)), pltpu.SemaphoreType.DMA((4,))],
        compiler_params=_cparams(("arbitrary",), VMEM_LIMIT_V7X), name=name)(dob, h, gate, up, w, x, g, dres)


def _in_proj_fwd(h, wint, name):
    T = h.shape[0]
    tm = min(512, T)

    def body(h_ref, w_ref, z_ref):
        z_ref[...] = _nt(h_ref[...], w_ref[...])

    return pl.pallas_call(
        body, grid=(T // tm,),
        in_specs=[pl.BlockSpec((tm, D), lambda i: (i, 0)), pl.BlockSpec((DIN, D), lambda i: (0, 0))],
        out_specs=pl.BlockSpec((tm, DIN), lambda i: (i, 0)),
        out_shape=SDS((T, DIN), f32), name=name)(h, wint)


def _in_proj_bwd(dz, wint, h, norm, out_scale, name):
    x, g, dres = norm
    T = h.shape[0]
    tm = min(512, T)
    nt = T // tm

    def body(dz_ref, w_ref, h_ref, x_ref, g_ref, dr_ref, dx_ref, dxb_ref, dg_ref, dw_ref, acc):
        i = pl.program_id(0)
        dzb = dz_ref[...].astype(bf16)
        dhv = _nn(dzb, w_ref[...])
        part = _tn(dzb, h_ref[...])
        xv = x_ref[...]
        rstd = lax.rsqrt(jnp.mean(xv * xv, axis=-1, keepdims=True) + EPS)
        xh = xv * rstd
        dxh = dhv * g_ref[...]
        dx = dr_ref[...] + rstd * (dxh - xh * jnp.mean(dxh * xh, axis=-1, keepdims=True))
        dx_ref[...] = dx
        dxb_ref[...] = (out_scale * dx).astype(bf16)
        dg = jnp.sum(dhv * xh, axis=0, keepdims=True)

        @pl.when(i == 0)
        def _():
            acc[...] = part
            dg_ref[...] = dg

        @pl.when(i > 0)
        def _():
            acc[...] += part
            dg_ref[...] += dg

        @pl.when(i == nt - 1)
        def _():
            dw_ref[...] = acc[...].astype(bf16)

    wspec = pl.BlockSpec((DIN, D), lambda i: (0, 0))
    tok = pl.BlockSpec((tm, D), lambda i: (i, 0))
    vec = pl.BlockSpec((1, D), lambda i: (0, 0))
    return pl.pallas_call(
        body, grid=(nt,),
        in_specs=[pl.BlockSpec((tm, DIN), lambda i: (i, 0)), wspec, tok, tok, vec, tok],
        out_specs=[tok, tok, vec, wspec],
        out_shape=(SDS((T, D), f32), SDS((T, D), bf16), SDS((1, D), f32), SDS((DIN, D), bf16)),
        scratch_shapes=[pltpu.VMEM((DIN, D), f32)],
        compiler_params=_cparams(("arbitrary",)), name=name)(dz, wint, h, x, g, dres)


def _out_proj_fwd(ymix, wout, x, g, name):
    T = x.shape[0]
    tm = min(512, T)

    def body(y_ref, w_ref, x_ref, g_ref, o_ref, h_ref):
        o = x_ref[...] + _nn(y_ref[...], w_ref[...])
        o_ref[...] = o
        r = lax.rsqrt(jnp.mean(o * o, axis=-1, keepdims=True) + EPS)
        h_ref[...] = (o * r * g_ref[...]).astype(bf16)

    tok = pl.BlockSpec((tm, D), lambda i: (i, 0))
    return pl.pallas_call(
        body, grid=(T // tm,),
        in_specs=[pl.BlockSpec((tm, DMIX), lambda i: (i, 0)), pl.BlockSpec((DMIX, D), lambda i: (0, 0)), tok,
                  pl.BlockSpec((1, D), lambda i: (0, 0))],
        out_specs=[tok, tok], out_shape=(SDS((T, D), f32), SDS((T, D), bf16)), name=name)(ymix, wout, x, g)


def _out_proj_bwd(dxb, wout, ymix, name):
    T = dxb.shape[0]
    tm = min(512, T)
    nt = T // tm

    def body(dx_ref, w_ref, y_ref, dy_ref, dw_ref, acc):
        i = pl.program_id(0)
        dxv = dx_ref[...]
        dy_ref[...] = _nt(dxv, w_ref[...])
        part = _tn(y_ref[...], dxv)

        @pl.when(i == 0)
        def _():
            acc[...] = part

        @pl.when(i > 0)
        def _():
            acc[...] += part

        @pl.when(i == nt - 1)
        def _():
            dw_ref[...] = acc[...].astype(bf16)

    wspec = pl.BlockSpec((DMIX, D), lambda i: (0, 0))
    return pl.pallas_call(
        body, grid=(nt,),
        in_specs=[pl.BlockSpec((tm, D), lambda i: (i, 0)), wspec, pl.BlockSpec((tm, DMIX), lambda i: (i, 0))],
        out_specs=[pl.BlockSpec((tm, DMIX), lambda i: (i, 0)), wspec],
        out_shape=(SDS((T, DMIX), f32), SDS((DMIX, D), bf16)),
        scratch_shapes=[pltpu.VMEM((DMIX, D), f32)],
        compiler_params=_cparams(("arbitrary",)), name=name)(dxb, wout, ymix)


def _t5_bucket_table():
    ql = np.arange(BLK)[:, None]
    kl = np.arange(2 * BLK)[None, :]
    n = np.maximum(ql + BLK - kl, 0)
    max_exact = NBUCK // 2
    large = max_exact + (np.log(np.maximum(n, 1) / max_exact) / np.log(MAX_DISTANCE / max_exact)
                         * (NBUCK - max_exact)).astype(np.int32)
    large = np.minimum(large, NBUCK - 1)
    return np.where(n < max_exact, n, large).astype(np.int32)


def _fill_bias(bk_ref, rb_ref, bias_scr):
    bk = bk_ref[...]
    for h in range(NH):
        def step(b, acc, h=h):
            return acc + jnp.where(bk == b, rb_ref[b, h], 0.0)
        bias_scr[h] = lax.fori_loop(0, NBUCK, step, jnp.zeros((BLK, 2 * BLK), f32))


MIX_SUB = 4


class _Window:
    def __init__(self, zc_ref, zp_ref, n, s):
        self.blk = n * MIX_SUB + s
        self.cur = lambda a, b: zc_ref[s * BLK:(s + 1) * BLK, a:b]
        self.prev = (lambda a, b: zp_ref[:, a:b]) if s == 0 else (lambda a, b: zc_ref[(s - 1) * BLK:s * BLK, a:b])


def _attn_qkv(win, kh, qg, kg):
    kc = DATTN + HD * kh
    vc = DATTN + DKV + HD * kh
    kx = jnp.concatenate([win.prev(kc, kc + HD), win.cur(kc, kc + HD)], axis=0)
    vx = jnp.concatenate([win.prev(vc, vc + HD), win.cur(vc, vc + HD)], axis=0)
    qx = jnp.concatenate([win.cur(HD * (GQA * kh + g), HD * (GQA * kh + g + 1)) for g in range(GQA)], axis=0)
    rq = lax.rsqrt(jnp.mean(qx * qx, axis=-1, keepdims=True) + EPS)
    rk = lax.rsqrt(jnp.mean(kx * kx, axis=-1, keepdims=True) + EPS)
    qhat, khat = qx * rq, kx * rk
    return dict(qhat=qhat, khat=khat, rq=rq, rk=rk, qsb=(qhat * (qg * SCALE)).astype(bf16),
                knb=(khat * kg).astype(bf16), vb=vx.astype(bf16))


def _window_masks(n):
    row = lax.broadcasted_iota(i32, (GQA * BLK, 2 * BLK), 0) & (BLK - 1)
    col = lax.broadcasted_iota(i32, (GQA * BLK, 2 * BLK), 1)
    band = (col > row) & (col <= row + BLK)
    return band & ((col >= BLK) | (n > 0)), band


def _attn_probs(a, kh, sk_ref, bias_scr, mask):
    s = _nt(a["qsb"], a["knb"]) + bias_scr[GQA * kh:GQA * (kh + 1)].reshape(GQA * BLK, 2 * BLK)
    s = jnp.where(mask, s, NEG)
    ridx = lax.broadcasted_iota(i32, (GQA * BLK, 1), 0)
    sink = jnp.full((GQA * BLK, 1), sk_ref[GQA * kh + GQA - 1], f32)
    for g in range(GQA - 2, -1, -1):
        sink = jnp.where(ridx < (g + 1) * BLK, sk_ref[GQA * kh + g], sink)
    m = jnp.maximum(jnp.max(s, axis=-1, keepdims=True), sink)
    e = jnp.exp(s - m)
    den = jnp.sum(e, axis=-1, keepdims=True) + jnp.exp(sink - m)
    return e / den


POOL_STEPS = {2: (1,), 4: (1, 2), 8: (1, 2, 4), 16: (1, 2, 4, 8)}


def _pool_group(win, g, w):
    n = win.blk
    c0 = DATTN + 2 * DKV + PGD * g
    uc = win.cur(c0, c0 + PGD)
    up = jnp.where(n > 0, win.prev(c0, c0 + PGD), 0.0)
    sm = jnp.concatenate([up, uc], axis=0)
    for k in POOL_STEPS[w]:
        sm = sm + pltpu.roll(sm, k, axis=0)
    pos = n * BLK + lax.broadcasted_iota(i32, (BLK, 1), 0) + 1
    cnt = jnp.minimum(pos, w).astype(f32)
    return sm[BLK:2 * BLK] / cnt - uc, cnt


def _mix_fwd(z, qg, kg, sinks, relb, bucket, pool_w, pscale, name):
    T = z.shape[0]
    step_rows = MIX_SUB * BLK
    nsteps = T // step_rows

    def body(zc_ref, zp_ref, qg_ref, kg_ref, sk_ref, rb_ref, bk_ref, pw_ref, ps_ref, y_ref, p_ref, bias_scr, yacc):
        n = pl.program_id(0)

        @pl.when(n == 0)
        def _():
            _fill_bias(bk_ref, rb_ref, bias_scr)

        first_mask, mask = _window_masks(n)
        for s in range(MIX_SUB):
            win = _Window(zc_ref, zp_ref, n, s)
            rows = slice(s * BLK, (s + 1) * BLK)
            for kh in range(NKV):
                a = _attn_qkv(win, kh, qg_ref[...], kg_ref[...])
                pb = _attn_probs(a, kh, sk_ref, bias_scr, first_mask if s == 0 else mask).astype(bf16)
                p_ref[s, GQA * kh:GQA * (kh + 1)] = pb.reshape(GQA, BLK, 2 * BLK)
                o = _nn(pb, a["vb"])
                for g in range(GQA):
                    hc = HD * (GQA * kh + g)
                    yacc[rows, hc:hc + HD] = o[g * BLK:(g + 1) * BLK]
            for g, w in enumerate(POOL_WINDOWS):
                pooled, _ = _pool_group(win, g, w)
                yp = _nn(pooled.astype(bf16), pw_ref[g].astype(bf16)) * ps_ref[:, PGD * g:PGD * (g + 1)]
                yacc[rows, DATTN + PGD * g:DATTN + PGD * (g + 1)] = yp
        y_ref[...] = yacc[...].astype(bf16)

    full = lambda *shape: pl.BlockSpec(shape, lambda n: (0,) * len(shape))
    smem = pl.BlockSpec(memory_space=pltpu.SMEM)
    return pl.pallas_call(
        body, grid=(nsteps,),
        in_specs=[pl.BlockSpec((step_rows, DIN), lambda n: (n, 0)),
                  pl.BlockSpec((BLK, DIN), lambda n: (jnp.maximum(n * MIX_SUB - 1, 0), 0)),
                  full(1, HD), full(1, HD), smem, smem, full(BLK, 2 * BLK),
                  full(len(POOL_WINDOWS), PGD, PGD), full(1, DPOOL)],
        out_specs=[pl.BlockSpec((step_rows, DMIX), lambda n: (n, 0)),
                   pl.BlockSpec((MIX_SUB, NH, BLK, 2 * BLK), lambda n: (n, 0, 0, 0))],
        out_shape=(SDS((T, DMIX), bf16), SDS((T // BLK, NH, BLK, 2 * BLK), bf16)),
        scratch_shapes=[pltpu.VMEM((NH, BLK, 2 * BLK), f32), pltpu.VMEM((step_rows, DMIX), f32)],
        compiler_params=_cparams(("arbitrary",)), name=name)(z, z, qg, kg, sinks, relb, bucket, pool_w, pscale)


def _mix_bwd(z, dy, probs, qg, kg, relb, bucket, pool_w, pscale, name):
    T = z.shape[0]
    step_rows = MIX_SUB * BLK
    nsteps = T // step_rows

    def body(zc_ref, zp_ref, dy_ref, p_ref, qg_ref, kg_ref, bk_ref, pw_ref, ps_ref,
             dz_ref, dqg_ref, dkg_ref, dsk_ref, drb_ref, dpw_ref, dps_ref, dbias_scr):
        n = pl.program_id(0)

        @pl.when(n == 0)
        def _():
            dbias_scr[...] = jnp.zeros_like(dbias_scr)
            dqg_ref[...] = jnp.zeros_like(dqg_ref)
            dkg_ref[...] = jnp.zeros_like(dkg_ref)
            dpw_ref[...] = jnp.zeros_like(dpw_ref)
            dps_ref[...] = jnp.zeros_like(dps_ref)

        qg, kg = qg_ref[...], kg_ref[...]
        for s in range(MIX_SUB):
            win = _Window(zc_ref, zp_ref, n, s)
            blk = win.blk
            rows = pl.ds(pl.multiple_of(blk * BLK, BLK), BLK)
            prow = pl.ds(pl.multiple_of(jnp.maximum(blk - 1, 0) * BLK, BLK), BLK)
            dyr = slice(s * BLK, (s + 1) * BLK)

            def into_prev(fn, s=s):
                if s == 0:
                    pl.when(n > 0)(fn)
                else:
                    fn()

            for kh in range(NKV):
                a = _attn_qkv(win, kh, qg, kg)
                pb = p_ref[s, GQA * kh:GQA * (kh + 1)].reshape(GQA * BLK, 2 * BLK)
                p = pb.astype(f32)
                do = jnp.concatenate([dy_ref[dyr, HD * (GQA * kh + g):HD * (GQA * kh + g + 1)] for g in range(GQA)],
                                     axis=0).astype(bf16)
                dv = _tn(pb, do)
                dp = _nt(do, a["vb"])
                delta = jnp.sum(p * dp, axis=-1, keepdims=True)
                ds = p * (dp - delta)
                for g in range(GQA):
                    dbias_scr[GQA * kh + g] += ds[g * BLK:(g + 1) * BLK]
                dsb = ds.astype(bf16)
                dqn = _nn(dsb, a["knb"]) * SCALE
                dkn = _tn(dsb, a["qsb"])
                qhat, khat = a["qhat"], a["khat"]
                dqg_ref[...] += jnp.sum(dqn * qhat, axis=0, keepdims=True)
                dkg_ref[...] += jnp.sum(dkn * khat, axis=0, keepdims=True)
                dqh = dqn * qg
                dq = a["rq"] * (dqh - qhat * jnp.mean(dqh * qhat, axis=-1, keepdims=True))
                dkh = dkn * kg
                dk = a["rk"] * (dkh - khat * jnp.mean(dkh * khat, axis=-1, keepdims=True))
                kc = DATTN + HD * kh
                vc = DATTN + DKV + HD * kh
                for g in range(GQA):
                    hc = HD * (GQA * kh + g)
                    dz_ref[rows, hc:hc + HD] = dq[g * BLK:(g + 1) * BLK]
                dz_ref[rows, kc:kc + HD] = dk[BLK:2 * BLK]
                dz_ref[rows, vc:vc + HD] = dv[BLK:2 * BLK]

                def kv_prev(dk=dk, dv=dv, kc=kc, vc=vc, prow=prow):
                    dz_ref[prow, kc:kc + HD] += dk[0:BLK]
                    dz_ref[prow, vc:vc + HD] += dv[0:BLK]

                into_prev(kv_prev)

            for g, w in enumerate(POOL_WINDOWS):
                c0 = DATTN + 2 * DKV + PGD * g
                pooled, cnt = _pool_group(win, g, w)
                pb = pooled.astype(bf16)
                wb = pw_ref[g].astype(bf16)
                dyp = dy_ref[dyr, DATTN + PGD * g:DATTN + PGD * (g + 1)]
                ypre = _nn(pb, wb)
                dps_ref[:, PGD * g:PGD * (g + 1)] += jnp.sum(dyp * ypre, axis=0, keepdims=True)
                dyg = (dyp * ps_ref[:, PGD * g:PGD * (g + 1)]).astype(bf16)
                dpw_ref[g] += _tn(pb, dyg)
                dpooled = _nt(dyg, wb)
                due = jnp.concatenate([jnp.zeros((BLK, PGD), f32), dpooled / cnt], axis=0)
                for k in POOL_STEPS[w]:
                    due = due + pltpu.roll(due, 2 * BLK - k, axis=0)
                dz_ref[rows, c0:c0 + PGD] = due[BLK:2 * BLK] - dpooled

                def pool_prev(due=due, c0=c0, prow=prow):
                    dz_ref[prow, c0:c0 + PGD] += due[0:BLK]

                into_prev(pool_prev)

        @pl.when(n == nsteps - 1)
        def _():
            bk = bk_ref[...]
            ri = lax.broadcasted_iota(i32, (NBUCK, NH), 0)
            ci = lax.broadcasted_iota(i32, (NBUCK, NH), 1)

            def step(b, acc):
                for h in range(NH):
                    sel = jnp.where(bk == b, dbias_scr[h], 0.0)
                    tot = jnp.sum(jnp.sum(sel, axis=1, keepdims=True), axis=0, keepdims=True)
                    acc = acc + jnp.where((ri == b) & (ci == h), tot, 0.0)
                return acc

            drb_ref[...] = lax.fori_loop(0, NBUCK, step, jnp.zeros((NBUCK, NH), f32))
            lane = lax.broadcasted_iota(i32, (1, 128), 1)
            dsk = jnp.zeros((1, 128), f32)
            for h in range(NH):
                tot = jnp.sum(jnp.sum(dbias_scr[h], axis=1, keepdims=True), axis=0, keepdims=True)
                dsk = dsk - jnp.where(lane == h, tot, 0.0)
            dsk_ref[...] = dsk

    full = lambda *shape: pl.BlockSpec(shape, lambda n: (0,) * len(shape))
    npg = len(POOL_WINDOWS)
    return pl.pallas_call(
        body, grid=(nsteps,),
        in_specs=[pl.BlockSpec((step_rows, DIN), lambda n: (n, 0)),
                  pl.BlockSpec((BLK, DIN), lambda n: (jnp.maximum(n * MIX_SUB - 1, 0), 0)),
                  pl.BlockSpec((step_rows, DMIX), lambda n: (n, 0)),
                  pl.BlockSpec((MIX_SUB, NH, BLK, 2 * BLK), lambda n: (n, 0, 0, 0)),
                  full(1, HD), full(1, HD), full(BLK, 2 * BLK), full(npg, PGD, PGD), full(1, DPOOL)],
        out_specs=[full(T, DIN), full(1, HD), full(1, HD), full(1, 128), full(NBUCK, NH),
                   full(npg, PGD, PGD), full(1, DPOOL)],
        out_shape=(SDS((T, DIN), f32), SDS((1, HD), f32), SDS((1, HD), f32), SDS((1, 128), f32),
                   SDS((NBUCK, NH), f32), SDS((npg, PGD, PGD), f32), SDS((1, DPOOL), f32)),
        scratch_shapes=[pltpu.VMEM((NH, BLK, 2 * BLK), f32)],
        compiler_params=_cparams(("arbitrary",), VMEM_LIMIT_V7X),
        name=name)(z, z, dy, probs, qg, kg, bucket, pool_w, pscale)


class _LocalWeights:
    def __init__(self, w1, wint, wout, w2):
        self.w1, self.wint, self.wout, self.w2 = w1, wint, wout, w2

    def ffn1(self):
        return self.w1

    def first_norm(self, x, gain):
        return _norm_fwd(x, gain, "norm1_fwd")

    def after_ffn1(self, gain, x1):
        return gain

    def mix(self, after):
        return self.wint, self.wout

    def before_out_proj(self, wout, after):
        return wout

    def ffn2(self, after):
        return self.w2

    def out_ffn2_grads_ready(self, dwout, dw2, after):
        return after

    def before_ffn1_bwd(self, dwint, dx1b):
        return dx1b


def _local_step(x, target, weights, g1, gm, g3, qg, kg, sinks, relb, pool_w, pscale):
    bucket = jnp.asarray(_t5_bucket_table())
    sk = sinks.reshape(NH)
    w1 = weights.ffn1()
    h1 = weights.first_norm(x, g1)
    x1, gate1, up1, h2 = _ffn_fwd(h1, w1, x, None, gm, "ffn1_fwd")
    gm = weights.after_ffn1(gm, x1)
    wint, wout = weights.mix(h2)
    z = _in_proj_fwd(h2, wint, "in_proj_fwd")
    ymix, probs = _mix_fwd(z, qg, kg, sk, relb, bucket, pool_w, pscale, "mix_fwd")
    wout = weights.before_out_proj(wout, ymix)
    x2, h3 = _out_proj_fwd(ymix, wout, x1, g3, "out_proj_fwd")
    w2 = weights.ffn2(h3)
    dy, gate2, up2, dyb, loss_lanes = _ffn_fwd(h3, w2, x2, target, None, "ffn2_fwd")

    dx2, dx2b, dg3, dw2 = _ffn_bwd(dyb, h3, gate2, up2, w2, (x2, g3, dy), "ffn2_bwd")
    dymix, dwout = _out_proj_bwd(dx2b, wout, ymix, "out_proj_bwd")
    dymix = weights.out_ffn2_grads_ready(dwout, dw2, dymix)
    dz, dqg, dkg, dsk, drb, dpw, dps = _mix_bwd(z, dymix, probs, qg, kg, relb, bucket, pool_w, pscale, "mix_bwd")
    dx1, dx1b, dgm, dwint = _in_proj_bwd(dz, wint, h2, (x1, gm, dx2), 0.5, "in_proj_bwd")
    dx1b = weights.before_ffn1_bwd(dwint, dx1b)
    gx, _, dg1, dw1 = _ffn_bwd(dx1b, h1, gate1, up1, w1, (x, g1, dx1), "ffn1_bwd")
    small = dict(ffn1_norm=dg1, mix_norm=dgm, ffn2_norm=dg3, pool_scale=dps, q_norm=dqg, k_norm=dkg,
                 attn_sinks=dsk[:, :NH], rel_bias=drb, pool_w=dpw, loss=loss_lanes)
    return gx, (dw1, dwint, dwout, dw2), small


SMALL_NAMES = ("ffn1_norm", "mix_norm", "ffn2_norm", "pool_scale", "q_norm", "k_norm", "attn_sinks", "rel_bias",
               "pool_w", "loss")
SMALL_SHAPES = dict(ffn1_norm=(1, D), mix_norm=(1, D), ffn2_norm=(1, D), pool_scale=(1, DPOOL), q_norm=(1, HD),
                    k_norm=(1, HD), attn_sinks=(1, NH), rel_bias=(NBUCK, NH),
                    pool_w=(1, len(POOL_WINDOWS), PGD, PGD), loss=(1, 128))


def _small_rows(name):
    return -(-int(np.prod(SMALL_SHAPES[name])) // 128)


SMALL_OFF = {}
_r = 0
for _n in SMALL_NAMES:
    SMALL_OFF[_n] = _r
    _r += _small_rows(_n)
SMALL_ROWS = -(-_r // 16) * 16
LOSS_ROW = SMALL_OFF["loss"]


def _pack_small(vals):
    parts = []
    for n in SMALL_NAMES:
        size = _small_rows(n) * 128
        if n in vals:
            flat = vals[n].astype(f32).reshape(-1)
            parts.append(jnp.pad(flat, (0, size - flat.shape[0])))
        else:
            parts.append(jnp.zeros((size,), f32))
    flat = jnp.concatenate(parts)
    flat = jnp.pad(flat, (0, SMALL_ROWS * 128 - flat.shape[0]))
    return flat.reshape(SMALL_ROWS, 128)


def _unpack_small(packed, name):
    size = int(np.prod(SMALL_SHAPES[name]))
    r0 = SMALL_OFF[name]
    return packed[r0:r0 + _small_rows(name)].reshape(-1)[:size].reshape(SMALL_SHAPES[name])


def _position():
    return lax.axis_index("x"), lax.axis_index("y"), lax.axis_index("c")


def _dev_index(x, y, c):
    return 4 * x + 2 * y + c


G1_PIECES, MIX_PIECES, F2_PIECES = (0, 1, 2), (3, 4), (5, 6, 7)


def _group_rows(pieces):
    return sum(PIECE_ROWS[k] for k in pieces)


def _shard_piece(s_ref, k):
    return s_ref.at[pl.ds(PIECE_OFF[k], PIECE_ROWS[k]), :]


def _shard_group(s_ref, pieces):
    return s_ref.at[pl.ds(PIECE_OFF[pieces[0]], _group_rows(pieces)), :]


def _weight_pieces(w1_ref=None, wi_ref=None, wo_ref=None, w2_ref=None):
    arrs = {}
    if w1_ref is not None:
        arrs.update({0: w1_ref.at[0], 1: w1_ref.at[1], 2: w1_ref.at[2]})
    if wi_ref is not None:
        arrs[3] = wi_ref
    if wo_ref is not None:
        arrs[4] = wo_ref
    if w2_ref is not None:
        arrs.update({5: w2_ref.at[0], 6: w2_ref.at[1], 7: w2_ref.at[2]})
    return arrs


def _block_rows(arrs, k, dev):
    r = PIECE_ROWS[k]
    return arrs[k].at[pl.ds(pl.multiple_of(_dev_index(*dev) * r, 16), r), :]


NORM_ROWS = 512


def _all_gather_ffn1(shard, x, gain):
    pieces = G1_PIECES
    rest_pieces = MIX_PIECES + F2_PIECES
    half = FS // 2
    T = x.shape[0]
    SIB, X0, X1, Y0, Y1, RELAY_Y, RELAY_X, ON_X, ON_Y, ON_D0, ON_D1 = range(11)

    def body(s_ref, x_ref, g_ref, w1_ref, h_ref, wi_ref, wo_ref, w2_ref, xbuf, hbuf, rest_buf,
             send_sems, recv_sems, local_sem, norm_sems):
        x, y, c = _position()
        me, sib = (x, y, c), (x, y, 1 - c)
        xn, yn, dg = (1 - x, y, c), (x, 1 - y, c), (1 - x, 1 - y, c)
        arrs = _weight_pieces(w1_ref=w1_ref)

        def place_rest():
            rest = _weight_pieces(wi_ref=wi_ref, wo_ref=wo_ref, w2_ref=w2_ref)
            grp = _shard_group(s_ref, rest_pieces)
            load = pltpu.make_async_copy(grp, rest_buf, norm_sems.at[0])
            load.start()
            load.wait()
            base = PIECE_OFF[rest_pieces[0]]
            for k in rest_pieces:
                pltpu.make_async_copy(rest_buf.at[pl.ds(PIECE_OFF[k] - base, PIECE_ROWS[k]), :],
                                      _block_rows(rest, k, me), norm_sems.at[1]).start()
            pltpu.make_async_copy(grp, rest_buf, norm_sems.at[1]).wait()

        def first_norm():
            for r in range(0, T, NORM_ROWS):
                load = pltpu.make_async_copy(x_ref.at[pl.ds(r, NORM_ROWS), :], xbuf, norm_sems.at[0])
                load.start()
                load.wait()
                xv = xbuf[...]
                rs = lax.rsqrt(jnp.mean(xv * xv, axis=-1, keepdims=True) + EPS)
                hbuf[...] = (xv * rs * g_ref[...]).astype(bf16)
                store = pltpu.make_async_copy(hbuf, h_ref.at[pl.ds(r, NORM_ROWS), :], norm_sems.at[1])
                store.start()
                store.wait()

        def rows_of(k, block, hf):
            r = PIECE_ROWS[k]
            start, size = (0, r) if hf is None else (hf * half, half)
            return arrs[k].at[pl.ds(pl.multiple_of(_dev_index(*block) * r + start, 16), size), :]

        def copies(rel, block, hf, to, from_shard=False):
            def src(k):
                if not from_shard:
                    return rows_of(k, block, hf)
                start, size = (0, PIECE_ROWS[k]) if hf is None else (hf * half, half)
                return s_ref.at[pl.ds(PIECE_OFF[k] + start, size), :]
            return [pltpu.make_async_remote_copy(
                src_ref=src(k), dst_ref=rows_of(k, block, hf), send_sem=send_sems.at[rel], recv_sem=recv_sems.at[rel],
                device_id=to, device_id_type=MESH) for k in pieces]

        def waiter(rel, hf):
            nrows = len(pieces) * (FS if hf is None else half)
            grp = s_ref.at[pl.ds(0, nrows), :]
            return pltpu.make_async_remote_copy(src_ref=grp, dst_ref=grp, send_sem=send_sems.at[rel],
                                                recv_sem=recv_sems.at[rel], device_id=me, device_id_type=MESH)

        def start(cps):
            for cp in cps:
                cp.start()

        mine = [pltpu.make_async_copy(_shard_piece(s_ref, k), _block_rows(arrs, k, me), local_sem) for k in pieces]
        start(mine)
        start(copies(SIB, me, None, sib, True))
        start(copies(X0, me, 0, xn, True))
        start(copies(Y1, me, 1, yn, True))
        start(copies(X1, me, 1, xn, True))
        start(copies(Y0, me, 0, yn, True))
        first_norm()
        place_rest()
        waiter(X0, 0).wait_recv()
        start(copies(RELAY_Y, xn, 0, yn))
        waiter(Y1, 1).wait_recv()
        start(copies(RELAY_X, yn, 1, xn))
        waiter(X1, 1).wait_recv()
        start(copies(ON_X, xn, None, sib))
        waiter(Y0, 0).wait_recv()
        start(copies(ON_Y, yn, None, sib))
        waiter(RELAY_Y, 0).wait_recv()
        start(copies(ON_D0, dg, 0, sib))
        waiter(RELAY_X, 1).wait_recv()
        start(copies(ON_D1, dg, 1, sib))
        waiter(SIB, None).wait_recv()
        waiter(ON_X, None).wait_recv()
        waiter(ON_Y, None).wait_recv()
        waiter(ON_D0, 0).wait_recv()
        waiter(ON_D1, 1).wait_recv()
        for rel, hf in ((SIB, None), (X0, 0), (X1, 1), (Y0, 0), (Y1, 1), (RELAY_Y, 0), (RELAY_X, 1),
                        (ON_X, None), (ON_Y, None), (ON_D0, 0), (ON_D1, 1)):
            waiter(rel, hf).wait_send()
        grp = _shard_group(s_ref, pieces)
        pltpu.make_async_copy(grp, grp, local_sem).wait()

    hbm = pl.BlockSpec(memory_space=pl.ANY)
    return pl.pallas_call(
        body, in_specs=[hbm, hbm, pl.BlockSpec(memory_space=pltpu.VMEM)], out_specs=[hbm] * 5,
        out_shape=(SDS((3, F, D), bf16), SDS((T, D), bf16),
                   SDS((DIN, D), bf16), SDS((DMIX, D), bf16), SDS((3, F, D), bf16)),
        scratch_shapes=[pltpu.VMEM((NORM_ROWS, D), f32), pltpu.VMEM((NORM_ROWS, D), bf16),
                        pltpu.VMEM((_group_rows(rest_pieces), D), bf16),
                        pltpu.SemaphoreType.DMA((11,)), pltpu.SemaphoreType.DMA((11,)), pltpu.SemaphoreType.DMA,
                        pltpu.SemaphoreType.DMA((2,))],
        compiler_params=pltpu.CompilerParams(has_side_effects=True),
        name="all_gather_ffn1")(shard, x, gain)


HBM_SPEC = pl.BlockSpec(memory_space=pltpu.HBM)
SEM_SPEC = pl.BlockSpec(memory_space=pltpu.SEMAPHORE)
ANY_SPEC = pl.BlockSpec(memory_space=pl.ANY)
SPLIT_EFFECT = pltpu.SideEffectType.DATAFLOW_SIDE_EFFECTING


def _in_hbm(a):
    return pltpu.with_memory_space_constraint(a, pltpu.HBM)


def _hbm_like(a):
    return pltpu.HBM(a.shape, a.dtype)


def _gather_rest_start(shard, wi, wo, w2, w1):
    def body(s_ref, wi_ref, wo_ref, w2_ref, w1_ref,
             ssem_m, rsem_m0, rsem_m, ssem_f, rsem_f0, rsem_f, s_o, wi_o, wo_o, w2_o, w1_o):
        x, y, c = _position()
        me, sib = (x, y, c), (x, y, 1 - c)
        chips = [(1 - x, y), (x, 1 - y), (1 - x, 1 - y)]
        arrs = _weight_pieces(wi_ref=wi_ref, wo_ref=wo_ref, w2_ref=w2_ref)
        for pieces, ssem, rsem0, rsem in ((MIX_PIECES, ssem_m, rsem_m0, rsem_m), (F2_PIECES, ssem_f, rsem_f0, rsem_f)):
            for p in pieces:
                pltpu.make_async_remote_copy(
                    src_ref=_shard_piece(s_ref, p), dst_ref=_block_rows(arrs, p, me), send_sem=ssem.at[0],
                    recv_sem=rsem0, device_id=sib, device_id_type=MESH).start()
            for j, chip in enumerate(chips):
                for p in pieces:
                    pltpu.make_async_remote_copy(
                        src_ref=_shard_piece(s_ref, p), dst_ref=_block_rows(arrs, p, me), send_sem=ssem.at[1 + j],
                        recv_sem=rsem.at[j], device_id=(*chip, c), device_id_type=MESH).start()

    dma = pltpu.SemaphoreType.DMA
    return pl.pallas_call(
        body, name="gather_rest_start",
        out_shape=(dma((4,)), dma(()), dma((3,)), dma((4,)), dma(()), dma((3,)),
                   _hbm_like(shard), _hbm_like(wi), _hbm_like(wo), _hbm_like(w2), _hbm_like(w1)),
        in_specs=(HBM_SPEC,) * 5, out_specs=(SEM_SPEC,) * 6 + (HBM_SPEC,) * 5,
        input_output_aliases={0: 6, 1: 7, 2: 8, 3: 9, 4: 10},
        compiler_params=pltpu.CompilerParams(has_side_effects=SPLIT_EFFECT),
    )(_in_hbm(shard), _in_hbm(wi), _in_hbm(wo), _in_hbm(w2), _in_hbm(w1))


def _gather_mix_pass_on(rsem_m, wi, wo, thru, after):
    def body(wi_ref, wo_ref, thru_ref, rsem, after_ref, fsend, frecv, wi_o, wo_o, thru_o):
        x, y, c = _position()
        sib = (x, y, 1 - c)
        arrs = _weight_pieces(wi_ref=wi_ref, wo_ref=wo_ref)
        both = wi_ref.at[pl.ds(0, _group_rows(MIX_PIECES)), :]
        for j, chip in enumerate([(1 - x, y), (x, 1 - y), (1 - x, 1 - y)]):
            pltpu.make_async_remote_copy(src_ref=both, dst_ref=both, send_sem=fsend.at[j], recv_sem=rsem.at[j],
                                         device_id=(x, y, c), device_id_type=MESH).wait_recv()
            for p in MIX_PIECES:
                rows = _block_rows(arrs, p, (*chip, c))
                pltpu.make_async_remote_copy(src_ref=rows, dst_ref=rows, send_sem=fsend.at[j], recv_sem=frecv.at[j],
                                             device_id=sib, device_id_type=MESH).start()

    dma = pltpu.SemaphoreType.DMA
    return pl.pallas_call(
        body, name="gather_mix_pass_on",
        out_shape=(dma((3,)), dma((3,)), _hbm_like(wi), _hbm_like(wo), _hbm_like(thru)),
        in_specs=(HBM_SPEC, HBM_SPEC, HBM_SPEC, SEM_SPEC, ANY_SPEC), out_specs=(SEM_SPEC, SEM_SPEC) + (HBM_SPEC,) * 3,
        input_output_aliases={0: 2, 1: 3, 2: 4},
        compiler_params=pltpu.CompilerParams(has_side_effects=SPLIT_EFFECT),
    )(wi, wo, _in_hbm(thru), rsem_m, after)


def _gather_mix_wait(ssem_m, rsem_m0, fsend, frecv, shard, wi, wo, after):
    def body(s_ref, wi_ref, wo_ref, ssem, rsem0, fs, fr, after_ref, s_o, wi_o, wo_o):
        x, y, c = _position()
        grp = _shard_group(s_ref, MIX_PIECES)

        def waiter(send_sem, recv_sem):
            return pltpu.make_async_remote_copy(src_ref=grp, dst_ref=grp, send_sem=send_sem, recv_sem=recv_sem,
                                                device_id=(x, y, c), device_id_type=MESH)

        waiter(ssem.at[0], rsem0).wait_recv()
        for j in range(3):
            waiter(fs.at[j], fr.at[j]).wait_recv()
        for rel in range(4):
            waiter(ssem.at[rel], rsem0).wait_send()
        for j in range(3):
            waiter(fs.at[j], fr.at[j]).wait_send()

    return pl.pallas_call(
        body, name="gather_mix_wait", out_shape=(_hbm_like(shard), _hbm_like(wi), _hbm_like(wo)),
        in_specs=(HBM_SPEC,) * 3 + (SEM_SPEC,) * 4 + (ANY_SPEC,), out_specs=(HBM_SPEC,) * 3,
        input_output_aliases={0: 0, 1: 1, 2: 2},
        compiler_params=pltpu.CompilerParams(has_side_effects=SPLIT_EFFECT),
    )(shard, wi, wo, ssem_m, rsem_m0, fsend, frecv, after)


def _gather_ffn2_pass_on(rsem_f, w2, wo, after):
    def body(w2_ref, wo_ref, rsem, after_ref, fsend, frecv, w2_o, wo_o):
        x, y, c = _position()
        sib = (x, y, 1 - c)
        chips = [(1 - x, y), (x, 1 - y), (1 - x, 1 - y)]
        arrs = _weight_pieces(w2_ref=w2_ref)
        three = w2_ref.at[0, pl.ds(0, _group_rows(F2_PIECES)), :]
        for j, chip in enumerate(chips):
            pltpu.make_async_remote_copy(src_ref=three, dst_ref=three, send_sem=fsend.at[j], recv_sem=rsem.at[j],
                                         device_id=(x, y, c), device_id_type=MESH).wait_recv()
            for p in F2_PIECES:
                rows = _block_rows(arrs, p, (*chip, c))
                pltpu.make_async_remote_copy(src_ref=rows, dst_ref=rows, send_sem=fsend.at[j], recv_sem=frecv.at[j],
                                             device_id=sib, device_id_type=MESH).start()

    dma = pltpu.SemaphoreType.DMA
    return pl.pallas_call(
        body, name="gather_ffn2_pass_on", out_shape=(dma((3,)), dma((3,)), _hbm_like(w2), _hbm_like(wo)),
        in_specs=(HBM_SPEC, HBM_SPEC, SEM_SPEC, ANY_SPEC), out_specs=(SEM_SPEC, SEM_SPEC, HBM_SPEC, HBM_SPEC),
        input_output_aliases={0: 2, 1: 3},
        compiler_params=pltpu.CompilerParams(has_side_effects=SPLIT_EFFECT),
    )(w2, wo, rsem_f, after)


def _gather_ffn2_wait(ssem_f, rsem_f0, fsend, frecv, shard, w2, after):
    def body(s_ref, w2_ref, ssem, rsem0, fs, fr, after_ref, w2_o):
        x, y, c = _position()
        grp = _shard_group(s_ref, F2_PIECES)

        def waiter(send_sem, recv_sem):
            return pltpu.make_async_remote_copy(src_ref=grp, dst_ref=grp, send_sem=send_sem, recv_sem=recv_sem,
                                                device_id=(x, y, c), device_id_type=MESH)

        waiter(ssem.at[0], rsem0).wait_recv()
        for j in range(3):
            waiter(fs.at[j], fr.at[j]).wait_recv()
        for rel in range(4):
            waiter(ssem.at[rel], rsem0).wait_send()
        for j in range(3):
            waiter(fs.at[j], fr.at[j]).wait_send()

    return pl.pallas_call(
        body, name="gather_ffn2_wait", out_shape=_hbm_like(w2),
        in_specs=(HBM_SPEC, HBM_SPEC, SEM_SPEC, SEM_SPEC, SEM_SPEC, SEM_SPEC, ANY_SPEC), out_specs=HBM_SPEC,
        input_output_aliases={1: 0},
        compiler_params=pltpu.CompilerParams(has_side_effects=SPLIT_EFFECT),
    )(shard, w2, ssem_f, rsem_f0, fsend, frecv, after)


class _GatheredWeights(_LocalWeights):
    def __init__(self, shard, x, gain1):
        w1, self.h1, wi, wo, w2 = _all_gather_ffn1(shard, x, gain1)
        (self.ssem_m, self.rsem_m0, self.rsem_m, self.ssem_f, self.rsem_f0, self.rsem_f,
         self.shard, self.wi, self.wo, self.w2_part, self.w1) = _gather_rest_start(shard, wi, wo, w2, w1)

    def first_norm(self, x, gain):
        return self.h1

    def after_ffn1(self, gain, x1):
        self.fsend_m, self.frecv_m, self.wi, self.wo, gain = _gather_mix_pass_on(self.rsem_m, self.wi, self.wo, gain, x1)
        return gain

    def mix(self, after):
        self.shard, wint, wout = _gather_mix_wait(self.ssem_m, self.rsem_m0, self.fsend_m, self.frecv_m, self.shard,
                                                  self.wi, self.wo, after)
        return wint, wout

    def before_out_proj(self, wout, after):
        self.fsend, self.frecv, self.w2_part, wout = _gather_ffn2_pass_on(self.rsem_f, self.w2_part, wout, after)
        return wout

    def ffn2(self, after):
        return _gather_ffn2_wait(self.ssem_f, self.rsem_f0, self.fsend, self.frecv, self.shard, self.w2_part, after)

    def out_ffn2_grads_ready(self, dwout, dw2, after):
        rx1 = lax.empty((4, RSA_ROWS, D), bf16)
        sa, ra, sent, rx1, after = _rsa_level1_start(dict(wo=dwout, w2=dw2), rx1, after, "rsa_level1_start_out_ffn2")
        self.level1 = ((sa, ra), sent, rx1)
        return after

    def before_ffn1_bwd(self, dwint, dx1b):
        early, sent, rx1 = self.level1
        sa, ra, late, rx1, dx1b = _rsa_level1_start(dict(wi=dwint), rx1, dx1b, "rsa_level1_start_in")
        started = (((MIX_PIECES[1],) + F2_PIECES, *early), ((MIX_PIECES[0],), sa, ra))
        rx2 = lax.empty((3, RSA_ROWS, D), bf16)
        self.sb, self.rb, self.tx, self.acc, self.rx2, dx1b = _rsa_sums_and_send(
            started, late["wi"], sent["wo"], sent["w2"], rx1, rx2, dx1b)
        return dx1b

    def mix_ffn2_grads_parts(self, after):
        rx2 = _rsa_level2_wait(self.sb, self.rb, self.tx, self.rx2, after)
        return self.acc, rx2


def _reduce_scatter_ffn1_head(dw1, small_packed):
    pieces = G1_PIECES
    half = FS // 2
    hrows = len(pieces) * half
    nrows = 2 * hrows
    X_RELAY, Y_RELAY = range(2)

    def body(d1_ref, p_ref, forx_ref, fory_ref, own_ref, rx1_ref, relx_ref, rely_ref, tot_ref,
             own_buf, rx_buf, tx1, tx2, tx3, acc, sa, ra, sb, rb, lsem, pair, chips, small_tot, small_send, small_recv):
        x, y, c = _position()
        me, sib = (x, y, c), (x, y, 1 - c)
        xn, yn = (1 - x, y, c), (x, 1 - y, c)
        rel_chips = [(x, y), (1 - x, y), (x, 1 - y), (1 - x, 1 - y)]
        srcs = _weight_pieces(w1_ref=d1_ref)

        my_chip = 2 * x + y
        pair[c] = p_ref[...]
        swap = pltpu.make_async_remote_copy(
            src_ref=p_ref, dst_ref=pair.at[c], send_sem=small_send.at[0], recv_sem=small_recv.at[0],
            device_id=sib, device_id_type=MESH)
        swap.start()
        mine = pl.ds(pl.multiple_of(c * (SMALL_ROWS // 2), 8), SMALL_ROWS // 2)
        small = [pltpu.make_async_remote_copy(
            src_ref=chips.at[my_chip, mine, :], dst_ref=chips.at[my_chip, mine, :], send_sem=small_send.at[j],
            recv_sem=small_recv.at[j], device_id=(*rel_chips[j], c), device_id_type=MESH) for j in (1, 2, 3)]
        give = pltpu.make_async_remote_copy(
            src_ref=small_tot.at[mine, :], dst_ref=small_tot.at[mine, :], send_sem=small_send.at[4],
            recv_sem=small_recv.at[4], device_id=sib, device_id_type=MESH)

        def part(k, dev, hf):
            r = PIECE_ROWS[k]
            return srcs[k].at[pl.ds(pl.multiple_of(_dev_index(*dev) * r + hf * half, 16), half), :]

        def slot(ref, k, hf):
            return ref.at[pl.ds(hf * hrows + k * half, half), :]

        halves = [(k, hf) for hf in (0, 1) for k in pieces]

        for j in (3, 1, 2, 0):
            for k, hf in halves:
                pltpu.make_async_remote_copy(
                    src_ref=part(k, (*rel_chips[j], 1 - c), hf), dst_ref=slot(rx1_ref.at[j], k, hf),
                    send_sem=sa.at[j], recv_sem=ra.at[j], device_id=sib, device_id_type=MESH).start()

        def wait_a(j):
            return pltpu.make_async_remote_copy(src_ref=rx1_ref.at[j], dst_ref=rx1_ref.at[j], send_sem=sa.at[j],
                                                recv_sem=ra.at[j], device_id=me, device_id_type=MESH)

        def ici(rel, src, dst, to):
            return pltpu.make_async_remote_copy(src_ref=src, dst_ref=dst, send_sem=sb.at[rel], recv_sem=rb.at[rel],
                                                device_id=to, device_id_type=MESH)

        first, second = pl.ds(0, hrows), pl.ds(hrows, hrows)
        sends = {
            X_RELAY: ici(X_RELAY, tx3.at[first, :], relx_ref, xn),
            Y_RELAY: ici(Y_RELAY, tx3.at[second, :], rely_ref, yn),
        }

        def chip_sum(j, dst):
            loads = [pltpu.make_async_copy(part(k, (*rel_chips[j], c), hf), slot(own_buf, k, hf), lsem.at[0])
                     for k, hf in halves]
            for cp in loads:
                cp.start()
            wait_a(j).wait_recv()
            got = pltpu.make_async_copy(rx1_ref.at[j], rx_buf, lsem.at[1])
            got.start()
            pltpu.make_async_copy(rx_buf, rx_buf, lsem.at[0]).wait()
            got.wait()

            def add(i, carry):
                rows = pl.ds(pl.multiple_of(i * half, 16), half)
                tot = own_buf[rows, :].astype(f32) + rx_buf[rows, :].astype(f32)
                dst[rows, :] = tot.astype(dst.dtype)
                return carry

            lax.fori_loop(0, nrows // half, add, 0)

        def add_landed(landed, dst, rows0, nrows_):
            got = pltpu.make_async_copy(landed, rx_buf.at[pl.ds(0, nrows_), :], lsem.at[1])
            got.start()
            got.wait()

            def add(i, carry):
                src_rows = pl.ds(pl.multiple_of(i * half, 16), half)
                dst_rows = pl.ds(pl.multiple_of(rows0 + i * half, 16), half)
                dst[dst_rows, :] = (dst[dst_rows, :].astype(f32) + rx_buf[src_rows, :].astype(f32)).astype(dst.dtype)
                return carry

            lax.fori_loop(0, nrows_ // half, add, 0)

        chip_sum(3, tx3)
        sends[X_RELAY].start()
        sends[Y_RELAY].start()
        swap.wait_recv()
        chips[my_chip] = pair[0] + pair[1]
        for cp in small:
            cp.start()
        chip_sum(1, tx1)
        chip_sum(2, tx2)
        chip_sum(0, acc)
        own_out = pltpu.make_async_copy(acc, own_ref, lsem.at[0])
        own_out.start()
        sends[X_RELAY].wait_recv()
        add_landed(relx_ref, tx2, 0, hrows)
        sends[Y_RELAY].wait_recv()
        add_landed(rely_ref, tx1, hrows, hrows)
        own_out.wait()
        outs = [pltpu.make_async_copy(tx1, forx_ref, lsem.at[0]), pltpu.make_async_copy(tx2, fory_ref, lsem.at[1])]
        for cp in outs:
            cp.start()
        for cp in outs:
            cp.wait()
        for cp in small:
            cp.wait_recv()
        small_tot[mine, :] = (chips[0, mine, :] + chips[1, mine, :]) + (chips[2, mine, :] + chips[3, mine, :])
        give.start()
        give.wait_recv()
        tot = small_tot[...]
        tot_ref[...] = tot
        loss = jnp.sum(tot[LOSS_ROW:LOSS_ROW + 1, :], axis=-1, keepdims=True)
        tot_ref[LOSS_ROW:LOSS_ROW + 1, :] = jnp.broadcast_to(loss, (1, 128))
        for j in range(4):
            wait_a(j).wait_send()
        for cp in sends.values():
            cp.wait_send()
        swap.wait_send()
        for cp in small + [give]:
            cp.wait_send()

    hbm = pl.BlockSpec(memory_space=pl.ANY)
    vm = pl.BlockSpec(memory_space=pltpu.VMEM)
    outs = pl.pallas_call(
        body, in_specs=[hbm, vm], out_specs=[hbm] * 6 + [vm],
        out_shape=(SDS((nrows, D), bf16), SDS((nrows, D), bf16), SDS((nrows, D), f32), SDS((4, nrows, D), bf16),
                   SDS((hrows, D), bf16), SDS((hrows, D), bf16), SDS((SMALL_ROWS, 128), f32)),
        scratch_shapes=[pltpu.VMEM((nrows, D), bf16), pltpu.VMEM((nrows, D), bf16),
                        pltpu.VMEM((nrows, D), bf16), pltpu.VMEM((nrows, D), bf16), pltpu.VMEM((nrows, D), bf16),
                        pltpu.VMEM((nrows, D), f32),
                        pltpu.SemaphoreType.DMA((4,)), pltpu.SemaphoreType.DMA((4,)),
                        pltpu.SemaphoreType.DMA((2,)), pltpu.SemaphoreType.DMA((2,)), pltpu.SemaphoreType.DMA((2,)),
                        pltpu.VMEM((2, SMALL_ROWS, 128), f32), pltpu.VMEM((4, SMALL_ROWS, 128), f32),
                        pltpu.VMEM((SMALL_ROWS, 128), f32),
                        pltpu.SemaphoreType.DMA((5,)), pltpu.SemaphoreType.DMA((5,))],
        compiler_params=pltpu.CompilerParams(has_side_effects=True, vmem_limit_bytes=VMEM_LIMIT_V7X),
        name="reduce_scatter_ffn1_head")(dw1, small_packed)
    return outs[0], outs[1], outs[2], outs[-1]


def _rs1_tail_start(for_x, for_y, from_x, from_y, thru):
    def body(fx_ref, fy_ref, lx_ref, ly_ref, thru_ref, ssem, rsem, fx_o, fy_o, lx_o, ly_o, thru_o):
        x, y, c = _position()
        pltpu.make_async_remote_copy(src_ref=fx_ref, dst_ref=lx_ref, send_sem=ssem.at[0], recv_sem=rsem.at[0],
                                     device_id=(1 - x, y, c), device_id_type=MESH).start()
        pltpu.make_async_remote_copy(src_ref=fy_ref, dst_ref=ly_ref, send_sem=ssem.at[1], recv_sem=rsem.at[1],
                                     device_id=(x, 1 - y, c), device_id_type=MESH).start()

    dma = pltpu.SemaphoreType.DMA
    arrs = (for_x, for_y, from_x, from_y, thru)
    return pl.pallas_call(
        body, name="rs1_tail_start", out_shape=(dma((2,)), dma((2,))) + tuple(_hbm_like(a) for a in arrs),
        in_specs=(HBM_SPEC,) * 5, out_specs=(SEM_SPEC,) * 2 + (HBM_SPEC,) * 5,
        input_output_aliases={0: 2, 1: 3, 2: 4, 3: 5, 4: 6},
        compiler_params=pltpu.CompilerParams(has_side_effects=SPLIT_EFFECT),
    )(*[_in_hbm(a) for a in arrs])


RSA_PIECES = MIX_PIECES + F2_PIECES
RSA_ROWS = _group_rows(RSA_PIECES)
RSA_OFF = {k: PIECE_OFF[k] - PIECE_OFF[RSA_PIECES[0]] for k in RSA_PIECES}
RSA_BLOCK = 192


def _rsa_rows(ref, k):
    return ref.at[pl.ds(RSA_OFF[k], PIECE_ROWS[k]), :]


def _rsa_level1_start(grads, rx1, thru, name):
    keys = sorted(grads)
    n = len(keys)

    def body(*refs):
        srcs = _weight_pieces(**{k + "_ref": ref for k, ref in zip(keys, refs[:n])})
        rx1_ref, sa, ra = refs[n], refs[n + 2], refs[n + 3]
        x, y, c = _position()
        for j, chip in enumerate([(x, y), (1 - x, y), (x, 1 - y), (1 - x, 1 - y)]):
            for k in sorted(srcs):
                pltpu.make_async_remote_copy(
                    src_ref=_block_rows(srcs, k, (*chip, 1 - c)), dst_ref=_rsa_rows(rx1_ref.at[j], k),
                    send_sem=sa.at[j], recv_sem=ra.at[j], device_id=(x, y, 1 - c), device_id_type=MESH).start()

    dma = pltpu.SemaphoreType.DMA
    arrs = tuple(grads[k] for k in keys) + (rx1, thru)
    outs = pl.pallas_call(
        body, name=name, out_shape=(dma((4,)), dma((4,))) + tuple(_hbm_like(a) for a in arrs),
        in_specs=(HBM_SPEC,) * len(arrs), out_specs=(SEM_SPEC,) * 2 + (HBM_SPEC,) * len(arrs),
        input_output_aliases={i: i + 2 for i in range(len(arrs))},
        compiler_params=pltpu.CompilerParams(has_side_effects=SPLIT_EFFECT),
    )(*[_in_hbm(a) for a in arrs])
    return outs[0], outs[1], dict(zip(keys, outs[2:2 + n])), outs[2 + n], outs[3 + n]


def _rsa_sums_and_send(started, dwint, dwout, dw2, rx1, rx2, thru):
    nblk = RSA_ROWS // RSA_BLOCK
    nstart = len(started)

    def body(*refs):
        di_ref, do_ref, d2_ref, rx1_ref, rx2_ref = refs[:5]
        l1_sems = refs[6:6 + 2 * nstart]
        sb, rb, tx_ref, acc_ref = refs[6 + 2 * nstart:10 + 2 * nstart]
        own_buf, rx_buf, tx_buf, acc_buf, in_sems, out_sems = refs[13 + 2 * nstart:]
        x, y, c = _position()
        srcs = _weight_pieces(wi_ref=di_ref, wo_ref=do_ref, w2_ref=d2_ref)
        chips = [(x, y), (1 - x, y), (x, 1 - y), (1 - x, 1 - y)]

        for g, (pieces, _, _) in enumerate(started):
            ssem, rsem = l1_sems[2 * g], l1_sems[2 * g + 1]
            for j in range(4):
                rows = rx1_ref.at[j, pl.ds(RSA_OFF[pieces[0]], _group_rows(pieces)), :]
                d = pltpu.make_async_remote_copy(src_ref=rows, dst_ref=rows, send_sem=ssem.at[j], recv_sem=rsem.at[j],
                                                 device_id=(x, y, c), device_id_type=MESH)
                d.wait_recv()
                d.wait_send()

        def start_loads(j):
            s = j % 2
            for k in RSA_PIECES:
                pltpu.make_async_copy(_block_rows(srcs, k, (*chips[j], c)), _rsa_rows(own_buf.at[s], k),
                                      in_sems.at[2 * s]).start()
            pltpu.make_async_copy(rx1_ref.at[j], rx_buf.at[s], in_sems.at[2 * s + 1]).start()

        def wait_loads(j):
            s = j % 2
            pltpu.make_async_copy(rx1_ref.at[j], own_buf.at[s], in_sems.at[2 * s]).wait()
            pltpu.make_async_copy(rx1_ref.at[j], rx_buf.at[s], in_sems.at[2 * s + 1]).wait()

        def store(j):
            if j == 0:
                return pltpu.make_async_copy(acc_buf, acc_ref, out_sems.at[2])
            return pltpu.make_async_copy(tx_buf.at[j % 2], tx_ref.at[j - 1], out_sems.at[j % 2])

        def send(j):
            return pltpu.make_async_remote_copy(src_ref=tx_ref.at[j - 1], dst_ref=rx2_ref.at[j - 1], send_sem=sb.at[j - 1],
                                                recv_sem=rb.at[j - 1], device_id=(*chips[j], c), device_id_type=MESH)

        start_loads(0)
        for j in range(4):
            s = j % 2
            if j + 1 < 4:
                start_loads(j + 1)
            wait_loads(j)
            if j == 3:
                store(1).wait()
                send(1).start()

            def add(i, carry, j=j, s=s):
                rows = pl.ds(pl.multiple_of(i * RSA_BLOCK, 16), RSA_BLOCK)
                tot = own_buf[s, rows, :].astype(f32) + rx_buf[s, rows, :].astype(f32)
                if j == 0:
                    acc_buf[rows, :] = tot
                else:
                    tx_buf[s, rows, :] = tot.astype(bf16)
                return carry

            lax.fori_loop(0, nblk, add, 0)
            store(j).start()
        store(0).wait()
        for j in (2, 3):
            store(j).wait()
            send(j).start()

    dma = pltpu.SemaphoreType.DMA
    passed = (rx1, rx2, thru)
    outs = pl.pallas_call(
        body, name="rsa_sums_and_send",
        in_specs=(HBM_SPEC,) * 6 + (SEM_SPEC,) * (2 * nstart),
        out_specs=(SEM_SPEC,) * 2 + (HBM_SPEC,) * 5,
        out_shape=(dma((3,)), dma((3,)), pltpu.HBM((3, RSA_ROWS, D), bf16), pltpu.HBM((RSA_ROWS, D), f32))
        + tuple(_hbm_like(a) for a in passed),
        input_output_aliases={3: 4, 4: 5, 5: 6},
        scratch_shapes=[pltpu.VMEM((2, RSA_ROWS, D), bf16), pltpu.VMEM((2, RSA_ROWS, D), bf16),
                        pltpu.VMEM((2, RSA_ROWS, D), bf16), pltpu.VMEM((RSA_ROWS, D), f32),
                        dma((4,)), dma((3,))],
        compiler_params=pltpu.CompilerParams(has_side_effects=SPLIT_EFFECT, vmem_limit_bytes=VMEM_LIMIT_V7X),
    )(*[_in_hbm(a) for a in (dwint, dwout, dw2) + passed], *[sem for _, sa, ra in started for sem in (sa, ra)])
    sb, rb, tx, acc, _, rx2, thru = outs
    return sb, rb, tx, acc, rx2, thru


def _rsa_level2_wait(sb, rb, tx, rx2, after):
    def body(tx_ref, rx2_ref, sb_ref, rb_ref, after_ref, rx2_o):
        x, y, c = _position()
        for j in range(3):
            d = pltpu.make_async_remote_copy(src_ref=tx_ref.at[j], dst_ref=rx2_ref.at[j], send_sem=sb_ref.at[j],
                                             recv_sem=rb_ref.at[j], device_id=(x, y, c), device_id_type=MESH)
            d.wait_recv()
            d.wait_send()

    return pl.pallas_call(
        body, name="rsa_level2_wait", out_shape=_hbm_like(rx2),
        in_specs=(HBM_SPEC, HBM_SPEC, SEM_SPEC, SEM_SPEC, ANY_SPEC), out_specs=HBM_SPEC,
        input_output_aliases={1: 0},
        compiler_params=pltpu.CompilerParams(has_side_effects=SPLIT_EFFECT),
    )(tx, rx2, sb, rb, after)


def _adamw_math(w, g, m, v):
    m = ADAM_B1 * m + (1.0 - ADAM_B1) * g
    v = ADAM_B2 * v + (1.0 - ADAM_B2) * (g * g)
    m_hat = m / (1.0 - ADAM_B1 ** ADAM_STEP)
    v_hat = v / (1.0 - ADAM_B2 ** ADAM_STEP)
    delta = -ADAM_LR * (m_hat / (jnp.sqrt(v_hat) + ADAM_EPS) + ADAM_WD * w)
    return delta, m, v


def _adamw_big(pieces, ws, ms, vs, own, landed, name, in_flight=None):
    npiece = len(pieces)
    nsent = len(landed) if in_flight else 0
    nland = sum(a.shape[0] if a.ndim == 3 else 1 for a in landed)
    rmax = max(PIECE_ROWS[k] for k in pieces)
    half = FS // 2

    def segments(k):
        if k in G1_PIECES:
            return [(hf * len(G1_PIECES) * half + k * half, hf * half, half) for hf in (0, 1)]
        return [(RSA_OFF[k], 0, PIECE_ROWS[k])]

    def body(*refs):
        ins = (refs[0:npiece], refs[npiece:2 * npiece], refs[2 * npiece:3 * npiece])
        own_ref = refs[3 * npiece]
        nin = 3 * npiece + 1 + len(landed)
        land_refs = []
        for ref, a in zip(refs[3 * npiece + 1:nin], landed):
            land_refs += [ref.at[j] for j in range(a.shape[0])] if a.ndim == 3 else [ref]
        if in_flight:
            sent_refs, (ssem, rsem) = refs[nin:nin + nsent], refs[nin + nsent:nin + nsent + 2]
            nin += nsent + 3
        out_refs = refs[nin:nin + 4 * npiece]
        inb, landb, outb, in_sems, land_sems, out_sems = refs[nin + 4 * npiece + nsent:]

        def loads(i):
            s, k = i % 2, pieces[i]
            r = PIECE_ROWS[k]
            cps = [pltpu.make_async_copy(ins[q][i].at[0], inb.at[s, q, pl.ds(0, r), :], in_sems.at[4 * s + q])
                   for q in range(3)]
            waits, late = list(cps), []
            for src0, dst0, n in segments(k):
                cps.append(pltpu.make_async_copy(own_ref.at[pl.ds(src0, n), :], inb.at[s, 3, pl.ds(dst0, n), :],
                                                 in_sems.at[4 * s + 3]))
                for p in range(nland):
                    late.append(pltpu.make_async_copy(land_refs[p].at[pl.ds(src0, n), :],
                                                      landb.at[s, p, pl.ds(dst0, n), :], land_sems.at[nland * s + p]))
            own_rows = inb.at[s, 3, pl.ds(0, r), :]
            waits.append(pltpu.make_async_copy(own_rows, own_rows, in_sems.at[4 * s + 3]))
            for p in range(nland):
                rows = landb.at[s, p, pl.ds(0, r), :]
                waits.append(pltpu.make_async_copy(rows, rows, land_sems.at[nland * s + p]))
            return cps, late, waits

        def stores(i):
            s, r = i % 2, PIECE_ROWS[pieces[i]]
            return [pltpu.make_async_copy(outb.at[s, q, pl.ds(0, r), :], out_refs[q * npiece + i].at[0],
                                          out_sems.at[4 * s + q]) for q in range(4)]

        ahead = min(2, npiece) if in_flight else 1
        for i in range(ahead):
            for cp in loads(i)[0]:
                cp.start()
        if in_flight:
            x, y, c = _position()
            for j in range(nsent):
                d = pltpu.make_async_remote_copy(src_ref=sent_refs[j], dst_ref=refs[3 * npiece + 1 + j],
                                                 send_sem=ssem.at[j], recv_sem=rsem.at[j], device_id=(x, y, c),
                                                 device_id_type=MESH)
                d.wait_recv()
                d.wait_send()
        for cp in loads(0)[1]:
            cp.start()
        for i in range(npiece):
            s, r = i % 2, PIECE_ROWS[pieces[i]]
            if i + 1 < npiece:
                first, late, _ = loads(i + 1)
                for cp in late if i + 1 < ahead else first + late:
                    cp.start()
            for cp in loads(i)[2]:
                cp.wait()
            if i >= 2:
                for cp in stores(i - 2):
                    cp.wait()
            g = inb[s, 3, 0:r, :]
            for p in range(nland):
                g = g + landb[s, p, 0:r, :].astype(f32)
            d, nm, nv = _adamw_math(inb[s, 0, 0:r, :], g, inb[s, 1, 0:r, :], inb[s, 2, 0:r, :])
            outb[s, 0, 0:r, :] = g
            outb[s, 1, 0:r, :] = d
            outb[s, 2, 0:r, :] = nm
            outb[s, 3, 0:r, :] = nv
            for cp in stores(i):
                cp.start()
        for i in range(max(npiece - 2, 0), npiece):
            for cp in stores(i):
                cp.wait()

    hbm = pl.BlockSpec(memory_space=pl.ANY)
    in_specs, out_specs = [hbm] * (3 * npiece + 1), [hbm] * (4 * npiece)
    out_shape = [SDS(w.shape, f32) for _ in range(4) for w in ws]
    args, aliases, effect = [*ws, *ms, *vs, own], {}, False
    if in_flight:
        ssem, rsem, sent, after = in_flight
        in_specs += [HBM_SPEC] * (2 * nsent) + [SEM_SPEC, SEM_SPEC, hbm]
        args += [_in_hbm(a) for a in (*landed, *sent)] + [ssem, rsem, after]
        out_specs += [HBM_SPEC] * nsent
        out_shape += [_hbm_like(a) for a in landed]
        aliases = {3 * npiece + 1 + j: 4 * npiece + j for j in range(nsent)}
        effect = SPLIT_EFFECT
    else:
        in_specs += [hbm] * len(landed)
        args += list(landed)
    outs = pl.pallas_call(
        body, in_specs=in_specs, out_specs=out_specs, out_shape=tuple(out_shape), input_output_aliases=aliases,
        scratch_shapes=[pltpu.VMEM((2, 4, rmax, D), f32), pltpu.VMEM((2, nland, rmax, D), bf16),
                        pltpu.VMEM((2, 4, rmax, D), f32),
                        pltpu.SemaphoreType.DMA((8,)), pltpu.SemaphoreType.DMA((2 * nland,)),
                        pltpu.SemaphoreType.DMA((8,))],
        compiler_params=pltpu.CompilerParams(has_side_effects=effect, vmem_limit_bytes=VMEM_LIMIT_V7X),
        name=name)(*args)
    return [list(outs[q * npiece:(q + 1) * npiece]) for q in range(4)]


def _adamw_small(ws, ms, vs, gs, name):
    n = len(ws)

    def body(*refs):
        w_refs, m_refs, v_refs, g_refs = refs[0:n], refs[n:2 * n], refs[2 * n:3 * n], refs[3 * n:4 * n]
        outs = refs[4 * n:]
        for i in range(n):
            d, nm, nv = _adamw_math(w_refs[i][...], g_refs[i][...], m_refs[i][...], v_refs[i][...])
            outs[i][...] = d
            outs[n + i][...] = nm
            outs[2 * n + i][...] = nv

    outs = pl.pallas_call(
        body, out_shape=tuple(SDS(w.shape, f32) for _ in range(3) for w in ws), name=name)(*ws, *ms, *vs, *gs)
    return [list(outs[q * n:(q + 1) * n]) for q in range(3)]


WEIGHTS = ("ffn1_norm", "ffn1_w_gate", "ffn1_w_up", "ffn1_w_down", "mix_norm", "w_in", "q_norm", "k_norm",
           "attn_sinks", "rel_bias", "pool_w", "pool_scale", "w_out", "ffn2_norm", "ffn2_w_gate", "ffn2_w_up",
           "ffn2_w_down")
BIG = (("ffn1_w_gate", True), ("ffn1_w_up", True), ("ffn1_w_down", False), ("w_in", True), ("w_out", False),
       ("ffn2_w_gate", True), ("ffn2_w_up", True), ("ffn2_w_down", False))


def kernel(x, ffn1_norm, ffn1_w_gate, ffn1_w_up, ffn1_w_down, mix_norm, w_in, q_norm, k_norm, attn_sinks, rel_bias, pool_w, pool_scale, w_out, ffn2_norm, ffn2_w_gate, ffn2_w_up, ffn2_w_down, loss_target, m_ffn1_norm, m_ffn1_w_gate, m_ffn1_w_up, m_ffn1_w_down, m_mix_norm, m_w_in, m_q_norm, m_k_norm, m_attn_sinks, m_rel_bias, m_pool_w, m_pool_scale, m_w_out, m_ffn2_norm, m_ffn2_w_gate, m_ffn2_w_up, m_ffn2_w_down, v_ffn1_norm, v_ffn1_w_gate, v_ffn1_w_up, v_ffn1_w_down, v_mix_norm, v_w_in, v_q_norm, v_k_norm, v_attn_sinks, v_rel_bias, v_pool_w, v_pool_scale, v_w_out, v_ffn2_norm, v_ffn2_w_gate, v_ffn2_w_up, v_ffn2_w_down):
    args = dict(locals())
    w = {n: args[n] for n in WEIGHTS}
    m = {n: args["m_" + n] for n in WEIGHTS}
    v = {n: args["v_" + n] for n in WEIGHTS}

    as_rows = lambda a, tr: jnp.swapaxes(a, 1, 2) if tr else a
    shard = jnp.concatenate([as_rows(w[n], tr)[0].astype(bf16) for n, tr in BIG], axis=0)
    exchanges = _GatheredWeights(shard, x[0], ffn1_norm)
    gx, (dw1, _, _, _), small = _local_step(
        x[0], loss_target[0], exchanges, ffn1_norm, mix_norm, ffn2_norm, q_norm, k_norm, attn_sinks,
        rel_bias, pool_w[0], pool_scale)

    nrows1 = len(G1_PIECES) * FS
    for_x, for_y, own1, small_tot = _reduce_scatter_ffn1_head(dw1, _pack_small(small))
    ssem, rsem, for_x, for_y, from_x, from_y, small_tot = _rs1_tail_start(
        for_x, for_y, lax.empty((nrows1, D), bf16), lax.empty((nrows1, D), bf16), small_tot)
    own_rest, landed_rest = exchanges.mix_ffn2_grads_parts(small_tot)

    grads, deltas, new_m, new_v = {}, {}, {}, {}
    rest = [k for k in range(len(BIG)) if k not in G1_PIECES]
    rows_of = lambda t, ks: [as_rows(t[BIG[k][0]], BIG[k][1]) for k in ks]
    rest_out = _adamw_big(rest, rows_of(w, rest), rows_of(m, rest), rows_of(v, rest), own_rest, [landed_rest],
                          "adamw_rest")
    ffn1 = list(G1_PIECES)
    ffn1_out = _adamw_big(ffn1, rows_of(w, ffn1), rows_of(m, ffn1), rows_of(v, ffn1), own1, [from_x, from_y],
                          "adamw_ffn1", in_flight=(ssem, rsem, [for_x, for_y], rest_out[0][0]))
    for ks, out in ((rest, rest_out), (ffn1, ffn1_out)):
        for i, k in enumerate(ks):
            n, tr = BIG[k]
            grads[n], deltas[n], new_m[n], new_v[n] = [as_rows(o[i], tr) for o in out]
    small_names = [n for n in SMALL_NAMES if n != "loss"]
    for n in small_names:
        grads[n] = _unpack_small(small_tot, n)
    ds, nms, nvs = _adamw_small([w[n] for n in small_names], [m[n] for n in small_names], [v[n] for n in small_names],
                                [grads[n] for n in small_names], "adamw_small")
    for i, n in enumerate(small_names):
        deltas[n], new_m[n], new_v[n] = ds[i], nms[i], nvs[i]
    loss = small_tot[LOSS_ROW, 0]
    return (loss, gx[None], *[grads[n] for n in WEIGHTS], *[deltas[n] for n in WEIGHTS],
            *[new_m[n] for n in WEIGHTS], *[new_v[n] for n in WEIGHTS])
```

```python
import jax
import jax.numpy as jnp
import numpy as np
from jax import lax
from jax.experimental import pallas as pl
from jax.experimental.pallas import tpu as pltpu

f32, bf16, i32 = jnp.float32, jnp.bfloat16, jnp.int32
SDS = jax.ShapeDtypeStruct

D = 1024
F = 2816
HD = 64
NH = 8
NKV = 2
GQA = NH // NKV
DATTN = NH * HD
DKV = NKV * HD
DPOOL = 512
POOL_WINDOWS = (2, 4, 8, 16)
PGD = DPOOL // len(POOL_WINDOWS)
DIN = DATTN + 2 * DKV + DPOOL
DMIX = DATTN + DPOOL
BLK = 128
NBUCK = 32
MAX_DISTANCE = 128
EPS = 1e-6
NEG = -1e30
SCALE = HD ** -0.5

ADAM_LR, ADAM_B1, ADAM_B2, ADAM_EPS, ADAM_WD, ADAM_STEP = 0.001, 0.9, 0.999, 1e-08, 0.01, 10

NDEV = 8
FS = F // NDEV
INS = DIN // NDEV
OUTS = DMIX // NDEV
PIECE_ROWS = (FS, FS, FS, INS, OUTS, FS, FS, FS)
PIECE_OFF = tuple(int(v) for v in np.cumsum((0,) + PIECE_ROWS[:-1]))
PACK_ROWS = sum(PIECE_ROWS)

VMEM_LIMIT_V7X = 56 * 1024 * 1024

MESH = pl.DeviceIdType.MESH


def _cparams(sem=None, vmem=None):
    return pltpu.CompilerParams(dimension_semantics=sem, vmem_limit_bytes=vmem)


def _nt(a, b):
    return lax.dot_general(a, b, (((1,), (1,)), ((), ())), preferred_element_type=f32)


def _tn(a, b):
    return lax.dot_general(a, b, (((0,), (0,)), ((), ())), preferred_element_type=f32)


def _nn(a, b):
    return jnp.dot(a, b, preferred_element_type=f32)


def _sigmoid(x):
    return 1.0 / (1.0 + jnp.exp(-x))


def _norm_fwd(x, g, name):
    T = x.shape[0]
    tm = min(512, T)

    def body(x_ref, g_ref, h_ref):
        xv = x_ref[...]
        r = lax.rsqrt(jnp.mean(xv * xv, axis=-1, keepdims=True) + EPS)
        h_ref[...] = (xv * r * g_ref[...]).astype(bf16)

    return pl.pallas_call(
        body, grid=(T // tm,),
        in_specs=[pl.BlockSpec((tm, D), lambda i: (i, 0)), pl.BlockSpec((1, D), lambda i: (0, 0))],
        out_specs=pl.BlockSpec((tm, D), lambda i: (i, 0)),
        out_shape=SDS((T, D), bf16), name=name)(x, g)


FFN_ROW_CHUNK = 256


def _ffn_tiles(T):
    return min(1024, T), 256


def _ffn_fwd(h, w, x, target, next_gain, name):
    T = h.shape[0]
    tm, tf = _ffn_tiles(T)
    nf = F // tf
    with_loss = target is not None
    assert with_loss != (next_gain is not None)

    def body(*refs):
        if with_loss:
            h_ref, w_ref, x_hbm, t_hbm, xo_ref, g_ref, u_ref, dyb_ref, loss_ref, tbuf, sem = refs
        else:
            h_ref, w_ref, x_hbm, gain_ref, xo_ref, g_ref, u_ref, hn_ref, sem = refs
        fi = pl.program_id(0)

        @pl.when(fi == 0)
        def _():
            cp = pltpu.make_async_copy(x_hbm, xo_ref, sem)
            cp.start()
            cp.wait()

        wgu = w_ref[0:2].reshape(2 * tf, D)
        for r in range(0, T, tm):
            rows = slice(r, r + tm)
            gu = _nt(h_ref[rows, :], wgu)
            gate, up = gu[:, :tf], gu[:, tf:]
            act = gate * _sigmoid(gate) * up
            g_ref[0, rows, :] = gate.astype(bf16)
            u_ref[0, rows, :] = up.astype(bf16)
            xo_ref[rows, :] += _nn((0.5 * act).astype(bf16), w_ref[2])

        if with_loss:
            @pl.when(fi == nf - 1)
            def _():
                lanes = jnp.zeros((1, 128), f32)
                for r in range(0, T, tm):
                    rows = slice(r, r + tm)
                    cp = pltpu.make_async_copy(t_hbm.at[pl.ds(r, tm), :], tbuf, sem)
                    cp.start()
                    cp.wait()
                    e = xo_ref[rows, :] - tbuf[...]
                    dy = e * (1.0 / D)
                    xo_ref[rows, :] = dy
                    dyb_ref[rows, :] = (0.5 * dy).astype(bf16)
                    col = jnp.sum(e * e, axis=0, keepdims=True) * (0.5 / D)
                    for k in range(D // 128):
                        lanes = lanes + col[:, 128 * k:128 * (k + 1)]
                loss_ref[...] = lanes
        else:
            @pl.when(fi == nf - 1)
            def _():
                for r in range(0, T, FFN_ROW_CHUNK):
                    rows = slice(r, r + FFN_ROW_CHUNK)
                    xv = xo_ref[rows, :]
                    rstd = lax.rsqrt(jnp.mean(xv * xv, axis=-1, keepdims=True) + EPS)
                    hn_ref[rows, :] = (xv * rstd * gain_ref[...]).astype(bf16)

    tok = pl.BlockSpec((T, D), lambda f: (0, 0))
    act_spec = pl.BlockSpec((1, T, tf), lambda f: (f, 0, 0))
    hbm = pl.BlockSpec(memory_space=pl.ANY)
    in_specs = [tok, pl.BlockSpec((3, tf, D), lambda f: (0, f, 0)), hbm]
    out_specs = [tok, act_spec, act_spec]
    out_shape = [SDS((T, D), f32), SDS((nf, T, tf), bf16), SDS((nf, T, tf), bf16)]
    scratch = [pltpu.SemaphoreType.DMA]
    args = [h, w, x]
    if with_loss:
        in_specs.append(hbm)
        args.append(target)
        out_specs += [tok, pl.BlockSpec((1, 128), lambda f: (0, 0))]
        out_shape += [SDS((T, D), bf16), SDS((1, 128), f32)]
        scratch = [pltpu.VMEM((tm, D), f32)] + scratch
    else:
        in_specs.append(pl.BlockSpec((1, D), lambda f: (0, 0)))
        args.append(next_gain)
        out_specs.append(tok)
        out_shape.append(SDS((T, D), bf16))
    return pl.pallas_call(
        body, grid=(nf,), in_specs=in_specs, out_specs=out_specs, out_shape=tuple(out_shape), scratch_shapes=scratch,
        compiler_params=_cparams(("arbitrary",), VMEM_LIMIT_V7X), name=name)(*args)


NORM_BWD_ROWS = 512


def _norm_bwd_scratch(dh_in_hbm, with_bf16):
    buf = lambda dt: pltpu.VMEM((2, NORM_BWD_ROWS, D), dt)
    return [buf(f32), buf(f32), buf(f32) if dh_in_hbm else None, buf(f32), buf(bf16) if with_bf16 else None,
            pltpu.SemaphoreType.DMA((6,)), pltpu.SemaphoreType.DMA((4,))]


def _norm_bwd_rows(T, dh_src, x_hbm, gain_ref, dr_hbm, dx_hbm, dxb_hbm, xbuf, rbuf, hbuf, obuf, obb, in_sems, out_sems):
    tm = min(NORM_BWD_ROWS, T)
    nchunk = T // tm

    def loads(i):
        s, rows = i % 2, pl.ds(i * tm, tm)
        cps = [pltpu.make_async_copy(x_hbm.at[rows, :], xbuf.at[s, pl.ds(0, tm), :], in_sems.at[3 * s]),
               pltpu.make_async_copy(dr_hbm.at[rows, :], rbuf.at[s, pl.ds(0, tm), :], in_sems.at[3 * s + 1])]
        if hbuf is not None:
            cps.append(pltpu.make_async_copy(dh_src.at[rows, :], hbuf.at[s, pl.ds(0, tm), :], in_sems.at[3 * s + 2]))
        return cps

    def stores(i):
        s, rows = i % 2, pl.ds(i * tm, tm)
        cps = [pltpu.make_async_copy(obuf.at[s, pl.ds(0, tm), :], dx_hbm.at[rows, :], out_sems.at[2 * s])]
        if dxb_hbm is not None:
            cps.append(pltpu.make_async_copy(obb.at[s, pl.ds(0, tm), :], dxb_hbm.at[rows, :], out_sems.at[2 * s + 1]))
        return cps

    for cp in loads(0):
        cp.start()
    dg = jnp.zeros((1, D), f32)
    for i in range(nchunk):
        s = i % 2
        if i + 1 < nchunk:
            for cp in loads(i + 1):
                cp.start()
        for cp in loads(i):
            cp.wait()
        if i >= 2:
            for cp in stores(i - 2):
                cp.wait()
        xv = xbuf[s, 0:tm, :]
        rstd = lax.rsqrt(jnp.mean(xv * xv, axis=-1, keepdims=True) + EPS)
        xh = xv * rstd
        dhv = hbuf[s, 0:tm, :] if hbuf is not None else dh_src[i * tm:(i + 1) * tm, :]
        dxh = dhv * gain_ref[...]
        dx = rbuf[s, 0:tm, :] + rstd * (dxh - xh * jnp.mean(dxh * xh, axis=-1, keepdims=True))
        obuf[s, 0:tm, :] = dx
        if dxb_hbm is not None:
            obb[s, 0:tm, :] = dx.astype(bf16)
        dg = dg + jnp.sum(dhv * xh, axis=0, keepdims=True)
        for cp in stores(i):
            cp.start()
    for i in range(max(nchunk - 2, 0), nchunk):
        for cp in stores(i):
            cp.wait()
    return dg


def _ffn_bwd(dob, h, gate, up, w, norm, name):
    T = h.shape[0]
    _, tf = _ffn_tiles(T)
    nf = F // tf
    nin = 5 if norm is None else 8
    nout = 2 if norm is None else 4

    def body(*refs):
        do_hbm, h_hbm, g_ref, u_ref, w_ref = refs[:5]
        dw_ref = refs[nin + nout - 1]
        do_v, h_v, dh_acc, dgu_s, act_s, sems = refs[nin + nout:nin + nout + 6]
        fi = pl.program_id(0)

        @pl.when(fi == 0)
        def _():
            loads = [pltpu.make_async_copy(do_hbm, do_v, sems.at[0]), pltpu.make_async_copy(h_hbm, h_v, sems.at[1])]
            for cp in loads:
                cp.start()
            dh_acc[...] = jnp.zeros_like(dh_acc)
            for cp in loads:
                cp.wait()

        wgu = w_ref[0:2].reshape(2 * tf, D)
        for r in range(0, T, FFN_ROW_CHUNK):
            rows = slice(r, r + FFN_ROW_CHUNK)
            dov = do_v[rows, :]
            gv = g_ref[0, rows, :].astype(f32)
            uv = u_ref[0, rows, :].astype(f32)
            sg = _sigmoid(gv)
            sil = gv * sg
            dact = _nt(dov, w_ref[2])
            dup = dact * sil
            dgate = dact * uv * (sg * (1.0 + gv * (1.0 - sg)))
            dgu = jnp.concatenate([dgate.astype(bf16), dup.astype(bf16)], axis=1)
            dgu_s[rows, :] = dgu
            act_s[rows, :] = (sil * uv).astype(bf16)
            dh_acc[rows, :] += _nn(dgu, wgu)
        dw_ref[0:2] = _tn(dgu_s[...], h_v[...]).reshape(2, tf, D).astype(bf16)
        dw_ref[2] = _tn(act_s[...], do_v[...]).astype(bf16)

        @pl.when(fi == nf - 1)
        def _():
            if norm is None:
                out = pltpu.make_async_copy(dh_acc, refs[nin], sems.at[0])
                out.start()
                out.wait()
            else:
                x_hbm, gain_ref, dr_hbm, dx_hbm, dxb_hbm, dg_ref = refs[5:11]
                xbuf, rbuf, obuf, obb, in_sems, out_sems = refs[nin + nout + 6:]
                dg_ref[...] = _norm_bwd_rows(T, dh_acc, x_hbm, gain_ref, dr_hbm, dx_hbm, dxb_hbm,
                                             xbuf, rbuf, None, obuf, obb, in_sems, out_sems)

    act_spec = pl.BlockSpec((1, T, tf), lambda f: (f, 0, 0))
    wspec = pl.BlockSpec((3, tf, D), lambda f: (0, f, 0))
    vec = pl.BlockSpec((1, D), lambda f: (0, 0))
    hbm = pl.BlockSpec(memory_space=pl.ANY)
    in_specs, out_specs = [hbm, hbm, act_spec, act_spec, wspec], [hbm, wspec]
    out_shape, args = [SDS((T, D), f32), SDS((3, F, D), bf16)], [dob, h, gate, up, w]
    scratch = [pltpu.VMEM((T, D), bf16), pltpu.VMEM((T, D), bf16), pltpu.VMEM((T, D), f32),
               pltpu.VMEM((T, 2 * tf), bf16), pltpu.VMEM((T, tf), bf16), pltpu.SemaphoreType.DMA((2,))]
    if norm is not None:
        in_specs += [hbm, vec, hbm]
        out_specs = [hbm, hbm, vec, wspec]
        out_shape = [SDS((T, D), f32), SDS((T, D), bf16), SDS((1, D), f32), SDS((3, F, D), bf16)]
        args += list(norm)
        scratch += [sc for sc in _norm_bwd_scratch(False, True) if sc is not None]
    return pl.pallas_call(
        body, grid=(nf,), in_specs=in_specs, out_specs=out_specs, out_shape=tuple(out_shape), scratch_shapes=scratch,
        compiler_params=_cparams(("arbitrary",), VMEM_LIMIT_V7X), name=name)(*args)


def _in_proj_fwd(h, wint, name):
    T = h.shape[0]
    tm = min(512, T)

    def body(h_ref, w_ref, z_ref):
        z_ref[...] = _nt(h_ref[...], w_ref[...])

    return pl.pallas_call(
        body, grid=(T // tm,),
        in_specs=[pl.BlockSpec((tm, D), lambda i: (i, 0)), pl.BlockSpec((DIN, D), lambda i: (0, 0))],
        out_specs=pl.BlockSpec((tm, DIN), lambda i: (i, 0)),
        out_shape=SDS((T, DIN), f32), name=name)(h, wint)


def _in_proj_bwd(dz, wint, h, norm, out_scale, name):
    x, g, dres = norm
    T = h.shape[0]
    tm = min(512, T)
    nt = T // tm

    def body(dz_ref, w_ref, h_ref, x_ref, g_ref, dr_ref, dx_ref, dxb_ref, dg_ref, dw_ref, acc):
        i = pl.program_id(0)
        dzb = dz_ref[...].astype(bf16)
        dhv = _nn(dzb, w_ref[...])
        part = _tn(dzb, h_ref[...])
        xv = x_ref[...]
        rstd = lax.rsqrt(jnp.mean(xv * xv, axis=-1, keepdims=True) + EPS)
        xh = xv * rstd
        dxh = dhv * g_ref[...]
        dx = dr_ref[...] + rstd * (dxh - xh * jnp.mean(dxh * xh, axis=-1, keepdims=True))
        dx_ref[...] = dx
        dxb_ref[...] = (out_scale * dx).astype(bf16)
        dg = jnp.sum(dhv * xh, axis=0, keepdims=True)

        @pl.when(i == 0)
        def _():
            acc[...] = part
            dg_ref[...] = dg

        @pl.when(i > 0)
        def _():
            acc[...] += part
            dg_ref[...] += dg

        @pl.when(i == nt - 1)
        def _():
            dw_ref[...] = acc[...].astype(bf16)

    wspec = pl.BlockSpec((DIN, D), lambda i: (0, 0))
    tok = pl.BlockSpec((tm, D), lambda i: (i, 0))
    vec = pl.BlockSpec((1, D), lambda i: (0, 0))
    return pl.pallas_call(
        body, grid=(nt,),
        in_specs=[pl.BlockSpec((tm, DIN), lambda i: (i, 0)), wspec, tok, tok, vec, tok],
        out_specs=[tok, tok, vec, wspec],
        out_shape=(SDS((T, D), f32), SDS((T, D), bf16), SDS((1, D), f32), SDS((DIN, D), bf16)),
        scratch_shapes=[pltpu.VMEM((DIN, D), f32)],
        compiler_params=_cparams(("arbitrary",)), name=name)(dz, wint, h, x, g, dres)


def _out_proj_fwd(ymix, wout, x, g, name):
    T = x.shape[0]
    tm = min(512, T)

    def body(y_ref, w_ref, x_ref, g_ref, o_ref, h_ref):
        o = x_ref[...] + _nn(y_ref[...], w_ref[...])
        o_ref[...] = o
        r = lax.rsqrt(jnp.mean(o * o, axis=-1, keepdims=True) + EPS)
        h_ref[...] = (o * r * g_ref[...]).astype(bf16)

    tok = pl.BlockSpec((tm, D), lambda i: (i, 0))
    return pl.pallas_call(
        body, grid=(T // tm,),
        in_specs=[pl.BlockSpec((tm, DMIX), lambda i: (i, 0)), pl.BlockSpec((DMIX, D), lambda i: (0, 0)), tok,
                  pl.BlockSpec((1, D), lambda i: (0, 0))],
        out_specs=[tok, tok], out_shape=(SDS((T, D), f32), SDS((T, D), bf16)), name=name)(ymix, wout, x, g)


def _out_proj_bwd(dxb, wout, ymix, name):
    T = dxb.shape[0]
    tm = min(512, T)
    nt = T // tm

    def body(dx_ref, w_ref, y_ref, dy_ref, dw_ref, acc):
        i = pl.program_id(0)
        dxv = dx_ref[...]
        dy_ref[...] = _nt(dxv, w_ref[...])
        part = _tn(y_ref[...], dxv)

        @pl.when(i == 0)
        def _():
            acc[...] = part

        @pl.when(i > 0)
        def _():
            acc[...] += part

        @pl.when(i == nt - 1)
        def _():
            dw_ref[...] = acc[...].astype(bf16)

    wspec = pl.BlockSpec((DMIX, D), lambda i: (0, 0))
    return pl.pallas_call(
        body, grid=(nt,),
        in_specs=[pl.BlockSpec((tm, D), lambda i: (i, 0)), wspec, pl.BlockSpec((tm, DMIX), lambda i: (i, 0))],
        out_specs=[pl.BlockSpec((tm, DMIX), lambda i: (i, 0)), wspec],
        out_shape=(SDS((T, DMIX), f32), SDS((DMIX, D), bf16)),
        scratch_shapes=[pltpu.VMEM((DMIX, D), f32)],
        compiler_params=_cparams(("arbitrary",)), name=name)(dxb, wout, ymix)


def _t5_bucket_table():
    ql = np.arange(BLK)[:, None]
    kl = np.arange(2 * BLK)[None, :]
    n = np.maximum(ql + BLK - kl, 0)
    max_exact = NBUCK // 2
    large = max_exact + (np.log(np.maximum(n, 1) / max_exact) / np.log(MAX_DISTANCE / max_exact)
                         * (NBUCK - max_exact)).astype(np.int32)
    large = np.minimum(large, NBUCK - 1)
    return np.where(n < max_exact, n, large).astype(np.int32)


def _fill_bias(bk_ref, rb_ref, bias_scr):
    bk = bk_ref[...]
    for h in range(NH):
        def step(b, acc, h=h):
            return acc + jnp.where(bk == b, rb_ref[b, h], 0.0)
        bias_scr[h] = lax.fori_loop(0, NBUCK, step, jnp.zeros((BLK, 2 * BLK), f32))


MIX_SUB = 4


class _Window:
    def __init__(self, zc_ref, zp_ref, n, s):
        self.blk = n * MIX_SUB + s
        self.cur = lambda a, b: zc_ref[s * BLK:(s + 1) * BLK, a:b]
        self.prev = (lambda a, b: zp_ref[:, a:b]) if s == 0 else (lambda a, b: zc_ref[(s - 1) * BLK:s * BLK, a:b])


def _attn_qkv(win, kh, qg, kg):
    kc = DATTN + HD * kh
    vc = DATTN + DKV + HD * kh
    kx = jnp.concatenate([win.prev(kc, kc + HD), win.cur(kc, kc + HD)], axis=0)
    vx = jnp.concatenate([win.prev(vc, vc + HD), win.cur(vc, vc + HD)], axis=0)
    qx = jnp.concatenate([win.cur(HD * (GQA * kh + g), HD * (GQA * kh + g + 1)) for g in range(GQA)], axis=0)
    rq = lax.rsqrt(jnp.mean(qx * qx, axis=-1, keepdims=True) + EPS)
    rk = lax.rsqrt(jnp.mean(kx * kx, axis=-1, keepdims=True) + EPS)
    qhat, khat = qx * rq, kx * rk
    return dict(qhat=qhat, khat=khat, rq=rq, rk=rk, qsb=(qhat * (qg * SCALE)).astype(bf16),
                knb=(khat * kg).astype(bf16), vb=vx.astype(bf16))


def _window_masks(n):
    row = lax.broadcasted_iota(i32, (GQA * BLK, 2 * BLK), 0) & (BLK - 1)
    col = lax.broadcasted_iota(i32, (GQA * BLK, 2 * BLK), 1)
    band = (col > row) & (col <= row + BLK)
    return band & ((col >= BLK) | (n > 0)), band


def _attn_probs(a, kh, sk_ref, bias_scr, mask):
    s = _nt(a["qsb"], a["knb"]) + bias_scr[GQA * kh:GQA * (kh + 1)].reshape(GQA * BLK, 2 * BLK)
    s = jnp.where(mask, s, NEG)
    ridx = lax.broadcasted_iota(i32, (GQA * BLK, 1), 0)
    sink = jnp.full((GQA * BLK, 1), sk_ref[GQA * kh + GQA - 1], f32)
    for g in range(GQA - 2, -1, -1):
        sink = jnp.where(ridx < (g + 1) * BLK, sk_ref[GQA * kh + g], sink)
    m = jnp.maximum(jnp.max(s, axis=-1, keepdims=True), sink)
    e = jnp.exp(s - m)
    den = jnp.sum(e, axis=-1, keepdims=True) + jnp.exp(sink - m)
    return e / den


POOL_STEPS = {2: (1,), 4: (1, 2), 8: (1, 2, 4), 16: (1, 2, 4, 8)}


def _pool_group(win, g, w):
    n = win.blk
    c0 = DATTN + 2 * DKV + PGD * g
    uc = win.cur(c0, c0 + PGD)
    up = jnp.where(n > 0, win.prev(c0, c0 + PGD), 0.0)
    sm = jnp.concatenate([up, uc], axis=0)
    for k in POOL_STEPS[w]:
        sm = sm + pltpu.roll(sm, k, axis=0)
    pos = n * BLK + lax.broadcasted_iota(i32, (BLK, 1), 0) + 1
    cnt = jnp.minimum(pos, w).astype(f32)
    return sm[BLK:2 * BLK] / cnt - uc, cnt


def _mix_fwd(z, qg, kg, sinks, relb, bucket, pool_w, pscale, name):
    T = z.shape[0]
    step_rows = MIX_SUB * BLK
    nsteps = T // step_rows

    def body(zc_ref, zp_ref, qg_ref, kg_ref, sk_ref, rb_ref, bk_ref, pw_ref, ps_ref, y_ref, p_ref, bias_scr, yacc):
        n = pl.program_id(0)

        @pl.when(n == 0)
        def _():
            _fill_bias(bk_ref, rb_ref, bias_scr)

        first_mask, mask = _window_masks(n)
        for s in range(MIX_SUB):
            win = _Window(zc_ref, zp_ref, n, s)
            rows = slice(s * BLK, (s + 1) * BLK)
            for kh in range(NKV):
                a = _attn_qkv(win, kh, qg_ref[...], kg_ref[...])
                pb = _attn_probs(a, kh, sk_ref, bias_scr, first_mask if s == 0 else mask).astype(bf16)
                p_ref[s, GQA * kh:GQA * (kh + 1)] = pb.reshape(GQA, BLK, 2 * BLK)
                o = _nn(pb, a["vb"])
                for g in range(GQA):
                    hc = HD * (GQA * kh + g)
                    yacc[rows, hc:hc + HD] = o[g * BLK:(g + 1) * BLK]
            for g, w in enumerate(POOL_WINDOWS):
                pooled, _ = _pool_group(win, g, w)
                yp = _nn(pooled.astype(bf16), pw_ref[g].astype(bf16)) * ps_ref[:, PGD * g:PGD * (g + 1)]
                yacc[rows, DATTN + PGD * g:DATTN + PGD * (g + 1)] = yp
        y_ref[...] = yacc[...].astype(bf16)

    full = lambda *shape: pl.BlockSpec(shape, lambda n: (0,) * len(shape))
    smem = pl.BlockSpec(memory_space=pltpu.SMEM)
    return pl.pallas_call(
        body, grid=(nsteps,),
        in_specs=[pl.BlockSpec((step_rows, DIN), lambda n: (n, 0)),
                  pl.BlockSpec((BLK, DIN), lambda n: (jnp.maximum(n * MIX_SUB - 1, 0), 0)),
                  full(1, HD), full(1, HD), smem, smem, full(BLK, 2 * BLK),
                  full(len(POOL_WINDOWS), PGD, PGD), full(1, DPOOL)],
        out_specs=[pl.BlockSpec((step_rows, DMIX), lambda n: (n, 0)),
                   pl.BlockSpec((MIX_SUB, NH, BLK, 2 * BLK), lambda n: (n, 0, 0, 0))],
        out_shape=(SDS((T, DMIX), bf16), SDS((T // BLK, NH, BLK, 2 * BLK), bf16)),
        scratch_shapes=[pltpu.VMEM((NH, BLK, 2 * BLK), f32), pltpu.VMEM((step_rows, DMIX), f32)],
        compiler_params=_cparams(("arbitrary",)), name=name)(z, z, qg, kg, sinks, relb, bucket, pool_w, pscale)


def _mix_bwd(z, dy, probs, qg, kg, relb, bucket, pool_w, pscale, name):
    T = z.shape[0]
    step_rows = MIX_SUB * BLK
    nsteps = T // step_rows

    def body(zc_ref, zp_ref, dy_ref, p_ref, qg_ref, kg_ref, bk_ref, pw_ref, ps_ref,
             dz_ref, dqg_ref, dkg_ref, dsk_ref, drb_ref, dpw_ref, dps_ref, dbias_scr):
        n = pl.program_id(0)

        @pl.when(n == 0)
        def _():
            dbias_scr[...] = jnp.zeros_like(dbias_scr)
            dqg_ref[...] = jnp.zeros_like(dqg_ref)
            dkg_ref[...] = jnp.zeros_like(dkg_ref)
            dpw_ref[...] = jnp.zeros_like(dpw_ref)
            dps_ref[...] = jnp.zeros_like(dps_ref)

        qg, kg = qg_ref[...], kg_ref[...]
        for s in range(MIX_SUB):
            win = _Window(zc_ref, zp_ref, n, s)
            blk = win.blk
            rows = pl.ds(pl.multiple_of(blk * BLK, BLK), BLK)
            prow = pl.ds(pl.multiple_of(jnp.maximum(blk - 1, 0) * BLK, BLK), BLK)
            dyr = slice(s * BLK, (s + 1) * BLK)

            def into_prev(fn, s=s):
                if s == 0:
                    pl.when(n > 0)(fn)
                else:
                    fn()

            for kh in range(NKV):
                a = _attn_qkv(win, kh, qg, kg)
                pb = p_ref[s, GQA * kh:GQA * (kh + 1)].reshape(GQA * BLK, 2 * BLK)
                p = pb.astype(f32)
                do = jnp.concatenate([dy_ref[dyr, HD * (GQA * kh + g):HD * (GQA * kh + g + 1)] for g in range(GQA)],
                                     axis=0).astype(bf16)
                dv = _tn(pb, do)
                dp = _nt(do, a["vb"])
                delta = jnp.sum(p * dp, axis=-1, keepdims=True)
                ds = p * (dp - delta)
                for g in range(GQA):
                    dbias_scr[GQA * kh + g] += ds[g * BLK:(g + 1) * BLK]
                dsb = ds.astype(bf16)
                dqn = _nn(dsb, a["knb"]) * SCALE
                dkn = _tn(dsb, a["qsb"])
                qhat, khat = a["qhat"], a["khat"]
                dqg_ref[...] += jnp.sum(dqn * qhat, axis=0, keepdims=True)
                dkg_ref[...] += jnp.sum(dkn * khat, axis=0, keepdims=True)
                dqh = dqn * qg
                dq = a["rq"] * (dqh - qhat * jnp.mean(dqh * qhat, axis=-1, keepdims=True))
                dkh = dkn * kg
                dk = a["rk"] * (dkh - khat * jnp.mean(dkh * khat, axis=-1, keepdims=True))
                kc = DATTN + HD * kh
                vc = DATTN + DKV + HD * kh
                for g in range(GQA):
                    hc = HD * (GQA * kh + g)
                    dz_ref[rows, hc:hc + HD] = dq[g * BLK:(g + 1) * BLK]
                dz_ref[rows, kc:kc + HD] = dk[BLK:2 * BLK]
                dz_ref[rows, vc:vc + HD] = dv[BLK:2 * BLK]

                def kv_prev(dk=dk, dv=dv, kc=kc, vc=vc, prow=prow):
                    dz_ref[prow, kc:kc + HD] += dk[0:BLK]
                    dz_ref[prow, vc:vc + HD] += dv[0:BLK]

                into_prev(kv_prev)

            for g, w in enumerate(POOL_WINDOWS):
                c0 = DATTN + 2 * DKV + PGD * g
                pooled, cnt = _pool_group(win, g, w)
                pb = pooled.astype(bf16)
                wb = pw_ref[g].astype(bf16)
                dyp = dy_ref[dyr, DATTN + PGD * g:DATTN + PGD * (g + 1)]
                ypre = _nn(pb, wb)
                dps_ref[:, PGD * g:PGD * (g + 1)] += jnp.sum(dyp * ypre, axis=0, keepdims=True)
                dyg = (dyp * ps_ref[:, PGD * g:PGD * (g + 1)]).astype(bf16)
                dpw_ref[g] += _tn(pb, dyg)
                dpooled = _nt(dyg, wb)
                due = jnp.concatenate([jnp.zeros((BLK, PGD), f32), dpooled / cnt], axis=0)
                for k in POOL_STEPS[w]:
                    due = due + pltpu.roll(due, 2 * BLK - k, axis=0)
                dz_ref[rows, c0:c0 + PGD] = due[BLK:2 * BLK] - dpooled

                def pool_prev(due=due, c0=c0, prow=prow):
                    dz_ref[prow, c0:c0 + PGD] += due[0:BLK]

                into_prev(pool_prev)

        @pl.when(n == nsteps - 1)
        def _():
            bk = bk_ref[...]
            ri = lax.broadcasted_iota(i32, (NBUCK, NH), 0)
            ci = lax.broadcasted_iota(i32, (NBUCK, NH), 1)

            def step(b, acc):
                for h in range(NH):
                    sel = jnp.where(bk == b, dbias_scr[h], 0.0)
                    tot = jnp.sum(jnp.sum(sel, axis=1, keepdims=True), axis=0, keepdims=True)
                    acc = acc + jnp.where((ri == b) & (ci == h), tot, 0.0)
                return acc

            drb_ref[...] = lax.fori_loop(0, NBUCK, step, jnp.zeros((NBUCK, NH), f32))
            lane = lax.broadcasted_iota(i32, (1, 128), 1)
            dsk = jnp.zeros((1, 128), f32)
            for h in range(NH):
                tot = jnp.sum(jnp.sum(dbias_scr[h], axis=1, keepdims=True), axis=0, keepdims=True)
                dsk = dsk - jnp.where(lane == h, tot, 0.0)
            dsk_ref[...] = dsk

    full = lambda *shape: pl.BlockSpec(shape, lambda n: (0,) * len(shape))
    npg = len(POOL_WINDOWS)
    return pl.pallas_call(
        body, grid=(nsteps,),
        in_specs=[pl.BlockSpec((step_rows, DIN), lambda n: (n, 0)),
                  pl.BlockSpec((BLK, DIN), lambda n: (jnp.maximum(n * MIX_SUB - 1, 0), 0)),
                  pl.BlockSpec((step_rows, DMIX), lambda n: (n, 0)),
                  pl.BlockSpec((MIX_SUB, NH, BLK, 2 * BLK), lambda n: (n, 0, 0, 0)),
                  full(1, HD), full(1, HD), full(BLK, 2 * BLK), full(npg, PGD, PGD), full(1, DPOOL)],
        out_specs=[full(T, DIN), full(1, HD), full(1, HD), full(1, 128), full(NBUCK, NH),
                   full(npg, PGD, PGD), full(1, DPOOL)],
        out_shape=(SDS((T, DIN), f32), SDS((1, HD), f32), SDS((1, HD), f32), SDS((1, 128), f32),
                   SDS((NBUCK, NH), f32), SDS((npg, PGD, PGD), f32), SDS((1, DPOOL), f32)),
        scratch_shapes=[pltpu.VMEM((NH, BLK, 2 * BLK), f32)],
        compiler_params=_cparams(("arbitrary",), VMEM_LIMIT_V7X),
        name=name)(z, z, dy, probs, qg, kg, bucket, pool_w, pscale)


class _LocalWeights:
    def __init__(self, w1, wint, wout, w2):
        self.w1, self.wint, self.wout, self.w2 = w1, wint, wout, w2

    def ffn1(self):
        return self.w1

    def first_norm(self, x, gain):
        return _norm_fwd(x, gain, "norm1_fwd")

    def after_ffn1(self, gain, x1):
        return gain

    def mix(self, after):
        return self.wint, self.wout

    def before_out_proj(self, wout, after):
        return wout

    def ffn2(self, after):
        return self.w2

    def out_ffn2_grads_ready(self, dwout, dw2, after):
        return after

    def before_ffn1_bwd(self, dwint, dx1b):
        return dx1b


def _local_step(x, target, weights, g1, gm, g3, qg, kg, sinks, relb, pool_w, pscale):
    bucket = jnp.asarray(_t5_bucket_table())
    sk = sinks.reshape(NH)
    w1 = weights.ffn1()
    h1 = weights.first_norm(x, g1)
    x1, gate1, up1, h2 = _ffn_fwd(h1, w1, x, None, gm, "ffn1_fwd")
    gm = weights.after_ffn1(gm, x1)
    wint, wout = weights.mix(h2)
    z = _in_proj_fwd(h2, wint, "in_proj_fwd")
    ymix, probs = _mix_fwd(z, qg, kg, sk, relb, bucket, pool_w, pscale, "mix_fwd")
    wout = weights.before_out_proj(wout, ymix)
    x2, h3 = _out_proj_fwd(ymix, wout, x1, g3, "out_proj_fwd")
    w2 = weights.ffn2(h3)
    dy, gate2, up2, dyb, loss_lanes = _ffn_fwd(h3, w2, x2, target, None, "ffn2_fwd")

    dx2, dx2b, dg3, dw2 = _ffn_bwd(dyb, h3, gate2, up2, w2, (x2, g3, dy), "ffn2_bwd")
    dymix, dwout = _out_proj_bwd(dx2b, wout, ymix, "out_proj_bwd")
    dymix = weights.out_ffn2_grads_ready(dwout, dw2, dymix)
    dz, dqg, dkg, dsk, drb, dpw, dps = _mix_bwd(z, dymix, probs, qg, kg, relb, bucket, pool_w, pscale, "mix_bwd")
    dx1, dx1b, dgm, dwint = _in_proj_bwd(dz, wint, h2, (x1, gm, dx2), 0.5, "in_proj_bwd")
    dx1b = weights.before_ffn1_bwd(dwint, dx1b)
    dh1, dw1 = _ffn_bwd(dx1b, h1, gate1, up1, w1, None, "ffn1_bwd")
    small = dict(mix_norm=dgm, ffn2_norm=dg3, pool_scale=dps, q_norm=dqg, k_norm=dkg,
                 attn_sinks=dsk[:, :NH], rel_bias=drb, pool_w=dpw, loss=loss_lanes)
    return (dh1, dx1), (dw1, dwint, dwout, dw2), small


SMALL_NAMES = ("ffn1_norm", "mix_norm", "ffn2_norm", "pool_scale", "q_norm", "k_norm", "attn_sinks", "rel_bias",
               "pool_w", "loss")
SMALL_SHAPES = dict(ffn1_norm=(1, D), mix_norm=(1, D), ffn2_norm=(1, D), pool_scale=(1, DPOOL), q_norm=(1, HD),
                    k_norm=(1, HD), attn_sinks=(1, NH), rel_bias=(NBUCK, NH),
                    pool_w=(1, len(POOL_WINDOWS), PGD, PGD), loss=(1, 128))


def _small_rows(name):
    return -(-int(np.prod(SMALL_SHAPES[name])) // 128)


SMALL_OFF = {}
_r = 0
for _n in SMALL_NAMES:
    SMALL_OFF[_n] = _r
    _r += _small_rows(_n)
SMALL_ROWS = -(-_r // 16) * 16
LOSS_ROW = SMALL_OFF["loss"]


def _pack_small(vals):
    parts = []
    for n in SMALL_NAMES:
        size = _small_rows(n) * 128
        if n in vals:
            flat = vals[n].astype(f32).reshape(-1)
            parts.append(jnp.pad(flat, (0, size - flat.shape[0])))
        else:
            parts.append(jnp.zeros((size,), f32))
    flat = jnp.concatenate(parts)
    flat = jnp.pad(flat, (0, SMALL_ROWS * 128 - flat.shape[0]))
    return flat.reshape(SMALL_ROWS, 128)


def _unpack_small(packed, name):
    size = int(np.prod(SMALL_SHAPES[name]))
    r0 = SMALL_OFF[name]
    return packed[r0:r0 + _small_rows(name)].reshape(-1)[:size].reshape(SMALL_SHAPES[name])


def _position():
    return lax.axis_index("x"), lax.axis_index("y"), lax.axis_index("c")


def _dev_index(x, y, c):
    return 4 * x + 2 * y + c


G1_PIECES, MIX_PIECES, F2_PIECES = (0, 1, 2), (3, 4), (5, 6, 7)


def _group_rows(pieces):
    return sum(PIECE_ROWS[k] for k in pieces)


def _shard_piece(s_ref, k):
    return s_ref.at[pl.ds(PIECE_OFF[k], PIECE_ROWS[k]), :]


def _shard_group(s_ref, pieces):
    return s_ref.at[pl.ds(PIECE_OFF[pieces[0]], _group_rows(pieces)), :]


def _weight_pieces(w1_ref=None, wi_ref=None, wo_ref=None, w2_ref=None):
    arrs = {}
    if w1_ref is not None:
        arrs.update({0: w1_ref.at[0], 1: w1_ref.at[1], 2: w1_ref.at[2]})
    if wi_ref is not None:
        arrs[3] = wi_ref
    if wo_ref is not None:
        arrs[4] = wo_ref
    if w2_ref is not None:
        arrs.update({5: w2_ref.at[0], 6: w2_ref.at[1], 7: w2_ref.at[2]})
    return arrs


def _block_rows(arrs, k, dev):
    r = PIECE_ROWS[k]
    return arrs[k].at[pl.ds(pl.multiple_of(_dev_index(*dev) * r, 16), r), :]


NORM_ROWS = 512


def _all_gather_ffn1(shard, x, gain):
    pieces = G1_PIECES
    rest_pieces = MIX_PIECES + F2_PIECES
    half = FS // 2
    T = x.shape[0]
    SIB, X0, X1, Y0, Y1, RELAY_Y, RELAY_X, ON_X, ON_Y, ON_D0, ON_D1 = range(11)

    def body(s_ref, x_ref, g_ref, w1_ref, h_ref, wi_ref, wo_ref, w2_ref, xbuf, hbuf, rest_buf,
             send_sems, recv_sems, local_sem, norm_sems):
        x, y, c = _position()
        me, sib = (x, y, c), (x, y, 1 - c)
        xn, yn, dg = (1 - x, y, c), (x, 1 - y, c), (1 - x, 1 - y, c)
        arrs = _weight_pieces(w1_ref=w1_ref)

        def place_rest():
            rest = _weight_pieces(wi_ref=wi_ref, wo_ref=wo_ref, w2_ref=w2_ref)
            grp = _shard_group(s_ref, rest_pieces)
            load = pltpu.make_async_copy(grp, rest_buf, norm_sems.at[0])
            load.start()
            load.wait()
            base = PIECE_OFF[rest_pieces[0]]
            for k in rest_pieces:
                pltpu.make_async_copy(rest_buf.at[pl.ds(PIECE_OFF[k] - base, PIECE_ROWS[k]), :],
                                      _block_rows(rest, k, me), norm_sems.at[1]).start()
            pltpu.make_async_copy(grp, rest_buf, norm_sems.at[1]).wait()

        def first_norm():
            for r in range(0, T, NORM_ROWS):
                load = pltpu.make_async_copy(x_ref.at[pl.ds(r, NORM_ROWS), :], xbuf, norm_sems.at[0])
                load.start()
                load.wait()
                xv = xbuf[...]
                rs = lax.rsqrt(jnp.mean(xv * xv, axis=-1, keepdims=True) + EPS)
                hbuf[...] = (xv * rs * g_ref[...]).astype(bf16)
                store = pltpu.make_async_copy(hbuf, h_ref.at[pl.ds(r, NORM_ROWS), :], norm_sems.at[1])
                store.start()
                store.wait()

        def rows_of(k, block, hf):
            r = PIECE_ROWS[k]
            start, size = (0, r) if hf is None else (hf * half, half)
            return arrs[k].at[pl.ds(pl.multiple_of(_dev_index(*block) * r + start, 16), size), :]

        def copies(rel, block, hf, to, from_shard=False):
            def src(k):
                if not from_shard:
                    return rows_of(k, block, hf)
                start, size = (0, PIECE_ROWS[k]) if hf is None else (hf * half, half)
                return s_ref.at[pl.ds(PIECE_OFF[k] + start, size), :]
            return [pltpu.make_async_remote_copy(
                src_ref=src(k), dst_ref=rows_of(k, block, hf), send_sem=send_sems.at[rel], recv_sem=recv_sems.at[rel],
                device_id=to, device_id_type=MESH) for k in pieces]

        def waiter(rel, hf):
            nrows = len(pieces) * (FS if hf is None else half)
            grp = s_ref.at[pl.ds(0, nrows), :]
            return pltpu.make_async_remote_copy(src_ref=grp, dst_ref=grp, send_sem=send_sems.at[rel],
                                                recv_sem=recv_sems.at[rel], device_id=me, device_id_type=MESH)

        def start(cps):
            for cp in cps:
                cp.start()

        mine = [pltpu.make_async_copy(_shard_piece(s_ref, k), _block_rows(arrs, k, me), local_sem) for k in pieces]
        start(mine)
        start(copies(SIB, me, None, sib, True))
        start(copies(X0, me, 0, xn, True))
        start(copies(Y1, me, 1, yn, True))
        start(copies(X1, me, 1, xn, True))
        start(copies(Y0, me, 0, yn, True))
        first_norm()
        place_rest()
        waiter(X0, 0).wait_recv()
        start(copies(RELAY_Y, xn, 0, yn))
        waiter(Y1, 1).wait_recv()
        start(copies(RELAY_X, yn, 1, xn))
        waiter(X1, 1).wait_recv()
        start(copies(ON_X, xn, None, sib))
        waiter(Y0, 0).wait_recv()
        start(copies(ON_Y, yn, None, sib))
        waiter(RELAY_Y, 0).wait_recv()
        start(copies(ON_D0, dg, 0, sib))
        waiter(RELAY_X, 1).wait_recv()
        start(copies(ON_D1, dg, 1, sib))
        waiter(SIB, None).wait_recv()
        waiter(ON_X, None).wait_recv()
        waiter(ON_Y, None).wait_recv()
        waiter(ON_D0, 0).wait_recv()
        waiter(ON_D1, 1).wait_recv()
        for rel, hf in ((SIB, None), (X0, 0), (X1, 1), (Y0, 0), (Y1, 1), (RELAY_Y, 0), (RELAY_X, 1),
                        (ON_X, None), (ON_Y, None), (ON_D0, 0), (ON_D1, 1)):
            waiter(rel, hf).wait_send()
        grp = _shard_group(s_ref, pieces)
        pltpu.make_async_copy(grp, grp, local_sem).wait()

    hbm = pl.BlockSpec(memory_space=pl.ANY)
    return pl.pallas_call(
        body, in_specs=[hbm, hbm, pl.BlockSpec(memory_space=pltpu.VMEM)], out_specs=[hbm] * 5,
        out_shape=(SDS((3, F, D), bf16), SDS((T, D), bf16),
                   SDS((DIN, D), bf16), SDS((DMIX, D), bf16), SDS((3, F, D), bf16)),
        scratch_shapes=[pltpu.VMEM((NORM_ROWS, D), f32), pltpu.VMEM((NORM_ROWS, D), bf16),
                        pltpu.VMEM((_group_rows(rest_pieces), D), bf16),
                        pltpu.SemaphoreType.DMA((11,)), pltpu.SemaphoreType.DMA((11,)), pltpu.SemaphoreType.DMA,
                        pltpu.SemaphoreType.DMA((2,))],
        compiler_params=pltpu.CompilerParams(has_side_effects=True),
        name="all_gather_ffn1")(shard, x, gain)


HBM_SPEC = pl.BlockSpec(memory_space=pltpu.HBM)
SEM_SPEC = pl.BlockSpec(memory_space=pltpu.SEMAPHORE)
ANY_SPEC = pl.BlockSpec(memory_space=pl.ANY)
SPLIT_EFFECT = pltpu.SideEffectType.DATAFLOW_SIDE_EFFECTING


def _in_hbm(a):
    return pltpu.with_memory_space_constraint(a, pltpu.HBM)


def _hbm_like(a):
    return pltpu.HBM(a.shape, a.dtype)


def _gather_rest_start(shard, wi, wo, w2, w1):
    def body(s_ref, wi_ref, wo_ref, w2_ref, w1_ref,
             ssem_m, rsem_m0, rsem_m, ssem_f, rsem_f0, rsem_f, s_o, wi_o, wo_o, w2_o, w1_o):
        x, y, c = _position()
        me, sib = (x, y, c), (x, y, 1 - c)
        chips = [(1 - x, y), (x, 1 - y), (1 - x, 1 - y)]
        arrs = _weight_pieces(wi_ref=wi_ref, wo_ref=wo_ref, w2_ref=w2_ref)
        for pieces, ssem, rsem0, rsem in ((MIX_PIECES, ssem_m, rsem_m0, rsem_m), (F2_PIECES, ssem_f, rsem_f0, rsem_f)):
            for p in pieces:
                pltpu.make_async_remote_copy(
                    src_ref=_shard_piece(s_ref, p), dst_ref=_block_rows(arrs, p, me), send_sem=ssem.at[0],
                    recv_sem=rsem0, device_id=sib, device_id_type=MESH).start()
            for j, chip in enumerate(chips):
                for p in pieces:
                    pltpu.make_async_remote_copy(
                        src_ref=_shard_piece(s_ref, p), dst_ref=_block_rows(arrs, p, me), send_sem=ssem.at[1 + j],
                        recv_sem=rsem.at[j], device_id=(*chip, c), device_id_type=MESH).start()

    dma = pltpu.SemaphoreType.DMA
    return pl.pallas_call(
        body, name="gather_rest_start",
        out_shape=(dma((4,)), dma(()), dma((3,)), dma((4,)), dma(()), dma((3,)),
                   _hbm_like(shard), _hbm_like(wi), _hbm_like(wo), _hbm_like(w2), _hbm_like(w1)),
        in_specs=(HBM_SPEC,) * 5, out_specs=(SEM_SPEC,) * 6 + (HBM_SPEC,) * 5,
        input_output_aliases={0: 6, 1: 7, 2: 8, 3: 9, 4: 10},
        compiler_params=pltpu.CompilerParams(has_side_effects=SPLIT_EFFECT),
    )(_in_hbm(shard), _in_hbm(wi), _in_hbm(wo), _in_hbm(w2), _in_hbm(w1))


def _gather_mix_pass_on(rsem_m, wi, wo, thru, after):
    def body(wi_ref, wo_ref, thru_ref, rsem, after_ref, fsend, frecv, wi_o, wo_o, thru_o):
        x, y, c = _position()
        sib = (x, y, 1 - c)
        arrs = _weight_pieces(wi_ref=wi_ref, wo_ref=wo_ref)
        both = wi_ref.at[pl.ds(0, _group_rows(MIX_PIECES)), :]
        for j, chip in enumerate([(1 - x, y), (x, 1 - y), (1 - x, 1 - y)]):
            pltpu.make_async_remote_copy(src_ref=both, dst_ref=both, send_sem=fsend.at[j], recv_sem=rsem.at[j],
                                         device_id=(x, y, c), device_id_type=MESH).wait_recv()
            for p in MIX_PIECES:
                rows = _block_rows(arrs, p, (*chip, c))
                pltpu.make_async_remote_copy(src_ref=rows, dst_ref=rows, send_sem=fsend.at[j], recv_sem=frecv.at[j],
                                             device_id=sib, device_id_type=MESH).start()

    dma = pltpu.SemaphoreType.DMA
    return pl.pallas_call(
        body, name="gather_mix_pass_on",
        out_shape=(dma((3,)), dma((3,)), _hbm_like(wi), _hbm_like(wo), _hbm_like(thru)),
        in_specs=(HBM_SPEC, HBM_SPEC, HBM_SPEC, SEM_SPEC, ANY_SPEC), out_specs=(SEM_SPEC, SEM_SPEC) + (HBM_SPEC,) * 3,
        input_output_aliases={0: 2, 1: 3, 2: 4},
        compiler_params=pltpu.CompilerParams(has_side_effects=SPLIT_EFFECT),
    )(wi, wo, _in_hbm(thru), rsem_m, after)


def _gather_mix_wait(ssem_m, rsem_m0, fsend, frecv, shard, wi, wo, after):
    def body(s_ref, wi_ref, wo_ref, ssem, rsem0, fs, fr, after_ref, s_o, wi_o, wo_o):
        x, y, c = _position()
        grp = _shard_group(s_ref, MIX_PIECES)

        def waiter(send_sem, recv_sem):
            return pltpu.make_async_remote_copy(src_ref=grp, dst_ref=grp, send_sem=send_sem, recv_sem=recv_sem,
                                                device_id=(x, y, c), device_id_type=MESH)

        waiter(ssem.at[0], rsem0).wait_recv()
        for j in range(3):
            waiter(fs.at[j], fr.at[j]).wait_recv()
        for rel in range(4):
            waiter(ssem.at[rel], rsem0).wait_send()
        for j in range(3):
            waiter(fs.at[j], fr.at[j]).wait_send()

    return pl.pallas_call(
        body, name="gather_mix_wait", out_shape=(_hbm_like(shard), _hbm_like(wi), _hbm_like(wo)),
        in_specs=(HBM_SPEC,) * 3 + (SEM_SPEC,) * 4 + (ANY_SPEC,), out_specs=(HBM_SPEC,) * 3,
        input_output_aliases={0: 0, 1: 1, 2: 2},
        compiler_params=pltpu.CompilerParams(has_side_effects=SPLIT_EFFECT),
    )(shard, wi, wo, ssem_m, rsem_m0, fsend, frecv, after)


def _gather_ffn2_pass_on(rsem_f, w2, wo, after):
    def body(w2_ref, wo_ref, rsem, after_ref, fsend, frecv, w2_o, wo_o):
        x, y, c = _position()
        sib = (x, y, 1 - c)
        chips = [(1 - x, y), (x, 1 - y), (1 - x, 1 - y)]
        arrs = _weight_pieces(w2_ref=w2_ref)
        three = w2_ref.at[0, pl.ds(0, _group_rows(F2_PIECES)), :]
        for j, chip in enumerate(chips):
            pltpu.make_async_remote_copy(src_ref=three, dst_ref=three, send_sem=fsend.at[j], recv_sem=rsem.at[j],
                                         device_id=(x, y, c), device_id_type=MESH).wait_recv()
            for p in F2_PIECES:
                rows = _block_rows(arrs, p, (*chip, c))
                pltpu.make_async_remote_copy(src_ref=rows, dst_ref=rows, send_sem=fsend.at[j], recv_sem=frecv.at[j],
                                             device_id=sib, device_id_type=MESH).start()

    dma = pltpu.SemaphoreType.DMA
    return pl.pallas_call(
        body, name="gather_ffn2_pass_on", out_shape=(dma((3,)), dma((3,)), _hbm_like(w2), _hbm_like(wo)),
        in_specs=(HBM_SPEC, HBM_SPEC, SEM_SPEC, ANY_SPEC), out_specs=(SEM_SPEC, SEM_SPEC, HBM_SPEC, HBM_SPEC),
        input_output_aliases={0: 2, 1: 3},
        compiler_params=pltpu.CompilerParams(has_side_effects=SPLIT_EFFECT),
    )(w2, wo, rsem_f, after)


def _gather_ffn2_wait(ssem_f, rsem_f0, fsend, frecv, shard, w2, after):
    def body(s_ref, w2_ref, ssem, rsem0, fs, fr, after_ref, w2_o):
        x, y, c = _position()
        grp = _shard_group(s_ref, F2_PIECES)

        def waiter(send_sem, recv_sem):
            return pltpu.make_async_remote_copy(src_ref=grp, dst_ref=grp, send_sem=send_sem, recv_sem=recv_sem,
                                                device_id=(x, y, c), device_id_type=MESH)

        waiter(ssem.at[0], rsem0).wait_recv()
        for j in range(3):
            waiter(fs.at[j], fr.at[j]).wait_recv()
        for rel in range(4):
            waiter(ssem.at[rel], rsem0).wait_send()
        for j in range(3):
            waiter(fs.at[j], fr.at[j]).wait_send()

    return pl.pallas_call(
        body, name="gather_ffn2_wait", out_shape=_hbm_like(w2),
        in_specs=(HBM_SPEC, HBM_SPEC, SEM_SPEC, SEM_SPEC, SEM_SPEC, SEM_SPEC, ANY_SPEC), out_specs=HBM_SPEC,
        input_output_aliases={1: 0},
        compiler_params=pltpu.CompilerParams(has_side_effects=SPLIT_EFFECT),
    )(shard, w2, ssem_f, rsem_f0, fsend, frecv, after)


class _GatheredWeights(_LocalWeights):
    def __init__(self, shard, x, gain1):
        w1, self.h1, wi, wo, w2 = _all_gather_ffn1(shard, x, gain1)
        (self.ssem_m, self.rsem_m0, self.rsem_m, self.ssem_f, self.rsem_f0, self.rsem_f,
         self.shard, self.wi, self.wo, self.w2_part, self.w1) = _gather_rest_start(shard, wi, wo, w2, w1)

    def first_norm(self, x, gain):
        return self.h1

    def after_ffn1(self, gain, x1):
        self.fsend_m, self.frecv_m, self.wi, self.wo, gain = _gather_mix_pass_on(self.rsem_m, self.wi, self.wo, gain, x1)
        return gain

    def mix(self, after):
        self.shard, wint, wout = _gather_mix_wait(self.ssem_m, self.rsem_m0, self.fsend_m, self.frecv_m, self.shard,
                                                  self.wi, self.wo, after)
        return wint, wout

    def before_out_proj(self, wout, after):
        self.fsend, self.frecv, self.w2_part, wout = _gather_ffn2_pass_on(self.rsem_f, self.w2_part, wout, after)
        return wout

    def ffn2(self, after):
        return _gather_ffn2_wait(self.ssem_f, self.rsem_f0, self.fsend, self.frecv, self.shard, self.w2_part, after)

    def out_ffn2_grads_ready(self, dwout, dw2, after):
        rx1 = lax.empty((4, RSA_ROWS, D), bf16)
        sa, ra, sent, rx1, after = _rsa_level1_start(dict(wo=dwout, w2=dw2), rx1, after, "rsa_level1_start_out_ffn2")
        self.level1 = ((sa, ra), sent, rx1)
        return after

    def before_ffn1_bwd(self, dwint, dx1b):
        early, sent, rx1 = self.level1
        sa, ra, late, rx1, dx1b = _rsa_level1_start(dict(wi=dwint), rx1, dx1b, "rsa_level1_start_in")
        started = (((MIX_PIECES[1],) + F2_PIECES, *early), ((MIX_PIECES[0],), sa, ra))
        rx2 = lax.empty((3, RSA_ROWS, D), bf16)
        self.sb, self.rb, self.tx, self.acc, self.rx2, dx1b = _rsa_sums_and_send(
            started, late["wi"], sent["wo"], sent["w2"], rx1, rx2, dx1b)
        return dx1b

    def mix_ffn2_grads_parts(self, after):
        rx2 = _rsa_level2_wait(self.sb, self.rb, self.tx, self.rx2, after)
        return self.acc, rx2


def _reduce_scatter_ffn1_head(dw1, small_packed, first_norm):
    pieces = G1_PIECES
    half = FS // 2
    hrows = len(pieces) * half
    nrows = 2 * hrows
    X_RELAY, Y_RELAY = range(2)

    T = first_norm[0].shape[0]

    def body(d1_ref, p_ref, dh_hbm, x_hbm, gain_ref, dr_hbm,
             forx_ref, fory_ref, own_ref, rx1_ref, relx_ref, rely_ref, gx_hbm, tot_ref,
             own_buf, rx_buf, tx1, tx2, tx3, acc, sa, ra, sb, rb, lsem, pair, chips, small_tot, small_send, small_recv,
             xbuf, rbuf, hbuf, obuf, norm_in_sems, norm_out_sems):
        x, y, c = _position()
        me, sib = (x, y, c), (x, y, 1 - c)
        xn, yn = (1 - x, y, c), (x, 1 - y, c)
        rel_chips = [(x, y), (1 - x, y), (x, 1 - y), (1 - x, 1 - y)]
        srcs = _weight_pieces(w1_ref=d1_ref)

        my_chip = 2 * x + y
        pair[c] = p_ref[...]
        swap = pltpu.make_async_remote_copy(
            src_ref=pair.at[c], dst_ref=pair.at[c], send_sem=small_send.at[0], recv_sem=small_recv.at[0],
            device_id=sib, device_id_type=MESH)
        mine = pl.ds(pl.multiple_of(c * (SMALL_ROWS // 2), 8), SMALL_ROWS // 2)
        small = [pltpu.make_async_remote_copy(
            src_ref=chips.at[my_chip, mine, :], dst_ref=chips.at[my_chip, mine, :], send_sem=small_send.at[j],
            recv_sem=small_recv.at[j], device_id=(*rel_chips[j], c), device_id_type=MESH) for j in (1, 2, 3)]
        give = pltpu.make_async_remote_copy(
            src_ref=small_tot.at[mine, :], dst_ref=small_tot.at[mine, :], send_sem=small_send.at[4],
            recv_sem=small_recv.at[4], device_id=sib, device_id_type=MESH)

        def part(k, dev, hf):
            r = PIECE_ROWS[k]
            return srcs[k].at[pl.ds(pl.multiple_of(_dev_index(*dev) * r + hf * half, 16), half), :]

        def slot(ref, k, hf):
            return ref.at[pl.ds(hf * hrows + k * half, half), :]

        halves = [(k, hf) for hf in (0, 1) for k in pieces]

        for j in (3, 1, 2, 0):
            for k, hf in halves:
                pltpu.make_async_remote_copy(
                    src_ref=part(k, (*rel_chips[j], 1 - c), hf), dst_ref=slot(rx1_ref.at[j], k, hf),
                    send_sem=sa.at[j], recv_sem=ra.at[j], device_id=sib, device_id_type=MESH).start()

        def wait_a(j):
            return pltpu.make_async_remote_copy(src_ref=rx1_ref.at[j], dst_ref=rx1_ref.at[j], send_sem=sa.at[j],
                                                recv_sem=ra.at[j], device_id=me, device_id_type=MESH)

        def ici(rel, src, dst, to):
            return pltpu.make_async_remote_copy(src_ref=src, dst_ref=dst, send_sem=sb.at[rel], recv_sem=rb.at[rel],
                                                device_id=to, device_id_type=MESH)

        first, second = pl.ds(0, hrows), pl.ds(hrows, hrows)
        sends = {
            X_RELAY: ici(X_RELAY, tx3.at[first, :], relx_ref, xn),
            Y_RELAY: ici(Y_RELAY, tx3.at[second, :], rely_ref, yn),
        }

        def chip_sum(j, dst):
            loads = [pltpu.make_async_copy(part(k, (*rel_chips[j], c), hf), slot(own_buf, k, hf), lsem.at[0])
                     for k, hf in halves]
            for cp in loads:
                cp.start()
            wait_a(j).wait_recv()
            got = pltpu.make_async_copy(rx1_ref.at[j], rx_buf, lsem.at[1])
            got.start()
            pltpu.make_async_copy(rx_buf, rx_buf, lsem.at[0]).wait()
            got.wait()

            def add(i, carry):
                rows = pl.ds(pl.multiple_of(i * half, 16), half)
                tot = own_buf[rows, :].astype(f32) + rx_buf[rows, :].astype(f32)
                dst[rows, :] = tot.astype(dst.dtype)
                return carry

            lax.fori_loop(0, nrows // half, add, 0)

        def add_landed(landed, dst, rows0, nrows_):
            got = pltpu.make_async_copy(landed, rx_buf.at[pl.ds(0, nrows_), :], lsem.at[1])
            got.start()
            got.wait()

            def add(i, carry):
                src_rows = pl.ds(pl.multiple_of(i * half, 16), half)
                dst_rows = pl.ds(pl.multiple_of(rows0 + i * half, 16), half)
                dst[dst_rows, :] = (dst[dst_rows, :].astype(f32) + rx_buf[src_rows, :].astype(f32)).astype(dst.dtype)
                return carry

            lax.fori_loop(0, nrows_ // half, add, 0)

        chip_sum(3, tx3)
        sends[X_RELAY].start()
        sends[Y_RELAY].start()
        dg = _norm_bwd_rows(T, dh_hbm, x_hbm, gain_ref, dr_hbm, gx_hbm, None,
                            xbuf, rbuf, hbuf, obuf, None, norm_in_sems, norm_out_sems)
        r0 = SMALL_OFF["ffn1_norm"]
        for k in range(D // 128):
            pair[c, r0 + k:r0 + k + 1, :] = dg[:, 128 * k:128 * (k + 1)]
        swap.start()
        swap.wait_recv()
        chips[my_chip] = pair[0] + pair[1]
        for cp in small:
            cp.start()
        chip_sum(1, tx1)
        chip_sum(2, tx2)
        chip_sum(0, acc)
        own_out = pltpu.make_async_copy(acc, own_ref, lsem.at[0])
        own_out.start()
        sends[X_RELAY].wait_recv()
        add_landed(relx_ref, tx2, 0, hrows)
        sends[Y_RELAY].wait_recv()
        add_landed(rely_ref, tx1, hrows, hrows)
        own_out.wait()
        outs = [pltpu.make_async_copy(tx1, forx_ref, lsem.at[0]), pltpu.make_async_copy(tx2, fory_ref, lsem.at[1])]
        for cp in outs:
            cp.start()
        for cp in outs:
            cp.wait()
        for cp in small:
            cp.wait_recv()
        small_tot[mine, :] = (chips[0, mine, :] + chips[1, mine, :]) + (chips[2, mine, :] + chips[3, mine, :])
        give.start()
        give.wait_recv()
        tot = small_tot[...]
        tot_ref[...] = tot
        loss = jnp.sum(tot[LOSS_ROW:LOSS_ROW + 1, :], axis=-1, keepdims=True)
        tot_ref[LOSS_ROW:LOSS_ROW + 1, :] = jnp.broadcast_to(loss, (1, 128))
        for j in range(4):
            wait_a(j).wait_send()
        for cp in sends.values():
            cp.wait_send()
        swap.wait_send()
        for cp in small + [give]:
            cp.wait_send()

    hbm = pl.BlockSpec(memory_space=pl.ANY)
    vm = pl.BlockSpec(memory_space=pltpu.VMEM)
    outs = pl.pallas_call(
        body, in_specs=[hbm, vm, hbm, hbm, vm, hbm], out_specs=[hbm] * 7 + [vm],
        out_shape=(SDS((nrows, D), bf16), SDS((nrows, D), bf16), SDS((nrows, D), f32), SDS((4, nrows, D), bf16),
                   SDS((hrows, D), bf16), SDS((hrows, D), bf16), SDS((T, D), f32), SDS((SMALL_ROWS, 128), f32)),
        scratch_shapes=[pltpu.VMEM((nrows, D), bf16), pltpu.VMEM((nrows, D), bf16),
                        pltpu.VMEM((nrows, D), bf16), pltpu.VMEM((nrows, D), bf16), pltpu.VMEM((nrows, D), bf16),
                        pltpu.VMEM((nrows, D), f32),
                        pltpu.SemaphoreType.DMA((4,)), pltpu.SemaphoreType.DMA((4,)),
                        pltpu.SemaphoreType.DMA((2,)), pltpu.SemaphoreType.DMA((2,)), pltpu.SemaphoreType.DMA((2,)),
                        pltpu.VMEM((2, SMALL_ROWS, 128), f32), pltpu.VMEM((4, SMALL_ROWS, 128), f32),
                        pltpu.VMEM((SMALL_ROWS, 128), f32),
                        pltpu.SemaphoreType.DMA((5,)), pltpu.SemaphoreType.DMA((5,))]
        + [sc for sc in _norm_bwd_scratch(True, False) if sc is not None],
        compiler_params=pltpu.CompilerParams(has_side_effects=True, vmem_limit_bytes=VMEM_LIMIT_V7X),
        name="reduce_scatter_ffn1_head")(dw1, small_packed, *first_norm)
    return outs[0], outs[1], outs[2], outs[-1], outs[-2]


def _rs1_tail_start(for_x, for_y, from_x, from_y, thru):
    def body(fx_ref, fy_ref, lx_ref, ly_ref, thru_ref, ssem, rsem, fx_o, fy_o, lx_o, ly_o, thru_o):
        x, y, c = _position()
        pltpu.make_async_remote_copy(src_ref=fx_ref, dst_ref=lx_ref, send_sem=ssem.at[0], recv_sem=rsem.at[0],
                                     device_id=(1 - x, y, c), device_id_type=MESH).start()
        pltpu.make_async_remote_copy(src_ref=fy_ref, dst_ref=ly_ref, send_sem=ssem.at[1], recv_sem=rsem.at[1],
                                     device_id=(x, 1 - y, c), device_id_type=MESH).start()

    dma = pltpu.SemaphoreType.DMA
    arrs = (for_x, for_y, from_x, from_y, thru)
    return pl.pallas_call(
        body, name="rs1_tail_start", out_shape=(dma((2,)), dma((2,))) + tuple(_hbm_like(a) for a in arrs),
        in_specs=(HBM_SPEC,) * 5, out_specs=(SEM_SPEC,) * 2 + (HBM_SPEC,) * 5,
        input_output_aliases={0: 2, 1: 3, 2: 4, 3: 5, 4: 6},
        compiler_params=pltpu.CompilerParams(has_side_effects=SPLIT_EFFECT),
    )(*[_in_hbm(a) for a in arrs])


RSA_PIECES = MIX_PIECES + F2_PIECES
RSA_ROWS = _group_rows(RSA_PIECES)
RSA_OFF = {k: PIECE_OFF[k] - PIECE_OFF[RSA_PIECES[0]] for k in RSA_PIECES}
RSA_BLOCK = 192


def _rsa_rows(ref, k):
    return ref.at[pl.ds(RSA_OFF[k], PIECE_ROWS[k]), :]


def _rsa_level1_start(grads, rx1, thru, name):
    keys = sorted(grads)
    n = len(keys)

    def body(*refs):
        srcs = _weight_pieces(**{k + "_ref": ref for k, ref in zip(keys, refs[:n])})
        rx1_ref, sa, ra = refs[n], refs[n + 2], refs[n + 3]
        x, y, c = _position()
        for j, chip in enumerate([(x, y), (1 - x, y), (x, 1 - y), (1 - x, 1 - y)]):
            for k in sorted(srcs):
                pltpu.make_async_remote_copy(
                    src_ref=_block_rows(srcs, k, (*chip, 1 - c)), dst_ref=_rsa_rows(rx1_ref.at[j], k),
                    send_sem=sa.at[j], recv_sem=ra.at[j], device_id=(x, y, 1 - c), device_id_type=MESH).start()

    dma = pltpu.SemaphoreType.DMA
    arrs = tuple(grads[k] for k in keys) + (rx1, thru)
    outs = pl.pallas_call(
        body, name=name, out_shape=(dma((4,)), dma((4,))) + tuple(_hbm_like(a) for a in arrs),
        in_specs=(HBM_SPEC,) * len(arrs), out_specs=(SEM_SPEC,) * 2 + (HBM_SPEC,) * len(arrs),
        input_output_aliases={i: i + 2 for i in range(len(arrs))},
        compiler_params=pltpu.CompilerParams(has_side_effects=SPLIT_EFFECT),
    )(*[_in_hbm(a) for a in arrs])
    return outs[0], outs[1], dict(zip(keys, outs[2:2 + n])), outs[2 + n], outs[3 + n]


def _rsa_sums_and_send(started, dwint, dwout, dw2, rx1, rx2, thru):
    nblk = RSA_ROWS // RSA_BLOCK
    nstart = len(started)

    def body(*refs):
        di_ref, do_ref, d2_ref, rx1_ref, rx2_ref = refs[:5]
        l1_sems = refs[6:6 + 2 * nstart]
        sb, rb, tx_ref, acc_ref = refs[6 + 2 * nstart:10 + 2 * nstart]
        own_buf, rx_buf, tx_buf, acc_buf, in_sems, out_sems = refs[13 + 2 * nstart:]
        x, y, c = _position()
        srcs = _weight_pieces(wi_ref=di_ref, wo_ref=do_ref, w2_ref=d2_ref)
        chips = [(x, y), (1 - x, y), (x, 1 - y), (1 - x, 1 - y)]

        for g, (pieces, _, _) in enumerate(started):
            ssem, rsem = l1_sems[2 * g], l1_sems[2 * g + 1]
            for j in range(4):
                rows = rx1_ref.at[j, pl.ds(RSA_OFF[pieces[0]], _group_rows(pieces)), :]
                d = pltpu.make_async_remote_copy(src_ref=rows, dst_ref=rows, send_sem=ssem.at[j], recv_sem=rsem.at[j],
                                                 device_id=(x, y, c), device_id_type=MESH)
                d.wait_recv()
                d.wait_send()

        def start_loads(j):
            s = j % 2
            for k in RSA_PIECES:
                pltpu.make_async_copy(_block_rows(srcs, k, (*chips[j], c)), _rsa_rows(own_buf.at[s], k),
                                      in_sems.at[2 * s]).start()
            pltpu.make_async_copy(rx1_ref.at[j], rx_buf.at[s], in_sems.at[2 * s + 1]).start()

        def wait_loads(j):
            s = j % 2
            pltpu.make_async_copy(rx1_ref.at[j], own_buf.at[s], in_sems.at[2 * s]).wait()
            pltpu.make_async_copy(rx1_ref.at[j], rx_buf.at[s], in_sems.at[2 * s + 1]).wait()

        def store(j):
            if j == 0:
                return pltpu.make_async_copy(acc_buf, acc_ref, out_sems.at[2])
            return pltpu.make_async_copy(tx_buf.at[j % 2], tx_ref.at[j - 1], out_sems.at[j % 2])

        def send(j):
            return pltpu.make_async_remote_copy(src_ref=tx_ref.at[j - 1], dst_ref=rx2_ref.at[j - 1], send_sem=sb.at[j - 1],
                                                recv_sem=rb.at[j - 1], device_id=(*chips[j], c), device_id_type=MESH)

        start_loads(0)
        for j in range(4):
            s = j % 2
            if j + 1 < 4:
                start_loads(j + 1)
            wait_loads(j)
            if j == 3:
                store(1).wait()
                send(1).start()

            def add(i, carry, j=j, s=s):
                rows = pl.ds(pl.multiple_of(i * RSA_BLOCK, 16), RSA_BLOCK)
                tot = own_buf[s, rows, :].astype(f32) + rx_buf[s, rows, :].astype(f32)
                if j == 0:
                    acc_buf[rows, :] = tot
                else:
                    tx_buf[s, rows, :] = tot.astype(bf16)
                return carry

            lax.fori_loop(0, nblk, add, 0)
            store(j).start()
        store(0).wait()
        for j in (2, 3):
            store(j).wait()
            send(j).start()

    dma = pltpu.SemaphoreType.DMA
    passed = (rx1, rx2, thru)
    outs = pl.pallas_call(
        body, name="rsa_sums_and_send",
        in_specs=(HBM_SPEC,) * 6 + (SEM_SPEC,) * (2 * nstart),
        out_specs=(SEM_SPEC,) * 2 + (HBM_SPEC,) * 5,
        out_shape=(dma((3,)), dma((3,)), pltpu.HBM((3, RSA_ROWS, D), bf16), pltpu.HBM((RSA_ROWS, D), f32))
        + tuple(_hbm_like(a) for a in passed),
        input_output_aliases={3: 4, 4: 5, 5: 6},
        scratch_shapes=[pltpu.VMEM((2, RSA_ROWS, D), bf16), pltpu.VMEM((2, RSA_ROWS, D), bf16),
                        pltpu.VMEM((2, RSA_ROWS, D), bf16), pltpu.VMEM((RSA_ROWS, D), f32),
                        dma((4,)), dma((3,))],
        compiler_params=pltpu.CompilerParams(has_side_effects=SPLIT_EFFECT, vmem_limit_bytes=VMEM_LIMIT_V7X),
    )(*[_in_hbm(a) for a in (dwint, dwout, dw2) + passed], *[sem for _, sa, ra in started for sem in (sa, ra)])
    sb, rb, tx, acc, _, rx2, thru = outs
    return sb, rb, tx, acc, rx2, thru


def _rsa_level2_wait(sb, rb, tx, rx2, after):
    def body(tx_ref, rx2_ref, sb_ref, rb_ref, after_ref, rx2_o):
        x, y, c = _position()
        for j in range(3):
            d = pltpu.make_async_remote_copy(src_ref=tx_ref.at[j], dst_ref=rx2_ref.at[j], send_sem=sb_ref.at[j],
                                             recv_sem=rb_ref.at[j], device_id=(x, y, c), device_id_type=MESH)
            d.wait_recv()
            d.wait_send()

    return pl.pallas_call(
        body, name="rsa_level2_wait", out_shape=_hbm_like(rx2),
        in_specs=(HBM_SPEC, HBM_SPEC, SEM_SPEC, SEM_SPEC, ANY_SPEC), out_specs=HBM_SPEC,
        input_output_aliases={1: 0},
        compiler_params=pltpu.CompilerParams(has_side_effects=SPLIT_EFFECT),
    )(tx, rx2, sb, rb, after)


def _adamw_math(w, g, m, v):
    m = ADAM_B1 * m + (1.0 - ADAM_B1) * g
    v = ADAM_B2 * v + (1.0 - ADAM_B2) * (g * g)
    m_hat = m / (1.0 - ADAM_B1 ** ADAM_STEP)
    v_hat = v / (1.0 - ADAM_B2 ** ADAM_STEP)
    delta = -ADAM_LR * (m_hat / (jnp.sqrt(v_hat) + ADAM_EPS) + ADAM_WD * w)
    return delta, m, v


def _adamw_big(pieces, ws, ms, vs, own, landed, name, in_flight=None):
    npiece = len(pieces)
    nsent = len(landed) if in_flight else 0
    nland = sum(a.shape[0] if a.ndim == 3 else 1 for a in landed)
    rmax = max(PIECE_ROWS[k] for k in pieces)
    half = FS // 2

    def segments(k):
        if k in G1_PIECES:
            return [(hf * len(G1_PIECES) * half + k * half, hf * half, half) for hf in (0, 1)]
        return [(RSA_OFF[k], 0, PIECE_ROWS[k])]

    def body(*refs):
        ins = (refs[0:npiece], refs[npiece:2 * npiece], refs[2 * npiece:3 * npiece])
        own_ref = refs[3 * npiece]
        nin = 3 * npiece + 1 + len(landed)
        land_refs = []
        for ref, a in zip(refs[3 * npiece + 1:nin], landed):
            land_refs += [ref.at[j] for j in range(a.shape[0])] if a.ndim == 3 else [ref]
        if in_flight:
            sent_refs, (ssem, rsem) = refs[nin:nin + nsent], refs[nin + nsent:nin + nsent + 2]
            nin += nsent + 3
        out_refs = refs[nin:nin + 4 * npiece]
        inb, landb, outb, in_sems, land_sems, out_sems = refs[nin + 4 * npiece + nsent:]

        def loads(i):
            s, k = i % 2, pieces[i]
            r = PIECE_ROWS[k]
            cps = [pltpu.make_async_copy(ins[q][i].at[0], inb.at[s, q, pl.ds(0, r), :], in_sems.at[4 * s + q])
                   for q in range(3)]
            waits, late = list(cps), []
            for src0, dst0, n in segments(k):
                cps.append(pltpu.make_async_copy(own_ref.at[pl.ds(src0, n), :], inb.at[s, 3, pl.ds(dst0, n), :],
                                                 in_sems.at[4 * s + 3]))
                for p in range(nland):
                    late.append(pltpu.make_async_copy(land_refs[p].at[pl.ds(src0, n), :],
                                                      landb.at[s, p, pl.ds(dst0, n), :], land_sems.at[nland * s + p]))
            own_rows = inb.at[s, 3, pl.ds(0, r), :]
            waits.append(pltpu.make_async_copy(own_rows, own_rows, in_sems.at[4 * s + 3]))
            for p in range(nland):
                rows = landb.at[s, p, pl.ds(0, r), :]
                waits.append(pltpu.make_async_copy(rows, rows, land_sems.at[nland * s + p]))
            return cps, late, waits

        def stores(i):
            s, r = i % 2, PIECE_ROWS[pieces[i]]
            return [pltpu.make_async_copy(outb.at[s, q, pl.ds(0, r), :], out_refs[q * npiece + i].at[0],
                                          out_sems.at[4 * s + q]) for q in range(4)]

        ahead = min(2, npiece) if in_flight else 1
        for i in range(ahead):
            for cp in loads(i)[0]:
                cp.start()
        if in_flight:
            x, y, c = _position()
            for j in range(nsent):
                d = pltpu.make_async_remote_copy(src_ref=sent_refs[j], dst_ref=refs[3 * npiece + 1 + j],
                                                 send_sem=ssem.at[j], recv_sem=rsem.at[j], device_id=(x, y, c),
                                                 device_id_type=MESH)
                d.wait_recv()
                d.wait_send()
        for cp in loads(0)[1]:
            cp.start()
        for i in range(npiece):
            s, r = i % 2, PIECE_ROWS[pieces[i]]
            if i + 1 < npiece:
                first, late, _ = loads(i + 1)
                for cp in late if i + 1 < ahead else first + late:
                    cp.start()
            for cp in loads(i)[2]:
                cp.wait()
            if i >= 2:
                for cp in stores(i - 2):
                    cp.wait()
            g = inb[s, 3, 0:r, :]
            for p in range(nland):
                g = g + landb[s, p, 0:r, :].astype(f32)
            d, nm, nv = _adamw_math(inb[s, 0, 0:r, :], g, inb[s, 1, 0:r, :], inb[s, 2, 0:r, :])
            outb[s, 0, 0:r, :] = g
            outb[s, 1, 0:r, :] = d
            outb[s, 2, 0:r, :] = nm
            outb[s, 3, 0:r, :] = nv
            for cp in stores(i):
                cp.start()
        for i in range(max(npiece - 2, 0), npiece):
            for cp in stores(i):
                cp.wait()

    hbm = pl.BlockSpec(memory_space=pl.ANY)
    in_specs, out_specs = [hbm] * (3 * npiece + 1), [hbm] * (4 * npiece)
    out_shape = [SDS(w.shape, f32) for _ in range(4) for w in ws]
    args, aliases, effect = [*ws, *ms, *vs, own], {}, False
    if in_flight:
        ssem, rsem, sent, after = in_flight
        in_specs += [HBM_SPEC] * (2 * nsent) + [SEM_SPEC, SEM_SPEC, hbm]
        args += [_in_hbm(a) for a in (*landed, *sent)] + [ssem, rsem, after]
        out_specs += [HBM_SPEC] * nsent
        out_shape += [_hbm_like(a) for a in landed]
        aliases = {3 * npiece + 1 + j: 4 * npiece + j for j in range(nsent)}
        effect = SPLIT_EFFECT
    else:
        in_specs += [hbm] * len(landed)
        args += list(landed)
    outs = pl.pallas_call(
        body, in_specs=in_specs, out_specs=out_specs, out_shape=tuple(out_shape), input_output_aliases=aliases,
        scratch_shapes=[pltpu.VMEM((2, 4, rmax, D), f32), pltpu.VMEM((2, nland, rmax, D), bf16),
                        pltpu.VMEM((2, 4, rmax, D), f32),
                        pltpu.SemaphoreType.DMA((8,)), pltpu.SemaphoreType.DMA((2 * nland,)),
                        pltpu.SemaphoreType.DMA((8,))],
        compiler_params=pltpu.CompilerParams(has_side_effects=effect, vmem_limit_bytes=VMEM_LIMIT_V7X),
        name=name)(*args)
    return [list(outs[q * npiece:(q + 1) * npiece]) for q in range(4)]


def _adamw_small(ws, ms, vs, gs, name):
    n = len(ws)

    def body(*refs):
        w_refs, m_refs, v_refs, g_refs = refs[0:n], refs[n:2 * n], refs[2 * n:3 * n], refs[3 * n:4 * n]
        outs = refs[4 * n:]
        for i in range(n):
            d, nm, nv = _adamw_math(w_refs[i][...], g_refs[i][...], m_refs[i][...], v_refs[i][...])
            outs[i][...] = d
            outs[n + i][...] = nm
            outs[2 * n + i][...] = nv

    outs = pl.pallas_call(
        body, out_shape=tuple(SDS(w.shape, f32) for _ in range(3) for w in ws), name=name)(*ws, *ms, *vs, *gs)
    return [list(outs[q * n:(q + 1) * n]) for q in range(3)]


WEIGHTS = ("ffn1_norm", "ffn1_w_gate", "ffn1_w_up", "ffn1_w_down", "mix_norm", "w_in", "q_norm", "k_norm",
           "attn_sinks", "rel_bias", "pool_w", "pool_scale", "w_out", "ffn2_norm", "ffn2_w_gate", "ffn2_w_up",
           "ffn2_w_down")
BIG = (("ffn1_w_gate", True), ("ffn1_w_up", True), ("ffn1_w_down", False), ("w_in", True), ("w_out", False),
       ("ffn2_w_gate", True), ("ffn2_w_up", True), ("ffn2_w_down", False))


def kernel(x, ffn1_norm, ffn1_w_gate, ffn1_w_up, ffn1_w_down, mix_norm, w_in, q_norm, k_norm, attn_sinks, rel_bias, pool_w, pool_scale, w_out, ffn2_norm, ffn2_w_gate, ffn2_w_up, ffn2_w_down, loss_target, m_ffn1_norm, m_ffn1_w_gate, m_ffn1_w_up, m_ffn1_w_down, m_mix_norm, m_w_in, m_q_norm, m_k_norm, m_attn_sinks, m_rel_bias, m_pool_w, m_pool_scale, m_w_out, m_ffn2_norm, m_ffn2_w_gate, m_ffn2_w_up, m_ffn2_w_down, v_ffn1_norm, v_ffn1_w_gate, v_ffn1_w_up, v_ffn1_w_down, v_mix_norm, v_w_in, v_q_norm, v_k_norm, v_attn_sinks, v_rel_bias, v_pool_w, v_pool_scale, v_w_out, v_ffn2_norm, v_ffn2_w_gate, v_ffn2_w_up, v_ffn2_w_down):
    args = dict(locals())
    w = {n: args[n] for n in WEIGHTS}
    m = {n: args["m_" + n] for n in WEIGHTS}
    v = {n: args["v_" + n] for n in WEIGHTS}

    as_rows = lambda a, tr: jnp.swapaxes(a, 1, 2) if tr else a
    shard = jnp.concatenate([as_rows(w[n], tr)[0].astype(bf16) for n, tr in BIG], axis=0)
    exchanges = _GatheredWeights(shard, x[0], ffn1_norm)
    (dh1, dx1), (dw1, _, _, _), small = _local_step(
        x[0], loss_target[0], exchanges, ffn1_norm, mix_norm, ffn2_norm, q_norm, k_norm, attn_sinks,
        rel_bias, pool_w[0], pool_scale)

    nrows1 = len(G1_PIECES) * FS
    for_x, for_y, own1, small_tot, gx = _reduce_scatter_ffn1_head(dw1, _pack_small(small),
                                                                  (dh1, x[0], ffn1_norm, dx1))
    ssem, rsem, for_x, for_y, from_x, from_y, small_tot = _rs1_tail_start(
        for_x, for_y, lax.empty((nrows1, D), bf16), lax.empty((nrows1, D), bf16), small_tot)
    own_rest, landed_rest = exchanges.mix_ffn2_grads_parts(small_tot)

    grads, deltas, new_m, new_v = {}, {}, {}, {}
    rest = [k for k in range(len(BIG)) if k not in G1_PIECES]
    rows_of = lambda t, ks: [as_rows(t[BIG[k][0]], BIG[k][1]) for k in ks]
    rest_out = _adamw_big(rest, rows_of(w, rest), rows_of(m, rest), rows_of(v, rest), own_rest, [landed_rest],
                          "adamw_rest")
    ffn1 = list(G1_PIECES)
    ffn1_out = _adamw_big(ffn1, rows_of(w, ffn1), rows_of(m, ffn1), rows_of(v, ffn1), own1, [from_x, from_y],
                          "adamw_ffn1", in_flight=(ssem, rsem, [for_x, for_y], rest_out[0][0]))
    for ks, out in ((rest, rest_out), (ffn1, ffn1_out)):
        for i, k in enumerate(ks):
            n, tr = BIG[k]
            grads[n], deltas[n], new_m[n], new_v[n] = [as_rows(o[i], tr) for o in out]
    small_names = [n for n in SMALL_NAMES if n != "loss"]
    for n in small_names:
        grads[n] = _unpack_small(small_tot, n)
    ds, nms, nvs = _adamw_small([w[n] for n in small_names], [m[n] for n in small_names], [v[n] for n in small_names],
                                [grads[n] for n in small_names], "adamw_small")
    for i, n in enumerate(small_names):
        deltas[n], new_m[n], new_v[n] = ds[i], nms[i], nvs[i]
    loss = small_tot[LOSS_ROW, 0]
    return (loss, gx[None], *[grads[n] for n in WEIGHTS], *[deltas[n] for n in WEIGHTS],
            *[new_m[n] for n in WEIGHTS], *[new_v[n] for n in WEIGHTS])
```

```python
import jax
import jax.numpy as jnp
import numpy as np
from jax import lax
from jax.experimental import pallas as pl
from jax.experimental.pallas import tpu as pltpu

f32, bf16, i32 = jnp.float32, jnp.bfloat16, jnp.int32
SDS = jax.ShapeDtypeStruct

D = 1024
F = 2816
HD = 64
NH = 8
NKV = 2
GQA = NH // NKV
DATTN = NH * HD
DKV = NKV * HD
DPOOL = 512
POOL_WINDOWS = (2, 4, 8, 16)
PGD = DPOOL // len(POOL_WINDOWS)
DIN = DATTN + 2 * DKV + DPOOL
DMIX = DATTN + DPOOL
BLK = 128
NBUCK = 32
MAX_DISTANCE = 128
EPS = 1e-6
NEG = -1e30
SCALE = HD ** -0.5

ADAM_LR, ADAM_B1, ADAM_B2, ADAM_EPS, ADAM_WD, ADAM_STEP = 0.001, 0.9, 0.999, 1e-08, 0.01, 10

NDEV = 8
FS = F // NDEV
INS = DIN // NDEV
OUTS = DMIX // NDEV
PIECE_ROWS = (FS, FS, FS, INS, OUTS, FS, FS, FS)
PIECE_OFF = tuple(int(v) for v in np.cumsum((0,) + PIECE_ROWS[:-1]))
PACK_ROWS = sum(PIECE_ROWS)

VMEM_LIMIT_V7X = 56 * 1024 * 1024

MESH = pl.DeviceIdType.MESH


def _cparams(sem=None, vmem=None):
    return pltpu.CompilerParams(dimension_semantics=sem, vmem_limit_bytes=vmem)


def _nt(a, b):
    return lax.dot_general(a, b, (((1,), (1,)), ((), ())), preferred_element_type=f32)


def _tn(a, b):
    return lax.dot_general(a, b, (((0,), (0,)), ((), ())), preferred_element_type=f32)


def _nn(a, b):
    return jnp.dot(a, b, preferred_element_type=f32)


def _sigmoid(x):
    return 1.0 / (1.0 + jnp.exp(-x))


def _norm_fwd(x, g, name):
    T = x.shape[0]
    tm = min(512, T)

    def body(x_ref, g_ref, h_ref):
        xv = x_ref[...]
        r = lax.rsqrt(jnp.mean(xv * xv, axis=-1, keepdims=True) + EPS)
        h_ref[...] = (xv * r * g_ref[...]).astype(bf16)

    return pl.pallas_call(
        body, grid=(T // tm,),
        in_specs=[pl.BlockSpec((tm, D), lambda i: (i, 0)), pl.BlockSpec((1, D), lambda i: (0, 0))],
        out_specs=pl.BlockSpec((tm, D), lambda i: (i, 0)),
        out_shape=SDS((T, D), bf16), name=name)(x, g)


FFN_ROW_CHUNK = 256


def _ffn_tiles(T):
    return min(1024, T), 256


def _ffn_fwd(h, w, x, target, next_gain, name):
    T = h.shape[0]
    tm, tf = _ffn_tiles(T)
    nf = F // tf
    with_loss = target is not None
    assert with_loss != (next_gain is not None)

    def body(*refs):
        if with_loss:
            h_ref, w_ref, x_hbm, t_hbm, xo_ref, g_ref, u_ref, dyb_ref, loss_ref, tbuf, sem = refs
        else:
            h_ref, w_ref, x_hbm, gain_ref, xo_ref, g_ref, u_ref, hn_ref, sem = refs
        fi = pl.program_id(0)

        @pl.when(fi == 0)
        def _():
            cp = pltpu.make_async_copy(x_hbm, xo_ref, sem)
            cp.start()
            cp.wait()

        wgu = w_ref[0:2].reshape(2 * tf, D)
        for r in range(0, T, tm):
            rows = slice(r, r + tm)
            gu = _nt(h_ref[rows, :], wgu)
            gate, up = gu[:, :tf], gu[:, tf:]
            act = gate * _sigmoid(gate) * up
            g_ref[0, rows, :] = gate.astype(bf16)
            u_ref[0, rows, :] = up.astype(bf16)
            xo_ref[rows, :] += _nn((0.5 * act).astype(bf16), w_ref[2])

        if with_loss:
            @pl.when(fi == nf - 1)
            def _():
                lanes = jnp.zeros((1, 128), f32)
                for r in range(0, T, tm):
                    rows = slice(r, r + tm)
                    cp = pltpu.make_async_copy(t_hbm.at[pl.ds(r, tm), :], tbuf, sem)
                    cp.start()
                    cp.wait()
                    e = xo_ref[rows, :] - tbuf[...]
                    dy = e * (1.0 / D)
                    xo_ref[rows, :] = dy
                    dyb_ref[rows, :] = (0.5 * dy).astype(bf16)
                    col = jnp.sum(e * e, axis=0, keepdims=True) * (0.5 / D)
                    for k in range(D // 128):
                        lanes = lanes + col[:, 128 * k:128 * (k + 1)]
                loss_ref[...] = lanes
        else:
            @pl.when(fi == nf - 1)
            def _():
                for r in range(0, T, FFN_ROW_CHUNK):
                    rows = slice(r, r + FFN_ROW_CHUNK)
                    xv = xo_ref[rows, :]
                    rstd = lax.rsqrt(jnp.mean(xv * xv, axis=-1, keepdims=True) + EPS)
                    hn_ref[rows, :] = (xv * rstd * gain_ref[...]).astype(bf16)

    tok = pl.BlockSpec((T, D), lambda f: (0, 0))
    act_spec = pl.BlockSpec((1, T, tf), lambda f: (f, 0, 0))
    hbm = pl.BlockSpec(memory_space=pl.ANY)
    in_specs = [tok, pl.BlockSpec((3, tf, D), lambda f: (0, f, 0)), hbm]
    out_specs = [tok, act_spec, act_spec]
    out_shape = [SDS((T, D), f32), SDS((nf, T, tf), bf16), SDS((nf, T, tf), bf16)]
    scratch = [pltpu.SemaphoreType.DMA]
    args = [h, w, x]
    if with_loss:
        in_specs.append(hbm)
        args.append(target)
        out_specs += [tok, pl.BlockSpec((1, 128), lambda f: (0, 0))]
        out_shape += [SDS((T, D), bf16), SDS((1, 128), f32)]
        scratch = [pltpu.VMEM((tm, D), f32)] + scratch
    else:
        in_specs.append(pl.BlockSpec((1, D), lambda f: (0, 0)))
        args.append(next_gain)
        out_specs.append(tok)
        out_shape.append(SDS((T, D), bf16))
    return pl.pallas_call(
        body, grid=(nf,), in_specs=in_specs, out_specs=out_specs, out_shape=tuple(out_shape), scratch_shapes=scratch,
        compiler_params=_cparams(("arbitrary",), VMEM_LIMIT_V7X), name=name)(*args)


NORM_BWD_ROWS = 512


def _norm_bwd_scratch(dh_in_hbm, with_bf16):
    buf = lambda dt: pltpu.VMEM((2, NORM_BWD_ROWS, D), dt)
    return [buf(f32), buf(f32), buf(f32) if dh_in_hbm else None, buf(f32), buf(bf16) if with_bf16 else None,
            pltpu.SemaphoreType.DMA((6,)), pltpu.SemaphoreType.DMA((4,))]


def _norm_bwd_rows(T, dh_src, x_hbm, gain_ref, dr_hbm, dx_hbm, dxb_hbm, xbuf, rbuf, hbuf, obuf, obb, in_sems, out_sems):
    tm = min(NORM_BWD_ROWS, T)
    nchunk = T // tm

    def loads(i):
        s, rows = i % 2, pl.ds(i * tm, tm)
        cps = [pltpu.make_async_copy(x_hbm.at[rows, :], xbuf.at[s, pl.ds(0, tm), :], in_sems.at[3 * s]),
               pltpu.make_async_copy(dr_hbm.at[rows, :], rbuf.at[s, pl.ds(0, tm), :], in_sems.at[3 * s + 1])]
        if hbuf is not None:
            cps.append(pltpu.make_async_copy(dh_src.at[rows, :], hbuf.at[s, pl.ds(0, tm), :], in_sems.at[3 * s + 2]))
        return cps

    def stores(i):
        s, rows = i % 2, pl.ds(i * tm, tm)
        cps = [pltpu.make_async_copy(obuf.at[s, pl.ds(0, tm), :], dx_hbm.at[rows, :], out_sems.at[2 * s])]
        if dxb_hbm is not None:
            cps.append(pltpu.make_async_copy(obb.at[s, pl.ds(0, tm), :], dxb_hbm.at[rows, :], out_sems.at[2 * s + 1]))
        return cps

    for cp in loads(0):
        cp.start()
    dg = jnp.zeros((1, D), f32)
    for i in range(nchunk):
        s = i % 2
        if i + 1 < nchunk:
            for cp in loads(i + 1):
                cp.start()
        for cp in loads(i):
            cp.wait()
        if i >= 2:
            for cp in stores(i - 2):
                cp.wait()
        xv = xbuf[s, 0:tm, :]
        rstd = lax.rsqrt(jnp.mean(xv * xv, axis=-1, keepdims=True) + EPS)
        xh = xv * rstd
        dhv = hbuf[s, 0:tm, :] if hbuf is not None else dh_src[i * tm:(i + 1) * tm, :]
        dxh = dhv * gain_ref[...]
        dx = rbuf[s, 0:tm, :] + rstd * (dxh - xh * jnp.mean(dxh * xh, axis=-1, keepdims=True))
        obuf[s, 0:tm, :] = dx
        if dxb_hbm is not None:
            obb[s, 0:tm, :] = dx.astype(bf16)
        dg = dg + jnp.sum(dhv * xh, axis=0, keepdims=True)
        for cp in stores(i):
            cp.start()
    for i in range(max(nchunk - 2, 0), nchunk):
        for cp in stores(i):
            cp.wait()
    return dg


def _ffn_bwd(dob, h, gate, up, w, norm, name):
    T = h.shape[0]
    _, tf = _ffn_tiles(T)
    nf = F // tf
    nin = 5 if norm is None else 8
    nout = 2 if norm is None else 4

    def body(*refs):
        do_hbm, h_hbm, g_ref, u_ref, w_ref = refs[:5]
        dw_ref = refs[nin + nout - 1]
        do_v, h_v, dh_acc, dgu_s, act_s, sems = refs[nin + nout:nin + nout + 6]
        fi = pl.program_id(0)

        @pl.when(fi == 0)
        def _():
            loads = [pltpu.make_async_copy(do_hbm, do_v, sems.at[0]), pltpu.make_async_copy(h_hbm, h_v, sems.at[1])]
            for cp in loads:
                cp.start()
            dh_acc[...] = jnp.zeros_like(dh_acc)
            for cp in loads:
                cp.wait()

        wgu = w_ref[0:2].reshape(2 * tf, D)
        for r in range(0, T, FFN_ROW_CHUNK):
            rows = slice(r, r + FFN_ROW_CHUNK)
            dov = do_v[rows, :]
            gv = g_ref[0, rows, :].astype(f32)
            uv = u_ref[0, rows, :].astype(f32)
            sg = _sigmoid(gv)
            sil = gv * sg
            dact = _nt(dov, w_ref[2])
            dup = dact * sil
            dgate = dact * uv * (sg * (1.0 + gv * (1.0 - sg)))
            dgu = jnp.concatenate([dgate.astype(bf16), dup.astype(bf16)], axis=1)
            dgu_s[rows, :] = dgu
            act_s[rows, :] = (sil * uv).astype(bf16)
            dh_acc[rows, :] += _nn(dgu, wgu)
        dw_ref[0:2] = _tn(dgu_s[...], h_v[...]).reshape(2, tf, D).astype(bf16)
        dw_ref[2] = _tn(act_s[...], do_v[...]).astype(bf16)

        @pl.when(fi == nf - 1)
        def _():
            if norm is None:
                out = pltpu.make_async_copy(dh_acc, refs[nin], sems.at[0])
                out.start()
                out.wait()
            else:
                x_hbm, gain_ref, dr_hbm, dx_hbm, dxb_hbm, dg_ref = refs[5:11]
                xbuf, rbuf, obuf, obb, in_sems, out_sems = refs[nin + nout + 6:]
                dg_ref[...] = _norm_bwd_rows(T, dh_acc, x_hbm, gain_ref, dr_hbm, dx_hbm, dxb_hbm,
                                             xbuf, rbuf, None, obuf, obb, in_sems, out_sems)

    act_spec = pl.BlockSpec((1, T, tf), lambda f: (f, 0, 0))
    wspec = pl.BlockSpec((3, tf, D), lambda f: (0, f, 0))
    vec = pl.BlockSpec((1, D), lambda f: (0, 0))
    hbm = pl.BlockSpec(memory_space=pl.ANY)
    in_specs, out_specs = [hbm, hbm, act_spec, act_spec, wspec], [hbm, wspec]
    out_shape, args = [SDS((T, D), f32), SDS((3, F, D), bf16)], [dob, h, gate, up, w]
    scratch = [pltpu.VMEM((T, D), bf16), pltpu.VMEM((T, D), bf16), pltpu.VMEM((T, D), f32),
               pltpu.VMEM((T, 2 * tf), bf16), pltpu.VMEM((T, tf), bf16), pltpu.SemaphoreType.DMA((2,))]
    if norm is not None:
        in_specs += [hbm, vec, hbm]
        out_specs = [hbm, hbm, vec, wspec]
        out_shape = [SDS((T, D), f32), SDS((T, D), bf16), SDS((1, D), f32), SDS((3, F, D), bf16)]
        args += list(norm)
        scratch += [sc for sc in _norm_bwd_scratch(False, True) if sc is not None]
    return pl.pallas_call(
        body, grid=(nf,), in_specs=in_specs, out_specs=out_specs, out_shape=tuple(out_shape), scratch_shapes=scratch,
        compiler_params=_cparams(("arbitrary",), VMEM_LIMIT_V7X), name=name)(*args)


def _in_proj_fwd(h, wint, name):
    T = h.shape[0]
    tm = min(512, T)

    def body(h_ref, w_ref, z_ref):
        z_ref[...] = _nt(h_ref[...], w_ref[...])

    return pl.pallas_call(
        body, grid=(T // tm,),
        in_specs=[pl.BlockSpec((tm, D), lambda i: (i, 0)), pl.BlockSpec((DIN, D), lambda i: (0, 0))],
        out_specs=pl.BlockSpec((tm, DIN), lambda i: (i, 0)),
        out_shape=SDS((T, DIN), f32), name=name)(h, wint)


def _in_proj_bwd(dz, wint, h, norm, out_scale, name):
    x, g, dres = norm
    T = h.shape[0]
    tm = min(512, T)
    nt = T // tm

    def body(dz_ref, w_ref, h_ref, x_ref, g_ref, dr_ref, dx_ref, dxb_ref, dg_ref, dw_ref, acc):
        i = pl.program_id(0)
        dzb = dz_ref[...].astype(bf16)
        dhv = _nn(dzb, w_ref[...])
        part = _tn(dzb, h_ref[...])
        xv = x_ref[...]
        rstd = lax.rsqrt(jnp.mean(xv * xv, axis=-1, keepdims=True) + EPS)
        xh = xv * rstd
        dxh = dhv * g_ref[...]
        dx = dr_ref[...] + rstd * (dxh - xh * jnp.mean(dxh * xh, axis=-1, keepdims=True))
        dx_ref[...] = dx
        dxb_ref[...] = (out_scale * dx).astype(bf16)
        dg = jnp.sum(dhv * xh, axis=0, keepdims=True)

        @pl.when(i == 0)
        def _():
            acc[...] = part
            dg_ref[...] = dg

        @pl.when(i > 0)
        def _():
            acc[...] += part
            dg_ref[...] += dg

        @pl.when(i == nt - 1)
        def _():
            dw_ref[...] = acc[...].astype(bf16)

    wspec = pl.BlockSpec((DIN, D), lambda i: (0, 0))
    tok = pl.BlockSpec((tm, D), lambda i: (i, 0))
    vec = pl.BlockSpec((1, D), lambda i: (0, 0))
    return pl.pallas_call(
        body, grid=(nt,),
        in_specs=[pl.BlockSpec((tm, DIN), lambda i: (i, 0)), wspec, tok, tok, vec, tok],
        out_specs=[tok, tok, vec, wspec],
        out_shape=(SDS((T, D), f32), SDS((T, D), bf16), SDS((1, D), f32), SDS((DIN, D), bf16)),
        scratch_shapes=[pltpu.VMEM((DIN, D), f32)],
        compiler_params=_cparams(("arbitrary",)), name=name)(dz, wint, h, x, g, dres)


def _out_proj_fwd(ymix, wout, x, g, name):
    T = x.shape[0]
    tm = min(512, T)

    def body(y_ref, w_ref, x_ref, g_ref, o_ref, h_ref):
        o = x_ref[...] + _nn(y_ref[...], w_ref[...])
        o_ref[...] = o
        r = lax.rsqrt(jnp.mean(o * o, axis=-1, keepdims=True) + EPS)
        h_ref[...] = (o * r * g_ref[...]).astype(bf16)

    tok = pl.BlockSpec((tm, D), lambda i: (i, 0))
    return pl.pallas_call(
        body, grid=(T // tm,),
        in_specs=[pl.BlockSpec((tm, DMIX), lambda i: (i, 0)), pl.BlockSpec((DMIX, D), lambda i: (0, 0)), tok,
                  pl.BlockSpec((1, D), lambda i: (0, 0))],
        out_specs=[tok, tok], out_shape=(SDS((T, D), f32), SDS((T, D), bf16)), name=name)(ymix, wout, x, g)


def _out_proj_bwd(dxb, wout, ymix, name):
    T = dxb.shape[0]
    tm = min(512, T)
    nt = T // tm

    def body(dx_ref, w_ref, y_ref, dy_ref, dw_ref, acc):
        i = pl.program_id(0)
        dxv = dx_ref[...]
        dy_ref[...] = _nt(dxv, w_ref[...])
        part = _tn(y_ref[...], dxv)

        @pl.when(i == 0)
        def _():
            acc[...] = part

        @pl.when(i > 0)
        def _():
            acc[...] += part

        @pl.when(i == nt - 1)
        def _():
            dw_ref[...] = acc[...].astype(bf16)

    wspec = pl.BlockSpec((DMIX, D), lambda i: (0, 0))
    return pl.pallas_call(
        body, grid=(nt,),
        in_specs=[pl.BlockSpec((tm, D), lambda i: (i, 0)), wspec, pl.BlockSpec((tm, DMIX), lambda i: (i, 0))],
        out_specs=[pl.BlockSpec((tm, DMIX), lambda i: (i, 0)), wspec],
        out_shape=(SDS((T, DMIX), f32), SDS((DMIX, D), bf16)),
        scratch_shapes=[pltpu.VMEM((DMIX, D), f32)],
        compiler_params=_cparams(("arbitrary",)), name=name)(dxb, wout, ymix)


def _t5_bucket_table():
    ql = np.arange(BLK)[:, None]
    kl = np.arange(2 * BLK)[None, :]
    n = np.maximum(ql + BLK - kl, 0)
    max_exact = NBUCK // 2
    large = max_exact + (np.log(np.maximum(n, 1) / max_exact) / np.log(MAX_DISTANCE / max_exact)
                         * (NBUCK - max_exact)).astype(np.int32)
    large = np.minimum(large, NBUCK - 1)
    return np.where(n < max_exact, n, large).astype(np.int32)


def _fill_bias(bk_ref, rb_ref, bias_scr):
    bk = bk_ref[...]
    for h in range(NH):
        def step(b, acc, h=h):
            return acc + jnp.where(bk == b, rb_ref[b, h], 0.0)
        bias_scr[h] = lax.fori_loop(0, NBUCK, step, jnp.zeros((BLK, 2 * BLK), f32))


MIX_SUB = 4


class _Window:
    def __init__(self, zc_ref, zp_ref, n, s):
        self.blk = n * MIX_SUB + s
        self.cur = lambda a, b: zc_ref[s * BLK:(s + 1) * BLK, a:b]
        self.prev = (lambda a, b: zp_ref[:, a:b]) if s == 0 else (lambda a, b: zc_ref[(s - 1) * BLK:s * BLK, a:b])


def _attn_qkv(win, kh, qg, kg):
    kc = DATTN + HD * kh
    vc = DATTN + DKV + HD * kh
    kx = jnp.concatenate([win.prev(kc, kc + HD), win.cur(kc, kc + HD)], axis=0)
    vx = jnp.concatenate([win.prev(vc, vc + HD), win.cur(vc, vc + HD)], axis=0)
    qx = jnp.concatenate([win.cur(HD * (GQA * kh + g), HD * (GQA * kh + g + 1)) for g in range(GQA)], axis=0)
    rq = lax.rsqrt(jnp.mean(qx * qx, axis=-1, keepdims=True) + EPS)
    rk = lax.rsqrt(jnp.mean(kx * kx, axis=-1, keepdims=True) + EPS)
    qhat, khat = qx * rq, kx * rk
    return dict(qhat=qhat, khat=khat, rq=rq, rk=rk, qsb=(qhat * (qg * SCALE)).astype(bf16),
                knb=(khat * kg).astype(bf16), vb=vx.astype(bf16))


def _window_masks(n):
    row = lax.broadcasted_iota(i32, (GQA * BLK, 2 * BLK), 0) & (BLK - 1)
    col = lax.broadcasted_iota(i32, (GQA * BLK, 2 * BLK), 1)
    band = (col > row) & (col <= row + BLK)
    return band & ((col >= BLK) | (n > 0)), band


def _attn_probs(a, kh, sk_ref, bias_scr, mask):
    s = _nt(a["qsb"], a["knb"]) + bias_scr[GQA * kh:GQA * (kh + 1)].reshape(GQA * BLK, 2 * BLK)
    s = jnp.where(mask, s, NEG)
    ridx = lax.broadcasted_iota(i32, (GQA * BLK, 1), 0)
    sink = jnp.full((GQA * BLK, 1), sk_ref[GQA * kh + GQA - 1], f32)
    for g in range(GQA - 2, -1, -1):
        sink = jnp.where(ridx < (g + 1) * BLK, sk_ref[GQA * kh + g], sink)
    m = jnp.maximum(jnp.max(s, axis=-1, keepdims=True), sink)
    e = jnp.exp(s - m)
    den = jnp.sum(e, axis=-1, keepdims=True) + jnp.exp(sink - m)
    return e / den


POOL_STEPS = {2: (1,), 4: (1, 2), 8: (1, 2, 4), 16: (1, 2, 4, 8)}


def _pool_group(win, g, w):
    n = win.blk
    c0 = DATTN + 2 * DKV + PGD * g
    uc = win.cur(c0, c0 + PGD)
    up = jnp.where(n > 0, win.prev(c0, c0 + PGD), 0.0)
    sm = jnp.concatenate([up, uc], axis=0)
    for k in POOL_STEPS[w]:
        sm = sm + pltpu.roll(sm, k, axis=0)
    pos = n * BLK + lax.broadcasted_iota(i32, (BLK, 1), 0) + 1
    cnt = jnp.minimum(pos, w).astype(f32)
    return sm[BLK:2 * BLK] / cnt - uc, cnt


def _mix_fwd(z, qg, kg, sinks, relb, bucket, pool_w, pscale, name):
    T = z.shape[0]
    step_rows = MIX_SUB * BLK
    nsteps = T // step_rows

    def body(zc_ref, zp_ref, qg_ref, kg_ref, sk_ref, rb_ref, bk_ref, pw_ref, ps_ref, y_ref, p_ref, bias_scr, yacc):
        n = pl.program_id(0)

        @pl.when(n == 0)
        def _():
            _fill_bias(bk_ref, rb_ref, bias_scr)

        first_mask, mask = _window_masks(n)
        for s in range(MIX_SUB):
            win = _Window(zc_ref, zp_ref, n, s)
            rows = slice(s * BLK, (s + 1) * BLK)
            for kh in range(NKV):
                a = _attn_qkv(win, kh, qg_ref[...], kg_ref[...])
                pb = _attn_probs(a, kh, sk_ref, bias_scr, first_mask if s == 0 else mask).astype(bf16)
                p_ref[s, GQA * kh:GQA * (kh + 1)] = pb.reshape(GQA, BLK, 2 * BLK)
                o = _nn(pb, a["vb"])
                for g in range(GQA):
                    hc = HD * (GQA * kh + g)
                    yacc[rows, hc:hc + HD] = o[g * BLK:(g + 1) * BLK]
            for g, w in enumerate(POOL_WINDOWS):
                pooled, _ = _pool_group(win, g, w)
                yp = _nn(pooled.astype(bf16), pw_ref[g].astype(bf16)) * ps_ref[:, PGD * g:PGD * (g + 1)]
                yacc[rows, DATTN + PGD * g:DATTN + PGD * (g + 1)] = yp
        y_ref[...] = yacc[...].astype(bf16)

    full = lambda *shape: pl.BlockSpec(shape, lambda n: (0,) * len(shape))
    smem = pl.BlockSpec(memory_space=pltpu.SMEM)
    return pl.pallas_call(
        body, grid=(nsteps,),
        in_specs=[pl.BlockSpec((step_rows, DIN), lambda n: (n, 0)),
                  pl.BlockSpec((BLK, DIN), lambda n: (jnp.maximum(n * MIX_SUB - 1, 0), 0)),
                  full(1, HD), full(1, HD), smem, smem, full(BLK, 2 * BLK),
                  full(len(POOL_WINDOWS), PGD, PGD), full(1, DPOOL)],
        out_specs=[pl.BlockSpec((step_rows, DMIX), lambda n: (n, 0)),
                   pl.BlockSpec((MIX_SUB, NH, BLK, 2 * BLK), lambda n: (n, 0, 0, 0))],
        out_shape=(SDS((T, DMIX), bf16), SDS((T // BLK, NH, BLK, 2 * BLK), bf16)),
        scratch_shapes=[pltpu.VMEM((NH, BLK, 2 * BLK), f32), pltpu.VMEM((step_rows, DMIX), f32)],
        compiler_params=_cparams(("arbitrary",)), name=name)(z, z, qg, kg, sinks, relb, bucket, pool_w, pscale)


def _mix_bwd(z, dy, probs, qg, kg, relb, bucket, pool_w, pscale, name):
    T = z.shape[0]
    step_rows = MIX_SUB * BLK
    nsteps = T // step_rows

    def body(zc_ref, zp_ref, dy_ref, p_ref, qg_ref, kg_ref, bk_ref, pw_ref, ps_ref,
             dz_ref, dqg_ref, dkg_ref, dsk_ref, drb_ref, dpw_ref, dps_ref, dbias_scr):
        n = pl.program_id(0)

        @pl.when(n == 0)
        def _():
            dbias_scr[...] = jnp.zeros_like(dbias_scr)
            dqg_ref[...] = jnp.zeros_like(dqg_ref)
            dkg_ref[...] = jnp.zeros_like(dkg_ref)
            dpw_ref[...] = jnp.zeros_like(dpw_ref)
            dps_ref[...] = jnp.zeros_like(dps_ref)

        qg, kg = qg_ref[...], kg_ref[...]
        for s in range(MIX_SUB):
            win = _Window(zc_ref, zp_ref, n, s)
            blk = win.blk
            rows = pl.ds(pl.multiple_of(blk * BLK, BLK), BLK)
            prow = pl.ds(pl.multiple_of(jnp.maximum(blk - 1, 0) * BLK, BLK), BLK)
            dyr = slice(s * BLK, (s + 1) * BLK)

            def into_prev(fn, s=s):
                if s == 0:
                    pl.when(n > 0)(fn)
                else:
                    fn()

            for kh in range(NKV):
                a = _attn_qkv(win, kh, qg, kg)
                pb = p_ref[s, GQA * kh:GQA * (kh + 1)].reshape(GQA * BLK, 2 * BLK)
                p = pb.astype(f32)
                do = jnp.concatenate([dy_ref[dyr, HD * (GQA * kh + g):HD * (GQA * kh + g + 1)] for g in range(GQA)],
                                     axis=0).astype(bf16)
                dv = _tn(pb, do)
                dp = _nt(do, a["vb"])
                delta = jnp.sum(p * dp, axis=-1, keepdims=True)
                ds = p * (dp - delta)
                for g in range(GQA):
                    dbias_scr[GQA * kh + g] += ds[g * BLK:(g + 1) * BLK]
                dsb = ds.astype(bf16)
                dqn = _nn(dsb, a["knb"]) * SCALE
                dkn = _tn(dsb, a["qsb"])
                qhat, khat = a["qhat"], a["khat"]
                dqg_ref[...] += jnp.sum(dqn * qhat, axis=0, keepdims=True)
                dkg_ref[...] += jnp.sum(dkn * khat, axis=0, keepdims=True)
                dqh = dqn * qg
                dq = a["rq"] * (dqh - qhat * jnp.mean(dqh * qhat, axis=-1, keepdims=True))
                dkh = dkn * kg
                dk = a["rk"] * (dkh - khat * jnp.mean(dkh * khat, axis=-1, keepdims=True))
                kc = DATTN + HD * kh
                vc = DATTN + DKV + HD * kh
                for g in range(GQA):
                    hc = HD * (GQA * kh + g)
                    dz_ref[rows, hc:hc + HD] = dq[g * BLK:(g + 1) * BLK]
                dz_ref[rows, kc:kc + HD] = dk[BLK:2 * BLK]
                dz_ref[rows, vc:vc + HD] = dv[BLK:2 * BLK]

                def kv_prev(dk=dk, dv=dv, kc=kc, vc=vc, prow=prow):
                    dz_ref[prow, kc:kc + HD] += dk[0:BLK]
                    dz_ref[prow, vc:vc + HD] += dv[0:BLK]

                into_prev(kv_prev)

            for g, w in enumerate(POOL_WINDOWS):
                c0 = DATTN + 2 * DKV + PGD * g
                pooled, cnt = _pool_group(win, g, w)
                pb = pooled.astype(bf16)
                wb = pw_ref[g].astype(bf16)
                dyp = dy_ref[dyr, DATTN + PGD * g:DATTN + PGD * (g + 1)]
                ypre = _nn(pb, wb)
                dps_ref[:, PGD * g:PGD * (g + 1)] += jnp.sum(dyp * ypre, axis=0, keepdims=True)
                dyg = (dyp * ps_ref[:, PGD * g:PGD * (g + 1)]).astype(bf16)
                dpw_ref[g] += _tn(pb, dyg)
                dpooled = _nt(dyg, wb)
                due = jnp.concatenate([jnp.zeros((BLK, PGD), f32), dpooled / cnt], axis=0)
                for k in POOL_STEPS[w]:
                    due = due + pltpu.roll(due, 2 * BLK - k, axis=0)
                dz_ref[rows, c0:c0 + PGD] = due[BLK:2 * BLK] - dpooled

                def pool_prev(due=due, c0=c0, prow=prow):
                    dz_ref[prow, c0:c0 + PGD] += due[0:BLK]

                into_prev(pool_prev)

        @pl.when(n == nsteps - 1)
        def _():
            bk = bk_ref[...]
            ri = lax.broadcasted_iota(i32, (NBUCK, NH), 0)
            ci = lax.broadcasted_iota(i32, (NBUCK, NH), 1)

            def step(b, acc):
                for h in range(NH):
                    sel = jnp.where(bk == b, dbias_scr[h], 0.0)
                    tot = jnp.sum(jnp.sum(sel, axis=1, keepdims=True), axis=0, keepdims=True)
                    acc = acc + jnp.where((ri == b) & (ci == h), tot, 0.0)
                return acc

            drb_ref[...] = lax.fori_loop(0, NBUCK, step, jnp.zeros((NBUCK, NH), f32))
            lane = lax.broadcasted_iota(i32, (1, 128), 1)
            dsk = jnp.zeros((1, 128), f32)
            for h in range(NH):
                tot = jnp.sum(jnp.sum(dbias_scr[h], axis=1, keepdims=True), axis=0, keepdims=True)
                dsk = dsk - jnp.where(lane == h, tot, 0.0)
            dsk_ref[...] = dsk

    full = lambda *shape: pl.BlockSpec(shape, lambda n: (0,) * len(shape))
    npg = len(POOL_WINDOWS)
    return pl.pallas_call(
        body, grid=(nsteps,),
        in_specs=[pl.BlockSpec((step_rows, DIN), lambda n: (n, 0)),
                  pl.BlockSpec((BLK, DIN), lambda n: (jnp.maximum(n * MIX_SUB - 1, 0), 0)),
                  pl.BlockSpec((step_rows, DMIX), lambda n: (n, 0)),
                  pl.BlockSpec((MIX_SUB, NH, BLK, 2 * BLK), lambda n: (n, 0, 0, 0)),
                  full(1, HD), full(1, HD), full(BLK, 2 * BLK), full(npg, PGD, PGD), full(1, DPOOL)],
        out_specs=[full(T, DIN), full(1, HD), full(1, HD), full(1, 128), full(NBUCK, NH),
                   full(npg, PGD, PGD), full(1, DPOOL)],
        out_shape=(SDS((T, DIN), f32), SDS((1, HD), f32), SDS((1, HD), f32), SDS((1, 128), f32),
                   SDS((NBUCK, NH), f32), SDS((npg, PGD, PGD), f32), SDS((1, DPOOL), f32)),
        scratch_shapes=[pltpu.VMEM((NH, BLK, 2 * BLK), f32)],
        compiler_params=_cparams(("arbitrary",), VMEM_LIMIT_V7X),
        name=name)(z, z, dy, probs, qg, kg, bucket, pool_w, pscale)


class _LocalWeights:
    def __init__(self, w1, wint, wout, w2):
        self.w1, self.wint, self.wout, self.w2 = w1, wint, wout, w2

    def ffn1(self):
        return self.w1

    def first_norm(self, x, gain):
        return _norm_fwd(x, gain, "norm1_fwd")

    def after_ffn1(self, gain, x1):
        return gain

    def mix(self, after):
        return self.wint, self.wout

    def before_out_proj(self, wout, after):
        return wout

    def ffn2(self, after):
        return self.w2

    def out_ffn2_grads_ready(self, dwout, dw2, after):
        return after

    def before_ffn1_bwd(self, dwint, dx1b):
        return dx1b


def _local_step(x, target, weights, g1, gm, g3, qg, kg, sinks, relb, pool_w, pscale):
    bucket = jnp.asarray(_t5_bucket_table())
    sk = sinks.reshape(NH)
    w1 = weights.ffn1()
    h1 = weights.first_norm(x, g1)
    x1, gate1, up1, h2 = _ffn_fwd(h1, w1, x, None, gm, "ffn1_fwd")
    gm = weights.after_ffn1(gm, x1)
    wint, wout = weights.mix(h2)
    z = _in_proj_fwd(h2, wint, "in_proj_fwd")
    ymix, probs = _mix_fwd(z, qg, kg, sk, relb, bucket, pool_w, pscale, "mix_fwd")
    wout = weights.before_out_proj(wout, ymix)
    x2, h3 = _out_proj_fwd(ymix, wout, x1, g3, "out_proj_fwd")
    w2 = weights.ffn2(h3)
    dy, gate2, up2, dyb, loss_lanes = _ffn_fwd(h3, w2, x2, target, None, "ffn2_fwd")

    dx2, dx2b, dg3, dw2 = _ffn_bwd(dyb, h3, gate2, up2, w2, (x2, g3, dy), "ffn2_bwd")
    dymix, dwout = _out_proj_bwd(dx2b, wout, ymix, "out_proj_bwd")
    dymix = weights.out_ffn2_grads_ready(dwout, dw2, dymix)
    dz, dqg, dkg, dsk, drb, dpw, dps = _mix_bwd(z, dymix, probs, qg, kg, relb, bucket, pool_w, pscale, "mix_bwd")
    dx1, dx1b, dgm, dwint = _in_proj_bwd(dz, wint, h2, (x1, gm, dx2), 0.5, "in_proj_bwd")
    dx1b = weights.before_ffn1_bwd(dwint, dx1b)
    dh1, dw1 = _ffn_bwd(dx1b, h1, gate1, up1, w1, None, "ffn1_bwd")
    small = dict(mix_norm=dgm, ffn2_norm=dg3, pool_scale=dps, q_norm=dqg, k_norm=dkg,
                 attn_sinks=dsk[:, :NH], rel_bias=drb, pool_w=dpw, loss=loss_lanes)
    return (dh1, dx1), (dw1, dwint, dwout, dw2), small


SMALL_NAMES = ("ffn1_norm", "mix_norm", "ffn2_norm", "pool_scale", "q_norm", "k_norm", "attn_sinks", "rel_bias",
               "pool_w", "loss")
SMALL_SHAPES = dict(ffn1_norm=(1, D), mix_norm=(1, D), ffn2_norm=(1, D), pool_scale=(1, DPOOL), q_norm=(1, HD),
                    k_norm=(1, HD), attn_sinks=(1, NH), rel_bias=(NBUCK, NH),
                    pool_w=(1, len(POOL_WINDOWS), PGD, PGD), loss=(1, 128))


def _small_rows(name):
    return -(-int(np.prod(SMALL_SHAPES[name])) // 128)


SMALL_OFF = {}
_r = 0
for _n in SMALL_NAMES:
    SMALL_OFF[_n] = _r
    _r += _small_rows(_n)
SMALL_ROWS = -(-_r // 16) * 16
LOSS_ROW = SMALL_OFF["loss"]


def _pack_small(vals):
    parts = []
    for n in SMALL_NAMES:
        size = _small_rows(n) * 128
        if n in vals:
            flat = vals[n].astype(f32).reshape(-1)
            parts.append(jnp.pad(flat, (0, size - flat.shape[0])))
        else:
            parts.append(jnp.zeros((size,), f32))
    flat = jnp.concatenate(parts)
    flat = jnp.pad(flat, (0, SMALL_ROWS * 128 - flat.shape[0]))
    return flat.reshape(SMALL_ROWS, 128)


def _unpack_small(packed, name):
    size = int(np.prod(SMALL_SHAPES[name]))
    r0 = SMALL_OFF[name]
    return packed[r0:r0 + _small_rows(name)].reshape(-1)[:size].reshape(SMALL_SHAPES[name])


def _position():
    return lax.axis_index("x"), lax.axis_index("y"), lax.axis_index("c")


def _dev_index(x, y, c):
    return 4 * x + 2 * y + c


G1_PIECES, MIX_PIECES, F2_PIECES = (0, 1, 2), (3, 4), (5, 6, 7)


def _group_rows(pieces):
    return sum(PIECE_ROWS[k] for k in pieces)


def _shard_piece(s_ref, k):
    return s_ref.at[pl.ds(PIECE_OFF[k], PIECE_ROWS[k]), :]


def _shard_group(s_ref, pieces):
    return s_ref.at[pl.ds(PIECE_OFF[pieces[0]], _group_rows(pieces)), :]


def _weight_pieces(w1_ref=None, wi_ref=None, wo_ref=None, w2_ref=None):
    arrs = {}
    if w1_ref is not None:
        arrs.update({0: w1_ref.at[0], 1: w1_ref.at[1], 2: w1_ref.at[2]})
    if wi_ref is not None:
        arrs[3] = wi_ref
    if wo_ref is not None:
        arrs[4] = wo_ref
    if w2_ref is not None:
        arrs.update({5: w2_ref.at[0], 6: w2_ref.at[1], 7: w2_ref.at[2]})
    return arrs


def _block_rows(arrs, k, dev):
    r = PIECE_ROWS[k]
    return arrs[k].at[pl.ds(pl.multiple_of(_dev_index(*dev) * r, 16), r), :]


NORM_ROWS = 512


def _all_gather_ffn1(shard, x, gain):
    pieces = G1_PIECES
    rest_pieces = MIX_PIECES + F2_PIECES
    half = FS // 2
    T = x.shape[0]
    SIB, X0, X1, Y0, Y1, RELAY_Y, RELAY_X, ON_X, ON_Y, ON_D0, ON_D1 = range(11)

    def body(s_ref, x_ref, g_ref, w1_ref, h_ref, wi_ref, wo_ref, w2_ref, xbuf, hbuf, rest_buf,
             send_sems, recv_sems, local_sem, norm_sems):
        x, y, c = _position()
        me, sib = (x, y, c), (x, y, 1 - c)
        xn, yn, dg = (1 - x, y, c), (x, 1 - y, c), (1 - x, 1 - y, c)
        arrs = _weight_pieces(w1_ref=w1_ref)

        def place_rest():
            rest = _weight_pieces(wi_ref=wi_ref, wo_ref=wo_ref, w2_ref=w2_ref)
            grp = _shard_group(s_ref, rest_pieces)
            load = pltpu.make_async_copy(grp, rest_buf, norm_sems.at[0])
            load.start()
            load.wait()
            base = PIECE_OFF[rest_pieces[0]]
            for k in rest_pieces:
                pltpu.make_async_copy(rest_buf.at[pl.ds(PIECE_OFF[k] - base, PIECE_ROWS[k]), :],
                                      _block_rows(rest, k, me), norm_sems.at[1]).start()
            pltpu.make_async_copy(grp, rest_buf, norm_sems.at[1]).wait()

        def first_norm():
            for r in range(0, T, NORM_ROWS):
                load = pltpu.make_async_copy(x_ref.at[pl.ds(r, NORM_ROWS), :], xbuf, norm_sems.at[0])
                load.start()
                load.wait()
                xv = xbuf[...]
                rs = lax.rsqrt(jnp.mean(xv * xv, axis=-1, keepdims=True) + EPS)
                hbuf[...] = (xv * rs * g_ref[...]).astype(bf16)
                store = pltpu.make_async_copy(hbuf, h_ref.at[pl.ds(r, NORM_ROWS), :], norm_sems.at[1])
                store.start()
                store.wait()

        def rows_of(k, block, hf):
            r = PIECE_ROWS[k]
            start, size = (0, r) if hf is None else (hf * half, half)
            return arrs[k].at[pl.ds(pl.multiple_of(_dev_index(*block) * r + start, 16), size), :]

        def copies(rel, block, hf, to, from_shard=False):
            def src(k):
                if not from_shard:
                    return rows_of(k, block, hf)
                start, size = (0, PIECE_ROWS[k]) if hf is None else (hf * half, half)
                return s_ref.at[pl.ds(PIECE_OFF[k] + start, size), :]
            return [pltpu.make_async_remote_copy(
                src_ref=src(k), dst_ref=rows_of(k, block, hf), send_sem=send_sems.at[rel], recv_sem=recv_sems.at[rel],
                device_id=to, device_id_type=MESH) for k in pieces]

        def waiter(rel, hf):
            nrows = len(pieces) * (FS if hf is None else half)
            grp = s_ref.at[pl.ds(0, nrows), :]
            return pltpu.make_async_remote_copy(src_ref=grp, dst_ref=grp, send_sem=send_sems.at[rel],
                                                recv_sem=recv_sems.at[rel], device_id=me, device_id_type=MESH)

        def start(cps):
            for cp in cps:
                cp.start()

        mine = [pltpu.make_async_copy(_shard_piece(s_ref, k), _block_rows(arrs, k, me), local_sem) for k in pieces]
        start(mine)
        start(copies(SIB, me, None, sib, True))
        start(copies(X0, me, 0, xn, True))
        start(copies(Y1, me, 1, yn, True))
        start(copies(X1, me, 1, xn, True))
        start(copies(Y0, me, 0, yn, True))
        first_norm()
        place_rest()
        waiter(X0, 0).wait_recv()
        start(copies(RELAY_Y, xn, 0, yn))
        waiter(Y1, 1).wait_recv()
        start(copies(RELAY_X, yn, 1, xn))
        waiter(X1, 1).wait_recv()
        start(copies(ON_X, xn, None, sib))
        waiter(Y0, 0).wait_recv()
        start(copies(ON_Y, yn, None, sib))
        waiter(RELAY_Y, 0).wait_recv()
        start(copies(ON_D0, dg, 0, sib))
        waiter(RELAY_X, 1).wait_recv()
        start(copies(ON_D1, dg, 1, sib))
        waiter(SIB, None).wait_recv()
        waiter(ON_X, None).wait_recv()
        waiter(ON_Y, None).wait_recv()
        waiter(ON_D0, 0).wait_recv()
        waiter(ON_D1, 1).wait_recv()
        for rel, hf in ((SIB, None), (X0, 0), (X1, 1), (Y0, 0), (Y1, 1), (RELAY_Y, 0), (RELAY_X, 1),
                        (ON_X, None), (ON_Y, None), (ON_D0, 0), (ON_D1, 1)):
            waiter(rel, hf).wait_send()
        grp = _shard_group(s_ref, pieces)
        pltpu.make_async_copy(grp, grp, local_sem).wait()

    hbm = pl.BlockSpec(memory_space=pl.ANY)
    return pl.pallas_call(
        body, in_specs=[hbm, hbm, pl.BlockSpec(memory_space=pltpu.VMEM)], out_specs=[hbm] * 5,
        out_shape=(SDS((3, F, D), bf16), SDS((T, D), bf16),
                   SDS((DIN, D), bf16), SDS((DMIX, D), bf16), SDS((3, F, D), bf16)),
        scratch_shapes=[pltpu.VMEM((NORM_ROWS, D), f32), pltpu.VMEM((NORM_ROWS, D), bf16),
                        pltpu.VMEM((_group_rows(rest_pieces), D), bf16),
                        pltpu.SemaphoreType.DMA((11,)), pltpu.SemaphoreType.DMA((11,)), pltpu.SemaphoreType.DMA,
                        pltpu.SemaphoreType.DMA((2,))],
        compiler_params=pltpu.CompilerParams(has_side_effects=True),
        name="all_gather_ffn1")(shard, x, gain)


HBM_SPEC = pl.BlockSpec(memory_space=pltpu.HBM)
SEM_SPEC = pl.BlockSpec(memory_space=pltpu.SEMAPHORE)
ANY_SPEC = pl.BlockSpec(memory_space=pl.ANY)
SPLIT_EFFECT = pltpu.SideEffectType.DATAFLOW_SIDE_EFFECTING


def _in_hbm(a):
    return pltpu.with_memory_space_constraint(a, pltpu.HBM)


def _hbm_like(a):
    return pltpu.HBM(a.shape, a.dtype)


def _gather_rest_start(shard, wi, wo, w2, w1):
    def body(s_ref, wi_ref, wo_ref, w2_ref, w1_ref,
             ssem_m, rsem_m0, rsem_m, ssem_f, rsem_f0, rsem_f, s_o, wi_o, wo_o, w2_o, w1_o):
        x, y, c = _position()
        me, sib = (x, y, c), (x, y, 1 - c)
        chips = [(1 - x, y), (x, 1 - y), (1 - x, 1 - y)]
        arrs = _weight_pieces(wi_ref=wi_ref, wo_ref=wo_ref, w2_ref=w2_ref)
        for pieces, ssem, rsem0, rsem in ((MIX_PIECES, ssem_m, rsem_m0, rsem_m), (F2_PIECES, ssem_f, rsem_f0, rsem_f)):
            for p in pieces:
                pltpu.make_async_remote_copy(
                    src_ref=_shard_piece(s_ref, p), dst_ref=_block_rows(arrs, p, me), send_sem=ssem.at[0],
                    recv_sem=rsem0, device_id=sib, device_id_type=MESH).start()
            for j, chip in enumerate(chips):
                for p in pieces:
                    pltpu.make_async_remote_copy(
                        src_ref=_shard_piece(s_ref, p), dst_ref=_block_rows(arrs, p, me), send_sem=ssem.at[1 + j],
                        recv_sem=rsem.at[j], device_id=(*chip, c), device_id_type=MESH).start()

    dma = pltpu.SemaphoreType.DMA
    return pl.pallas_call(
        body, name="gather_rest_start",
        out_shape=(dma((4,)), dma(()), dma((3,)), dma((4,)), dma(()), dma((3,)),
                   _hbm_like(shard), _hbm_like(wi), _hbm_like(wo), _hbm_like(w2), _hbm_like(w1)),
        in_specs=(HBM_SPEC,) * 5, out_specs=(SEM_SPEC,) * 6 + (HBM_SPEC,) * 5,
        input_output_aliases={0: 6, 1: 7, 2: 8, 3: 9, 4: 10},
        compiler_params=pltpu.CompilerParams(has_side_effects=SPLIT_EFFECT),
    )(_in_hbm(shard), _in_hbm(wi), _in_hbm(wo), _in_hbm(w2), _in_hbm(w1))


def _gather_mix_pass_on(rsem_m, wi, wo, thru, after):
    def body(wi_ref, wo_ref, thru_ref, rsem, after_ref, fsend, frecv, wi_o, wo_o, thru_o):
        x, y, c = _position()
        sib = (x, y, 1 - c)
        arrs = _weight_pieces(wi_ref=wi_ref, wo_ref=wo_ref)
        both = wi_ref.at[pl.ds(0, _group_rows(MIX_PIECES)), :]
        for j, chip in enumerate([(1 - x, y), (x, 1 - y), (1 - x, 1 - y)]):
            pltpu.make_async_remote_copy(src_ref=both, dst_ref=both, send_sem=fsend.at[j], recv_sem=rsem.at[j],
                                         device_id=(x, y, c), device_id_type=MESH).wait_recv()
            for p in MIX_PIECES:
                rows = _block_rows(arrs, p, (*chip, c))
                pltpu.make_async_remote_copy(src_ref=rows, dst_ref=rows, send_sem=fsend.at[j], recv_sem=frecv.at[j],
                                             device_id=sib, device_id_type=MESH).start()

    dma = pltpu.SemaphoreType.DMA
    return pl.pallas_call(
        body, name="gather_mix_pass_on",
        out_shape=(dma((3,)), dma((3,)), _hbm_like(wi), _hbm_like(wo), _hbm_like(thru)),
        in_specs=(HBM_SPEC, HBM_SPEC, HBM_SPEC, SEM_SPEC, ANY_SPEC), out_specs=(SEM_SPEC, SEM_SPEC) + (HBM_SPEC,) * 3,
        input_output_aliases={0: 2, 1: 3, 2: 4},
        compiler_params=pltpu.CompilerParams(has_side_effects=SPLIT_EFFECT),
    )(wi, wo, _in_hbm(thru), rsem_m, after)


def _gather_mix_wait(ssem_m, rsem_m0, fsend, frecv, shard, wi, wo, after):
    def body(s_ref, wi_ref, wo_ref, ssem, rsem0, fs, fr, after_ref, s_o, wi_o, wo_o):
        x, y, c = _position()
        grp = _shard_group(s_ref, MIX_PIECES)

        def waiter(send_sem, recv_sem):
            return pltpu.make_async_remote_copy(src_ref=grp, dst_ref=grp, send_sem=send_sem, recv_sem=recv_sem,
                                                device_id=(x, y, c), device_id_type=MESH)

        waiter(ssem.at[0], rsem0).wait_recv()
        for j in range(3):
            waiter(fs.at[j], fr.at[j]).wait_recv()
        for rel in range(4):
            waiter(ssem.at[rel], rsem0).wait_send()
        for j in range(3):
            waiter(fs.at[j], fr.at[j]).wait_send()

    return pl.pallas_call(
        body, name="gather_mix_wait", out_shape=(_hbm_like(shard), _hbm_like(wi), _hbm_like(wo)),
        in_specs=(HBM_SPEC,) * 3 + (SEM_SPEC,) * 4 + (ANY_SPEC,), out_specs=(HBM_SPEC,) * 3,
        input_output_aliases={0: 0, 1: 1, 2: 2},
        compiler_params=pltpu.CompilerParams(has_side_effects=SPLIT_EFFECT),
    )(shard, wi, wo, ssem_m, rsem_m0, fsend, frecv, after)


def _gather_ffn2_pass_on(rsem_f, w2, wo, after):
    def body(w2_ref, wo_ref, rsem, after_ref, fsend, frecv, w2_o, wo_o):
        x, y, c = _position()
        sib = (x, y, 1 - c)
        chips = [(1 - x, y), (x, 1 - y), (1 - x, 1 - y)]
        arrs = _weight_pieces(w2_ref=w2_ref)
        three = w2_ref.at[0, pl.ds(0, _group_rows(F2_PIECES)), :]
        for j, chip in enumerate(chips):
            pltpu.make_async_remote_copy(src_ref=three, dst_ref=three, send_sem=fsend.at[j], recv_sem=rsem.at[j],
                                         device_id=(x, y, c), device_id_type=MESH).wait_recv()
            for p in F2_PIECES:
                rows = _block_rows(arrs, p, (*chip, c))
                pltpu.make_async_remote_copy(src_ref=rows, dst_ref=rows, send_sem=fsend.at[j], recv_sem=frecv.at[j],
                                             device_id=sib, device_id_type=MESH).start()

    dma = pltpu.SemaphoreType.DMA
    return pl.pallas_call(
        body, name="gather_ffn2_pass_on", out_shape=(dma((3,)), dma((3,)), _hbm_like(w2), _hbm_like(wo)),
        in_specs=(HBM_SPEC, HBM_SPEC, SEM_SPEC, ANY_SPEC), out_specs=(SEM_SPEC, SEM_SPEC, HBM_SPEC, HBM_SPEC),
        input_output_aliases={0: 2, 1: 3},
        compiler_params=pltpu.CompilerParams(has_side_effects=SPLIT_EFFECT),
    )(w2, wo, rsem_f, after)


def _gather_ffn2_wait(ssem_f, rsem_f0, fsend, frecv, shard, w2, after):
    def body(s_ref, w2_ref, ssem, rsem0, fs, fr, after_ref, w2_o):
        x, y, c = _position()
        grp = _shard_group(s_ref, F2_PIECES)

        def waiter(send_sem, recv_sem):
            return pltpu.make_async_remote_copy(src_ref=grp, dst_ref=grp, send_sem=send_sem, recv_sem=recv_sem,
                                                device_id=(x, y, c), device_id_type=MESH)

        waiter(ssem.at[0], rsem0).wait_recv()
        for j in range(3):
            waiter(fs.at[j], fr.at[j]).wait_recv()
        for rel in range(4):
            waiter(ssem.at[rel], rsem0).wait_send()
        for j in range(3):
            waiter(fs.at[j], fr.at[j]).wait_send()

    return pl.pallas_call(
        body, name="gather_ffn2_wait", out_shape=_hbm_like(w2),
        in_specs=(HBM_SPEC, HBM_SPEC, SEM_SPEC, SEM_SPEC, SEM_SPEC, SEM_SPEC, ANY_SPEC), out_specs=HBM_SPEC,
        input_output_aliases={1: 0},
        compiler_params=pltpu.CompilerParams(has_side_effects=SPLIT_EFFECT),
    )(shard, w2, ssem_f, rsem_f0, fsend, frecv, after)


class _GatheredWeights(_LocalWeights):
    def __init__(self, shard, x, gain1):
        w1, self.h1, wi, wo, w2 = _all_gather_ffn1(shard, x, gain1)
        (self.ssem_m, self.rsem_m0, self.rsem_m, self.ssem_f, self.rsem_f0, self.rsem_f,
         self.shard, self.wi, self.wo, self.w2_part, self.w1) = _gather_rest_start(shard, wi, wo, w2, w1)

    def first_norm(self, x, gain):
        return self.h1

    def after_ffn1(self, gain, x1):
        self.fsend_m, self.frecv_m, self.wi, self.wo, gain = _gather_mix_pass_on(self.rsem_m, self.wi, self.wo, gain, x1)
        return gain

    def mix(self, after):
        self.shard, wint, wout = _gather_mix_wait(self.ssem_m, self.rsem_m0, self.fsend_m, self.frecv_m, self.shard,
                                                  self.wi, self.wo, after)
        return wint, wout

    def before_out_proj(self, wout, after):
        self.fsend, self.frecv, self.w2_part, wout = _gather_ffn2_pass_on(self.rsem_f, self.w2_part, wout, after)
        return wout

    def ffn2(self, after):
        return _gather_ffn2_wait(self.ssem_f, self.rsem_f0, self.fsend, self.frecv, self.shard, self.w2_part, after)

    def out_ffn2_grads_ready(self, dwout, dw2, after):
        rx1 = lax.empty((4, RSA_ROWS, D), bf16)
        sa, ra, sent, rx1, after = _rsa_level1_start(dict(wo=dwout, w2=dw2), rx1, after, "rsa_level1_start_out_ffn2")
        self.level1 = ((sa, ra), sent, rx1)
        return after

    def before_ffn1_bwd(self, dwint, dx1b):
        early, sent, rx1 = self.level1
        sa, ra, late, rx1, dx1b = _rsa_level1_start(dict(wi=dwint), rx1, dx1b, "rsa_level1_start_in")
        started = (((MIX_PIECES[1],) + F2_PIECES, *early), ((MIX_PIECES[0],), sa, ra))
        rx2 = lax.empty((3, RSA_ROWS, D), bf16)
        self.sb, self.rb, self.tx, self.acc, self.rx2, dx1b = _rsa_sums_and_send(
            started, late["wi"], sent["wo"], sent["w2"], rx1, rx2, dx1b)
        return dx1b

    def mix_ffn2_grads_parts(self, after):
        rx2 = _rsa_level2_wait(self.sb, self.rb, self.tx, self.rx2, after)
        return self.acc, rx2


def _reduce_scatter_ffn1_head(dw1, small_packed, first_norm):
    pieces = G1_PIECES
    half = FS // 2
    hrows = len(pieces) * half
    nrows = 2 * hrows
    X_RELAY, Y_RELAY = range(2)

    T = first_norm[0].shape[0]

    def body(d1_ref, p_ref, dh_hbm, x_hbm, gain_ref, dr_hbm,
             forx_ref, fory_ref, own_ref, rx1_ref, relx_ref, rely_ref, gx_hbm, tot_ref,
             own_buf, rx_buf, tx1, tx2, tx3, acc, sa, ra, sb, rb, lsem, pair, chips, small_tot, small_send, small_recv,
             xbuf, rbuf, hbuf, obuf, norm_in_sems, norm_out_sems):
        x, y, c = _position()
        me, sib = (x, y, c), (x, y, 1 - c)
        xn, yn = (1 - x, y, c), (x, 1 - y, c)
        rel_chips = [(x, y), (1 - x, y), (x, 1 - y), (1 - x, 1 - y)]
        srcs = _weight_pieces(w1_ref=d1_ref)

        my_chip = 2 * x + y
        pair[c] = p_ref[...]
        swap = pltpu.make_async_remote_copy(
            src_ref=pair.at[c], dst_ref=pair.at[c], send_sem=small_send.at[0], recv_sem=small_recv.at[0],
            device_id=sib, device_id_type=MESH)
        mine = pl.ds(pl.multiple_of(c * (SMALL_ROWS // 2), 8), SMALL_ROWS // 2)
        small = [pltpu.make_async_remote_copy(
            src_ref=chips.at[my_chip, mine, :], dst_ref=chips.at[my_chip, mine, :], send_sem=small_send.at[j],
            recv_sem=small_recv.at[j], device_id=(*rel_chips[j], c), device_id_type=MESH) for j in (1, 2, 3)]
        give = pltpu.make_async_remote_copy(
            src_ref=small_tot.at[mine, :], dst_ref=small_tot.at[mine, :], send_sem=small_send.at[4],
            recv_sem=small_recv.at[4], device_id=sib, device_id_type=MESH)

        def part(k, dev, hf):
            r = PIECE_ROWS[k]
            return srcs[k].at[pl.ds(pl.multiple_of(_dev_index(*dev) * r + hf * half, 16), half), :]

        def slot(ref, k, hf):
            return ref.at[pl.ds(hf * hrows + k * half, half), :]

        halves = [(k, hf) for hf in (0, 1) for k in pieces]

        for j in (3, 1, 2, 0):
            for k, hf in halves:
                pltpu.make_async_remote_copy(
                    src_ref=part(k, (*rel_chips[j], 1 - c), hf), dst_ref=slot(rx1_ref.at[j], k, hf),
                    send_sem=sa.at[j], recv_sem=ra.at[j], device_id=sib, device_id_type=MESH).start()

        def wait_a(j):
            return pltpu.make_async_remote_copy(src_ref=rx1_ref.at[j], dst_ref=rx1_ref.at[j], send_sem=sa.at[j],
                                                recv_sem=ra.at[j], device_id=me, device_id_type=MESH)

        def ici(rel, src, dst, to):
            return pltpu.make_async_remote_copy(src_ref=src, dst_ref=dst, send_sem=sb.at[rel], recv_sem=rb.at[rel],
                                                device_id=to, device_id_type=MESH)

        first, second = pl.ds(0, hrows), pl.ds(hrows, hrows)
        sends = {
            X_RELAY: ici(X_RELAY, tx3.at[first, :], relx_ref, xn),
            Y_RELAY: ici(Y_RELAY, tx3.at[second, :], rely_ref, yn),
        }

        def chip_sum(j, dst):
            loads = [pltpu.make_async_copy(part(k, (*rel_chips[j], c), hf), slot(own_buf, k, hf), lsem.at[0])
                     for k, hf in halves]
            for cp in loads:
                cp.start()
            wait_a(j).wait_recv()
            got = pltpu.make_async_copy(rx1_ref.at[j], rx_buf, lsem.at[1])
            got.start()
            pltpu.make_async_copy(rx_buf, rx_buf, lsem.at[0]).wait()
            got.wait()

            def add(i, carry):
                rows = pl.ds(pl.multiple_of(i * half, 16), half)
                tot = own_buf[rows, :].astype(f32) + rx_buf[rows, :].astype(f32)
                dst[rows, :] = tot.astype(dst.dtype)
                return carry

            lax.fori_loop(0, nrows // half, add, 0)

        def add_landed(landed, dst, rows0, nrows_):
            got = pltpu.make_async_copy(landed, rx_buf.at[pl.ds(0, nrows_), :], lsem.at[1])
            got.start()
            got.wait()

            def add(i, carry):
                src_rows = pl.ds(pl.multiple_of(i * half, 16), half)
                dst_rows = pl.ds(pl.multiple_of(rows0 + i * half, 16), half)
                dst[dst_rows, :] = (dst[dst_rows, :].astype(f32) + rx_buf[src_rows, :].astype(f32)).astype(dst.dtype)
                return carry

            lax.fori_loop(0, nrows_ // half, add, 0)

        chip_sum(3, tx3)
        sends[X_RELAY].start()
        sends[Y_RELAY].start()
        dg = _norm_bwd_rows(T, dh_hbm, x_hbm, gain_ref, dr_hbm, gx_hbm.at[0], None,
                            xbuf, rbuf, hbuf, obuf, None, norm_in_sems, norm_out_sems)
        r0 = SMALL_OFF["ffn1_norm"]
        for k in range(D // 128):
            pair[c, r0 + k:r0 + k + 1, :] = dg[:, 128 * k:128 * (k + 1)]
        swap.start()
        swap.wait_recv()
        chips[my_chip] = pair[0] + pair[1]
        for cp in small:
            cp.start()
        chip_sum(1, tx1)
        chip_sum(2, tx2)
        chip_sum(0, acc)
        own_out = pltpu.make_async_copy(acc, own_ref, lsem.at[0])
        own_out.start()
        sends[X_RELAY].wait_recv()
        add_landed(relx_ref, tx2, 0, hrows)
        sends[Y_RELAY].wait_recv()
        add_landed(rely_ref, tx1, hrows, hrows)
        own_out.wait()
        outs = [pltpu.make_async_copy(tx1, forx_ref, lsem.at[0]), pltpu.make_async_copy(tx2, fory_ref, lsem.at[1])]
        for cp in outs:
            cp.start()
        for cp in outs:
            cp.wait()
        for cp in small:
            cp.wait_recv()
        small_tot[mine, :] = (chips[0, mine, :] + chips[1, mine, :]) + (chips[2, mine, :] + chips[3, mine, :])
        give.start()
        give.wait_recv()
        tot = small_tot[...]
        tot_ref[...] = tot
        loss = jnp.sum(tot[LOSS_ROW:LOSS_ROW + 1, :], axis=-1, keepdims=True)
        tot_ref[LOSS_ROW:LOSS_ROW + 1, :] = jnp.broadcast_to(loss, (1, 128))
        for j in range(4):
            wait_a(j).wait_send()
        for cp in sends.values():
            cp.wait_send()
        swap.wait_send()
        for cp in small + [give]:
            cp.wait_send()

    hbm = pl.BlockSpec(memory_space=pl.ANY)
    vm = pl.BlockSpec(memory_space=pltpu.VMEM)
    outs = pl.pallas_call(
        body, in_specs=[hbm, vm, hbm, hbm, vm, hbm], out_specs=[hbm] * 7 + [vm],
        out_shape=(SDS((nrows, D), bf16), SDS((nrows, D), bf16), SDS((nrows, D), f32), SDS((4, nrows, D), bf16),
                   SDS((hrows, D), bf16), SDS((hrows, D), bf16), SDS((1, T, D), f32), SDS((SMALL_ROWS, 128), f32)),
        scratch_shapes=[pltpu.VMEM((nrows, D), bf16), pltpu.VMEM((nrows, D), bf16),
                        pltpu.VMEM((nrows, D), bf16), pltpu.VMEM((nrows, D), bf16), pltpu.VMEM((nrows, D), bf16),
                        pltpu.VMEM((nrows, D), f32),
                        pltpu.SemaphoreType.DMA((4,)), pltpu.SemaphoreType.DMA((4,)),
                        pltpu.SemaphoreType.DMA((2,)), pltpu.SemaphoreType.DMA((2,)), pltpu.SemaphoreType.DMA((2,)),
                        pltpu.VMEM((2, SMALL_ROWS, 128), f32), pltpu.VMEM((4, SMALL_ROWS, 128), f32),
                        pltpu.VMEM((SMALL_ROWS, 128), f32),
                        pltpu.SemaphoreType.DMA((5,)), pltpu.SemaphoreType.DMA((5,))]
        + [sc for sc in _norm_bwd_scratch(True, False) if sc is not None],
        compiler_params=pltpu.CompilerParams(has_side_effects=True, vmem_limit_bytes=VMEM_LIMIT_V7X),
        name="reduce_scatter_ffn1_head")(dw1, small_packed, *first_norm)
    return outs[0], outs[1], outs[2], outs[-1], outs[-2]


def _rs1_tail_start(for_x, for_y, from_x, from_y, thru):
    def body(fx_ref, fy_ref, lx_ref, ly_ref, thru_ref, ssem, rsem, fx_o, fy_o, lx_o, ly_o, thru_o):
        x, y, c = _position()
        pltpu.make_async_remote_copy(src_ref=fx_ref, dst_ref=lx_ref, send_sem=ssem.at[0], recv_sem=rsem.at[0],
                                     device_id=(1 - x, y, c), device_id_type=MESH).start()
        pltpu.make_async_remote_copy(src_ref=fy_ref, dst_ref=ly_ref, send_sem=ssem.at[1], recv_sem=rsem.at[1],
                                     device_id=(x, 1 - y, c), device_id_type=MESH).start()

    dma = pltpu.SemaphoreType.DMA
    arrs = (for_x, for_y, from_x, from_y, thru)
    return pl.pallas_call(
        body, name="rs1_tail_start", out_shape=(dma((2,)), dma((2,))) + tuple(_hbm_like(a) for a in arrs),
        in_specs=(HBM_SPEC,) * 5, out_specs=(SEM_SPEC,) * 2 + (HBM_SPEC,) * 5,
        input_output_aliases={0: 2, 1: 3, 2: 4, 3: 5, 4: 6},
        compiler_params=pltpu.CompilerParams(has_side_effects=SPLIT_EFFECT),
    )(*[_in_hbm(a) for a in arrs])


RSA_PIECES = MIX_PIECES + F2_PIECES
RSA_ROWS = _group_rows(RSA_PIECES)
RSA_OFF = {k: PIECE_OFF[k] - PIECE_OFF[RSA_PIECES[0]] for k in RSA_PIECES}
RSA_BLOCK = 192


def _rsa_rows(ref, k):
    return ref.at[pl.ds(RSA_OFF[k], PIECE_ROWS[k]), :]


def _rsa_level1_start(grads, rx1, thru, name):
    keys = sorted(grads)
    n = len(keys)

    def body(*refs):
        srcs = _weight_pieces(**{k + "_ref": ref for k, ref in zip(keys, refs[:n])})
        rx1_ref, sa, ra = refs[n], refs[n + 2], refs[n + 3]
        x, y, c = _position()
        for j, chip in enumerate([(x, y), (1 - x, y), (x, 1 - y), (1 - x, 1 - y)]):
            for k in sorted(srcs):
                pltpu.make_async_remote_copy(
                    src_ref=_block_rows(srcs, k, (*chip, 1 - c)), dst_ref=_rsa_rows(rx1_ref.at[j], k),
                    send_sem=sa.at[j], recv_sem=ra.at[j], device_id=(x, y, 1 - c), device_id_type=MESH).start()

    dma = pltpu.SemaphoreType.DMA
    arrs = tuple(grads[k] for k in keys) + (rx1, thru)
    outs = pl.pallas_call(
        body, name=name, out_shape=(dma((4,)), dma((4,))) + tuple(_hbm_like(a) for a in arrs),
        in_specs=(HBM_SPEC,) * len(arrs), out_specs=(SEM_SPEC,) * 2 + (HBM_SPEC,) * len(arrs),
        input_output_aliases={i: i + 2 for i in range(len(arrs))},
        compiler_params=pltpu.CompilerParams(has_side_effects=SPLIT_EFFECT),
    )(*[_in_hbm(a) for a in arrs])
    return outs[0], outs[1], dict(zip(keys, outs[2:2 + n])), outs[2 + n], outs[3 + n]


def _rsa_sums_and_send(started, dwint, dwout, dw2, rx1, rx2, thru):
    nblk = RSA_ROWS // RSA_BLOCK
    nstart = len(started)

    def body(*refs):
        di_ref, do_ref, d2_ref, rx1_ref, rx2_ref = refs[:5]
        l1_sems = refs[6:6 + 2 * nstart]
        sb, rb, tx_ref, acc_ref = refs[6 + 2 * nstart:10 + 2 * nstart]
        own_buf, rx_buf, tx_buf, acc_buf, in_sems, out_sems = refs[13 + 2 * nstart:]
        x, y, c = _position()
        srcs = _weight_pieces(wi_ref=di_ref, wo_ref=do_ref, w2_ref=d2_ref)
        chips = [(x, y), (1 - x, y), (x, 1 - y), (1 - x, 1 - y)]

        for g, (pieces, _, _) in enumerate(started):
            ssem, rsem = l1_sems[2 * g], l1_sems[2 * g + 1]
            for j in range(4):
                rows = rx1_ref.at[j, pl.ds(RSA_OFF[pieces[0]], _group_rows(pieces)), :]
                d = pltpu.make_async_remote_copy(src_ref=rows, dst_ref=rows, send_sem=ssem.at[j], recv_sem=rsem.at[j],
                                                 device_id=(x, y, c), device_id_type=MESH)
                d.wait_recv()
                d.wait_send()

        def start_loads(j):
            s = j % 2
            for k in RSA_PIECES:
                pltpu.make_async_copy(_block_rows(srcs, k, (*chips[j], c)), _rsa_rows(own_buf.at[s], k),
                                      in_sems.at[2 * s]).start()
            pltpu.make_async_copy(rx1_ref.at[j], rx_buf.at[s], in_sems.at[2 * s + 1]).start()

        def wait_loads(j):
            s = j % 2
            pltpu.make_async_copy(rx1_ref.at[j], own_buf.at[s], in_sems.at[2 * s]).wait()
            pltpu.make_async_copy(rx1_ref.at[j], rx_buf.at[s], in_sems.at[2 * s + 1]).wait()

        def store(j):
            if j == 0:
                return pltpu.make_async_copy(acc_buf, acc_ref, out_sems.at[2])
            return pltpu.make_async_copy(tx_buf.at[j % 2], tx_ref.at[j - 1], out_sems.at[j % 2])

        def send(j):
            return pltpu.make_async_remote_copy(src_ref=tx_ref.at[j - 1], dst_ref=rx2_ref.at[j - 1], send_sem=sb.at[j - 1],
                                                recv_sem=rb.at[j - 1], device_id=(*chips[j], c), device_id_type=MESH)

        start_loads(0)
        for j in range(4):
            s = j % 2
            if j + 1 < 4:
                start_loads(j + 1)
            wait_loads(j)
            if j == 3:
                store(1).wait()
                send(1).start()

            def add(i, carry, j=j, s=s):
                rows = pl.ds(pl.multiple_of(i * RSA_BLOCK, 16), RSA_BLOCK)
                tot = own_buf[s, rows, :].astype(f32) + rx_buf[s, rows, :].astype(f32)
                if j == 0:
                    acc_buf[rows, :] = tot
                else:
                    tx_buf[s, rows, :] = tot.astype(bf16)
                return carry

            lax.fori_loop(0, nblk, add, 0)
            store(j).start()
        store(0).wait()
        for j in (2, 3):
            store(j).wait()
            send(j).start()

    dma = pltpu.SemaphoreType.DMA
    passed = (rx1, rx2, thru)
    outs = pl.pallas_call(
        body, name="rsa_sums_and_send",
        in_specs=(HBM_SPEC,) * 6 + (SEM_SPEC,) * (2 * nstart),
        out_specs=(SEM_SPEC,) * 2 + (HBM_SPEC,) * 5,
        out_shape=(dma((3,)), dma((3,)), pltpu.HBM((3, RSA_ROWS, D), bf16), pltpu.HBM((RSA_ROWS, D), f32))
        + tuple(_hbm_like(a) for a in passed),
        input_output_aliases={3: 4, 4: 5, 5: 6},
        scratch_shapes=[pltpu.VMEM((2, RSA_ROWS, D), bf16), pltpu.VMEM((2, RSA_ROWS, D), bf16),
                        pltpu.VMEM((2, RSA_ROWS, D), bf16), pltpu.VMEM((RSA_ROWS, D), f32),
                        dma((4,)), dma((3,))],
        compiler_params=pltpu.CompilerParams(has_side_effects=SPLIT_EFFECT, vmem_limit_bytes=VMEM_LIMIT_V7X),
    )(*[_in_hbm(a) for a in (dwint, dwout, dw2) + passed], *[sem for _, sa, ra in started for sem in (sa, ra)])
    sb, rb, tx, acc, _, rx2, thru = outs
    return sb, rb, tx, acc, rx2, thru


def _rsa_level2_wait(sb, rb, tx, rx2, after):
    def body(tx_ref, rx2_ref, sb_ref, rb_ref, after_ref, rx2_o):
        x, y, c = _position()
        for j in range(3):
            d = pltpu.make_async_remote_copy(src_ref=tx_ref.at[j], dst_ref=rx2_ref.at[j], send_sem=sb_ref.at[j],
                                             recv_sem=rb_ref.at[j], device_id=(x, y, c), device_id_type=MESH)
            d.wait_recv()
            d.wait_send()

    return pl.pallas_call(
        body, name="rsa_level2_wait", out_shape=_hbm_like(rx2),
        in_specs=(HBM_SPEC, HBM_SPEC, SEM_SPEC, SEM_SPEC, ANY_SPEC), out_specs=HBM_SPEC,
        input_output_aliases={1: 0},
        compiler_params=pltpu.CompilerParams(has_side_effects=SPLIT_EFFECT),
    )(tx, rx2, sb, rb, after)


def _adamw_math(w, g, m, v):
    m = ADAM_B1 * m + (1.0 - ADAM_B1) * g
    v = ADAM_B2 * v + (1.0 - ADAM_B2) * (g * g)
    m_hat = m / (1.0 - ADAM_B1 ** ADAM_STEP)
    v_hat = v / (1.0 - ADAM_B2 ** ADAM_STEP)
    delta = -ADAM_LR * (m_hat / (jnp.sqrt(v_hat) + ADAM_EPS) + ADAM_WD * w)
    return delta, m, v


def _adamw_big(pieces, ws, ms, vs, own, landed, name, in_flight=None):
    npiece = len(pieces)
    nsent = len(landed) if in_flight else 0
    nland = sum(a.shape[0] if a.ndim == 3 else 1 for a in landed)
    rmax = max(PIECE_ROWS[k] for k in pieces)
    half = FS // 2

    def segments(k):
        if k in G1_PIECES:
            return [(hf * len(G1_PIECES) * half + k * half, hf * half, half) for hf in (0, 1)]
        return [(RSA_OFF[k], 0, PIECE_ROWS[k])]

    def body(*refs):
        ins = (refs[0:npiece], refs[npiece:2 * npiece], refs[2 * npiece:3 * npiece])
        own_ref = refs[3 * npiece]
        nin = 3 * npiece + 1 + len(landed)
        land_refs = []
        for ref, a in zip(refs[3 * npiece + 1:nin], landed):
            land_refs += [ref.at[j] for j in range(a.shape[0])] if a.ndim == 3 else [ref]
        if in_flight:
            sent_refs, (ssem, rsem) = refs[nin:nin + nsent], refs[nin + nsent:nin + nsent + 2]
            nin += nsent + 3
        out_refs = refs[nin:nin + 4 * npiece]
        inb, landb, outb, in_sems, land_sems, out_sems = refs[nin + 4 * npiece + nsent:]

        def loads(i):
            s, k = i % 2, pieces[i]
            r = PIECE_ROWS[k]
            cps = [pltpu.make_async_copy(ins[q][i].at[0], inb.at[s, q, pl.ds(0, r), :], in_sems.at[4 * s + q])
                   for q in range(3)]
            waits, late = list(cps), []
            for src0, dst0, n in segments(k):
                cps.append(pltpu.make_async_copy(own_ref.at[pl.ds(src0, n), :], inb.at[s, 3, pl.ds(dst0, n), :],
                                                 in_sems.at[4 * s + 3]))
                for p in range(nland):
                    late.append(pltpu.make_async_copy(land_refs[p].at[pl.ds(src0, n), :],
                                                      landb.at[s, p, pl.ds(dst0, n), :], land_sems.at[nland * s + p]))
            own_rows = inb.at[s, 3, pl.ds(0, r), :]
            waits.append(pltpu.make_async_copy(own_rows, own_rows, in_sems.at[4 * s + 3]))
            for p in range(nland):
                rows = landb.at[s, p, pl.ds(0, r), :]
                waits.append(pltpu.make_async_copy(rows, rows, land_sems.at[nland * s + p]))
            return cps, late, waits

        def stores(i):
            s, r = i % 2, PIECE_ROWS[pieces[i]]
            return [pltpu.make_async_copy(outb.at[s, q, pl.ds(0, r), :], out_refs[q * npiece + i].at[0],
                                          out_sems.at[4 * s + q]) for q in range(4)]

        ahead = min(2, npiece) if in_flight else 1
        for i in range(ahead):
            for cp in loads(i)[0]:
                cp.start()
        if in_flight:
            x, y, c = _position()
            for j in range(nsent):
                d = pltpu.make_async_remote_copy(src_ref=sent_refs[j], dst_ref=refs[3 * npiece + 1 + j],
                                                 send_sem=ssem.at[j], recv_sem=rsem.at[j], device_id=(x, y, c),
                                                 device_id_type=MESH)
                d.wait_recv()
                d.wait_send()
        for cp in loads(0)[1]:
            cp.start()
        for i in range(npiece):
            s, r = i % 2, PIECE_ROWS[pieces[i]]
            if i + 1 < npiece:
                first, late, _ = loads(i + 1)
                for cp in late if i + 1 < ahead else first + late:
                    cp.start()
            for cp in loads(i)[2]:
                cp.wait()
            if i >= 2:
                for cp in stores(i - 2):
                    cp.wait()
            g = inb[s, 3, 0:r, :]
            for p in range(nland):
                g = g + landb[s, p, 0:r, :].astype(f32)
            d, nm, nv = _adamw_math(inb[s, 0, 0:r, :], g, inb[s, 1, 0:r, :], inb[s, 2, 0:r, :])
            outb[s, 0, 0:r, :] = g
            outb[s, 1, 0:r, :] = d
            outb[s, 2, 0:r, :] = nm
            outb[s, 3, 0:r, :] = nv
            for cp in stores(i):
                cp.start()
        for i in range(max(npiece - 2, 0), npiece):
            for cp in stores(i):
                cp.wait()

    hbm = pl.BlockSpec(memory_space=pl.ANY)
    in_specs, out_specs = [hbm] * (3 * npiece + 1), [hbm] * (4 * npiece)
    out_shape = [SDS(w.shape, f32) for _ in range(4) for w in ws]
    args, aliases, effect = [*ws, *ms, *vs, own], {}, False
    if in_flight:
        ssem, rsem, sent, after = in_flight
        in_specs += [HBM_SPEC] * (2 * nsent) + [SEM_SPEC, SEM_SPEC, hbm]
        args += [_in_hbm(a) for a in (*landed, *sent)] + [ssem, rsem, after]
        out_specs += [HBM_SPEC] * nsent
        out_shape += [_hbm_like(a) for a in landed]
        aliases = {3 * npiece + 1 + j: 4 * npiece + j for j in range(nsent)}
        effect = SPLIT_EFFECT
    else:
        in_specs += [hbm] * len(landed)
        args += list(landed)
    outs = pl.pallas_call(
        body, in_specs=in_specs, out_specs=out_specs, out_shape=tuple(out_shape), input_output_aliases=aliases,
        scratch_shapes=[pltpu.VMEM((2, 4, rmax, D), f32), pltpu.VMEM((2, nland, rmax, D), bf16),
                        pltpu.VMEM((2, 4, rmax, D), f32),
                        pltpu.SemaphoreType.DMA((8,)), pltpu.SemaphoreType.DMA((2 * nland,)),
                        pltpu.SemaphoreType.DMA((8,))],
        compiler_params=pltpu.CompilerParams(has_side_effects=effect, vmem_limit_bytes=VMEM_LIMIT_V7X),
        name=name)(*args)
    return [list(outs[q * npiece:(q + 1) * npiece]) for q in range(4)]


def _adamw_small(ws, ms, vs, gs, name):
    n = len(ws)

    def body(*refs):
        w_refs, m_refs, v_refs, g_refs = refs[0:n], refs[n:2 * n], refs[2 * n:3 * n], refs[3 * n:4 * n]
        outs = refs[4 * n:]
        for i in range(n):
            d, nm, nv = _adamw_math(w_refs[i][...], g_refs[i][...], m_refs[i][...], v_refs[i][...])
            outs[i][...] = d
            outs[n + i][...] = nm
            outs[2 * n + i][...] = nv

    outs = pl.pallas_call(
        body, out_shape=tuple(SDS(w.shape, f32) for _ in range(3) for w in ws), name=name)(*ws, *ms, *vs, *gs)
    return [list(outs[q * n:(q + 1) * n]) for q in range(3)]


WEIGHTS = ("ffn1_norm", "ffn1_w_gate", "ffn1_w_up", "ffn1_w_down", "mix_norm", "w_in", "q_norm", "k_norm",
           "attn_sinks", "rel_bias", "pool_w", "pool_scale", "w_out", "ffn2_norm", "ffn2_w_gate", "ffn2_w_up",
           "ffn2_w_down")
BIG = (("ffn1_w_gate", True), ("ffn1_w_up", True), ("ffn1_w_down", False), ("w_in", True), ("w_out", False),
       ("ffn2_w_gate", True), ("ffn2_w_up", True), ("ffn2_w_down", False))


def kernel(x, ffn1_norm, ffn1_w_gate, ffn1_w_up, ffn1_w_down, mix_norm, w_in, q_norm, k_norm, attn_sinks, rel_bias, pool_w, pool_scale, w_out, ffn2_norm, ffn2_w_gate, ffn2_w_up, ffn2_w_down, loss_target, m_ffn1_norm, m_ffn1_w_gate, m_ffn1_w_up, m_ffn1_w_down, m_mix_norm, m_w_in, m_q_norm, m_k_norm, m_attn_sinks, m_rel_bias, m_pool_w, m_pool_scale, m_w_out, m_ffn2_norm, m_ffn2_w_gate, m_ffn2_w_up, m_ffn2_w_down, v_ffn1_norm, v_ffn1_w_gate, v_ffn1_w_up, v_ffn1_w_down, v_mix_norm, v_w_in, v_q_norm, v_k_norm, v_attn_sinks, v_rel_bias, v_pool_w, v_pool_scale, v_w_out, v_ffn2_norm, v_ffn2_w_gate, v_ffn2_w_up, v_ffn2_w_down):
    args = dict(locals())
    w = {n: args[n] for n in WEIGHTS}
    m = {n: args["m_" + n] for n in WEIGHTS}
    v = {n: args["v_" + n] for n in WEIGHTS}

    as_rows = lambda a, tr: jnp.swapaxes(a, 1, 2) if tr else a
    shard = jnp.concatenate([as_rows(w[n], tr)[0].astype(bf16) for n, tr in BIG], axis=0)
    exchanges = _GatheredWeights(shard, x[0], ffn1_norm)
    (dh1, dx1), (dw1, _, _, _), small = _local_step(
        x[0], loss_target[0], exchanges, ffn1_norm, mix_norm, ffn2_norm, q_norm, k_norm, attn_sinks,
        rel_bias, pool_w[0], pool_scale)

    nrows1 = len(G1_PIECES) * FS
    for_x, for_y, own1, small_tot, gx = _reduce_scatter_ffn1_head(dw1, _pack_small(small),
                                                                  (dh1, x[0], ffn1_norm, dx1))
    ssem, rsem, for_x, for_y, from_x, from_y, small_tot = _rs1_tail_start(
        for_x, for_y, lax.empty((nrows1, D), bf16), lax.empty((nrows1, D), bf16), small_tot)
    own_rest, landed_rest = exchanges.mix_ffn2_grads_parts(small_tot)

    grads, deltas, new_m, new_v = {}, {}, {}, {}
    rest = [k for k in range(len(BIG)) if k not in G1_PIECES]
    rows_of = lambda t, ks: [as_rows(t[BIG[k][0]], BIG[k][1]) for k in ks]
    rest_out = _adamw_big(rest, rows_of(w, rest), rows_of(m, rest), rows_of(v, rest), own_rest, [landed_rest],
                          "adamw_rest")
    ffn1 = list(G1_PIECES)
    ffn1_out = _adamw_big(ffn1, rows_of(w, ffn1), rows_of(m, ffn1), rows_of(v, ffn1), own1, [from_x, from_y],
                          "adamw_ffn1", in_flight=(ssem, rsem, [for_x, for_y], rest_out[0][0]))
    for ks, out in ((rest, rest_out), (ffn1, ffn1_out)):
        for i, k in enumerate(ks):
            n, tr = BIG[k]
            grads[n], deltas[n], new_m[n], new_v[n] = [as_rows(o[i], tr) for o in out]
    small_names = [n for n in SMALL_NAMES if n != "loss"]
    for n in small_names:
        grads[n] = _unpack_small(small_tot, n)
    ds, nms, nvs = _adamw_small([w[n] for n in small_names], [m[n] for n in small_names], [v[n] for n in small_names],
                                [grads[n] for n in small_names], "adamw_small")
    for i, n in enumerate(small_names):
        deltas[n], new_m[n], new_v[n] = ds[i], nms[i], nvs[i]
    loss = small_tot[LOSS_ROW, 0]
    return (loss, gx, *[grads[n] for n in WEIGHTS], *[deltas[n] for n in WEIGHTS],
            *[new_m[n] for n in WEIGHTS], *[new_v[n] for n in WEIGHTS])
```

```python
import jax
import jax.numpy as jnp
import numpy as np
from jax import lax
from jax.experimental import pallas as pl
from jax.experimental.pallas import tpu as pltpu

f32, bf16, i32 = jnp.float32, jnp.bfloat16, jnp.int32
SDS = jax.ShapeDtypeStruct

D = 1024
F = 2816
HD = 64
NH = 8
NKV = 2
GQA = NH // NKV
DATTN = NH * HD
DKV = NKV * HD
DPOOL = 512
POOL_WINDOWS = (2, 4, 8, 16)
PGD = DPOOL // len(POOL_WINDOWS)
DIN = DATTN + 2 * DKV + DPOOL
DMIX = DATTN + DPOOL
BLK = 128
NBUCK = 32
MAX_DISTANCE = 128
EPS = 1e-6
NEG = -1e30
SCALE = HD ** -0.5

ADAM_LR, ADAM_B1, ADAM_B2, ADAM_EPS, ADAM_WD, ADAM_STEP = 0.001, 0.9, 0.999, 1e-08, 0.01, 10

NDEV = 8
FS = F // NDEV
INS = DIN // NDEV
OUTS = DMIX // NDEV
PIECE_ROWS = (FS, FS, FS, INS, OUTS, FS, FS, FS)
PIECE_OFF = tuple(int(v) for v in np.cumsum((0,) + PIECE_ROWS[:-1]))
PACK_ROWS = sum(PIECE_ROWS)

VMEM_LIMIT_V7X = 56 * 1024 * 1024

MESH = pl.DeviceIdType.MESH


def _cparams(sem=None, vmem=None):
    return pltpu.CompilerParams(dimension_semantics=sem, vmem_limit_bytes=vmem)


def _nt(a, b):
    return lax.dot_general(a, b, (((1,), (1,)), ((), ())), preferred_element_type=f32)


def _tn(a, b):
    return lax.dot_general(a, b, (((0,), (0,)), ((), ())), preferred_element_type=f32)


def _nn(a, b):
    return jnp.dot(a, b, preferred_element_type=f32)


def _sigmoid(x):
    return 1.0 / (1.0 + jnp.exp(-x))


def _norm_fwd(x, g, name):
    T = x.shape[0]
    tm = min(512, T)

    def body(x_ref, g_ref, h_ref):
        xv = x_ref[...]
        r = lax.rsqrt(jnp.mean(xv * xv, axis=-1, keepdims=True) + EPS)
        h_ref[...] = (xv * r * g_ref[...]).astype(bf16)

    return pl.pallas_call(
        body, grid=(T // tm,),
        in_specs=[pl.BlockSpec((tm, D), lambda i: (i, 0)), pl.BlockSpec((1, D), lambda i: (0, 0))],
        out_specs=pl.BlockSpec((tm, D), lambda i: (i, 0)),
        out_shape=SDS((T, D), bf16), name=name)(x, g)


FFN_ROW_CHUNK = 256


def _ffn_tiles(T):
    return min(1024, T), 256


def _ffn_fwd(h, w, x, target, next_gain, name):
    T = h.shape[0]
    tm, tf = _ffn_tiles(T)
    nf = F // tf
    with_loss = target is not None
    assert with_loss != (next_gain is not None)

    def body(*refs):
        if with_loss:
            h_ref, w_ref, x_hbm, t_hbm, xo_ref, g_ref, u_ref, dyb_ref, loss_ref, tbuf, sem = refs
        else:
            h_ref, w_ref, x_hbm, gain_ref, xo_ref, g_ref, u_ref, hn_ref, sem = refs
        fi = pl.program_id(0)

        @pl.when(fi == 0)
        def _():
            cp = pltpu.make_async_copy(x_hbm, xo_ref, sem)
            cp.start()
            cp.wait()

        wgu = w_ref[0:2].reshape(2 * tf, D)
        for r in range(0, T, tm):
            rows = slice(r, r + tm)
            gu = _nt(h_ref[rows, :], wgu)
            gate, up = gu[:, :tf], gu[:, tf:]
            act = gate * _sigmoid(gate) * up
            g_ref[0, rows, :] = gate.astype(bf16)
            u_ref[0, rows, :] = up.astype(bf16)
            xo_ref[rows, :] += _nn((0.5 * act).astype(bf16), w_ref[2])

        if with_loss:
            @pl.when(fi == nf - 1)
            def _():
                lanes = jnp.zeros((1, 128), f32)
                for r in range(0, T, tm):
                    rows = slice(r, r + tm)
                    cp = pltpu.make_async_copy(t_hbm.at[pl.ds(r, tm), :], tbuf, sem)
                    cp.start()
                    cp.wait()
                    e = xo_ref[rows, :] - tbuf[...]
                    dy = e * (1.0 / D)
                    xo_ref[rows, :] = dy
                    dyb_ref[rows, :] = (0.5 * dy).astype(bf16)
                    col = jnp.sum(e * e, axis=0, keepdims=True) * (0.5 / D)
                    for k in range(D // 128):
                        lanes = lanes + col[:, 128 * k:128 * (k + 1)]
                loss_ref[...] = lanes
        else:
            @pl.when(fi == nf - 1)
            def _():
                for r in range(0, T, FFN_ROW_CHUNK):
                    rows = slice(r, r + FFN_ROW_CHUNK)
                    xv = xo_ref[rows, :]
                    rstd = lax.rsqrt(jnp.mean(xv * xv, axis=-1, keepdims=True) + EPS)
                    hn_ref[rows, :] = (xv * rstd * gain_ref[...]).astype(bf16)

    tok = pl.BlockSpec((T, D), lambda f: (0, 0))
    act_spec = pl.BlockSpec((1, T, tf), lambda f: (f, 0, 0))
    hbm = pl.BlockSpec(memory_space=pl.ANY)
    in_specs = [tok, pl.BlockSpec((3, tf, D), lambda f: (0, f, 0)), hbm]
    out_specs = [tok, act_spec, act_spec]
    out_shape = [SDS((T, D), f32), SDS((nf, T, tf), bf16), SDS((nf, T, tf), bf16)]
    scratch = [pltpu.SemaphoreType.DMA]
    args = [h, w, x]
    if with_loss:
        in_specs.append(hbm)
        args.append(target)
        out_specs += [tok, pl.BlockSpec((1, 128), lambda f: (0, 0))]
        out_shape += [SDS((T, D), bf16), SDS((1, 128), f32)]
        scratch = [pltpu.VMEM((tm, D), f32)] + scratch
    else:
        in_specs.append(pl.BlockSpec((1, D), lambda f: (0, 0)))
        args.append(next_gain)
        out_specs.append(tok)
        out_shape.append(SDS((T, D), bf16))
    return pl.pallas_call(
        body, grid=(nf,), in_specs=in_specs, out_specs=out_specs, out_shape=tuple(out_shape), scratch_shapes=scratch,
        compiler_params=_cparams(("arbitrary",), VMEM_LIMIT_V7X), name=name)(*args)


NORM_BWD_ROWS = 512


def _norm_bwd_scratch(dh_in_hbm, with_bf16):
    buf = lambda dt: pltpu.VMEM((2, NORM_BWD_ROWS, D), dt)
    return [buf(f32), buf(f32), buf(f32) if dh_in_hbm else None, buf(f32), buf(bf16) if with_bf16 else None,
            pltpu.SemaphoreType.DMA((6,)), pltpu.SemaphoreType.DMA((4,))]


def _norm_bwd_rows(T, dh_src, x_hbm, gain_ref, dr_hbm, dx_hbm, dxb_hbm, xbuf, rbuf, hbuf, obuf, obb, in_sems, out_sems):
    tm = min(NORM_BWD_ROWS, T)
    nchunk = T // tm

    def loads(i):
        s, rows = i % 2, pl.ds(i * tm, tm)
        cps = [pltpu.make_async_copy(x_hbm.at[rows, :], xbuf.at[s, pl.ds(0, tm), :], in_sems.at[3 * s]),
               pltpu.make_async_copy(dr_hbm.at[rows, :], rbuf.at[s, pl.ds(0, tm), :], in_sems.at[3 * s + 1])]
        if hbuf is not None:
            cps.append(pltpu.make_async_copy(dh_src.at[rows, :], hbuf.at[s, pl.ds(0, tm), :], in_sems.at[3 * s + 2]))
        return cps

    def stores(i):
        s, rows = i % 2, pl.ds(i * tm, tm)
        cps = [pltpu.make_async_copy(obuf.at[s, pl.ds(0, tm), :], dx_hbm.at[rows, :], out_sems.at[2 * s])]
        if dxb_hbm is not None:
            cps.append(pltpu.make_async_copy(obb.at[s, pl.ds(0, tm), :], dxb_hbm.at[rows, :], out_sems.at[2 * s + 1]))
        return cps

    for cp in loads(0):
        cp.start()
    dg = jnp.zeros((1, D), f32)
    for i in range(nchunk):
        s = i % 2
        if i + 1 < nchunk:
            for cp in loads(i + 1):
                cp.start()
        for cp in loads(i):
            cp.wait()
        if i >= 2:
            for cp in stores(i - 2):
                cp.wait()
        xv = xbuf[s, 0:tm, :]
        rstd = lax.rsqrt(jnp.mean(xv * xv, axis=-1, keepdims=True) + EPS)
        xh = xv * rstd
        dhv = hbuf[s, 0:tm, :] if hbuf is not None else dh_src[i * tm:(i + 1) * tm, :]
        dxh = dhv * gain_ref[...]
        dx = rbuf[s, 0:tm, :] + rstd * (dxh - xh * jnp.mean(dxh * xh, axis=-1, keepdims=True))
        obuf[s, 0:tm, :] = dx
        if dxb_hbm is not None:
            obb[s, 0:tm, :] = dx.astype(bf16)
        dg = dg + jnp.sum(dhv * xh, axis=0, keepdims=True)
        for cp in stores(i):
            cp.start()
    for i in range(max(nchunk - 2, 0), nchunk):
        for cp in stores(i):
            cp.wait()
    return dg


def _ffn_bwd(dob, h, gate, up, w, norm, name):
    T = h.shape[0]
    _, tf = _ffn_tiles(T)
    nf = F // tf
    nin = 5 if norm is None else 8
    nout = 2 if norm is None else 4

    def body(*refs):
        do_hbm, h_hbm, g_ref, u_ref, w_ref = refs[:5]
        dw_ref = refs[nin + nout - 1]
        do_v, h_v, dh_acc, dgu_s, act_s, sems = refs[nin + nout:nin + nout + 6]
        fi = pl.program_id(0)

        @pl.when(fi == 0)
        def _():
            loads = [pltpu.make_async_copy(do_hbm, do_v, sems.at[0]), pltpu.make_async_copy(h_hbm, h_v, sems.at[1])]
            for cp in loads:
                cp.start()
            dh_acc[...] = jnp.zeros_like(dh_acc)
            for cp in loads:
                cp.wait()

        wgu = w_ref[0:2].reshape(2 * tf, D)
        for r in range(0, T, FFN_ROW_CHUNK):
            rows = slice(r, r + FFN_ROW_CHUNK)
            dov = do_v[rows, :]
            gv = g_ref[0, rows, :].astype(f32)
            uv = u_ref[0, rows, :].astype(f32)
            sg = _sigmoid(gv)
            sil = gv * sg
            dact = _nt(dov, w_ref[2])
            dup = dact * sil
            dgate = dact * uv * (sg * (1.0 + gv * (1.0 - sg)))
            dgu = jnp.concatenate([dgate.astype(bf16), dup.astype(bf16)], axis=1)
            dgu_s[rows, :] = dgu
            act_s[rows, :] = (sil * uv).astype(bf16)
            dh_acc[rows, :] += _nn(dgu, wgu)
        dw_ref[0:2] = _tn(dgu_s[...], h_v[...]).reshape(2, tf, D).astype(bf16)
        dw_ref[2] = _tn(act_s[...], do_v[...]).astype(bf16)

        @pl.when(fi == nf - 1)
        def _():
            if norm is None:
                out = pltpu.make_async_copy(dh_acc, refs[nin], sems.at[0])
                out.start()
                out.wait()
            else:
                x_hbm, gain_ref, dr_hbm, dx_hbm, dxb_hbm, dg_ref = refs[5:11]
                xbuf, rbuf, obuf, obb, in_sems, out_sems = refs[nin + nout + 6:]
                dg_ref[...] = _norm_bwd_rows(T, dh_acc, x_hbm, gain_ref, dr_hbm, dx_hbm, dxb_hbm,
                                             xbuf, rbuf, None, obuf, obb, in_sems, out_sems)

    act_spec = pl.BlockSpec((1, T, tf), lambda f: (f, 0, 0))
    wspec = pl.BlockSpec((3, tf, D), lambda f: (0, f, 0))
    vec = pl.BlockSpec((1, D), lambda f: (0, 0))
    hbm = pl.BlockSpec(memory_space=pl.ANY)
    in_specs, out_specs = [hbm, hbm, act_spec, act_spec, wspec], [hbm, wspec]
    out_shape, args = [SDS((T, D), f32), SDS((3, F, D), bf16)], [dob, h, gate, up, w]
    scratch = [pltpu.VMEM((T, D), bf16), pltpu.VMEM((T, D), bf16), pltpu.VMEM((T, D), f32),
               pltpu.VMEM((T, 2 * tf), bf16), pltpu.VMEM((T, tf), bf16), pltpu.SemaphoreType.DMA((2,))]
    if norm is not None:
        in_specs += [hbm, vec, hbm]
        out_specs = [hbm, hbm, vec, wspec]
        out_shape = [SDS((T, D), f32), SDS((T, D), bf16), SDS((1, D), f32), SDS((3, F, D), bf16)]
        args += list(norm)
        scratch += [sc for sc in _norm_bwd_scratch(False, True) if sc is not None]
    return pl.pallas_call(
        body, grid=(nf,), in_specs=in_specs, out_specs=out_specs, out_shape=tuple(out_shape), scratch_shapes=scratch,
        compiler_params=_cparams(("arbitrary",), VMEM_LIMIT_V7X), name=name)(*args)


def _in_proj_fwd(h, wint, name):
    T = h.shape[0]
    tm = min(512, T)

    def body(h_ref, w_ref, z_ref):
        z_ref[...] = _nt(h_ref[...], w_ref[...])

    return pl.pallas_call(
        body, grid=(T // tm,),
        in_specs=[pl.BlockSpec((tm, D), lambda i: (i, 0)), pl.BlockSpec((DIN, D), lambda i: (0, 0))],
        out_specs=pl.BlockSpec((tm, DIN), lambda i: (i, 0)),
        out_shape=SDS((T, DIN), f32), name=name)(h, wint)


def _in_proj_bwd(dz, wint, h, norm, out_scale, name):
    x, g, dres = norm
    T = h.shape[0]
    tm = min(512, T)
    nt = T // tm

    def body(dz_ref, w_ref, h_ref, x_ref, g_ref, dr_ref, dx_ref, dxb_ref, dg_ref, dw_ref, acc):
        i = pl.program_id(0)
        dzb = dz_ref[...].astype(bf16)
        dhv = _nn(dzb, w_ref[...])
        part = _tn(dzb, h_ref[...])
        xv = x_ref[...]
        rstd = lax.rsqrt(jnp.mean(xv * xv, axis=-1, keepdims=True) + EPS)
        xh = xv * rstd
        dxh = dhv * g_ref[...]
        dx = dr_ref[...] + rstd * (dxh - xh * jnp.mean(dxh * xh, axis=-1, keepdims=True))
        dx_ref[...] = dx
        dxb_ref[...] = (out_scale * dx).astype(bf16)
        dg = jnp.sum(dhv * xh, axis=0, keepdims=True)

        @pl.when(i == 0)
        def _():
            acc[...] = part
            dg_ref[...] = dg

        @pl.when(i > 0)
        def _():
            acc[...] += part
            dg_ref[...] += dg

        @pl.when(i == nt - 1)
        def _():
            dw_ref[...] = acc[...].astype(bf16)

    wspec = pl.BlockSpec((DIN, D), lambda i: (0, 0))
    tok = pl.BlockSpec((tm, D), lambda i: (i, 0))
    vec = pl.BlockSpec((1, D), lambda i: (0, 0))
    return pl.pallas_call(
        body, grid=(nt,),
        in_specs=[pl.BlockSpec((tm, DIN), lambda i: (i, 0)), wspec, tok, tok, vec, tok],
        out_specs=[tok, tok, vec, wspec],
        out_shape=(SDS((T, D), f32), SDS((T, D), bf16), SDS((1, D), f32), SDS((DIN, D), bf16)),
        scratch_shapes=[pltpu.VMEM((DIN, D), f32)],
        compiler_params=_cparams(("arbitrary",)), name=name)(dz, wint, h, x, g, dres)


def _out_proj_fwd(ymix, wout, x, g, name):
    T = x.shape[0]
    tm = min(512, T)

    def body(y_ref, w_ref, x_ref, g_ref, o_ref, h_ref):
        o = x_ref[...] + _nn(y_ref[...], w_ref[...])
        o_ref[...] = o
        r = lax.rsqrt(jnp.mean(o * o, axis=-1, keepdims=True) + EPS)
        h_ref[...] = (o * r * g_ref[...]).astype(bf16)

    tok = pl.BlockSpec((tm, D), lambda i: (i, 0))
    return pl.pallas_call(
        body, grid=(T // tm,),
        in_specs=[pl.BlockSpec((tm, DMIX), lambda i: (i, 0)), pl.BlockSpec((DMIX, D), lambda i: (0, 0)), tok,
                  pl.BlockSpec((1, D), lambda i: (0, 0))],
        out_specs=[tok, tok], out_shape=(SDS((T, D), f32), SDS((T, D), bf16)), name=name)(ymix, wout, x, g)


def _out_proj_bwd(dxb, wout, ymix, name):
    T = dxb.shape[0]
    tm = min(512, T)
    nt = T // tm

    def body(dx_ref, w_ref, y_ref, dy_ref, dw_ref, acc):
        i = pl.program_id(0)
        dxv = dx_ref[...]
        dy_ref[...] = _nt(dxv, w_ref[...])
        part = _tn(y_ref[...], dxv)

        @pl.when(i == 0)
        def _():
            acc[...] = part

        @pl.when(i > 0)
        def _():
            acc[...] += part

        @pl.when(i == nt - 1)
        def _():
            dw_ref[...] = acc[...].astype(bf16)

    wspec = pl.BlockSpec((DMIX, D), lambda i: (0, 0))
    return pl.pallas_call(
        body, grid=(nt,),
        in_specs=[pl.BlockSpec((tm, D), lambda i: (i, 0)), wspec, pl.BlockSpec((tm, DMIX), lambda i: (i, 0))],
        out_specs=[pl.BlockSpec((tm, DMIX), lambda i: (i, 0)), wspec],
        out_shape=(SDS((T, DMIX), f32), SDS((DMIX, D), bf16)),
        scratch_shapes=[pltpu.VMEM((DMIX, D), f32)],
        compiler_params=_cparams(("arbitrary",)), name=name)(dxb, wout, ymix)


def _t5_bucket_table():
    ql = np.arange(BLK)[:, None]
    kl = np.arange(2 * BLK)[None, :]
    n = np.maximum(ql + BLK - kl, 0)
    max_exact = NBUCK // 2
    large = max_exact + (np.log(np.maximum(n, 1) / max_exact) / np.log(MAX_DISTANCE / max_exact)
                         * (NBUCK - max_exact)).astype(np.int32)
    large = np.minimum(large, NBUCK - 1)
    return np.where(n < max_exact, n, large).astype(np.int32)


def _fill_bias(bk_ref, rb_ref, bias_scr):
    bk = bk_ref[...]
    for h in range(NH):
        def step(b, acc, h=h):
            return acc + jnp.where(bk == b, rb_ref[b, h], 0.0)
        bias_scr[h] = lax.fori_loop(0, NBUCK, step, jnp.zeros((BLK, 2 * BLK), f32))


MIX_SUB = 4


class _Window:
    def __init__(self, zc_ref, zp_ref, n, s):
        self.blk = n * MIX_SUB + s
        self.cur = lambda a, b: zc_ref[s * BLK:(s + 1) * BLK, a:b]
        self.prev = (lambda a, b: zp_ref[:, a:b]) if s == 0 else (lambda a, b: zc_ref[(s - 1) * BLK:s * BLK, a:b])


def _attn_qkv(win, kh, qg, kg):
    kc = DATTN + HD * kh
    vc = DATTN + DKV + HD * kh
    kx = jnp.concatenate([win.prev(kc, kc + HD), win.cur(kc, kc + HD)], axis=0)
    vx = jnp.concatenate([win.prev(vc, vc + HD), win.cur(vc, vc + HD)], axis=0)
    qx = jnp.concatenate([win.cur(HD * (GQA * kh + g), HD * (GQA * kh + g + 1)) for g in range(GQA)], axis=0)
    rq = lax.rsqrt(jnp.mean(qx * qx, axis=-1, keepdims=True) + EPS)
    rk = lax.rsqrt(jnp.mean(kx * kx, axis=-1, keepdims=True) + EPS)
    qhat, khat = qx * rq, kx * rk
    return dict(qhat=qhat, khat=khat, rq=rq, rk=rk, qsb=(qhat * (qg * SCALE)).astype(bf16),
                knb=(khat * kg).astype(bf16), vb=vx.astype(bf16))


def _window_masks(n):
    row = lax.broadcasted_iota(i32, (GQA * BLK, 2 * BLK), 0) & (BLK - 1)
    col = lax.broadcasted_iota(i32, (GQA * BLK, 2 * BLK), 1)
    band = (col > row) & (col <= row + BLK)
    return band & ((col >= BLK) | (n > 0)), band


def _attn_probs(a, kh, sk_ref, bias_scr, mask):
    s = _nt(a["qsb"], a["knb"]) + bias_scr[GQA * kh:GQA * (kh + 1)].reshape(GQA * BLK, 2 * BLK)
    s = jnp.where(mask, s, NEG)
    ridx = lax.broadcasted_iota(i32, (GQA * BLK, 1), 0)
    sink = jnp.full((GQA * BLK, 1), sk_ref[GQA * kh + GQA - 1], f32)
    for g in range(GQA - 2, -1, -1):
        sink = jnp.where(ridx < (g + 1) * BLK, sk_ref[GQA * kh + g], sink)
    m = jnp.maximum(jnp.max(s, axis=-1, keepdims=True), sink)
    e = jnp.exp(s - m)
    den = jnp.sum(e, axis=-1, keepdims=True) + jnp.exp(sink - m)
    return e / den


POOL_STEPS = {2: (1,), 4: (1, 2), 8: (1, 2, 4), 16: (1, 2, 4, 8)}


def _pool_group(win, g, w):
    n = win.blk
    c0 = DATTN + 2 * DKV + PGD * g
    uc = win.cur(c0, c0 + PGD)
    up = jnp.where(n > 0, win.prev(c0, c0 + PGD), 0.0)
    sm = jnp.concatenate([up, uc], axis=0)
    for k in POOL_STEPS[w]:
        sm = sm + pltpu.roll(sm, k, axis=0)
    pos = n * BLK + lax.broadcasted_iota(i32, (BLK, 1), 0) + 1
    cnt = jnp.minimum(pos, w).astype(f32)
    return sm[BLK:2 * BLK] / cnt - uc, cnt


def _mix_fwd(z, qg, kg, sinks, relb, bucket, pool_w, pscale, name):
    T = z.shape[0]
    step_rows = MIX_SUB * BLK
    nsteps = T // step_rows

    def body(zc_ref, zp_ref, qg_ref, kg_ref, sk_ref, rb_ref, bk_ref, pw_ref, ps_ref, y_ref, p_ref, bias_scr, yacc):
        n = pl.program_id(0)

        @pl.when(n == 0)
        def _():
            _fill_bias(bk_ref, rb_ref, bias_scr)

        first_mask, mask = _window_masks(n)
        for s in range(MIX_SUB):
            win = _Window(zc_ref, zp_ref, n, s)
            rows = slice(s * BLK, (s + 1) * BLK)
            for kh in range(NKV):
                a = _attn_qkv(win, kh, qg_ref[...], kg_ref[...])
                pb = _attn_probs(a, kh, sk_ref, bias_scr, first_mask if s == 0 else mask).astype(bf16)
                p_ref[s, GQA * kh:GQA * (kh + 1)] = pb.reshape(GQA, BLK, 2 * BLK)
                o = _nn(pb, a["vb"])
                for g in range(GQA):
                    hc = HD * (GQA * kh + g)
                    yacc[rows, hc:hc + HD] = o[g * BLK:(g + 1) * BLK]
            for g, w in enumerate(POOL_WINDOWS):
                pooled, _ = _pool_group(win, g, w)
                yp = _nn(pooled.astype(bf16), pw_ref[g].astype(bf16)) * ps_ref[:, PGD * g:PGD * (g + 1)]
                yacc[rows, DATTN + PGD * g:DATTN + PGD * (g + 1)] = yp
        y_ref[...] = yacc[...].astype(bf16)

    full = lambda *shape: pl.BlockSpec(shape, lambda n: (0,) * len(shape))
    smem = pl.BlockSpec(memory_space=pltpu.SMEM)
    return pl.pallas_call(
        body, grid=(nsteps,),
        in_specs=[pl.BlockSpec((step_rows, DIN), lambda n: (n, 0)),
                  pl.BlockSpec((BLK, DIN), lambda n: (jnp.maximum(n * MIX_SUB - 1, 0), 0)),
                  full(1, HD), full(1, HD), smem, smem, full(BLK, 2 * BLK),
                  full(len(POOL_WINDOWS), PGD, PGD), full(1, DPOOL)],
        out_specs=[pl.BlockSpec((step_rows, DMIX), lambda n: (n, 0)),
                   pl.BlockSpec((MIX_SUB, NH, BLK, 2 * BLK), lambda n: (n, 0, 0, 0))],
        out_shape=(SDS((T, DMIX), bf16), SDS((T // BLK, NH, BLK, 2 * BLK), bf16)),
        scratch_shapes=[pltpu.VMEM((NH, BLK, 2 * BLK), f32), pltpu.VMEM((step_rows, DMIX), f32)],
        compiler_params=_cparams(("arbitrary",)), name=name)(z, z, qg, kg, sinks, relb, bucket, pool_w, pscale)


def _mix_bwd(z, dy, probs, qg, kg, relb, bucket, pool_w, pscale, name):
    T = z.shape[0]
    step_rows = MIX_SUB * BLK
    nsteps = T // step_rows

    def body(zc_ref, zp_ref, dy_ref, p_ref, qg_ref, kg_ref, bk_ref, pw_ref, ps_ref,
             dz_ref, dqg_ref, dkg_ref, dsk_ref, drb_ref, dpw_ref, dps_ref, dbias_scr):
        n = pl.program_id(0)

        @pl.when(n == 0)
        def _():
            dbias_scr[...] = jnp.zeros_like(dbias_scr)
            dqg_ref[...] = jnp.zeros_like(dqg_ref)
            dkg_ref[...] = jnp.zeros_like(dkg_ref)
            dpw_ref[...] = jnp.zeros_like(dpw_ref)
            dps_ref[...] = jnp.zeros_like(dps_ref)

        qg, kg = qg_ref[...], kg_ref[...]
        for s in range(MIX_SUB):
            win = _Window(zc_ref, zp_ref, n, s)
            blk = win.blk
            rows = pl.ds(pl.multiple_of(blk * BLK, BLK), BLK)
            prow = pl.ds(pl.multiple_of(jnp.maximum(blk - 1, 0) * BLK, BLK), BLK)
            dyr = slice(s * BLK, (s + 1) * BLK)

            def into_prev(fn, s=s):
                if s == 0:
                    pl.when(n > 0)(fn)
                else:
                    fn()

            for kh in range(NKV):
                a = _attn_qkv(win, kh, qg, kg)
                pb = p_ref[s, GQA * kh:GQA * (kh + 1)].reshape(GQA * BLK, 2 * BLK)
                p = pb.astype(f32)
                do = jnp.concatenate([dy_ref[dyr, HD * (GQA * kh + g):HD * (GQA * kh + g + 1)] for g in range(GQA)],
                                     axis=0).astype(bf16)
                dv = _tn(pb, do)
                dp = _nt(do, a["vb"])
                delta = jnp.sum(p * dp, axis=-1, keepdims=True)
                ds = p * (dp - delta)
                for g in range(GQA):
                    dbias_scr[GQA * kh + g] += ds[g * BLK:(g + 1) * BLK]
                dsb = ds.astype(bf16)
                dqn = _nn(dsb, a["knb"]) * SCALE
                dkn = _tn(dsb, a["qsb"])
                qhat, khat = a["qhat"], a["khat"]
                dqg_ref[...] += jnp.sum(dqn * qhat, axis=0, keepdims=True)
                dkg_ref[...] += jnp.sum(dkn * khat, axis=0, keepdims=True)
                dqh = dqn * qg
                dq = a["rq"] * (dqh - qhat * jnp.mean(dqh * qhat, axis=-1, keepdims=True))
                dkh = dkn * kg
                dk = a["rk"] * (dkh - khat * jnp.mean(dkh * khat, axis=-1, keepdims=True))
                kc = DATTN + HD * kh
                vc = DATTN + DKV + HD * kh
                for g in range(GQA):
                    hc = HD * (GQA * kh + g)
                    dz_ref[rows, hc:hc + HD] = dq[g * BLK:(g + 1) * BLK]
                dz_ref[rows, kc:kc + HD] = dk[BLK:2 * BLK]
                dz_ref[rows, vc:vc + HD] = dv[BLK:2 * BLK]

                def kv_prev(dk=dk, dv=dv, kc=kc, vc=vc, prow=prow):
                    dz_ref[prow, kc:kc + HD] += dk[0:BLK]
                    dz_ref[prow, vc:vc + HD] += dv[0:BLK]

                into_prev(kv_prev)

            for g, w in enumerate(POOL_WINDOWS):
                c0 = DATTN + 2 * DKV + PGD * g
                pooled, cnt = _pool_group(win, g, w)
                pb = pooled.astype(bf16)
                wb = pw_ref[g].astype(bf16)
                dyp = dy_ref[dyr, DATTN + PGD * g:DATTN + PGD * (g + 1)]
                ypre = _nn(pb, wb)
                dps_ref[:, PGD * g:PGD * (g + 1)] += jnp.sum(dyp * ypre, axis=0, keepdims=True)
                dyg = (dyp * ps_ref[:, PGD * g:PGD * (g + 1)]).astype(bf16)
                dpw_ref[g] += _tn(pb, dyg)
                dpooled = _nt(dyg, wb)
                due = jnp.concatenate([jnp.zeros((BLK, PGD), f32), dpooled / cnt], axis=0)
                for k in POOL_STEPS[w]:
                    due = due + pltpu.roll(due, 2 * BLK - k, axis=0)
                dz_ref[rows, c0:c0 + PGD] = due[BLK:2 * BLK] - dpooled

                def pool_prev(due=due, c0=c0, prow=prow):
                    dz_ref[prow, c0:c0 + PGD] += due[0:BLK]

                into_prev(pool_prev)

        @pl.when(n == nsteps - 1)
        def _():
            bk = bk_ref[...]
            ri = lax.broadcasted_iota(i32, (NBUCK, NH), 0)
            ci = lax.broadcasted_iota(i32, (NBUCK, NH), 1)

            def step(b, acc):
                for h in range(NH):
                    sel = jnp.where(bk == b, dbias_scr[h], 0.0)
                    tot = jnp.sum(jnp.sum(sel, axis=1, keepdims=True), axis=0, keepdims=True)
                    acc = acc + jnp.where((ri == b) & (ci == h), tot, 0.0)
                return acc

            drb_ref[...] = lax.fori_loop(0, NBUCK, step, jnp.zeros((NBUCK, NH), f32))
            lane = lax.broadcasted_iota(i32, (1, 128), 1)
            dsk = jnp.zeros((1, 128), f32)
            for h in range(NH):
                tot = jnp.sum(jnp.sum(dbias_scr[h], axis=1, keepdims=True), axis=0, keepdims=True)
                dsk = dsk - jnp.where(lane == h, tot, 0.0)
            dsk_ref[...] = dsk

    full = lambda *shape: pl.BlockSpec(shape, lambda n: (0,) * len(shape))
    npg = len(POOL_WINDOWS)
    return pl.pallas_call(
        body, grid=(nsteps,),
        in_specs=[pl.BlockSpec((step_rows, DIN), lambda n: (n, 0)),
                  pl.BlockSpec((BLK, DIN), lambda n: (jnp.maximum(n * MIX_SUB - 1, 0), 0)),
                  pl.BlockSpec((step_rows, DMIX), lambda n: (n, 0)),
                  pl.BlockSpec((MIX_SUB, NH, BLK, 2 * BLK), lambda n: (n, 0, 0, 0)),
                  full(1, HD), full(1, HD), full(BLK, 2 * BLK), full(npg, PGD, PGD), full(1, DPOOL)],
        out_specs=[full(T, DIN), full(1, HD), full(1, HD), full(1, 128), full(NBUCK, NH),
                   full(npg, PGD, PGD), full(1, DPOOL)],
        out_shape=(SDS((T, DIN), f32), SDS((1, HD), f32), SDS((1, HD), f32), SDS((1, 128), f32),
                   SDS((NBUCK, NH), f32), SDS((npg, PGD, PGD), f32), SDS((1, DPOOL), f32)),
        scratch_shapes=[pltpu.VMEM((NH, BLK, 2 * BLK), f32)],
        compiler_params=_cparams(("arbitrary",), VMEM_LIMIT_V7X),
        name=name)(z, z, dy, probs, qg, kg, bucket, pool_w, pscale)


class _LocalWeights:
    def __init__(self, w1, wint, wout, w2):
        self.w1, self.wint, self.wout, self.w2 = w1, wint, wout, w2

    def ffn1(self):
        return self.w1

    def first_norm(self, x, gain):
        return _norm_fwd(x, gain, "norm1_fwd")

    def after_ffn1(self, gain, x1):
        return gain

    def mix(self, after):
        return self.wint, self.wout

    def before_out_proj(self, wout, after):
        return wout

    def ffn2(self, after):
        return self.w2

    def out_ffn2_grads_ready(self, dwout, dw2, after):
        return after

    def before_ffn1_bwd(self, dwint, dx1b):
        return dx1b


def _local_step(x, target, weights, g1, gm, g3, qg, kg, sinks, relb, pool_w, pscale):
    bucket = jnp.asarray(_t5_bucket_table())
    sk = sinks.reshape(NH)
    w1 = weights.ffn1()
    h1 = weights.first_norm(x, g1)
    x1, gate1, up1, h2 = _ffn_fwd(h1, w1, x, None, gm, "ffn1_fwd")
    gm = weights.after_ffn1(gm, x1)
    wint, wout = weights.mix(h2)
    z = _in_proj_fwd(h2, wint, "in_proj_fwd")
    ymix, probs = _mix_fwd(z, qg, kg, sk, relb, bucket, pool_w, pscale, "mix_fwd")
    wout = weights.before_out_proj(wout, ymix)
    x2, h3 = _out_proj_fwd(ymix, wout, x1, g3, "out_proj_fwd")
    w2 = weights.ffn2(h3)
    dy, gate2, up2, dyb, loss_lanes = _ffn_fwd(h3, w2, x2, target, None, "ffn2_fwd")

    dx2, dx2b, dg3, dw2 = _ffn_bwd(dyb, h3, gate2, up2, w2, (x2, g3, dy), "ffn2_bwd")
    dymix, dwout = _out_proj_bwd(dx2b, wout, ymix, "out_proj_bwd")
    dymix = weights.out_ffn2_grads_ready(dwout, dw2, dymix)
    dz, dqg, dkg, dsk, drb, dpw, dps = _mix_bwd(z, dymix, probs, qg, kg, relb, bucket, pool_w, pscale, "mix_bwd")
    dx1, dx1b, dgm, dwint = _in_proj_bwd(dz, wint, h2, (x1, gm, dx2), 0.5, "in_proj_bwd")
    dx1b = weights.before_ffn1_bwd(dwint, dx1b)
    dh1, dw1 = _ffn_bwd(dx1b, h1, gate1, up1, w1, None, "ffn1_bwd")
    small = dict(mix_norm=dgm, ffn2_norm=dg3, pool_scale=dps, q_norm=dqg, k_norm=dkg,
                 attn_sinks=dsk[:, :NH], rel_bias=drb, pool_w=dpw, loss=loss_lanes)
    return (dh1, dx1), (dw1, dwint, dwout, dw2), small


SMALL_NAMES = ("ffn1_norm", "mix_norm", "ffn2_norm", "pool_scale", "q_norm", "k_norm", "attn_sinks", "rel_bias",
               "pool_w", "loss")
SMALL_SHAPES = dict(ffn1_norm=(1, D), mix_norm=(1, D), ffn2_norm=(1, D), pool_scale=(1, DPOOL), q_norm=(1, HD),
                    k_norm=(1, HD), attn_sinks=(1, NH), rel_bias=(NBUCK, NH),
                    pool_w=(1, len(POOL_WINDOWS), PGD, PGD), loss=(1, 128))


def _small_rows(name):
    return -(-int(np.prod(SMALL_SHAPES[name])) // 128)


SMALL_OFF = {}
_r = 0
for _n in SMALL_NAMES:
    SMALL_OFF[_n] = _r
    _r += _small_rows(_n)
SMALL_ROWS = -(-_r // 16) * 16
LOSS_ROW = SMALL_OFF["loss"]


def _pack_small(vals):
    parts = []
    for n in SMALL_NAMES:
        size = _small_rows(n) * 128
        if n in vals:
            flat = vals[n].astype(f32).reshape(-1)
            parts.append(jnp.pad(flat, (0, size - flat.shape[0])))
        else:
            parts.append(jnp.zeros((size,), f32))
    flat = jnp.concatenate(parts)
    flat = jnp.pad(flat, (0, SMALL_ROWS * 128 - flat.shape[0]))
    return flat.reshape(SMALL_ROWS, 128)


def _unpack_small(packed, name):
    size = int(np.prod(SMALL_SHAPES[name]))
    r0 = SMALL_OFF[name]
    return packed[r0:r0 + _small_rows(name)].reshape(-1)[:size].reshape(SMALL_SHAPES[name])


def _position():
    return lax.axis_index("x"), lax.axis_index("y"), lax.axis_index("c")


def _dev_index(x, y, c):
    return 4 * x + 2 * y + c


G1_PIECES, MIX_PIECES, F2_PIECES = (0, 1, 2), (3, 4), (5, 6, 7)


def _group_rows(pieces):
    return sum(PIECE_ROWS[k] for k in pieces)


def _shard_piece(s_ref, k):
    return s_ref.at[pl.ds(PIECE_OFF[k], PIECE_ROWS[k]), :]


def _shard_group(s_ref, pieces):
    return s_ref.at[pl.ds(PIECE_OFF[pieces[0]], _group_rows(pieces)), :]


def _weight_pieces(w1_ref=None, wi_ref=None, wo_ref=None, w2_ref=None):
    arrs = {}
    if w1_ref is not None:
        arrs.update({0: w1_ref.at[0], 1: w1_ref.at[1], 2: w1_ref.at[2]})
    if wi_ref is not None:
        arrs[3] = wi_ref
    if wo_ref is not None:
        arrs[4] = wo_ref
    if w2_ref is not None:
        arrs.update({5: w2_ref.at[0], 6: w2_ref.at[1], 7: w2_ref.at[2]})
    return arrs


def _block_rows(arrs, k, dev):
    r = PIECE_ROWS[k]
    return arrs[k].at[pl.ds(pl.multiple_of(_dev_index(*dev) * r, 16), r), :]


NORM_ROWS = 512


def _all_gather_ffn1(shard, x, gain):
    pieces = G1_PIECES
    rest_pieces = MIX_PIECES + F2_PIECES
    half = FS // 2
    T = x.shape[0]
    SIB, X0, X1, Y0, Y1, RELAY_Y, RELAY_X, ON_X, ON_Y, ON_D0, ON_D1 = range(11)

    def body(s_ref, x_ref, g_ref, w1_ref, h_ref, wi_ref, wo_ref, w2_ref, xbuf, hbuf, rest_buf,
             send_sems, recv_sems, local_sem, norm_sems):
        x, y, c = _position()
        me, sib = (x, y, c), (x, y, 1 - c)
        xn, yn, dg = (1 - x, y, c), (x, 1 - y, c), (1 - x, 1 - y, c)
        arrs = _weight_pieces(w1_ref=w1_ref)

        def place_rest():
            rest = _weight_pieces(wi_ref=wi_ref, wo_ref=wo_ref, w2_ref=w2_ref)
            grp = _shard_group(s_ref, rest_pieces)
            load = pltpu.make_async_copy(grp, rest_buf, norm_sems.at[0])
            load.start()
            load.wait()
            base = PIECE_OFF[rest_pieces[0]]
            for k in rest_pieces:
                pltpu.make_async_copy(rest_buf.at[pl.ds(PIECE_OFF[k] - base, PIECE_ROWS[k]), :],
                                      _block_rows(rest, k, me), norm_sems.at[1]).start()
            pltpu.make_async_copy(grp, rest_buf, norm_sems.at[1]).wait()

        def first_norm():
            for r in range(0, T, NORM_ROWS):
                load = pltpu.make_async_copy(x_ref.at[pl.ds(r, NORM_ROWS), :], xbuf, norm_sems.at[0])
                load.start()
                load.wait()
                xv = xbuf[...]
                rs = lax.rsqrt(jnp.mean(xv * xv, axis=-1, keepdims=True) + EPS)
                hbuf[...] = (xv * rs * g_ref[...]).astype(bf16)
                store = pltpu.make_async_copy(hbuf, h_ref.at[pl.ds(r, NORM_ROWS), :], norm_sems.at[1])
                store.start()
                store.wait()

        def rows_of(k, block, hf):
            r = PIECE_ROWS[k]
            start, size = (0, r) if hf is None else (hf * half, half)
            return arrs[k].at[pl.ds(pl.multiple_of(_dev_index(*block) * r + start, 16), size), :]

        def copies(rel, block, hf, to, from_shard=False):
            def src(k):
                if not from_shard:
                    return rows_of(k, block, hf)
                start, size = (0, PIECE_ROWS[k]) if hf is None else (hf * half, half)
                return s_ref.at[pl.ds(PIECE_OFF[k] + start, size), :]
            return [pltpu.make_async_remote_copy(
                src_ref=src(k), dst_ref=rows_of(k, block, hf), send_sem=send_sems.at[rel], recv_sem=recv_sems.at[rel],
                device_id=to, device_id_type=MESH) for k in pieces]

        def waiter(rel, hf):
            nrows = len(pieces) * (FS if hf is None else half)
            grp = s_ref.at[pl.ds(0, nrows), :]
            return pltpu.make_async_remote_copy(src_ref=grp, dst_ref=grp, send_sem=send_sems.at[rel],
                                                recv_sem=recv_sems.at[rel], device_id=me, device_id_type=MESH)

        def start(cps):
            for cp in cps:
                cp.start()

        mine = [pltpu.make_async_copy(_shard_piece(s_ref, k), _block_rows(arrs, k, me), local_sem) for k in pieces]
        start(mine)
        start(copies(SIB, me, None, sib, True))
        start(copies(X0, me, 0, xn, True))
        start(copies(Y1, me, 1, yn, True))
        start(copies(X1, me, 1, xn, True))
        start(copies(Y0, me, 0, yn, True))
        first_norm()
        place_rest()
        waiter(X0, 0).wait_recv()
        start(copies(RELAY_Y, xn, 0, yn))
        waiter(Y1, 1).wait_recv()
        start(copies(RELAY_X, yn, 1, xn))
        waiter(X1, 1).wait_recv()
        start(copies(ON_X, xn, None, sib))
        waiter(Y0, 0).wait_recv()
        start(copies(ON_Y, yn, None, sib))
        waiter(RELAY_Y, 0).wait_recv()
        start(copies(ON_D0, dg, 0, sib))
        waiter(RELAY_X, 1).wait_recv()
        start(copies(ON_D1, dg, 1, sib))
        waiter(SIB, None).wait_recv()
        waiter(ON_X, None).wait_recv()
        waiter(ON_Y, None).wait_recv()
        waiter(ON_D0, 0).wait_recv()
        waiter(ON_D1, 1).wait_recv()
        for rel, hf in ((SIB, None), (X0, 0), (X1, 1), (Y0, 0), (Y1, 1), (RELAY_Y, 0), (RELAY_X, 1),
                        (ON_X, None), (ON_Y, None), (ON_D0, 0), (ON_D1, 1)):
            waiter(rel, hf).wait_send()
        grp = _shard_group(s_ref, pieces)
        pltpu.make_async_copy(grp, grp, local_sem).wait()

    hbm = pl.BlockSpec(memory_space=pl.ANY)
    return pl.pallas_call(
        body, in_specs=[hbm, hbm, pl.BlockSpec(memory_space=pltpu.VMEM)], out_specs=[hbm] * 5,
        out_shape=(SDS((3, F, D), bf16), SDS((T, D), bf16),
                   SDS((DIN, D), bf16), SDS((DMIX, D), bf16), SDS((3, F, D), bf16)),
        scratch_shapes=[pltpu.VMEM((NORM_ROWS, D), f32), pltpu.VMEM((NORM_ROWS, D), bf16),
                        pltpu.VMEM((_group_rows(rest_pieces), D), bf16),
                        pltpu.SemaphoreType.DMA((11,)), pltpu.SemaphoreType.DMA((11,)), pltpu.SemaphoreType.DMA,
                        pltpu.SemaphoreType.DMA((2,))],
        compiler_params=pltpu.CompilerParams(has_side_effects=True),
        name="all_gather_ffn1")(shard, x, gain)


HBM_SPEC = pl.BlockSpec(memory_space=pltpu.HBM)
SEM_SPEC = pl.BlockSpec(memory_space=pltpu.SEMAPHORE)
ANY_SPEC = pl.BlockSpec(memory_space=pl.ANY)
SPLIT_EFFECT = pltpu.SideEffectType.DATAFLOW_SIDE_EFFECTING


def _in_hbm(a):
    return pltpu.with_memory_space_constraint(a, pltpu.HBM)


def _hbm_like(a):
    return pltpu.HBM(a.shape, a.dtype)


def _gather_rest_start(shard, wi, wo, w2, w1):
    def body(s_ref, wi_ref, wo_ref, w2_ref, w1_ref,
             ssem_m, rsem_m0, rsem_m, ssem_f, rsem_f0, rsem_f, s_o, wi_o, wo_o, w2_o, w1_o):
        x, y, c = _position()
        me, sib = (x, y, c), (x, y, 1 - c)
        chips = [(1 - x, y), (x, 1 - y), (1 - x, 1 - y)]
        arrs = _weight_pieces(wi_ref=wi_ref, wo_ref=wo_ref, w2_ref=w2_ref)
        for pieces, ssem, rsem0, rsem in ((MIX_PIECES, ssem_m, rsem_m0, rsem_m), (F2_PIECES, ssem_f, rsem_f0, rsem_f)):
            for p in pieces:
                pltpu.make_async_remote_copy(
                    src_ref=_shard_piece(s_ref, p), dst_ref=_block_rows(arrs, p, me), send_sem=ssem.at[0],
                    recv_sem=rsem0, device_id=sib, device_id_type=MESH).start()
            for j, chip in enumerate(chips):
                for p in pieces:
                    pltpu.make_async_remote_copy(
                        src_ref=_shard_piece(s_ref, p), dst_ref=_block_rows(arrs, p, me), send_sem=ssem.at[1 + j],
                        recv_sem=rsem.at[j], device_id=(*chip, c), device_id_type=MESH).start()

    dma = pltpu.SemaphoreType.DMA
    return pl.pallas_call(
        body, name="gather_rest_start",
        out_shape=(dma((4,)), dma(()), dma((3,)), dma((4,)), dma(()), dma((3,)),
                   _hbm_like(shard), _hbm_like(wi), _hbm_like(wo), _hbm_like(w2), _hbm_like(w1)),
        in_specs=(HBM_SPEC,) * 5, out_specs=(SEM_SPEC,) * 6 + (HBM_SPEC,) * 5,
        input_output_aliases={0: 6, 1: 7, 2: 8, 3: 9, 4: 10},
        compiler_params=pltpu.CompilerParams(has_side_effects=SPLIT_EFFECT),
    )(_in_hbm(shard), _in_hbm(wi), _in_hbm(wo), _in_hbm(w2), _in_hbm(w1))


def _gather_mix_pass_on(rsem_m, wi, wo, thru, after):
    def body(wi_ref, wo_ref, thru_ref, rsem, after_ref, fsend, frecv, wi_o, wo_o, thru_o):
        x, y, c = _position()
        sib = (x, y, 1 - c)
        arrs = _weight_pieces(wi_ref=wi_ref, wo_ref=wo_ref)
        both = wi_ref.at[pl.ds(0, _group_rows(MIX_PIECES)), :]
        for j, chip in enumerate([(1 - x, y), (x, 1 - y), (1 - x, 1 - y)]):
            pltpu.make_async_remote_copy(src_ref=both, dst_ref=both, send_sem=fsend.at[j], recv_sem=rsem.at[j],
                                         device_id=(x, y, c), device_id_type=MESH).wait_recv()
            for p in MIX_PIECES:
                rows = _block_rows(arrs, p, (*chip, c))
                pltpu.make_async_remote_copy(src_ref=rows, dst_ref=rows, send_sem=fsend.at[j], recv_sem=frecv.at[j],
                                             device_id=sib, device_id_type=MESH).start()

    dma = pltpu.SemaphoreType.DMA
    return pl.pallas_call(
        body, name="gather_mix_pass_on",
        out_shape=(dma((3,)), dma((3,)), _hbm_like(wi), _hbm_like(wo), _hbm_like(thru)),
        in_specs=(HBM_SPEC, HBM_SPEC, HBM_SPEC, SEM_SPEC, ANY_SPEC), out_specs=(SEM_SPEC, SEM_SPEC) + (HBM_SPEC,) * 3,
        input_output_aliases={0: 2, 1: 3, 2: 4},
        compiler_params=pltpu.CompilerParams(has_side_effects=SPLIT_EFFECT),
    )(wi, wo, _in_hbm(thru), rsem_m, after)


def _gather_mix_wait(ssem_m, rsem_m0, fsend, frecv, shard, wi, wo, after):
    def body(s_ref, wi_ref, wo_ref, ssem, rsem0, fs, fr, after_ref, s_o, wi_o, wo_o):
        x, y, c = _position()
        grp = _shard_group(s_ref, MIX_PIECES)

        def waiter(send_sem, recv_sem):
            return pltpu.make_async_remote_copy(src_ref=grp, dst_ref=grp, send_sem=send_sem, recv_sem=recv_sem,
                                                device_id=(x, y, c), device_id_type=MESH)

        waiter(ssem.at[0], rsem0).wait_recv()
        for j in range(3):
            waiter(fs.at[j], fr.at[j]).wait_recv()
        for rel in range(4):
            waiter(ssem.at[rel], rsem0).wait_send()
        for j in range(3):
            waiter(fs.at[j], fr.at[j]).wait_send()

    return pl.pallas_call(
        body, name="gather_mix_wait", out_shape=(_hbm_like(shard), _hbm_like(wi), _hbm_like(wo)),
        in_specs=(HBM_SPEC,) * 3 + (SEM_SPEC,) * 4 + (ANY_SPEC,), out_specs=(HBM_SPEC,) * 3,
        input_output_aliases={0: 0, 1: 1, 2: 2},
        compiler_params=pltpu.CompilerParams(has_side_effects=SPLIT_EFFECT),
    )(shard, wi, wo, ssem_m, rsem_m0, fsend, frecv, after)


def _gather_ffn2_pass_on(rsem_f, w2, wo, after):
    def body(w2_ref, wo_ref, rsem, after_ref, fsend, frecv, w2_o, wo_o):
        x, y, c = _position()
        sib = (x, y, 1 - c)
        chips = [(1 - x, y), (x, 1 - y), (1 - x, 1 - y)]
        arrs = _weight_pieces(w2_ref=w2_ref)
        three = w2_ref.at[0, pl.ds(0, _group_rows(F2_PIECES)), :]
        for j, chip in enumerate(chips):
            pltpu.make_async_remote_copy(src_ref=three, dst_ref=three, send_sem=fsend.at[j], recv_sem=rsem.at[j],
                                         device_id=(x, y, c), device_id_type=MESH).wait_recv()
            for p in F2_PIECES:
                rows = _block_rows(arrs, p, (*chip, c))
                pltpu.make_async_remote_copy(src_ref=rows, dst_ref=rows, send_sem=fsend.at[j], recv_sem=frecv.at[j],
                                             device_id=sib, device_id_type=MESH).start()

    dma = pltpu.SemaphoreType.DMA
    return pl.pallas_call(
        body, name="gather_ffn2_pass_on", out_shape=(dma((3,)), dma((3,)), _hbm_like(w2), _hbm_like(wo)),
        in_specs=(HBM_SPEC, HBM_SPEC, SEM_SPEC, ANY_SPEC), out_specs=(SEM_SPEC, SEM_SPEC, HBM_SPEC, HBM_SPEC),
        input_output_aliases={0: 2, 1: 3},
        compiler_params=pltpu.CompilerParams(has_side_effects=SPLIT_EFFECT),
    )(w2, wo, rsem_f, after)


def _gather_ffn2_wait(ssem_f, rsem_f0, fsend, frecv, shard, w2, after):
    def body(s_ref, w2_ref, ssem, rsem0, fs, fr, after_ref, w2_o):
        x, y, c = _position()
        grp = _shard_group(s_ref, F2_PIECES)

        def waiter(send_sem, recv_sem):
            return pltpu.make_async_remote_copy(src_ref=grp, dst_ref=grp, send_sem=send_sem, recv_sem=recv_sem,
                                                device_id=(x, y, c), device_id_type=MESH)

        waiter(ssem.at[0], rsem0).wait_recv()
        for j in range(3):
            waiter(fs.at[j], fr.at[j]).wait_recv()
        for rel in range(4):
            waiter(ssem.at[rel], rsem0).wait_send()
        for j in range(3):
            waiter(fs.at[j], fr.at[j]).wait_send()

    return pl.pallas_call(
        body, name="gather_ffn2_wait", out_shape=_hbm_like(w2),
        in_specs=(HBM_SPEC, HBM_SPEC, SEM_SPEC, SEM_SPEC, SEM_SPEC, SEM_SPEC, ANY_SPEC), out_specs=HBM_SPEC,
        input_output_aliases={1: 0},
        compiler_params=pltpu.CompilerParams(has_side_effects=SPLIT_EFFECT),
    )(shard, w2, ssem_f, rsem_f0, fsend, frecv, after)


class _GatheredWeights(_LocalWeights):
    def __init__(self, shard, x, gain1):
        w1, self.h1, wi, wo, w2 = _all_gather_ffn1(shard, x, gain1)
        (self.ssem_m, self.rsem_m0, self.rsem_m, self.ssem_f, self.rsem_f0, self.rsem_f,
         self.shard, self.wi, self.wo, self.w2_part, self.w1) = _gather_rest_start(shard, wi, wo, w2, w1)

    def first_norm(self, x, gain):
        return self.h1

    def after_ffn1(self, gain, x1):
        self.fsend_m, self.frecv_m, self.wi, self.wo, gain = _gather_mix_pass_on(self.rsem_m, self.wi, self.wo, gain, x1)
        return gain

    def mix(self, after):
        self.shard, wint, wout = _gather_mix_wait(self.ssem_m, self.rsem_m0, self.fsend_m, self.frecv_m, self.shard,
                                                  self.wi, self.wo, after)
        return wint, wout

    def before_out_proj(self, wout, after):
        self.fsend, self.frecv, self.w2_part, wout = _gather_ffn2_pass_on(self.rsem_f, self.w2_part, wout, after)
        return wout

    def ffn2(self, after):
        return _gather_ffn2_wait(self.ssem_f, self.rsem_f0, self.fsend, self.frecv, self.shard, self.w2_part, after)

    def out_ffn2_grads_ready(self, dwout, dw2, after):
        rx1 = lax.empty((4, RSA_ROWS, D), bf16)
        sa, ra, sent, rx1, after = _rsa_level1_start(dict(wo=dwout, w2=dw2), rx1, after, "rsa_level1_start_out_ffn2")
        self.level1 = ((sa, ra), sent, rx1)
        return after

    def before_ffn1_bwd(self, dwint, dx1b):
        early, sent, rx1 = self.level1
        sa, ra, late, rx1, dx1b = _rsa_level1_start(dict(wi=dwint), rx1, dx1b, "rsa_level1_start_in")
        started = (((MIX_PIECES[1],) + F2_PIECES, *early), ((MIX_PIECES[0],), sa, ra))
        rx2 = lax.empty((3, RSA_ROWS, D), bf16)
        self.sb, self.rb, self.tx, self.acc, self.rx2, dx1b = _rsa_sums_and_send(
            started, late["wi"], sent["wo"], sent["w2"], rx1, rx2, dx1b)
        return dx1b

    def mix_ffn2_grads_parts(self, after):
        rx2 = _rsa_level2_wait(self.sb, self.rb, self.tx, self.rx2, after)
        return self.acc, rx2


def _reduce_scatter_ffn1_head(dw1, small_packed, first_norm):
    pieces = G1_PIECES
    half = FS // 2
    hrows = len(pieces) * half
    nrows = 2 * hrows
    X_RELAY, Y_RELAY = range(2)

    T = first_norm[0].shape[0]

    def body(d1_ref, p_ref, dh_hbm, x_hbm, gain_ref, dr_hbm,
             forx_ref, fory_ref, own_ref, rx1_ref, relx_ref, rely_ref, gx_hbm, tot_ref,
             own_buf, rx_buf, tx1, tx2, tx3, acc, sa, ra, sb, rb, lsem, pair, chips, small_tot, small_send, small_recv,
             xbuf, rbuf, hbuf, obuf, norm_in_sems, norm_out_sems):
        x, y, c = _position()
        me, sib = (x, y, c), (x, y, 1 - c)
        xn, yn = (1 - x, y, c), (x, 1 - y, c)
        rel_chips = [(x, y), (1 - x, y), (x, 1 - y), (1 - x, 1 - y)]
        srcs = _weight_pieces(w1_ref=d1_ref)

        my_chip = 2 * x + y
        pair[c] = p_ref[...]
        swap = pltpu.make_async_remote_copy(
            src_ref=pair.at[c], dst_ref=pair.at[c], send_sem=small_send.at[0], recv_sem=small_recv.at[0],
            device_id=sib, device_id_type=MESH)
        mine = pl.ds(pl.multiple_of(c * (SMALL_ROWS // 2), 8), SMALL_ROWS // 2)
        small = [pltpu.make_async_remote_copy(
            src_ref=chips.at[my_chip, mine, :], dst_ref=chips.at[my_chip, mine, :], send_sem=small_send.at[j],
            recv_sem=small_recv.at[j], device_id=(*rel_chips[j], c), device_id_type=MESH) for j in (1, 2, 3)]
        give = pltpu.make_async_remote_copy(
            src_ref=small_tot.at[mine, :], dst_ref=small_tot.at[mine, :], send_sem=small_send.at[4],
            recv_sem=small_recv.at[4], device_id=sib, device_id_type=MESH)

        def part(k, dev, hf):
            r = PIECE_ROWS[k]
            return srcs[k].at[pl.ds(pl.multiple_of(_dev_index(*dev) * r + hf * half, 16), half), :]

        def slot(ref, k, hf):
            return ref.at[pl.ds(hf * hrows + k * half, half), :]

        halves = [(k, hf) for hf in (0, 1) for k in pieces]

        for j in (3, 1, 2, 0):
            for k, hf in halves:
                pltpu.make_async_remote_copy(
                    src_ref=part(k, (*rel_chips[j], 1 - c), hf), dst_ref=slot(rx1_ref.at[j], k, hf),
                    send_sem=sa.at[j], recv_sem=ra.at[j], device_id=sib, device_id_type=MESH).start()

        def wait_a(j):
            return pltpu.make_async_remote_copy(src_ref=rx1_ref.at[j], dst_ref=rx1_ref.at[j], send_sem=sa.at[j],
                                                recv_sem=ra.at[j], device_id=me, device_id_type=MESH)

        def ici(rel, src, dst, to):
            return pltpu.make_async_remote_copy(src_ref=src, dst_ref=dst, send_sem=sb.at[rel], recv_sem=rb.at[rel],
                                                device_id=to, device_id_type=MESH)

        first, second = pl.ds(0, hrows), pl.ds(hrows, hrows)
        sends = {
            X_RELAY: ici(X_RELAY, tx3.at[first, :], relx_ref, xn),
            Y_RELAY: ici(Y_RELAY, tx3.at[second, :], rely_ref, yn),
        }

        def chip_sum(j, dst):
            loads = [pltpu.make_async_copy(part(k, (*rel_chips[j], c), hf), slot(own_buf, k, hf), lsem.at[0])
                     for k, hf in halves]
            for cp in loads:
                cp.start()
            wait_a(j).wait_recv()
            got = pltpu.make_async_copy(rx1_ref.at[j], rx_buf, lsem.at[1])
            got.start()
            pltpu.make_async_copy(rx_buf, rx_buf, lsem.at[0]).wait()
            got.wait()

            def add(i, carry):
                rows = pl.ds(pl.multiple_of(i * half, 16), half)
                tot = own_buf[rows, :].astype(f32) + rx_buf[rows, :].astype(f32)
                dst[rows, :] = tot.astype(dst.dtype)
                return carry

            lax.fori_loop(0, nrows // half, add, 0)

        def add_landed(landed, dst, rows0, nrows_):
            got = pltpu.make_async_copy(landed, rx_buf.at[pl.ds(0, nrows_), :], lsem.at[1])
            got.start()
            got.wait()

            def add(i, carry):
                src_rows = pl.ds(pl.multiple_of(i * half, 16), half)
                dst_rows = pl.ds(pl.multiple_of(rows0 + i * half, 16), half)
                dst[dst_rows, :] = (dst[dst_rows, :].astype(f32) + rx_buf[src_rows, :].astype(f32)).astype(dst.dtype)
                return carry

            lax.fori_loop(0, nrows_ // half, add, 0)

        chip_sum(3, tx3)
        sends[X_RELAY].start()
        sends[Y_RELAY].start()
        dg = _norm_bwd_rows(T, dh_hbm, x_hbm, gain_ref, dr_hbm, gx_hbm.at[0], None,
                            xbuf, rbuf, hbuf, obuf, None, norm_in_sems, norm_out_sems)
        r0 = SMALL_OFF["ffn1_norm"]
        for k in range(D // 128):
            pair[c, r0 + k:r0 + k + 1, :] = dg[:, 128 * k:128 * (k + 1)]
        swap.start()
        swap.wait_recv()
        chips[my_chip] = pair[0] + pair[1]
        for cp in small:
            cp.start()
        chip_sum(1, tx1)
        chip_sum(2, tx2)
        chip_sum(0, acc)
        own_out = pltpu.make_async_copy(acc, own_ref, lsem.at[0])
        own_out.start()
        sends[X_RELAY].wait_recv()
        add_landed(relx_ref, tx2, 0, hrows)
        sends[Y_RELAY].wait_recv()
        add_landed(rely_ref, tx1, hrows, hrows)
        own_out.wait()
        outs = [pltpu.make_async_copy(tx1, forx_ref, lsem.at[0]), pltpu.make_async_copy(tx2, fory_ref, lsem.at[1])]
        for cp in outs:
            cp.start()
        for cp in outs:
            cp.wait()
        for cp in small:
            cp.wait_recv()
        small_tot[mine, :] = (chips[0, mine, :] + chips[1, mine, :]) + (chips[2, mine, :] + chips[3, mine, :])
        give.start()
        give.wait_recv()
        tot = small_tot[...]
        tot_ref[...] = tot
        loss = jnp.sum(tot[LOSS_ROW:LOSS_ROW + 1, :], axis=-1, keepdims=True)
        tot_ref[LOSS_ROW:LOSS_ROW + 1, :] = jnp.broadcast_to(loss, (1, 128))
        for j in range(4):
            wait_a(j).wait_send()
        for cp in sends.values():
            cp.wait_send()
        swap.wait_send()
        for cp in small + [give]:
            cp.wait_send()

    hbm = pl.BlockSpec(memory_space=pl.ANY)
    vm = pl.BlockSpec(memory_space=pltpu.VMEM)
    outs = pl.pallas_call(
        body, in_specs=[hbm, vm, hbm, hbm, vm, hbm], out_specs=[hbm] * 7 + [vm],
        out_shape=(SDS((nrows, D), bf16), SDS((nrows, D), bf16), SDS((nrows, D), f32), SDS((4, nrows, D), bf16),
                   SDS((hrows, D), bf16), SDS((hrows, D), bf16), SDS((1, T, D), f32), SDS((SMALL_ROWS, 128), f32)),
        scratch_shapes=[pltpu.VMEM((nrows, D), bf16), pltpu.VMEM((nrows, D), bf16),
                        pltpu.VMEM((nrows, D), bf16), pltpu.VMEM((nrows, D), bf16), pltpu.VMEM((nrows, D), bf16),
                        pltpu.VMEM((nrows, D), f32),
                        pltpu.SemaphoreType.DMA((4,)), pltpu.SemaphoreType.DMA((4,)),
                        pltpu.SemaphoreType.DMA((2,)), pltpu.SemaphoreType.DMA((2,)), pltpu.SemaphoreType.DMA((2,)),
                        pltpu.VMEM((2, SMALL_ROWS, 128), f32), pltpu.VMEM((4, SMALL_ROWS, 128), f32),
                        pltpu.VMEM((SMALL_ROWS, 128), f32),
                        pltpu.SemaphoreType.DMA((5,)), pltpu.SemaphoreType.DMA((5,))]
        + [sc for sc in _norm_bwd_scratch(True, False) if sc is not None],
        compiler_params=pltpu.CompilerParams(has_side_effects=True, vmem_limit_bytes=VMEM_LIMIT_V7X),
        name="reduce_scatter_ffn1_head")(dw1, small_packed, *first_norm)
    return outs[0], outs[1], outs[2], outs[-1], outs[-2]


def _rs1_tail_start(for_x, for_y, from_x, from_y, *thru):
    def body(fx_ref, fy_ref, lx_ref, ly_ref, *rest):
        ssem, rsem = rest[len(thru):len(thru) + 2]
        x, y, c = _position()
        pltpu.make_async_remote_copy(src_ref=fx_ref, dst_ref=lx_ref, send_sem=ssem.at[0], recv_sem=rsem.at[0],
                                     device_id=(1 - x, y, c), device_id_type=MESH).start()
        pltpu.make_async_remote_copy(src_ref=fy_ref, dst_ref=ly_ref, send_sem=ssem.at[1], recv_sem=rsem.at[1],
                                     device_id=(x, 1 - y, c), device_id_type=MESH).start()

    dma = pltpu.SemaphoreType.DMA
    arrs = (for_x, for_y, from_x, from_y, *thru)
    return pl.pallas_call(
        body, name="rs1_tail_start", out_shape=(dma((2,)), dma((2,))) + tuple(_hbm_like(a) for a in arrs),
        in_specs=(HBM_SPEC,) * len(arrs), out_specs=(SEM_SPEC,) * 2 + (HBM_SPEC,) * len(arrs),
        input_output_aliases={i: i + 2 for i in range(len(arrs))},
        compiler_params=pltpu.CompilerParams(has_side_effects=SPLIT_EFFECT),
    )(*[_in_hbm(a) for a in arrs])


RSA_PIECES = MIX_PIECES + F2_PIECES
RSA_ROWS = _group_rows(RSA_PIECES)
RSA_OFF = {k: PIECE_OFF[k] - PIECE_OFF[RSA_PIECES[0]] for k in RSA_PIECES}
RSA_BLOCK = 192


def _rsa_rows(ref, k):
    return ref.at[pl.ds(RSA_OFF[k], PIECE_ROWS[k]), :]


def _rsa_level1_start(grads, rx1, thru, name):
    keys = sorted(grads)
    n = len(keys)

    def body(*refs):
        srcs = _weight_pieces(**{k + "_ref": ref for k, ref in zip(keys, refs[:n])})
        rx1_ref, sa, ra = refs[n], refs[n + 2], refs[n + 3]
        x, y, c = _position()
        for j, chip in enumerate([(x, y), (1 - x, y), (x, 1 - y), (1 - x, 1 - y)]):
            for k in sorted(srcs):
                pltpu.make_async_remote_copy(
                    src_ref=_block_rows(srcs, k, (*chip, 1 - c)), dst_ref=_rsa_rows(rx1_ref.at[j], k),
                    send_sem=sa.at[j], recv_sem=ra.at[j], device_id=(x, y, 1 - c), device_id_type=MESH).start()

    dma = pltpu.SemaphoreType.DMA
    arrs = tuple(grads[k] for k in keys) + (rx1, thru)
    outs = pl.pallas_call(
        body, name=name, out_shape=(dma((4,)), dma((4,))) + tuple(_hbm_like(a) for a in arrs),
        in_specs=(HBM_SPEC,) * len(arrs), out_specs=(SEM_SPEC,) * 2 + (HBM_SPEC,) * len(arrs),
        input_output_aliases={i: i + 2 for i in range(len(arrs))},
        compiler_params=pltpu.CompilerParams(has_side_effects=SPLIT_EFFECT),
    )(*[_in_hbm(a) for a in arrs])
    return outs[0], outs[1], dict(zip(keys, outs[2:2 + n])), outs[2 + n], outs[3 + n]


def _rsa_sums_and_send(started, dwint, dwout, dw2, rx1, rx2, thru):
    nblk = RSA_ROWS // RSA_BLOCK
    nstart = len(started)

    def body(*refs):
        di_ref, do_ref, d2_ref, rx1_ref, rx2_ref = refs[:5]
        l1_sems = refs[6:6 + 2 * nstart]
        sb, rb, tx_ref, acc_ref = refs[6 + 2 * nstart:10 + 2 * nstart]
        own_buf, rx_buf, tx_buf, acc_buf, in_sems, out_sems = refs[13 + 2 * nstart:]
        x, y, c = _position()
        srcs = _weight_pieces(wi_ref=di_ref, wo_ref=do_ref, w2_ref=d2_ref)
        chips = [(x, y), (1 - x, y), (x, 1 - y), (1 - x, 1 - y)]

        for g, (pieces, _, _) in enumerate(started):
            ssem, rsem = l1_sems[2 * g], l1_sems[2 * g + 1]
            for j in range(4):
                rows = rx1_ref.at[j, pl.ds(RSA_OFF[pieces[0]], _group_rows(pieces)), :]
                d = pltpu.make_async_remote_copy(src_ref=rows, dst_ref=rows, send_sem=ssem.at[j], recv_sem=rsem.at[j],
                                                 device_id=(x, y, c), device_id_type=MESH)
                d.wait_recv()
                d.wait_send()

        def start_loads(j):
            s = j % 2
            for k in RSA_PIECES:
                pltpu.make_async_copy(_block_rows(srcs, k, (*chips[j], c)), _rsa_rows(own_buf.at[s], k),
                                      in_sems.at[2 * s]).start()
            pltpu.make_async_copy(rx1_ref.at[j], rx_buf.at[s], in_sems.at[2 * s + 1]).start()

        def wait_loads(j):
            s = j % 2
            pltpu.make_async_copy(rx1_ref.at[j], own_buf.at[s], in_sems.at[2 * s]).wait()
            pltpu.make_async_copy(rx1_ref.at[j], rx_buf.at[s], in_sems.at[2 * s + 1]).wait()

        def store(j):
            if j == 0:
                return pltpu.make_async_copy(acc_buf, acc_ref, out_sems.at[2])
            return pltpu.make_async_copy(tx_buf.at[j % 2], tx_ref.at[j - 1], out_sems.at[j % 2])

        def send(j):
            return pltpu.make_async_remote_copy(src_ref=tx_ref.at[j - 1], dst_ref=rx2_ref.at[j - 1], send_sem=sb.at[j - 1],
                                                recv_sem=rb.at[j - 1], device_id=(*chips[j], c), device_id_type=MESH)

        start_loads(0)
        for j in range(4):
            s = j % 2
            if j + 1 < 4:
                start_loads(j + 1)
            wait_loads(j)
            if j == 3:
                store(1).wait()
                send(1).start()

            def add(i, carry, j=j, s=s):
                rows = pl.ds(pl.multiple_of(i * RSA_BLOCK, 16), RSA_BLOCK)
                tot = own_buf[s, rows, :].astype(f32) + rx_buf[s, rows, :].astype(f32)
                if j == 0:
                    acc_buf[rows, :] = tot
                else:
                    tx_buf[s, rows, :] = tot.astype(bf16)
                return carry

            lax.fori_loop(0, nblk, add, 0)
            store(j).start()
        store(0).wait()
        for j in (2, 3):
            store(j).wait()
            send(j).start()

    dma = pltpu.SemaphoreType.DMA
    passed = (rx1, rx2, thru)
    outs = pl.pallas_call(
        body, name="rsa_sums_and_send",
        in_specs=(HBM_SPEC,) * 6 + (SEM_SPEC,) * (2 * nstart),
        out_specs=(SEM_SPEC,) * 2 + (HBM_SPEC,) * 5,
        out_shape=(dma((3,)), dma((3,)), pltpu.HBM((3, RSA_ROWS, D), bf16), pltpu.HBM((RSA_ROWS, D), f32))
        + tuple(_hbm_like(a) for a in passed),
        input_output_aliases={3: 4, 4: 5, 5: 6},
        scratch_shapes=[pltpu.VMEM((2, RSA_ROWS, D), bf16), pltpu.VMEM((2, RSA_ROWS, D), bf16),
                        pltpu.VMEM((2, RSA_ROWS, D), bf16), pltpu.VMEM((RSA_ROWS, D), f32),
                        dma((4,)), dma((3,))],
        compiler_params=pltpu.CompilerParams(has_side_effects=SPLIT_EFFECT, vmem_limit_bytes=VMEM_LIMIT_V7X),
    )(*[_in_hbm(a) for a in (dwint, dwout, dw2) + passed], *[sem for _, sa, ra in started for sem in (sa, ra)])
    sb, rb, tx, acc, _, rx2, thru = outs
    return sb, rb, tx, acc, rx2, thru


def _rsa_level2_wait(sb, rb, tx, rx2, after):
    def body(tx_ref, rx2_ref, sb_ref, rb_ref, after_ref, rx2_o):
        x, y, c = _position()
        for j in range(3):
            d = pltpu.make_async_remote_copy(src_ref=tx_ref.at[j], dst_ref=rx2_ref.at[j], send_sem=sb_ref.at[j],
                                             recv_sem=rb_ref.at[j], device_id=(x, y, c), device_id_type=MESH)
            d.wait_recv()
            d.wait_send()

    return pl.pallas_call(
        body, name="rsa_level2_wait", out_shape=_hbm_like(rx2),
        in_specs=(HBM_SPEC, HBM_SPEC, SEM_SPEC, SEM_SPEC, ANY_SPEC), out_specs=HBM_SPEC,
        input_output_aliases={1: 0},
        compiler_params=pltpu.CompilerParams(has_side_effects=SPLIT_EFFECT),
    )(tx, rx2, sb, rb, after)


def _adamw_math(w, g, m, v):
    m = ADAM_B1 * m + (1.0 - ADAM_B1) * g
    v = ADAM_B2 * v + (1.0 - ADAM_B2) * (g * g)
    m_hat = m / (1.0 - ADAM_B1 ** ADAM_STEP)
    v_hat = v / (1.0 - ADAM_B2 ** ADAM_STEP)
    delta = -ADAM_LR * (m_hat / (jnp.sqrt(v_hat) + ADAM_EPS) + ADAM_WD * w)
    return delta, m, v


def _adamw_big(pieces, ws, ms, vs, own, landed, name, in_flight=None):
    npiece = len(pieces)
    nsent = len(landed) if in_flight else 0
    nland = sum(a.shape[0] if a.ndim == 3 else 1 for a in landed)
    rmax = max(PIECE_ROWS[k] for k in pieces)
    half = FS // 2

    def segments(k):
        if k in G1_PIECES:
            return [(hf * len(G1_PIECES) * half + k * half, hf * half, half) for hf in (0, 1)]
        return [(RSA_OFF[k], 0, PIECE_ROWS[k])]

    def body(*refs):
        ins = (refs[0:npiece], refs[npiece:2 * npiece], refs[2 * npiece:3 * npiece])
        own_ref = refs[3 * npiece]
        nin = 3 * npiece + 1 + len(landed)
        land_refs = []
        for ref, a in zip(refs[3 * npiece + 1:nin], landed):
            land_refs += [ref.at[j] for j in range(a.shape[0])] if a.ndim == 3 else [ref]
        if in_flight:
            sent_refs, (ssem, rsem) = refs[nin:nin + nsent], refs[nin + nsent:nin + nsent + 2]
            nin += nsent + 3
        out_refs = refs[nin:nin + 4 * npiece]
        inb, landb, outb, in_sems, land_sems, out_sems = refs[nin + 4 * npiece + nsent:]

        def loads(i):
            s, k = i % 2, pieces[i]
            r = PIECE_ROWS[k]
            cps = [pltpu.make_async_copy(ins[q][i].at[0], inb.at[s, q, pl.ds(0, r), :], in_sems.at[4 * s + q])
                   for q in range(3)]
            waits, late = list(cps), []
            for src0, dst0, n in segments(k):
                cps.append(pltpu.make_async_copy(own_ref.at[pl.ds(src0, n), :], inb.at[s, 3, pl.ds(dst0, n), :],
                                                 in_sems.at[4 * s + 3]))
                for p in range(nland):
                    late.append(pltpu.make_async_copy(land_refs[p].at[pl.ds(src0, n), :],
                                                      landb.at[s, p, pl.ds(dst0, n), :], land_sems.at[nland * s + p]))
            own_rows = inb.at[s, 3, pl.ds(0, r), :]
            waits.append(pltpu.make_async_copy(own_rows, own_rows, in_sems.at[4 * s + 3]))
            for p in range(nland):
                rows = landb.at[s, p, pl.ds(0, r), :]
                waits.append(pltpu.make_async_copy(rows, rows, land_sems.at[nland * s + p]))
            return cps, late, waits

        def stores(i):
            s, r = i % 2, PIECE_ROWS[pieces[i]]
            return [pltpu.make_async_copy(outb.at[s, q, pl.ds(0, r), :], out_refs[q * npiece + i].at[0],
                                          out_sems.at[4 * s + q]) for q in range(4)]

        ahead = min(2, npiece) if in_flight else 1
        for i in range(ahead):
            for cp in loads(i)[0]:
                cp.start()
        if in_flight:
            x, y, c = _position()
            for j in range(nsent):
                d = pltpu.make_async_remote_copy(src_ref=sent_refs[j], dst_ref=refs[3 * npiece + 1 + j],
                                                 send_sem=ssem.at[j], recv_sem=rsem.at[j], device_id=(x, y, c),
                                                 device_id_type=MESH)
                d.wait_recv()
                d.wait_send()
        for cp in loads(0)[1]:
            cp.start()
        for i in range(npiece):
            s, r = i % 2, PIECE_ROWS[pieces[i]]
            if i + 1 < npiece:
                first, late, _ = loads(i + 1)
                for cp in late if i + 1 < ahead else first + late:
                    cp.start()
            for cp in loads(i)[2]:
                cp.wait()
            if i >= 2:
                for cp in stores(i - 2):
                    cp.wait()
            g = inb[s, 3, 0:r, :]
            for p in range(nland):
                g = g + landb[s, p, 0:r, :].astype(f32)
            d, nm, nv = _adamw_math(inb[s, 0, 0:r, :], g, inb[s, 1, 0:r, :], inb[s, 2, 0:r, :])
            outb[s, 0, 0:r, :] = g
            outb[s, 1, 0:r, :] = d
            outb[s, 2, 0:r, :] = nm
            outb[s, 3, 0:r, :] = nv
            for cp in stores(i):
                cp.start()
        for i in range(max(npiece - 2, 0), npiece):
            for cp in stores(i):
                cp.wait()

    hbm = pl.BlockSpec(memory_space=pl.ANY)
    in_specs, out_specs = [hbm] * (3 * npiece + 1), [hbm] * (4 * npiece)
    out_shape = [SDS(w.shape, f32) for _ in range(4) for w in ws]
    args, aliases, effect = [*ws, *ms, *vs, own], {}, False
    if in_flight:
        ssem, rsem, sent, after = in_flight
        in_specs += [HBM_SPEC] * (2 * nsent) + [SEM_SPEC, SEM_SPEC, hbm]
        args += [_in_hbm(a) for a in (*landed, *sent)] + [ssem, rsem, after]
        out_specs += [HBM_SPEC] * nsent
        out_shape += [_hbm_like(a) for a in landed]
        aliases = {3 * npiece + 1 + j: 4 * npiece + j for j in range(nsent)}
        effect = SPLIT_EFFECT
    else:
        in_specs += [hbm] * len(landed)
        args += list(landed)
    outs = pl.pallas_call(
        body, in_specs=in_specs, out_specs=out_specs, out_shape=tuple(out_shape), input_output_aliases=aliases,
        scratch_shapes=[pltpu.VMEM((2, 4, rmax, D), f32), pltpu.VMEM((2, nland, rmax, D), bf16),
                        pltpu.VMEM((2, 4, rmax, D), f32),
                        pltpu.SemaphoreType.DMA((8,)), pltpu.SemaphoreType.DMA((2 * nland,)),
                        pltpu.SemaphoreType.DMA((8,))],
        compiler_params=pltpu.CompilerParams(has_side_effects=effect, vmem_limit_bytes=VMEM_LIMIT_V7X),
        name=name)(*args)
    return [list(outs[q * npiece:(q + 1) * npiece]) for q in range(4)]


def _adamw_small(ws, ms, vs, gs, name):
    n = len(ws)

    def body(*refs):
        w_refs, m_refs, v_refs, g_refs = refs[0:n], refs[n:2 * n], refs[2 * n:3 * n], refs[3 * n:4 * n]
        outs = refs[4 * n:]
        for i in range(n):
            d, nm, nv = _adamw_math(w_refs[i][...], g_refs[i][...], m_refs[i][...], v_refs[i][...])
            outs[i][...] = d
            outs[n + i][...] = nm
            outs[2 * n + i][...] = nv

    outs = pl.pallas_call(
        body, out_shape=tuple(SDS(w.shape, f32) for _ in range(3) for w in ws), name=name)(*ws, *ms, *vs, *gs)
    return [list(outs[q * n:(q + 1) * n]) for q in range(3)]


WEIGHTS = ("ffn1_norm", "ffn1_w_gate", "ffn1_w_up", "ffn1_w_down", "mix_norm", "w_in", "q_norm", "k_norm",
           "attn_sinks", "rel_bias", "pool_w", "pool_scale", "w_out", "ffn2_norm", "ffn2_w_gate", "ffn2_w_up",
           "ffn2_w_down")
BIG = (("ffn1_w_gate", True), ("ffn1_w_up", True), ("ffn1_w_down", False), ("w_in", True), ("w_out", False),
       ("ffn2_w_gate", True), ("ffn2_w_up", True), ("ffn2_w_down", False))


def kernel(x, ffn1_norm, ffn1_w_gate, ffn1_w_up, ffn1_w_down, mix_norm, w_in, q_norm, k_norm, attn_sinks, rel_bias, pool_w, pool_scale, w_out, ffn2_norm, ffn2_w_gate, ffn2_w_up, ffn2_w_down, loss_target, m_ffn1_norm, m_ffn1_w_gate, m_ffn1_w_up, m_ffn1_w_down, m_mix_norm, m_w_in, m_q_norm, m_k_norm, m_attn_sinks, m_rel_bias, m_pool_w, m_pool_scale, m_w_out, m_ffn2_norm, m_ffn2_w_gate, m_ffn2_w_up, m_ffn2_w_down, v_ffn1_norm, v_ffn1_w_gate, v_ffn1_w_up, v_ffn1_w_down, v_mix_norm, v_w_in, v_q_norm, v_k_norm, v_attn_sinks, v_rel_bias, v_pool_w, v_pool_scale, v_w_out, v_ffn2_norm, v_ffn2_w_gate, v_ffn2_w_up, v_ffn2_w_down):
    args = dict(locals())
    w = {n: args[n] for n in WEIGHTS}
    m = {n: args["m_" + n] for n in WEIGHTS}
    v = {n: args["v_" + n] for n in WEIGHTS}

    as_rows = lambda a, tr: jnp.swapaxes(a, 1, 2) if tr else a
    shard = jnp.concatenate([as_rows(w[n], tr)[0].astype(bf16) for n, tr in BIG], axis=0)
    exchanges = _GatheredWeights(shard, x[0], ffn1_norm)
    (dh1, dx1), (dw1, _, _, _), small = _local_step(
        x[0], loss_target[0], exchanges, ffn1_norm, mix_norm, ffn2_norm, q_norm, k_norm, attn_sinks,
        rel_bias, pool_w[0], pool_scale)

    nrows1 = len(G1_PIECES) * FS
    for_x, for_y, own1, small_tot, gx = _reduce_scatter_ffn1_head(dw1, _pack_small(small),
                                                                  (dh1, x[0], ffn1_norm, dx1))
    ssem, rsem, for_x, for_y, from_x, from_y, small_tot, gx = _rs1_tail_start(
        for_x, for_y, lax.empty((nrows1, D), bf16), lax.empty((nrows1, D), bf16), small_tot, gx)
    own_rest, landed_rest = exchanges.mix_ffn2_grads_parts(small_tot)

    grads, deltas, new_m, new_v = {}, {}, {}, {}
    rest = [k for k in range(len(BIG)) if k not in G1_PIECES]
    rows_of = lambda t, ks: [as_rows(t[BIG[k][0]], BIG[k][1]) for k in ks]
    rest_out = _adamw_big(rest, rows_of(w, rest), rows_of(m, rest), rows_of(v, rest), own_rest, [landed_rest],
                          "adamw_rest")
    ffn1 = list(G1_PIECES)
    ffn1_out = _adamw_big(ffn1, rows_of(w, ffn1), rows_of(m, ffn1), rows_of(v, ffn1), own1, [from_x, from_y],
                          "adamw_ffn1", in_flight=(ssem, rsem, [for_x, for_y], rest_out[0][0]))
    for ks, out in ((rest, rest_out), (ffn1, ffn1_out)):
        for i, k in enumerate(ks):
            n, tr = BIG[k]
            grads[n], deltas[n], new_m[n], new_v[n] = [as_rows(o[i], tr) for o in out]
    small_names = [n for n in SMALL_NAMES if n != "loss"]
    for n in small_names:
        grads[n] = _unpack_small(small_tot, n)
    ds, nms, nvs = _adamw_small([w[n] for n in small_names], [m[n] for n in small_names], [v[n] for n in small_names],
                                [grads[n] for n in small_names], "adamw_small")
    for i, n in enumerate(small_names):
        deltas[n], new_m[n], new_v[n] = ds[i], nms[i], nvs[i]
    loss = small_tot[LOSS_ROW, 0]
    return (loss, gx, *[grads[n] for n in WEIGHTS], *[deltas[n] for n in WEIGHTS],
            *[new_m[n] for n in WEIGHTS], *[new_v[n] for n in WEIGHTS])
```

```python
import jax
import jax.numpy as jnp
import numpy as np
from jax import lax
from jax.experimental import pallas as pl
from jax.experimental.pallas import tpu as pltpu

f32, bf16, i32 = jnp.float32, jnp.bfloat16, jnp.int32
SDS = jax.ShapeDtypeStruct

D = 1024
F = 2816
HD = 64
NH = 8
NKV = 2
GQA = NH // NKV
DATTN = NH * HD
DKV = NKV * HD
DPOOL = 512
POOL_WINDOWS = (2, 4, 8, 16)
PGD = DPOOL // len(POOL_WINDOWS)
DIN = DATTN + 2 * DKV + DPOOL
DMIX = DATTN + DPOOL
BLK = 128
NBUCK = 32
MAX_DISTANCE = 128
EPS = 1e-6
NEG = -1e30
SCALE = HD ** -0.5

ADAM_LR, ADAM_B1, ADAM_B2, ADAM_EPS, ADAM_WD, ADAM_STEP = 0.001, 0.9, 0.999, 1e-08, 0.01, 10

NDEV = 8
FS = F // NDEV
INS = DIN // NDEV
OUTS = DMIX // NDEV
PIECE_ROWS = (FS, FS, FS, INS, OUTS, FS, FS, FS)
PIECE_OFF = tuple(int(v) for v in np.cumsum((0,) + PIECE_ROWS[:-1]))
PACK_ROWS = sum(PIECE_ROWS)

VMEM_LIMIT_V7X = 56 * 1024 * 1024

MESH = pl.DeviceIdType.MESH


def _cparams(sem=None, vmem=None):
    return pltpu.CompilerParams(dimension_semantics=sem, vmem_limit_bytes=vmem)


def _nt(a, b):
    return lax.dot_general(a, b, (((1,), (1,)), ((), ())), preferred_element_type=f32)


def _tn(a, b):
    return lax.dot_general(a, b, (((0,), (0,)), ((), ())), preferred_element_type=f32)


def _nn(a, b):
    return jnp.dot(a, b, preferred_element_type=f32)


def _sigmoid(x):
    return 1.0 / (1.0 + jnp.exp(-x))


def _fresh_copy(a, name):
    T = a.shape[1]
    tm = min(512, T)

    def body(a_ref, o_ref):
        o_ref[...] = a_ref[...]

    tok = pl.BlockSpec((1, tm, D), lambda i: (0, i, 0))
    return pl.pallas_call(body, grid=(T // tm,), in_specs=[tok], out_specs=tok, out_shape=SDS(a.shape, a.dtype),
                          name=name)(a)


def _norm_fwd(x, g, name):
    T = x.shape[0]
    tm = min(512, T)

    def body(x_ref, g_ref, h_ref):
        xv = x_ref[...]
        r = lax.rsqrt(jnp.mean(xv * xv, axis=-1, keepdims=True) + EPS)
        h_ref[...] = (xv * r * g_ref[...]).astype(bf16)

    return pl.pallas_call(
        body, grid=(T // tm,),
        in_specs=[pl.BlockSpec((tm, D), lambda i: (i, 0)), pl.BlockSpec((1, D), lambda i: (0, 0))],
        out_specs=pl.BlockSpec((tm, D), lambda i: (i, 0)),
        out_shape=SDS((T, D), bf16), name=name)(x, g)


FFN_ROW_CHUNK = 256


def _ffn_tiles(T):
    return min(1024, T), 256


def _ffn_fwd(h, w, x, target, next_gain, name):
    T = h.shape[0]
    tm, tf = _ffn_tiles(T)
    nf = F // tf
    with_loss = target is not None
    assert with_loss != (next_gain is not None)

    def body(*refs):
        if with_loss:
            h_ref, w_ref, x_hbm, t_hbm, xo_ref, g_ref, u_ref, dyb_ref, loss_ref, tbuf, sem = refs
        else:
            h_ref, w_ref, x_hbm, gain_ref, xo_ref, g_ref, u_ref, hn_ref, sem = refs
        fi = pl.program_id(0)

        @pl.when(fi == 0)
        def _():
            cp = pltpu.make_async_copy(x_hbm, xo_ref, sem)
            cp.start()
            cp.wait()

        wgu = w_ref[0:2].reshape(2 * tf, D)
        for r in range(0, T, tm):
            rows = slice(r, r + tm)
            gu = _nt(h_ref[rows, :], wgu)
            gate, up = gu[:, :tf], gu[:, tf:]
            act = gate * _sigmoid(gate) * up
            g_ref[0, rows, :] = gate.astype(bf16)
            u_ref[0, rows, :] = up.astype(bf16)
            xo_ref[rows, :] += _nn((0.5 * act).astype(bf16), w_ref[2])

        if with_loss:
            @pl.when(fi == nf - 1)
            def _():
                lanes = jnp.zeros((1, 128), f32)
                for r in range(0, T, tm):
                    rows = slice(r, r + tm)
                    cp = pltpu.make_async_copy(t_hbm.at[pl.ds(r, tm), :], tbuf, sem)
                    cp.start()
                    cp.wait()
                    e = xo_ref[rows, :] - tbuf[...]
                    dy = e * (1.0 / D)
                    xo_ref[rows, :] = dy
                    dyb_ref[rows, :] = (0.5 * dy).astype(bf16)
                    col = jnp.sum(e * e, axis=0, keepdims=True) * (0.5 / D)
                    for k in range(D // 128):
                        lanes = lanes + col[:, 128 * k:128 * (k + 1)]
                loss_ref[...] = lanes
        else:
            @pl.when(fi == nf - 1)
            def _():
                for r in range(0, T, FFN_ROW_CHUNK):
                    rows = slice(r, r + FFN_ROW_CHUNK)
                    xv = xo_ref[rows, :]
                    rstd = lax.rsqrt(jnp.mean(xv * xv, axis=-1, keepdims=True) + EPS)
                    hn_ref[rows, :] = (xv * rstd * gain_ref[...]).astype(bf16)

    tok = pl.BlockSpec((T, D), lambda f: (0, 0))
    act_spec = pl.BlockSpec((1, T, tf), lambda f: (f, 0, 0))
    hbm = pl.BlockSpec(memory_space=pl.ANY)
    in_specs = [tok, pl.BlockSpec((3, tf, D), lambda f: (0, f, 0)), hbm]
    out_specs = [tok, act_spec, act_spec]
    out_shape = [SDS((T, D), f32), SDS((nf, T, tf), bf16), SDS((nf, T, tf), bf16)]
    scratch = [pltpu.SemaphoreType.DMA]
    args = [h, w, x]
    if with_loss:
        in_specs.append(hbm)
        args.append(target)
        out_specs += [tok, pl.BlockSpec((1, 128), lambda f: (0, 0))]
        out_shape += [SDS((T, D), bf16), SDS((1, 128), f32)]
        scratch = [pltpu.VMEM((tm, D), f32)] + scratch
    else:
        in_specs.append(pl.BlockSpec((1, D), lambda f: (0, 0)))
        args.append(next_gain)
        out_specs.append(tok)
        out_shape.append(SDS((T, D), bf16))
    return pl.pallas_call(
        body, grid=(nf,), in_specs=in_specs, out_specs=out_specs, out_shape=tuple(out_shape), scratch_shapes=scratch,
        compiler_params=_cparams(("arbitrary",), VMEM_LIMIT_V7X), name=name)(*args)


NORM_BWD_ROWS = 512


def _norm_bwd_scratch(dh_in_hbm, with_bf16):
    buf = lambda dt: pltpu.VMEM((2, NORM_BWD_ROWS, D), dt)
    return [buf(f32), buf(f32), buf(f32) if dh_in_hbm else None, buf(f32), buf(bf16) if with_bf16 else None,
            pltpu.SemaphoreType.DMA((6,)), pltpu.SemaphoreType.DMA((4,))]


def _norm_bwd_rows(T, dh_src, x_hbm, gain_ref, dr_hbm, dx_hbm, dxb_hbm, xbuf, rbuf, hbuf, obuf, obb, in_sems, out_sems):
    tm = min(NORM_BWD_ROWS, T)
    nchunk = T // tm

    def loads(i):
        s, rows = i % 2, pl.ds(i * tm, tm)
        cps = [pltpu.make_async_copy(x_hbm.at[rows, :], xbuf.at[s, pl.ds(0, tm), :], in_sems.at[3 * s]),
               pltpu.make_async_copy(dr_hbm.at[rows, :], rbuf.at[s, pl.ds(0, tm), :], in_sems.at[3 * s + 1])]
        if hbuf is not None:
            cps.append(pltpu.make_async_copy(dh_src.at[rows, :], hbuf.at[s, pl.ds(0, tm), :], in_sems.at[3 * s + 2]))
        return cps

    def stores(i):
        s, rows = i % 2, pl.ds(i * tm, tm)
        cps = [pltpu.make_async_copy(obuf.at[s, pl.ds(0, tm), :], dx_hbm.at[rows, :], out_sems.at[2 * s])]
        if dxb_hbm is not None:
            cps.append(pltpu.make_async_copy(obb.at[s, pl.ds(0, tm), :], dxb_hbm.at[rows, :], out_sems.at[2 * s + 1]))
        return cps

    for cp in loads(0):
        cp.start()
    dg = jnp.zeros((1, D), f32)
    for i in range(nchunk):
        s = i % 2
        if i + 1 < nchunk:
            for cp in loads(i + 1):
                cp.start()
        for cp in loads(i):
            cp.wait()
        if i >= 2:
            for cp in stores(i - 2):
                cp.wait()
        xv = xbuf[s, 0:tm, :]
        rstd = lax.rsqrt(jnp.mean(xv * xv, axis=-1, keepdims=True) + EPS)
        xh = xv * rstd
        dhv = hbuf[s, 0:tm, :] if hbuf is not None else dh_src[i * tm:(i + 1) * tm, :]
        dxh = dhv * gain_ref[...]
        dx = rbuf[s, 0:tm, :] + rstd * (dxh - xh * jnp.mean(dxh * xh, axis=-1, keepdims=True))
        obuf[s, 0:tm, :] = dx
        if dxb_hbm is not None:
            obb[s, 0:tm, :] = dx.astype(bf16)
        dg = dg + jnp.sum(dhv * xh, axis=0, keepdims=True)
        for cp in stores(i):
            cp.start()
    for i in range(max(nchunk - 2, 0), nchunk):
        for cp in stores(i):
            cp.wait()
    return dg


def _ffn_bwd(dob, h, gate, up, w, norm, name):
    T = h.shape[0]
    _, tf = _ffn_tiles(T)
    nf = F // tf
    nin = 5 if norm is None else 8
    nout = 2 if norm is None else 4

    def body(*refs):
        do_hbm, h_hbm, g_ref, u_ref, w_ref = refs[:5]
        dw_ref = refs[nin + nout - 1]
        do_v, h_v, dh_acc, dgu_s, act_s, sems = refs[nin + nout:nin + nout + 6]
        fi = pl.program_id(0)

        @pl.when(fi == 0)
        def _():
            loads = [pltpu.make_async_copy(do_hbm, do_v, sems.at[0]), pltpu.make_async_copy(h_hbm, h_v, sems.at[1])]
            for cp in loads:
                cp.start()
            dh_acc[...] = jnp.zeros_like(dh_acc)
            for cp in loads:
                cp.wait()

        wgu = w_ref[0:2].reshape(2 * tf, D)
        for r in range(0, T, FFN_ROW_CHUNK):
            rows = slice(r, r + FFN_ROW_CHUNK)
            dov = do_v[rows, :]
            gv = g_ref[0, rows, :].astype(f32)
            uv = u_ref[0, rows, :].astype(f32)
            sg = _sigmoid(gv)
            sil = gv * sg
            dact = _nt(dov, w_ref[2])
            dup = dact * sil
            dgate = dact * uv * (sg * (1.0 + gv * (1.0 - sg)))
            dgu = jnp.concatenate([dgate.astype(bf16), dup.astype(bf16)], axis=1)
            dgu_s[rows, :] = dgu
            act_s[rows, :] = (sil * uv).astype(bf16)
            dh_acc[rows, :] += _nn(dgu, wgu)
        dw_ref[0:2] = _tn(dgu_s[...], h_v[...]).reshape(2, tf, D).astype(bf16)
        dw_ref[2] = _tn(act_s[...], do_v[...]).astype(bf16)

        @pl.when(fi == nf - 1)
        def _():
            if norm is None:
                out = pltpu.make_async_copy(dh_acc, refs[nin], sems.at[0])
                out.start()
                out.wait()
            else:
                x_hbm, gain_ref, dr_hbm, dx_hbm, dxb_hbm, dg_ref = refs[5:11]
                xbuf, rbuf, obuf, obb, in_sems, out_sems = refs[nin + nout + 6:]
                dg_ref[...] = _norm_bwd_rows(T, dh_acc, x_hbm, gain_ref, dr_hbm, dx_hbm, dxb_hbm,
                                             xbuf, rbuf, None, obuf, obb, in_sems, out_sems)

    act_spec = pl.BlockSpec((1, T, tf), lambda f: (f, 0, 0))
    wspec = pl.BlockSpec((3, tf, D), lambda f: (0, f, 0))
    vec = pl.BlockSpec((1, D), lambda f: (0, 0))
    hbm = pl.BlockSpec(memory_space=pl.ANY)
    in_specs, out_specs = [hbm, hbm, act_spec, act_spec, wspec], [hbm, wspec]
    out_shape, args = [SDS((T, D), f32), SDS((3, F, D), bf16)], [dob, h, gate, up, w]
    scratch = [pltpu.VMEM((T, D), bf16), pltpu.VMEM((T, D), bf16), pltpu.VMEM((T, D), f32),
               pltpu.VMEM((T, 2 * tf), bf16), pltpu.VMEM((T, tf), bf16), pltpu.SemaphoreType.DMA((2,))]
    if norm is not None:
        in_specs += [hbm, vec, hbm]
        out_specs = [hbm, hbm, vec, wspec]
        out_shape = [SDS((T, D), f32), SDS((T, D), bf16), SDS((1, D), f32), SDS((3, F, D), bf16)]
        args += list(norm)
        scratch += [sc for sc in _norm_bwd_scratch(False, True) if sc is not None]
    return pl.pallas_call(
        body, grid=(nf,), in_specs=in_specs, out_specs=out_specs, out_shape=tuple(out_shape), scratch_shapes=scratch,
        compiler_params=_cparams(("arbitrary",), VMEM_LIMIT_V7X), name=name)(*args)


def _in_proj_fwd(h, wint, name):
    T = h.shape[0]
    tm = min(512, T)

    def body(h_ref, w_ref, z_ref):
        z_ref[...] = _nt(h_ref[...], w_ref[...])

    return pl.pallas_call(
        body, grid=(T // tm,),
        in_specs=[pl.BlockSpec((tm, D), lambda i: (i, 0)), pl.BlockSpec((DIN, D), lambda i: (0, 0))],
        out_specs=pl.BlockSpec((tm, DIN), lambda i: (i, 0)),
        out_shape=SDS((T, DIN), f32), name=name)(h, wint)


def _in_proj_bwd(dz, wint, h, norm, out_scale, name):
    x, g, dres = norm
    T = h.shape[0]
    tm = min(512, T)
    nt = T // tm

    def body(dz_ref, w_ref, h_ref, x_ref, g_ref, dr_ref, dx_ref, dxb_ref, dg_ref, dw_ref, acc):
        i = pl.program_id(0)
        dzb = dz_ref[...].astype(bf16)
        dhv = _nn(dzb, w_ref[...])
        part = _tn(dzb, h_ref[...])
        xv = x_ref[...]
        rstd = lax.rsqrt(jnp.mean(xv * xv, axis=-1, keepdims=True) + EPS)
        xh = xv * rstd
        dxh = dhv * g_ref[...]
        dx = dr_ref[...] + rstd * (dxh - xh * jnp.mean(dxh * xh, axis=-1, keepdims=True))
        dx_ref[...] = dx
        dxb_ref[...] = (out_scale * dx).astype(bf16)
        dg = jnp.sum(dhv * xh, axis=0, keepdims=True)

        @pl.when(i == 0)
        def _():
            acc[...] = part
            dg_ref[...] = dg

        @pl.when(i > 0)
        def _():
            acc[...] += part
            dg_ref[...] += dg

        @pl.when(i == nt - 1)
        def _():
            dw_ref[...] = acc[...].astype(bf16)

    wspec = pl.BlockSpec((DIN, D), lambda i: (0, 0))
    tok = pl.BlockSpec((tm, D), lambda i: (i, 0))
    vec = pl.BlockSpec((1, D), lambda i: (0, 0))
    return pl.pallas_call(
        body, grid=(nt,),
        in_specs=[pl.BlockSpec((tm, DIN), lambda i: (i, 0)), wspec, tok, tok, vec, tok],
        out_specs=[tok, tok, vec, wspec],
        out_shape=(SDS((T, D), f32), SDS((T, D), bf16), SDS((1, D), f32), SDS((DIN, D), bf16)),
        scratch_shapes=[pltpu.VMEM((DIN, D), f32)],
        compiler_params=_cparams(("arbitrary",)), name=name)(dz, wint, h, x, g, dres)


def _out_proj_fwd(ymix, wout, x, g, name):
    T = x.shape[0]
    tm = min(512, T)

    def body(y_ref, w_ref, x_ref, g_ref, o_ref, h_ref):
        o = x_ref[...] + _nn(y_ref[...], w_ref[...])
        o_ref[...] = o
        r = lax.rsqrt(jnp.mean(o * o, axis=-1, keepdims=True) + EPS)
        h_ref[...] = (o * r * g_ref[...]).astype(bf16)

    tok = pl.BlockSpec((tm, D), lambda i: (i, 0))
    return pl.pallas_call(
        body, grid=(T // tm,),
        in_specs=[pl.BlockSpec((tm, DMIX), lambda i: (i, 0)), pl.BlockSpec((DMIX, D), lambda i: (0, 0)), tok,
                  pl.BlockSpec((1, D), lambda i: (0, 0))],
        out_specs=[tok, tok], out_shape=(SDS((T, D), f32), SDS((T, D), bf16)), name=name)(ymix, wout, x, g)


def _out_proj_bwd(dxb, wout, ymix, name):
    T = dxb.shape[0]
    tm = min(512, T)
    nt = T // tm

    def body(dx_ref, w_ref, y_ref, dy_ref, dw_ref, acc):
        i = pl.program_id(0)
        dxv = dx_ref[...]
        dy_ref[...] = _nt(dxv, w_ref[...])
        part = _tn(y_ref[...], dxv)

        @pl.when(i == 0)
        def _():
            acc[...] = part

        @pl.when(i > 0)
        def _():
            acc[...] += part

        @pl.when(i == nt - 1)
        def _():
            dw_ref[...] = acc[...].astype(bf16)

    wspec = pl.BlockSpec((DMIX, D), lambda i: (0, 0))
    return pl.pallas_call(
        body, grid=(nt,),
        in_specs=[pl.BlockSpec((tm, D), lambda i: (i, 0)), wspec, pl.BlockSpec((tm, DMIX), lambda i: (i, 0))],
        out_specs=[pl.BlockSpec((tm, DMIX), lambda i: (i, 0)), wspec],
        out_shape=(SDS((T, DMIX), f32), SDS((DMIX, D), bf16)),
        scratch_shapes=[pltpu.VMEM((DMIX, D), f32)],
        compiler_params=_cparams(("arbitrary",)), name=name)(dxb, wout, ymix)


def _t5_bucket_table():
    ql = np.arange(BLK)[:, None]
    kl = np.arange(2 * BLK)[None, :]
    n = np.maximum(ql + BLK - kl, 0)
    max_exact = NBUCK // 2
    large = max_exact + (np.log(np.maximum(n, 1) / max_exact) / np.log(MAX_DISTANCE / max_exact)
                         * (NBUCK - max_exact)).astype(np.int32)
    large = np.minimum(large, NBUCK - 1)
    return np.where(n < max_exact, n, large).astype(np.int32)


def _fill_bias(bk_ref, rb_ref, bias_scr):
    bk = bk_ref[...]
    for h in range(NH):
        def step(b, acc, h=h):
            return acc + jnp.where(bk == b, rb_ref[b, h], 0.0)
        bias_scr[h] = lax.fori_loop(0, NBUCK, step, jnp.zeros((BLK, 2 * BLK), f32))


MIX_SUB = 4


class _Window:
    def __init__(self, zc_ref, zp_ref, n, s):
        self.blk = n * MIX_SUB + s
        self.cur = lambda a, b: zc_ref[s * BLK:(s + 1) * BLK, a:b]
        self.prev = (lambda a, b: zp_ref[:, a:b]) if s == 0 else (lambda a, b: zc_ref[(s - 1) * BLK:s * BLK, a:b])


def _attn_qkv(win, kh, qg, kg):
    kc = DATTN + HD * kh
    vc = DATTN + DKV + HD * kh
    kx = jnp.concatenate([win.prev(kc, kc + HD), win.cur(kc, kc + HD)], axis=0)
    vx = jnp.concatenate([win.prev(vc, vc + HD), win.cur(vc, vc + HD)], axis=0)
    qx = jnp.concatenate([win.cur(HD * (GQA * kh + g), HD * (GQA * kh + g + 1)) for g in range(GQA)], axis=0)
    rq = lax.rsqrt(jnp.mean(qx * qx, axis=-1, keepdims=True) + EPS)
    rk = lax.rsqrt(jnp.mean(kx * kx, axis=-1, keepdims=True) + EPS)
    qhat, khat = qx * rq, kx * rk
    return dict(qhat=qhat, khat=khat, rq=rq, rk=rk, qsb=(qhat * (qg * SCALE)).astype(bf16),
                knb=(khat * kg).astype(bf16), vb=vx.astype(bf16))


def _window_masks(n):
    row = lax.broadcasted_iota(i32, (GQA * BLK, 2 * BLK), 0) & (BLK - 1)
    col = lax.broadcasted_iota(i32, (GQA * BLK, 2 * BLK), 1)
    band = (col > row) & (col <= row + BLK)
    return band & ((col >= BLK) | (n > 0)), band


def _attn_probs(a, kh, sk_ref, bias_scr, mask):
    s = _nt(a["qsb"], a["knb"]) + bias_scr[GQA * kh:GQA * (kh + 1)].reshape(GQA * BLK, 2 * BLK)
    s = jnp.where(mask, s, NEG)
    ridx = lax.broadcasted_iota(i32, (GQA * BLK, 1), 0)
    sink = jnp.full((GQA * BLK, 1), sk_ref[GQA * kh + GQA - 1], f32)
    for g in range(GQA - 2, -1, -1):
        sink = jnp.where(ridx < (g + 1) * BLK, sk_ref[GQA * kh + g], sink)
    m = jnp.maximum(jnp.max(s, axis=-1, keepdims=True), sink)
    e = jnp.exp(s - m)
    den = jnp.sum(e, axis=-1, keepdims=True) + jnp.exp(sink - m)
    return e / den


POOL_STEPS = {2: (1,), 4: (1, 2), 8: (1, 2, 4), 16: (1, 2, 4, 8)}


def _pool_group(win, g, w):
    n = win.blk
    c0 = DATTN + 2 * DKV + PGD * g
    uc = win.cur(c0, c0 + PGD)
    up = jnp.where(n > 0, win.prev(c0, c0 + PGD), 0.0)
    sm = jnp.concatenate([up, uc], axis=0)
    for k in POOL_STEPS[w]:
        sm = sm + pltpu.roll(sm, k, axis=0)
    pos = n * BLK + lax.broadcasted_iota(i32, (BLK, 1), 0) + 1
    cnt = jnp.minimum(pos, w).astype(f32)
    return sm[BLK:2 * BLK] / cnt - uc, cnt


def _mix_fwd(z, qg, kg, sinks, relb, bucket, pool_w, pscale, name):
    T = z.shape[0]
    step_rows = MIX_SUB * BLK
    nsteps = T // step_rows

    def body(zc_ref, zp_ref, qg_ref, kg_ref, sk_ref, rb_ref, bk_ref, pw_ref, ps_ref, y_ref, p_ref, bias_scr, yacc):
        n = pl.program_id(0)

        @pl.when(n == 0)
        def _():
            _fill_bias(bk_ref, rb_ref, bias_scr)

        first_mask, mask = _window_masks(n)
        for s in range(MIX_SUB):
            win = _Window(zc_ref, zp_ref, n, s)
            rows = slice(s * BLK, (s + 1) * BLK)
            for kh in range(NKV):
                a = _attn_qkv(win, kh, qg_ref[...], kg_ref[...])
                pb = _attn_probs(a, kh, sk_ref, bias_scr, first_mask if s == 0 else mask).astype(bf16)
                p_ref[s, GQA * kh:GQA * (kh + 1)] = pb.reshape(GQA, BLK, 2 * BLK)
                o = _nn(pb, a["vb"])
                for g in range(GQA):
                    hc = HD * (GQA * kh + g)
                    yacc[rows, hc:hc + HD] = o[g * BLK:(g + 1) * BLK]
            for g, w in enumerate(POOL_WINDOWS):
                pooled, _ = _pool_group(win, g, w)
                yp = _nn(pooled.astype(bf16), pw_ref[g].astype(bf16)) * ps_ref[:, PGD * g:PGD * (g + 1)]
                yacc[rows, DATTN + PGD * g:DATTN + PGD * (g + 1)] = yp
        y_ref[...] = yacc[...].astype(bf16)

    full = lambda *shape: pl.BlockSpec(shape, lambda n: (0,) * len(shape))
    smem = pl.BlockSpec(memory_space=pltpu.SMEM)
    return pl.pallas_call(
        body, grid=(nsteps,),
        in_specs=[pl.BlockSpec((step_rows, DIN), lambda n: (n, 0)),
                  pl.BlockSpec((BLK, DIN), lambda n: (jnp.maximum(n * MIX_SUB - 1, 0), 0)),
                  full(1, HD), full(1, HD), smem, smem, full(BLK, 2 * BLK),
                  full(len(POOL_WINDOWS), PGD, PGD), full(1, DPOOL)],
        out_specs=[pl.BlockSpec((step_rows, DMIX), lambda n: (n, 0)),
                   pl.BlockSpec((MIX_SUB, NH, BLK, 2 * BLK), lambda n: (n, 0, 0, 0))],
        out_shape=(SDS((T, DMIX), bf16), SDS((T // BLK, NH, BLK, 2 * BLK), bf16)),
        scratch_shapes=[pltpu.VMEM((NH, BLK, 2 * BLK), f32), pltpu.VMEM((step_rows, DMIX), f32)],
        compiler_params=_cparams(("arbitrary",)), name=name)(z, z, qg, kg, sinks, relb, bucket, pool_w, pscale)


def _mix_bwd(z, dy, probs, qg, kg, relb, bucket, pool_w, pscale, name):
    T = z.shape[0]
    step_rows = MIX_SUB * BLK
    nsteps = T // step_rows

    def body(zc_ref, zp_ref, dy_ref, p_ref, qg_ref, kg_ref, bk_ref, pw_ref, ps_ref,
             dz_ref, dqg_ref, dkg_ref, dsk_ref, drb_ref, dpw_ref, dps_ref, dbias_scr):
        n = pl.program_id(0)

        @pl.when(n == 0)
        def _():
            dbias_scr[...] = jnp.zeros_like(dbias_scr)
            dqg_ref[...] = jnp.zeros_like(dqg_ref)
            dkg_ref[...] = jnp.zeros_like(dkg_ref)
            dpw_ref[...] = jnp.zeros_like(dpw_ref)
            dps_ref[...] = jnp.zeros_like(dps_ref)

        qg, kg = qg_ref[...], kg_ref[...]
        for s in range(MIX_SUB):
            win = _Window(zc_ref, zp_ref, n, s)
            blk = win.blk
            rows = pl.ds(pl.multiple_of(blk * BLK, BLK), BLK)
            prow = pl.ds(pl.multiple_of(jnp.maximum(blk - 1, 0) * BLK, BLK), BLK)
            dyr = slice(s * BLK, (s + 1) * BLK)

            def into_prev(fn, s=s):
                if s == 0:
                    pl.when(n > 0)(fn)
                else:
                    fn()

            for kh in range(NKV):
                a = _attn_qkv(win, kh, qg, kg)
                pb = p_ref[s, GQA * kh:GQA * (kh + 1)].reshape(GQA * BLK, 2 * BLK)
                p = pb.astype(f32)
                do = jnp.concatenate([dy_ref[dyr, HD * (GQA * kh + g):HD * (GQA * kh + g + 1)] for g in range(GQA)],
                                     axis=0).astype(bf16)
                dv = _tn(pb, do)
                dp = _nt(do, a["vb"])
                delta = jnp.sum(p * dp, axis=-1, keepdims=True)
                ds = p * (dp - delta)
                for g in range(GQA):
                    dbias_scr[GQA * kh + g] += ds[g * BLK:(g + 1) * BLK]
                dsb = ds.astype(bf16)
                dqn = _nn(dsb, a["knb"]) * SCALE
                dkn = _tn(dsb, a["qsb"])
                qhat, khat = a["qhat"], a["khat"]
                dqg_ref[...] += jnp.sum(dqn * qhat, axis=0, keepdims=True)
                dkg_ref[...] += jnp.sum(dkn * khat, axis=0, keepdims=True)
                dqh = dqn * qg
                dq = a["rq"] * (dqh - qhat * jnp.mean(dqh * qhat, axis=-1, keepdims=True))
                dkh = dkn * kg
                dk = a["rk"] * (dkh - khat * jnp.mean(dkh * khat, axis=-1, keepdims=True))
                kc = DATTN + HD * kh
                vc = DATTN + DKV + HD * kh
                for g in range(GQA):
                    hc = HD * (GQA * kh + g)
                    dz_ref[rows, hc:hc + HD] = dq[g * BLK:(g + 1) * BLK]
                dz_ref[rows, kc:kc + HD] = dk[BLK:2 * BLK]
                dz_ref[rows, vc:vc + HD] = dv[BLK:2 * BLK]

                def kv_prev(dk=dk, dv=dv, kc=kc, vc=vc, prow=prow):
                    dz_ref[prow, kc:kc + HD] += dk[0:BLK]
                    dz_ref[prow, vc:vc + HD] += dv[0:BLK]

                into_prev(kv_prev)

            for g, w in enumerate(POOL_WINDOWS):
                c0 = DATTN + 2 * DKV + PGD * g
                pooled, cnt = _pool_group(win, g, w)
                pb = pooled.astype(bf16)
                wb = pw_ref[g].astype(bf16)
                dyp = dy_ref[dyr, DATTN + PGD * g:DATTN + PGD * (g + 1)]
                ypre = _nn(pb, wb)
                dps_ref[:, PGD * g:PGD * (g + 1)] += jnp.sum(dyp * ypre, axis=0, keepdims=True)
                dyg = (dyp * ps_ref[:, PGD * g:PGD * (g + 1)]).astype(bf16)
                dpw_ref[g] += _tn(pb, dyg)
                dpooled = _nt(dyg, wb)
                due = jnp.concatenate([jnp.zeros((BLK, PGD), f32), dpooled / cnt], axis=0)
                for k in POOL_STEPS[w]:
                    due = due + pltpu.roll(due, 2 * BLK - k, axis=0)
                dz_ref[rows, c0:c0 + PGD] = due[BLK:2 * BLK] - dpooled

                def pool_prev(due=due, c0=c0, prow=prow):
                    dz_ref[prow, c0:c0 + PGD] += due[0:BLK]

                into_prev(pool_prev)

        @pl.when(n == nsteps - 1)
        def _():
            bk = bk_ref[...]
            ri = lax.broadcasted_iota(i32, (NBUCK, NH), 0)
            ci = lax.broadcasted_iota(i32, (NBUCK, NH), 1)

            def step(b, acc):
                for h in range(NH):
                    sel = jnp.where(bk == b, dbias_scr[h], 0.0)
                    tot = jnp.sum(jnp.sum(sel, axis=1, keepdims=True), axis=0, keepdims=True)
                    acc = acc + jnp.where((ri == b) & (ci == h), tot, 0.0)
                return acc

            drb_ref[...] = lax.fori_loop(0, NBUCK, step, jnp.zeros((NBUCK, NH), f32))
            lane = lax.broadcasted_iota(i32, (1, 128), 1)
            dsk = jnp.zeros((1, 128), f32)
            for h in range(NH):
                tot = jnp.sum(jnp.sum(dbias_scr[h], axis=1, keepdims=True), axis=0, keepdims=True)
                dsk = dsk - jnp.where(lane == h, tot, 0.0)
            dsk_ref[...] = dsk

    full = lambda *shape: pl.BlockSpec(shape, lambda n: (0,) * len(shape))
    npg = len(POOL_WINDOWS)
    return pl.pallas_call(
        body, grid=(nsteps,),
        in_specs=[pl.BlockSpec((step_rows, DIN), lambda n: (n, 0)),
                  pl.BlockSpec((BLK, DIN), lambda n: (jnp.maximum(n * MIX_SUB - 1, 0), 0)),
                  pl.BlockSpec((step_rows, DMIX), lambda n: (n, 0)),
                  pl.BlockSpec((MIX_SUB, NH, BLK, 2 * BLK), lambda n: (n, 0, 0, 0)),
                  full(1, HD), full(1, HD), full(BLK, 2 * BLK), full(npg, PGD, PGD), full(1, DPOOL)],
        out_specs=[full(T, DIN), full(1, HD), full(1, HD), full(1, 128), full(NBUCK, NH),
                   full(npg, PGD, PGD), full(1, DPOOL)],
        out_shape=(SDS((T, DIN), f32), SDS((1, HD), f32), SDS((1, HD), f32), SDS((1, 128), f32),
                   SDS((NBUCK, NH), f32), SDS((npg, PGD, PGD), f32), SDS((1, DPOOL), f32)),
        scratch_shapes=[pltpu.VMEM((NH, BLK, 2 * BLK), f32)],
        compiler_params=_cparams(("arbitrary",), VMEM_LIMIT_V7X),
        name=name)(z, z, dy, probs, qg, kg, bucket, pool_w, pscale)


class _LocalWeights:
    def __init__(self, w1, wint, wout, w2):
        self.w1, self.wint, self.wout, self.w2 = w1, wint, wout, w2

    def ffn1(self):
        return self.w1

    def first_norm(self, x, gain):
        return _norm_fwd(x, gain, "norm1_fwd")

    def after_ffn1(self, gain, x1):
        return gain

    def mix(self, after):
        return self.wint, self.wout

    def before_out_proj(self, wout, after):
        return wout

    def ffn2(self, after):
        return self.w2

    def out_ffn2_grads_ready(self, dwout, dw2, after):
        return after

    def before_ffn1_bwd(self, dwint, dx1b):
        return dx1b


def _local_step(x, target, weights, g1, gm, g3, qg, kg, sinks, relb, pool_w, pscale):
    bucket = jnp.asarray(_t5_bucket_table())
    sk = sinks.reshape(NH)
    w1 = weights.ffn1()
    h1 = weights.first_norm(x, g1)
    x1, gate1, up1, h2 = _ffn_fwd(h1, w1, x, None, gm, "ffn1_fwd")
    gm = weights.after_ffn1(gm, x1)
    wint, wout = weights.mix(h2)
    z = _in_proj_fwd(h2, wint, "in_proj_fwd")
    ymix, probs = _mix_fwd(z, qg, kg, sk, relb, bucket, pool_w, pscale, "mix_fwd")
    wout = weights.before_out_proj(wout, ymix)
    x2, h3 = _out_proj_fwd(ymix, wout, x1, g3, "out_proj_fwd")
    w2 = weights.ffn2(h3)
    dy, gate2, up2, dyb, loss_lanes = _ffn_fwd(h3, w2, x2, target, None, "ffn2_fwd")

    dx2, dx2b, dg3, dw2 = _ffn_bwd(dyb, h3, gate2, up2, w2, (x2, g3, dy), "ffn2_bwd")
    dymix, dwout = _out_proj_bwd(dx2b, wout, ymix, "out_proj_bwd")
    dymix = weights.out_ffn2_grads_ready(dwout, dw2, dymix)
    dz, dqg, dkg, dsk, drb, dpw, dps = _mix_bwd(z, dymix, probs, qg, kg, relb, bucket, pool_w, pscale, "mix_bwd")
    dx1, dx1b, dgm, dwint = _in_proj_bwd(dz, wint, h2, (x1, gm, dx2), 0.5, "in_proj_bwd")
    dx1b = weights.before_ffn1_bwd(dwint, dx1b)
    dh1, dw1 = _ffn_bwd(dx1b, h1, gate1, up1, w1, None, "ffn1_bwd")
    small = dict(mix_norm=dgm, ffn2_norm=dg3, pool_scale=dps, q_norm=dqg, k_norm=dkg,
                 attn_sinks=dsk[:, :NH], rel_bias=drb, pool_w=dpw, loss=loss_lanes)
    return (dh1, dx1), (dw1, dwint, dwout, dw2), small


SMALL_NAMES = ("ffn1_norm", "mix_norm", "ffn2_norm", "pool_scale", "q_norm", "k_norm", "attn_sinks", "rel_bias",
               "pool_w", "loss")
SMALL_SHAPES = dict(ffn1_norm=(1, D), mix_norm=(1, D), ffn2_norm=(1, D), pool_scale=(1, DPOOL), q_norm=(1, HD),
                    k_norm=(1, HD), attn_sinks=(1, NH), rel_bias=(NBUCK, NH),
                    pool_w=(1, len(POOL_WINDOWS), PGD, PGD), loss=(1, 128))


def _small_rows(name):
    return -(-int(np.prod(SMALL_SHAPES[name])) // 128)


SMALL_OFF = {}
_r = 0
for _n in SMALL_NAMES:
    SMALL_OFF[_n] = _r
    _r += _small_rows(_n)
SMALL_ROWS = -(-_r // 16) * 16
LOSS_ROW = SMALL_OFF["loss"]


def _pack_small(vals):
    parts = []
    for n in SMALL_NAMES:
        size = _small_rows(n) * 128
        if n in vals:
            flat = vals[n].astype(f32).reshape(-1)
            parts.append(jnp.pad(flat, (0, size - flat.shape[0])))
        else:
            parts.append(jnp.zeros((size,), f32))
    flat = jnp.concatenate(parts)
    flat = jnp.pad(flat, (0, SMALL_ROWS * 128 - flat.shape[0]))
    return flat.reshape(SMALL_ROWS, 128)


def _unpack_small(packed, name):
    size = int(np.prod(SMALL_SHAPES[name]))
    r0 = SMALL_OFF[name]
    return packed[r0:r0 + _small_rows(name)].reshape(-1)[:size].reshape(SMALL_SHAPES[name])


def _position():
    return lax.axis_index("x"), lax.axis_index("y"), lax.axis_index("c")


def _dev_index(x, y, c):
    return 4 * x + 2 * y + c


G1_PIECES, MIX_PIECES, F2_PIECES = (0, 1, 2), (3, 4), (5, 6, 7)


def _group_rows(pieces):
    return sum(PIECE_ROWS[k] for k in pieces)


def _shard_piece(s_ref, k):
    return s_ref.at[pl.ds(PIECE_OFF[k], PIECE_ROWS[k]), :]


def _shard_group(s_ref, pieces):
    return s_ref.at[pl.ds(PIECE_OFF[pieces[0]], _group_rows(pieces)), :]


def _weight_pieces(w1_ref=None, wi_ref=None, wo_ref=None, w2_ref=None):
    arrs = {}
    if w1_ref is not None:
        arrs.update({0: w1_ref.at[0], 1: w1_ref.at[1], 2: w1_ref.at[2]})
    if wi_ref is not None:
        arrs[3] = wi_ref
    if wo_ref is not None:
        arrs[4] = wo_ref
    if w2_ref is not None:
        arrs.update({5: w2_ref.at[0], 6: w2_ref.at[1], 7: w2_ref.at[2]})
    return arrs


def _block_rows(arrs, k, dev):
    r = PIECE_ROWS[k]
    return arrs[k].at[pl.ds(pl.multiple_of(_dev_index(*dev) * r, 16), r), :]


NORM_ROWS = 512


def _all_gather_ffn1(shard, x, gain):
    pieces = G1_PIECES
    rest_pieces = MIX_PIECES + F2_PIECES
    half = FS // 2
    T = x.shape[0]
    SIB, X0, X1, Y0, Y1, RELAY_Y, RELAY_X, ON_X, ON_Y, ON_D0, ON_D1 = range(11)

    def body(s_ref, x_ref, g_ref, w1_ref, h_ref, wi_ref, wo_ref, w2_ref, xbuf, hbuf, rest_buf,
             send_sems, recv_sems, local_sem, norm_sems):
        x, y, c = _position()
        me, sib = (x, y, c), (x, y, 1 - c)
        xn, yn, dg = (1 - x, y, c), (x, 1 - y, c), (1 - x, 1 - y, c)
        arrs = _weight_pieces(w1_ref=w1_ref)

        def place_rest():
            rest = _weight_pieces(wi_ref=wi_ref, wo_ref=wo_ref, w2_ref=w2_ref)
            grp = _shard_group(s_ref, rest_pieces)
            load = pltpu.make_async_copy(grp, rest_buf, norm_sems.at[0])
            load.start()
            load.wait()
            base = PIECE_OFF[rest_pieces[0]]
            for k in rest_pieces:
                pltpu.make_async_copy(rest_buf.at[pl.ds(PIECE_OFF[k] - base, PIECE_ROWS[k]), :],
                                      _block_rows(rest, k, me), norm_sems.at[1]).start()
            pltpu.make_async_copy(grp, rest_buf, norm_sems.at[1]).wait()

        def first_norm():
            for r in range(0, T, NORM_ROWS):
                load = pltpu.make_async_copy(x_ref.at[pl.ds(r, NORM_ROWS), :], xbuf, norm_sems.at[0])
                load.start()
                load.wait()
                xv = xbuf[...]
                rs = lax.rsqrt(jnp.mean(xv * xv, axis=-1, keepdims=True) + EPS)
                hbuf[...] = (xv * rs * g_ref[...]).astype(bf16)
                store = pltpu.make_async_copy(hbuf, h_ref.at[pl.ds(r, NORM_ROWS), :], norm_sems.at[1])
                store.start()
                store.wait()

        def rows_of(k, block, hf):
            r = PIECE_ROWS[k]
            start, size = (0, r) if hf is None else (hf * half, half)
            return arrs[k].at[pl.ds(pl.multiple_of(_dev_index(*block) * r + start, 16), size), :]

        def copies(rel, block, hf, to, from_shard=False):
            def src(k):
                if not from_shard:
                    return rows_of(k, block, hf)
                start, size = (0, PIECE_ROWS[k]) if hf is None else (hf * half, half)
                return s_ref.at[pl.ds(PIECE_OFF[k] + start, size), :]
            return [pltpu.make_async_remote_copy(
                src_ref=src(k), dst_ref=rows_of(k, block, hf), send_sem=send_sems.at[rel], recv_sem=recv_sems.at[rel],
                device_id=to, device_id_type=MESH) for k in pieces]

        def waiter(rel, hf):
            nrows = len(pieces) * (FS if hf is None else half)
            grp = s_ref.at[pl.ds(0, nrows), :]
            return pltpu.make_async_remote_copy(src_ref=grp, dst_ref=grp, send_sem=send_sems.at[rel],
                                                recv_sem=recv_sems.at[rel], device_id=me, device_id_type=MESH)

        def start(cps):
            for cp in cps:
                cp.start()

        mine = [pltpu.make_async_copy(_shard_piece(s_ref, k), _block_rows(arrs, k, me), local_sem) for k in pieces]
        start(mine)
        start(copies(SIB, me, None, sib, True))
        start(copies(X0, me, 0, xn, True))
        start(copies(Y1, me, 1, yn, True))
        start(copies(X1, me, 1, xn, True))
        start(copies(Y0, me, 0, yn, True))
        first_norm()
        place_rest()
        waiter(X0, 0).wait_recv()
        start(copies(RELAY_Y, xn, 0, yn))
        waiter(Y1, 1).wait_recv()
        start(copies(RELAY_X, yn, 1, xn))
        waiter(X1, 1).wait_recv()
        start(copies(ON_X, xn, None, sib))
        waiter(Y0, 0).wait_recv()
        start(copies(ON_Y, yn, None, sib))
        waiter(RELAY_Y, 0).wait_recv()
        start(copies(ON_D0, dg, 0, sib))
        waiter(RELAY_X, 1).wait_recv()
        start(copies(ON_D1, dg, 1, sib))
        waiter(SIB, None).wait_recv()
        waiter(ON_X, None).wait_recv()
        waiter(ON_Y, None).wait_recv()
        waiter(ON_D0, 0).wait_recv()
        waiter(ON_D1, 1).wait_recv()
        for rel, hf in ((SIB, None), (X0, 0), (X1, 1), (Y0, 0), (Y1, 1), (RELAY_Y, 0), (RELAY_X, 1),
                        (ON_X, None), (ON_Y, None), (ON_D0, 0), (ON_D1, 1)):
            waiter(rel, hf).wait_send()
        grp = _shard_group(s_ref, pieces)
        pltpu.make_async_copy(grp, grp, local_sem).wait()

    hbm = pl.BlockSpec(memory_space=pl.ANY)
    return pl.pallas_call(
        body, in_specs=[hbm, hbm, pl.BlockSpec(memory_space=pltpu.VMEM)], out_specs=[hbm] * 5,
        out_shape=(SDS((3, F, D), bf16), SDS((T, D), bf16),
                   SDS((DIN, D), bf16), SDS((DMIX, D), bf16), SDS((3, F, D), bf16)),
        scratch_shapes=[pltpu.VMEM((NORM_ROWS, D), f32), pltpu.VMEM((NORM_ROWS, D), bf16),
                        pltpu.VMEM((_group_rows(rest_pieces), D), bf16),
                        pltpu.SemaphoreType.DMA((11,)), pltpu.SemaphoreType.DMA((11,)), pltpu.SemaphoreType.DMA,
                        pltpu.SemaphoreType.DMA((2,))],
        compiler_params=pltpu.CompilerParams(has_side_effects=True),
        name="all_gather_ffn1")(shard, x, gain)


HBM_SPEC = pl.BlockSpec(memory_space=pltpu.HBM)
SEM_SPEC = pl.BlockSpec(memory_space=pltpu.SEMAPHORE)
ANY_SPEC = pl.BlockSpec(memory_space=pl.ANY)
SPLIT_EFFECT = pltpu.SideEffectType.DATAFLOW_SIDE_EFFECTING


def _in_hbm(a):
    return pltpu.with_memory_space_constraint(a, pltpu.HBM)


def _hbm_like(a):
    return pltpu.HBM(a.shape, a.dtype)


def _gather_rest_start(shard, wi, wo, w2, w1):
    def body(s_ref, wi_ref, wo_ref, w2_ref, w1_ref,
             ssem_m, rsem_m0, rsem_m, ssem_f, rsem_f0, rsem_f, s_o, wi_o, wo_o, w2_o, w1_o):
        x, y, c = _position()
        me, sib = (x, y, c), (x, y, 1 - c)
        chips = [(1 - x, y), (x, 1 - y), (1 - x, 1 - y)]
        arrs = _weight_pieces(wi_ref=wi_ref, wo_ref=wo_ref, w2_ref=w2_ref)
        for pieces, ssem, rsem0, rsem in ((MIX_PIECES, ssem_m, rsem_m0, rsem_m), (F2_PIECES, ssem_f, rsem_f0, rsem_f)):
            for p in pieces:
                pltpu.make_async_remote_copy(
                    src_ref=_shard_piece(s_ref, p), dst_ref=_block_rows(arrs, p, me), send_sem=ssem.at[0],
                    recv_sem=rsem0, device_id=sib, device_id_type=MESH).start()
            for j, chip in enumerate(chips):
                for p in pieces:
                    pltpu.make_async_remote_copy(
                        src_ref=_shard_piece(s_ref, p), dst_ref=_block_rows(arrs, p, me), send_sem=ssem.at[1 + j],
                        recv_sem=rsem.at[j], device_id=(*chip, c), device_id_type=MESH).start()

    dma = pltpu.SemaphoreType.DMA
    return pl.pallas_call(
        body, name="gather_rest_start",
        out_shape=(dma((4,)), dma(()), dma((3,)), dma((4,)), dma(()), dma((3,)),
                   _hbm_like(shard), _hbm_like(wi), _hbm_like(wo), _hbm_like(w2), _hbm_like(w1)),
        in_specs=(HBM_SPEC,) * 5, out_specs=(SEM_SPEC,) * 6 + (HBM_SPEC,) * 5,
        input_output_aliases={0: 6, 1: 7, 2: 8, 3: 9, 4: 10},
        compiler_params=pltpu.CompilerParams(has_side_effects=SPLIT_EFFECT),
    )(_in_hbm(shard), _in_hbm(wi), _in_hbm(wo), _in_hbm(w2), _in_hbm(w1))


def _gather_mix_pass_on(rsem_m, wi, wo, thru, after):
    def body(wi_ref, wo_ref, thru_ref, rsem, after_ref, fsend, frecv, wi_o, wo_o, thru_o):
        x, y, c = _position()
        sib = (x, y, 1 - c)
        arrs = _weight_pieces(wi_ref=wi_ref, wo_ref=wo_ref)
        both = wi_ref.at[pl.ds(0, _group_rows(MIX_PIECES)), :]
        for j, chip in enumerate([(1 - x, y), (x, 1 - y), (1 - x, 1 - y)]):
            pltpu.make_async_remote_copy(src_ref=both, dst_ref=both, send_sem=fsend.at[j], recv_sem=rsem.at[j],
                                         device_id=(x, y, c), device_id_type=MESH).wait_recv()
            for p in MIX_PIECES:
                rows = _block_rows(arrs, p, (*chip, c))
                pltpu.make_async_remote_copy(src_ref=rows, dst_ref=rows, send_sem=fsend.at[j], recv_sem=frecv.at[j],
                                             device_id=sib, device_id_type=MESH).start()

    dma = pltpu.SemaphoreType.DMA
    return pl.pallas_call(
        body, name="gather_mix_pass_on",
        out_shape=(dma((3,)), dma((3,)), _hbm_like(wi), _hbm_like(wo), _hbm_like(thru)),
        in_specs=(HBM_SPEC, HBM_SPEC, HBM_SPEC, SEM_SPEC, ANY_SPEC), out_specs=(SEM_SPEC, SEM_SPEC) + (HBM_SPEC,) * 3,
        input_output_aliases={0: 2, 1: 3, 2: 4},
        compiler_params=pltpu.CompilerParams(has_side_effects=SPLIT_EFFECT),
    )(wi, wo, _in_hbm(thru), rsem_m, after)


def _gather_mix_wait(ssem_m, rsem_m0, fsend, frecv, shard, wi, wo, after):
    def body(s_ref, wi_ref, wo_ref, ssem, rsem0, fs, fr, after_ref, s_o, wi_o, wo_o):
        x, y, c = _position()
        grp = _shard_group(s_ref, MIX_PIECES)

        def waiter(send_sem, recv_sem):
            return pltpu.make_async_remote_copy(src_ref=grp, dst_ref=grp, send_sem=send_sem, recv_sem=recv_sem,
                                                device_id=(x, y, c), device_id_type=MESH)

        waiter(ssem.at[0], rsem0).wait_recv()
        for j in range(3):
            waiter(fs.at[j], fr.at[j]).wait_recv()
        for rel in range(4):
            waiter(ssem.at[rel], rsem0).wait_send()
        for j in range(3):
            waiter(fs.at[j], fr.at[j]).wait_send()

    return pl.pallas_call(
        body, name="gather_mix_wait", out_shape=(_hbm_like(shard), _hbm_like(wi), _hbm_like(wo)),
        in_specs=(HBM_SPEC,) * 3 + (SEM_SPEC,) * 4 + (ANY_SPEC,), out_specs=(HBM_SPEC,) * 3,
        input_output_aliases={0: 0, 1: 1, 2: 2},
        compiler_params=pltpu.CompilerParams(has_side_effects=SPLIT_EFFECT),
    )(shard, wi, wo, ssem_m, rsem_m0, fsend, frecv, after)


def _gather_ffn2_pass_on(rsem_f, w2, wo, after):
    def body(w2_ref, wo_ref, rsem, after_ref, fsend, frecv, w2_o, wo_o):
        x, y, c = _position()
        sib = (x, y, 1 - c)
        chips = [(1 - x, y), (x, 1 - y), (1 - x, 1 - y)]
        arrs = _weight_pieces(w2_ref=w2_ref)
        three = w2_ref.at[0, pl.ds(0, _group_rows(F2_PIECES)), :]
        for j, chip in enumerate(chips):
            pltpu.make_async_remote_copy(src_ref=three, dst_ref=three, send_sem=fsend.at[j], recv_sem=rsem.at[j],
                                         device_id=(x, y, c), device_id_type=MESH).wait_recv()
            for p in F2_PIECES:
                rows = _block_rows(arrs, p, (*chip, c))
                pltpu.make_async_remote_copy(src_ref=rows, dst_ref=rows, send_sem=fsend.at[j], recv_sem=frecv.at[j],
                                             device_id=sib, device_id_type=MESH).start()

    dma = pltpu.SemaphoreType.DMA
    return pl.pallas_call(
        body, name="gather_ffn2_pass_on", out_shape=(dma((3,)), dma((3,)), _hbm_like(w2), _hbm_like(wo)),
        in_specs=(HBM_SPEC, HBM_SPEC, SEM_SPEC, ANY_SPEC), out_specs=(SEM_SPEC, SEM_SPEC, HBM_SPEC, HBM_SPEC),
        input_output_aliases={0: 2, 1: 3},
        compiler_params=pltpu.CompilerParams(has_side_effects=SPLIT_EFFECT),
    )(w2, wo, rsem_f, after)


def _gather_ffn2_wait(ssem_f, rsem_f0, fsend, frecv, shard, w2, after):
    def body(s_ref, w2_ref, ssem, rsem0, fs, fr, after_ref, w2_o):
        x, y, c = _position()
        grp = _shard_group(s_ref, F2_PIECES)

        def waiter(send_sem, recv_sem):
            return pltpu.make_async_remote_copy(src_ref=grp, dst_ref=grp, send_sem=send_sem, recv_sem=recv_sem,
                                                device_id=(x, y, c), device_id_type=MESH)

        waiter(ssem.at[0], rsem0).wait_recv()
        for j in range(3):
            waiter(fs.at[j], fr.at[j]).wait_recv()
        for rel in range(4):
            waiter(ssem.at[rel], rsem0).wait_send()
        for j in range(3):
            waiter(fs.at[j], fr.at[j]).wait_send()

    return pl.pallas_call(
        body, name="gather_ffn2_wait", out_shape=_hbm_like(w2),
        in_specs=(HBM_SPEC, HBM_SPEC, SEM_SPEC, SEM_SPEC, SEM_SPEC, SEM_SPEC, ANY_SPEC), out_specs=HBM_SPEC,
        input_output_aliases={1: 0},
        compiler_params=pltpu.CompilerParams(has_side_effects=SPLIT_EFFECT),
    )(shard, w2, ssem_f, rsem_f0, fsend, frecv, after)


class _GatheredWeights(_LocalWeights):
    def __init__(self, shard, x, gain1):
        w1, self.h1, wi, wo, w2 = _all_gather_ffn1(shard, x, gain1)
        (self.ssem_m, self.rsem_m0, self.rsem_m, self.ssem_f, self.rsem_f0, self.rsem_f,
         self.shard, self.wi, self.wo, self.w2_part, self.w1) = _gather_rest_start(shard, wi, wo, w2, w1)

    def first_norm(self, x, gain):
        return self.h1

    def after_ffn1(self, gain, x1):
        self.fsend_m, self.frecv_m, self.wi, self.wo, gain = _gather_mix_pass_on(self.rsem_m, self.wi, self.wo, gain, x1)
        return gain

    def mix(self, after):
        self.shard, wint, wout = _gather_mix_wait(self.ssem_m, self.rsem_m0, self.fsend_m, self.frecv_m, self.shard,
                                                  self.wi, self.wo, after)
        return wint, wout

    def before_out_proj(self, wout, after):
        self.fsend, self.frecv, self.w2_part, wout = _gather_ffn2_pass_on(self.rsem_f, self.w2_part, wout, after)
        return wout

    def ffn2(self, after):
        return _gather_ffn2_wait(self.ssem_f, self.rsem_f0, self.fsend, self.frecv, self.shard, self.w2_part, after)

    def out_ffn2_grads_ready(self, dwout, dw2, after):
        rx1 = lax.empty((4, RSA_ROWS, D), bf16)
        sa, ra, sent, rx1, after = _rsa_level1_start(dict(wo=dwout, w2=dw2), rx1, after, "rsa_level1_start_out_ffn2")
        self.level1 = ((sa, ra), sent, rx1)
        return after

    def before_ffn1_bwd(self, dwint, dx1b):
        early, sent, rx1 = self.level1
        sa, ra, late, rx1, dx1b = _rsa_level1_start(dict(wi=dwint), rx1, dx1b, "rsa_level1_start_in")
        started = (((MIX_PIECES[1],) + F2_PIECES, *early), ((MIX_PIECES[0],), sa, ra))
        rx2 = lax.empty((3, RSA_ROWS, D), bf16)
        self.sb, self.rb, self.tx, self.acc, self.rx2, dx1b = _rsa_sums_and_send(
            started, late["wi"], sent["wo"], sent["w2"], rx1, rx2, dx1b)
        return dx1b

    def mix_ffn2_grads_parts(self, after):
        rx2 = _rsa_level2_wait(self.sb, self.rb, self.tx, self.rx2, after)
        return self.acc, rx2


def _reduce_scatter_ffn1_head(dw1, small_packed, first_norm):
    pieces = G1_PIECES
    half = FS // 2
    hrows = len(pieces) * half
    nrows = 2 * hrows
    X_RELAY, Y_RELAY = range(2)

    T = first_norm[0].shape[0]

    def body(d1_ref, p_ref, dh_hbm, x_hbm, gain_ref, dr_hbm,
             forx_ref, fory_ref, own_ref, rx1_ref, relx_ref, rely_ref, gx_hbm, tot_ref,
             own_buf, rx_buf, tx1, tx2, tx3, acc, sa, ra, sb, rb, lsem, pair, chips, small_tot, small_send, small_recv,
             xbuf, rbuf, hbuf, obuf, norm_in_sems, norm_out_sems):
        x, y, c = _position()
        me, sib = (x, y, c), (x, y, 1 - c)
        xn, yn = (1 - x, y, c), (x, 1 - y, c)
        rel_chips = [(x, y), (1 - x, y), (x, 1 - y), (1 - x, 1 - y)]
        srcs = _weight_pieces(w1_ref=d1_ref)

        my_chip = 2 * x + y
        pair[c] = p_ref[...]
        swap = pltpu.make_async_remote_copy(
            src_ref=pair.at[c], dst_ref=pair.at[c], send_sem=small_send.at[0], recv_sem=small_recv.at[0],
            device_id=sib, device_id_type=MESH)
        mine = pl.ds(pl.multiple_of(c * (SMALL_ROWS // 2), 8), SMALL_ROWS // 2)
        small = [pltpu.make_async_remote_copy(
            src_ref=chips.at[my_chip, mine, :], dst_ref=chips.at[my_chip, mine, :], send_sem=small_send.at[j],
            recv_sem=small_recv.at[j], device_id=(*rel_chips[j], c), device_id_type=MESH) for j in (1, 2, 3)]
        give = pltpu.make_async_remote_copy(
            src_ref=small_tot.at[mine, :], dst_ref=small_tot.at[mine, :], send_sem=small_send.at[4],
            recv_sem=small_recv.at[4], device_id=sib, device_id_type=MESH)

        def part(k, dev, hf):
            r = PIECE_ROWS[k]
            return srcs[k].at[pl.ds(pl.multiple_of(_dev_index(*dev) * r + hf * half, 16), half), :]

        def slot(ref, k, hf):
            return ref.at[pl.ds(hf * hrows + k * half, half), :]

        halves = [(k, hf) for hf in (0, 1) for k in pieces]

        for j in (3, 1, 2, 0):
            for k, hf in halves:
                pltpu.make_async_remote_copy(
                    src_ref=part(k, (*rel_chips[j], 1 - c), hf), dst_ref=slot(rx1_ref.at[j], k, hf),
                    send_sem=sa.at[j], recv_sem=ra.at[j], device_id=sib, device_id_type=MESH).start()

        def wait_a(j):
            return pltpu.make_async_remote_copy(src_ref=rx1_ref.at[j], dst_ref=rx1_ref.at[j], send_sem=sa.at[j],
                                                recv_sem=ra.at[j], device_id=me, device_id_type=MESH)

        def ici(rel, src, dst, to):
            return pltpu.make_async_remote_copy(src_ref=src, dst_ref=dst, send_sem=sb.at[rel], recv_sem=rb.at[rel],
                                                device_id=to, device_id_type=MESH)

        first, second = pl.ds(0, hrows), pl.ds(hrows, hrows)
        sends = {
            X_RELAY: ici(X_RELAY, tx3.at[first, :], relx_ref, xn),
            Y_RELAY: ici(Y_RELAY, tx3.at[second, :], rely_ref, yn),
        }

        def chip_sum(j, dst):
            loads = [pltpu.make_async_copy(part(k, (*rel_chips[j], c), hf), slot(own_buf, k, hf), lsem.at[0])
                     for k, hf in halves]
            for cp in loads:
                cp.start()
            wait_a(j).wait_recv()
            got = pltpu.make_async_copy(rx1_ref.at[j], rx_buf, lsem.at[1])
            got.start()
            pltpu.make_async_copy(rx_buf, rx_buf, lsem.at[0]).wait()
            got.wait()

            def add(i, carry):
                rows = pl.ds(pl.multiple_of(i * half, 16), half)
                tot = own_buf[rows, :].astype(f32) + rx_buf[rows, :].astype(f32)
                dst[rows, :] = tot.astype(dst.dtype)
                return carry

            lax.fori_loop(0, nrows // half, add, 0)

        def add_landed(landed, dst, rows0, nrows_):
            got = pltpu.make_async_copy(landed, rx_buf.at[pl.ds(0, nrows_), :], lsem.at[1])
            got.start()
            got.wait()

            def add(i, carry):
                src_rows = pl.ds(pl.multiple_of(i * half, 16), half)
                dst_rows = pl.ds(pl.multiple_of(rows0 + i * half, 16), half)
                dst[dst_rows, :] = (dst[dst_rows, :].astype(f32) + rx_buf[src_rows, :].astype(f32)).astype(dst.dtype)
                return carry

            lax.fori_loop(0, nrows_ // half, add, 0)

        chip_sum(3, tx3)
        sends[X_RELAY].start()
        sends[Y_RELAY].start()
        dg = _norm_bwd_rows(T, dh_hbm, x_hbm, gain_ref, dr_hbm, gx_hbm.at[0], None,
                            xbuf, rbuf, hbuf, obuf, None, norm_in_sems, norm_out_sems)
        r0 = SMALL_OFF["ffn1_norm"]
        for k in range(D // 128):
            pair[c, r0 + k:r0 + k + 1, :] = dg[:, 128 * k:128 * (k + 1)]
        swap.start()
        swap.wait_recv()
        chips[my_chip] = pair[0] + pair[1]
        for cp in small:
            cp.start()
        chip_sum(1, tx1)
        chip_sum(2, tx2)
        chip_sum(0, acc)
        own_out = pltpu.make_async_copy(acc, own_ref, lsem.at[0])
        own_out.start()
        sends[X_RELAY].wait_recv()
        add_landed(relx_ref, tx2, 0, hrows)
        sends[Y_RELAY].wait_recv()
        add_landed(rely_ref, tx1, hrows, hrows)
        own_out.wait()
        outs = [pltpu.make_async_copy(tx1, forx_ref, lsem.at[0]), pltpu.make_async_copy(tx2, fory_ref, lsem.at[1])]
        for cp in outs:
            cp.start()
        for cp in outs:
            cp.wait()
        for cp in small:
            cp.wait_recv()
        small_tot[mine, :] = (chips[0, mine, :] + chips[1, mine, :]) + (chips[2, mine, :] + chips[3, mine, :])
        give.start()
        give.wait_recv()
        tot = small_tot[...]
        tot_ref[...] = tot
        loss = jnp.sum(tot[LOSS_ROW:LOSS_ROW + 1, :], axis=-1, keepdims=True)
        tot_ref[LOSS_ROW:LOSS_ROW + 1, :] = jnp.broadcast_to(loss, (1, 128))
        for j in range(4):
            wait_a(j).wait_send()
        for cp in sends.values():
            cp.wait_send()
        swap.wait_send()
        for cp in small + [give]:
            cp.wait_send()

    hbm = pl.BlockSpec(memory_space=pl.ANY)
    vm = pl.BlockSpec(memory_space=pltpu.VMEM)
    outs = pl.pallas_call(
        body, in_specs=[hbm, vm, hbm, hbm, vm, hbm], out_specs=[hbm] * 7 + [vm],
        out_shape=(SDS((nrows, D), bf16), SDS((nrows, D), bf16), SDS((nrows, D), f32), SDS((4, nrows, D), bf16),
                   SDS((hrows, D), bf16), SDS((hrows, D), bf16), SDS((1, T, D), f32), SDS((SMALL_ROWS, 128), f32)),
        scratch_shapes=[pltpu.VMEM((nrows, D), bf16), pltpu.VMEM((nrows, D), bf16),
                        pltpu.VMEM((nrows, D), bf16), pltpu.VMEM((nrows, D), bf16), pltpu.VMEM((nrows, D), bf16),
                        pltpu.VMEM((nrows, D), f32),
                        pltpu.SemaphoreType.DMA((4,)), pltpu.SemaphoreType.DMA((4,)),
                        pltpu.SemaphoreType.DMA((2,)), pltpu.SemaphoreType.DMA((2,)), pltpu.SemaphoreType.DMA((2,)),
                        pltpu.VMEM((2, SMALL_ROWS, 128), f32), pltpu.VMEM((4, SMALL_ROWS, 128), f32),
                        pltpu.VMEM((SMALL_ROWS, 128), f32),
                        pltpu.SemaphoreType.DMA((5,)), pltpu.SemaphoreType.DMA((5,))]
        + [sc for sc in _norm_bwd_scratch(True, False) if sc is not None],
        compiler_params=pltpu.CompilerParams(has_side_effects=True, vmem_limit_bytes=VMEM_LIMIT_V7X),
        name="reduce_scatter_ffn1_head")(dw1, small_packed, *first_norm)
    return outs[0], outs[1], outs[2], outs[-1], outs[-2]


def _rs1_tail_start(for_x, for_y, from_x, from_y, *thru):
    def body(fx_ref, fy_ref, lx_ref, ly_ref, *rest):
        ssem, rsem = rest[len(thru):len(thru) + 2]
        x, y, c = _position()
        pltpu.make_async_remote_copy(src_ref=fx_ref, dst_ref=lx_ref, send_sem=ssem.at[0], recv_sem=rsem.at[0],
                                     device_id=(1 - x, y, c), device_id_type=MESH).start()
        pltpu.make_async_remote_copy(src_ref=fy_ref, dst_ref=ly_ref, send_sem=ssem.at[1], recv_sem=rsem.at[1],
                                     device_id=(x, 1 - y, c), device_id_type=MESH).start()

    dma = pltpu.SemaphoreType.DMA
    arrs = (for_x, for_y, from_x, from_y, *thru)
    return pl.pallas_call(
        body, name="rs1_tail_start", out_shape=(dma((2,)), dma((2,))) + tuple(_hbm_like(a) for a in arrs),
        in_specs=(HBM_SPEC,) * len(arrs), out_specs=(SEM_SPEC,) * 2 + (HBM_SPEC,) * len(arrs),
        input_output_aliases={i: i + 2 for i in range(len(arrs))},
        compiler_params=pltpu.CompilerParams(has_side_effects=SPLIT_EFFECT),
    )(*[_in_hbm(a) for a in arrs])


RSA_PIECES = MIX_PIECES + F2_PIECES
RSA_ROWS = _group_rows(RSA_PIECES)
RSA_OFF = {k: PIECE_OFF[k] - PIECE_OFF[RSA_PIECES[0]] for k in RSA_PIECES}
RSA_BLOCK = 192


def _rsa_rows(ref, k):
    return ref.at[pl.ds(RSA_OFF[k], PIECE_ROWS[k]), :]


def _rsa_level1_start(grads, rx1, thru, name):
    keys = sorted(grads)
    n = len(keys)

    def body(*refs):
        srcs = _weight_pieces(**{k + "_ref": ref for k, ref in zip(keys, refs[:n])})
        rx1_ref, sa, ra = refs[n], refs[n + 2], refs[n + 3]
        x, y, c = _position()
        for j, chip in enumerate([(x, y), (1 - x, y), (x, 1 - y), (1 - x, 1 - y)]):
            for k in sorted(srcs):
                pltpu.make_async_remote_copy(
                    src_ref=_block_rows(srcs, k, (*chip, 1 - c)), dst_ref=_rsa_rows(rx1_ref.at[j], k),
                    send_sem=sa.at[j], recv_sem=ra.at[j], device_id=(x, y, 1 - c), device_id_type=MESH).start()

    dma = pltpu.SemaphoreType.DMA
    arrs = tuple(grads[k] for k in keys) + (rx1, thru)
    outs = pl.pallas_call(
        body, name=name, out_shape=(dma((4,)), dma((4,))) + tuple(_hbm_like(a) for a in arrs),
        in_specs=(HBM_SPEC,) * len(arrs), out_specs=(SEM_SPEC,) * 2 + (HBM_SPEC,) * len(arrs),
        input_output_aliases={i: i + 2 for i in range(len(arrs))},
        compiler_params=pltpu.CompilerParams(has_side_effects=SPLIT_EFFECT),
    )(*[_in_hbm(a) for a in arrs])
    return outs[0], outs[1], dict(zip(keys, outs[2:2 + n])), outs[2 + n], outs[3 + n]


def _rsa_sums_and_send(started, dwint, dwout, dw2, rx1, rx2, thru):
    nblk = RSA_ROWS // RSA_BLOCK
    nstart = len(started)

    def body(*refs):
        di_ref, do_ref, d2_ref, rx1_ref, rx2_ref = refs[:5]
        l1_sems = refs[6:6 + 2 * nstart]
        sb, rb, tx_ref, acc_ref = refs[6 + 2 * nstart:10 + 2 * nstart]
        own_buf, rx_buf, tx_buf, acc_buf, in_sems, out_sems = refs[13 + 2 * nstart:]
        x, y, c = _position()
        srcs = _weight_pieces(wi_ref=di_ref, wo_ref=do_ref, w2_ref=d2_ref)
        chips = [(x, y), (1 - x, y), (x, 1 - y), (1 - x, 1 - y)]

        for g, (pieces, _, _) in enumerate(started):
            ssem, rsem = l1_sems[2 * g], l1_sems[2 * g + 1]
            for j in range(4):
                rows = rx1_ref.at[j, pl.ds(RSA_OFF[pieces[0]], _group_rows(pieces)), :]
                d = pltpu.make_async_remote_copy(src_ref=rows, dst_ref=rows, send_sem=ssem.at[j], recv_sem=rsem.at[j],
                                                 device_id=(x, y, c), device_id_type=MESH)
                d.wait_recv()
                d.wait_send()

        def start_loads(j):
            s = j % 2
            for k in RSA_PIECES:
                pltpu.make_async_copy(_block_rows(srcs, k, (*chips[j], c)), _rsa_rows(own_buf.at[s], k),
                                      in_sems.at[2 * s]).start()
            pltpu.make_async_copy(rx1_ref.at[j], rx_buf.at[s], in_sems.at[2 * s + 1]).start()

        def wait_loads(j):
            s = j % 2
            pltpu.make_async_copy(rx1_ref.at[j], own_buf.at[s], in_sems.at[2 * s]).wait()
            pltpu.make_async_copy(rx1_ref.at[j], rx_buf.at[s], in_sems.at[2 * s + 1]).wait()

        def store(j):
            if j == 0:
                return pltpu.make_async_copy(acc_buf, acc_ref, out_sems.at[2])
            return pltpu.make_async_copy(tx_buf.at[j % 2], tx_ref.at[j - 1], out_sems.at[j % 2])

        def send(j):
            return pltpu.make_async_remote_copy(src_ref=tx_ref.at[j - 1], dst_ref=rx2_ref.at[j - 1], send_sem=sb.at[j - 1],
                                                recv_sem=rb.at[j - 1], device_id=(*chips[j], c), device_id_type=MESH)

        start_loads(0)
        for j in range(4):
            s = j % 2
            if j + 1 < 4:
                start_loads(j + 1)
            wait_loads(j)
            if j == 3:
                store(1).wait()
                send(1).start()

            def add(i, carry, j=j, s=s):
                rows = pl.ds(pl.multiple_of(i * RSA_BLOCK, 16), RSA_BLOCK)
                tot = own_buf[s, rows, :].astype(f32) + rx_buf[s, rows, :].astype(f32)
                if j == 0:
                    acc_buf[rows, :] = tot
                else:
                    tx_buf[s, rows, :] = tot.astype(bf16)
                return carry

            lax.fori_loop(0, nblk, add, 0)
            store(j).start()
        store(0).wait()
        for j in (2, 3):
            store(j).wait()
            send(j).start()

    dma = pltpu.SemaphoreType.DMA
    passed = (rx1, rx2, thru)
    outs = pl.pallas_call(
        body, name="rsa_sums_and_send",
        in_specs=(HBM_SPEC,) * 6 + (SEM_SPEC,) * (2 * nstart),
        out_specs=(SEM_SPEC,) * 2 + (HBM_SPEC,) * 5,
        out_shape=(dma((3,)), dma((3,)), pltpu.HBM((3, RSA_ROWS, D), bf16), pltpu.HBM((RSA_ROWS, D), f32))
        + tuple(_hbm_like(a) for a in passed),
        input_output_aliases={3: 4, 4: 5, 5: 6},
        scratch_shapes=[pltpu.VMEM((2, RSA_ROWS, D), bf16), pltpu.VMEM((2, RSA_ROWS, D), bf16),
                        pltpu.VMEM((2, RSA_ROWS, D), bf16), pltpu.VMEM((RSA_ROWS, D), f32),
                        dma((4,)), dma((3,))],
        compiler_params=pltpu.CompilerParams(has_side_effects=SPLIT_EFFECT, vmem_limit_bytes=VMEM_LIMIT_V7X),
    )(*[_in_hbm(a) for a in (dwint, dwout, dw2) + passed], *[sem for _, sa, ra in started for sem in (sa, ra)])
    sb, rb, tx, acc, _, rx2, thru = outs
    return sb, rb, tx, acc, rx2, thru


def _rsa_level2_wait(sb, rb, tx, rx2, after):
    def body(tx_ref, rx2_ref, sb_ref, rb_ref, after_ref, rx2_o):
        x, y, c = _position()
        for j in range(3):
            d = pltpu.make_async_remote_copy(src_ref=tx_ref.at[j], dst_ref=rx2_ref.at[j], send_sem=sb_ref.at[j],
                                             recv_sem=rb_ref.at[j], device_id=(x, y, c), device_id_type=MESH)
            d.wait_recv()
            d.wait_send()

    return pl.pallas_call(
        body, name="rsa_level2_wait", out_shape=_hbm_like(rx2),
        in_specs=(HBM_SPEC, HBM_SPEC, SEM_SPEC, SEM_SPEC, ANY_SPEC), out_specs=HBM_SPEC,
        input_output_aliases={1: 0},
        compiler_params=pltpu.CompilerParams(has_side_effects=SPLIT_EFFECT),
    )(tx, rx2, sb, rb, after)


def _adamw_math(w, g, m, v):
    m = ADAM_B1 * m + (1.0 - ADAM_B1) * g
    v = ADAM_B2 * v + (1.0 - ADAM_B2) * (g * g)
    m_hat = m / (1.0 - ADAM_B1 ** ADAM_STEP)
    v_hat = v / (1.0 - ADAM_B2 ** ADAM_STEP)
    delta = -ADAM_LR * (m_hat / (jnp.sqrt(v_hat) + ADAM_EPS) + ADAM_WD * w)
    return delta, m, v


def _adamw_big(pieces, ws, ms, vs, own, landed, name, in_flight=None):
    npiece = len(pieces)
    nsent = len(landed) if in_flight else 0
    nland = sum(a.shape[0] if a.ndim == 3 else 1 for a in landed)
    rmax = max(PIECE_ROWS[k] for k in pieces)
    half = FS // 2

    def segments(k):
        if k in G1_PIECES:
            return [(hf * len(G1_PIECES) * half + k * half, hf * half, half) for hf in (0, 1)]
        return [(RSA_OFF[k], 0, PIECE_ROWS[k])]

    def body(*refs):
        ins = (refs[0:npiece], refs[npiece:2 * npiece], refs[2 * npiece:3 * npiece])
        own_ref = refs[3 * npiece]
        nin = 3 * npiece + 1 + len(landed)
        land_refs = []
        for ref, a in zip(refs[3 * npiece + 1:nin], landed):
            land_refs += [ref.at[j] for j in range(a.shape[0])] if a.ndim == 3 else [ref]
        if in_flight:
            sent_refs, (ssem, rsem) = refs[nin:nin + nsent], refs[nin + nsent:nin + nsent + 2]
            nin += nsent + 3
        out_refs = refs[nin:nin + 4 * npiece]
        inb, landb, outb, in_sems, land_sems, out_sems = refs[nin + 4 * npiece + nsent:]

        def loads(i):
            s, k = i % 2, pieces[i]
            r = PIECE_ROWS[k]
            cps = [pltpu.make_async_copy(ins[q][i].at[0], inb.at[s, q, pl.ds(0, r), :], in_sems.at[4 * s + q])
                   for q in range(3)]
            waits, late = list(cps), []
            for src0, dst0, n in segments(k):
                cps.append(pltpu.make_async_copy(own_ref.at[pl.ds(src0, n), :], inb.at[s, 3, pl.ds(dst0, n), :],
                                                 in_sems.at[4 * s + 3]))
                for p in range(nland):
                    late.append(pltpu.make_async_copy(land_refs[p].at[pl.ds(src0, n), :],
                                                      landb.at[s, p, pl.ds(dst0, n), :], land_sems.at[nland * s + p]))
            own_rows = inb.at[s, 3, pl.ds(0, r), :]
            waits.append(pltpu.make_async_copy(own_rows, own_rows, in_sems.at[4 * s + 3]))
            for p in range(nland):
                rows = landb.at[s, p, pl.ds(0, r), :]
                waits.append(pltpu.make_async_copy(rows, rows, land_sems.at[nland * s + p]))
            return cps, late, waits

        def stores(i):
            s, r = i % 2, PIECE_ROWS[pieces[i]]
            return [pltpu.make_async_copy(outb.at[s, q, pl.ds(0, r), :], out_refs[q * npiece + i].at[0],
                                          out_sems.at[4 * s + q]) for q in range(4)]

        ahead = min(2, npiece) if in_flight else 1
        for i in range(ahead):
            for cp in loads(i)[0]:
                cp.start()
        if in_flight:
            x, y, c = _position()
            for j in range(nsent):
                d = pltpu.make_async_remote_copy(src_ref=sent_refs[j], dst_ref=refs[3 * npiece + 1 + j],
                                                 send_sem=ssem.at[j], recv_sem=rsem.at[j], device_id=(x, y, c),
                                                 device_id_type=MESH)
                d.wait_recv()
                d.wait_send()
        for cp in loads(0)[1]:
            cp.start()
        for i in range(npiece):
            s, r = i % 2, PIECE_ROWS[pieces[i]]
            if i + 1 < npiece:
                first, late, _ = loads(i + 1)
                for cp in late if i + 1 < ahead else first + late:
                    cp.start()
            for cp in loads(i)[2]:
                cp.wait()
            if i >= 2:
                for cp in stores(i - 2):
                    cp.wait()
            g = inb[s, 3, 0:r, :]
            for p in range(nland):
                g = g + landb[s, p, 0:r, :].astype(f32)
            d, nm, nv = _adamw_math(inb[s, 0, 0:r, :], g, inb[s, 1, 0:r, :], inb[s, 2, 0:r, :])
            outb[s, 0, 0:r, :] = g
            outb[s, 1, 0:r, :] = d
            outb[s, 2, 0:r, :] = nm
            outb[s, 3, 0:r, :] = nv
            for cp in stores(i):
                cp.start()
        for i in range(max(npiece - 2, 0), npiece):
            for cp in stores(i):
                cp.wait()

    hbm = pl.BlockSpec(memory_space=pl.ANY)
    in_specs, out_specs = [hbm] * (3 * npiece + 1), [hbm] * (4 * npiece)
    out_shape = [SDS(w.shape, f32) for _ in range(4) for w in ws]
    args, aliases, effect = [*ws, *ms, *vs, own], {}, False
    if in_flight:
        ssem, rsem, sent, after = in_flight
        in_specs += [HBM_SPEC] * (2 * nsent) + [SEM_SPEC, SEM_SPEC, hbm]
        args += [_in_hbm(a) for a in (*landed, *sent)] + [ssem, rsem, after]
        out_specs += [HBM_SPEC] * nsent
        out_shape += [_hbm_like(a) for a in landed]
        aliases = {3 * npiece + 1 + j: 4 * npiece + j for j in range(nsent)}
        effect = SPLIT_EFFECT
    else:
        in_specs += [hbm] * len(landed)
        args += list(landed)
    outs = pl.pallas_call(
        body, in_specs=in_specs, out_specs=out_specs, out_shape=tuple(out_shape), input_output_aliases=aliases,
        scratch_shapes=[pltpu.VMEM((2, 4, rmax, D), f32), pltpu.VMEM((2, nland, rmax, D), bf16),
                        pltpu.VMEM((2, 4, rmax, D), f32),
                        pltpu.SemaphoreType.DMA((8,)), pltpu.SemaphoreType.DMA((2 * nland,)),
                        pltpu.SemaphoreType.DMA((8,))],
        compiler_params=pltpu.CompilerParams(has_side_effects=effect, vmem_limit_bytes=VMEM_LIMIT_V7X),
        name=name)(*args)
    return [list(outs[q * npiece:(q + 1) * npiece]) for q in range(4)]


def _adamw_small(ws, ms, vs, gs, name):
    n = len(ws)

    def body(*refs):
        w_refs, m_refs, v_refs, g_refs = refs[0:n], refs[n:2 * n], refs[2 * n:3 * n], refs[3 * n:4 * n]
        outs = refs[4 * n:]
        for i in range(n):
            d, nm, nv = _adamw_math(w_refs[i][...], g_refs[i][...], m_refs[i][...], v_refs[i][...])
            outs[i][...] = d
            outs[n + i][...] = nm
            outs[2 * n + i][...] = nv

    outs = pl.pallas_call(
        body, out_shape=tuple(SDS(w.shape, f32) for _ in range(3) for w in ws), name=name)(*ws, *ms, *vs, *gs)
    return [list(outs[q * n:(q + 1) * n]) for q in range(3)]


WEIGHTS = ("ffn1_norm", "ffn1_w_gate", "ffn1_w_up", "ffn1_w_down", "mix_norm", "w_in", "q_norm", "k_norm",
           "attn_sinks", "rel_bias", "pool_w", "pool_scale", "w_out", "ffn2_norm", "ffn2_w_gate", "ffn2_w_up",
           "ffn2_w_down")
BIG = (("ffn1_w_gate", True), ("ffn1_w_up", True), ("ffn1_w_down", False), ("w_in", True), ("w_out", False),
       ("ffn2_w_gate", True), ("ffn2_w_up", True), ("ffn2_w_down", False))


def kernel(x, ffn1_norm, ffn1_w_gate, ffn1_w_up, ffn1_w_down, mix_norm, w_in, q_norm, k_norm, attn_sinks, rel_bias, pool_w, pool_scale, w_out, ffn2_norm, ffn2_w_gate, ffn2_w_up, ffn2_w_down, loss_target, m_ffn1_norm, m_ffn1_w_gate, m_ffn1_w_up, m_ffn1_w_down, m_mix_norm, m_w_in, m_q_norm, m_k_norm, m_attn_sinks, m_rel_bias, m_pool_w, m_pool_scale, m_w_out, m_ffn2_norm, m_ffn2_w_gate, m_ffn2_w_up, m_ffn2_w_down, v_ffn1_norm, v_ffn1_w_gate, v_ffn1_w_up, v_ffn1_w_down, v_mix_norm, v_w_in, v_q_norm, v_k_norm, v_attn_sinks, v_rel_bias, v_pool_w, v_pool_scale, v_w_out, v_ffn2_norm, v_ffn2_w_gate, v_ffn2_w_up, v_ffn2_w_down):
    args = dict(locals())
    w = {n: args[n] for n in WEIGHTS}
    m = {n: args["m_" + n] for n in WEIGHTS}
    v = {n: args["v_" + n] for n in WEIGHTS}

    as_rows = lambda a, tr: jnp.swapaxes(a, 1, 2) if tr else a
    shard = jnp.concatenate([as_rows(w[n], tr)[0].astype(bf16) for n, tr in BIG], axis=0)
    exchanges = _GatheredWeights(shard, x[0], ffn1_norm)
    (dh1, dx1), (dw1, _, _, _), small = _local_step(
        x[0], loss_target[0], exchanges, ffn1_norm, mix_norm, ffn2_norm, q_norm, k_norm, attn_sinks,
        rel_bias, pool_w[0], pool_scale)

    nrows1 = len(G1_PIECES) * FS
    for_x, for_y, own1, small_tot, gx = _reduce_scatter_ffn1_head(dw1, _pack_small(small),
                                                                  (dh1, x[0], ffn1_norm, dx1))
    ssem, rsem, for_x, for_y, from_x, from_y, small_tot, gx = _rs1_tail_start(
        for_x, for_y, lax.empty((nrows1, D), bf16), lax.empty((nrows1, D), bf16), small_tot, gx)
    gx = _fresh_copy(gx, "grad_x_out")
    own_rest, landed_rest = exchanges.mix_ffn2_grads_parts(small_tot)

    grads, deltas, new_m, new_v = {}, {}, {}, {}
    rest = [k for k in range(len(BIG)) if k not in G1_PIECES]
    rows_of = lambda t, ks: [as_rows(t[BIG[k][0]], BIG[k][1]) for k in ks]
    rest_out = _adamw_big(rest, rows_of(w, rest), rows_of(m, rest), rows_of(v, rest), own_rest, [landed_rest],
                          "adamw_rest")
    ffn1 = list(G1_PIECES)
    ffn1_out = _adamw_big(ffn1, rows_of(w, ffn1), rows_of(m, ffn1), rows_of(v, ffn1), own1, [from_x, from_y],
                          "adamw_ffn1", in_flight=(ssem, rsem, [for_x, for_y], rest_out[0][0]))
    for ks, out in ((rest, rest_out), (ffn1, ffn1_out)):
        for i, k in enumerate(ks):
            n, tr = BIG[k]
            grads[n], deltas[n], new_m[n], new_v[n] = [as_rows(o[i], tr) for o in out]
    small_names = [n for n in SMALL_NAMES if n != "loss"]
    for n in small_names:
        grads[n] = _unpack_small(small_tot, n)
    ds, nms, nvs = _adamw_small([w[n] for n in small_names], [m[n] for n in small_names], [v[n] for n in small_names],
                                [grads[n] for n in small_names], "adamw_small")
    for i, n in enumerate(small_names):
        deltas[n], new_m[n], new_v[n] = ds[i], nms[i], nvs[i]
    loss = small_tot[LOSS_ROW, 0]
    return (loss, gx, *[grads[n] for n in WEIGHTS], *[deltas[n] for n in WEIGHTS],
            *[new_m[n] for n in WEIGHTS], *[new_v[n] for n in WEIGHTS])
```

```python
import jax
import jax.numpy as jnp
import numpy as np
from jax import lax
from jax.experimental import pallas as pl
from jax.experimental.pallas import tpu as pltpu

f32, bf16, i32 = jnp.float32, jnp.bfloat16, jnp.int32
SDS = jax.ShapeDtypeStruct

D = 1024
F = 2816
HD = 64
NH = 8
NKV = 2
GQA = NH // NKV
DATTN = NH * HD
DKV = NKV * HD
DPOOL = 512
POOL_WINDOWS = (2, 4, 8, 16)
PGD = DPOOL // len(POOL_WINDOWS)
DIN = DATTN + 2 * DKV + DPOOL
DMIX = DATTN + DPOOL
BLK = 128
NBUCK = 32
MAX_DISTANCE = 128
EPS = 1e-6
NEG = -1e30
SCALE = HD ** -0.5

ADAM_LR, ADAM_B1, ADAM_B2, ADAM_EPS, ADAM_WD, ADAM_STEP = 0.001, 0.9, 0.999, 1e-08, 0.01, 10

NDEV = 8
FS = F // NDEV
INS = DIN // NDEV
OUTS = DMIX // NDEV
PIECE_ROWS = (FS, FS, FS, INS, OUTS, FS, FS, FS)
PIECE_OFF = tuple(int(v) for v in np.cumsum((0,) + PIECE_ROWS[:-1]))
PACK_ROWS = sum(PIECE_ROWS)

VMEM_LIMIT_V7X = 56 * 1024 * 1024

MESH = pl.DeviceIdType.MESH


def _cparams(sem=None, vmem=None):
    return pltpu.CompilerParams(dimension_semantics=sem, vmem_limit_bytes=vmem)


def _nt(a, b):
    return lax.dot_general(a, b, (((1,), (1,)), ((), ())), preferred_element_type=f32)


def _tn(a, b):
    return lax.dot_general(a, b, (((0,), (0,)), ((), ())), preferred_element_type=f32)


def _nn(a, b):
    return jnp.dot(a, b, preferred_element_type=f32)


def _sigmoid(x):
    return 1.0 / (1.0 + jnp.exp(-x))


def _fresh_copy(a, name):
    T = a.shape[1]
    tm = min(512, T)

    def body(a_ref, o_ref):
        o_ref[...] = a_ref[...]

    tok = pl.BlockSpec((1, tm, D), lambda i: (0, i, 0))
    return pl.pallas_call(body, grid=(T // tm,), in_specs=[tok], out_specs=tok, out_shape=SDS(a.shape, a.dtype),
                          name=name)(a)


def _norm_fwd(x, g, name):
    T = x.shape[0]
    tm = min(512, T)

    def body(x_ref, g_ref, h_ref):
        xv = x_ref[...]
        r = lax.rsqrt(jnp.mean(xv * xv, axis=-1, keepdims=True) + EPS)
        h_ref[...] = (xv * r * g_ref[...]).astype(bf16)

    return pl.pallas_call(
        body, grid=(T // tm,),
        in_specs=[pl.BlockSpec((tm, D), lambda i: (i, 0)), pl.BlockSpec((1, D), lambda i: (0, 0))],
        out_specs=pl.BlockSpec((tm, D), lambda i: (i, 0)),
        out_shape=SDS((T, D), bf16), name=name)(x, g)


FFN_ROW_CHUNK = 256


def _ffn_tiles(T):
    return min(1024, T), 256


def _ffn_fwd(h, w, x, target, next_gain, name):
    T = h.shape[0]
    tm, tf = _ffn_tiles(T)
    nf = F // tf
    with_loss = target is not None
    assert with_loss != (next_gain is not None)

    def body(*refs):
        if with_loss:
            h_ref, w_ref, x_hbm, t_hbm, xo_ref, g_ref, u_ref, dyb_ref, loss_ref, tbuf, sem = refs
        else:
            h_ref, w_ref, x_hbm, gain_ref, xo_ref, g_ref, u_ref, hn_ref, sem = refs
        fi = pl.program_id(0)

        @pl.when(fi == 0)
        def _():
            cp = pltpu.make_async_copy(x_hbm, xo_ref, sem)
            cp.start()
            cp.wait()

        wgu = w_ref[0:2].reshape(2 * tf, D)
        for r in range(0, T, tm):
            rows = slice(r, r + tm)
            gu = _nt(h_ref[rows, :], wgu)
            gate, up = gu[:, :tf], gu[:, tf:]
            act = gate * _sigmoid(gate) * up
            g_ref[0, rows, :] = gate.astype(bf16)
            u_ref[0, rows, :] = up.astype(bf16)
            xo_ref[rows, :] += _nn((0.5 * act).astype(bf16), w_ref[2])

        if with_loss:
            @pl.when(fi == nf - 1)
            def _():
                lanes = jnp.zeros((1, 128), f32)
                for r in range(0, T, tm):
                    rows = slice(r, r + tm)
                    cp = pltpu.make_async_copy(t_hbm.at[pl.ds(r, tm), :], tbuf, sem)
                    cp.start()
                    cp.wait()
                    e = xo_ref[rows, :] - tbuf[...]
                    dy = e * (1.0 / D)
                    xo_ref[rows, :] = dy
                    dyb_ref[rows, :] = (0.5 * dy).astype(bf16)
                    col = jnp.sum(e * e, axis=0, keepdims=True) * (0.5 / D)
                    for k in range(D // 128):
                        lanes = lanes + col[:, 128 * k:128 * (k + 1)]
                loss_ref[...] = lanes
        else:
            @pl.when(fi == nf - 1)
            def _():
                for r in range(0, T, FFN_ROW_CHUNK):
                    rows = slice(r, r + FFN_ROW_CHUNK)
                    xv = xo_ref[rows, :]
                    rstd = lax.rsqrt(jnp.mean(xv * xv, axis=-1, keepdims=True) + EPS)
                    hn_ref[rows, :] = (xv * rstd * gain_ref[...]).astype(bf16)

    tok = pl.BlockSpec((T, D), lambda f: (0, 0))
    act_spec = pl.BlockSpec((1, T, tf), lambda f: (f, 0, 0))
    hbm = pl.BlockSpec(memory_space=pl.ANY)
    in_specs = [tok, pl.BlockSpec((3, tf, D), lambda f: (0, f, 0)), hbm]
    out_specs = [tok, act_spec, act_spec]
    out_shape = [SDS((T, D), f32), SDS((nf, T, tf), bf16), SDS((nf, T, tf), bf16)]
    scratch = [pltpu.SemaphoreType.DMA]
    args = [h, w, x]
    if with_loss:
        in_specs.append(hbm)
        args.append(target)
        out_specs += [tok, pl.BlockSpec((1, 128), lambda f: (0, 0))]
        out_shape += [SDS((T, D), bf16), SDS((1, 128), f32)]
        scratch = [pltpu.VMEM((tm, D), f32)] + scratch
    else:
        in_specs.append(pl.BlockSpec((1, D), lambda f: (0, 0)))
        args.append(next_gain)
        out_specs.append(tok)
        out_shape.append(SDS((T, D), bf16))
    return pl.pallas_call(
        body, grid=(nf,), in_specs=in_specs, out_specs=out_specs, out_shape=tuple(out_shape), scratch_shapes=scratch,
        compiler_params=_cparams(("arbitrary",), VMEM_LIMIT_V7X), name=name)(*args)


NORM_BWD_ROWS = 512


def _norm_bwd_scratch(dh_in_hbm, with_bf16):
    buf = lambda dt: pltpu.VMEM((2, NORM_BWD_ROWS, D), dt)
    return [buf(f32), buf(f32), buf(f32) if dh_in_hbm else None, buf(f32), buf(bf16) if with_bf16 else None,
            pltpu.SemaphoreType.DMA((6,)), pltpu.SemaphoreType.DMA((4,))]


def _norm_bwd_rows(T, dh_src, x_hbm, gain_ref, dr_hbm, dx_hbm, dxb_hbm, xbuf, rbuf, hbuf, obuf, obb, in_sems, out_sems):
    tm = min(NORM_BWD_ROWS, T)
    nchunk = T // tm

    def loads(i):
        s, rows = i % 2, pl.ds(i * tm, tm)
        cps = [pltpu.make_async_copy(x_hbm.at[rows, :], xbuf.at[s, pl.ds(0, tm), :], in_sems.at[3 * s]),
               pltpu.make_async_copy(dr_hbm.at[rows, :], rbuf.at[s, pl.ds(0, tm), :], in_sems.at[3 * s + 1])]
        if hbuf is not None:
            cps.append(pltpu.make_async_copy(dh_src.at[rows, :], hbuf.at[s, pl.ds(0, tm), :], in_sems.at[3 * s + 2]))
        return cps

    def stores(i):
        s, rows = i % 2, pl.ds(i * tm, tm)
        cps = [pltpu.make_async_copy(obuf.at[s, pl.ds(0, tm), :], dx_hbm.at[rows, :], out_sems.at[2 * s])]
        if dxb_hbm is not None:
            cps.append(pltpu.make_async_copy(obb.at[s, pl.ds(0, tm), :], dxb_hbm.at[rows, :], out_sems.at[2 * s + 1]))
        return cps

    for cp in loads(0):
        cp.start()
    dg = jnp.zeros((1, D), f32)
    for i in range(nchunk):
        s = i % 2
        if i + 1 < nchunk:
            for cp in loads(i + 1):
                cp.start()
        for cp in loads(i):
            cp.wait()
        if i >= 2:
            for cp in stores(i - 2):
                cp.wait()
        xv = xbuf[s, 0:tm, :]
        rstd = lax.rsqrt(jnp.mean(xv * xv, axis=-1, keepdims=True) + EPS)
        xh = xv * rstd
        dhv = hbuf[s, 0:tm, :] if hbuf is not None else dh_src[i * tm:(i + 1) * tm, :]
        dxh = dhv * gain_ref[...]
        dx = rbuf[s, 0:tm, :] + rstd * (dxh - xh * jnp.mean(dxh * xh, axis=-1, keepdims=True))
        obuf[s, 0:tm, :] = dx
        if dxb_hbm is not None:
            obb[s, 0:tm, :] = dx.astype(bf16)
        dg = dg + jnp.sum(dhv * xh, axis=0, keepdims=True)
        for cp in stores(i):
            cp.start()
    for i in range(max(nchunk - 2, 0), nchunk):
        for cp in stores(i):
            cp.wait()
    return dg


def _ffn_bwd(dob, h, gate, up, w, norm, name):
    T = h.shape[0]
    _, tf = _ffn_tiles(T)
    nf = F // tf
    nin = 5 if norm is None else 8
    nout = 2 if norm is None else 4

    def body(*refs):
        do_hbm, h_hbm, g_ref, u_ref, w_ref = refs[:5]
        dw_ref = refs[nin + nout - 1]
        do_v, h_v, dh_acc, dgu_s, act_s, sems = refs[nin + nout:nin + nout + 6]
        fi = pl.program_id(0)

        @pl.when(fi == 0)
        def _():
            loads = [pltpu.make_async_copy(do_hbm, do_v, sems.at[0]), pltpu.make_async_copy(h_hbm, h_v, sems.at[1])]
            for cp in loads:
                cp.start()
            dh_acc[...] = jnp.zeros_like(dh_acc)
            for cp in loads:
                cp.wait()

        wgu = w_ref[0:2].reshape(2 * tf, D)
        for r in range(0, T, FFN_ROW_CHUNK):
            rows = slice(r, r + FFN_ROW_CHUNK)
            dov = do_v[rows, :]
            gv = g_ref[0, rows, :].astype(f32)
            uv = u_ref[0, rows, :].astype(f32)
            sg = _sigmoid(gv)
            sil = gv * sg
            dact = _nt(dov, w_ref[2])
            dup = dact * sil
            dgate = dact * uv * (sg * (1.0 + gv * (1.0 - sg)))
            dgu = jnp.concatenate([dgate.astype(bf16), dup.astype(bf16)], axis=1)
            dgu_s[rows, :] = dgu
            act_s[rows, :] = (sil * uv).astype(bf16)
            dh_acc[rows, :] += _nn(dgu, wgu)
        dw_ref[0:2] = _tn(dgu_s[...], h_v[...]).reshape(2, tf, D).astype(bf16)
        dw_ref[2] = _tn(act_s[...], do_v[...]).astype(bf16)

        @pl.when(fi == nf - 1)
        def _():
            if norm is None:
                out = pltpu.make_async_copy(dh_acc, refs[nin], sems.at[0])
                out.start()
                out.wait()
            else:
                x_hbm, gain_ref, dr_hbm, dx_hbm, dxb_hbm, dg_ref = refs[5:11]
                xbuf, rbuf, obuf, obb, in_sems, out_sems = refs[nin + nout + 6:]
                dg_ref[...] = _norm_bwd_rows(T, dh_acc, x_hbm, gain_ref, dr_hbm, dx_hbm, dxb_hbm,
                                             xbuf, rbuf, None, obuf, obb, in_sems, out_sems)

    act_spec = pl.BlockSpec((1, T, tf), lambda f: (f, 0, 0))
    wspec = pl.BlockSpec((3, tf, D), lambda f: (0, f, 0))
    vec = pl.BlockSpec((1, D), lambda f: (0, 0))
    hbm = pl.BlockSpec(memory_space=pl.ANY)
    in_specs, out_specs = [hbm, hbm, act_spec, act_spec, wspec], [hbm, wspec]
    out_shape, args = [SDS((T, D), f32), SDS((3, F, D), bf16)], [dob, h, gate, up, w]
    scratch = [pltpu.VMEM((T, D), bf16), pltpu.VMEM((T, D), bf16), pltpu.VMEM((T, D), f32),
               pltpu.VMEM((T, 2 * tf), bf16), pltpu.VMEM((T, tf), bf16), pltpu.SemaphoreType.DMA((2,))]
    if norm is not None:
        in_specs += [hbm, vec, hbm]
        out_specs = [hbm, hbm, vec, wspec]
        out_shape = [SDS((T, D), f32), SDS((T, D), bf16), SDS((1, D), f32), SDS((3, F, D), bf16)]
        args += list(norm)
        scratch += [sc for sc in _norm_bwd_scratch(False, True) if sc is not None]
    return pl.pallas_call(
        body, grid=(nf,), in_specs=in_specs, out_specs=out_specs, out_shape=tuple(out_shape), scratch_shapes=scratch,
        compiler_params=_cparams(("arbitrary",), VMEM_LIMIT_V7X), name=name)(*args)


def _in_proj_fwd(h, wint, name):
    T = h.shape[0]
    tm = min(512, T)

    def body(h_ref, w_ref, z_ref):
        z_ref[...] = _nt(h_ref[...], w_ref[...])

    return pl.pallas_call(
        body, grid=(T // tm,),
        in_specs=[pl.BlockSpec((tm, D), lambda i: (i, 0)), pl.BlockSpec((DIN, D), lambda i: (0, 0))],
        out_specs=pl.BlockSpec((tm, DIN), lambda i: (i, 0)),
        out_shape=SDS((T, DIN), f32), name=name)(h, wint)


def _in_proj_bwd(dz, wint, h, norm, out_scale, name):
    x, g, dres = norm
    T = h.shape[0]
    tm = min(512, T)
    nt = T // tm

    def body(dz_ref, w_ref, h_ref, x_ref, g_ref, dr_ref, dx_ref, dxb_ref, dg_ref, dw_ref, acc):
        i = pl.program_id(0)
        dzb = dz_ref[...].astype(bf16)
        dhv = _nn(dzb, w_ref[...])
        part = _tn(dzb, h_ref[...])
        xv = x_ref[...]
        rstd = lax.rsqrt(jnp.mean(xv * xv, axis=-1, keepdims=True) + EPS)
        xh = xv * rstd
        dxh = dhv * g_ref[...]
        dx = dr_ref[...] + rstd * (dxh - xh * jnp.mean(dxh * xh, axis=-1, keepdims=True))
        dx_ref[...] = dx
        dxb_ref[...] = (out_scale * dx).astype(bf16)
        dg = jnp.sum(dhv * xh, axis=0, keepdims=True)

        @pl.when(i == 0)
        def _():
            acc[...] = part
            dg_ref[...] = dg

        @pl.when(i > 0)
        def _():
            acc[...] += part
            dg_ref[...] += dg

        @pl.when(i == nt - 1)
        def _():
            dw_ref[...] = acc[...].astype(bf16)

    wspec = pl.BlockSpec((DIN, D), lambda i: (0, 0))
    tok = pl.BlockSpec((tm, D), lambda i: (i, 0))
    vec = pl.BlockSpec((1, D), lambda i: (0, 0))
    return pl.pallas_call(
        body, grid=(nt,),
        in_specs=[pl.BlockSpec((tm, DIN), lambda i: (i, 0)), wspec, tok, tok, vec, tok],
        out_specs=[tok, tok, vec, wspec],
        out_shape=(SDS((T, D), f32), SDS((T, D), bf16), SDS((1, D), f32), SDS((DIN, D), bf16)),
        scratch_shapes=[pltpu.VMEM((DIN, D), f32)],
        compiler_params=_cparams(("arbitrary",)), name=name)(dz, wint, h, x, g, dres)


def _out_proj_fwd(ymix, wout, x, g, name):
    T = x.shape[0]
    tm = min(512, T)

    def body(y_ref, w_ref, x_ref, g_ref, o_ref, h_ref):
        o = x_ref[...] + _nn(y_ref[...], w_ref[...])
        o_ref[...] = o
        r = lax.rsqrt(jnp.mean(o * o, axis=-1, keepdims=True) + EPS)
        h_ref[...] = (o * r * g_ref[...]).astype(bf16)

    tok = pl.BlockSpec((tm, D), lambda i: (i, 0))
    return pl.pallas_call(
        body, grid=(T // tm,),
        in_specs=[pl.BlockSpec((tm, DMIX), lambda i: (i, 0)), pl.BlockSpec((DMIX, D), lambda i: (0, 0)), tok,
                  pl.BlockSpec((1, D), lambda i: (0, 0))],
        out_specs=[tok, tok], out_shape=(SDS((T, D), f32), SDS((T, D), bf16)), name=name)(ymix, wout, x, g)


def _out_proj_bwd(dxb, wout, ymix, name):
    T = dxb.shape[0]
    tm = min(512, T)
    nt = T // tm

    def body(dx_ref, w_ref, y_ref, dy_ref, dw_ref, acc):
        i = pl.program_id(0)
        dxv = dx_ref[...]
        dy_ref[...] = _nt(dxv, w_ref[...])
        part = _tn(y_ref[...], dxv)

        @pl.when(i == 0)
        def _():
            acc[...] = part

        @pl.when(i > 0)
        def _():
            acc[...] += part

        @pl.when(i == nt - 1)
        def _():
            dw_ref[...] = acc[...].astype(bf16)

    wspec = pl.BlockSpec((DMIX, D), lambda i: (0, 0))
    return pl.pallas_call(
        body, grid=(nt,),
        in_specs=[pl.BlockSpec((tm, D), lambda i: (i, 0)), wspec, pl.BlockSpec((tm, DMIX), lambda i: (i, 0))],
        out_specs=[pl.BlockSpec((tm, DMIX), lambda i: (i, 0)), wspec],
        out_shape=(SDS((T, DMIX), f32), SDS((DMIX, D), bf16)),
        scratch_shapes=[pltpu.VMEM((DMIX, D), f32)],
        compiler_params=_cparams(("arbitrary",)), name=name)(dxb, wout, ymix)


def _t5_bucket_table():
    ql = np.arange(BLK)[:, None]
    kl = np.arange(2 * BLK)[None, :]
    n = np.maximum(ql + BLK - kl, 0)
    max_exact = NBUCK // 2
    large = max_exact + (np.log(np.maximum(n, 1) / max_exact) / np.log(MAX_DISTANCE / max_exact)
                         * (NBUCK - max_exact)).astype(np.int32)
    large = np.minimum(large, NBUCK - 1)
    return np.where(n < max_exact, n, large).astype(np.int32)


def _fill_bias(bk_ref, rb_ref, bias_scr):
    bk = bk_ref[...]
    for h in range(NH):
        def step(b, acc, h=h):
            return acc + jnp.where(bk == b, rb_ref[b, h], 0.0)
        bias_scr[h] = lax.fori_loop(0, NBUCK, step, jnp.zeros((BLK, 2 * BLK), f32))


MIX_SUB = 4


class _Window:
    def __init__(self, zc_ref, zp_ref, n, s):
        self.blk = n * MIX_SUB + s
        self.cur = lambda a, b: zc_ref[s * BLK:(s + 1) * BLK, a:b]
        self.prev = (lambda a, b: zp_ref[:, a:b]) if s == 0 else (lambda a, b: zc_ref[(s - 1) * BLK:s * BLK, a:b])


def _attn_qkv(win, kh, qg, kg):
    kc = DATTN + HD * kh
    vc = DATTN + DKV + HD * kh
    kx = jnp.concatenate([win.prev(kc, kc + HD), win.cur(kc, kc + HD)], axis=0)
    vx = jnp.concatenate([win.prev(vc, vc + HD), win.cur(vc, vc + HD)], axis=0)
    qx = jnp.concatenate([win.cur(HD * (GQA * kh + g), HD * (GQA * kh + g + 1)) for g in range(GQA)], axis=0)
    rq = lax.rsqrt(jnp.mean(qx * qx, axis=-1, keepdims=True) + EPS)
    rk = lax.rsqrt(jnp.mean(kx * kx, axis=-1, keepdims=True) + EPS)
    qhat, khat = qx * rq, kx * rk
    return dict(qhat=qhat, khat=khat, rq=rq, rk=rk, qsb=(qhat * (qg * SCALE)).astype(bf16),
                knb=(khat * kg).astype(bf16), vb=vx.astype(bf16))


def _window_masks(n):
    row = lax.broadcasted_iota(i32, (GQA * BLK, 2 * BLK), 0) & (BLK - 1)
    col = lax.broadcasted_iota(i32, (GQA * BLK, 2 * BLK), 1)
    band = (col > row) & (col <= row + BLK)
    return band & ((col >= BLK) | (n > 0)), band


def _attn_probs(a, kh, sk_ref, bias_scr, mask):
    s = _nt(a["qsb"], a["knb"]) + bias_scr[GQA * kh:GQA * (kh + 1)].reshape(GQA * BLK, 2 * BLK)
    s = jnp.where(mask, s, NEG)
    ridx = lax.broadcasted_iota(i32, (GQA * BLK, 1), 0)
    sink = jnp.full((GQA * BLK, 1), sk_ref[GQA * kh + GQA - 1], f32)
    for g in range(GQA - 2, -1, -1):
        sink = jnp.where(ridx < (g + 1) * BLK, sk_ref[GQA * kh + g], sink)
    m = jnp.maximum(jnp.max(s, axis=-1, keepdims=True), sink)
    e = jnp.exp(s - m)
    den = jnp.sum(e, axis=-1, keepdims=True) + jnp.exp(sink - m)
    return e / den


POOL_STEPS = {2: (1,), 4: (1, 2), 8: (1, 2, 4), 16: (1, 2, 4, 8)}


def _pool_group(win, g, w):
    n = win.blk
    c0 = DATTN + 2 * DKV + PGD * g
    uc = win.cur(c0, c0 + PGD)
    up = jnp.where(n > 0, win.prev(c0, c0 + PGD), 0.0)
    sm = jnp.concatenate([up, uc], axis=0)
    for k in POOL_STEPS[w]:
        sm = sm + pltpu.roll(sm, k, axis=0)
    pos = n * BLK + lax.broadcasted_iota(i32, (BLK, 1), 0) + 1
    cnt = jnp.minimum(pos, w).astype(f32)
    return sm[BLK:2 * BLK] / cnt - uc, cnt


def _mix_fwd(z, qg, kg, sinks, relb, bucket, pool_w, pscale, name):
    T = z.shape[0]
    step_rows = MIX_SUB * BLK
    nsteps = T // step_rows

    def body(zc_ref, zp_ref, qg_ref, kg_ref, sk_ref, rb_ref, bk_ref, pw_ref, ps_ref, y_ref, p_ref, bias_scr, yacc):
        n = pl.program_id(0)

        @pl.when(n == 0)
        def _():
            _fill_bias(bk_ref, rb_ref, bias_scr)

        first_mask, mask = _window_masks(n)
        for s in range(MIX_SUB):
            win = _Window(zc_ref, zp_ref, n, s)
            rows = slice(s * BLK, (s + 1) * BLK)
            for kh in range(NKV):
                a = _attn_qkv(win, kh, qg_ref[...], kg_ref[...])
                pb = _attn_probs(a, kh, sk_ref, bias_scr, first_mask if s == 0 else mask).astype(bf16)
                p_ref[s, GQA * kh:GQA * (kh + 1)] = pb.reshape(GQA, BLK, 2 * BLK)
                o = _nn(pb, a["vb"])
                for g in range(GQA):
                    hc = HD * (GQA * kh + g)
                    yacc[rows, hc:hc + HD] = o[g * BLK:(g + 1) * BLK]
            for g, w in enumerate(POOL_WINDOWS):
                pooled, _ = _pool_group(win, g, w)
                yp = _nn(pooled.astype(bf16), pw_ref[g].astype(bf16)) * ps_ref[:, PGD * g:PGD * (g + 1)]
                yacc[rows, DATTN + PGD * g:DATTN + PGD * (g + 1)] = yp
        y_ref[...] = yacc[...].astype(bf16)

    full = lambda *shape: pl.BlockSpec(shape, lambda n: (0,) * len(shape))
    smem = pl.BlockSpec(memory_space=pltpu.SMEM)
    return pl.pallas_call(
        body, grid=(nsteps,),
        in_specs=[pl.BlockSpec((step_rows, DIN), lambda n: (n, 0)),
                  pl.BlockSpec((BLK, DIN), lambda n: (jnp.maximum(n * MIX_SUB - 1, 0), 0)),
                  full(1, HD), full(1, HD), smem, smem, full(BLK, 2 * BLK),
                  full(len(POOL_WINDOWS), PGD, PGD), full(1, DPOOL)],
        out_specs=[pl.BlockSpec((step_rows, DMIX), lambda n: (n, 0)),
                   pl.BlockSpec((MIX_SUB, NH, BLK, 2 * BLK), lambda n: (n, 0, 0, 0))],
        out_shape=(SDS((T, DMIX), bf16), SDS((T // BLK, NH, BLK, 2 * BLK), bf16)),
        scratch_shapes=[pltpu.VMEM((NH, BLK, 2 * BLK), f32), pltpu.VMEM((step_rows, DMIX), f32)],
        compiler_params=_cparams(("arbitrary",)), name=name)(z, z, qg, kg, sinks, relb, bucket, pool_w, pscale)


def _mix_bwd(z, dy, probs, qg, kg, relb, bucket, pool_w, pscale, name):
    T = z.shape[0]
    step_rows = MIX_SUB * BLK
    nsteps = T // step_rows

    def body(zc_ref, zp_ref, dy_ref, p_ref, qg_ref, kg_ref, bk_ref, pw_ref, ps_ref,
             dz_ref, dqg_ref, dkg_ref, dsk_ref, drb_ref, dpw_ref, dps_ref, dbias_scr):
        n = pl.program_id(0)

        @pl.when(n == 0)
        def _():
            dbias_scr[...] = jnp.zeros_like(dbias_scr)
            dqg_ref[...] = jnp.zeros_like(dqg_ref)
            dkg_ref[...] = jnp.zeros_like(dkg_ref)
            dpw_ref[...] = jnp.zeros_like(dpw_ref)
            dps_ref[...] = jnp.zeros_like(dps_ref)

        qg, kg = qg_ref[...], kg_ref[...]
        for s in range(MIX_SUB):
            win = _Window(zc_ref, zp_ref, n, s)
            blk = win.blk
            rows = pl.ds(pl.multiple_of(blk * BLK, BLK), BLK)
            prow = pl.ds(pl.multiple_of(jnp.maximum(blk - 1, 0) * BLK, BLK), BLK)
            dyr = slice(s * BLK, (s + 1) * BLK)

            def into_prev(fn, s=s):
                if s == 0:
                    pl.when(n > 0)(fn)
                else:
                    fn()

            for kh in range(NKV):
                a = _attn_qkv(win, kh, qg, kg)
                pb = p_ref[s, GQA * kh:GQA * (kh + 1)].reshape(GQA * BLK, 2 * BLK)
                p = pb.astype(f32)
                do = jnp.concatenate([dy_ref[dyr, HD * (GQA * kh + g):HD * (GQA * kh + g + 1)] for g in range(GQA)],
                                     axis=0).astype(bf16)
                dv = _tn(pb, do)
                dp = _nt(do, a["vb"])
                delta = jnp.sum(p * dp, axis=-1, keepdims=True)
                ds = p * (dp - delta)
                for g in range(GQA):
                    dbias_scr[GQA * kh + g] += ds[g * BLK:(g + 1) * BLK]
                dsb = ds.astype(bf16)
                dqn = _nn(dsb, a["knb"]) * SCALE
                dkn = _tn(dsb, a["qsb"])
                qhat, khat = a["qhat"], a["khat"]
                dqg_ref[...] += jnp.sum(dqn * qhat, axis=0, keepdims=True)
                dkg_ref[...] += jnp.sum(dkn * khat, axis=0, keepdims=True)
                dqh = dqn * qg
                dq = a["rq"] * (dqh - qhat * jnp.mean(dqh * qhat, axis=-1, keepdims=True))
                dkh = dkn * kg
                dk = a["rk"] * (dkh - khat * jnp.mean(dkh * khat, axis=-1, keepdims=True))
                kc = DATTN + HD * kh
                vc = DATTN + DKV + HD * kh
                for g in range(GQA):
                    hc = HD * (GQA * kh + g)
                    dz_ref[rows, hc:hc + HD] = dq[g * BLK:(g + 1) * BLK]
                dz_ref[rows, kc:kc + HD] = dk[BLK:2 * BLK]
                dz_ref[rows, vc:vc + HD] = dv[BLK:2 * BLK]

                def kv_prev(dk=dk, dv=dv, kc=kc, vc=vc, prow=prow):
                    dz_ref[prow, kc:kc + HD] += dk[0:BLK]
                    dz_ref[prow, vc:vc + HD] += dv[0:BLK]

                into_prev(kv_prev)

            for g, w in enumerate(POOL_WINDOWS):
                c0 = DATTN + 2 * DKV + PGD * g
                pooled, cnt = _pool_group(win, g, w)
                pb = pooled.astype(bf16)
                wb = pw_ref[g].astype(bf16)
                dyp = dy_ref[dyr, DATTN + PGD * g:DATTN + PGD * (g + 1)]
                ypre = _nn(pb, wb)
                dps_ref[:, PGD * g:PGD * (g + 1)] += jnp.sum(dyp * ypre, axis=0, keepdims=True)
                dyg = (dyp * ps_ref[:, PGD * g:PGD * (g + 1)]).astype(bf16)
                dpw_ref[g] += _tn(pb, dyg)
                dpooled = _nt(dyg, wb)
                due = jnp.concatenate([jnp.zeros((BLK, PGD), f32), dpooled / cnt], axis=0)
                for k in POOL_STEPS[w]:
                    due = due + pltpu.roll(due, 2 * BLK - k, axis=0)
                dz_ref[rows, c0:c0 + PGD] = due[BLK:2 * BLK] - dpooled

                def pool_prev(due=due, c0=c0, prow=prow):
                    dz_ref[prow, c0:c0 + PGD] += due[0:BLK]

                into_prev(pool_prev)

        @pl.when(n == nsteps - 1)
        def _():
            bk = bk_ref[...]
            ri = lax.broadcasted_iota(i32, (NBUCK, NH), 0)
            ci = lax.broadcasted_iota(i32, (NBUCK, NH), 1)

            def step(b, acc):
                for h in range(NH):
                    sel = jnp.where(bk == b, dbias_scr[h], 0.0)
                    tot = jnp.sum(jnp.sum(sel, axis=1, keepdims=True), axis=0, keepdims=True)
                    acc = acc + jnp.where((ri == b) & (ci == h), tot, 0.0)
                return acc

            drb_ref[...] = lax.fori_loop(0, NBUCK, step, jnp.zeros((NBUCK, NH), f32))
            lane = lax.broadcasted_iota(i32, (1, 128), 1)
            dsk = jnp.zeros((1, 128), f32)
            for h in range(NH):
                tot = jnp.sum(jnp.sum(dbias_scr[h], axis=1, keepdims=True), axis=0, keepdims=True)
                dsk = dsk - jnp.where(lane == h, tot, 0.0)
            dsk_ref[...] = dsk

    full = lambda *shape: pl.BlockSpec(shape, lambda n: (0,) * len(shape))
    npg = len(POOL_WINDOWS)
    return pl.pallas_call(
        body, grid=(nsteps,),
        in_specs=[pl.BlockSpec((step_rows, DIN), lambda n: (n, 0)),
                  pl.BlockSpec((BLK, DIN), lambda n: (jnp.maximum(n * MIX_SUB - 1, 0), 0)),
                  pl.BlockSpec((step_rows, DMIX), lambda n: (n, 0)),
                  pl.BlockSpec((MIX_SUB, NH, BLK, 2 * BLK), lambda n: (n, 0, 0, 0)),
                  full(1, HD), full(1, HD), full(BLK, 2 * BLK), full(npg, PGD, PGD), full(1, DPOOL)],
        out_specs=[full(T, DIN), full(1, HD), full(1, HD), full(1, 128), full(NBUCK, NH),
                   full(npg, PGD, PGD), full(1, DPOOL)],
        out_shape=(SDS((T, DIN), f32), SDS((1, HD), f32), SDS((1, HD), f32), SDS((1, 128), f32),
                   SDS((NBUCK, NH), f32), SDS((npg, PGD, PGD), f32), SDS((1, DPOOL), f32)),
        scratch_shapes=[pltpu.VMEM((NH, BLK, 2 * BLK), f32)],
        compiler_params=_cparams(("arbitrary",), VMEM_LIMIT_V7X),
        name=name)(z, z, dy, probs, qg, kg, bucket, pool_w, pscale)


class _LocalWeights:
    def __init__(self, w1, wint, wout, w2):
        self.w1, self.wint, self.wout, self.w2 = w1, wint, wout, w2

    def ffn1(self):
        return self.w1

    def first_norm(self, x, gain):
        return _norm_fwd(x, gain, "norm1_fwd")

    def after_ffn1(self, gain, x1):
        return gain

    def mix(self, after):
        return self.wint, self.wout

    def before_out_proj(self, wout, after):
        return wout

    def ffn2(self, after):
        return self.w2

    def out_ffn2_grads_ready(self, dwout, dw2, after):
        return after

    def before_ffn1_bwd(self, dwint, dx1b):
        return dx1b


def _local_step(x, target, weights, g1, gm, g3, qg, kg, sinks, relb, pool_w, pscale):
    bucket = jnp.asarray(_t5_bucket_table())
    sk = sinks.reshape(NH)
    w1 = weights.ffn1()
    h1 = weights.first_norm(x, g1)
    x1, gate1, up1, h2 = _ffn_fwd(h1, w1, x, None, gm, "ffn1_fwd")
    gm = weights.after_ffn1(gm, x1)
    wint, wout = weights.mix(h2)
    z = _in_proj_fwd(h2, wint, "in_proj_fwd")
    ymix, probs = _mix_fwd(z, qg, kg, sk, relb, bucket, pool_w, pscale, "mix_fwd")
    wout = weights.before_out_proj(wout, ymix)
    x2, h3 = _out_proj_fwd(ymix, wout, x1, g3, "out_proj_fwd")
    w2 = weights.ffn2(h3)
    dy, gate2, up2, dyb, loss_lanes = _ffn_fwd(h3, w2, x2, target, None, "ffn2_fwd")

    dx2, dx2b, dg3, dw2 = _ffn_bwd(dyb, h3, gate2, up2, w2, (x2, g3, dy), "ffn2_bwd")
    dymix, dwout = _out_proj_bwd(dx2b, wout, ymix, "out_proj_bwd")
    dymix = weights.out_ffn2_grads_ready(dwout, dw2, dymix)
    dz, dqg, dkg, dsk, drb, dpw, dps = _mix_bwd(z, dymix, probs, qg, kg, relb, bucket, pool_w, pscale, "mix_bwd")
    dx1, dx1b, dgm, dwint = _in_proj_bwd(dz, wint, h2, (x1, gm, dx2), 0.5, "in_proj_bwd")
    dx1b = weights.before_ffn1_bwd(dwint, dx1b)
    dh1, dw1 = _ffn_bwd(dx1b, h1, gate1, up1, w1, None, "ffn1_bwd")
    small = dict(mix_norm=dgm, ffn2_norm=dg3, pool_scale=dps, q_norm=dqg, k_norm=dkg,
                 attn_sinks=dsk[:, :NH], rel_bias=drb, pool_w=dpw, loss=loss_lanes)
    return (dh1, dx1), (dw1, dwint, dwout, dw2), small


SMALL_NAMES = ("ffn1_norm", "mix_norm", "ffn2_norm", "pool_scale", "q_norm", "k_norm", "attn_sinks", "rel_bias",
               "pool_w", "loss")
SMALL_SHAPES = dict(ffn1_norm=(1, D), mix_norm=(1, D), ffn2_norm=(1, D), pool_scale=(1, DPOOL), q_norm=(1, HD),
                    k_norm=(1, HD), attn_sinks=(1, NH), rel_bias=(NBUCK, NH),
                    pool_w=(1, len(POOL_WINDOWS), PGD, PGD), loss=(1, 128))


def _small_rows(name):
    return -(-int(np.prod(SMALL_SHAPES[name])) // 128)


SMALL_OFF = {}
_r = 0
for _n in SMALL_NAMES:
    SMALL_OFF[_n] = _r
    _r += _small_rows(_n)
SMALL_ROWS = -(-_r // 16) * 16
LOSS_ROW = SMALL_OFF["loss"]


def _pack_small(vals):
    parts = []
    for n in SMALL_NAMES:
        size = _small_rows(n) * 128
        if n in vals:
            flat = vals[n].astype(f32).reshape(-1)
            parts.append(jnp.pad(flat, (0, size - flat.shape[0])))
        else:
            parts.append(jnp.zeros((size,), f32))
    flat = jnp.concatenate(parts)
    flat = jnp.pad(flat, (0, SMALL_ROWS * 128 - flat.shape[0]))
    return flat.reshape(SMALL_ROWS, 128)


def _unpack_small(packed, name):
    size = int(np.prod(SMALL_SHAPES[name]))
    r0 = SMALL_OFF[name]
    return packed[r0:r0 + _small_rows(name)].reshape(-1)[:size].reshape(SMALL_SHAPES[name])


def _position():
    return lax.axis_index("x"), lax.axis_index("y"), lax.axis_index("c")


def _dev_index(x, y, c):
    return 4 * x + 2 * y + c


G1_PIECES, MIX_PIECES, F2_PIECES = (0, 1, 2), (3, 4), (5, 6, 7)


def _group_rows(pieces):
    return sum(PIECE_ROWS[k] for k in pieces)


def _shard_piece(s_ref, k):
    return s_ref.at[pl.ds(PIECE_OFF[k], PIECE_ROWS[k]), :]


def _shard_group(s_ref, pieces):
    return s_ref.at[pl.ds(PIECE_OFF[pieces[0]], _group_rows(pieces)), :]


def _weight_pieces(w1_ref=None, wi_ref=None, wo_ref=None, w2_ref=None):
    arrs = {}
    if w1_ref is not None:
        arrs.update({0: w1_ref.at[0], 1: w1_ref.at[1], 2: w1_ref.at[2]})
    if wi_ref is not None:
        arrs[3] = wi_ref
    if wo_ref is not None:
        arrs[4] = wo_ref
    if w2_ref is not None:
        arrs.update({5: w2_ref.at[0], 6: w2_ref.at[1], 7: w2_ref.at[2]})
    return arrs


def _block_rows(arrs, k, dev):
    r = PIECE_ROWS[k]
    return arrs[k].at[pl.ds(pl.multiple_of(_dev_index(*dev) * r, 16), r), :]


NORM_ROWS = 512


def _all_gather_ffn1(shard, x, gain):
    pieces = G1_PIECES
    rest_pieces = MIX_PIECES + F2_PIECES
    half = FS // 2
    T = x.shape[0]
    SIB, X0, X1, Y0, Y1, RELAY_Y, RELAY_X, ON_X, ON_Y, ON_D0, ON_D1 = range(11)

    def body(s_ref, x_ref, g_ref, w1_ref, h_ref, wi_ref, wo_ref, w2_ref, xbuf, hbuf, rest_buf,
             send_sems, recv_sems, local_sem, norm_sems):
        x, y, c = _position()
        me, sib = (x, y, c), (x, y, 1 - c)
        xn, yn, dg = (1 - x, y, c), (x, 1 - y, c), (1 - x, 1 - y, c)
        arrs = _weight_pieces(w1_ref=w1_ref)

        def place_rest():
            rest = _weight_pieces(wi_ref=wi_ref, wo_ref=wo_ref, w2_ref=w2_ref)
            grp = _shard_group(s_ref, rest_pieces)
            load = pltpu.make_async_copy(grp, rest_buf, norm_sems.at[0])
            load.start()
            load.wait()
            base = PIECE_OFF[rest_pieces[0]]
            for k in rest_pieces:
                pltpu.make_async_copy(rest_buf.at[pl.ds(PIECE_OFF[k] - base, PIECE_ROWS[k]), :],
                                      _block_rows(rest, k, me), norm_sems.at[1]).start()
            pltpu.make_async_copy(grp, rest_buf, norm_sems.at[1]).wait()

        def first_norm():
            for r in range(0, T, NORM_ROWS):
                load = pltpu.make_async_copy(x_ref.at[pl.ds(r, NORM_ROWS), :], xbuf, norm_sems.at[0])
                load.start()
                load.wait()
                xv = xbuf[...]
                rs = lax.rsqrt(jnp.mean(xv * xv, axis=-1, keepdims=True) + EPS)
                hbuf[...] = (xv * rs * g_ref[...]).astype(bf16)
                store = pltpu.make_async_copy(hbuf, h_ref.at[pl.ds(r, NORM_ROWS), :], norm_sems.at[1])
                store.start()
                store.wait()

        def rows_of(k, block, hf):
            r = PIECE_ROWS[k]
            start, size = (0, r) if hf is None else (hf * half, half)
            return arrs[k].at[pl.ds(pl.multiple_of(_dev_index(*block) * r + start, 16), size), :]

        def copies(rel, block, hf, to, from_shard=False):
            def src(k):
                if not from_shard:
                    return rows_of(k, block, hf)
                start, size = (0, PIECE_ROWS[k]) if hf is None else (hf * half, half)
                return s_ref.at[pl.ds(PIECE_OFF[k] + start, size), :]
            return [pltpu.make_async_remote_copy(
                src_ref=src(k), dst_ref=rows_of(k, block, hf), send_sem=send_sems.at[rel], recv_sem=recv_sems.at[rel],
                device_id=to, device_id_type=MESH) for k in pieces]

        def waiter(rel, hf):
            nrows = len(pieces) * (FS if hf is None else half)
            grp = s_ref.at[pl.ds(0, nrows), :]
            return pltpu.make_async_remote_copy(src_ref=grp, dst_ref=grp, send_sem=send_sems.at[rel],
                                                recv_sem=recv_sems.at[rel], device_id=me, device_id_type=MESH)

        def start(cps):
            for cp in cps:
                cp.start()

        mine = [pltpu.make_async_copy(_shard_piece(s_ref, k), _block_rows(arrs, k, me), local_sem) for k in pieces]
        start(mine)
        start(copies(SIB, me, None, sib, True))
        start(copies(X0, me, 0, xn, True))
        start(copies(Y1, me, 1, yn, True))
        start(copies(X1, me, 1, xn, True))
        start(copies(Y0, me, 0, yn, True))
        first_norm()
        place_rest()
        waiter(X0, 0).wait_recv()
        start(copies(RELAY_Y, xn, 0, yn))
        waiter(Y1, 1).wait_recv()
        start(copies(RELAY_X, yn, 1, xn))
        waiter(X1, 1).wait_recv()
        start(copies(ON_X, xn, None, sib))
        waiter(Y0, 0).wait_recv()
        start(copies(ON_Y, yn, None, sib))
        waiter(RELAY_Y, 0).wait_recv()
        start(copies(ON_D0, dg, 0, sib))
        waiter(RELAY_X, 1).wait_recv()
        start(copies(ON_D1, dg, 1, sib))
        waiter(SIB, None).wait_recv()
        waiter(ON_X, None).wait_recv()
        waiter(ON_Y, None).wait_recv()
        waiter(ON_D0, 0).wait_recv()
        waiter(ON_D1, 1).wait_recv()
        for rel, hf in ((SIB, None), (X0, 0), (X1, 1), (Y0, 0), (Y1, 1), (RELAY_Y, 0), (RELAY_X, 1),
                        (ON_X, None), (ON_Y, None), (ON_D0, 0), (ON_D1, 1)):
            waiter(rel, hf).wait_send()
        grp = _shard_group(s_ref, pieces)
        pltpu.make_async_copy(grp, grp, local_sem).wait()

    hbm = pl.BlockSpec(memory_space=pl.ANY)
    return pl.pallas_call(
        body, in_specs=[hbm, hbm, pl.BlockSpec(memory_space=pltpu.VMEM)], out_specs=[hbm] * 5,
        out_shape=(SDS((3, F, D), bf16), SDS((T, D), bf16),
                   SDS((DIN, D), bf16), SDS((DMIX, D), bf16), SDS((3, F, D), bf16)),
        scratch_shapes=[pltpu.VMEM((NORM_ROWS, D), f32), pltpu.VMEM((NORM_ROWS, D), bf16),
                        pltpu.VMEM((_group_rows(rest_pieces), D), bf16),
                        pltpu.SemaphoreType.DMA((11,)), pltpu.SemaphoreType.DMA((11,)), pltpu.SemaphoreType.DMA,
                        pltpu.SemaphoreType.DMA((2,))],
        compiler_params=pltpu.CompilerParams(has_side_effects=True),
        name="all_gather_ffn1")(shard, x, gain)


HBM_SPEC = pl.BlockSpec(memory_space=pltpu.HBM)
SEM_SPEC = pl.BlockSpec(memory_space=pltpu.SEMAPHORE)
ANY_SPEC = pl.BlockSpec(memory_space=pl.ANY)
SPLIT_EFFECT = pltpu.SideEffectType.DATAFLOW_SIDE_EFFECTING


def _in_hbm(a):
    return pltpu.with_memory_space_constraint(a, pltpu.HBM)


def _hbm_like(a):
    return pltpu.HBM(a.shape, a.dtype)


def _gather_rest_start(shard, wi, wo, w2, w1):
    def body(s_ref, wi_ref, wo_ref, w2_ref, w1_ref,
             ssem_m, rsem_m0, rsem_m, ssem_f, rsem_f0, rsem_f, s_o, wi_o, wo_o, w2_o, w1_o):
        x, y, c = _position()
        me, sib = (x, y, c), (x, y, 1 - c)
        chips = [(1 - x, y), (x, 1 - y), (1 - x, 1 - y)]
        arrs = _weight_pieces(wi_ref=wi_ref, wo_ref=wo_ref, w2_ref=w2_ref)
        for pieces, ssem, rsem0, rsem in ((MIX_PIECES, ssem_m, rsem_m0, rsem_m), (F2_PIECES, ssem_f, rsem_f0, rsem_f)):
            for p in pieces:
                pltpu.make_async_remote_copy(
                    src_ref=_shard_piece(s_ref, p), dst_ref=_block_rows(arrs, p, me), send_sem=ssem.at[0],
                    recv_sem=rsem0, device_id=sib, device_id_type=MESH).start()
            for j, chip in enumerate(chips):
                for p in pieces:
                    pltpu.make_async_remote_copy(
                        src_ref=_shard_piece(s_ref, p), dst_ref=_block_rows(arrs, p, me), send_sem=ssem.at[1 + j],
                        recv_sem=rsem.at[j], device_id=(*chip, c), device_id_type=MESH).start()

    dma = pltpu.SemaphoreType.DMA
    return pl.pallas_call(
        body, name="gather_rest_start",
        out_shape=(dma((4,)), dma(()), dma((3,)), dma((4,)), dma(()), dma((3,)),
                   _hbm_like(shard), _hbm_like(wi), _hbm_like(wo), _hbm_like(w2), _hbm_like(w1)),
        in_specs=(HBM_SPEC,) * 5, out_specs=(SEM_SPEC,) * 6 + (HBM_SPEC,) * 5,
        input_output_aliases={0: 6, 1: 7, 2: 8, 3: 9, 4: 10},
        compiler_params=pltpu.CompilerParams(has_side_effects=SPLIT_EFFECT),
    )(_in_hbm(shard), _in_hbm(wi), _in_hbm(wo), _in_hbm(w2), _in_hbm(w1))


def _gather_mix_pass_on(rsem_m, wi, wo, thru, after):
    def body(wi_ref, wo_ref, thru_ref, rsem, after_ref, fsend, frecv, wi_o, wo_o, thru_o):
        x, y, c = _position()
        sib = (x, y, 1 - c)
        arrs = _weight_pieces(wi_ref=wi_ref, wo_ref=wo_ref)
        both = wi_ref.at[pl.ds(0, _group_rows(MIX_PIECES)), :]
        for j, chip in enumerate([(1 - x, y), (x, 1 - y), (1 - x, 1 - y)]):
            pltpu.make_async_remote_copy(src_ref=both, dst_ref=both, send_sem=fsend.at[j], recv_sem=rsem.at[j],
                                         device_id=(x, y, c), device_id_type=MESH).wait_recv()
            for p in MIX_PIECES:
                rows = _block_rows(arrs, p, (*chip, c))
                pltpu.make_async_remote_copy(src_ref=rows, dst_ref=rows, send_sem=fsend.at[j], recv_sem=frecv.at[j],
                                             device_id=sib, device_id_type=MESH).start()

    dma = pltpu.SemaphoreType.DMA
    return pl.pallas_call(
        body, name="gather_mix_pass_on",
        out_shape=(dma((3,)), dma((3,)), _hbm_like(wi), _hbm_like(wo), _hbm_like(thru)),
        in_specs=(HBM_SPEC, HBM_SPEC, HBM_SPEC, SEM_SPEC, ANY_SPEC), out_specs=(SEM_SPEC, SEM_SPEC) + (HBM_SPEC,) * 3,
        input_output_aliases={0: 2, 1: 3, 2: 4},
        compiler_params=pltpu.CompilerParams(has_side_effects=SPLIT_EFFECT),
    )(wi, wo, _in_hbm(thru), rsem_m, after)


def _gather_mix_wait(ssem_m, rsem_m0, fsend, frecv, shard, wi, wo, after):
    def body(s_ref, wi_ref, wo_ref, ssem, rsem0, fs, fr, after_ref, s_o, wi_o, wo_o):
        x, y, c = _position()
        grp = _shard_group(s_ref, MIX_PIECES)

        def waiter(send_sem, recv_sem):
            return pltpu.make_async_remote_copy(src_ref=grp, dst_ref=grp, send_sem=send_sem, recv_sem=recv_sem,
                                                device_id=(x, y, c), device_id_type=MESH)

        waiter(ssem.at[0], rsem0).wait_recv()
        for j in range(3):
            waiter(fs.at[j], fr.at[j]).wait_recv()
        for rel in range(4):
            waiter(ssem.at[rel], rsem0).wait_send()
        for j in range(3):
            waiter(fs.at[j], fr.at[j]).wait_send()

    return pl.pallas_call(
        body, name="gather_mix_wait", out_shape=(_hbm_like(shard), _hbm_like(wi), _hbm_like(wo)),
        in_specs=(HBM_SPEC,) * 3 + (SEM_SPEC,) * 4 + (ANY_SPEC,), out_specs=(HBM_SPEC,) * 3,
        input_output_aliases={0: 0, 1: 1, 2: 2},
        compiler_params=pltpu.CompilerParams(has_side_effects=SPLIT_EFFECT),
    )(shard, wi, wo, ssem_m, rsem_m0, fsend, frecv, after)


def _gather_ffn2_pass_on(rsem_f, w2, wo, after):
    def body(w2_ref, wo_ref, rsem, after_ref, fsend, frecv, w2_o, wo_o):
        x, y, c = _position()
        sib = (x, y, 1 - c)
        chips = [(1 - x, y), (x, 1 - y), (1 - x, 1 - y)]
        arrs = _weight_pieces(w2_ref=w2_ref)
        three = w2_ref.at[0, pl.ds(0, _group_rows(F2_PIECES)), :]
        for j, chip in enumerate(chips):
            pltpu.make_async_remote_copy(src_ref=three, dst_ref=three, send_sem=fsend.at[j], recv_sem=rsem.at[j],
                                         device_id=(x, y, c), device_id_type=MESH).wait_recv()
            for p in F2_PIECES:
                rows = _block_rows(arrs, p, (*chip, c))
                pltpu.make_async_remote_copy(src_ref=rows, dst_ref=rows, send_sem=fsend.at[j], recv_sem=frecv.at[j],
                                             device_id=sib, device_id_type=MESH).start()

    dma = pltpu.SemaphoreType.DMA
    return pl.pallas_call(
        body, name="gather_ffn2_pass_on", out_shape=(dma((3,)), dma((3,)), _hbm_like(w2), _hbm_like(wo)),
        in_specs=(HBM_SPEC, HBM_SPEC, SEM_SPEC, ANY_SPEC), out_specs=(SEM_SPEC, SEM_SPEC, HBM_SPEC, HBM_SPEC),
        input_output_aliases={0: 2, 1: 3},
        compiler_params=pltpu.CompilerParams(has_side_effects=SPLIT_EFFECT),
    )(w2, wo, rsem_f, after)


def _gather_ffn2_wait(ssem_f, rsem_f0, fsend, frecv, shard, w2, after):
    def body(s_ref, w2_ref, ssem, rsem0, fs, fr, after_ref, w2_o):
        x, y, c = _position()
        grp = _shard_group(s_ref, F2_PIECES)

        def waiter(send_sem, recv_sem):
            return pltpu.make_async_remote_copy(src_ref=grp, dst_ref=grp, send_sem=send_sem, recv_sem=recv_sem,
                                                device_id=(x, y, c), device_id_type=MESH)

        waiter(ssem.at[0], rsem0).wait_recv()
        for j in range(3):
            waiter(fs.at[j], fr.at[j]).wait_recv()
        for rel in range(4):
            waiter(ssem.at[rel], rsem0).wait_send()
        for j in range(3):
            waiter(fs.at[j], fr.at[j]).wait_send()

    return pl.pallas_call(
        body, name="gather_ffn2_wait", out_shape=_hbm_like(w2),
        in_specs=(HBM_SPEC, HBM_SPEC, SEM_SPEC, SEM_SPEC, SEM_SPEC, SEM_SPEC, ANY_SPEC), out_specs=HBM_SPEC,
        input_output_aliases={1: 0},
        compiler_params=pltpu.CompilerParams(has_side_effects=SPLIT_EFFECT),
    )(shard, w2, ssem_f, rsem_f0, fsend, frecv, after)


class _GatheredWeights(_LocalWeights):
    def __init__(self, shard, x, gain1):
        w1, self.h1, wi, wo, w2 = _all_gather_ffn1(shard, x, gain1)
        (self.ssem_m, self.rsem_m0, self.rsem_m, self.ssem_f, self.rsem_f0, self.rsem_f,
         self.shard, self.wi, self.wo, self.w2_part, self.w1) = _gather_rest_start(shard, wi, wo, w2, w1)

    def first_norm(self, x, gain):
        return self.h1

    def after_ffn1(self, gain, x1):
        self.fsend_m, self.frecv_m, self.wi, self.wo, gain = _gather_mix_pass_on(self.rsem_m, self.wi, self.wo, gain, x1)
        return gain

    def mix(self, after):
        self.shard, wint, wout = _gather_mix_wait(self.ssem_m, self.rsem_m0, self.fsend_m, self.frecv_m, self.shard,
                                                  self.wi, self.wo, after)
        return wint, wout

    def before_out_proj(self, wout, after):
        self.fsend, self.frecv, self.w2_part, wout = _gather_ffn2_pass_on(self.rsem_f, self.w2_part, wout, after)
        return wout

    def ffn2(self, after):
        return _gather_ffn2_wait(self.ssem_f, self.rsem_f0, self.fsend, self.frecv, self.shard, self.w2_part, after)

    def out_ffn2_grads_ready(self, dwout, dw2, after):
        rx1 = lax.empty((4, RSA_ROWS, D), bf16)
        sa, ra, sent, rx1, after = _rsa_level1_start(dict(wo=dwout, w2=dw2), rx1, after, "rsa_level1_start_out_ffn2")
        self.level1 = ((sa, ra), sent, rx1)
        return after

    def before_ffn1_bwd(self, dwint, dx1b):
        early, sent, rx1 = self.level1
        sa, ra, late, rx1, dx1b = _rsa_level1_start(dict(wi=dwint), rx1, dx1b, "rsa_level1_start_in")
        started = (((MIX_PIECES[1],) + F2_PIECES, *early), ((MIX_PIECES[0],), sa, ra))
        rx2 = lax.empty((3, RSA_ROWS, D), bf16)
        self.sb, self.rb, self.tx, self.acc, self.rx2, dx1b = _rsa_sums_and_send(
            started, late["wi"], sent["wo"], sent["w2"], rx1, rx2, dx1b)
        return dx1b

    def mix_ffn2_grads_parts(self, after):
        rx2 = _rsa_level2_wait(self.sb, self.rb, self.tx, self.rx2, after)
        return self.acc, rx2


def _reduce_scatter_ffn1_head(dw1, small_packed, first_norm):
    pieces = G1_PIECES
    half = FS // 2
    hrows = len(pieces) * half
    nrows = 2 * hrows
    X_RELAY, Y_RELAY = range(2)

    T = first_norm[0].shape[0]

    def body(d1_ref, p_ref, dh_hbm, x_hbm, gain_ref, dr_hbm,
             forx_ref, fory_ref, own_ref, rx1_ref, relx_ref, rely_ref, gx_hbm, tot_ref,
             own_buf, rx_buf, tx1, tx2, tx3, acc, sa, ra, sb, rb, lsem, pair, chips, small_tot, small_send, small_recv,
             xbuf, rbuf, hbuf, obuf, norm_in_sems, norm_out_sems):
        x, y, c = _position()
        me, sib = (x, y, c), (x, y, 1 - c)
        xn, yn = (1 - x, y, c), (x, 1 - y, c)
        rel_chips = [(x, y), (1 - x, y), (x, 1 - y), (1 - x, 1 - y)]
        srcs = _weight_pieces(w1_ref=d1_ref)

        my_chip = 2 * x + y
        pair[c] = p_ref[...]
        swap = pltpu.make_async_remote_copy(
            src_ref=pair.at[c], dst_ref=pair.at[c], send_sem=small_send.at[0], recv_sem=small_recv.at[0],
            device_id=sib, device_id_type=MESH)
        mine = pl.ds(pl.multiple_of(c * (SMALL_ROWS // 2), 8), SMALL_ROWS // 2)
        small = [pltpu.make_async_remote_copy(
            src_ref=chips.at[my_chip, mine, :], dst_ref=chips.at[my_chip, mine, :], send_sem=small_send.at[j],
            recv_sem=small_recv.at[j], device_id=(*rel_chips[j], c), device_id_type=MESH) for j in (1, 2, 3)]
        give = pltpu.make_async_remote_copy(
            src_ref=small_tot.at[mine, :], dst_ref=small_tot.at[mine, :], send_sem=small_send.at[4],
            recv_sem=small_recv.at[4], device_id=sib, device_id_type=MESH)

        def part(k, dev, hf):
            r = PIECE_ROWS[k]
            return srcs[k].at[pl.ds(pl.multiple_of(_dev_index(*dev) * r + hf * half, 16), half), :]

        def slot(ref, k, hf):
            return ref.at[pl.ds(hf * hrows + k * half, half), :]

        halves = [(k, hf) for hf in (0, 1) for k in pieces]

        for j in (3, 1, 2, 0):
            for k, hf in halves:
                pltpu.make_async_remote_copy(
                    src_ref=part(k, (*rel_chips[j], 1 - c), hf), dst_ref=slot(rx1_ref.at[j], k, hf),
                    send_sem=sa.at[j], recv_sem=ra.at[j], device_id=sib, device_id_type=MESH).start()

        def wait_a(j):
            return pltpu.make_async_remote_copy(src_ref=rx1_ref.at[j], dst_ref=rx1_ref.at[j], send_sem=sa.at[j],
                                                recv_sem=ra.at[j], device_id=me, device_id_type=MESH)

        def ici(rel, src, dst, to):
            return pltpu.make_async_remote_copy(src_ref=src, dst_ref=dst, send_sem=sb.at[rel], recv_sem=rb.at[rel],
                                                device_id=to, device_id_type=MESH)

        first, second = pl.ds(0, hrows), pl.ds(hrows, hrows)
        sends = {
            X_RELAY: ici(X_RELAY, tx3.at[first, :], relx_ref, xn),
            Y_RELAY: ici(Y_RELAY, tx3.at[second, :], rely_ref, yn),
        }

        def chip_sum(j, dst):
            loads = [pltpu.make_async_copy(part(k, (*rel_chips[j], c), hf), slot(own_buf, k, hf), lsem.at[0])
                     for k, hf in halves]
            for cp in loads:
                cp.start()
            wait_a(j).wait_recv()
            got = pltpu.make_async_copy(rx1_ref.at[j], rx_buf, lsem.at[1])
            got.start()
            pltpu.make_async_copy(rx_buf, rx_buf, lsem.at[0]).wait()
            got.wait()

            def add(i, carry):
                rows = pl.ds(pl.multiple_of(i * half, 16), half)
                tot = own_buf[rows, :].astype(f32) + rx_buf[rows, :].astype(f32)
                dst[rows, :] = tot.astype(dst.dtype)
                return carry

            lax.fori_loop(0, nrows // half, add, 0)

        def add_landed(landed, dst, rows0, nrows_):
            got = pltpu.make_async_copy(landed, rx_buf.at[pl.ds(0, nrows_), :], lsem.at[1])
            got.start()
            got.wait()

            def add(i, carry):
                src_rows = pl.ds(pl.multiple_of(i * half, 16), half)
                dst_rows = pl.ds(pl.multiple_of(rows0 + i * half, 16), half)
                dst[dst_rows, :] = (dst[dst_rows, :].astype(f32) + rx_buf[src_rows, :].astype(f32)).astype(dst.dtype)
                return carry

            lax.fori_loop(0, nrows_ // half, add, 0)

        chip_sum(3, tx3)
        sends[X_RELAY].start()
        sends[Y_RELAY].start()
        dg = _norm_bwd_rows(T, dh_hbm, x_hbm, gain_ref, dr_hbm, gx_hbm.at[0], None,
                            xbuf, rbuf, hbuf, obuf, None, norm_in_sems, norm_out_sems)
        r0 = SMALL_OFF["ffn1_norm"]
        for k in range(D // 128):
            pair[c, r0 + k:r0 + k + 1, :] = dg[:, 128 * k:128 * (k + 1)]
        swap.start()
        swap.wait_recv()
        chips[my_chip] = pair[0] + pair[1]
        for cp in small:
            cp.start()
        chip_sum(1, tx1)
        chip_sum(2, tx2)
        chip_sum(0, acc)
        own_out = pltpu.make_async_copy(acc, own_ref, lsem.at[0])
        own_out.start()
        sends[X_RELAY].wait_recv()
        add_landed(relx_ref, tx2, 0, hrows)
        sends[Y_RELAY].wait_recv()
        add_landed(rely_ref, tx1, hrows, hrows)
        own_out.wait()
        outs = [pltpu.make_async_copy(tx1, forx_ref, lsem.at[0]), pltpu.make_async_copy(tx2, fory_ref, lsem.at[1])]
        for cp in outs:
            cp.start()
        for cp in outs:
            cp.wait()
        for cp in small:
            cp.wait_recv()
        small_tot[mine, :] = (chips[0, mine, :] + chips[1, mine, :]) + (chips[2, mine, :] + chips[3, mine, :])
        give.start()
        give.wait_recv()
        tot = small_tot[...]
        tot_ref[...] = tot
        loss = jnp.sum(tot[LOSS_ROW:LOSS_ROW + 1, :], axis=-1, keepdims=True)
        tot_ref[LOSS_ROW:LOSS_ROW + 1, :] = jnp.broadcast_to(loss, (1, 128))
        for j in range(4):
            wait_a(j).wait_send()
        for cp in sends.values():
            cp.wait_send()
        swap.wait_send()
        for cp in small + [give]:
            cp.wait_send()

    hbm = pl.BlockSpec(memory_space=pl.ANY)
    vm = pl.BlockSpec(memory_space=pltpu.VMEM)
    outs = pl.pallas_call(
        body, in_specs=[hbm, vm, hbm, hbm, vm, hbm], out_specs=[hbm] * 7 + [vm],
        out_shape=(SDS((nrows, D), bf16), SDS((nrows, D), bf16), SDS((nrows, D), f32), SDS((4, nrows, D), bf16),
                   SDS((hrows, D), bf16), SDS((hrows, D), bf16), SDS((1, T, D), f32), SDS((SMALL_ROWS, 128), f32)),
        scratch_shapes=[pltpu.VMEM((nrows, D), bf16), pltpu.VMEM((nrows, D), bf16),
                        pltpu.VMEM((nrows, D), bf16), pltpu.VMEM((nrows, D), bf16), pltpu.VMEM((nrows, D), bf16),
                        pltpu.VMEM((nrows, D), f32),
                        pltpu.SemaphoreType.DMA((4,)), pltpu.SemaphoreType.DMA((4,)),
                        pltpu.SemaphoreType.DMA((2,)), pltpu.SemaphoreType.DMA((2,)), pltpu.SemaphoreType.DMA((2,)),
                        pltpu.VMEM((2, SMALL_ROWS, 128), f32), pltpu.VMEM((4, SMALL_ROWS, 128), f32),
                        pltpu.VMEM((SMALL_ROWS, 128), f32),
                        pltpu.SemaphoreType.DMA((5,)), pltpu.SemaphoreType.DMA((5,))]
        + [sc for sc in _norm_bwd_scratch(True, False) if sc is not None],
        compiler_params=pltpu.CompilerParams(has_side_effects=True, vmem_limit_bytes=VMEM_LIMIT_V7X),
        name="reduce_scatter_ffn1_head")(dw1, small_packed, *first_norm)
    return outs[0], outs[1], outs[2], outs[-1], outs[-2]


def _rs1_tail_start(for_x, for_y, from_x, from_y, *thru):
    def body(fx_ref, fy_ref, lx_ref, ly_ref, *rest):
        ssem, rsem = rest[len(thru):len(thru) + 2]
        x, y, c = _position()
        pltpu.make_async_remote_copy(src_ref=fx_ref, dst_ref=lx_ref, send_sem=ssem.at[0], recv_sem=rsem.at[0],
                                     device_id=(1 - x, y, c), device_id_type=MESH).start()
        pltpu.make_async_remote_copy(src_ref=fy_ref, dst_ref=ly_ref, send_sem=ssem.at[1], recv_sem=rsem.at[1],
                                     device_id=(x, 1 - y, c), device_id_type=MESH).start()

    dma = pltpu.SemaphoreType.DMA
    arrs = (for_x, for_y, from_x, from_y, *thru)
    return pl.pallas_call(
        body, name="rs1_tail_start", out_shape=(dma((2,)), dma((2,))) + tuple(_hbm_like(a) for a in arrs),
        in_specs=(HBM_SPEC,) * len(arrs), out_specs=(SEM_SPEC,) * 2 + (HBM_SPEC,) * len(arrs),
        input_output_aliases={i: i + 2 for i in range(len(arrs))},
        compiler_params=pltpu.CompilerParams(has_side_effects=SPLIT_EFFECT),
    )(*[_in_hbm(a) for a in arrs])


RSA_PIECES = MIX_PIECES + F2_PIECES
RSA_ROWS = _group_rows(RSA_PIECES)
RSA_OFF = {k: PIECE_OFF[k] - PIECE_OFF[RSA_PIECES[0]] for k in RSA_PIECES}
RSA_BLOCK = 192


def _rsa_rows(ref, k):
    return ref.at[pl.ds(RSA_OFF[k], PIECE_ROWS[k]), :]


def _rsa_level1_start(grads, rx1, thru, name):
    keys = sorted(grads)
    n = len(keys)

    def body(*refs):
        srcs = _weight_pieces(**{k + "_ref": ref for k, ref in zip(keys, refs[:n])})
        rx1_ref, sa, ra = refs[n], refs[n + 2], refs[n + 3]
        x, y, c = _position()
        for j, chip in enumerate([(x, y), (1 - x, y), (x, 1 - y), (1 - x, 1 - y)]):
            for k in sorted(srcs):
                pltpu.make_async_remote_copy(
                    src_ref=_block_rows(srcs, k, (*chip, 1 - c)), dst_ref=_rsa_rows(rx1_ref.at[j], k),
                    send_sem=sa.at[j], recv_sem=ra.at[j], device_id=(x, y, 1 - c), device_id_type=MESH).start()

    dma = pltpu.SemaphoreType.DMA
    arrs = tuple(grads[k] for k in keys) + (rx1, thru)
    outs = pl.pallas_call(
        body, name=name, out_shape=(dma((4,)), dma((4,))) + tuple(_hbm_like(a) for a in arrs),
        in_specs=(HBM_SPEC,) * len(arrs), out_specs=(SEM_SPEC,) * 2 + (HBM_SPEC,) * len(arrs),
        input_output_aliases={i: i + 2 for i in range(len(arrs))},
        compiler_params=pltpu.CompilerParams(has_side_effects=SPLIT_EFFECT),
    )(*[_in_hbm(a) for a in arrs])
    return outs[0], outs[1], dict(zip(keys, outs[2:2 + n])), outs[2 + n], outs[3 + n]


def _rsa_sums_and_send(started, dwint, dwout, dw2, rx1, rx2, thru):
    nblk = RSA_ROWS // RSA_BLOCK
    nstart = len(started)

    def body(*refs):
        di_ref, do_ref, d2_ref, rx1_ref, rx2_ref = refs[:5]
        l1_sems = refs[6:6 + 2 * nstart]
        sb, rb, tx_ref, acc_ref = refs[6 + 2 * nstart:10 + 2 * nstart]
        own_buf, rx_buf, tx_buf, acc_buf, in_sems, out_sems = refs[13 + 2 * nstart:]
        x, y, c = _position()
        srcs = _weight_pieces(wi_ref=di_ref, wo_ref=do_ref, w2_ref=d2_ref)
        chips = [(x, y), (1 - x, y), (x, 1 - y), (1 - x, 1 - y)]

        for g, (pieces, _, _) in enumerate(started):
            ssem, rsem = l1_sems[2 * g], l1_sems[2 * g + 1]
            for j in range(4):
                rows = rx1_ref.at[j, pl.ds(RSA_OFF[pieces[0]], _group_rows(pieces)), :]
                d = pltpu.make_async_remote_copy(src_ref=rows, dst_ref=rows, send_sem=ssem.at[j], recv_sem=rsem.at[j],
                                                 device_id=(x, y, c), device_id_type=MESH)
                d.wait_recv()
                d.wait_send()

        def start_loads(j):
            s = j % 2
            for k in RSA_PIECES:
                pltpu.make_async_copy(_block_rows(srcs, k, (*chips[j], c)), _rsa_rows(own_buf.at[s], k),
                                      in_sems.at[2 * s]).start()
            pltpu.make_async_copy(rx1_ref.at[j], rx_buf.at[s], in_sems.at[2 * s + 1]).start()

        def wait_loads(j):
            s = j % 2
            pltpu.make_async_copy(rx1_ref.at[j], own_buf.at[s], in_sems.at[2 * s]).wait()
            pltpu.make_async_copy(rx1_ref.at[j], rx_buf.at[s], in_sems.at[2 * s + 1]).wait()

        def store(j):
            if j == 0:
                return pltpu.make_async_copy(acc_buf, acc_ref, out_sems.at[2])
            return pltpu.make_async_copy(tx_buf.at[j % 2], tx_ref.at[j - 1], out_sems.at[j % 2])

        def send(j):
            return pltpu.make_async_remote_copy(src_ref=tx_ref.at[j - 1], dst_ref=rx2_ref.at[j - 1], send_sem=sb.at[j - 1],
                                                recv_sem=rb.at[j - 1], device_id=(*chips[j], c), device_id_type=MESH)

        start_loads(0)
        for j in range(4):
            s = j % 2
            if j + 1 < 4:
                start_loads(j + 1)
            wait_loads(j)
            if j == 3:
                store(1).wait()
                send(1).start()

            def add(i, carry, j=j, s=s):
                rows = pl.ds(pl.multiple_of(i * RSA_BLOCK, 16), RSA_BLOCK)
                tot = own_buf[s, rows, :].astype(f32) + rx_buf[s, rows, :].astype(f32)
                if j == 0:
                    acc_buf[rows, :] = tot
                else:
                    tx_buf[s, rows, :] = tot.astype(bf16)
                return carry

            lax.fori_loop(0, nblk, add, 0)
            store(j).start()
        store(0).wait()
        for j in (2, 3):
            store(j).wait()
            send(j).start()

    dma = pltpu.SemaphoreType.DMA
    passed = (rx1, rx2, thru)
    outs = pl.pallas_call(
        body, name="rsa_sums_and_send",
        in_specs=(HBM_SPEC,) * 6 + (SEM_SPEC,) * (2 * nstart),
        out_specs=(SEM_SPEC,) * 2 + (HBM_SPEC,) * 5,
        out_shape=(dma((3,)), dma((3,)), pltpu.HBM((3, RSA_ROWS, D), bf16), pltpu.HBM((RSA_ROWS, D), f32))
        + tuple(_hbm_like(a) for a in passed),
        input_output_aliases={3: 4, 4: 5, 5: 6},
        scratch_shapes=[pltpu.VMEM((2, RSA_ROWS, D), bf16), pltpu.VMEM((2, RSA_ROWS, D), bf16),
                        pltpu.VMEM((2, RSA_ROWS, D), bf16), pltpu.VMEM((RSA_ROWS, D), f32),
                        dma((4,)), dma((3,))],
        compiler_params=pltpu.CompilerParams(has_side_effects=SPLIT_EFFECT, vmem_limit_bytes=VMEM_LIMIT_V7X),
    )(*[_in_hbm(a) for a in (dwint, dwout, dw2) + passed], *[sem for _, sa, ra in started for sem in (sa, ra)])
    sb, rb, tx, acc, _, rx2, thru = outs
    return sb, rb, tx, acc, rx2, thru


def _rsa_level2_wait(sb, rb, tx, rx2, after):
    def body(tx_ref, rx2_ref, sb_ref, rb_ref, after_ref, rx2_o):
        x, y, c = _position()
        for j in range(3):
            d = pltpu.make_async_remote_copy(src_ref=tx_ref.at[j], dst_ref=rx2_ref.at[j], send_sem=sb_ref.at[j],
                                             recv_sem=rb_ref.at[j], device_id=(x, y, c), device_id_type=MESH)
            d.wait_recv()
            d.wait_send()

    return pl.pallas_call(
        body, name="rsa_level2_wait", out_shape=_hbm_like(rx2),
        in_specs=(HBM_SPEC, HBM_SPEC, SEM_SPEC, SEM_SPEC, ANY_SPEC), out_specs=HBM_SPEC,
        input_output_aliases={1: 0},
        compiler_params=pltpu.CompilerParams(has_side_effects=SPLIT_EFFECT),
    )(tx, rx2, sb, rb, after)


def _adamw_math(w, g, m, v):
    m = ADAM_B1 * m + (1.0 - ADAM_B1) * g
    v = ADAM_B2 * v + (1.0 - ADAM_B2) * (g * g)
    m_hat = m / (1.0 - ADAM_B1 ** ADAM_STEP)
    v_hat = v / (1.0 - ADAM_B2 ** ADAM_STEP)
    delta = -ADAM_LR * (m_hat / (jnp.sqrt(v_hat) + ADAM_EPS) + ADAM_WD * w)
    return delta, m, v


def _adamw_big(pieces, ws, ms, vs, own, landed, name, in_flight=None):
    npiece = len(pieces)
    nsent = len(landed) if in_flight else 0
    nland = sum(a.shape[0] if a.ndim == 3 else 1 for a in landed)
    rmax = max(PIECE_ROWS[k] for k in pieces)
    half = FS // 2

    def segments(k):
        if k in G1_PIECES:
            return [(hf * len(G1_PIECES) * half + k * half, hf * half, half) for hf in (0, 1)]
        return [(RSA_OFF[k], 0, PIECE_ROWS[k])]

    def body(*refs):
        ins = (refs[0:npiece], refs[npiece:2 * npiece], refs[2 * npiece:3 * npiece])
        own_ref = refs[3 * npiece]
        nin = 3 * npiece + 1 + len(landed)
        land_refs = []
        for ref, a in zip(refs[3 * npiece + 1:nin], landed):
            land_refs += [ref.at[j] for j in range(a.shape[0])] if a.ndim == 3 else [ref]
        if in_flight:
            sent_refs, (ssem, rsem) = refs[nin:nin + nsent], refs[nin + nsent:nin + nsent + 2]
            nin += nsent + 3
        out_refs = refs[nin:nin + 4 * npiece]
        inb, landb, outb, in_sems, land_sems, out_sems = refs[nin + 4 * npiece + nsent:]

        def loads(i):
            s, k = i % 2, pieces[i]
            r = PIECE_ROWS[k]
            cps = [pltpu.make_async_copy(ins[q][i].at[0], inb.at[s, q, pl.ds(0, r), :], in_sems.at[4 * s + q])
                   for q in range(3)]
            waits, late = list(cps), []
            for src0, dst0, n in segments(k):
                cps.append(pltpu.make_async_copy(own_ref.at[pl.ds(src0, n), :], inb.at[s, 3, pl.ds(dst0, n), :],
                                                 in_sems.at[4 * s + 3]))
                for p in range(nland):
                    late.append(pltpu.make_async_copy(land_refs[p].at[pl.ds(src0, n), :],
                                                      landb.at[s, p, pl.ds(dst0, n), :], land_sems.at[nland * s + p]))
            own_rows = inb.at[s, 3, pl.ds(0, r), :]
            waits.append(pltpu.make_async_copy(own_rows, own_rows, in_sems.at[4 * s + 3]))
            for p in range(nland):
                rows = landb.at[s, p, pl.ds(0, r), :]
                waits.append(pltpu.make_async_copy(rows, rows, land_sems.at[nland * s + p]))
            return cps, late, waits

        def stores(i):
            s, r = i % 2, PIECE_ROWS[pieces[i]]
            return [pltpu.make_async_copy(outb.at[s, q, pl.ds(0, r), :], out_refs[q * npiece + i].at[0],
                                          out_sems.at[4 * s + q]) for q in range(4)]

        ahead = min(2, npiece) if in_flight else 1
        for i in range(ahead):
            for cp in loads(i)[0]:
                cp.start()
        if in_flight:
            x, y, c = _position()
            for j in range(nsent):
                d = pltpu.make_async_remote_copy(src_ref=sent_refs[j], dst_ref=refs[3 * npiece + 1 + j],
                                                 send_sem=ssem.at[j], recv_sem=rsem.at[j], device_id=(x, y, c),
                                                 device_id_type=MESH)
                d.wait_recv()
                d.wait_send()
        for cp in loads(0)[1]:
            cp.start()
        for i in range(npiece):
            s, r = i % 2, PIECE_ROWS[pieces[i]]
            if i + 1 < npiece:
                first, late, _ = loads(i + 1)
                for cp in late if i + 1 < ahead else first + late:
                    cp.start()
            for cp in loads(i)[2]:
                cp.wait()
            if i >= 2:
                for cp in stores(i - 2):
                    cp.wait()
            g = inb[s, 3, 0:r, :]
            for p in range(nland):
                g = g + landb[s, p, 0:r, :].astype(f32)
            d, nm, nv = _adamw_math(inb[s, 0, 0:r, :], g, inb[s, 1, 0:r, :], inb[s, 2, 0:r, :])
            outb[s, 0, 0:r, :] = g
            outb[s, 1, 0:r, :] = d
            outb[s, 2, 0:r, :] = nm
            outb[s, 3, 0:r, :] = nv
            for cp in stores(i):
                cp.start()
        for i in range(max(npiece - 2, 0), npiece):
            for cp in stores(i):
                cp.wait()

    hbm = pl.BlockSpec(memory_space=pl.ANY)
    in_specs, out_specs = [hbm] * (3 * npiece + 1), [hbm] * (4 * npiece)
    out_shape = [SDS(w.shape, f32) for _ in range(4) for w in ws]
    args, aliases, effect = [*ws, *ms, *vs, own], {}, False
    if in_flight:
        ssem, rsem, sent, after = in_flight
        in_specs += [HBM_SPEC] * (2 * nsent) + [SEM_SPEC, SEM_SPEC, hbm]
        args += [_in_hbm(a) for a in (*landed, *sent)] + [ssem, rsem, after]
        out_specs += [HBM_SPEC] * nsent
        out_shape += [_hbm_like(a) for a in landed]
        aliases = {3 * npiece + 1 + j: 4 * npiece + j for j in range(nsent)}
        effect = SPLIT_EFFECT
    else:
        in_specs += [hbm] * len(landed)
        args += list(landed)
    outs = pl.pallas_call(
        body, in_specs=in_specs, out_specs=out_specs, out_shape=tuple(out_shape), input_output_aliases=aliases,
        scratch_shapes=[pltpu.VMEM((2, 4, rmax, D), f32), pltpu.VMEM((2, nland, rmax, D), bf16),
                        pltpu.VMEM((2, 4, rmax, D), f32),
                        pltpu.SemaphoreType.DMA((8,)), pltpu.SemaphoreType.DMA((2 * nland,)),
                        pltpu.SemaphoreType.DMA((8,))],
        compiler_params=pltpu.CompilerParams(has_side_effects=effect, vmem_limit_bytes=VMEM_LIMIT_V7X),
        name=name)(*args)
    return [list(outs[q * npiece:(q + 1) * npiece]) for q in range(4)]


def _adamw_small(ws, ms, vs, gs, name):
    n = len(ws)

    def body(*refs):
        w_refs, m_refs, v_refs, g_refs = refs[0:n], refs[n:2 * n], refs[2 * n:3 * n], refs[3 * n:4 * n]
        outs = refs[4 * n:]
        for i in range(n):
            d, nm, nv = _adamw_math(w_refs[i][...], g_refs[i][...], m_refs[i][...], v_refs[i][...])
            outs[i][...] = d
            outs[n + i][...] = nm
            outs[2 * n + i][...] = nv

    outs = pl.pallas_call(
        body, out_shape=tuple(SDS(w.shape, f32) for _ in range(3) for w in ws), name=name)(*ws, *ms, *vs, *gs)
    return [list(outs[q * n:(q + 1) * n]) for q in range(3)]


WEIGHTS = ("ffn1_norm", "ffn1_w_gate", "ffn1_w_up", "ffn1_w_down", "mix_norm", "w_in", "q_norm", "k_norm",
           "attn_sinks", "rel_bias", "pool_w", "pool_scale", "w_out", "ffn2_norm", "ffn2_w_gate", "ffn2_w_up",
           "ffn2_w_down")
BIG = (("ffn1_w_gate", True), ("ffn1_w_up", True), ("ffn1_w_down", False), ("w_in", True), ("w_out", False),
       ("ffn2_w_gate", True), ("ffn2_w_up", True), ("ffn2_w_down", False))


def kernel(x, ffn1_norm, ffn1_w_gate, ffn1_w_up, ffn1_w_down, mix_norm, w_in, q_norm, k_norm, attn_sinks, rel_bias, pool_w, pool_scale, w_out, ffn2_norm, ffn2_w_gate, ffn2_w_up, ffn2_w_down, loss_target, m_ffn1_norm, m_ffn1_w_gate, m_ffn1_w_up, m_ffn1_w_down, m_mix_norm, m_w_in, m_q_norm, m_k_norm, m_attn_sinks, m_rel_bias, m_pool_w, m_pool_scale, m_w_out, m_ffn2_norm, m_ffn2_w_gate, m_ffn2_w_up, m_ffn2_w_down, v_ffn1_norm, v_ffn1_w_gate, v_ffn1_w_up, v_ffn1_w_down, v_mix_norm, v_w_in, v_q_norm, v_k_norm, v_attn_sinks, v_rel_bias, v_pool_w, v_pool_scale, v_w_out, v_ffn2_norm, v_ffn2_w_gate, v_ffn2_w_up, v_ffn2_w_down):
    args = dict(locals())
    w = {n: args[n] for n in WEIGHTS}
    m = {n: args["m_" + n] for n in WEIGHTS}
    v = {n: args["v_" + n] for n in WEIGHTS}

    as_rows = lambda a, tr: jnp.swapaxes(a, 1, 2) if tr else a
    shard = jnp.concatenate([as_rows(w[n], tr)[0].astype(bf16) for n, tr in BIG], axis=0)
    exchanges = _GatheredWeights(shard, x[0], ffn1_norm)
    (dh1, dx1), (dw1, _, _, _), small = _local_step(
        x[0], loss_target[0], exchanges, ffn1_norm, mix_norm, ffn2_norm, q_norm, k_norm, attn_sinks,
        rel_bias, pool_w[0], pool_scale)

    nrows1 = len(G1_PIECES) * FS
    for_x, for_y, own1, small_tot, gx = _reduce_scatter_ffn1_head(dw1, _pack_small(small),
                                                                  (dh1, x[0], ffn1_norm, dx1))
    ssem, rsem, for_x, for_y, from_x, from_y, small_tot, gx = _rs1_tail_start(
        for_x, for_y, lax.empty((nrows1, D), bf16), lax.empty((nrows1, D), bf16), small_tot, gx)
    gx = _fresh_copy(gx, "grad_x_out")
    own_rest, landed_rest = exchanges.mix_ffn2_grads_parts(gx)

    grads, deltas, new_m, new_v = {}, {}, {}, {}
    rest = [k for k in range(len(BIG)) if k not in G1_PIECES]
    rows_of = lambda t, ks: [as_rows(t[BIG[k][0]], BIG[k][1]) for k in ks]
    rest_out = _adamw_big(rest, rows_of(w, rest), rows_of(m, rest), rows_of(v, rest), own_rest, [landed_rest],
                          "adamw_rest")
    ffn1 = list(G1_PIECES)
    ffn1_out = _adamw_big(ffn1, rows_of(w, ffn1), rows_of(m, ffn1), rows_of(v, ffn1), own1, [from_x, from_y],
                          "adamw_ffn1", in_flight=(ssem, rsem, [for_x, for_y], rest_out[0][0]))
    for ks, out in ((rest, rest_out), (ffn1, ffn1_out)):
        for i, k in enumerate(ks):
            n, tr = BIG[k]
            grads[n], deltas[n], new_m[n], new_v[n] = [as_rows(o[i], tr) for o in out]
    small_names = [n for n in SMALL_NAMES if n != "loss"]
    for n in small_names:
        grads[n] = _unpack_small(small_tot, n)
    ds, nms, nvs = _adamw_small([w[n] for n in small_names], [m[n] for n in small_names], [v[n] for n in small_names],
                                [grads[n] for n in small_names], "adamw_small")
    for i, n in enumerate(small_names):
        deltas[n], new_m[n], new_v[n] = ds[i], nms[i], nvs[i]
    loss = small_tot[LOSS_ROW, 0]
    return (loss, gx, *[grads[n] for n in WEIGHTS], *[deltas[n] for n in WEIGHTS],
            *[new_m[n] for n in WEIGHTS], *[new_v[n] for n in WEIGHTS])
```

```python
import jax
import jax.numpy as jnp
import numpy as np
from jax import lax
from jax.experimental import pallas as pl
from jax.experimental.pallas import tpu as pltpu

f32, bf16, i32 = jnp.float32, jnp.bfloat16, jnp.int32
SDS = jax.ShapeDtypeStruct

D = 1024
F = 2816
HD = 64
NH = 8
NKV = 2
GQA = NH // NKV
DATTN = NH * HD
DKV = NKV * HD
DPOOL = 512
POOL_WINDOWS = (2, 4, 8, 16)
PGD = DPOOL // len(POOL_WINDOWS)
DIN = DATTN + 2 * DKV + DPOOL
DMIX = DATTN + DPOOL
BLK = 128
NBUCK = 32
MAX_DISTANCE = 128
EPS = 1e-6
NEG = -1e30
SCALE = HD ** -0.5

ADAM_LR, ADAM_B1, ADAM_B2, ADAM_EPS, ADAM_WD, ADAM_STEP = 0.001, 0.9, 0.999, 1e-08, 0.01, 10

NDEV = 8
FS = F // NDEV
INS = DIN // NDEV
OUTS = DMIX // NDEV
PIECE_ROWS = (FS, FS, FS, INS, OUTS, FS, FS, FS)
PIECE_OFF = tuple(int(v) for v in np.cumsum((0,) + PIECE_ROWS[:-1]))
PACK_ROWS = sum(PIECE_ROWS)

VMEM_LIMIT_V7X = 56 * 1024 * 1024

MESH = pl.DeviceIdType.MESH


def _cparams(sem=None, vmem=None):
    return pltpu.CompilerParams(dimension_semantics=sem, vmem_limit_bytes=vmem)


def _nt(a, b):
    return lax.dot_general(a, b, (((1,), (1,)), ((), ())), preferred_element_type=f32)


def _tn(a, b):
    return lax.dot_general(a, b, (((0,), (0,)), ((), ())), preferred_element_type=f32)


def _nn(a, b):
    return jnp.dot(a, b, preferred_element_type=f32)


def _sigmoid(x):
    return 1.0 / (1.0 + jnp.exp(-x))


def _fresh_copy(a, name):
    T = a.shape[1]
    tm = min(512, T)

    def body(a_ref, o_ref):
        o_ref[...] = a_ref[...]

    tok = pl.BlockSpec((1, tm, D), lambda i: (0, i, 0))
    return pl.pallas_call(body, grid=(T // tm,), in_specs=[tok], out_specs=tok, out_shape=SDS(a.shape, a.dtype),
                          name=name)(a)


def _norm_fwd(x, g, name):
    T = x.shape[0]
    tm = min(512, T)

    def body(x_ref, g_ref, h_ref):
        xv = x_ref[...]
        r = lax.rsqrt(jnp.mean(xv * xv, axis=-1, keepdims=True) + EPS)
        h_ref[...] = (xv * r * g_ref[...]).astype(bf16)

    return pl.pallas_call(
        body, grid=(T // tm,),
        in_specs=[pl.BlockSpec((tm, D), lambda i: (i, 0)), pl.BlockSpec((1, D), lambda i: (0, 0))],
        out_specs=pl.BlockSpec((tm, D), lambda i: (i, 0)),
        out_shape=SDS((T, D), bf16), name=name)(x, g)


FFN_ROW_CHUNK = 256


def _ffn_tiles(T):
    return min(1024, T), 256


def _ffn_fwd(h, w, x, target, next_gain, name):
    T = h.shape[0]
    tm, tf = _ffn_tiles(T)
    nf = F // tf
    with_loss = target is not None
    assert with_loss != (next_gain is not None)

    def body(*refs):
        if with_loss:
            h_ref, w_ref, x_hbm, t_hbm, xo_ref, g_ref, u_ref, dyb_ref, loss_ref, tbuf, sem = refs
        else:
            h_ref, w_ref, x_hbm, gain_ref, xo_ref, g_ref, u_ref, hn_ref, sem = refs
        fi = pl.program_id(0)

        def x_rows(r):
            return pltpu.make_async_copy(x_hbm.at[pl.ds(r, tm), :], xo_ref.at[pl.ds(r, tm), :], sem.at[r // tm])

        @pl.when(fi == 0)
        def _():
            for r in range(0, T, tm):
                x_rows(r).start()

        wgu = w_ref[0:2].reshape(2 * tf, D)
        for r in range(0, T, tm):
            rows = slice(r, r + tm)
            gu = _nt(h_ref[rows, :], wgu)
            gate, up = gu[:, :tf], gu[:, tf:]
            act = gate * _sigmoid(gate) * up
            g_ref[0, rows, :] = gate.astype(bf16)
            u_ref[0, rows, :] = up.astype(bf16)
            down = _nn((0.5 * act).astype(bf16), w_ref[2])

            @pl.when(fi == 0)
            def _():
                x_rows(r).wait()

            xo_ref[rows, :] += down

        if with_loss:
            @pl.when(fi == nf - 1)
            def _():
                lanes = jnp.zeros((1, 128), f32)
                for r in range(0, T, tm):
                    rows = slice(r, r + tm)
                    cp = pltpu.make_async_copy(t_hbm.at[pl.ds(r, tm), :], tbuf, sem.at[0])
                    cp.start()
                    cp.wait()
                    e = xo_ref[rows, :] - tbuf[...]
                    dy = e * (1.0 / D)
                    xo_ref[rows, :] = dy
                    dyb_ref[rows, :] = (0.5 * dy).astype(bf16)
                    col = jnp.sum(e * e, axis=0, keepdims=True) * (0.5 / D)
                    for k in range(D // 128):
                        lanes = lanes + col[:, 128 * k:128 * (k + 1)]
                loss_ref[...] = lanes
        else:
            @pl.when(fi == nf - 1)
            def _():
                for r in range(0, T, FFN_ROW_CHUNK):
                    rows = slice(r, r + FFN_ROW_CHUNK)
                    xv = xo_ref[rows, :]
                    rstd = lax.rsqrt(jnp.mean(xv * xv, axis=-1, keepdims=True) + EPS)
                    hn_ref[rows, :] = (xv * rstd * gain_ref[...]).astype(bf16)

    tok = pl.BlockSpec((T, D), lambda f: (0, 0))
    act_spec = pl.BlockSpec((1, T, tf), lambda f: (f, 0, 0))
    hbm = pl.BlockSpec(memory_space=pl.ANY)
    in_specs = [tok, pl.BlockSpec((3, tf, D), lambda f: (0, f, 0)), hbm]
    out_specs = [tok, act_spec, act_spec]
    out_shape = [SDS((T, D), f32), SDS((nf, T, tf), bf16), SDS((nf, T, tf), bf16)]
    scratch = [pltpu.SemaphoreType.DMA((T // tm,))]
    args = [h, w, x]
    if with_loss:
        in_specs.append(hbm)
        args.append(target)
        out_specs += [tok, pl.BlockSpec((1, 128), lambda f: (0, 0))]
        out_shape += [SDS((T, D), bf16), SDS((1, 128), f32)]
        scratch = [pltpu.VMEM((tm, D), f32)] + scratch
    else:
        in_specs.append(pl.BlockSpec((1, D), lambda f: (0, 0)))
        args.append(next_gain)
        out_specs.append(tok)
        out_shape.append(SDS((T, D), bf16))
    return pl.pallas_call(
        body, grid=(nf,), in_specs=in_specs, out_specs=out_specs, out_shape=tuple(out_shape), scratch_shapes=scratch,
        compiler_params=_cparams(("arbitrary",), VMEM_LIMIT_V7X), name=name)(*args)


NORM_BWD_ROWS = 512


def _norm_bwd_scratch(dh_in_hbm, with_bf16):
    buf = lambda dt: pltpu.VMEM((2, NORM_BWD_ROWS, D), dt)
    return [buf(f32), buf(f32), buf(f32) if dh_in_hbm else None, buf(f32), buf(bf16) if with_bf16 else None,
            pltpu.SemaphoreType.DMA((6,)), pltpu.SemaphoreType.DMA((4,))]


def _norm_bwd_rows(T, dh_src, x_hbm, gain_ref, dr_hbm, dx_hbm, dxb_hbm, xbuf, rbuf, hbuf, obuf, obb, in_sems, out_sems):
    tm = min(NORM_BWD_ROWS, T)
    nchunk = T // tm

    def loads(i):
        s, rows = i % 2, pl.ds(i * tm, tm)
        cps = [pltpu.make_async_copy(x_hbm.at[rows, :], xbuf.at[s, pl.ds(0, tm), :], in_sems.at[3 * s]),
               pltpu.make_async_copy(dr_hbm.at[rows, :], rbuf.at[s, pl.ds(0, tm), :], in_sems.at[3 * s + 1])]
        if hbuf is not None:
            cps.append(pltpu.make_async_copy(dh_src.at[rows, :], hbuf.at[s, pl.ds(0, tm), :], in_sems.at[3 * s + 2]))
        return cps

    def stores(i):
        s, rows = i % 2, pl.ds(i * tm, tm)
        cps = [pltpu.make_async_copy(obuf.at[s, pl.ds(0, tm), :], dx_hbm.at[rows, :], out_sems.at[2 * s])]
        if dxb_hbm is not None:
            cps.append(pltpu.make_async_copy(obb.at[s, pl.ds(0, tm), :], dxb_hbm.at[rows, :], out_sems.at[2 * s + 1]))
        return cps

    for cp in loads(0):
        cp.start()
    dg = jnp.zeros((1, D), f32)
    for i in range(nchunk):
        s = i % 2
        if i + 1 < nchunk:
            for cp in loads(i + 1):
                cp.start()
        for cp in loads(i):
            cp.wait()
        if i >= 2:
            for cp in stores(i - 2):
                cp.wait()
        xv = xbuf[s, 0:tm, :]
        rstd = lax.rsqrt(jnp.mean(xv * xv, axis=-1, keepdims=True) + EPS)
        xh = xv * rstd
        dhv = hbuf[s, 0:tm, :] if hbuf is not None else dh_src[i * tm:(i + 1) * tm, :]
        dxh = dhv * gain_ref[...]
        dx = rbuf[s, 0:tm, :] + rstd * (dxh - xh * jnp.mean(dxh * xh, axis=-1, keepdims=True))
        obuf[s, 0:tm, :] = dx
        if dxb_hbm is not None:
            obb[s, 0:tm, :] = dx.astype(bf16)
        dg = dg + jnp.sum(dhv * xh, axis=0, keepdims=True)
        for cp in stores(i):
            cp.start()
    for i in range(max(nchunk - 2, 0), nchunk):
        for cp in stores(i):
            cp.wait()
    return dg


def _ffn_bwd(dob, h, gate, up, w, norm, name):
    T = h.shape[0]
    _, tf = _ffn_tiles(T)
    nf = F // tf
    nin = 5 if norm is None else 8
    nout = 2 if norm is None else 4

    def body(*refs):
        do_hbm, h_hbm, g_ref, u_ref, w_ref = refs[:5]
        dw_ref = refs[nin + nout - 1]
        do_v, h_v, dh_acc, dgu_s, act_s, sems, do_sems = refs[nin + nout:nin + nout + 7]
        fi = pl.program_id(0)
        h_load = pltpu.make_async_copy(h_hbm, h_v, sems.at[1])

        def do_rows(r):
            rows = pl.ds(r, FFN_ROW_CHUNK)
            return pltpu.make_async_copy(do_hbm.at[rows, :], do_v.at[rows, :], do_sems.at[r // FFN_ROW_CHUNK])

        @pl.when(fi == 0)
        def _():
            for r in range(0, T, FFN_ROW_CHUNK):
                do_rows(r).start()
            h_load.start()
            dh_acc[...] = jnp.zeros_like(dh_acc)

        wgu = w_ref[0:2].reshape(2 * tf, D)
        for r in range(0, T, FFN_ROW_CHUNK):
            rows = slice(r, r + FFN_ROW_CHUNK)

            @pl.when(fi == 0)
            def _():
                do_rows(r).wait()

            dov = do_v[rows, :]
            gv = g_ref[0, rows, :].astype(f32)
            uv = u_ref[0, rows, :].astype(f32)
            sg = _sigmoid(gv)
            sil = gv * sg
            dact = _nt(dov, w_ref[2])
            dup = dact * sil
            dgate = dact * uv * (sg * (1.0 + gv * (1.0 - sg)))
            dgu = jnp.concatenate([dgate.astype(bf16), dup.astype(bf16)], axis=1)
            dgu_s[rows, :] = dgu
            act_s[rows, :] = (sil * uv).astype(bf16)
            dh_acc[rows, :] += _nn(dgu, wgu)

        @pl.when(fi == 0)
        def _():
            h_load.wait()

        dw_ref[0:2] = _tn(dgu_s[...], h_v[...]).reshape(2, tf, D).astype(bf16)
        dw_ref[2] = _tn(act_s[...], do_v[...]).astype(bf16)

        @pl.when(fi == nf - 1)
        def _():
            if norm is None:
                out = pltpu.make_async_copy(dh_acc, refs[nin], sems.at[0])
                out.start()
                out.wait()
            else:
                x_hbm, gain_ref, dr_hbm, dx_hbm, dxb_hbm, dg_ref = refs[5:11]
                xbuf, rbuf, obuf, obb, in_sems, out_sems = refs[nin + nout + 7:]
                dg_ref[...] = _norm_bwd_rows(T, dh_acc, x_hbm, gain_ref, dr_hbm, dx_hbm, dxb_hbm,
                                             xbuf, rbuf, None, obuf, obb, in_sems, out_sems)

    act_spec = pl.BlockSpec((1, T, tf), lambda f: (f, 0, 0))
    wspec = pl.BlockSpec((3, tf, D), lambda f: (0, f, 0))
    vec = pl.BlockSpec((1, D), lambda f: (0, 0))
    hbm = pl.BlockSpec(memory_space=pl.ANY)
    in_specs, out_specs = [hbm, hbm, act_spec, act_spec, wspec], [hbm, wspec]
    out_shape, args = [SDS((T, D), f32), SDS((3, F, D), bf16)], [dob, h, gate, up, w]
    scratch = [pltpu.VMEM((T, D), bf16), pltpu.VMEM((T, D), bf16), pltpu.VMEM((T, D), f32),
               pltpu.VMEM((T, 2 * tf), bf16), pltpu.VMEM((T, tf), bf16), pltpu.SemaphoreType.DMA((2,)),
               pltpu.SemaphoreType.DMA((T // FFN_ROW_CHUNK,))]
    if norm is not None:
        in_specs += [hbm, vec, hbm]
        out_specs = [hbm, hbm, vec, wspec]
        out_shape = [SDS((T, D), f32), SDS((T, D), bf16), SDS((1, D), f32), SDS((3, F, D), bf16)]
        args += list(norm)
        scratch += [sc for sc in _norm_bwd_scratch(False, True) if sc is not None]
    return pl.pallas_call(
        body, grid=(nf,), in_specs=in_specs, out_specs=out_specs, out_shape=tuple(out_shape), scratch_shapes=scratch,
        compiler_params=_cparams(("arbitrary",), VMEM_LIMIT_V7X), name=name)(*args)


def _in_proj_fwd(h, wint, name):
    T = h.shape[0]
    tm = min(512, T)

    def body(h_ref, w_ref, z_ref):
        z_ref[...] = _nt(h_ref[...], w_ref[...])

    return pl.pallas_call(
        body, grid=(T // tm,),
        in_specs=[pl.BlockSpec((tm, D), lambda i: (i, 0)), pl.BlockSpec((DIN, D), lambda i: (0, 0))],
        out_specs=pl.BlockSpec((tm, DIN), lambda i: (i, 0)),
        out_shape=SDS((T, DIN), f32), name=name)(h, wint)


def _in_proj_bwd(dz, wint, h, norm, out_scale, name):
    x, g, dres = norm
    T = h.shape[0]
    tm = min(512, T)
    nt = T // tm

    def body(dz_ref, w_ref, h_ref, x_ref, g_ref, dr_ref, dx_ref, dxb_ref, dg_ref, dw_ref, acc):
        i = pl.program_id(0)
        dzb = dz_ref[...].astype(bf16)
        dhv = _nn(dzb, w_ref[...])
        part = _tn(dzb, h_ref[...])
        xv = x_ref[...]
        rstd = lax.rsqrt(jnp.mean(xv * xv, axis=-1, keepdims=True) + EPS)
        xh = xv * rstd
        dxh = dhv * g_ref[...]
        dx = dr_ref[...] + rstd * (dxh - xh * jnp.mean(dxh * xh, axis=-1, keepdims=True))
        dx_ref[...] = dx
        dxb_ref[...] = (out_scale * dx).astype(bf16)
        dg = jnp.sum(dhv * xh, axis=0, keepdims=True)

        @pl.when(i == 0)
        def _():
            acc[...] = part
            dg_ref[...] = dg

        @pl.when(i > 0)
        def _():
            acc[...] += part
            dg_ref[...] += dg

        @pl.when(i == nt - 1)
        def _():
            dw_ref[...] = acc[...].astype(bf16)

    wspec = pl.BlockSpec((DIN, D), lambda i: (0, 0))
    tok = pl.BlockSpec((tm, D), lambda i: (i, 0))
    vec = pl.BlockSpec((1, D), lambda i: (0, 0))
    return pl.pallas_call(
        body, grid=(nt,),
        in_specs=[pl.BlockSpec((tm, DIN), lambda i: (i, 0)), wspec, tok, tok, vec, tok],
        out_specs=[tok, tok, vec, wspec],
        out_shape=(SDS((T, D), f32), SDS((T, D), bf16), SDS((1, D), f32), SDS((DIN, D), bf16)),
        scratch_shapes=[pltpu.VMEM((DIN, D), f32)],
        compiler_params=_cparams(("arbitrary",)), name=name)(dz, wint, h, x, g, dres)


def _out_proj_fwd(ymix, wout, x, g, name):
    T = x.shape[0]
    tm = min(512, T)

    def body(y_ref, w_ref, x_ref, g_ref, o_ref, h_ref):
        o = x_ref[...] + _nn(y_ref[...], w_ref[...])
        o_ref[...] = o
        r = lax.rsqrt(jnp.mean(o * o, axis=-1, keepdims=True) + EPS)
        h_ref[...] = (o * r * g_ref[...]).astype(bf16)

    tok = pl.BlockSpec((tm, D), lambda i: (i, 0))
    return pl.pallas_call(
        body, grid=(T // tm,),
        in_specs=[pl.BlockSpec((tm, DMIX), lambda i: (i, 0)), pl.BlockSpec((DMIX, D), lambda i: (0, 0)), tok,
                  pl.BlockSpec((1, D), lambda i: (0, 0))],
        out_specs=[tok, tok], out_shape=(SDS((T, D), f32), SDS((T, D), bf16)), name=name)(ymix, wout, x, g)


def _out_proj_bwd(dxb, wout, ymix, name):
    T = dxb.shape[0]
    tm = min(512, T)
    nt = T // tm

    def body(dx_ref, w_ref, y_ref, dy_ref, dw_ref, acc):
        i = pl.program_id(0)
        dxv = dx_ref[...]
        dy_ref[...] = _nt(dxv, w_ref[...])
        part = _tn(y_ref[...], dxv)

        @pl.when(i == 0)
        def _():
            acc[...] = part

        @pl.when(i > 0)
        def _():
            acc[...] += part

        @pl.when(i == nt - 1)
        def _():
            dw_ref[...] = acc[...].astype(bf16)

    wspec = pl.BlockSpec((DMIX, D), lambda i: (0, 0))
    return pl.pallas_call(
        body, grid=(nt,),
        in_specs=[pl.BlockSpec((tm, D), lambda i: (i, 0)), wspec, pl.BlockSpec((tm, DMIX), lambda i: (i, 0))],
        out_specs=[pl.BlockSpec((tm, DMIX), lambda i: (i, 0)), wspec],
        out_shape=(SDS((T, DMIX), f32), SDS((DMIX, D), bf16)),
        scratch_shapes=[pltpu.VMEM((DMIX, D), f32)],
        compiler_params=_cparams(("arbitrary",)), name=name)(dxb, wout, ymix)


def _t5_bucket_table():
    ql = np.arange(BLK)[:, None]
    kl = np.arange(2 * BLK)[None, :]
    n = np.maximum(ql + BLK - kl, 0)
    max_exact = NBUCK // 2
    large = max_exact + (np.log(np.maximum(n, 1) / max_exact) / np.log(MAX_DISTANCE / max_exact)
                         * (NBUCK - max_exact)).astype(np.int32)
    large = np.minimum(large, NBUCK - 1)
    return np.where(n < max_exact, n, large).astype(np.int32)


def _fill_bias(bk_ref, rb_ref, bias_scr):
    bk = bk_ref[...]
    for h in range(NH):
        def step(b, acc, h=h):
            return acc + jnp.where(bk == b, rb_ref[b, h], 0.0)
        bias_scr[h] = lax.fori_loop(0, NBUCK, step, jnp.zeros((BLK, 2 * BLK), f32))


MIX_SUB = 4


class _Window:
    def __init__(self, zc_ref, zp_ref, n, s):
        self.blk = n * MIX_SUB + s
        self.cur = lambda a, b: zc_ref[s * BLK:(s + 1) * BLK, a:b]
        self.prev = (lambda a, b: zp_ref[:, a:b]) if s == 0 else (lambda a, b: zc_ref[(s - 1) * BLK:s * BLK, a:b])


def _attn_qkv(win, kh, qg, kg):
    kc = DATTN + HD * kh
    vc = DATTN + DKV + HD * kh
    kx = jnp.concatenate([win.prev(kc, kc + HD), win.cur(kc, kc + HD)], axis=0)
    vx = jnp.concatenate([win.prev(vc, vc + HD), win.cur(vc, vc + HD)], axis=0)
    qx = jnp.concatenate([win.cur(HD * (GQA * kh + g), HD * (GQA * kh + g + 1)) for g in range(GQA)], axis=0)
    rq = lax.rsqrt(jnp.mean(qx * qx, axis=-1, keepdims=True) + EPS)
    rk = lax.rsqrt(jnp.mean(kx * kx, axis=-1, keepdims=True) + EPS)
    qhat, khat = qx * rq, kx * rk
    return dict(qhat=qhat, khat=khat, rq=rq, rk=rk, qsb=(qhat * (qg * SCALE)).astype(bf16),
                knb=(khat * kg).astype(bf16), vb=vx.astype(bf16))


def _window_masks(n):
    row = lax.broadcasted_iota(i32, (GQA * BLK, 2 * BLK), 0) & (BLK - 1)
    col = lax.broadcasted_iota(i32, (GQA * BLK, 2 * BLK), 1)
    band = (col > row) & (col <= row + BLK)
    return band & ((col >= BLK) | (n > 0)), band


def _attn_probs(a, kh, sk_ref, bias_scr, mask):
    s = _nt(a["qsb"], a["knb"]) + bias_scr[GQA * kh:GQA * (kh + 1)].reshape(GQA * BLK, 2 * BLK)
    s = jnp.where(mask, s, NEG)
    ridx = lax.broadcasted_iota(i32, (GQA * BLK, 1), 0)
    sink = jnp.full((GQA * BLK, 1), sk_ref[GQA * kh + GQA - 1], f32)
    for g in range(GQA - 2, -1, -1):
        sink = jnp.where(ridx < (g + 1) * BLK, sk_ref[GQA * kh + g], sink)
    m = jnp.maximum(jnp.max(s, axis=-1, keepdims=True), sink)
    e = jnp.exp(s - m)
    den = jnp.sum(e, axis=-1, keepdims=True) + jnp.exp(sink - m)
    return e / den


POOL_STEPS = {2: (1,), 4: (1, 2), 8: (1, 2, 4), 16: (1, 2, 4, 8)}


def _pool_group(win, g, w):
    n = win.blk
    c0 = DATTN + 2 * DKV + PGD * g
    uc = win.cur(c0, c0 + PGD)
    up = jnp.where(n > 0, win.prev(c0, c0 + PGD), 0.0)
    sm = jnp.concatenate([up, uc], axis=0)
    for k in POOL_STEPS[w]:
        sm = sm + pltpu.roll(sm, k, axis=0)
    pos = n * BLK + lax.broadcasted_iota(i32, (BLK, 1), 0) + 1
    cnt = jnp.minimum(pos, w).astype(f32)
    return sm[BLK:2 * BLK] / cnt - uc, cnt


def _mix_fwd(z, qg, kg, sinks, relb, bucket, pool_w, pscale, name):
    T = z.shape[0]
    step_rows = MIX_SUB * BLK
    nsteps = T // step_rows

    def body(zc_ref, zp_ref, qg_ref, kg_ref, sk_ref, rb_ref, bk_ref, pw_ref, ps_ref, y_ref, p_ref, bias_scr, yacc):
        n = pl.program_id(0)

        @pl.when(n == 0)
        def _():
            _fill_bias(bk_ref, rb_ref, bias_scr)

        first_mask, mask = _window_masks(n)
        for s in range(MIX_SUB):
            win = _Window(zc_ref, zp_ref, n, s)
            rows = slice(s * BLK, (s + 1) * BLK)
            for kh in range(NKV):
                a = _attn_qkv(win, kh, qg_ref[...], kg_ref[...])
                pb = _attn_probs(a, kh, sk_ref, bias_scr, first_mask if s == 0 else mask).astype(bf16)
                p_ref[s, GQA * kh:GQA * (kh + 1)] = pb.reshape(GQA, BLK, 2 * BLK)
                o = _nn(pb, a["vb"])
                for g in range(GQA):
                    hc = HD * (GQA * kh + g)
                    yacc[rows, hc:hc + HD] = o[g * BLK:(g + 1) * BLK]
            for g, w in enumerate(POOL_WINDOWS):
                pooled, _ = _pool_group(win, g, w)
                yp = _nn(pooled.astype(bf16), pw_ref[g].astype(bf16)) * ps_ref[:, PGD * g:PGD * (g + 1)]
                yacc[rows, DATTN + PGD * g:DATTN + PGD * (g + 1)] = yp
        y_ref[...] = yacc[...].astype(bf16)

    full = lambda *shape: pl.BlockSpec(shape, lambda n: (0,) * len(shape))
    smem = pl.BlockSpec(memory_space=pltpu.SMEM)
    return pl.pallas_call(
        body, grid=(nsteps,),
        in_specs=[pl.BlockSpec((step_rows, DIN), lambda n: (n, 0)),
                  pl.BlockSpec((BLK, DIN), lambda n: (jnp.maximum(n * MIX_SUB - 1, 0), 0)),
                  full(1, HD), full(1, HD), smem, smem, full(BLK, 2 * BLK),
                  full(len(POOL_WINDOWS), PGD, PGD), full(1, DPOOL)],
        out_specs=[pl.BlockSpec((step_rows, DMIX), lambda n: (n, 0)),
                   pl.BlockSpec((MIX_SUB, NH, BLK, 2 * BLK), lambda n: (n, 0, 0, 0))],
        out_shape=(SDS((T, DMIX), bf16), SDS((T // BLK, NH, BLK, 2 * BLK), bf16)),
        scratch_shapes=[pltpu.VMEM((NH, BLK, 2 * BLK), f32), pltpu.VMEM((step_rows, DMIX), f32)],
        compiler_params=_cparams(("arbitrary",)), name=name)(z, z, qg, kg, sinks, relb, bucket, pool_w, pscale)


def _mix_bwd(z, dy, probs, qg, kg, relb, bucket, pool_w, pscale, name):
    T = z.shape[0]
    step_rows = MIX_SUB * BLK
    nsteps = T // step_rows

    def body(zc_ref, zp_ref, dy_ref, p_ref, qg_ref, kg_ref, bk_ref, pw_ref, ps_ref,
             dz_ref, dqg_ref, dkg_ref, dsk_ref, drb_ref, dpw_ref, dps_ref, dbias_scr):
        n = pl.program_id(0)

        @pl.when(n == 0)
        def _():
            dbias_scr[...] = jnp.zeros_like(dbias_scr)
            dqg_ref[...] = jnp.zeros_like(dqg_ref)
            dkg_ref[...] = jnp.zeros_like(dkg_ref)
            dpw_ref[...] = jnp.zeros_like(dpw_ref)
            dps_ref[...] = jnp.zeros_like(dps_ref)

        qg, kg = qg_ref[...], kg_ref[...]
        for s in range(MIX_SUB):
            win = _Window(zc_ref, zp_ref, n, s)
            blk = win.blk
            rows = pl.ds(pl.multiple_of(blk * BLK, BLK), BLK)
            prow = pl.ds(pl.multiple_of(jnp.maximum(blk - 1, 0) * BLK, BLK), BLK)
            dyr = slice(s * BLK, (s + 1) * BLK)

            def into_prev(fn, s=s):
                if s == 0:
                    pl.when(n > 0)(fn)
                else:
                    fn()

            for kh in range(NKV):
                a = _attn_qkv(win, kh, qg, kg)
                pb = p_ref[s, GQA * kh:GQA * (kh + 1)].reshape(GQA * BLK, 2 * BLK)
                p = pb.astype(f32)
                do = jnp.concatenate([dy_ref[dyr, HD * (GQA * kh + g):HD * (GQA * kh + g + 1)] for g in range(GQA)],
                                     axis=0).astype(bf16)
                dv = _tn(pb, do)
                dp = _nt(do, a["vb"])
                delta = jnp.sum(p * dp, axis=-1, keepdims=True)
                ds = p * (dp - delta)
                for g in range(GQA):
                    dbias_scr[GQA * kh + g] += ds[g * BLK:(g + 1) * BLK]
                dsb = ds.astype(bf16)
                dqn = _nn(dsb, a["knb"]) * SCALE
                dkn = _tn(dsb, a["qsb"])
                qhat, khat = a["qhat"], a["khat"]
                dqg_ref[...] += jnp.sum(dqn * qhat, axis=0, keepdims=True)
                dkg_ref[...] += jnp.sum(dkn * khat, axis=0, keepdims=True)
                dqh = dqn * qg
                dq = a["rq"] * (dqh - qhat * jnp.mean(dqh * qhat, axis=-1, keepdims=True))
                dkh = dkn * kg
                dk = a["rk"] * (dkh - khat * jnp.mean(dkh * khat, axis=-1, keepdims=True))
                kc = DATTN + HD * kh
                vc = DATTN + DKV + HD * kh
                for g in range(GQA):
                    hc = HD * (GQA * kh + g)
                    dz_ref[rows, hc:hc + HD] = dq[g * BLK:(g + 1) * BLK]
                dz_ref[rows, kc:kc + HD] = dk[BLK:2 * BLK]
                dz_ref[rows, vc:vc + HD] = dv[BLK:2 * BLK]

                def kv_prev(dk=dk, dv=dv, kc=kc, vc=vc, prow=prow):
                    dz_ref[prow, kc:kc + HD] += dk[0:BLK]
                    dz_ref[prow, vc:vc + HD] += dv[0:BLK]

                into_prev(kv_prev)

            for g, w in enumerate(POOL_WINDOWS):
                c0 = DATTN + 2 * DKV + PGD * g
                pooled, cnt = _pool_group(win, g, w)
                pb = pooled.astype(bf16)
                wb = pw_ref[g].astype(bf16)
                dyp = dy_ref[dyr, DATTN + PGD * g:DATTN + PGD * (g + 1)]
                ypre = _nn(pb, wb)
                dps_ref[:, PGD * g:PGD * (g + 1)] += jnp.sum(dyp * ypre, axis=0, keepdims=True)
                dyg = (dyp * ps_ref[:, PGD * g:PGD * (g + 1)]).astype(bf16)
                dpw_ref[g] += _tn(pb, dyg)
                dpooled = _nt(dyg, wb)
                due = jnp.concatenate([jnp.zeros((BLK, PGD), f32), dpooled / cnt], axis=0)
                for k in POOL_STEPS[w]:
                    due = due + pltpu.roll(due, 2 * BLK - k, axis=0)
                dz_ref[rows, c0:c0 + PGD] = due[BLK:2 * BLK] - dpooled

                def pool_prev(due=due, c0=c0, prow=prow):
                    dz_ref[prow, c0:c0 + PGD] += due[0:BLK]

                into_prev(pool_prev)

        @pl.when(n == nsteps - 1)
        def _():
            bk = bk_ref[...]
            ri = lax.broadcasted_iota(i32, (NBUCK, NH), 0)
            ci = lax.broadcasted_iota(i32, (NBUCK, NH), 1)

            def step(b, acc):
                for h in range(NH):
                    sel = jnp.where(bk == b, dbias_scr[h], 0.0)
                    tot = jnp.sum(jnp.sum(sel, axis=1, keepdims=True), axis=0, keepdims=True)
                    acc = acc + jnp.where((ri == b) & (ci == h), tot, 0.0)
                return acc

            drb_ref[...] = lax.fori_loop(0, NBUCK, step, jnp.zeros((NBUCK, NH), f32))
            lane = lax.broadcasted_iota(i32, (1, 128), 1)
            dsk = jnp.zeros((1, 128), f32)
            for h in range(NH):
                tot = jnp.sum(jnp.sum(dbias_scr[h], axis=1, keepdims=True), axis=0, keepdims=True)
                dsk = dsk - jnp.where(lane == h, tot, 0.0)
            dsk_ref[...] = dsk

    full = lambda *shape: pl.BlockSpec(shape, lambda n: (0,) * len(shape))
    npg = len(POOL_WINDOWS)
    return pl.pallas_call(
        body, grid=(nsteps,),
        in_specs=[pl.BlockSpec((step_rows, DIN), lambda n: (n, 0)),
                  pl.BlockSpec((BLK, DIN), lambda n: (jnp.maximum(n * MIX_SUB - 1, 0), 0)),
                  pl.BlockSpec((step_rows, DMIX), lambda n: (n, 0)),
                  pl.BlockSpec((MIX_SUB, NH, BLK, 2 * BLK), lambda n: (n, 0, 0, 0)),
                  full(1, HD), full(1, HD), full(BLK, 2 * BLK), full(npg, PGD, PGD), full(1, DPOOL)],
        out_specs=[full(T, DIN), full(1, HD), full(1, HD), full(1, 128), full(NBUCK, NH),
                   full(npg, PGD, PGD), full(1, DPOOL)],
        out_shape=(SDS((T, DIN), f32), SDS((1, HD), f32), SDS((1, HD), f32), SDS((1, 128), f32),
                   SDS((NBUCK, NH), f32), SDS((npg, PGD, PGD), f32), SDS((1, DPOOL), f32)),
        scratch_shapes=[pltpu.VMEM((NH, BLK, 2 * BLK), f32)],
        compiler_params=_cparams(("arbitrary",), VMEM_LIMIT_V7X),
        name=name)(z, z, dy, probs, qg, kg, bucket, pool_w, pscale)


class _LocalWeights:
    def __init__(self, w1, wint, wout, w2):
        self.w1, self.wint, self.wout, self.w2 = w1, wint, wout, w2

    def ffn1(self):
        return self.w1

    def first_norm(self, x, gain):
        return _norm_fwd(x, gain, "norm1_fwd")

    def after_ffn1(self, gain, x1):
        return gain

    def mix(self, after):
        return self.wint, self.wout

    def before_out_proj(self, wout, after):
        return wout

    def ffn2(self, after):
        return self.w2

    def out_ffn2_grads_ready(self, dwout, dw2, after):
        return after

    def before_ffn1_bwd(self, dwint, dx1b):
        return dx1b


def _local_step(x, target, weights, g1, gm, g3, qg, kg, sinks, relb, pool_w, pscale):
    bucket = jnp.asarray(_t5_bucket_table())
    sk = sinks.reshape(NH)
    w1 = weights.ffn1()
    h1 = weights.first_norm(x, g1)
    x1, gate1, up1, h2 = _ffn_fwd(h1, w1, x, None, gm, "ffn1_fwd")
    gm = weights.after_ffn1(gm, x1)
    wint, wout = weights.mix(h2)
    z = _in_proj_fwd(h2, wint, "in_proj_fwd")
    ymix, probs = _mix_fwd(z, qg, kg, sk, relb, bucket, pool_w, pscale, "mix_fwd")
    wout = weights.before_out_proj(wout, ymix)
    x2, h3 = _out_proj_fwd(ymix, wout, x1, g3, "out_proj_fwd")
    w2 = weights.ffn2(h3)
    dy, gate2, up2, dyb, loss_lanes = _ffn_fwd(h3, w2, x2, target, None, "ffn2_fwd")

    dx2, dx2b, dg3, dw2 = _ffn_bwd(dyb, h3, gate2, up2, w2, (x2, g3, dy), "ffn2_bwd")
    dymix, dwout = _out_proj_bwd(dx2b, wout, ymix, "out_proj_bwd")
    dymix = weights.out_ffn2_grads_ready(dwout, dw2, dymix)
    dz, dqg, dkg, dsk, drb, dpw, dps = _mix_bwd(z, dymix, probs, qg, kg, relb, bucket, pool_w, pscale, "mix_bwd")
    dx1, dx1b, dgm, dwint = _in_proj_bwd(dz, wint, h2, (x1, gm, dx2), 0.5, "in_proj_bwd")
    dx1b = weights.before_ffn1_bwd(dwint, dx1b)
    dh1, dw1 = _ffn_bwd(dx1b, h1, gate1, up1, w1, None, "ffn1_bwd")
    small = dict(mix_norm=dgm, ffn2_norm=dg3, pool_scale=dps, q_norm=dqg, k_norm=dkg,
                 attn_sinks=dsk[:, :NH], rel_bias=drb, pool_w=dpw, loss=loss_lanes)
    return (dh1, dx1), (dw1, dwint, dwout, dw2), small


SMALL_NAMES = ("ffn1_norm", "mix_norm", "ffn2_norm", "pool_scale", "q_norm", "k_norm", "attn_sinks", "rel_bias",
               "pool_w", "loss")
SMALL_SHAPES = dict(ffn1_norm=(1, D), mix_norm=(1, D), ffn2_norm=(1, D), pool_scale=(1, DPOOL), q_norm=(1, HD),
                    k_norm=(1, HD), attn_sinks=(1, NH), rel_bias=(NBUCK, NH),
                    pool_w=(1, len(POOL_WINDOWS), PGD, PGD), loss=(1, 128))


def _small_rows(name):
    return -(-int(np.prod(SMALL_SHAPES[name])) // 128)


SMALL_OFF = {}
_r = 0
for _n in SMALL_NAMES:
    SMALL_OFF[_n] = _r
    _r += _small_rows(_n)
SMALL_ROWS = -(-_r // 16) * 16
LOSS_ROW = SMALL_OFF["loss"]


def _pack_small(vals):
    parts = []
    for n in SMALL_NAMES:
        size = _small_rows(n) * 128
        if n in vals:
            flat = vals[n].astype(f32).reshape(-1)
            parts.append(jnp.pad(flat, (0, size - flat.shape[0])))
        else:
            parts.append(jnp.zeros((size,), f32))
    flat = jnp.concatenate(parts)
    flat = jnp.pad(flat, (0, SMALL_ROWS * 128 - flat.shape[0]))
    return flat.reshape(SMALL_ROWS, 128)


def _unpack_small(packed, name):
    size = int(np.prod(SMALL_SHAPES[name]))
    r0 = SMALL_OFF[name]
    return packed[r0:r0 + _small_rows(name)].reshape(-1)[:size].reshape(SMALL_SHAPES[name])


def _position():
    return lax.axis_index("x"), lax.axis_index("y"), lax.axis_index("c")


def _dev_index(x, y, c):
    return 4 * x + 2 * y + c


G1_PIECES, MIX_PIECES, F2_PIECES = (0, 1, 2), (3, 4), (5, 6, 7)


def _group_rows(pieces):
    return sum(PIECE_ROWS[k] for k in pieces)


def _shard_piece(s_ref, k):
    return s_ref.at[pl.ds(PIECE_OFF[k], PIECE_ROWS[k]), :]


def _shard_group(s_ref, pieces):
    return s_ref.at[pl.ds(PIECE_OFF[pieces[0]], _group_rows(pieces)), :]


def _weight_pieces(w1_ref=None, wi_ref=None, wo_ref=None, w2_ref=None):
    arrs = {}
    if w1_ref is not None:
        arrs.update({0: w1_ref.at[0], 1: w1_ref.at[1], 2: w1_ref.at[2]})
    if wi_ref is not None:
        arrs[3] = wi_ref
    if wo_ref is not None:
        arrs[4] = wo_ref
    if w2_ref is not None:
        arrs.update({5: w2_ref.at[0], 6: w2_ref.at[1], 7: w2_ref.at[2]})
    return arrs


def _block_rows(arrs, k, dev):
    r = PIECE_ROWS[k]
    return arrs[k].at[pl.ds(pl.multiple_of(_dev_index(*dev) * r, 16), r), :]


NORM_ROWS = 512


def _all_gather_ffn1(shard, x, gain):
    pieces = G1_PIECES
    rest_pieces = MIX_PIECES + F2_PIECES
    half = FS // 2
    T = x.shape[0]
    SIB, X0, X1, Y0, Y1, RELAY_Y, RELAY_X, ON_X, ON_Y, ON_D0, ON_D1 = range(11)

    def body(s_ref, x_ref, g_ref, w1_ref, h_ref, wi_ref, wo_ref, w2_ref, xbuf, hbuf, rest_buf,
             send_sems, recv_sems, local_sem, norm_sems):
        x, y, c = _position()
        me, sib = (x, y, c), (x, y, 1 - c)
        xn, yn, dg = (1 - x, y, c), (x, 1 - y, c), (1 - x, 1 - y, c)
        arrs = _weight_pieces(w1_ref=w1_ref)

        def place_rest():
            rest = _weight_pieces(wi_ref=wi_ref, wo_ref=wo_ref, w2_ref=w2_ref)
            grp = _shard_group(s_ref, rest_pieces)
            load = pltpu.make_async_copy(grp, rest_buf, norm_sems.at[0])
            load.start()
            load.wait()
            base = PIECE_OFF[rest_pieces[0]]
            for k in rest_pieces:
                pltpu.make_async_copy(rest_buf.at[pl.ds(PIECE_OFF[k] - base, PIECE_ROWS[k]), :],
                                      _block_rows(rest, k, me), norm_sems.at[1]).start()
            pltpu.make_async_copy(grp, rest_buf, norm_sems.at[1]).wait()

        def first_norm():
            for r in range(0, T, NORM_ROWS):
                load = pltpu.make_async_copy(x_ref.at[pl.ds(r, NORM_ROWS), :], xbuf, norm_sems.at[0])
                load.start()
                load.wait()
                xv = xbuf[...]
                rs = lax.rsqrt(jnp.mean(xv * xv, axis=-1, keepdims=True) + EPS)
                hbuf[...] = (xv * rs * g_ref[...]).astype(bf16)
                store = pltpu.make_async_copy(hbuf, h_ref.at[pl.ds(r, NORM_ROWS), :], norm_sems.at[1])
                store.start()
                store.wait()

        def rows_of(k, block, hf):
            r = PIECE_ROWS[k]
            start, size = (0, r) if hf is None else (hf * half, half)
            return arrs[k].at[pl.ds(pl.multiple_of(_dev_index(*block) * r + start, 16), size), :]

        def copies(rel, block, hf, to, from_shard=False):
            def src(k):
                if not from_shard:
                    return rows_of(k, block, hf)
                start, size = (0, PIECE_ROWS[k]) if hf is None else (hf * half, half)
                return s_ref.at[pl.ds(PIECE_OFF[k] + start, size), :]
            return [pltpu.make_async_remote_copy(
                src_ref=src(k), dst_ref=rows_of(k, block, hf), send_sem=send_sems.at[rel], recv_sem=recv_sems.at[rel],
                device_id=to, device_id_type=MESH) for k in pieces]

        def waiter(rel, hf):
            nrows = len(pieces) * (FS if hf is None else half)
            grp = s_ref.at[pl.ds(0, nrows), :]
            return pltpu.make_async_remote_copy(src_ref=grp, dst_ref=grp, send_sem=send_sems.at[rel],
                                                recv_sem=recv_sems.at[rel], device_id=me, device_id_type=MESH)

        def start(cps):
            for cp in cps:
                cp.start()

        mine = [pltpu.make_async_copy(_shard_piece(s_ref, k), _block_rows(arrs, k, me), local_sem) for k in pieces]
        start(mine)
        start(copies(SIB, me, None, sib, True))
        start(copies(X0, me, 0, xn, True))
        start(copies(Y1, me, 1, yn, True))
        start(copies(X1, me, 1, xn, True))
        start(copies(Y0, me, 0, yn, True))
        first_norm()
        place_rest()
        waiter(X0, 0).wait_recv()
        start(copies(RELAY_Y, xn, 0, yn))
        waiter(Y1, 1).wait_recv()
        start(copies(RELAY_X, yn, 1, xn))
        waiter(X1, 1).wait_recv()
        start(copies(ON_X, xn, None, sib))
        waiter(Y0, 0).wait_recv()
        start(copies(ON_Y, yn, None, sib))
        waiter(RELAY_Y, 0).wait_recv()
        start(copies(ON_D0, dg, 0, sib))
        waiter(RELAY_X, 1).wait_recv()
        start(copies(ON_D1, dg, 1, sib))
        waiter(SIB, None).wait_recv()
        waiter(ON_X, None).wait_recv()
        waiter(ON_Y, None).wait_recv()
        waiter(ON_D0, 0).wait_recv()
        waiter(ON_D1, 1).wait_recv()
        for rel, hf in ((SIB, None), (X0, 0), (X1, 1), (Y0, 0), (Y1, 1), (RELAY_Y, 0), (RELAY_X, 1),
                        (ON_X, None), (ON_Y, None), (ON_D0, 0), (ON_D1, 1)):
            waiter(rel, hf).wait_send()
        grp = _shard_group(s_ref, pieces)
        pltpu.make_async_copy(grp, grp, local_sem).wait()

    hbm = pl.BlockSpec(memory_space=pl.ANY)
    return pl.pallas_call(
        body, in_specs=[hbm, hbm, pl.BlockSpec(memory_space=pltpu.VMEM)], out_specs=[hbm] * 5,
        out_shape=(SDS((3, F, D), bf16), SDS((T, D), bf16),
                   SDS((DIN, D), bf16), SDS((DMIX, D), bf16), SDS((3, F, D), bf16)),
        scratch_shapes=[pltpu.VMEM((NORM_ROWS, D), f32), pltpu.VMEM((NORM_ROWS, D), bf16),
                        pltpu.VMEM((_group_rows(rest_pieces), D), bf16),
                        pltpu.SemaphoreType.DMA((11,)), pltpu.SemaphoreType.DMA((11,)), pltpu.SemaphoreType.DMA,
                        pltpu.SemaphoreType.DMA((2,))],
        compiler_params=pltpu.CompilerParams(has_side_effects=True),
        name="all_gather_ffn1")(shard, x, gain)


HBM_SPEC = pl.BlockSpec(memory_space=pltpu.HBM)
SEM_SPEC = pl.BlockSpec(memory_space=pltpu.SEMAPHORE)
ANY_SPEC = pl.BlockSpec(memory_space=pl.ANY)
SPLIT_EFFECT = pltpu.SideEffectType.DATAFLOW_SIDE_EFFECTING


def _in_hbm(a):
    return pltpu.with_memory_space_constraint(a, pltpu.HBM)


def _hbm_like(a):
    return pltpu.HBM(a.shape, a.dtype)


def _gather_rest_start(shard, wi, wo, w2, w1):
    def body(s_ref, wi_ref, wo_ref, w2_ref, w1_ref,
             ssem_m, rsem_m0, rsem_m, ssem_f, rsem_f0, rsem_f, s_o, wi_o, wo_o, w2_o, w1_o):
        x, y, c = _position()
        me, sib = (x, y, c), (x, y, 1 - c)
        chips = [(1 - x, y), (x, 1 - y), (1 - x, 1 - y)]
        arrs = _weight_pieces(wi_ref=wi_ref, wo_ref=wo_ref, w2_ref=w2_ref)
        for pieces, ssem, rsem0, rsem in ((MIX_PIECES, ssem_m, rsem_m0, rsem_m), (F2_PIECES, ssem_f, rsem_f0, rsem_f)):
            for p in pieces:
                pltpu.make_async_remote_copy(
                    src_ref=_shard_piece(s_ref, p), dst_ref=_block_rows(arrs, p, me), send_sem=ssem.at[0],
                    recv_sem=rsem0, device_id=sib, device_id_type=MESH).start()
            for j, chip in enumerate(chips):
                for p in pieces:
                    pltpu.make_async_remote_copy(
                        src_ref=_shard_piece(s_ref, p), dst_ref=_block_rows(arrs, p, me), send_sem=ssem.at[1 + j],
                        recv_sem=rsem.at[j], device_id=(*chip, c), device_id_type=MESH).start()

    dma = pltpu.SemaphoreType.DMA
    return pl.pallas_call(
        body, name="gather_rest_start",
        out_shape=(dma((4,)), dma(()), dma((3,)), dma((4,)), dma(()), dma((3,)),
                   _hbm_like(shard), _hbm_like(wi), _hbm_like(wo), _hbm_like(w2), _hbm_like(w1)),
        in_specs=(HBM_SPEC,) * 5, out_specs=(SEM_SPEC,) * 6 + (HBM_SPEC,) * 5,
        input_output_aliases={0: 6, 1: 7, 2: 8, 3: 9, 4: 10},
        compiler_params=pltpu.CompilerParams(has_side_effects=SPLIT_EFFECT),
    )(_in_hbm(shard), _in_hbm(wi), _in_hbm(wo), _in_hbm(w2), _in_hbm(w1))


def _gather_mix_pass_on(rsem_m, wi, wo, thru, after):
    def body(wi_ref, wo_ref, thru_ref, rsem, after_ref, fsend, frecv, wi_o, wo_o, thru_o):
        x, y, c = _position()
        sib = (x, y, 1 - c)
        arrs = _weight_pieces(wi_ref=wi_ref, wo_ref=wo_ref)
        both = wi_ref.at[pl.ds(0, _group_rows(MIX_PIECES)), :]
        for j, chip in enumerate([(1 - x, y), (x, 1 - y), (1 - x, 1 - y)]):
            pltpu.make_async_remote_copy(src_ref=both, dst_ref=both, send_sem=fsend.at[j], recv_sem=rsem.at[j],
                                         device_id=(x, y, c), device_id_type=MESH).wait_recv()
            for p in MIX_PIECES:
                rows = _block_rows(arrs, p, (*chip, c))
                pltpu.make_async_remote_copy(src_ref=rows, dst_ref=rows, send_sem=fsend.at[j], recv_sem=frecv.at[j],
                                             device_id=sib, device_id_type=MESH).start()

    dma = pltpu.SemaphoreType.DMA
    return pl.pallas_call(
        body, name="gather_mix_pass_on",
        out_shape=(dma((3,)), dma((3,)), _hbm_like(wi), _hbm_like(wo), _hbm_like(thru)),
        in_specs=(HBM_SPEC, HBM_SPEC, HBM_SPEC, SEM_SPEC, ANY_SPEC), out_specs=(SEM_SPEC, SEM_SPEC) + (HBM_SPEC,) * 3,
        input_output_aliases={0: 2, 1: 3, 2: 4},
        compiler_params=pltpu.CompilerParams(has_side_effects=SPLIT_EFFECT),
    )(wi, wo, _in_hbm(thru), rsem_m, after)


def _gather_mix_wait(ssem_m, rsem_m0, fsend, frecv, shard, wi, wo, after):
    def body(s_ref, wi_ref, wo_ref, ssem, rsem0, fs, fr, after_ref, s_o, wi_o, wo_o):
        x, y, c = _position()
        grp = _shard_group(s_ref, MIX_PIECES)

        def waiter(send_sem, recv_sem):
            return pltpu.make_async_remote_copy(src_ref=grp, dst_ref=grp, send_sem=send_sem, recv_sem=recv_sem,
                                                device_id=(x, y, c), device_id_type=MESH)

        waiter(ssem.at[0], rsem0).wait_recv()
        for j in range(3):
            waiter(fs.at[j], fr.at[j]).wait_recv()
        for rel in range(4):
            waiter(ssem.at[rel], rsem0).wait_send()
        for j in range(3):
            waiter(fs.at[j], fr.at[j]).wait_send()

    return pl.pallas_call(
        body, name="gather_mix_wait", out_shape=(_hbm_like(shard), _hbm_like(wi), _hbm_like(wo)),
        in_specs=(HBM_SPEC,) * 3 + (SEM_SPEC,) * 4 + (ANY_SPEC,), out_specs=(HBM_SPEC,) * 3,
        input_output_aliases={0: 0, 1: 1, 2: 2},
        compiler_params=pltpu.CompilerParams(has_side_effects=SPLIT_EFFECT),
    )(shard, wi, wo, ssem_m, rsem_m0, fsend, frecv, after)


def _gather_ffn2_pass_on(rsem_f, w2, wo, after):
    def body(w2_ref, wo_ref, rsem, after_ref, fsend, frecv, w2_o, wo_o):
        x, y, c = _position()
        sib = (x, y, 1 - c)
        chips = [(1 - x, y), (x, 1 - y), (1 - x, 1 - y)]
        arrs = _weight_pieces(w2_ref=w2_ref)
        three = w2_ref.at[0, pl.ds(0, _group_rows(F2_PIECES)), :]
        for j, chip in enumerate(chips):
            pltpu.make_async_remote_copy(src_ref=three, dst_ref=three, send_sem=fsend.at[j], recv_sem=rsem.at[j],
                                         device_id=(x, y, c), device_id_type=MESH).wait_recv()
            for p in F2_PIECES:
                rows = _block_rows(arrs, p, (*chip, c))
                pltpu.make_async_remote_copy(src_ref=rows, dst_ref=rows, send_sem=fsend.at[j], recv_sem=frecv.at[j],
                                             device_id=sib, device_id_type=MESH).start()

    dma = pltpu.SemaphoreType.DMA
    return pl.pallas_call(
        body, name="gather_ffn2_pass_on", out_shape=(dma((3,)), dma((3,)), _hbm_like(w2), _hbm_like(wo)),
        in_specs=(HBM_SPEC, HBM_SPEC, SEM_SPEC, ANY_SPEC), out_specs=(SEM_SPEC, SEM_SPEC, HBM_SPEC, HBM_SPEC),
        input_output_aliases={0: 2, 1: 3},
        compiler_params=pltpu.CompilerParams(has_side_effects=SPLIT_EFFECT),
    )(w2, wo, rsem_f, after)


def _gather_ffn2_wait(ssem_f, rsem_f0, fsend, frecv, shard, w2, after):
    def body(s_ref, w2_ref, ssem, rsem0, fs, fr, after_ref, w2_o):
        x, y, c = _position()
        grp = _shard_group(s_ref, F2_PIECES)

        def waiter(send_sem, recv_sem):
            return pltpu.make_async_remote_copy(src_ref=grp, dst_ref=grp, send_sem=send_sem, recv_sem=recv_sem,
                                                device_id=(x, y, c), device_id_type=MESH)

        waiter(ssem.at[0], rsem0).wait_recv()
        for j in range(3):
            waiter(fs.at[j], fr.at[j]).wait_recv()
        for rel in range(4):
            waiter(ssem.at[rel], rsem0).wait_send()
        for j in range(3):
            waiter(fs.at[j], fr.at[j]).wait_send()

    return pl.pallas_call(
        body, name="gather_ffn2_wait", out_shape=_hbm_like(w2),
        in_specs=(HBM_SPEC, HBM_SPEC, SEM_SPEC, SEM_SPEC, SEM_SPEC, SEM_SPEC, ANY_SPEC), out_specs=HBM_SPEC,
        input_output_aliases={1: 0},
        compiler_params=pltpu.CompilerParams(has_side_effects=SPLIT_EFFECT),
    )(shard, w2, ssem_f, rsem_f0, fsend, frecv, after)


class _GatheredWeights(_LocalWeights):
    def __init__(self, shard, x, gain1):
        w1, self.h1, wi, wo, w2 = _all_gather_ffn1(shard, x, gain1)
        (self.ssem_m, self.rsem_m0, self.rsem_m, self.ssem_f, self.rsem_f0, self.rsem_f,
         self.shard, self.wi, self.wo, self.w2_part, self.w1) = _gather_rest_start(shard, wi, wo, w2, w1)

    def first_norm(self, x, gain):
        return self.h1

    def after_ffn1(self, gain, x1):
        self.fsend_m, self.frecv_m, self.wi, self.wo, gain = _gather_mix_pass_on(self.rsem_m, self.wi, self.wo, gain, x1)
        return gain

    def mix(self, after):
        self.shard, wint, wout = _gather_mix_wait(self.ssem_m, self.rsem_m0, self.fsend_m, self.frecv_m, self.shard,
                                                  self.wi, self.wo, after)
        return wint, wout

    def before_out_proj(self, wout, after):
        self.fsend, self.frecv, self.w2_part, wout = _gather_ffn2_pass_on(self.rsem_f, self.w2_part, wout, after)
        return wout

    def ffn2(self, after):
        return _gather_ffn2_wait(self.ssem_f, self.rsem_f0, self.fsend, self.frecv, self.shard, self.w2_part, after)

    def out_ffn2_grads_ready(self, dwout, dw2, after):
        rx1 = lax.empty((4, RSA_ROWS, D), bf16)
        sa, ra, sent, rx1, after = _rsa_level1_start(dict(wo=dwout, w2=dw2), rx1, after, "rsa_level1_start_out_ffn2")
        self.level1 = ((sa, ra), sent, rx1)
        return after

    def before_ffn1_bwd(self, dwint, dx1b):
        early, sent, rx1 = self.level1
        sa, ra, late, rx1, dx1b = _rsa_level1_start(dict(wi=dwint), rx1, dx1b, "rsa_level1_start_in")
        started = (((MIX_PIECES[1],) + F2_PIECES, *early), ((MIX_PIECES[0],), sa, ra))
        rx2 = lax.empty((3, RSA_ROWS, D), bf16)
        self.sb, self.rb, self.tx, self.acc, self.rx2, dx1b = _rsa_sums_and_send(
            started, late["wi"], sent["wo"], sent["w2"], rx1, rx2, dx1b)
        return dx1b

    def mix_ffn2_grads_parts(self, after):
        rx2 = _rsa_level2_wait(self.sb, self.rb, self.tx, self.rx2, after)
        return self.acc, rx2


def _reduce_scatter_ffn1_head(dw1, small_packed, first_norm):
    pieces = G1_PIECES
    half = FS // 2
    hrows = len(pieces) * half
    nrows = 2 * hrows
    X_RELAY, Y_RELAY = range(2)

    T = first_norm[0].shape[0]

    def body(d1_ref, p_ref, dh_hbm, x_hbm, gain_ref, dr_hbm,
             forx_ref, fory_ref, own_ref, rx1_ref, relx_ref, rely_ref, gx_hbm, tot_ref,
             own_buf, rx_buf, tx1, tx2, tx3, acc, sa, ra, sb, rb, lsem, pair, chips, small_tot, small_send, small_recv,
             xbuf, rbuf, hbuf, obuf, norm_in_sems, norm_out_sems):
        x, y, c = _position()
        me, sib = (x, y, c), (x, y, 1 - c)
        xn, yn = (1 - x, y, c), (x, 1 - y, c)
        rel_chips = [(x, y), (1 - x, y), (x, 1 - y), (1 - x, 1 - y)]
        srcs = _weight_pieces(w1_ref=d1_ref)

        my_chip = 2 * x + y
        pair[c] = p_ref[...]
        swap = pltpu.make_async_remote_copy(
            src_ref=pair.at[c], dst_ref=pair.at[c], send_sem=small_send.at[0], recv_sem=small_recv.at[0],
            device_id=sib, device_id_type=MESH)
        mine = pl.ds(pl.multiple_of(c * (SMALL_ROWS // 2), 8), SMALL_ROWS // 2)
        small = [pltpu.make_async_remote_copy(
            src_ref=chips.at[my_chip, mine, :], dst_ref=chips.at[my_chip, mine, :], send_sem=small_send.at[j],
            recv_sem=small_recv.at[j], device_id=(*rel_chips[j], c), device_id_type=MESH) for j in (1, 2, 3)]
        give = pltpu.make_async_remote_copy(
            src_ref=small_tot.at[mine, :], dst_ref=small_tot.at[mine, :], send_sem=small_send.at[4],
            recv_sem=small_recv.at[4], device_id=sib, device_id_type=MESH)

        def part(k, dev, hf):
            r = PIECE_ROWS[k]
            return srcs[k].at[pl.ds(pl.multiple_of(_dev_index(*dev) * r + hf * half, 16), half), :]

        def slot(ref, k, hf):
            return ref.at[pl.ds(hf * hrows + k * half, half), :]

        halves = [(k, hf) for hf in (0, 1) for k in pieces]

        for j in (3, 1, 2, 0):
            for k, hf in halves:
                pltpu.make_async_remote_copy(
                    src_ref=part(k, (*rel_chips[j], 1 - c), hf), dst_ref=slot(rx1_ref.at[j], k, hf),
                    send_sem=sa.at[j], recv_sem=ra.at[j], device_id=sib, device_id_type=MESH).start()

        def wait_a(j):
            return pltpu.make_async_remote_copy(src_ref=rx1_ref.at[j], dst_ref=rx1_ref.at[j], send_sem=sa.at[j],
                                                recv_sem=ra.at[j], device_id=me, device_id_type=MESH)

        def ici(rel, src, dst, to):
            return pltpu.make_async_remote_copy(src_ref=src, dst_ref=dst, send_sem=sb.at[rel], recv_sem=rb.at[rel],
                                                device_id=to, device_id_type=MESH)

        first, second = pl.ds(0, hrows), pl.ds(hrows, hrows)
        sends = {
            X_RELAY: ici(X_RELAY, tx3.at[first, :], relx_ref, xn),
            Y_RELAY: ici(Y_RELAY, tx3.at[second, :], rely_ref, yn),
        }

        def chip_sum(j, dst):
            loads = [pltpu.make_async_copy(part(k, (*rel_chips[j], c), hf), slot(own_buf, k, hf), lsem.at[0])
                     for k, hf in halves]
            for cp in loads:
                cp.start()
            wait_a(j).wait_recv()
            got = pltpu.make_async_copy(rx1_ref.at[j], rx_buf, lsem.at[1])
            got.start()
            pltpu.make_async_copy(rx_buf, rx_buf, lsem.at[0]).wait()
            got.wait()

            def add(i, carry):
                rows = pl.ds(pl.multiple_of(i * half, 16), half)
                tot = own_buf[rows, :].astype(f32) + rx_buf[rows, :].astype(f32)
                dst[rows, :] = tot.astype(dst.dtype)
                return carry

            lax.fori_loop(0, nrows // half, add, 0)

        def add_landed(landed, dst, rows0, nrows_):
            got = pltpu.make_async_copy(landed, rx_buf.at[pl.ds(0, nrows_), :], lsem.at[1])
            got.start()
            got.wait()

            def add(i, carry):
                src_rows = pl.ds(pl.multiple_of(i * half, 16), half)
                dst_rows = pl.ds(pl.multiple_of(rows0 + i * half, 16), half)
                dst[dst_rows, :] = (dst[dst_rows, :].astype(f32) + rx_buf[src_rows, :].astype(f32)).astype(dst.dtype)
                return carry

            lax.fori_loop(0, nrows_ // half, add, 0)

        chip_sum(3, tx3)
        sends[X_RELAY].start()
        sends[Y_RELAY].start()
        dg = _norm_bwd_rows(T, dh_hbm, x_hbm, gain_ref, dr_hbm, gx_hbm.at[0], None,
                            xbuf, rbuf, hbuf, obuf, None, norm_in_sems, norm_out_sems)
        r0 = SMALL_OFF["ffn1_norm"]
        for k in range(D // 128):
            pair[c, r0 + k:r0 + k + 1, :] = dg[:, 128 * k:128 * (k + 1)]
        swap.start()
        swap.wait_recv()
        chips[my_chip] = pair[0] + pair[1]
        for cp in small:
            cp.start()
        chip_sum(1, tx1)
        chip_sum(2, tx2)
        chip_sum(0, acc)
        own_out = pltpu.make_async_copy(acc, own_ref, lsem.at[0])
        own_out.start()
        sends[X_RELAY].wait_recv()
        add_landed(relx_ref, tx2, 0, hrows)
        sends[Y_RELAY].wait_recv()
        add_landed(rely_ref, tx1, hrows, hrows)
        own_out.wait()
        outs = [pltpu.make_async_copy(tx1, forx_ref, lsem.at[0]), pltpu.make_async_copy(tx2, fory_ref, lsem.at[1])]
        for cp in outs:
            cp.start()
        for cp in outs:
            cp.wait()
        for cp in small:
            cp.wait_recv()
        small_tot[mine, :] = (chips[0, mine, :] + chips[1, mine, :]) + (chips[2, mine, :] + chips[3, mine, :])
        give.start()
        give.wait_recv()
        tot = small_tot[...]
        tot_ref[...] = tot
        loss = jnp.sum(tot[LOSS_ROW:LOSS_ROW + 1, :], axis=-1, keepdims=True)
        tot_ref[LOSS_ROW:LOSS_ROW + 1, :] = jnp.broadcast_to(loss, (1, 128))
        for j in range(4):
            wait_a(j).wait_send()
        for cp in sends.values():
            cp.wait_send()
        swap.wait_send()
        for cp in small + [give]:
            cp.wait_send()

    hbm = pl.BlockSpec(memory_space=pl.ANY)
    vm = pl.BlockSpec(memory_space=pltpu.VMEM)
    outs = pl.pallas_call(
        body, in_specs=[hbm, vm, hbm, hbm, vm, hbm], out_specs=[hbm] * 7 + [vm],
        out_shape=(SDS((nrows, D), bf16), SDS((nrows, D), bf16), SDS((nrows, D), f32), SDS((4, nrows, D), bf16),
                   SDS((hrows, D), bf16), SDS((hrows, D), bf16), SDS((1, T, D), f32), SDS((SMALL_ROWS, 128), f32)),
        scratch_shapes=[pltpu.VMEM((nrows, D), bf16), pltpu.VMEM((nrows, D), bf16),
                        pltpu.VMEM((nrows, D), bf16), pltpu.VMEM((nrows, D), bf16), pltpu.VMEM((nrows, D), bf16),
                        pltpu.VMEM((nrows, D), f32),
                        pltpu.SemaphoreType.DMA((4,)), pltpu.SemaphoreType.DMA((4,)),
                        pltpu.SemaphoreType.DMA((2,)), pltpu.SemaphoreType.DMA((2,)), pltpu.SemaphoreType.DMA((2,)),
                        pltpu.VMEM((2, SMALL_ROWS, 128), f32), pltpu.VMEM((4, SMALL_ROWS, 128), f32),
                        pltpu.VMEM((SMALL_ROWS, 128), f32),
                        pltpu.SemaphoreType.DMA((5,)), pltpu.SemaphoreType.DMA((5,))]
        + [sc for sc in _norm_bwd_scratch(True, False) if sc is not None],
        compiler_params=pltpu.CompilerParams(has_side_effects=True, vmem_limit_bytes=VMEM_LIMIT_V7X),
        name="reduce_scatter_ffn1_head")(dw1, small_packed, *first_norm)
    return outs[0], outs[1], outs[2], outs[-1], outs[-2]


def _rs1_tail_start(for_x, for_y, from_x, from_y, *thru):
    def body(fx_ref, fy_ref, lx_ref, ly_ref, *rest):
        ssem, rsem = rest[len(thru):len(thru) + 2]
        x, y, c = _position()
        pltpu.make_async_remote_copy(src_ref=fx_ref, dst_ref=lx_ref, send_sem=ssem.at[0], recv_sem=rsem.at[0],
                                     device_id=(1 - x, y, c), device_id_type=MESH).start()
        pltpu.make_async_remote_copy(src_ref=fy_ref, dst_ref=ly_ref, send_sem=ssem.at[1], recv_sem=rsem.at[1],
                                     device_id=(x, 1 - y, c), device_id_type=MESH).start()

    dma = pltpu.SemaphoreType.DMA
    arrs = (for_x, for_y, from_x, from_y, *thru)
    return pl.pallas_call(
        body, name="rs1_tail_start", out_shape=(dma((2,)), dma((2,))) + tuple(_hbm_like(a) for a in arrs),
        in_specs=(HBM_SPEC,) * len(arrs), out_specs=(SEM_SPEC,) * 2 + (HBM_SPEC,) * len(arrs),
        input_output_aliases={i: i + 2 for i in range(len(arrs))},
        compiler_params=pltpu.CompilerParams(has_side_effects=SPLIT_EFFECT),
    )(*[_in_hbm(a) for a in arrs])


RSA_PIECES = MIX_PIECES + F2_PIECES
RSA_ROWS = _group_rows(RSA_PIECES)
RSA_OFF = {k: PIECE_OFF[k] - PIECE_OFF[RSA_PIECES[0]] for k in RSA_PIECES}
RSA_BLOCK = 192


def _rsa_rows(ref, k):
    return ref.at[pl.ds(RSA_OFF[k], PIECE_ROWS[k]), :]


def _rsa_level1_start(grads, rx1, thru, name):
    keys = sorted(grads)
    n = len(keys)

    def body(*refs):
        srcs = _weight_pieces(**{k + "_ref": ref for k, ref in zip(keys, refs[:n])})
        rx1_ref, sa, ra = refs[n], refs[n + 2], refs[n + 3]
        x, y, c = _position()
        for j, chip in enumerate([(x, y), (1 - x, y), (x, 1 - y), (1 - x, 1 - y)]):
            for k in sorted(srcs):
                pltpu.make_async_remote_copy(
                    src_ref=_block_rows(srcs, k, (*chip, 1 - c)), dst_ref=_rsa_rows(rx1_ref.at[j], k),
                    send_sem=sa.at[j], recv_sem=ra.at[j], device_id=(x, y, 1 - c), device_id_type=MESH).start()

    dma = pltpu.SemaphoreType.DMA
    arrs = tuple(grads[k] for k in keys) + (rx1, thru)
    outs = pl.pallas_call(
        body, name=name, out_shape=(dma((4,)), dma((4,))) + tuple(_hbm_like(a) for a in arrs),
        in_specs=(HBM_SPEC,) * len(arrs), out_specs=(SEM_SPEC,) * 2 + (HBM_SPEC,) * len(arrs),
        input_output_aliases={i: i + 2 for i in range(len(arrs))},
        compiler_params=pltpu.CompilerParams(has_side_effects=SPLIT_EFFECT),
    )(*[_in_hbm(a) for a in arrs])
    return outs[0], outs[1], dict(zip(keys, outs[2:2 + n])), outs[2 + n], outs[3 + n]


def _rsa_sums_and_send(started, dwint, dwout, dw2, rx1, rx2, thru):
    nblk = RSA_ROWS // RSA_BLOCK
    nstart = len(started)

    def body(*refs):
        di_ref, do_ref, d2_ref, rx1_ref, rx2_ref = refs[:5]
        l1_sems = refs[6:6 + 2 * nstart]
        sb, rb, tx_ref, acc_ref = refs[6 + 2 * nstart:10 + 2 * nstart]
        own_buf, rx_buf, tx_buf, acc_buf, in_sems, out_sems = refs[13 + 2 * nstart:]
        x, y, c = _position()
        srcs = _weight_pieces(wi_ref=di_ref, wo_ref=do_ref, w2_ref=d2_ref)
        chips = [(x, y), (1 - x, y), (x, 1 - y), (1 - x, 1 - y)]

        for g, (pieces, _, _) in enumerate(started):
            ssem, rsem = l1_sems[2 * g], l1_sems[2 * g + 1]
            for j in range(4):
                rows = rx1_ref.at[j, pl.ds(RSA_OFF[pieces[0]], _group_rows(pieces)), :]
                d = pltpu.make_async_remote_copy(src_ref=rows, dst_ref=rows, send_sem=ssem.at[j], recv_sem=rsem.at[j],
                                                 device_id=(x, y, c), device_id_type=MESH)
                d.wait_recv()
                d.wait_send()

        def start_loads(j):
            s = j % 2
            for k in RSA_PIECES:
                pltpu.make_async_copy(_block_rows(srcs, k, (*chips[j], c)), _rsa_rows(own_buf.at[s], k),
                                      in_sems.at[2 * s]).start()
            pltpu.make_async_copy(rx1_ref.at[j], rx_buf.at[s], in_sems.at[2 * s + 1]).start()

        def wait_loads(j):
            s = j % 2
            pltpu.make_async_copy(rx1_ref.at[j], own_buf.at[s], in_sems.at[2 * s]).wait()
            pltpu.make_async_copy(rx1_ref.at[j], rx_buf.at[s], in_sems.at[2 * s + 1]).wait()

        def store(j):
            if j == 0:
                return pltpu.make_async_copy(acc_buf, acc_ref, out_sems.at[2])
            return pltpu.make_async_copy(tx_buf.at[j % 2], tx_ref.at[j - 1], out_sems.at[j % 2])

        def send(j):
            return pltpu.make_async_remote_copy(src_ref=tx_ref.at[j - 1], dst_ref=rx2_ref.at[j - 1], send_sem=sb.at[j - 1],
                                                recv_sem=rb.at[j - 1], device_id=(*chips[j], c), device_id_type=MESH)

        start_loads(0)
        for j in range(4):
            s = j % 2
            if j + 1 < 4:
                start_loads(j + 1)
            wait_loads(j)
            if j == 3:
                store(1).wait()
                send(1).start()

            def add(i, carry, j=j, s=s):
                rows = pl.ds(pl.multiple_of(i * RSA_BLOCK, 16), RSA_BLOCK)
                tot = own_buf[s, rows, :].astype(f32) + rx_buf[s, rows, :].astype(f32)
                if j == 0:
                    acc_buf[rows, :] = tot
                else:
                    tx_buf[s, rows, :] = tot.astype(bf16)
                return carry

            lax.fori_loop(0, nblk, add, 0)
            store(j).start()
        store(0).wait()
        for j in (2, 3):
            store(j).wait()
            send(j).start()

    dma = pltpu.SemaphoreType.DMA
    passed = (rx1, rx2, thru)
    outs = pl.pallas_call(
        body, name="rsa_sums_and_send",
        in_specs=(HBM_SPEC,) * 6 + (SEM_SPEC,) * (2 * nstart),
        out_specs=(SEM_SPEC,) * 2 + (HBM_SPEC,) * 5,
        out_shape=(dma((3,)), dma((3,)), pltpu.HBM((3, RSA_ROWS, D), bf16), pltpu.HBM((RSA_ROWS, D), f32))
        + tuple(_hbm_like(a) for a in passed),
        input_output_aliases={3: 4, 4: 5, 5: 6},
        scratch_shapes=[pltpu.VMEM((2, RSA_ROWS, D), bf16), pltpu.VMEM((2, RSA_ROWS, D), bf16),
                        pltpu.VMEM((2, RSA_ROWS, D), bf16), pltpu.VMEM((RSA_ROWS, D), f32),
                        dma((4,)), dma((3,))],
        compiler_params=pltpu.CompilerParams(has_side_effects=SPLIT_EFFECT, vmem_limit_bytes=VMEM_LIMIT_V7X),
    )(*[_in_hbm(a) for a in (dwint, dwout, dw2) + passed], *[sem for _, sa, ra in started for sem in (sa, ra)])
    sb, rb, tx, acc, _, rx2, thru = outs
    return sb, rb, tx, acc, rx2, thru


def _rsa_level2_wait(sb, rb, tx, rx2, after):
    def body(tx_ref, rx2_ref, sb_ref, rb_ref, after_ref, rx2_o):
        x, y, c = _position()
        for j in range(3):
            d = pltpu.make_async_remote_copy(src_ref=tx_ref.at[j], dst_ref=rx2_ref.at[j], send_sem=sb_ref.at[j],
                                             recv_sem=rb_ref.at[j], device_id=(x, y, c), device_id_type=MESH)
            d.wait_recv()
            d.wait_send()

    return pl.pallas_call(
        body, name="rsa_level2_wait", out_shape=_hbm_like(rx2),
        in_specs=(HBM_SPEC, HBM_SPEC, SEM_SPEC, SEM_SPEC, ANY_SPEC), out_specs=HBM_SPEC,
        input_output_aliases={1: 0},
        compiler_params=pltpu.CompilerParams(has_side_effects=SPLIT_EFFECT),
    )(tx, rx2, sb, rb, after)


def _adamw_math(w, g, m, v):
    m = ADAM_B1 * m + (1.0 - ADAM_B1) * g
    v = ADAM_B2 * v + (1.0 - ADAM_B2) * (g * g)
    m_hat = m / (1.0 - ADAM_B1 ** ADAM_STEP)
    v_hat = v / (1.0 - ADAM_B2 ** ADAM_STEP)
    delta = -ADAM_LR * (m_hat / (jnp.sqrt(v_hat) + ADAM_EPS) + ADAM_WD * w)
    return delta, m, v


def _adamw_big(pieces, ws, ms, vs, own, landed, name, in_flight=None):
    npiece = len(pieces)
    nsent = len(landed) if in_flight else 0
    nland = sum(a.shape[0] if a.ndim == 3 else 1 for a in landed)
    rmax = max(PIECE_ROWS[k] for k in pieces)
    half = FS // 2

    def segments(k):
        if k in G1_PIECES:
            return [(hf * len(G1_PIECES) * half + k * half, hf * half, half) for hf in (0, 1)]
        return [(RSA_OFF[k], 0, PIECE_ROWS[k])]

    def body(*refs):
        ins = (refs[0:npiece], refs[npiece:2 * npiece], refs[2 * npiece:3 * npiece])
        own_ref = refs[3 * npiece]
        nin = 3 * npiece + 1 + len(landed)
        land_refs = []
        for ref, a in zip(refs[3 * npiece + 1:nin], landed):
            land_refs += [ref.at[j] for j in range(a.shape[0])] if a.ndim == 3 else [ref]
        if in_flight:
            sent_refs, (ssem, rsem) = refs[nin:nin + nsent], refs[nin + nsent:nin + nsent + 2]
            nin += nsent + 3
        out_refs = refs[nin:nin + 4 * npiece]
        inb, landb, outb, in_sems, land_sems, out_sems = refs[nin + 4 * npiece + nsent:]

        def loads(i):
            s, k = i % 2, pieces[i]
            r = PIECE_ROWS[k]
            cps = [pltpu.make_async_copy(ins[q][i].at[0], inb.at[s, q, pl.ds(0, r), :], in_sems.at[4 * s + q])
                   for q in range(3)]
            waits, late = list(cps), []
            for src0, dst0, n in segments(k):
                cps.append(pltpu.make_async_copy(own_ref.at[pl.ds(src0, n), :], inb.at[s, 3, pl.ds(dst0, n), :],
                                                 in_sems.at[4 * s + 3]))
                for p in range(nland):
                    late.append(pltpu.make_async_copy(land_refs[p].at[pl.ds(src0, n), :],
                                                      landb.at[s, p, pl.ds(dst0, n), :], land_sems.at[nland * s + p]))
            own_rows = inb.at[s, 3, pl.ds(0, r), :]
            waits.append(pltpu.make_async_copy(own_rows, own_rows, in_sems.at[4 * s + 3]))
            for p in range(nland):
                rows = landb.at[s, p, pl.ds(0, r), :]
                waits.append(pltpu.make_async_copy(rows, rows, land_sems.at[nland * s + p]))
            return cps, late, waits

        def stores(i):
            s, r = i % 2, PIECE_ROWS[pieces[i]]
            return [pltpu.make_async_copy(outb.at[s, q, pl.ds(0, r), :], out_refs[q * npiece + i].at[0],
                                          out_sems.at[4 * s + q]) for q in range(4)]

        ahead = min(2, npiece) if in_flight else 1
        for i in range(ahead):
            for cp in loads(i)[0]:
                cp.start()
        if in_flight:
            x, y, c = _position()
            for j in range(nsent):
                d = pltpu.make_async_remote_copy(src_ref=sent_refs[j], dst_ref=refs[3 * npiece + 1 + j],
                                                 send_sem=ssem.at[j], recv_sem=rsem.at[j], device_id=(x, y, c),
                                                 device_id_type=MESH)
                d.wait_recv()
                d.wait_send()
        for cp in loads(0)[1]:
            cp.start()
        for i in range(npiece):
            s, r = i % 2, PIECE_ROWS[pieces[i]]
            if i + 1 < npiece:
                first, late, _ = loads(i + 1)
                for cp in late if i + 1 < ahead else first + late:
                    cp.start()
            for cp in loads(i)[2]:
                cp.wait()
            if i >= 2:
                for cp in stores(i - 2):
                    cp.wait()
            g = inb[s, 3, 0:r, :]
            for p in range(nland):
                g = g + landb[s, p, 0:r, :].astype(f32)
            d, nm, nv = _adamw_math(inb[s, 0, 0:r, :], g, inb[s, 1, 0:r, :], inb[s, 2, 0:r, :])
            outb[s, 0, 0:r, :] = g
            outb[s, 1, 0:r, :] = d
            outb[s, 2, 0:r, :] = nm
            outb[s, 3, 0:r, :] = nv
            for cp in stores(i):
                cp.start()
        for i in range(max(npiece - 2, 0), npiece):
            for cp in stores(i):
                cp.wait()

    hbm = pl.BlockSpec(memory_space=pl.ANY)
    in_specs, out_specs = [hbm] * (3 * npiece + 1), [hbm] * (4 * npiece)
    out_shape = [SDS(w.shape, f32) for _ in range(4) for w in ws]
    args, aliases, effect = [*ws, *ms, *vs, own], {}, False
    if in_flight:
        ssem, rsem, sent, after = in_flight
        in_specs += [HBM_SPEC] * (2 * nsent) + [SEM_SPEC, SEM_SPEC, hbm]
        args += [_in_hbm(a) for a in (*landed, *sent)] + [ssem, rsem, after]
        out_specs += [HBM_SPEC] * nsent
        out_shape += [_hbm_like(a) for a in landed]
        aliases = {3 * npiece + 1 + j: 4 * npiece + j for j in range(nsent)}
        effect = SPLIT_EFFECT
    else:
        in_specs += [hbm] * len(landed)
        args += list(landed)
    outs = pl.pallas_call(
        body, in_specs=in_specs, out_specs=out_specs, out_shape=tuple(out_shape), input_output_aliases=aliases,
        scratch_shapes=[pltpu.VMEM((2, 4, rmax, D), f32), pltpu.VMEM((2, nland, rmax, D), bf16),
                        pltpu.VMEM((2, 4, rmax, D), f32),
                        pltpu.SemaphoreType.DMA((8,)), pltpu.SemaphoreType.DMA((2 * nland,)),
                        pltpu.SemaphoreType.DMA((8,))],
        compiler_params=pltpu.CompilerParams(has_side_effects=effect, vmem_limit_bytes=VMEM_LIMIT_V7X),
        name=name)(*args)
    return [list(outs[q * npiece:(q + 1) * npiece]) for q in range(4)]


def _adamw_small(ws, ms, vs, gs, name):
    n = len(ws)

    def body(*refs):
        w_refs, m_refs, v_refs, g_refs = refs[0:n], refs[n:2 * n], refs[2 * n:3 * n], refs[3 * n:4 * n]
        outs = refs[4 * n:]
        for i in range(n):
            d, nm, nv = _adamw_math(w_refs[i][...], g_refs[i][...], m_refs[i][...], v_refs[i][...])
            outs[i][...] = d
            outs[n + i][...] = nm
            outs[2 * n + i][...] = nv

    outs = pl.pallas_call(
        body, out_shape=tuple(SDS(w.shape, f32) for _ in range(3) for w in ws), name=name)(*ws, *ms, *vs, *gs)
    return [list(outs[q * n:(q + 1) * n]) for q in range(3)]


WEIGHTS = ("ffn1_norm", "ffn1_w_gate", "ffn1_w_up", "ffn1_w_down", "mix_norm", "w_in", "q_norm", "k_norm",
           "attn_sinks", "rel_bias", "pool_w", "pool_scale", "w_out", "ffn2_norm", "ffn2_w_gate", "ffn2_w_up",
           "ffn2_w_down")
BIG = (("ffn1_w_gate", True), ("ffn1_w_up", True), ("ffn1_w_down", False), ("w_in", True), ("w_out", False),
       ("ffn2_w_gate", True), ("ffn2_w_up", True), ("ffn2_w_down", False))


def kernel(x, ffn1_norm, ffn1_w_gate, ffn1_w_up, ffn1_w_down, mix_norm, w_in, q_norm, k_norm, attn_sinks, rel_bias, pool_w, pool_scale, w_out, ffn2_norm, ffn2_w_gate, ffn2_w_up, ffn2_w_down, loss_target, m_ffn1_norm, m_ffn1_w_gate, m_ffn1_w_up, m_ffn1_w_down, m_mix_norm, m_w_in, m_q_norm, m_k_norm, m_attn_sinks, m_rel_bias, m_pool_w, m_pool_scale, m_w_out, m_ffn2_norm, m_ffn2_w_gate, m_ffn2_w_up, m_ffn2_w_down, v_ffn1_norm, v_ffn1_w_gate, v_ffn1_w_up, v_ffn1_w_down, v_mix_norm, v_w_in, v_q_norm, v_k_norm, v_attn_sinks, v_rel_bias, v_pool_w, v_pool_scale, v_w_out, v_ffn2_norm, v_ffn2_w_gate, v_ffn2_w_up, v_ffn2_w_down):
    args = dict(locals())
    w = {n: args[n] for n in WEIGHTS}
    m = {n: args["m_" + n] for n in WEIGHTS}
    v = {n: args["v_" + n] for n in WEIGHTS}

    as_rows = lambda a, tr: jnp.swapaxes(a, 1, 2) if tr else a
    shard = jnp.concatenate([as_rows(w[n], tr)[0].astype(bf16) for n, tr in BIG], axis=0)
    exchanges = _GatheredWeights(shard, x[0], ffn1_norm)
    (dh1, dx1), (dw1, _, _, _), small = _local_step(
        x[0], loss_target[0], exchanges, ffn1_norm, mix_norm, ffn2_norm, q_norm, k_norm, attn_sinks,
        rel_bias, pool_w[0], pool_scale)

    nrows1 = len(G1_PIECES) * FS
    for_x, for_y, own1, small_tot, gx = _reduce_scatter_ffn1_head(dw1, _pack_small(small),
                                                                  (dh1, x[0], ffn1_norm, dx1))
    ssem, rsem, for_x, for_y, from_x, from_y, small_tot, gx = _rs1_tail_start(
        for_x, for_y, lax.empty((nrows1, D), bf16), lax.empty((nrows1, D), bf16), small_tot, gx)
    gx = _fresh_copy(gx, "grad_x_out")
    own_rest, landed_rest = exchanges.mix_ffn2_grads_parts(gx)

    grads, deltas, new_m, new_v = {}, {}, {}, {}
    rest = [k for k in range(len(BIG)) if k not in G1_PIECES]
    rows_of = lambda t, ks: [as_rows(t[BIG[k][0]], BIG[k][1]) for k in ks]
    rest_out = _adamw_big(rest, rows_of(w, rest), rows_of(m, rest), rows_of(v, rest), own_rest, [landed_rest],
                          "adamw_rest")
    ffn1 = list(G1_PIECES)
    ffn1_out = _adamw_big(ffn1, rows_of(w, ffn1), rows_of(m, ffn1), rows_of(v, ffn1), own1, [from_x, from_y],
                          "adamw_ffn1", in_flight=(ssem, rsem, [for_x, for_y], rest_out[0][0]))
    for ks, out in ((rest, rest_out), (ffn1, ffn1_out)):
        for i, k in enumerate(ks):
            n, tr = BIG[k]
            grads[n], deltas[n], new_m[n], new_v[n] = [as_rows(o[i], tr) for o in out]
    small_names = [n for n in SMALL_NAMES if n != "loss"]
    for n in small_names:
        grads[n] = _unpack_small(small_tot, n)
    ds, nms, nvs = _adamw_small([w[n] for n in small_names], [m[n] for n in small_names], [v[n] for n in small_names],
                                [grads[n] for n in small_names], "adamw_small")
    for i, n in enumerate(small_names):
        deltas[n], new_m[n], new_v[n] = ds[i], nms[i], nvs[i]
    loss = small_tot[LOSS_ROW, 0]
    return (loss, gx, *[grads[n] for n in WEIGHTS], *[deltas[n] for n in WEIGHTS],
            *[new_m[n] for n in WEIGHTS], *[new_v[n] for n in WEIGHTS])
```

```python
import jax
import jax.numpy as jnp
import numpy as np
from jax import lax
from jax.experimental import pallas as pl
from jax.experimental.pallas import tpu as pltpu

f32, bf16, i32 = jnp.float32, jnp.bfloat16, jnp.int32
SDS = jax.ShapeDtypeStruct

D = 1024
F = 2816
HD = 64
NH = 8
NKV = 2
GQA = NH // NKV
DATTN = NH * HD
DKV = NKV * HD
DPOOL = 512
POOL_WINDOWS = (2, 4, 8, 16)
PGD = DPOOL // len(POOL_WINDOWS)
DIN = DATTN + 2 * DKV + DPOOL
DMIX = DATTN + DPOOL
BLK = 128
NBUCK = 32
MAX_DISTANCE = 128
EPS = 1e-6
NEG = -1e30
SCALE = HD ** -0.5

ADAM_LR, ADAM_B1, ADAM_B2, ADAM_EPS, ADAM_WD, ADAM_STEP = 0.001, 0.9, 0.999, 1e-08, 0.01, 10

NDEV = 8
FS = F // NDEV
INS = DIN // NDEV
OUTS = DMIX // NDEV
PIECE_ROWS = (FS, FS, FS, INS, OUTS, FS, FS, FS)
PIECE_OFF = tuple(int(v) for v in np.cumsum((0,) + PIECE_ROWS[:-1]))
PACK_ROWS = sum(PIECE_ROWS)

VMEM_LIMIT_V7X = 56 * 1024 * 1024

MESH = pl.DeviceIdType.MESH


def _cparams(sem=None, vmem=None):
    return pltpu.CompilerParams(dimension_semantics=sem, vmem_limit_bytes=vmem)


def _nt(a, b):
    return lax.dot_general(a, b, (((1,), (1,)), ((), ())), preferred_element_type=f32)


def _tn(a, b):
    return lax.dot_general(a, b, (((0,), (0,)), ((), ())), preferred_element_type=f32)


def _nn(a, b):
    return jnp.dot(a, b, preferred_element_type=f32)


def _sigmoid(x):
    return 1.0 / (1.0 + jnp.exp(-x))


def _fresh_copy(a, name):
    T = a.shape[1]
    tm = min(512, T)

    def body(a_ref, o_ref):
        o_ref[...] = a_ref[...]

    tok = pl.BlockSpec((1, tm, D), lambda i: (0, i, 0))
    return pl.pallas_call(body, grid=(T // tm,), in_specs=[tok], out_specs=tok, out_shape=SDS(a.shape, a.dtype),
                          name=name)(a)


def _norm_fwd(x, g, name):
    T = x.shape[0]
    tm = min(512, T)

    def body(x_ref, g_ref, h_ref):
        xv = x_ref[...]
        r = lax.rsqrt(jnp.mean(xv * xv, axis=-1, keepdims=True) + EPS)
        h_ref[...] = (xv * r * g_ref[...]).astype(bf16)

    return pl.pallas_call(
        body, grid=(T // tm,),
        in_specs=[pl.BlockSpec((tm, D), lambda i: (i, 0)), pl.BlockSpec((1, D), lambda i: (0, 0))],
        out_specs=pl.BlockSpec((tm, D), lambda i: (i, 0)),
        out_shape=SDS((T, D), bf16), name=name)(x, g)


FFN_ROW_CHUNK = 256


def _ffn_tiles(T):
    return min(1024, T), 256


def _ffn_fwd(h, w, x, target, next_gain, name):
    T = h.shape[0]
    tm, tf = _ffn_tiles(T)
    nf = F // tf
    with_loss = target is not None
    assert with_loss != (next_gain is not None)

    def body(*refs):
        if with_loss:
            h_ref, w_ref, x_hbm, t_hbm, xo_ref, g_ref, u_ref, dyb_ref, loss_ref, tbuf, sem = refs
        else:
            h_ref, w_ref, x_hbm, gain_ref, xo_ref, g_ref, u_ref, hn_ref, sem = refs
        fi = pl.program_id(0)

        def x_rows(r):
            return pltpu.make_async_copy(x_hbm.at[pl.ds(r, tm), :], xo_ref.at[pl.ds(r, tm), :], sem.at[r // tm])

        def tile(first):
            if first:
                for r in range(0, T, tm):
                    x_rows(r).start()
            wgu = w_ref[0:2].reshape(2 * tf, D)
            for r in range(0, T, tm):
                rows = slice(r, r + tm)
                gu = _nt(h_ref[rows, :], wgu)
                gate, up = gu[:, :tf], gu[:, tf:]
                act = gate * _sigmoid(gate) * up
                g_ref[0, rows, :] = gate.astype(bf16)
                u_ref[0, rows, :] = up.astype(bf16)
                down = _nn((0.5 * act).astype(bf16), w_ref[2])
                if first:
                    x_rows(r).wait()
                xo_ref[rows, :] += down

        @pl.when(fi == 0)
        def _():
            tile(True)

        @pl.when(fi > 0)
        def _():
            tile(False)

        if with_loss:
            @pl.when(fi == nf - 1)
            def _():
                lanes = jnp.zeros((1, 128), f32)
                for r in range(0, T, tm):
                    rows = slice(r, r + tm)
                    cp = pltpu.make_async_copy(t_hbm.at[pl.ds(r, tm), :], tbuf, sem.at[0])
                    cp.start()
                    cp.wait()
                    e = xo_ref[rows, :] - tbuf[...]
                    dy = e * (1.0 / D)
                    xo_ref[rows, :] = dy
                    dyb_ref[rows, :] = (0.5 * dy).astype(bf16)
                    col = jnp.sum(e * e, axis=0, keepdims=True) * (0.5 / D)
                    for k in range(D // 128):
                        lanes = lanes + col[:, 128 * k:128 * (k + 1)]
                loss_ref[...] = lanes
        else:
            @pl.when(fi == nf - 1)
            def _():
                for r in range(0, T, FFN_ROW_CHUNK):
                    rows = slice(r, r + FFN_ROW_CHUNK)
                    xv = xo_ref[rows, :]
                    rstd = lax.rsqrt(jnp.mean(xv * xv, axis=-1, keepdims=True) + EPS)
                    hn_ref[rows, :] = (xv * rstd * gain_ref[...]).astype(bf16)

    tok = pl.BlockSpec((T, D), lambda f: (0, 0))
    act_spec = pl.BlockSpec((1, T, tf), lambda f: (f, 0, 0))
    hbm = pl.BlockSpec(memory_space=pl.ANY)
    in_specs = [tok, pl.BlockSpec((3, tf, D), lambda f: (0, f, 0)), hbm]
    out_specs = [tok, act_spec, act_spec]
    out_shape = [SDS((T, D), f32), SDS((nf, T, tf), bf16), SDS((nf, T, tf), bf16)]
    scratch = [pltpu.SemaphoreType.DMA((T // tm,))]
    args = [h, w, x]
    if with_loss:
        in_specs.append(hbm)
        args.append(target)
        out_specs += [tok, pl.BlockSpec((1, 128), lambda f: (0, 0))]
        out_shape += [SDS((T, D), bf16), SDS((1, 128), f32)]
        scratch = [pltpu.VMEM((tm, D), f32)] + scratch
    else:
        in_specs.append(pl.BlockSpec((1, D), lambda f: (0, 0)))
        args.append(next_gain)
        out_specs.append(tok)
        out_shape.append(SDS((T, D), bf16))
    return pl.pallas_call(
        body, grid=(nf,), in_specs=in_specs, out_specs=out_specs, out_shape=tuple(out_shape), scratch_shapes=scratch,
        compiler_params=_cparams(("arbitrary",), VMEM_LIMIT_V7X), name=name)(*args)


NORM_BWD_ROWS = 512


def _norm_bwd_scratch(dh_in_hbm, with_bf16):
    buf = lambda dt: pltpu.VMEM((2, NORM_BWD_ROWS, D), dt)
    return [buf(f32), buf(f32), buf(f32) if dh_in_hbm else None, buf(f32), buf(bf16) if with_bf16 else None,
            pltpu.SemaphoreType.DMA((6,)), pltpu.SemaphoreType.DMA((4,))]


def _norm_bwd_rows(T, dh_src, x_hbm, gain_ref, dr_hbm, dx_hbm, dxb_hbm, xbuf, rbuf, hbuf, obuf, obb, in_sems, out_sems):
    tm = min(NORM_BWD_ROWS, T)
    nchunk = T // tm

    def loads(i):
        s, rows = i % 2, pl.ds(i * tm, tm)
        cps = [pltpu.make_async_copy(x_hbm.at[rows, :], xbuf.at[s, pl.ds(0, tm), :], in_sems.at[3 * s]),
               pltpu.make_async_copy(dr_hbm.at[rows, :], rbuf.at[s, pl.ds(0, tm), :], in_sems.at[3 * s + 1])]
        if hbuf is not None:
            cps.append(pltpu.make_async_copy(dh_src.at[rows, :], hbuf.at[s, pl.ds(0, tm), :], in_sems.at[3 * s + 2]))
        return cps

    def stores(i):
        s, rows = i % 2, pl.ds(i * tm, tm)
        cps = [pltpu.make_async_copy(obuf.at[s, pl.ds(0, tm), :], dx_hbm.at[rows, :], out_sems.at[2 * s])]
        if dxb_hbm is not None:
            cps.append(pltpu.make_async_copy(obb.at[s, pl.ds(0, tm), :], dxb_hbm.at[rows, :], out_sems.at[2 * s + 1]))
        return cps

    for cp in loads(0):
        cp.start()
    dg = jnp.zeros((1, D), f32)
    for i in range(nchunk):
        s = i % 2
        if i + 1 < nchunk:
            for cp in loads(i + 1):
                cp.start()
        for cp in loads(i):
            cp.wait()
        if i >= 2:
            for cp in stores(i - 2):
                cp.wait()
        xv = xbuf[s, 0:tm, :]
        rstd = lax.rsqrt(jnp.mean(xv * xv, axis=-1, keepdims=True) + EPS)
        xh = xv * rstd
        dhv = hbuf[s, 0:tm, :] if hbuf is not None else dh_src[i * tm:(i + 1) * tm, :]
        dxh = dhv * gain_ref[...]
        dx = rbuf[s, 0:tm, :] + rstd * (dxh - xh * jnp.mean(dxh * xh, axis=-1, keepdims=True))
        obuf[s, 0:tm, :] = dx
        if dxb_hbm is not None:
            obb[s, 0:tm, :] = dx.astype(bf16)
        dg = dg + jnp.sum(dhv * xh, axis=0, keepdims=True)
        for cp in stores(i):
            cp.start()
    for i in range(max(nchunk - 2, 0), nchunk):
        for cp in stores(i):
            cp.wait()
    return dg


def _ffn_bwd(dob, h, gate, up, w, norm, name):
    T = h.shape[0]
    _, tf = _ffn_tiles(T)
    nf = F // tf
    nin = 5 if norm is None else 8
    nout = 2 if norm is None else 4

    def body(*refs):
        do_hbm, h_hbm, g_ref, u_ref, w_ref = refs[:5]
        dw_ref = refs[nin + nout - 1]
        do_v, h_v, dh_acc, dgu_s, act_s, sems, do_sems = refs[nin + nout:nin + nout + 7]
        fi = pl.program_id(0)
        h_load = pltpu.make_async_copy(h_hbm, h_v, sems.at[1])

        def do_rows(r):
            rows = pl.ds(r, FFN_ROW_CHUNK)
            return pltpu.make_async_copy(do_hbm.at[rows, :], do_v.at[rows, :], do_sems.at[r // FFN_ROW_CHUNK])

        def tile(first):
            if first:
                for r in range(0, T, FFN_ROW_CHUNK):
                    do_rows(r).start()
                h_load.start()
                dh_acc[...] = jnp.zeros_like(dh_acc)
            wgu = w_ref[0:2].reshape(2 * tf, D)
            for r in range(0, T, FFN_ROW_CHUNK):
                rows = slice(r, r + FFN_ROW_CHUNK)
                if first:
                    do_rows(r).wait()
                dov = do_v[rows, :]
                gv = g_ref[0, rows, :].astype(f32)
                uv = u_ref[0, rows, :].astype(f32)
                sg = _sigmoid(gv)
                sil = gv * sg
                dact = _nt(dov, w_ref[2])
                dup = dact * sil
                dgate = dact * uv * (sg * (1.0 + gv * (1.0 - sg)))
                dgu = jnp.concatenate([dgate.astype(bf16), dup.astype(bf16)], axis=1)
                dgu_s[rows, :] = dgu
                act_s[rows, :] = (sil * uv).astype(bf16)
                dh_acc[rows, :] += _nn(dgu, wgu)
            if first:
                h_load.wait()
            dw_ref[0:2] = _tn(dgu_s[...], h_v[...]).reshape(2, tf, D).astype(bf16)
            dw_ref[2] = _tn(act_s[...], do_v[...]).astype(bf16)

        @pl.when(fi == 0)
        def _():
            tile(True)

        @pl.when(fi > 0)
        def _():
            tile(False)

        @pl.when(fi == nf - 1)
        def _():
            if norm is None:
                out = pltpu.make_async_copy(dh_acc, refs[nin], sems.at[0])
                out.start()
                out.wait()
            else:
                x_hbm, gain_ref, dr_hbm, dx_hbm, dxb_hbm, dg_ref = refs[5:11]
                xbuf, rbuf, obuf, obb, in_sems, out_sems = refs[nin + nout + 7:]
                dg_ref[...] = _norm_bwd_rows(T, dh_acc, x_hbm, gain_ref, dr_hbm, dx_hbm, dxb_hbm,
                                             xbuf, rbuf, None, obuf, obb, in_sems, out_sems)

    act_spec = pl.BlockSpec((1, T, tf), lambda f: (f, 0, 0))
    wspec = pl.BlockSpec((3, tf, D), lambda f: (0, f, 0))
    vec = pl.BlockSpec((1, D), lambda f: (0, 0))
    hbm = pl.BlockSpec(memory_space=pl.ANY)
    in_specs, out_specs = [hbm, hbm, act_spec, act_spec, wspec], [hbm, wspec]
    out_shape, args = [SDS((T, D), f32), SDS((3, F, D), bf16)], [dob, h, gate, up, w]
    scratch = [pltpu.VMEM((T, D), bf16), pltpu.VMEM((T, D), bf16), pltpu.VMEM((T, D), f32),
               pltpu.VMEM((T, 2 * tf), bf16), pltpu.VMEM((T, tf), bf16), pltpu.SemaphoreType.DMA((2,)),
               pltpu.SemaphoreType.DMA((T // FFN_ROW_CHUNK,))]
    if norm is not None:
        in_specs += [hbm, vec, hbm]
        out_specs = [hbm, hbm, vec, wspec]
        out_shape = [SDS((T, D), f32), SDS((T, D), bf16), SDS((1, D), f32), SDS((3, F, D), bf16)]
        args += list(norm)
        scratch += [sc for sc in _norm_bwd_scratch(False, True) if sc is not None]
    return pl.pallas_call(
        body, grid=(nf,), in_specs=in_specs, out_specs=out_specs, out_shape=tuple(out_shape), scratch_shapes=scratch,
        compiler_params=_cparams(("arbitrary",), VMEM_LIMIT_V7X), name=name)(*args)


def _in_proj_fwd(h, wint, name):
    T = h.shape[0]
    tm = min(512, T)

    def body(h_ref, w_ref, z_ref):
        z_ref[...] = _nt(h_ref[...], w_ref[...])

    return pl.pallas_call(
        body, grid=(T // tm,),
        in_specs=[pl.BlockSpec((tm, D), lambda i: (i, 0)), pl.BlockSpec((DIN, D), lambda i: (0, 0))],
        out_specs=pl.BlockSpec((tm, DIN), lambda i: (i, 0)),
        out_shape=SDS((T, DIN), f32), name=name)(h, wint)


def _in_proj_bwd(dz, wint, h, norm, out_scale, name):
    x, g, dres = norm
    T = h.shape[0]
    tm = min(512, T)
    nt = T // tm

    def body(dz_ref, w_ref, h_ref, x_ref, g_ref, dr_ref, dx_ref, dxb_ref, dg_ref, dw_ref, acc):
        i = pl.program_id(0)
        dzb = dz_ref[...].astype(bf16)
        dhv = _nn(dzb, w_ref[...])
        part = _tn(dzb, h_ref[...])
        xv = x_ref[...]
        rstd = lax.rsqrt(jnp.mean(xv * xv, axis=-1, keepdims=True) + EPS)
        xh = xv * rstd
        dxh = dhv * g_ref[...]
        dx = dr_ref[...] + rstd * (dxh - xh * jnp.mean(dxh * xh, axis=-1, keepdims=True))
        dx_ref[...] = dx
        dxb_ref[...] = (out_scale * dx).astype(bf16)
        dg = jnp.sum(dhv * xh, axis=0, keepdims=True)

        @pl.when(i == 0)
        def _():
            acc[...] = part
            dg_ref[...] = dg

        @pl.when(i > 0)
        def _():
            acc[...] += part
            dg_ref[...] += dg

        @pl.when(i == nt - 1)
        def _():
            dw_ref[...] = acc[...].astype(bf16)

    wspec = pl.BlockSpec((DIN, D), lambda i: (0, 0))
    tok = pl.BlockSpec((tm, D), lambda i: (i, 0))
    vec = pl.BlockSpec((1, D), lambda i: (0, 0))
    return pl.pallas_call(
        body, grid=(nt,),
        in_specs=[pl.BlockSpec((tm, DIN), lambda i: (i, 0)), wspec, tok, tok, vec, tok],
        out_specs=[tok, tok, vec, wspec],
        out_shape=(SDS((T, D), f32), SDS((T, D), bf16), SDS((1, D), f32), SDS((DIN, D), bf16)),
        scratch_shapes=[pltpu.VMEM((DIN, D), f32)],
        compiler_params=_cparams(("arbitrary",)), name=name)(dz, wint, h, x, g, dres)


def _out_proj_fwd(ymix, wout, x, g, name):
    T = x.shape[0]
    tm = min(512, T)

    def body(y_ref, w_ref, x_ref, g_ref, o_ref, h_ref):
        o = x_ref[...] + _nn(y_ref[...], w_ref[...])
        o_ref[...] = o
        r = lax.rsqrt(jnp.mean(o * o, axis=-1, keepdims=True) + EPS)
        h_ref[...] = (o * r * g_ref[...]).astype(bf16)

    tok = pl.BlockSpec((tm, D), lambda i: (i, 0))
    return pl.pallas_call(
        body, grid=(T // tm,),
        in_specs=[pl.BlockSpec((tm, DMIX), lambda i: (i, 0)), pl.BlockSpec((DMIX, D), lambda i: (0, 0)), tok,
                  pl.BlockSpec((1, D), lambda i: (0, 0))],
        out_specs=[tok, tok], out_shape=(SDS((T, D), f32), SDS((T, D), bf16)), name=name)(ymix, wout, x, g)


def _out_proj_bwd(dxb, wout, ymix, name):
    T = dxb.shape[0]
    tm = min(512, T)
    nt = T // tm

    def body(dx_ref, w_ref, y_ref, dy_ref, dw_ref, acc):
        i = pl.program_id(0)
        dxv = dx_ref[...]
        dy_ref[...] = _nt(dxv, w_ref[...])
        part = _tn(y_ref[...], dxv)

        @pl.when(i == 0)
        def _():
            acc[...] = part

        @pl.when(i > 0)
        def _():
            acc[...] += part

        @pl.when(i == nt - 1)
        def _():
            dw_ref[...] = acc[...].astype(bf16)

    wspec = pl.BlockSpec((DMIX, D), lambda i: (0, 0))
    return pl.pallas_call(
        body, grid=(nt,),
        in_specs=[pl.BlockSpec((tm, D), lambda i: (i, 0)), wspec, pl.BlockSpec((tm, DMIX), lambda i: (i, 0))],
        out_specs=[pl.BlockSpec((tm, DMIX), lambda i: (i, 0)), wspec],
        out_shape=(SDS((T, DMIX), f32), SDS((DMIX, D), bf16)),
        scratch_shapes=[pltpu.VMEM((DMIX, D), f32)],
        compiler_params=_cparams(("arbitrary",)), name=name)(dxb, wout, ymix)


def _t5_bucket_table():
    ql = np.arange(BLK)[:, None]
    kl = np.arange(2 * BLK)[None, :]
    n = np.maximum(ql + BLK - kl, 0)
    max_exact = NBUCK // 2
    large = max_exact + (np.log(np.maximum(n, 1) / max_exact) / np.log(MAX_DISTANCE / max_exact)
                         * (NBUCK - max_exact)).astype(np.int32)
    large = np.minimum(large, NBUCK - 1)
    return np.where(n < max_exact, n, large).astype(np.int32)


def _fill_bias(bk_ref, rb_ref, bias_scr):
    bk = bk_ref[...]
    for h in range(NH):
        def step(b, acc, h=h):
            return acc + jnp.where(bk == b, rb_ref[b, h], 0.0)
        bias_scr[h] = lax.fori_loop(0, NBUCK, step, jnp.zeros((BLK, 2 * BLK), f32))


MIX_SUB = 4


class _Window:
    def __init__(self, zc_ref, zp_ref, n, s):
        self.blk = n * MIX_SUB + s
        self.cur = lambda a, b: zc_ref[s * BLK:(s + 1) * BLK, a:b]
        self.prev = (lambda a, b: zp_ref[:, a:b]) if s == 0 else (lambda a, b: zc_ref[(s - 1) * BLK:s * BLK, a:b])


def _attn_qkv(win, kh, qg, kg):
    kc = DATTN + HD * kh
    vc = DATTN + DKV + HD * kh
    kx = jnp.concatenate([win.prev(kc, kc + HD), win.cur(kc, kc + HD)], axis=0)
    vx = jnp.concatenate([win.prev(vc, vc + HD), win.cur(vc, vc + HD)], axis=0)
    qx = jnp.concatenate([win.cur(HD * (GQA * kh + g), HD * (GQA * kh + g + 1)) for g in range(GQA)], axis=0)
    rq = lax.rsqrt(jnp.mean(qx * qx, axis=-1, keepdims=True) + EPS)
    rk = lax.rsqrt(jnp.mean(kx * kx, axis=-1, keepdims=True) + EPS)
    qhat, khat = qx * rq, kx * rk
    return dict(qhat=qhat, khat=khat, rq=rq, rk=rk, qsb=(qhat * (qg * SCALE)).astype(bf16),
                knb=(khat * kg).astype(bf16), vb=vx.astype(bf16))


def _window_masks(n):
    row = lax.broadcasted_iota(i32, (GQA * BLK, 2 * BLK), 0) & (BLK - 1)
    col = lax.broadcasted_iota(i32, (GQA * BLK, 2 * BLK), 1)
    band = (col > row) & (col <= row + BLK)
    return band & ((col >= BLK) | (n > 0)), band


def _attn_probs(a, kh, sk_ref, bias_scr, mask):
    s = _nt(a["qsb"], a["knb"]) + bias_scr[GQA * kh:GQA * (kh + 1)].reshape(GQA * BLK, 2 * BLK)
    s = jnp.where(mask, s, NEG)
    ridx = lax.broadcasted_iota(i32, (GQA * BLK, 1), 0)
    sink = jnp.full((GQA * BLK, 1), sk_ref[GQA * kh + GQA - 1], f32)
    for g in range(GQA - 2, -1, -1):
        sink = jnp.where(ridx < (g + 1) * BLK, sk_ref[GQA * kh + g], sink)
    m = jnp.maximum(jnp.max(s, axis=-1, keepdims=True), sink)
    e = jnp.exp(s - m)
    den = jnp.sum(e, axis=-1, keepdims=True) + jnp.exp(sink - m)
    return e / den


POOL_STEPS = {2: (1,), 4: (1, 2), 8: (1, 2, 4), 16: (1, 2, 4, 8)}


def _pool_group(win, g, w):
    n = win.blk
    c0 = DATTN + 2 * DKV + PGD * g
    uc = win.cur(c0, c0 + PGD)
    up = jnp.where(n > 0, win.prev(c0, c0 + PGD), 0.0)
    sm = jnp.concatenate([up, uc], axis=0)
    for k in POOL_STEPS[w]:
        sm = sm + pltpu.roll(sm, k, axis=0)
    pos = n * BLK + lax.broadcasted_iota(i32, (BLK, 1), 0) + 1
    cnt = jnp.minimum(pos, w).astype(f32)
    return sm[BLK:2 * BLK] / cnt - uc, cnt


def _mix_fwd(z, qg, kg, sinks, relb, bucket, pool_w, pscale, name):
    T = z.shape[0]
    step_rows = MIX_SUB * BLK
    nsteps = T // step_rows

    def body(zc_ref, zp_ref, qg_ref, kg_ref, sk_ref, rb_ref, bk_ref, pw_ref, ps_ref, y_ref, p_ref, bias_scr, yacc):
        n = pl.program_id(0)

        @pl.when(n == 0)
        def _():
            _fill_bias(bk_ref, rb_ref, bias_scr)

        first_mask, mask = _window_masks(n)
        for s in range(MIX_SUB):
            win = _Window(zc_ref, zp_ref, n, s)
            rows = slice(s * BLK, (s + 1) * BLK)
            for kh in range(NKV):
                a = _attn_qkv(win, kh, qg_ref[...], kg_ref[...])
                pb = _attn_probs(a, kh, sk_ref, bias_scr, first_mask if s == 0 else mask).astype(bf16)
                p_ref[s, GQA * kh:GQA * (kh + 1)] = pb.reshape(GQA, BLK, 2 * BLK)
                o = _nn(pb, a["vb"])
                for g in range(GQA):
                    hc = HD * (GQA * kh + g)
                    yacc[rows, hc:hc + HD] = o[g * BLK:(g + 1) * BLK]
            for g, w in enumerate(POOL_WINDOWS):
                pooled, _ = _pool_group(win, g, w)
                yp = _nn(pooled.astype(bf16), pw_ref[g].astype(bf16)) * ps_ref[:, PGD * g:PGD * (g + 1)]
                yacc[rows, DATTN + PGD * g:DATTN + PGD * (g + 1)] = yp
        y_ref[...] = yacc[...].astype(bf16)

    full = lambda *shape: pl.BlockSpec(shape, lambda n: (0,) * len(shape))
    smem = pl.BlockSpec(memory_space=pltpu.SMEM)
    return pl.pallas_call(
        body, grid=(nsteps,),
        in_specs=[pl.BlockSpec((step_rows, DIN), lambda n: (n, 0)),
                  pl.BlockSpec((BLK, DIN), lambda n: (jnp.maximum(n * MIX_SUB - 1, 0), 0)),
                  full(1, HD), full(1, HD), smem, smem, full(BLK, 2 * BLK),
                  full(len(POOL_WINDOWS), PGD, PGD), full(1, DPOOL)],
        out_specs=[pl.BlockSpec((step_rows, DMIX), lambda n: (n, 0)),
                   pl.BlockSpec((MIX_SUB, NH, BLK, 2 * BLK), lambda n: (n, 0, 0, 0))],
        out_shape=(SDS((T, DMIX), bf16), SDS((T // BLK, NH, BLK, 2 * BLK), bf16)),
        scratch_shapes=[pltpu.VMEM((NH, BLK, 2 * BLK), f32), pltpu.VMEM((step_rows, DMIX), f32)],
        compiler_params=_cparams(("arbitrary",)), name=name)(z, z, qg, kg, sinks, relb, bucket, pool_w, pscale)


def _mix_bwd(z, dy, probs, qg, kg, relb, bucket, pool_w, pscale, name):
    T = z.shape[0]
    step_rows = MIX_SUB * BLK
    nsteps = T // step_rows

    def body(zc_ref, zp_ref, dy_ref, p_ref, qg_ref, kg_ref, bk_ref, pw_ref, ps_ref,
             dz_ref, dqg_ref, dkg_ref, dsk_ref, drb_ref, dpw_ref, dps_ref, dbias_scr):
        n = pl.program_id(0)

        @pl.when(n == 0)
        def _():
            dbias_scr[...] = jnp.zeros_like(dbias_scr)
            dqg_ref[...] = jnp.zeros_like(dqg_ref)
            dkg_ref[...] = jnp.zeros_like(dkg_ref)
            dpw_ref[...] = jnp.zeros_like(dpw_ref)
            dps_ref[...] = jnp.zeros_like(dps_ref)

        qg, kg = qg_ref[...], kg_ref[...]
        for s in range(MIX_SUB):
            win = _Window(zc_ref, zp_ref, n, s)
            blk = win.blk
            rows = pl.ds(pl.multiple_of(blk * BLK, BLK), BLK)
            prow = pl.ds(pl.multiple_of(jnp.maximum(blk - 1, 0) * BLK, BLK), BLK)
            dyr = slice(s * BLK, (s + 1) * BLK)

            def into_prev(fn, s=s):
                if s == 0:
                    pl.when(n > 0)(fn)
                else:
                    fn()

            for kh in range(NKV):
                a = _attn_qkv(win, kh, qg, kg)
                pb = p_ref[s, GQA * kh:GQA * (kh + 1)].reshape(GQA * BLK, 2 * BLK)
                p = pb.astype(f32)
                do = jnp.concatenate([dy_ref[dyr, HD * (GQA * kh + g):HD * (GQA * kh + g + 1)] for g in range(GQA)],
                                     axis=0).astype(bf16)
                dv = _tn(pb, do)
                dp = _nt(do, a["vb"])
                delta = jnp.sum(p * dp, axis=-1, keepdims=True)
                ds = p * (dp - delta)
                for g in range(GQA):
                    dbias_scr[GQA * kh + g] += ds[g * BLK:(g + 1) * BLK]
                dsb = ds.astype(bf16)
                dqn = _nn(dsb, a["knb"]) * SCALE
                dkn = _tn(dsb, a["qsb"])
                qhat, khat = a["qhat"], a["khat"]
                dqg_ref[...] += jnp.sum(dqn * qhat, axis=0, keepdims=True)
                dkg_ref[...] += jnp.sum(dkn * khat, axis=0, keepdims=True)
                dqh = dqn * qg
                dq = a["rq"] * (dqh - qhat * jnp.mean(dqh * qhat, axis=-1, keepdims=True))
                dkh = dkn * kg
                dk = a["rk"] * (dkh - khat * jnp.mean(dkh * khat, axis=-1, keepdims=True))
                kc = DATTN + HD * kh
                vc = DATTN + DKV + HD * kh
                for g in range(GQA):
                    hc = HD * (GQA * kh + g)
                    dz_ref[rows, hc:hc + HD] = dq[g * BLK:(g + 1) * BLK]
                dz_ref[rows, kc:kc + HD] = dk[BLK:2 * BLK]
                dz_ref[rows, vc:vc + HD] = dv[BLK:2 * BLK]

                def kv_prev(dk=dk, dv=dv, kc=kc, vc=vc, prow=prow):
                    dz_ref[prow, kc:kc + HD] += dk[0:BLK]
                    dz_ref[prow, vc:vc + HD] += dv[0:BLK]

                into_prev(kv_prev)

            for g, w in enumerate(POOL_WINDOWS):
                c0 = DATTN + 2 * DKV + PGD * g
                pooled, cnt = _pool_group(win, g, w)
                pb = pooled.astype(bf16)
                wb = pw_ref[g].astype(bf16)
                dyp = dy_ref[dyr, DATTN + PGD * g:DATTN + PGD * (g + 1)]
                ypre = _nn(pb, wb)
                dps_ref[:, PGD * g:PGD * (g + 1)] += jnp.sum(dyp * ypre, axis=0, keepdims=True)
                dyg = (dyp * ps_ref[:, PGD * g:PGD * (g + 1)]).astype(bf16)
                dpw_ref[g] += _tn(pb, dyg)
                dpooled = _nt(dyg, wb)
                due = jnp.concatenate([jnp.zeros((BLK, PGD), f32), dpooled / cnt], axis=0)
                for k in POOL_STEPS[w]:
                    due = due + pltpu.roll(due, 2 * BLK - k, axis=0)
                dz_ref[rows, c0:c0 + PGD] = due[BLK:2 * BLK] - dpooled

                def pool_prev(due=due, c0=c0, prow=prow):
                    dz_ref[prow, c0:c0 + PGD] += due[0:BLK]

                into_prev(pool_prev)

        @pl.when(n == nsteps - 1)
        def _():
            bk = bk_ref[...]
            ri = lax.broadcasted_iota(i32, (NBUCK, NH), 0)
            ci = lax.broadcasted_iota(i32, (NBUCK, NH), 1)

            def step(b, acc):
                for h in range(NH):
                    sel = jnp.where(bk == b, dbias_scr[h], 0.0)
                    tot = jnp.sum(jnp.sum(sel, axis=1, keepdims=True), axis=0, keepdims=True)
                    acc = acc + jnp.where((ri == b) & (ci == h), tot, 0.0)
                return acc

            drb_ref[...] = lax.fori_loop(0, NBUCK, step, jnp.zeros((NBUCK, NH), f32))
            lane = lax.broadcasted_iota(i32, (1, 128), 1)
            dsk = jnp.zeros((1, 128), f32)
            for h in range(NH):
                tot = jnp.sum(jnp.sum(dbias_scr[h], axis=1, keepdims=True), axis=0, keepdims=True)
                dsk = dsk - jnp.where(lane == h, tot, 0.0)
            dsk_ref[...] = dsk

    full = lambda *shape: pl.BlockSpec(shape, lambda n: (0,) * len(shape))
    npg = len(POOL_WINDOWS)
    return pl.pallas_call(
        body, grid=(nsteps,),
        in_specs=[pl.BlockSpec((step_rows, DIN), lambda n: (n, 0)),
                  pl.BlockSpec((BLK, DIN), lambda n: (jnp.maximum(n * MIX_SUB - 1, 0), 0)),
                  pl.BlockSpec((step_rows, DMIX), lambda n: (n, 0)),
                  pl.BlockSpec((MIX_SUB, NH, BLK, 2 * BLK), lambda n: (n, 0, 0, 0)),
                  full(1, HD), full(1, HD), full(BLK, 2 * BLK), full(npg, PGD, PGD), full(1, DPOOL)],
        out_specs=[full(T, DIN), full(1, HD), full(1, HD), full(1, 128), full(NBUCK, NH),
                   full(npg, PGD, PGD), full(1, DPOOL)],
        out_shape=(SDS((T, DIN), f32), SDS((1, HD), f32), SDS((1, HD), f32), SDS((1, 128), f32),
                   SDS((NBUCK, NH), f32), SDS((npg, PGD, PGD), f32), SDS((1, DPOOL), f32)),
        scratch_shapes=[pltpu.VMEM((NH, BLK, 2 * BLK), f32)],
        compiler_params=_cparams(("arbitrary",), VMEM_LIMIT_V7X),
        name=name)(z, z, dy, probs, qg, kg, bucket, pool_w, pscale)


class _LocalWeights:
    def __init__(self, w1, wint, wout, w2):
        self.w1, self.wint, self.wout, self.w2 = w1, wint, wout, w2

    def ffn1(self):
        return self.w1

    def first_norm(self, x, gain):
        return _norm_fwd(x, gain, "norm1_fwd")

    def after_ffn1(self, gain, x1):
        return gain

    def mix(self, after):
        return self.wint, self.wout

    def before_out_proj(self, wout, after):
        return wout

    def ffn2(self, after):
        return self.w2

    def out_ffn2_grads_ready(self, dwout, dw2, after):
        return after

    def before_ffn1_bwd(self, dwint, dx1b):
        return dx1b


def _local_step(x, target, weights, g1, gm, g3, qg, kg, sinks, relb, pool_w, pscale):
    bucket = jnp.asarray(_t5_bucket_table())
    sk = sinks.reshape(NH)
    w1 = weights.ffn1()
    h1 = weights.first_norm(x, g1)
    x1, gate1, up1, h2 = _ffn_fwd(h1, w1, x, None, gm, "ffn1_fwd")
    gm = weights.after_ffn1(gm, x1)
    wint, wout = weights.mix(h2)
    z = _in_proj_fwd(h2, wint, "in_proj_fwd")
    ymix, probs = _mix_fwd(z, qg, kg, sk, relb, bucket, pool_w, pscale, "mix_fwd")
    wout = weights.before_out_proj(wout, ymix)
    x2, h3 = _out_proj_fwd(ymix, wout, x1, g3, "out_proj_fwd")
    w2 = weights.ffn2(h3)
    dy, gate2, up2, dyb, loss_lanes = _ffn_fwd(h3, w2, x2, target, None, "ffn2_fwd")

    dx2, dx2b, dg3, dw2 = _ffn_bwd(dyb, h3, gate2, up2, w2, (x2, g3, dy), "ffn2_bwd")
    dymix, dwout = _out_proj_bwd(dx2b, wout, ymix, "out_proj_bwd")
    dymix = weights.out_ffn2_grads_ready(dwout, dw2, dymix)
    dz, dqg, dkg, dsk, drb, dpw, dps = _mix_bwd(z, dymix, probs, qg, kg, relb, bucket, pool_w, pscale, "mix_bwd")
    dx1, dx1b, dgm, dwint = _in_proj_bwd(dz, wint, h2, (x1, gm, dx2), 0.5, "in_proj_bwd")
    dx1b = weights.before_ffn1_bwd(dwint, dx1b)
    dh1, dw1 = _ffn_bwd(dx1b, h1, gate1, up1, w1, None, "ffn1_bwd")
    small = dict(mix_norm=dgm, ffn2_norm=dg3, pool_scale=dps, q_norm=dqg, k_norm=dkg,
                 attn_sinks=dsk[:, :NH], rel_bias=drb, pool_w=dpw, loss=loss_lanes)
    return (dh1, dx1), (dw1, dwint, dwout, dw2), small


SMALL_NAMES = ("ffn1_norm", "mix_norm", "ffn2_norm", "pool_scale", "q_norm", "k_norm", "attn_sinks", "rel_bias",
               "pool_w", "loss")
SMALL_SHAPES = dict(ffn1_norm=(1, D), mix_norm=(1, D), ffn2_norm=(1, D), pool_scale=(1, DPOOL), q_norm=(1, HD),
                    k_norm=(1, HD), attn_sinks=(1, NH), rel_bias=(NBUCK, NH),
                    pool_w=(1, len(POOL_WINDOWS), PGD, PGD), loss=(1, 128))


def _small_rows(name):
    return -(-int(np.prod(SMALL_SHAPES[name])) // 128)


SMALL_OFF = {}
_r = 0
for _n in SMALL_NAMES:
    SMALL_OFF[_n] = _r
    _r += _small_rows(_n)
SMALL_ROWS = -(-_r // 16) * 16
LOSS_ROW = SMALL_OFF["loss"]


def _pack_small(vals):
    parts = []
    for n in SMALL_NAMES:
        size = _small_rows(n) * 128
        if n in vals:
            flat = vals[n].astype(f32).reshape(-1)
            parts.append(jnp.pad(flat, (0, size - flat.shape[0])))
        else:
            parts.append(jnp.zeros((size,), f32))
    flat = jnp.concatenate(parts)
    flat = jnp.pad(flat, (0, SMALL_ROWS * 128 - flat.shape[0]))
    return flat.reshape(SMALL_ROWS, 128)


def _unpack_small(packed, name):
    size = int(np.prod(SMALL_SHAPES[name]))
    r0 = SMALL_OFF[name]
    return packed[r0:r0 + _small_rows(name)].reshape(-1)[:size].reshape(SMALL_SHAPES[name])


def _position():
    return lax.axis_index("x"), lax.axis_index("y"), lax.axis_index("c")


def _dev_index(x, y, c):
    return 4 * x + 2 * y + c


G1_PIECES, MIX_PIECES, F2_PIECES = (0, 1, 2), (3, 4), (5, 6, 7)


def _group_rows(pieces):
    return sum(PIECE_ROWS[k] for k in pieces)


def _shard_piece(s_ref, k):
    return s_ref.at[pl.ds(PIECE_OFF[k], PIECE_ROWS[k]), :]


def _shard_group(s_ref, pieces):
    return s_ref.at[pl.ds(PIECE_OFF[pieces[0]], _group_rows(pieces)), :]


def _weight_pieces(w1_ref=None, wi_ref=None, wo_ref=None, w2_ref=None):
    arrs = {}
    if w1_ref is not None:
        arrs.update({0: w1_ref.at[0], 1: w1_ref.at[1], 2: w1_ref.at[2]})
    if wi_ref is not None:
        arrs[3] = wi_ref
    if wo_ref is not None:
        arrs[4] = wo_ref
    if w2_ref is not None:
        arrs.update({5: w2_ref.at[0], 6: w2_ref.at[1], 7: w2_ref.at[2]})
    return arrs


def _block_rows(arrs, k, dev):
    r = PIECE_ROWS[k]
    return arrs[k].at[pl.ds(pl.multiple_of(_dev_index(*dev) * r, 16), r), :]


NORM_ROWS = 512


def _all_gather_ffn1(shard, x, gain):
    pieces = G1_PIECES
    rest_pieces = MIX_PIECES + F2_PIECES
    half = FS // 2
    T = x.shape[0]
    SIB, X0, X1, Y0, Y1, RELAY_Y, RELAY_X, ON_X, ON_Y, ON_D0, ON_D1 = range(11)

    def body(s_ref, x_ref, g_ref, w1_ref, h_ref, wi_ref, wo_ref, w2_ref, xbuf, hbuf, rest_buf,
             send_sems, recv_sems, local_sem, norm_sems):
        x, y, c = _position()
        me, sib = (x, y, c), (x, y, 1 - c)
        xn, yn, dg = (1 - x, y, c), (x, 1 - y, c), (1 - x, 1 - y, c)
        arrs = _weight_pieces(w1_ref=w1_ref)

        def place_rest():
            rest = _weight_pieces(wi_ref=wi_ref, wo_ref=wo_ref, w2_ref=w2_ref)
            grp = _shard_group(s_ref, rest_pieces)
            load = pltpu.make_async_copy(grp, rest_buf, norm_sems.at[0])
            load.start()
            load.wait()
            base = PIECE_OFF[rest_pieces[0]]
            for k in rest_pieces:
                pltpu.make_async_copy(rest_buf.at[pl.ds(PIECE_OFF[k] - base, PIECE_ROWS[k]), :],
                                      _block_rows(rest, k, me), norm_sems.at[1]).start()
            pltpu.make_async_copy(grp, rest_buf, norm_sems.at[1]).wait()

        def first_norm():
            for r in range(0, T, NORM_ROWS):
                load = pltpu.make_async_copy(x_ref.at[pl.ds(r, NORM_ROWS), :], xbuf, norm_sems.at[0])
                load.start()
                load.wait()
                xv = xbuf[...]
                rs = lax.rsqrt(jnp.mean(xv * xv, axis=-1, keepdims=True) + EPS)
                hbuf[...] = (xv * rs * g_ref[...]).astype(bf16)
                store = pltpu.make_async_copy(hbuf, h_ref.at[pl.ds(r, NORM_ROWS), :], norm_sems.at[1])
                store.start()
                store.wait()

        def rows_of(k, block, hf):
            r = PIECE_ROWS[k]
            start, size = (0, r) if hf is None else (hf * half, half)
            return arrs[k].at[pl.ds(pl.multiple_of(_dev_index(*block) * r + start, 16), size), :]

        def copies(rel, block, hf, to, from_shard=False):
            def src(k):
                if not from_shard:
                    return rows_of(k, block, hf)
                start, size = (0, PIECE_ROWS[k]) if hf is None else (hf * half, half)
                return s_ref.at[pl.ds(PIECE_OFF[k] + start, size), :]
            return [pltpu.make_async_remote_copy(
                src_ref=src(k), dst_ref=rows_of(k, block, hf), send_sem=send_sems.at[rel], recv_sem=recv_sems.at[rel],
                device_id=to, device_id_type=MESH) for k in pieces]

        def waiter(rel, hf):
            nrows = len(pieces) * (FS if hf is None else half)
            grp = s_ref.at[pl.ds(0, nrows), :]
            return pltpu.make_async_remote_copy(src_ref=grp, dst_ref=grp, send_sem=send_sems.at[rel],
                                                recv_sem=recv_sems.at[rel], device_id=me, device_id_type=MESH)

        def start(cps):
            for cp in cps:
                cp.start()

        mine = [pltpu.make_async_copy(_shard_piece(s_ref, k), _block_rows(arrs, k, me), local_sem) for k in pieces]
        start(mine)
        start(copies(SIB, me, None, sib, True))
        start(copies(X0, me, 0, xn, True))
        start(copies(Y1, me, 1, yn, True))
        start(copies(X1, me, 1, xn, True))
        start(copies(Y0, me, 0, yn, True))
        first_norm()
        place_rest()
        waiter(X0, 0).wait_recv()
        start(copies(RELAY_Y, xn, 0, yn))
        waiter(Y1, 1).wait_recv()
        start(copies(RELAY_X, yn, 1, xn))
        waiter(X1, 1).wait_recv()
        start(copies(ON_X, xn, None, sib))
        waiter(Y0, 0).wait_recv()
        start(copies(ON_Y, yn, None, sib))
        waiter(RELAY_Y, 0).wait_recv()
        start(copies(ON_D0, dg, 0, sib))
        waiter(RELAY_X, 1).wait_recv()
        start(copies(ON_D1, dg, 1, sib))
        waiter(SIB, None).wait_recv()
        waiter(ON_X, None).wait_recv()
        waiter(ON_Y, None).wait_recv()
        waiter(ON_D0, 0).wait_recv()
        waiter(ON_D1, 1).wait_recv()
        for rel, hf in ((SIB, None), (X0, 0), (X1, 1), (Y0, 0), (Y1, 1), (RELAY_Y, 0), (RELAY_X, 1),
                        (ON_X, None), (ON_Y, None), (ON_D0, 0), (ON_D1, 1)):
            waiter(rel, hf).wait_send()
        grp = _shard_group(s_ref, pieces)
        pltpu.make_async_copy(grp, grp, local_sem).wait()

    hbm = pl.BlockSpec(memory_space=pl.ANY)
    return pl.pallas_call(
        body, in_specs=[hbm, hbm, pl.BlockSpec(memory_space=pltpu.VMEM)], out_specs=[hbm] * 5,
        out_shape=(SDS((3, F, D), bf16), SDS((T, D), bf16),
                   SDS((DIN, D), bf16), SDS((DMIX, D), bf16), SDS((3, F, D), bf16)),
        scratch_shapes=[pltpu.VMEM((NORM_ROWS, D), f32), pltpu.VMEM((NORM_ROWS, D), bf16),
                        pltpu.VMEM((_group_rows(rest_pieces), D), bf16),
                        pltpu.SemaphoreType.DMA((11,)), pltpu.SemaphoreType.DMA((11,)), pltpu.SemaphoreType.DMA,
                        pltpu.SemaphoreType.DMA((2,))],
        compiler_params=pltpu.CompilerParams(has_side_effects=True),
        name="all_gather_ffn1")(shard, x, gain)


HBM_SPEC = pl.BlockSpec(memory_space=pltpu.HBM)
SEM_SPEC = pl.BlockSpec(memory_space=pltpu.SEMAPHORE)
ANY_SPEC = pl.BlockSpec(memory_space=pl.ANY)
SPLIT_EFFECT = pltpu.SideEffectType.DATAFLOW_SIDE_EFFECTING


def _in_hbm(a):
    return pltpu.with_memory_space_constraint(a, pltpu.HBM)


def _hbm_like(a):
    return pltpu.HBM(a.shape, a.dtype)


def _gather_rest_start(shard, wi, wo, w2, w1):
    def body(s_ref, wi_ref, wo_ref, w2_ref, w1_ref,
             ssem_m, rsem_m0, rsem_m, ssem_f, rsem_f0, rsem_f, s_o, wi_o, wo_o, w2_o, w1_o):
        x, y, c = _position()
        me, sib = (x, y, c), (x, y, 1 - c)
        chips = [(1 - x, y), (x, 1 - y), (1 - x, 1 - y)]
        arrs = _weight_pieces(wi_ref=wi_ref, wo_ref=wo_ref, w2_ref=w2_ref)
        for pieces, ssem, rsem0, rsem in ((MIX_PIECES, ssem_m, rsem_m0, rsem_m), (F2_PIECES, ssem_f, rsem_f0, rsem_f)):
            for p in pieces:
                pltpu.make_async_remote_copy(
                    src_ref=_shard_piece(s_ref, p), dst_ref=_block_rows(arrs, p, me), send_sem=ssem.at[0],
                    recv_sem=rsem0, device_id=sib, device_id_type=MESH).start()
            for j, chip in enumerate(chips):
                for p in pieces:
                    pltpu.make_async_remote_copy(
                        src_ref=_shard_piece(s_ref, p), dst_ref=_block_rows(arrs, p, me), send_sem=ssem.at[1 + j],
                        recv_sem=rsem.at[j], device_id=(*chip, c), device_id_type=MESH).start()

    dma = pltpu.SemaphoreType.DMA
    return pl.pallas_call(
        body, name="gather_rest_start",
        out_shape=(dma((4,)), dma(()), dma((3,)), dma((4,)), dma(()), dma((3,)),
                   _hbm_like(shard), _hbm_like(wi), _hbm_like(wo), _hbm_like(w2), _hbm_like(w1)),
        in_specs=(HBM_SPEC,) * 5, out_specs=(SEM_SPEC,) * 6 + (HBM_SPEC,) * 5,
        input_output_aliases={0: 6, 1: 7, 2: 8, 3: 9, 4: 10},
        compiler_params=pltpu.CompilerParams(has_side_effects=SPLIT_EFFECT),
    )(_in_hbm(shard), _in_hbm(wi), _in_hbm(wo), _in_hbm(w2), _in_hbm(w1))


def _gather_mix_pass_on(rsem_m, wi, wo, thru, after):
    def body(wi_ref, wo_ref, thru_ref, rsem, after_ref, fsend, frecv, wi_o, wo_o, thru_o):
        x, y, c = _position()
        sib = (x, y, 1 - c)
        arrs = _weight_pieces(wi_ref=wi_ref, wo_ref=wo_ref)
        both = wi_ref.at[pl.ds(0, _group_rows(MIX_PIECES)), :]
        for j, chip in enumerate([(1 - x, y), (x, 1 - y), (1 - x, 1 - y)]):
            pltpu.make_async_remote_copy(src_ref=both, dst_ref=both, send_sem=fsend.at[j], recv_sem=rsem.at[j],
                                         device_id=(x, y, c), device_id_type=MESH).wait_recv()
            for p in MIX_PIECES:
                rows = _block_rows(arrs, p, (*chip, c))
                pltpu.make_async_remote_copy(src_ref=rows, dst_ref=rows, send_sem=fsend.at[j], recv_sem=frecv.at[j],
                                             device_id=sib, device_id_type=MESH).start()

    dma = pltpu.SemaphoreType.DMA
    return pl.pallas_call(
        body, name="gather_mix_pass_on",
        out_shape=(dma((3,)), dma((3,)), _hbm_like(wi), _hbm_like(wo), _hbm_like(thru)),
        in_specs=(HBM_SPEC, HBM_SPEC, HBM_SPEC, SEM_SPEC, ANY_SPEC), out_specs=(SEM_SPEC, SEM_SPEC) + (HBM_SPEC,) * 3,
        input_output_aliases={0: 2, 1: 3, 2: 4},
        compiler_params=pltpu.CompilerParams(has_side_effects=SPLIT_EFFECT),
    )(wi, wo, _in_hbm(thru), rsem_m, after)


def _gather_mix_wait(ssem_m, rsem_m0, fsend, frecv, shard, wi, wo, after):
    def body(s_ref, wi_ref, wo_ref, ssem, rsem0, fs, fr, after_ref, s_o, wi_o, wo_o):
        x, y, c = _position()
        grp = _shard_group(s_ref, MIX_PIECES)

        def waiter(send_sem, recv_sem):
            return pltpu.make_async_remote_copy(src_ref=grp, dst_ref=grp, send_sem=send_sem, recv_sem=recv_sem,
                                                device_id=(x, y, c), device_id_type=MESH)

        waiter(ssem.at[0], rsem0).wait_recv()
        for j in range(3):
            waiter(fs.at[j], fr.at[j]).wait_recv()
        for rel in range(4):
            waiter(ssem.at[rel], rsem0).wait_send()
        for j in range(3):
            waiter(fs.at[j], fr.at[j]).wait_send()

    return pl.pallas_call(
        body, name="gather_mix_wait", out_shape=(_hbm_like(shard), _hbm_like(wi), _hbm_like(wo)),
        in_specs=(HBM_SPEC,) * 3 + (SEM_SPEC,) * 4 + (ANY_SPEC,), out_specs=(HBM_SPEC,) * 3,
        input_output_aliases={0: 0, 1: 1, 2: 2},
        compiler_params=pltpu.CompilerParams(has_side_effects=SPLIT_EFFECT),
    )(shard, wi, wo, ssem_m, rsem_m0, fsend, frecv, after)


def _gather_ffn2_pass_on(rsem_f, w2, wo, after):
    def body(w2_ref, wo_ref, rsem, after_ref, fsend, frecv, w2_o, wo_o):
        x, y, c = _position()
        sib = (x, y, 1 - c)
        chips = [(1 - x, y), (x, 1 - y), (1 - x, 1 - y)]
        arrs = _weight_pieces(w2_ref=w2_ref)
        three = w2_ref.at[0, pl.ds(0, _group_rows(F2_PIECES)), :]
        for j, chip in enumerate(chips):
            pltpu.make_async_remote_copy(src_ref=three, dst_ref=three, send_sem=fsend.at[j], recv_sem=rsem.at[j],
                                         device_id=(x, y, c), device_id_type=MESH).wait_recv()
            for p in F2_PIECES:
                rows = _block_rows(arrs, p, (*chip, c))
                pltpu.make_async_remote_copy(src_ref=rows, dst_ref=rows, send_sem=fsend.at[j], recv_sem=frecv.at[j],
                                             device_id=sib, device_id_type=MESH).start()

    dma = pltpu.SemaphoreType.DMA
    return pl.pallas_call(
        body, name="gather_ffn2_pass_on", out_shape=(dma((3,)), dma((3,)), _hbm_like(w2), _hbm_like(wo)),
        in_specs=(HBM_SPEC, HBM_SPEC, SEM_SPEC, ANY_SPEC), out_specs=(SEM_SPEC, SEM_SPEC, HBM_SPEC, HBM_SPEC),
        input_output_aliases={0: 2, 1: 3},
        compiler_params=pltpu.CompilerParams(has_side_effects=SPLIT_EFFECT),
    )(w2, wo, rsem_f, after)


def _gather_ffn2_wait(ssem_f, rsem_f0, fsend, frecv, shard, w2, after):
    def body(s_ref, w2_ref, ssem, rsem0, fs, fr, after_ref, w2_o):
        x, y, c = _position()
        grp = _shard_group(s_ref, F2_PIECES)

        def waiter(send_sem, recv_sem):
            return pltpu.make_async_remote_copy(src_ref=grp, dst_ref=grp, send_sem=send_sem, recv_sem=recv_sem,
                                                device_id=(x, y, c), device_id_type=MESH)

        waiter(ssem.at[0], rsem0).wait_recv()
        for j in range(3):
            waiter(fs.at[j], fr.at[j]).wait_recv()
        for rel in range(4):
            waiter(ssem.at[rel], rsem0).wait_send()
        for j in range(3):
            waiter(fs.at[j], fr.at[j]).wait_send()

    return pl.pallas_call(
        body, name="gather_ffn2_wait", out_shape=_hbm_like(w2),
        in_specs=(HBM_SPEC, HBM_SPEC, SEM_SPEC, SEM_SPEC, SEM_SPEC, SEM_SPEC, ANY_SPEC), out_specs=HBM_SPEC,
        input_output_aliases={1: 0},
        compiler_params=pltpu.CompilerParams(has_side_effects=SPLIT_EFFECT),
    )(shard, w2, ssem_f, rsem_f0, fsend, frecv, after)


class _GatheredWeights(_LocalWeights):
    def __init__(self, shard, x, gain1):
        w1, self.h1, wi, wo, w2 = _all_gather_ffn1(shard, x, gain1)
        (self.ssem_m, self.rsem_m0, self.rsem_m, self.ssem_f, self.rsem_f0, self.rsem_f,
         self.shard, self.wi, self.wo, self.w2_part, self.w1) = _gather_rest_start(shard, wi, wo, w2, w1)

    def first_norm(self, x, gain):
        return self.h1

    def after_ffn1(self, gain, x1):
        self.fsend_m, self.frecv_m, self.wi, self.wo, gain = _gather_mix_pass_on(self.rsem_m, self.wi, self.wo, gain, x1)
        return gain

    def mix(self, after):
        self.shard, wint, wout = _gather_mix_wait(self.ssem_m, self.rsem_m0, self.fsend_m, self.frecv_m, self.shard,
                                                  self.wi, self.wo, after)
        return wint, wout

    def before_out_proj(self, wout, after):
        self.fsend, self.frecv, self.w2_part, wout = _gather_ffn2_pass_on(self.rsem_f, self.w2_part, wout, after)
        return wout

    def ffn2(self, after):
        return _gather_ffn2_wait(self.ssem_f, self.rsem_f0, self.fsend, self.frecv, self.shard, self.w2_part, after)

    def out_ffn2_grads_ready(self, dwout, dw2, after):
        rx1 = lax.empty((4, RSA_ROWS, D), bf16)
        sa, ra, sent, rx1, after = _rsa_level1_start(dict(wo=dwout, w2=dw2), rx1, after, "rsa_level1_start_out_ffn2")
        self.level1 = ((sa, ra), sent, rx1)
        return after

    def before_ffn1_bwd(self, dwint, dx1b):
        early, sent, rx1 = self.level1
        sa, ra, late, rx1, dx1b = _rsa_level1_start(dict(wi=dwint), rx1, dx1b, "rsa_level1_start_in")
        started = (((MIX_PIECES[1],) + F2_PIECES, *early), ((MIX_PIECES[0],), sa, ra))
        rx2 = lax.empty((3, RSA_ROWS, D), bf16)
        self.sb, self.rb, self.tx, self.acc, self.rx2, dx1b = _rsa_sums_and_send(
            started, late["wi"], sent["wo"], sent["w2"], rx1, rx2, dx1b)
        return dx1b

    def mix_ffn2_grads_parts(self, after):
        rx2 = _rsa_level2_wait(self.sb, self.rb, self.tx, self.rx2, after)
        return self.acc, rx2


def _reduce_scatter_ffn1_head(dw1, small_packed, first_norm):
    pieces = G1_PIECES
    half = FS // 2
    hrows = len(pieces) * half
    nrows = 2 * hrows
    X_RELAY, Y_RELAY = range(2)

    T = first_norm[0].shape[0]

    def body(d1_ref, p_ref, dh_hbm, x_hbm, gain_ref, dr_hbm,
             forx_ref, fory_ref, own_ref, rx1_ref, relx_ref, rely_ref, gx_hbm, tot_ref,
             own_buf, rx_buf, tx1, tx2, tx3, acc, sa, ra, sb, rb, lsem, pair, chips, small_tot, small_send, small_recv,
             xbuf, rbuf, hbuf, obuf, norm_in_sems, norm_out_sems):
        x, y, c = _position()
        me, sib = (x, y, c), (x, y, 1 - c)
        xn, yn = (1 - x, y, c), (x, 1 - y, c)
        rel_chips = [(x, y), (1 - x, y), (x, 1 - y), (1 - x, 1 - y)]
        srcs = _weight_pieces(w1_ref=d1_ref)

        my_chip = 2 * x + y
        pair[c] = p_ref[...]
        swap = pltpu.make_async_remote_copy(
            src_ref=pair.at[c], dst_ref=pair.at[c], send_sem=small_send.at[0], recv_sem=small_recv.at[0],
            device_id=sib, device_id_type=MESH)
        mine = pl.ds(pl.multiple_of(c * (SMALL_ROWS // 2), 8), SMALL_ROWS // 2)
        small = [pltpu.make_async_remote_copy(
            src_ref=chips.at[my_chip, mine, :], dst_ref=chips.at[my_chip, mine, :], send_sem=small_send.at[j],
            recv_sem=small_recv.at[j], device_id=(*rel_chips[j], c), device_id_type=MESH) for j in (1, 2, 3)]
        give = pltpu.make_async_remote_copy(
            src_ref=small_tot.at[mine, :], dst_ref=small_tot.at[mine, :], send_sem=small_send.at[4],
            recv_sem=small_recv.at[4], device_id=sib, device_id_type=MESH)

        def part(k, dev, hf):
            r = PIECE_ROWS[k]
            return srcs[k].at[pl.ds(pl.multiple_of(_dev_index(*dev) * r + hf * half, 16), half), :]

        def slot(ref, k, hf):
            return ref.at[pl.ds(hf * hrows + k * half, half), :]

        halves = [(k, hf) for hf in (0, 1) for k in pieces]

        for j in (3, 1, 2, 0):
            for k, hf in halves:
                pltpu.make_async_remote_copy(
                    src_ref=part(k, (*rel_chips[j], 1 - c), hf), dst_ref=slot(rx1_ref.at[j], k, hf),
                    send_sem=sa.at[j], recv_sem=ra.at[j], device_id=sib, device_id_type=MESH).start()

        def wait_a(j):
            return pltpu.make_async_remote_copy(src_ref=rx1_ref.at[j], dst_ref=rx1_ref.at[j], send_sem=sa.at[j],
                                                recv_sem=ra.at[j], device_id=me, device_id_type=MESH)

        def ici(rel, src, dst, to):
            return pltpu.make_async_remote_copy(src_ref=src, dst_ref=dst, send_sem=sb.at[rel], recv_sem=rb.at[rel],
                                                device_id=to, device_id_type=MESH)

        first, second = pl.ds(0, hrows), pl.ds(hrows, hrows)
        sends = {
            X_RELAY: ici(X_RELAY, tx3.at[first, :], relx_ref, xn),
            Y_RELAY: ici(Y_RELAY, tx3.at[second, :], rely_ref, yn),
        }

        def chip_sum(j, dst):
            loads = [pltpu.make_async_copy(part(k, (*rel_chips[j], c), hf), slot(own_buf, k, hf), lsem.at[0])
                     for k, hf in halves]
            for cp in loads:
                cp.start()
            wait_a(j).wait_recv()
            got = pltpu.make_async_copy(rx1_ref.at[j], rx_buf, lsem.at[1])
            got.start()
            pltpu.make_async_copy(rx_buf, rx_buf, lsem.at[0]).wait()
            got.wait()

            def add(i, carry):
                rows = pl.ds(pl.multiple_of(i * half, 16), half)
                tot = own_buf[rows, :].astype(f32) + rx_buf[rows, :].astype(f32)
                dst[rows, :] = tot.astype(dst.dtype)
                return carry

            lax.fori_loop(0, nrows // half, add, 0)

        def add_landed(landed, dst, rows0, nrows_):
            got = pltpu.make_async_copy(landed, rx_buf.at[pl.ds(0, nrows_), :], lsem.at[1])
            got.start()
            got.wait()

            def add(i, carry):
                src_rows = pl.ds(pl.multiple_of(i * half, 16), half)
                dst_rows = pl.ds(pl.multiple_of(rows0 + i * half, 16), half)
                dst[dst_rows, :] = (dst[dst_rows, :].astype(f32) + rx_buf[src_rows, :].astype(f32)).astype(dst.dtype)
                return carry

            lax.fori_loop(0, nrows_ // half, add, 0)

        chip_sum(3, tx3)
        sends[X_RELAY].start()
        sends[Y_RELAY].start()
        dg = _norm_bwd_rows(T, dh_hbm, x_hbm, gain_ref, dr_hbm, gx_hbm.at[0], None,
                            xbuf, rbuf, hbuf, obuf, None, norm_in_sems, norm_out_sems)
        r0 = SMALL_OFF["ffn1_norm"]
        for k in range(D // 128):
            pair[c, r0 + k:r0 + k + 1, :] = dg[:, 128 * k:128 * (k + 1)]
        swap.start()
        swap.wait_recv()
        chips[my_chip] = pair[0] + pair[1]
        for cp in small:
            cp.start()
        chip_sum(1, tx1)
        chip_sum(2, tx2)
        chip_sum(0, acc)
        own_out = pltpu.make_async_copy(acc, own_ref, lsem.at[0])
        own_out.start()
        sends[X_RELAY].wait_recv()
        add_landed(relx_ref, tx2, 0, hrows)
        sends[Y_RELAY].wait_recv()
        add_landed(rely_ref, tx1, hrows, hrows)
        own_out.wait()
        outs = [pltpu.make_async_copy(tx1, forx_ref, lsem.at[0]), pltpu.make_async_copy(tx2, fory_ref, lsem.at[1])]
        for cp in outs:
            cp.start()
        for cp in outs:
            cp.wait()
        for cp in small:
            cp.wait_recv()
        small_tot[mine, :] = (chips[0, mine, :] + chips[1, mine, :]) + (chips[2, mine, :] + chips[3, mine, :])
        give.start()
        give.wait_recv()
        tot = small_tot[...]
        tot_ref[...] = tot
        loss = jnp.sum(tot[LOSS_ROW:LOSS_ROW + 1, :], axis=-1, keepdims=True)
        tot_ref[LOSS_ROW:LOSS_ROW + 1, :] = jnp.broadcast_to(loss, (1, 128))
        for j in range(4):
            wait_a(j).wait_send()
        for cp in sends.values():
            cp.wait_send()
        swap.wait_send()
        for cp in small + [give]:
            cp.wait_send()

    hbm = pl.BlockSpec(memory_space=pl.ANY)
    vm = pl.BlockSpec(memory_space=pltpu.VMEM)
    outs = pl.pallas_call(
        body, in_specs=[hbm, vm, hbm, hbm, vm, hbm], out_specs=[hbm] * 7 + [vm],
        out_shape=(SDS((nrows, D), bf16), SDS((nrows, D), bf16), SDS((nrows, D), f32), SDS((4, nrows, D), bf16),
                   SDS((hrows, D), bf16), SDS((hrows, D), bf16), SDS((1, T, D), f32), SDS((SMALL_ROWS, 128), f32)),
        scratch_shapes=[pltpu.VMEM((nrows, D), bf16), pltpu.VMEM((nrows, D), bf16),
                        pltpu.VMEM((nrows, D), bf16), pltpu.VMEM((nrows, D), bf16), pltpu.VMEM((nrows, D), bf16),
                        pltpu.VMEM((nrows, D), f32),
                        pltpu.SemaphoreType.DMA((4,)), pltpu.SemaphoreType.DMA((4,)),
                        pltpu.SemaphoreType.DMA((2,)), pltpu.SemaphoreType.DMA((2,)), pltpu.SemaphoreType.DMA((2,)),
                        pltpu.VMEM((2, SMALL_ROWS, 128), f32), pltpu.VMEM((4, SMALL_ROWS, 128), f32),
                        pltpu.VMEM((SMALL_ROWS, 128), f32),
                        pltpu.SemaphoreType.DMA((5,)), pltpu.SemaphoreType.DMA((5,))]
        + [sc for sc in _norm_bwd_scratch(True, False) if sc is not None],
        compiler_params=pltpu.CompilerParams(has_side_effects=True, vmem_limit_bytes=VMEM_LIMIT_V7X),
        name="reduce_scatter_ffn1_head")(dw1, small_packed, *first_norm)
    return outs[0], outs[1], outs[2], outs[-1], outs[-2]


def _rs1_tail_start(for_x, for_y, from_x, from_y, *thru):
    def body(fx_ref, fy_ref, lx_ref, ly_ref, *rest):
        ssem, rsem = rest[len(thru):len(thru) + 2]
        x, y, c = _position()
        pltpu.make_async_remote_copy(src_ref=fx_ref, dst_ref=lx_ref, send_sem=ssem.at[0], recv_sem=rsem.at[0],
                                     device_id=(1 - x, y, c), device_id_type=MESH).start()
        pltpu.make_async_remote_copy(src_ref=fy_ref, dst_ref=ly_ref, send_sem=ssem.at[1], recv_sem=rsem.at[1],
                                     device_id=(x, 1 - y, c), device_id_type=MESH).start()

    dma = pltpu.SemaphoreType.DMA
    arrs = (for_x, for_y, from_x, from_y, *thru)
    return pl.pallas_call(
        body, name="rs1_tail_start", out_shape=(dma((2,)), dma((2,))) + tuple(_hbm_like(a) for a in arrs),
        in_specs=(HBM_SPEC,) * len(arrs), out_specs=(SEM_SPEC,) * 2 + (HBM_SPEC,) * len(arrs),
        input_output_aliases={i: i + 2 for i in range(len(arrs))},
        compiler_params=pltpu.CompilerParams(has_side_effects=SPLIT_EFFECT),
    )(*[_in_hbm(a) for a in arrs])


RSA_PIECES = MIX_PIECES + F2_PIECES
RSA_ROWS = _group_rows(RSA_PIECES)
RSA_OFF = {k: PIECE_OFF[k] - PIECE_OFF[RSA_PIECES[0]] for k in RSA_PIECES}
RSA_BLOCK = 192


def _rsa_rows(ref, k):
    return ref.at[pl.ds(RSA_OFF[k], PIECE_ROWS[k]), :]


def _rsa_level1_start(grads, rx1, thru, name):
    keys = sorted(grads)
    n = len(keys)

    def body(*refs):
        srcs = _weight_pieces(**{k + "_ref": ref for k, ref in zip(keys, refs[:n])})
        rx1_ref, sa, ra = refs[n], refs[n + 2], refs[n + 3]
        x, y, c = _position()
        for j, chip in enumerate([(x, y), (1 - x, y), (x, 1 - y), (1 - x, 1 - y)]):
            for k in sorted(srcs):
                pltpu.make_async_remote_copy(
                    src_ref=_block_rows(srcs, k, (*chip, 1 - c)), dst_ref=_rsa_rows(rx1_ref.at[j], k),
                    send_sem=sa.at[j], recv_sem=ra.at[j], device_id=(x, y, 1 - c), device_id_type=MESH).start()

    dma = pltpu.SemaphoreType.DMA
    arrs = tuple(grads[k] for k in keys) + (rx1, thru)
    outs = pl.pallas_call(
        body, name=name, out_shape=(dma((4,)), dma((4,))) + tuple(_hbm_like(a) for a in arrs),
        in_specs=(HBM_SPEC,) * len(arrs), out_specs=(SEM_SPEC,) * 2 + (HBM_SPEC,) * len(arrs),
        input_output_aliases={i: i + 2 for i in range(len(arrs))},
        compiler_params=pltpu.CompilerParams(has_side_effects=SPLIT_EFFECT),
    )(*[_in_hbm(a) for a in arrs])
    return outs[0], outs[1], dict(zip(keys, outs[2:2 + n])), outs[2 + n], outs[3 + n]


def _rsa_sums_and_send(started, dwint, dwout, dw2, rx1, rx2, thru):
    nblk = RSA_ROWS // RSA_BLOCK
    nstart = len(started)

    def body(*refs):
        di_ref, do_ref, d2_ref, rx1_ref, rx2_ref = refs[:5]
        l1_sems = refs[6:6 + 2 * nstart]
        sb, rb, tx_ref, acc_ref = refs[6 + 2 * nstart:10 + 2 * nstart]
        own_buf, rx_buf, tx_buf, acc_buf, in_sems, out_sems = refs[13 + 2 * nstart:]
        x, y, c = _position()
        srcs = _weight_pieces(wi_ref=di_ref, wo_ref=do_ref, w2_ref=d2_ref)
        chips = [(x, y), (1 - x, y), (x, 1 - y), (1 - x, 1 - y)]

        for g, (pieces, _, _) in enumerate(started):
            ssem, rsem = l1_sems[2 * g], l1_sems[2 * g + 1]
            for j in range(4):
                rows = rx1_ref.at[j, pl.ds(RSA_OFF[pieces[0]], _group_rows(pieces)), :]
                d = pltpu.make_async_remote_copy(src_ref=rows, dst_ref=rows, send_sem=ssem.at[j], recv_sem=rsem.at[j],
                                                 device_id=(x, y, c), device_id_type=MESH)
                d.wait_recv()
                d.wait_send()

        def start_loads(j):
            s = j % 2
            for k in RSA_PIECES:
                pltpu.make_async_copy(_block_rows(srcs, k, (*chips[j], c)), _rsa_rows(own_buf.at[s], k),
                                      in_sems.at[2 * s]).start()
            pltpu.make_async_copy(rx1_ref.at[j], rx_buf.at[s], in_sems.at[2 * s + 1]).start()

        def wait_loads(j):
            s = j % 2
            pltpu.make_async_copy(rx1_ref.at[j], own_buf.at[s], in_sems.at[2 * s]).wait()
            pltpu.make_async_copy(rx1_ref.at[j], rx_buf.at[s], in_sems.at[2 * s + 1]).wait()

        def store(j):
            if j == 0:
                return pltpu.make_async_copy(acc_buf, acc_ref, out_sems.at[2])
            return pltpu.make_async_copy(tx_buf.at[j % 2], tx_ref.at[j - 1], out_sems.at[j % 2])

        def send(j):
            return pltpu.make_async_remote_copy(src_ref=tx_ref.at[j - 1], dst_ref=rx2_ref.at[j - 1], send_sem=sb.at[j - 1],
                                                recv_sem=rb.at[j - 1], device_id=(*chips[j], c), device_id_type=MESH)

        start_loads(0)
        for j in range(4):
            s = j % 2
            if j + 1 < 4:
                start_loads(j + 1)
            wait_loads(j)
            if j == 3:
                store(1).wait()
                send(1).start()

            def add(i, carry, j=j, s=s):
                rows = pl.ds(pl.multiple_of(i * RSA_BLOCK, 16), RSA_BLOCK)
                tot = own_buf[s, rows, :].astype(f32) + rx_buf[s, rows, :].astype(f32)
                if j == 0:
                    acc_buf[rows, :] = tot
                else:
                    tx_buf[s, rows, :] = tot.astype(bf16)
                return carry

            lax.fori_loop(0, nblk, add, 0)
            store(j).start()
        store(0).wait()
        for j in (2, 3):
            store(j).wait()
            send(j).start()

    dma = pltpu.SemaphoreType.DMA
    passed = (rx1, rx2, thru)
    outs = pl.pallas_call(
        body, name="rsa_sums_and_send",
        in_specs=(HBM_SPEC,) * 6 + (SEM_SPEC,) * (2 * nstart),
        out_specs=(SEM_SPEC,) * 2 + (HBM_SPEC,) * 5,
        out_shape=(dma((3,)), dma((3,)), pltpu.HBM((3, RSA_ROWS, D), bf16), pltpu.HBM((RSA_ROWS, D), f32))
        + tuple(_hbm_like(a) for a in passed),
        input_output_aliases={3: 4, 4: 5, 5: 6},
        scratch_shapes=[pltpu.VMEM((2, RSA_ROWS, D), bf16), pltpu.VMEM((2, RSA_ROWS, D), bf16),
                        pltpu.VMEM((2, RSA_ROWS, D), bf16), pltpu.VMEM((RSA_ROWS, D), f32),
                        dma((4,)), dma((3,))],
        compiler_params=pltpu.CompilerParams(has_side_effects=SPLIT_EFFECT, vmem_limit_bytes=VMEM_LIMIT_V7X),
    )(*[_in_hbm(a) for a in (dwint, dwout, dw2) + passed], *[sem for _, sa, ra in started for sem in (sa, ra)])
    sb, rb, tx, acc, _, rx2, thru = outs
    return sb, rb, tx, acc, rx2, thru


def _rsa_level2_wait(sb, rb, tx, rx2, after):
    def body(tx_ref, rx2_ref, sb_ref, rb_ref, after_ref, rx2_o):
        x, y, c = _position()
        for j in range(3):
            d = pltpu.make_async_remote_copy(src_ref=tx_ref.at[j], dst_ref=rx2_ref.at[j], send_sem=sb_ref.at[j],
                                             recv_sem=rb_ref.at[j], device_id=(x, y, c), device_id_type=MESH)
            d.wait_recv()
            d.wait_send()

    return pl.pallas_call(
        body, name="rsa_level2_wait", out_shape=_hbm_like(rx2),
        in_specs=(HBM_SPEC, HBM_SPEC, SEM_SPEC, SEM_SPEC, ANY_SPEC), out_specs=HBM_SPEC,
        input_output_aliases={1: 0},
        compiler_params=pltpu.CompilerParams(has_side_effects=SPLIT_EFFECT),
    )(tx, rx2, sb, rb, after)


def _adamw_math(w, g, m, v):
    m = ADAM_B1 * m + (1.0 - ADAM_B1) * g
    v = ADAM_B2 * v + (1.0 - ADAM_B2) * (g * g)
    m_hat = m / (1.0 - ADAM_B1 ** ADAM_STEP)
    v_hat = v / (1.0 - ADAM_B2 ** ADAM_STEP)
    delta = -ADAM_LR * (m_hat / (jnp.sqrt(v_hat) + ADAM_EPS) + ADAM_WD * w)
    return delta, m, v


def _adamw_big(pieces, ws, ms, vs, own, landed, name, in_flight=None):
    npiece = len(pieces)
    nsent = len(landed) if in_flight else 0
    nland = sum(a.shape[0] if a.ndim == 3 else 1 for a in landed)
    rmax = max(PIECE_ROWS[k] for k in pieces)
    half = FS // 2

    def segments(k):
        if k in G1_PIECES:
            return [(hf * len(G1_PIECES) * half + k * half, hf * half, half) for hf in (0, 1)]
        return [(RSA_OFF[k], 0, PIECE_ROWS[k])]

    def body(*refs):
        ins = (refs[0:npiece], refs[npiece:2 * npiece], refs[2 * npiece:3 * npiece])
        own_ref = refs[3 * npiece]
        nin = 3 * npiece + 1 + len(landed)
        land_refs = []
        for ref, a in zip(refs[3 * npiece + 1:nin], landed):
            land_refs += [ref.at[j] for j in range(a.shape[0])] if a.ndim == 3 else [ref]
        if in_flight:
            sent_refs, (ssem, rsem) = refs[nin:nin + nsent], refs[nin + nsent:nin + nsent + 2]
            nin += nsent + 3
        out_refs = refs[nin:nin + 4 * npiece]
        inb, landb, outb, in_sems, land_sems, out_sems = refs[nin + 4 * npiece + nsent:]

        def loads(i):
            s, k = i % 2, pieces[i]
            r = PIECE_ROWS[k]
            cps = [pltpu.make_async_copy(ins[q][i].at[0], inb.at[s, q, pl.ds(0, r), :], in_sems.at[4 * s + q])
                   for q in range(3)]
            waits, late = list(cps), []
            for src0, dst0, n in segments(k):
                cps.append(pltpu.make_async_copy(own_ref.at[pl.ds(src0, n), :], inb.at[s, 3, pl.ds(dst0, n), :],
                                                 in_sems.at[4 * s + 3]))
                for p in range(nland):
                    late.append(pltpu.make_async_copy(land_refs[p].at[pl.ds(src0, n), :],
                                                      landb.at[s, p, pl.ds(dst0, n), :], land_sems.at[nland * s + p]))
            own_rows = inb.at[s, 3, pl.ds(0, r), :]
            waits.append(pltpu.make_async_copy(own_rows, own_rows, in_sems.at[4 * s + 3]))
            for p in range(nland):
                rows = landb.at[s, p, pl.ds(0, r), :]
                waits.append(pltpu.make_async_copy(rows, rows, land_sems.at[nland * s + p]))
            return cps, late, waits

        def stores(i):
            s, r = i % 2, PIECE_ROWS[pieces[i]]
            return [pltpu.make_async_copy(outb.at[s, q, pl.ds(0, r), :], out_refs[q * npiece + i].at[0],
                                          out_sems.at[4 * s + q]) for q in range(4)]

        ahead = min(2, npiece) if in_flight else 1
        for i in range(ahead):
            for cp in loads(i)[0]:
                cp.start()
        if in_flight:
            x, y, c = _position()
            for j in range(nsent):
                d = pltpu.make_async_remote_copy(src_ref=sent_refs[j], dst_ref=refs[3 * npiece + 1 + j],
                                                 send_sem=ssem.at[j], recv_sem=rsem.at[j], device_id=(x, y, c),
                                                 device_id_type=MESH)
                d.wait_recv()
                d.wait_send()
        for cp in loads(0)[1]:
            cp.start()
        for i in range(npiece):
            s, r = i % 2, PIECE_ROWS[pieces[i]]
            if i + 1 < npiece:
                first, late, _ = loads(i + 1)
                for cp in late if i + 1 < ahead else first + late:
                    cp.start()
            for cp in loads(i)[2]:
                cp.wait()
            if i >= 2:
                for cp in stores(i - 2):
                    cp.wait()
            g = inb[s, 3, 0:r, :]
            for p in range(nland):
                g = g + landb[s, p, 0:r, :].astype(f32)
            d, nm, nv = _adamw_math(inb[s, 0, 0:r, :], g, inb[s, 1, 0:r, :], inb[s, 2, 0:r, :])
            outb[s, 0, 0:r, :] = g
            outb[s, 1, 0:r, :] = d
            outb[s, 2, 0:r, :] = nm
            outb[s, 3, 0:r, :] = nv
            for cp in stores(i):
                cp.start()
        for i in range(max(npiece - 2, 0), npiece):
            for cp in stores(i):
                cp.wait()

    hbm = pl.BlockSpec(memory_space=pl.ANY)
    in_specs, out_specs = [hbm] * (3 * npiece + 1), [hbm] * (4 * npiece)
    out_shape = [SDS(w.shape, f32) for _ in range(4) for w in ws]
    args, aliases, effect = [*ws, *ms, *vs, own], {}, False
    if in_flight:
        ssem, rsem, sent, after = in_flight
        in_specs += [HBM_SPEC] * (2 * nsent) + [SEM_SPEC, SEM_SPEC, hbm]
        args += [_in_hbm(a) for a in (*landed, *sent)] + [ssem, rsem, after]
        out_specs += [HBM_SPEC] * nsent
        out_shape += [_hbm_like(a) for a in landed]
        aliases = {3 * npiece + 1 + j: 4 * npiece + j for j in range(nsent)}
        effect = SPLIT_EFFECT
    else:
        in_specs += [hbm] * len(landed)
        args += list(landed)
    outs = pl.pallas_call(
        body, in_specs=in_specs, out_specs=out_specs, out_shape=tuple(out_shape), input_output_aliases=aliases,
        scratch_shapes=[pltpu.VMEM((2, 4, rmax, D), f32), pltpu.VMEM((2, nland, rmax, D), bf16),
                        pltpu.VMEM((2, 4, rmax, D), f32),
                        pltpu.SemaphoreType.DMA((8,)), pltpu.SemaphoreType.DMA((2 * nland,)),
                        pltpu.SemaphoreType.DMA((8,))],
        compiler_params=pltpu.CompilerParams(has_side_effects=effect, vmem_limit_bytes=VMEM_LIMIT_V7X),
        name=name)(*args)
    return [list(outs[q * npiece:(q + 1) * npiece]) for q in range(4)]


def _adamw_small(ws, ms, vs, gs, name):
    n = len(ws)

    def body(*refs):
        w_refs, m_refs, v_refs, g_refs = refs[0:n], refs[n:2 * n], refs[2 * n:3 * n], refs[3 * n:4 * n]
        outs = refs[4 * n:]
        for i in range(n):
            d, nm, nv = _adamw_math(w_refs[i][...], g_refs[i][...], m_refs[i][...], v_refs[i][...])
            outs[i][...] = d
            outs[n + i][...] = nm
            outs[2 * n + i][...] = nv

    outs = pl.pallas_call(
        body, out_shape=tuple(SDS(w.shape, f32) for _ in range(3) for w in ws), name=name)(*ws, *ms, *vs, *gs)
    return [list(outs[q * n:(q + 1) * n]) for q in range(3)]


WEIGHTS = ("ffn1_norm", "ffn1_w_gate", "ffn1_w_up", "ffn1_w_down", "mix_norm", "w_in", "q_norm", "k_norm",
           "attn_sinks", "rel_bias", "pool_w", "pool_scale", "w_out", "ffn2_norm", "ffn2_w_gate", "ffn2_w_up",
           "ffn2_w_down")
BIG = (("ffn1_w_gate", True), ("ffn1_w_up", True), ("ffn1_w_down", False), ("w_in", True), ("w_out", False),
       ("ffn2_w_gate", True), ("ffn2_w_up", True), ("ffn2_w_down", False))


def kernel(x, ffn1_norm, ffn1_w_gate, ffn1_w_up, ffn1_w_down, mix_norm, w_in, q_norm, k_norm, attn_sinks, rel_bias, pool_w, pool_scale, w_out, ffn2_norm, ffn2_w_gate, ffn2_w_up, ffn2_w_down, loss_target, m_ffn1_norm, m_ffn1_w_gate, m_ffn1_w_up, m_ffn1_w_down, m_mix_norm, m_w_in, m_q_norm, m_k_norm, m_attn_sinks, m_rel_bias, m_pool_w, m_pool_scale, m_w_out, m_ffn2_norm, m_ffn2_w_gate, m_ffn2_w_up, m_ffn2_w_down, v_ffn1_norm, v_ffn1_w_gate, v_ffn1_w_up, v_ffn1_w_down, v_mix_norm, v_w_in, v_q_norm, v_k_norm, v_attn_sinks, v_rel_bias, v_pool_w, v_pool_scale, v_w_out, v_ffn2_norm, v_ffn2_w_gate, v_ffn2_w_up, v_ffn2_w_down):
    args = dict(locals())
    w = {n: args[n] for n in WEIGHTS}
    m = {n: args["m_" + n] for n in WEIGHTS}
    v = {n: args["v_" + n] for n in WEIGHTS}

    as_rows = lambda a, tr: jnp.swapaxes(a, 1, 2) if tr else a
    shard = jnp.concatenate([as_rows(w[n], tr)[0].astype(bf16) for n, tr in BIG], axis=0)
    exchanges = _GatheredWeights(shard, x[0], ffn1_norm)
    (dh1, dx1), (dw1, _, _, _), small = _local_step(
        x[0], loss_target[0], exchanges, ffn1_norm, mix_norm, ffn2_norm, q_norm, k_norm, attn_sinks,
        rel_bias, pool_w[0], pool_scale)

    nrows1 = len(G1_PIECES) * FS
    for_x, for_y, own1, small_tot, gx = _reduce_scatter_ffn1_head(dw1, _pack_small(small),
                                                                  (dh1, x[0], ffn1_norm, dx1))
    ssem, rsem, for_x, for_y, from_x, from_y, small_tot, gx = _rs1_tail_start(
        for_x, for_y, lax.empty((nrows1, D), bf16), lax.empty((nrows1, D), bf16), small_tot, gx)
    gx = _fresh_copy(gx, "grad_x_out")
    own_rest, landed_rest = exchanges.mix_ffn2_grads_parts(gx)

    grads, deltas, new_m, new_v = {}, {}, {}, {}
    rest = [k for k in range(len(BIG)) if k not in G1_PIECES]
    rows_of = lambda t, ks: [as_rows(t[BIG[k][0]], BIG[k][1]) for k in ks]
    rest_out = _adamw_big(rest, rows_of(w, rest), rows_of(m, rest), rows_of(v, rest), own_rest, [landed_rest],
                          "adamw_rest")
    ffn1 = list(G1_PIECES)
    ffn1_out = _adamw_big(ffn1, rows_of(w, ffn1), rows_of(m, ffn1), rows_of(v, ffn1), own1, [from_x, from_y],
                          "adamw_ffn1", in_flight=(ssem, rsem, [for_x, for_y], rest_out[0][0]))
    for ks, out in ((rest, rest_out), (ffn1, ffn1_out)):
        for i, k in enumerate(ks):
            n, tr = BIG[k]
            grads[n], deltas[n], new_m[n], new_v[n] = [as_rows(o[i], tr) for o in out]
    small_names = [n for n in SMALL_NAMES if n != "loss"]
    for n in small_names:
        grads[n] = _unpack_small(small_tot, n)
    ds, nms, nvs = _adamw_small([w[n] for n in small_names], [m[n] for n in small_names], [v[n] for n in small_names],
                                [grads[n] for n in small_names], "adamw_small")
    for i, n in enumerate(small_names):
        deltas[n], new_m[n], new_v[n] = ds[i], nms[i], nvs[i]
    loss = small_tot[LOSS_ROW, 0]
    return (loss, gx, *[grads[n] for n in WEIGHTS], *[deltas[n] for n in WEIGHTS],
            *[new_m[n] for n in WEIGHTS], *[new_v[n] for n in WEIGHTS])
```

```python
import jax
import jax.numpy as jnp
import numpy as np
from jax import lax
from jax.experimental import pallas as pl
from jax.experimental.pallas import tpu as pltpu

f32, bf16, i32 = jnp.float32, jnp.bfloat16, jnp.int32
SDS = jax.ShapeDtypeStruct

D = 1024
F = 2816
HD = 64
NH = 8
NKV = 2
GQA = NH // NKV
DATTN = NH * HD
DKV = NKV * HD
DPOOL = 512
POOL_WINDOWS = (2, 4, 8, 16)
PGD = DPOOL // len(POOL_WINDOWS)
DIN = DATTN + 2 * DKV + DPOOL
DMIX = DATTN + DPOOL
BLK = 128
NBUCK = 32
MAX_DISTANCE = 128
EPS = 1e-6
NEG = -1e30
SCALE = HD ** -0.5

ADAM_LR, ADAM_B1, ADAM_B2, ADAM_EPS, ADAM_WD, ADAM_STEP = 0.001, 0.9, 0.999, 1e-08, 0.01, 10

NDEV = 8
FS = F // NDEV
INS = DIN // NDEV
OUTS = DMIX // NDEV
PIECE_ROWS = (FS, FS, FS, INS, OUTS, FS, FS, FS)
PIECE_OFF = tuple(int(v) for v in np.cumsum((0,) + PIECE_ROWS[:-1]))
PACK_ROWS = sum(PIECE_ROWS)

VMEM_LIMIT_V7X = 56 * 1024 * 1024

MESH = pl.DeviceIdType.MESH


def _cparams(sem=None, vmem=None):
    return pltpu.CompilerParams(dimension_semantics=sem, vmem_limit_bytes=vmem)


def _nt(a, b):
    return lax.dot_general(a, b, (((1,), (1,)), ((), ())), preferred_element_type=f32)


def _tn(a, b):
    return lax.dot_general(a, b, (((0,), (0,)), ((), ())), preferred_element_type=f32)


def _nn(a, b):
    return jnp.dot(a, b, preferred_element_type=f32)


def _sigmoid(x):
    return 1.0 / (1.0 + jnp.exp(-x))


def _fresh_copy(a, name):
    T = a.shape[1]
    tm = min(512, T)

    def body(a_ref, o_ref):
        o_ref[...] = a_ref[...]

    tok = pl.BlockSpec((1, tm, D), lambda i: (0, i, 0))
    return pl.pallas_call(body, grid=(T // tm,), in_specs=[tok], out_specs=tok, out_shape=SDS(a.shape, a.dtype),
                          name=name)(a)


def _norm_fwd(x, g, name):
    T = x.shape[0]
    tm = min(512, T)

    def body(x_ref, g_ref, h_ref):
        xv = x_ref[...]
        r = lax.rsqrt(jnp.mean(xv * xv, axis=-1, keepdims=True) + EPS)
        h_ref[...] = (xv * r * g_ref[...]).astype(bf16)

    return pl.pallas_call(
        body, grid=(T // tm,),
        in_specs=[pl.BlockSpec((tm, D), lambda i: (i, 0)), pl.BlockSpec((1, D), lambda i: (0, 0))],
        out_specs=pl.BlockSpec((tm, D), lambda i: (i, 0)),
        out_shape=SDS((T, D), bf16), name=name)(x, g)


FFN_ROW_CHUNK = 256


def _ffn_tiles(T):
    return min(1024, T), 256


def _ffn_fwd(h, w, x, target, next_gain, name):
    T = h.shape[0]
    tm, tf = _ffn_tiles(T)
    nf = F // tf
    with_loss = target is not None
    assert with_loss != (next_gain is not None)

    def body(*refs):
        if with_loss:
            h_ref, w_ref, x_hbm, t_hbm, xo_ref, g_ref, u_ref, dyb_ref, loss_ref, tbuf, sem = refs
        else:
            h_ref, w_ref, x_hbm, gain_ref, xo_ref, g_ref, u_ref, hn_ref, sem = refs
        fi = pl.program_id(0)

        def x_rows(r):
            return pltpu.make_async_copy(x_hbm.at[pl.ds(r, tm), :], xo_ref.at[pl.ds(r, tm), :], sem.at[r // tm])

        def tile(first):
            if first:
                for r in range(0, T, tm):
                    x_rows(r).start()
            wgu = w_ref[0:2].reshape(2 * tf, D)
            for r in range(0, T, tm):
                rows = slice(r, r + tm)
                gu = _nt(h_ref[rows, :], wgu)
                gate, up = gu[:, :tf], gu[:, tf:]
                act = gate * _sigmoid(gate) * up
                g_ref[0, rows, :] = gate.astype(bf16)
                u_ref[0, rows, :] = up.astype(bf16)
                down = _nn((0.5 * act).astype(bf16), w_ref[2])
                if first:
                    x_rows(r).wait()
                xo_ref[rows, :] += down

        @pl.when(fi == 0)
        def _():
            tile(True)

        @pl.when(fi > 0)
        def _():
            tile(False)

        if with_loss:
            @pl.when(fi == nf - 1)
            def _():
                lanes = jnp.zeros((1, 128), f32)
                for r in range(0, T, tm):
                    rows = slice(r, r + tm)
                    cp = pltpu.make_async_copy(t_hbm.at[pl.ds(r, tm), :], tbuf, sem.at[0])
                    cp.start()
                    cp.wait()
                    e = xo_ref[rows, :] - tbuf[...]
                    dy = e * (1.0 / D)
                    xo_ref[rows, :] = dy
                    dyb_ref[rows, :] = (0.5 * dy).astype(bf16)
                    col = jnp.sum(e * e, axis=0, keepdims=True) * (0.5 / D)
                    for k in range(D // 128):
                        lanes = lanes + col[:, 128 * k:128 * (k + 1)]
                loss_ref[...] = lanes
        else:
            @pl.when(fi == nf - 1)
            def _():
                for r in range(0, T, FFN_ROW_CHUNK):
                    rows = slice(r, r + FFN_ROW_CHUNK)
                    xv = xo_ref[rows, :]
                    rstd = lax.rsqrt(jnp.mean(xv * xv, axis=-1, keepdims=True) + EPS)
                    hn_ref[rows, :] = (xv * rstd * gain_ref[...]).astype(bf16)

    tok = pl.BlockSpec((T, D), lambda f: (0, 0))
    act_spec = pl.BlockSpec((1, T, tf), lambda f: (f, 0, 0))
    hbm = pl.BlockSpec(memory_space=pl.ANY)
    in_specs = [tok, pl.BlockSpec((3, tf, D), lambda f: (0, f, 0)), hbm]
    out_specs = [tok, act_spec, act_spec]
    out_shape = [SDS((T, D), f32), SDS((nf, T, tf), bf16), SDS((nf, T, tf), bf16)]
    scratch = [pltpu.SemaphoreType.DMA((T // tm,))]
    args = [h, w, x]
    if with_loss:
        in_specs.append(hbm)
        args.append(target)
        out_specs += [tok, pl.BlockSpec((1, 128), lambda f: (0, 0))]
        out_shape += [SDS((T, D), bf16), SDS((1, 128), f32)]
        scratch = [pltpu.VMEM((tm, D), f32)] + scratch
    else:
        in_specs.append(pl.BlockSpec((1, D), lambda f: (0, 0)))
        args.append(next_gain)
        out_specs.append(tok)
        out_shape.append(SDS((T, D), bf16))
    return pl.pallas_call(
        body, grid=(nf,), in_specs=in_specs, out_specs=out_specs, out_shape=tuple(out_shape), scratch_shapes=scratch,
        compiler_params=_cparams(("arbitrary",), VMEM_LIMIT_V7X), name=name)(*args)


NORM_BWD_ROWS = 512


def _norm_bwd_scratch(dh_in_hbm, with_bf16):
    buf = lambda dt: pltpu.VMEM((2, NORM_BWD_ROWS, D), dt)
    return [buf(f32), buf(f32), buf(f32) if dh_in_hbm else None, buf(f32), buf(bf16) if with_bf16 else None,
            pltpu.SemaphoreType.DMA((6,)), pltpu.SemaphoreType.DMA((4,))]


def _norm_bwd_rows(T, dh_src, x_hbm, gain_ref, dr_hbm, dx_hbm, dxb_hbm, xbuf, rbuf, hbuf, obuf, obb, in_sems, out_sems):
    tm = min(NORM_BWD_ROWS, T)
    nchunk = T // tm

    def loads(i):
        s, rows = i % 2, pl.ds(i * tm, tm)
        cps = [pltpu.make_async_copy(x_hbm.at[rows, :], xbuf.at[s, pl.ds(0, tm), :], in_sems.at[3 * s]),
               pltpu.make_async_copy(dr_hbm.at[rows, :], rbuf.at[s, pl.ds(0, tm), :], in_sems.at[3 * s + 1])]
        if hbuf is not None:
            cps.append(pltpu.make_async_copy(dh_src.at[rows, :], hbuf.at[s, pl.ds(0, tm), :], in_sems.at[3 * s + 2]))
        return cps

    def stores(i):
        s, rows = i % 2, pl.ds(i * tm, tm)
        cps = [pltpu.make_async_copy(obuf.at[s, pl.ds(0, tm), :], dx_hbm.at[rows, :], out_sems.at[2 * s])]
        if dxb_hbm is not None:
            cps.append(pltpu.make_async_copy(obb.at[s, pl.ds(0, tm), :], dxb_hbm.at[rows, :], out_sems.at[2 * s + 1]))
        return cps

    for cp in loads(0):
        cp.start()
    dg = jnp.zeros((1, D), f32)
    for i in range(nchunk):
        s = i % 2
        if i + 1 < nchunk:
            for cp in loads(i + 1):
                cp.start()
        for cp in loads(i):
            cp.wait()
        if i >= 2:
            for cp in stores(i - 2):
                cp.wait()
        xv = xbuf[s, 0:tm, :]
        rstd = lax.rsqrt(jnp.mean(xv * xv, axis=-1, keepdims=True) + EPS)
        xh = xv * rstd
        dhv = hbuf[s, 0:tm, :] if hbuf is not None else dh_src[i * tm:(i + 1) * tm, :]
        dxh = dhv * gain_ref[...]
        dx = rbuf[s, 0:tm, :] + rstd * (dxh - xh * jnp.mean(dxh * xh, axis=-1, keepdims=True))
        obuf[s, 0:tm, :] = dx
        if dxb_hbm is not None:
            obb[s, 0:tm, :] = dx.astype(bf16)
        dg = dg + jnp.sum(dhv * xh, axis=0, keepdims=True)
        for cp in stores(i):
            cp.start()
    for i in range(max(nchunk - 2, 0), nchunk):
        for cp in stores(i):
            cp.wait()
    return dg


def _ffn_bwd(dob, h, gate, up, w, norm, name):
    T = h.shape[0]
    _, tf = _ffn_tiles(T)
    nf = F // tf
    nin = 5 if norm is None else 8
    nout = 2 if norm is None else 4

    def body(*refs):
        do_hbm, h_hbm, g_ref, u_ref, w_ref = refs[:5]
        dw_ref = refs[nin + nout - 1]
        do_v, h_v, dh_acc, dgu_s, act_s, sems = refs[nin + nout:nin + nout + 6]
        fi = pl.program_id(0)

        @pl.when(fi == 0)
        def _():
            loads = [pltpu.make_async_copy(do_hbm, do_v, sems.at[0]), pltpu.make_async_copy(h_hbm, h_v, sems.at[1])]
            for cp in loads:
                cp.start()
            dh_acc[...] = jnp.zeros_like(dh_acc)
            for cp in loads:
                cp.wait()

        wgu = w_ref[0:2].reshape(2 * tf, D)
        for r in range(0, T, FFN_ROW_CHUNK):
            rows = slice(r, r + FFN_ROW_CHUNK)
            dov = do_v[rows, :]
            gv = g_ref[0, rows, :].astype(f32)
            uv = u_ref[0, rows, :].astype(f32)
            sg = _sigmoid(gv)
            sil = gv * sg
            dact = _nt(dov, w_ref[2])
            dup = dact * sil
            dgate = dact * uv * (sg * (1.0 + gv * (1.0 - sg)))
            dgu = jnp.concatenate([dgate.astype(bf16), dup.astype(bf16)], axis=1)
            dgu_s[rows, :] = dgu
            act_s[rows, :] = (sil * uv).astype(bf16)
            dh_acc[rows, :] += _nn(dgu, wgu)
        dw_ref[0:2] = _tn(dgu_s[...], h_v[...]).reshape(2, tf, D).astype(bf16)
        dw_ref[2] = _tn(act_s[...], do_v[...]).astype(bf16)

        @pl.when(fi == nf - 1)
        def _():
            if norm is None:
                out = pltpu.make_async_copy(dh_acc, refs[nin], sems.at[0])
                out.start()
                out.wait()
            else:
                x_hbm, gain_ref, dr_hbm, dx_hbm, dxb_hbm, dg_ref = refs[5:11]
                xbuf, rbuf, obuf, obb, in_sems, out_sems = refs[nin + nout + 6:]
                dg_ref[...] = _norm_bwd_rows(T, dh_acc, x_hbm, gain_ref, dr_hbm, dx_hbm, dxb_hbm,
                                             xbuf, rbuf, None, obuf, obb, in_sems, out_sems)

    act_spec = pl.BlockSpec((1, T, tf), lambda f: (f, 0, 0))
    wspec = pl.BlockSpec((3, tf, D), lambda f: (0, f, 0))
    vec = pl.BlockSpec((1, D), lambda f: (0, 0))
    hbm = pl.BlockSpec(memory_space=pl.ANY)
    in_specs, out_specs = [hbm, hbm, act_spec, act_spec, wspec], [hbm, wspec]
    out_shape, args = [SDS((T, D), f32), SDS((3, F, D), bf16)], [dob, h, gate, up, w]
    scratch = [pltpu.VMEM((T, D), bf16), pltpu.VMEM((T, D), bf16), pltpu.VMEM((T, D), f32),
               pltpu.VMEM((T, 2 * tf), bf16), pltpu.VMEM((T, tf), bf16), pltpu.SemaphoreType.DMA((2,))]
    if norm is not None:
        in_specs += [hbm, vec, hbm]
        out_specs = [hbm, hbm, vec, wspec]
        out_shape = [SDS((T, D), f32), SDS((T, D), bf16), SDS((1, D), f32), SDS((3, F, D), bf16)]
        args += list(norm)
        scratch += [sc for sc in _norm_bwd_scratch(False, True) if sc is not None]
    return pl.pallas_call(
        body, grid=(nf,), in_specs=in_specs, out_specs=out_specs, out_shape=tuple(out_shape), scratch_shapes=scratch,
        compiler_params=_cparams(("arbitrary",), VMEM_LIMIT_V7X), name=name)(*args)


def _in_proj_fwd(h, wint, name):
    T = h.shape[0]
    tm = min(512, T)

    def body(h_ref, w_ref, z_ref):
        z_ref[...] = _nt(h_ref[...], w_ref[...])

    return pl.pallas_call(
        body, grid=(T // tm,),
        in_specs=[pl.BlockSpec((tm, D), lambda i: (i, 0)), pl.BlockSpec((DIN, D), lambda i: (0, 0))],
        out_specs=pl.BlockSpec((tm, DIN), lambda i: (i, 0)),
        out_shape=SDS((T, DIN), f32), name=name)(h, wint)


def _in_proj_bwd(dz, wint, h, norm, out_scale, name):
    x, g, dres = norm
    T = h.shape[0]
    tm = min(512, T)
    nt = T // tm

    def body(dz_ref, w_ref, h_ref, x_ref, g_ref, dr_ref, dx_ref, dxb_ref, dg_ref, dw_ref, acc):
        i = pl.program_id(0)
        dzb = dz_ref[...].astype(bf16)
        dhv = _nn(dzb, w_ref[...])
        part = _tn(dzb, h_ref[...])
        xv = x_ref[...]
        rstd = lax.rsqrt(jnp.mean(xv * xv, axis=-1, keepdims=True) + EPS)
        xh = xv * rstd
        dxh = dhv * g_ref[...]
        dx = dr_ref[...] + rstd * (dxh - xh * jnp.mean(dxh * xh, axis=-1, keepdims=True))
        dx_ref[...] = dx
        dxb_ref[...] = (out_scale * dx).astype(bf16)
        dg = jnp.sum(dhv * xh, axis=0, keepdims=True)

        @pl.when(i == 0)
        def _():
            acc[...] = part
            dg_ref[...] = dg

        @pl.when(i > 0)
        def _():
            acc[...] += part
            dg_ref[...] += dg

        @pl.when(i == nt - 1)
        def _():
            dw_ref[...] = acc[...].astype(bf16)

    wspec = pl.BlockSpec((DIN, D), lambda i: (0, 0))
    tok = pl.BlockSpec((tm, D), lambda i: (i, 0))
    vec = pl.BlockSpec((1, D), lambda i: (0, 0))
    return pl.pallas_call(
        body, grid=(nt,),
        in_specs=[pl.BlockSpec((tm, DIN), lambda i: (i, 0)), wspec, tok, tok, vec, tok],
        out_specs=[tok, tok, vec, wspec],
        out_shape=(SDS((T, D), f32), SDS((T, D), bf16), SDS((1, D), f32), SDS((DIN, D), bf16)),
        scratch_shapes=[pltpu.VMEM((DIN, D), f32)],
        compiler_params=_cparams(("arbitrary",)), name=name)(dz, wint, h, x, g, dres)


def _out_proj_fwd(ymix, wout, x, g, name):
    T = x.shape[0]
    tm = min(512, T)

    def body(y_ref, w_ref, x_ref, g_ref, o_ref, h_ref):
        o = x_ref[...] + _nn(y_ref[...], w_ref[...])
        o_ref[...] = o
        r = lax.rsqrt(jnp.mean(o * o, axis=-1, keepdims=True) + EPS)
        h_ref[...] = (o * r * g_ref[...]).astype(bf16)

    tok = pl.BlockSpec((tm, D), lambda i: (i, 0))
    return pl.pallas_call(
        body, grid=(T // tm,),
        in_specs=[pl.BlockSpec((tm, DMIX), lambda i: (i, 0)), pl.BlockSpec((DMIX, D), lambda i: (0, 0)), tok,
                  pl.BlockSpec((1, D), lambda i: (0, 0))],
        out_specs=[tok, tok], out_shape=(SDS((T, D), f32), SDS((T, D), bf16)), name=name)(ymix, wout, x, g)


def _out_proj_bwd(dxb, wout, ymix, name):
    T = dxb.shape[0]
    tm = min(512, T)
    nt = T // tm

    def body(dx_ref, w_ref, y_ref, dy_ref, dw_ref, acc):
        i = pl.program_id(0)
        dxv = dx_ref[...]
        dy_ref[...] = _nt(dxv, w_ref[...])
        part = _tn(y_ref[...], dxv)

        @pl.when(i == 0)
        def _():
            acc[...] = part

        @pl.when(i > 0)
        def _():
            acc[...] += part

        @pl.when(i == nt - 1)
        def _():
            dw_ref[...] = acc[...].astype(bf16)

    wspec = pl.BlockSpec((DMIX, D), lambda i: (0, 0))
    return pl.pallas_call(
        body, grid=(nt,),
        in_specs=[pl.BlockSpec((tm, D), lambda i: (i, 0)), wspec, pl.BlockSpec((tm, DMIX), lambda i: (i, 0))],
        out_specs=[pl.BlockSpec((tm, DMIX), lambda i: (i, 0)), wspec],
        out_shape=(SDS((T, DMIX), f32), SDS((DMIX, D), bf16)),
        scratch_shapes=[pltpu.VMEM((DMIX, D), f32)],
        compiler_params=_cparams(("arbitrary",)), name=name)(dxb, wout, ymix)


def _t5_bucket_table():
    ql = np.arange(BLK)[:, None]
    kl = np.arange(2 * BLK)[None, :]
    n = np.maximum(ql + BLK - kl, 0)
    max_exact = NBUCK // 2
    large = max_exact + (np.log(np.maximum(n, 1) / max_exact) / np.log(MAX_DISTANCE / max_exact)
                         * (NBUCK - max_exact)).astype(np.int32)
    large = np.minimum(large, NBUCK - 1)
    return np.where(n < max_exact, n, large).astype(np.int32)


def _fill_bias(bk_ref, rb_ref, bias_scr):
    bk = bk_ref[...]
    for h in range(NH):
        def step(b, acc, h=h):
            return acc + jnp.where(bk == b, rb_ref[b, h], 0.0)
        bias_scr[h] = lax.fori_loop(0, NBUCK, step, jnp.zeros((BLK, 2 * BLK), f32))


MIX_SUB = 4


class _Window:
    def __init__(self, zc_ref, zp_ref, n, s):
        self.blk = n * MIX_SUB + s
        self.cur = lambda a, b: zc_ref[s * BLK:(s + 1) * BLK, a:b]
        self.prev = (lambda a, b: zp_ref[:, a:b]) if s == 0 else (lambda a, b: zc_ref[(s - 1) * BLK:s * BLK, a:b])


def _attn_qkv(win, kh, qg, kg):
    kc = DATTN + HD * kh
    vc = DATTN + DKV + HD * kh
    kx = jnp.concatenate([win.prev(kc, kc + HD), win.cur(kc, kc + HD)], axis=0)
    vx = jnp.concatenate([win.prev(vc, vc + HD), win.cur(vc, vc + HD)], axis=0)
    qx = jnp.concatenate([win.cur(HD * (GQA * kh + g), HD * (GQA * kh + g + 1)) for g in range(GQA)], axis=0)
    rq = lax.rsqrt(jnp.mean(qx * qx, axis=-1, keepdims=True) + EPS)
    rk = lax.rsqrt(jnp.mean(kx * kx, axis=-1, keepdims=True) + EPS)
    qhat, khat = qx * rq, kx * rk
    return dict(qhat=qhat, khat=khat, rq=rq, rk=rk, qsb=(qhat * (qg * SCALE)).astype(bf16),
                knb=(khat * kg).astype(bf16), vb=vx.astype(bf16))


def _window_masks(n):
    row = lax.broadcasted_iota(i32, (GQA * BLK, 2 * BLK), 0) & (BLK - 1)
    col = lax.broadcasted_iota(i32, (GQA * BLK, 2 * BLK), 1)
    band = (col > row) & (col <= row + BLK)
    return band & ((col >= BLK) | (n > 0)), band


def _attn_probs(a, kh, sk_ref, bias_scr, mask):
    s = _nt(a["qsb"], a["knb"]) + bias_scr[GQA * kh:GQA * (kh + 1)].reshape(GQA * BLK, 2 * BLK)
    s = jnp.where(mask, s, NEG)
    ridx = lax.broadcasted_iota(i32, (GQA * BLK, 1), 0)
    sink = jnp.full((GQA * BLK, 1), sk_ref[GQA * kh + GQA - 1], f32)
    for g in range(GQA - 2, -1, -1):
        sink = jnp.where(ridx < (g + 1) * BLK, sk_ref[GQA * kh + g], sink)
    m = jnp.maximum(jnp.max(s, axis=-1, keepdims=True), sink)
    e = jnp.exp(s - m)
    den = jnp.sum(e, axis=-1, keepdims=True) + jnp.exp(sink - m)
    return e / den


POOL_STEPS = {2: (1,), 4: (1, 2), 8: (1, 2, 4), 16: (1, 2, 4, 8)}


def _pool_group(win, g, w):
    n = win.blk
    c0 = DATTN + 2 * DKV + PGD * g
    uc = win.cur(c0, c0 + PGD)
    up = jnp.where(n > 0, win.prev(c0, c0 + PGD), 0.0)
    sm = jnp.concatenate([up, uc], axis=0)
    for k in POOL_STEPS[w]:
        sm = sm + pltpu.roll(sm, k, axis=0)
    pos = n * BLK + lax.broadcasted_iota(i32, (BLK, 1), 0) + 1
    cnt = jnp.minimum(pos, w).astype(f32)
    return sm[BLK:2 * BLK] / cnt - uc, cnt


def _mix_fwd(z, qg, kg, sinks, relb, bucket, pool_w, pscale, name):
    T = z.shape[0]
    step_rows = MIX_SUB * BLK
    nsteps = T // step_rows

    def body(zc_ref, zp_ref, qg_ref, kg_ref, sk_ref, rb_ref, bk_ref, pw_ref, ps_ref, y_ref, p_ref, bias_scr, yacc):
        n = pl.program_id(0)

        @pl.when(n == 0)
        def _():
            _fill_bias(bk_ref, rb_ref, bias_scr)

        first_mask, mask = _window_masks(n)
        for s in range(MIX_SUB):
            win = _Window(zc_ref, zp_ref, n, s)
            rows = slice(s * BLK, (s + 1) * BLK)
            for kh in range(NKV):
                a = _attn_qkv(win, kh, qg_ref[...], kg_ref[...])
                pb = _attn_probs(a, kh, sk_ref, bias_scr, first_mask if s == 0 else mask).astype(bf16)
                p_ref[s, GQA * kh:GQA * (kh + 1)] = pb.reshape(GQA, BLK, 2 * BLK)
                o = _nn(pb, a["vb"])
                for g in range(GQA):
                    hc = HD * (GQA * kh + g)
                    yacc[rows, hc:hc + HD] = o[g * BLK:(g + 1) * BLK]
            for g, w in enumerate(POOL_WINDOWS):
                pooled, _ = _pool_group(win, g, w)
                yp = _nn(pooled.astype(bf16), pw_ref[g].astype(bf16)) * ps_ref[:, PGD * g:PGD * (g + 1)]
                yacc[rows, DATTN + PGD * g:DATTN + PGD * (g + 1)] = yp
        y_ref[...] = yacc[...].astype(bf16)

    full = lambda *shape: pl.BlockSpec(shape, lambda n: (0,) * len(shape))
    smem = pl.BlockSpec(memory_space=pltpu.SMEM)
    return pl.pallas_call(
        body, grid=(nsteps,),
        in_specs=[pl.BlockSpec((step_rows, DIN), lambda n: (n, 0)),
                  pl.BlockSpec((BLK, DIN), lambda n: (jnp.maximum(n * MIX_SUB - 1, 0), 0)),
                  full(1, HD), full(1, HD), smem, smem, full(BLK, 2 * BLK),
                  full(len(POOL_WINDOWS), PGD, PGD), full(1, DPOOL)],
        out_specs=[pl.BlockSpec((step_rows, DMIX), lambda n: (n, 0)),
                   pl.BlockSpec((MIX_SUB, NH, BLK, 2 * BLK), lambda n: (n, 0, 0, 0))],
        out_shape=(SDS((T, DMIX), bf16), SDS((T // BLK, NH, BLK, 2 * BLK), bf16)),
        scratch_shapes=[pltpu.VMEM((NH, BLK, 2 * BLK), f32), pltpu.VMEM((step_rows, DMIX), f32)],
        compiler_params=_cparams(("arbitrary",)), name=name)(z, z, qg, kg, sinks, relb, bucket, pool_w, pscale)


def _mix_bwd(z, dy, probs, qg, kg, relb, bucket, pool_w, pscale, name):
    T = z.shape[0]
    step_rows = MIX_SUB * BLK
    nsteps = T // step_rows

    def body(zc_ref, zp_ref, dy_ref, p_ref, qg_ref, kg_ref, bk_ref, pw_ref, ps_ref,
             dz_ref, dqg_ref, dkg_ref, dsk_ref, drb_ref, dpw_ref, dps_ref, dbias_scr):
        n = pl.program_id(0)

        @pl.when(n == 0)
        def _():
            dbias_scr[...] = jnp.zeros_like(dbias_scr)
            dqg_ref[...] = jnp.zeros_like(dqg_ref)
            dkg_ref[...] = jnp.zeros_like(dkg_ref)
            dpw_ref[...] = jnp.zeros_like(dpw_ref)
            dps_ref[...] = jnp.zeros_like(dps_ref)

        qg, kg = qg_ref[...], kg_ref[...]
        for s in range(MIX_SUB):
            win = _Window(zc_ref, zp_ref, n, s)
            blk = win.blk
            rows = pl.ds(pl.multiple_of(blk * BLK, BLK), BLK)
            prow = pl.ds(pl.multiple_of(jnp.maximum(blk - 1, 0) * BLK, BLK), BLK)
            dyr = slice(s * BLK, (s + 1) * BLK)

            def into_prev(fn, s=s):
                if s == 0:
                    pl.when(n > 0)(fn)
                else:
                    fn()

            for kh in range(NKV):
                a = _attn_qkv(win, kh, qg, kg)
                pb = p_ref[s, GQA * kh:GQA * (kh + 1)].reshape(GQA * BLK, 2 * BLK)
                p = pb.astype(f32)
                do = jnp.concatenate([dy_ref[dyr, HD * (GQA * kh + g):HD * (GQA * kh + g + 1)] for g in range(GQA)],
                                     axis=0).astype(bf16)
                dv = _tn(pb, do)
                dp = _nt(do, a["vb"])
                delta = jnp.sum(p * dp, axis=-1, keepdims=True)
                ds = p * (dp - delta)
                for g in range(GQA):
                    dbias_scr[GQA * kh + g] += ds[g * BLK:(g + 1) * BLK]
                dsb = ds.astype(bf16)
                dqn = _nn(dsb, a["knb"]) * SCALE
                dkn = _tn(dsb, a["qsb"])
                qhat, khat = a["qhat"], a["khat"]
                dqg_ref[...] += jnp.sum(dqn * qhat, axis=0, keepdims=True)
                dkg_ref[...] += jnp.sum(dkn * khat, axis=0, keepdims=True)
                dqh = dqn * qg
                dq = a["rq"] * (dqh - qhat * jnp.mean(dqh * qhat, axis=-1, keepdims=True))
                dkh = dkn * kg
                dk = a["rk"] * (dkh - khat * jnp.mean(dkh * khat, axis=-1, keepdims=True))
                kc = DATTN + HD * kh
                vc = DATTN + DKV + HD * kh
                for g in range(GQA):
                    hc = HD * (GQA * kh + g)
                    dz_ref[rows, hc:hc + HD] = dq[g * BLK:(g + 1) * BLK]
                dz_ref[rows, kc:kc + HD] = dk[BLK:2 * BLK]
                dz_ref[rows, vc:vc + HD] = dv[BLK:2 * BLK]

                def kv_prev(dk=dk, dv=dv, kc=kc, vc=vc, prow=prow):
                    dz_ref[prow, kc:kc + HD] += dk[0:BLK]
                    dz_ref[prow, vc:vc + HD] += dv[0:BLK]

                into_prev(kv_prev)

            for g, w in enumerate(POOL_WINDOWS):
                c0 = DATTN + 2 * DKV + PGD * g
                pooled, cnt = _pool_group(win, g, w)
                pb = pooled.astype(bf16)
                wb = pw_ref[g].astype(bf16)
                dyp = dy_ref[dyr, DATTN + PGD * g:DATTN + PGD * (g + 1)]
                ypre = _nn(pb, wb)
                dps_ref[:, PGD * g:PGD * (g + 1)] += jnp.sum(dyp * ypre, axis=0, keepdims=True)
                dyg = (dyp * ps_ref[:, PGD * g:PGD * (g + 1)]).astype(bf16)
                dpw_ref[g] += _tn(pb, dyg)
                dpooled = _nt(dyg, wb)
                due = jnp.concatenate([jnp.zeros((BLK, PGD), f32), dpooled / cnt], axis=0)
                for k in POOL_STEPS[w]:
                    due = due + pltpu.roll(due, 2 * BLK - k, axis=0)
                dz_ref[rows, c0:c0 + PGD] = due[BLK:2 * BLK] - dpooled

                def pool_prev(due=due, c0=c0, prow=prow):
                    dz_ref[prow, c0:c0 + PGD] += due[0:BLK]

                into_prev(pool_prev)

        @pl.when(n == nsteps - 1)
        def _():
            bk = bk_ref[...]
            ri = lax.broadcasted_iota(i32, (NBUCK, NH), 0)
            ci = lax.broadcasted_iota(i32, (NBUCK, NH), 1)

            def step(b, acc):
                for h in range(NH):
                    sel = jnp.where(bk == b, dbias_scr[h], 0.0)
                    tot = jnp.sum(jnp.sum(sel, axis=1, keepdims=True), axis=0, keepdims=True)
                    acc = acc + jnp.where((ri == b) & (ci == h), tot, 0.0)
                return acc

            drb_ref[...] = lax.fori_loop(0, NBUCK, step, jnp.zeros((NBUCK, NH), f32))
            lane = lax.broadcasted_iota(i32, (1, 128), 1)
            dsk = jnp.zeros((1, 128), f32)
            for h in range(NH):
                tot = jnp.sum(jnp.sum(dbias_scr[h], axis=1, keepdims=True), axis=0, keepdims=True)
                dsk = dsk - jnp.where(lane == h, tot, 0.0)
            dsk_ref[...] = dsk

    full = lambda *shape: pl.BlockSpec(shape, lambda n: (0,) * len(shape))
    npg = len(POOL_WINDOWS)
    return pl.pallas_call(
        body, grid=(nsteps,),
        in_specs=[pl.BlockSpec((step_rows, DIN), lambda n: (n, 0)),
                  pl.BlockSpec((BLK, DIN), lambda n: (jnp.maximum(n * MIX_SUB - 1, 0), 0)),
                  pl.BlockSpec((step_rows, DMIX), lambda n: (n, 0)),
                  pl.BlockSpec((MIX_SUB, NH, BLK, 2 * BLK), lambda n: (n, 0, 0, 0)),
                  full(1, HD), full(1, HD), full(BLK, 2 * BLK), full(npg, PGD, PGD), full(1, DPOOL)],
        out_specs=[full(T, DIN), full(1, HD), full(1, HD), full(1, 128), full(NBUCK, NH),
                   full(npg, PGD, PGD), full(1, DPOOL)],
        out_shape=(SDS((T, DIN), f32), SDS((1, HD), f32), SDS((1, HD), f32), SDS((1, 128), f32),
                   SDS((NBUCK, NH), f32), SDS((npg, PGD, PGD), f32), SDS((1, DPOOL), f32)),
        scratch_shapes=[pltpu.VMEM((NH, BLK, 2 * BLK), f32)],
        compiler_params=_cparams(("arbitrary",), VMEM_LIMIT_V7X),
        name=name)(z, z, dy, probs, qg, kg, bucket, pool_w, pscale)


class _LocalWeights:
    def __init__(self, w1, wint, wout, w2):
        self.w1, self.wint, self.wout, self.w2 = w1, wint, wout, w2

    def ffn1(self):
        return self.w1

    def first_norm(self, x, gain):
        return _norm_fwd(x, gain, "norm1_fwd")

    def after_ffn1(self, gain, x1):
        return gain

    def mix(self, after):
        return self.wint, self.wout

    def before_out_proj(self, wout, after):
        return wout

    def ffn2(self, after):
        return self.w2

    def out_ffn2_grads_ready(self, dwout, dw2, after):
        return after

    def before_ffn1_bwd(self, dwint, dx1b):
        return dx1b


def _local_step(x, target, weights, g1, gm, g3, qg, kg, sinks, relb, pool_w, pscale):
    bucket = jnp.asarray(_t5_bucket_table())
    sk = sinks.reshape(NH)
    w1 = weights.ffn1()
    h1 = weights.first_norm(x, g1)
    x1, gate1, up1, h2 = _ffn_fwd(h1, w1, x, None, gm, "ffn1_fwd")
    gm = weights.after_ffn1(gm, x1)
    wint, wout = weights.mix(h2)
    z = _in_proj_fwd(h2, wint, "in_proj_fwd")
    ymix, probs = _mix_fwd(z, qg, kg, sk, relb, bucket, pool_w, pscale, "mix_fwd")
    wout = weights.before_out_proj(wout, ymix)
    x2, h3 = _out_proj_fwd(ymix, wout, x1, g3, "out_proj_fwd")
    w2 = weights.ffn2(h3)
    dy, gate2, up2, dyb, loss_lanes = _ffn_fwd(h3, w2, x2, target, None, "ffn2_fwd")

    dx2, dx2b, dg3, dw2 = _ffn_bwd(dyb, h3, gate2, up2, w2, (x2, g3, dy), "ffn2_bwd")
    dymix, dwout = _out_proj_bwd(dx2b, wout, ymix, "out_proj_bwd")
    dymix = weights.out_ffn2_grads_ready(dwout, dw2, dymix)
    dz, dqg, dkg, dsk, drb, dpw, dps = _mix_bwd(z, dymix, probs, qg, kg, relb, bucket, pool_w, pscale, "mix_bwd")
    dx1, dx1b, dgm, dwint = _in_proj_bwd(dz, wint, h2, (x1, gm, dx2), 0.5, "in_proj_bwd")
    dx1b = weights.before_ffn1_bwd(dwint, dx1b)
    dh1, dw1 = _ffn_bwd(dx1b, h1, gate1, up1, w1, None, "ffn1_bwd")
    small = dict(mix_norm=dgm, ffn2_norm=dg3, pool_scale=dps, q_norm=dqg, k_norm=dkg,
                 attn_sinks=dsk[:, :NH], rel_bias=drb, pool_w=dpw, loss=loss_lanes)
    return (dh1, dx1), (dw1, dwint, dwout, dw2), small


SMALL_NAMES = ("ffn1_norm", "mix_norm", "ffn2_norm", "pool_scale", "q_norm", "k_norm", "attn_sinks", "rel_bias",
               "pool_w", "loss")
SMALL_SHAPES = dict(ffn1_norm=(1, D), mix_norm=(1, D), ffn2_norm=(1, D), pool_scale=(1, DPOOL), q_norm=(1, HD),
                    k_norm=(1, HD), attn_sinks=(1, NH), rel_bias=(NBUCK, NH),
                    pool_w=(1, len(POOL_WINDOWS), PGD, PGD), loss=(1, 128))


def _small_rows(name):
    return -(-int(np.prod(SMALL_SHAPES[name])) // 128)


SMALL_OFF = {}
_r = 0
for _n in SMALL_NAMES:
    SMALL_OFF[_n] = _r
    _r += _small_rows(_n)
SMALL_ROWS = -(-_r // 16) * 16
LOSS_ROW = SMALL_OFF["loss"]


def _pack_small(vals):
    parts = []
    for n in SMALL_NAMES:
        size = _small_rows(n) * 128
        if n in vals:
            flat = vals[n].astype(f32).reshape(-1)
            parts.append(jnp.pad(flat, (0, size - flat.shape[0])))
        else:
            parts.append(jnp.zeros((size,), f32))
    flat = jnp.concatenate(parts)
    flat = jnp.pad(flat, (0, SMALL_ROWS * 128 - flat.shape[0]))
    return flat.reshape(SMALL_ROWS, 128)


def _unpack_small(packed, name):
    size = int(np.prod(SMALL_SHAPES[name]))
    r0 = SMALL_OFF[name]
    return packed[r0:r0 + _small_rows(name)].reshape(-1)[:size].reshape(SMALL_SHAPES[name])


def _position():
    return lax.axis_index("x"), lax.axis_index("y"), lax.axis_index("c")


def _dev_index(x, y, c):
    return 4 * x + 2 * y + c


G1_PIECES, MIX_PIECES, F2_PIECES = (0, 1, 2), (3, 4), (5, 6, 7)


def _group_rows(pieces):
    return sum(PIECE_ROWS[k] for k in pieces)


def _shard_piece(s_ref, k):
    return s_ref.at[pl.ds(PIECE_OFF[k], PIECE_ROWS[k]), :]


def _shard_group(s_ref, pieces):
    return s_ref.at[pl.ds(PIECE_OFF[pieces[0]], _group_rows(pieces)), :]


def _weight_pieces(w1_ref=None, wi_ref=None, wo_ref=None, w2_ref=None):
    arrs = {}
    if w1_ref is not None:
        arrs.update({0: w1_ref.at[0], 1: w1_ref.at[1], 2: w1_ref.at[2]})
    if wi_ref is not None:
        arrs[3] = wi_ref
    if wo_ref is not None:
        arrs[4] = wo_ref
    if w2_ref is not None:
        arrs.update({5: w2_ref.at[0], 6: w2_ref.at[1], 7: w2_ref.at[2]})
    return arrs


def _block_rows(arrs, k, dev):
    r = PIECE_ROWS[k]
    return arrs[k].at[pl.ds(pl.multiple_of(_dev_index(*dev) * r, 16), r), :]


NORM_ROWS = 512


def _all_gather_ffn1(shard, x, gain):
    pieces = G1_PIECES
    rest_pieces = MIX_PIECES + F2_PIECES
    half = FS // 2
    T = x.shape[0]
    SIB, X0, X1, Y0, Y1, RELAY_Y, RELAY_X, ON_X, ON_Y, ON_D0, ON_D1 = range(11)

    def body(s_ref, x_ref, g_ref, w1_ref, h_ref, wi_ref, wo_ref, w2_ref, xbuf, hbuf, rest_buf,
             send_sems, recv_sems, local_sem, norm_sems):
        x, y, c = _position()
        me, sib = (x, y, c), (x, y, 1 - c)
        xn, yn, dg = (1 - x, y, c), (x, 1 - y, c), (1 - x, 1 - y, c)
        arrs = _weight_pieces(w1_ref=w1_ref)

        def place_rest():
            rest = _weight_pieces(wi_ref=wi_ref, wo_ref=wo_ref, w2_ref=w2_ref)
            grp = _shard_group(s_ref, rest_pieces)
            load = pltpu.make_async_copy(grp, rest_buf, norm_sems.at[0])
            load.start()
            load.wait()
            base = PIECE_OFF[rest_pieces[0]]
            for k in rest_pieces:
                pltpu.make_async_copy(rest_buf.at[pl.ds(PIECE_OFF[k] - base, PIECE_ROWS[k]), :],
                                      _block_rows(rest, k, me), norm_sems.at[1]).start()
            pltpu.make_async_copy(grp, rest_buf, norm_sems.at[1]).wait()

        def first_norm():
            for r in range(0, T, NORM_ROWS):
                load = pltpu.make_async_copy(x_ref.at[pl.ds(r, NORM_ROWS), :], xbuf, norm_sems.at[0])
                load.start()
                load.wait()
                xv = xbuf[...]
                rs = lax.rsqrt(jnp.mean(xv * xv, axis=-1, keepdims=True) + EPS)
                hbuf[...] = (xv * rs * g_ref[...]).astype(bf16)
                store = pltpu.make_async_copy(hbuf, h_ref.at[pl.ds(r, NORM_ROWS), :], norm_sems.at[1])
                store.start()
                store.wait()

        def rows_of(k, block, hf):
            r = PIECE_ROWS[k]
            start, size = (0, r) if hf is None else (hf * half, half)
            return arrs[k].at[pl.ds(pl.multiple_of(_dev_index(*block) * r + start, 16), size), :]

        def copies(rel, block, hf, to, from_shard=False):
            def src(k):
                if not from_shard:
                    return rows_of(k, block, hf)
                start, size = (0, PIECE_ROWS[k]) if hf is None else (hf * half, half)
                return s_ref.at[pl.ds(PIECE_OFF[k] + start, size), :]
            return [pltpu.make_async_remote_copy(
                src_ref=src(k), dst_ref=rows_of(k, block, hf), send_sem=send_sems.at[rel], recv_sem=recv_sems.at[rel],
                device_id=to, device_id_type=MESH) for k in pieces]

        def waiter(rel, hf):
            nrows = len(pieces) * (FS if hf is None else half)
            grp = s_ref.at[pl.ds(0, nrows), :]
            return pltpu.make_async_remote_copy(src_ref=grp, dst_ref=grp, send_sem=send_sems.at[rel],
                                                recv_sem=recv_sems.at[rel], device_id=me, device_id_type=MESH)

        def start(cps):
            for cp in cps:
                cp.start()

        mine = [pltpu.make_async_copy(_shard_piece(s_ref, k), _block_rows(arrs, k, me), local_sem) for k in pieces]
        start(mine)
        start(copies(SIB, me, None, sib, True))
        start(copies(X0, me, 0, xn, True))
        start(copies(Y1, me, 1, yn, True))
        start(copies(X1, me, 1, xn, True))
        start(copies(Y0, me, 0, yn, True))
        first_norm()
        place_rest()
        waiter(X0, 0).wait_recv()
        start(copies(RELAY_Y, xn, 0, yn))
        waiter(Y1, 1).wait_recv()
        start(copies(RELAY_X, yn, 1, xn))
        waiter(X1, 1).wait_recv()
        start(copies(ON_X, xn, None, sib))
        waiter(Y0, 0).wait_recv()
        start(copies(ON_Y, yn, None, sib))
        waiter(RELAY_Y, 0).wait_recv()
        start(copies(ON_D0, dg, 0, sib))
        waiter(RELAY_X, 1).wait_recv()
        start(copies(ON_D1, dg, 1, sib))
        waiter(SIB, None).wait_recv()
        waiter(ON_X, None).wait_recv()
        waiter(ON_Y, None).wait_recv()
        waiter(ON_D0, 0).wait_recv()
        waiter(ON_D1, 1).wait_recv()
        for rel, hf in ((SIB, None), (X0, 0), (X1, 1), (Y0, 0), (Y1, 1), (RELAY_Y, 0), (RELAY_X, 1),
                        (ON_X, None), (ON_Y, None), (ON_D0, 0), (ON_D1, 1)):
            waiter(rel, hf).wait_send()
        grp = _shard_group(s_ref, pieces)
        pltpu.make_async_copy(grp, grp, local_sem).wait()

    hbm = pl.BlockSpec(memory_space=pl.ANY)
    return pl.pallas_call(
        body, in_specs=[hbm, hbm, pl.BlockSpec(memory_space=pltpu.VMEM)], out_specs=[hbm] * 5,
        out_shape=(SDS((3, F, D), bf16), SDS((T, D), bf16),
                   SDS((DIN, D), bf16), SDS((DMIX, D), bf16), SDS((3, F, D), bf16)),
        scratch_shapes=[pltpu.VMEM((NORM_ROWS, D), f32), pltpu.VMEM((NORM_ROWS, D), bf16),
                        pltpu.VMEM((_group_rows(rest_pieces), D), bf16),
                        pltpu.SemaphoreType.DMA((11,)), pltpu.SemaphoreType.DMA((11,)), pltpu.SemaphoreType.DMA,
                        pltpu.SemaphoreType.DMA((2,))],
        compiler_params=pltpu.CompilerParams(has_side_effects=True),
        name="all_gather_ffn1")(shard, x, gain)


HBM_SPEC = pl.BlockSpec(memory_space=pltpu.HBM)
SEM_SPEC = pl.BlockSpec(memory_space=pltpu.SEMAPHORE)
ANY_SPEC = pl.BlockSpec(memory_space=pl.ANY)
SPLIT_EFFECT = pltpu.SideEffectType.DATAFLOW_SIDE_EFFECTING


def _in_hbm(a):
    return pltpu.with_memory_space_constraint(a, pltpu.HBM)


def _hbm_like(a):
    return pltpu.HBM(a.shape, a.dtype)


def _gather_rest_start(shard, wi, wo, w2, w1):
    def body(s_ref, wi_ref, wo_ref, w2_ref, w1_ref,
             ssem_m, rsem_m0, rsem_m, ssem_f, rsem_f0, rsem_f, s_o, wi_o, wo_o, w2_o, w1_o):
        x, y, c = _position()
        me, sib = (x, y, c), (x, y, 1 - c)
        chips = [(1 - x, y), (x, 1 - y), (1 - x, 1 - y)]
        arrs = _weight_pieces(wi_ref=wi_ref, wo_ref=wo_ref, w2_ref=w2_ref)
        for pieces, ssem, rsem0, rsem in ((MIX_PIECES, ssem_m, rsem_m0, rsem_m), (F2_PIECES, ssem_f, rsem_f0, rsem_f)):
            for p in pieces:
                pltpu.make_async_remote_copy(
                    src_ref=_shard_piece(s_ref, p), dst_ref=_block_rows(arrs, p, me), send_sem=ssem.at[0],
                    recv_sem=rsem0, device_id=sib, device_id_type=MESH).start()
            for j, chip in enumerate(chips):
                for p in pieces:
                    pltpu.make_async_remote_copy(
                        src_ref=_shard_piece(s_ref, p), dst_ref=_block_rows(arrs, p, me), send_sem=ssem.at[1 + j],
                        recv_sem=rsem.at[j], device_id=(*chip, c), device_id_type=MESH).start()

    dma = pltpu.SemaphoreType.DMA
    return pl.pallas_call(
        body, name="gather_rest_start",
        out_shape=(dma((4,)), dma(()), dma((3,)), dma((4,)), dma(()), dma((3,)),
                   _hbm_like(shard), _hbm_like(wi), _hbm_like(wo), _hbm_like(w2), _hbm_like(w1)),
        in_specs=(HBM_SPEC,) * 5, out_specs=(SEM_SPEC,) * 6 + (HBM_SPEC,) * 5,
        input_output_aliases={0: 6, 1: 7, 2: 8, 3: 9, 4: 10},
        compiler_params=pltpu.CompilerParams(has_side_effects=SPLIT_EFFECT),
    )(_in_hbm(shard), _in_hbm(wi), _in_hbm(wo), _in_hbm(w2), _in_hbm(w1))


def _gather_mix_pass_on(rsem_m, wi, wo, thru, after):
    def body(wi_ref, wo_ref, thru_ref, rsem, after_ref, fsend, frecv, wi_o, wo_o, thru_o):
        x, y, c = _position()
        sib = (x, y, 1 - c)
        arrs = _weight_pieces(wi_ref=wi_ref, wo_ref=wo_ref)
        both = wi_ref.at[pl.ds(0, _group_rows(MIX_PIECES)), :]
        for j, chip in enumerate([(1 - x, y), (x, 1 - y), (1 - x, 1 - y)]):
            pltpu.make_async_remote_copy(src_ref=both, dst_ref=both, send_sem=fsend.at[j], recv_sem=rsem.at[j],
                                         device_id=(x, y, c), device_id_type=MESH).wait_recv()
            for p in MIX_PIECES:
                rows = _block_rows(arrs, p, (*chip, c))
                pltpu.make_async_remote_copy(src_ref=rows, dst_ref=rows, send_sem=fsend.at[j], recv_sem=frecv.at[j],
                                             device_id=sib, device_id_type=MESH).start()

    dma = pltpu.SemaphoreType.DMA
    return pl.pallas_call(
        body, name="gather_mix_pass_on",
        out_shape=(dma((3,)), dma((3,)), _hbm_like(wi), _hbm_like(wo), _hbm_like(thru)),
        in_specs=(HBM_SPEC, HBM_SPEC, HBM_SPEC, SEM_SPEC, ANY_SPEC), out_specs=(SEM_SPEC, SEM_SPEC) + (HBM_SPEC,) * 3,
        input_output_aliases={0: 2, 1: 3, 2: 4},
        compiler_params=pltpu.CompilerParams(has_side_effects=SPLIT_EFFECT),
    )(wi, wo, _in_hbm(thru), rsem_m, after)


def _gather_mix_wait(ssem_m, rsem_m0, fsend, frecv, shard, wi, wo, after):
    def body(s_ref, wi_ref, wo_ref, ssem, rsem0, fs, fr, after_ref, s_o, wi_o, wo_o):
        x, y, c = _position()
        grp = _shard_group(s_ref, MIX_PIECES)

        def waiter(send_sem, recv_sem):
            return pltpu.make_async_remote_copy(src_ref=grp, dst_ref=grp, send_sem=send_sem, recv_sem=recv_sem,
                                                device_id=(x, y, c), device_id_type=MESH)

        waiter(ssem.at[0], rsem0).wait_recv()
        for j in range(3):
            waiter(fs.at[j], fr.at[j]).wait_recv()
        for rel in range(4):
            waiter(ssem.at[rel], rsem0).wait_send()
        for j in range(3):
            waiter(fs.at[j], fr.at[j]).wait_send()

    return pl.pallas_call(
        body, name="gather_mix_wait", out_shape=(_hbm_like(shard), _hbm_like(wi), _hbm_like(wo)),
        in_specs=(HBM_SPEC,) * 3 + (SEM_SPEC,) * 4 + (ANY_SPEC,), out_specs=(HBM_SPEC,) * 3,
        input_output_aliases={0: 0, 1: 1, 2: 2},
        compiler_params=pltpu.CompilerParams(has_side_effects=SPLIT_EFFECT),
    )(shard, wi, wo, ssem_m, rsem_m0, fsend, frecv, after)


def _gather_ffn2_pass_on(rsem_f, w2, wo, after):
    def body(w2_ref, wo_ref, rsem, after_ref, fsend, frecv, w2_o, wo_o):
        x, y, c = _position()
        sib = (x, y, 1 - c)
        chips = [(1 - x, y), (x, 1 - y), (1 - x, 1 - y)]
        arrs = _weight_pieces(w2_ref=w2_ref)
        three = w2_ref.at[0, pl.ds(0, _group_rows(F2_PIECES)), :]
        for j, chip in enumerate(chips):
            pltpu.make_async_remote_copy(src_ref=three, dst_ref=three, send_sem=fsend.at[j], recv_sem=rsem.at[j],
                                         device_id=(x, y, c), device_id_type=MESH).wait_recv()
            for p in F2_PIECES:
                rows = _block_rows(arrs, p, (*chip, c))
                pltpu.make_async_remote_copy(src_ref=rows, dst_ref=rows, send_sem=fsend.at[j], recv_sem=frecv.at[j],
                                             device_id=sib, device_id_type=MESH).start()

    dma = pltpu.SemaphoreType.DMA
    return pl.pallas_call(
        body, name="gather_ffn2_pass_on", out_shape=(dma((3,)), dma((3,)), _hbm_like(w2), _hbm_like(wo)),
        in_specs=(HBM_SPEC, HBM_SPEC, SEM_SPEC, ANY_SPEC), out_specs=(SEM_SPEC, SEM_SPEC, HBM_SPEC, HBM_SPEC),
        input_output_aliases={0: 2, 1: 3},
        compiler_params=pltpu.CompilerParams(has_side_effects=SPLIT_EFFECT),
    )(w2, wo, rsem_f, after)


def _gather_ffn2_wait(ssem_f, rsem_f0, fsend, frecv, shard, w2, after):
    def body(s_ref, w2_ref, ssem, rsem0, fs, fr, after_ref, w2_o):
        x, y, c = _position()
        grp = _shard_group(s_ref, F2_PIECES)

        def waiter(send_sem, recv_sem):
            return pltpu.make_async_remote_copy(src_ref=grp, dst_ref=grp, send_sem=send_sem, recv_sem=recv_sem,
                                                device_id=(x, y, c), device_id_type=MESH)

        waiter(ssem.at[0], rsem0).wait_recv()
        for j in range(3):
            waiter(fs.at[j], fr.at[j]).wait_recv()
        for rel in range(4):
            waiter(ssem.at[rel], rsem0).wait_send()
        for j in range(3):
            waiter(fs.at[j], fr.at[j]).wait_send()

    return pl.pallas_call(
        body, name="gather_ffn2_wait", out_shape=_hbm_like(w2),
        in_specs=(HBM_SPEC, HBM_SPEC, SEM_SPEC, SEM_SPEC, SEM_SPEC, SEM_SPEC, ANY_SPEC), out_specs=HBM_SPEC,
        input_output_aliases={1: 0},
        compiler_params=pltpu.CompilerParams(has_side_effects=SPLIT_EFFECT),
    )(shard, w2, ssem_f, rsem_f0, fsend, frecv, after)


class _GatheredWeights(_LocalWeights):
    def __init__(self, shard, x, gain1):
        w1, self.h1, wi, wo, w2 = _all_gather_ffn1(shard, x, gain1)
        (self.ssem_m, self.rsem_m0, self.rsem_m, self.ssem_f, self.rsem_f0, self.rsem_f,
         self.shard, self.wi, self.wo, self.w2_part, self.w1) = _gather_rest_start(shard, wi, wo, w2, w1)

    def first_norm(self, x, gain):
        return self.h1

    def after_ffn1(self, gain, x1):
        self.fsend_m, self.frecv_m, self.wi, self.wo, gain = _gather_mix_pass_on(self.rsem_m, self.wi, self.wo, gain, x1)
        return gain

    def mix(self, after):
        self.shard, wint, wout = _gather_mix_wait(self.ssem_m, self.rsem_m0, self.fsend_m, self.frecv_m, self.shard,
                                                  self.wi, self.wo, after)
        return wint, wout

    def before_out_proj(self, wout, after):
        self.fsend, self.frecv, self.w2_part, wout = _gather_ffn2_pass_on(self.rsem_f, self.w2_part, wout, after)
        return wout

    def ffn2(self, after):
        return _gather_ffn2_wait(self.ssem_f, self.rsem_f0, self.fsend, self.frecv, self.shard, self.w2_part, after)

    def out_ffn2_grads_ready(self, dwout, dw2, after):
        rx1 = lax.empty((4, RSA_ROWS, D), bf16)
        sa, ra, sent, rx1, after = _rsa_level1_start(dict(wo=dwout, w2=dw2), rx1, after, "rsa_level1_start_out_ffn2")
        self.level1 = ((sa, ra), sent, rx1)
        return after

    def before_ffn1_bwd(self, dwint, dx1b):
        early, sent, rx1 = self.level1
        sa, ra, late, rx1, dx1b = _rsa_level1_start(dict(wi=dwint), rx1, dx1b, "rsa_level1_start_in")
        started = (((MIX_PIECES[1],) + F2_PIECES, *early), ((MIX_PIECES[0],), sa, ra))
        rx2 = lax.empty((3, RSA_ROWS, D), bf16)
        self.sb, self.rb, self.tx, self.acc, self.rx2, dx1b = _rsa_sums_and_send(
            started, late["wi"], sent["wo"], sent["w2"], rx1, rx2, dx1b)
        return dx1b

    def mix_ffn2_grads_parts(self, after):
        rx2 = _rsa_level2_wait(self.sb, self.rb, self.tx, self.rx2, after)
        return self.acc, rx2


def _reduce_scatter_ffn1_head(dw1, small_packed, first_norm):
    pieces = G1_PIECES
    half = FS // 2
    hrows = len(pieces) * half
    nrows = 2 * hrows
    X_RELAY, Y_RELAY = range(2)

    T = first_norm[0].shape[0]

    def body(d1_ref, p_ref, dh_hbm, x_hbm, gain_ref, dr_hbm,
             forx_ref, fory_ref, own_ref, rx1_ref, relx_ref, rely_ref, gx_hbm, tot_ref,
             own_buf, rx_buf, tx1, tx2, tx3, acc, sa, ra, sb, rb, lsem, pair, chips, small_tot, small_send, small_recv,
             xbuf, rbuf, hbuf, obuf, norm_in_sems, norm_out_sems):
        x, y, c = _position()
        me, sib = (x, y, c), (x, y, 1 - c)
        xn, yn = (1 - x, y, c), (x, 1 - y, c)
        rel_chips = [(x, y), (1 - x, y), (x, 1 - y), (1 - x, 1 - y)]
        srcs = _weight_pieces(w1_ref=d1_ref)

        my_chip = 2 * x + y
        pair[c] = p_ref[...]
        swap = pltpu.make_async_remote_copy(
            src_ref=pair.at[c], dst_ref=pair.at[c], send_sem=small_send.at[0], recv_sem=small_recv.at[0],
            device_id=sib, device_id_type=MESH)
        mine = pl.ds(pl.multiple_of(c * (SMALL_ROWS // 2), 8), SMALL_ROWS // 2)
        small = [pltpu.make_async_remote_copy(
            src_ref=chips.at[my_chip, mine, :], dst_ref=chips.at[my_chip, mine, :], send_sem=small_send.at[j],
            recv_sem=small_recv.at[j], device_id=(*rel_chips[j], c), device_id_type=MESH) for j in (1, 2, 3)]
        give = pltpu.make_async_remote_copy(
            src_ref=small_tot.at[mine, :], dst_ref=small_tot.at[mine, :], send_sem=small_send.at[4],
            recv_sem=small_recv.at[4], device_id=sib, device_id_type=MESH)

        def part(k, dev, hf):
            r = PIECE_ROWS[k]
            return srcs[k].at[pl.ds(pl.multiple_of(_dev_index(*dev) * r + hf * half, 16), half), :]

        def slot(ref, k, hf):
            return ref.at[pl.ds(hf * hrows + k * half, half), :]

        halves = [(k, hf) for hf in (0, 1) for k in pieces]

        for j in (3, 1, 2, 0):
            for k, hf in halves:
                pltpu.make_async_remote_copy(
                    src_ref=part(k, (*rel_chips[j], 1 - c), hf), dst_ref=slot(rx1_ref.at[j], k, hf),
                    send_sem=sa.at[j], recv_sem=ra.at[j], device_id=sib, device_id_type=MESH).start()

        def wait_a(j):
            return pltpu.make_async_remote_copy(src_ref=rx1_ref.at[j], dst_ref=rx1_ref.at[j], send_sem=sa.at[j],
                                                recv_sem=ra.at[j], device_id=me, device_id_type=MESH)

        def ici(rel, src, dst, to):
            return pltpu.make_async_remote_copy(src_ref=src, dst_ref=dst, send_sem=sb.at[rel], recv_sem=rb.at[rel],
                                                device_id=to, device_id_type=MESH)

        first, second = pl.ds(0, hrows), pl.ds(hrows, hrows)
        sends = {
            X_RELAY: ici(X_RELAY, tx3.at[first, :], relx_ref, xn),
            Y_RELAY: ici(Y_RELAY, tx3.at[second, :], rely_ref, yn),
        }

        def chip_sum(j, dst):
            loads = [pltpu.make_async_copy(part(k, (*rel_chips[j], c), hf), slot(own_buf, k, hf), lsem.at[0])
                     for k, hf in halves]
            for cp in loads:
                cp.start()
            wait_a(j).wait_recv()
            got = pltpu.make_async_copy(rx1_ref.at[j], rx_buf, lsem.at[1])
            got.start()
            pltpu.make_async_copy(rx_buf, rx_buf, lsem.at[0]).wait()
            got.wait()

            def add(i, carry):
                rows = pl.ds(pl.multiple_of(i * half, 16), half)
                tot = own_buf[rows, :].astype(f32) + rx_buf[rows, :].astype(f32)
                dst[rows, :] = tot.astype(dst.dtype)
                return carry

            lax.fori_loop(0, nrows // half, add, 0)

        def add_landed(landed, dst, rows0, nrows_):
            got = pltpu.make_async_copy(landed, rx_buf.at[pl.ds(0, nrows_), :], lsem.at[1])
            got.start()
            got.wait()

            def add(i, carry):
                src_rows = pl.ds(pl.multiple_of(i * half, 16), half)
                dst_rows = pl.ds(pl.multiple_of(rows0 + i * half, 16), half)
                dst[dst_rows, :] = (dst[dst_rows, :].astype(f32) + rx_buf[src_rows, :].astype(f32)).astype(dst.dtype)
                return carry

            lax.fori_loop(0, nrows_ // half, add, 0)

        chip_sum(3, tx3)
        sends[X_RELAY].start()
        sends[Y_RELAY].start()
        dg = _norm_bwd_rows(T, dh_hbm, x_hbm, gain_ref, dr_hbm, gx_hbm.at[0], None,
                            xbuf, rbuf, hbuf, obuf, None, norm_in_sems, norm_out_sems)
        r0 = SMALL_OFF["ffn1_norm"]
        for k in range(D // 128):
            pair[c, r0 + k:r0 + k + 1, :] = dg[:, 128 * k:128 * (k + 1)]
        swap.start()
        swap.wait_recv()
        chips[my_chip] = pair[0] + pair[1]
        for cp in small:
            cp.start()
        chip_sum(1, tx1)
        chip_sum(2, tx2)
        chip_sum(0, acc)
        own_out = pltpu.make_async_copy(acc, own_ref, lsem.at[0])
        own_out.start()
        sends[X_RELAY].wait_recv()
        add_landed(relx_ref, tx2, 0, hrows)
        sends[Y_RELAY].wait_recv()
        add_landed(rely_ref, tx1, hrows, hrows)
        own_out.wait()
        outs = [pltpu.make_async_copy(tx1, forx_ref, lsem.at[0]), pltpu.make_async_copy(tx2, fory_ref, lsem.at[1])]
        for cp in outs:
            cp.start()
        for cp in outs:
            cp.wait()
        for cp in small:
            cp.wait_recv()
        small_tot[mine, :] = (chips[0, mine, :] + chips[1, mine, :]) + (chips[2, mine, :] + chips[3, mine, :])
        give.start()
        give.wait_recv()
        tot = small_tot[...]
        tot_ref[...] = tot
        loss = jnp.sum(tot[LOSS_ROW:LOSS_ROW + 1, :], axis=-1, keepdims=True)
        tot_ref[LOSS_ROW:LOSS_ROW + 1, :] = jnp.broadcast_to(loss, (1, 128))
        for j in range(4):
            wait_a(j).wait_send()
        for cp in sends.values():
            cp.wait_send()
        swap.wait_send()
        for cp in small + [give]:
            cp.wait_send()

    hbm = pl.BlockSpec(memory_space=pl.ANY)
    vm = pl.BlockSpec(memory_space=pltpu.VMEM)
    outs = pl.pallas_call(
        body, in_specs=[hbm, vm, hbm, hbm, vm, hbm], out_specs=[hbm] * 7 + [vm],
        out_shape=(SDS((nrows, D), bf16), SDS((nrows, D), bf16), SDS((nrows, D), f32), SDS((4, nrows, D), bf16),
                   SDS((hrows, D), bf16), SDS((hrows, D), bf16), SDS((1, T, D), f32), SDS((SMALL_ROWS, 128), f32)),
        scratch_shapes=[pltpu.VMEM((nrows, D), bf16), pltpu.VMEM((nrows, D), bf16),
                        pltpu.VMEM((nrows, D), bf16), pltpu.VMEM((nrows, D), bf16), pltpu.VMEM((nrows, D), bf16),
                        pltpu.VMEM((nrows, D), f32),
                        pltpu.SemaphoreType.DMA((4,)), pltpu.SemaphoreType.DMA((4,)),
                        pltpu.SemaphoreType.DMA((2,)), pltpu.SemaphoreType.DMA((2,)), pltpu.SemaphoreType.DMA((2,)),
                        pltpu.VMEM((2, SMALL_ROWS, 128), f32), pltpu.VMEM((4, SMALL_ROWS, 128), f32),
                        pltpu.VMEM((SMALL_ROWS, 128), f32),
                        pltpu.SemaphoreType.DMA((5,)), pltpu.SemaphoreType.DMA((5,))]
        + [sc for sc in _norm_bwd_scratch(True, False) if sc is not None],
        compiler_params=pltpu.CompilerParams(has_side_effects=True, vmem_limit_bytes=VMEM_LIMIT_V7X),
        name="reduce_scatter_ffn1_head")(dw1, small_packed, *first_norm)
    return outs[0], outs[1], outs[2], outs[-1], outs[-2]


def _rs1_tail_start(for_x, for_y, from_x, from_y, *thru):
    def body(fx_ref, fy_ref, lx_ref, ly_ref, *rest):
        ssem, rsem = rest[len(thru):len(thru) + 2]
        x, y, c = _position()
        pltpu.make_async_remote_copy(src_ref=fx_ref, dst_ref=lx_ref, send_sem=ssem.at[0], recv_sem=rsem.at[0],
                                     device_id=(1 - x, y, c), device_id_type=MESH).start()
        pltpu.make_async_remote_copy(src_ref=fy_ref, dst_ref=ly_ref, send_sem=ssem.at[1], recv_sem=rsem.at[1],
                                     device_id=(x, 1 - y, c), device_id_type=MESH).start()

    dma = pltpu.SemaphoreType.DMA
    arrs = (for_x, for_y, from_x, from_y, *thru)
    return pl.pallas_call(
        body, name="rs1_tail_start", out_shape=(dma((2,)), dma((2,))) + tuple(_hbm_like(a) for a in arrs),
        in_specs=(HBM_SPEC,) * len(arrs), out_specs=(SEM_SPEC,) * 2 + (HBM_SPEC,) * len(arrs),
        input_output_aliases={i: i + 2 for i in range(len(arrs))},
        compiler_params=pltpu.CompilerParams(has_side_effects=SPLIT_EFFECT),
    )(*[_in_hbm(a) for a in arrs])


RSA_PIECES = MIX_PIECES + F2_PIECES
RSA_ROWS = _group_rows(RSA_PIECES)
RSA_OFF = {k: PIECE_OFF[k] - PIECE_OFF[RSA_PIECES[0]] for k in RSA_PIECES}
RSA_BLOCK = 192


def _rsa_rows(ref, k):
    return ref.at[pl.ds(RSA_OFF[k], PIECE_ROWS[k]), :]


def _rsa_level1_start(grads, rx1, thru, name):
    keys = sorted(grads)
    n = len(keys)

    def body(*refs):
        srcs = _weight_pieces(**{k + "_ref": ref for k, ref in zip(keys, refs[:n])})
        rx1_ref, sa, ra = refs[n], refs[n + 2], refs[n + 3]
        x, y, c = _position()
        for j, chip in enumerate([(x, y), (1 - x, y), (x, 1 - y), (1 - x, 1 - y)]):
            for k in sorted(srcs):
                pltpu.make_async_remote_copy(
                    src_ref=_block_rows(srcs, k, (*chip, 1 - c)), dst_ref=_rsa_rows(rx1_ref.at[j], k),
                    send_sem=sa.at[j], recv_sem=ra.at[j], device_id=(x, y, 1 - c), device_id_type=MESH).start()

    dma = pltpu.SemaphoreType.DMA
    arrs = tuple(grads[k] for k in keys) + (rx1, thru)
    outs = pl.pallas_call(
        body, name=name, out_shape=(dma((4,)), dma((4,))) + tuple(_hbm_like(a) for a in arrs),
        in_specs=(HBM_SPEC,) * len(arrs), out_specs=(SEM_SPEC,) * 2 + (HBM_SPEC,) * len(arrs),
        input_output_aliases={i: i + 2 for i in range(len(arrs))},
        compiler_params=pltpu.CompilerParams(has_side_effects=SPLIT_EFFECT),
    )(*[_in_hbm(a) for a in arrs])
    return outs[0], outs[1], dict(zip(keys, outs[2:2 + n])), outs[2 + n], outs[3 + n]


def _rsa_sums_and_send(started, dwint, dwout, dw2, rx1, rx2, thru):
    nblk = RSA_ROWS // RSA_BLOCK
    nstart = len(started)

    def body(*refs):
        di_ref, do_ref, d2_ref, rx1_ref, rx2_ref = refs[:5]
        l1_sems = refs[6:6 + 2 * nstart]
        sb, rb, tx_ref, acc_ref = refs[6 + 2 * nstart:10 + 2 * nstart]
        own_buf, rx_buf, tx_buf, acc_buf, in_sems, out_sems = refs[13 + 2 * nstart:]
        x, y, c = _position()
        srcs = _weight_pieces(wi_ref=di_ref, wo_ref=do_ref, w2_ref=d2_ref)
        chips = [(x, y), (1 - x, y), (x, 1 - y), (1 - x, 1 - y)]

        for g, (pieces, _, _) in enumerate(started):
            ssem, rsem = l1_sems[2 * g], l1_sems[2 * g + 1]
            for j in range(4):
                rows = rx1_ref.at[j, pl.ds(RSA_OFF[pieces[0]], _group_rows(pieces)), :]
                d = pltpu.make_async_remote_copy(src_ref=rows, dst_ref=rows, send_sem=ssem.at[j], recv_sem=rsem.at[j],
                                                 device_id=(x, y, c), device_id_type=MESH)
                d.wait_recv()
                d.wait_send()

        def start_loads(j):
            s = j % 2
            for k in RSA_PIECES:
                pltpu.make_async_copy(_block_rows(srcs, k, (*chips[j], c)), _rsa_rows(own_buf.at[s], k),
                                      in_sems.at[2 * s]).start()
            pltpu.make_async_copy(rx1_ref.at[j], rx_buf.at[s], in_sems.at[2 * s + 1]).start()

        def wait_loads(j):
            s = j % 2
            pltpu.make_async_copy(rx1_ref.at[j], own_buf.at[s], in_sems.at[2 * s]).wait()
            pltpu.make_async_copy(rx1_ref.at[j], rx_buf.at[s], in_sems.at[2 * s + 1]).wait()

        def store(j):
            if j == 0:
                return pltpu.make_async_copy(acc_buf, acc_ref, out_sems.at[2])
            return pltpu.make_async_copy(tx_buf.at[j % 2], tx_ref.at[j - 1], out_sems.at[j % 2])

        def send(j):
            return pltpu.make_async_remote_copy(src_ref=tx_ref.at[j - 1], dst_ref=rx2_ref.at[j - 1], send_sem=sb.at[j - 1],
                                                recv_sem=rb.at[j - 1], device_id=(*chips[j], c), device_id_type=MESH)

        start_loads(0)
        for j in range(4):
            s = j % 2
            if j + 1 < 4:
                start_loads(j + 1)
            wait_loads(j)
            if j == 3:
                store(1).wait()
                send(1).start()

            def add(i, carry, j=j, s=s):
                rows = pl.ds(pl.multiple_of(i * RSA_BLOCK, 16), RSA_BLOCK)
                tot = own_buf[s, rows, :].astype(f32) + rx_buf[s, rows, :].astype(f32)
                if j == 0:
                    acc_buf[rows, :] = tot
                else:
                    tx_buf[s, rows, :] = tot.astype(bf16)
                return carry

            lax.fori_loop(0, nblk, add, 0)
            store(j).start()
        store(0).wait()
        for j in (2, 3):
            store(j).wait()
            send(j).start()

    dma = pltpu.SemaphoreType.DMA
    passed = (rx1, rx2, thru)
    outs = pl.pallas_call(
        body, name="rsa_sums_and_send",
        in_specs=(HBM_SPEC,) * 6 + (SEM_SPEC,) * (2 * nstart),
        out_specs=(SEM_SPEC,) * 2 + (HBM_SPEC,) * 5,
        out_shape=(dma((3,)), dma((3,)), pltpu.HBM((3, RSA_ROWS, D), bf16), pltpu.HBM((RSA_ROWS, D), f32))
        + tuple(_hbm_like(a) for a in passed),
        input_output_aliases={3: 4, 4: 5, 5: 6},
        scratch_shapes=[pltpu.VMEM((2, RSA_ROWS, D), bf16), pltpu.VMEM((2, RSA_ROWS, D), bf16),
                        pltpu.VMEM((2, RSA_ROWS, D), bf16), pltpu.VMEM((RSA_ROWS, D), f32),
                        dma((4,)), dma((3,))],
        compiler_params=pltpu.CompilerParams(has_side_effects=SPLIT_EFFECT, vmem_limit_bytes=VMEM_LIMIT_V7X),
    )(*[_in_hbm(a) for a in (dwint, dwout, dw2) + passed], *[sem for _, sa, ra in started for sem in (sa, ra)])
    sb, rb, tx, acc, _, rx2, thru = outs
    return sb, rb, tx, acc, rx2, thru


def _rsa_level2_wait(sb, rb, tx, rx2, after):
    def body(tx_ref, rx2_ref, sb_ref, rb_ref, after_ref, rx2_o):
        x, y, c = _position()
        for j in range(3):
            d = pltpu.make_async_remote_copy(src_ref=tx_ref.at[j], dst_ref=rx2_ref.at[j], send_sem=sb_ref.at[j],
                                             recv_sem=rb_ref.at[j], device_id=(x, y, c), device_id_type=MESH)
            d.wait_recv()
            d.wait_send()

    return pl.pallas_call(
        body, name="rsa_level2_wait", out_shape=_hbm_like(rx2),
        in_specs=(HBM_SPEC, HBM_SPEC, SEM_SPEC, SEM_SPEC, ANY_SPEC), out_specs=HBM_SPEC,
        input_output_aliases={1: 0},
        compiler_params=pltpu.CompilerParams(has_side_effects=SPLIT_EFFECT),
    )(tx, rx2, sb, rb, after)


def _adamw_math(w, g, m, v):
    m = ADAM_B1 * m + (1.0 - ADAM_B1) * g
    v = ADAM_B2 * v + (1.0 - ADAM_B2) * (g * g)
    m_hat = m / (1.0 - ADAM_B1 ** ADAM_STEP)
    v_hat = v / (1.0 - ADAM_B2 ** ADAM_STEP)
    delta = -ADAM_LR * (m_hat / (jnp.sqrt(v_hat) + ADAM_EPS) + ADAM_WD * w)
    return delta, m, v


def _adamw_big(pieces, ws, ms, vs, own, landed, name, in_flight=None):
    npiece = len(pieces)
    nsent = len(landed) if in_flight else 0
    nland = sum(a.shape[0] if a.ndim == 3 else 1 for a in landed)
    rmax = max(PIECE_ROWS[k] for k in pieces)
    half = FS // 2

    def segments(k):
        if k in G1_PIECES:
            return [(hf * len(G1_PIECES) * half + k * half, hf * half, half) for hf in (0, 1)]
        return [(RSA_OFF[k], 0, PIECE_ROWS[k])]

    def body(*refs):
        ins = (refs[0:npiece], refs[npiece:2 * npiece], refs[2 * npiece:3 * npiece])
        own_ref = refs[3 * npiece]
        nin = 3 * npiece + 1 + len(landed)
        land_refs = []
        for ref, a in zip(refs[3 * npiece + 1:nin], landed):
            land_refs += [ref.at[j] for j in range(a.shape[0])] if a.ndim == 3 else [ref]
        if in_flight:
            sent_refs, (ssem, rsem) = refs[nin:nin + nsent], refs[nin + nsent:nin + nsent + 2]
            nin += nsent + 3
        out_refs = refs[nin:nin + 4 * npiece]
        inb, landb, outb, in_sems, land_sems, out_sems = refs[nin + 4 * npiece + nsent:]

        def loads(i):
            s, k = i % 2, pieces[i]
            r = PIECE_ROWS[k]
            cps = [pltpu.make_async_copy(ins[q][i].at[0], inb.at[s, q, pl.ds(0, r), :], in_sems.at[4 * s + q])
                   for q in range(3)]
            waits, late = list(cps), []
            for src0, dst0, n in segments(k):
                cps.append(pltpu.make_async_copy(own_ref.at[pl.ds(src0, n), :], inb.at[s, 3, pl.ds(dst0, n), :],
                                                 in_sems.at[4 * s + 3]))
                for p in range(nland):
                    late.append(pltpu.make_async_copy(land_refs[p].at[pl.ds(src0, n), :],
                                                      landb.at[s, p, pl.ds(dst0, n), :], land_sems.at[nland * s + p]))
            own_rows = inb.at[s, 3, pl.ds(0, r), :]
            waits.append(pltpu.make_async_copy(own_rows, own_rows, in_sems.at[4 * s + 3]))
            for p in range(nland):
                rows = landb.at[s, p, pl.ds(0, r), :]
                waits.append(pltpu.make_async_copy(rows, rows, land_sems.at[nland * s + p]))
            return cps, late, waits

        def stores(i):
            s, r = i % 2, PIECE_ROWS[pieces[i]]
            return [pltpu.make_async_copy(outb.at[s, q, pl.ds(0, r), :], out_refs[q * npiece + i].at[0],
                                          out_sems.at[4 * s + q]) for q in range(4)]

        ahead = min(2, npiece) if in_flight else 1
        for i in range(ahead):
            for cp in loads(i)[0]:
                cp.start()
        if in_flight:
            x, y, c = _position()
            for j in range(nsent):
                d = pltpu.make_async_remote_copy(src_ref=sent_refs[j], dst_ref=refs[3 * npiece + 1 + j],
                                                 send_sem=ssem.at[j], recv_sem=rsem.at[j], device_id=(x, y, c),
                                                 device_id_type=MESH)
                d.wait_recv()
                d.wait_send()
        for cp in loads(0)[1]:
            cp.start()
        for i in range(npiece):
            s, r = i % 2, PIECE_ROWS[pieces[i]]
            if i + 1 < npiece:
                first, late, _ = loads(i + 1)
                for cp in late if i + 1 < ahead else first + late:
                    cp.start()
            for cp in loads(i)[2]:
                cp.wait()
            if i >= 2:
                for cp in stores(i - 2):
                    cp.wait()
            g = inb[s, 3, 0:r, :]
            for p in range(nland):
                g = g + landb[s, p, 0:r, :].astype(f32)
            d, nm, nv = _adamw_math(inb[s, 0, 0:r, :], g, inb[s, 1, 0:r, :], inb[s, 2, 0:r, :])
            outb[s, 0, 0:r, :] = g
            outb[s, 1, 0:r, :] = d
            outb[s, 2, 0:r, :] = nm
            outb[s, 3, 0:r, :] = nv
            for cp in stores(i):
                cp.start()
        for i in range(max(npiece - 2, 0), npiece):
            for cp in stores(i):
                cp.wait()

    hbm = pl.BlockSpec(memory_space=pl.ANY)
    in_specs, out_specs = [hbm] * (3 * npiece + 1), [hbm] * (4 * npiece)
    out_shape = [SDS(w.shape, f32) for _ in range(4) for w in ws]
    args, aliases, effect = [*ws, *ms, *vs, own], {}, False
    if in_flight:
        ssem, rsem, sent, after = in_flight
        in_specs += [HBM_SPEC] * (2 * nsent) + [SEM_SPEC, SEM_SPEC, hbm]
        args += [_in_hbm(a) for a in (*landed, *sent)] + [ssem, rsem, after]
        out_specs += [HBM_SPEC] * nsent
        out_shape += [_hbm_like(a) for a in landed]
        aliases = {3 * npiece + 1 + j: 4 * npiece + j for j in range(nsent)}
        effect = SPLIT_EFFECT
    else:
        in_specs += [hbm] * len(landed)
        args += list(landed)
    outs = pl.pallas_call(
        body, in_specs=in_specs, out_specs=out_specs, out_shape=tuple(out_shape), input_output_aliases=aliases,
        scratch_shapes=[pltpu.VMEM((2, 4, rmax, D), f32), pltpu.VMEM((2, nland, rmax, D), bf16),
                        pltpu.VMEM((2, 4, rmax, D), f32),
                        pltpu.SemaphoreType.DMA((8,)), pltpu.SemaphoreType.DMA((2 * nland,)),
                        pltpu.SemaphoreType.DMA((8,))],
        compiler_params=pltpu.CompilerParams(has_side_effects=effect, vmem_limit_bytes=VMEM_LIMIT_V7X),
        name=name)(*args)
    return [list(outs[q * npiece:(q + 1) * npiece]) for q in range(4)]


def _adamw_small(ws, ms, vs, gs, name):
    n = len(ws)

    def body(*refs):
        w_refs, m_refs, v_refs, g_refs = refs[0:n], refs[n:2 * n], refs[2 * n:3 * n], refs[3 * n:4 * n]
        outs = refs[4 * n:]
        for i in range(n):
            d, nm, nv = _adamw_math(w_refs[i][...], g_refs[i][...], m_refs[i][...], v_refs[i][...])
            outs[i][...] = d
            outs[n + i][...] = nm
            outs[2 * n + i][...] = nv

    outs = pl.pallas_call(
        body, out_shape=tuple(SDS(w.shape, f32) for _ in range(3) for w in ws), name=name)(*ws, *ms, *vs, *gs)
    return [list(outs[q * n:(q + 1) * n]) for q in range(3)]


WEIGHTS = ("ffn1_norm", "ffn1_w_gate", "ffn1_w_up", "ffn1_w_down", "mix_norm", "w_in", "q_norm", "k_norm",
           "attn_sinks", "rel_bias", "pool_w", "pool_scale", "w_out", "ffn2_norm", "ffn2_w_gate", "ffn2_w_up",
           "ffn2_w_down")
BIG = (("ffn1_w_gate", True), ("ffn1_w_up", True), ("ffn1_w_down", False), ("w_in", True), ("w_out", False),
       ("ffn2_w_gate", True), ("ffn2_w_up", True), ("ffn2_w_down", False))


def kernel(x, ffn1_norm, ffn1_w_gate, ffn1_w_up, ffn1_w_down, mix_norm, w_in, q_norm, k_norm, attn_sinks, rel_bias, pool_w, pool_scale, w_out, ffn2_norm, ffn2_w_gate, ffn2_w_up, ffn2_w_down, loss_target, m_ffn1_norm, m_ffn1_w_gate, m_ffn1_w_up, m_ffn1_w_down, m_mix_norm, m_w_in, m_q_norm, m_k_norm, m_attn_sinks, m_rel_bias, m_pool_w, m_pool_scale, m_w_out, m_ffn2_norm, m_ffn2_w_gate, m_ffn2_w_up, m_ffn2_w_down, v_ffn1_norm, v_ffn1_w_gate, v_ffn1_w_up, v_ffn1_w_down, v_mix_norm, v_w_in, v_q_norm, v_k_norm, v_attn_sinks, v_rel_bias, v_pool_w, v_pool_scale, v_w_out, v_ffn2_norm, v_ffn2_w_gate, v_ffn2_w_up, v_ffn2_w_down):
    args = dict(locals())
    w = {n: args[n] for n in WEIGHTS}
    m = {n: args["m_" + n] for n in WEIGHTS}
    v = {n: args["v_" + n] for n in WEIGHTS}

    as_rows = lambda a, tr: jnp.swapaxes(a, 1, 2) if tr else a
    shard = jnp.concatenate([as_rows(w[n], tr)[0].astype(bf16) for n, tr in BIG], axis=0)
    exchanges = _GatheredWeights(shard, x[0], ffn1_norm)
    (dh1, dx1), (dw1, _, _, _), small = _local_step(
        x[0], loss_target[0], exchanges, ffn1_norm, mix_norm, ffn2_norm, q_norm, k_norm, attn_sinks,
        rel_bias, pool_w[0], pool_scale)

    nrows1 = len(G1_PIECES) * FS
    for_x, for_y, own1, small_tot, gx = _reduce_scatter_ffn1_head(dw1, _pack_small(small),
                                                                  (dh1, x[0], ffn1_norm, dx1))
    ssem, rsem, for_x, for_y, from_x, from_y, small_tot, gx = _rs1_tail_start(
        for_x, for_y, lax.empty((nrows1, D), bf16), lax.empty((nrows1, D), bf16), small_tot, gx)
    gx = _fresh_copy(gx, "grad_x_out")
    own_rest, landed_rest = exchanges.mix_ffn2_grads_parts(gx)

    grads, deltas, new_m, new_v = {}, {}, {}, {}
    rest = [k for k in range(len(BIG)) if k not in G1_PIECES]
    rows_of = lambda t, ks: [as_rows(t[BIG[k][0]], BIG[k][1]) for k in ks]
    rest_out = _adamw_big(rest, rows_of(w, rest), rows_of(m, rest), rows_of(v, rest), own_rest, [landed_rest],
                          "adamw_rest")
    ffn1 = list(G1_PIECES)
    ffn1_out = _adamw_big(ffn1, rows_of(w, ffn1), rows_of(m, ffn1), rows_of(v, ffn1), own1, [from_x, from_y],
                          "adamw_ffn1", in_flight=(ssem, rsem, [for_x, for_y], rest_out[0][0]))
    for ks, out in ((rest, rest_out), (ffn1, ffn1_out)):
        for i, k in enumerate(ks):
            n, tr = BIG[k]
            grads[n], deltas[n], new_m[n], new_v[n] = [as_rows(o[i], tr) for o in out]
    small_names = [n for n in SMALL_NAMES if n != "loss"]
    for n in small_names:
        grads[n] = _unpack_small(small_tot, n)
    ds, nms, nvs = _adamw_small([w[n] for n in small_names], [m[n] for n in small_names], [v[n] for n in small_names],
                                [grads[n] for n in small_names], "adamw_small")
    for i, n in enumerate(small_names):
        deltas[n], new_m[n], new_v[n] = ds[i], nms[i], nvs[i]
    loss = small_tot[LOSS_ROW, 0]
    return (loss, gx, *[grads[n] for n in WEIGHTS], *[deltas[n] for n in WEIGHTS],
            *[new_m[n] for n in WEIGHTS], *[new_v[n] for n in WEIGHTS])
```

```python
import jax
import jax.numpy as jnp
import numpy as np
from jax import lax
from jax.experimental import pallas as pl
from jax.experimental.pallas import tpu as pltpu

f32, bf16, i32 = jnp.float32, jnp.bfloat16, jnp.int32
SDS = jax.ShapeDtypeStruct

D = 1024
F = 2816
HD = 64
NH = 8
NKV = 2
GQA = NH // NKV
DATTN = NH * HD
DKV = NKV * HD
DPOOL = 512
POOL_WINDOWS = (2, 4, 8, 16)
PGD = DPOOL // len(POOL_WINDOWS)
DIN = DATTN + 2 * DKV + DPOOL
DMIX = DATTN + DPOOL
BLK = 128
NBUCK = 32
MAX_DISTANCE = 128
EPS = 1e-6
NEG = -1e30
SCALE = HD ** -0.5

ADAM_LR, ADAM_B1, ADAM_B2, ADAM_EPS, ADAM_WD, ADAM_STEP = 0.001, 0.9, 0.999, 1e-08, 0.01, 10

NDEV = 8
FS = F // NDEV
INS = DIN // NDEV
OUTS = DMIX // NDEV
PIECE_ROWS = (FS, FS, FS, INS, OUTS, FS, FS, FS)
PIECE_OFF = tuple(int(v) for v in np.cumsum((0,) + PIECE_ROWS[:-1]))
PACK_ROWS = sum(PIECE_ROWS)

VMEM_LIMIT_V7X = 56 * 1024 * 1024

MESH = pl.DeviceIdType.MESH


def _cparams(sem=None, vmem=None):
    return pltpu.CompilerParams(dimension_semantics=sem, vmem_limit_bytes=vmem)


def _nt(a, b):
    return lax.dot_general(a, b, (((1,), (1,)), ((), ())), preferred_element_type=f32)


def _tn(a, b):
    return lax.dot_general(a, b, (((0,), (0,)), ((), ())), preferred_element_type=f32)


def _nn(a, b):
    return jnp.dot(a, b, preferred_element_type=f32)


def _sigmoid(x):
    return 1.0 / (1.0 + jnp.exp(-x))


def _fresh_copy(a, name):
    T = a.shape[1]
    tm = min(512, T)

    def body(a_ref, o_ref):
        o_ref[...] = a_ref[...]

    tok = pl.BlockSpec((1, tm, D), lambda i: (0, i, 0))
    return pl.pallas_call(body, grid=(T // tm,), in_specs=[tok], out_specs=tok, out_shape=SDS(a.shape, a.dtype),
                          name=name)(a)


def _norm_fwd(x, g, name):
    T = x.shape[0]
    tm = min(512, T)

    def body(x_ref, g_ref, h_ref):
        xv = x_ref[...]
        r = lax.rsqrt(jnp.mean(xv * xv, axis=-1, keepdims=True) + EPS)
        h_ref[...] = (xv * r * g_ref[...]).astype(bf16)

    return pl.pallas_call(
        body, grid=(T // tm,),
        in_specs=[pl.BlockSpec((tm, D), lambda i: (i, 0)), pl.BlockSpec((1, D), lambda i: (0, 0))],
        out_specs=pl.BlockSpec((tm, D), lambda i: (i, 0)),
        out_shape=SDS((T, D), bf16), name=name)(x, g)


FFN_ROW_CHUNK = 256


def _ffn_tiles(T):
    return min(1024, T), 256


def _ffn_fwd(h, w, x, target, next_gain, name):
    T = h.shape[0]
    tm, tf = _ffn_tiles(T)
    nf = F // tf
    with_loss = target is not None
    assert with_loss != (next_gain is not None)

    def body(*refs):
        if with_loss:
            h_ref, w_ref, x_hbm, t_hbm, xo_ref, g_ref, u_ref, dyb_ref, loss_ref, tbuf, sem = refs
        else:
            h_ref, w_ref, x_hbm, gain_ref, xo_ref, g_ref, u_ref, hn_ref, sem = refs
        fi = pl.program_id(0)

        def x_rows(r):
            return pltpu.make_async_copy(x_hbm.at[pl.ds(r, tm), :], xo_ref.at[pl.ds(r, tm), :], sem.at[r // tm])

        def tile(first):
            if first:
                for r in range(0, T, tm):
                    x_rows(r).start()
            wgu = w_ref[0:2].reshape(2 * tf, D)
            for r in range(0, T, tm):
                rows = slice(r, r + tm)
                gu = _nt(h_ref[rows, :], wgu)
                gate, up = gu[:, :tf], gu[:, tf:]
                act = gate * _sigmoid(gate) * up
                g_ref[0, rows, :] = gate.astype(bf16)
                u_ref[0, rows, :] = up.astype(bf16)
                down = _nn((0.5 * act).astype(bf16), w_ref[2])
                if first:
                    x_rows(r).wait()
                xo_ref[rows, :] += down

        @pl.when(fi == 0)
        def _():
            tile(True)

        @pl.when(fi > 0)
        def _():
            tile(False)

        if with_loss:
            @pl.when(fi == nf - 1)
            def _():
                lanes = jnp.zeros((1, 128), f32)
                for r in range(0, T, tm):
                    rows = slice(r, r + tm)
                    cp = pltpu.make_async_copy(t_hbm.at[pl.ds(r, tm), :], tbuf, sem.at[0])
                    cp.start()
                    cp.wait()
                    e = xo_ref[rows, :] - tbuf[...]
                    dy = e * (1.0 / D)
                    xo_ref[rows, :] = dy
                    dyb_ref[rows, :] = (0.5 * dy).astype(bf16)
                    col = jnp.sum(e * e, axis=0, keepdims=True) * (0.5 / D)
                    for k in range(D // 128):
                        lanes = lanes + col[:, 128 * k:128 * (k + 1)]
                loss_ref[...] = lanes
        else:
            @pl.when(fi == nf - 1)
            def _():
                for r in range(0, T, FFN_ROW_CHUNK):
                    rows = slice(r, r + FFN_ROW_CHUNK)
                    xv = xo_ref[rows, :]
                    rstd = lax.rsqrt(jnp.mean(xv * xv, axis=-1, keepdims=True) + EPS)
                    hn_ref[rows, :] = (xv * rstd * gain_ref[...]).astype(bf16)

    tok = pl.BlockSpec((T, D), lambda f: (0, 0))
    act_spec = pl.BlockSpec((1, T, tf), lambda f: (f, 0, 0))
    hbm = pl.BlockSpec(memory_space=pl.ANY)
    in_specs = [tok, pl.BlockSpec((3, tf, D), lambda f: (0, f, 0)), hbm]
    out_specs = [tok, act_spec, act_spec]
    out_shape = [SDS((T, D), f32), SDS((nf, T, tf), bf16), SDS((nf, T, tf), bf16)]
    scratch = [pltpu.SemaphoreType.DMA((T // tm,))]
    args = [h, w, x]
    if with_loss:
        in_specs.append(hbm)
        args.append(target)
        out_specs += [tok, pl.BlockSpec((1, 128), lambda f: (0, 0))]
        out_shape += [SDS((T, D), bf16), SDS((1, 128), f32)]
        scratch = [pltpu.VMEM((tm, D), f32)] + scratch
    else:
        in_specs.append(pl.BlockSpec((1, D), lambda f: (0, 0)))
        args.append(next_gain)
        out_specs.append(tok)
        out_shape.append(SDS((T, D), bf16))
    return pl.pallas_call(
        body, grid=(nf,), in_specs=in_specs, out_specs=out_specs, out_shape=tuple(out_shape), scratch_shapes=scratch,
        compiler_params=_cparams(("arbitrary",), VMEM_LIMIT_V7X), name=name)(*args)


NORM_BWD_ROWS = 512


def _norm_bwd_scratch(dh_in_hbm, with_bf16):
    buf = lambda dt: pltpu.VMEM((2, NORM_BWD_ROWS, D), dt)
    return [buf(f32), buf(f32), buf(f32) if dh_in_hbm else None, buf(f32), buf(bf16) if with_bf16 else None,
            pltpu.SemaphoreType.DMA((6,)), pltpu.SemaphoreType.DMA((4,))]


def _norm_bwd_rows(T, dh_src, x_hbm, gain_ref, dr_hbm, dx_hbm, dxb_hbm, xbuf, rbuf, hbuf, obuf, obb, in_sems, out_sems):
    tm = min(NORM_BWD_ROWS, T)
    nchunk = T // tm

    def loads(i):
        s, rows = i % 2, pl.ds(i * tm, tm)
        cps = [pltpu.make_async_copy(x_hbm.at[rows, :], xbuf.at[s, pl.ds(0, tm), :], in_sems.at[3 * s]),
               pltpu.make_async_copy(dr_hbm.at[rows, :], rbuf.at[s, pl.ds(0, tm), :], in_sems.at[3 * s + 1])]
        if hbuf is not None:
            cps.append(pltpu.make_async_copy(dh_src.at[rows, :], hbuf.at[s, pl.ds(0, tm), :], in_sems.at[3 * s + 2]))
        return cps

    def stores(i):
        s, rows = i % 2, pl.ds(i * tm, tm)
        cps = [pltpu.make_async_copy(obuf.at[s, pl.ds(0, tm), :], dx_hbm.at[rows, :], out_sems.at[2 * s])]
        if dxb_hbm is not None:
            cps.append(pltpu.make_async_copy(obb.at[s, pl.ds(0, tm), :], dxb_hbm.at[rows, :], out_sems.at[2 * s + 1]))
        return cps

    for cp in loads(0):
        cp.start()
    dg = jnp.zeros((1, D), f32)
    for i in range(nchunk):
        s = i % 2
        if i + 1 < nchunk:
            for cp in loads(i + 1):
                cp.start()
        for cp in loads(i):
            cp.wait()
        if i >= 2:
            for cp in stores(i - 2):
                cp.wait()
        xv = xbuf[s, 0:tm, :]
        rstd = lax.rsqrt(jnp.mean(xv * xv, axis=-1, keepdims=True) + EPS)
        xh = xv * rstd
        dhv = hbuf[s, 0:tm, :] if hbuf is not None else dh_src[i * tm:(i + 1) * tm, :]
        dxh = dhv * gain_ref[...]
        dx = rbuf[s, 0:tm, :] + rstd * (dxh - xh * jnp.mean(dxh * xh, axis=-1, keepdims=True))
        obuf[s, 0:tm, :] = dx
        if dxb_hbm is not None:
            obb[s, 0:tm, :] = dx.astype(bf16)
        dg = dg + jnp.sum(dhv * xh, axis=0, keepdims=True)
        for cp in stores(i):
            cp.start(priority=1)
    for i in range(max(nchunk - 2, 0), nchunk):
        for cp in stores(i):
            cp.wait()
    return dg


def _ffn_bwd(dob, h, gate, up, w, norm, name):
    T = h.shape[0]
    _, tf = _ffn_tiles(T)
    nf = F // tf
    nin = 5 if norm is None else 8
    nout = 2 if norm is None else 4

    def body(*refs):
        do_hbm, h_hbm, g_ref, u_ref, w_ref = refs[:5]
        dw_ref = refs[nin + nout - 1]
        do_v, h_v, dh_acc, dgu_s, act_s, sems = refs[nin + nout:nin + nout + 6]
        fi = pl.program_id(0)

        @pl.when(fi == 0)
        def _():
            loads = [pltpu.make_async_copy(do_hbm, do_v, sems.at[0]), pltpu.make_async_copy(h_hbm, h_v, sems.at[1])]
            for cp in loads:
                cp.start()
            dh_acc[...] = jnp.zeros_like(dh_acc)
            for cp in loads:
                cp.wait()

        wgu = w_ref[0:2].reshape(2 * tf, D)
        for r in range(0, T, FFN_ROW_CHUNK):
            rows = slice(r, r + FFN_ROW_CHUNK)
            dov = do_v[rows, :]
            gv = g_ref[0, rows, :].astype(f32)
            uv = u_ref[0, rows, :].astype(f32)
            sg = _sigmoid(gv)
            sil = gv * sg
            dact = _nt(dov, w_ref[2])
            dup = dact * sil
            dgate = dact * uv * (sg * (1.0 + gv * (1.0 - sg)))
            dgu = jnp.concatenate([dgate.astype(bf16), dup.astype(bf16)], axis=1)
            dgu_s[rows, :] = dgu
            act_s[rows, :] = (sil * uv).astype(bf16)
            dh_acc[rows, :] += _nn(dgu, wgu)
        dw_ref[0:2] = _tn(dgu_s[...], h_v[...]).reshape(2, tf, D).astype(bf16)
        dw_ref[2] = _tn(act_s[...], do_v[...]).astype(bf16)

        @pl.when(fi == nf - 1)
        def _():
            if norm is None:
                out = pltpu.make_async_copy(dh_acc, refs[nin], sems.at[0])
                out.start()
                out.wait()
            else:
                x_hbm, gain_ref, dr_hbm, dx_hbm, dxb_hbm, dg_ref = refs[5:11]
                xbuf, rbuf, obuf, obb, in_sems, out_sems = refs[nin + nout + 6:]
                dg_ref[...] = _norm_bwd_rows(T, dh_acc, x_hbm, gain_ref, dr_hbm, dx_hbm, dxb_hbm,
                                             xbuf, rbuf, None, obuf, obb, in_sems, out_sems)

    act_spec = pl.BlockSpec((1, T, tf), lambda f: (f, 0, 0))
    wspec = pl.BlockSpec((3, tf, D), lambda f: (0, f, 0))
    vec = pl.BlockSpec((1, D), lambda f: (0, 0))
    hbm = pl.BlockSpec(memory_space=pl.ANY)
    in_specs, out_specs = [hbm, hbm, act_spec, act_spec, wspec], [hbm, wspec]
    out_shape, args = [SDS((T, D), f32), SDS((3, F, D), bf16)], [dob, h, gate, up, w]
    scratch = [pltpu.VMEM((T, D), bf16), pltpu.VMEM((T, D), bf16), pltpu.VMEM((T, D), f32),
               pltpu.VMEM((T, 2 * tf), bf16), pltpu.VMEM((T, tf), bf16), pltpu.SemaphoreType.DMA((2,))]
    if norm is not None:
        in_specs += [hbm, vec, hbm]
        out_specs = [hbm, hbm, vec, wspec]
        out_shape = [SDS((T, D), f32), SDS((T, D), bf16), SDS((1, D), f32), SDS((3, F, D), bf16)]
        args += list(norm)
        scratch += [sc for sc in _norm_bwd_scratch(False, True) if sc is not None]
    return pl.pallas_call(
        body, grid=(nf,), in_specs=in_specs, out_specs=out_specs, out_shape=tuple(out_shape), scratch_shapes=scratch,
        compiler_params=_cparams(("arbitrary",), VMEM_LIMIT_V7X), name=name)(*args)


def _in_proj_fwd(h, wint, name):
    T = h.shape[0]
    tm = min(512, T)

    def body(h_ref, w_ref, z_ref):
        z_ref[...] = _nt(h_ref[...], w_ref[...])

    return pl.pallas_call(
        body, grid=(T // tm,),
        in_specs=[pl.BlockSpec((tm, D), lambda i: (i, 0)), pl.BlockSpec((DIN, D), lambda i: (0, 0))],
        out_specs=pl.BlockSpec((tm, DIN), lambda i: (i, 0)),
        out_shape=SDS((T, DIN), f32), name=name)(h, wint)


def _in_proj_bwd(dz, wint, h, norm, out_scale, name):
    x, g, dres = norm
    T = h.shape[0]
    tm = min(512, T)
    nt = T // tm

    def body(dz_ref, w_ref, h_ref, x_ref, g_ref, dr_ref, dx_ref, dxb_ref, dg_ref, dw_ref, acc):
        i = pl.program_id(0)
        dzb = dz_ref[...].astype(bf16)
        dhv = _nn(dzb, w_ref[...])
        part = _tn(dzb, h_ref[...])
        xv = x_ref[...]
        rstd = lax.rsqrt(jnp.mean(xv * xv, axis=-1, keepdims=True) + EPS)
        xh = xv * rstd
        dxh = dhv * g_ref[...]
        dx = dr_ref[...] + rstd * (dxh - xh * jnp.mean(dxh * xh, axis=-1, keepdims=True))
        dx_ref[...] = dx
        dxb_ref[...] = (out_scale * dx).astype(bf16)
        dg = jnp.sum(dhv * xh, axis=0, keepdims=True)

        @pl.when(i == 0)
        def _():
            acc[...] = part
            dg_ref[...] = dg

        @pl.when(i > 0)
        def _():
            acc[...] += part
            dg_ref[...] += dg

        @pl.when(i == nt - 1)
        def _():
            dw_ref[...] = acc[...].astype(bf16)

    wspec = pl.BlockSpec((DIN, D), lambda i: (0, 0))
    tok = pl.BlockSpec((tm, D), lambda i: (i, 0))
    vec = pl.BlockSpec((1, D), lambda i: (0, 0))
    return pl.pallas_call(
        body, grid=(nt,),
        in_specs=[pl.BlockSpec((tm, DIN), lambda i: (i, 0)), wspec, tok, tok, vec, tok],
        out_specs=[tok, tok, vec, wspec],
        out_shape=(SDS((T, D), f32), SDS((T, D), bf16), SDS((1, D), f32), SDS((DIN, D), bf16)),
        scratch_shapes=[pltpu.VMEM((DIN, D), f32)],
        compiler_params=_cparams(("arbitrary",)), name=name)(dz, wint, h, x, g, dres)


def _out_proj_fwd(ymix, wout, x, g, name):
    T = x.shape[0]
    tm = min(512, T)

    def body(y_ref, w_ref, x_ref, g_ref, o_ref, h_ref):
        o = x_ref[...] + _nn(y_ref[...], w_ref[...])
        o_ref[...] = o
        r = lax.rsqrt(jnp.mean(o * o, axis=-1, keepdims=True) + EPS)
        h_ref[...] = (o * r * g_ref[...]).astype(bf16)

    tok = pl.BlockSpec((tm, D), lambda i: (i, 0))
    return pl.pallas_call(
        body, grid=(T // tm,),
        in_specs=[pl.BlockSpec((tm, DMIX), lambda i: (i, 0)), pl.BlockSpec((DMIX, D), lambda i: (0, 0)), tok,
                  pl.BlockSpec((1, D), lambda i: (0, 0))],
        out_specs=[tok, tok], out_shape=(SDS((T, D), f32), SDS((T, D), bf16)), name=name)(ymix, wout, x, g)


def _out_proj_bwd(dxb, wout, ymix, name):
    T = dxb.shape[0]
    tm = min(512, T)
    nt = T // tm

    def body(dx_ref, w_ref, y_ref, dy_ref, dw_ref, acc):
        i = pl.program_id(0)
        dxv = dx_ref[...]
        dy_ref[...] = _nt(dxv, w_ref[...])
        part = _tn(y_ref[...], dxv)

        @pl.when(i == 0)
        def _():
            acc[...] = part

        @pl.when(i > 0)
        def _():
            acc[...] += part

        @pl.when(i == nt - 1)
        def _():
            dw_ref[...] = acc[...].astype(bf16)

    wspec = pl.BlockSpec((DMIX, D), lambda i: (0, 0))
    return pl.pallas_call(
        body, grid=(nt,),
        in_specs=[pl.BlockSpec((tm, D), lambda i: (i, 0)), wspec, pl.BlockSpec((tm, DMIX), lambda i: (i, 0))],
        out_specs=[pl.BlockSpec((tm, DMIX), lambda i: (i, 0)), wspec],
        out_shape=(SDS((T, DMIX), f32), SDS((DMIX, D), bf16)),
        scratch_shapes=[pltpu.VMEM((DMIX, D), f32)],
        compiler_params=_cparams(("arbitrary",)), name=name)(dxb, wout, ymix)


def _t5_bucket_table():
    ql = np.arange(BLK)[:, None]
    kl = np.arange(2 * BLK)[None, :]
    n = np.maximum(ql + BLK - kl, 0)
    max_exact = NBUCK // 2
    large = max_exact + (np.log(np.maximum(n, 1) / max_exact) / np.log(MAX_DISTANCE / max_exact)
                         * (NBUCK - max_exact)).astype(np.int32)
    large = np.minimum(large, NBUCK - 1)
    return np.where(n < max_exact, n, large).astype(np.int32)


def _fill_bias(bk_ref, rb_ref, bias_scr):
    bk = bk_ref[...]
    for h in range(NH):
        def step(b, acc, h=h):
            return acc + jnp.where(bk == b, rb_ref[b, h], 0.0)
        bias_scr[h] = lax.fori_loop(0, NBUCK, step, jnp.zeros((BLK, 2 * BLK), f32))


MIX_SUB = 4


class _Window:
    def __init__(self, zc_ref, zp_ref, n, s):
        self.blk = n * MIX_SUB + s
        self.cur = lambda a, b: zc_ref[s * BLK:(s + 1) * BLK, a:b]
        self.prev = (lambda a, b: zp_ref[:, a:b]) if s == 0 else (lambda a, b: zc_ref[(s - 1) * BLK:s * BLK, a:b])


def _attn_qkv(win, kh, qg, kg):
    kc = DATTN + HD * kh
    vc = DATTN + DKV + HD * kh
    kx = jnp.concatenate([win.prev(kc, kc + HD), win.cur(kc, kc + HD)], axis=0)
    vx = jnp.concatenate([win.prev(vc, vc + HD), win.cur(vc, vc + HD)], axis=0)
    qx = jnp.concatenate([win.cur(HD * (GQA * kh + g), HD * (GQA * kh + g + 1)) for g in range(GQA)], axis=0)
    rq = lax.rsqrt(jnp.mean(qx * qx, axis=-1, keepdims=True) + EPS)
    rk = lax.rsqrt(jnp.mean(kx * kx, axis=-1, keepdims=True) + EPS)
    qhat, khat = qx * rq, kx * rk
    return dict(qhat=qhat, khat=khat, rq=rq, rk=rk, qsb=(qhat * (qg * SCALE)).astype(bf16),
                knb=(khat * kg).astype(bf16), vb=vx.astype(bf16))


def _window_masks(n):
    row = lax.broadcasted_iota(i32, (GQA * BLK, 2 * BLK), 0) & (BLK - 1)
    col = lax.broadcasted_iota(i32, (GQA * BLK, 2 * BLK), 1)
    band = (col > row) & (col <= row + BLK)
    return band & ((col >= BLK) | (n > 0)), band


def _attn_probs(a, kh, sk_ref, bias_scr, mask):
    s = _nt(a["qsb"], a["knb"]) + bias_scr[GQA * kh:GQA * (kh + 1)].reshape(GQA * BLK, 2 * BLK)
    s = jnp.where(mask, s, NEG)
    ridx = lax.broadcasted_iota(i32, (GQA * BLK, 1), 0)
    sink = jnp.full((GQA * BLK, 1), sk_ref[GQA * kh + GQA - 1], f32)
    for g in range(GQA - 2, -1, -1):
        sink = jnp.where(ridx < (g + 1) * BLK, sk_ref[GQA * kh + g], sink)
    m = jnp.maximum(jnp.max(s, axis=-1, keepdims=True), sink)
    e = jnp.exp(s - m)
    den = jnp.sum(e, axis=-1, keepdims=True) + jnp.exp(sink - m)
    return e / den


POOL_STEPS = {2: (1,), 4: (1, 2), 8: (1, 2, 4), 16: (1, 2, 4, 8)}


def _pool_group(win, g, w):
    n = win.blk
    c0 = DATTN + 2 * DKV + PGD * g
    uc = win.cur(c0, c0 + PGD)
    up = jnp.where(n > 0, win.prev(c0, c0 + PGD), 0.0)
    sm = jnp.concatenate([up, uc], axis=0)
    for k in POOL_STEPS[w]:
        sm = sm + pltpu.roll(sm, k, axis=0)
    pos = n * BLK + lax.broadcasted_iota(i32, (BLK, 1), 0) + 1
    cnt = jnp.minimum(pos, w).astype(f32)
    return sm[BLK:2 * BLK] / cnt - uc, cnt


def _mix_fwd(z, qg, kg, sinks, relb, bucket, pool_w, pscale, name):
    T = z.shape[0]
    step_rows = MIX_SUB * BLK
    nsteps = T // step_rows

    def body(zc_ref, zp_ref, qg_ref, kg_ref, sk_ref, rb_ref, bk_ref, pw_ref, ps_ref, y_ref, p_ref, bias_scr, yacc):
        n = pl.program_id(0)

        @pl.when(n == 0)
        def _():
            _fill_bias(bk_ref, rb_ref, bias_scr)

        first_mask, mask = _window_masks(n)
        for s in range(MIX_SUB):
            win = _Window(zc_ref, zp_ref, n, s)
            rows = slice(s * BLK, (s + 1) * BLK)
            for kh in range(NKV):
                a = _attn_qkv(win, kh, qg_ref[...], kg_ref[...])
                pb = _attn_probs(a, kh, sk_ref, bias_scr, first_mask if s == 0 else mask).astype(bf16)
                p_ref[s, GQA * kh:GQA * (kh + 1)] = pb.reshape(GQA, BLK, 2 * BLK)
                o = _nn(pb, a["vb"])
                for g in range(GQA):
                    hc = HD * (GQA * kh + g)
                    yacc[rows, hc:hc + HD] = o[g * BLK:(g + 1) * BLK]
            for g, w in enumerate(POOL_WINDOWS):
                pooled, _ = _pool_group(win, g, w)
                yp = _nn(pooled.astype(bf16), pw_ref[g].astype(bf16)) * ps_ref[:, PGD * g:PGD * (g + 1)]
                yacc[rows, DATTN + PGD * g:DATTN + PGD * (g + 1)] = yp
        y_ref[...] = yacc[...].astype(bf16)

    full = lambda *shape: pl.BlockSpec(shape, lambda n: (0,) * len(shape))
    smem = pl.BlockSpec(memory_space=pltpu.SMEM)
    return pl.pallas_call(
        body, grid=(nsteps,),
        in_specs=[pl.BlockSpec((step_rows, DIN), lambda n: (n, 0)),
                  pl.BlockSpec((BLK, DIN), lambda n: (jnp.maximum(n * MIX_SUB - 1, 0), 0)),
                  full(1, HD), full(1, HD), smem, smem, full(BLK, 2 * BLK),
                  full(len(POOL_WINDOWS), PGD, PGD), full(1, DPOOL)],
        out_specs=[pl.BlockSpec((step_rows, DMIX), lambda n: (n, 0)),
                   pl.BlockSpec((MIX_SUB, NH, BLK, 2 * BLK), lambda n: (n, 0, 0, 0))],
        out_shape=(SDS((T, DMIX), bf16), SDS((T // BLK, NH, BLK, 2 * BLK), bf16)),
        scratch_shapes=[pltpu.VMEM((NH, BLK, 2 * BLK), f32), pltpu.VMEM((step_rows, DMIX), f32)],
        compiler_params=_cparams(("arbitrary",)), name=name)(z, z, qg, kg, sinks, relb, bucket, pool_w, pscale)


def _mix_bwd(z, dy, probs, qg, kg, relb, bucket, pool_w, pscale, name):
    T = z.shape[0]
    step_rows = MIX_SUB * BLK
    nsteps = T // step_rows

    def body(zc_ref, zp_ref, dy_ref, p_ref, qg_ref, kg_ref, bk_ref, pw_ref, ps_ref,
             dz_ref, dqg_ref, dkg_ref, dsk_ref, drb_ref, dpw_ref, dps_ref, dbias_scr):
        n = pl.program_id(0)

        @pl.when(n == 0)
        def _():
            dbias_scr[...] = jnp.zeros_like(dbias_scr)
            dqg_ref[...] = jnp.zeros_like(dqg_ref)
            dkg_ref[...] = jnp.zeros_like(dkg_ref)
            dpw_ref[...] = jnp.zeros_like(dpw_ref)
            dps_ref[...] = jnp.zeros_like(dps_ref)

        qg, kg = qg_ref[...], kg_ref[...]
        for s in range(MIX_SUB):
            win = _Window(zc_ref, zp_ref, n, s)
            blk = win.blk
            rows = pl.ds(pl.multiple_of(blk * BLK, BLK), BLK)
            prow = pl.ds(pl.multiple_of(jnp.maximum(blk - 1, 0) * BLK, BLK), BLK)
            dyr = slice(s * BLK, (s + 1) * BLK)

            def into_prev(fn, s=s):
                if s == 0:
                    pl.when(n > 0)(fn)
                else:
                    fn()

            for kh in range(NKV):
                a = _attn_qkv(win, kh, qg, kg)
                pb = p_ref[s, GQA * kh:GQA * (kh + 1)].reshape(GQA * BLK, 2 * BLK)
                p = pb.astype(f32)
                do = jnp.concatenate([dy_ref[dyr, HD * (GQA * kh + g):HD * (GQA * kh + g + 1)] for g in range(GQA)],
                                     axis=0).astype(bf16)
                dv = _tn(pb, do)
                dp = _nt(do, a["vb"])
                delta = jnp.sum(p * dp, axis=-1, keepdims=True)
                ds = p * (dp - delta)
                for g in range(GQA):
                    dbias_scr[GQA * kh + g] += ds[g * BLK:(g + 1) * BLK]
                dsb = ds.astype(bf16)
                dqn = _nn(dsb, a["knb"]) * SCALE
                dkn = _tn(dsb, a["qsb"])
                qhat, khat = a["qhat"], a["khat"]
                dqg_ref[...] += jnp.sum(dqn * qhat, axis=0, keepdims=True)
                dkg_ref[...] += jnp.sum(dkn * khat, axis=0, keepdims=True)
                dqh = dqn * qg
                dq = a["rq"] * (dqh - qhat * jnp.mean(dqh * qhat, axis=-1, keepdims=True))
                dkh = dkn * kg
                dk = a["rk"] * (dkh - khat * jnp.mean(dkh * khat, axis=-1, keepdims=True))
                kc = DATTN + HD * kh
                vc = DATTN + DKV + HD * kh
                for g in range(GQA):
                    hc = HD * (GQA * kh + g)
                    dz_ref[rows, hc:hc + HD] = dq[g * BLK:(g + 1) * BLK]
                dz_ref[rows, kc:kc + HD] = dk[BLK:2 * BLK]
                dz_ref[rows, vc:vc + HD] = dv[BLK:2 * BLK]

                def kv_prev(dk=dk, dv=dv, kc=kc, vc=vc, prow=prow):
                    dz_ref[prow, kc:kc + HD] += dk[0:BLK]
                    dz_ref[prow, vc:vc + HD] += dv[0:BLK]

                into_prev(kv_prev)

            for g, w in enumerate(POOL_WINDOWS):
                c0 = DATTN + 2 * DKV + PGD * g
                pooled, cnt = _pool_group(win, g, w)
                pb = pooled.astype(bf16)
                wb = pw_ref[g].astype(bf16)
                dyp = dy_ref[dyr, DATTN + PGD * g:DATTN + PGD * (g + 1)]
                ypre = _nn(pb, wb)
                dps_ref[:, PGD * g:PGD * (g + 1)] += jnp.sum(dyp * ypre, axis=0, keepdims=True)
                dyg = (dyp * ps_ref[:, PGD * g:PGD * (g + 1)]).astype(bf16)
                dpw_ref[g] += _tn(pb, dyg)
                dpooled = _nt(dyg, wb)
                due = jnp.concatenate([jnp.zeros((BLK, PGD), f32), dpooled / cnt], axis=0)
                for k in POOL_STEPS[w]:
                    due = due + pltpu.roll(due, 2 * BLK - k, axis=0)
                dz_ref[rows, c0:c0 + PGD] = due[BLK:2 * BLK] - dpooled

                def pool_prev(due=due, c0=c0, prow=prow):
                    dz_ref[prow, c0:c0 + PGD] += due[0:BLK]

                into_prev(pool_prev)

        @pl.when(n == nsteps - 1)
        def _():
            bk = bk_ref[...]
            ri = lax.broadcasted_iota(i32, (NBUCK, NH), 0)
            ci = lax.broadcasted_iota(i32, (NBUCK, NH), 1)

            def step(b, acc):
                for h in range(NH):
                    sel = jnp.where(bk == b, dbias_scr[h], 0.0)
                    tot = jnp.sum(jnp.sum(sel, axis=1, keepdims=True), axis=0, keepdims=True)
                    acc = acc + jnp.where((ri == b) & (ci == h), tot, 0.0)
                return acc

            drb_ref[...] = lax.fori_loop(0, NBUCK, step, jnp.zeros((NBUCK, NH), f32))
            lane = lax.broadcasted_iota(i32, (1, 128), 1)
            dsk = jnp.zeros((1, 128), f32)
            for h in range(NH):
                tot = jnp.sum(jnp.sum(dbias_scr[h], axis=1, keepdims=True), axis=0, keepdims=True)
                dsk = dsk - jnp.where(lane == h, tot, 0.0)
            dsk_ref[...] = dsk

    full = lambda *shape: pl.BlockSpec(shape, lambda n: (0,) * len(shape))
    npg = len(POOL_WINDOWS)
    return pl.pallas_call(
        body, grid=(nsteps,),
        in_specs=[pl.BlockSpec((step_rows, DIN), lambda n: (n, 0)),
                  pl.BlockSpec((BLK, DIN), lambda n: (jnp.maximum(n * MIX_SUB - 1, 0), 0)),
                  pl.BlockSpec((step_rows, DMIX), lambda n: (n, 0)),
                  pl.BlockSpec((MIX_SUB, NH, BLK, 2 * BLK), lambda n: (n, 0, 0, 0)),
                  full(1, HD), full(1, HD), full(BLK, 2 * BLK), full(npg, PGD, PGD), full(1, DPOOL)],
        out_specs=[full(T, DIN), full(1, HD), full(1, HD), full(1, 128), full(NBUCK, NH),
                   full(npg, PGD, PGD), full(1, DPOOL)],
        out_shape=(SDS((T, DIN), f32), SDS((1, HD), f32), SDS((1, HD), f32), SDS((1, 128), f32),
                   SDS((NBUCK, NH), f32), SDS((npg, PGD, PGD), f32), SDS((1, DPOOL), f32)),
        scratch_shapes=[pltpu.VMEM((NH, BLK, 2 * BLK), f32)],
        compiler_params=_cparams(("arbitrary",), VMEM_LIMIT_V7X),
        name=name)(z, z, dy, probs, qg, kg, bucket, pool_w, pscale)


class _LocalWeights:
    def __init__(self, w1, wint, wout, w2):
        self.w1, self.wint, self.wout, self.w2 = w1, wint, wout, w2

    def ffn1(self):
        return self.w1

    def first_norm(self, x, gain):
        return _norm_fwd(x, gain, "norm1_fwd")

    def after_ffn1(self, gain, x1):
        return gain

    def mix(self, after):
        return self.wint, self.wout

    def before_out_proj(self, wout, after):
        return wout

    def ffn2(self, after):
        return self.w2

    def out_ffn2_grads_ready(self, dwout, dw2, after):
        return after

    def before_ffn1_bwd(self, dwint, dx1b):
        return dx1b


def _local_step(x, target, weights, g1, gm, g3, qg, kg, sinks, relb, pool_w, pscale):
    bucket = jnp.asarray(_t5_bucket_table())
    sk = sinks.reshape(NH)
    w1 = weights.ffn1()
    h1 = weights.first_norm(x, g1)
    x1, gate1, up1, h2 = _ffn_fwd(h1, w1, x, None, gm, "ffn1_fwd")
    gm = weights.after_ffn1(gm, x1)
    wint, wout = weights.mix(h2)
    z = _in_proj_fwd(h2, wint, "in_proj_fwd")
    ymix, probs = _mix_fwd(z, qg, kg, sk, relb, bucket, pool_w, pscale, "mix_fwd")
    wout = weights.before_out_proj(wout, ymix)
    x2, h3 = _out_proj_fwd(ymix, wout, x1, g3, "out_proj_fwd")
    w2 = weights.ffn2(h3)
    dy, gate2, up2, dyb, loss_lanes = _ffn_fwd(h3, w2, x2, target, None, "ffn2_fwd")

    dx2, dx2b, dg3, dw2 = _ffn_bwd(dyb, h3, gate2, up2, w2, (x2, g3, dy), "ffn2_bwd")
    dymix, dwout = _out_proj_bwd(dx2b, wout, ymix, "out_proj_bwd")
    dymix = weights.out_ffn2_grads_ready(dwout, dw2, dymix)
    dz, dqg, dkg, dsk, drb, dpw, dps = _mix_bwd(z, dymix, probs, qg, kg, relb, bucket, pool_w, pscale, "mix_bwd")
    dx1, dx1b, dgm, dwint = _in_proj_bwd(dz, wint, h2, (x1, gm, dx2), 0.5, "in_proj_bwd")
    dx1b = weights.before_ffn1_bwd(dwint, dx1b)
    dh1, dw1 = _ffn_bwd(dx1b, h1, gate1, up1, w1, None, "ffn1_bwd")
    small = dict(mix_norm=dgm, ffn2_norm=dg3, pool_scale=dps, q_norm=dqg, k_norm=dkg,
                 attn_sinks=dsk[:, :NH], rel_bias=drb, pool_w=dpw, loss=loss_lanes)
    return (dh1, dx1), (dw1, dwint, dwout, dw2), small


SMALL_NAMES = ("ffn1_norm", "mix_norm", "ffn2_norm", "pool_scale", "q_norm", "k_norm", "attn_sinks", "rel_bias",
               "pool_w", "loss")
SMALL_SHAPES = dict(ffn1_norm=(1, D), mix_norm=(1, D), ffn2_norm=(1, D), pool_scale=(1, DPOOL), q_norm=(1, HD),
                    k_norm=(1, HD), attn_sinks=(1, NH), rel_bias=(NBUCK, NH),
                    pool_w=(1, len(POOL_WINDOWS), PGD, PGD), loss=(1, 128))


def _small_rows(name):
    return -(-int(np.prod(SMALL_SHAPES[name])) // 128)


SMALL_OFF = {}
_r = 0
for _n in SMALL_NAMES:
    SMALL_OFF[_n] = _r
    _r += _small_rows(_n)
SMALL_ROWS = -(-_r // 16) * 16
LOSS_ROW = SMALL_OFF["loss"]


def _pack_small(vals):
    parts = []
    for n in SMALL_NAMES:
        size = _small_rows(n) * 128
        if n in vals:
            flat = vals[n].astype(f32).reshape(-1)
            parts.append(jnp.pad(flat, (0, size - flat.shape[0])))
        else:
            parts.append(jnp.zeros((size,), f32))
    flat = jnp.concatenate(parts)
    flat = jnp.pad(flat, (0, SMALL_ROWS * 128 - flat.shape[0]))
    return flat.reshape(SMALL_ROWS, 128)


def _unpack_small(packed, name):
    size = int(np.prod(SMALL_SHAPES[name]))
    r0 = SMALL_OFF[name]
    return packed[r0:r0 + _small_rows(name)].reshape(-1)[:size].reshape(SMALL_SHAPES[name])


def _position():
    return lax.axis_index("x"), lax.axis_index("y"), lax.axis_index("c")


def _dev_index(x, y, c):
    return 4 * x + 2 * y + c


G1_PIECES, MIX_PIECES, F2_PIECES = (0, 1, 2), (3, 4), (5, 6, 7)


def _group_rows(pieces):
    return sum(PIECE_ROWS[k] for k in pieces)


def _shard_piece(s_ref, k):
    return s_ref.at[pl.ds(PIECE_OFF[k], PIECE_ROWS[k]), :]


def _shard_group(s_ref, pieces):
    return s_ref.at[pl.ds(PIECE_OFF[pieces[0]], _group_rows(pieces)), :]


def _weight_pieces(w1_ref=None, wi_ref=None, wo_ref=None, w2_ref=None):
    arrs = {}
    if w1_ref is not None:
        arrs.update({0: w1_ref.at[0], 1: w1_ref.at[1], 2: w1_ref.at[2]})
    if wi_ref is not None:
        arrs[3] = wi_ref
    if wo_ref is not None:
        arrs[4] = wo_ref
    if w2_ref is not None:
        arrs.update({5: w2_ref.at[0], 6: w2_ref.at[1], 7: w2_ref.at[2]})
    return arrs


def _block_rows(arrs, k, dev):
    r = PIECE_ROWS[k]
    return arrs[k].at[pl.ds(pl.multiple_of(_dev_index(*dev) * r, 16), r), :]


NORM_ROWS = 512


def _all_gather_ffn1(shard, x, gain):
    pieces = G1_PIECES
    rest_pieces = MIX_PIECES + F2_PIECES
    half = FS // 2
    T = x.shape[0]
    SIB, X0, X1, Y0, Y1, RELAY_Y, RELAY_X, ON_X, ON_Y, ON_D0, ON_D1 = range(11)

    def body(s_ref, x_ref, g_ref, w1_ref, h_ref, wi_ref, wo_ref, w2_ref, xbuf, hbuf, rest_buf,
             send_sems, recv_sems, local_sem, norm_sems):
        x, y, c = _position()
        me, sib = (x, y, c), (x, y, 1 - c)
        xn, yn, dg = (1 - x, y, c), (x, 1 - y, c), (1 - x, 1 - y, c)
        arrs = _weight_pieces(w1_ref=w1_ref)

        def place_rest():
            rest = _weight_pieces(wi_ref=wi_ref, wo_ref=wo_ref, w2_ref=w2_ref)
            grp = _shard_group(s_ref, rest_pieces)
            load = pltpu.make_async_copy(grp, rest_buf, norm_sems.at[0])
            load.start()
            load.wait()
            base = PIECE_OFF[rest_pieces[0]]
            for k in rest_pieces:
                pltpu.make_async_copy(rest_buf.at[pl.ds(PIECE_OFF[k] - base, PIECE_ROWS[k]), :],
                                      _block_rows(rest, k, me), norm_sems.at[1]).start()
            pltpu.make_async_copy(grp, rest_buf, norm_sems.at[1]).wait()

        def first_norm():
            for r in range(0, T, NORM_ROWS):
                load = pltpu.make_async_copy(x_ref.at[pl.ds(r, NORM_ROWS), :], xbuf, norm_sems.at[0])
                load.start()
                load.wait()
                xv = xbuf[...]
                rs = lax.rsqrt(jnp.mean(xv * xv, axis=-1, keepdims=True) + EPS)
                hbuf[...] = (xv * rs * g_ref[...]).astype(bf16)
                store = pltpu.make_async_copy(hbuf, h_ref.at[pl.ds(r, NORM_ROWS), :], norm_sems.at[1])
                store.start()
                store.wait()

        def rows_of(k, block, hf):
            r = PIECE_ROWS[k]
            start, size = (0, r) if hf is None else (hf * half, half)
            return arrs[k].at[pl.ds(pl.multiple_of(_dev_index(*block) * r + start, 16), size), :]

        def copies(rel, block, hf, to, from_shard=False):
            def src(k):
                if not from_shard:
                    return rows_of(k, block, hf)
                start, size = (0, PIECE_ROWS[k]) if hf is None else (hf * half, half)
                return s_ref.at[pl.ds(PIECE_OFF[k] + start, size), :]
            return [pltpu.make_async_remote_copy(
                src_ref=src(k), dst_ref=rows_of(k, block, hf), send_sem=send_sems.at[rel], recv_sem=recv_sems.at[rel],
                device_id=to, device_id_type=MESH) for k in pieces]

        def waiter(rel, hf):
            nrows = len(pieces) * (FS if hf is None else half)
            grp = s_ref.at[pl.ds(0, nrows), :]
            return pltpu.make_async_remote_copy(src_ref=grp, dst_ref=grp, send_sem=send_sems.at[rel],
                                                recv_sem=recv_sems.at[rel], device_id=me, device_id_type=MESH)

        def start(cps):
            for cp in cps:
                cp.start()

        mine = [pltpu.make_async_copy(_shard_piece(s_ref, k), _block_rows(arrs, k, me), local_sem) for k in pieces]
        start(mine)
        start(copies(SIB, me, None, sib, True))
        start(copies(X0, me, 0, xn, True))
        start(copies(Y1, me, 1, yn, True))
        start(copies(X1, me, 1, xn, True))
        start(copies(Y0, me, 0, yn, True))
        first_norm()
        place_rest()
        waiter(X0, 0).wait_recv()
        start(copies(RELAY_Y, xn, 0, yn))
        waiter(Y1, 1).wait_recv()
        start(copies(RELAY_X, yn, 1, xn))
        waiter(X1, 1).wait_recv()
        start(copies(ON_X, xn, None, sib))
        waiter(Y0, 0).wait_recv()
        start(copies(ON_Y, yn, None, sib))
        waiter(RELAY_Y, 0).wait_recv()
        start(copies(ON_D0, dg, 0, sib))
        waiter(RELAY_X, 1).wait_recv()
        start(copies(ON_D1, dg, 1, sib))
        waiter(SIB, None).wait_recv()
        waiter(ON_X, None).wait_recv()
        waiter(ON_Y, None).wait_recv()
        waiter(ON_D0, 0).wait_recv()
        waiter(ON_D1, 1).wait_recv()
        for rel, hf in ((SIB, None), (X0, 0), (X1, 1), (Y0, 0), (Y1, 1), (RELAY_Y, 0), (RELAY_X, 1),
                        (ON_X, None), (ON_Y, None), (ON_D0, 0), (ON_D1, 1)):
            waiter(rel, hf).wait_send()
        grp = _shard_group(s_ref, pieces)
        pltpu.make_async_copy(grp, grp, local_sem).wait()

    hbm = pl.BlockSpec(memory_space=pl.ANY)
    return pl.pallas_call(
        body, in_specs=[hbm, hbm, pl.BlockSpec(memory_space=pltpu.VMEM)], out_specs=[hbm] * 5,
        out_shape=(SDS((3, F, D), bf16), SDS((T, D), bf16),
                   SDS((DIN, D), bf16), SDS((DMIX, D), bf16), SDS((3, F, D), bf16)),
        scratch_shapes=[pltpu.VMEM((NORM_ROWS, D), f32), pltpu.VMEM((NORM_ROWS, D), bf16),
                        pltpu.VMEM((_group_rows(rest_pieces), D), bf16),
                        pltpu.SemaphoreType.DMA((11,)), pltpu.SemaphoreType.DMA((11,)), pltpu.SemaphoreType.DMA,
                        pltpu.SemaphoreType.DMA((2,))],
        compiler_params=pltpu.CompilerParams(has_side_effects=True),
        name="all_gather_ffn1")(shard, x, gain)


HBM_SPEC = pl.BlockSpec(memory_space=pltpu.HBM)
SEM_SPEC = pl.BlockSpec(memory_space=pltpu.SEMAPHORE)
ANY_SPEC = pl.BlockSpec(memory_space=pl.ANY)
SPLIT_EFFECT = pltpu.SideEffectType.DATAFLOW_SIDE_EFFECTING


def _in_hbm(a):
    return pltpu.with_memory_space_constraint(a, pltpu.HBM)


def _hbm_like(a):
    return pltpu.HBM(a.shape, a.dtype)


def _gather_rest_start(shard, wi, wo, w2, w1):
    def body(s_ref, wi_ref, wo_ref, w2_ref, w1_ref,
             ssem_m, rsem_m0, rsem_m, ssem_f, rsem_f0, rsem_f, s_o, wi_o, wo_o, w2_o, w1_o):
        x, y, c = _position()
        me, sib = (x, y, c), (x, y, 1 - c)
        chips = [(1 - x, y), (x, 1 - y), (1 - x, 1 - y)]
        arrs = _weight_pieces(wi_ref=wi_ref, wo_ref=wo_ref, w2_ref=w2_ref)
        for pieces, ssem, rsem0, rsem in ((MIX_PIECES, ssem_m, rsem_m0, rsem_m), (F2_PIECES, ssem_f, rsem_f0, rsem_f)):
            for p in pieces:
                pltpu.make_async_remote_copy(
                    src_ref=_shard_piece(s_ref, p), dst_ref=_block_rows(arrs, p, me), send_sem=ssem.at[0],
                    recv_sem=rsem0, device_id=sib, device_id_type=MESH).start()
            for j, chip in enumerate(chips):
                for p in pieces:
                    pltpu.make_async_remote_copy(
                        src_ref=_shard_piece(s_ref, p), dst_ref=_block_rows(arrs, p, me), send_sem=ssem.at[1 + j],
                        recv_sem=rsem.at[j], device_id=(*chip, c), device_id_type=MESH).start()

    dma = pltpu.SemaphoreType.DMA
    return pl.pallas_call(
        body, name="gather_rest_start",
        out_shape=(dma((4,)), dma(()), dma((3,)), dma((4,)), dma(()), dma((3,)),
                   _hbm_like(shard), _hbm_like(wi), _hbm_like(wo), _hbm_like(w2), _hbm_like(w1)),
        in_specs=(HBM_SPEC,) * 5, out_specs=(SEM_SPEC,) * 6 + (HBM_SPEC,) * 5,
        input_output_aliases={0: 6, 1: 7, 2: 8, 3: 9, 4: 10},
        compiler_params=pltpu.CompilerParams(has_side_effects=SPLIT_EFFECT),
    )(_in_hbm(shard), _in_hbm(wi), _in_hbm(wo), _in_hbm(w2), _in_hbm(w1))


def _gather_mix_pass_on(rsem_m, wi, wo, thru, after):
    def body(wi_ref, wo_ref, thru_ref, rsem, after_ref, fsend, frecv, wi_o, wo_o, thru_o):
        x, y, c = _position()
        sib = (x, y, 1 - c)
        arrs = _weight_pieces(wi_ref=wi_ref, wo_ref=wo_ref)
        both = wi_ref.at[pl.ds(0, _group_rows(MIX_PIECES)), :]
        for j, chip in enumerate([(1 - x, y), (x, 1 - y), (1 - x, 1 - y)]):
            pltpu.make_async_remote_copy(src_ref=both, dst_ref=both, send_sem=fsend.at[j], recv_sem=rsem.at[j],
                                         device_id=(x, y, c), device_id_type=MESH).wait_recv()
            for p in MIX_PIECES:
                rows = _block_rows(arrs, p, (*chip, c))
                pltpu.make_async_remote_copy(src_ref=rows, dst_ref=rows, send_sem=fsend.at[j], recv_sem=frecv.at[j],
                                             device_id=sib, device_id_type=MESH).start()

    dma = pltpu.SemaphoreType.DMA
    return pl.pallas_call(
        body, name="gather_mix_pass_on",
        out_shape=(dma((3,)), dma((3,)), _hbm_like(wi), _hbm_like(wo), _hbm_like(thru)),
        in_specs=(HBM_SPEC, HBM_SPEC, HBM_SPEC, SEM_SPEC, ANY_SPEC), out_specs=(SEM_SPEC, SEM_SPEC) + (HBM_SPEC,) * 3,
        input_output_aliases={0: 2, 1: 3, 2: 4},
        compiler_params=pltpu.CompilerParams(has_side_effects=SPLIT_EFFECT),
    )(wi, wo, _in_hbm(thru), rsem_m, after)


def _gather_mix_wait(ssem_m, rsem_m0, fsend, frecv, shard, wi, wo, after):
    def body(s_ref, wi_ref, wo_ref, ssem, rsem0, fs, fr, after_ref, s_o, wi_o, wo_o):
        x, y, c = _position()
        grp = _shard_group(s_ref, MIX_PIECES)

        def waiter(send_sem, recv_sem):
            return pltpu.make_async_remote_copy(src_ref=grp, dst_ref=grp, send_sem=send_sem, recv_sem=recv_sem,
                                                device_id=(x, y, c), device_id_type=MESH)

        waiter(ssem.at[0], rsem0).wait_recv()
        for j in range(3):
            waiter(fs.at[j], fr.at[j]).wait_recv()
        for rel in range(4):
            waiter(ssem.at[rel], rsem0).wait_send()
        for j in range(3):
            waiter(fs.at[j], fr.at[j]).wait_send()

    return pl.pallas_call(
        body, name="gather_mix_wait", out_shape=(_hbm_like(shard), _hbm_like(wi), _hbm_like(wo)),
        in_specs=(HBM_SPEC,) * 3 + (SEM_SPEC,) * 4 + (ANY_SPEC,), out_specs=(HBM_SPEC,) * 3,
        input_output_aliases={0: 0, 1: 1, 2: 2},
        compiler_params=pltpu.CompilerParams(has_side_effects=SPLIT_EFFECT),
    )(shard, wi, wo, ssem_m, rsem_m0, fsend, frecv, after)


def _gather_ffn2_pass_on(rsem_f, w2, wo, after):
    def body(w2_ref, wo_ref, rsem, after_ref, fsend, frecv, w2_o, wo_o):
        x, y, c = _position()
        sib = (x, y, 1 - c)
        chips = [(1 - x, y), (x, 1 - y), (1 - x, 1 - y)]
        arrs = _weight_pieces(w2_ref=w2_ref)
        three = w2_ref.at[0, pl.ds(0, _group_rows(F2_PIECES)), :]
        for j, chip in enumerate(chips):
            pltpu.make_async_remote_copy(src_ref=three, dst_ref=three, send_sem=fsend.at[j], recv_sem=rsem.at[j],
                                         device_id=(x, y, c), device_id_type=MESH).wait_recv()
            for p in F2_PIECES:
                rows = _block_rows(arrs, p, (*chip, c))
                pltpu.make_async_remote_copy(src_ref=rows, dst_ref=rows, send_sem=fsend.at[j], recv_sem=frecv.at[j],
                                             device_id=sib, device_id_type=MESH).start()

    dma = pltpu.SemaphoreType.DMA
    return pl.pallas_call(
        body, name="gather_ffn2_pass_on", out_shape=(dma((3,)), dma((3,)), _hbm_like(w2), _hbm_like(wo)),
        in_specs=(HBM_SPEC, HBM_SPEC, SEM_SPEC, ANY_SPEC), out_specs=(SEM_SPEC, SEM_SPEC, HBM_SPEC, HBM_SPEC),
        input_output_aliases={0: 2, 1: 3},
        compiler_params=pltpu.CompilerParams(has_side_effects=SPLIT_EFFECT),
    )(w2, wo, rsem_f, after)


def _gather_ffn2_wait(ssem_f, rsem_f0, fsend, frecv, shard, w2, after):
    def body(s_ref, w2_ref, ssem, rsem0, fs, fr, after_ref, w2_o):
        x, y, c = _position()
        grp = _shard_group(s_ref, F2_PIECES)

        def waiter(send_sem, recv_sem):
            return pltpu.make_async_remote_copy(src_ref=grp, dst_ref=grp, send_sem=send_sem, recv_sem=recv_sem,
                                                device_id=(x, y, c), device_id_type=MESH)

        waiter(ssem.at[0], rsem0).wait_recv()
        for j in range(3):
            waiter(fs.at[j], fr.at[j]).wait_recv()
        for rel in range(4):
            waiter(ssem.at[rel], rsem0).wait_send()
        for j in range(3):
            waiter(fs.at[j], fr.at[j]).wait_send()

    return pl.pallas_call(
        body, name="gather_ffn2_wait", out_shape=_hbm_like(w2),
        in_specs=(HBM_SPEC, HBM_SPEC, SEM_SPEC, SEM_SPEC, SEM_SPEC, SEM_SPEC, ANY_SPEC), out_specs=HBM_SPEC,
        input_output_aliases={1: 0},
        compiler_params=pltpu.CompilerParams(has_side_effects=SPLIT_EFFECT),
    )(shard, w2, ssem_f, rsem_f0, fsend, frecv, after)


class _GatheredWeights(_LocalWeights):
    def __init__(self, shard, x, gain1):
        w1, self.h1, wi, wo, w2 = _all_gather_ffn1(shard, x, gain1)
        (self.ssem_m, self.rsem_m0, self.rsem_m, self.ssem_f, self.rsem_f0, self.rsem_f,
         self.shard, self.wi, self.wo, self.w2_part, self.w1) = _gather_rest_start(shard, wi, wo, w2, w1)

    def first_norm(self, x, gain):
        return self.h1

    def after_ffn1(self, gain, x1):
        self.fsend_m, self.frecv_m, self.wi, self.wo, gain = _gather_mix_pass_on(self.rsem_m, self.wi, self.wo, gain, x1)
        return gain

    def mix(self, after):
        self.shard, wint, wout = _gather_mix_wait(self.ssem_m, self.rsem_m0, self.fsend_m, self.frecv_m, self.shard,
                                                  self.wi, self.wo, after)
        return wint, wout

    def before_out_proj(self, wout, after):
        self.fsend, self.frecv, self.w2_part, wout = _gather_ffn2_pass_on(self.rsem_f, self.w2_part, wout, after)
        return wout

    def ffn2(self, after):
        return _gather_ffn2_wait(self.ssem_f, self.rsem_f0, self.fsend, self.frecv, self.shard, self.w2_part, after)

    def out_ffn2_grads_ready(self, dwout, dw2, after):
        rx1 = lax.empty((4, RSA_ROWS, D), bf16)
        sa, ra, sent, rx1, after = _rsa_level1_start(dict(wo=dwout, w2=dw2), rx1, after, "rsa_level1_start_out_ffn2")
        self.level1 = ((sa, ra), sent, rx1)
        return after

    def before_ffn1_bwd(self, dwint, dx1b):
        early, sent, rx1 = self.level1
        sa, ra, late, rx1, dx1b = _rsa_level1_start(dict(wi=dwint), rx1, dx1b, "rsa_level1_start_in")
        started = (((MIX_PIECES[1],) + F2_PIECES, *early), ((MIX_PIECES[0],), sa, ra))
        rx2 = lax.empty((3, RSA_ROWS, D), bf16)
        self.sb, self.rb, self.tx, self.acc, self.rx2, dx1b = _rsa_sums_and_send(
            started, late["wi"], sent["wo"], sent["w2"], rx1, rx2, dx1b)
        return dx1b

    def mix_ffn2_grads_parts(self, after):
        rx2 = _rsa_level2_wait(self.sb, self.rb, self.tx, self.rx2, after)
        return self.acc, rx2


def _reduce_scatter_ffn1_head(dw1, small_packed, first_norm):
    pieces = G1_PIECES
    half = FS // 2
    hrows = len(pieces) * half
    nrows = 2 * hrows
    X_RELAY, Y_RELAY = range(2)

    T = first_norm[0].shape[0]

    def body(d1_ref, p_ref, dh_hbm, x_hbm, gain_ref, dr_hbm,
             forx_ref, fory_ref, own_ref, rx1_ref, relx_ref, rely_ref, gx_hbm, tot_ref,
             own_buf, rx_buf, tx1, tx2, tx3, acc, sa, ra, sb, rb, lsem, pair, chips, small_tot, small_send, small_recv,
             xbuf, rbuf, hbuf, obuf, norm_in_sems, norm_out_sems):
        x, y, c = _position()
        me, sib = (x, y, c), (x, y, 1 - c)
        xn, yn = (1 - x, y, c), (x, 1 - y, c)
        rel_chips = [(x, y), (1 - x, y), (x, 1 - y), (1 - x, 1 - y)]
        srcs = _weight_pieces(w1_ref=d1_ref)

        my_chip = 2 * x + y
        pair[c] = p_ref[...]
        swap = pltpu.make_async_remote_copy(
            src_ref=pair.at[c], dst_ref=pair.at[c], send_sem=small_send.at[0], recv_sem=small_recv.at[0],
            device_id=sib, device_id_type=MESH)
        mine = pl.ds(pl.multiple_of(c * (SMALL_ROWS // 2), 8), SMALL_ROWS // 2)
        small = [pltpu.make_async_remote_copy(
            src_ref=chips.at[my_chip, mine, :], dst_ref=chips.at[my_chip, mine, :], send_sem=small_send.at[j],
            recv_sem=small_recv.at[j], device_id=(*rel_chips[j], c), device_id_type=MESH) for j in (1, 2, 3)]
        give = pltpu.make_async_remote_copy(
            src_ref=small_tot.at[mine, :], dst_ref=small_tot.at[mine, :], send_sem=small_send.at[4],
            recv_sem=small_recv.at[4], device_id=sib, device_id_type=MESH)

        def part(k, dev, hf):
            r = PIECE_ROWS[k]
            return srcs[k].at[pl.ds(pl.multiple_of(_dev_index(*dev) * r + hf * half, 16), half), :]

        def slot(ref, k, hf):
            return ref.at[pl.ds(hf * hrows + k * half, half), :]

        halves = [(k, hf) for hf in (0, 1) for k in pieces]

        for j in (3, 1, 2, 0):
            for k, hf in halves:
                pltpu.make_async_remote_copy(
                    src_ref=part(k, (*rel_chips[j], 1 - c), hf), dst_ref=slot(rx1_ref.at[j], k, hf),
                    send_sem=sa.at[j], recv_sem=ra.at[j], device_id=sib, device_id_type=MESH).start()

        def wait_a(j):
            return pltpu.make_async_remote_copy(src_ref=rx1_ref.at[j], dst_ref=rx1_ref.at[j], send_sem=sa.at[j],
                                                recv_sem=ra.at[j], device_id=me, device_id_type=MESH)

        def ici(rel, src, dst, to):
            return pltpu.make_async_remote_copy(src_ref=src, dst_ref=dst, send_sem=sb.at[rel], recv_sem=rb.at[rel],
                                                device_id=to, device_id_type=MESH)

        first, second = pl.ds(0, hrows), pl.ds(hrows, hrows)
        sends = {
            X_RELAY: ici(X_RELAY, tx3.at[first, :], relx_ref, xn),
            Y_RELAY: ici(Y_RELAY, tx3.at[second, :], rely_ref, yn),
        }

        def chip_sum(j, dst):
            loads = [pltpu.make_async_copy(part(k, (*rel_chips[j], c), hf), slot(own_buf, k, hf), lsem.at[0])
                     for k, hf in halves]
            for cp in loads:
                cp.start()
            wait_a(j).wait_recv()
            got = pltpu.make_async_copy(rx1_ref.at[j], rx_buf, lsem.at[1])
            got.start()
            pltpu.make_async_copy(rx_buf, rx_buf, lsem.at[0]).wait()
            got.wait()

            def add(i, carry):
                rows = pl.ds(pl.multiple_of(i * half, 16), half)
                tot = own_buf[rows, :].astype(f32) + rx_buf[rows, :].astype(f32)
                dst[rows, :] = tot.astype(dst.dtype)
                return carry

            lax.fori_loop(0, nrows // half, add, 0)

        def add_landed(landed, dst, rows0, nrows_):
            got = pltpu.make_async_copy(landed, rx_buf.at[pl.ds(0, nrows_), :], lsem.at[1])
            got.start()
            got.wait()

            def add(i, carry):
                src_rows = pl.ds(pl.multiple_of(i * half, 16), half)
                dst_rows = pl.ds(pl.multiple_of(rows0 + i * half, 16), half)
                dst[dst_rows, :] = (dst[dst_rows, :].astype(f32) + rx_buf[src_rows, :].astype(f32)).astype(dst.dtype)
                return carry

            lax.fori_loop(0, nrows_ // half, add, 0)

        chip_sum(3, tx3)
        sends[X_RELAY].start()
        sends[Y_RELAY].start()
        dg = _norm_bwd_rows(T, dh_hbm, x_hbm, gain_ref, dr_hbm, gx_hbm.at[0], None,
                            xbuf, rbuf, hbuf, obuf, None, norm_in_sems, norm_out_sems)
        r0 = SMALL_OFF["ffn1_norm"]
        for k in range(D // 128):
            pair[c, r0 + k:r0 + k + 1, :] = dg[:, 128 * k:128 * (k + 1)]
        swap.start()
        swap.wait_recv()
        chips[my_chip] = pair[0] + pair[1]
        for cp in small:
            cp.start()
        chip_sum(1, tx1)
        chip_sum(2, tx2)
        chip_sum(0, acc)
        own_out = pltpu.make_async_copy(acc, own_ref, lsem.at[0])
        own_out.start()
        sends[X_RELAY].wait_recv()
        add_landed(relx_ref, tx2, 0, hrows)
        sends[Y_RELAY].wait_recv()
        add_landed(rely_ref, tx1, hrows, hrows)
        own_out.wait()
        outs = [pltpu.make_async_copy(tx1, forx_ref, lsem.at[0]), pltpu.make_async_copy(tx2, fory_ref, lsem.at[1])]
        for cp in outs:
            cp.start()
        for cp in outs:
            cp.wait()
        for cp in small:
            cp.wait_recv()
        small_tot[mine, :] = (chips[0, mine, :] + chips[1, mine, :]) + (chips[2, mine, :] + chips[3, mine, :])
        give.start()
        give.wait_recv()
        tot = small_tot[...]
        tot_ref[...] = tot
        loss = jnp.sum(tot[LOSS_ROW:LOSS_ROW + 1, :], axis=-1, keepdims=True)
        tot_ref[LOSS_ROW:LOSS_ROW + 1, :] = jnp.broadcast_to(loss, (1, 128))
        for j in range(4):
            wait_a(j).wait_send()
        for cp in sends.values():
            cp.wait_send()
        swap.wait_send()
        for cp in small + [give]:
            cp.wait_send()

    hbm = pl.BlockSpec(memory_space=pl.ANY)
    vm = pl.BlockSpec(memory_space=pltpu.VMEM)
    outs = pl.pallas_call(
        body, in_specs=[hbm, vm, hbm, hbm, vm, hbm], out_specs=[hbm] * 7 + [vm],
        out_shape=(SDS((nrows, D), bf16), SDS((nrows, D), bf16), SDS((nrows, D), f32), SDS((4, nrows, D), bf16),
                   SDS((hrows, D), bf16), SDS((hrows, D), bf16), SDS((1, T, D), f32), SDS((SMALL_ROWS, 128), f32)),
        scratch_shapes=[pltpu.VMEM((nrows, D), bf16), pltpu.VMEM((nrows, D), bf16),
                        pltpu.VMEM((nrows, D), bf16), pltpu.VMEM((nrows, D), bf16), pltpu.VMEM((nrows, D), bf16),
                        pltpu.VMEM((nrows, D), f32),
                        pltpu.SemaphoreType.DMA((4,)), pltpu.SemaphoreType.DMA((4,)),
                        pltpu.SemaphoreType.DMA((2,)), pltpu.SemaphoreType.DMA((2,)), pltpu.SemaphoreType.DMA((2,)),
                        pltpu.VMEM((2, SMALL_ROWS, 128), f32), pltpu.VMEM((4, SMALL_ROWS, 128), f32),
                        pltpu.VMEM((SMALL_ROWS, 128), f32),
                        pltpu.SemaphoreType.DMA((5,)), pltpu.SemaphoreType.DMA((5,))]
        + [sc for sc in _norm_bwd_scratch(True, False) if sc is not None],
        compiler_params=pltpu.CompilerParams(has_side_effects=True, vmem_limit_bytes=VMEM_LIMIT_V7X),
        name="reduce_scatter_ffn1_head")(dw1, small_packed, *first_norm)
    return outs[0], outs[1], outs[2], outs[-1], outs[-2]


def _rs1_tail_start(for_x, for_y, from_x, from_y, *thru):
    def body(fx_ref, fy_ref, lx_ref, ly_ref, *rest):
        ssem, rsem = rest[len(thru):len(thru) + 2]
        x, y, c = _position()
        pltpu.make_async_remote_copy(src_ref=fx_ref, dst_ref=lx_ref, send_sem=ssem.at[0], recv_sem=rsem.at[0],
                                     device_id=(1 - x, y, c), device_id_type=MESH).start()
        pltpu.make_async_remote_copy(src_ref=fy_ref, dst_ref=ly_ref, send_sem=ssem.at[1], recv_sem=rsem.at[1],
                                     device_id=(x, 1 - y, c), device_id_type=MESH).start()

    dma = pltpu.SemaphoreType.DMA
    arrs = (for_x, for_y, from_x, from_y, *thru)
    return pl.pallas_call(
        body, name="rs1_tail_start", out_shape=(dma((2,)), dma((2,))) + tuple(_hbm_like(a) for a in arrs),
        in_specs=(HBM_SPEC,) * len(arrs), out_specs=(SEM_SPEC,) * 2 + (HBM_SPEC,) * len(arrs),
        input_output_aliases={i: i + 2 for i in range(len(arrs))},
        compiler_params=pltpu.CompilerParams(has_side_effects=SPLIT_EFFECT),
    )(*[_in_hbm(a) for a in arrs])


RSA_PIECES = MIX_PIECES + F2_PIECES
RSA_ROWS = _group_rows(RSA_PIECES)
RSA_OFF = {k: PIECE_OFF[k] - PIECE_OFF[RSA_PIECES[0]] for k in RSA_PIECES}
RSA_BLOCK = 192


def _rsa_rows(ref, k):
    return ref.at[pl.ds(RSA_OFF[k], PIECE_ROWS[k]), :]


def _rsa_level1_start(grads, rx1, thru, name):
    keys = sorted(grads)
    n = len(keys)

    def body(*refs):
        srcs = _weight_pieces(**{k + "_ref": ref for k, ref in zip(keys, refs[:n])})
        rx1_ref, sa, ra = refs[n], refs[n + 2], refs[n + 3]
        x, y, c = _position()
        for j, chip in enumerate([(x, y), (1 - x, y), (x, 1 - y), (1 - x, 1 - y)]):
            for k in sorted(srcs):
                pltpu.make_async_remote_copy(
                    src_ref=_block_rows(srcs, k, (*chip, 1 - c)), dst_ref=_rsa_rows(rx1_ref.at[j], k),
                    send_sem=sa.at[j], recv_sem=ra.at[j], device_id=(x, y, 1 - c), device_id_type=MESH).start()

    dma = pltpu.SemaphoreType.DMA
    arrs = tuple(grads[k] for k in keys) + (rx1, thru)
    outs = pl.pallas_call(
        body, name=name, out_shape=(dma((4,)), dma((4,))) + tuple(_hbm_like(a) for a in arrs),
        in_specs=(HBM_SPEC,) * len(arrs), out_specs=(SEM_SPEC,) * 2 + (HBM_SPEC,) * len(arrs),
        input_output_aliases={i: i + 2 for i in range(len(arrs))},
        compiler_params=pltpu.CompilerParams(has_side_effects=SPLIT_EFFECT),
    )(*[_in_hbm(a) for a in arrs])
    return outs[0], outs[1], dict(zip(keys, outs[2:2 + n])), outs[2 + n], outs[3 + n]


def _rsa_sums_and_send(started, dwint, dwout, dw2, rx1, rx2, thru):
    nblk = RSA_ROWS // RSA_BLOCK
    nstart = len(started)

    def body(*refs):
        di_ref, do_ref, d2_ref, rx1_ref, rx2_ref = refs[:5]
        l1_sems = refs[6:6 + 2 * nstart]
        sb, rb, tx_ref, acc_ref = refs[6 + 2 * nstart:10 + 2 * nstart]
        own_buf, rx_buf, tx_buf, acc_buf, in_sems, out_sems = refs[13 + 2 * nstart:]
        x, y, c = _position()
        srcs = _weight_pieces(wi_ref=di_ref, wo_ref=do_ref, w2_ref=d2_ref)
        chips = [(x, y), (1 - x, y), (x, 1 - y), (1 - x, 1 - y)]

        for g, (pieces, _, _) in enumerate(started):
            ssem, rsem = l1_sems[2 * g], l1_sems[2 * g + 1]
            for j in range(4):
                rows = rx1_ref.at[j, pl.ds(RSA_OFF[pieces[0]], _group_rows(pieces)), :]
                d = pltpu.make_async_remote_copy(src_ref=rows, dst_ref=rows, send_sem=ssem.at[j], recv_sem=rsem.at[j],
                                                 device_id=(x, y, c), device_id_type=MESH)
                d.wait_recv()
                d.wait_send()

        def start_loads(j):
            s = j % 2
            for k in RSA_PIECES:
                pltpu.make_async_copy(_block_rows(srcs, k, (*chips[j], c)), _rsa_rows(own_buf.at[s], k),
                                      in_sems.at[2 * s]).start()
            pltpu.make_async_copy(rx1_ref.at[j], rx_buf.at[s], in_sems.at[2 * s + 1]).start()

        def wait_loads(j):
            s = j % 2
            pltpu.make_async_copy(rx1_ref.at[j], own_buf.at[s], in_sems.at[2 * s]).wait()
            pltpu.make_async_copy(rx1_ref.at[j], rx_buf.at[s], in_sems.at[2 * s + 1]).wait()

        def store(j):
            if j == 0:
                return pltpu.make_async_copy(acc_buf, acc_ref, out_sems.at[2])
            return pltpu.make_async_copy(tx_buf.at[j % 2], tx_ref.at[j - 1], out_sems.at[j % 2])

        def send(j):
            return pltpu.make_async_remote_copy(src_ref=tx_ref.at[j - 1], dst_ref=rx2_ref.at[j - 1], send_sem=sb.at[j - 1],
                                                recv_sem=rb.at[j - 1], device_id=(*chips[j], c), device_id_type=MESH)

        start_loads(0)
        for j in range(4):
            s = j % 2
            if j + 1 < 4:
                start_loads(j + 1)
            wait_loads(j)
            if j == 3:
                store(1).wait()
                send(1).start()

            def add(i, carry, j=j, s=s):
                rows = pl.ds(pl.multiple_of(i * RSA_BLOCK, 16), RSA_BLOCK)
                tot = own_buf[s, rows, :].astype(f32) + rx_buf[s, rows, :].astype(f32)
                if j == 0:
                    acc_buf[rows, :] = tot
                else:
                    tx_buf[s, rows, :] = tot.astype(bf16)
                return carry

            lax.fori_loop(0, nblk, add, 0)
            store(j).start()
        store(0).wait()
        for j in (2, 3):
            store(j).wait()
            send(j).start()

    dma = pltpu.SemaphoreType.DMA
    passed = (rx1, rx2, thru)
    outs = pl.pallas_call(
        body, name="rsa_sums_and_send",
        in_specs=(HBM_SPEC,) * 6 + (SEM_SPEC,) * (2 * nstart),
        out_specs=(SEM_SPEC,) * 2 + (HBM_SPEC,) * 5,
        out_shape=(dma((3,)), dma((3,)), pltpu.HBM((3, RSA_ROWS, D), bf16), pltpu.HBM((RSA_ROWS, D), f32))
        + tuple(_hbm_like(a) for a in passed),
        input_output_aliases={3: 4, 4: 5, 5: 6},
        scratch_shapes=[pltpu.VMEM((2, RSA_ROWS, D), bf16), pltpu.VMEM((2, RSA_ROWS, D), bf16),
                        pltpu.VMEM((2, RSA_ROWS, D), bf16), pltpu.VMEM((RSA_ROWS, D), f32),
                        dma((4,)), dma((3,))],
        compiler_params=pltpu.CompilerParams(has_side_effects=SPLIT_EFFECT, vmem_limit_bytes=VMEM_LIMIT_V7X),
    )(*[_in_hbm(a) for a in (dwint, dwout, dw2) + passed], *[sem for _, sa, ra in started for sem in (sa, ra)])
    sb, rb, tx, acc, _, rx2, thru = outs
    return sb, rb, tx, acc, rx2, thru


def _rsa_level2_wait(sb, rb, tx, rx2, after):
    def body(tx_ref, rx2_ref, sb_ref, rb_ref, after_ref, rx2_o):
        x, y, c = _position()
        for j in range(3):
            d = pltpu.make_async_remote_copy(src_ref=tx_ref.at[j], dst_ref=rx2_ref.at[j], send_sem=sb_ref.at[j],
                                             recv_sem=rb_ref.at[j], device_id=(x, y, c), device_id_type=MESH)
            d.wait_recv()
            d.wait_send()

    return pl.pallas_call(
        body, name="rsa_level2_wait", out_shape=_hbm_like(rx2),
        in_specs=(HBM_SPEC, HBM_SPEC, SEM_SPEC, SEM_SPEC, ANY_SPEC), out_specs=HBM_SPEC,
        input_output_aliases={1: 0},
        compiler_params=pltpu.CompilerParams(has_side_effects=SPLIT_EFFECT),
    )(tx, rx2, sb, rb, after)


def _adamw_math(w, g, m, v):
    m = ADAM_B1 * m + (1.0 - ADAM_B1) * g
    v = ADAM_B2 * v + (1.0 - ADAM_B2) * (g * g)
    m_hat = m / (1.0 - ADAM_B1 ** ADAM_STEP)
    v_hat = v / (1.0 - ADAM_B2 ** ADAM_STEP)
    delta = -ADAM_LR * (m_hat / (jnp.sqrt(v_hat) + ADAM_EPS) + ADAM_WD * w)
    return delta, m, v


def _adamw_big(pieces, ws, ms, vs, own, landed, name, in_flight=None):
    npiece = len(pieces)
    nsent = len(landed) if in_flight else 0
    nland = sum(a.shape[0] if a.ndim == 3 else 1 for a in landed)
    rmax = max(PIECE_ROWS[k] for k in pieces)
    half = FS // 2

    def segments(k):
        if k in G1_PIECES:
            return [(hf * len(G1_PIECES) * half + k * half, hf * half, half) for hf in (0, 1)]
        return [(RSA_OFF[k], 0, PIECE_ROWS[k])]

    def body(*refs):
        ins = (refs[0:npiece], refs[npiece:2 * npiece], refs[2 * npiece:3 * npiece])
        own_ref = refs[3 * npiece]
        nin = 3 * npiece + 1 + len(landed)
        land_refs = []
        for ref, a in zip(refs[3 * npiece + 1:nin], landed):
            land_refs += [ref.at[j] for j in range(a.shape[0])] if a.ndim == 3 else [ref]
        if in_flight:
            sent_refs, (ssem, rsem) = refs[nin:nin + nsent], refs[nin + nsent:nin + nsent + 2]
            nin += nsent + 3
        out_refs = refs[nin:nin + 4 * npiece]
        inb, landb, outb, in_sems, land_sems, out_sems = refs[nin + 4 * npiece + nsent:]

        def loads(i):
            s, k = i % 2, pieces[i]
            r = PIECE_ROWS[k]
            cps = [pltpu.make_async_copy(ins[q][i].at[0], inb.at[s, q, pl.ds(0, r), :], in_sems.at[4 * s + q])
                   for q in range(3)]
            waits, late = list(cps), []
            for src0, dst0, n in segments(k):
                cps.append(pltpu.make_async_copy(own_ref.at[pl.ds(src0, n), :], inb.at[s, 3, pl.ds(dst0, n), :],
                                                 in_sems.at[4 * s + 3]))
                for p in range(nland):
                    late.append(pltpu.make_async_copy(land_refs[p].at[pl.ds(src0, n), :],
                                                      landb.at[s, p, pl.ds(dst0, n), :], land_sems.at[nland * s + p]))
            own_rows = inb.at[s, 3, pl.ds(0, r), :]
            waits.append(pltpu.make_async_copy(own_rows, own_rows, in_sems.at[4 * s + 3]))
            for p in range(nland):
                rows = landb.at[s, p, pl.ds(0, r), :]
                waits.append(pltpu.make_async_copy(rows, rows, land_sems.at[nland * s + p]))
            return cps, late, waits

        def stores(i):
            s, r = i % 2, PIECE_ROWS[pieces[i]]
            return [pltpu.make_async_copy(outb.at[s, q, pl.ds(0, r), :], out_refs[q * npiece + i].at[0],
                                          out_sems.at[4 * s + q]) for q in range(4)]

        ahead = min(2, npiece) if in_flight else 1
        for i in range(ahead):
            for cp in loads(i)[0]:
                cp.start()
        if in_flight:
            x, y, c = _position()
            for j in range(nsent):
                d = pltpu.make_async_remote_copy(src_ref=sent_refs[j], dst_ref=refs[3 * npiece + 1 + j],
                                                 send_sem=ssem.at[j], recv_sem=rsem.at[j], device_id=(x, y, c),
                                                 device_id_type=MESH)
                d.wait_recv()
                d.wait_send()
        for cp in loads(0)[1]:
            cp.start()
        for i in range(npiece):
            s, r = i % 2, PIECE_ROWS[pieces[i]]
            if i + 1 < npiece:
                first, late, _ = loads(i + 1)
                for cp in late if i + 1 < ahead else first + late:
                    cp.start()
            for cp in loads(i)[2]:
                cp.wait()
            if i >= 2:
                for cp in stores(i - 2):
                    cp.wait()
            g = inb[s, 3, 0:r, :]
            for p in range(nland):
                g = g + landb[s, p, 0:r, :].astype(f32)
            d, nm, nv = _adamw_math(inb[s, 0, 0:r, :], g, inb[s, 1, 0:r, :], inb[s, 2, 0:r, :])
            outb[s, 0, 0:r, :] = g
            outb[s, 1, 0:r, :] = d
            outb[s, 2, 0:r, :] = nm
            outb[s, 3, 0:r, :] = nv
            for cp in stores(i):
                cp.start()
        for i in range(max(npiece - 2, 0), npiece):
            for cp in stores(i):
                cp.wait()

    hbm = pl.BlockSpec(memory_space=pl.ANY)
    in_specs, out_specs = [hbm] * (3 * npiece + 1), [hbm] * (4 * npiece)
    out_shape = [SDS(w.shape, f32) for _ in range(4) for w in ws]
    args, aliases, effect = [*ws, *ms, *vs, own], {}, False
    if in_flight:
        ssem, rsem, sent, after = in_flight
        in_specs += [HBM_SPEC] * (2 * nsent) + [SEM_SPEC, SEM_SPEC, hbm]
        args += [_in_hbm(a) for a in (*landed, *sent)] + [ssem, rsem, after]
        out_specs += [HBM_SPEC] * nsent
        out_shape += [_hbm_like(a) for a in landed]
        aliases = {3 * npiece + 1 + j: 4 * npiece + j for j in range(nsent)}
        effect = SPLIT_EFFECT
    else:
        in_specs += [hbm] * len(landed)
        args += list(landed)
    outs = pl.pallas_call(
        body, in_specs=in_specs, out_specs=out_specs, out_shape=tuple(out_shape), input_output_aliases=aliases,
        scratch_shapes=[pltpu.VMEM((2, 4, rmax, D), f32), pltpu.VMEM((2, nland, rmax, D), bf16),
                        pltpu.VMEM((2, 4, rmax, D), f32),
                        pltpu.SemaphoreType.DMA((8,)), pltpu.SemaphoreType.DMA((2 * nland,)),
                        pltpu.SemaphoreType.DMA((8,))],
        compiler_params=pltpu.CompilerParams(has_side_effects=effect, vmem_limit_bytes=VMEM_LIMIT_V7X),
        name=name)(*args)
    return [list(outs[q * npiece:(q + 1) * npiece]) for q in range(4)]


def _adamw_small(ws, ms, vs, gs, name):
    n = len(ws)

    def body(*refs):
        w_refs, m_refs, v_refs, g_refs = refs[0:n], refs[n:2 * n], refs[2 * n:3 * n], refs[3 * n:4 * n]
        outs = refs[4 * n:]
        for i in range(n):
            d, nm, nv = _adamw_math(w_refs[i][...], g_refs[i][...], m_refs[i][...], v_refs[i][...])
            outs[i][...] = d
            outs[n + i][...] = nm
            outs[2 * n + i][...] = nv

    outs = pl.pallas_call(
        body, out_shape=tuple(SDS(w.shape, f32) for _ in range(3) for w in ws), name=name)(*ws, *ms, *vs, *gs)
    return [list(outs[q * n:(q + 1) * n]) for q in range(3)]


WEIGHTS = ("ffn1_norm", "ffn1_w_gate", "ffn1_w_up", "ffn1_w_down", "mix_norm", "w_in", "q_norm", "k_norm",
           "attn_sinks", "rel_bias", "pool_w", "pool_scale", "w_out", "ffn2_norm", "ffn2_w_gate", "ffn2_w_up",
           "ffn2_w_down")
BIG = (("ffn1_w_gate", True), ("ffn1_w_up", True), ("ffn1_w_down", False), ("w_in", True), ("w_out", False),
       ("ffn2_w_gate", True), ("ffn2_w_up", True), ("ffn2_w_down", False))


def kernel(x, ffn1_norm, ffn1_w_gate, ffn1_w_up, ffn1_w_down, mix_norm, w_in, q_norm, k_norm, attn_sinks, rel_bias, pool_w, pool_scale, w_out, ffn2_norm, ffn2_w_gate, ffn2_w_up, ffn2_w_down, loss_target, m_ffn1_norm, m_ffn1_w_gate, m_ffn1_w_up, m_ffn1_w_down, m_mix_norm, m_w_in, m_q_norm, m_k_norm, m_attn_sinks, m_rel_bias, m_pool_w, m_pool_scale, m_w_out, m_ffn2_norm, m_ffn2_w_gate, m_ffn2_w_up, m_ffn2_w_down, v_ffn1_norm, v_ffn1_w_gate, v_ffn1_w_up, v_ffn1_w_down, v_mix_norm, v_w_in, v_q_norm, v_k_norm, v_attn_sinks, v_rel_bias, v_pool_w, v_pool_scale, v_w_out, v_ffn2_norm, v_ffn2_w_gate, v_ffn2_w_up, v_ffn2_w_down):
    args = dict(locals())
    w = {n: args[n] for n in WEIGHTS}
    m = {n: args["m_" + n] for n in WEIGHTS}
    v = {n: args["v_" + n] for n in WEIGHTS}

    as_rows = lambda a, tr: jnp.swapaxes(a, 1, 2) if tr else a
    shard = jnp.concatenate([as_rows(w[n], tr)[0].astype(bf16) for n, tr in BIG], axis=0)
    exchanges = _GatheredWeights(shard, x[0], ffn1_norm)
    (dh1, dx1), (dw1, _, _, _), small = _local_step(
        x[0], loss_target[0], exchanges, ffn1_norm, mix_norm, ffn2_norm, q_norm, k_norm, attn_sinks,
        rel_bias, pool_w[0], pool_scale)

    nrows1 = len(G1_PIECES) * FS
    for_x, for_y, own1, small_tot, gx = _reduce_scatter_ffn1_head(dw1, _pack_small(small),
                                                                  (dh1, x[0], ffn1_norm, dx1))
    ssem, rsem, for_x, for_y, from_x, from_y, small_tot, gx = _rs1_tail_start(
        for_x, for_y, lax.empty((nrows1, D), bf16), lax.empty((nrows1, D), bf16), small_tot, gx)
    gx = _fresh_copy(gx, "grad_x_out")
    own_rest, landed_rest = exchanges.mix_ffn2_grads_parts(gx)

    grads, deltas, new_m, new_v = {}, {}, {}, {}
    rest = [k for k in range(len(BIG)) if k not in G1_PIECES]
    rows_of = lambda t, ks: [as_rows(t[BIG[k][0]], BIG[k][1]) for k in ks]
    rest_out = _adamw_big(rest, rows_of(w, rest), rows_of(m, rest), rows_of(v, rest), own_rest, [landed_rest],
                          "adamw_rest")
    ffn1 = list(G1_PIECES)
    ffn1_out = _adamw_big(ffn1, rows_of(w, ffn1), rows_of(m, ffn1), rows_of(v, ffn1), own1, [from_x, from_y],
                          "adamw_ffn1", in_flight=(ssem, rsem, [for_x, for_y], rest_out[0][0]))
    for ks, out in ((rest, rest_out), (ffn1, ffn1_out)):
        for i, k in enumerate(ks):
            n, tr = BIG[k]
            grads[n], deltas[n], new_m[n], new_v[n] = [as_rows(o[i], tr) for o in out]
    small_names = [n for n in SMALL_NAMES if n != "loss"]
    for n in small_names:
        grads[n] = _unpack_small(small_tot, n)
    ds, nms, nvs = _adamw_small([w[n] for n in small_names], [m[n] for n in small_names], [v[n] for n in small_names],
                                [grads[n] for n in small_names], "adamw_small")
    for i, n in enumerate(small_names):
        deltas[n], new_m[n], new_v[n] = ds[i], nms[i], nvs[i]
    loss = small_tot[LOSS_ROW, 0]
    return (loss, gx, *[grads[n] for n in WEIGHTS], *[deltas[n] for n in WEIGHTS],
            *[new_m[n] for n in WEIGHTS], *[new_v[n] for n in WEIGHTS])
```
